```python
import jax, jax.numpy as jnp
from jax import lax
import numpy as np

D_MODEL = 1024
BATCH = 8
SEQ = 4096
DEPTH = 1

N_META = 16
ATTN_HEADS = 8
HEAD_DIM = D_MODEL // 16
D_ATTN = ATTN_HEADS * HEAD_DIM
CONV_GROUPS = 8
D_CONV = D_MODEL // 2
CONV_WIDTH = 3
D_MIX = D_ATTN + D_CONV
Q_BLOCK = 128
EPS = 1e-6
SPLIT_SIZES = (D_ATTN, D_ATTN, D_ATTN, ATTN_HEADS, D_ATTN, D_CONV, D_CONV, D_CONV, D_CONV)
D_IN = sum(SPLIT_SIZES)
SPLIT_POINTS = tuple(int(s) for s in np.cumsum(SPLIT_SIZES)[:-1])

kernel_name = "hymba_fox_shortconv_hybrid"


def _rmsnorm(x, g):
    xf = x.astype(jnp.float32)
    y = xf * lax.rsqrt(jnp.mean(xf * xf, axis=-1, keepdims=True) + EPS)
    return (y * g.astype(jnp.float32)).astype(x.dtype)


def _group_rmsnorm(y, g, n_groups):
    lead = y.shape[:-1]
    c = y.shape[-1]
    yf = y.astype(jnp.float32).reshape(lead + (n_groups, c // n_groups))
    yf = yf * lax.rsqrt(jnp.mean(yf * yf, axis=-1, keepdims=True) + EPS)
    return (yf.reshape(lead + (c,)) * g.astype(jnp.float32)).astype(y.dtype)


def _fox_attention(q, k, v, cum_logf):
    b, l, h, dh = q.shape
    scale = dh ** -0.5
    key_pos = jnp.arange(l)
    c_keys = jnp.transpose(cum_logf, (0, 2, 1))

    def block(args):
        q_blk, c_blk, t_blk = args
        s = jnp.einsum('bqhd,bkhd->bhqk', q_blk, k,
                       preferred_element_type=jnp.float32) * scale
        s = s + jnp.transpose(c_blk, (0, 2, 1))[..., :, None] - c_keys[:, :, None, :]
        s = jnp.where(key_pos[None, :] <= t_blk[:, None], s, -jnp.inf)
        p = jax.nn.softmax(s, axis=-1)
        return jnp.einsum('bhqk,bkhd->bqhd', p.astype(v.dtype), v)

    out_meta = block((q[:, :N_META], cum_logf[:, :N_META], key_pos[:N_META]))
    n_blk = (l - N_META) // Q_BLOCK
    q_r = jnp.transpose(q[:, N_META:].reshape(b, n_blk, Q_BLOCK, h, dh), (1, 0, 2, 3, 4))
    c_r = jnp.transpose(cum_logf[:, N_META:].reshape(b, n_blk, Q_BLOCK, h), (1, 0, 2, 3))
    t_r = key_pos[N_META:].reshape(n_blk, Q_BLOCK)
    out_r = lax.map(block, (q_r, c_r, t_r))
    out_r = jnp.transpose(out_r, (1, 0, 2, 3, 4)).reshape(b, l - N_META, h, dh)
    return jnp.concatenate([out_meta, out_r], axis=1)


def _causal_depthwise_conv(x, w):
    c = x.shape[-1]
    return lax.conv_general_dilated(
        x, w.reshape(CONV_WIDTH, 1, c).astype(x.dtype),
        window_strides=(1,), padding=[(CONV_WIDTH - 1, 0)],
        dimension_numbers=('NWC', 'WIO', 'NWC'), feature_group_count=c)


def _hybrid_layer(h, norm_g, w_in, b_f, conv_w, attn_norm_g, conv_norm_g, w_out):
    b, l, _ = h.shape
    u = _rmsnorm(h, norm_g)
    proj = jnp.einsum('bld,de->ble', u, w_in)
    q, k, v, f_logit, z_attn, gate_b, gate_c, xc, z_conv = jnp.split(proj, SPLIT_POINTS, axis=-1)

    log_f = jax.nn.log_sigmoid(f_logit.astype(jnp.float32) + b_f.astype(jnp.float32))
    cum_logf = jnp.cumsum(log_f, axis=1)
    shp = (b, l, ATTN_HEADS, HEAD_DIM)
    attn = _fox_attention(q.reshape(shp), k.reshape(shp), v.reshape(shp), cum_logf)
    y_attn = _group_rmsnorm(attn.reshape(b, l, D_ATTN), attn_norm_g, ATTN_HEADS) * jax.nn.silu(z_attn)

    conv = _causal_depthwise_conv(gate_c * xc, conv_w)
    y_conv = _group_rmsnorm(gate_b * conv, conv_norm_g, CONV_GROUPS) * jax.nn.silu(z_conv)

    mix = jnp.concatenate([y_attn, y_conv], axis=-1)
    return h + jnp.einsum('ble,ed->bld', mix, w_out)


def _fwd_setup_inputs(seed: int = 0) -> dict:
    key = jax.random.key(seed)
    ks = jax.random.split(key, 10)
    f32 = jnp.float32
    x = jax.random.normal(ks[0], (BATCH, SEQ, D_MODEL), f32)
    meta = jax.random.normal(ks[1], (N_META, D_MODEL), f32)
    norm_g = 1.0 + 0.02 * jax.random.normal(ks[2], (DEPTH, D_MODEL), f32)
    w_in = jax.random.normal(ks[3], (DEPTH, D_MODEL, D_IN), f32) * D_MODEL ** -0.5
    b_f = jax.random.uniform(ks[4], (DEPTH, ATTN_HEADS), f32, minval=1.0, maxval=5.0)
    conv_w = jax.random.normal(ks[5], (DEPTH, CONV_WIDTH, D_CONV), f32) * CONV_WIDTH ** -0.5
    attn_norm_g = 1.0 + 0.02 * jax.random.normal(ks[6], (DEPTH, D_ATTN), f32)
    conv_norm_g = 1.0 + 0.02 * jax.random.normal(ks[7], (DEPTH, D_CONV), f32)
    w_out = jax.random.normal(ks[8], (DEPTH, D_MIX, D_MODEL), f32) * D_MIX ** -0.5
    final_norm_g = 1.0 + 0.02 * jax.random.normal(ks[9], (D_MODEL,), f32)
    return {"x": x, "meta": meta, "norm_g": norm_g, "w_in": w_in, "b_f": b_f,
            "conv_w": conv_w, "attn_norm_g": attn_norm_g, "conv_norm_g": conv_norm_g,
            "w_out": w_out, "final_norm_g": final_norm_g}


def _fwd_reference(x, meta, norm_g, w_in, b_f, conv_w, attn_norm_g, conv_norm_g, w_out, final_norm_g):
    b = x.shape[0]
    meta_b = jnp.broadcast_to(meta.astype(x.dtype)[None], (b, N_META, x.shape[-1]))
    h = jnp.concatenate([meta_b, x], axis=1)
    for layer in range(DEPTH):
        h = _hybrid_layer(h, norm_g[layer], w_in[layer], b_f[layer], conv_w[layer],
                          attn_norm_g[layer], conv_norm_g[layer], w_out[layer])
    return _rmsnorm(h[:, N_META:], final_norm_g)


import jax as _jax
import jax.numpy as _jnp

TWIN_FORMAT = 'train_step'
FWD_PARAMS = ['x', 'meta', 'norm_g', 'w_in', 'b_f', 'conv_w', 'attn_norm_g', 'conv_norm_g', 'w_out', 'final_norm_g']
TWIN_WEIGHTS = ['meta', 'norm_g', 'w_in', 'b_f', 'conv_w', 'attn_norm_g', 'conv_norm_g', 'w_out', 'final_norm_g']
TWIN_DIFF_INPUT = 'x'
TWIN_INPUTS = ['x', 'meta', 'norm_g', 'w_in', 'b_f', 'conv_w', 'attn_norm_g', 'conv_norm_g', 'w_out', 'final_norm_g', 'loss_target', 'm_meta', 'm_norm_g', 'm_w_in', 'm_b_f', 'm_conv_w', 'm_attn_norm_g', 'm_conv_norm_g', 'm_w_out', 'm_final_norm_g', 'v_meta', 'v_norm_g', 'v_w_in', 'v_b_f', 'v_conv_w', 'v_attn_norm_g', 'v_conv_norm_g', 'v_w_out', 'v_final_norm_g']
TWIN_OUTPUTS = ['loss', 'grad_x', 'grad_meta', 'grad_norm_g', 'grad_w_in', 'grad_b_f', 'grad_conv_w', 'grad_attn_norm_g', 'grad_conv_norm_g', 'grad_w_out', 'grad_final_norm_g', 'delta_meta', 'delta_norm_g', 'delta_w_in', 'delta_b_f', 'delta_conv_w', 'delta_attn_norm_g', 'delta_conv_norm_g', 'delta_w_out', 'delta_final_norm_g', 'new_m_meta', 'new_m_norm_g', 'new_m_w_in', 'new_m_b_f', 'new_m_conv_w', 'new_m_attn_norm_g', 'new_m_conv_norm_g', 'new_m_w_out', 'new_m_final_norm_g', 'new_v_meta', 'new_v_norm_g', 'new_v_w_in', 'new_v_b_f', 'new_v_conv_w', 'new_v_attn_norm_g', 'new_v_conv_norm_g', 'new_v_w_out', 'new_v_final_norm_g']
TWIN_LEAF_KINDS = {'loss': 'loss', 'grad_x': 'grad_x', 'grad_meta': 'grad_w', 'grad_norm_g': 'grad_w', 'grad_w_in': 'grad_w', 'grad_b_f': 'grad_w', 'grad_conv_w': 'grad_w', 'grad_attn_norm_g': 'grad_w', 'grad_conv_norm_g': 'grad_w', 'grad_w_out': 'grad_w', 'grad_final_norm_g': 'grad_w', 'delta_meta': 'delta_w', 'delta_norm_g': 'delta_w', 'delta_w_in': 'delta_w', 'delta_b_f': 'delta_w', 'delta_conv_w': 'delta_w', 'delta_attn_norm_g': 'delta_w', 'delta_conv_norm_g': 'delta_w', 'delta_w_out': 'delta_w', 'delta_final_norm_g': 'delta_w', 'new_m_meta': 'new_m', 'new_m_norm_g': 'new_m', 'new_m_w_in': 'new_m', 'new_m_b_f': 'new_m', 'new_m_conv_w': 'new_m', 'new_m_attn_norm_g': 'new_m', 'new_m_conv_norm_g': 'new_m', 'new_m_w_out': 'new_m', 'new_m_final_norm_g': 'new_m', 'new_v_meta': 'new_v', 'new_v_norm_g': 'new_v', 'new_v_w_in': 'new_v', 'new_v_b_f': 'new_v', 'new_v_conv_w': 'new_v', 'new_v_attn_norm_g': 'new_v', 'new_v_conv_norm_g': 'new_v', 'new_v_w_out': 'new_v', 'new_v_final_norm_g': 'new_v'}


def _forward(args):
    return _fwd_reference(*[args[k] for k in FWD_PARAMS])


def _output_shape():
    out = _jax.eval_shape(lambda: _forward(_fwd_setup_inputs(0)))
    return out.shape, out.dtype

N_MICROBATCH = 1
ADAM_LR = 0.001
ADAM_B1 = 0.9
ADAM_B2 = 0.999
ADAM_EPS = 1e-08
ADAM_WD = 0.01
ADAM_STEP = 10
PER_EXAMPLE_BATCH_AXIS = {'x': 0, 'loss_target': 0}
SHARED_INPUTS = []
_WEIGHT_DTYPES = {'meta': _jnp.float32, 'norm_g': _jnp.float32, 'w_in': _jnp.float32, 'b_f': _jnp.float32, 'conv_w': _jnp.float32, 'attn_norm_g': _jnp.float32, 'conv_norm_g': _jnp.float32, 'w_out': _jnp.float32, 'final_norm_g': _jnp.float32}
MOMENT_SCALE = {'meta': 3.935765e-03, 'norm_g': 1.841572e-01, 'w_in': 8.984132e-02, 'b_f': 3.425286e-01, 'conv_w': 9.440184e-02, 'attn_norm_g': 9.346652e-02, 'conv_norm_g': 1.085157e-01, 'w_out': 9.055560e-02, 'final_norm_g': 3.195487e+01}


def _to_microbatches(a, axis):
    t = _jnp.moveaxis(a, axis, 0)
    t = t.reshape((N_MICROBATCH, t.shape[0] // N_MICROBATCH) + t.shape[1:])
    return _jnp.moveaxis(t, 1, axis + 1)


def setup_inputs(seed: int = 0) -> dict:
    inp = _fwd_setup_inputs(seed)
    key = _jax.random.fold_in(_jax.random.key(seed), 7919)
    shape, _ = _output_shape()
    out = dict(inp)
    out["loss_target"] = _jax.random.normal(_jax.random.fold_in(key, 0), shape, _jnp.float32)
    for i, name in enumerate(TWIN_WEIGHTS):
        w = inp[name].astype(_jnp.float32)
        if MOMENT_SCALE is None:
            s = _jnp.sqrt(_jnp.mean(_jnp.square(w)) + 1e-30)
        else:
            s = MOMENT_SCALE[name]
        km, kv = _jax.random.split(_jax.random.fold_in(key, i + 1))
        out[name] = w
        out["m_" + name] = s * _jax.random.normal(km, w.shape, _jnp.float32)
        out["v_" + name] = (s * s) * _jax.random.uniform(kv, w.shape, _jnp.float32, 0.5, 1.5)
    if N_MICROBATCH > 1:
        for name, axis in PER_EXAMPLE_BATCH_AXIS.items():
            out[name] = _to_microbatches(out[name], axis)
    return {'x': out['x'], 'meta': out['meta'], 'norm_g': out['norm_g'], 'w_in': out['w_in'], 'b_f': out['b_f'], 'conv_w': out['conv_w'], 'attn_norm_g': out['attn_norm_g'], 'conv_norm_g': out['conv_norm_g'], 'w_out': out['w_out'], 'final_norm_g': out['final_norm_g'], 'loss_target': out['loss_target'], 'm_meta': out['m_meta'], 'm_norm_g': out['m_norm_g'], 'm_w_in': out['m_w_in'], 'm_b_f': out['m_b_f'], 'm_conv_w': out['m_conv_w'], 'm_attn_norm_g': out['m_attn_norm_g'], 'm_conv_norm_g': out['m_conv_norm_g'], 'm_w_out': out['m_w_out'], 'm_final_norm_g': out['m_final_norm_g'], 'v_meta': out['v_meta'], 'v_norm_g': out['v_norm_g'], 'v_w_in': out['v_w_in'], 'v_b_f': out['v_b_f'], 'v_conv_w': out['v_conv_w'], 'v_attn_norm_g': out['v_attn_norm_g'], 'v_conv_norm_g': out['v_conv_norm_g'], 'v_w_out': out['v_w_out'], 'v_final_norm_g': out['v_final_norm_g']}


def _loss(weights, diff, rest, loss_target):
    with _jax.named_scope("forward"):
        args = {**rest, TWIN_DIFF_INPUT: diff, **{k: w.astype(_WEIGHT_DTYPES[k]) for k, w in weights.items()}}
        y = _forward(args)
    with _jax.named_scope("loss_head"):
        err = _jnp.square(y.astype(_jnp.float32) - loss_target)
        return 0.5 * _jnp.sum(_jnp.mean(err, axis=-1)) if err.ndim else 0.5 * err


def _adamw(w, g, m, v):
    m = ADAM_B1 * m + (1.0 - ADAM_B1) * g
    v = ADAM_B2 * v + (1.0 - ADAM_B2) * _jnp.square(g)
    m_hat = m / (1.0 - ADAM_B1 ** ADAM_STEP)
    v_hat = v / (1.0 - ADAM_B2 ** ADAM_STEP)
    delta = -ADAM_LR * (m_hat / (_jnp.sqrt(v_hat) + ADAM_EPS) + ADAM_WD * w)
    return delta, m, v


def reference(x, meta, norm_g, w_in, b_f, conv_w, attn_norm_g, conv_norm_g, w_out, final_norm_g, loss_target, m_meta, m_norm_g, m_w_in, m_b_f, m_conv_w, m_attn_norm_g, m_conv_norm_g, m_w_out, m_final_norm_g, v_meta, v_norm_g, v_w_in, v_b_f, v_conv_w, v_attn_norm_g, v_conv_norm_g, v_w_out, v_final_norm_g):
    given = dict(x=x, meta=meta, norm_g=norm_g, w_in=w_in, b_f=b_f, conv_w=conv_w, attn_norm_g=attn_norm_g, conv_norm_g=conv_norm_g, w_out=w_out, final_norm_g=final_norm_g, loss_target=loss_target, m_meta=m_meta, m_norm_g=m_norm_g, m_w_in=m_w_in, m_b_f=m_b_f, m_conv_w=m_conv_w, m_attn_norm_g=m_attn_norm_g, m_conv_norm_g=m_conv_norm_g, m_w_out=m_w_out, m_final_norm_g=m_final_norm_g, v_meta=v_meta, v_norm_g=v_norm_g, v_w_in=v_w_in, v_b_f=v_b_f, v_conv_w=v_conv_w, v_attn_norm_g=v_attn_norm_g, v_conv_norm_g=v_conv_norm_g, v_w_out=v_w_out, v_final_norm_g=v_final_norm_g)
    weights = {n: given[n] for n in TWIN_WEIGHTS}
    shared = {n: given[n] for n in SHARED_INPUTS}
    per_example = {n: given[n] for n in ['x']}
    grad_fn = _jax.value_and_grad(_loss, argnums=(0, 1))

    def one_microbatch(ex, loss_target):
        ex = dict(ex)
        diff = ex.pop(TWIN_DIFF_INPUT)
        return grad_fn(weights, diff, {**shared, **ex}, loss_target)

    if N_MICROBATCH == 1:
        loss, (grad_w, grad_x) = one_microbatch(per_example, given["loss_target"])
    else:
        def body(carry, xs):
            loss_sum, grad_sum = carry
            l_k, (gw_k, gx_k) = one_microbatch(xs[0], xs[1])
            with _jax.named_scope("update"):
                return (loss_sum + l_k, _jax.tree.map(_jnp.add, grad_sum, gw_k)), gx_k

        init = (_jnp.zeros((), _jnp.float32), _jax.tree.map(_jnp.zeros_like, weights))
        (loss, grad_w), grad_x = _jax.lax.scan(body, init, (per_example, given["loss_target"]))
    with _jax.named_scope("update"):
        delta_w, new_m, new_v = {}, {}, {}
        for n in TWIN_WEIGHTS:
            delta_w[n], new_m[n], new_v[n] = _adamw(weights[n], grad_w[n], given["m_" + n], given["v_" + n])
    return (loss, grad_x, *[grad_w[n] for n in TWIN_WEIGHTS], *[delta_w[n] for n in TWIN_WEIGHTS],
            *[new_m[n] for n in TWIN_WEIGHTS], *[new_v[n] for n in TWIN_WEIGHTS])
```

```python
import functools

import jax
import jax.numpy as jnp
from jax import lax
from jax.experimental import pallas as pl
from jax.experimental.pallas import tpu as pltpu

F32 = jnp.float32
BF16 = jnp.bfloat16

D = 1024
DA = 512
H = 8
DH = 64
NM = 16
TB = 128
P0 = TB - NM
TT = 3 * TB
NDEV = 8
NSEC = 8
DF = 16
DPROJ = NSEC * DA + DF
WSH = 513
WSHP = 528
WROWS = WSHP + D // NDEV
SROWS = 48
EPS = 1e-6
NEG = -1e30
VMEM_LIMIT = 56 * 1024 * 1024

ADAM_LR = 0.001
ADAM_B1 = 0.9
ADAM_B2 = 0.999
ADAM_EPS = 1e-08
ADAM_WD = 0.01
ADAM_STEP = 10

NT_DIMS = (((1,), (1,)), ((), ()))
TN_DIMS = (((0,), (0,)), ((), ()))
MESH = pl.DeviceIdType.MESH


def _params(n_axes=1, vmem=VMEM_LIMIT):
    return pltpu.CompilerParams(dimension_semantics=("arbitrary",) * n_axes, vmem_limit_bytes=vmem)


def _dot(a, b, dims=None):
    if dims is None:
        return jnp.dot(a, b, preferred_element_type=F32)
    return lax.dot_general(a, b, dims, preferred_element_type=F32)


def _my_place():
    return lax.axis_index("x"), lax.axis_index("y"), lax.axis_index("c")


def _all_gather(x, name):
    def body(x_ref, out_ref, send_sems, recv_sems, local_sem):
        mx, my, mc = _my_place()
        me, sibling = (mx, my, mc), (mx, my, 1 - mc)
        chips = [(1 - mx, my), (mx, 1 - my), (1 - mx, 1 - my)]

        def slot(px, py, pc):
            return out_ref.at[4 * px + 2 * py + pc]

        def copy(k, block, to, src=None):
            return pltpu.make_async_remote_copy(
                src_ref=slot(*block) if src is None else src, dst_ref=slot(*block),
                send_sem=send_sems.at[k], recv_sem=recv_sems.at[k], device_id=to, device_id_type=MESH)

        mine = pltpu.make_async_copy(x_ref, slot(*me), local_sem)
        mine.start()
        first = [copy(0, me, sibling, src=x_ref)]
        first += [copy(1 + j, me, (*chip, mc), src=x_ref) for j, chip in enumerate(chips)]
        for cp in first:
            cp.start()
        passed = [copy(4 + j, (*chip, mc), sibling) for j, chip in enumerate(chips)]
        for j, chip in enumerate(chips):
            copy(1 + j, (*chip, mc), me).wait_recv()
            passed[j].start()
        copy(0, sibling, me).wait_recv()
        for j, chip in enumerate(chips):
            copy(4 + j, (*chip, 1 - mc), me).wait_recv()
        for cp in first + passed:
            cp.wait_send()
        mine.wait()

    return pl.pallas_call(
        body, name=name,
        out_shape=jax.ShapeDtypeStruct((NDEV,) + x.shape, x.dtype),
        in_specs=[pl.BlockSpec(memory_space=pl.ANY)],
        out_specs=pl.BlockSpec(memory_space=pl.ANY),
        scratch_shapes=[pltpu.SemaphoreType.DMA((7,)), pltpu.SemaphoreType.DMA((7,)), pltpu.SemaphoreType.DMA],
    )(x)


def _exchange(parts, name):
    def body(p_ref, out_ref, send_sems, recv_sems, local_sem):
        mx, my, mc = _my_place()
        me = 4 * mx + 2 * my + mc
        mine = pltpu.make_async_copy(p_ref.at[me], out_ref.at[me], local_sem)
        mine.start()

        def peer_of(m):
            return ((1 - mx) if m & 4 else mx, (1 - my) if m & 2 else my, (1 - mc) if m & 1 else mc)

        def copy(m, src_slot, dst_slot):
            px, py, pc = peer_of(m)
            return pltpu.make_async_remote_copy(
                src_ref=p_ref.at[src_slot], dst_ref=out_ref.at[dst_slot],
                send_sem=send_sems.at[m - 1], recv_sem=recv_sems.at[m - 1],
                device_id=(px, py, pc), device_id_type=MESH)

        sends = []
        for m in range(1, NDEV):
            px, py, pc = peer_of(m)
            cp = copy(m, 4 * px + 2 * py + pc, me)
            cp.start()
            sends.append(cp)
        for m in range(1, NDEV):
            px, py, pc = peer_of(m)
            copy(m, me, 4 * px + 2 * py + pc).wait_recv()
        for cp in sends:
            cp.wait_send()
        mine.wait()

    return pl.pallas_call(
        body, name=name,
        out_shape=jax.ShapeDtypeStruct(parts.shape, parts.dtype),
        in_specs=[pl.BlockSpec(memory_space=pl.ANY)],
        out_specs=pl.BlockSpec(memory_space=pl.ANY),
        scratch_shapes=[pltpu.SemaphoreType.DMA((7,)), pltpu.SemaphoreType.DMA((7,)), pltpu.SemaphoreType.DMA],
    )(parts)


def _h_block(t, x_ref, meta_ref):
    first = jnp.concatenate([jnp.zeros((P0, D), F32), meta_ref[...]], axis=0)
    return jnp.where(t == 0, first, x_ref[...])


def _x_spec():
    return pl.BlockSpec((TB, D), lambda t: (jnp.maximum(t - 1, 0), 0))


def _full_spec(shape):
    return pl.BlockSpec(shape, lambda *_: (0,) * len(shape))


def _sigmoid(z):
    return 1.0 / (1.0 + jnp.exp(-z))


def _grouped(x):
    return x.reshape(H, DH, x.shape[-1])


def _group_rstd(x3):
    return lax.rsqrt(jnp.mean(x3 * x3, axis=1, keepdims=True) + EPS)


def _lane_tiles_sum(x):
    out = x[:, :TB]
    for i in range(1, x.shape[1] // TB):
        out = out + x[:, i * TB:(i + 1) * TB]
    return out


def _inproj_fwd(x, meta_full, norm_g, w_t, L):
    nb = L // TB

    def body(x_ref, meta_ref, g_ref, w_ref, u_ref, proj_ref, f_ref, ktok_ref, vtok_ref):
        t = pl.program_id(0)
        hb = _h_block(t, x_ref, meta_ref)
        r = lax.rsqrt(jnp.mean(hb * hb, axis=-1, keepdims=True) + EPS)
        u = (hb * r * g_ref[...]).astype(BF16)
        u_ref[...] = u
        for s in range(NSEC):
            p = _dot(w_ref[s * DA:(s + 1) * DA, :], u, NT_DIMS)
            if s == 0:
                p = p * (DH ** -0.5)
            proj_ref[s * DA:(s + 1) * DA, :] = p.astype(BF16)
        f_ref[...] = _dot(w_ref[NSEC * DA:DPROJ, :], u, NT_DIMS)[:H]
        k_tm = _dot(u, w_ref[DA:2 * DA, :], NT_DIMS)
        v_tm = _dot(u, w_ref[2 * DA:3 * DA, :], NT_DIMS)
        for h in range(H):
            ktok_ref[h] = k_tm[:, h * DH:(h + 1) * DH].astype(BF16)
            vtok_ref[h] = v_tm[:, h * DH:(h + 1) * DH].astype(BF16)

    return pl.pallas_call(
        body, name="inproj_fwd", grid=(nb,),
        in_specs=[_x_spec(), _full_spec((NM, D)), _full_spec((1, D)), _full_spec((DPROJ, D))],
        out_specs=[
            pl.BlockSpec((TB, D), lambda t: (t, 0)),
            pl.BlockSpec((NSEC * DA, TB), lambda t: (0, t)),
            pl.BlockSpec((H, TB), lambda t: (0, t)),
            pl.BlockSpec((H, TB, DH), lambda t: (0, t, 0)),
            pl.BlockSpec((H, TB, DH), lambda t: (0, t, 0)),
        ],
        out_shape=[
            jax.ShapeDtypeStruct((L, D), BF16),
            jax.ShapeDtypeStruct((NSEC * DA, L), BF16),
            jax.ShapeDtypeStruct((H, L), F32),
            jax.ShapeDtypeStruct((H, L, DH), BF16),
            jax.ShapeDtypeStruct((H, L, DH), BF16),
        ],
        compiler_params=_params(),
    )(x, meta_full, norm_g, w_t)


def _fgate_fwd(f_t, b_col, L):
    nb = L // TB

    def body(f_ref, b_ref, cq_ref, ckb_ref, sg_ref):
        z = f_ref[...] + b_ref[...]
        idx = lax.broadcasted_iota(jnp.int32, (H, L), 1)
        real = idx >= P0
        lf = jnp.where(real, jnp.minimum(z, 0.0) - jnp.log1p(jnp.exp(-jnp.abs(z))), 0.0)
        sg_ref[...] = jnp.where(real, 1.0 / (1.0 + jnp.exp(z)), 0.0)
        c = lf
        s = 1
        while s < L:
            c = c + jnp.where(idx >= s, pltpu.roll(c, s, 1), 0.0)
            s *= 2
        for h in range(H):
            cq_ref[h] = c[h:h + 1, :]
        ck = jnp.where(real, c, -NEG)
        for h in range(H):
            for b in range(nb):
                row = ck[h:h + 1, b * TB:(b + 1) * TB]
                ckb_ref[h, b * TB:(b + 1) * TB, :] = jnp.broadcast_to(row, (TB, TB)).T

    return pl.pallas_call(
        body, name="fgate_fwd",
        out_shape=[
            jax.ShapeDtypeStruct((H, 1, L), F32),
            jax.ShapeDtypeStruct((H, L, TB), F32),
            jax.ShapeDtypeStruct((H, L), F32),
        ],
        compiler_params=pltpu.CompilerParams(vmem_limit_bytes=VMEM_LIMIT),
    )(f_t, b_col)


def _causal_mask():
    r = lax.broadcasted_iota(jnp.int32, (TT, TT), 0)
    c = lax.broadcasted_iota(jnp.int32, (TT, TT), 1)
    return r <= c


def _attn_fwd(proj_t, ktok, cq, ckb, L):
    nq = L // TT

    def body(q_ref, ktok_ref, v_ref, cq_ref, ckb_ref, o_ref, lse_ref):
        j = pl.program_id(1)
        q = q_ref[...]
        cq_row = cq_ref[0]

        def step(k_off, carry, masked):
            m, l, acc = carry
            k = ktok_ref[0, pl.ds(k_off, TT), :]
            s = _dot(k, q) + cq_row - jnp.tile(ckb_ref[0, pl.ds(k_off, TT), :], (1, TT // TB))
            if masked:
                s = jnp.where(_causal_mask(), s, NEG)
            m_new = jnp.maximum(m, jnp.max(s, axis=0, keepdims=True))
            alpha = jnp.exp(m - m_new)
            p = jnp.exp(s - m_new)
            l = alpha * l + jnp.sum(p, axis=0, keepdims=True)
            v = v_ref[:, pl.ds(k_off, TT)]
            acc = alpha * acc + _dot(v, p.astype(BF16))
            return m_new, l, acc

        init = (jnp.full((1, TT), NEG, F32), jnp.zeros((1, TT), F32), jnp.zeros((DH, TT), F32))
        carry = lax.fori_loop(0, j, lambda i, c: step(pl.multiple_of(i * TT, TT), c, False), init)
        m, l, acc = step(pl.multiple_of(j * TT, TT), carry, True)
        o_ref[...] = acc * (1.0 / l)
        lse_ref[0] = m + jnp.log(l)

    return pl.pallas_call(
        body, name="attn_fwd", grid=(H, nq),
        in_specs=[
            pl.BlockSpec((DH, TT), lambda h, j: (h, j)),
            pl.BlockSpec((1, L, DH), lambda h, j: (h, 0, 0)),
            pl.BlockSpec((DH, L), lambda h, j: (2 * H + h, 0)),
            pl.BlockSpec((1, 1, TT), lambda h, j: (h, 0, j)),
            pl.BlockSpec((1, L, TB), lambda h, j: (h, 0, 0)),
        ],
        out_specs=[
            pl.BlockSpec((DH, TT), lambda h, j: (h, j)),
            pl.BlockSpec((1, 1, TT), lambda h, j: (h, 0, j)),
        ],
        out_shape=[jax.ShapeDtypeStruct((DA, L), F32), jax.ShapeDtypeStruct((H, 1, L), F32)],
        compiler_params=_params(2),
    )(proj_t, ktok, proj_t, cq, ckb)


def _gate_common(o, za, gb, gc, xc, zc, gcp, xcp, cw_ref, ga_ref, gcn_ref, first):
    n_rep = TT // TB
    a = gc * xc
    a_prev = jnp.where(first, 0.0, gcp * xcp)
    full = jnp.concatenate([a_prev, a], axis=1)
    a1 = pltpu.roll(full, 1, 1)[:, TB:]
    a2 = pltpu.roll(full, 2, 1)[:, TB:]
    w0 = jnp.tile(cw_ref[0], (1, n_rep))
    w1 = jnp.tile(cw_ref[1], (1, n_rep))
    w2 = jnp.tile(cw_ref[2], (1, n_rep))
    cv = w0 * a2 + w1 * a1 + w2 * a
    e = gb * cv
    e3 = _grouped(e)
    rc = _group_rstd(e3)
    ec = (e3 * rc).reshape(DA, TT)
    o3 = _grouped(o)
    ra = _group_rstd(o3)
    oa = (o3 * ra).reshape(DA, TT)
    g_a = jnp.tile(ga_ref[...], (1, n_rep))
    g_c = jnp.tile(gcn_ref[...], (1, n_rep))
    sa = _sigmoid(za)
    sc = _sigmoid(zc)
    return dict(a=a, a1=a1, a2=a2, w0=w0, w1=w1, w2=w2, cv=cv, e=e, rc=rc, ec=ec, ra=ra, oa=oa,
                g_a=g_a, g_c=g_c, sa=sa, sc=sc)


def _gate_specs(nj, rev):
    def jj(i):
        return (nj - 1 - i) if rev else i

    def sec(s):
        return pl.BlockSpec((DA, TT), lambda i: (s, jj(i)))

    def halo(s):
        return pl.BlockSpec((DA, TB), lambda i: (s, jnp.maximum(3 * jj(i) - 1, 0)))

    return [pl.BlockSpec((DA, TT), lambda i: (0, jj(i))), sec(3), sec(4), sec(5), sec(6), sec(7), halo(5), halo(6),
            _full_spec((3, DA, TB)), _full_spec((DA, TB)), _full_spec((DA, TB))]


def _gate_fwd(o_t, proj_t, cw_b, ga_b, gcn_b, L):
    nj = L // TT

    def body(o_ref, za_ref, gb_ref, gc_ref, xc_ref, zc_ref, gcp_ref, xcp_ref, cw_ref, ga_ref, gcn_ref, mix_ref):
        j = pl.program_id(0)
        f32 = lambda r: r[...].astype(F32)
        za, zc = f32(za_ref), f32(zc_ref)
        g = _gate_common(o_ref[...], za, f32(gb_ref), f32(gc_ref), f32(xc_ref), zc, f32(gcp_ref), f32(xcp_ref),
                         cw_ref, ga_ref, gcn_ref, j == 0)
        mix_ref[:DA, :] = (g["oa"] * g["g_a"] * (za * g["sa"])).astype(BF16)
        mix_ref[DA:, :] = (g["ec"] * g["g_c"] * (zc * g["sc"])).astype(BF16)

    return pl.pallas_call(
        body, name="gate_fwd", grid=(nj,),
        in_specs=_gate_specs(nj, False),
        out_specs=pl.BlockSpec((2 * DA, TT), lambda j: (0, j)),
        out_shape=jax.ShapeDtypeStruct((2 * DA, L), BF16),
        compiler_params=_params(),
    )(o_t, proj_t, proj_t, proj_t, proj_t, proj_t, proj_t, proj_t, cw_b, ga_b, gcn_b)


def _outproj(mix_t, w_out, x, meta_full, fng, target, L):
    nb = L // TB

    def body(mix_ref, w_ref, x_ref, meta_ref, g_ref, tgt_ref, dout_ref, dmix_ref, dw_ref, loss_ref, dg_ref):
        t = pl.program_id(0)

        @pl.when(t == 0)
        def _():
            dw_ref[...] = jnp.zeros_like(dw_ref)
            loss_ref[...] = jnp.zeros_like(loss_ref)
            dg_ref[...] = jnp.zeros_like(dg_ref)

        mix = mix_ref[...]
        o = _dot(mix, w_ref[...], TN_DIMS) + _h_block(t, x_ref, meta_ref)
        r = lax.rsqrt(jnp.mean(o * o, axis=-1, keepdims=True) + EPS)
        g = g_ref[...]
        orn = o * r
        real = jnp.where(t > 0, 1.0, 0.0)
        diff = (orn * g - tgt_ref[...]) * real
        loss_ref[...] += 0.5 * jnp.sum(diff * diff) * (1.0 / D)
        dy = diff * (1.0 / D)
        dg_ref[...] += jnp.sum(dy * orn, axis=0, keepdims=True)
        gy = dy * g
        dout = r * gy - orn * (r * jnp.mean(gy * orn, axis=-1, keepdims=True))
        dout_ref[...] = dout
        db = dout.astype(BF16)
        dmix_ref[...] = _dot(w_ref[...], db, NT_DIMS).astype(BF16)
        dw_ref[...] += _dot(mix, db)

    return pl.pallas_call(
        body, name="outproj", grid=(nb,),
        in_specs=[pl.BlockSpec((D, TB), lambda t: (0, t)), _full_spec((D, D)), _x_spec(), _full_spec((NM, D)),
                  _full_spec((1, D)), _x_spec()],
        out_specs=[pl.BlockSpec((TB, D), lambda t: (t, 0)), pl.BlockSpec((D, TB), lambda t: (0, t)),
                   _full_spec((D, D)), _full_spec((1, 1)), _full_spec((1, D))],
        out_shape=[jax.ShapeDtypeStruct((L, D), F32), jax.ShapeDtypeStruct((D, L), BF16),
                   jax.ShapeDtypeStruct((D, D), F32), jax.ShapeDtypeStruct((1, 1), F32),
                   jax.ShapeDtypeStruct((1, D), F32)],
        compiler_params=_params(),
    )(mix_t, w_out, x, meta_full, fng, target)


def _gate_bwd(dmix_t, o_t, proj_t, cw_b, ga_b, gcn_b, L):
    nj = L // TT

    def body(dmix_ref, o_ref, za_ref, gb_ref, gc_ref, xc_ref, zc_ref, gcp_ref, xcp_ref, cw_ref, ga_ref, gcn_ref,
             do_ref, dd_ref, dg5_ref, dga_ref, dgc_ref, dcw_ref, carry_ref):
        i = pl.program_id(0)
        j = nj - 1 - i

        @pl.when(i == 0)
        def _():
            carry_ref[...] = jnp.zeros_like(carry_ref)
            dga_ref[...] = jnp.zeros_like(dga_ref)
            dgc_ref[...] = jnp.zeros_like(dgc_ref)
            dcw_ref[...] = jnp.zeros_like(dcw_ref)

        f32 = lambda r: r[...].astype(F32)
        o, za, gb, gc, xc, zc = o_ref[...], f32(za_ref), f32(gb_ref), f32(gc_ref), f32(xc_ref), f32(zc_ref)
        g = _gate_common(o, za, gb, gc, xc, zc, f32(gcp_ref), f32(xcp_ref), cw_ref, ga_ref, gcn_ref, j == 0)
        dya = dmix_ref[:DA, :].astype(F32)
        dyc = dmix_ref[DA:, :].astype(F32)
        sa, sc = g["sa"], g["sc"]

        dn = dya * (za * sa)
        dg5_ref[0:DA, :] = (dya * (g["oa"] * g["g_a"]) * (sa * (1.0 + za * (1.0 - sa)))).astype(BF16)
        dga_ref[...] += _lane_tiles_sum(dn * g["oa"])
        dng = dn * g["g_a"]
        mean_a = jnp.mean(_grouped(dng * g["oa"]), axis=1, keepdims=True)
        do = ((_grouped(dng) - _grouped(g["oa"]) * mean_a) * g["ra"]).reshape(DA, TT)
        do_ref[...] = do.astype(BF16)
        dd = jnp.sum(_grouped(do * o), axis=1)
        for h in range(H):
            dd_ref[h] = dd[h:h + 1, :]

        dnc = dyc * (zc * sc)
        dg5_ref[4 * DA:5 * DA, :] = (dyc * (g["ec"] * g["g_c"]) * (sc * (1.0 + zc * (1.0 - sc)))).astype(BF16)
        dgc_ref[...] += _lane_tiles_sum(dnc * g["ec"])
        dncg = dnc * g["g_c"]
        mean_c = jnp.mean(_grouped(dncg * g["ec"]), axis=1, keepdims=True)
        de = ((_grouped(dncg) - _grouped(g["ec"]) * mean_c) * g["rc"]).reshape(DA, TT)
        dg5_ref[DA:2 * DA, :] = (de * g["cv"]).astype(BF16)
        dcv = de * gb
        full = jnp.concatenate([dcv, carry_ref[...]], axis=1)
        d1 = pltpu.roll(full, TT + TB - 1, 1)[:, :TT]
        d2 = pltpu.roll(full, TT + TB - 2, 1)[:, :TT]
        carry_ref[...] = dcv[:, :TB]
        da = g["w2"] * dcv + g["w1"] * d1 + g["w0"] * d2
        dg5_ref[2 * DA:3 * DA, :] = (da * xc).astype(BF16)
        dg5_ref[3 * DA:4 * DA, :] = (da * gc).astype(BF16)
        dcw_ref[0] += _lane_tiles_sum(dcv * g["a2"])
        dcw_ref[1] += _lane_tiles_sum(dcv * g["a1"])
        dcw_ref[2] += _lane_tiles_sum(dcv * g["a"])

    rj = lambda i: nj - 1 - i
    return pl.pallas_call(
        body, name="gate_bwd", grid=(nj,),
        in_specs=[pl.BlockSpec((2 * DA, TT), lambda i: (0, rj(i)))] + _gate_specs(nj, True),
        out_specs=[
            pl.BlockSpec((DA, TT), lambda i: (0, rj(i))),
            pl.BlockSpec((H, 1, TT), lambda i: (0, 0, rj(i))),
            pl.BlockSpec((5 * DA, TT), lambda i: (0, rj(i))),
            _full_spec((DA, TB)), _full_spec((DA, TB)), _full_spec((3, DA, TB)),
        ],
        out_shape=[
            jax.ShapeDtypeStruct((DA, L), BF16),
            jax.ShapeDtypeStruct((H, 1, L), F32),
            jax.ShapeDtypeStruct((5 * DA, L), BF16),
            jax.ShapeDtypeStruct((DA, TB), F32),
            jax.ShapeDtypeStruct((DA, TB), F32),
            jax.ShapeDtypeStruct((3, DA, TB), F32),
        ],
        scratch_shapes=[pltpu.VMEM((DA, TB), F32)],
        compiler_params=_params(),
    )(dmix_t, o_t, proj_t, proj_t, proj_t, proj_t, proj_t, proj_t, proj_t, cw_b, ga_b, gcn_b)


def _attn_bwd(proj_t, ktok, vtok, do_t, lse, dd, cq, ckb, L):
    nk = L // TT

    def body(q_ref, ktok_ref, vtok_ref, kt_ref, do_ref, lse_ref, dd_ref, cq_ref, ckb_ref,
             dq_ref, dk_ref, dv_ref, dck_ref, dcq_ref, dq_acc):
        i = pl.program_id(1)

        @pl.when(i == 0)
        def _():
            dq_acc[...] = jnp.zeros_like(dq_acc)

        k = ktok_ref[0]
        v = vtok_ref[0]
        ones = jnp.ones((DF, TT), BF16)
        k_t = jnp.concatenate([kt_ref[...], ones], axis=0)
        ck = jnp.tile(ckb_ref[0], (1, TT // TB))

        def step(q_off, carry, masked):
            dv, dk = carry
            q = q_ref[:, pl.ds(q_off, TT)]
            do = do_ref[:, pl.ds(q_off, TT)]
            s = _dot(k, q) + cq_ref[0, :, pl.ds(q_off, TT)] - ck
            p = jnp.exp(s - lse_ref[0, :, pl.ds(q_off, TT)])
            if masked:
                p = jnp.where(_causal_mask(), p, 0.0)
            dp = _dot(v, do)
            ds = (p * (dp - dd_ref[0, :, pl.ds(q_off, TT)])).astype(BF16)
            dv = dv + _dot(do, p.astype(BF16), NT_DIMS)
            dk = dk + _dot(jnp.concatenate([q, ones], axis=0), ds, NT_DIMS)
            dq_acc[:, pl.ds(q_off, TT)] += _dot(k_t, ds)
            return dv, dk

        carry = (jnp.zeros((DH, TT), F32), jnp.zeros((DH + DF, TT), F32))
        carry = step(pl.multiple_of(i * TT, TT), carry, True)
        dv, dk = lax.fori_loop(i + 1, nk, lambda jq, c: step(pl.multiple_of(jq * TT, TT), c, False), carry)
        dv_ref[...] = dv.astype(BF16)
        dk_ref[...] = dk[:DH].astype(BF16)
        dck_ref[0] = dk[DH:DH + 1]

        @pl.when(i == nk - 1)
        def _():
            dq_ref[...] = (dq_acc[:DH, :] * (DH ** -0.5)).astype(BF16)
            dcq_ref[0] = dq_acc[DH:DH + 1, :]

    head = lambda h, i: (h, 0)
    row = lambda h, i: (h, 0, 0)
    return pl.pallas_call(
        body, name="attn_bwd", grid=(H, nk),
        in_specs=[
            pl.BlockSpec((DH, L), head),
            pl.BlockSpec((1, TT, DH), lambda h, i: (h, i, 0)),
            pl.BlockSpec((1, TT, DH), lambda h, i: (h, i, 0)),
            pl.BlockSpec((DH, TT), lambda h, i: (H + h, i)),
            pl.BlockSpec((DH, L), head),
            pl.BlockSpec((1, 1, L), row), pl.BlockSpec((1, 1, L), row), pl.BlockSpec((1, 1, L), row),
            pl.BlockSpec((1, TT, TB), lambda h, i: (h, i, 0)),
        ],
        out_specs=[
            pl.BlockSpec((DH, L), head),
            pl.BlockSpec((DH, TT), lambda h, i: (h, i)),
            pl.BlockSpec((DH, TT), lambda h, i: (h, i)),
            pl.BlockSpec((1, 1, TT), lambda h, i: (h, 0, i)),
            pl.BlockSpec((1, 1, L), row),
        ],
        out_shape=[jax.ShapeDtypeStruct((DA, L), BF16), jax.ShapeDtypeStruct((DA, L), BF16),
                   jax.ShapeDtypeStruct((DA, L), BF16), jax.ShapeDtypeStruct((H, 1, L), F32),
                   jax.ShapeDtypeStruct((H, 1, L), F32)],
        scratch_shapes=[pltpu.VMEM((DH + DF, L), F32)],
        compiler_params=_params(2),
    )(proj_t, ktok, vtok, proj_t, do_t, lse, dd, cq, ckb)


def _fgate_bwd(dcq, dck, sg, L):
    def body(dcq_ref, dck_ref, sg_ref, df_ref, db_ref):
        dc = jnp.concatenate([dcq_ref[h] - dck_ref[h] for h in range(H)], axis=0)
        idx = lax.broadcasted_iota(jnp.int32, (H, L), 1)
        r = dc
        s = 1
        while s < L:
            r = r + jnp.where(idx + s < L, pltpu.roll(r, L - s, 1), 0.0)
            s *= 2
        df = r * sg_ref[...]
        db_ref[...] = jnp.sum(df, axis=1, keepdims=True)
        df_ref[...] = jnp.concatenate([df, jnp.zeros((DF - H, L), F32)], axis=0).astype(BF16)

    return pl.pallas_call(
        body, name="fgate_bwd",
        out_shape=[jax.ShapeDtypeStruct((DF, L), BF16), jax.ShapeDtypeStruct((H, 1), F32)],
        compiler_params=pltpu.CompilerParams(vmem_limit_bytes=VMEM_LIMIT),
    )(dcq, dck, sg)


def _inproj_bwd_x(w, dq_t, dk_t, dv_t, dg5_t, df_t, dout, x, meta_full, norm_g, L):
    nb = L // TB
    seq = x.shape[0]

    def body(w_ref, dq_ref, dk_ref, dv_ref, dg5_ref, df_ref, dout_ref, x_ref, meta_ref, g_ref,
             gx_ref, dmeta_ref, dg_ref):
        t = pl.program_id(0)

        @pl.when(t == 0)
        def _():
            dg_ref[...] = jnp.zeros_like(dg_ref)

        du_t = _dot(w_ref[:, 0:DA], dq_ref[...])
        du_t += _dot(w_ref[:, DA:2 * DA], dk_ref[...])
        du_t += _dot(w_ref[:, 2 * DA:3 * DA], dv_ref[...])
        du_t += _dot(w_ref[:, 3 * DA:NSEC * DA], dg5_ref[...])
        du_t += _dot(w_ref[:, NSEC * DA:DPROJ], df_ref[...])
        du = du_t.T
        hb = _h_block(t, x_ref, meta_ref)
        r = lax.rsqrt(jnp.mean(hb * hb, axis=-1, keepdims=True) + EPS)
        hn = hb * r
        dg_ref[...] += jnp.sum(du * hn, axis=0, keepdims=True)
        gu = du * g_ref[...]
        dh = dout_ref[...] + r * gu - hn * (r * jnp.mean(gu * hn, axis=-1, keepdims=True))
        gx_ref[...] = dh

        @pl.when(t == 0)
        def _():
            dmeta_ref[...] = dh[P0:, :]

    blk = lambda rows: pl.BlockSpec((rows, TB), lambda t: (0, t))
    return pl.pallas_call(
        body, name="inproj_bwd_x", grid=(nb,),
        in_specs=[_full_spec((D, DPROJ)), blk(DA), blk(DA), blk(DA), blk(5 * DA), blk(DF),
                  pl.BlockSpec((TB, D), lambda t: (t, 0)), _x_spec(), _full_spec((NM, D)), _full_spec((1, D))],
        out_specs=[_x_spec(), _full_spec((NM, D)), _full_spec((1, D))],
        out_shape=[jax.ShapeDtypeStruct((seq, D), F32), jax.ShapeDtypeStruct((NM, D), F32),
                   jax.ShapeDtypeStruct((1, D), F32)],
        compiler_params=_params(),
    )(w, dq_t, dk_t, dv_t, dg5_t, df_t, dout, x, meta_full, norm_g)


def _inproj_bwd_w(u, dq_t, dk_t, dv_t, dg5_t, df_t, L):
    kt = L // 3
    nkt = 3

    def body(u_ref, dq_ref, dk_ref, dv_ref, dg5_ref, df_ref, dw_ref, dwf_ref):
        s = pl.program_id(0)
        k = pl.program_id(1)

        @pl.when(k == 0)
        def _():
            dw_ref[...] = jnp.zeros_like(dw_ref)

        @pl.when((s == 0) & (k == 0))
        def _():
            dwf_ref[...] = jnp.zeros_like(dwf_ref)

        u_blk = u_ref[...]
        for sec, ref in ((0, dq_ref), (1, dk_ref), (2, dv_ref)):
            @pl.when(s == sec)
            def _(ref=ref):
                dw_ref[...] += _dot(ref[...], u_blk)

        @pl.when(s >= 3)
        def _():
            dw_ref[...] += _dot(dg5_ref[...], u_blk)

        @pl.when(s == NSEC - 1)
        def _():
            dwf_ref[...] += _dot(df_ref[...], u_blk)

    def only(sec):
        return lambda s, k: (0, jnp.where(s == sec, k, 0))

    return pl.pallas_call(
        body, name="inproj_bwd_w", grid=(NSEC, nkt),
        in_specs=[
            pl.BlockSpec((kt, D), lambda s, k: (k, 0)),
            pl.BlockSpec((DA, kt), only(0)), pl.BlockSpec((DA, kt), only(1)), pl.BlockSpec((DA, kt), only(2)),
            pl.BlockSpec((DA, kt), lambda s, k: (jnp.maximum(s - 3, 0), jnp.where(s >= 3, k, 0))),
            pl.BlockSpec((DF, kt), only(NSEC - 1)),
        ],
        out_specs=[pl.BlockSpec((DA, D), lambda s, k: (s, 0)), _full_spec((DF, D))],
        out_shape=[jax.ShapeDtypeStruct((NSEC * DA, D), F32), jax.ShapeDtypeStruct((DF, D), F32)],
        compiler_params=_params(2),
    )(u, dq_t, dk_t, dv_t, dg5_t, df_t)


def _adamw(w, g, m, v):
    m = ADAM_B1 * m + (1.0 - ADAM_B1) * g
    v = ADAM_B2 * v + (1.0 - ADAM_B2) * (g * g)
    m_hat = m / (1.0 - ADAM_B1 ** ADAM_STEP)
    v_hat = v / (1.0 - ADAM_B2 ** ADAM_STEP)
    delta = -ADAM_LR * (m_hat / (jnp.sqrt(v_hat) + ADAM_EPS) + ADAM_WD * w)
    return delta, m, v


def _sum_slabs(ref, rows):
    g = ref[0, rows, :].astype(F32)
    for j in range(1, NDEV):
        g = g + ref[j, rows, :].astype(F32)
    return g


def _adamw_big(recv, w_in, m_in, v_in, w_out, m_out, v_out):
    cb = 256
    e_sh = D // NDEV
    in_shape = jax.ShapeDtypeStruct(w_in.shape, F32)
    out_shape = jax.ShapeDtypeStruct(w_out.shape, F32)

    def body(r_ref, wi_ref, mi_ref, vi_ref, wo_ref, mo_ref, vo_ref, gi, di, mi, vi, go, do, mo, vo):
        g = _sum_slabs(r_ref, slice(0, WSHP)).T[:, :WSH]
        d, mn, vn = _adamw(wi_ref[0], g, mi_ref[0], vi_ref[0])
        gi[0], di[0], mi[0], vi[0] = g, d, mn, vn
        g = _sum_slabs(r_ref, slice(WSHP, WROWS))
        d, mn, vn = _adamw(wo_ref[0], g, mo_ref[0], vo_ref[0])
        go[0], do[0], mo[0], vo[0] = g, d, mn, vn

    ispec = pl.BlockSpec((1, cb, WSH), lambda i: (0, i, 0))
    ospec = pl.BlockSpec((1, e_sh, cb), lambda i: (0, 0, i))
    return pl.pallas_call(
        body, name="adamw_big", grid=(D // cb,),
        in_specs=[pl.BlockSpec((NDEV, WROWS, cb), lambda i: (0, 0, i)), ispec, ispec, ispec, ospec, ospec, ospec],
        out_specs=[ispec] * 4 + [ospec] * 4, out_shape=[in_shape] * 4 + [out_shape] * 4,
        compiler_params=_params(),
    )(recv, w_in, m_in, v_in, w_out, m_out, v_out)


def _adamw_small(recv, w, m, v):
    shape = jax.ShapeDtypeStruct((SROWS, TB), F32)

    def body(r_ref, w_ref, m_ref, v_ref, g_out, d_out, m_out, v_out):
        g = _sum_slabs(r_ref, slice(0, SROWS))
        d, mn, vn = _adamw(w_ref[...], g, m_ref[...], v_ref[...])
        g_out[...], d_out[...], m_out[...], v_out[...] = g, d, mn, vn

    return pl.pallas_call(body, name="adamw_small", out_shape=[shape] * 4)(recv, w, m, v)


def _pack_small(norm_g, final_norm_g, attn_norm_g, conv_norm_g, b_f, meta_sh, conv_w_sh):
    rows = [
        norm_g.reshape(8, TB), final_norm_g.reshape(8, TB), attn_norm_g.reshape(4, TB), conv_norm_g.reshape(4, TB),
        jnp.pad(b_f.reshape(1, H), ((0, 0), (0, TB - H))),
        meta_sh.reshape(NM, TB),
        jnp.pad(conv_w_sh.reshape(3, DH), ((0, 0), (0, TB - DH))),
    ]
    packed = jnp.concatenate(rows, axis=0)
    return jnp.pad(packed, ((0, SROWS - packed.shape[0]), (0, 0)))


def _unpack_small(p):
    return dict(
        norm_g=p[0:8].reshape(1, D), final_norm_g=p[8:16].reshape(D), attn_norm_g=p[16:20].reshape(1, DA),
        conv_norm_g=p[20:24].reshape(1, DA), b_f=p[24:25, :H].reshape(1, H), meta=p[25:41].reshape(NM, TB),
        conv_w=p[41:44, :DH].reshape(1, 3, DH))


def kernel(x, meta, norm_g, w_in, b_f, conv_w, attn_norm_g, conv_norm_g, w_out, final_norm_g, loss_target, m_meta, m_norm_g, m_w_in, m_b_f, m_conv_w, m_attn_norm_g, m_conv_norm_g, m_w_out, m_final_norm_g, v_meta, v_norm_g, v_w_in, v_b_f, v_conv_w, v_attn_norm_g, v_conv_norm_g, v_w_out, v_final_norm_g):
    seq = x.shape[1]
    L = seq + TB
    assert x.shape == (1, seq, D) and L % TT == 0 and w_in.shape == (1, D, WSH)
    x2 = x[0]
    tgt = loss_target[0]

    w_slab = jnp.concatenate(
        [jnp.pad(w_in[0].T, ((0, WSHP - WSH), (0, 0))), w_out[0]], axis=0).astype(BF16)
    small = jnp.concatenate([meta, jnp.pad(conv_w[0], ((0, 0), (0, TB - DH))),
                             jnp.zeros((5, TB), F32)], axis=0)
    w_all = _all_gather(w_slab, "gather_weights")
    small_all = _all_gather(small, "gather_small")

    w_ref_order = jnp.concatenate([w_all[i, :WSH] for i in range(NDEV)], axis=0)
    f0 = 3 * DA
    w_t = jnp.concatenate([w_ref_order[:f0], w_ref_order[f0 + H:], w_ref_order[f0:f0 + H],
                           jnp.zeros((DF - H, D), BF16)], axis=0)
    w_n = w_t.T
    w_out_full = w_all[:, WSHP:, :].reshape(D, D)
    meta_full = jnp.transpose(small_all[:, :NM, :], (1, 0, 2)).reshape(NM, D)
    conv_w_full = jnp.transpose(small_all[:, NM:NM + 3, :DH], (1, 0, 2)).reshape(3, DA)

    lane_b = lambda p: jnp.broadcast_to(p.reshape(-1, DA, 1), (p.size // DA, DA, TB))
    cw_b = lane_b(conv_w_full)
    ga_b = lane_b(attn_norm_g)[0]
    gcn_b = lane_b(conv_norm_g)[0]

    u, proj_t, f_t, ktok, vtok = _inproj_fwd(x2, meta_full, norm_g, w_t, L)
    cq, ckb, sg = _fgate_fwd(f_t, b_f.reshape(H, 1), L)
    o_t, lse = _attn_fwd(proj_t, ktok, cq, ckb, L)
    mix_t = _gate_fwd(o_t, proj_t, cw_b, ga_b, gcn_b, L)

    dout, dmix_t, dw_out, loss_part, dg_final = _outproj(
        mix_t, w_out_full, x2, meta_full, final_norm_g.reshape(1, D), tgt, L)
    do_t, dd, dg5_t, dga_p, dgc_p, dcw_p = _gate_bwd(dmix_t, o_t, proj_t, cw_b, ga_b, gcn_b, L)
    dq_t, dk_t, dv_t, dck, dcq = _attn_bwd(proj_t, ktok, vtok, do_t, lse, dd, cq, ckb, L)
    df_t, db_f = _fgate_bwd(dcq, dck, sg, L)
    grad_x, dmeta, dg_norm = _inproj_bwd_x(w_n, dq_t, dk_t, dv_t, dg5_t, df_t, dout, x2, meta_full, norm_g, L)
    dw_main, dw_f = _inproj_bwd_w(u, dq_t, dk_t, dv_t, dg5_t, df_t, L)

    dw_ref_order = jnp.concatenate([dw_main[:f0], dw_f[:H], dw_main[f0:]], axis=0)
    dw_slabs = jnp.pad(dw_ref_order.reshape(NDEV, WSH, D), ((0, 0), (0, WSHP - WSH), (0, 0)))
    big_parts = jnp.concatenate([dw_slabs, dw_out.reshape(NDEV, D // NDEV, D)], axis=1).astype(BF16)
    dga = jnp.sum(dga_p, axis=1)
    dgc = jnp.sum(dgc_p, axis=1)
    dcw = jnp.sum(dcw_p, axis=2)
    small_parts = jnp.stack([
        _pack_small(dg_norm, dg_final, dga, dgc, db_f, dmeta[:, j * TB:(j + 1) * TB], dcw[:, j * DH:(j + 1) * DH])
        for j in range(NDEV)], axis=0)
    big_recv = _exchange(big_parts, "exchange_weight_grads")
    small_recv = _exchange(small_parts, "exchange_small_grads")

    (g_w_in, d_w_in, nm_w_in, nv_w_in, g_w_out, d_w_out, nm_w_out, nv_w_out) = _adamw_big(
        big_recv, w_in, m_w_in, v_w_in, w_out, m_w_out, v_w_out)
    wp = _pack_small(norm_g, final_norm_g, attn_norm_g, conv_norm_g, b_f, meta, conv_w)
    mp = _pack_small(m_norm_g, m_final_norm_g, m_attn_norm_g, m_conv_norm_g, m_b_f, m_meta, m_conv_w)
    vp = _pack_small(v_norm_g, v_final_norm_g, v_attn_norm_g, v_conv_norm_g, v_b_f, v_meta, v_conv_w)
    sm = [_unpack_small(p) for p in _adamw_small(small_recv, wp, mp, vp)]

    loss = lax.psum(loss_part[0, 0], ("x", "y", "c"))
    order = ("meta", "norm_g", "w_in", "b_f", "conv_w", "attn_norm_g", "conv_norm_g", "w_out", "final_norm_g")
    groups = []
    for k, (big_in, big_out) in enumerate(((g_w_in, g_w_out), (d_w_in, d_w_out), (nm_w_in, nm_w_out),
                                           (nv_w_in, nv_w_out))):
        d = dict(sm[k], w_in=big_in, w_out=big_out)
        groups.append([d[n] for n in order])
    return (loss, grad_x[None], *groups[0], *groups[1], *groups[2], *groups[3])
```

```python
import functools

import jax
import jax.numpy as jnp
from jax import lax
from jax.experimental import pallas as pl
from jax.experimental.pallas import tpu as pltpu

F32 = jnp.float32
BF16 = jnp.bfloat16

D = 1024
DA = 512
H = 8
DH = 64
NM = 16
TB = 128
P0 = TB - NM
TT = 3 * TB
HG = 2
NDEV = 8
NSEC = 8
DF = 16
DPROJ = NSEC * DA + DF
WSH = 513
WSHP = 528
WROWS = WSHP + D // NDEV
SROWS = 64
EPS = 1e-6
NEG = -1e30
VMEM_LIMIT = 56 * 1024 * 1024

ADAM_LR = 0.001
ADAM_B1 = 0.9
ADAM_B2 = 0.999
ADAM_EPS = 1e-08
ADAM_WD = 0.01
ADAM_STEP = 10

NT_DIMS = (((1,), (1,)), ((), ()))
TN_DIMS = (((0,), (0,)), ((), ()))
MESH = pl.DeviceIdType.MESH


def _params(n_axes=1, vmem=VMEM_LIMIT):
    return pltpu.CompilerParams(dimension_semantics=("arbitrary",) * n_axes, vmem_limit_bytes=vmem)


def _dot(a, b, dims=None):
    if dims is None:
        return jnp.dot(a, b, preferred_element_type=F32)
    return lax.dot_general(a, b, dims, preferred_element_type=F32)


def _my_place():
    return lax.axis_index("x"), lax.axis_index("y"), lax.axis_index("c")


def _all_gather(x, name):
    def body(x_ref, out_ref, send_sems, recv_sems, local_sem):
        mx, my, mc = _my_place()
        me, sibling = (mx, my, mc), (mx, my, 1 - mc)
        chips = [(1 - mx, my), (mx, 1 - my), (1 - mx, 1 - my)]

        def slot(px, py, pc):
            return out_ref.at[4 * px + 2 * py + pc]

        def copy(k, block, to, src=None):
            return pltpu.make_async_remote_copy(
                src_ref=slot(*block) if src is None else src, dst_ref=slot(*block),
                send_sem=send_sems.at[k], recv_sem=recv_sems.at[k], device_id=to, device_id_type=MESH)

        mine = pltpu.make_async_copy(x_ref, slot(*me), local_sem)
        mine.start()
        first = [copy(0, me, sibling, src=x_ref)]
        first += [copy(1 + j, me, (*chip, mc), src=x_ref) for j, chip in enumerate(chips)]
        for cp in first:
            cp.start()
        passed = [copy(4 + j, (*chip, mc), sibling) for j, chip in enumerate(chips)]
        for j, chip in enumerate(chips):
            copy(1 + j, (*chip, mc), me).wait_recv()
            passed[j].start()
        copy(0, sibling, me).wait_recv()
        for j, chip in enumerate(chips):
            copy(4 + j, (*chip, 1 - mc), me).wait_recv()
        for cp in first + passed:
            cp.wait_send()
        mine.wait()

    return pl.pallas_call(
        body, name=name,
        out_shape=jax.ShapeDtypeStruct((NDEV,) + x.shape, x.dtype),
        in_specs=[pl.BlockSpec(memory_space=pl.ANY)],
        out_specs=pl.BlockSpec(memory_space=pl.ANY),
        scratch_shapes=[pltpu.SemaphoreType.DMA((7,)), pltpu.SemaphoreType.DMA((7,)), pltpu.SemaphoreType.DMA],
    )(x)


def _exchange(parts, name):
    def body(p_ref, out_ref, send_sems, recv_sems, local_sem):
        mx, my, mc = _my_place()
        me = 4 * mx + 2 * my + mc
        mine = pltpu.make_async_copy(p_ref.at[me], out_ref.at[me], local_sem)
        mine.start()

        def peer_of(m):
            return ((1 - mx) if m & 4 else mx, (1 - my) if m & 2 else my, (1 - mc) if m & 1 else mc)

        def copy(m, src_slot, dst_slot):
            px, py, pc = peer_of(m)
            return pltpu.make_async_remote_copy(
                src_ref=p_ref.at[src_slot], dst_ref=out_ref.at[dst_slot],
                send_sem=send_sems.at[m - 1], recv_sem=recv_sems.at[m - 1],
                device_id=(px, py, pc), device_id_type=MESH)

        sends = []
        for m in range(1, NDEV):
            px, py, pc = peer_of(m)
            cp = copy(m, 4 * px + 2 * py + pc, me)
            cp.start()
            sends.append(cp)
        for m in range(1, NDEV):
            px, py, pc = peer_of(m)
            copy(m, me, 4 * px + 2 * py + pc).wait_recv()
        for cp in sends:
            cp.wait_send()
        mine.wait()

    return pl.pallas_call(
        body, name=name,
        out_shape=jax.ShapeDtypeStruct(parts.shape, parts.dtype),
        in_specs=[pl.BlockSpec(memory_space=pl.ANY)],
        out_specs=pl.BlockSpec(memory_space=pl.ANY),
        scratch_shapes=[pltpu.SemaphoreType.DMA((7,)), pltpu.SemaphoreType.DMA((7,)), pltpu.SemaphoreType.DMA],
    )(parts)


def _h_block(t, x_ref, meta_ref):
    first = jnp.concatenate([jnp.zeros((P0, D), F32), meta_ref[...]], axis=0)
    return jnp.where(t == 0, first, x_ref[...])


def _x_spec():
    return pl.BlockSpec((TB, D), lambda t: (jnp.maximum(t - 1, 0), 0))


def _x_specs3():
    return [pl.BlockSpec((TB, D), lambda j: (jnp.maximum(3 * j - 1, 0), 0)),
            pl.BlockSpec((TB, D), lambda j: (3 * j, 0)),
            pl.BlockSpec((TB, D), lambda j: (3 * j + 1, 0))]


def _h_tile(j, xa_ref, xb_ref, xc_ref, meta_ref):
    first = jnp.concatenate([jnp.zeros((P0, D), F32), meta_ref[...]], axis=0)
    return jnp.concatenate([jnp.where(j == 0, first, xa_ref[...]), xb_ref[...], xc_ref[...]], axis=0)


def _full_spec(shape):
    return pl.BlockSpec(shape, lambda *_: (0,) * len(shape))


def _sigmoid(z):
    return 1.0 / (1.0 + jnp.exp(-z))


def _grouped(x):
    return x.reshape(H, DH, x.shape[-1])


def _group_rstd(x3):
    return lax.rsqrt(jnp.mean(x3 * x3, axis=1, keepdims=True) + EPS)


def _lane_tiles_sum(x):
    out = x[:, :TB]
    for i in range(1, x.shape[1] // TB):
        out = out + x[:, i * TB:(i + 1) * TB]
    return out


def _inproj_fwd(x, meta_full, norm_g, w_t, L):
    nj = L // TT

    def body(xa_ref, xb_ref, xc_ref, meta_ref, g_ref, w_ref, u_ref, proj_ref, f_ref, ktok_ref, vtok_ref):
        hb = _h_tile(pl.program_id(0), xa_ref, xb_ref, xc_ref, meta_ref)
        r = lax.rsqrt(jnp.mean(hb * hb, axis=-1, keepdims=True) + EPS)
        u = (hb * r * g_ref[...]).astype(BF16)
        u_ref[...] = u
        for s in range(NSEC):
            p = _dot(w_ref[s * DA:(s + 1) * DA, :], u, NT_DIMS)
            if s == 0:
                p = p * (DH ** -0.5)
            proj_ref[s * DA:(s + 1) * DA, :] = p.astype(BF16)
        f_ref[...] = _dot(w_ref[NSEC * DA:DPROJ, :], u, NT_DIMS)[:H]
        k_tm = _dot(u, w_ref[DA:2 * DA, :], NT_DIMS)
        v_tm = _dot(u, w_ref[2 * DA:3 * DA, :], NT_DIMS)
        for h in range(H):
            ktok_ref[h] = k_tm[:, h * DH:(h + 1) * DH].astype(BF16)
            vtok_ref[h] = v_tm[:, h * DH:(h + 1) * DH].astype(BF16)

    return pl.pallas_call(
        body, name="inproj_fwd", grid=(nj,),
        in_specs=_x_specs3() + [_full_spec((NM, D)), _full_spec((1, D)), _full_spec((DPROJ, D))],
        out_specs=[
            pl.BlockSpec((TT, D), lambda t: (t, 0)),
            pl.BlockSpec((NSEC * DA, TT), lambda t: (0, t)),
            pl.BlockSpec((H, TT), lambda t: (0, t)),
            pl.BlockSpec((H, TT, DH), lambda t: (0, t, 0)),
            pl.BlockSpec((H, TT, DH), lambda t: (0, t, 0)),
        ],
        out_shape=[
            jax.ShapeDtypeStruct((L, D), BF16),
            jax.ShapeDtypeStruct((NSEC * DA, L), BF16),
            jax.ShapeDtypeStruct((H, L), F32),
            jax.ShapeDtypeStruct((H, L, DH), BF16),
            jax.ShapeDtypeStruct((H, L, DH), BF16),
        ],
        compiler_params=_params(),
    )(x, x, x, meta_full, norm_g, w_t)


def _fgate_fwd(f_t, b_col, L):
    nb = L // TB

    def body(f_ref, b_ref, cq_ref, ckb_ref, sg_ref):
        z = f_ref[...] + b_ref[...]
        idx = lax.broadcasted_iota(jnp.int32, (H, L), 1)
        real = idx >= P0
        lf = jnp.where(real, jnp.minimum(z, 0.0) - jnp.log1p(jnp.exp(-jnp.abs(z))), 0.0)
        sg_ref[...] = jnp.where(real, 1.0 / (1.0 + jnp.exp(z)), 0.0)
        c = lf
        s = 1
        while s < L:
            c = c + jnp.where(idx >= s, pltpu.roll(c, s, 1), 0.0)
            s *= 2
        for h in range(H):
            cq_ref[h] = c[h:h + 1, :]
        ck = jnp.where(real, c, -NEG)
        for h in range(H):
            for b in range(nb):
                row = ck[h:h + 1, b * TB:(b + 1) * TB]
                ckb_ref[h, b * TB:(b + 1) * TB, :] = jnp.broadcast_to(row, (TB, TB)).T

    return pl.pallas_call(
        body, name="fgate_fwd",
        out_shape=[
            jax.ShapeDtypeStruct((H, 1, L), F32),
            jax.ShapeDtypeStruct((H, L, TB), F32),
            jax.ShapeDtypeStruct((H, L), F32),
        ],
        compiler_params=pltpu.CompilerParams(vmem_limit_bytes=VMEM_LIMIT),
    )(f_t, b_col)


def _causal_mask():
    r = lax.broadcasted_iota(jnp.int32, (TT, TT), 0)
    c = lax.broadcasted_iota(jnp.int32, (TT, TT), 1)
    return r <= c


def _attn_fwd(proj_t, ktok, cq, ckb, L):
    nq = L // TT

    def body(q_ref, ktok_ref, v_ref, cq_ref, ckb_ref, o_ref, lse_ref):
        j = pl.program_id(1)
        rows = [slice(g * DH, (g + 1) * DH) for g in range(HG)]
        qs = [q_ref[r, :] for r in rows]

        def scores(k_off):
            return tuple(
                _dot(ktok_ref[g, pl.ds(k_off, TT), :], qs[g])
                - jnp.tile(ckb_ref[g, pl.ds(k_off, TT), :], (1, TT // TB)) for g in range(HG))

        def consume(k_off, ss, carry, masked):
            out = []
            for g in range(HG):
                m, l, acc = carry[g]
                s = jnp.where(_causal_mask(), ss[g], NEG) if masked else ss[g]
                m_new = jnp.maximum(m, jnp.max(s, axis=0, keepdims=True))
                alpha = jnp.exp(m - m_new)
                p = jnp.exp(s - m_new)
                l = alpha * l + jnp.sum(p, axis=0, keepdims=True)
                v = v_ref[rows[g], pl.ds(k_off, TT)]
                acc = alpha * acc + _dot(v, p.astype(BF16))
                out.append((m_new, l, acc))
            return tuple(out)

        def body_i(i, c):
            carry, ss = c
            nxt = scores(pl.multiple_of((i + 1) * TT, TT))
            return consume(pl.multiple_of(i * TT, TT), ss, carry, False), nxt

        init = tuple((jnp.full((1, TT), NEG, F32), jnp.zeros((1, TT), F32), jnp.zeros((DH, TT), F32))
                     for _ in range(HG))
        carry, ss = lax.fori_loop(0, j, body_i, (init, scores(0)))
        carry = consume(pl.multiple_of(j * TT, TT), ss, carry, True)
        for g in range(HG):
            m, l, acc = carry[g]
            o_ref[rows[g], :] = acc * (1.0 / l)
            lse_ref[g] = m + jnp.log(l) + cq_ref[g]

    return pl.pallas_call(
        body, name="attn_fwd", grid=(H // HG, nq),
        in_specs=[
            pl.BlockSpec((HG * DH, TT), lambda h, j: (h, j)),
            pl.BlockSpec((HG, L, DH), lambda h, j: (h, 0, 0)),
            pl.BlockSpec((HG * DH, L), lambda h, j: (2 * H // HG + h, 0)),
            pl.BlockSpec((HG, 1, TT), lambda h, j: (h, 0, j)),
            pl.BlockSpec((HG, L, TB), lambda h, j: (h, 0, 0)),
        ],
        out_specs=[
            pl.BlockSpec((HG * DH, TT), lambda h, j: (h, j)),
            pl.BlockSpec((HG, 1, TT), lambda h, j: (h, 0, j)),
        ],
        out_shape=[jax.ShapeDtypeStruct((DA, L), F32), jax.ShapeDtypeStruct((H, 1, L), F32)],
        compiler_params=_params(2),
    )(proj_t, ktok, proj_t, cq, ckb)


def _gate_common(o, za, gb, gc, xc, zc, gcp, xcp, cw_ref, ga_ref, gcn_ref, first):
    n_rep = TT // TB
    a = gc * xc
    a_prev = jnp.where(first, 0.0, gcp * xcp)
    full = jnp.concatenate([a_prev, a], axis=1)
    a1 = pltpu.roll(full, 1, 1)[:, TB:]
    a2 = pltpu.roll(full, 2, 1)[:, TB:]
    w0 = jnp.tile(cw_ref[0], (1, n_rep))
    w1 = jnp.tile(cw_ref[1], (1, n_rep))
    w2 = jnp.tile(cw_ref[2], (1, n_rep))
    cv = w0 * a2 + w1 * a1 + w2 * a
    e = gb * cv
    e3 = _grouped(e)
    rc = _group_rstd(e3)
    ec = (e3 * rc).reshape(DA, TT)
    o3 = _grouped(o)
    ra = _group_rstd(o3)
    oa = (o3 * ra).reshape(DA, TT)
    g_a = jnp.tile(ga_ref[...], (1, n_rep))
    g_c = jnp.tile(gcn_ref[...], (1, n_rep))
    sa = _sigmoid(za)
    sc = _sigmoid(zc)
    return dict(a=a, a1=a1, a2=a2, w0=w0, w1=w1, w2=w2, cv=cv, e=e, rc=rc, ec=ec, ra=ra, oa=oa,
                g_a=g_a, g_c=g_c, sa=sa, sc=sc)


def _gate_specs(nj, rev):
    def jj(i):
        return (nj - 1 - i) if rev else i

    def sec(s):
        return pl.BlockSpec((DA, TT), lambda i: (s, jj(i)))

    def halo(s):
        return pl.BlockSpec((DA, TB), lambda i: (s, jnp.maximum(3 * jj(i) - 1, 0)))

    return [pl.BlockSpec((DA, TT), lambda i: (0, jj(i))), sec(3), sec(4), sec(5), sec(6), sec(7), halo(5), halo(6),
            _full_spec((3, DA, TB)), _full_spec((DA, TB)), _full_spec((DA, TB))]


def _gate_fwd(o_t, proj_t, cw_b, ga_b, gcn_b, L):
    nj = L // TT

    def body(o_ref, za_ref, gb_ref, gc_ref, xc_ref, zc_ref, gcp_ref, xcp_ref, cw_ref, ga_ref, gcn_ref, mix_ref):
        j = pl.program_id(0)
        f32 = lambda r: r[...].astype(F32)
        za, zc = f32(za_ref), f32(zc_ref)
        g = _gate_common(o_ref[...], za, f32(gb_ref), f32(gc_ref), f32(xc_ref), zc, f32(gcp_ref), f32(xcp_ref),
                         cw_ref, ga_ref, gcn_ref, j == 0)
        mix_ref[:DA, :] = (g["oa"] * g["g_a"] * (za * g["sa"])).astype(BF16)
        mix_ref[DA:, :] = (g["ec"] * g["g_c"] * (zc * g["sc"])).astype(BF16)

    return pl.pallas_call(
        body, name="gate_fwd", grid=(nj,),
        in_specs=_gate_specs(nj, False),
        out_specs=pl.BlockSpec((2 * DA, TT), lambda j: (0, j)),
        out_shape=jax.ShapeDtypeStruct((2 * DA, L), BF16),
        compiler_params=_params(),
    )(o_t, proj_t, proj_t, proj_t, proj_t, proj_t, proj_t, proj_t, cw_b, ga_b, gcn_b)


def _outproj(mix_t, w_out, x, meta_full, fng, target, L):
    nj = L // TT

    def body(mix_ref, w_ref, xa_ref, xb_ref, xc_ref, meta_ref, g_ref, ta_ref, tb_ref, tc_ref,
             dout_ref, dmix_ref, dw_ref, loss_ref, dg_ref):
        t = pl.program_id(0)

        @pl.when(t == 0)
        def _():
            dw_ref[...] = jnp.zeros_like(dw_ref)
            loss_ref[...] = jnp.zeros_like(loss_ref)
            dg_ref[...] = jnp.zeros_like(dg_ref)

        mix = mix_ref[...]
        o = _dot(mix, w_ref[...], TN_DIMS) + _h_tile(t, xa_ref, xb_ref, xc_ref, meta_ref)
        r = lax.rsqrt(jnp.mean(o * o, axis=-1, keepdims=True) + EPS)
        g = g_ref[...]
        orn = o * r
        tgt = jnp.concatenate([ta_ref[...], tb_ref[...], tc_ref[...]], axis=0)
        row = lax.broadcasted_iota(jnp.int32, (TT, 1), 0)
        real = jnp.where((t > 0) | (row >= TB), 1.0, 0.0)
        diff = (orn * g - tgt) * real
        loss_ref[...] += 0.5 * jnp.sum(diff * diff) * (1.0 / D)
        dy = diff * (1.0 / D)
        dg_ref[...] += jnp.sum(dy * orn, axis=0, keepdims=True)
        gy = dy * g
        dout = r * gy - orn * (r * jnp.mean(gy * orn, axis=-1, keepdims=True))
        dout_ref[...] = dout
        db = dout.astype(BF16)
        dmix_ref[...] = _dot(w_ref[...], db, NT_DIMS).astype(BF16)
        dw_ref[...] += _dot(mix, db)

    return pl.pallas_call(
        body, name="outproj", grid=(nj,),
        in_specs=[pl.BlockSpec((D, TT), lambda t: (0, t)), _full_spec((D, D))] + _x_specs3()
                 + [_full_spec((NM, D)), _full_spec((1, D))] + _x_specs3(),
        out_specs=[pl.BlockSpec((TT, D), lambda t: (t, 0)), pl.BlockSpec((D, TT), lambda t: (0, t)),
                   _full_spec((D, D)), _full_spec((1, 1)), _full_spec((1, D))],
        out_shape=[jax.ShapeDtypeStruct((L, D), F32), jax.ShapeDtypeStruct((D, L), BF16),
                   jax.ShapeDtypeStruct((D, D), F32), jax.ShapeDtypeStruct((1, 1), F32),
                   jax.ShapeDtypeStruct((1, D), F32)],
        compiler_params=_params(),
    )(mix_t, w_out, x, x, x, meta_full, fng, target, target, target)


def _gate_bwd(dmix_t, o_t, proj_t, cw_b, ga_b, gcn_b, L):
    nj = L // TT

    def body(dmix_ref, o_ref, za_ref, gb_ref, gc_ref, xc_ref, zc_ref, gcp_ref, xcp_ref, cw_ref, ga_ref, gcn_ref,
             do_ref, dd_ref, dg5_ref, dga_ref, dgc_ref, dcw_ref, carry_ref):
        i = pl.program_id(0)
        j = nj - 1 - i

        @pl.when(i == 0)
        def _():
            carry_ref[...] = jnp.zeros_like(carry_ref)
            dga_ref[...] = jnp.zeros_like(dga_ref)
            dgc_ref[...] = jnp.zeros_like(dgc_ref)
            dcw_ref[...] = jnp.zeros_like(dcw_ref)

        f32 = lambda r: r[...].astype(F32)
        o, za, gb, gc, xc, zc = o_ref[...], f32(za_ref), f32(gb_ref), f32(gc_ref), f32(xc_ref), f32(zc_ref)
        g = _gate_common(o, za, gb, gc, xc, zc, f32(gcp_ref), f32(xcp_ref), cw_ref, ga_ref, gcn_ref, j == 0)
        dya = dmix_ref[:DA, :].astype(F32)
        dyc = dmix_ref[DA:, :].astype(F32)
        sa, sc = g["sa"], g["sc"]

        dn = dya * (za * sa)
        dg5_ref[0:DA, :] = (dya * (g["oa"] * g["g_a"]) * (sa * (1.0 + za * (1.0 - sa)))).astype(BF16)
        dga_ref[...] += _lane_tiles_sum(dn * g["oa"])
        dng = dn * g["g_a"]
        mean_a = jnp.mean(_grouped(dng * g["oa"]), axis=1, keepdims=True)
        do = ((_grouped(dng) - _grouped(g["oa"]) * mean_a) * g["ra"]).reshape(DA, TT)
        do_ref[...] = do.astype(BF16)
        dd = jnp.sum(_grouped(do * o), axis=1)
        for h in range(H):
            dd_ref[h] = dd[h:h + 1, :]

        dnc = dyc * (zc * sc)
        dg5_ref[4 * DA:5 * DA, :] = (dyc * (g["ec"] * g["g_c"]) * (sc * (1.0 + zc * (1.0 - sc)))).astype(BF16)
        dgc_ref[...] += _lane_tiles_sum(dnc * g["ec"])
        dncg = dnc * g["g_c"]
        mean_c = jnp.mean(_grouped(dncg * g["ec"]), axis=1, keepdims=True)
        de = ((_grouped(dncg) - _grouped(g["ec"]) * mean_c) * g["rc"]).reshape(DA, TT)
        dg5_ref[DA:2 * DA, :] = (de * g["cv"]).astype(BF16)
        dcv = de * gb
        full = jnp.concatenate([dcv, carry_ref[...]], axis=1)
        d1 = pltpu.roll(full, TT + TB - 1, 1)[:, :TT]
        d2 = pltpu.roll(full, TT + TB - 2, 1)[:, :TT]
        carry_ref[...] = dcv[:, :TB]
        da = g["w2"] * dcv + g["w1"] * d1 + g["w0"] * d2
        dg5_ref[2 * DA:3 * DA, :] = (da * xc).astype(BF16)
        dg5_ref[3 * DA:4 * DA, :] = (da * gc).astype(BF16)
        dcw_ref[0] += _lane_tiles_sum(dcv * g["a2"])
        dcw_ref[1] += _lane_tiles_sum(dcv * g["a1"])
        dcw_ref[2] += _lane_tiles_sum(dcv * g["a"])

    rj = lambda i: nj - 1 - i
    return pl.pallas_call(
        body, name="gate_bwd", grid=(nj,),
        in_specs=[pl.BlockSpec((2 * DA, TT), lambda i: (0, rj(i)))] + _gate_specs(nj, True),
        out_specs=[
            pl.BlockSpec((DA, TT), lambda i: (0, rj(i))),
            pl.BlockSpec((H, 1, TT), lambda i: (0, 0, rj(i))),
            pl.BlockSpec((5 * DA, TT), lambda i: (0, rj(i))),
            _full_spec((DA, TB)), _full_spec((DA, TB)), _full_spec((3, DA, TB)),
        ],
        out_shape=[
            jax.ShapeDtypeStruct((DA, L), BF16),
            jax.ShapeDtypeStruct((H, 1, L), F32),
            jax.ShapeDtypeStruct((5 * DA, L), BF16),
            jax.ShapeDtypeStruct((DA, TB), F32),
            jax.ShapeDtypeStruct((DA, TB), F32),
            jax.ShapeDtypeStruct((3, DA, TB), F32),
        ],
        scratch_shapes=[pltpu.VMEM((DA, TB), F32)],
        compiler_params=_params(),
    )(dmix_t, o_t, proj_t, proj_t, proj_t, proj_t, proj_t, proj_t, proj_t, cw_b, ga_b, gcn_b)


def _attn_bwd(proj_t, ktok, vtok, do_t, lse, dd, cq, ckb, L):
    nk = L // TT

    def body(q_ref, ktok_ref, vtok_ref, kt_ref, do_ref, lse_ref, dd_ref, cq_ref, ckb_ref,
             dq_ref, dk_ref, dv_ref, dck_ref, dcq_ref, dq_acc):
        i = pl.program_id(1)

        @pl.when(i == 0)
        def _():
            dq_acc[...] = jnp.zeros_like(dq_acc)

        rows = [slice(g * DH, (g + 1) * DH) for g in range(HG)]
        ones = jnp.ones((DF, TT), BF16)
        ks = [ktok_ref[g] for g in range(HG)]
        vs = [vtok_ref[g] for g in range(HG)]
        kts = [jnp.concatenate([kt_ref[r, :], ones], axis=0) for r in rows]

        def step(q_off, carry, masked):
            out = []
            for g in range(HG):
                dv, dk = carry[g]
                q = q_ref[rows[g], pl.ds(q_off, TT)]
                do = do_ref[rows[g], pl.ds(q_off, TT)]
                bias = cq_ref[g, :, pl.ds(q_off, TT)] - lse_ref[g, :, pl.ds(q_off, TT)]
                s = _dot(ks[g], q) - jnp.tile(ckb_ref[g], (1, TT // TB))
                p = jnp.exp(s + bias)
                if masked:
                    p = jnp.where(_causal_mask(), p, 0.0)
                dp = _dot(vs[g], do)
                ds = (p * (dp - dd_ref[g, :, pl.ds(q_off, TT)])).astype(BF16)
                dv = dv + _dot(do, p.astype(BF16), NT_DIMS)
                dk = dk + _dot(jnp.concatenate([q, ones], axis=0), ds, NT_DIMS)
                dq_acc[g, :, pl.ds(q_off, TT)] += _dot(kts[g], ds)
                out.append((dv, dk))
            return tuple(out)

        carry = tuple((jnp.zeros((DH, TT), F32), jnp.zeros((DH + DF, TT), F32)) for _ in range(HG))
        carry = step(pl.multiple_of(i * TT, TT), carry, True)
        carry = lax.fori_loop(i + 1, nk, lambda jq, c: step(pl.multiple_of(jq * TT, TT), c, False), carry)
        for g in range(HG):
            dv, dk = carry[g]
            dv_ref[rows[g], :] = dv.astype(BF16)
            dk_ref[rows[g], :] = dk[:DH].astype(BF16)
            dck_ref[g] = dk[DH:DH + 1]

        @pl.when(i == nk - 1)
        def _():
            for g in range(HG):
                dq_ref[rows[g], :] = (dq_acc[g, :DH, :] * (DH ** -0.5)).astype(BF16)
                dcq_ref[g] = dq_acc[g, DH:DH + 1, :]

    head = lambda h, i: (h, 0)
    row = lambda h, i: (h, 0, 0)
    return pl.pallas_call(
        body, name="attn_bwd", grid=(H // HG, nk),
        in_specs=[
            pl.BlockSpec((HG * DH, L), head),
            pl.BlockSpec((HG, TT, DH), lambda h, i: (h, i, 0)),
            pl.BlockSpec((HG, TT, DH), lambda h, i: (h, i, 0)),
            pl.BlockSpec((HG * DH, TT), lambda h, i: (H // HG + h, i)),
            pl.BlockSpec((HG * DH, L), head),
            pl.BlockSpec((HG, 1, L), row), pl.BlockSpec((HG, 1, L), row), pl.BlockSpec((HG, 1, L), row),
            pl.BlockSpec((HG, TT, TB), lambda h, i: (h, i, 0)),
        ],
        out_specs=[
            pl.BlockSpec((HG * DH, L), head),
            pl.BlockSpec((HG * DH, TT), lambda h, i: (h, i)),
            pl.BlockSpec((HG * DH, TT), lambda h, i: (h, i)),
            pl.BlockSpec((HG, 1, TT), lambda h, i: (h, 0, i)),
            pl.BlockSpec((HG, 1, L), row),
        ],
        out_shape=[jax.ShapeDtypeStruct((DA, L), BF16), jax.ShapeDtypeStruct((DA, L), BF16),
                   jax.ShapeDtypeStruct((DA, L), BF16), jax.ShapeDtypeStruct((H, 1, L), F32),
                   jax.ShapeDtypeStruct((H, 1, L), F32)],
        scratch_shapes=[pltpu.VMEM((HG, DH + DF, L), F32)],
        compiler_params=_params(2),
    )(proj_t, ktok, vtok, proj_t, do_t, lse, dd, cq, ckb)


def _fgate_bwd(dcq, dck, sg, L):
    def body(dcq_ref, dck_ref, sg_ref, df_ref, db_ref):
        dc = jnp.concatenate([dcq_ref[h] - dck_ref[h] for h in range(H)], axis=0)
        idx = lax.broadcasted_iota(jnp.int32, (H, L), 1)
        r = dc
        s = 1
        while s < L:
            r = r + jnp.where(idx + s < L, pltpu.roll(r, L - s, 1), 0.0)
            s *= 2
        df = r * sg_ref[...]
        db_ref[...] = jnp.sum(df, axis=1, keepdims=True)
        df_ref[...] = jnp.concatenate([df, jnp.zeros((DF - H, L), F32)], axis=0).astype(BF16)

    return pl.pallas_call(
        body, name="fgate_bwd",
        out_shape=[jax.ShapeDtypeStruct((DF, L), BF16), jax.ShapeDtypeStruct((H, 1), F32)],
        compiler_params=pltpu.CompilerParams(vmem_limit_bytes=VMEM_LIMIT),
    )(dcq, dck, sg)


def _inproj_bwd_x(w, dq_t, dk_t, dv_t, dg5_t, df_t, dout, x, meta_full, norm_g, L):
    nb = L // TB
    seq = x.shape[0]

    def body(w_ref, dq_ref, dk_ref, dv_ref, dg5_ref, df_ref, dout_ref, x_ref, meta_ref, g_ref,
             gx_ref, dmeta_ref, dg_ref):
        t = pl.program_id(0)

        @pl.when(t == 0)
        def _():
            dg_ref[...] = jnp.zeros_like(dg_ref)

        du = _dot(dq_ref[...], w_ref[0:DA, :], TN_DIMS)
        du += _dot(dk_ref[...], w_ref[DA:2 * DA, :], TN_DIMS)
        du += _dot(dv_ref[...], w_ref[2 * DA:3 * DA, :], TN_DIMS)
        du += _dot(dg5_ref[...], w_ref[3 * DA:NSEC * DA, :], TN_DIMS)
        du += _dot(df_ref[...], w_ref[NSEC * DA:DPROJ, :], TN_DIMS)
        hb = _h_block(t, x_ref, meta_ref)
        r = lax.rsqrt(jnp.mean(hb * hb, axis=-1, keepdims=True) + EPS)
        hn = hb * r
        dg_ref[...] += jnp.sum(du * hn, axis=0, keepdims=True)
        gu = du * g_ref[...]
        dh = dout_ref[...] + r * gu - hn * (r * jnp.mean(gu * hn, axis=-1, keepdims=True))
        gx_ref[...] = dh

        @pl.when(t == 0)
        def _():
            dmeta_ref[...] = dh[P0:, :]

    blk = lambda rows: pl.BlockSpec((rows, TB), lambda t: (0, t))
    return pl.pallas_call(
        body, name="inproj_bwd_x", grid=(nb,),
        in_specs=[_full_spec((DPROJ, D)), blk(DA), blk(DA), blk(DA), blk(5 * DA), blk(DF),
                  pl.BlockSpec((TB, D), lambda t: (t, 0)), _x_spec(), _full_spec((NM, D)), _full_spec((1, D))],
        out_specs=[_x_spec(), _full_spec((NM, D)), _full_spec((1, D))],
        out_shape=[jax.ShapeDtypeStruct((seq, D), F32), jax.ShapeDtypeStruct((NM, D), F32),
                   jax.ShapeDtypeStruct((1, D), F32)],
        compiler_params=_params(),
    )(w, dq_t, dk_t, dv_t, dg5_t, df_t, dout, x, meta_full, norm_g)


def _inproj_bwd_w(u, dq_t, dk_t, dv_t, dg5_t, df_t, L):
    kt = L // 3
    nkt = 3

    def body(u_ref, dq_ref, dk_ref, dv_ref, dg5_ref, df_ref, dw_ref, dwf_ref):
        s = pl.program_id(0)
        k = pl.program_id(1)

        @pl.when(k == 0)
        def _():
            dw_ref[...] = jnp.zeros_like(dw_ref)

        @pl.when((s == 0) & (k == 0))
        def _():
            dwf_ref[...] = jnp.zeros_like(dwf_ref)

        u_blk = u_ref[...]
        for sec, ref in ((0, dq_ref), (1, dk_ref), (2, dv_ref)):
            @pl.when(s == sec)
            def _(ref=ref):
                dw_ref[...] += _dot(ref[...], u_blk)

        @pl.when(s >= 3)
        def _():
            dw_ref[...] += _dot(dg5_ref[...], u_blk)

        @pl.when(s == NSEC - 1)
        def _():
            dwf_ref[...] += _dot(df_ref[...], u_blk)

    def only(sec):
        return lambda s, k: (0, jnp.where(s == sec, k, 0))

    return pl.pallas_call(
        body, name="inproj_bwd_w", grid=(NSEC, nkt),
        in_specs=[
            pl.BlockSpec((kt, D), lambda s, k: (k, 0)),
            pl.BlockSpec((DA, kt), only(0)), pl.BlockSpec((DA, kt), only(1)), pl.BlockSpec((DA, kt), only(2)),
            pl.BlockSpec((DA, kt), lambda s, k: (jnp.maximum(s - 3, 0), jnp.where(s >= 3, k, 0))),
            pl.BlockSpec((DF, kt), only(NSEC - 1)),
        ],
        out_specs=[pl.BlockSpec((DA, D), lambda s, k: (s, 0)), _full_spec((DF, D))],
        out_shape=[jax.ShapeDtypeStruct((NSEC * DA, D), F32), jax.ShapeDtypeStruct((DF, D), F32)],
        compiler_params=_params(2),
    )(u, dq_t, dk_t, dv_t, dg5_t, df_t)


def _adamw(w, g, m, v):
    m = ADAM_B1 * m + (1.0 - ADAM_B1) * g
    v = ADAM_B2 * v + (1.0 - ADAM_B2) * (g * g)
    m_hat = m / (1.0 - ADAM_B1 ** ADAM_STEP)
    v_hat = v / (1.0 - ADAM_B2 ** ADAM_STEP)
    delta = -ADAM_LR * (m_hat / (jnp.sqrt(v_hat) + ADAM_EPS) + ADAM_WD * w)
    return delta, m, v


def _sum_slabs(ref, rows):
    g = ref[0, rows, :].astype(F32)
    for j in range(1, NDEV):
        g = g + ref[j, rows, :].astype(F32)
    return g


def _adamw_big(recv, w_in, m_in, v_in, w_out, m_out, v_out):
    cb = 256
    e_sh = D // NDEV
    in_shape = jax.ShapeDtypeStruct(w_in.shape, F32)
    out_shape = jax.ShapeDtypeStruct(w_out.shape, F32)

    def body(r_ref, wi_ref, mi_ref, vi_ref, wo_ref, mo_ref, vo_ref, gi, di, mi, vi, go, do, mo, vo):
        g = _sum_slabs(r_ref, slice(0, WSHP)).T[:, :WSH]
        d, mn, vn = _adamw(wi_ref[0], g, mi_ref[0], vi_ref[0])
        gi[0], di[0], mi[0], vi[0] = g, d, mn, vn
        g = _sum_slabs(r_ref, slice(WSHP, WROWS))
        d, mn, vn = _adamw(wo_ref[0], g, mo_ref[0], vo_ref[0])
        go[0], do[0], mo[0], vo[0] = g, d, mn, vn

    ispec = pl.BlockSpec((1, cb, WSH), lambda i: (0, i, 0))
    ospec = pl.BlockSpec((1, e_sh, cb), lambda i: (0, 0, i))
    return pl.pallas_call(
        body, name="adamw_big", grid=(D // cb,),
        in_specs=[pl.BlockSpec((NDEV, WROWS, cb), lambda i: (0, 0, i)), ispec, ispec, ispec, ospec, ospec, ospec],
        out_specs=[ispec] * 4 + [ospec] * 4, out_shape=[in_shape] * 4 + [out_shape] * 4,
        compiler_params=_params(),
    )(recv, w_in, m_in, v_in, w_out, m_out, v_out)


def _adamw_small(recv, w, m, v):
    shape = jax.ShapeDtypeStruct((SROWS, TB), F32)

    def body(r_ref, w_ref, m_ref, v_ref, g_out, d_out, m_out, v_out):
        g = _sum_slabs(r_ref, slice(0, SROWS))
        d, mn, vn = _adamw(w_ref[...], g, m_ref[...], v_ref[...])
        g_out[...], d_out[...], m_out[...], v_out[...] = g, d, mn, vn

    return pl.pallas_call(body, name="adamw_small", out_shape=[shape] * 4)(recv, w, m, v)


def _tile_rows(a, rows, lanes=TB):
    a = a.reshape(rows, lanes)
    return jnp.pad(a, ((0, -rows % 8), (0, TB - lanes)))


def _pack_small(norm_g, final_norm_g, attn_norm_g, conv_norm_g, b_f, meta_sh, conv_w_sh, loss=None):
    b_row = b_f.reshape(1, H) if loss is None else jnp.concatenate([b_f.reshape(1, H), loss.reshape(1, 1)], axis=1)
    packed = jnp.concatenate([
        _tile_rows(norm_g, 8), _tile_rows(final_norm_g, 8), _tile_rows(attn_norm_g, 4), _tile_rows(conv_norm_g, 4),
        _tile_rows(b_row, 1, b_row.shape[1]), _tile_rows(meta_sh, NM), _tile_rows(conv_w_sh, 3, DH)], axis=0)
    assert packed.shape == (SROWS, TB)
    return packed


def _unpack_small(p):
    return dict(
        norm_g=p[0:8].reshape(1, D), final_norm_g=p[8:16].reshape(D), attn_norm_g=p[16:20].reshape(1, DA),
        conv_norm_g=p[24:28].reshape(1, DA), b_f=p[32:33, :H].reshape(1, H), meta=p[40:56].reshape(NM, TB),
        conv_w=p[56:59, :DH].reshape(1, 3, DH))


def kernel(x, meta, norm_g, w_in, b_f, conv_w, attn_norm_g, conv_norm_g, w_out, final_norm_g, loss_target, m_meta, m_norm_g, m_w_in, m_b_f, m_conv_w, m_attn_norm_g, m_conv_norm_g, m_w_out, m_final_norm_g, v_meta, v_norm_g, v_w_in, v_b_f, v_conv_w, v_attn_norm_g, v_conv_norm_g, v_w_out, v_final_norm_g):
    seq = x.shape[1]
    L = seq + TB
    assert x.shape == (1, seq, D) and L % TT == 0 and w_in.shape == (1, D, WSH)
    x2 = x[0]
    tgt = loss_target[0]

    w_slab = jnp.concatenate(
        [jnp.pad(w_in[0].T, ((0, WSHP - WSH), (0, 0))), w_out[0]], axis=0).astype(BF16)
    small = jnp.concatenate([meta, _tile_rows(conv_w[0], 3, DH)], axis=0)
    w_all = _all_gather(w_slab, "gather_weights")
    small_all = _all_gather(small, "gather_small")

    w_ref_order = w_all[:, :WSH, :].reshape(NDEV * WSH, D)
    f0 = 3 * DA
    w_t = jnp.concatenate([w_ref_order[:f0], w_ref_order[f0 + H:],
                           jnp.pad(w_ref_order[f0:f0 + H], ((0, DF - H), (0, 0)))], axis=0)
    w_out_full = w_all[:, WSHP:, :].reshape(D, D)
    meta_full = jnp.transpose(small_all[:, :NM, :], (1, 0, 2)).reshape(NM, D)
    conv_w_full = jnp.transpose(small_all[:, NM:NM + 3, :DH], (1, 0, 2)).reshape(3, DA)

    lane_b = lambda p: jnp.broadcast_to(p.reshape(-1, DA, 1), (p.size // DA, DA, TB))
    cw_b = lane_b(conv_w_full)
    ga_b = lane_b(attn_norm_g)[0]
    gcn_b = lane_b(conv_norm_g)[0]

    u, proj_t, f_t, ktok, vtok = _inproj_fwd(x2, meta_full, norm_g, w_t, L)
    cq, ckb, sg = _fgate_fwd(f_t, b_f.reshape(H, 1), L)
    o_t, lse = _attn_fwd(proj_t, ktok, cq, ckb, L)
    mix_t = _gate_fwd(o_t, proj_t, cw_b, ga_b, gcn_b, L)

    dout, dmix_t, dw_out, loss_part, dg_final = _outproj(
        mix_t, w_out_full, x2, meta_full, final_norm_g.reshape(1, D), tgt, L)
    do_t, dd, dg5_t, dga_p, dgc_p, dcw_p = _gate_bwd(dmix_t, o_t, proj_t, cw_b, ga_b, gcn_b, L)
    dq_t, dk_t, dv_t, dck, dcq = _attn_bwd(proj_t, ktok, vtok, do_t, lse, dd, cq, ckb, L)
    df_t, db_f = _fgate_bwd(dcq, dck, sg, L)
    grad_x, dmeta, dg_norm = _inproj_bwd_x(w_t, dq_t, dk_t, dv_t, dg5_t, df_t, dout, x2, meta_full, norm_g, L)
    dw_main, dw_f = _inproj_bwd_w(u, dq_t, dk_t, dv_t, dg5_t, df_t, L)

    dw_ref_order = jnp.concatenate([dw_main[:f0], dw_f[:H], dw_main[f0:]], axis=0)
    dw_slabs = jnp.pad(dw_ref_order.reshape(NDEV, WSH, D), ((0, 0), (0, WSHP - WSH), (0, 0)))
    big_parts = jnp.concatenate([dw_slabs, dw_out.reshape(NDEV, D // NDEV, D)], axis=1).astype(BF16)
    dga = jnp.sum(dga_p, axis=1)
    dgc = jnp.sum(dgc_p, axis=1)
    dcw = jnp.sum(dcw_p, axis=2)
    small_parts = jnp.stack([
        _pack_small(dg_norm, dg_final, dga, dgc, db_f, dmeta[:, j * TB:(j + 1) * TB], dcw[:, j * DH:(j + 1) * DH],
                    loss=loss_part)
        for j in range(NDEV)], axis=0)
    big_recv = _exchange(big_parts, "exchange_weight_grads")
    small_recv = _exchange(small_parts, "exchange_small_grads")

    (g_w_in, d_w_in, nm_w_in, nv_w_in, g_w_out, d_w_out, nm_w_out, nv_w_out) = _adamw_big(
        big_recv, w_in, m_w_in, v_w_in, w_out, m_w_out, v_w_out)
    wp = _pack_small(norm_g, final_norm_g, attn_norm_g, conv_norm_g, b_f, meta, conv_w)
    mp = _pack_small(m_norm_g, m_final_norm_g, m_attn_norm_g, m_conv_norm_g, m_b_f, m_meta, m_conv_w)
    vp = _pack_small(v_norm_g, v_final_norm_g, v_attn_norm_g, v_conv_norm_g, v_b_f, v_meta, v_conv_w)
    small_out = _adamw_small(small_recv, wp, mp, vp)
    sm = [_unpack_small(p) for p in small_out]
    loss = small_out[0][32, H]
    order = ("meta", "norm_g", "w_in", "b_f", "conv_w", "attn_norm_g", "conv_norm_g", "w_out", "final_norm_g")
    groups = []
    for k, (big_in, big_out) in enumerate(((g_w_in, g_w_out), (d_w_in, d_w_out), (nm_w_in, nm_w_out),
                                           (nv_w_in, nv_w_out))):
        d = dict(sm[k], w_in=big_in, w_out=big_out)
        groups.append([d[n] for n in order])
    return (loss, grad_x[None], *groups[0], *groups[1], *groups[2], *groups[3])
```

```python
import functools

import jax
import jax.numpy as jnp
from jax import lax
from jax.experimental import pallas as pl
from jax.experimental.pallas import tpu as pltpu

F32 = jnp.float32
BF16 = jnp.bfloat16

D = 1024
DA = 512
H = 8
DH = 64
NM = 16
TB = 128
P0 = TB - NM
TT = 3 * TB
HG = 4
NDEV = 8
NSEC = 8
DF = 16
DPROJ = NSEC * DA + DF
WSH = 513
WSHP = 528
WROWS = WSHP + D // NDEV
SROWS = 64
EPS = 1e-6
NEG = -1e30
LOG2E = 1.4426950408889634
LN2 = 0.6931471805599453
QSCALE = DH ** -0.5 * LOG2E
KA = 128
VMEM_LIMIT = 56 * 1024 * 1024

ADAM_LR = 0.001
ADAM_B1 = 0.9
ADAM_B2 = 0.999
ADAM_EPS = 1e-08
ADAM_WD = 0.01
ADAM_STEP = 10

NT_DIMS = (((1,), (1,)), ((), ()))
TN_DIMS = (((0,), (0,)), ((), ()))
MESH = pl.DeviceIdType.MESH


def _params(n_axes=1, vmem=VMEM_LIMIT):
    return pltpu.CompilerParams(dimension_semantics=("arbitrary",) * n_axes, vmem_limit_bytes=vmem)


def _dot(a, b, dims=None):
    if dims is None:
        return jnp.dot(a, b, preferred_element_type=F32)
    return lax.dot_general(a, b, dims, preferred_element_type=F32)


def _my_place():
    return lax.axis_index("x"), lax.axis_index("y"), lax.axis_index("c")


def _all_gather(x, name):
    def body(x_ref, out_ref, send_sems, recv_sems, local_sem):
        mx, my, mc = _my_place()
        me, sibling = (mx, my, mc), (mx, my, 1 - mc)
        chips = [(1 - mx, my), (mx, 1 - my), (1 - mx, 1 - my)]

        def slot(px, py, pc):
            return out_ref.at[4 * px + 2 * py + pc]

        def copy(k, block, to, src=None):
            return pltpu.make_async_remote_copy(
                src_ref=slot(*block) if src is None else src, dst_ref=slot(*block),
                send_sem=send_sems.at[k], recv_sem=recv_sems.at[k], device_id=to, device_id_type=MESH)

        mine = pltpu.make_async_copy(x_ref, slot(*me), local_sem)
        mine.start()
        first = [copy(0, me, sibling, src=x_ref)]
        first += [copy(1 + j, me, (*chip, mc), src=x_ref) for j, chip in enumerate(chips)]
        for cp in first:
            cp.start()
        passed = [copy(4 + j, (*chip, mc), sibling) for j, chip in enumerate(chips)]
        for j, chip in enumerate(chips):
            copy(1 + j, (*chip, mc), me).wait_recv()
            passed[j].start()
        copy(0, sibling, me).wait_recv()
        for j, chip in enumerate(chips):
            copy(4 + j, (*chip, 1 - mc), me).wait_recv()
        for cp in first + passed:
            cp.wait_send()
        mine.wait()

    return pl.pallas_call(
        body, name=name,
        out_shape=jax.ShapeDtypeStruct((NDEV,) + x.shape, x.dtype),
        in_specs=[pl.BlockSpec(memory_space=pl.ANY)],
        out_specs=pl.BlockSpec(memory_space=pl.ANY),
        scratch_shapes=[pltpu.SemaphoreType.DMA((7,)), pltpu.SemaphoreType.DMA((7,)), pltpu.SemaphoreType.DMA],
    )(x)


def _exchange(parts, name):
    def body(p_ref, out_ref, send_sems, recv_sems, local_sem):
        mx, my, mc = _my_place()
        me = 4 * mx + 2 * my + mc
        mine = pltpu.make_async_copy(p_ref.at[me], out_ref.at[me], local_sem)
        mine.start()

        def peer_of(m):
            return ((1 - mx) if m & 4 else mx, (1 - my) if m & 2 else my, (1 - mc) if m & 1 else mc)

        def copy(m, src_slot, dst_slot):
            px, py, pc = peer_of(m)
            return pltpu.make_async_remote_copy(
                src_ref=p_ref.at[src_slot], dst_ref=out_ref.at[dst_slot],
                send_sem=send_sems.at[m - 1], recv_sem=recv_sems.at[m - 1],
                device_id=(px, py, pc), device_id_type=MESH)

        sends = []
        for m in range(1, NDEV):
            px, py, pc = peer_of(m)
            cp = copy(m, 4 * px + 2 * py + pc, me)
            cp.start()
            sends.append(cp)
        for m in range(1, NDEV):
            px, py, pc = peer_of(m)
            copy(m, me, 4 * px + 2 * py + pc).wait_recv()
        for cp in sends:
            cp.wait_send()
        mine.wait()

    return pl.pallas_call(
        body, name=name,
        out_shape=jax.ShapeDtypeStruct(parts.shape, parts.dtype),
        in_specs=[pl.BlockSpec(memory_space=pl.ANY)],
        out_specs=pl.BlockSpec(memory_space=pl.ANY),
        scratch_shapes=[pltpu.SemaphoreType.DMA((7,)), pltpu.SemaphoreType.DMA((7,)), pltpu.SemaphoreType.DMA],
    )(parts)


def _h_block(t, x_ref, meta_ref):
    first = jnp.concatenate([jnp.zeros((P0, D), F32), meta_ref[...]], axis=0)
    return jnp.where(t == 0, first, x_ref[...])


def _x_spec():
    return pl.BlockSpec((TB, D), lambda t: (jnp.maximum(t - 1, 0), 0))


def _x_specs3():
    return [pl.BlockSpec((TB, D), lambda j: (jnp.maximum(3 * j - 1, 0), 0)),
            pl.BlockSpec((TB, D), lambda j: (3 * j, 0)),
            pl.BlockSpec((TB, D), lambda j: (3 * j + 1, 0))]


def _h_tile(j, xa_ref, xb_ref, xc_ref, meta_ref):
    first = jnp.concatenate([jnp.zeros((P0, D), F32), meta_ref[...]], axis=0)
    return jnp.concatenate([jnp.where(j == 0, first, xa_ref[...]), xb_ref[...], xc_ref[...]], axis=0)


def _full_spec(shape):
    return pl.BlockSpec(shape, lambda *_: (0,) * len(shape))


def _sigmoid(z):
    return 1.0 / (1.0 + jnp.exp(-z))


def _grouped(x):
    return x.reshape(H, DH, x.shape[-1])


def _group_rstd(x3):
    return lax.rsqrt(jnp.mean(x3 * x3, axis=1, keepdims=True) + EPS)


def _lane_tiles_sum(x):
    out = x[:, :TB]
    for i in range(1, x.shape[1] // TB):
        out = out + x[:, i * TB:(i + 1) * TB]
    return out


def _inproj_fwd(x, meta_full, norm_g, w_t, L):
    nj = L // TT

    def body(xa_ref, xb_ref, xc_ref, meta_ref, g_ref, w_ref, u_ref, proj_ref, f_ref, ktok_ref, vtok_ref):
        hb = _h_tile(pl.program_id(0), xa_ref, xb_ref, xc_ref, meta_ref)
        r = lax.rsqrt(jnp.mean(hb * hb, axis=-1, keepdims=True) + EPS)
        u = (hb * r * g_ref[...]).astype(BF16)
        u_ref[...] = u
        for s in range(NSEC):
            p = _dot(w_ref[s * DA:(s + 1) * DA, :], u, NT_DIMS)
            if s == 0:
                p = p * QSCALE
            proj_ref[s * DA:(s + 1) * DA, :] = p.astype(BF16)
        f_ref[...] = _dot(w_ref[NSEC * DA:DPROJ, :], u, NT_DIMS)[:H]
        k_tm = _dot(u, w_ref[DA:2 * DA, :], NT_DIMS)
        v_tm = _dot(u, w_ref[2 * DA:3 * DA, :], NT_DIMS)
        for h in range(H):
            ktok_ref[h] = k_tm[:, h * DH:(h + 1) * DH].astype(BF16)
            vtok_ref[h] = v_tm[:, h * DH:(h + 1) * DH].astype(BF16)

    return pl.pallas_call(
        body, name="inproj_fwd", grid=(nj,),
        in_specs=_x_specs3() + [_full_spec((NM, D)), _full_spec((1, D)), _full_spec((DPROJ, D))],
        out_specs=[
            pl.BlockSpec((TT, D), lambda t: (t, 0)),
            pl.BlockSpec((NSEC * DA, TT), lambda t: (0, t)),
            pl.BlockSpec((H, TT), lambda t: (0, t)),
            pl.BlockSpec((H, TT, DH), lambda t: (0, t, 0)),
            pl.BlockSpec((H, TT, DH), lambda t: (0, t, 0)),
        ],
        out_shape=[
            jax.ShapeDtypeStruct((L, D), BF16),
            jax.ShapeDtypeStruct((NSEC * DA, L), BF16),
            jax.ShapeDtypeStruct((H, L), F32),
            jax.ShapeDtypeStruct((H, L, DH), BF16),
            jax.ShapeDtypeStruct((H, L, DH), BF16),
        ],
        compiler_params=_params(),
    )(x, x, x, meta_full, norm_g, w_t)


def _split3(x):
    hi = x.astype(BF16).astype(F32)
    r = x - hi
    mid = r.astype(BF16).astype(F32)
    return hi, mid, (r - mid).astype(BF16).astype(F32)


def _bias_rows(bias):
    one = jnp.ones((1, TT), F32)
    zero = jnp.zeros((1, TT), F32)
    parts = [zero] * 3 if bias is None else list(_split3(bias))
    return jnp.concatenate([one] * 3 + parts + [zero] * (DF - 6), axis=0).astype(BF16)


def _fgate_fwd(f_t, b_col, ktok, L):
    nb = L // TB

    def body(f_ref, b_ref, ktok_ref, cq_ref, kaug_ref, sg_ref):
        z = f_ref[...] + b_ref[...]
        idx = lax.broadcasted_iota(jnp.int32, (H, L), 1)
        real = idx >= P0
        lf = jnp.where(real, jnp.minimum(z, 0.0) - jnp.log1p(jnp.exp(-jnp.abs(z))), 0.0)
        sg_ref[...] = jnp.where(real, 1.0 / (1.0 + jnp.exp(z)), 0.0)
        c = lf
        s = 1
        while s < L:
            c = c + jnp.where(idx >= s, pltpu.roll(c, s, 1), 0.0)
            s *= 2
        c = c * LOG2E
        for h in range(H):
            cq_ref[h] = c[h:h + 1, :]
        ck = jnp.where(real, c, -NEG)
        lane = lax.broadcasted_iota(jnp.int32, (TB, KA), 1)
        tail = jnp.where((lane >= DH + 3) & (lane < DH + 6), 1.0, 0.0)
        for h in range(H):
            for b in range(nb):
                blk = slice(b * TB, (b + 1) * TB)
                col = jnp.broadcast_to(ck[h:h + 1, blk], (TB, TB)).T
                hi, mid, lo = _split3(-col)
                k = jnp.concatenate([ktok_ref[h, blk, :].astype(F32), jnp.zeros((TB, KA - DH), F32)], axis=1)
                out = jnp.where(lane < DH, k, jnp.where(lane == DH, hi, jnp.where(
                    lane == DH + 1, mid, jnp.where(lane == DH + 2, lo, tail))))
                kaug_ref[h, blk, :] = out.astype(BF16)

    return pl.pallas_call(
        body, name="fgate_fwd",
        out_shape=[
            jax.ShapeDtypeStruct((H, 1, L), F32),
            jax.ShapeDtypeStruct((H, L, KA), BF16),
            jax.ShapeDtypeStruct((H, L), F32),
        ],
        compiler_params=pltpu.CompilerParams(vmem_limit_bytes=VMEM_LIMIT),
    )(f_t, b_col, ktok)


def _causal_mask():
    r = lax.broadcasted_iota(jnp.int32, (TT, TT), 0)
    c = lax.broadcasted_iota(jnp.int32, (TT, TT), 1)
    return r <= c


def _attn_fwd(proj_t, kaug, cq, L):
    nq = L // TT

    def body(q_ref, kaug_ref, v_ref, cq_ref, o_ref, lse_ref, qa_scr, s_scr, cmax_scr, m_scr, acc_scr):
        j = pl.program_id(1)
        rows = [slice(g * DH, (g + 1) * DH) for g in range(HG)]
        ones = jnp.ones((DF, TT), BF16)
        for g in range(HG):
            qa_scr[g] = jnp.concatenate(
                [q_ref[rows[g], :], _bias_rows(None), jnp.zeros((KA - DH - DF, TT), BF16)], axis=0)

        def scores(kt, masked):
            k_off = pl.multiple_of(kt * TT, TT)
            for g in range(HG):
                s = _dot(kaug_ref[g, pl.ds(k_off, TT), :], qa_scr[g])
                if masked:
                    s = jnp.where(_causal_mask(), s, NEG)
                s_scr[g] = s
                cmax_scr[g] = jnp.max(s, axis=0, keepdims=True)

        def softmax_pv(kt):
            k_off = pl.multiple_of(kt * TT, TT)
            for g in range(HG):
                m_old = m_scr[g]
                m_new = jnp.maximum(m_old, cmax_scr[g])
                alpha = jnp.exp2(m_old - m_new)
                p = jnp.exp2(s_scr[g] - m_new).astype(BF16)
                v1 = jnp.concatenate([v_ref[rows[g], pl.ds(k_off, TT)], ones], axis=0)
                acc_scr[g] = alpha * acc_scr[g] + _dot(v1, p)
                m_scr[g] = m_new

        m_scr[...] = jnp.full_like(m_scr, NEG)
        acc_scr[...] = jnp.zeros_like(acc_scr)

        scores(j, True)

        def step(t, c):
            softmax_pv(j - t)
            scores(j - 1 - t, False)
            return c

        lax.fori_loop(0, j, step, 0)
        softmax_pv(0)
        for g in range(HG):
            l = acc_scr[g, DH:DH + 1, :]
            o_ref[rows[g], :] = acc_scr[g, :DH, :] * (1.0 / l)
            lse_ref[g] = m_scr[g] + jnp.log2(l) + cq_ref[g]

    return pl.pallas_call(
        body, name="attn_fwd", grid=(H // HG, nq),
        in_specs=[
            pl.BlockSpec((HG * DH, TT), lambda h, j: (h, j)),
            pl.BlockSpec((HG, L, KA), lambda h, j: (h, 0, 0)),
            pl.BlockSpec((HG * DH, L), lambda h, j: (2 * H // HG + h, 0)),
            pl.BlockSpec((HG, 1, TT), lambda h, j: (h, 0, j)),
        ],
        out_specs=[
            pl.BlockSpec((HG * DH, TT), lambda h, j: (h, j)),
            pl.BlockSpec((HG, 1, TT), lambda h, j: (h, 0, j)),
        ],
        out_shape=[jax.ShapeDtypeStruct((DA, L), F32), jax.ShapeDtypeStruct((H, 1, L), F32)],
        scratch_shapes=[pltpu.VMEM((HG, KA, TT), BF16), pltpu.VMEM((HG, TT, TT), F32), pltpu.VMEM((HG, 1, TT), F32),
                        pltpu.VMEM((HG, 1, TT), F32), pltpu.VMEM((HG, DH + DF, TT), F32)],
        compiler_params=_params(2),
    )(proj_t, kaug, proj_t, cq)


def _gate_common(o, za, gb, gc, xc, zc, gcp, xcp, cw_ref, ga_ref, gcn_ref, first):
    n_rep = TT // TB
    a = gc * xc
    a_prev = jnp.where(first, 0.0, gcp * xcp)
    full = jnp.concatenate([a_prev, a], axis=1)
    a1 = pltpu.roll(full, 1, 1)[:, TB:]
    a2 = pltpu.roll(full, 2, 1)[:, TB:]
    w0 = jnp.tile(cw_ref[0], (1, n_rep))
    w1 = jnp.tile(cw_ref[1], (1, n_rep))
    w2 = jnp.tile(cw_ref[2], (1, n_rep))
    cv = w0 * a2 + w1 * a1 + w2 * a
    e = gb * cv
    e3 = _grouped(e)
    rc = _group_rstd(e3)
    ec = (e3 * rc).reshape(DA, TT)
    o3 = _grouped(o)
    ra = _group_rstd(o3)
    oa = (o3 * ra).reshape(DA, TT)
    g_a = jnp.tile(ga_ref[...], (1, n_rep))
    g_c = jnp.tile(gcn_ref[...], (1, n_rep))
    sa = _sigmoid(za)
    sc = _sigmoid(zc)
    return dict(a=a, a1=a1, a2=a2, w0=w0, w1=w1, w2=w2, cv=cv, e=e, rc=rc, ec=ec, ra=ra, oa=oa,
                g_a=g_a, g_c=g_c, sa=sa, sc=sc)


def _gate_specs(nj, rev):
    def jj(i):
        return (nj - 1 - i) if rev else i

    def sec(s):
        return pl.BlockSpec((DA, TT), lambda i: (s, jj(i)))

    def halo(s):
        return pl.BlockSpec((DA, TB), lambda i: (s, jnp.maximum(3 * jj(i) - 1, 0)))

    return [pl.BlockSpec((DA, TT), lambda i: (0, jj(i))), sec(3), sec(4), sec(5), sec(6), sec(7), halo(5), halo(6),
            _full_spec((3, DA, TB)), _full_spec((DA, TB)), _full_spec((DA, TB))]


def _gate_fwd(o_t, proj_t, cw_b, ga_b, gcn_b, L):
    nj = L // TT

    def body(o_ref, za_ref, gb_ref, gc_ref, xc_ref, zc_ref, gcp_ref, xcp_ref, cw_ref, ga_ref, gcn_ref, mix_ref):
        j = pl.program_id(0)
        f32 = lambda r: r[...].astype(F32)
        za, zc = f32(za_ref), f32(zc_ref)
        g = _gate_common(o_ref[...], za, f32(gb_ref), f32(gc_ref), f32(xc_ref), zc, f32(gcp_ref), f32(xcp_ref),
                         cw_ref, ga_ref, gcn_ref, j == 0)
        mix_ref[:DA, :] = (g["oa"] * g["g_a"] * (za * g["sa"])).astype(BF16)
        mix_ref[DA:, :] = (g["ec"] * g["g_c"] * (zc * g["sc"])).astype(BF16)

    return pl.pallas_call(
        body, name="gate_fwd", grid=(nj,),
        in_specs=_gate_specs(nj, False),
        out_specs=pl.BlockSpec((2 * DA, TT), lambda j: (0, j)),
        out_shape=jax.ShapeDtypeStruct((2 * DA, L), BF16),
        compiler_params=_params(),
    )(o_t, proj_t, proj_t, proj_t, proj_t, proj_t, proj_t, proj_t, cw_b, ga_b, gcn_b)


def _outproj(mix_t, w_out, x, meta_full, fng, target, L):
    nj = L // TT

    def body(mix_ref, w_ref, xa_ref, xb_ref, xc_ref, meta_ref, g_ref, ta_ref, tb_ref, tc_ref,
             dout_ref, dmix_ref, dw_ref, loss_ref, dg_ref):
        t = pl.program_id(0)

        @pl.when(t == 0)
        def _():
            dw_ref[...] = jnp.zeros_like(dw_ref)
            loss_ref[...] = jnp.zeros_like(loss_ref)
            dg_ref[...] = jnp.zeros_like(dg_ref)

        mix = mix_ref[...]
        o = _dot(mix, w_ref[...], TN_DIMS) + _h_tile(t, xa_ref, xb_ref, xc_ref, meta_ref)
        r = lax.rsqrt(jnp.mean(o * o, axis=-1, keepdims=True) + EPS)
        g = g_ref[...]
        orn = o * r
        tgt = jnp.concatenate([ta_ref[...], tb_ref[...], tc_ref[...]], axis=0)
        row = lax.broadcasted_iota(jnp.int32, (TT, 1), 0)
        real = jnp.where((t > 0) | (row >= TB), 1.0, 0.0)
        diff = (orn * g - tgt) * real
        loss_ref[...] += 0.5 * jnp.sum(diff * diff) * (1.0 / D)
        dy = diff * (1.0 / D)
        dg_ref[...] += jnp.sum(dy * orn, axis=0, keepdims=True)
        gy = dy * g
        dout = r * gy - orn * (r * jnp.mean(gy * orn, axis=-1, keepdims=True))
        dout_ref[...] = dout
        db = dout.astype(BF16)
        dmix_ref[...] = _dot(w_ref[...], db, NT_DIMS).astype(BF16)
        dw_ref[...] += _dot(mix, db)

    return pl.pallas_call(
        body, name="outproj", grid=(nj,),
        in_specs=[pl.BlockSpec((D, TT), lambda t: (0, t)), _full_spec((D, D))] + _x_specs3()
                 + [_full_spec((NM, D)), _full_spec((1, D))] + _x_specs3(),
        out_specs=[pl.BlockSpec((TT, D), lambda t: (t, 0)), pl.BlockSpec((D, TT), lambda t: (0, t)),
                   _full_spec((D, D)), _full_spec((1, 1)), _full_spec((1, D))],
        out_shape=[jax.ShapeDtypeStruct((L, D), F32), jax.ShapeDtypeStruct((D, L), BF16),
                   jax.ShapeDtypeStruct((D, D), F32), jax.ShapeDtypeStruct((1, 1), F32),
                   jax.ShapeDtypeStruct((1, D), F32)],
        compiler_params=_params(),
    )(mix_t, w_out, x, x, x, meta_full, fng, target, target, target)


def _gate_bwd(dmix_t, o_t, proj_t, cw_b, ga_b, gcn_b, L):
    nj = L // TT

    def body(dmix_ref, o_ref, za_ref, gb_ref, gc_ref, xc_ref, zc_ref, gcp_ref, xcp_ref, cw_ref, ga_ref, gcn_ref,
             do_ref, dd_ref, dg5_ref, dga_ref, dgc_ref, dcw_ref, carry_ref):
        i = pl.program_id(0)
        j = nj - 1 - i

        @pl.when(i == 0)
        def _():
            carry_ref[...] = jnp.zeros_like(carry_ref)
            dga_ref[...] = jnp.zeros_like(dga_ref)
            dgc_ref[...] = jnp.zeros_like(dgc_ref)
            dcw_ref[...] = jnp.zeros_like(dcw_ref)

        f32 = lambda r: r[...].astype(F32)
        o, za, gb, gc, xc, zc = o_ref[...], f32(za_ref), f32(gb_ref), f32(gc_ref), f32(xc_ref), f32(zc_ref)
        g = _gate_common(o, za, gb, gc, xc, zc, f32(gcp_ref), f32(xcp_ref), cw_ref, ga_ref, gcn_ref, j == 0)
        dya = dmix_ref[:DA, :].astype(F32)
        dyc = dmix_ref[DA:, :].astype(F32)
        sa, sc = g["sa"], g["sc"]

        dn = dya * (za * sa)
        dg5_ref[0:DA, :] = (dya * (g["oa"] * g["g_a"]) * (sa * (1.0 + za * (1.0 - sa)))).astype(BF16)
        dga_ref[...] += _lane_tiles_sum(dn * g["oa"])
        dng = dn * g["g_a"]
        mean_a = jnp.mean(_grouped(dng * g["oa"]), axis=1, keepdims=True)
        do = ((_grouped(dng) - _grouped(g["oa"]) * mean_a) * g["ra"]).reshape(DA, TT)
        do_ref[...] = do.astype(BF16)
        dd = jnp.sum(_grouped(do * o), axis=1)
        for h in range(H):
            dd_ref[h] = dd[h:h + 1, :]

        dnc = dyc * (zc * sc)
        dg5_ref[4 * DA:5 * DA, :] = (dyc * (g["ec"] * g["g_c"]) * (sc * (1.0 + zc * (1.0 - sc)))).astype(BF16)
        dgc_ref[...] += _lane_tiles_sum(dnc * g["ec"])
        dncg = dnc * g["g_c"]
        mean_c = jnp.mean(_grouped(dncg * g["ec"]), axis=1, keepdims=True)
        de = ((_grouped(dncg) - _grouped(g["ec"]) * mean_c) * g["rc"]).reshape(DA, TT)
        dg5_ref[DA:2 * DA, :] = (de * g["cv"]).astype(BF16)
        dcv = de * gb
        full = jnp.concatenate([dcv, carry_ref[...]], axis=1)
        d1 = pltpu.roll(full, TT + TB - 1, 1)[:, :TT]
        d2 = pltpu.roll(full, TT + TB - 2, 1)[:, :TT]
        carry_ref[...] = dcv[:, :TB]
        da = g["w2"] * dcv + g["w1"] * d1 + g["w0"] * d2
        dg5_ref[2 * DA:3 * DA, :] = (da * xc).astype(BF16)
        dg5_ref[3 * DA:4 * DA, :] = (da * gc).astype(BF16)
        dcw_ref[0] += _lane_tiles_sum(dcv * g["a2"])
        dcw_ref[1] += _lane_tiles_sum(dcv * g["a1"])
        dcw_ref[2] += _lane_tiles_sum(dcv * g["a"])

    rj = lambda i: nj - 1 - i
    return pl.pallas_call(
        body, name="gate_bwd", grid=(nj,),
        in_specs=[pl.BlockSpec((2 * DA, TT), lambda i: (0, rj(i)))] + _gate_specs(nj, True),
        out_specs=[
            pl.BlockSpec((DA, TT), lambda i: (0, rj(i))),
            pl.BlockSpec((H, 1, TT), lambda i: (0, 0, rj(i))),
            pl.BlockSpec((5 * DA, TT), lambda i: (0, rj(i))),
            _full_spec((DA, TB)), _full_spec((DA, TB)), _full_spec((3, DA, TB)),
        ],
        out_shape=[
            jax.ShapeDtypeStruct((DA, L), BF16),
            jax.ShapeDtypeStruct((H, 1, L), F32),
            jax.ShapeDtypeStruct((5 * DA, L), BF16),
            jax.ShapeDtypeStruct((DA, TB), F32),
            jax.ShapeDtypeStruct((DA, TB), F32),
            jax.ShapeDtypeStruct((3, DA, TB), F32),
        ],
        scratch_shapes=[pltpu.VMEM((DA, TB), F32)],
        compiler_params=_params(),
    )(dmix_t, o_t, proj_t, proj_t, proj_t, proj_t, proj_t, proj_t, proj_t, cw_b, ga_b, gcn_b)


def _attn_bwd(proj_t, kaug, vtok, do_t, lse, dd, cq, L):
    nk = L // TT

    def body(q_ref, kaug_ref, vtok_ref, kt_ref, do_ref, lse_ref, dd_ref, cq_ref,
             dq_ref, dk_ref, dv_ref, dck_ref, dcq_ref, dq_acc, q1_scr, kt1_scr, s_scr, dp_scr, dv_scr, dk_scr):
        i = pl.program_id(1)

        @pl.when(i == 0)
        def _():
            dq_acc[...] = jnp.zeros_like(dq_acc)

        rows = [slice(g * DH, (g + 1) * DH) for g in range(HG)]
        ones = jnp.ones((DF, TT), BF16)
        zpad = jnp.zeros((KA - DH - DF, TT), BF16)
        for g in range(HG):
            kt1_scr[g] = jnp.concatenate([kt_ref[rows[g], :], ones], axis=0)
        dv_scr[...] = jnp.zeros_like(dv_scr)
        dk_scr[...] = jnp.zeros_like(dk_scr)

        def scores(jq, masked):
            q_off = pl.multiple_of(jq * TT, TT)
            for g in range(HG):
                bias = cq_ref[g, :, pl.ds(q_off, TT)] - lse_ref[g, :, pl.ds(q_off, TT)]
                q1 = jnp.concatenate([q_ref[rows[g], pl.ds(q_off, TT)], _bias_rows(bias)], axis=0)
                q1_scr[g] = q1
                s = _dot(kaug_ref[g], jnp.concatenate([q1, zpad], axis=0))
                if masked:
                    s = jnp.where(_causal_mask(), s, NEG)
                s_scr[g] = s
                dp_scr[g] = _dot(vtok_ref[g], do_ref[rows[g], pl.ds(q_off, TT)])

        def grads(jq):
            q_off = pl.multiple_of(jq * TT, TT)
            for g in range(HG):
                p = jnp.exp2(s_scr[g])
                ds = (p * (dp_scr[g] - dd_ref[g, :, pl.ds(q_off, TT)])).astype(BF16)
                dv_scr[g] += _dot(do_ref[rows[g], pl.ds(q_off, TT)], p.astype(BF16), NT_DIMS)
                dk_scr[g] += _dot(q1_scr[g], ds, NT_DIMS)
                dq_acc[g, :, pl.ds(q_off, TT)] += _dot(kt1_scr[g], ds)

        scores(i, True)

        def step(jq, c):
            grads(jq)
            scores(jq + 1, False)
            return c

        lax.fori_loop(i, nk - 1, step, 0)
        grads(nk - 1)
        for g in range(HG):
            dv_ref[rows[g], :] = dv_scr[g].astype(BF16)
            dk_ref[rows[g], :] = (dk_scr[g, :DH, :] * LN2).astype(BF16)
            dck_ref[g] = dk_scr[g, DH:DH + 1, :]

        @pl.when(i == nk - 1)
        def _():
            for g in range(HG):
                dq_ref[rows[g], :] = (dq_acc[g, :DH, :] * (DH ** -0.5)).astype(BF16)
                dcq_ref[g] = dq_acc[g, DH:DH + 1, :]

    head = lambda h, i: (h, 0)
    row = lambda h, i: (h, 0, 0)
    return pl.pallas_call(
        body, name="attn_bwd", grid=(H // HG, nk),
        in_specs=[
            pl.BlockSpec((HG * DH, L), head),
            pl.BlockSpec((HG, TT, KA), lambda h, i: (h, i, 0)),
            pl.BlockSpec((HG, TT, DH), lambda h, i: (h, i, 0)),
            pl.BlockSpec((HG * DH, TT), lambda h, i: (H // HG + h, i)),
            pl.BlockSpec((HG * DH, L), head),
            pl.BlockSpec((HG, 1, L), row), pl.BlockSpec((HG, 1, L), row), pl.BlockSpec((HG, 1, L), row),
        ],
        out_specs=[
            pl.BlockSpec((HG * DH, L), head),
            pl.BlockSpec((HG * DH, TT), lambda h, i: (h, i)),
            pl.BlockSpec((HG * DH, TT), lambda h, i: (h, i)),
            pl.BlockSpec((HG, 1, TT), lambda h, i: (h, 0, i)),
            pl.BlockSpec((HG, 1, L), row),
        ],
        out_shape=[jax.ShapeDtypeStruct((DA, L), BF16), jax.ShapeDtypeStruct((DA, L), BF16),
                   jax.ShapeDtypeStruct((DA, L), BF16), jax.ShapeDtypeStruct((H, 1, L), F32),
                   jax.ShapeDtypeStruct((H, 1, L), F32)],
        scratch_shapes=[
            pltpu.VMEM((HG, DH + DF, L), F32),
            pltpu.VMEM((HG, DH + DF, TT), BF16), pltpu.VMEM((HG, DH + DF, TT), BF16),
            pltpu.VMEM((HG, TT, TT), F32), pltpu.VMEM((HG, TT, TT), F32),
            pltpu.VMEM((HG, DH, TT), F32), pltpu.VMEM((HG, DH + DF, TT), F32)],
        compiler_params=_params(2),
    )(proj_t, kaug, vtok, proj_t, do_t, lse, dd, cq)


def _fgate_bwd(dcq, dck, sg, L):
    def body(dcq_ref, dck_ref, sg_ref, df_ref, db_ref):
        dc = jnp.concatenate([dcq_ref[h] - dck_ref[h] for h in range(H)], axis=0)
        idx = lax.broadcasted_iota(jnp.int32, (H, L), 1)
        r = dc
        s = 1
        while s < L:
            r = r + jnp.where(idx + s < L, pltpu.roll(r, L - s, 1), 0.0)
            s *= 2
        df = r * sg_ref[...]
        db_ref[...] = jnp.sum(df, axis=1, keepdims=True)
        df_ref[...] = jnp.concatenate([df, jnp.zeros((DF - H, L), F32)], axis=0).astype(BF16)

    return pl.pallas_call(
        body, name="fgate_bwd",
        out_shape=[jax.ShapeDtypeStruct((DF, L), BF16), jax.ShapeDtypeStruct((H, 1), F32)],
        compiler_params=pltpu.CompilerParams(vmem_limit_bytes=VMEM_LIMIT),
    )(dcq, dck, sg)


def _inproj_bwd_x(w, dq_t, dk_t, dv_t, dg5_t, df_t, dout, x, meta_full, norm_g, L):
    nb = L // TB
    seq = x.shape[0]

    def body(w_ref, dq_ref, dk_ref, dv_ref, dg5_ref, df_ref, dout_ref, x_ref, meta_ref, g_ref,
             gx_ref, dmeta_ref, dg_ref):
        t = pl.program_id(0)

        @pl.when(t == 0)
        def _():
            dg_ref[...] = jnp.zeros_like(dg_ref)

        du = _dot(dq_ref[...], w_ref[0:DA, :], TN_DIMS)
        du += _dot(dk_ref[...], w_ref[DA:2 * DA, :], TN_DIMS)
        du += _dot(dv_ref[...], w_ref[2 * DA:3 * DA, :], TN_DIMS)
        du += _dot(dg5_ref[...], w_ref[3 * DA:NSEC * DA, :], TN_DIMS)
        du += _dot(df_ref[...], w_ref[NSEC * DA:DPROJ, :], TN_DIMS)
        hb = _h_block(t, x_ref, meta_ref)
        r = lax.rsqrt(jnp.mean(hb * hb, axis=-1, keepdims=True) + EPS)
        hn = hb * r
        dg_ref[...] += jnp.sum(du * hn, axis=0, keepdims=True)
        gu = du * g_ref[...]
        dh = dout_ref[...] + r * gu - hn * (r * jnp.mean(gu * hn, axis=-1, keepdims=True))
        gx_ref[...] = dh

        @pl.when(t == 0)
        def _():
            dmeta_ref[...] = dh[P0:, :]

    blk = lambda rows: pl.BlockSpec((rows, TB), lambda t: (0, t))
    return pl.pallas_call(
        body, name="inproj_bwd_x", grid=(nb,),
        in_specs=[_full_spec((DPROJ, D)), blk(DA), blk(DA), blk(DA), blk(5 * DA), blk(DF),
                  pl.BlockSpec((TB, D), lambda t: (t, 0)), _x_spec(), _full_spec((NM, D)), _full_spec((1, D))],
        out_specs=[_x_spec(), _full_spec((NM, D)), _full_spec((1, D))],
        out_shape=[jax.ShapeDtypeStruct((seq, D), F32), jax.ShapeDtypeStruct((NM, D), F32),
                   jax.ShapeDtypeStruct((1, D), F32)],
        compiler_params=_params(),
    )(w, dq_t, dk_t, dv_t, dg5_t, df_t, dout, x, meta_full, norm_g)


def _inproj_bwd_w(u, dq_t, dk_t, dv_t, dg5_t, df_t, L):
    kt = L // 3
    nkt = 3

    def body(u_ref, dq_ref, dk_ref, dv_ref, dg5_ref, df_ref, dw_ref, dwf_ref):
        s = pl.program_id(0)
        k = pl.program_id(1)

        @pl.when(k == 0)
        def _():
            dw_ref[...] = jnp.zeros_like(dw_ref)

        @pl.when((s == 0) & (k == 0))
        def _():
            dwf_ref[...] = jnp.zeros_like(dwf_ref)

        u_blk = u_ref[...]
        for sec, ref in ((0, dq_ref), (1, dk_ref), (2, dv_ref)):
            @pl.when(s == sec)
            def _(ref=ref):
                dw_ref[...] += _dot(ref[...], u_blk)

        @pl.when(s >= 3)
        def _():
            dw_ref[...] += _dot(dg5_ref[...], u_blk)

        @pl.when(s == NSEC - 1)
        def _():
            dwf_ref[...] += _dot(df_ref[...], u_blk)

    def only(sec):
        return lambda s, k: (0, jnp.where(s == sec, k, 0))

    return pl.pallas_call(
        body, name="inproj_bwd_w", grid=(NSEC, nkt),
        in_specs=[
            pl.BlockSpec((kt, D), lambda s, k: (k, 0)),
            pl.BlockSpec((DA, kt), only(0)), pl.BlockSpec((DA, kt), only(1)), pl.BlockSpec((DA, kt), only(2)),
            pl.BlockSpec((DA, kt), lambda s, k: (jnp.maximum(s - 3, 0), jnp.where(s >= 3, k, 0))),
            pl.BlockSpec((DF, kt), only(NSEC - 1)),
        ],
        out_specs=[pl.BlockSpec((DA, D), lambda s, k: (s, 0)), _full_spec((DF, D))],
        out_shape=[jax.ShapeDtypeStruct((NSEC * DA, D), F32), jax.ShapeDtypeStruct((DF, D), F32)],
        compiler_params=_params(2),
    )(u, dq_t, dk_t, dv_t, dg5_t, df_t)


def _adamw(w, g, m, v):
    m = ADAM_B1 * m + (1.0 - ADAM_B1) * g
    v = ADAM_B2 * v + (1.0 - ADAM_B2) * (g * g)
    m_hat = m / (1.0 - ADAM_B1 ** ADAM_STEP)
    v_hat = v / (1.0 - ADAM_B2 ** ADAM_STEP)
    delta = -ADAM_LR * (m_hat / (jnp.sqrt(v_hat) + ADAM_EPS) + ADAM_WD * w)
    return delta, m, v


def _sum_slabs(ref, rows):
    g = ref[0, rows, :].astype(F32)
    for j in range(1, NDEV):
        g = g + ref[j, rows, :].astype(F32)
    return g


def _adamw_big(recv, w_in, m_in, v_in, w_out, m_out, v_out):
    cb = 256
    e_sh = D // NDEV
    in_shape = jax.ShapeDtypeStruct(w_in.shape, F32)
    out_shape = jax.ShapeDtypeStruct(w_out.shape, F32)

    def body(r_ref, wi_ref, mi_ref, vi_ref, wo_ref, mo_ref, vo_ref, gi, di, mi, vi, go, do, mo, vo):
        g = _sum_slabs(r_ref, slice(0, WSHP)).T[:, :WSH]
        d, mn, vn = _adamw(wi_ref[0], g, mi_ref[0], vi_ref[0])
        gi[0], di[0], mi[0], vi[0] = g, d, mn, vn
        g = _sum_slabs(r_ref, slice(WSHP, WROWS))
        d, mn, vn = _adamw(wo_ref[0], g, mo_ref[0], vo_ref[0])
        go[0], do[0], mo[0], vo[0] = g, d, mn, vn

    ispec = pl.BlockSpec((1, cb, WSH), lambda i: (0, i, 0))
    ospec = pl.BlockSpec((1, e_sh, cb), lambda i: (0, 0, i))
    return pl.pallas_call(
        body, name="adamw_big", grid=(D // cb,),
        in_specs=[pl.BlockSpec((NDEV, WROWS, cb), lambda i: (0, 0, i)), ispec, ispec, ispec, ospec, ospec, ospec],
        out_specs=[ispec] * 4 + [ospec] * 4, out_shape=[in_shape] * 4 + [out_shape] * 4,
        compiler_params=_params(),
    )(recv, w_in, m_in, v_in, w_out, m_out, v_out)


def _adamw_small(recv, w, m, v):
    shape = jax.ShapeDtypeStruct((SROWS, TB), F32)

    def body(r_ref, w_ref, m_ref, v_ref, g_out, d_out, m_out, v_out):
        g = _sum_slabs(r_ref, slice(0, SROWS))
        d, mn, vn = _adamw(w_ref[...], g, m_ref[...], v_ref[...])
        g_out[...], d_out[...], m_out[...], v_out[...] = g, d, mn, vn

    return pl.pallas_call(body, name="adamw_small", out_shape=[shape] * 4)(recv, w, m, v)


def _tile_rows(a, rows, lanes=TB):
    a = a.reshape(rows, lanes)
    return jnp.pad(a, ((0, -rows % 8), (0, TB - lanes)))


def _pack_small(norm_g, final_norm_g, attn_norm_g, conv_norm_g, b_f, meta_sh, conv_w_sh, loss=None):
    b_row = b_f.reshape(1, H) if loss is None else jnp.concatenate([b_f.reshape(1, H), loss.reshape(1, 1)], axis=1)
    packed = jnp.concatenate([
        _tile_rows(norm_g, 8), _tile_rows(final_norm_g, 8), _tile_rows(attn_norm_g, 4), _tile_rows(conv_norm_g, 4),
        _tile_rows(b_row, 1, b_row.shape[1]), _tile_rows(meta_sh, NM), _tile_rows(conv_w_sh, 3, DH)], axis=0)
    assert packed.shape == (SROWS, TB)
    return packed


def _unpack_small(p):
    return dict(
        norm_g=p[0:8].reshape(1, D), final_norm_g=p[8:16].reshape(D), attn_norm_g=p[16:20].reshape(1, DA),
        conv_norm_g=p[24:28].reshape(1, DA), b_f=p[32:33, :H].reshape(1, H), meta=p[40:56].reshape(NM, TB),
        conv_w=p[56:59, :DH].reshape(1, 3, DH))


def kernel(x, meta, norm_g, w_in, b_f, conv_w, attn_norm_g, conv_norm_g, w_out, final_norm_g, loss_target, m_meta, m_norm_g, m_w_in, m_b_f, m_conv_w, m_attn_norm_g, m_conv_norm_g, m_w_out, m_final_norm_g, v_meta, v_norm_g, v_w_in, v_b_f, v_conv_w, v_attn_norm_g, v_conv_norm_g, v_w_out, v_final_norm_g):
    seq = x.shape[1]
    L = seq + TB
    assert x.shape == (1, seq, D) and L % TT == 0 and w_in.shape == (1, D, WSH)
    x2 = x[0]
    tgt = loss_target[0]

    w_slab = jnp.concatenate(
        [jnp.pad(w_in[0].T, ((0, WSHP - WSH), (0, 0))), w_out[0]], axis=0).astype(BF16)
    small = jnp.concatenate([meta, _tile_rows(conv_w[0], 3, DH)], axis=0)
    w_all = _all_gather(w_slab, "gather_weights")
    small_all = _all_gather(small, "gather_small")

    w_ref_order = w_all[:, :WSH, :].reshape(NDEV * WSH, D)
    f0 = 3 * DA
    w_t = jnp.concatenate([w_ref_order[:f0], w_ref_order[f0 + H:],
                           jnp.pad(w_ref_order[f0:f0 + H], ((0, DF - H), (0, 0)))], axis=0)
    w_out_full = w_all[:, WSHP:, :].reshape(D, D)
    meta_full = jnp.transpose(small_all[:, :NM, :], (1, 0, 2)).reshape(NM, D)
    conv_w_full = jnp.transpose(small_all[:, NM:NM + 3, :DH], (1, 0, 2)).reshape(3, DA)

    lane_b = lambda p: jnp.broadcast_to(p.reshape(-1, DA, 1), (p.size // DA, DA, TB))
    cw_b = lane_b(conv_w_full)
    ga_b = lane_b(attn_norm_g)[0]
    gcn_b = lane_b(conv_norm_g)[0]

    u, proj_t, f_t, ktok, vtok = _inproj_fwd(x2, meta_full, norm_g, w_t, L)
    cq, kaug, sg = _fgate_fwd(f_t, b_f.reshape(H, 1), ktok, L)
    o_t, lse = _attn_fwd(proj_t, kaug, cq, L)
    mix_t = _gate_fwd(o_t, proj_t, cw_b, ga_b, gcn_b, L)

    dout, dmix_t, dw_out, loss_part, dg_final = _outproj(
        mix_t, w_out_full, x2, meta_full, final_norm_g.reshape(1, D), tgt, L)
    do_t, dd, dg5_t, dga_p, dgc_p, dcw_p = _gate_bwd(dmix_t, o_t, proj_t, cw_b, ga_b, gcn_b, L)
    dq_t, dk_t, dv_t, dck, dcq = _attn_bwd(proj_t, kaug, vtok, do_t, lse, dd, cq, L)
    df_t, db_f = _fgate_bwd(dcq, dck, sg, L)
    grad_x, dmeta, dg_norm = _inproj_bwd_x(w_t, dq_t, dk_t, dv_t, dg5_t, df_t, dout, x2, meta_full, norm_g, L)
    dw_main, dw_f = _inproj_bwd_w(u, dq_t, dk_t, dv_t, dg5_t, df_t, L)

    dw_ref_order = jnp.concatenate([dw_main[:f0], dw_f[:H], dw_main[f0:]], axis=0)
    dw_slabs = jnp.pad(dw_ref_order.reshape(NDEV, WSH, D), ((0, 0), (0, WSHP - WSH), (0, 0)))
    big_parts = jnp.concatenate([dw_slabs, dw_out.reshape(NDEV, D // NDEV, D)], axis=1).astype(BF16)
    dga = jnp.sum(dga_p, axis=1)
    dgc = jnp.sum(dgc_p, axis=1)
    dcw = jnp.sum(dcw_p, axis=2)
    small_parts = jnp.stack([
        _pack_small(dg_norm, dg_final, dga, dgc, db_f, dmeta[:, j * TB:(j + 1) * TB], dcw[:, j * DH:(j + 1) * DH],
                    loss=loss_part)
        for j in range(NDEV)], axis=0)
    big_recv = _exchange(big_parts, "exchange_weight_grads")
    small_recv = _exchange(small_parts, "exchange_small_grads")

    (g_w_in, d_w_in, nm_w_in, nv_w_in, g_w_out, d_w_out, nm_w_out, nv_w_out) = _adamw_big(
        big_recv, w_in, m_w_in, v_w_in, w_out, m_w_out, v_w_out)
    wp = _pack_small(norm_g, final_norm_g, attn_norm_g, conv_norm_g, b_f, meta, conv_w)
    mp = _pack_small(m_norm_g, m_final_norm_g, m_attn_norm_g, m_conv_norm_g, m_b_f, m_meta, m_conv_w)
    vp = _pack_small(v_norm_g, v_final_norm_g, v_attn_norm_g, v_conv_norm_g, v_b_f, v_meta, v_conv_w)
    small_out = _adamw_small(small_recv, wp, mp, vp)
    sm = [_unpack_small(p) for p in small_out]
    loss = small_out[0][32, H]
    order = ("meta", "norm_g", "w_in", "b_f", "conv_w", "attn_norm_g", "conv_norm_g", "w_out", "final_norm_g")
    groups = []
    for k, (big_in, big_out) in enumerate(((g_w_in, g_w_out), (d_w_in, d_w_out), (nm_w_in, nm_w_out),
                                           (nv_w_in, nv_w_out))):
        d = dict(sm[k], w_in=big_in, w_out=big_out)
        groups.append([d[n] for n in order])
    return (loss, grad_x[None], *groups[0], *groups[1], *groups[2], *groups[3])
```

```python
import functools

import jax
import jax.numpy as jnp
from jax import lax
from jax.experimental import pallas as pl
from jax.experimental.pallas import tpu as pltpu

F32 = jnp.float32
BF16 = jnp.bfloat16

D = 1024
DA = 512
H = 8
DH = 64
NM = 16
TB = 128
P0 = TB - NM
TT = 3 * TB
HG = 4
NDEV = 8
NSEC = 8
DF = 16
DPROJ = NSEC * DA + DF
WSH = 513
WSHP = 528
WROWS = WSHP + D // NDEV
SROWS = 64
EPS = 1e-6
NEG = -1e30
LOG2E = 1.4426950408889634
LN2 = 0.6931471805599453
QSCALE = DH ** -0.5 * LOG2E
KA = 128
CB = 256
VMEM_LIMIT = 56 * 1024 * 1024

ADAM_LR = 0.001
ADAM_B1 = 0.9
ADAM_B2 = 0.999
ADAM_EPS = 1e-08
ADAM_WD = 0.01
ADAM_STEP = 10

NT_DIMS = (((1,), (1,)), ((), ()))
TN_DIMS = (((0,), (0,)), ((), ()))
MESH = pl.DeviceIdType.MESH


def _params(n_axes=1, vmem=VMEM_LIMIT):
    return pltpu.CompilerParams(dimension_semantics=("arbitrary",) * n_axes, vmem_limit_bytes=vmem)


def _dot(a, b, dims=None):
    if dims is None:
        return jnp.dot(a, b, preferred_element_type=F32)
    return lax.dot_general(a, b, dims, preferred_element_type=F32)


def _my_place():
    return lax.axis_index("x"), lax.axis_index("y"), lax.axis_index("c")


def _all_gather(x, name):
    def body(x_ref, out_ref, send_sems, recv_sems, local_sem):
        mx, my, mc = _my_place()
        me, sibling = (mx, my, mc), (mx, my, 1 - mc)
        chips = [(1 - mx, my), (mx, 1 - my), (1 - mx, 1 - my)]

        def slot(px, py, pc):
            return out_ref.at[4 * px + 2 * py + pc]

        def copy(k, block, to, src=None):
            return pltpu.make_async_remote_copy(
                src_ref=slot(*block) if src is None else src, dst_ref=slot(*block),
                send_sem=send_sems.at[k], recv_sem=recv_sems.at[k], device_id=to, device_id_type=MESH)

        mine = pltpu.make_async_copy(x_ref, slot(*me), local_sem)
        mine.start()
        first = [copy(0, me, sibling, src=x_ref)]
        first += [copy(1 + j, me, (*chip, mc), src=x_ref) for j, chip in enumerate(chips)]
        for cp in first:
            cp.start()
        passed = [copy(4 + j, (*chip, mc), sibling) for j, chip in enumerate(chips)]
        for j, chip in enumerate(chips):
            copy(1 + j, (*chip, mc), me).wait_recv()
            passed[j].start()
        copy(0, sibling, me).wait_recv()
        for j, chip in enumerate(chips):
            copy(4 + j, (*chip, 1 - mc), me).wait_recv()
        for cp in first + passed:
            cp.wait_send()
        mine.wait()

    return pl.pallas_call(
        body, name=name,
        out_shape=jax.ShapeDtypeStruct((NDEV,) + x.shape, x.dtype),
        in_specs=[pl.BlockSpec(memory_space=pl.ANY)],
        out_specs=pl.BlockSpec(memory_space=pl.ANY),
        scratch_shapes=[pltpu.SemaphoreType.DMA((7,)), pltpu.SemaphoreType.DMA((7,)), pltpu.SemaphoreType.DMA],
    )(x)


def _exchange(parts, name):
    def body(p_ref, out_ref, send_sems, recv_sems, local_sem):
        mx, my, mc = _my_place()
        me = 4 * mx + 2 * my + mc
        mine = pltpu.make_async_copy(p_ref.at[me], out_ref.at[me], local_sem)
        mine.start()

        def peer_of(m):
            return ((1 - mx) if m & 4 else mx, (1 - my) if m & 2 else my, (1 - mc) if m & 1 else mc)

        def copy(m, src_slot, dst_slot):
            px, py, pc = peer_of(m)
            return pltpu.make_async_remote_copy(
                src_ref=p_ref.at[src_slot], dst_ref=out_ref.at[dst_slot],
                send_sem=send_sems.at[m - 1], recv_sem=recv_sems.at[m - 1],
                device_id=(px, py, pc), device_id_type=MESH)

        sends = []
        for m in range(1, NDEV):
            px, py, pc = peer_of(m)
            cp = copy(m, 4 * px + 2 * py + pc, me)
            cp.start()
            sends.append(cp)
        for m in range(1, NDEV):
            px, py, pc = peer_of(m)
            copy(m, me, 4 * px + 2 * py + pc).wait_recv()
        for cp in sends:
            cp.wait_send()
        mine.wait()

    return pl.pallas_call(
        body, name=name,
        out_shape=jax.ShapeDtypeStruct(parts.shape, parts.dtype),
        in_specs=[pl.BlockSpec(memory_space=pl.ANY)],
        out_specs=pl.BlockSpec(memory_space=pl.ANY),
        scratch_shapes=[pltpu.SemaphoreType.DMA((7,)), pltpu.SemaphoreType.DMA((7,)), pltpu.SemaphoreType.DMA],
    )(parts)


def _h_block(t, x_ref, meta_ref):
    first = jnp.concatenate([jnp.zeros((P0, D), F32), meta_ref[...]], axis=0)
    return jnp.where(t == 0, first, x_ref[...])


def _x_spec():
    return pl.BlockSpec((TB, D), lambda t: (jnp.maximum(t - 1, 0), 0))


def _x_specs3():
    return [pl.BlockSpec((TB, D), lambda j: (jnp.maximum(3 * j - 1, 0), 0)),
            pl.BlockSpec((TB, D), lambda j: (3 * j, 0)),
            pl.BlockSpec((TB, D), lambda j: (3 * j + 1, 0))]


def _h_tile(j, xa_ref, xb_ref, xc_ref, meta_ref):
    first = jnp.concatenate([jnp.zeros((P0, D), F32), meta_ref[...]], axis=0)
    return jnp.concatenate([jnp.where(j == 0, first, xa_ref[...]), xb_ref[...], xc_ref[...]], axis=0)


def _full_spec(shape):
    return pl.BlockSpec(shape, lambda *_: (0,) * len(shape))


def _sigmoid(z):
    return 1.0 / (1.0 + jnp.exp(-z))


def _grouped(x):
    return x.reshape(H, DH, x.shape[-1])


def _group_rstd(x3):
    return lax.rsqrt(jnp.mean(x3 * x3, axis=1, keepdims=True) + EPS)


def _lane_tiles_sum(x):
    out = x[:, :TB]
    for i in range(1, x.shape[1] // TB):
        out = out + x[:, i * TB:(i + 1) * TB]
    return out


def _inproj_fwd(x, meta_full, norm_g, w_t, L):
    nj = L // TT

    def body(xa_ref, xb_ref, xc_ref, meta_ref, g_ref, w_ref, u_ref, proj_ref, f_ref, ktok_ref, vtok_ref):
        hb = _h_tile(pl.program_id(0), xa_ref, xb_ref, xc_ref, meta_ref)
        r = lax.rsqrt(jnp.mean(hb * hb, axis=-1, keepdims=True) + EPS)
        u = (hb * r * g_ref[...]).astype(BF16)
        u_ref[...] = u
        for s in range(NSEC):
            p = _dot(w_ref[s * DA:(s + 1) * DA, :], u, NT_DIMS)
            if s == 0:
                p = p * QSCALE
            proj_ref[s * DA:(s + 1) * DA, :] = p.astype(BF16)
        f_ref[...] = _dot(w_ref[NSEC * DA:DPROJ, :], u, NT_DIMS)[:H]
        k_tm = _dot(u, w_ref[DA:2 * DA, :], NT_DIMS)
        v_tm = _dot(u, w_ref[2 * DA:3 * DA, :], NT_DIMS)
        for h in range(H):
            ktok_ref[h] = k_tm[:, h * DH:(h + 1) * DH].astype(BF16)
            vtok_ref[h] = v_tm[:, h * DH:(h + 1) * DH].astype(BF16)

    return pl.pallas_call(
        body, name="inproj_fwd", grid=(nj,),
        in_specs=_x_specs3() + [_full_spec((NM, D)), _full_spec((1, D)), _full_spec((DPROJ, D))],
        out_specs=[
            pl.BlockSpec((TT, D), lambda t: (t, 0)),
            pl.BlockSpec((NSEC * DA, TT), lambda t: (0, t)),
            pl.BlockSpec((H, TT), lambda t: (0, t)),
            pl.BlockSpec((H, TT, DH), lambda t: (0, t, 0)),
            pl.BlockSpec((H, TT, DH), lambda t: (0, t, 0)),
        ],
        out_shape=[
            jax.ShapeDtypeStruct((L, D), BF16),
            jax.ShapeDtypeStruct((NSEC * DA, L), BF16),
            jax.ShapeDtypeStruct((H, L), F32),
            jax.ShapeDtypeStruct((H, L, DH), BF16),
            jax.ShapeDtypeStruct((H, L, DH), BF16),
        ],
        compiler_params=_params(),
    )(x, x, x, meta_full, norm_g, w_t)


def _split3(x):
    hi = x.astype(BF16).astype(F32)
    r = x - hi
    mid = r.astype(BF16).astype(F32)
    return hi, mid, (r - mid).astype(BF16).astype(F32)


def _bias_rows(bias):
    one = jnp.ones((1, TT), F32)
    zero = jnp.zeros((1, TT), F32)
    parts = [zero] * 3 if bias is None else list(_split3(bias))
    return jnp.concatenate([one] * 3 + parts + [zero] * (DF - 6), axis=0).astype(BF16)


def _fgate_fwd(f_t, b_col, ktok, L):
    nb = L // TB

    def body(f_ref, b_ref, ktok_ref, cq_ref, kaug_ref, sg_ref):
        z = f_ref[...] + b_ref[...]
        idx = lax.broadcasted_iota(jnp.int32, (H, L), 1)
        real = idx >= P0
        lf = jnp.where(real, jnp.minimum(z, 0.0) - jnp.log1p(jnp.exp(-jnp.abs(z))), 0.0)
        sg_ref[...] = jnp.where(real, 1.0 / (1.0 + jnp.exp(z)), 0.0)
        c = lf
        s = 1
        while s < L:
            c = c + jnp.where(idx >= s, pltpu.roll(c, s, 1), 0.0)
            s *= 2
        c = c * LOG2E
        for h in range(H):
            cq_ref[h] = c[h:h + 1, :]
        ck = jnp.where(real, c, -NEG)
        lane = lax.broadcasted_iota(jnp.int32, (TB, KA), 1)
        tail = jnp.where((lane >= DH + 3) & (lane < DH + 6), 1.0, 0.0)
        for h in range(H):
            for b in range(nb):
                blk = slice(b * TB, (b + 1) * TB)
                col = jnp.broadcast_to(ck[h:h + 1, blk], (TB, TB)).T
                hi, mid, lo = _split3(-col)
                k = jnp.concatenate([ktok_ref[h, blk, :].astype(F32), jnp.zeros((TB, KA - DH), F32)], axis=1)
                out = jnp.where(lane < DH, k, jnp.where(lane == DH, hi, jnp.where(
                    lane == DH + 1, mid, jnp.where(lane == DH + 2, lo, tail))))
                kaug_ref[h, blk, :] = out.astype(BF16)

    return pl.pallas_call(
        body, name="fgate_fwd",
        out_shape=[
            jax.ShapeDtypeStruct((H, 1, L), F32),
            jax.ShapeDtypeStruct((H, L, KA), BF16),
            jax.ShapeDtypeStruct((H, L), F32),
        ],
        compiler_params=pltpu.CompilerParams(vmem_limit_bytes=VMEM_LIMIT),
    )(f_t, b_col, ktok)


def _causal_mask():
    r = lax.broadcasted_iota(jnp.int32, (TT, TT), 0)
    c = lax.broadcasted_iota(jnp.int32, (TT, TT), 1)
    return r <= c


def _attn_fwd(proj_t, kaug, cq, L):
    nq = L // TT

    def body(q_ref, kaug_ref, v_ref, cq_ref, o_ref, lse_ref, qa_scr, s_scr, cmax_scr, m_scr, acc_scr):
        j = pl.program_id(1)
        rows = [slice(g * DH, (g + 1) * DH) for g in range(HG)]
        ones = jnp.ones((DF, TT), BF16)
        for g in range(HG):
            qa_scr[g] = jnp.concatenate(
                [q_ref[rows[g], :], _bias_rows(None), jnp.zeros((KA - DH - DF, TT), BF16)], axis=0)

        def scores(kt, masked):
            k_off = pl.multiple_of(kt * TT, TT)
            for g in range(HG):
                s = _dot(kaug_ref[g, pl.ds(k_off, TT), :], qa_scr[g])
                if masked:
                    s = jnp.where(_causal_mask(), s, NEG)
                s_scr[g] = s
                cmax_scr[g] = jnp.max(s, axis=0, keepdims=True)

        def softmax_pv(kt):
            k_off = pl.multiple_of(kt * TT, TT)
            for g in range(HG):
                m_old = m_scr[g]
                m_new = jnp.maximum(m_old, cmax_scr[g])
                alpha = jnp.exp2(m_old - m_new)
                p = jnp.exp2(s_scr[g] - m_new).astype(BF16)
                v1 = jnp.concatenate([v_ref[rows[g], pl.ds(k_off, TT)], ones], axis=0)
                acc_scr[g] = alpha * acc_scr[g] + _dot(v1, p)
                m_scr[g] = m_new

        m_scr[...] = jnp.full_like(m_scr, NEG)
        acc_scr[...] = jnp.zeros_like(acc_scr)

        scores(j, True)

        def step(t, c):
            softmax_pv(j - t)
            scores(j - 1 - t, False)
            return c

        lax.fori_loop(0, j, step, 0)
        softmax_pv(0)
        for g in range(HG):
            l = acc_scr[g, DH:DH + 1, :]
            o_ref[rows[g], :] = acc_scr[g, :DH, :] * (1.0 / l)
            lse_ref[g] = m_scr[g] + jnp.log2(l) + cq_ref[g]

    return pl.pallas_call(
        body, name="attn_fwd", grid=(H // HG, nq),
        in_specs=[
            pl.BlockSpec((HG * DH, TT), lambda h, j: (h, j)),
            pl.BlockSpec((HG, L, KA), lambda h, j: (h, 0, 0)),
            pl.BlockSpec((HG * DH, L), lambda h, j: (2 * H // HG + h, 0)),
            pl.BlockSpec((HG, 1, TT), lambda h, j: (h, 0, j)),
        ],
        out_specs=[
            pl.BlockSpec((HG * DH, TT), lambda h, j: (h, j)),
            pl.BlockSpec((HG, 1, TT), lambda h, j: (h, 0, j)),
        ],
        out_shape=[jax.ShapeDtypeStruct((DA, L), F32), jax.ShapeDtypeStruct((H, 1, L), F32)],
        scratch_shapes=[pltpu.VMEM((HG, KA, TT), BF16), pltpu.VMEM((HG, TT, TT), F32), pltpu.VMEM((HG, 1, TT), F32),
                        pltpu.VMEM((HG, 1, TT), F32), pltpu.VMEM((HG, DH + DF, TT), F32)],
        compiler_params=_params(2),
    )(proj_t, kaug, proj_t, cq)


def _gate_common(o, za, gb, gc, xc, zc, gcp, xcp, cw_ref, ga_ref, gcn_ref, first):
    n_rep = TT // TB
    a = gc * xc
    a_prev = jnp.where(first, 0.0, gcp * xcp)
    full = jnp.concatenate([a_prev, a], axis=1)
    a1 = pltpu.roll(full, 1, 1)[:, TB:]
    a2 = pltpu.roll(full, 2, 1)[:, TB:]
    w0 = jnp.tile(cw_ref[0], (1, n_rep))
    w1 = jnp.tile(cw_ref[1], (1, n_rep))
    w2 = jnp.tile(cw_ref[2], (1, n_rep))
    cv = w0 * a2 + w1 * a1 + w2 * a
    e = gb * cv
    e3 = _grouped(e)
    rc = _group_rstd(e3)
    ec = (e3 * rc).reshape(DA, TT)
    o3 = _grouped(o)
    ra = _group_rstd(o3)
    oa = (o3 * ra).reshape(DA, TT)
    g_a = jnp.tile(ga_ref[...], (1, n_rep))
    g_c = jnp.tile(gcn_ref[...], (1, n_rep))
    sa = _sigmoid(za)
    sc = _sigmoid(zc)
    return dict(a=a, a1=a1, a2=a2, w0=w0, w1=w1, w2=w2, cv=cv, e=e, rc=rc, ec=ec, ra=ra, oa=oa,
                g_a=g_a, g_c=g_c, sa=sa, sc=sc)


def _gate_specs(nj, rev):
    def jj(i):
        return (nj - 1 - i) if rev else i

    def sec(s):
        return pl.BlockSpec((DA, TT), lambda i: (s, jj(i)))

    def halo(s):
        return pl.BlockSpec((DA, TB), lambda i: (s, jnp.maximum(3 * jj(i) - 1, 0)))

    return [pl.BlockSpec((DA, TT), lambda i: (0, jj(i))), sec(3), sec(4), sec(5), sec(6), sec(7), halo(5), halo(6),
            _full_spec((3, DA, TB)), _full_spec((DA, TB)), _full_spec((DA, TB))]


def _gate_fwd(o_t, proj_t, cw_b, ga_b, gcn_b, L):
    nj = L // TT

    def body(o_ref, za_ref, gb_ref, gc_ref, xc_ref, zc_ref, gcp_ref, xcp_ref, cw_ref, ga_ref, gcn_ref, mix_ref):
        j = pl.program_id(0)
        f32 = lambda r: r[...].astype(F32)
        za, zc = f32(za_ref), f32(zc_ref)
        g = _gate_common(o_ref[...], za, f32(gb_ref), f32(gc_ref), f32(xc_ref), zc, f32(gcp_ref), f32(xcp_ref),
                         cw_ref, ga_ref, gcn_ref, j == 0)
        mix_ref[:DA, :] = (g["oa"] * g["g_a"] * (za * g["sa"])).astype(BF16)
        mix_ref[DA:, :] = (g["ec"] * g["g_c"] * (zc * g["sc"])).astype(BF16)

    return pl.pallas_call(
        body, name="gate_fwd", grid=(nj,),
        in_specs=_gate_specs(nj, False),
        out_specs=pl.BlockSpec((2 * DA, TT), lambda j: (0, j)),
        out_shape=jax.ShapeDtypeStruct((2 * DA, L), BF16),
        compiler_params=_params(),
    )(o_t, proj_t, proj_t, proj_t, proj_t, proj_t, proj_t, proj_t, cw_b, ga_b, gcn_b)


def _outproj(mix_t, w_out, x, meta_full, fng, target, L):
    nj = L // TT

    def body(mix_ref, w_ref, xa_ref, xb_ref, xc_ref, meta_ref, g_ref, ta_ref, tb_ref, tc_ref,
             dout_ref, dmix_ref, dw_ref, loss_ref, dg_ref):
        t = pl.program_id(0)

        @pl.when(t == 0)
        def _():
            dw_ref[...] = jnp.zeros_like(dw_ref)
            loss_ref[...] = jnp.zeros_like(loss_ref)
            dg_ref[...] = jnp.zeros_like(dg_ref)

        mix = mix_ref[...]
        o = _dot(mix, w_ref[...], TN_DIMS) + _h_tile(t, xa_ref, xb_ref, xc_ref, meta_ref)
        r = lax.rsqrt(jnp.mean(o * o, axis=-1, keepdims=True) + EPS)
        g = g_ref[...]
        orn = o * r
        tgt = jnp.concatenate([ta_ref[...], tb_ref[...], tc_ref[...]], axis=0)
        row = lax.broadcasted_iota(jnp.int32, (TT, 1), 0)
        real = jnp.where((t > 0) | (row >= TB), 1.0, 0.0)
        diff = (orn * g - tgt) * real
        loss_ref[...] += 0.5 * jnp.sum(diff * diff) * (1.0 / D)
        dy = diff * (1.0 / D)
        dg_ref[...] += jnp.sum(dy * orn, axis=0, keepdims=True)
        gy = dy * g
        dout = r * gy - orn * (r * jnp.mean(gy * orn, axis=-1, keepdims=True))
        dout_ref[...] = dout
        db = dout.astype(BF16)
        dmix_ref[...] = _dot(w_ref[...], db, NT_DIMS).astype(BF16)
        dw_ref[...] += _dot(mix, db)

    return pl.pallas_call(
        body, name="outproj", grid=(nj,),
        in_specs=[pl.BlockSpec((D, TT), lambda t: (0, t)), _full_spec((D, D))] + _x_specs3()
                 + [_full_spec((NM, D)), _full_spec((1, D))] + _x_specs3(),
        out_specs=[pl.BlockSpec((TT, D), lambda t: (t, 0)), pl.BlockSpec((D, TT), lambda t: (0, t)),
                   _full_spec((D, D)), _full_spec((1, 1)), _full_spec((1, D))],
        out_shape=[jax.ShapeDtypeStruct((L, D), F32), jax.ShapeDtypeStruct((D, L), BF16),
                   jax.ShapeDtypeStruct((D, D), F32), jax.ShapeDtypeStruct((1, 1), F32),
                   jax.ShapeDtypeStruct((1, D), F32)],
        compiler_params=_params(),
    )(mix_t, w_out, x, x, x, meta_full, fng, target, target, target)


def _gate_bwd(dmix_t, o_t, proj_t, cw_b, ga_b, gcn_b, L):
    nj = L // TT

    def body(dmix_ref, o_ref, za_ref, gb_ref, gc_ref, xc_ref, zc_ref, gcp_ref, xcp_ref, cw_ref, ga_ref, gcn_ref,
             do_ref, dd_ref, dg5_ref, dga_ref, dgc_ref, dcw_ref, carry_ref):
        i = pl.program_id(0)
        j = nj - 1 - i

        @pl.when(i == 0)
        def _():
            carry_ref[...] = jnp.zeros_like(carry_ref)
            dga_ref[...] = jnp.zeros_like(dga_ref)
            dgc_ref[...] = jnp.zeros_like(dgc_ref)
            dcw_ref[...] = jnp.zeros_like(dcw_ref)

        f32 = lambda r: r[...].astype(F32)
        o, za, gb, gc, xc, zc = o_ref[...], f32(za_ref), f32(gb_ref), f32(gc_ref), f32(xc_ref), f32(zc_ref)
        g = _gate_common(o, za, gb, gc, xc, zc, f32(gcp_ref), f32(xcp_ref), cw_ref, ga_ref, gcn_ref, j == 0)
        dya = dmix_ref[:DA, :].astype(F32)
        dyc = dmix_ref[DA:, :].astype(F32)
        sa, sc = g["sa"], g["sc"]

        dn = dya * (za * sa)
        dg5_ref[0:DA, :] = (dya * (g["oa"] * g["g_a"]) * (sa * (1.0 + za * (1.0 - sa)))).astype(BF16)
        dga_ref[...] += _lane_tiles_sum(dn * g["oa"])
        dng = dn * g["g_a"]
        mean_a = jnp.mean(_grouped(dng * g["oa"]), axis=1, keepdims=True)
        do = ((_grouped(dng) - _grouped(g["oa"]) * mean_a) * g["ra"]).reshape(DA, TT)
        do_ref[...] = do.astype(BF16)
        dd = jnp.sum(_grouped(do * o), axis=1)
        for h in range(H):
            dd_ref[h] = dd[h:h + 1, :]

        dnc = dyc * (zc * sc)
        dg5_ref[4 * DA:5 * DA, :] = (dyc * (g["ec"] * g["g_c"]) * (sc * (1.0 + zc * (1.0 - sc)))).astype(BF16)
        dgc_ref[...] += _lane_tiles_sum(dnc * g["ec"])
        dncg = dnc * g["g_c"]
        mean_c = jnp.mean(_grouped(dncg * g["ec"]), axis=1, keepdims=True)
        de = ((_grouped(dncg) - _grouped(g["ec"]) * mean_c) * g["rc"]).reshape(DA, TT)
        dg5_ref[DA:2 * DA, :] = (de * g["cv"]).astype(BF16)
        dcv = de * gb
        full = jnp.concatenate([dcv, carry_ref[...]], axis=1)
        d1 = pltpu.roll(full, TT + TB - 1, 1)[:, :TT]
        d2 = pltpu.roll(full, TT + TB - 2, 1)[:, :TT]
        carry_ref[...] = dcv[:, :TB]
        da = g["w2"] * dcv + g["w1"] * d1 + g["w0"] * d2
        dg5_ref[2 * DA:3 * DA, :] = (da * xc).astype(BF16)
        dg5_ref[3 * DA:4 * DA, :] = (da * gc).astype(BF16)
        dcw_ref[0] += _lane_tiles_sum(dcv * g["a2"])
        dcw_ref[1] += _lane_tiles_sum(dcv * g["a1"])
        dcw_ref[2] += _lane_tiles_sum(dcv * g["a"])

    rj = lambda i: nj - 1 - i
    return pl.pallas_call(
        body, name="gate_bwd", grid=(nj,),
        in_specs=[pl.BlockSpec((2 * DA, TT), lambda i: (0, rj(i)))] + _gate_specs(nj, True),
        out_specs=[
            pl.BlockSpec((DA, TT), lambda i: (0, rj(i))),
            pl.BlockSpec((H, 1, TT), lambda i: (0, 0, rj(i))),
            pl.BlockSpec((5 * DA, TT), lambda i: (0, rj(i))),
            _full_spec((DA, TB)), _full_spec((DA, TB)), _full_spec((3, DA, TB)),
        ],
        out_shape=[
            jax.ShapeDtypeStruct((DA, L), BF16),
            jax.ShapeDtypeStruct((H, 1, L), F32),
            jax.ShapeDtypeStruct((5 * DA, L), BF16),
            jax.ShapeDtypeStruct((DA, TB), F32),
            jax.ShapeDtypeStruct((DA, TB), F32),
            jax.ShapeDtypeStruct((3, DA, TB), F32),
        ],
        scratch_shapes=[pltpu.VMEM((DA, TB), F32)],
        compiler_params=_params(),
    )(dmix_t, o_t, proj_t, proj_t, proj_t, proj_t, proj_t, proj_t, proj_t, cw_b, ga_b, gcn_b)


def _attn_bwd(proj_t, kaug, vtok, do_t, lse, dd, cq, L):
    nk = L // TT

    def body(q_ref, kaug_ref, vtok_ref, kt_ref, do_ref, lse_ref, dd_ref, cq_ref,
             dq_ref, dk_ref, dv_ref, dck_ref, dcq_ref, dq_acc, q1_scr, kt1_scr, s_scr, dp_scr, dv_scr, dk_scr):
        i = pl.program_id(1)

        @pl.when(i == 0)
        def _():
            dq_acc[...] = jnp.zeros_like(dq_acc)

        rows = [slice(g * DH, (g + 1) * DH) for g in range(HG)]
        ones = jnp.ones((DF, TT), BF16)
        zpad = jnp.zeros((KA - DH - DF, TT), BF16)
        for g in range(HG):
            kt1_scr[g] = jnp.concatenate([kt_ref[rows[g], :], ones], axis=0)
        dv_scr[...] = jnp.zeros_like(dv_scr)
        dk_scr[...] = jnp.zeros_like(dk_scr)

        def scores(jq, masked):
            q_off = pl.multiple_of(jq * TT, TT)
            for g in range(HG):
                bias = cq_ref[g, :, pl.ds(q_off, TT)] - lse_ref[g, :, pl.ds(q_off, TT)]
                q1 = jnp.concatenate([q_ref[rows[g], pl.ds(q_off, TT)], _bias_rows(bias)], axis=0)
                q1_scr[g] = q1
                s = _dot(kaug_ref[g], jnp.concatenate([q1, zpad], axis=0))
                if masked:
                    s = jnp.where(_causal_mask(), s, NEG)
                s_scr[g] = s
                dp_scr[g] = _dot(vtok_ref[g], do_ref[rows[g], pl.ds(q_off, TT)])

        def grads(jq):
            q_off = pl.multiple_of(jq * TT, TT)
            for g in range(HG):
                p = jnp.exp2(s_scr[g])
                ds = (p * (dp_scr[g] - dd_ref[g, :, pl.ds(q_off, TT)])).astype(BF16)
                dv_scr[g] += _dot(do_ref[rows[g], pl.ds(q_off, TT)], p.astype(BF16), NT_DIMS)
                dk_scr[g] += _dot(q1_scr[g], ds, NT_DIMS)
                dq_acc[g, :, pl.ds(q_off, TT)] += _dot(kt1_scr[g], ds)

        scores(i, True)

        def step(jq, c):
            grads(jq)
            scores(jq + 1, False)
            return c

        lax.fori_loop(i, nk - 1, step, 0)
        grads(nk - 1)
        for g in range(HG):
            dv_ref[rows[g], :] = dv_scr[g].astype(BF16)
            dk_ref[rows[g], :] = (dk_scr[g, :DH, :] * LN2).astype(BF16)
            dck_ref[g] = dk_scr[g, DH:DH + 1, :]

        @pl.when(i == nk - 1)
        def _():
            for g in range(HG):
                dq_ref[rows[g], :] = (dq_acc[g, :DH, :] * (DH ** -0.5)).astype(BF16)
                dcq_ref[g] = dq_acc[g, DH:DH + 1, :]

    head = lambda h, i: (h, 0)
    row = lambda h, i: (h, 0, 0)
    return pl.pallas_call(
        body, name="attn_bwd", grid=(H // HG, nk),
        in_specs=[
            pl.BlockSpec((HG * DH, L), head),
            pl.BlockSpec((HG, TT, KA), lambda h, i: (h, i, 0)),
            pl.BlockSpec((HG, TT, DH), lambda h, i: (h, i, 0)),
            pl.BlockSpec((HG * DH, TT), lambda h, i: (H // HG + h, i)),
            pl.BlockSpec((HG * DH, L), head),
            pl.BlockSpec((HG, 1, L), row), pl.BlockSpec((HG, 1, L), row), pl.BlockSpec((HG, 1, L), row),
        ],
        out_specs=[
            pl.BlockSpec((HG * DH, L), head),
            pl.BlockSpec((HG * DH, TT), lambda h, i: (h, i)),
            pl.BlockSpec((HG * DH, TT), lambda h, i: (h, i)),
            pl.BlockSpec((HG, 1, TT), lambda h, i: (h, 0, i)),
            pl.BlockSpec((HG, 1, L), row),
        ],
        out_shape=[jax.ShapeDtypeStruct((DA, L), BF16), jax.ShapeDtypeStruct((DA, L), BF16),
                   jax.ShapeDtypeStruct((DA, L), BF16), jax.ShapeDtypeStruct((H, 1, L), F32),
                   jax.ShapeDtypeStruct((H, 1, L), F32)],
        scratch_shapes=[
            pltpu.VMEM((HG, DH + DF, L), F32),
            pltpu.VMEM((HG, DH + DF, TT), BF16), pltpu.VMEM((HG, DH + DF, TT), BF16),
            pltpu.VMEM((HG, TT, TT), F32), pltpu.VMEM((HG, TT, TT), F32),
            pltpu.VMEM((HG, DH, TT), F32), pltpu.VMEM((HG, DH + DF, TT), F32)],
        compiler_params=_params(2),
    )(proj_t, kaug, vtok, proj_t, do_t, lse, dd, cq)


def _fgate_bwd(dcq, dck, sg, L):
    def body(dcq_ref, dck_ref, sg_ref, df_ref, db_ref):
        dc = jnp.concatenate([dcq_ref[h] - dck_ref[h] for h in range(H)], axis=0)
        idx = lax.broadcasted_iota(jnp.int32, (H, L), 1)
        r = dc
        s = 1
        while s < L:
            r = r + jnp.where(idx + s < L, pltpu.roll(r, L - s, 1), 0.0)
            s *= 2
        df = r * sg_ref[...]
        db_ref[...] = jnp.sum(df, axis=1, keepdims=True)
        df_ref[...] = jnp.concatenate([df, jnp.zeros((DF - H, L), F32)], axis=0).astype(BF16)

    return pl.pallas_call(
        body, name="fgate_bwd",
        out_shape=[jax.ShapeDtypeStruct((DF, L), BF16), jax.ShapeDtypeStruct((H, 1), F32)],
        compiler_params=pltpu.CompilerParams(vmem_limit_bytes=VMEM_LIMIT),
    )(dcq, dck, sg)


def _inproj_bwd_x(w, dq_t, dk_t, dv_t, dg5_t, df_t, dout, x, meta_full, norm_g, L):
    nb = L // TB
    seq = x.shape[0]

    def body(w_ref, dq_ref, dk_ref, dv_ref, dg5_ref, df_ref, dout_ref, x_ref, meta_ref, g_ref,
             gx_ref, dmeta_ref, dg_ref):
        t = pl.program_id(0)

        @pl.when(t == 0)
        def _():
            dg_ref[...] = jnp.zeros_like(dg_ref)

        du = _dot(dq_ref[...], w_ref[0:DA, :], TN_DIMS)
        du += _dot(dk_ref[...], w_ref[DA:2 * DA, :], TN_DIMS)
        du += _dot(dv_ref[...], w_ref[2 * DA:3 * DA, :], TN_DIMS)
        du += _dot(dg5_ref[...], w_ref[3 * DA:NSEC * DA, :], TN_DIMS)
        du += _dot(df_ref[...], w_ref[NSEC * DA:DPROJ, :], TN_DIMS)
        hb = _h_block(t, x_ref, meta_ref)
        r = lax.rsqrt(jnp.mean(hb * hb, axis=-1, keepdims=True) + EPS)
        hn = hb * r
        dg_ref[...] += jnp.sum(du * hn, axis=0, keepdims=True)
        gu = du * g_ref[...]
        dh = dout_ref[...] + r * gu - hn * (r * jnp.mean(gu * hn, axis=-1, keepdims=True))
        gx_ref[...] = dh

        @pl.when(t == 0)
        def _():
            dmeta_ref[...] = dh[P0:, :]

    blk = lambda rows: pl.BlockSpec((rows, TB), lambda t: (0, t))
    return pl.pallas_call(
        body, name="inproj_bwd_x", grid=(nb,),
        in_specs=[_full_spec((DPROJ, D)), blk(DA), blk(DA), blk(DA), blk(5 * DA), blk(DF),
                  pl.BlockSpec((TB, D), lambda t: (t, 0)), _x_spec(), _full_spec((NM, D)), _full_spec((1, D))],
        out_specs=[_x_spec(), _full_spec((NM, D)), _full_spec((1, D))],
        out_shape=[jax.ShapeDtypeStruct((seq, D), F32), jax.ShapeDtypeStruct((NM, D), F32),
                   jax.ShapeDtypeStruct((1, D), F32)],
        compiler_params=_params(),
    )(w, dq_t, dk_t, dv_t, dg5_t, df_t, dout, x, meta_full, norm_g)


def _inproj_bwd_w(u, dq_t, dk_t, dv_t, dg5_t, df_t, L):
    kt = L // 3
    nkt = 3

    def body(u_ref, dq_ref, dk_ref, dv_ref, dg5_ref, df_ref, dw_ref, dwf_ref):
        s = pl.program_id(0)
        k = pl.program_id(1)

        @pl.when(k == 0)
        def _():
            dw_ref[...] = jnp.zeros_like(dw_ref)

        @pl.when((s == 0) & (k == 0))
        def _():
            dwf_ref[...] = jnp.zeros_like(dwf_ref)

        u_blk = u_ref[...]
        for sec, ref in ((0, dq_ref), (1, dk_ref), (2, dv_ref)):
            @pl.when(s == sec)
            def _(ref=ref):
                dw_ref[...] += _dot(ref[...], u_blk)

        @pl.when(s >= 3)
        def _():
            dw_ref[...] += _dot(dg5_ref[...], u_blk)

        @pl.when(s == NSEC - 1)
        def _():
            dwf_ref[...] += _dot(df_ref[...], u_blk)

    def only(sec):
        return lambda s, k: (0, jnp.where(s == sec, k, 0))

    return pl.pallas_call(
        body, name="inproj_bwd_w", grid=(NSEC, nkt),
        in_specs=[
            pl.BlockSpec((kt, D), lambda s, k: (k, 0)),
            pl.BlockSpec((DA, kt), only(0)), pl.BlockSpec((DA, kt), only(1)), pl.BlockSpec((DA, kt), only(2)),
            pl.BlockSpec((DA, kt), lambda s, k: (jnp.maximum(s - 3, 0), jnp.where(s >= 3, k, 0))),
            pl.BlockSpec((DF, kt), only(NSEC - 1)),
        ],
        out_specs=[pl.BlockSpec((DA, D), lambda s, k: (s, 0)), _full_spec((DF, D))],
        out_shape=[jax.ShapeDtypeStruct((NSEC * DA, D), F32), jax.ShapeDtypeStruct((DF, D), F32)],
        compiler_params=_params(2),
    )(u, dq_t, dk_t, dv_t, dg5_t, df_t)


def _adamw(w, g, m, v):
    m = ADAM_B1 * m + (1.0 - ADAM_B1) * g
    v = ADAM_B2 * v + (1.0 - ADAM_B2) * (g * g)
    m_hat = m / (1.0 - ADAM_B1 ** ADAM_STEP)
    v_hat = v / (1.0 - ADAM_B2 ** ADAM_STEP)
    delta = -ADAM_LR * (m_hat / (jnp.sqrt(v_hat) + ADAM_EPS) + ADAM_WD * w)
    return delta, m, v


def _sum_slabs(ref, rows):
    g = ref[0, rows, :].astype(F32)
    for j in range(1, NDEV):
        g = g + ref[j, rows, :].astype(F32)
    return g


def _adamw_big(recv, w_in_t, m_in_t, v_in_t, w_out, m_out, v_out):
    cb = CB
    e_sh = D // NDEV
    in_shape = jax.ShapeDtypeStruct(w_in_t.shape, F32)
    out_shape = jax.ShapeDtypeStruct(w_out.shape, F32)

    def body(r_ref, wi_ref, mi_ref, vi_ref, wo_ref, mo_ref, vo_ref, gi, di, mi, vi, go, do, mo, vo):
        g = _sum_slabs(r_ref, slice(0, WSHP))[:WSH]
        d, mn, vn = _adamw(wi_ref[...], g, mi_ref[...], vi_ref[...])
        gi[...], di[...], mi[...], vi[...] = g, d, mn, vn
        g = _sum_slabs(r_ref, slice(WSHP, WROWS))
        d, mn, vn = _adamw(wo_ref[0], g, mo_ref[0], vo_ref[0])
        go[0], do[0], mo[0], vo[0] = g, d, mn, vn

    ispec = pl.BlockSpec((WSH, cb), lambda i: (0, i))
    ospec = pl.BlockSpec((1, e_sh, cb), lambda i: (0, 0, i))
    return pl.pallas_call(
        body, name="adamw_big", grid=(D // cb,),
        in_specs=[pl.BlockSpec((NDEV, WROWS, cb), lambda i: (0, 0, i)), ispec, ispec, ispec, ospec, ospec, ospec],
        out_specs=[ispec] * 4 + [ospec] * 4, out_shape=[in_shape] * 4 + [out_shape] * 4,
        compiler_params=_params(),
    )(recv, w_in_t, m_in_t, v_in_t, w_out, m_out, v_out)


F0 = 3 * DA


def _unshard_weights(w_all):
    def body(w_ref, wt_ref, wo_ref):
        def ref_rows(lo, hi):
            pieces, r = [], lo
            while r < hi:
                sh, off = divmod(r, WSH)
                n = min(hi - r, WSH - off)
                pieces.append(w_ref[sh, off:off + n, :])
                r += n
            return pieces

        for s in range(NSEC):
            lo = s * DA if s < 3 else s * DA + H
            wt_ref[s * DA:(s + 1) * DA, :] = jnp.concatenate(ref_rows(lo, lo + DA), axis=0)
        wt_ref[NSEC * DA:DPROJ, :] = jnp.concatenate(
            ref_rows(F0, F0 + H) + [jnp.zeros((DF - H, CB), BF16)], axis=0)
        for i in range(NDEV):
            wo_ref[i * (D // NDEV):(i + 1) * (D // NDEV), :] = w_ref[i, WSHP:WROWS, :]

    return pl.pallas_call(
        body, name="unshard_weights", grid=(D // CB,),
        in_specs=[pl.BlockSpec((NDEV, WROWS, CB), lambda i: (0, 0, i))],
        out_specs=[pl.BlockSpec((DPROJ, CB), lambda i: (0, i)), pl.BlockSpec((D, CB), lambda i: (0, i))],
        out_shape=[jax.ShapeDtypeStruct((DPROJ, D), BF16), jax.ShapeDtypeStruct((D, D), BF16)],
        compiler_params=_params(),
    )(w_all)


def _shard_weight_grads(dw_main, dw_f, dw_out):
    def body(dm_ref, df_ref, do_ref, p_ref):
        def ref_rows(lo, hi):
            pieces, r = [], lo
            while r < hi:
                if r < F0:
                    n = min(hi, F0) - r
                    pieces.append(dm_ref[r:r + n, :])
                elif r < F0 + H:
                    n = min(hi, F0 + H) - r
                    pieces.append(df_ref[r - F0:r - F0 + n, :])
                else:
                    n = hi - r
                    pieces.append(dm_ref[r - H:r - H + n, :])
                r += n
            return pieces

        for i in range(NDEV):
            rows = jnp.concatenate(ref_rows(i * WSH, (i + 1) * WSH) + [jnp.zeros((WSHP - WSH, CB), F32)], axis=0)
            p_ref[i, 0:WSHP, :] = rows.astype(BF16)
            p_ref[i, WSHP:WROWS, :] = do_ref[i * (D // NDEV):(i + 1) * (D // NDEV), :].astype(BF16)

    col = lambda rows: pl.BlockSpec((rows, CB), lambda i: (0, i))
    return pl.pallas_call(
        body, name="shard_weight_grads", grid=(D // CB,),
        in_specs=[col(NSEC * DA), col(DF), col(D)],
        out_specs=pl.BlockSpec((NDEV, WROWS, CB), lambda i: (0, 0, i)),
        out_shape=jax.ShapeDtypeStruct((NDEV, WROWS, D), BF16),
        compiler_params=_params(),
    )(dw_main, dw_f, dw_out)


def _adamw_small(recv, w, m, v):
    shape = jax.ShapeDtypeStruct((SROWS, TB), F32)

    def body(r_ref, w_ref, m_ref, v_ref, g_out, d_out, m_out, v_out):
        g = _sum_slabs(r_ref, slice(0, SROWS))
        d, mn, vn = _adamw(w_ref[...], g, m_ref[...], v_ref[...])
        g_out[...], d_out[...], m_out[...], v_out[...] = g, d, mn, vn

    return pl.pallas_call(body, name="adamw_small", out_shape=[shape] * 4)(recv, w, m, v)


def _tile_rows(a, rows, lanes=TB):
    a = a.reshape(rows, lanes)
    return jnp.pad(a, ((0, -rows % 8), (0, TB - lanes)))


def _pack_small(norm_g, final_norm_g, attn_norm_g, conv_norm_g, b_f, meta_sh, conv_w_sh, loss=None):
    b_row = b_f.reshape(1, H) if loss is None else jnp.concatenate([b_f.reshape(1, H), loss.reshape(1, 1)], axis=1)
    packed = jnp.concatenate([
        _tile_rows(norm_g, 8), _tile_rows(final_norm_g, 8), _tile_rows(attn_norm_g, 4), _tile_rows(conv_norm_g, 4),
        _tile_rows(b_row, 1, b_row.shape[1]), _tile_rows(meta_sh, NM), _tile_rows(conv_w_sh, 3, DH)], axis=0)
    assert packed.shape == (SROWS, TB)
    return packed


def _unpack_small(p):
    return dict(
        norm_g=p[0:8].reshape(1, D), final_norm_g=p[8:16].reshape(D), attn_norm_g=p[16:20].reshape(1, DA),
        conv_norm_g=p[24:28].reshape(1, DA), b_f=p[32:33, :H].reshape(1, H), meta=p[40:56].reshape(NM, TB),
        conv_w=p[56:59, :DH].reshape(1, 3, DH))


def kernel(x, meta, norm_g, w_in, b_f, conv_w, attn_norm_g, conv_norm_g, w_out, final_norm_g, loss_target, m_meta, m_norm_g, m_w_in, m_b_f, m_conv_w, m_attn_norm_g, m_conv_norm_g, m_w_out, m_final_norm_g, v_meta, v_norm_g, v_w_in, v_b_f, v_conv_w, v_attn_norm_g, v_conv_norm_g, v_w_out, v_final_norm_g):
    seq = x.shape[1]
    L = seq + TB
    assert x.shape == (1, seq, D) and L % TT == 0 and w_in.shape == (1, D, WSH)
    x2 = x[0]
    tgt = loss_target[0]

    w_slab = jnp.concatenate(
        [jnp.pad(w_in[0].T, ((0, WSHP - WSH), (0, 0))), w_out[0]], axis=0).astype(BF16)
    small = jnp.concatenate([meta, _tile_rows(conv_w[0], 3, DH)], axis=0)
    w_all = _all_gather(w_slab, "gather_weights")
    small_all = _all_gather(small, "gather_small")

    w_t, w_out_full = _unshard_weights(w_all)
    meta_full = jnp.transpose(small_all[:, :NM, :], (1, 0, 2)).reshape(NM, D)
    conv_w_full = jnp.transpose(small_all[:, NM:NM + 3, :DH], (1, 0, 2)).reshape(3, DA)

    lane_b = lambda p: jnp.broadcast_to(p.reshape(-1, DA, 1), (p.size // DA, DA, TB))
    cw_b = lane_b(conv_w_full)
    ga_b = lane_b(attn_norm_g)[0]
    gcn_b = lane_b(conv_norm_g)[0]

    u, proj_t, f_t, ktok, vtok = _inproj_fwd(x2, meta_full, norm_g, w_t, L)
    cq, kaug, sg = _fgate_fwd(f_t, b_f.reshape(H, 1), ktok, L)
    o_t, lse = _attn_fwd(proj_t, kaug, cq, L)
    mix_t = _gate_fwd(o_t, proj_t, cw_b, ga_b, gcn_b, L)

    dout, dmix_t, dw_out, loss_part, dg_final = _outproj(
        mix_t, w_out_full, x2, meta_full, final_norm_g.reshape(1, D), tgt, L)
    do_t, dd, dg5_t, dga_p, dgc_p, dcw_p = _gate_bwd(dmix_t, o_t, proj_t, cw_b, ga_b, gcn_b, L)
    dq_t, dk_t, dv_t, dck, dcq = _attn_bwd(proj_t, kaug, vtok, do_t, lse, dd, cq, L)
    df_t, db_f = _fgate_bwd(dcq, dck, sg, L)
    grad_x, dmeta, dg_norm = _inproj_bwd_x(w_t, dq_t, dk_t, dv_t, dg5_t, df_t, dout, x2, meta_full, norm_g, L)
    dw_main, dw_f = _inproj_bwd_w(u, dq_t, dk_t, dv_t, dg5_t, df_t, L)

    big_parts = _shard_weight_grads(dw_main, dw_f, dw_out)
    dga = jnp.sum(dga_p, axis=1)
    dgc = jnp.sum(dgc_p, axis=1)
    dcw = jnp.sum(dcw_p, axis=2)
    small_parts = jnp.stack([
        _pack_small(dg_norm, dg_final, dga, dgc, db_f, dmeta[:, j * TB:(j + 1) * TB], dcw[:, j * DH:(j + 1) * DH],
                    loss=loss_part)
        for j in range(NDEV)], axis=0)
    big_recv = _exchange(big_parts, "exchange_weight_grads")
    small_recv = _exchange(small_parts, "exchange_small_grads")

    big_out = _adamw_big(big_recv, w_in[0].T, m_w_in[0].T, v_w_in[0].T, w_out, m_w_out, v_w_out)
    g_w_in, d_w_in, nm_w_in, nv_w_in = [a.T[None] for a in big_out[:4]]
    g_w_out, d_w_out, nm_w_out, nv_w_out = big_out[4:]
    wp = _pack_small(norm_g, final_norm_g, attn_norm_g, conv_norm_g, b_f, meta, conv_w)
    mp = _pack_small(m_norm_g, m_final_norm_g, m_attn_norm_g, m_conv_norm_g, m_b_f, m_meta, m_conv_w)
    vp = _pack_small(v_norm_g, v_final_norm_g, v_attn_norm_g, v_conv_norm_g, v_b_f, v_meta, v_conv_w)
    small_out = _adamw_small(small_recv, wp, mp, vp)
    sm = [_unpack_small(p) for p in small_out]
    loss = small_out[0][32, H]
    order = ("meta", "norm_g", "w_in", "b_f", "conv_w", "attn_norm_g", "conv_norm_g", "w_out", "final_norm_g")
    groups = []
    for k, (wi, wo) in enumerate(((g_w_in, g_w_out), (d_w_in, d_w_out), (nm_w_in, nm_w_out), (nv_w_in, nv_w_out))):
        d = dict(sm[k], w_in=wi, w_out=wo)
        groups.append([d[n] for n in order])
    return (loss, grad_x[None], *groups[0], *groups[1], *groups[2], *groups[3])
```

```python
import functools

import jax
import jax.numpy as jnp
from jax import lax
from jax.experimental import pallas as pl
from jax.experimental.pallas import tpu as pltpu

F32 = jnp.float32
BF16 = jnp.bfloat16

D = 1024
DA = 512
H = 8
DH = 64
NM = 16
TB = 128
P0 = TB - NM
TT = 3 * TB
HG = 4
NDEV = 8
NSEC = 8
DF = 16
DPROJ = NSEC * DA + DF
WSH = 513
WSHP = 528
WROWS = WSHP + D // NDEV
SROWS = 64
EPS = 1e-6
NEG = -1e30
LOG2E = 1.4426950408889634
LN2 = 0.6931471805599453
QSCALE = DH ** -0.5 * LOG2E
KA = 128
CB = 256
VMEM_LIMIT = 56 * 1024 * 1024

ADAM_LR = 0.001
ADAM_B1 = 0.9
ADAM_B2 = 0.999
ADAM_EPS = 1e-08
ADAM_WD = 0.01
ADAM_STEP = 10

NT_DIMS = (((1,), (1,)), ((), ()))
TN_DIMS = (((0,), (0,)), ((), ()))
MESH = pl.DeviceIdType.MESH


def _params(n_axes=1, vmem=VMEM_LIMIT):
    return pltpu.CompilerParams(dimension_semantics=("arbitrary",) * n_axes, vmem_limit_bytes=vmem)


def _dot(a, b, dims=None):
    if dims is None:
        return jnp.dot(a, b, preferred_element_type=F32)
    return lax.dot_general(a, b, dims, preferred_element_type=F32)


def _my_place():
    return lax.axis_index("x"), lax.axis_index("y"), lax.axis_index("c")


def _all_gather(x, name):
    def body(x_ref, out_ref, send_sems, recv_sems, local_sem):
        mx, my, mc = _my_place()
        me, sibling = (mx, my, mc), (mx, my, 1 - mc)
        chips = [(1 - mx, my), (mx, 1 - my), (1 - mx, 1 - my)]

        def slot(px, py, pc):
            return out_ref.at[4 * px + 2 * py + pc]

        def copy(k, block, to, src=None):
            return pltpu.make_async_remote_copy(
                src_ref=slot(*block) if src is None else src, dst_ref=slot(*block),
                send_sem=send_sems.at[k], recv_sem=recv_sems.at[k], device_id=to, device_id_type=MESH)

        mine = pltpu.make_async_copy(x_ref, slot(*me), local_sem)
        mine.start()
        first = [copy(0, me, sibling, src=x_ref)]
        first += [copy(1 + j, me, (*chip, mc), src=x_ref) for j, chip in enumerate(chips)]
        for cp in first:
            cp.start()
        passed = [copy(4 + j, (*chip, mc), sibling) for j, chip in enumerate(chips)]
        for j, chip in enumerate(chips):
            copy(1 + j, (*chip, mc), me).wait_recv()
            passed[j].start()
        copy(0, sibling, me).wait_recv()
        for j, chip in enumerate(chips):
            copy(4 + j, (*chip, 1 - mc), me).wait_recv()
        for cp in first + passed:
            cp.wait_send()
        mine.wait()

    return pl.pallas_call(
        body, name=name,
        out_shape=jax.ShapeDtypeStruct((NDEV,) + x.shape, x.dtype),
        in_specs=[pl.BlockSpec(memory_space=pl.ANY)],
        out_specs=pl.BlockSpec(memory_space=pl.ANY),
        scratch_shapes=[pltpu.SemaphoreType.DMA((7,)), pltpu.SemaphoreType.DMA((7,)), pltpu.SemaphoreType.DMA],
    )(x)


def _exchange(parts, name):
    def body(p_ref, out_ref, send_sems, recv_sems, local_sem):
        mx, my, mc = _my_place()
        me = 4 * mx + 2 * my + mc
        mine = pltpu.make_async_copy(p_ref.at[me], out_ref.at[me], local_sem)
        mine.start()

        def peer_of(m):
            return ((1 - mx) if m & 4 else mx, (1 - my) if m & 2 else my, (1 - mc) if m & 1 else mc)

        def copy(m, src_slot, dst_slot):
            px, py, pc = peer_of(m)
            return pltpu.make_async_remote_copy(
                src_ref=p_ref.at[src_slot], dst_ref=out_ref.at[dst_slot],
                send_sem=send_sems.at[m - 1], recv_sem=recv_sems.at[m - 1],
                device_id=(px, py, pc), device_id_type=MESH)

        sends = []
        for m in range(1, NDEV):
            px, py, pc = peer_of(m)
            cp = copy(m, 4 * px + 2 * py + pc, me)
            cp.start()
            sends.append(cp)
        for m in range(1, NDEV):
            px, py, pc = peer_of(m)
            copy(m, me, 4 * px + 2 * py + pc).wait_recv()
        for cp in sends:
            cp.wait_send()
        mine.wait()

    return pl.pallas_call(
        body, name=name,
        out_shape=jax.ShapeDtypeStruct(parts.shape, parts.dtype),
        in_specs=[pl.BlockSpec(memory_space=pl.ANY)],
        out_specs=pl.BlockSpec(memory_space=pl.ANY),
        scratch_shapes=[pltpu.SemaphoreType.DMA((7,)), pltpu.SemaphoreType.DMA((7,)), pltpu.SemaphoreType.DMA],
    )(parts)


_HBM = pl.BlockSpec(memory_space=pltpu.HBM)
_SEM = pl.BlockSpec(memory_space=pltpu.SEMAPHORE)
_EFFECT = pltpu.SideEffectType.DATAFLOW_SIDE_EFFECTING


def _peer_of(m, place):
    mx, my, mc = place
    return ((1 - mx) if m & 4 else mx, (1 - my) if m & 2 else my, (1 - mc) if m & 1 else mc)


def _split_copies(src_ref, land_ref, send_sems, recv_sems, per_peer, incoming):
    place = _my_place()
    me = 4 * place[0] + 2 * place[1] + place[2]
    out = []
    for m in range(1, NDEV):
        px, py, pc = _peer_of(m, place)
        peer = 4 * px + 2 * py + pc
        src = (src_ref.at[me] if incoming else src_ref.at[peer]) if per_peer else src_ref
        out.append(pltpu.make_async_remote_copy(
            src_ref=src, dst_ref=land_ref.at[peer if incoming else me],
            send_sem=send_sems.at[m - 1], recv_sem=recv_sems.at[m - 1],
            device_id=(px, py, pc), device_id_type=MESH))
    return out


def _split_start(src, name, per_peer):
    slab = src.shape[1:] if per_peer else src.shape

    def body(src_ref, land_ref, send_sems, recv_sems, src_thru, land_thru, token):
        for cp in _split_copies(src_ref, land_ref, send_sems, recv_sems, per_peer, incoming=False):
            cp.start()
        token[...] = jnp.zeros_like(token)

    return pl.pallas_call(
        body, name=name,
        out_shape=(pltpu.SemaphoreType.DMA((NDEV - 1,)), pltpu.SemaphoreType.DMA((NDEV - 1,)),
                   pltpu.HBM(src.shape, src.dtype), pltpu.HBM((NDEV,) + slab, src.dtype),
                   jax.ShapeDtypeStruct((8, TB), F32)),
        in_specs=(_HBM, _HBM), out_specs=(_SEM, _SEM, _HBM, _HBM, pl.BlockSpec(memory_space=pltpu.VMEM)),
        input_output_aliases={0: 2, 1: 3},
        compiler_params=pltpu.CompilerParams(has_side_effects=_EFFECT),
    )(pltpu.with_memory_space_constraint(src, pltpu.HBM),
      pltpu.with_memory_space_constraint(lax.empty((NDEV,) + slab, src.dtype), pltpu.HBM))


def _split_wait(handles, after, name, per_peer):
    send_sems, recv_sems, src_thru, land_thru, _ = handles

    def body(src_ref, land_ref, send_sems, recv_sems, after_ref, src_out, land_out):
        for cp in _split_copies(src_ref, land_ref, send_sems, recv_sems, per_peer, incoming=False):
            cp.wait_send()
        for cp in _split_copies(src_ref, land_ref, send_sems, recv_sems, per_peer, incoming=True):
            cp.wait_recv()

    return pl.pallas_call(
        body, name=name,
        out_shape=(pltpu.HBM(src_thru.shape, src_thru.dtype), pltpu.HBM(land_thru.shape, land_thru.dtype)),
        in_specs=(_HBM, _HBM, _SEM, _SEM, pl.BlockSpec(memory_space=pl.ANY)), out_specs=(_HBM, _HBM),
        input_output_aliases={0: 0, 1: 1},
        compiler_params=pltpu.CompilerParams(has_side_effects=_EFFECT),
    )(src_thru, land_thru, send_sems, recv_sems, after)


def _pick_slab(j, own_ref, land_ref, rows, per_peer=True):
    mx, my, mc = _my_place()
    me = 4 * mx + 2 * my + mc
    own = (lambda: own_ref[j, rows, :]) if per_peer else (lambda: own_ref[rows, :])
    return lax.cond(me == j, own, lambda: land_ref[j, rows, :])


def _h_block(t, x_ref, meta_ref):
    first = jnp.concatenate([jnp.zeros((P0, D), F32), meta_ref[...]], axis=0)
    return jnp.where(t == 0, first, x_ref[...])


def _x_spec():
    return pl.BlockSpec((TB, D), lambda t: (jnp.maximum(t - 1, 0), 0))


def _x_specs3():
    return [pl.BlockSpec((TB, D), lambda j: (jnp.maximum(3 * j - 1, 0), 0)),
            pl.BlockSpec((TB, D), lambda j: (3 * j, 0)),
            pl.BlockSpec((TB, D), lambda j: (3 * j + 1, 0))]


def _h_tile(j, xa_ref, xb_ref, xc_ref, meta_ref):
    first = jnp.concatenate([jnp.zeros((P0, D), F32), meta_ref[...]], axis=0)
    return jnp.concatenate([jnp.where(j == 0, first, xa_ref[...]), xb_ref[...], xc_ref[...]], axis=0)


def _full_spec(shape):
    return pl.BlockSpec(shape, lambda *_: (0,) * len(shape))


def _sigmoid(z):
    return 1.0 / (1.0 + jnp.exp(-z))


def _grouped(x):
    return x.reshape(H, DH, x.shape[-1])


def _group_rstd(x3):
    return lax.rsqrt(jnp.mean(x3 * x3, axis=1, keepdims=True) + EPS)


def _lane_tiles_sum(x):
    out = x[:, :TB]
    for i in range(1, x.shape[1] // TB):
        out = out + x[:, i * TB:(i + 1) * TB]
    return out


def _inproj_fwd(x, meta_full, norm_g, w_t, L):
    nj = L // TT

    def body(xa_ref, xb_ref, xc_ref, meta_ref, g_ref, w_ref, u_ref, proj_ref, f_ref, ktok_ref, vtok_ref):
        hb = _h_tile(pl.program_id(0), xa_ref, xb_ref, xc_ref, meta_ref)
        r = lax.rsqrt(jnp.mean(hb * hb, axis=-1, keepdims=True) + EPS)
        u = (hb * r * g_ref[...]).astype(BF16)
        u_ref[...] = u
        for s in range(NSEC):
            p = _dot(w_ref[s * DA:(s + 1) * DA, :], u, NT_DIMS)
            if s == 0:
                p = p * QSCALE
            proj_ref[s * DA:(s + 1) * DA, :] = p.astype(BF16)
        f_ref[...] = _dot(w_ref[NSEC * DA:DPROJ, :], u, NT_DIMS)[:H]
        k_tm = _dot(u, w_ref[DA:2 * DA, :], NT_DIMS)
        v_tm = _dot(u, w_ref[2 * DA:3 * DA, :], NT_DIMS)
        for h in range(H):
            ktok_ref[h] = k_tm[:, h * DH:(h + 1) * DH].astype(BF16)
            vtok_ref[h] = v_tm[:, h * DH:(h + 1) * DH].astype(BF16)

    return pl.pallas_call(
        body, name="inproj_fwd", grid=(nj,),
        in_specs=_x_specs3() + [_full_spec((NM, D)), _full_spec((1, D)), _full_spec((DPROJ, D))],
        out_specs=[
            pl.BlockSpec((TT, D), lambda t: (t, 0)),
            pl.BlockSpec((NSEC * DA, TT), lambda t: (0, t)),
            pl.BlockSpec((H, TT), lambda t: (0, t)),
            pl.BlockSpec((H, TT, DH), lambda t: (0, t, 0)),
            pl.BlockSpec((H, TT, DH), lambda t: (0, t, 0)),
        ],
        out_shape=[
            jax.ShapeDtypeStruct((L, D), BF16),
            jax.ShapeDtypeStruct((NSEC * DA, L), BF16),
            jax.ShapeDtypeStruct((H, L), F32),
            jax.ShapeDtypeStruct((H, L, DH), BF16),
            jax.ShapeDtypeStruct((H, L, DH), BF16),
        ],
        compiler_params=_params(),
    )(x, x, x, meta_full, norm_g, w_t)


def _split3(x):
    hi = x.astype(BF16).astype(F32)
    r = x - hi
    mid = r.astype(BF16).astype(F32)
    return hi, mid, (r - mid).astype(BF16).astype(F32)


def _bias_rows(bias):
    one = jnp.ones((1, TT), F32)
    zero = jnp.zeros((1, TT), F32)
    parts = [zero] * 3 if bias is None else list(_split3(bias))
    return jnp.concatenate([one] * 3 + parts + [zero] * (DF - 6), axis=0).astype(BF16)


def _fgate_fwd(f_t, b_col, ktok, L):
    nb = L // TB

    def body(f_ref, b_ref, ktok_ref, cq_ref, kaug_ref, sg_ref):
        z = f_ref[...] + b_ref[...]
        idx = lax.broadcasted_iota(jnp.int32, (H, L), 1)
        real = idx >= P0
        lf = jnp.where(real, jnp.minimum(z, 0.0) - jnp.log1p(jnp.exp(-jnp.abs(z))), 0.0)
        sg_ref[...] = jnp.where(real, 1.0 / (1.0 + jnp.exp(z)), 0.0)
        c = lf
        s = 1
        while s < L:
            c = c + jnp.where(idx >= s, pltpu.roll(c, s, 1), 0.0)
            s *= 2
        c = c * LOG2E
        for h in range(H):
            cq_ref[h] = c[h:h + 1, :]
        ck = jnp.where(real, c, -NEG)
        lane = lax.broadcasted_iota(jnp.int32, (TB, KA), 1)
        tail = jnp.where((lane >= DH + 3) & (lane < DH + 6), 1.0, 0.0)
        for h in range(H):
            for b in range(nb):
                blk = slice(b * TB, (b + 1) * TB)
                col = jnp.broadcast_to(ck[h:h + 1, blk], (TB, TB)).T
                hi, mid, lo = _split3(-col)
                k = jnp.concatenate([ktok_ref[h, blk, :].astype(F32), jnp.zeros((TB, KA - DH), F32)], axis=1)
                out = jnp.where(lane < DH, k, jnp.where(lane == DH, hi, jnp.where(
                    lane == DH + 1, mid, jnp.where(lane == DH + 2, lo, tail))))
                kaug_ref[h, blk, :] = out.astype(BF16)

    return pl.pallas_call(
        body, name="fgate_fwd",
        out_shape=[
            jax.ShapeDtypeStruct((H, 1, L), F32),
            jax.ShapeDtypeStruct((H, L, KA), BF16),
            jax.ShapeDtypeStruct((H, L), F32),
        ],
        compiler_params=pltpu.CompilerParams(vmem_limit_bytes=VMEM_LIMIT),
    )(f_t, b_col, ktok)


def _causal_mask():
    r = lax.broadcasted_iota(jnp.int32, (TT, TT), 0)
    c = lax.broadcasted_iota(jnp.int32, (TT, TT), 1)
    return r <= c


def _attn_fwd(proj_t, kaug, cq, L):
    nq = L // TT

    def body(q_ref, kaug_ref, v_ref, cq_ref, o_ref, lse_ref, qa_scr, s_scr, cmax_scr, m_scr, acc_scr):
        j = pl.program_id(1)
        rows = [slice(g * DH, (g + 1) * DH) for g in range(HG)]
        ones = jnp.ones((DF, TT), BF16)
        for g in range(HG):
            qa_scr[g] = jnp.concatenate(
                [q_ref[rows[g], :], _bias_rows(None), jnp.zeros((KA - DH - DF, TT), BF16)], axis=0)

        def scores(kt, masked):
            k_off = pl.multiple_of(kt * TT, TT)
            for g in range(HG):
                s = _dot(kaug_ref[g, pl.ds(k_off, TT), :], qa_scr[g])
                if masked:
                    s = jnp.where(_causal_mask(), s, NEG)
                s_scr[g] = s
                cmax_scr[g] = jnp.max(s, axis=0, keepdims=True)

        def softmax_pv(kt):
            k_off = pl.multiple_of(kt * TT, TT)
            for g in range(HG):
                m_old = m_scr[g]
                m_new = jnp.maximum(m_old, cmax_scr[g])
                alpha = jnp.exp2(m_old - m_new)
                p = jnp.exp2(s_scr[g] - m_new).astype(BF16)
                v1 = jnp.concatenate([v_ref[rows[g], pl.ds(k_off, TT)], ones], axis=0)
                acc_scr[g] = alpha * acc_scr[g] + _dot(v1, p)
                m_scr[g] = m_new

        m_scr[...] = jnp.full_like(m_scr, NEG)
        acc_scr[...] = jnp.zeros_like(acc_scr)

        scores(j, True)

        def step(t, c):
            softmax_pv(j - t)
            scores(j - 1 - t, False)
            return c

        lax.fori_loop(0, j, step, 0)
        softmax_pv(0)
        for g in range(HG):
            l = acc_scr[g, DH:DH + 1, :]
            o_ref[rows[g], :] = acc_scr[g, :DH, :] * (1.0 / l)
            lse_ref[g] = m_scr[g] + jnp.log2(l) + cq_ref[g]

    return pl.pallas_call(
        body, name="attn_fwd", grid=(H // HG, nq),
        in_specs=[
            pl.BlockSpec((HG * DH, TT), lambda h, j: (h, j)),
            pl.BlockSpec((HG, L, KA), lambda h, j: (h, 0, 0)),
            pl.BlockSpec((HG * DH, L), lambda h, j: (2 * H // HG + h, 0)),
            pl.BlockSpec((HG, 1, TT), lambda h, j: (h, 0, j)),
        ],
        out_specs=[
            pl.BlockSpec((HG * DH, TT), lambda h, j: (h, j)),
            pl.BlockSpec((HG, 1, TT), lambda h, j: (h, 0, j)),
        ],
        out_shape=[jax.ShapeDtypeStruct((DA, L), F32), jax.ShapeDtypeStruct((H, 1, L), F32)],
        scratch_shapes=[pltpu.VMEM((HG, KA, TT), BF16), pltpu.VMEM((HG, TT, TT), F32), pltpu.VMEM((HG, 1, TT), F32),
                        pltpu.VMEM((HG, 1, TT), F32), pltpu.VMEM((HG, DH + DF, TT), F32)],
        compiler_params=_params(2),
    )(proj_t, kaug, proj_t, cq)


def _gate_common(o, za, gb, gc, xc, zc, gcp, xcp, cw_ref, ga_ref, gcn_ref, first):
    n_rep = TT // TB
    a = gc * xc
    a_prev = jnp.where(first, 0.0, gcp * xcp)
    full = jnp.concatenate([a_prev, a], axis=1)
    a1 = pltpu.roll(full, 1, 1)[:, TB:]
    a2 = pltpu.roll(full, 2, 1)[:, TB:]
    w0 = jnp.tile(cw_ref[0], (1, n_rep))
    w1 = jnp.tile(cw_ref[1], (1, n_rep))
    w2 = jnp.tile(cw_ref[2], (1, n_rep))
    cv = w0 * a2 + w1 * a1 + w2 * a
    e = gb * cv
    e3 = _grouped(e)
    rc = _group_rstd(e3)
    ec = (e3 * rc).reshape(DA, TT)
    o3 = _grouped(o)
    ra = _group_rstd(o3)
    oa = (o3 * ra).reshape(DA, TT)
    g_a = jnp.tile(ga_ref[...], (1, n_rep))
    g_c = jnp.tile(gcn_ref[...], (1, n_rep))
    sa = _sigmoid(za)
    sc = _sigmoid(zc)
    return dict(a=a, a1=a1, a2=a2, w0=w0, w1=w1, w2=w2, cv=cv, e=e, rc=rc, ec=ec, ra=ra, oa=oa,
                g_a=g_a, g_c=g_c, sa=sa, sc=sc)


def _gate_specs(nj, rev):
    def jj(i):
        return (nj - 1 - i) if rev else i

    def sec(s):
        return pl.BlockSpec((DA, TT), lambda i: (s, jj(i)))

    def halo(s):
        return pl.BlockSpec((DA, TB), lambda i: (s, jnp.maximum(3 * jj(i) - 1, 0)))

    return [pl.BlockSpec((DA, TT), lambda i: (0, jj(i))), sec(3), sec(4), sec(5), sec(6), sec(7), halo(5), halo(6),
            _full_spec((3, DA, TB)), _full_spec((DA, TB)), _full_spec((DA, TB))]


def _gate_fwd(o_t, proj_t, cw_b, ga_b, gcn_b, L):
    nj = L // TT

    def body(o_ref, za_ref, gb_ref, gc_ref, xc_ref, zc_ref, gcp_ref, xcp_ref, cw_ref, ga_ref, gcn_ref, mix_ref):
        j = pl.program_id(0)
        f32 = lambda r: r[...].astype(F32)
        za, zc = f32(za_ref), f32(zc_ref)
        g = _gate_common(o_ref[...], za, f32(gb_ref), f32(gc_ref), f32(xc_ref), zc, f32(gcp_ref), f32(xcp_ref),
                         cw_ref, ga_ref, gcn_ref, j == 0)
        mix_ref[:DA, :] = (g["oa"] * g["g_a"] * (za * g["sa"])).astype(BF16)
        mix_ref[DA:, :] = (g["ec"] * g["g_c"] * (zc * g["sc"])).astype(BF16)

    return pl.pallas_call(
        body, name="gate_fwd", grid=(nj,),
        in_specs=_gate_specs(nj, False),
        out_specs=pl.BlockSpec((2 * DA, TT), lambda j: (0, j)),
        out_shape=jax.ShapeDtypeStruct((2 * DA, L), BF16),
        compiler_params=_params(),
    )(o_t, proj_t, proj_t, proj_t, proj_t, proj_t, proj_t, proj_t, cw_b, ga_b, gcn_b)


def _outproj(mix_t, w_out, x, meta_full, fng, target, L):
    nj = L // TT

    def body(mix_ref, w_ref, xa_ref, xb_ref, xc_ref, meta_ref, g_ref, ta_ref, tb_ref, tc_ref,
             dout_ref, dmix_ref, dwb_ref, loss_ref, dg_ref, dw_ref):
        t = pl.program_id(0)

        @pl.when(t == 0)
        def _():
            dw_ref[...] = jnp.zeros_like(dw_ref)
            loss_ref[...] = jnp.zeros_like(loss_ref)
            dg_ref[...] = jnp.zeros_like(dg_ref)

        mix = mix_ref[...]
        o = _dot(mix, w_ref[...], TN_DIMS) + _h_tile(t, xa_ref, xb_ref, xc_ref, meta_ref)
        r = lax.rsqrt(jnp.mean(o * o, axis=-1, keepdims=True) + EPS)
        g = g_ref[...]
        orn = o * r
        tgt = jnp.concatenate([ta_ref[...], tb_ref[...], tc_ref[...]], axis=0)
        row = lax.broadcasted_iota(jnp.int32, (TT, 1), 0)
        real = jnp.where((t > 0) | (row >= TB), 1.0, 0.0)
        diff = (orn * g - tgt) * real
        loss_ref[...] += 0.5 * jnp.sum(diff * diff) * (1.0 / D)
        dy = diff * (1.0 / D)
        dg_ref[...] += jnp.sum(dy * orn, axis=0, keepdims=True)
        gy = dy * g
        dout = r * gy - orn * (r * jnp.mean(gy * orn, axis=-1, keepdims=True))
        dout_ref[...] = dout
        db = dout.astype(BF16)
        dmix_ref[...] = _dot(w_ref[...], db, NT_DIMS).astype(BF16)
        dw_ref[...] += _dot(mix, db)

        @pl.when(t == nj - 1)
        def _():
            dwb_ref[...] = dw_ref[...].astype(BF16)

    return pl.pallas_call(
        body, name="outproj", grid=(nj,),
        in_specs=[pl.BlockSpec((D, TT), lambda t: (0, t)), _full_spec((D, D))] + _x_specs3()
                 + [_full_spec((NM, D)), _full_spec((1, D))] + _x_specs3(),
        out_specs=[pl.BlockSpec((TT, D), lambda t: (t, 0)), pl.BlockSpec((D, TT), lambda t: (0, t)),
                   _full_spec((D, D)), _full_spec((1, 1)), _full_spec((1, D))],
        out_shape=[jax.ShapeDtypeStruct((L, D), F32), jax.ShapeDtypeStruct((D, L), BF16),
                   jax.ShapeDtypeStruct((D, D), BF16), jax.ShapeDtypeStruct((1, 1), F32),
                   jax.ShapeDtypeStruct((1, D), F32)],
        scratch_shapes=[pltpu.VMEM((D, D), F32)],
        compiler_params=_params(),
    )(mix_t, w_out, x, x, x, meta_full, fng, target, target, target)


def _gate_bwd(dmix_t, o_t, proj_t, cw_b, ga_b, gcn_b, L):
    nj = L // TT

    def body(dmix_ref, o_ref, za_ref, gb_ref, gc_ref, xc_ref, zc_ref, gcp_ref, xcp_ref, cw_ref, ga_ref, gcn_ref,
             do_ref, dd_ref, dg5_ref, dga_ref, dgc_ref, dcw_ref, carry_ref):
        i = pl.program_id(0)
        j = nj - 1 - i

        @pl.when(i == 0)
        def _():
            carry_ref[...] = jnp.zeros_like(carry_ref)
            dga_ref[...] = jnp.zeros_like(dga_ref)
            dgc_ref[...] = jnp.zeros_like(dgc_ref)
            dcw_ref[...] = jnp.zeros_like(dcw_ref)

        f32 = lambda r: r[...].astype(F32)
        o, za, gb, gc, xc, zc = o_ref[...], f32(za_ref), f32(gb_ref), f32(gc_ref), f32(xc_ref), f32(zc_ref)
        g = _gate_common(o, za, gb, gc, xc, zc, f32(gcp_ref), f32(xcp_ref), cw_ref, ga_ref, gcn_ref, j == 0)
        dya = dmix_ref[:DA, :].astype(F32)
        dyc = dmix_ref[DA:, :].astype(F32)
        sa, sc = g["sa"], g["sc"]

        dn = dya * (za * sa)
        dg5_ref[0:DA, :] = (dya * (g["oa"] * g["g_a"]) * (sa * (1.0 + za * (1.0 - sa)))).astype(BF16)
        dga_ref[...] += _lane_tiles_sum(dn * g["oa"])
        dng = dn * g["g_a"]
        mean_a = jnp.mean(_grouped(dng * g["oa"]), axis=1, keepdims=True)
        do = ((_grouped(dng) - _grouped(g["oa"]) * mean_a) * g["ra"]).reshape(DA, TT)
        do_ref[...] = do.astype(BF16)
        dd = jnp.sum(_grouped(do * o), axis=1)
        for h in range(H):
            dd_ref[h] = dd[h:h + 1, :]

        dnc = dyc * (zc * sc)
        dg5_ref[4 * DA:5 * DA, :] = (dyc * (g["ec"] * g["g_c"]) * (sc * (1.0 + zc * (1.0 - sc)))).astype(BF16)
        dgc_ref[...] += _lane_tiles_sum(dnc * g["ec"])
        dncg = dnc * g["g_c"]
        mean_c = jnp.mean(_grouped(dncg * g["ec"]), axis=1, keepdims=True)
        de = ((_grouped(dncg) - _grouped(g["ec"]) * mean_c) * g["rc"]).reshape(DA, TT)
        dg5_ref[DA:2 * DA, :] = (de * g["cv"]).astype(BF16)
        dcv = de * gb
        full = jnp.concatenate([dcv, carry_ref[...]], axis=1)
        d1 = pltpu.roll(full, TT + TB - 1, 1)[:, :TT]
        d2 = pltpu.roll(full, TT + TB - 2, 1)[:, :TT]
        carry_ref[...] = dcv[:, :TB]
        da = g["w2"] * dcv + g["w1"] * d1 + g["w0"] * d2
        dg5_ref[2 * DA:3 * DA, :] = (da * xc).astype(BF16)
        dg5_ref[3 * DA:4 * DA, :] = (da * gc).astype(BF16)
        dcw_ref[0] += _lane_tiles_sum(dcv * g["a2"])
        dcw_ref[1] += _lane_tiles_sum(dcv * g["a1"])
        dcw_ref[2] += _lane_tiles_sum(dcv * g["a"])

    rj = lambda i: nj - 1 - i
    return pl.pallas_call(
        body, name="gate_bwd", grid=(nj,),
        in_specs=[pl.BlockSpec((2 * DA, TT), lambda i: (0, rj(i)))] + _gate_specs(nj, True),
        out_specs=[
            pl.BlockSpec((DA, TT), lambda i: (0, rj(i))),
            pl.BlockSpec((H, 1, TT), lambda i: (0, 0, rj(i))),
            pl.BlockSpec((5 * DA, TT), lambda i: (0, rj(i))),
            _full_spec((DA, TB)), _full_spec((DA, TB)), _full_spec((3, DA, TB)),
        ],
        out_shape=[
            jax.ShapeDtypeStruct((DA, L), BF16),
            jax.ShapeDtypeStruct((H, 1, L), F32),
            jax.ShapeDtypeStruct((5 * DA, L), BF16),
            jax.ShapeDtypeStruct((DA, TB), F32),
            jax.ShapeDtypeStruct((DA, TB), F32),
            jax.ShapeDtypeStruct((3, DA, TB), F32),
        ],
        scratch_shapes=[pltpu.VMEM((DA, TB), F32)],
        compiler_params=_params(),
    )(dmix_t, o_t, proj_t, proj_t, proj_t, proj_t, proj_t, proj_t, proj_t, cw_b, ga_b, gcn_b)


def _attn_bwd(proj_t, kaug, vtok, do_t, lse, dd, cq, L):
    nk = L // TT

    def body(q_ref, kaug_ref, vtok_ref, kt_ref, do_ref, lse_ref, dd_ref, cq_ref,
             dq_ref, dk_ref, dv_ref, dck_ref, dcq_ref, dq_acc, q1_scr, kt1_scr, s_scr, dp_scr, dv_scr, dk_scr):
        i = pl.program_id(1)

        @pl.when(i == 0)
        def _():
            dq_acc[...] = jnp.zeros_like(dq_acc)

        rows = [slice(g * DH, (g + 1) * DH) for g in range(HG)]
        ones = jnp.ones((DF, TT), BF16)
        zpad = jnp.zeros((KA - DH - DF, TT), BF16)
        for g in range(HG):
            kt1_scr[g] = jnp.concatenate([kt_ref[rows[g], :], ones], axis=0)
        dv_scr[...] = jnp.zeros_like(dv_scr)
        dk_scr[...] = jnp.zeros_like(dk_scr)

        def scores(jq, masked):
            q_off = pl.multiple_of(jq * TT, TT)
            for g in range(HG):
                bias = cq_ref[g, :, pl.ds(q_off, TT)] - lse_ref[g, :, pl.ds(q_off, TT)]
                q1 = jnp.concatenate([q_ref[rows[g], pl.ds(q_off, TT)], _bias_rows(bias)], axis=0)
                q1_scr[g] = q1
                s = _dot(kaug_ref[g], jnp.concatenate([q1, zpad], axis=0))
                if masked:
                    s = jnp.where(_causal_mask(), s, NEG)
                s_scr[g] = s
                dp_scr[g] = _dot(vtok_ref[g], do_ref[rows[g], pl.ds(q_off, TT)])

        def grads(jq):
            q_off = pl.multiple_of(jq * TT, TT)
            for g in range(HG):
                p = jnp.exp2(s_scr[g])
                ds = (p * (dp_scr[g] - dd_ref[g, :, pl.ds(q_off, TT)])).astype(BF16)
                dv_scr[g] += _dot(do_ref[rows[g], pl.ds(q_off, TT)], p.astype(BF16), NT_DIMS)
                dk_scr[g] += _dot(q1_scr[g], ds, NT_DIMS)
                dq_acc[g, :, pl.ds(q_off, TT)] += _dot(kt1_scr[g], ds)

        scores(i, True)

        def step(jq, c):
            grads(jq)
            scores(jq + 1, False)
            return c

        lax.fori_loop(i, nk - 1, step, 0)
        grads(nk - 1)
        for g in range(HG):
            dv_ref[rows[g], :] = dv_scr[g].astype(BF16)
            dk_ref[rows[g], :] = (dk_scr[g, :DH, :] * LN2).astype(BF16)
            dck_ref[g] = dk_scr[g, DH:DH + 1, :]

        @pl.when(i == nk - 1)
        def _():
            for g in range(HG):
                dq_ref[rows[g], :] = (dq_acc[g, :DH, :] * (DH ** -0.5)).astype(BF16)
                dcq_ref[g] = dq_acc[g, DH:DH + 1, :]

    head = lambda h, i: (h, 0)
    row = lambda h, i: (h, 0, 0)
    return pl.pallas_call(
        body, name="attn_bwd", grid=(H // HG, nk),
        in_specs=[
            pl.BlockSpec((HG * DH, L), head),
            pl.BlockSpec((HG, TT, KA), lambda h, i: (h, i, 0)),
            pl.BlockSpec((HG, TT, DH), lambda h, i: (h, i, 0)),
            pl.BlockSpec((HG * DH, TT), lambda h, i: (H // HG + h, i)),
            pl.BlockSpec((HG * DH, L), head),
            pl.BlockSpec((HG, 1, L), row), pl.BlockSpec((HG, 1, L), row), pl.BlockSpec((HG, 1, L), row),
        ],
        out_specs=[
            pl.BlockSpec((HG * DH, L), head),
            pl.BlockSpec((HG * DH, TT), lambda h, i: (h, i)),
            pl.BlockSpec((HG * DH, TT), lambda h, i: (h, i)),
            pl.BlockSpec((HG, 1, TT), lambda h, i: (h, 0, i)),
            pl.BlockSpec((HG, 1, L), row),
        ],
        out_shape=[jax.ShapeDtypeStruct((DA, L), BF16), jax.ShapeDtypeStruct((DA, L), BF16),
                   jax.ShapeDtypeStruct((DA, L), BF16), jax.ShapeDtypeStruct((H, 1, L), F32),
                   jax.ShapeDtypeStruct((H, 1, L), F32)],
        scratch_shapes=[
            pltpu.VMEM((HG, DH + DF, L), F32),
            pltpu.VMEM((HG, DH + DF, TT), BF16), pltpu.VMEM((HG, DH + DF, TT), BF16),
            pltpu.VMEM((HG, TT, TT), F32), pltpu.VMEM((HG, TT, TT), F32),
            pltpu.VMEM((HG, DH, TT), F32), pltpu.VMEM((HG, DH + DF, TT), F32)],
        compiler_params=_params(2),
    )(proj_t, kaug, vtok, proj_t, do_t, lse, dd, cq)


def _fgate_bwd(dcq, dck, sg, L):
    def body(dcq_ref, dck_ref, sg_ref, df_ref, db_ref):
        dc = jnp.concatenate([dcq_ref[h] - dck_ref[h] for h in range(H)], axis=0)
        idx = lax.broadcasted_iota(jnp.int32, (H, L), 1)
        r = dc
        s = 1
        while s < L:
            r = r + jnp.where(idx + s < L, pltpu.roll(r, L - s, 1), 0.0)
            s *= 2
        df = r * sg_ref[...]
        db_ref[...] = jnp.sum(df, axis=1, keepdims=True)
        df_ref[...] = jnp.concatenate([df, jnp.zeros((DF - H, L), F32)], axis=0).astype(BF16)

    return pl.pallas_call(
        body, name="fgate_bwd",
        out_shape=[jax.ShapeDtypeStruct((DF, L), BF16), jax.ShapeDtypeStruct((H, 1), F32)],
        compiler_params=pltpu.CompilerParams(vmem_limit_bytes=VMEM_LIMIT),
    )(dcq, dck, sg)


def _inproj_bwd_x(w, dq_t, dk_t, dv_t, dg5_t, df_t, dout, x, meta_full, norm_g, L):
    nb = L // TB
    seq = x.shape[0]

    def body(w_ref, dq_ref, dk_ref, dv_ref, dg5_ref, df_ref, dout_ref, x_ref, meta_ref, g_ref,
             gx_ref, dmeta_ref, dg_ref):
        t = pl.program_id(0)

        @pl.when(t == 0)
        def _():
            dg_ref[...] = jnp.zeros_like(dg_ref)

        du = _dot(dq_ref[...], w_ref[0:DA, :], TN_DIMS)
        du += _dot(dk_ref[...], w_ref[DA:2 * DA, :], TN_DIMS)
        du += _dot(dv_ref[...], w_ref[2 * DA:3 * DA, :], TN_DIMS)
        du += _dot(dg5_ref[...], w_ref[3 * DA:NSEC * DA, :], TN_DIMS)
        du += _dot(df_ref[...], w_ref[NSEC * DA:DPROJ, :], TN_DIMS)
        hb = _h_block(t, x_ref, meta_ref)
        r = lax.rsqrt(jnp.mean(hb * hb, axis=-1, keepdims=True) + EPS)
        hn = hb * r
        dg_ref[...] += jnp.sum(du * hn, axis=0, keepdims=True)
        gu = du * g_ref[...]
        dh = dout_ref[...] + r * gu - hn * (r * jnp.mean(gu * hn, axis=-1, keepdims=True))
        gx_ref[...] = dh

        @pl.when(t == 0)
        def _():
            dmeta_ref[...] = dh[P0:, :]

    blk = lambda rows: pl.BlockSpec((rows, TB), lambda t: (0, t))
    return pl.pallas_call(
        body, name="inproj_bwd_x", grid=(nb,),
        in_specs=[_full_spec((DPROJ, D)), blk(DA), blk(DA), blk(DA), blk(5 * DA), blk(DF),
                  pl.BlockSpec((TB, D), lambda t: (t, 0)), _x_spec(), _full_spec((NM, D)), _full_spec((1, D))],
        out_specs=[_x_spec(), _full_spec((NM, D)), _full_spec((1, D))],
        out_shape=[jax.ShapeDtypeStruct((seq, D), F32), jax.ShapeDtypeStruct((NM, D), F32),
                   jax.ShapeDtypeStruct((1, D), F32)],
        compiler_params=_params(),
    )(w, dq_t, dk_t, dv_t, dg5_t, df_t, dout, x, meta_full, norm_g)


def _inproj_bwd_w(u, dq_t, dk_t, dv_t, dg5_t, df_t, L):
    kt = L // 3
    nkt = 3

    def body(u_ref, dq_ref, dk_ref, dv_ref, dg5_ref, df_ref, dw_ref, dwf_ref):
        s = pl.program_id(0)
        k = pl.program_id(1)

        @pl.when(k == 0)
        def _():
            dw_ref[...] = jnp.zeros_like(dw_ref)

        @pl.when((s == 0) & (k == 0))
        def _():
            dwf_ref[...] = jnp.zeros_like(dwf_ref)

        u_blk = u_ref[...]
        for sec, ref in ((0, dq_ref), (1, dk_ref), (2, dv_ref)):
            @pl.when(s == sec)
            def _(ref=ref):
                dw_ref[...] += _dot(ref[...], u_blk)

        @pl.when(s >= 3)
        def _():
            dw_ref[...] += _dot(dg5_ref[...], u_blk)

        @pl.when(s == NSEC - 1)
        def _():
            dwf_ref[...] += _dot(df_ref[...], u_blk)

    def only(sec):
        return lambda s, k: (0, jnp.where(s == sec, k, 0))

    return pl.pallas_call(
        body, name="inproj_bwd_w", grid=(NSEC, nkt),
        in_specs=[
            pl.BlockSpec((kt, D), lambda s, k: (k, 0)),
            pl.BlockSpec((DA, kt), only(0)), pl.BlockSpec((DA, kt), only(1)), pl.BlockSpec((DA, kt), only(2)),
            pl.BlockSpec((DA, kt), lambda s, k: (jnp.maximum(s - 3, 0), jnp.where(s >= 3, k, 0))),
            pl.BlockSpec((DF, kt), only(NSEC - 1)),
        ],
        out_specs=[pl.BlockSpec((DA, D), lambda s, k: (s, 0)), _full_spec((DF, D))],
        out_shape=[jax.ShapeDtypeStruct((NSEC * DA, D), F32), jax.ShapeDtypeStruct((DF, D), F32)],
        compiler_params=_params(2),
    )(u, dq_t, dk_t, dv_t, dg5_t, df_t)


def _adamw(w, g, m, v):
    m = ADAM_B1 * m + (1.0 - ADAM_B1) * g
    v = ADAM_B2 * v + (1.0 - ADAM_B2) * (g * g)
    m_hat = m / (1.0 - ADAM_B1 ** ADAM_STEP)
    v_hat = v / (1.0 - ADAM_B2 ** ADAM_STEP)
    delta = -ADAM_LR * (m_hat / (jnp.sqrt(v_hat) + ADAM_EPS) + ADAM_WD * w)
    return delta, m, v


def _sum_slabs(ref, rows):
    g = ref[0, rows, :].astype(F32)
    for j in range(1, NDEV):
        g = g + ref[j, rows, :].astype(F32)
    return g


def _adamw_big(own_in, land_in, own_out, land_out, w_in_t, m_in_t, v_in_t, w_out, m_out, v_out):
    cb = CB
    e_sh = D // NDEV
    in_shape = jax.ShapeDtypeStruct(w_in_t.shape, F32)
    out_shape = jax.ShapeDtypeStruct(w_out.shape, F32)

    def total(own_ref, land_ref, rows):
        g = _pick_slab(0, own_ref, land_ref, rows).astype(F32)
        for j in range(1, NDEV):
            g = g + _pick_slab(j, own_ref, land_ref, rows).astype(F32)
        return g

    def body(oi_ref, li_ref, oo_ref, lo_ref, wi_ref, mi_ref, vi_ref, wo_ref, mo_ref, vo_ref,
             gi, di, mi, vi, go, do, mo, vo):
        g = total(oi_ref, li_ref, slice(0, WSHP))[:WSH]
        d, mn, vn = _adamw(wi_ref[...], g, mi_ref[...], vi_ref[...])
        gi[...], di[...], mi[...], vi[...] = g, d, mn, vn
        g = total(oo_ref, lo_ref, slice(0, e_sh))
        d, mn, vn = _adamw(wo_ref[0], g, mo_ref[0], vo_ref[0])
        go[0], do[0], mo[0], vo[0] = g, d, mn, vn

    slab = lambda rows: pl.BlockSpec((NDEV, rows, cb), lambda i: (0, 0, i))
    ispec = pl.BlockSpec((WSH, cb), lambda i: (0, i))
    ospec = pl.BlockSpec((1, e_sh, cb), lambda i: (0, 0, i))
    return pl.pallas_call(
        body, name="adamw_big", grid=(D // cb,),
        in_specs=[slab(WSHP), slab(WSHP), slab(e_sh), slab(e_sh), ispec, ispec, ispec, ospec, ospec, ospec],
        out_specs=[ispec] * 4 + [ospec] * 4, out_shape=[in_shape] * 4 + [out_shape] * 4,
        compiler_params=_params(),
    )(own_in, land_in, own_out, land_out, w_in_t, m_in_t, v_in_t, w_out, m_out, v_out)


F0 = 3 * DA


def _unshard_w_out(own, land):
    e_sh = D // NDEV

    def body(own_ref, land_ref, wo_ref):
        for j in range(NDEV):
            wo_ref[j * e_sh:(j + 1) * e_sh, :] = _pick_slab(j, own_ref, land_ref, slice(0, e_sh), per_peer=False)

    return pl.pallas_call(
        body, name="unshard_w_out", grid=(D // CB,),
        in_specs=[pl.BlockSpec((e_sh, CB), lambda i: (0, i)), pl.BlockSpec((NDEV, e_sh, CB), lambda i: (0, 0, i))],
        out_specs=pl.BlockSpec((D, CB), lambda i: (0, i)),
        out_shape=jax.ShapeDtypeStruct((D, D), BF16),
        compiler_params=_params(),
    )(own, land)


def _unshard_w_in(w_all):
    def body(w_ref, wt_ref):
        def ref_rows(lo, hi):
            pieces, r = [], lo
            while r < hi:
                sh, off = divmod(r, WSH)
                n = min(hi - r, WSH - off)
                pieces.append(w_ref[sh, off:off + n, :])
                r += n
            return pieces

        for s in range(NSEC):
            lo = s * DA if s < 3 else s * DA + H
            wt_ref[s * DA:(s + 1) * DA, :] = jnp.concatenate(ref_rows(lo, lo + DA), axis=0)
        wt_ref[NSEC * DA:DPROJ, :] = jnp.concatenate(
            ref_rows(F0, F0 + H) + [jnp.zeros((DF - H, CB), BF16)], axis=0)

    return pl.pallas_call(
        body, name="unshard_w_in", grid=(D // CB,),
        in_specs=[pl.BlockSpec((NDEV, WSHP, CB), lambda i: (0, 0, i))],
        out_specs=pl.BlockSpec((DPROJ, CB), lambda i: (0, i)),
        out_shape=jax.ShapeDtypeStruct((DPROJ, D), BF16),
        compiler_params=_params(),
    )(w_all)


def _shard_w_in_grads(dw_main, dw_f):
    def body(dm_ref, df_ref, p_ref):
        def ref_rows(lo, hi):
            pieces, r = [], lo
            while r < hi:
                if r < F0:
                    n = min(hi, F0) - r
                    pieces.append(dm_ref[r:r + n, :])
                elif r < F0 + H:
                    n = min(hi, F0 + H) - r
                    pieces.append(df_ref[r - F0:r - F0 + n, :])
                else:
                    n = hi - r
                    pieces.append(dm_ref[r - H:r - H + n, :])
                r += n
            return pieces

        for i in range(NDEV):
            rows = jnp.concatenate(ref_rows(i * WSH, (i + 1) * WSH) + [jnp.zeros((WSHP - WSH, CB), F32)], axis=0)
            p_ref[i] = rows.astype(BF16)

    col = lambda rows: pl.BlockSpec((rows, CB), lambda i: (0, i))
    return pl.pallas_call(
        body, name="shard_w_in_grads", grid=(D // CB,),
        in_specs=[col(NSEC * DA), col(DF)],
        out_specs=pl.BlockSpec((NDEV, WSHP, CB), lambda i: (0, 0, i)),
        out_shape=jax.ShapeDtypeStruct((NDEV, WSHP, D), BF16),
        compiler_params=_params(),
    )(dw_main, dw_f)


def _adamw_small(recv, w, m, v):
    shape = jax.ShapeDtypeStruct((SROWS, TB), F32)

    def body(r_ref, w_ref, m_ref, v_ref, g_out, d_out, m_out, v_out):
        g = _sum_slabs(r_ref, slice(0, SROWS))
        d, mn, vn = _adamw(w_ref[...], g, m_ref[...], v_ref[...])
        g_out[...], d_out[...], m_out[...], v_out[...] = g, d, mn, vn

    return pl.pallas_call(body, name="adamw_small", out_shape=[shape] * 4)(recv, w, m, v)


def _tile_rows(a, rows, lanes=TB):
    a = a.reshape(rows, lanes)
    return jnp.pad(a, ((0, -rows % 8), (0, TB - lanes)))


def _pack_small(norm_g, final_norm_g, attn_norm_g, conv_norm_g, b_f, meta_sh, conv_w_sh, loss=None):
    b_row = b_f.reshape(1, H) if loss is None else jnp.concatenate([b_f.reshape(1, H), loss.reshape(1, 1)], axis=1)
    packed = jnp.concatenate([
        _tile_rows(norm_g, 8), _tile_rows(final_norm_g, 8), _tile_rows(attn_norm_g, 4), _tile_rows(conv_norm_g, 4),
        _tile_rows(b_row, 1, b_row.shape[1]), _tile_rows(meta_sh, NM), _tile_rows(conv_w_sh, 3, DH)], axis=0)
    assert packed.shape == (SROWS, TB)
    return packed


def _unpack_small(p):
    return dict(
        norm_g=p[0:8].reshape(1, D), final_norm_g=p[8:16].reshape(D), attn_norm_g=p[16:20].reshape(1, DA),
        conv_norm_g=p[24:28].reshape(1, DA), b_f=p[32:33, :H].reshape(1, H), meta=p[40:56].reshape(NM, TB),
        conv_w=p[56:59, :DH].reshape(1, 3, DH))


def kernel(x, meta, norm_g, w_in, b_f, conv_w, attn_norm_g, conv_norm_g, w_out, final_norm_g, loss_target, m_meta, m_norm_g, m_w_in, m_b_f, m_conv_w, m_attn_norm_g, m_conv_norm_g, m_w_out, m_final_norm_g, v_meta, v_norm_g, v_w_in, v_b_f, v_conv_w, v_attn_norm_g, v_conv_norm_g, v_w_out, v_final_norm_g):
    seq = x.shape[1]
    L = seq + TB
    assert x.shape == (1, seq, D) and L % TT == 0 and w_in.shape == (1, D, WSH)
    x2 = x[0]
    tgt = loss_target[0]

    w_in_slab = jnp.pad(w_in[0].T, ((0, WSHP - WSH), (0, 0))).astype(BF16)
    w_out_slab = w_out[0].astype(BF16)
    small = jnp.concatenate([meta, _tile_rows(conv_w[0], 3, DH)], axis=0)
    wout_flight = _split_start(w_out_slab, "gather_w_out_start", per_peer=False)
    w_all = _all_gather(w_in_slab, "gather_w_in")
    small_all = _all_gather(small, "gather_small")

    w_t = _unshard_w_in(w_all)
    meta_full = jnp.transpose(small_all[:, :NM, :], (1, 0, 2)).reshape(NM, D)
    conv_w_full = jnp.transpose(small_all[:, NM:NM + 3, :DH], (1, 0, 2)).reshape(3, DA)

    lane_b = lambda p: jnp.broadcast_to(p.reshape(-1, DA, 1), (p.size // DA, DA, TB))
    cw_b = lane_b(conv_w_full)
    ga_b = lane_b(attn_norm_g)[0]
    gcn_b = lane_b(conv_norm_g)[0]

    u, proj_t, f_t, ktok, vtok = _inproj_fwd(x2, meta_full, norm_g + wout_flight[4][0, 0], w_t, L)
    cq, kaug, sg = _fgate_fwd(f_t, b_f.reshape(H, 1), ktok, L)
    o_t, lse = _attn_fwd(proj_t, kaug, cq, L)
    mix_t = _gate_fwd(o_t, proj_t, cw_b, ga_b, gcn_b, L)

    w_out_own, w_out_land = _split_wait(wout_flight, mix_t, "gather_w_out_wait", per_peer=False)
    w_out_full = _unshard_w_out(w_out_own, w_out_land)
    dout, dmix_t, dw_out, loss_part, dg_final = _outproj(
        mix_t, w_out_full, x2, meta_full, final_norm_g.reshape(1, D), tgt, L)
    dwo_flight = _split_start(dw_out.reshape(NDEV, D // NDEV, D), "exchange_dw_out_start", per_peer=True)
    do_t, dd, dg5_t, dga_p, dgc_p, dcw_p = _gate_bwd(dmix_t, o_t, proj_t, cw_b, ga_b + dwo_flight[4][0, 0], gcn_b, L)
    dq_t, dk_t, dv_t, dck, dcq = _attn_bwd(proj_t, kaug, vtok, do_t, lse, dd, cq, L)
    df_t, db_f = _fgate_bwd(dcq, dck, sg, L)
    dw_main, dw_f = _inproj_bwd_w(u, dq_t, dk_t, dv_t, dg5_t, df_t, L)
    dwi_flight = _split_start(_shard_w_in_grads(dw_main, dw_f), "exchange_dw_in_start", per_peer=True)
    grad_x, dmeta, dg_norm = _inproj_bwd_x(
        w_t, dq_t, dk_t, dv_t, dg5_t, df_t, dout, x2, meta_full, norm_g + dwi_flight[4][0, 0], L)
    dga = jnp.sum(dga_p, axis=1)
    dgc = jnp.sum(dgc_p, axis=1)
    dcw = jnp.sum(dcw_p, axis=2)
    small_parts = jnp.stack([
        _pack_small(dg_norm, dg_final, dga, dgc, db_f, dmeta[:, j * TB:(j + 1) * TB], dcw[:, j * DH:(j + 1) * DH],
                    loss=loss_part)
        for j in range(NDEV)], axis=0)
    small_recv = _exchange(small_parts, "exchange_small_grads")
    dwo_own, dwo_land = _split_wait(dwo_flight, small_recv, "exchange_dw_out_wait", per_peer=True)
    dwi_own, dwi_land = _split_wait(dwi_flight, dwo_land, "exchange_dw_in_wait", per_peer=True)

    big_out = _adamw_big(dwi_own, dwi_land, dwo_own, dwo_land,
                         w_in[0].T, m_w_in[0].T, v_w_in[0].T, w_out, m_w_out, v_w_out)
    g_w_in, d_w_in, nm_w_in, nv_w_in = [a.T[None] for a in big_out[:4]]
    g_w_out, d_w_out, nm_w_out, nv_w_out = big_out[4:]
    wp = _pack_small(norm_g, final_norm_g, attn_norm_g, conv_norm_g, b_f, meta, conv_w)
    mp = _pack_small(m_norm_g, m_final_norm_g, m_attn_norm_g, m_conv_norm_g, m_b_f, m_meta, m_conv_w)
    vp = _pack_small(v_norm_g, v_final_norm_g, v_attn_norm_g, v_conv_norm_g, v_b_f, v_meta, v_conv_w)
    small_out = _adamw_small(small_recv, wp, mp, vp)
    sm = [_unpack_small(p) for p in small_out]
    loss = small_out[0][32, H]
    order = ("meta", "norm_g", "w_in", "b_f", "conv_w", "attn_norm_g", "conv_norm_g", "w_out", "final_norm_g")
    groups = []
    for k, (wi, wo) in enumerate(((g_w_in, g_w_out), (d_w_in, d_w_out), (nm_w_in, nm_w_out), (nv_w_in, nv_w_out))):
        d = dict(sm[k], w_in=wi, w_out=wo)
        groups.append([d[n] for n in order])
    return (loss, grad_x[None], *groups[0], *groups[1], *groups[2], *groups[3])
```

```python
import functools

import jax
import jax.numpy as jnp
from jax import lax
from jax.experimental import pallas as pl
from jax.experimental.pallas import tpu as pltpu

F32 = jnp.float32
BF16 = jnp.bfloat16

D = 1024
DA = 512
H = 8
DH = 64
NM = 16
TB = 128
P0 = TB - NM
TT = 3 * TB
HG = 8
NDEV = 8
NSEC = 8
DF = 16
DPROJ = NSEC * DA + DF
WSH = 513
WSHP = 528
WROWS = WSHP + D // NDEV
SROWS = 64
EPS = 1e-6
NEG = -1e30
LOG2E = 1.4426950408889634
LN2 = 0.6931471805599453
QSCALE = DH ** -0.5 * LOG2E
KA = 128
CB = 256
VMEM_LIMIT = 56 * 1024 * 1024

ADAM_LR = 0.001
ADAM_B1 = 0.9
ADAM_B2 = 0.999
ADAM_EPS = 1e-08
ADAM_WD = 0.01
ADAM_STEP = 10

NT_DIMS = (((1,), (1,)), ((), ()))
TN_DIMS = (((0,), (0,)), ((), ()))
MESH = pl.DeviceIdType.MESH


def _params(n_axes=1, vmem=VMEM_LIMIT):
    return pltpu.CompilerParams(dimension_semantics=("arbitrary",) * n_axes, vmem_limit_bytes=vmem)


def _dot(a, b, dims=None):
    if dims is None:
        return jnp.dot(a, b, preferred_element_type=F32)
    return lax.dot_general(a, b, dims, preferred_element_type=F32)


def _my_place():
    return lax.axis_index("x"), lax.axis_index("y"), lax.axis_index("c")


def _all_gather(x, name):
    def body(x_ref, out_ref, send_sems, recv_sems, local_sem):
        mx, my, mc = _my_place()
        me, sibling = (mx, my, mc), (mx, my, 1 - mc)
        chips = [(1 - mx, my), (mx, 1 - my), (1 - mx, 1 - my)]

        def slot(px, py, pc):
            return out_ref.at[4 * px + 2 * py + pc]

        def copy(k, block, to, src=None):
            return pltpu.make_async_remote_copy(
                src_ref=slot(*block) if src is None else src, dst_ref=slot(*block),
                send_sem=send_sems.at[k], recv_sem=recv_sems.at[k], device_id=to, device_id_type=MESH)

        mine = pltpu.make_async_copy(x_ref, slot(*me), local_sem)
        mine.start()
        first = [copy(0, me, sibling, src=x_ref)]
        first += [copy(1 + j, me, (*chip, mc), src=x_ref) for j, chip in enumerate(chips)]
        for cp in first:
            cp.start()
        passed = [copy(4 + j, (*chip, mc), sibling) for j, chip in enumerate(chips)]
        for j, chip in enumerate(chips):
            copy(1 + j, (*chip, mc), me).wait_recv()
            passed[j].start()
        copy(0, sibling, me).wait_recv()
        for j, chip in enumerate(chips):
            copy(4 + j, (*chip, 1 - mc), me).wait_recv()
        for cp in first + passed:
            cp.wait_send()
        mine.wait()

    return pl.pallas_call(
        body, name=name,
        out_shape=jax.ShapeDtypeStruct((NDEV,) + x.shape, x.dtype),
        in_specs=[pl.BlockSpec(memory_space=pl.ANY)],
        out_specs=pl.BlockSpec(memory_space=pl.ANY),
        scratch_shapes=[pltpu.SemaphoreType.DMA((7,)), pltpu.SemaphoreType.DMA((7,)), pltpu.SemaphoreType.DMA],
    )(x)


def _exchange(parts, name):
    def body(p_ref, out_ref, send_sems, recv_sems, local_sem):
        mx, my, mc = _my_place()
        me = 4 * mx + 2 * my + mc
        mine = pltpu.make_async_copy(p_ref.at[me], out_ref.at[me], local_sem)
        mine.start()

        def peer_of(m):
            return ((1 - mx) if m & 4 else mx, (1 - my) if m & 2 else my, (1 - mc) if m & 1 else mc)

        def copy(m, src_slot, dst_slot):
            px, py, pc = peer_of(m)
            return pltpu.make_async_remote_copy(
                src_ref=p_ref.at[src_slot], dst_ref=out_ref.at[dst_slot],
                send_sem=send_sems.at[m - 1], recv_sem=recv_sems.at[m - 1],
                device_id=(px, py, pc), device_id_type=MESH)

        sends = []
        for m in range(1, NDEV):
            px, py, pc = peer_of(m)
            cp = copy(m, 4 * px + 2 * py + pc, me)
            cp.start()
            sends.append(cp)
        for m in range(1, NDEV):
            px, py, pc = peer_of(m)
            copy(m, me, 4 * px + 2 * py + pc).wait_recv()
        for cp in sends:
            cp.wait_send()
        mine.wait()

    return pl.pallas_call(
        body, name=name,
        out_shape=jax.ShapeDtypeStruct(parts.shape, parts.dtype),
        in_specs=[pl.BlockSpec(memory_space=pl.ANY)],
        out_specs=pl.BlockSpec(memory_space=pl.ANY),
        scratch_shapes=[pltpu.SemaphoreType.DMA((7,)), pltpu.SemaphoreType.DMA((7,)), pltpu.SemaphoreType.DMA],
    )(parts)


_HBM = pl.BlockSpec(memory_space=pltpu.HBM)
_SEM = pl.BlockSpec(memory_space=pltpu.SEMAPHORE)
_EFFECT = pltpu.SideEffectType.DATAFLOW_SIDE_EFFECTING


def _peer_of(m, place):
    mx, my, mc = place
    return ((1 - mx) if m & 4 else mx, (1 - my) if m & 2 else my, (1 - mc) if m & 1 else mc)


def _split_copies(src_ref, land_ref, send_sems, recv_sems, per_peer, incoming):
    place = _my_place()
    me = 4 * place[0] + 2 * place[1] + place[2]
    out = []
    for m in range(1, NDEV):
        px, py, pc = _peer_of(m, place)
        peer = 4 * px + 2 * py + pc
        src = (src_ref.at[me] if incoming else src_ref.at[peer]) if per_peer else src_ref
        out.append(pltpu.make_async_remote_copy(
            src_ref=src, dst_ref=land_ref.at[peer if incoming else me],
            send_sem=send_sems.at[m - 1], recv_sem=recv_sems.at[m - 1],
            device_id=(px, py, pc), device_id_type=MESH))
    return out


def _split_start(src, name, per_peer):
    slab = src.shape[1:] if per_peer else src.shape

    def body(src_ref, land_ref, send_sems, recv_sems, src_thru, land_thru, token):
        for cp in _split_copies(src_ref, land_ref, send_sems, recv_sems, per_peer, incoming=False):
            cp.start()
        token[...] = jnp.zeros_like(token)

    return pl.pallas_call(
        body, name=name,
        out_shape=(pltpu.SemaphoreType.DMA((NDEV - 1,)), pltpu.SemaphoreType.DMA((NDEV - 1,)),
                   pltpu.HBM(src.shape, src.dtype), pltpu.HBM((NDEV,) + slab, src.dtype),
                   jax.ShapeDtypeStruct((8, TB), F32)),
        in_specs=(_HBM, _HBM), out_specs=(_SEM, _SEM, _HBM, _HBM, pl.BlockSpec(memory_space=pltpu.VMEM)),
        input_output_aliases={0: 2, 1: 3},
        compiler_params=pltpu.CompilerParams(has_side_effects=_EFFECT),
    )(pltpu.with_memory_space_constraint(src, pltpu.HBM),
      pltpu.with_memory_space_constraint(lax.empty((NDEV,) + slab, src.dtype), pltpu.HBM))


def _split_wait(handles, after, name, per_peer):
    send_sems, recv_sems, src_thru, land_thru, _ = handles

    def body(src_ref, land_ref, send_sems, recv_sems, after_ref, src_out, land_out):
        for cp in _split_copies(src_ref, land_ref, send_sems, recv_sems, per_peer, incoming=False):
            cp.wait_send()
        for cp in _split_copies(src_ref, land_ref, send_sems, recv_sems, per_peer, incoming=True):
            cp.wait_recv()

    return pl.pallas_call(
        body, name=name,
        out_shape=(pltpu.HBM(src_thru.shape, src_thru.dtype), pltpu.HBM(land_thru.shape, land_thru.dtype)),
        in_specs=(_HBM, _HBM, _SEM, _SEM, pl.BlockSpec(memory_space=pl.ANY)), out_specs=(_HBM, _HBM),
        input_output_aliases={0: 0, 1: 1},
        compiler_params=pltpu.CompilerParams(has_side_effects=_EFFECT),
    )(src_thru, land_thru, send_sems, recv_sems, after)


def _pick_slab(j, own_ref, land_ref, rows, per_peer=True):
    mx, my, mc = _my_place()
    me = 4 * mx + 2 * my + mc
    own = (lambda: own_ref[j, rows, :]) if per_peer else (lambda: own_ref[rows, :])
    return lax.cond(me == j, own, lambda: land_ref[j, rows, :])


def _h_block(t, x_ref, meta_ref):
    first = jnp.concatenate([jnp.zeros((P0, D), F32), meta_ref[...]], axis=0)
    return jnp.where(t == 0, first, x_ref[...])


def _x_spec():
    return pl.BlockSpec((TB, D), lambda t: (jnp.maximum(t - 1, 0), 0))


def _x_specs3():
    return [pl.BlockSpec((TB, D), lambda j: (jnp.maximum(3 * j - 1, 0), 0)),
            pl.BlockSpec((TB, D), lambda j: (3 * j, 0)),
            pl.BlockSpec((TB, D), lambda j: (3 * j + 1, 0))]


def _h_tile(j, xa_ref, xb_ref, xc_ref, meta_ref):
    first = jnp.concatenate([jnp.zeros((P0, D), F32), meta_ref[...]], axis=0)
    return jnp.concatenate([jnp.where(j == 0, first, xa_ref[...]), xb_ref[...], xc_ref[...]], axis=0)


def _full_spec(shape):
    return pl.BlockSpec(shape, lambda *_: (0,) * len(shape))


def _sigmoid(z):
    return 1.0 / (1.0 + jnp.exp(-z))


def _grouped(x):
    return x.reshape(H, DH, x.shape[-1])


def _group_rstd(x3):
    return lax.rsqrt(jnp.mean(x3 * x3, axis=1, keepdims=True) + EPS)


def _lane_tiles_sum(x):
    out = x[:, :TB]
    for i in range(1, x.shape[1] // TB):
        out = out + x[:, i * TB:(i + 1) * TB]
    return out


def _inproj_fwd(x, meta_full, norm_g, w_t, L):
    nj = L // TT

    def body(xa_ref, xb_ref, xc_ref, meta_ref, g_ref, w_ref, u_ref, proj_ref, f_ref, ktok_ref, vtok_ref):
        hb = _h_tile(pl.program_id(0), xa_ref, xb_ref, xc_ref, meta_ref)
        r = lax.rsqrt(jnp.mean(hb * hb, axis=-1, keepdims=True) + EPS)
        u = (hb * r * g_ref[...]).astype(BF16)
        u_ref[...] = u
        for s in range(NSEC):
            p = _dot(u, w_ref[s * DA:(s + 1) * DA, :], NT_DIMS)
            if s == 0:
                p = p * QSCALE
            if s in (1, 2):
                tok_ref = ktok_ref if s == 1 else vtok_ref
                for h in range(H):
                    tok_ref[h] = p[:, h * DH:(h + 1) * DH].astype(BF16)
            proj_ref[s * DA:(s + 1) * DA, :] = p.T.astype(BF16)
        f_ref[...] = _dot(w_ref[NSEC * DA:DPROJ, :], u, NT_DIMS)[:H]

    return pl.pallas_call(
        body, name="inproj_fwd", grid=(nj,),
        in_specs=_x_specs3() + [_full_spec((NM, D)), _full_spec((1, D)), _full_spec((DPROJ, D))],
        out_specs=[
            pl.BlockSpec((TT, D), lambda t: (t, 0)),
            pl.BlockSpec((NSEC * DA, TT), lambda t: (0, t)),
            pl.BlockSpec((H, TT), lambda t: (0, t)),
            pl.BlockSpec((H, TT, DH), lambda t: (0, t, 0)),
            pl.BlockSpec((H, TT, DH), lambda t: (0, t, 0)),
        ],
        out_shape=[
            jax.ShapeDtypeStruct((L, D), BF16),
            jax.ShapeDtypeStruct((NSEC * DA, L), BF16),
            jax.ShapeDtypeStruct((H, L), F32),
            jax.ShapeDtypeStruct((H, L, DH), BF16),
            jax.ShapeDtypeStruct((H, L, DH), BF16),
        ],
        compiler_params=_params(),
    )(x, x, x, meta_full, norm_g, w_t)


def _split3(x):
    hi = x.astype(BF16).astype(F32)
    r = x - hi
    mid = r.astype(BF16).astype(F32)
    return hi, mid, (r - mid).astype(BF16).astype(F32)


def _bias_rows(bias):
    one = jnp.ones((1, TT), F32)
    zero = jnp.zeros((1, TT), F32)
    parts = [zero] * 3 if bias is None else list(_split3(bias))
    return jnp.concatenate([one] * 3 + parts + [zero] * (DF - 6), axis=0).astype(BF16)


def _fgate_fwd(f_t, b_col, ktok, L):
    nb = L // TB

    def body(f_ref, b_ref, ktok_ref, cq_ref, kaug_ref, sg_ref):
        z = f_ref[...] + b_ref[...]
        idx = lax.broadcasted_iota(jnp.int32, (H, L), 1)
        real = idx >= P0
        lf = jnp.where(real, jnp.minimum(z, 0.0) - jnp.log1p(jnp.exp(-jnp.abs(z))), 0.0)
        sg_ref[...] = jnp.where(real, 1.0 / (1.0 + jnp.exp(z)), 0.0)
        c = lf
        s = 1
        while s < L:
            c = c + jnp.where(idx >= s, pltpu.roll(c, s, 1), 0.0)
            s *= 2
        c = c * LOG2E
        for h in range(H):
            cq_ref[h] = c[h:h + 1, :]
        ck = jnp.where(real, c, -NEG)
        lane = lax.broadcasted_iota(jnp.int32, (TB, KA), 1)
        tail = jnp.where((lane >= DH + 3) & (lane < DH + 6), 1.0, 0.0)
        for h in range(H):
            for b in range(nb):
                blk = slice(b * TB, (b + 1) * TB)
                col = jnp.broadcast_to(ck[h:h + 1, blk], (TB, TB)).T
                hi, mid, lo = _split3(-col)
                k = jnp.concatenate([ktok_ref[h, blk, :].astype(F32), jnp.zeros((TB, KA - DH), F32)], axis=1)
                out = jnp.where(lane < DH, k, jnp.where(lane == DH, hi, jnp.where(
                    lane == DH + 1, mid, jnp.where(lane == DH + 2, lo, tail))))
                kaug_ref[h, blk, :] = out.astype(BF16)

    return pl.pallas_call(
        body, name="fgate_fwd",
        out_shape=[
            jax.ShapeDtypeStruct((H, 1, L), F32),
            jax.ShapeDtypeStruct((H, L, KA), BF16),
            jax.ShapeDtypeStruct((H, L), F32),
        ],
        compiler_params=pltpu.CompilerParams(vmem_limit_bytes=VMEM_LIMIT),
    )(f_t, b_col, ktok)


def _causal_mask():
    r = lax.broadcasted_iota(jnp.int32, (TT, TT), 0)
    c = lax.broadcasted_iota(jnp.int32, (TT, TT), 1)
    return r <= c


def _attn_fwd(proj_t, kaug, cq, L):
    nq = L // TT

    def body(q_ref, kaug_ref, v_ref, cq_ref, o_ref, lse_ref, qa_scr, s_scr, cmax_scr, m_scr, acc_scr):
        j = pl.program_id(1)
        rows = [slice(g * DH, (g + 1) * DH) for g in range(HG)]
        ones = jnp.ones((DF, TT), BF16)
        for g in range(HG):
            qa_scr[g] = jnp.concatenate(
                [q_ref[rows[g], :], _bias_rows(None), jnp.zeros((KA - DH - DF, TT), BF16)], axis=0)

        def scores(kt, masked):
            k_off = pl.multiple_of(kt * TT, TT)
            for g in range(HG):
                s = _dot(kaug_ref[g, pl.ds(k_off, TT), :], qa_scr[g])
                if masked:
                    s = jnp.where(_causal_mask(), s, NEG)
                s_scr[g] = s
                cmax_scr[g] = jnp.max(s, axis=0, keepdims=True)

        def softmax_pv(kt):
            k_off = pl.multiple_of(kt * TT, TT)
            for g in range(HG):
                m_old = m_scr[g]
                m_new = jnp.maximum(m_old, cmax_scr[g])
                alpha = jnp.exp2(m_old - m_new)
                p = jnp.exp2(s_scr[g] - m_new).astype(BF16)
                v1 = jnp.concatenate([v_ref[rows[g], pl.ds(k_off, TT)], ones], axis=0)
                acc_scr[g] = alpha * acc_scr[g] + _dot(v1, p)
                m_scr[g] = m_new

        m_scr[...] = jnp.full_like(m_scr, NEG)
        acc_scr[...] = jnp.zeros_like(acc_scr)

        scores(j, True)

        def step(t, c):
            softmax_pv(j - t)
            scores(j - 1 - t, False)
            return c

        lax.fori_loop(0, j, step, 0)
        softmax_pv(0)
        for g in range(HG):
            l = acc_scr[g, DH:DH + 1, :]
            o_ref[rows[g], :] = acc_scr[g, :DH, :] * (1.0 / l)
            lse_ref[g] = m_scr[g] + jnp.log2(l) + cq_ref[g]

    return pl.pallas_call(
        body, name="attn_fwd", grid=(H // HG, nq),
        in_specs=[
            pl.BlockSpec((HG * DH, TT), lambda h, j: (h, j)),
            pl.BlockSpec((HG, L, KA), lambda h, j: (h, 0, 0)),
            pl.BlockSpec((HG * DH, L), lambda h, j: (2 * H // HG + h, 0)),
            pl.BlockSpec((HG, 1, TT), lambda h, j: (h, 0, j)),
        ],
        out_specs=[
            pl.BlockSpec((HG * DH, TT), lambda h, j: (h, j)),
            pl.BlockSpec((HG, 1, TT), lambda h, j: (h, 0, j)),
        ],
        out_shape=[jax.ShapeDtypeStruct((DA, L), F32), jax.ShapeDtypeStruct((H, 1, L), F32)],
        scratch_shapes=[pltpu.VMEM((HG, KA, TT), BF16), pltpu.VMEM((HG, TT, TT), F32), pltpu.VMEM((HG, 1, TT), F32),
                        pltpu.VMEM((HG, 1, TT), F32), pltpu.VMEM((HG, DH + DF, TT), F32)],
        compiler_params=_params(2),
    )(proj_t, kaug, proj_t, cq)


def _gate_common(o, za, gb, gc, xc, zc, gcp, xcp, cw_ref, ga_ref, gcn_ref, first):
    n_rep = TT // TB
    a = gc * xc
    a_prev = jnp.where(first, 0.0, gcp * xcp)
    full = jnp.concatenate([a_prev, a], axis=1)
    a1 = pltpu.roll(full, 1, 1)[:, TB:]
    a2 = pltpu.roll(full, 2, 1)[:, TB:]
    w0 = jnp.tile(cw_ref[0], (1, n_rep))
    w1 = jnp.tile(cw_ref[1], (1, n_rep))
    w2 = jnp.tile(cw_ref[2], (1, n_rep))
    cv = w0 * a2 + w1 * a1 + w2 * a
    e = gb * cv
    e3 = _grouped(e)
    rc = _group_rstd(e3)
    ec = (e3 * rc).reshape(DA, TT)
    o3 = _grouped(o)
    ra = _group_rstd(o3)
    oa = (o3 * ra).reshape(DA, TT)
    g_a = jnp.tile(ga_ref[...], (1, n_rep))
    g_c = jnp.tile(gcn_ref[...], (1, n_rep))
    sa = _sigmoid(za)
    sc = _sigmoid(zc)
    return dict(a=a, a1=a1, a2=a2, w0=w0, w1=w1, w2=w2, cv=cv, e=e, rc=rc, ec=ec, ra=ra, oa=oa,
                g_a=g_a, g_c=g_c, sa=sa, sc=sc)


def _gate_specs(nj, rev):
    def jj(i):
        return (nj - 1 - i) if rev else i

    def sec(s):
        return pl.BlockSpec((DA, TT), lambda i: (s, jj(i)))

    def halo(s):
        return pl.BlockSpec((DA, TB), lambda i: (s, jnp.maximum(3 * jj(i) - 1, 0)))

    return [pl.BlockSpec((DA, TT), lambda i: (0, jj(i))), sec(3), sec(4), sec(5), sec(6), sec(7), halo(5), halo(6),
            _full_spec((3, DA, TB)), _full_spec((DA, TB)), _full_spec((DA, TB))]


def _gate_fwd(o_t, proj_t, cw_b, ga_b, gcn_b, L):
    nj = L // TT

    def body(o_ref, za_ref, gb_ref, gc_ref, xc_ref, zc_ref, gcp_ref, xcp_ref, cw_ref, ga_ref, gcn_ref, mix_ref):
        j = pl.program_id(0)
        f32 = lambda r: r[...].astype(F32)
        za, zc = f32(za_ref), f32(zc_ref)
        g = _gate_common(o_ref[...], za, f32(gb_ref), f32(gc_ref), f32(xc_ref), zc, f32(gcp_ref), f32(xcp_ref),
                         cw_ref, ga_ref, gcn_ref, j == 0)
        mix_ref[:DA, :] = (g["oa"] * g["g_a"] * (za * g["sa"])).astype(BF16)
        mix_ref[DA:, :] = (g["ec"] * g["g_c"] * (zc * g["sc"])).astype(BF16)

    return pl.pallas_call(
        body, name="gate_fwd", grid=(nj,),
        in_specs=_gate_specs(nj, False),
        out_specs=pl.BlockSpec((2 * DA, TT), lambda j: (0, j)),
        out_shape=jax.ShapeDtypeStruct((2 * DA, L), BF16),
        compiler_params=_params(),
    )(o_t, proj_t, proj_t, proj_t, proj_t, proj_t, proj_t, proj_t, cw_b, ga_b, gcn_b)


def _outproj(mix_t, w_out, x, meta_full, fng, target, L):
    nj = L // TT

    def body(mix_ref, w_ref, xa_ref, xb_ref, xc_ref, meta_ref, g_ref, ta_ref, tb_ref, tc_ref,
             dout_ref, dmix_ref, dwb_ref, loss_ref, dg_ref, dw_ref):
        t = pl.program_id(0)

        @pl.when(t == 0)
        def _():
            dw_ref[...] = jnp.zeros_like(dw_ref)
            loss_ref[...] = jnp.zeros_like(loss_ref)
            dg_ref[...] = jnp.zeros_like(dg_ref)

        mix = mix_ref[...]
        o = _dot(mix, w_ref[...], TN_DIMS) + _h_tile(t, xa_ref, xb_ref, xc_ref, meta_ref)
        r = lax.rsqrt(jnp.mean(o * o, axis=-1, keepdims=True) + EPS)
        g = g_ref[...]
        orn = o * r
        tgt = jnp.concatenate([ta_ref[...], tb_ref[...], tc_ref[...]], axis=0)
        row = lax.broadcasted_iota(jnp.int32, (TT, 1), 0)
        real = jnp.where((t > 0) | (row >= TB), 1.0, 0.0)
        diff = (orn * g - tgt) * real
        loss_ref[...] += 0.5 * jnp.sum(diff * diff) * (1.0 / D)
        dy = diff * (1.0 / D)
        dg_ref[...] += jnp.sum(dy * orn, axis=0, keepdims=True)
        gy = dy * g
        dout = r * gy - orn * (r * jnp.mean(gy * orn, axis=-1, keepdims=True))
        dout_ref[...] = dout
        db = dout.astype(BF16)
        dmix_ref[...] = _dot(db, w_ref[...], NT_DIMS).T.astype(BF16)
        dw_ref[...] += _dot(mix, db)

        @pl.when(t == nj - 1)
        def _():
            dwb_ref[...] = dw_ref[...].astype(BF16)

    return pl.pallas_call(
        body, name="outproj", grid=(nj,),
        in_specs=[pl.BlockSpec((D, TT), lambda t: (0, t)), _full_spec((D, D))] + _x_specs3()
                 + [_full_spec((NM, D)), _full_spec((1, D))] + _x_specs3(),
        out_specs=[pl.BlockSpec((TT, D), lambda t: (t, 0)), pl.BlockSpec((D, TT), lambda t: (0, t)),
                   _full_spec((D, D)), _full_spec((1, 1)), _full_spec((1, D))],
        out_shape=[jax.ShapeDtypeStruct((L, D), F32), jax.ShapeDtypeStruct((D, L), BF16),
                   jax.ShapeDtypeStruct((D, D), BF16), jax.ShapeDtypeStruct((1, 1), F32),
                   jax.ShapeDtypeStruct((1, D), F32)],
        scratch_shapes=[pltpu.VMEM((D, D), F32)],
        compiler_params=_params(),
    )(mix_t, w_out, x, x, x, meta_full, fng, target, target, target)


def _gate_bwd(dmix_t, o_t, proj_t, cw_b, ga_b, gcn_b, L):
    nj = L // TT

    def body(dmix_ref, o_ref, za_ref, gb_ref, gc_ref, xc_ref, zc_ref, gcp_ref, xcp_ref, cw_ref, ga_ref, gcn_ref,
             do_ref, dd_ref, dg5_ref, dga_ref, dgc_ref, dcw_ref, carry_ref):
        i = pl.program_id(0)
        j = nj - 1 - i

        @pl.when(i == 0)
        def _():
            carry_ref[...] = jnp.zeros_like(carry_ref)
            dga_ref[...] = jnp.zeros_like(dga_ref)
            dgc_ref[...] = jnp.zeros_like(dgc_ref)
            dcw_ref[...] = jnp.zeros_like(dcw_ref)

        f32 = lambda r: r[...].astype(F32)
        o, za, gb, gc, xc, zc = o_ref[...], f32(za_ref), f32(gb_ref), f32(gc_ref), f32(xc_ref), f32(zc_ref)
        g = _gate_common(o, za, gb, gc, xc, zc, f32(gcp_ref), f32(xcp_ref), cw_ref, ga_ref, gcn_ref, j == 0)
        dya = dmix_ref[:DA, :].astype(F32)
        dyc = dmix_ref[DA:, :].astype(F32)
        sa, sc = g["sa"], g["sc"]

        dn = dya * (za * sa)
        dg5_ref[0:DA, :] = (dya * (g["oa"] * g["g_a"]) * (sa * (1.0 + za * (1.0 - sa)))).astype(BF16)
        dga_ref[...] += _lane_tiles_sum(dn * g["oa"])
        dng = dn * g["g_a"]
        mean_a = jnp.mean(_grouped(dng * g["oa"]), axis=1, keepdims=True)
        do = ((_grouped(dng) - _grouped(g["oa"]) * mean_a) * g["ra"]).reshape(DA, TT)
        do_ref[...] = do.astype(BF16)
        dd = jnp.sum(_grouped(do * o), axis=1)
        for h in range(H):
            dd_ref[h] = dd[h:h + 1, :]

        dnc = dyc * (zc * sc)
        dg5_ref[4 * DA:5 * DA, :] = (dyc * (g["ec"] * g["g_c"]) * (sc * (1.0 + zc * (1.0 - sc)))).astype(BF16)
        dgc_ref[...] += _lane_tiles_sum(dnc * g["ec"])
        dncg = dnc * g["g_c"]
        mean_c = jnp.mean(_grouped(dncg * g["ec"]), axis=1, keepdims=True)
        de = ((_grouped(dncg) - _grouped(g["ec"]) * mean_c) * g["rc"]).reshape(DA, TT)
        dg5_ref[DA:2 * DA, :] = (de * g["cv"]).astype(BF16)
        dcv = de * gb
        full = jnp.concatenate([dcv, carry_ref[...]], axis=1)
        d1 = pltpu.roll(full, TT + TB - 1, 1)[:, :TT]
        d2 = pltpu.roll(full, TT + TB - 2, 1)[:, :TT]
        carry_ref[...] = dcv[:, :TB]
        da = g["w2"] * dcv + g["w1"] * d1 + g["w0"] * d2
        dg5_ref[2 * DA:3 * DA, :] = (da * xc).astype(BF16)
        dg5_ref[3 * DA:4 * DA, :] = (da * gc).astype(BF16)
        dcw_ref[0] += _lane_tiles_sum(dcv * g["a2"])
        dcw_ref[1] += _lane_tiles_sum(dcv * g["a1"])
        dcw_ref[2] += _lane_tiles_sum(dcv * g["a"])

    rj = lambda i: nj - 1 - i
    return pl.pallas_call(
        body, name="gate_bwd", grid=(nj,),
        in_specs=[pl.BlockSpec((2 * DA, TT), lambda i: (0, rj(i)))] + _gate_specs(nj, True),
        out_specs=[
            pl.BlockSpec((DA, TT), lambda i: (0, rj(i))),
            pl.BlockSpec((H, 1, TT), lambda i: (0, 0, rj(i))),
            pl.BlockSpec((5 * DA, TT), lambda i: (0, rj(i))),
            _full_spec((DA, TB)), _full_spec((DA, TB)), _full_spec((3, DA, TB)),
        ],
        out_shape=[
            jax.ShapeDtypeStruct((DA, L), BF16),
            jax.ShapeDtypeStruct((H, 1, L), F32),
            jax.ShapeDtypeStruct((5 * DA, L), BF16),
            jax.ShapeDtypeStruct((DA, TB), F32),
            jax.ShapeDtypeStruct((DA, TB), F32),
            jax.ShapeDtypeStruct((3, DA, TB), F32),
        ],
        scratch_shapes=[pltpu.VMEM((DA, TB), F32)],
        compiler_params=_params(),
    )(dmix_t, o_t, proj_t, proj_t, proj_t, proj_t, proj_t, proj_t, proj_t, cw_b, ga_b, gcn_b)


def _attn_bwd(proj_t, kaug, vtok, do_t, lse, dd, cq, L):
    nk = L // TT

    def body(q_ref, kaug_ref, vtok_ref, kt_ref, do_ref, lse_ref, dd_ref, cq_ref,
             dq_ref, dk_ref, dv_ref, dck_ref, dcq_ref, dq_acc, q1_scr, kt1_scr, s_scr, dp_scr, dv_scr, dk_scr):
        i = pl.program_id(1)

        @pl.when(i == 0)
        def _():
            dq_acc[...] = jnp.zeros_like(dq_acc)

        rows = [slice(g * DH, (g + 1) * DH) for g in range(HG)]
        ones = jnp.ones((DF, TT), BF16)
        zpad = jnp.zeros((KA - DH - DF, TT), BF16)
        for g in range(HG):
            kt1_scr[g] = jnp.concatenate([kt_ref[rows[g], :], ones], axis=0)
        dv_scr[...] = jnp.zeros_like(dv_scr)
        dk_scr[...] = jnp.zeros_like(dk_scr)

        def scores(jq, masked):
            q_off = pl.multiple_of(jq * TT, TT)
            for g in range(HG):
                bias = cq_ref[g, :, pl.ds(q_off, TT)] - lse_ref[g, :, pl.ds(q_off, TT)]
                q1 = jnp.concatenate([q_ref[rows[g], pl.ds(q_off, TT)], _bias_rows(bias)], axis=0)
                q1_scr[g] = q1
                s = _dot(kaug_ref[g], jnp.concatenate([q1, zpad], axis=0))
                if masked:
                    s = jnp.where(_causal_mask(), s, NEG)
                s_scr[g] = s
                dp_scr[g] = _dot(vtok_ref[g], do_ref[rows[g], pl.ds(q_off, TT)])

        def grads(jq):
            q_off = pl.multiple_of(jq * TT, TT)
            for g in range(HG):
                p = jnp.exp2(s_scr[g])
                ds = (p * (dp_scr[g] - dd_ref[g, :, pl.ds(q_off, TT)])).astype(BF16)
                dv_scr[g] += _dot(do_ref[rows[g], pl.ds(q_off, TT)], p.astype(BF16), NT_DIMS)
                dk_scr[g] += _dot(q1_scr[g], ds, NT_DIMS)
                dq_acc[g, :, pl.ds(q_off, TT)] += _dot(kt1_scr[g], ds)

        scores(i, True)

        def step(jq, c):
            grads(jq)
            scores(jq + 1, False)
            return c

        lax.fori_loop(i, nk - 1, step, 0)
        grads(nk - 1)
        for g in range(HG):
            dv_ref[rows[g], :] = dv_scr[g].astype(BF16)
            dk_ref[rows[g], :] = (dk_scr[g, :DH, :] * LN2).astype(BF16)
            dck_ref[g] = dk_scr[g, DH:DH + 1, :]

        @pl.when(i == nk - 1)
        def _():
            for g in range(HG):
                dq_ref[rows[g], :] = (dq_acc[g, :DH, :] * (DH ** -0.5)).astype(BF16)
                dcq_ref[g] = dq_acc[g, DH:DH + 1, :]

    head = lambda h, i: (h, 0)
    row = lambda h, i: (h, 0, 0)
    return pl.pallas_call(
        body, name="attn_bwd", grid=(H // HG, nk),
        in_specs=[
            pl.BlockSpec((HG * DH, L), head),
            pl.BlockSpec((HG, TT, KA), lambda h, i: (h, i, 0)),
            pl.BlockSpec((HG, TT, DH), lambda h, i: (h, i, 0)),
            pl.BlockSpec((HG * DH, TT), lambda h, i: (H // HG + h, i)),
            pl.BlockSpec((HG * DH, L), head),
            pl.BlockSpec((HG, 1, L), row), pl.BlockSpec((HG, 1, L), row), pl.BlockSpec((HG, 1, L), row),
        ],
        out_specs=[
            pl.BlockSpec((HG * DH, L), head),
            pl.BlockSpec((HG * DH, TT), lambda h, i: (h, i)),
            pl.BlockSpec((HG * DH, TT), lambda h, i: (h, i)),
            pl.BlockSpec((HG, 1, TT), lambda h, i: (h, 0, i)),
            pl.BlockSpec((HG, 1, L), row),
        ],
        out_shape=[jax.ShapeDtypeStruct((DA, L), BF16), jax.ShapeDtypeStruct((DA, L), BF16),
                   jax.ShapeDtypeStruct((DA, L), BF16), jax.ShapeDtypeStruct((H, 1, L), F32),
                   jax.ShapeDtypeStruct((H, 1, L), F32)],
        scratch_shapes=[
            pltpu.VMEM((HG, DH + DF, L), F32),
            pltpu.VMEM((HG, DH + DF, TT), BF16), pltpu.VMEM((HG, DH + DF, TT), BF16),
            pltpu.VMEM((HG, TT, TT), F32), pltpu.VMEM((HG, TT, TT), F32),
            pltpu.VMEM((HG, DH, TT), F32), pltpu.VMEM((HG, DH + DF, TT), F32)],
        compiler_params=_params(2),
    )(proj_t, kaug, vtok, proj_t, do_t, lse, dd, cq)


def _fgate_bwd(dcq, dck, sg, L):
    def body(dcq_ref, dck_ref, sg_ref, df_ref, db_ref):
        dc = jnp.concatenate([dcq_ref[h] - dck_ref[h] for h in range(H)], axis=0)
        idx = lax.broadcasted_iota(jnp.int32, (H, L), 1)
        r = dc
        s = 1
        while s < L:
            r = r + jnp.where(idx + s < L, pltpu.roll(r, L - s, 1), 0.0)
            s *= 2
        df = r * sg_ref[...]
        db_ref[...] = jnp.sum(df, axis=1, keepdims=True)
        df_ref[...] = jnp.concatenate([df, jnp.zeros((DF - H, L), F32)], axis=0).astype(BF16)

    return pl.pallas_call(
        body, name="fgate_bwd",
        out_shape=[jax.ShapeDtypeStruct((DF, L), BF16), jax.ShapeDtypeStruct((H, 1), F32)],
        compiler_params=pltpu.CompilerParams(vmem_limit_bytes=VMEM_LIMIT),
    )(dcq, dck, sg)


def _inproj_bwd_x(w, dq_t, dk_t, dv_t, dg5_t, df_t, dout, x, meta_full, norm_g, L):
    nb = L // TB
    seq = x.shape[0]

    def body(w_ref, dq_ref, dk_ref, dv_ref, dg5_ref, df_ref, dout_ref, x_ref, meta_ref, g_ref,
             gx_ref, dmeta_ref, dg_ref):
        t = pl.program_id(0)

        @pl.when(t == 0)
        def _():
            dg_ref[...] = jnp.zeros_like(dg_ref)

        du = _dot(dq_ref[...], w_ref[0:DA, :], TN_DIMS)
        du += _dot(dk_ref[...], w_ref[DA:2 * DA, :], TN_DIMS)
        du += _dot(dv_ref[...], w_ref[2 * DA:3 * DA, :], TN_DIMS)
        du += _dot(dg5_ref[...], w_ref[3 * DA:NSEC * DA, :], TN_DIMS)
        du += _dot(df_ref[...], w_ref[NSEC * DA:DPROJ, :], TN_DIMS)
        hb = _h_block(t, x_ref, meta_ref)
        r = lax.rsqrt(jnp.mean(hb * hb, axis=-1, keepdims=True) + EPS)
        hn = hb * r
        dg_ref[...] += jnp.sum(du * hn, axis=0, keepdims=True)
        gu = du * g_ref[...]
        dh = dout_ref[...] + r * gu - hn * (r * jnp.mean(gu * hn, axis=-1, keepdims=True))
        gx_ref[...] = dh

        @pl.when(t == 0)
        def _():
            dmeta_ref[...] = dh[P0:, :]

    blk = lambda rows: pl.BlockSpec((rows, TB), lambda t: (0, t))
    return pl.pallas_call(
        body, name="inproj_bwd_x", grid=(nb,),
        in_specs=[_full_spec((DPROJ, D)), blk(DA), blk(DA), blk(DA), blk(5 * DA), blk(DF),
                  pl.BlockSpec((TB, D), lambda t: (t, 0)), _x_spec(), _full_spec((NM, D)), _full_spec((1, D))],
        out_specs=[_x_spec(), _full_spec((NM, D)), _full_spec((1, D))],
        out_shape=[jax.ShapeDtypeStruct((seq, D), F32), jax.ShapeDtypeStruct((NM, D), F32),
                   jax.ShapeDtypeStruct((1, D), F32)],
        compiler_params=_params(),
    )(w, dq_t, dk_t, dv_t, dg5_t, df_t, dout, x, meta_full, norm_g)


def _inproj_bwd_w(u, dq_t, dk_t, dv_t, dg5_t, df_t, L):
    kt = L // 3
    nkt = 3

    def body(u_ref, dq_ref, dk_ref, dv_ref, dg5_ref, df_ref, dw_ref, dwf_ref):
        s = pl.program_id(0)
        k = pl.program_id(1)

        @pl.when(k == 0)
        def _():
            dw_ref[...] = jnp.zeros_like(dw_ref)

        @pl.when((s == 0) & (k == 0))
        def _():
            dwf_ref[...] = jnp.zeros_like(dwf_ref)

        u_blk = u_ref[...]
        for sec, ref in ((0, dq_ref), (1, dk_ref), (2, dv_ref)):
            @pl.when(s == sec)
            def _(ref=ref):
                dw_ref[...] += _dot(ref[...], u_blk)

        @pl.when(s >= 3)
        def _():
            dw_ref[...] += _dot(dg5_ref[...], u_blk)

        @pl.when(s == NSEC - 1)
        def _():
            dwf_ref[...] += _dot(df_ref[...], u_blk)

    def only(sec):
        return lambda s, k: (0, jnp.where(s == sec, k, 0))

    return pl.pallas_call(
        body, name="inproj_bwd_w", grid=(NSEC, nkt),
        in_specs=[
            pl.BlockSpec((kt, D), lambda s, k: (k, 0)),
            pl.BlockSpec((DA, kt), only(0)), pl.BlockSpec((DA, kt), only(1)), pl.BlockSpec((DA, kt), only(2)),
            pl.BlockSpec((DA, kt), lambda s, k: (jnp.maximum(s - 3, 0), jnp.where(s >= 3, k, 0))),
            pl.BlockSpec((DF, kt), only(NSEC - 1)),
        ],
        out_specs=[pl.BlockSpec((DA, D), lambda s, k: (s, 0)), _full_spec((DF, D))],
        out_shape=[jax.ShapeDtypeStruct((NSEC * DA, D), F32), jax.ShapeDtypeStruct((DF, D), F32)],
        compiler_params=_params(2),
    )(u, dq_t, dk_t, dv_t, dg5_t, df_t)


def _adamw(w, g, m, v):
    m = ADAM_B1 * m + (1.0 - ADAM_B1) * g
    v = ADAM_B2 * v + (1.0 - ADAM_B2) * (g * g)
    m_hat = m / (1.0 - ADAM_B1 ** ADAM_STEP)
    v_hat = v / (1.0 - ADAM_B2 ** ADAM_STEP)
    delta = -ADAM_LR * (m_hat / (jnp.sqrt(v_hat) + ADAM_EPS) + ADAM_WD * w)
    return delta, m, v


def _sum_slabs(ref, rows):
    g = ref[0, rows, :].astype(F32)
    for j in range(1, NDEV):
        g = g + ref[j, rows, :].astype(F32)
    return g


def _adamw_big(own_in, land_in, own_out, land_out, w_in_t, m_in_t, v_in_t, w_out, m_out, v_out):
    cb = CB
    e_sh = D // NDEV
    in_shape = jax.ShapeDtypeStruct(w_in_t.shape, F32)
    out_shape = jax.ShapeDtypeStruct(w_out.shape, F32)

    def total(own_ref, land_ref, rows):
        g = _pick_slab(0, own_ref, land_ref, rows).astype(F32)
        for j in range(1, NDEV):
            g = g + _pick_slab(j, own_ref, land_ref, rows).astype(F32)
        return g

    def body(oi_ref, li_ref, oo_ref, lo_ref, wi_ref, mi_ref, vi_ref, wo_ref, mo_ref, vo_ref,
             gi, di, mi, vi, go, do, mo, vo):
        g = total(oi_ref, li_ref, slice(0, WSHP))[:WSH]
        d, mn, vn = _adamw(wi_ref[...], g, mi_ref[...], vi_ref[...])
        gi[...], di[...], mi[...], vi[...] = g, d, mn, vn
        g = total(oo_ref, lo_ref, slice(0, e_sh))
        d, mn, vn = _adamw(wo_ref[0], g, mo_ref[0], vo_ref[0])
        go[0], do[0], mo[0], vo[0] = g, d, mn, vn

    slab = lambda rows: pl.BlockSpec((NDEV, rows, cb), lambda i: (0, 0, i))
    ispec = pl.BlockSpec((WSH, cb), lambda i: (0, i))
    ospec = pl.BlockSpec((1, e_sh, cb), lambda i: (0, 0, i))
    return pl.pallas_call(
        body, name="adamw_big", grid=(D // cb,),
        in_specs=[slab(WSHP), slab(WSHP), slab(e_sh), slab(e_sh), ispec, ispec, ispec, ospec, ospec, ospec],
        out_specs=[ispec] * 4 + [ospec] * 4, out_shape=[in_shape] * 4 + [out_shape] * 4,
        compiler_params=_params(),
    )(own_in, land_in, own_out, land_out, w_in_t, m_in_t, v_in_t, w_out, m_out, v_out)


F0 = 3 * DA


def _unshard_w_out(own, land):
    e_sh = D // NDEV

    def body(own_ref, land_ref, wo_ref):
        for j in range(NDEV):
            wo_ref[j * e_sh:(j + 1) * e_sh, :] = _pick_slab(j, own_ref, land_ref, slice(0, e_sh), per_peer=False)

    return pl.pallas_call(
        body, name="unshard_w_out", grid=(D // CB,),
        in_specs=[pl.BlockSpec((e_sh, CB), lambda i: (0, i)), pl.BlockSpec((NDEV, e_sh, CB), lambda i: (0, 0, i))],
        out_specs=pl.BlockSpec((D, CB), lambda i: (0, i)),
        out_shape=jax.ShapeDtypeStruct((D, D), BF16),
        compiler_params=_params(),
    )(own, land)


def _unshard_w_in(w_all):
    def body(w_ref, wt_ref):
        def ref_rows(lo, hi):
            pieces, r = [], lo
            while r < hi:
                sh, off = divmod(r, WSH)
                n = min(hi - r, WSH - off)
                pieces.append(w_ref[sh, off:off + n, :])
                r += n
            return pieces

        for s in range(NSEC):
            lo = s * DA if s < 3 else s * DA + H
            wt_ref[s * DA:(s + 1) * DA, :] = jnp.concatenate(ref_rows(lo, lo + DA), axis=0)
        wt_ref[NSEC * DA:DPROJ, :] = jnp.concatenate(
            ref_rows(F0, F0 + H) + [jnp.zeros((DF - H, CB), BF16)], axis=0)

    return pl.pallas_call(
        body, name="unshard_w_in", grid=(D // CB,),
        in_specs=[pl.BlockSpec((NDEV, WSHP, CB), lambda i: (0, 0, i))],
        out_specs=pl.BlockSpec((DPROJ, CB), lambda i: (0, i)),
        out_shape=jax.ShapeDtypeStruct((DPROJ, D), BF16),
        compiler_params=_params(),
    )(w_all)


def _shard_w_in_grads(dw_main, dw_f):
    def body(dm_ref, df_ref, p_ref):
        def ref_rows(lo, hi):
            pieces, r = [], lo
            while r < hi:
                if r < F0:
                    n = min(hi, F0) - r
                    pieces.append(dm_ref[r:r + n, :])
                elif r < F0 + H:
                    n = min(hi, F0 + H) - r
                    pieces.append(df_ref[r - F0:r - F0 + n, :])
                else:
                    n = hi - r
                    pieces.append(dm_ref[r - H:r - H + n, :])
                r += n
            return pieces

        for i in range(NDEV):
            rows = jnp.concatenate(ref_rows(i * WSH, (i + 1) * WSH) + [jnp.zeros((WSHP - WSH, CB), F32)], axis=0)
            p_ref[i] = rows.astype(BF16)

    col = lambda rows: pl.BlockSpec((rows, CB), lambda i: (0, i))
    return pl.pallas_call(
        body, name="shard_w_in_grads", grid=(D // CB,),
        in_specs=[col(NSEC * DA), col(DF)],
        out_specs=pl.BlockSpec((NDEV, WSHP, CB), lambda i: (0, 0, i)),
        out_shape=jax.ShapeDtypeStruct((NDEV, WSHP, D), BF16),
        compiler_params=_params(),
    )(dw_main, dw_f)


def _adamw_small(recv, w, m, v):
    shape = jax.ShapeDtypeStruct((SROWS, TB), F32)

    def body(r_ref, w_ref, m_ref, v_ref, g_out, d_out, m_out, v_out):
        g = _sum_slabs(r_ref, slice(0, SROWS))
        d, mn, vn = _adamw(w_ref[...], g, m_ref[...], v_ref[...])
        g_out[...], d_out[...], m_out[...], v_out[...] = g, d, mn, vn

    return pl.pallas_call(body, name="adamw_small", out_shape=[shape] * 4)(recv, w, m, v)


def _tile_rows(a, rows, lanes=TB):
    a = a.reshape(rows, lanes)
    return jnp.pad(a, ((0, -rows % 8), (0, TB - lanes)))


def _pack_small(norm_g, final_norm_g, attn_norm_g, conv_norm_g, b_f, meta_sh, conv_w_sh, loss=None):
    b_row = b_f.reshape(1, H) if loss is None else jnp.concatenate([b_f.reshape(1, H), loss.reshape(1, 1)], axis=1)
    packed = jnp.concatenate([
        _tile_rows(norm_g, 8), _tile_rows(final_norm_g, 8), _tile_rows(attn_norm_g, 4), _tile_rows(conv_norm_g, 4),
        _tile_rows(b_row, 1, b_row.shape[1]), _tile_rows(meta_sh, NM), _tile_rows(conv_w_sh, 3, DH)], axis=0)
    assert packed.shape == (SROWS, TB)
    return packed


def _unpack_small(p):
    return dict(
        norm_g=p[0:8].reshape(1, D), final_norm_g=p[8:16].reshape(D), attn_norm_g=p[16:20].reshape(1, DA),
        conv_norm_g=p[24:28].reshape(1, DA), b_f=p[32:33, :H].reshape(1, H), meta=p[40:56].reshape(NM, TB),
        conv_w=p[56:59, :DH].reshape(1, 3, DH))


def kernel(x, meta, norm_g, w_in, b_f, conv_w, attn_norm_g, conv_norm_g, w_out, final_norm_g, loss_target, m_meta, m_norm_g, m_w_in, m_b_f, m_conv_w, m_attn_norm_g, m_conv_norm_g, m_w_out, m_final_norm_g, v_meta, v_norm_g, v_w_in, v_b_f, v_conv_w, v_attn_norm_g, v_conv_norm_g, v_w_out, v_final_norm_g):
    seq = x.shape[1]
    L = seq + TB
    assert x.shape == (1, seq, D) and L % TT == 0 and w_in.shape == (1, D, WSH)
    x2 = x[0]
    tgt = loss_target[0]

    w_in_slab = jnp.pad(w_in[0].T, ((0, WSHP - WSH), (0, 0))).astype(BF16)
    w_out_slab = w_out[0].astype(BF16)
    small = jnp.concatenate([meta, _tile_rows(conv_w[0], 3, DH)], axis=0)
    wout_flight = _split_start(w_out_slab, "gather_w_out_start", per_peer=False)
    w_all = _all_gather(w_in_slab, "gather_w_in")
    small_all = _all_gather(small, "gather_small")

    w_t = _unshard_w_in(w_all)
    meta_full = jnp.transpose(small_all[:, :NM, :], (1, 0, 2)).reshape(NM, D)
    conv_w_full = jnp.transpose(small_all[:, NM:NM + 3, :DH], (1, 0, 2)).reshape(3, DA)

    lane_b = lambda p: jnp.broadcast_to(p.reshape(-1, DA, 1), (p.size // DA, DA, TB))
    cw_b = lane_b(conv_w_full)
    ga_b = lane_b(attn_norm_g)[0]
    gcn_b = lane_b(conv_norm_g)[0]

    u, proj_t, f_t, ktok, vtok = _inproj_fwd(x2, meta_full, norm_g + wout_flight[4][0, 0], w_t, L)
    cq, kaug, sg = _fgate_fwd(f_t, b_f.reshape(H, 1), ktok, L)
    o_t, lse = _attn_fwd(proj_t, kaug, cq, L)
    mix_t = _gate_fwd(o_t, proj_t, cw_b, ga_b, gcn_b, L)

    w_out_own, w_out_land = _split_wait(wout_flight, mix_t, "gather_w_out_wait", per_peer=False)
    w_out_full = _unshard_w_out(w_out_own, w_out_land)
    dout, dmix_t, dw_out, loss_part, dg_final = _outproj(
        mix_t, w_out_full, x2, meta_full, final_norm_g.reshape(1, D), tgt, L)
    dwo_flight = _split_start(dw_out.reshape(NDEV, D // NDEV, D), "exchange_dw_out_start", per_peer=True)
    do_t, dd, dg5_t, dga_p, dgc_p, dcw_p = _gate_bwd(dmix_t, o_t, proj_t, cw_b, ga_b + dwo_flight[4][0, 0], gcn_b, L)
    dq_t, dk_t, dv_t, dck, dcq = _attn_bwd(proj_t, kaug, vtok, do_t, lse, dd, cq, L)
    df_t, db_f = _fgate_bwd(dcq, dck, sg, L)
    dw_main, dw_f = _inproj_bwd_w(u, dq_t, dk_t, dv_t, dg5_t, df_t, L)
    dwi_flight = _split_start(_shard_w_in_grads(dw_main, dw_f), "exchange_dw_in_start", per_peer=True)
    grad_x, dmeta, dg_norm = _inproj_bwd_x(
        w_t, dq_t, dk_t, dv_t, dg5_t, df_t, dout, x2, meta_full, norm_g + dwi_flight[4][0, 0], L)
    dga = jnp.sum(dga_p, axis=1)
    dgc = jnp.sum(dgc_p, axis=1)
    dcw = jnp.sum(dcw_p, axis=2)
    small_parts = jnp.stack([
        _pack_small(dg_norm, dg_final, dga, dgc, db_f, dmeta[:, j * TB:(j + 1) * TB], dcw[:, j * DH:(j + 1) * DH],
                    loss=loss_part)
        for j in range(NDEV)], axis=0)
    small_recv = _exchange(small_parts, "exchange_small_grads")
    dwo_own, dwo_land = _split_wait(dwo_flight, small_recv, "exchange_dw_out_wait", per_peer=True)
    dwi_own, dwi_land = _split_wait(dwi_flight, dwo_land, "exchange_dw_in_wait", per_peer=True)

    big_out = _adamw_big(dwi_own, dwi_land, dwo_own, dwo_land,
                         w_in[0].T, m_w_in[0].T, v_w_in[0].T, w_out, m_w_out, v_w_out)
    g_w_in, d_w_in, nm_w_in, nv_w_in = [a.T[None] for a in big_out[:4]]
    g_w_out, d_w_out, nm_w_out, nv_w_out = big_out[4:]
    wp = _pack_small(norm_g, final_norm_g, attn_norm_g, conv_norm_g, b_f, meta, conv_w)
    mp = _pack_small(m_norm_g, m_final_norm_g, m_attn_norm_g, m_conv_norm_g, m_b_f, m_meta, m_conv_w)
    vp = _pack_small(v_norm_g, v_final_norm_g, v_attn_norm_g, v_conv_norm_g, v_b_f, v_meta, v_conv_w)
    small_out = _adamw_small(small_recv, wp, mp, vp)
    sm = [_unpack_small(p) for p in small_out]
    loss = small_out[0][32, H]
    order = ("meta", "norm_g", "w_in", "b_f", "conv_w", "attn_norm_g", "conv_norm_g", "w_out", "final_norm_g")
    groups = []
    for k, (wi, wo) in enumerate(((g_w_in, g_w_out), (d_w_in, d_w_out), (nm_w_in, nm_w_out), (nv_w_in, nv_w_out))):
        d = dict(sm[k], w_in=wi, w_out=wo)
        groups.append([d[n] for n in order])
    return (loss, grad_x[None], *groups[0], *groups[1], *groups[2], *groups[3])
```

```python
import functools

import jax
import jax.numpy as jnp
from jax import lax
from jax.experimental import pallas as pl
from jax.experimental.pallas import tpu as pltpu

F32 = jnp.float32
BF16 = jnp.bfloat16

D = 1024
DA = 512
H = 8
DH = 64
NM = 16
TB = 128
P0 = TB - NM
TT = 3 * TB
HG = 8
NDEV = 8
NSEC = 8
DF = 16
DPROJ = NSEC * DA + DF
WSH = 513
WSHP = 528
WROWS = WSHP + D // NDEV
SROWS = 64
EPS = 1e-6
NEG = -1e30
LOG2E = 1.4426950408889634
LN2 = 0.6931471805599453
QSCALE = DH ** -0.5 * LOG2E
KA = 128
CB = 256
VMEM_LIMIT = 56 * 1024 * 1024

ADAM_LR = 0.001
ADAM_B1 = 0.9
ADAM_B2 = 0.999
ADAM_EPS = 1e-08
ADAM_WD = 0.01
ADAM_STEP = 10

NT_DIMS = (((1,), (1,)), ((), ()))
TN_DIMS = (((0,), (0,)), ((), ()))
MESH = pl.DeviceIdType.MESH


def _params(n_axes=1, vmem=VMEM_LIMIT):
    return pltpu.CompilerParams(dimension_semantics=("arbitrary",) * n_axes, vmem_limit_bytes=vmem)


def _dot(a, b, dims=None):
    if dims is None:
        return jnp.dot(a, b, preferred_element_type=F32)
    return lax.dot_general(a, b, dims, preferred_element_type=F32)


def _my_place():
    return lax.axis_index("x"), lax.axis_index("y"), lax.axis_index("c")


def _all_gather(x, name):
    def body(x_ref, out_ref, send_sems, recv_sems, local_sem):
        mx, my, mc = _my_place()
        me, sibling = (mx, my, mc), (mx, my, 1 - mc)
        chips = [(1 - mx, my), (mx, 1 - my), (1 - mx, 1 - my)]

        def slot(px, py, pc):
            return out_ref.at[4 * px + 2 * py + pc]

        def copy(k, block, to, src=None):
            return pltpu.make_async_remote_copy(
                src_ref=slot(*block) if src is None else src, dst_ref=slot(*block),
                send_sem=send_sems.at[k], recv_sem=recv_sems.at[k], device_id=to, device_id_type=MESH)

        mine = pltpu.make_async_copy(x_ref, slot(*me), local_sem)
        mine.start()
        first = [copy(0, me, sibling, src=x_ref)]
        first += [copy(1 + j, me, (*chip, mc), src=x_ref) for j, chip in enumerate(chips)]
        for cp in first:
            cp.start()
        passed = [copy(4 + j, (*chip, mc), sibling) for j, chip in enumerate(chips)]
        for j, chip in enumerate(chips):
            copy(1 + j, (*chip, mc), me).wait_recv()
            passed[j].start()
        copy(0, sibling, me).wait_recv()
        for j, chip in enumerate(chips):
            copy(4 + j, (*chip, 1 - mc), me).wait_recv()
        for cp in first + passed:
            cp.wait_send()
        mine.wait()

    return pl.pallas_call(
        body, name=name,
        out_shape=jax.ShapeDtypeStruct((NDEV,) + x.shape, x.dtype),
        in_specs=[pl.BlockSpec(memory_space=pl.ANY)],
        out_specs=pl.BlockSpec(memory_space=pl.ANY),
        scratch_shapes=[pltpu.SemaphoreType.DMA((7,)), pltpu.SemaphoreType.DMA((7,)), pltpu.SemaphoreType.DMA],
    )(x)


def _exchange(parts, name):
    def body(p_ref, out_ref, send_sems, recv_sems, local_sem):
        mx, my, mc = _my_place()
        me = 4 * mx + 2 * my + mc
        mine = pltpu.make_async_copy(p_ref.at[me], out_ref.at[me], local_sem)
        mine.start()

        def peer_of(m):
            return ((1 - mx) if m & 4 else mx, (1 - my) if m & 2 else my, (1 - mc) if m & 1 else mc)

        def copy(m, src_slot, dst_slot):
            px, py, pc = peer_of(m)
            return pltpu.make_async_remote_copy(
                src_ref=p_ref.at[src_slot], dst_ref=out_ref.at[dst_slot],
                send_sem=send_sems.at[m - 1], recv_sem=recv_sems.at[m - 1],
                device_id=(px, py, pc), device_id_type=MESH)

        sends = []
        for m in range(1, NDEV):
            px, py, pc = peer_of(m)
            cp = copy(m, 4 * px + 2 * py + pc, me)
            cp.start()
            sends.append(cp)
        for m in range(1, NDEV):
            px, py, pc = peer_of(m)
            copy(m, me, 4 * px + 2 * py + pc).wait_recv()
        for cp in sends:
            cp.wait_send()
        mine.wait()

    return pl.pallas_call(
        body, name=name,
        out_shape=jax.ShapeDtypeStruct(parts.shape, parts.dtype),
        in_specs=[pl.BlockSpec(memory_space=pl.ANY)],
        out_specs=pl.BlockSpec(memory_space=pl.ANY),
        scratch_shapes=[pltpu.SemaphoreType.DMA((7,)), pltpu.SemaphoreType.DMA((7,)), pltpu.SemaphoreType.DMA],
    )(parts)


_HBM = pl.BlockSpec(memory_space=pltpu.HBM)
_SEM = pl.BlockSpec(memory_space=pltpu.SEMAPHORE)
_EFFECT = pltpu.SideEffectType.DATAFLOW_SIDE_EFFECTING


def _peer_of(m, place):
    mx, my, mc = place
    return ((1 - mx) if m & 4 else mx, (1 - my) if m & 2 else my, (1 - mc) if m & 1 else mc)


def _split_copies(src_ref, land_ref, send_sems, recv_sems, per_peer, incoming):
    place = _my_place()
    me = 4 * place[0] + 2 * place[1] + place[2]
    out = []
    for m in range(1, NDEV):
        px, py, pc = _peer_of(m, place)
        peer = 4 * px + 2 * py + pc
        src = (src_ref.at[me] if incoming else src_ref.at[peer]) if per_peer else src_ref
        out.append(pltpu.make_async_remote_copy(
            src_ref=src, dst_ref=land_ref.at[peer if incoming else me],
            send_sem=send_sems.at[m - 1], recv_sem=recv_sems.at[m - 1],
            device_id=(px, py, pc), device_id_type=MESH))
    return out


def _split_start(src, name, per_peer):
    slab = src.shape[1:] if per_peer else src.shape

    def body(src_ref, land_ref, send_sems, recv_sems, src_thru, land_thru, token):
        for cp in _split_copies(src_ref, land_ref, send_sems, recv_sems, per_peer, incoming=False):
            cp.start()
        token[...] = jnp.zeros_like(token)

    return pl.pallas_call(
        body, name=name,
        out_shape=(pltpu.SemaphoreType.DMA((NDEV - 1,)), pltpu.SemaphoreType.DMA((NDEV - 1,)),
                   pltpu.HBM(src.shape, src.dtype), pltpu.HBM((NDEV,) + slab, src.dtype),
                   jax.ShapeDtypeStruct((8, TB), F32)),
        in_specs=(_HBM, _HBM), out_specs=(_SEM, _SEM, _HBM, _HBM, pl.BlockSpec(memory_space=pltpu.VMEM)),
        input_output_aliases={0: 2, 1: 3},
        compiler_params=pltpu.CompilerParams(has_side_effects=_EFFECT),
    )(pltpu.with_memory_space_constraint(src, pltpu.HBM),
      pltpu.with_memory_space_constraint(lax.empty((NDEV,) + slab, src.dtype), pltpu.HBM))


def _split_wait(handles, after, name, per_peer):
    send_sems, recv_sems, src_thru, land_thru, _ = handles

    def body(src_ref, land_ref, send_sems, recv_sems, after_ref, src_out, land_out):
        for cp in _split_copies(src_ref, land_ref, send_sems, recv_sems, per_peer, incoming=False):
            cp.wait_send()
        for cp in _split_copies(src_ref, land_ref, send_sems, recv_sems, per_peer, incoming=True):
            cp.wait_recv()

    return pl.pallas_call(
        body, name=name,
        out_shape=(pltpu.HBM(src_thru.shape, src_thru.dtype), pltpu.HBM(land_thru.shape, land_thru.dtype)),
        in_specs=(_HBM, _HBM, _SEM, _SEM, pl.BlockSpec(memory_space=pl.ANY)), out_specs=(_HBM, _HBM),
        input_output_aliases={0: 0, 1: 1},
        compiler_params=pltpu.CompilerParams(has_side_effects=_EFFECT),
    )(src_thru, land_thru, send_sems, recv_sems, after)


def _pick_slab(j, own_ref, land_ref, rows, per_peer=True):
    mx, my, mc = _my_place()
    me = 4 * mx + 2 * my + mc
    own = (lambda: own_ref[j, rows, :]) if per_peer else (lambda: own_ref[rows, :])
    return lax.cond(me == j, own, lambda: land_ref[j, rows, :])


def _h_block(t, x_ref, meta_ref):
    first = jnp.concatenate([jnp.zeros((P0, D), F32), meta_ref[...]], axis=0)
    return jnp.where(t == 0, first, x_ref[...])


def _x_spec():
    return pl.BlockSpec((TB, D), lambda t: (jnp.maximum(t - 1, 0), 0))


def _x_specs3():
    return [pl.BlockSpec((TB, D), lambda j: (jnp.maximum(3 * j - 1, 0), 0)),
            pl.BlockSpec((TB, D), lambda j: (3 * j, 0)),
            pl.BlockSpec((TB, D), lambda j: (3 * j + 1, 0))]


def _h_tile(j, xa_ref, xb_ref, xc_ref, meta_ref):
    first = jnp.concatenate([jnp.zeros((P0, D), F32), meta_ref[...]], axis=0)
    return jnp.concatenate([jnp.where(j == 0, first, xa_ref[...]), xb_ref[...], xc_ref[...]], axis=0)


def _full_spec(shape):
    return pl.BlockSpec(shape, lambda *_: (0,) * len(shape))


def _sigmoid(z):
    return 1.0 / (1.0 + jnp.exp(-z))


def _grouped(x):
    return x.reshape(H, DH, x.shape[-1])


def _group_rstd(x3):
    return lax.rsqrt(jnp.mean(x3 * x3, axis=1, keepdims=True) + EPS)


def _lane_tiles_sum(x):
    out = x[:, :TB]
    for i in range(1, x.shape[1] // TB):
        out = out + x[:, i * TB:(i + 1) * TB]
    return out


def _inproj_fwd(x, meta_full, norm_g, w_t, L):
    nj = L // TT

    def body(xa_ref, xb_ref, xc_ref, meta_ref, g_ref, w_ref, u_ref, proj_ref, f_ref, ktok_ref, vtok_ref):
        hb = _h_tile(pl.program_id(0), xa_ref, xb_ref, xc_ref, meta_ref)
        r = lax.rsqrt(jnp.mean(hb * hb, axis=-1, keepdims=True) + EPS)
        u = (hb * r * g_ref[...]).astype(BF16)
        u_ref[...] = u
        for s in range(NSEC):
            p = _dot(u, w_ref[s * DA:(s + 1) * DA, :], NT_DIMS)
            if s == 0:
                p = p * QSCALE
            if s in (1, 2):
                tok_ref = ktok_ref if s == 1 else vtok_ref
                for h in range(H):
                    tok_ref[h] = p[:, h * DH:(h + 1) * DH].astype(BF16)
            proj_ref[s * DA:(s + 1) * DA, :] = p.T.astype(BF16)
        f_ref[...] = _dot(w_ref[NSEC * DA:DPROJ, :], u, NT_DIMS)[:H]

    return pl.pallas_call(
        body, name="inproj_fwd", grid=(nj,),
        in_specs=_x_specs3() + [_full_spec((NM, D)), _full_spec((1, D)), _full_spec((DPROJ, D))],
        out_specs=[
            pl.BlockSpec((TT, D), lambda t: (t, 0)),
            pl.BlockSpec((NSEC * DA, TT), lambda t: (0, t)),
            pl.BlockSpec((H, TT), lambda t: (0, t)),
            pl.BlockSpec((H, TT, DH), lambda t: (0, t, 0)),
            pl.BlockSpec((H, TT, DH), lambda t: (0, t, 0)),
        ],
        out_shape=[
            jax.ShapeDtypeStruct((L, D), BF16),
            jax.ShapeDtypeStruct((NSEC * DA, L), BF16),
            jax.ShapeDtypeStruct((H, L), F32),
            jax.ShapeDtypeStruct((H, L, DH), BF16),
            jax.ShapeDtypeStruct((H, L, DH), BF16),
        ],
        compiler_params=_params(),
    )(x, x, x, meta_full, norm_g, w_t)


def _split3(x):
    hi = x.astype(BF16).astype(F32)
    r = x - hi
    mid = r.astype(BF16).astype(F32)
    return hi, mid, (r - mid).astype(BF16).astype(F32)


def _bias_rows(bias):
    one = jnp.ones((1, TT), F32)
    zero = jnp.zeros((1, TT), F32)
    parts = [zero] * 3 if bias is None else list(_split3(bias))
    return jnp.concatenate([one] * 3 + parts + [zero] * (DF - 6), axis=0).astype(BF16)


def _fgate_fwd(f_t, b_col, ktok, L):
    nb = L // TB

    def body(f_ref, b_ref, ktok_ref, cq_ref, kaug_ref, sg_ref):
        z = f_ref[...] + b_ref[...]
        idx = lax.broadcasted_iota(jnp.int32, (H, L), 1)
        real = idx >= P0
        lf = jnp.where(real, jnp.minimum(z, 0.0) - jnp.log1p(jnp.exp(-jnp.abs(z))), 0.0)
        sg_ref[...] = jnp.where(real, 1.0 / (1.0 + jnp.exp(z)), 0.0)
        c = lf
        s = 1
        while s < L:
            c = c + jnp.where(idx >= s, pltpu.roll(c, s, 1), 0.0)
            s *= 2
        c = c * LOG2E
        for h in range(H):
            cq_ref[h] = c[h:h + 1, :]
        ck = jnp.where(real, c, -NEG)
        lane = lax.broadcasted_iota(jnp.int32, (TB, KA), 1)
        tail = jnp.where((lane >= DH + 3) & (lane < DH + 6), 1.0, 0.0)
        for h in range(H):
            for b in range(nb):
                blk = slice(b * TB, (b + 1) * TB)
                col = jnp.broadcast_to(ck[h:h + 1, blk], (TB, TB)).T
                hi, mid, lo = _split3(-col)
                k = jnp.concatenate([ktok_ref[h, blk, :].astype(F32), jnp.zeros((TB, KA - DH), F32)], axis=1)
                out = jnp.where(lane < DH, k, jnp.where(lane == DH, hi, jnp.where(
                    lane == DH + 1, mid, jnp.where(lane == DH + 2, lo, tail))))
                kaug_ref[h, blk, :] = out.astype(BF16)

    return pl.pallas_call(
        body, name="fgate_fwd",
        out_shape=[
            jax.ShapeDtypeStruct((H, 1, L), F32),
            jax.ShapeDtypeStruct((H, L, KA), BF16),
            jax.ShapeDtypeStruct((H, L), F32),
        ],
        compiler_params=pltpu.CompilerParams(vmem_limit_bytes=VMEM_LIMIT),
    )(f_t, b_col, ktok)


def _causal_mask():
    r = lax.broadcasted_iota(jnp.int32, (TT, TT), 0)
    c = lax.broadcasted_iota(jnp.int32, (TT, TT), 1)
    return r <= c


def _attn_fwd(proj_t, kaug, cq, L):
    nq = L // TT

    def body(q_ref, kaug_ref, v_ref, cq_ref, o_ref, lse_ref,
             qa_scr, s_scr, cmax_scr, m_scr, p_scr, alpha_scr, acc_scr):
        j = pl.program_id(1)
        rows = [slice(g * DH, (g + 1) * DH) for g in range(HG)]
        ones = jnp.ones((DF, TT), BF16)
        for g in range(HG):
            qa_scr[g] = jnp.concatenate(
                [q_ref[rows[g], :], _bias_rows(None), jnp.zeros((KA - DH - DF, TT), BF16)], axis=0)

        def scores(kt, masked):
            k_off = pl.multiple_of(kt * TT, TT)
            for g in range(HG):
                s = _dot(kaug_ref[g, pl.ds(k_off, TT), :], qa_scr[g])
                if masked:
                    s = jnp.where(_causal_mask(), s, NEG)
                s_scr[g] = s
                cmax_scr[g] = jnp.max(s, axis=0, keepdims=True)

        def softmax():
            for g in range(HG):
                m_old = m_scr[g]
                m_new = jnp.maximum(m_old, cmax_scr[g])
                alpha_scr[g] = jnp.exp2(m_old - m_new)
                p_scr[g] = jnp.exp2(s_scr[g] - m_new).astype(BF16)
                m_scr[g] = m_new

        def weighted_sum(kt):
            k_off = pl.multiple_of(kt * TT, TT)
            for g in range(HG):
                v1 = jnp.concatenate([v_ref[rows[g], pl.ds(k_off, TT)], ones], axis=0)
                acc_scr[g] = alpha_scr[g] * acc_scr[g] + _dot(v1, p_scr[g])

        m_scr[...] = jnp.full_like(m_scr, NEG)
        acc_scr[...] = jnp.zeros_like(acc_scr)

        scores(j, True)

        @pl.when(j >= 1)
        def _():
            softmax()
            scores(j - 1, False)

        def step(i, c):
            weighted_sum(j - i + 1)
            softmax()
            scores(j - i - 1, False)
            return c

        lax.fori_loop(1, j, step, 0)

        @pl.when(j >= 1)
        def _():
            weighted_sum(1)

        softmax()
        weighted_sum(0)
        for g in range(HG):
            l = acc_scr[g, DH:DH + 1, :]
            o_ref[rows[g], :] = acc_scr[g, :DH, :] * (1.0 / l)
            lse_ref[g] = m_scr[g] + jnp.log2(l) + cq_ref[g]

    return pl.pallas_call(
        body, name="attn_fwd", grid=(H // HG, nq),
        in_specs=[
            pl.BlockSpec((HG * DH, TT), lambda h, j: (h, j)),
            pl.BlockSpec((HG, L, KA), lambda h, j: (h, 0, 0)),
            pl.BlockSpec((HG * DH, L), lambda h, j: (2 * H // HG + h, 0)),
            pl.BlockSpec((HG, 1, TT), lambda h, j: (h, 0, j)),
        ],
        out_specs=[
            pl.BlockSpec((HG * DH, TT), lambda h, j: (h, j)),
            pl.BlockSpec((HG, 1, TT), lambda h, j: (h, 0, j)),
        ],
        out_shape=[jax.ShapeDtypeStruct((DA, L), F32), jax.ShapeDtypeStruct((H, 1, L), F32)],
        scratch_shapes=[pltpu.VMEM((HG, KA, TT), BF16), pltpu.VMEM((HG, TT, TT), F32), pltpu.VMEM((HG, 1, TT), F32),
                        pltpu.VMEM((HG, 1, TT), F32), pltpu.VMEM((HG, TT, TT), BF16), pltpu.VMEM((HG, 1, TT), F32),
                        pltpu.VMEM((HG, DH + DF, TT), F32)],
        compiler_params=_params(2),
    )(proj_t, kaug, proj_t, cq)


def _gate_common(o, za, gb, gc, xc, zc, gcp, xcp, cw_ref, ga_ref, gcn_ref, first):
    n_rep = TT // TB
    a = gc * xc
    a_prev = jnp.where(first, 0.0, gcp * xcp)
    full = jnp.concatenate([a_prev, a], axis=1)
    a1 = pltpu.roll(full, 1, 1)[:, TB:]
    a2 = pltpu.roll(full, 2, 1)[:, TB:]
    w0 = jnp.tile(cw_ref[0], (1, n_rep))
    w1 = jnp.tile(cw_ref[1], (1, n_rep))
    w2 = jnp.tile(cw_ref[2], (1, n_rep))
    cv = w0 * a2 + w1 * a1 + w2 * a
    e = gb * cv
    e3 = _grouped(e)
    rc = _group_rstd(e3)
    ec = (e3 * rc).reshape(DA, TT)
    o3 = _grouped(o)
    ra = _group_rstd(o3)
    oa = (o3 * ra).reshape(DA, TT)
    g_a = jnp.tile(ga_ref[...], (1, n_rep))
    g_c = jnp.tile(gcn_ref[...], (1, n_rep))
    sa = _sigmoid(za)
    sc = _sigmoid(zc)
    return dict(a=a, a1=a1, a2=a2, w0=w0, w1=w1, w2=w2, cv=cv, e=e, rc=rc, ec=ec, ra=ra, oa=oa,
                g_a=g_a, g_c=g_c, sa=sa, sc=sc)


def _gate_specs(nj, rev):
    def jj(i):
        return (nj - 1 - i) if rev else i

    def sec(s):
        return pl.BlockSpec((DA, TT), lambda i: (s, jj(i)))

    def halo(s):
        return pl.BlockSpec((DA, TB), lambda i: (s, jnp.maximum(3 * jj(i) - 1, 0)))

    return [pl.BlockSpec((DA, TT), lambda i: (0, jj(i))), sec(3), sec(4), sec(5), sec(6), sec(7), halo(5), halo(6),
            _full_spec((3, DA, TB)), _full_spec((DA, TB)), _full_spec((DA, TB))]


def _gate_fwd(o_t, proj_t, cw_b, ga_b, gcn_b, L):
    nj = L // TT

    def body(o_ref, za_ref, gb_ref, gc_ref, xc_ref, zc_ref, gcp_ref, xcp_ref, cw_ref, ga_ref, gcn_ref, mix_ref):
        j = pl.program_id(0)
        f32 = lambda r: r[...].astype(F32)
        za, zc = f32(za_ref), f32(zc_ref)
        g = _gate_common(o_ref[...], za, f32(gb_ref), f32(gc_ref), f32(xc_ref), zc, f32(gcp_ref), f32(xcp_ref),
                         cw_ref, ga_ref, gcn_ref, j == 0)
        mix_ref[:DA, :] = (g["oa"] * g["g_a"] * (za * g["sa"])).astype(BF16)
        mix_ref[DA:, :] = (g["ec"] * g["g_c"] * (zc * g["sc"])).astype(BF16)

    return pl.pallas_call(
        body, name="gate_fwd", grid=(nj,),
        in_specs=_gate_specs(nj, False),
        out_specs=pl.BlockSpec((2 * DA, TT), lambda j: (0, j)),
        out_shape=jax.ShapeDtypeStruct((2 * DA, L), BF16),
        compiler_params=_params(),
    )(o_t, proj_t, proj_t, proj_t, proj_t, proj_t, proj_t, proj_t, cw_b, ga_b, gcn_b)


def _outproj(mix_t, w_out, x, meta_full, fng, target, L):
    nj = L // TT

    def body(mix_ref, w_ref, xa_ref, xb_ref, xc_ref, meta_ref, g_ref, ta_ref, tb_ref, tc_ref,
             dout_ref, dmix_ref, dwb_ref, loss_ref, dg_ref, dw_ref):
        t = pl.program_id(0)

        @pl.when(t == 0)
        def _():
            dw_ref[...] = jnp.zeros_like(dw_ref)
            loss_ref[...] = jnp.zeros_like(loss_ref)
            dg_ref[...] = jnp.zeros_like(dg_ref)

        mix = mix_ref[...]
        o = _dot(mix, w_ref[...], TN_DIMS) + _h_tile(t, xa_ref, xb_ref, xc_ref, meta_ref)
        r = lax.rsqrt(jnp.mean(o * o, axis=-1, keepdims=True) + EPS)
        g = g_ref[...]
        orn = o * r
        tgt = jnp.concatenate([ta_ref[...], tb_ref[...], tc_ref[...]], axis=0)
        row = lax.broadcasted_iota(jnp.int32, (TT, 1), 0)
        real = jnp.where((t > 0) | (row >= TB), 1.0, 0.0)
        diff = (orn * g - tgt) * real
        loss_ref[...] += 0.5 * jnp.sum(diff * diff) * (1.0 / D)
        dy = diff * (1.0 / D)
        dg_ref[...] += jnp.sum(dy * orn, axis=0, keepdims=True)
        gy = dy * g
        dout = r * gy - orn * (r * jnp.mean(gy * orn, axis=-1, keepdims=True))
        dout_ref[...] = dout
        db = dout.astype(BF16)
        dmix_ref[...] = _dot(db, w_ref[...], NT_DIMS).T.astype(BF16)
        dw_ref[...] += _dot(mix, db)

        @pl.when(t == nj - 1)
        def _():
            dwb_ref[...] = dw_ref[...].astype(BF16)

    return pl.pallas_call(
        body, name="outproj", grid=(nj,),
        in_specs=[pl.BlockSpec((D, TT), lambda t: (0, t)), _full_spec((D, D))] + _x_specs3()
                 + [_full_spec((NM, D)), _full_spec((1, D))] + _x_specs3(),
        out_specs=[pl.BlockSpec((TT, D), lambda t: (t, 0)), pl.BlockSpec((D, TT), lambda t: (0, t)),
                   _full_spec((D, D)), _full_spec((1, 1)), _full_spec((1, D))],
        out_shape=[jax.ShapeDtypeStruct((L, D), F32), jax.ShapeDtypeStruct((D, L), BF16),
                   jax.ShapeDtypeStruct((D, D), BF16), jax.ShapeDtypeStruct((1, 1), F32),
                   jax.ShapeDtypeStruct((1, D), F32)],
        scratch_shapes=[pltpu.VMEM((D, D), F32)],
        compiler_params=_params(),
    )(mix_t, w_out, x, x, x, meta_full, fng, target, target, target)


def _gate_bwd(dmix_t, o_t, proj_t, cw_b, ga_b, gcn_b, L):
    nj = L // TT

    def body(dmix_ref, o_ref, za_ref, gb_ref, gc_ref, xc_ref, zc_ref, gcp_ref, xcp_ref, cw_ref, ga_ref, gcn_ref,
             do_ref, dd_ref, dg5_ref, dga_ref, dgc_ref, dcw_ref, carry_ref):
        i = pl.program_id(0)
        j = nj - 1 - i

        @pl.when(i == 0)
        def _():
            carry_ref[...] = jnp.zeros_like(carry_ref)
            dga_ref[...] = jnp.zeros_like(dga_ref)
            dgc_ref[...] = jnp.zeros_like(dgc_ref)
            dcw_ref[...] = jnp.zeros_like(dcw_ref)

        f32 = lambda r: r[...].astype(F32)
        o, za, gb, gc, xc, zc = o_ref[...], f32(za_ref), f32(gb_ref), f32(gc_ref), f32(xc_ref), f32(zc_ref)
        g = _gate_common(o, za, gb, gc, xc, zc, f32(gcp_ref), f32(xcp_ref), cw_ref, ga_ref, gcn_ref, j == 0)
        dya = dmix_ref[:DA, :].astype(F32)
        dyc = dmix_ref[DA:, :].astype(F32)
        sa, sc = g["sa"], g["sc"]

        dn = dya * (za * sa)
        dg5_ref[0:DA, :] = (dya * (g["oa"] * g["g_a"]) * (sa * (1.0 + za * (1.0 - sa)))).astype(BF16)
        dga_ref[...] += _lane_tiles_sum(dn * g["oa"])
        dng = dn * g["g_a"]
        mean_a = jnp.mean(_grouped(dng * g["oa"]), axis=1, keepdims=True)
        do = ((_grouped(dng) - _grouped(g["oa"]) * mean_a) * g["ra"]).reshape(DA, TT)
        do_ref[...] = do.astype(BF16)
        dd = jnp.sum(_grouped(do * o), axis=1)
        for h in range(H):
            dd_ref[h] = dd[h:h + 1, :]

        dnc = dyc * (zc * sc)
        dg5_ref[4 * DA:5 * DA, :] = (dyc * (g["ec"] * g["g_c"]) * (sc * (1.0 + zc * (1.0 - sc)))).astype(BF16)
        dgc_ref[...] += _lane_tiles_sum(dnc * g["ec"])
        dncg = dnc * g["g_c"]
        mean_c = jnp.mean(_grouped(dncg * g["ec"]), axis=1, keepdims=True)
        de = ((_grouped(dncg) - _grouped(g["ec"]) * mean_c) * g["rc"]).reshape(DA, TT)
        dg5_ref[DA:2 * DA, :] = (de * g["cv"]).astype(BF16)
        dcv = de * gb
        full = jnp.concatenate([dcv, carry_ref[...]], axis=1)
        d1 = pltpu.roll(full, TT + TB - 1, 1)[:, :TT]
        d2 = pltpu.roll(full, TT + TB - 2, 1)[:, :TT]
        carry_ref[...] = dcv[:, :TB]
        da = g["w2"] * dcv + g["w1"] * d1 + g["w0"] * d2
        dg5_ref[2 * DA:3 * DA, :] = (da * xc).astype(BF16)
        dg5_ref[3 * DA:4 * DA, :] = (da * gc).astype(BF16)
        dcw_ref[0] += _lane_tiles_sum(dcv * g["a2"])
        dcw_ref[1] += _lane_tiles_sum(dcv * g["a1"])
        dcw_ref[2] += _lane_tiles_sum(dcv * g["a"])

    rj = lambda i: nj - 1 - i
    return pl.pallas_call(
        body, name="gate_bwd", grid=(nj,),
        in_specs=[pl.BlockSpec((2 * DA, TT), lambda i: (0, rj(i)))] + _gate_specs(nj, True),
        out_specs=[
            pl.BlockSpec((DA, TT), lambda i: (0, rj(i))),
            pl.BlockSpec((H, 1, TT), lambda i: (0, 0, rj(i))),
            pl.BlockSpec((5 * DA, TT), lambda i: (0, rj(i))),
            _full_spec((DA, TB)), _full_spec((DA, TB)), _full_spec((3, DA, TB)),
        ],
        out_shape=[
            jax.ShapeDtypeStruct((DA, L), BF16),
            jax.ShapeDtypeStruct((H, 1, L), F32),
            jax.ShapeDtypeStruct((5 * DA, L), BF16),
            jax.ShapeDtypeStruct((DA, TB), F32),
            jax.ShapeDtypeStruct((DA, TB), F32),
            jax.ShapeDtypeStruct((3, DA, TB), F32),
        ],
        scratch_shapes=[pltpu.VMEM((DA, TB), F32)],
        compiler_params=_params(),
    )(dmix_t, o_t, proj_t, proj_t, proj_t, proj_t, proj_t, proj_t, proj_t, cw_b, ga_b, gcn_b)


def _attn_bwd(proj_t, kaug, vtok, do_t, lse, dd, cq, L):
    nk = L // TT

    def body(q_ref, kaug_ref, vtok_ref, kt_ref, do_ref, lse_ref, dd_ref, cq_ref,
             dq_ref, dk_ref, dv_ref, dck_ref, dcq_ref, dq_acc, kt1_scr, s_scr, dp_scr, dv_scr, dk_scr):
        i = pl.program_id(1)

        @pl.when(i == 0)
        def _():
            dq_acc[...] = jnp.zeros_like(dq_acc)

        rows = [slice(g * DH, (g + 1) * DH) for g in range(HG)]
        ones = jnp.ones((DF, TT), BF16)
        zpad = jnp.zeros((KA - DH - DF, TT), BF16)
        for g in range(HG):
            kt1_scr[g] = jnp.concatenate([kt_ref[rows[g], :], ones], axis=0)
        dv_scr[...] = jnp.zeros_like(dv_scr)
        dk_scr[...] = jnp.zeros_like(dk_scr)

        def q_rows(g, q_off):
            bias = cq_ref[g, :, pl.ds(q_off, TT)] - lse_ref[g, :, pl.ds(q_off, TT)]
            return jnp.concatenate([q_ref[rows[g], pl.ds(q_off, TT)], _bias_rows(bias)], axis=0)

        def scores(jq, masked):
            q_off = pl.multiple_of(jq * TT, TT)
            for g in range(HG):
                s = _dot(kaug_ref[g], jnp.concatenate([q_rows(g, q_off), zpad], axis=0))
                if masked:
                    s = jnp.where(_causal_mask(), s, NEG)
                s_scr[g] = s
                dp_scr[g] = _dot(vtok_ref[g], do_ref[rows[g], pl.ds(q_off, TT)])

        def grads(jq):
            q_off = pl.multiple_of(jq * TT, TT)
            for g in range(HG):
                p = jnp.exp2(s_scr[g])
                ds = (p * (dp_scr[g] - dd_ref[g, :, pl.ds(q_off, TT)])).astype(BF16)
                do1 = jnp.concatenate([do_ref[rows[g], pl.ds(q_off, TT)], jnp.zeros((KA - DH, TT), BF16)], axis=0)
                q1 = jnp.concatenate([q_rows(g, q_off), zpad], axis=0)
                dv_scr[g] += _dot(p.astype(BF16), do1, NT_DIMS)
                dk_scr[g] += _dot(ds, q1, NT_DIMS)
                dq_acc[g, :, pl.ds(q_off, TT)] += _dot(kt1_scr[g], ds)

        scores(i, True)

        def step(jq, c):
            grads(jq)
            scores(jq + 1, False)
            return c

        lax.fori_loop(i, nk - 1, step, 0)
        grads(nk - 1)
        for g in range(HG):
            dv_ref[rows[g], :] = dv_scr[g].T[:DH, :].astype(BF16)
            dk_t = dk_scr[g].T
            dk_ref[rows[g], :] = (dk_t[:DH, :] * LN2).astype(BF16)
            dck_ref[g] = dk_t[DH:DH + 1, :]

        @pl.when(i == nk - 1)
        def _():
            for g in range(HG):
                dq_ref[rows[g], :] = (dq_acc[g, :DH, :] * (DH ** -0.5)).astype(BF16)
                dcq_ref[g] = dq_acc[g, DH:DH + 1, :]

    head = lambda h, i: (h, 0)
    row = lambda h, i: (h, 0, 0)
    return pl.pallas_call(
        body, name="attn_bwd", grid=(H // HG, nk),
        in_specs=[
            pl.BlockSpec((HG * DH, L), head),
            pl.BlockSpec((HG, TT, KA), lambda h, i: (h, i, 0)),
            pl.BlockSpec((HG, TT, DH), lambda h, i: (h, i, 0)),
            pl.BlockSpec((HG * DH, TT), lambda h, i: (H // HG + h, i)),
            pl.BlockSpec((HG * DH, L), head),
            pl.BlockSpec((HG, 1, L), row), pl.BlockSpec((HG, 1, L), row), pl.BlockSpec((HG, 1, L), row),
        ],
        out_specs=[
            pl.BlockSpec((HG * DH, L), head),
            pl.BlockSpec((HG * DH, TT), lambda h, i: (h, i)),
            pl.BlockSpec((HG * DH, TT), lambda h, i: (h, i)),
            pl.BlockSpec((HG, 1, TT), lambda h, i: (h, 0, i)),
            pl.BlockSpec((HG, 1, L), row),
        ],
        out_shape=[jax.ShapeDtypeStruct((DA, L), BF16), jax.ShapeDtypeStruct((DA, L), BF16),
                   jax.ShapeDtypeStruct((DA, L), BF16), jax.ShapeDtypeStruct((H, 1, L), F32),
                   jax.ShapeDtypeStruct((H, 1, L), F32)],
        scratch_shapes=[
            pltpu.VMEM((HG, DH + DF, L), F32),
            pltpu.VMEM((HG, DH + DF, TT), BF16),
            pltpu.VMEM((HG, TT, TT), F32), pltpu.VMEM((HG, TT, TT), F32),
            pltpu.VMEM((HG, TT, KA), F32), pltpu.VMEM((HG, TT, KA), F32)],
        compiler_params=_params(2),
    )(proj_t, kaug, vtok, proj_t, do_t, lse, dd, cq)


def _fgate_bwd(dcq, dck, sg, L):
    def body(dcq_ref, dck_ref, sg_ref, df_ref, db_ref):
        dc = jnp.concatenate([dcq_ref[h] - dck_ref[h] for h in range(H)], axis=0)
        idx = lax.broadcasted_iota(jnp.int32, (H, L), 1)
        r = dc
        s = 1
        while s < L:
            r = r + jnp.where(idx + s < L, pltpu.roll(r, L - s, 1), 0.0)
            s *= 2
        df = r * sg_ref[...]
        db_ref[...] = jnp.sum(df, axis=1, keepdims=True)
        df_ref[...] = jnp.concatenate([df, jnp.zeros((DF - H, L), F32)], axis=0).astype(BF16)

    return pl.pallas_call(
        body, name="fgate_bwd",
        out_shape=[jax.ShapeDtypeStruct((DF, L), BF16), jax.ShapeDtypeStruct((H, 1), F32)],
        compiler_params=pltpu.CompilerParams(vmem_limit_bytes=VMEM_LIMIT),
    )(dcq, dck, sg)


def _inproj_bwd_x(w, dq_t, dk_t, dv_t, dg5_t, df_t, dout, x, meta_full, norm_g, L):
    nb = L // TB
    seq = x.shape[0]

    def body(w_ref, dq_ref, dk_ref, dv_ref, dg5_ref, df_ref, dout_ref, x_ref, meta_ref, g_ref,
             gx_ref, dmeta_ref, dg_ref):
        t = pl.program_id(0)

        @pl.when(t == 0)
        def _():
            dg_ref[...] = jnp.zeros_like(dg_ref)

        du = _dot(dq_ref[...], w_ref[0:DA, :], TN_DIMS)
        du += _dot(dk_ref[...], w_ref[DA:2 * DA, :], TN_DIMS)
        du += _dot(dv_ref[...], w_ref[2 * DA:3 * DA, :], TN_DIMS)
        du += _dot(dg5_ref[...], w_ref[3 * DA:NSEC * DA, :], TN_DIMS)
        du += _dot(df_ref[...], w_ref[NSEC * DA:DPROJ, :], TN_DIMS)
        hb = _h_block(t, x_ref, meta_ref)
        r = lax.rsqrt(jnp.mean(hb * hb, axis=-1, keepdims=True) + EPS)
        hn = hb * r
        dg_ref[...] += jnp.sum(du * hn, axis=0, keepdims=True)
        gu = du * g_ref[...]
        dh = dout_ref[...] + r * gu - hn * (r * jnp.mean(gu * hn, axis=-1, keepdims=True))
        gx_ref[...] = dh

        @pl.when(t == 0)
        def _():
            dmeta_ref[...] = dh[P0:, :]

    blk = lambda rows: pl.BlockSpec((rows, TB), lambda t: (0, t))
    return pl.pallas_call(
        body, name="inproj_bwd_x", grid=(nb,),
        in_specs=[_full_spec((DPROJ, D)), blk(DA), blk(DA), blk(DA), blk(5 * DA), blk(DF),
                  pl.BlockSpec((TB, D), lambda t: (t, 0)), _x_spec(), _full_spec((NM, D)), _full_spec((1, D))],
        out_specs=[_x_spec(), _full_spec((NM, D)), _full_spec((1, D))],
        out_shape=[jax.ShapeDtypeStruct((seq, D), F32), jax.ShapeDtypeStruct((NM, D), F32),
                   jax.ShapeDtypeStruct((1, D), F32)],
        compiler_params=_params(),
    )(w, dq_t, dk_t, dv_t, dg5_t, df_t, dout, x, meta_full, norm_g)


def _inproj_bwd_w(u, dq_t, dk_t, dv_t, dg5_t, df_t, L):
    kt = L // 3
    nkt = 3

    def body(u_ref, dq_ref, dk_ref, dv_ref, dg5_ref, df_ref, dw_ref, dwf_ref):
        s = pl.program_id(0)
        k = pl.program_id(1)

        @pl.when(k == 0)
        def _():
            dw_ref[...] = jnp.zeros_like(dw_ref)

        @pl.when((s == 0) & (k == 0))
        def _():
            dwf_ref[...] = jnp.zeros_like(dwf_ref)

        u_blk = u_ref[...]
        for sec, ref in ((0, dq_ref), (1, dk_ref), (2, dv_ref)):
            @pl.when(s == sec)
            def _(ref=ref):
                dw_ref[...] += _dot(ref[...], u_blk)

        @pl.when(s >= 3)
        def _():
            dw_ref[...] += _dot(dg5_ref[...], u_blk)

        @pl.when(s == NSEC - 1)
        def _():
            dwf_ref[...] += _dot(df_ref[...], u_blk)

    def only(sec):
        return lambda s, k: (0, jnp.where(s == sec, k, 0))

    return pl.pallas_call(
        body, name="inproj_bwd_w", grid=(NSEC, nkt),
        in_specs=[
            pl.BlockSpec((kt, D), lambda s, k: (k, 0)),
            pl.BlockSpec((DA, kt), only(0)), pl.BlockSpec((DA, kt), only(1)), pl.BlockSpec((DA, kt), only(2)),
            pl.BlockSpec((DA, kt), lambda s, k: (jnp.maximum(s - 3, 0), jnp.where(s >= 3, k, 0))),
            pl.BlockSpec((DF, kt), only(NSEC - 1)),
        ],
        out_specs=[pl.BlockSpec((DA, D), lambda s, k: (s, 0)), _full_spec((DF, D))],
        out_shape=[jax.ShapeDtypeStruct((NSEC * DA, D), F32), jax.ShapeDtypeStruct((DF, D), F32)],
        compiler_params=_params(2),
    )(u, dq_t, dk_t, dv_t, dg5_t, df_t)


def _adamw(w, g, m, v):
    m = ADAM_B1 * m + (1.0 - ADAM_B1) * g
    v = ADAM_B2 * v + (1.0 - ADAM_B2) * (g * g)
    m_hat = m / (1.0 - ADAM_B1 ** ADAM_STEP)
    v_hat = v / (1.0 - ADAM_B2 ** ADAM_STEP)
    delta = -ADAM_LR * (m_hat / (jnp.sqrt(v_hat) + ADAM_EPS) + ADAM_WD * w)
    return delta, m, v


def _sum_slabs(ref, rows):
    g = ref[0, rows, :].astype(F32)
    for j in range(1, NDEV):
        g = g + ref[j, rows, :].astype(F32)
    return g


def _adamw_big(own_in, land_in, own_out, land_out, w_in_t, m_in_t, v_in_t, w_out, m_out, v_out):
    cb = CB
    e_sh = D // NDEV
    in_shape = jax.ShapeDtypeStruct(w_in_t.shape, F32)
    out_shape = jax.ShapeDtypeStruct(w_out.shape, F32)

    def total(own_ref, land_ref, rows):
        g = _pick_slab(0, own_ref, land_ref, rows).astype(F32)
        for j in range(1, NDEV):
            g = g + _pick_slab(j, own_ref, land_ref, rows).astype(F32)
        return g

    def body(oi_ref, li_ref, oo_ref, lo_ref, wi_ref, mi_ref, vi_ref, wo_ref, mo_ref, vo_ref,
             gi, di, mi, vi, go, do, mo, vo):
        g = total(oi_ref, li_ref, slice(0, WSHP))[:WSH]
        d, mn, vn = _adamw(wi_ref[...], g, mi_ref[...], vi_ref[...])
        gi[...], di[...], mi[...], vi[...] = g, d, mn, vn
        g = total(oo_ref, lo_ref, slice(0, e_sh))
        d, mn, vn = _adamw(wo_ref[0], g, mo_ref[0], vo_ref[0])
        go[0], do[0], mo[0], vo[0] = g, d, mn, vn

    slab = lambda rows: pl.BlockSpec((NDEV, rows, cb), lambda i: (0, 0, i))
    ispec = pl.BlockSpec((WSH, cb), lambda i: (0, i))
    ospec = pl.BlockSpec((1, e_sh, cb), lambda i: (0, 0, i))
    return pl.pallas_call(
        body, name="adamw_big", grid=(D // cb,),
        in_specs=[slab(WSHP), slab(WSHP), slab(e_sh), slab(e_sh), ispec, ispec, ispec, ospec, ospec, ospec],
        out_specs=[ispec] * 4 + [ospec] * 4, out_shape=[in_shape] * 4 + [out_shape] * 4,
        compiler_params=_params(),
    )(own_in, land_in, own_out, land_out, w_in_t, m_in_t, v_in_t, w_out, m_out, v_out)


F0 = 3 * DA


def _unshard_w_out(own, land):
    e_sh = D // NDEV

    def body(own_ref, land_ref, wo_ref):
        for j in range(NDEV):
            wo_ref[j * e_sh:(j + 1) * e_sh, :] = _pick_slab(j, own_ref, land_ref, slice(0, e_sh), per_peer=False)

    return pl.pallas_call(
        body, name="unshard_w_out", grid=(D // CB,),
        in_specs=[pl.BlockSpec((e_sh, CB), lambda i: (0, i)), pl.BlockSpec((NDEV, e_sh, CB), lambda i: (0, 0, i))],
        out_specs=pl.BlockSpec((D, CB), lambda i: (0, i)),
        out_shape=jax.ShapeDtypeStruct((D, D), BF16),
        compiler_params=_params(),
    )(own, land)


def _unshard_w_in(w_all):
    def body(w_ref, wt_ref):
        def ref_rows(lo, hi):
            pieces, r = [], lo
            while r < hi:
                sh, off = divmod(r, WSH)
                n = min(hi - r, WSH - off)
                pieces.append(w_ref[sh, off:off + n, :])
                r += n
            return pieces

        for s in range(NSEC):
            lo = s * DA if s < 3 else s * DA + H
            wt_ref[s * DA:(s + 1) * DA, :] = jnp.concatenate(ref_rows(lo, lo + DA), axis=0)
        wt_ref[NSEC * DA:DPROJ, :] = jnp.concatenate(
            ref_rows(F0, F0 + H) + [jnp.zeros((DF - H, CB), BF16)], axis=0)

    return pl.pallas_call(
        body, name="unshard_w_in", grid=(D // CB,),
        in_specs=[pl.BlockSpec((NDEV, WSHP, CB), lambda i: (0, 0, i))],
        out_specs=pl.BlockSpec((DPROJ, CB), lambda i: (0, i)),
        out_shape=jax.ShapeDtypeStruct((DPROJ, D), BF16),
        compiler_params=_params(),
    )(w_all)


def _shard_w_in_grads(dw_main, dw_f):
    def body(dm_ref, df_ref, p_ref):
        def ref_rows(lo, hi):
            pieces, r = [], lo
            while r < hi:
                if r < F0:
                    n = min(hi, F0) - r
                    pieces.append(dm_ref[r:r + n, :])
                elif r < F0 + H:
                    n = min(hi, F0 + H) - r
                    pieces.append(df_ref[r - F0:r - F0 + n, :])
                else:
                    n = hi - r
                    pieces.append(dm_ref[r - H:r - H + n, :])
                r += n
            return pieces

        for i in range(NDEV):
            rows = jnp.concatenate(ref_rows(i * WSH, (i + 1) * WSH) + [jnp.zeros((WSHP - WSH, CB), F32)], axis=0)
            p_ref[i] = rows.astype(BF16)

    col = lambda rows: pl.BlockSpec((rows, CB), lambda i: (0, i))
    return pl.pallas_call(
        body, name="shard_w_in_grads", grid=(D // CB,),
        in_specs=[col(NSEC * DA), col(DF)],
        out_specs=pl.BlockSpec((NDEV, WSHP, CB), lambda i: (0, 0, i)),
        out_shape=jax.ShapeDtypeStruct((NDEV, WSHP, D), BF16),
        compiler_params=_params(),
    )(dw_main, dw_f)


def _adamw_small(recv, w, m, v):
    shape = jax.ShapeDtypeStruct((SROWS, TB), F32)

    def body(r_ref, w_ref, m_ref, v_ref, g_out, d_out, m_out, v_out):
        g = _sum_slabs(r_ref, slice(0, SROWS))
        d, mn, vn = _adamw(w_ref[...], g, m_ref[...], v_ref[...])
        g_out[...], d_out[...], m_out[...], v_out[...] = g, d, mn, vn

    return pl.pallas_call(body, name="adamw_small", out_shape=[shape] * 4)(recv, w, m, v)


def _tile_rows(a, rows, lanes=TB):
    a = a.reshape(rows, lanes)
    return jnp.pad(a, ((0, -rows % 8), (0, TB - lanes)))


def _pack_small(norm_g, final_norm_g, attn_norm_g, conv_norm_g, b_f, meta_sh, conv_w_sh, loss=None):
    b_row = b_f.reshape(1, H) if loss is None else jnp.concatenate([b_f.reshape(1, H), loss.reshape(1, 1)], axis=1)
    packed = jnp.concatenate([
        _tile_rows(norm_g, 8), _tile_rows(final_norm_g, 8), _tile_rows(attn_norm_g, 4), _tile_rows(conv_norm_g, 4),
        _tile_rows(b_row, 1, b_row.shape[1]), _tile_rows(meta_sh, NM), _tile_rows(conv_w_sh, 3, DH)], axis=0)
    assert packed.shape == (SROWS, TB)
    return packed


def _unpack_small(p):
    return dict(
        norm_g=p[0:8].reshape(1, D), final_norm_g=p[8:16].reshape(D), attn_norm_g=p[16:20].reshape(1, DA),
        conv_norm_g=p[24:28].reshape(1, DA), b_f=p[32:33, :H].reshape(1, H), meta=p[40:56].reshape(NM, TB),
        conv_w=p[56:59, :DH].reshape(1, 3, DH))


def kernel(x, meta, norm_g, w_in, b_f, conv_w, attn_norm_g, conv_norm_g, w_out, final_norm_g, loss_target, m_meta, m_norm_g, m_w_in, m_b_f, m_conv_w, m_attn_norm_g, m_conv_norm_g, m_w_out, m_final_norm_g, v_meta, v_norm_g, v_w_in, v_b_f, v_conv_w, v_attn_norm_g, v_conv_norm_g, v_w_out, v_final_norm_g):
    seq = x.shape[1]
    L = seq + TB
    assert x.shape == (1, seq, D) and L % TT == 0 and w_in.shape == (1, D, WSH)
    x2 = x[0]
    tgt = loss_target[0]

    w_in_slab = jnp.pad(w_in[0].T, ((0, WSHP - WSH), (0, 0))).astype(BF16)
    w_out_slab = w_out[0].astype(BF16)
    small = jnp.concatenate([meta, _tile_rows(conv_w[0], 3, DH)], axis=0)
    wout_flight = _split_start(w_out_slab, "gather_w_out_start", per_peer=False)
    w_all = _all_gather(w_in_slab, "gather_w_in")
    small_all = _all_gather(small, "gather_small")

    w_t = _unshard_w_in(w_all)
    meta_full = jnp.transpose(small_all[:, :NM, :], (1, 0, 2)).reshape(NM, D)
    conv_w_full = jnp.transpose(small_all[:, NM:NM + 3, :DH], (1, 0, 2)).reshape(3, DA)

    lane_b = lambda p: jnp.broadcast_to(p.reshape(-1, DA, 1), (p.size // DA, DA, TB))
    cw_b = lane_b(conv_w_full)
    ga_b = lane_b(attn_norm_g)[0]
    gcn_b = lane_b(conv_norm_g)[0]

    u, proj_t, f_t, ktok, vtok = _inproj_fwd(x2, meta_full, norm_g + wout_flight[4][0, 0], w_t, L)
    cq, kaug, sg = _fgate_fwd(f_t, b_f.reshape(H, 1), ktok, L)
    o_t, lse = _attn_fwd(proj_t, kaug, cq, L)
    mix_t = _gate_fwd(o_t, proj_t, cw_b, ga_b, gcn_b, L)

    w_out_own, w_out_land = _split_wait(wout_flight, mix_t, "gather_w_out_wait", per_peer=False)
    w_out_full = _unshard_w_out(w_out_own, w_out_land)
    dout, dmix_t, dw_out, loss_part, dg_final = _outproj(
        mix_t, w_out_full, x2, meta_full, final_norm_g.reshape(1, D), tgt, L)
    dwo_flight = _split_start(dw_out.reshape(NDEV, D // NDEV, D), "exchange_dw_out_start", per_peer=True)
    do_t, dd, dg5_t, dga_p, dgc_p, dcw_p = _gate_bwd(dmix_t, o_t, proj_t, cw_b, ga_b + dwo_flight[4][0, 0], gcn_b, L)
    dq_t, dk_t, dv_t, dck, dcq = _attn_bwd(proj_t, kaug, vtok, do_t, lse, dd, cq, L)
    df_t, db_f = _fgate_bwd(dcq, dck, sg, L)
    dw_main, dw_f = _inproj_bwd_w(u, dq_t, dk_t, dv_t, dg5_t, df_t, L)
    dwi_flight = _split_start(_shard_w_in_grads(dw_main, dw_f), "exchange_dw_in_start", per_peer=True)
    grad_x, dmeta, dg_norm = _inproj_bwd_x(
        w_t, dq_t, dk_t, dv_t, dg5_t, df_t, dout, x2, meta_full, norm_g + dwi_flight[4][0, 0], L)
    dga = jnp.sum(dga_p, axis=1)
    dgc = jnp.sum(dgc_p, axis=1)
    dcw = jnp.sum(dcw_p, axis=2)
    small_parts = jnp.stack([
        _pack_small(dg_norm, dg_final, dga, dgc, db_f, dmeta[:, j * TB:(j + 1) * TB], dcw[:, j * DH:(j + 1) * DH],
                    loss=loss_part)
        for j in range(NDEV)], axis=0)
    small_recv = _exchange(small_parts, "exchange_small_grads")
    dwo_own, dwo_land = _split_wait(dwo_flight, small_recv, "exchange_dw_out_wait", per_peer=True)
    dwi_own, dwi_land = _split_wait(dwi_flight, dwo_land, "exchange_dw_in_wait", per_peer=True)

    big_out = _adamw_big(dwi_own, dwi_land, dwo_own, dwo_land,
                         w_in[0].T, m_w_in[0].T, v_w_in[0].T, w_out, m_w_out, v_w_out)
    g_w_in, d_w_in, nm_w_in, nv_w_in = [a.T[None] for a in big_out[:4]]
    g_w_out, d_w_out, nm_w_out, nv_w_out = big_out[4:]
    wp = _pack_small(norm_g, final_norm_g, attn_norm_g, conv_norm_g, b_f, meta, conv_w)
    mp = _pack_small(m_norm_g, m_final_norm_g, m_attn_norm_g, m_conv_norm_g, m_b_f, m_meta, m_conv_w)
    vp = _pack_small(v_norm_g, v_final_norm_g, v_attn_norm_g, v_conv_norm_g, v_b_f, v_meta, v_conv_w)
    small_out = _adamw_small(small_recv, wp, mp, vp)
    sm = [_unpack_small(p) for p in small_out]
    loss = small_out[0][32, H]
    order = ("meta", "norm_g", "w_in", "b_f", "conv_w", "attn_norm_g", "conv_norm_g", "w_out", "final_norm_g")
    groups = []
    for k, (wi, wo) in enumerate(((g_w_in, g_w_out), (d_w_in, d_w_out), (nm_w_in, nm_w_out), (nv_w_in, nv_w_out))):
        d = dict(sm[k], w_in=wi, w_out=wo)
        groups.append([d[n] for n in order])
    return (loss, grad_x[None], *groups[0], *groups[1], *groups[2], *groups[3])
```

```python
import jax
import jax.numpy as jnp
from jax import lax
from jax.experimental import pallas as pl
from jax.experimental.pallas import tpu as pltpu

F32 = jnp.float32
BF16 = jnp.bfloat16

D = 1024
DA = 512
H = 8
DH = 64
NM = 16
TB = 128
P0 = TB - NM
TT = 3 * TB
HG = 8
NDEV = 8
NSEC = 8
DF = 16
DPROJ = NSEC * DA + DF
WSH = 513
WSHP = 528
WROWS = WSHP + D // NDEV
SROWS = 64
EPS = 1e-6
NEG = -1e30
LOG2E = 1.4426950408889634
LN2 = 0.6931471805599453
QSCALE = DH ** -0.5 * LOG2E
KA = 128
CB = 256
VMEM_LIMIT = 56 * 1024 * 1024

ADAM_LR = 0.001
ADAM_B1 = 0.9
ADAM_B2 = 0.999
ADAM_EPS = 1e-08
ADAM_WD = 0.01
ADAM_STEP = 10

NT_DIMS = (((1,), (1,)), ((), ()))
TN_DIMS = (((0,), (0,)), ((), ()))
MESH = pl.DeviceIdType.MESH


def _params(n_axes=1, vmem=VMEM_LIMIT):
    return pltpu.CompilerParams(dimension_semantics=("arbitrary",) * n_axes, vmem_limit_bytes=vmem)


def _dot(a, b, dims=None):
    if dims is None:
        return jnp.dot(a, b, preferred_element_type=F32)
    return lax.dot_general(a, b, dims, preferred_element_type=F32)


def _my_place():
    return lax.axis_index("x"), lax.axis_index("y"), lax.axis_index("c")


def _all_gather(xs, name):
    n = len(xs)

    def body(*refs):
        x_refs, out_refs = refs[:n], refs[n:2 * n]
        send_sems, recv_sems, local_sems = refs[2 * n:]
        mx, my, mc = _my_place()

        def across(px, py, pc, axis_a):
            flip_x = pc if axis_a else 1 - pc
            return (px + flip_x) % 2, (py + 1 - flip_x) % 2, pc

        def idx(p):
            return 4 * p[0] + 2 * p[1] + p[2]

        me, sib = (mx, my, mc), (mx, my, 1 - mc)
        a_nbr, b_nbr = across(*me, True), across(*me, False)
        diag = across(*b_nbr, True)
        sib_a, sib_b = across(*sib, True), across(*sib, False)
        sib_diag = across(*sib_b, True)

        waits = []
        for t in range(n):
            out_ref = out_refs[t]

            def copy(k, block, to, src=None, out_ref=out_ref, t=t):
                return pltpu.make_async_remote_copy(
                    src_ref=out_ref.at[idx(block)] if src is None else src, dst_ref=out_ref.at[idx(block)],
                    send_sem=send_sems.at[7 * t + k], recv_sem=recv_sems.at[7 * t + k],
                    device_id=to, device_id_type=MESH)

            mine = pltpu.make_async_copy(x_refs[t], out_ref.at[idx(me)], local_sems.at[t])
            mine.start()
            started = [copy(0, me, sib, src=x_refs[t]), copy(1, me, a_nbr, src=x_refs[t]),
                       copy(2, me, b_nbr, src=x_refs[t])]
            for cp in started:
                cp.start()
            waits.append((copy, mine, started))
        relays = ((1, a_nbr, ((3, b_nbr), (4, sib))), (2, b_nbr, ((5, sib),)), (3, diag, ((6, sib),)))
        for landed, block, onward in relays:
            for copy, _, started in waits:
                copy(landed, block, me).wait_recv()
                for k, to in onward:
                    started.append(copy(k, block, to))
                    started[-1].start()
        for copy, mine, started in waits:
            for k, block in ((0, sib), (4, sib_a), (5, sib_b), (6, sib_diag)):
                copy(k, block, me).wait_recv()
            for cp in started:
                cp.wait_send()
            mine.wait()

    any_spec = pl.BlockSpec(memory_space=pl.ANY)
    return pl.pallas_call(
        body, name=name,
        out_shape=[jax.ShapeDtypeStruct((NDEV,) + x.shape, x.dtype) for x in xs],
        in_specs=[any_spec] * n, out_specs=[any_spec] * n,
        scratch_shapes=[pltpu.SemaphoreType.DMA((7 * n,)), pltpu.SemaphoreType.DMA((7 * n,)),
                        pltpu.SemaphoreType.DMA((n,))],
    )(*xs)


_HBM =pl.BlockSpec(memory_space=pltpu.HBM)
_SEM = pl.BlockSpec(memory_space=pltpu.SEMAPHORE)
_EFFECT = pltpu.SideEffectType.DATAFLOW_SIDE_EFFECTING


def _peer_of(m, place):
    mx, my, mc = place
    return ((1 - mx) if m & 4 else mx, (1 - my) if m & 2 else my, (1 - mc) if m & 1 else mc)


def _split_copies(src_ref, land_ref, send_sems, recv_sems, per_peer, incoming):
    place = _my_place()
    me = 4 * place[0] + 2 * place[1] + place[2]
    out = []
    for m in range(1, NDEV):
        px, py, pc = _peer_of(m, place)
        peer = 4 * px + 2 * py + pc
        src = (src_ref.at[me] if incoming else src_ref.at[peer]) if per_peer else src_ref
        out.append(pltpu.make_async_remote_copy(
            src_ref=src, dst_ref=land_ref.at[peer if incoming else me],
            send_sem=send_sems.at[m - 1], recv_sem=recv_sems.at[m - 1],
            device_id=(px, py, pc), device_id_type=MESH))
    return out


def _split_start(src, name, per_peer):
    slab = src.shape[1:] if per_peer else src.shape

    def body(src_ref, land_ref, send_sems, recv_sems, src_thru, land_thru, token):
        for cp in _split_copies(src_ref, land_ref, send_sems, recv_sems, per_peer, incoming=False):
            cp.start()
        token[...] = jnp.zeros_like(token)

    return pl.pallas_call(
        body, name=name,
        out_shape=(pltpu.SemaphoreType.DMA((NDEV - 1,)), pltpu.SemaphoreType.DMA((NDEV - 1,)),
                   pltpu.HBM(src.shape, src.dtype), pltpu.HBM((NDEV,) + slab, src.dtype),
                   jax.ShapeDtypeStruct((8, TB), F32)),
        in_specs=(_HBM, _HBM), out_specs=(_SEM, _SEM, _HBM, _HBM, pl.BlockSpec(memory_space=pltpu.VMEM)),
        input_output_aliases={0: 2, 1: 3},
        compiler_params=pltpu.CompilerParams(has_side_effects=_EFFECT),
    )(pltpu.with_memory_space_constraint(src, pltpu.HBM),
      pltpu.with_memory_space_constraint(lax.empty((NDEV,) + slab, src.dtype), pltpu.HBM))


def _split_wait(handles, after, name, per_peer):
    send_sems, recv_sems, src_thru, land_thru, _ = handles

    def body(src_ref, land_ref, send_sems, recv_sems, after_ref, src_out, land_out):
        for cp in _split_copies(src_ref, land_ref, send_sems, recv_sems, per_peer, incoming=False):
            cp.wait_send()
        for cp in _split_copies(src_ref, land_ref, send_sems, recv_sems, per_peer, incoming=True):
            cp.wait_recv()

    return pl.pallas_call(
        body, name=name,
        out_shape=(pltpu.HBM(src_thru.shape, src_thru.dtype), pltpu.HBM(land_thru.shape, land_thru.dtype)),
        in_specs=(_HBM, _HBM, _SEM, _SEM, pl.BlockSpec(memory_space=pl.ANY)), out_specs=(_HBM, _HBM),
        input_output_aliases={0: 0, 1: 1},
        compiler_params=pltpu.CompilerParams(has_side_effects=_EFFECT),
    )(src_thru, land_thru, send_sems, recv_sems, after)


def _pick_slab(j, own_ref, land_ref, rows, per_peer=True):
    mx, my, mc = _my_place()
    me = 4 * mx + 2 * my + mc
    own = (lambda: own_ref[j, rows, :]) if per_peer else (lambda: own_ref[rows, :])
    return lax.cond(me == j, own, lambda: land_ref[j, rows, :])


def _h_block(t, x_ref, meta_ref):
    first = jnp.concatenate([jnp.zeros((P0, D), F32), meta_ref[...]], axis=0)
    return jnp.where(t == 0, first, x_ref[...])


def _x_spec():
    return pl.BlockSpec((TB, D), lambda t: (jnp.maximum(t - 1, 0), 0))


def _x_specs3():
    return [pl.BlockSpec((TB, D), lambda j: (jnp.maximum(3 * j - 1, 0), 0)),
            pl.BlockSpec((TB, D), lambda j: (3 * j, 0)),
            pl.BlockSpec((TB, D), lambda j: (3 * j + 1, 0))]


def _h_tile(j, xa_ref, xb_ref, xc_ref, meta_ref):
    first = jnp.concatenate([jnp.zeros((P0, D), F32), meta_ref[...]], axis=0)
    return jnp.concatenate([jnp.where(j == 0, first, xa_ref[...]), xb_ref[...], xc_ref[...]], axis=0)


def _full_spec(shape):
    return pl.BlockSpec(shape, lambda *_: (0,) * len(shape))


def _sigmoid(z):
    return 1.0 / (1.0 + jnp.exp(-z))


def _grouped(x):
    return x.reshape(H, DH, x.shape[-1])


def _group_rstd(x3):
    return lax.rsqrt(jnp.mean(x3 * x3, axis=1, keepdims=True) + EPS)


def _lane_tiles_sum(x):
    out = x[:, :TB]
    for i in range(1, x.shape[1] // TB):
        out = out + x[:, i * TB:(i + 1) * TB]
    return out


def _inproj_fwd(x, meta_full, norm_g, w_t, L):
    nj = L // TT

    def body(xa_ref, xb_ref, xc_ref, meta_ref, g_ref, w_ref, u_ref, proj_ref, f_ref, ktok_ref, vtok_ref):
        hb = _h_tile(pl.program_id(0), xa_ref, xb_ref, xc_ref, meta_ref)
        r = lax.rsqrt(jnp.mean(hb * hb, axis=-1, keepdims=True) + EPS)
        u = (hb * r * g_ref[...]).astype(BF16)
        u_ref[...] = u
        for s in range(NSEC):
            p = _dot(u, w_ref[s * DA:(s + 1) * DA, :], NT_DIMS)
            if s == 0:
                p = p * QSCALE
            if s in (1, 2):
                tok_ref = ktok_ref if s == 1 else vtok_ref
                for h in range(H):
                    tok_ref[h] = p[:, h * DH:(h + 1) * DH].astype(BF16)
            proj_ref[s * DA:(s + 1) * DA, :] = p.T.astype(BF16)
        f_ref[...] = _dot(w_ref[NSEC * DA:DPROJ, :], u, NT_DIMS)[:H]

    return pl.pallas_call(
        body, name="inproj_fwd", grid=(nj,),
        in_specs=_x_specs3() + [_full_spec((NM, D)), _full_spec((1, D)), _full_spec((DPROJ, D))],
        out_specs=[
            pl.BlockSpec((TT, D), lambda t: (t, 0)),
            pl.BlockSpec((NSEC * DA, TT), lambda t: (0, t)),
            pl.BlockSpec((H, TT), lambda t: (0, t)),
            pl.BlockSpec((H, TT, DH), lambda t: (0, t, 0)),
            pl.BlockSpec((H, TT, DH), lambda t: (0, t, 0)),
        ],
        out_shape=[
            jax.ShapeDtypeStruct((L, D), BF16),
            jax.ShapeDtypeStruct((NSEC * DA, L), BF16),
            jax.ShapeDtypeStruct((H, L), F32),
            jax.ShapeDtypeStruct((H, L, DH), BF16),
            jax.ShapeDtypeStruct((H, L, DH), BF16),
        ],
        compiler_params=_params(),
    )(x, x, x, meta_full, norm_g, w_t)


def _split3(x):
    hi = x.astype(BF16).astype(F32)
    r = x - hi
    mid = r.astype(BF16).astype(F32)
    return hi, mid, (r - mid).astype(BF16).astype(F32)


def _bias_rows(bias):
    one = jnp.ones((1, TT), F32)
    zero = jnp.zeros((1, TT), F32)
    parts = [zero] * 3 if bias is None else list(_split3(bias))
    return jnp.concatenate([one] * 3 + parts + [zero] * (DF - 6), axis=0).astype(BF16)


def _fgate_fwd(f_t, b_col, ktok, L):
    nb = L // TB

    def body(f_ref, b_ref, ktok_ref, cq_ref, kaug_ref, sg_ref):
        z = f_ref[...] + b_ref[...]
        idx = lax.broadcasted_iota(jnp.int32, (H, L), 1)
        real = idx >= P0
        lf = jnp.where(real, jnp.minimum(z, 0.0) - jnp.log1p(jnp.exp(-jnp.abs(z))), 0.0)
        sg_ref[...] = jnp.where(real, 1.0 / (1.0 + jnp.exp(z)), 0.0)
        c = lf
        s = 1
        while s < L:
            c = c + jnp.where(idx >= s, pltpu.roll(c, s, 1), 0.0)
            s *= 2
        c = c * LOG2E
        for h in range(H):
            cq_ref[h] = c[h:h + 1, :]
        ck = jnp.where(real, c, -NEG)
        lane = lax.broadcasted_iota(jnp.int32, (TB, KA), 1)
        tail = jnp.where((lane >= DH + 3) & (lane < DH + 6), 1.0, 0.0)
        for h in range(H):
            for b in range(nb):
                blk = slice(b * TB, (b + 1) * TB)
                col = jnp.broadcast_to(ck[h:h + 1, blk], (TB, TB)).T
                hi, mid, lo = _split3(-col)
                k = jnp.concatenate([ktok_ref[h, blk, :].astype(F32), jnp.zeros((TB, KA - DH), F32)], axis=1)
                out = jnp.where(lane < DH, k, jnp.where(lane == DH, hi, jnp.where(
                    lane == DH + 1, mid, jnp.where(lane == DH + 2, lo, tail))))
                kaug_ref[h, blk, :] = out.astype(BF16)

    return pl.pallas_call(
        body, name="fgate_fwd",
        out_shape=[
            jax.ShapeDtypeStruct((H, 1, L), F32),
            jax.ShapeDtypeStruct((H, L, KA), BF16),
            jax.ShapeDtypeStruct((H, L), F32),
        ],
        compiler_params=pltpu.CompilerParams(vmem_limit_bytes=VMEM_LIMIT),
    )(f_t, b_col, ktok)


def _causal_mask():
    r = lax.broadcasted_iota(jnp.int32, (TT, TT), 0)
    c = lax.broadcasted_iota(jnp.int32, (TT, TT), 1)
    return r <= c


def _attn_fwd(proj_t, kaug, cq, L):
    nq = L // TT

    def body(q_ref, kaug_ref, v_ref, cq_ref, o_ref, lse_ref,
             qa_scr, s_scr, cmax_scr, m_scr, p_scr, alpha_scr, acc_scr):
        j = pl.program_id(1)
        rows = [slice(g * DH, (g + 1) * DH) for g in range(HG)]
        ones = jnp.ones((DF, TT), BF16)
        for g in range(HG):
            qa_scr[g] = jnp.concatenate(
                [q_ref[rows[g], :], _bias_rows(None), jnp.zeros((KA - DH - DF, TT), BF16)], axis=0)

        def scores(kt, masked):
            k_off = pl.multiple_of(kt * TT, TT)
            for g in range(HG):
                s = _dot(kaug_ref[g, pl.ds(k_off, TT), :], qa_scr[g])
                if masked:
                    s = jnp.where(_causal_mask(), s, NEG)
                s_scr[g] = s
                cmax_scr[g] = jnp.max(s, axis=0, keepdims=True)

        def softmax():
            for g in range(HG):
                m_old = m_scr[g]
                m_new = jnp.maximum(m_old, cmax_scr[g])
                alpha_scr[g] = jnp.exp2(m_old - m_new)
                p_scr[g] = jnp.exp2(s_scr[g] - m_new).astype(BF16)
                m_scr[g] = m_new

        def weighted_sum(kt):
            k_off = pl.multiple_of(kt * TT, TT)
            for g in range(HG):
                v1 = jnp.concatenate([v_ref[rows[g], pl.ds(k_off, TT)], ones], axis=0)
                acc_scr[g] = alpha_scr[g] * acc_scr[g] + _dot(v1, p_scr[g])

        m_scr[...] = jnp.full_like(m_scr, NEG)
        acc_scr[...] = jnp.zeros_like(acc_scr)

        scores(j, True)

        @pl.when(j >= 1)
        def _():
            softmax()
            scores(j - 1, False)

        def step(i, c):
            weighted_sum(j - i + 1)
            softmax()
            scores(j - i - 1, False)
            return c

        lax.fori_loop(1, j, step, 0)

        @pl.when(j >= 1)
        def _():
            weighted_sum(1)

        softmax()
        weighted_sum(0)
        for g in range(HG):
            l = acc_scr[g, DH:DH + 1, :]
            o_ref[rows[g], :] = acc_scr[g, :DH, :] * (1.0 / l)
            lse_ref[g] = m_scr[g] + jnp.log2(l) + cq_ref[g]

    return pl.pallas_call(
        body, name="attn_fwd", grid=(H // HG, nq),
        in_specs=[
            pl.BlockSpec((HG * DH, TT), lambda h, j: (h, j)),
            pl.BlockSpec((HG, L, KA), lambda h, j: (h, 0, 0)),
            pl.BlockSpec((HG * DH, L), lambda h, j: (2 * H // HG + h, 0)),
            pl.BlockSpec((HG, 1, TT), lambda h, j: (h, 0, j)),
        ],
        out_specs=[
            pl.BlockSpec((HG * DH, TT), lambda h, j: (h, j)),
            pl.BlockSpec((HG, 1, TT), lambda h, j: (h, 0, j)),
        ],
        out_shape=[jax.ShapeDtypeStruct((DA, L), F32), jax.ShapeDtypeStruct((H, 1, L), F32)],
        scratch_shapes=[pltpu.VMEM((HG, KA, TT), BF16), pltpu.VMEM((HG, TT, TT), F32), pltpu.VMEM((HG, 1, TT), F32),
                        pltpu.VMEM((HG, 1, TT), F32), pltpu.VMEM((HG, TT, TT), BF16), pltpu.VMEM((HG, 1, TT), F32),
                        pltpu.VMEM((HG, DH + DF, TT), F32)],
        compiler_params=_params(2),
    )(proj_t, kaug, proj_t, cq)


def _gate_common(o, za, gb, gc, xc, zc, gcp, xcp, cw_ref, ga_ref, gcn_ref, first):
    n_rep = TT // TB
    a = gc * xc
    a_prev = jnp.where(first, 0.0, gcp * xcp)
    full = jnp.concatenate([a_prev, a], axis=1)
    a1 = pltpu.roll(full, 1, 1)[:, TB:]
    a2 = pltpu.roll(full, 2, 1)[:, TB:]
    w0 = jnp.tile(cw_ref[0], (1, n_rep))
    w1 = jnp.tile(cw_ref[1], (1, n_rep))
    w2 = jnp.tile(cw_ref[2], (1, n_rep))
    cv = w0 * a2 + w1 * a1 + w2 * a
    e = gb * cv
    e3 = _grouped(e)
    rc = _group_rstd(e3)
    ec = (e3 * rc).reshape(DA, TT)
    o3 = _grouped(o)
    ra = _group_rstd(o3)
    oa = (o3 * ra).reshape(DA, TT)
    g_a = jnp.tile(ga_ref[...], (1, n_rep))
    g_c = jnp.tile(gcn_ref[...], (1, n_rep))
    sa = _sigmoid(za)
    sc = _sigmoid(zc)
    return dict(a=a, a1=a1, a2=a2, w0=w0, w1=w1, w2=w2, cv=cv, e=e, rc=rc, ec=ec, ra=ra, oa=oa,
                g_a=g_a, g_c=g_c, sa=sa, sc=sc)


def _gate_specs(nj, rev):
    def jj(i):
        return (nj - 1 - i) if rev else i

    def sec(s):
        return pl.BlockSpec((DA, TT), lambda i: (s, jj(i)))

    def halo(s):
        return pl.BlockSpec((DA, TB), lambda i: (s, jnp.maximum(3 * jj(i) - 1, 0)))

    return [pl.BlockSpec((DA, TT), lambda i: (0, jj(i))), sec(3), sec(4), sec(5), sec(6), sec(7), halo(5), halo(6),
            _full_spec((3, DA, TB)), _full_spec((DA, TB)), _full_spec((DA, TB))]


def _gate_fwd(o_t, proj_t, cw_b, ga_b, gcn_b, L):
    nj = L // TT

    def body(o_ref, za_ref, gb_ref, gc_ref, xc_ref, zc_ref, gcp_ref, xcp_ref, cw_ref, ga_ref, gcn_ref, mix_ref):
        j = pl.program_id(0)
        f32 = lambda r: r[...].astype(F32)
        za, zc = f32(za_ref), f32(zc_ref)
        g = _gate_common(o_ref[...], za, f32(gb_ref), f32(gc_ref), f32(xc_ref), zc, f32(gcp_ref), f32(xcp_ref),
                         cw_ref, ga_ref, gcn_ref, j == 0)
        mix_ref[:DA, :] = (g["oa"] * g["g_a"] * (za * g["sa"])).astype(BF16)
        mix_ref[DA:, :] = (g["ec"] * g["g_c"] * (zc * g["sc"])).astype(BF16)

    return pl.pallas_call(
        body, name="gate_fwd", grid=(nj,),
        in_specs=_gate_specs(nj, False),
        out_specs=pl.BlockSpec((2 * DA, TT), lambda j: (0, j)),
        out_shape=jax.ShapeDtypeStruct((2 * DA, L), BF16),
        compiler_params=_params(),
    )(o_t, proj_t, proj_t, proj_t, proj_t, proj_t, proj_t, proj_t, cw_b, ga_b, gcn_b)


def _outproj(mix_t, w_out, x, meta_full, fng, target, L):
    nj = L // TT

    def body(mix_ref, w_ref, xa_ref, xb_ref, xc_ref, meta_ref, g_ref, ta_ref, tb_ref, tc_ref,
             dout_ref, dmix_ref, dwb_ref, loss_ref, dg_ref, dw_ref):
        t = pl.program_id(0)

        @pl.when(t == 0)
        def _():
            dw_ref[...] = jnp.zeros_like(dw_ref)
            loss_ref[...] = jnp.zeros_like(loss_ref)
            dg_ref[...] = jnp.zeros_like(dg_ref)

        mix = mix_ref[...]
        o = _dot(mix, w_ref[...], TN_DIMS) + _h_tile(t, xa_ref, xb_ref, xc_ref, meta_ref)
        r = lax.rsqrt(jnp.mean(o * o, axis=-1, keepdims=True) + EPS)
        g = g_ref[...]
        orn = o * r
        tgt = jnp.concatenate([ta_ref[...], tb_ref[...], tc_ref[...]], axis=0)
        row = lax.broadcasted_iota(jnp.int32, (TT, 1), 0)
        real = jnp.where((t > 0) | (row >= TB), 1.0, 0.0)
        diff = (orn * g - tgt) * real
        loss_ref[...] += 0.5 * jnp.sum(diff * diff) * (1.0 / D)
        dy = diff * (1.0 / D)
        dg_ref[...] += jnp.sum(dy * orn, axis=0, keepdims=True)
        gy = dy * g
        dout = r * gy - orn * (r * jnp.mean(gy * orn, axis=-1, keepdims=True))
        dout_ref[...] = dout
        db = dout.astype(BF16)
        dmix_ref[...] = _dot(db, w_ref[...], NT_DIMS).T.astype(BF16)
        dw_ref[...] += _dot(mix, db)

        @pl.when(t == nj - 1)
        def _():
            dwb_ref[...] = dw_ref[...].astype(BF16)

    return pl.pallas_call(
        body, name="outproj", grid=(nj,),
        in_specs=[pl.BlockSpec((D, TT), lambda t: (0, t)), _full_spec((D, D))] + _x_specs3()
                 + [_full_spec((NM, D)), _full_spec((1, D))] + _x_specs3(),
        out_specs=[pl.BlockSpec((TT, D), lambda t: (t, 0)), pl.BlockSpec((D, TT), lambda t: (0, t)),
                   _full_spec((D, D)), _full_spec((1, 1)), _full_spec((1, D))],
        out_shape=[jax.ShapeDtypeStruct((L, D), F32), jax.ShapeDtypeStruct((D, L), BF16),
                   jax.ShapeDtypeStruct((D, D), BF16), jax.ShapeDtypeStruct((1, 1), F32),
                   jax.ShapeDtypeStruct((1, D), F32)],
        scratch_shapes=[pltpu.VMEM((D, D), F32)],
        compiler_params=_params(),
    )(mix_t, w_out, x, x, x, meta_full, fng, target, target, target)


def _gate_bwd(dmix_t, o_t, proj_t, cw_b, ga_b, gcn_b, L):
    nj = L // TT

    def body(dmix_ref, o_ref, za_ref, gb_ref, gc_ref, xc_ref, zc_ref, gcp_ref, xcp_ref, cw_ref, ga_ref, gcn_ref,
             do_ref, dd_ref, dg5_ref, dga_ref, dgc_ref, dcw_ref, carry_ref):
        i = pl.program_id(0)
        j = nj - 1 - i

        @pl.when(i == 0)
        def _():
            carry_ref[...] = jnp.zeros_like(carry_ref)
            dga_ref[...] = jnp.zeros_like(dga_ref)
            dgc_ref[...] = jnp.zeros_like(dgc_ref)
            dcw_ref[...] = jnp.zeros_like(dcw_ref)

        f32 = lambda r: r[...].astype(F32)
        o, za, gb, gc, xc, zc = o_ref[...], f32(za_ref), f32(gb_ref), f32(gc_ref), f32(xc_ref), f32(zc_ref)
        g = _gate_common(o, za, gb, gc, xc, zc, f32(gcp_ref), f32(xcp_ref), cw_ref, ga_ref, gcn_ref, j == 0)
        dya = dmix_ref[:DA, :].astype(F32)
        dyc = dmix_ref[DA:, :].astype(F32)
        sa, sc = g["sa"], g["sc"]

        dn = dya * (za * sa)
        dg5_ref[0:DA, :] = (dya * (g["oa"] * g["g_a"]) * (sa * (1.0 + za * (1.0 - sa)))).astype(BF16)
        dga_ref[...] += _lane_tiles_sum(dn * g["oa"])
        dng = dn * g["g_a"]
        mean_a = jnp.mean(_grouped(dng * g["oa"]), axis=1, keepdims=True)
        do = ((_grouped(dng) - _grouped(g["oa"]) * mean_a) * g["ra"]).reshape(DA, TT)
        do_ref[...] = do.astype(BF16)
        dd = jnp.sum(_grouped(do * o), axis=1)
        for h in range(H):
            dd_ref[h] = dd[h:h + 1, :]

        dnc = dyc * (zc * sc)
        dg5_ref[4 * DA:5 * DA, :] = (dyc * (g["ec"] * g["g_c"]) * (sc * (1.0 + zc * (1.0 - sc)))).astype(BF16)
        dgc_ref[...] += _lane_tiles_sum(dnc * g["ec"])
        dncg = dnc * g["g_c"]
        mean_c = jnp.mean(_grouped(dncg * g["ec"]), axis=1, keepdims=True)
        de = ((_grouped(dncg) - _grouped(g["ec"]) * mean_c) * g["rc"]).reshape(DA, TT)
        dg5_ref[DA:2 * DA, :] = (de * g["cv"]).astype(BF16)
        dcv = de * gb
        full = jnp.concatenate([dcv, carry_ref[...]], axis=1)
        d1 = pltpu.roll(full, TT + TB - 1, 1)[:, :TT]
        d2 = pltpu.roll(full, TT + TB - 2, 1)[:, :TT]
        carry_ref[...] = dcv[:, :TB]
        da = g["w2"] * dcv + g["w1"] * d1 + g["w0"] * d2
        dg5_ref[2 * DA:3 * DA, :] = (da * xc).astype(BF16)
        dg5_ref[3 * DA:4 * DA, :] = (da * gc).astype(BF16)
        dcw_ref[0] += _lane_tiles_sum(dcv * g["a2"])
        dcw_ref[1] += _lane_tiles_sum(dcv * g["a1"])
        dcw_ref[2] += _lane_tiles_sum(dcv * g["a"])

    rj = lambda i: nj - 1 - i
    return pl.pallas_call(
        body, name="gate_bwd", grid=(nj,),
        in_specs=[pl.BlockSpec((2 * DA, TT), lambda i: (0, rj(i)))] + _gate_specs(nj, True),
        out_specs=[
            pl.BlockSpec((DA, TT), lambda i: (0, rj(i))),
            pl.BlockSpec((H, 1, TT), lambda i: (0, 0, rj(i))),
            pl.BlockSpec((5 * DA, TT), lambda i: (0, rj(i))),
            _full_spec((DA, TB)), _full_spec((DA, TB)), _full_spec((3, DA, TB)),
        ],
        out_shape=[
            jax.ShapeDtypeStruct((DA, L), BF16),
            jax.ShapeDtypeStruct((H, 1, L), F32),
            jax.ShapeDtypeStruct((5 * DA, L), BF16),
            jax.ShapeDtypeStruct((DA, TB), F32),
            jax.ShapeDtypeStruct((DA, TB), F32),
            jax.ShapeDtypeStruct((3, DA, TB), F32),
        ],
        scratch_shapes=[pltpu.VMEM((DA, TB), F32)],
        compiler_params=_params(),
    )(dmix_t, o_t, proj_t, proj_t, proj_t, proj_t, proj_t, proj_t, proj_t, cw_b, ga_b, gcn_b)


def _attn_bwd(proj_t, kaug, vtok, do_t, lse, dd, cq, L):
    nk = L // TT

    def body(q_ref, kaug_ref, vtok_ref, kt_ref, do_ref, lse_ref, dd_ref, cq_ref,
             dq_ref, dk_ref, dv_ref, dck_ref, dcq_ref, dq_acc, kt1_scr, s_scr, dp_scr, dv_scr, dk_scr):
        i = pl.program_id(1)

        @pl.when(i == 0)
        def _():
            dq_acc[...] = jnp.zeros_like(dq_acc)

        rows = [slice(g * DH, (g + 1) * DH) for g in range(HG)]
        ones = jnp.ones((DF, TT), BF16)
        zpad = jnp.zeros((KA - DH - DF, TT), BF16)
        for g in range(HG):
            kt1_scr[g] = jnp.concatenate([kt_ref[rows[g], :], ones], axis=0)
        dv_scr[...] = jnp.zeros_like(dv_scr)
        dk_scr[...] = jnp.zeros_like(dk_scr)

        def q_rows(g, q_off):
            bias = cq_ref[g, :, pl.ds(q_off, TT)] - lse_ref[g, :, pl.ds(q_off, TT)]
            return jnp.concatenate([q_ref[rows[g], pl.ds(q_off, TT)], _bias_rows(bias)], axis=0)

        def scores(jq, masked):
            q_off = pl.multiple_of(jq * TT, TT)
            for g in range(HG):
                s = _dot(kaug_ref[g], jnp.concatenate([q_rows(g, q_off), zpad], axis=0))
                if masked:
                    s = jnp.where(_causal_mask(), s, NEG)
                s_scr[g] = s
                dp_scr[g] = _dot(vtok_ref[g], do_ref[rows[g], pl.ds(q_off, TT)])

        def grads(jq):
            q_off = pl.multiple_of(jq * TT, TT)
            for g in range(HG):
                p = jnp.exp2(s_scr[g])
                ds = (p * (dp_scr[g] - dd_ref[g, :, pl.ds(q_off, TT)])).astype(BF16)
                do1 = jnp.concatenate([do_ref[rows[g], pl.ds(q_off, TT)], jnp.zeros((KA - DH, TT), BF16)], axis=0)
                q1 = jnp.concatenate([q_rows(g, q_off), zpad], axis=0)
                dv_scr[g] += _dot(p.astype(BF16), do1, NT_DIMS)
                dk_scr[g] += _dot(ds, q1, NT_DIMS)
                dq_acc[g, :, pl.ds(q_off, TT)] += _dot(kt1_scr[g], ds)

        scores(i, True)

        def step(jq, c):
            grads(jq)
            scores(jq + 1, False)
            return c

        lax.fori_loop(i, nk - 1, step, 0)
        grads(nk - 1)
        for g in range(HG):
            dv_ref[rows[g], :] = dv_scr[g].T[:DH, :].astype(BF16)
            dk_t = dk_scr[g].T
            dk_ref[rows[g], :] = (dk_t[:DH, :] * LN2).astype(BF16)
            dck_ref[g] = dk_t[DH:DH + 1, :]

        @pl.when(i == nk - 1)
        def _():
            for g in range(HG):
                dq_ref[rows[g], :] = (dq_acc[g, :DH, :] * (DH ** -0.5)).astype(BF16)
                dcq_ref[g] = dq_acc[g, DH:DH + 1, :]

    head = lambda h, i: (h, 0)
    row = lambda h, i: (h, 0, 0)
    return pl.pallas_call(
        body, name="attn_bwd", grid=(H // HG, nk),
        in_specs=[
            pl.BlockSpec((HG * DH, L), head),
            pl.BlockSpec((HG, TT, KA), lambda h, i: (h, i, 0)),
            pl.BlockSpec((HG, TT, DH), lambda h, i: (h, i, 0)),
            pl.BlockSpec((HG * DH, TT), lambda h, i: (H // HG + h, i)),
            pl.BlockSpec((HG * DH, L), head),
            pl.BlockSpec((HG, 1, L), row), pl.BlockSpec((HG, 1, L), row), pl.BlockSpec((HG, 1, L), row),
        ],
        out_specs=[
            pl.BlockSpec((HG * DH, L), head),
            pl.BlockSpec((HG * DH, TT), lambda h, i: (h, i)),
            pl.BlockSpec((HG * DH, TT), lambda h, i: (h, i)),
            pl.BlockSpec((HG, 1, TT), lambda h, i: (h, 0, i)),
            pl.BlockSpec((HG, 1, L), row),
        ],
        out_shape=[jax.ShapeDtypeStruct((DA, L), BF16), jax.ShapeDtypeStruct((DA, L), BF16),
                   jax.ShapeDtypeStruct((DA, L), BF16), jax.ShapeDtypeStruct((H, 1, L), F32),
                   jax.ShapeDtypeStruct((H, 1, L), F32)],
        scratch_shapes=[
            pltpu.VMEM((HG, DH + DF, L), F32),
            pltpu.VMEM((HG, DH + DF, TT), BF16),
            pltpu.VMEM((HG, TT, TT), F32), pltpu.VMEM((HG, TT, TT), F32),
            pltpu.VMEM((HG, TT, KA), F32), pltpu.VMEM((HG, TT, KA), F32)],
        compiler_params=_params(2),
    )(proj_t, kaug, vtok, proj_t, do_t, lse, dd, cq)


def _fgate_bwd(dcq, dck, sg, L):
    def body(dcq_ref, dck_ref, sg_ref, df_ref, db_ref):
        dc = jnp.concatenate([dcq_ref[h] - dck_ref[h] for h in range(H)], axis=0)
        idx = lax.broadcasted_iota(jnp.int32, (H, L), 1)
        r = dc
        s = 1
        while s < L:
            r = r + jnp.where(idx + s < L, pltpu.roll(r, L - s, 1), 0.0)
            s *= 2
        df = r * sg_ref[...]
        db_ref[...] = jnp.sum(df, axis=1, keepdims=True)
        df_ref[...] = jnp.concatenate([df, jnp.zeros((DF - H, L), F32)], axis=0).astype(BF16)

    return pl.pallas_call(
        body, name="fgate_bwd",
        out_shape=[jax.ShapeDtypeStruct((DF, L), BF16), jax.ShapeDtypeStruct((H, 1), F32)],
        compiler_params=pltpu.CompilerParams(vmem_limit_bytes=VMEM_LIMIT),
    )(dcq, dck, sg)


def _inproj_bwd_x(w, dq_t, dk_t, dv_t, dg5_t, df_t, dout, x, meta_full, norm_g, L):
    nb = L // TB
    seq = x.shape[0]

    def body(w_ref, dq_ref, dk_ref, dv_ref, dg5_ref, df_ref, dout_ref, x_ref, meta_ref, g_ref,
             gx_ref, dmeta_ref, dg_ref):
        t = pl.program_id(0)

        @pl.when(t == 0)
        def _():
            dg_ref[...] = jnp.zeros_like(dg_ref)

        du = _dot(dq_ref[...], w_ref[0:DA, :], TN_DIMS)
        du += _dot(dk_ref[...], w_ref[DA:2 * DA, :], TN_DIMS)
        du += _dot(dv_ref[...], w_ref[2 * DA:3 * DA, :], TN_DIMS)
        du += _dot(dg5_ref[...], w_ref[3 * DA:NSEC * DA, :], TN_DIMS)
        du += _dot(df_ref[...], w_ref[NSEC * DA:DPROJ, :], TN_DIMS)
        hb = _h_block(t, x_ref, meta_ref)
        r = lax.rsqrt(jnp.mean(hb * hb, axis=-1, keepdims=True) + EPS)
        hn = hb * r
        dg_ref[...] += jnp.sum(du * hn, axis=0, keepdims=True)
        gu = du * g_ref[...]
        dh = dout_ref[...] + r * gu - hn * (r * jnp.mean(gu * hn, axis=-1, keepdims=True))
        gx_ref[...] = dh

        @pl.when(t == 0)
        def _():
            dmeta_ref[...] = dh[P0:, :]

    blk = lambda rows: pl.BlockSpec((rows, TB), lambda t: (0, t))
    return pl.pallas_call(
        body, name="inproj_bwd_x", grid=(nb,),
        in_specs=[_full_spec((DPROJ, D)), blk(DA), blk(DA), blk(DA), blk(5 * DA), blk(DF),
                  pl.BlockSpec((TB, D), lambda t: (t, 0)), _x_spec(), _full_spec((NM, D)), _full_spec((1, D))],
        out_specs=[_x_spec(), _full_spec((NM, D)), _full_spec((1, D))],
        out_shape=[jax.ShapeDtypeStruct((seq, D), F32), jax.ShapeDtypeStruct((NM, D), F32),
                   jax.ShapeDtypeStruct((1, D), F32)],
        compiler_params=_params(),
    )(w, dq_t, dk_t, dv_t, dg5_t, df_t, dout, x, meta_full, norm_g)


def _inproj_bwd_w(u, dq_t, dk_t, dv_t, dg5_t, df_t, L):
    kt = L // 3
    nkt = 3

    def body(u_ref, dq_ref, dk_ref, dv_ref, dg5_ref, df_ref, dw_ref, dwf_ref):
        s = pl.program_id(0)
        k = pl.program_id(1)

        @pl.when(k == 0)
        def _():
            dw_ref[...] = jnp.zeros_like(dw_ref)

        @pl.when((s == 0) & (k == 0))
        def _():
            dwf_ref[...] = jnp.zeros_like(dwf_ref)

        u_blk = u_ref[...]
        for sec, ref in ((0, dq_ref), (1, dk_ref), (2, dv_ref)):
            @pl.when(s == sec)
            def _(ref=ref):
                dw_ref[...] += _dot(ref[...], u_blk)

        @pl.when(s >= 3)
        def _():
            dw_ref[...] += _dot(dg5_ref[...], u_blk)

        @pl.when(s == NSEC - 1)
        def _():
            dwf_ref[...] += _dot(df_ref[...], u_blk)

    def only(sec):
        return lambda s, k: (0, jnp.where(s == sec, k, 0))

    return pl.pallas_call(
        body, name="inproj_bwd_w", grid=(NSEC, nkt),
        in_specs=[
            pl.BlockSpec((kt, D), lambda s, k: (k, 0)),
            pl.BlockSpec((DA, kt), only(0)), pl.BlockSpec((DA, kt), only(1)), pl.BlockSpec((DA, kt), only(2)),
            pl.BlockSpec((DA, kt), lambda s, k: (jnp.maximum(s - 3, 0), jnp.where(s >= 3, k, 0))),
            pl.BlockSpec((DF, kt), only(NSEC - 1)),
        ],
        out_specs=[pl.BlockSpec((DA, D), lambda s, k: (s, 0)), _full_spec((DF, D))],
        out_shape=[jax.ShapeDtypeStruct((NSEC * DA, D), F32), jax.ShapeDtypeStruct((DF, D), F32)],
        compiler_params=_params(2),
    )(u, dq_t, dk_t, dv_t, dg5_t, df_t)


def _adamw(w, g, m, v):
    m = ADAM_B1 * m + (1.0 - ADAM_B1) * g
    v = ADAM_B2 * v + (1.0 - ADAM_B2) * (g * g)
    m_hat = m / (1.0 - ADAM_B1 ** ADAM_STEP)
    v_hat = v / (1.0 - ADAM_B2 ** ADAM_STEP)
    delta = -ADAM_LR * (m_hat / (jnp.sqrt(v_hat) + ADAM_EPS) + ADAM_WD * w)
    return delta, m, v


def _adamw_big(own_in, land_in, own_out, land_out, w_in_t, m_in_t, v_in_t, w_out, m_out, v_out):
    cb = CB
    e_sh = D // NDEV
    in_shape = jax.ShapeDtypeStruct(w_in_t.shape, F32)
    out_shape = jax.ShapeDtypeStruct(w_out.shape, F32)

    def total(own_ref, land_ref, rows):
        g = _pick_slab(0, own_ref, land_ref, rows).astype(F32)
        for j in range(1, NDEV):
            g = g + _pick_slab(j, own_ref, land_ref, rows).astype(F32)
        return g

    def body(oi_ref, li_ref, oo_ref, lo_ref, wi_ref, mi_ref, vi_ref, wo_ref, mo_ref, vo_ref,
             gi, di, mi, vi, go, do, mo, vo):
        g = total(oi_ref, li_ref, slice(0, WSHP))[:WSH]
        d, mn, vn = _adamw(wi_ref[...], g, mi_ref[...], vi_ref[...])
        gi[...], di[...], mi[...], vi[...] = g, d, mn, vn
        g = total(oo_ref, lo_ref, slice(0, e_sh))
        d, mn, vn = _adamw(wo_ref[0], g, mo_ref[0], vo_ref[0])
        go[0], do[0], mo[0], vo[0] = g, d, mn, vn

    slab = lambda rows: pl.BlockSpec((NDEV, rows, cb), lambda i: (0, 0, i))
    ispec = pl.BlockSpec((WSH, cb), lambda i: (0, i))
    ospec = pl.BlockSpec((1, e_sh, cb), lambda i: (0, 0, i))
    return pl.pallas_call(
        body, name="adamw_big", grid=(D // cb,),
        in_specs=[slab(WSHP), slab(WSHP), slab(e_sh), slab(e_sh), ispec, ispec, ispec, ospec, ospec, ospec],
        out_specs=[ispec] * 4 + [ospec] * 4, out_shape=[in_shape] * 4 + [out_shape] * 4,
        compiler_params=_params(),
    )(own_in, land_in, own_out, land_out, w_in_t, m_in_t, v_in_t, w_out, m_out, v_out)


F0 = 3 * DA


def _unshard_w_out(own, land):
    e_sh = D // NDEV

    def body(own_ref, land_ref, wo_ref):
        for j in range(NDEV):
            wo_ref[j * e_sh:(j + 1) * e_sh, :] = _pick_slab(j, own_ref, land_ref, slice(0, e_sh), per_peer=False)

    return pl.pallas_call(
        body, name="unshard_w_out", grid=(D // CB,),
        in_specs=[pl.BlockSpec((e_sh, CB), lambda i: (0, i)), pl.BlockSpec((NDEV, e_sh, CB), lambda i: (0, 0, i))],
        out_specs=pl.BlockSpec((D, CB), lambda i: (0, i)),
        out_shape=jax.ShapeDtypeStruct((D, D), BF16),
        compiler_params=_params(),
    )(own, land)


def _unshard_w_in(w_all):
    def body(w_ref, wt_ref):
        def ref_rows(lo, hi):
            pieces, r = [], lo
            while r < hi:
                sh, off = divmod(r, WSH)
                n = min(hi - r, WSH - off)
                pieces.append(w_ref[sh, off:off + n, :])
                r += n
            return pieces

        for s in range(NSEC):
            lo = s * DA if s < 3 else s * DA + H
            wt_ref[s * DA:(s + 1) * DA, :] = jnp.concatenate(ref_rows(lo, lo + DA), axis=0)
        wt_ref[NSEC * DA:DPROJ, :] = jnp.concatenate(
            ref_rows(F0, F0 + H) + [jnp.zeros((DF - H, CB), BF16)], axis=0)

    return pl.pallas_call(
        body, name="unshard_w_in", grid=(D // CB,),
        in_specs=[pl.BlockSpec((NDEV, WSHP, CB), lambda i: (0, 0, i))],
        out_specs=pl.BlockSpec((DPROJ, CB), lambda i: (0, i)),
        out_shape=jax.ShapeDtypeStruct((DPROJ, D), BF16),
        compiler_params=_params(),
    )(w_all)


def _shard_w_in_grads(dw_main, dw_f):
    def body(dm_ref, df_ref, p_ref):
        def ref_rows(lo, hi):
            pieces, r = [], lo
            while r < hi:
                if r < F0:
                    n = min(hi, F0) - r
                    pieces.append(dm_ref[r:r + n, :])
                elif r < F0 + H:
                    n = min(hi, F0 + H) - r
                    pieces.append(df_ref[r - F0:r - F0 + n, :])
                else:
                    n = hi - r
                    pieces.append(dm_ref[r - H:r - H + n, :])
                r += n
            return pieces

        for i in range(NDEV):
            rows = jnp.concatenate(ref_rows(i * WSH, (i + 1) * WSH) + [jnp.zeros((WSHP - WSH, CB), F32)], axis=0)
            p_ref[i] = rows.astype(BF16)

    col = lambda rows: pl.BlockSpec((rows, CB), lambda i: (0, i))
    return pl.pallas_call(
        body, name="shard_w_in_grads", grid=(D // CB,),
        in_specs=[col(NSEC * DA), col(DF)],
        out_specs=pl.BlockSpec((NDEV, WSHP, CB), lambda i: (0, 0, i)),
        out_shape=jax.ShapeDtypeStruct((NDEV, WSHP, D), BF16),
        compiler_params=_params(),
    )(dw_main, dw_f)


def _adamw_small(own, land, w, m, v):
    shape = jax.ShapeDtypeStruct((SROWS, TB), F32)

    def body(own_ref, land_ref, w_ref, m_ref, v_ref, g_out, d_out, m_out, v_out):
        g = _pick_slab(0, own_ref, land_ref, slice(0, SROWS))
        for j in range(1, NDEV):
            g = g + _pick_slab(j, own_ref, land_ref, slice(0, SROWS))
        d, mn, vn = _adamw(w_ref[...], g, m_ref[...], v_ref[...])
        g_out[...], d_out[...], m_out[...], v_out[...] = g, d, mn, vn

    return pl.pallas_call(body, name="adamw_small", out_shape=[shape] * 4)(own, land, w, m, v)


def _tile_rows(a, rows, lanes=TB):
    a = a.reshape(rows, lanes)
    return jnp.pad(a, ((0, -rows % 8), (0, TB - lanes)))


def _pack_small(norm_g, final_norm_g, attn_norm_g, conv_norm_g, b_f, meta_sh, conv_w_sh, loss=None):
    b_row = b_f.reshape(1, H) if loss is None else jnp.concatenate([b_f.reshape(1, H), loss.reshape(1, 1)], axis=1)
    packed = jnp.concatenate([
        _tile_rows(norm_g, 8), _tile_rows(final_norm_g, 8), _tile_rows(attn_norm_g, 4), _tile_rows(conv_norm_g, 4),
        _tile_rows(b_row, 1, b_row.shape[1]), _tile_rows(meta_sh, NM), _tile_rows(conv_w_sh, 3, DH)], axis=0)
    assert packed.shape == (SROWS, TB)
    return packed


def _unpack_small(p):
    return dict(
        norm_g=p[0:8].reshape(1, D), final_norm_g=p[8:16].reshape(D), attn_norm_g=p[16:20].reshape(1, DA),
        conv_norm_g=p[24:28].reshape(1, DA), b_f=p[32:33, :H].reshape(1, H), meta=p[40:56].reshape(NM, TB),
        conv_w=p[56:59, :DH].reshape(1, 3, DH))


def kernel(x, meta, norm_g, w_in, b_f, conv_w, attn_norm_g, conv_norm_g, w_out, final_norm_g, loss_target, m_meta, m_norm_g, m_w_in, m_b_f, m_conv_w, m_attn_norm_g, m_conv_norm_g, m_w_out, m_final_norm_g, v_meta, v_norm_g, v_w_in, v_b_f, v_conv_w, v_attn_norm_g, v_conv_norm_g, v_w_out, v_final_norm_g):
    seq = x.shape[1]
    L = seq + TB
    assert x.shape == (1, seq, D) and L % TT == 0 and w_in.shape == (1, D, WSH)
    x2 = x[0]
    tgt = loss_target[0]

    w_in_slab = jnp.pad(w_in[0].T, ((0, WSHP - WSH), (0, 0))).astype(BF16)
    w_out_slab = w_out[0].astype(BF16)
    small = jnp.concatenate([meta, _tile_rows(conv_w[0], 3, DH)], axis=0)
    wout_flight = _split_start(w_out_slab, "gather_w_out_start", per_peer=False)
    w_all, small_all = _all_gather([w_in_slab, small], "gather_w_in")

    w_t = _unshard_w_in(w_all)
    meta_full = jnp.transpose(small_all[:, :NM, :], (1, 0, 2)).reshape(NM, D)
    conv_w_full = jnp.transpose(small_all[:, NM:NM + 3, :DH], (1, 0, 2)).reshape(3, DA)

    lane_b = lambda p: jnp.broadcast_to(p.reshape(-1, DA, 1), (p.size // DA, DA, TB))
    cw_b = lane_b(conv_w_full)
    ga_b = lane_b(attn_norm_g)[0]
    gcn_b = lane_b(conv_norm_g)[0]

    u, proj_t, f_t, ktok, vtok = _inproj_fwd(x2, meta_full, norm_g + wout_flight[4][0, 0], w_t, L)
    cq, kaug, sg = _fgate_fwd(f_t, b_f.reshape(H, 1), ktok, L)
    o_t, lse = _attn_fwd(proj_t, kaug, cq, L)
    mix_t = _gate_fwd(o_t, proj_t, cw_b, ga_b, gcn_b, L)

    w_out_own, w_out_land = _split_wait(wout_flight, mix_t, "gather_w_out_wait", per_peer=False)
    w_out_full = _unshard_w_out(w_out_own, w_out_land)
    dout, dmix_t, dw_out, loss_part, dg_final = _outproj(
        mix_t, w_out_full, x2, meta_full, final_norm_g.reshape(1, D), tgt, L)
    dwo_flight = _split_start(dw_out.reshape(NDEV, D // NDEV, D), "exchange_dw_out_start", per_peer=True)
    do_t, dd, dg5_t, dga_p, dgc_p, dcw_p = _gate_bwd(dmix_t, o_t, proj_t, cw_b, ga_b + dwo_flight[4][0, 0], gcn_b, L)
    dq_t, dk_t, dv_t, dck, dcq = _attn_bwd(proj_t, kaug, vtok, do_t, lse, dd, cq, L)
    df_t, db_f = _fgate_bwd(dcq, dck, sg, L)
    dw_main, dw_f = _inproj_bwd_w(u, dq_t, dk_t, dv_t, dg5_t, df_t, L)
    dwi_flight = _split_start(_shard_w_in_grads(dw_main, dw_f), "exchange_dw_in_start", per_peer=True)
    grad_x, dmeta, dg_norm = _inproj_bwd_x(
        w_t, dq_t, dk_t, dv_t, dg5_t, df_t, dout, x2, meta_full, norm_g + dwi_flight[4][0, 0], L)
    dga = jnp.sum(dga_p, axis=1)
    dgc = jnp.sum(dgc_p, axis=1)
    dcw = jnp.sum(dcw_p, axis=2)
    small_parts = jnp.stack([
        _pack_small(dg_norm, dg_final, dga, dgc, db_f, dmeta[:, j * TB:(j + 1) * TB], dcw[:, j * DH:(j + 1) * DH],
                    loss=loss_part)
        for j in range(NDEV)], axis=0)
    small_flight = _split_start(small_parts, "exchange_small_start", per_peer=True)
    dwo_own, dwo_land = _split_wait(dwo_flight, small_flight[4], "exchange_dw_out_wait", per_peer=True)
    dwi_own, dwi_land = _split_wait(dwi_flight, dwo_land, "exchange_dw_in_wait", per_peer=True)

    big_out = _adamw_big(dwi_own, dwi_land, dwo_own, dwo_land,
                         w_in[0].T, m_w_in[0].T, v_w_in[0].T, w_out, m_w_out, v_w_out)
    g_w_in, d_w_in, nm_w_in, nv_w_in = [a.T[None] for a in big_out[:4]]
    g_w_out, d_w_out, nm_w_out, nv_w_out = big_out[4:]
    wp = _pack_small(norm_g, final_norm_g, attn_norm_g, conv_norm_g, b_f, meta, conv_w)
    mp = _pack_small(m_norm_g, m_final_norm_g, m_attn_norm_g, m_conv_norm_g, m_b_f, m_meta, m_conv_w)
    vp = _pack_small(v_norm_g, v_final_norm_g, v_attn_norm_g, v_conv_norm_g, v_b_f, v_meta, v_conv_w)
    sm_own, sm_land = _split_wait(small_flight, big_out[4], "exchange_small_wait", per_peer=True)
    small_out = _adamw_small(sm_own, sm_land, wp, mp, vp)
    sm = [_unpack_small(p) for p in small_out]
    loss = small_out[0][32, H]
    order = ("meta", "norm_g", "w_in", "b_f", "conv_w", "attn_norm_g", "conv_norm_g", "w_out", "final_norm_g")
    groups = []
    for k, (wi, wo) in enumerate(((g_w_in, g_w_out), (d_w_in, d_w_out), (nm_w_in, nm_w_out), (nv_w_in, nv_w_out))):
        d = dict(sm[k], w_in=wi, w_out=wo)
        groups.append([d[n] for n in order])
    return (loss, grad_x[None], *groups[0], *groups[1], *groups[2], *groups[3])
```

```python
import jax
import jax.numpy as jnp
from jax import lax
from jax.experimental import pallas as pl
from jax.experimental.pallas import tpu as pltpu

F32 = jnp.float32
BF16 = jnp.bfloat16

D = 1024
DA = 512
H = 8
DH = 64
NM = 16
TB = 128
P0 = TB - NM
TT = 3 * TB
HG = 8
NDEV = 8
NSEC = 8
DF = 16
DPROJ = NSEC * DA + DF
WSH = 513
WSHP = 528
WROWS = WSHP + D // NDEV
SROWS = 64
EPS = 1e-6
NEG = -1e30
LOG2E = 1.4426950408889634
LN2 = 0.6931471805599453
QSCALE = DH ** -0.5 * LOG2E
KA = 128
CB = 256
VMEM_LIMIT = 56 * 1024 * 1024

ADAM_LR = 0.001
ADAM_B1 = 0.9
ADAM_B2 = 0.999
ADAM_EPS = 1e-08
ADAM_WD = 0.01
ADAM_STEP = 10

NT_DIMS = (((1,), (1,)), ((), ()))
TN_DIMS = (((0,), (0,)), ((), ()))
MESH = pl.DeviceIdType.MESH


def _params(n_axes=1, vmem=VMEM_LIMIT):
    return pltpu.CompilerParams(dimension_semantics=("arbitrary",) * n_axes, vmem_limit_bytes=vmem)


def _dot(a, b, dims=None):
    if dims is None:
        return jnp.dot(a, b, preferred_element_type=F32)
    return lax.dot_general(a, b, dims, preferred_element_type=F32)


def _my_place():
    return lax.axis_index("x"), lax.axis_index("y"), lax.axis_index("c")


def _all_gather(xs, name):
    n = len(xs)

    def body(*refs):
        x_refs, out_refs = refs[:n], refs[n:2 * n]
        send_sems, recv_sems, local_sems = refs[2 * n:]
        mx, my, mc = _my_place()

        def across(px, py, pc, axis_a):
            flip_x = pc if axis_a else 1 - pc
            return (px + flip_x) % 2, (py + 1 - flip_x) % 2, pc

        def idx(p):
            return 4 * p[0] + 2 * p[1] + p[2]

        me, sib = (mx, my, mc), (mx, my, 1 - mc)
        a_nbr, b_nbr = across(*me, True), across(*me, False)
        diag = across(*b_nbr, True)
        sib_a, sib_b = across(*sib, True), across(*sib, False)
        sib_diag = across(*sib_b, True)

        waits = []
        for t in range(n):
            out_ref = out_refs[t]

            def copy(k, block, to, src=None, out_ref=out_ref, t=t):
                return pltpu.make_async_remote_copy(
                    src_ref=out_ref.at[idx(block)] if src is None else src, dst_ref=out_ref.at[idx(block)],
                    send_sem=send_sems.at[7 * t + k], recv_sem=recv_sems.at[7 * t + k],
                    device_id=to, device_id_type=MESH)

            mine = pltpu.make_async_copy(x_refs[t], out_ref.at[idx(me)], local_sems.at[t])
            mine.start()
            started = [copy(0, me, sib, src=x_refs[t]), copy(1, me, a_nbr, src=x_refs[t]),
                       copy(2, me, b_nbr, src=x_refs[t])]
            for cp in started:
                cp.start()
            waits.append((copy, mine, started))
        relays = ((1, a_nbr, ((3, b_nbr), (4, sib))), (2, b_nbr, ((5, sib),)), (3, diag, ((6, sib),)))
        for landed, block, onward in relays:
            for copy, _, started in waits:
                copy(landed, block, me).wait_recv()
                for k, to in onward:
                    started.append(copy(k, block, to))
                    started[-1].start()
        for copy, mine, started in waits:
            for k, block in ((0, sib), (4, sib_a), (5, sib_b), (6, sib_diag)):
                copy(k, block, me).wait_recv()
            for cp in started:
                cp.wait_send()
            mine.wait()

    any_spec = pl.BlockSpec(memory_space=pl.ANY)
    return pl.pallas_call(
        body, name=name,
        out_shape=[jax.ShapeDtypeStruct((NDEV,) + x.shape, x.dtype) for x in xs],
        in_specs=[any_spec] * n, out_specs=[any_spec] * n,
        scratch_shapes=[pltpu.SemaphoreType.DMA((7 * n,)), pltpu.SemaphoreType.DMA((7 * n,)),
                        pltpu.SemaphoreType.DMA((n,))],
    )(*xs)


_HBM =pl.BlockSpec(memory_space=pltpu.HBM)
_SEM = pl.BlockSpec(memory_space=pltpu.SEMAPHORE)
_EFFECT = pltpu.SideEffectType.DATAFLOW_SIDE_EFFECTING


def _peer_of(m, place):
    mx, my, mc = place
    return ((1 - mx) if m & 4 else mx, (1 - my) if m & 2 else my, (1 - mc) if m & 1 else mc)


def _split_copies(src_ref, land_ref, send_sems, recv_sems, per_peer, incoming):
    place = _my_place()
    me = 4 * place[0] + 2 * place[1] + place[2]
    out = []
    for m in range(1, NDEV):
        px, py, pc = _peer_of(m, place)
        peer = 4 * px + 2 * py + pc
        src = (src_ref.at[me] if incoming else src_ref.at[peer]) if per_peer else src_ref
        out.append(pltpu.make_async_remote_copy(
            src_ref=src, dst_ref=land_ref.at[peer if incoming else me],
            send_sem=send_sems.at[m - 1], recv_sem=recv_sems.at[m - 1],
            device_id=(px, py, pc), device_id_type=MESH))
    return out


def _split_start(src, name, per_peer):
    slab = src.shape[1:] if per_peer else src.shape

    def body(src_ref, land_ref, send_sems, recv_sems, src_thru, land_thru, token):
        for cp in _split_copies(src_ref, land_ref, send_sems, recv_sems, per_peer, incoming=False):
            cp.start()
        token[...] = jnp.zeros_like(token)

    return pl.pallas_call(
        body, name=name,
        out_shape=(pltpu.SemaphoreType.DMA((NDEV - 1,)), pltpu.SemaphoreType.DMA((NDEV - 1,)),
                   pltpu.HBM(src.shape, src.dtype), pltpu.HBM((NDEV,) + slab, src.dtype),
                   jax.ShapeDtypeStruct((8, TB), F32)),
        in_specs=(_HBM, _HBM), out_specs=(_SEM, _SEM, _HBM, _HBM, pl.BlockSpec(memory_space=pltpu.VMEM)),
        input_output_aliases={0: 2, 1: 3},
        compiler_params=pltpu.CompilerParams(has_side_effects=_EFFECT),
    )(pltpu.with_memory_space_constraint(src, pltpu.HBM),
      pltpu.with_memory_space_constraint(lax.empty((NDEV,) + slab, src.dtype), pltpu.HBM))


def _split_wait(handles, after, name, per_peer):
    send_sems, recv_sems, src_thru, land_thru, _ = handles

    def body(src_ref, land_ref, send_sems, recv_sems, after_ref, src_out, land_out):
        for cp in _split_copies(src_ref, land_ref, send_sems, recv_sems, per_peer, incoming=False):
            cp.wait_send()
        for cp in _split_copies(src_ref, land_ref, send_sems, recv_sems, per_peer, incoming=True):
            cp.wait_recv()

    return pl.pallas_call(
        body, name=name,
        out_shape=(pltpu.HBM(src_thru.shape, src_thru.dtype), pltpu.HBM(land_thru.shape, land_thru.dtype)),
        in_specs=(_HBM, _HBM, _SEM, _SEM, pl.BlockSpec(memory_space=pl.ANY)), out_specs=(_HBM, _HBM),
        input_output_aliases={0: 0, 1: 1},
        compiler_params=pltpu.CompilerParams(has_side_effects=_EFFECT),
    )(src_thru, land_thru, send_sems, recv_sems, after)


def _pick_slab(j, own_ref, land_ref, rows, per_peer=True):
    mx, my, mc = _my_place()
    me = 4 * mx + 2 * my + mc
    own = (lambda: own_ref[j, rows, :]) if per_peer else (lambda: own_ref[rows, :])
    return lax.cond(me == j, own, lambda: land_ref[j, rows, :])


def _h_block(t, x_ref, meta_ref):
    first = jnp.concatenate([jnp.zeros((P0, D), F32), meta_ref[...]], axis=0)
    return jnp.where(t == 0, first, x_ref[...])


def _x_spec():
    return pl.BlockSpec((TB, D), lambda t: (jnp.maximum(t - 1, 0), 0))


def _x_specs3():
    return [pl.BlockSpec((TB, D), lambda j: (jnp.maximum(3 * j - 1, 0), 0)),
            pl.BlockSpec((TB, D), lambda j: (3 * j, 0)),
            pl.BlockSpec((TB, D), lambda j: (3 * j + 1, 0))]


def _h_tile(j, xa_ref, xb_ref, xc_ref, meta_ref):
    first = jnp.concatenate([jnp.zeros((P0, D), F32), meta_ref[...]], axis=0)
    return jnp.concatenate([jnp.where(j == 0, first, xa_ref[...]), xb_ref[...], xc_ref[...]], axis=0)


def _full_spec(shape):
    return pl.BlockSpec(shape, lambda *_: (0,) * len(shape))


def _sigmoid(z):
    return 1.0 / (1.0 + jnp.exp(-z))


def _grouped(x):
    return x.reshape(H, DH, x.shape[-1])


def _group_rstd(x3):
    return lax.rsqrt(jnp.mean(x3 * x3, axis=1, keepdims=True) + EPS)


def _lane_tiles_sum(x):
    out = x[:, :TB]
    for i in range(1, x.shape[1] // TB):
        out = out + x[:, i * TB:(i + 1) * TB]
    return out


def _inproj_fwd(x, meta_full, norm_g, w_t, L):
    nj = L // TT

    def body(xa_ref, xb_ref, xc_ref, meta_ref, g_ref, w_ref, u_ref, proj_ref, f_ref, ktok_ref, vtok_ref):
        hb = _h_tile(pl.program_id(0), xa_ref, xb_ref, xc_ref, meta_ref)
        r = lax.rsqrt(jnp.mean(hb * hb, axis=-1, keepdims=True) + EPS)
        u = (hb * r * g_ref[...]).astype(BF16)
        u_ref[...] = u
        for s in range(NSEC):
            p = _dot(u, w_ref[s * DA:(s + 1) * DA, :], NT_DIMS)
            if s == 0:
                p = p * QSCALE
            if s in (1, 2):
                tok_ref = ktok_ref if s == 1 else vtok_ref
                for h in range(H):
                    tok_ref[h] = p[:, h * DH:(h + 1) * DH].astype(BF16)
            proj_ref[s * DA:(s + 1) * DA, :] = p.T.astype(BF16)
        f_ref[...] = _dot(w_ref[NSEC * DA:DPROJ, :], u, NT_DIMS)[:H]

    return pl.pallas_call(
        body, name="inproj_fwd", grid=(nj,),
        in_specs=_x_specs3() + [_full_spec((NM, D)), _full_spec((1, D)), _full_spec((DPROJ, D))],
        out_specs=[
            pl.BlockSpec((TT, D), lambda t: (t, 0)),
            pl.BlockSpec((NSEC * DA, TT), lambda t: (0, t)),
            pl.BlockSpec((H, TT), lambda t: (0, t)),
            pl.BlockSpec((H, TT, DH), lambda t: (0, t, 0)),
            pl.BlockSpec((H, TT, DH), lambda t: (0, t, 0)),
        ],
        out_shape=[
            jax.ShapeDtypeStruct((L, D), BF16),
            jax.ShapeDtypeStruct((NSEC * DA, L), BF16),
            jax.ShapeDtypeStruct((H, L), F32),
            jax.ShapeDtypeStruct((H, L, DH), BF16),
            jax.ShapeDtypeStruct((H, L, DH), BF16),
        ],
        compiler_params=_params(),
    )(x, x, x, meta_full, norm_g, w_t)


def _split3(x):
    hi = x.astype(BF16).astype(F32)
    r = x - hi
    mid = r.astype(BF16).astype(F32)
    return hi, mid, (r - mid).astype(BF16).astype(F32)


def _bias_rows(bias):
    one = jnp.ones((1, TT), F32)
    zero = jnp.zeros((1, TT), F32)
    parts = [zero] * 3 if bias is None else list(_split3(bias))
    return jnp.concatenate([one] * 3 + parts + [zero] * (DF - 6), axis=0).astype(BF16)


def _fgate_fwd(f_t, b_col, ktok, L):
    nb = L // TB

    def body(f_ref, b_ref, ktok_ref, cq_ref, kaug_ref, sg_ref):
        z = f_ref[...] + b_ref[...]
        idx = lax.broadcasted_iota(jnp.int32, (H, L), 1)
        real = idx >= P0
        lf = jnp.where(real, jnp.minimum(z, 0.0) - jnp.log1p(jnp.exp(-jnp.abs(z))), 0.0)
        sg_ref[...] = jnp.where(real, 1.0 / (1.0 + jnp.exp(z)), 0.0)
        c = lf
        s = 1
        while s < L:
            c = c + jnp.where(idx >= s, pltpu.roll(c, s, 1), 0.0)
            s *= 2
        c = c * LOG2E
        for h in range(H):
            cq_ref[h] = c[h:h + 1, :]
        ck = jnp.where(real, c, -NEG)
        lane = lax.broadcasted_iota(jnp.int32, (TB, KA), 1)
        tail = jnp.where((lane >= DH + 3) & (lane < DH + 6), 1.0, 0.0)
        for h in range(H):
            for b in range(nb):
                blk = slice(b * TB, (b + 1) * TB)
                col = jnp.broadcast_to(ck[h:h + 1, blk], (TB, TB)).T
                hi, mid, lo = _split3(-col)
                k = jnp.concatenate([ktok_ref[h, blk, :].astype(F32), jnp.zeros((TB, KA - DH), F32)], axis=1)
                out = jnp.where(lane < DH, k, jnp.where(lane == DH, hi, jnp.where(
                    lane == DH + 1, mid, jnp.where(lane == DH + 2, lo, tail))))
                kaug_ref[h, blk, :] = out.astype(BF16)

    return pl.pallas_call(
        body, name="fgate_fwd",
        out_shape=[
            jax.ShapeDtypeStruct((H, 1, L), F32),
            jax.ShapeDtypeStruct((H, L, KA), BF16),
            jax.ShapeDtypeStruct((H, L), F32),
        ],
        compiler_params=pltpu.CompilerParams(vmem_limit_bytes=VMEM_LIMIT),
    )(f_t, b_col, ktok)


def _causal_mask():
    r = lax.broadcasted_iota(jnp.int32, (TT, TT), 0)
    c = lax.broadcasted_iota(jnp.int32, (TT, TT), 1)
    return r <= c


def _attn_fwd(proj_t, kaug, cq, L):
    nq = L // TT

    def body(q_ref, kaug_ref, v_ref, cq_ref, o_ref, lse_ref,
             qa_scr, s_scr, cmax_scr, m_scr, p_scr, alpha_scr, acc_scr):
        j = pl.program_id(1)
        rows = [slice(g * DH, (g + 1) * DH) for g in range(HG)]
        ones = jnp.ones((DF, TT), BF16)
        for g in range(HG):
            qa_scr[g] = jnp.concatenate(
                [q_ref[rows[g], :], _bias_rows(None), jnp.zeros((KA - DH - DF, TT), BF16)], axis=0)

        def scores(kt, masked):
            k_off = pl.multiple_of(kt * TT, TT)
            for g in range(HG):
                s = _dot(kaug_ref[g, pl.ds(k_off, TT), :], qa_scr[g])
                if masked:
                    s = jnp.where(_causal_mask(), s, NEG)
                s_scr[g] = s
                cmax_scr[g] = jnp.max(s, axis=0, keepdims=True)

        def softmax():
            for g in range(HG):
                m_old = m_scr[g]
                m_new = jnp.maximum(m_old, cmax_scr[g])
                alpha_scr[g] = jnp.exp2(m_old - m_new)
                p_scr[g] = jnp.exp2(s_scr[g] - m_new).astype(BF16)
                m_scr[g] = m_new

        def weighted_sum(kt):
            k_off = pl.multiple_of(kt * TT, TT)
            for g in range(HG):
                v1 = jnp.concatenate([v_ref[rows[g], pl.ds(k_off, TT)], ones], axis=0)
                acc_scr[g] = alpha_scr[g] * acc_scr[g] + _dot(v1, p_scr[g])

        m_scr[...] = jnp.full_like(m_scr, NEG)
        acc_scr[...] = jnp.zeros_like(acc_scr)

        scores(j, True)

        @pl.when(j >= 1)
        def _():
            softmax()
            scores(j - 1, False)

        def step(i, c):
            weighted_sum(j - i + 1)
            softmax()
            scores(j - i - 1, False)
            return c

        lax.fori_loop(1, j, step, 0)

        @pl.when(j >= 1)
        def _():
            weighted_sum(1)

        softmax()
        weighted_sum(0)
        for g in range(HG):
            l = acc_scr[g, DH:DH + 1, :]
            o_ref[rows[g], :] = acc_scr[g, :DH, :] * (1.0 / l)
            lse_ref[g] = m_scr[g] + jnp.log2(l) + cq_ref[g]

    return pl.pallas_call(
        body, name="attn_fwd", grid=(H // HG, nq),
        in_specs=[
            pl.BlockSpec((HG * DH, TT), lambda h, j: (h, j)),
            pl.BlockSpec((HG, L, KA), lambda h, j: (h, 0, 0)),
            pl.BlockSpec((HG * DH, L), lambda h, j: (2 * H // HG + h, 0)),
            pl.BlockSpec((HG, 1, TT), lambda h, j: (h, 0, j)),
        ],
        out_specs=[
            pl.BlockSpec((HG * DH, TT), lambda h, j: (h, j)),
            pl.BlockSpec((HG, 1, TT), lambda h, j: (h, 0, j)),
        ],
        out_shape=[jax.ShapeDtypeStruct((DA, L), F32), jax.ShapeDtypeStruct((H, 1, L), F32)],
        scratch_shapes=[pltpu.VMEM((HG, KA, TT), BF16), pltpu.VMEM((HG, TT, TT), F32), pltpu.VMEM((HG, 1, TT), F32),
                        pltpu.VMEM((HG, 1, TT), F32), pltpu.VMEM((HG, TT, TT), BF16), pltpu.VMEM((HG, 1, TT), F32),
                        pltpu.VMEM((HG, DH + DF, TT), F32)],
        compiler_params=_params(2),
    )(proj_t, kaug, proj_t, cq)


def _gate_common(o, za, gb, gc, xc, zc, gcp, xcp, cw_ref, ga_ref, gcn_ref, first):
    n_rep = TT // TB
    a = gc * xc
    a_prev = jnp.where(first, 0.0, gcp * xcp)
    full = jnp.concatenate([a_prev, a], axis=1)
    a1 = pltpu.roll(full, 1, 1)[:, TB:]
    a2 = pltpu.roll(full, 2, 1)[:, TB:]
    w0 = jnp.tile(cw_ref[0], (1, n_rep))
    w1 = jnp.tile(cw_ref[1], (1, n_rep))
    w2 = jnp.tile(cw_ref[2], (1, n_rep))
    cv = w0 * a2 + w1 * a1 + w2 * a
    e = gb * cv
    e3 = _grouped(e)
    rc = _group_rstd(e3)
    ec = (e3 * rc).reshape(DA, TT)
    o3 = _grouped(o)
    ra = _group_rstd(o3)
    oa = (o3 * ra).reshape(DA, TT)
    g_a = jnp.tile(ga_ref[...], (1, n_rep))
    g_c = jnp.tile(gcn_ref[...], (1, n_rep))
    sa = _sigmoid(za)
    sc = _sigmoid(zc)
    return dict(a=a, a1=a1, a2=a2, w0=w0, w1=w1, w2=w2, cv=cv, e=e, rc=rc, ec=ec, ra=ra, oa=oa,
                g_a=g_a, g_c=g_c, sa=sa, sc=sc)


def _gate_specs(nj, rev):
    def jj(i):
        return (nj - 1 - i) if rev else i

    def sec(s):
        return pl.BlockSpec((DA, TT), lambda i: (s, jj(i)))

    def halo(s):
        return pl.BlockSpec((DA, TB), lambda i: (s, jnp.maximum(3 * jj(i) - 1, 0)))

    return [pl.BlockSpec((DA, TT), lambda i: (0, jj(i))), sec(3), sec(4), sec(5), sec(6), sec(7), halo(5), halo(6),
            _full_spec((3, DA, TB)), _full_spec((DA, TB)), _full_spec((DA, TB))]


def _gate_fwd(o_t, proj_t, cw_b, ga_b, gcn_b, L):
    nj = L // TT

    def body(o_ref, za_ref, gb_ref, gc_ref, xc_ref, zc_ref, gcp_ref, xcp_ref, cw_ref, ga_ref, gcn_ref, mix_ref):
        j = pl.program_id(0)
        f32 = lambda r: r[...].astype(F32)
        za, zc = f32(za_ref), f32(zc_ref)
        g = _gate_common(o_ref[...], za, f32(gb_ref), f32(gc_ref), f32(xc_ref), zc, f32(gcp_ref), f32(xcp_ref),
                         cw_ref, ga_ref, gcn_ref, j == 0)
        mix_ref[:DA, :] = (g["oa"] * g["g_a"] * (za * g["sa"])).astype(BF16)
        mix_ref[DA:, :] = (g["ec"] * g["g_c"] * (zc * g["sc"])).astype(BF16)

    return pl.pallas_call(
        body, name="gate_fwd", grid=(nj,),
        in_specs=_gate_specs(nj, False),
        out_specs=pl.BlockSpec((2 * DA, TT), lambda j: (0, j)),
        out_shape=jax.ShapeDtypeStruct((2 * DA, L), BF16),
        compiler_params=_params(),
    )(o_t, proj_t, proj_t, proj_t, proj_t, proj_t, proj_t, proj_t, cw_b, ga_b, gcn_b)


def _outproj(mix_t, w_out, x, meta_full, fng, target, L):
    nj = L // TT

    def body(mix_ref, w_ref, xa_ref, xb_ref, xc_ref, meta_ref, g_ref, ta_ref, tb_ref, tc_ref,
             dout_ref, dmix_ref, dwb_ref, loss_ref, dg_ref, dw_ref):
        t = pl.program_id(0)

        @pl.when(t == 0)
        def _():
            dw_ref[...] = jnp.zeros_like(dw_ref)
            loss_ref[...] = jnp.zeros_like(loss_ref)
            dg_ref[...] = jnp.zeros_like(dg_ref)

        mix = mix_ref[...]
        o = _dot(mix, w_ref[...], TN_DIMS) + _h_tile(t, xa_ref, xb_ref, xc_ref, meta_ref)
        r = lax.rsqrt(jnp.mean(o * o, axis=-1, keepdims=True) + EPS)
        g = g_ref[...]
        orn = o * r
        tgt = jnp.concatenate([ta_ref[...], tb_ref[...], tc_ref[...]], axis=0)
        row = lax.broadcasted_iota(jnp.int32, (TT, 1), 0)
        real = jnp.where((t > 0) | (row >= TB), 1.0, 0.0)
        diff = (orn * g - tgt) * real
        loss_ref[...] += 0.5 * jnp.sum(diff * diff) * (1.0 / D)
        dy = diff * (1.0 / D)
        dg_ref[...] += jnp.sum(dy * orn, axis=0, keepdims=True)
        gy = dy * g
        dout = r * gy - orn * (r * jnp.mean(gy * orn, axis=-1, keepdims=True))
        dout_ref[...] = dout
        db = dout.astype(BF16)
        dmix_ref[...] = _dot(db, w_ref[...], NT_DIMS).T.astype(BF16)
        dw_ref[...] += _dot(mix, db)

        @pl.when(t == nj - 1)
        def _():
            dwb_ref[...] = dw_ref[...].astype(BF16)

    return pl.pallas_call(
        body, name="outproj", grid=(nj,),
        in_specs=[pl.BlockSpec((D, TT), lambda t: (0, t)), _full_spec((D, D))] + _x_specs3()
                 + [_full_spec((NM, D)), _full_spec((1, D))] + _x_specs3(),
        out_specs=[pl.BlockSpec((TT, D), lambda t: (t, 0)), pl.BlockSpec((D, TT), lambda t: (0, t)),
                   _full_spec((D, D)), _full_spec((1, 1)), _full_spec((1, D))],
        out_shape=[jax.ShapeDtypeStruct((L, D), F32), jax.ShapeDtypeStruct((D, L), BF16),
                   jax.ShapeDtypeStruct((D, D), BF16), jax.ShapeDtypeStruct((1, 1), F32),
                   jax.ShapeDtypeStruct((1, D), F32)],
        scratch_shapes=[pltpu.VMEM((D, D), F32)],
        compiler_params=_params(),
    )(mix_t, w_out, x, x, x, meta_full, fng, target, target, target)


def _gate_bwd(dmix_t, o_t, proj_t, cw_b, ga_b, gcn_b, L):
    nj = L // TT

    def body(dmix_ref, o_ref, za_ref, gb_ref, gc_ref, xc_ref, zc_ref, gcp_ref, xcp_ref, cw_ref, ga_ref, gcn_ref,
             do_ref, dd_ref, dg5_ref, dga_ref, dgc_ref, dcw_ref, carry_ref):
        i = pl.program_id(0)
        j = nj - 1 - i

        @pl.when(i == 0)
        def _():
            carry_ref[...] = jnp.zeros_like(carry_ref)
            dga_ref[...] = jnp.zeros_like(dga_ref)
            dgc_ref[...] = jnp.zeros_like(dgc_ref)
            dcw_ref[...] = jnp.zeros_like(dcw_ref)

        f32 = lambda r: r[...].astype(F32)
        o, za, gb, gc, xc, zc = o_ref[...], f32(za_ref), f32(gb_ref), f32(gc_ref), f32(xc_ref), f32(zc_ref)
        g = _gate_common(o, za, gb, gc, xc, zc, f32(gcp_ref), f32(xcp_ref), cw_ref, ga_ref, gcn_ref, j == 0)
        dya = dmix_ref[:DA, :].astype(F32)
        dyc = dmix_ref[DA:, :].astype(F32)
        sa, sc = g["sa"], g["sc"]

        dn = dya * (za * sa)
        dg5_ref[0:DA, :] = (dya * (g["oa"] * g["g_a"]) * (sa * (1.0 + za * (1.0 - sa)))).astype(BF16)
        dga_ref[...] += _lane_tiles_sum(dn * g["oa"])
        dng = dn * g["g_a"]
        mean_a = jnp.mean(_grouped(dng * g["oa"]), axis=1, keepdims=True)
        do = ((_grouped(dng) - _grouped(g["oa"]) * mean_a) * g["ra"]).reshape(DA, TT)
        do_ref[...] = do.astype(BF16)
        dd = jnp.sum(_grouped(do * o), axis=1)
        for h in range(H):
            dd_ref[h] = dd[h:h + 1, :]

        dnc = dyc * (zc * sc)
        dg5_ref[4 * DA:5 * DA, :] = (dyc * (g["ec"] * g["g_c"]) * (sc * (1.0 + zc * (1.0 - sc)))).astype(BF16)
        dgc_ref[...] += _lane_tiles_sum(dnc * g["ec"])
        dncg = dnc * g["g_c"]
        mean_c = jnp.mean(_grouped(dncg * g["ec"]), axis=1, keepdims=True)
        de = ((_grouped(dncg) - _grouped(g["ec"]) * mean_c) * g["rc"]).reshape(DA, TT)
        dg5_ref[DA:2 * DA, :] = (de * g["cv"]).astype(BF16)
        dcv = de * gb
        full = jnp.concatenate([dcv, carry_ref[...]], axis=1)
        d1 = pltpu.roll(full, TT + TB - 1, 1)[:, :TT]
        d2 = pltpu.roll(full, TT + TB - 2, 1)[:, :TT]
        carry_ref[...] = dcv[:, :TB]
        da = g["w2"] * dcv + g["w1"] * d1 + g["w0"] * d2
        dg5_ref[2 * DA:3 * DA, :] = (da * xc).astype(BF16)
        dg5_ref[3 * DA:4 * DA, :] = (da * gc).astype(BF16)
        dcw_ref[0] += _lane_tiles_sum(dcv * g["a2"])
        dcw_ref[1] += _lane_tiles_sum(dcv * g["a1"])
        dcw_ref[2] += _lane_tiles_sum(dcv * g["a"])

    rj = lambda i: nj - 1 - i
    return pl.pallas_call(
        body, name="gate_bwd", grid=(nj,),
        in_specs=[pl.BlockSpec((2 * DA, TT), lambda i: (0, rj(i)))] + _gate_specs(nj, True),
        out_specs=[
            pl.BlockSpec((DA, TT), lambda i: (0, rj(i))),
            pl.BlockSpec((H, 1, TT), lambda i: (0, 0, rj(i))),
            pl.BlockSpec((5 * DA, TT), lambda i: (0, rj(i))),
            _full_spec((DA, TB)), _full_spec((DA, TB)), _full_spec((3, DA, TB)),
        ],
        out_shape=[
            jax.ShapeDtypeStruct((DA, L), BF16),
            jax.ShapeDtypeStruct((H, 1, L), F32),
            jax.ShapeDtypeStruct((5 * DA, L), BF16),
            jax.ShapeDtypeStruct((DA, TB), F32),
            jax.ShapeDtypeStruct((DA, TB), F32),
            jax.ShapeDtypeStruct((3, DA, TB), F32),
        ],
        scratch_shapes=[pltpu.VMEM((DA, TB), F32)],
        compiler_params=_params(),
    )(dmix_t, o_t, proj_t, proj_t, proj_t, proj_t, proj_t, proj_t, proj_t, cw_b, ga_b, gcn_b)


def _attn_bwd(proj_t, kaug, vtok, do_t, lse, dd, cq, L):
    nk = L // TT

    def body(q_ref, kaug_ref, vtok_ref, kt_ref, do_ref, lse_ref, dd_ref, cq_ref,
             dq_ref, dk_ref, dv_ref, dck_ref, dcq_ref, dq_acc, kt1_scr, s_scr, dp_scr, dv_scr, dk_scr):
        i = pl.program_id(1)

        @pl.when(i == 0)
        def _():
            dq_acc[...] = jnp.zeros_like(dq_acc)

        rows = [slice(g * DH, (g + 1) * DH) for g in range(HG)]
        ones = jnp.ones((DF, TT), BF16)
        zpad = jnp.zeros((KA - DH - DF, TT), BF16)
        for g in range(HG):
            kt1_scr[g] = jnp.concatenate([kt_ref[rows[g], :], ones], axis=0)
        dv_scr[...] = jnp.zeros_like(dv_scr)
        dk_scr[...] = jnp.zeros_like(dk_scr)

        def q_rows(g, q_off):
            bias = cq_ref[g, :, pl.ds(q_off, TT)] - lse_ref[g, :, pl.ds(q_off, TT)]
            return jnp.concatenate([q_ref[rows[g], pl.ds(q_off, TT)], _bias_rows(bias)], axis=0)

        def scores(jq, masked):
            q_off = pl.multiple_of(jq * TT, TT)
            for g in range(HG):
                s = _dot(kaug_ref[g], jnp.concatenate([q_rows(g, q_off), zpad], axis=0))
                if masked:
                    s = jnp.where(_causal_mask(), s, NEG)
                s_scr[g] = s
                dp_scr[g] = _dot(vtok_ref[g], do_ref[rows[g], pl.ds(q_off, TT)])

        def grads(jq):
            q_off = pl.multiple_of(jq * TT, TT)
            for g in range(HG):
                p = jnp.exp2(s_scr[g])
                ds = (p * (dp_scr[g] - dd_ref[g, :, pl.ds(q_off, TT)])).astype(BF16)
                do1 = jnp.concatenate([do_ref[rows[g], pl.ds(q_off, TT)], jnp.zeros((KA - DH, TT), BF16)], axis=0)
                q1 = jnp.concatenate([q_rows(g, q_off), zpad], axis=0)
                dv_scr[g] += _dot(p.astype(BF16), do1, NT_DIMS)
                dk_scr[g] += _dot(ds, q1, NT_DIMS)
                dq_acc[g, :, pl.ds(q_off, TT)] += _dot(kt1_scr[g], ds)

        scores(i, True)

        def step(jq, c):
            grads(jq)
            scores(jq + 1, False)
            return c

        lax.fori_loop(i, nk - 1, step, 0)
        grads(nk - 1)
        for g in range(HG):
            dv_ref[rows[g], :] = dv_scr[g].T[:DH, :].astype(BF16)
            dk_t = dk_scr[g].T
            dk_ref[rows[g], :] = (dk_t[:DH, :] * LN2).astype(BF16)
            dck_ref[g] = dk_t[DH:DH + 1, :]

        @pl.when(i == nk - 1)
        def _():
            for g in range(HG):
                dq_ref[rows[g], :] = (dq_acc[g, :DH, :] * (DH ** -0.5)).astype(BF16)
                dcq_ref[g] = dq_acc[g, DH:DH + 1, :]

    head = lambda h, i: (h, 0)
    row = lambda h, i: (h, 0, 0)
    return pl.pallas_call(
        body, name="attn_bwd", grid=(H // HG, nk),
        in_specs=[
            pl.BlockSpec((HG * DH, L), head),
            pl.BlockSpec((HG, TT, KA), lambda h, i: (h, i, 0)),
            pl.BlockSpec((HG, TT, DH), lambda h, i: (h, i, 0)),
            pl.BlockSpec((HG * DH, TT), lambda h, i: (H // HG + h, i)),
            pl.BlockSpec((HG * DH, L), head),
            pl.BlockSpec((HG, 1, L), row), pl.BlockSpec((HG, 1, L), row), pl.BlockSpec((HG, 1, L), row),
        ],
        out_specs=[
            pl.BlockSpec((HG * DH, L), head),
            pl.BlockSpec((HG * DH, TT), lambda h, i: (h, i)),
            pl.BlockSpec((HG * DH, TT), lambda h, i: (h, i)),
            pl.BlockSpec((HG, 1, TT), lambda h, i: (h, 0, i)),
            pl.BlockSpec((HG, 1, L), row),
        ],
        out_shape=[jax.ShapeDtypeStruct((DA, L), BF16), jax.ShapeDtypeStruct((DA, L), BF16),
                   jax.ShapeDtypeStruct((DA, L), BF16), jax.ShapeDtypeStruct((H, 1, L), F32),
                   jax.ShapeDtypeStruct((H, 1, L), F32)],
        scratch_shapes=[
            pltpu.VMEM((HG, DH + DF, L), F32),
            pltpu.VMEM((HG, DH + DF, TT), BF16),
            pltpu.VMEM((HG, TT, TT), F32), pltpu.VMEM((HG, TT, TT), F32),
            pltpu.VMEM((HG, TT, KA), F32), pltpu.VMEM((HG, TT, KA), F32)],
        compiler_params=_params(2),
    )(proj_t, kaug, vtok, proj_t, do_t, lse, dd, cq)


def _fgate_bwd(dcq, dck, sg, L):
    def body(dcq_ref, dck_ref, sg_ref, df_ref, db_ref):
        dc = jnp.concatenate([dcq_ref[h] - dck_ref[h] for h in range(H)], axis=0)
        idx = lax.broadcasted_iota(jnp.int32, (H, L), 1)
        r = dc
        s = 1
        while s < L:
            r = r + jnp.where(idx + s < L, pltpu.roll(r, L - s, 1), 0.0)
            s *= 2
        df = r * sg_ref[...]
        db_ref[...] = jnp.sum(df, axis=1, keepdims=True)
        df_ref[...] = jnp.concatenate([df, jnp.zeros((DF - H, L), F32)], axis=0).astype(BF16)

    return pl.pallas_call(
        body, name="fgate_bwd",
        out_shape=[jax.ShapeDtypeStruct((DF, L), BF16), jax.ShapeDtypeStruct((H, 1), F32)],
        compiler_params=pltpu.CompilerParams(vmem_limit_bytes=VMEM_LIMIT),
    )(dcq, dck, sg)


def _inproj_bwd_x(w, dq_t, dk_t, dv_t, dg5_t, df_t, dout, x, meta_full, norm_g, L):
    nb = L // TB
    seq = x.shape[0]

    def body(w_ref, dq_ref, dk_ref, dv_ref, dg5_ref, df_ref, dout_ref, x_ref, meta_ref, g_ref,
             gx_ref, dmeta_ref, dg_ref):
        t = pl.program_id(0)

        @pl.when(t == 0)
        def _():
            dg_ref[...] = jnp.zeros_like(dg_ref)

        du = _dot(dq_ref[...], w_ref[0:DA, :], TN_DIMS)
        du += _dot(dk_ref[...], w_ref[DA:2 * DA, :], TN_DIMS)
        du += _dot(dv_ref[...], w_ref[2 * DA:3 * DA, :], TN_DIMS)
        du += _dot(dg5_ref[...], w_ref[3 * DA:NSEC * DA, :], TN_DIMS)
        du += _dot(df_ref[...], w_ref[NSEC * DA:DPROJ, :], TN_DIMS)
        hb = _h_block(t, x_ref, meta_ref)
        r = lax.rsqrt(jnp.mean(hb * hb, axis=-1, keepdims=True) + EPS)
        hn = hb * r
        dg_ref[...] += jnp.sum(du * hn, axis=0, keepdims=True)
        gu = du * g_ref[...]
        dh = dout_ref[...] + r * gu - hn * (r * jnp.mean(gu * hn, axis=-1, keepdims=True))
        gx_ref[...] = dh

        @pl.when(t == 0)
        def _():
            dmeta_ref[...] = dh[P0:, :]

    blk = lambda rows: pl.BlockSpec((rows, TB), lambda t: (0, t))
    return pl.pallas_call(
        body, name="inproj_bwd_x", grid=(nb,),
        in_specs=[_full_spec((DPROJ, D)), blk(DA), blk(DA), blk(DA), blk(5 * DA), blk(DF),
                  pl.BlockSpec((TB, D), lambda t: (t, 0)), _x_spec(), _full_spec((NM, D)), _full_spec((1, D))],
        out_specs=[_x_spec(), _full_spec((NM, D)), _full_spec((1, D))],
        out_shape=[jax.ShapeDtypeStruct((seq, D), F32), jax.ShapeDtypeStruct((NM, D), F32),
                   jax.ShapeDtypeStruct((1, D), F32)],
        compiler_params=_params(),
    )(w, dq_t, dk_t, dv_t, dg5_t, df_t, dout, x, meta_full, norm_g)


def _inproj_bwd_w(u, dq_t, dk_t, dv_t, dg5_t, df_t, L):
    def body(u_ref, dq_ref, dk_ref, dv_ref, dg5_ref, df_ref, dw_ref, dwf_ref):
        s = pl.program_id(0)
        u_all = u_ref[...]

        @pl.when(s < 5)
        def _():
            dw_ref[...] = _dot(dg5_ref[...], u_all)

        for step, ref in ((5, dq_ref), (6, dk_ref), (7, dv_ref)):
            @pl.when(s == step)
            def _(ref=ref):
                dw_ref[...] = _dot(ref[...], u_all)

        @pl.when(s == NSEC - 1)
        def _():
            dwf_ref[...] = _dot(df_ref[...], u_all)

    once = lambda shape: pl.BlockSpec(shape, lambda s: (0, 0), pipeline_mode=pl.Buffered(1))
    return pl.pallas_call(
        body, name="inproj_bwd_w", grid=(NSEC,),
        in_specs=[
            once((L, D)), once((DA, L)), once((DA, L)), once((DA, L)),
            pl.BlockSpec((DA, L), lambda s: (jnp.minimum(s, 4), 0)),
            once((DF, L)),
        ],
        out_specs=[pl.BlockSpec((DA, D), lambda s: (jnp.where(s < 5, s + 3, s - 5), 0)), _full_spec((DF, D))],
        out_shape=[jax.ShapeDtypeStruct((NSEC * DA, D), F32), jax.ShapeDtypeStruct((DF, D), F32)],
        compiler_params=_params(),
    )(u, dq_t, dk_t, dv_t, dg5_t, df_t)


def _adamw(w, g, m, v):
    m = ADAM_B1 * m + (1.0 - ADAM_B1) * g
    v = ADAM_B2 * v + (1.0 - ADAM_B2) * (g * g)
    m_hat = m / (1.0 - ADAM_B1 ** ADAM_STEP)
    v_hat = v / (1.0 - ADAM_B2 ** ADAM_STEP)
    delta = -ADAM_LR * (m_hat / (jnp.sqrt(v_hat) + ADAM_EPS) + ADAM_WD * w)
    return delta, m, v


def _adamw_big(own_in, land_in, own_out, land_out, w_in_t, m_in_t, v_in_t, w_out, m_out, v_out):
    cb = CB
    e_sh = D // NDEV
    in_shape = jax.ShapeDtypeStruct(w_in_t.shape, F32)
    out_shape = jax.ShapeDtypeStruct(w_out.shape, F32)

    def total(own_ref, land_ref, rows):
        g = _pick_slab(0, own_ref, land_ref, rows).astype(F32)
        for j in range(1, NDEV):
            g = g + _pick_slab(j, own_ref, land_ref, rows).astype(F32)
        return g

    def body(oi_ref, li_ref, oo_ref, lo_ref, wi_ref, mi_ref, vi_ref, wo_ref, mo_ref, vo_ref,
             gi, di, mi, vi, go, do, mo, vo):
        g = total(oi_ref, li_ref, slice(0, WSHP))[:WSH]
        d, mn, vn = _adamw(wi_ref[...], g, mi_ref[...], vi_ref[...])
        gi[...], di[...], mi[...], vi[...] = g, d, mn, vn
        g = total(oo_ref, lo_ref, slice(0, e_sh))
        d, mn, vn = _adamw(wo_ref[0], g, mo_ref[0], vo_ref[0])
        go[0], do[0], mo[0], vo[0] = g, d, mn, vn

    slab = lambda rows: pl.BlockSpec((NDEV, rows, cb), lambda i: (0, 0, i))
    ispec = pl.BlockSpec((WSH, cb), lambda i: (0, i))
    ospec = pl.BlockSpec((1, e_sh, cb), lambda i: (0, 0, i))
    return pl.pallas_call(
        body, name="adamw_big", grid=(D // cb,),
        in_specs=[slab(WSHP), slab(WSHP), slab(e_sh), slab(e_sh), ispec, ispec, ispec, ospec, ospec, ospec],
        out_specs=[ispec] * 4 + [ospec] * 4, out_shape=[in_shape] * 4 + [out_shape] * 4,
        compiler_params=_params(),
    )(own_in, land_in, own_out, land_out, w_in_t, m_in_t, v_in_t, w_out, m_out, v_out)


F0 = 3 * DA


def _unshard_w_out(own, land):
    e_sh = D // NDEV

    def body(own_ref, land_ref, wo_ref):
        for j in range(NDEV):
            wo_ref[j * e_sh:(j + 1) * e_sh, :] = _pick_slab(j, own_ref, land_ref, slice(0, e_sh), per_peer=False)

    return pl.pallas_call(
        body, name="unshard_w_out", grid=(D // CB,),
        in_specs=[pl.BlockSpec((e_sh, CB), lambda i: (0, i)), pl.BlockSpec((NDEV, e_sh, CB), lambda i: (0, 0, i))],
        out_specs=pl.BlockSpec((D, CB), lambda i: (0, i)),
        out_shape=jax.ShapeDtypeStruct((D, D), BF16),
        compiler_params=_params(),
    )(own, land)


def _unshard_w_in(w_all):
    def body(w_ref, wt_ref):
        def ref_rows(lo, hi):
            pieces, r = [], lo
            while r < hi:
                sh, off = divmod(r, WSH)
                n = min(hi - r, WSH - off)
                pieces.append(w_ref[sh, off:off + n, :])
                r += n
            return pieces

        for s in range(NSEC):
            lo = s * DA if s < 3 else s * DA + H
            wt_ref[s * DA:(s + 1) * DA, :] = jnp.concatenate(ref_rows(lo, lo + DA), axis=0)
        wt_ref[NSEC * DA:DPROJ, :] = jnp.concatenate(
            ref_rows(F0, F0 + H) + [jnp.zeros((DF - H, CB), BF16)], axis=0)

    return pl.pallas_call(
        body, name="unshard_w_in", grid=(D // CB,),
        in_specs=[pl.BlockSpec((NDEV, WSHP, CB), lambda i: (0, 0, i))],
        out_specs=pl.BlockSpec((DPROJ, CB), lambda i: (0, i)),
        out_shape=jax.ShapeDtypeStruct((DPROJ, D), BF16),
        compiler_params=_params(),
    )(w_all)


def _shard_w_in_grads(dw_main, dw_f):
    def body(dm_ref, df_ref, p_ref):
        def ref_rows(lo, hi):
            pieces, r = [], lo
            while r < hi:
                if r < F0:
                    n = min(hi, F0) - r
                    pieces.append(dm_ref[r:r + n, :])
                elif r < F0 + H:
                    n = min(hi, F0 + H) - r
                    pieces.append(df_ref[r - F0:r - F0 + n, :])
                else:
                    n = hi - r
                    pieces.append(dm_ref[r - H:r - H + n, :])
                r += n
            return pieces

        for i in range(NDEV):
            rows = jnp.concatenate(ref_rows(i * WSH, (i + 1) * WSH) + [jnp.zeros((WSHP - WSH, CB), F32)], axis=0)
            p_ref[i] = rows.astype(BF16)

    col = lambda rows: pl.BlockSpec((rows, CB), lambda i: (0, i))
    return pl.pallas_call(
        body, name="shard_w_in_grads", grid=(D // CB,),
        in_specs=[col(NSEC * DA), col(DF)],
        out_specs=pl.BlockSpec((NDEV, WSHP, CB), lambda i: (0, 0, i)),
        out_shape=jax.ShapeDtypeStruct((NDEV, WSHP, D), BF16),
        compiler_params=_params(),
    )(dw_main, dw_f)


def _adamw_small(own, land, w, m, v):
    shape = jax.ShapeDtypeStruct((SROWS, TB), F32)

    def body(own_ref, land_ref, w_ref, m_ref, v_ref, g_out, d_out, m_out, v_out):
        g = _pick_slab(0, own_ref, land_ref, slice(0, SROWS))
        for j in range(1, NDEV):
            g = g + _pick_slab(j, own_ref, land_ref, slice(0, SROWS))
        d, mn, vn = _adamw(w_ref[...], g, m_ref[...], v_ref[...])
        g_out[...], d_out[...], m_out[...], v_out[...] = g, d, mn, vn

    return pl.pallas_call(body, name="adamw_small", out_shape=[shape] * 4)(own, land, w, m, v)


def _tile_rows(a, rows, lanes=TB):
    a = a.reshape(rows, lanes)
    return jnp.pad(a, ((0, -rows % 8), (0, TB - lanes)))


def _pack_small(norm_g, final_norm_g, attn_norm_g, conv_norm_g, b_f, meta_sh, conv_w_sh, loss=None):
    b_row = b_f.reshape(1, H) if loss is None else jnp.concatenate([b_f.reshape(1, H), loss.reshape(1, 1)], axis=1)
    packed = jnp.concatenate([
        _tile_rows(norm_g, 8), _tile_rows(final_norm_g, 8), _tile_rows(attn_norm_g, 4), _tile_rows(conv_norm_g, 4),
        _tile_rows(b_row, 1, b_row.shape[1]), _tile_rows(meta_sh, NM), _tile_rows(conv_w_sh, 3, DH)], axis=0)
    assert packed.shape == (SROWS, TB)
    return packed


def _unpack_small(p):
    return dict(
        norm_g=p[0:8].reshape(1, D), final_norm_g=p[8:16].reshape(D), attn_norm_g=p[16:20].reshape(1, DA),
        conv_norm_g=p[24:28].reshape(1, DA), b_f=p[32:33, :H].reshape(1, H), meta=p[40:56].reshape(NM, TB),
        conv_w=p[56:59, :DH].reshape(1, 3, DH))


def kernel(x, meta, norm_g, w_in, b_f, conv_w, attn_norm_g, conv_norm_g, w_out, final_norm_g, loss_target, m_meta, m_norm_g, m_w_in, m_b_f, m_conv_w, m_attn_norm_g, m_conv_norm_g, m_w_out, m_final_norm_g, v_meta, v_norm_g, v_w_in, v_b_f, v_conv_w, v_attn_norm_g, v_conv_norm_g, v_w_out, v_final_norm_g):
    seq = x.shape[1]
    L = seq + TB
    assert x.shape == (1, seq, D) and L % TT == 0 and w_in.shape == (1, D, WSH)
    x2 = x[0]
    tgt = loss_target[0]

    w_in_slab = jnp.pad(w_in[0].T, ((0, WSHP - WSH), (0, 0))).astype(BF16)
    w_out_slab = w_out[0].astype(BF16)
    small = jnp.concatenate([meta, _tile_rows(conv_w[0], 3, DH)], axis=0)
    wout_flight = _split_start(w_out_slab, "gather_w_out_start", per_peer=False)
    w_all, small_all = _all_gather([w_in_slab, small], "gather_w_in")

    w_t = _unshard_w_in(w_all)
    meta_full = jnp.transpose(small_all[:, :NM, :], (1, 0, 2)).reshape(NM, D)
    conv_w_full = jnp.transpose(small_all[:, NM:NM + 3, :DH], (1, 0, 2)).reshape(3, DA)

    lane_b = lambda p: jnp.broadcast_to(p.reshape(-1, DA, 1), (p.size // DA, DA, TB))
    cw_b = lane_b(conv_w_full)
    ga_b = lane_b(attn_norm_g)[0]
    gcn_b = lane_b(conv_norm_g)[0]

    u, proj_t, f_t, ktok, vtok = _inproj_fwd(x2, meta_full, norm_g + wout_flight[4][0, 0], w_t, L)
    cq, kaug, sg = _fgate_fwd(f_t, b_f.reshape(H, 1), ktok, L)
    o_t, lse = _attn_fwd(proj_t, kaug, cq, L)
    mix_t = _gate_fwd(o_t, proj_t, cw_b, ga_b, gcn_b, L)

    w_out_own, w_out_land = _split_wait(wout_flight, mix_t, "gather_w_out_wait", per_peer=False)
    w_out_full = _unshard_w_out(w_out_own, w_out_land)
    dout, dmix_t, dw_out, loss_part, dg_final = _outproj(
        mix_t, w_out_full, x2, meta_full, final_norm_g.reshape(1, D), tgt, L)
    dwo_flight = _split_start(dw_out.reshape(NDEV, D // NDEV, D), "exchange_dw_out_start", per_peer=True)
    do_t, dd, dg5_t, dga_p, dgc_p, dcw_p = _gate_bwd(dmix_t, o_t, proj_t, cw_b, ga_b + dwo_flight[4][0, 0], gcn_b, L)
    dq_t, dk_t, dv_t, dck, dcq = _attn_bwd(proj_t, kaug, vtok, do_t, lse, dd, cq, L)
    df_t, db_f = _fgate_bwd(dcq, dck, sg, L)
    dw_main, dw_f = _inproj_bwd_w(u, dq_t, dk_t, dv_t, dg5_t, df_t, L)
    dwi_flight = _split_start(_shard_w_in_grads(dw_main, dw_f), "exchange_dw_in_start", per_peer=True)
    grad_x, dmeta, dg_norm = _inproj_bwd_x(
        w_t, dq_t, dk_t, dv_t, dg5_t, df_t, dout, x2, meta_full, norm_g + dwi_flight[4][0, 0], L)
    dga = jnp.sum(dga_p, axis=1)
    dgc = jnp.sum(dgc_p, axis=1)
    dcw = jnp.sum(dcw_p, axis=2)
    small_parts = jnp.stack([
        _pack_small(dg_norm, dg_final, dga, dgc, db_f, dmeta[:, j * TB:(j + 1) * TB], dcw[:, j * DH:(j + 1) * DH],
                    loss=loss_part)
        for j in range(NDEV)], axis=0)
    small_flight = _split_start(small_parts, "exchange_small_start", per_peer=True)
    dwo_own, dwo_land = _split_wait(dwo_flight, small_flight[4], "exchange_dw_out_wait", per_peer=True)
    dwi_own, dwi_land = _split_wait(dwi_flight, dwo_land, "exchange_dw_in_wait", per_peer=True)

    big_out = _adamw_big(dwi_own, dwi_land, dwo_own, dwo_land,
                         w_in[0].T, m_w_in[0].T, v_w_in[0].T, w_out, m_w_out, v_w_out)
    g_w_in, d_w_in, nm_w_in, nv_w_in = [a.T[None] for a in big_out[:4]]
    g_w_out, d_w_out, nm_w_out, nv_w_out = big_out[4:]
    wp = _pack_small(norm_g, final_norm_g, attn_norm_g, conv_norm_g, b_f, meta, conv_w)
    mp = _pack_small(m_norm_g, m_final_norm_g, m_attn_norm_g, m_conv_norm_g, m_b_f, m_meta, m_conv_w)
    vp = _pack_small(v_norm_g, v_final_norm_g, v_attn_norm_g, v_conv_norm_g, v_b_f, v_meta, v_conv_w)
    sm_own, sm_land = _split_wait(small_flight, big_out[4], "exchange_small_wait", per_peer=True)
    small_out = _adamw_small(sm_own, sm_land, wp, mp, vp)
    sm = [_unpack_small(p) for p in small_out]
    loss = small_out[0][32, H]
    order = ("meta", "norm_g", "w_in", "b_f", "conv_w", "attn_norm_g", "conv_norm_g", "w_out", "final_norm_g")
    groups = []
    for k, (wi, wo) in enumerate(((g_w_in, g_w_out), (d_w_in, d_w_out), (nm_w_in, nm_w_out), (nv_w_in, nv_w_out))):
        d = dict(sm[k], w_in=wi, w_out=wo)
        groups.append([d[n] for n in order])
    return (loss, grad_x[None], *groups[0], *groups[1], *groups[2], *groups[3])
```

```python
import jax
import jax.numpy as jnp
from jax import lax
from jax.experimental import pallas as pl
from jax.experimental.pallas import tpu as pltpu

F32 = jnp.float32
BF16 = jnp.bfloat16

D = 1024
DA = 512
H = 8
DH = 64
NM = 16
TB = 128
P0 = TB - NM
TT = 3 * TB
HG = 8
NDEV = 8
NSEC = 8
DF = 16
DPROJ = NSEC * DA + DF
WSH = 513
WSHP = 528
WROWS = WSHP + D // NDEV
SROWS = 64
EPS = 1e-6
NEG = -1e30
LOG2E = 1.4426950408889634
LN2 = 0.6931471805599453
QSCALE = DH ** -0.5 * LOG2E
KA = 128
CB = 256
VMEM_LIMIT = 56 * 1024 * 1024

ADAM_LR = 0.001
ADAM_B1 = 0.9
ADAM_B2 = 0.999
ADAM_EPS = 1e-08
ADAM_WD = 0.01
ADAM_STEP = 10

NT_DIMS = (((1,), (1,)), ((), ()))
TN_DIMS = (((0,), (0,)), ((), ()))
MESH = pl.DeviceIdType.MESH


def _params(n_axes=1, vmem=VMEM_LIMIT):
    return pltpu.CompilerParams(dimension_semantics=("arbitrary",) * n_axes, vmem_limit_bytes=vmem)


def _dot(a, b, dims=None):
    if dims is None:
        return jnp.dot(a, b, preferred_element_type=F32)
    return lax.dot_general(a, b, dims, preferred_element_type=F32)


def _my_place():
    return lax.axis_index("x"), lax.axis_index("y"), lax.axis_index("c")


def _all_gather(xs, name):
    n = len(xs)

    def body(*refs):
        x_refs, out_refs = refs[:n], refs[n:2 * n]
        send_sems, recv_sems, local_sems = refs[2 * n:]
        mx, my, mc = _my_place()

        def across(px, py, pc, axis_a):
            flip_x = pc if axis_a else 1 - pc
            return (px + flip_x) % 2, (py + 1 - flip_x) % 2, pc

        def idx(p):
            return 4 * p[0] + 2 * p[1] + p[2]

        me, sib = (mx, my, mc), (mx, my, 1 - mc)
        a_nbr, b_nbr = across(*me, True), across(*me, False)
        diag = across(*b_nbr, True)
        sib_a, sib_b = across(*sib, True), across(*sib, False)
        sib_diag = across(*sib_b, True)

        waits = []
        for t in range(n):
            out_ref = out_refs[t]

            def copy(k, block, to, src=None, out_ref=out_ref, t=t):
                return pltpu.make_async_remote_copy(
                    src_ref=out_ref.at[idx(block)] if src is None else src, dst_ref=out_ref.at[idx(block)],
                    send_sem=send_sems.at[7 * t + k], recv_sem=recv_sems.at[7 * t + k],
                    device_id=to, device_id_type=MESH)

            mine = pltpu.make_async_copy(x_refs[t], out_ref.at[idx(me)], local_sems.at[t])
            mine.start()
            started = [copy(0, me, sib, src=x_refs[t]), copy(1, me, a_nbr, src=x_refs[t]),
                       copy(2, me, b_nbr, src=x_refs[t])]
            for cp in started:
                cp.start()
            waits.append((copy, mine, started))
        relays = ((1, a_nbr, ((3, b_nbr), (4, sib))), (2, b_nbr, ((5, sib),)), (3, diag, ((6, sib),)))
        for landed, block, onward in relays:
            for copy, _, started in waits:
                copy(landed, block, me).wait_recv()
                for k, to in onward:
                    started.append(copy(k, block, to))
                    started[-1].start()
        for copy, mine, started in waits:
            for k, block in ((0, sib), (4, sib_a), (5, sib_b), (6, sib_diag)):
                copy(k, block, me).wait_recv()
            for cp in started:
                cp.wait_send()
            mine.wait()

    any_spec = pl.BlockSpec(memory_space=pl.ANY)
    return pl.pallas_call(
        body, name=name,
        out_shape=[jax.ShapeDtypeStruct((NDEV,) + x.shape, x.dtype) for x in xs],
        in_specs=[any_spec] * n, out_specs=[any_spec] * n,
        scratch_shapes=[pltpu.SemaphoreType.DMA((7 * n,)), pltpu.SemaphoreType.DMA((7 * n,)),
                        pltpu.SemaphoreType.DMA((n,))],
    )(*xs)


_HBM =pl.BlockSpec(memory_space=pltpu.HBM)
_SEM = pl.BlockSpec(memory_space=pltpu.SEMAPHORE)
_EFFECT = pltpu.SideEffectType.DATAFLOW_SIDE_EFFECTING


def _peer_of(m, place):
    mx, my, mc = place
    return ((1 - mx) if m & 4 else mx, (1 - my) if m & 2 else my, (1 - mc) if m & 1 else mc)


def _split_copies(src_ref, land_ref, send_sems, recv_sems, per_peer, incoming):
    place = _my_place()
    me = 4 * place[0] + 2 * place[1] + place[2]
    out = []
    for m in range(1, NDEV):
        px, py, pc = _peer_of(m, place)
        peer = 4 * px + 2 * py + pc
        src = (src_ref.at[me] if incoming else src_ref.at[peer]) if per_peer else src_ref
        out.append(pltpu.make_async_remote_copy(
            src_ref=src, dst_ref=land_ref.at[peer if incoming else me],
            send_sem=send_sems.at[m - 1], recv_sem=recv_sems.at[m - 1],
            device_id=(px, py, pc), device_id_type=MESH))
    return out


def _split_start(src, name, per_peer):
    slab = src.shape[1:] if per_peer else src.shape

    def body(src_ref, land_ref, send_sems, recv_sems, src_thru, land_thru, token):
        for cp in _split_copies(src_ref, land_ref, send_sems, recv_sems, per_peer, incoming=False):
            cp.start()
        token[...] = jnp.zeros_like(token)

    return pl.pallas_call(
        body, name=name,
        out_shape=(pltpu.SemaphoreType.DMA((NDEV - 1,)), pltpu.SemaphoreType.DMA((NDEV - 1,)),
                   pltpu.HBM(src.shape, src.dtype), pltpu.HBM((NDEV,) + slab, src.dtype),
                   jax.ShapeDtypeStruct((8, TB), F32)),
        in_specs=(_HBM, _HBM), out_specs=(_SEM, _SEM, _HBM, _HBM, pl.BlockSpec(memory_space=pltpu.VMEM)),
        input_output_aliases={0: 2, 1: 3},
        compiler_params=pltpu.CompilerParams(has_side_effects=_EFFECT),
    )(pltpu.with_memory_space_constraint(src, pltpu.HBM),
      pltpu.with_memory_space_constraint(lax.empty((NDEV,) + slab, src.dtype), pltpu.HBM))


def _split_wait(handles, after, name, per_peer):
    send_sems, recv_sems, src_thru, land_thru, _ = handles

    def body(src_ref, land_ref, send_sems, recv_sems, after_ref, src_out, land_out):
        for cp in _split_copies(src_ref, land_ref, send_sems, recv_sems, per_peer, incoming=False):
            cp.wait_send()
        for cp in _split_copies(src_ref, land_ref, send_sems, recv_sems, per_peer, incoming=True):
            cp.wait_recv()

    return pl.pallas_call(
        body, name=name,
        out_shape=(pltpu.HBM(src_thru.shape, src_thru.dtype), pltpu.HBM(land_thru.shape, land_thru.dtype)),
        in_specs=(_HBM, _HBM, _SEM, _SEM, pl.BlockSpec(memory_space=pl.ANY)), out_specs=(_HBM, _HBM),
        input_output_aliases={0: 0, 1: 1},
        compiler_params=pltpu.CompilerParams(has_side_effects=_EFFECT),
    )(src_thru, land_thru, send_sems, recv_sems, after)


def _pick_slab(j, own_ref, land_ref, rows, per_peer=True):
    mx, my, mc = _my_place()
    me = 4 * mx + 2 * my + mc
    own = (lambda: own_ref[j, rows, :]) if per_peer else (lambda: own_ref[rows, :])
    return lax.cond(me == j, own, lambda: land_ref[j, rows, :])


def _h_block(t, x_ref, meta_ref):
    first = jnp.concatenate([jnp.zeros((P0, D), F32), meta_ref[...]], axis=0)
    return jnp.where(t == 0, first, x_ref[...])


def _x_spec():
    return pl.BlockSpec((TB, D), lambda t: (jnp.maximum(t - 1, 0), 0))


def _x_specs3():
    return [pl.BlockSpec((TB, D), lambda j: (jnp.maximum(3 * j - 1, 0), 0)),
            pl.BlockSpec((TB, D), lambda j: (3 * j, 0)),
            pl.BlockSpec((TB, D), lambda j: (3 * j + 1, 0))]


def _h_tile(j, xa_ref, xb_ref, xc_ref, meta_ref):
    first = jnp.concatenate([jnp.zeros((P0, D), F32), meta_ref[...]], axis=0)
    return jnp.concatenate([jnp.where(j == 0, first, xa_ref[...]), xb_ref[...], xc_ref[...]], axis=0)


def _full_spec(shape):
    return pl.BlockSpec(shape, lambda *_: (0,) * len(shape))


def _sigmoid(z):
    return 1.0 / (1.0 + jnp.exp(-z))


def _grouped(x):
    return x.reshape(H, DH, x.shape[-1])


def _group_rstd(x3):
    return lax.rsqrt(jnp.mean(x3 * x3, axis=1, keepdims=True) + EPS)


def _lane_tiles_sum(x):
    out = x[:, :TB]
    for i in range(1, x.shape[1] // TB):
        out = out + x[:, i * TB:(i + 1) * TB]
    return out


def _inproj_fwd(x, meta_full, norm_g, w_t, L):
    nj = L // TT

    def body(xa_ref, xb_ref, xc_ref, meta_ref, g_ref, w_ref, u_ref, proj_ref, f_ref, ktok_ref, vtok_ref):
        hb = _h_tile(pl.program_id(0), xa_ref, xb_ref, xc_ref, meta_ref)
        r = lax.rsqrt(jnp.mean(hb * hb, axis=-1, keepdims=True) + EPS)
        u = (hb * r * g_ref[...]).astype(BF16)
        u_ref[...] = u
        for s in range(NSEC):
            p = _dot(u, w_ref[s * DA:(s + 1) * DA, :], NT_DIMS)
            if s == 0:
                p = p * QSCALE
            if s in (1, 2):
                tok_ref = ktok_ref if s == 1 else vtok_ref
                for h in range(H):
                    tok_ref[h] = p[:, h * DH:(h + 1) * DH].astype(BF16)
            proj_ref[s * DA:(s + 1) * DA, :] = p.T.astype(BF16)
        f_ref[...] = _dot(w_ref[NSEC * DA:DPROJ, :], u, NT_DIMS)[:H]

    return pl.pallas_call(
        body, name="inproj_fwd", grid=(nj,),
        in_specs=_x_specs3() + [_full_spec((NM, D)), _full_spec((1, D)), _full_spec((DPROJ, D))],
        out_specs=[
            pl.BlockSpec((TT, D), lambda t: (t, 0)),
            pl.BlockSpec((NSEC * DA, TT), lambda t: (0, t)),
            pl.BlockSpec((H, TT), lambda t: (0, t)),
            pl.BlockSpec((H, TT, DH), lambda t: (0, t, 0)),
            pl.BlockSpec((H, TT, DH), lambda t: (0, t, 0)),
        ],
        out_shape=[
            jax.ShapeDtypeStruct((L, D), BF16),
            jax.ShapeDtypeStruct((NSEC * DA, L), BF16),
            jax.ShapeDtypeStruct((H, L), F32),
            jax.ShapeDtypeStruct((H, L, DH), BF16),
            jax.ShapeDtypeStruct((H, L, DH), BF16),
        ],
        compiler_params=_params(),
    )(x, x, x, meta_full, norm_g, w_t)


def _split3(x):
    hi = x.astype(BF16).astype(F32)
    r = x - hi
    mid = r.astype(BF16).astype(F32)
    return hi, mid, (r - mid).astype(BF16).astype(F32)


def _bias_rows(bias):
    one = jnp.ones((1, TT), F32)
    zero = jnp.zeros((1, TT), F32)
    parts = [zero] * 3 if bias is None else list(_split3(bias))
    return jnp.concatenate([one] * 3 + parts + [zero] * (DF - 6), axis=0).astype(BF16)


def _fgate_fwd(f_t, b_col, ktok, L):
    nb = L // TB

    def body(f_ref, b_ref, ktok_ref, cq_ref, kaug_ref, sg_ref):
        z = f_ref[...] + b_ref[...]
        idx = lax.broadcasted_iota(jnp.int32, (H, L), 1)
        real = idx >= P0
        lf = jnp.where(real, jnp.minimum(z, 0.0) - jnp.log1p(jnp.exp(-jnp.abs(z))), 0.0)
        sg_ref[...] = jnp.where(real, 1.0 / (1.0 + jnp.exp(z)), 0.0)
        c = lf
        s = 1
        while s < L:
            c = c + jnp.where(idx >= s, pltpu.roll(c, s, 1), 0.0)
            s *= 2
        c = c * LOG2E
        for h in range(H):
            cq_ref[h] = c[h:h + 1, :]
        ck = jnp.where(real, c, -NEG)
        lane = lax.broadcasted_iota(jnp.int32, (TB, KA), 1)
        tail = jnp.where((lane >= DH + 3) & (lane < DH + 6), 1.0, 0.0)
        for h in range(H):
            for b in range(nb):
                blk = slice(b * TB, (b + 1) * TB)
                col = jnp.broadcast_to(ck[h:h + 1, blk], (TB, TB)).T
                hi, mid, lo = _split3(-col)
                k = jnp.concatenate([ktok_ref[h, blk, :].astype(F32), jnp.zeros((TB, KA - DH), F32)], axis=1)
                out = jnp.where(lane < DH, k, jnp.where(lane == DH, hi, jnp.where(
                    lane == DH + 1, mid, jnp.where(lane == DH + 2, lo, tail))))
                kaug_ref[h, blk, :] = out.astype(BF16)

    return pl.pallas_call(
        body, name="fgate_fwd",
        out_shape=[
            jax.ShapeDtypeStruct((H, 1, L), F32),
            jax.ShapeDtypeStruct((H, L, KA), BF16),
            jax.ShapeDtypeStruct((H, L), F32),
        ],
        compiler_params=pltpu.CompilerParams(vmem_limit_bytes=VMEM_LIMIT),
    )(f_t, b_col, ktok)


def _causal_mask():
    r = lax.broadcasted_iota(jnp.int32, (TT, TT), 0)
    c = lax.broadcasted_iota(jnp.int32, (TT, TT), 1)
    return r <= c


def _attn_fwd(proj_t, kaug, cq, L):
    nq = L // TT

    def body(q_ref, kaug_ref, v_ref, cq_ref, o_ref, lse_ref,
             qa_scr, s_scr, cmax_scr, m_scr, p_scr, alpha_scr, acc_scr):
        j = pl.program_id(1)
        rows = [slice(g * DH, (g + 1) * DH) for g in range(HG)]
        ones = jnp.ones((DF, TT), BF16)
        for g in range(HG):
            qa_scr[g] = jnp.concatenate(
                [q_ref[rows[g], :], _bias_rows(None), jnp.zeros((KA - DH - DF, TT), BF16)], axis=0)

        def scores(kt, masked):
            k_off = pl.multiple_of(kt * TT, TT)
            for g in range(HG):
                s = _dot(kaug_ref[g, pl.ds(k_off, TT), :], qa_scr[g])
                if masked:
                    s = jnp.where(_causal_mask(), s, NEG)
                s_scr[g] = s
                cmax_scr[g] = jnp.max(s, axis=0, keepdims=True)

        def softmax():
            for g in range(HG):
                m_old = m_scr[g]
                m_new = jnp.maximum(m_old, cmax_scr[g])
                alpha_scr[g] = jnp.exp2(m_old - m_new)
                p_scr[g] = jnp.exp2(s_scr[g] - m_new).astype(BF16)
                m_scr[g] = m_new

        def weighted_sum(kt):
            k_off = pl.multiple_of(kt * TT, TT)
            for g in range(HG):
                v1 = jnp.concatenate([v_ref[rows[g], pl.ds(k_off, TT)], ones], axis=0)
                acc_scr[g] = alpha_scr[g] * acc_scr[g] + _dot(v1, p_scr[g])

        m_scr[...] = jnp.full_like(m_scr, NEG)
        acc_scr[...] = jnp.zeros_like(acc_scr)

        scores(j, True)

        @pl.when(j >= 1)
        def _():
            softmax()
            scores(j - 1, False)

        def step(i, c):
            weighted_sum(j - i + 1)
            softmax()
            scores(j - i - 1, False)
            return c

        lax.fori_loop(1, j, step, 0)

        @pl.when(j >= 1)
        def _():
            weighted_sum(1)

        softmax()
        weighted_sum(0)
        for g in range(HG):
            l = acc_scr[g, DH:DH + 1, :]
            o_ref[rows[g], :] = acc_scr[g, :DH, :] * (1.0 / l)
            lse_ref[g] = m_scr[g] + jnp.log2(l) + cq_ref[g]

    return pl.pallas_call(
        body, name="attn_fwd", grid=(H // HG, nq),
        in_specs=[
            pl.BlockSpec((HG * DH, TT), lambda h, j: (h, j)),
            pl.BlockSpec((HG, L, KA), lambda h, j: (h, 0, 0)),
            pl.BlockSpec((HG * DH, L), lambda h, j: (2 * H // HG + h, 0)),
            pl.BlockSpec((HG, 1, TT), lambda h, j: (h, 0, j)),
        ],
        out_specs=[
            pl.BlockSpec((HG * DH, TT), lambda h, j: (h, j)),
            pl.BlockSpec((HG, 1, TT), lambda h, j: (h, 0, j)),
        ],
        out_shape=[jax.ShapeDtypeStruct((DA, L), F32), jax.ShapeDtypeStruct((H, 1, L), F32)],
        scratch_shapes=[pltpu.VMEM((HG, KA, TT), BF16), pltpu.VMEM((HG, TT, TT), F32), pltpu.VMEM((HG, 1, TT), F32),
                        pltpu.VMEM((HG, 1, TT), F32), pltpu.VMEM((HG, TT, TT), BF16), pltpu.VMEM((HG, 1, TT), F32),
                        pltpu.VMEM((HG, DH + DF, TT), F32)],
        compiler_params=_params(2),
    )(proj_t, kaug, proj_t, cq)


def _gate_common(o, za, gb, gc, xc, zc, gcp, xcp, cw_ref, ga_ref, gcn_ref, first):
    n_rep = TT // TB
    a = gc * xc
    a_prev = jnp.where(first, 0.0, gcp * xcp)
    full = jnp.concatenate([a_prev, a], axis=1)
    a1 = pltpu.roll(full, 1, 1)[:, TB:]
    a2 = pltpu.roll(full, 2, 1)[:, TB:]
    w0 = jnp.tile(cw_ref[0], (1, n_rep))
    w1 = jnp.tile(cw_ref[1], (1, n_rep))
    w2 = jnp.tile(cw_ref[2], (1, n_rep))
    cv = w0 * a2 + w1 * a1 + w2 * a
    e = gb * cv
    e3 = _grouped(e)
    rc = _group_rstd(e3)
    ec = (e3 * rc).reshape(DA, TT)
    o3 = _grouped(o)
    ra = _group_rstd(o3)
    oa = (o3 * ra).reshape(DA, TT)
    g_a = jnp.tile(ga_ref[...], (1, n_rep))
    g_c = jnp.tile(gcn_ref[...], (1, n_rep))
    sa = _sigmoid(za)
    sc = _sigmoid(zc)
    return dict(a=a, a1=a1, a2=a2, w0=w0, w1=w1, w2=w2, cv=cv, e=e, rc=rc, ec=ec, ra=ra, oa=oa,
                g_a=g_a, g_c=g_c, sa=sa, sc=sc)


def _gate_specs(nj, rev):
    def jj(i):
        return (nj - 1 - i) if rev else i

    def sec(s):
        return pl.BlockSpec((DA, TT), lambda i: (s, jj(i)))

    def halo(s):
        return pl.BlockSpec((DA, TB), lambda i: (s, jnp.maximum(3 * jj(i) - 1, 0)))

    return [pl.BlockSpec((DA, TT), lambda i: (0, jj(i))), sec(3), sec(4), sec(5), sec(6), sec(7), halo(5), halo(6),
            _full_spec((3, DA, TB)), _full_spec((DA, TB)), _full_spec((DA, TB))]


def _gate_fwd(o_t, proj_t, cw_b, ga_b, gcn_b, L):
    nj = L // TT

    def body(o_ref, za_ref, gb_ref, gc_ref, xc_ref, zc_ref, gcp_ref, xcp_ref, cw_ref, ga_ref, gcn_ref, mix_ref):
        j = pl.program_id(0)
        f32 = lambda r: r[...].astype(F32)
        za, zc = f32(za_ref), f32(zc_ref)
        g = _gate_common(o_ref[...], za, f32(gb_ref), f32(gc_ref), f32(xc_ref), zc, f32(gcp_ref), f32(xcp_ref),
                         cw_ref, ga_ref, gcn_ref, j == 0)
        mix_ref[:DA, :] = (g["oa"] * g["g_a"] * (za * g["sa"])).astype(BF16)
        mix_ref[DA:, :] = (g["ec"] * g["g_c"] * (zc * g["sc"])).astype(BF16)

    return pl.pallas_call(
        body, name="gate_fwd", grid=(nj,),
        in_specs=_gate_specs(nj, False),
        out_specs=pl.BlockSpec((2 * DA, TT), lambda j: (0, j)),
        out_shape=jax.ShapeDtypeStruct((2 * DA, L), BF16),
        compiler_params=_params(),
    )(o_t, proj_t, proj_t, proj_t, proj_t, proj_t, proj_t, proj_t, cw_b, ga_b, gcn_b)


def _outproj(mix_t, w_out, x, meta_full, fng, target, L):
    nj = L // TT

    def body(mix_ref, w_ref, xa_ref, xb_ref, xc_ref, meta_ref, g_ref, ta_ref, tb_ref, tc_ref,
             dout_ref, dmix_ref, dwb_ref, loss_ref, dg_ref, dw_ref):
        t = pl.program_id(0)

        @pl.when(t == 0)
        def _():
            dw_ref[...] = jnp.zeros_like(dw_ref)
            loss_ref[...] = jnp.zeros_like(loss_ref)
            dg_ref[...] = jnp.zeros_like(dg_ref)

        mix = mix_ref[...]
        o = _dot(mix, w_ref[...], TN_DIMS) + _h_tile(t, xa_ref, xb_ref, xc_ref, meta_ref)
        r = lax.rsqrt(jnp.mean(o * o, axis=-1, keepdims=True) + EPS)
        g = g_ref[...]
        orn = o * r
        tgt = jnp.concatenate([ta_ref[...], tb_ref[...], tc_ref[...]], axis=0)
        row = lax.broadcasted_iota(jnp.int32, (TT, 1), 0)
        real = jnp.where((t > 0) | (row >= TB), 1.0, 0.0)
        diff = (orn * g - tgt) * real
        loss_ref[...] += 0.5 * jnp.sum(diff * diff) * (1.0 / D)
        dy = diff * (1.0 / D)
        dg_ref[...] += jnp.sum(dy * orn, axis=0, keepdims=True)
        gy = dy * g
        dout = r * gy - orn * (r * jnp.mean(gy * orn, axis=-1, keepdims=True))
        dout_ref[...] = dout
        db = dout.astype(BF16)
        dmix_ref[...] = _dot(db, w_ref[...], NT_DIMS).T.astype(BF16)
        dw_ref[...] += _dot(mix, db)

        @pl.when(t == nj - 1)
        def _():
            dwb_ref[...] = dw_ref[...].astype(BF16)

    return pl.pallas_call(
        body, name="outproj", grid=(nj,),
        in_specs=[pl.BlockSpec((D, TT), lambda t: (0, t)), _full_spec((D, D))] + _x_specs3()
                 + [_full_spec((NM, D)), _full_spec((1, D))] + _x_specs3(),
        out_specs=[pl.BlockSpec((TT, D), lambda t: (t, 0)), pl.BlockSpec((D, TT), lambda t: (0, t)),
                   _full_spec((D, D)), _full_spec((1, 1)), _full_spec((1, D))],
        out_shape=[jax.ShapeDtypeStruct((L, D), F32), jax.ShapeDtypeStruct((D, L), BF16),
                   jax.ShapeDtypeStruct((D, D), BF16), jax.ShapeDtypeStruct((1, 1), F32),
                   jax.ShapeDtypeStruct((1, D), F32)],
        scratch_shapes=[pltpu.VMEM((D, D), F32)],
        compiler_params=_params(),
    )(mix_t, w_out, x, x, x, meta_full, fng, target, target, target)


def _gate_bwd(dmix_t, o_t, proj_t, cw_b, ga_b, gcn_b, L):
    nj = L // TT

    def body(dmix_ref, o_ref, za_ref, gb_ref, gc_ref, xc_ref, zc_ref, gcp_ref, xcp_ref, cw_ref, ga_ref, gcn_ref,
             do_ref, dd_ref, dg5_ref, dga_ref, dgc_ref, dcw_ref, carry_ref):
        i = pl.program_id(0)
        j = nj - 1 - i

        @pl.when(i == 0)
        def _():
            carry_ref[...] = jnp.zeros_like(carry_ref)
            dga_ref[...] = jnp.zeros_like(dga_ref)
            dgc_ref[...] = jnp.zeros_like(dgc_ref)
            dcw_ref[...] = jnp.zeros_like(dcw_ref)

        f32 = lambda r: r[...].astype(F32)
        o, za, gb, gc, xc, zc = o_ref[...], f32(za_ref), f32(gb_ref), f32(gc_ref), f32(xc_ref), f32(zc_ref)
        g = _gate_common(o, za, gb, gc, xc, zc, f32(gcp_ref), f32(xcp_ref), cw_ref, ga_ref, gcn_ref, j == 0)
        dya = dmix_ref[:DA, :].astype(F32)
        dyc = dmix_ref[DA:, :].astype(F32)
        sa, sc = g["sa"], g["sc"]

        dn = dya * (za * sa)
        dg5_ref[0:DA, :] = (dya * (g["oa"] * g["g_a"]) * (sa * (1.0 + za * (1.0 - sa)))).astype(BF16)
        dga_ref[...] += _lane_tiles_sum(dn * g["oa"])
        dng = dn * g["g_a"]
        mean_a = jnp.mean(_grouped(dng * g["oa"]), axis=1, keepdims=True)
        do = ((_grouped(dng) - _grouped(g["oa"]) * mean_a) * g["ra"]).reshape(DA, TT)
        do_ref[...] = do.astype(BF16)
        dd = jnp.sum(_grouped(do * o), axis=1)
        for h in range(H):
            dd_ref[h] = dd[h:h + 1, :]

        dnc = dyc * (zc * sc)
        dg5_ref[4 * DA:5 * DA, :] = (dyc * (g["ec"] * g["g_c"]) * (sc * (1.0 + zc * (1.0 - sc)))).astype(BF16)
        dgc_ref[...] += _lane_tiles_sum(dnc * g["ec"])
        dncg = dnc * g["g_c"]
        mean_c = jnp.mean(_grouped(dncg * g["ec"]), axis=1, keepdims=True)
        de = ((_grouped(dncg) - _grouped(g["ec"]) * mean_c) * g["rc"]).reshape(DA, TT)
        dg5_ref[DA:2 * DA, :] = (de * g["cv"]).astype(BF16)
        dcv = de * gb
        full = jnp.concatenate([dcv, carry_ref[...]], axis=1)
        d1 = pltpu.roll(full, TT + TB - 1, 1)[:, :TT]
        d2 = pltpu.roll(full, TT + TB - 2, 1)[:, :TT]
        carry_ref[...] = dcv[:, :TB]
        da = g["w2"] * dcv + g["w1"] * d1 + g["w0"] * d2
        dg5_ref[2 * DA:3 * DA, :] = (da * xc).astype(BF16)
        dg5_ref[3 * DA:4 * DA, :] = (da * gc).astype(BF16)
        dcw_ref[0] += _lane_tiles_sum(dcv * g["a2"])
        dcw_ref[1] += _lane_tiles_sum(dcv * g["a1"])
        dcw_ref[2] += _lane_tiles_sum(dcv * g["a"])

    rj = lambda i: nj - 1 - i
    return pl.pallas_call(
        body, name="gate_bwd", grid=(nj,),
        in_specs=[pl.BlockSpec((2 * DA, TT), lambda i: (0, rj(i)))] + _gate_specs(nj, True),
        out_specs=[
            pl.BlockSpec((DA, TT), lambda i: (0, rj(i))),
            pl.BlockSpec((H, 1, TT), lambda i: (0, 0, rj(i))),
            pl.BlockSpec((5 * DA, TT), lambda i: (0, rj(i))),
            _full_spec((DA, TB)), _full_spec((DA, TB)), _full_spec((3, DA, TB)),
        ],
        out_shape=[
            jax.ShapeDtypeStruct((DA, L), BF16),
            jax.ShapeDtypeStruct((H, 1, L), F32),
            jax.ShapeDtypeStruct((5 * DA, L), BF16),
            jax.ShapeDtypeStruct((DA, TB), F32),
            jax.ShapeDtypeStruct((DA, TB), F32),
            jax.ShapeDtypeStruct((3, DA, TB), F32),
        ],
        scratch_shapes=[pltpu.VMEM((DA, TB), F32)],
        compiler_params=_params(),
    )(dmix_t, o_t, proj_t, proj_t, proj_t, proj_t, proj_t, proj_t, proj_t, cw_b, ga_b, gcn_b)


def _attn_bwd(proj_t, kaug, vtok, do_t, lse, dd, cq, L):
    nk = L // TT

    def body(q_ref, kaug_ref, vtok_ref, kt_ref, do_ref, lse_ref, dd_ref, cq_ref,
             dq_ref, dk_ref, dv_ref, dck_ref, dcq_ref, dq_acc, kt1_scr, s_scr, dp_scr, dv_scr, dk_scr):
        i = pl.program_id(1)

        @pl.when(i == 0)
        def _():
            dq_acc[...] = jnp.zeros_like(dq_acc)

        rows = [slice(g * DH, (g + 1) * DH) for g in range(HG)]
        ones = jnp.ones((DF, TT), BF16)
        zpad = jnp.zeros((KA - DH - DF, TT), BF16)
        for g in range(HG):
            kt1_scr[g] = jnp.concatenate([kt_ref[rows[g], :], ones], axis=0)
        dv_scr[...] = jnp.zeros_like(dv_scr)
        dk_scr[...] = jnp.zeros_like(dk_scr)

        def q_rows(g, q_off):
            bias = cq_ref[g, :, pl.ds(q_off, TT)] - lse_ref[g, :, pl.ds(q_off, TT)]
            return jnp.concatenate([q_ref[rows[g], pl.ds(q_off, TT)], _bias_rows(bias)], axis=0)

        def scores(jq, masked):
            q_off = pl.multiple_of(jq * TT, TT)
            for g in range(HG):
                s = _dot(kaug_ref[g], jnp.concatenate([q_rows(g, q_off), zpad], axis=0))
                if masked:
                    s = jnp.where(_causal_mask(), s, NEG)
                s_scr[g] = s
                dp_scr[g] = _dot(vtok_ref[g], do_ref[rows[g], pl.ds(q_off, TT)])

        def grads(jq):
            q_off = pl.multiple_of(jq * TT, TT)
            for g in range(HG):
                p = jnp.exp2(s_scr[g])
                ds = (p * (dp_scr[g] - dd_ref[g, :, pl.ds(q_off, TT)])).astype(BF16)
                do1 = jnp.concatenate([do_ref[rows[g], pl.ds(q_off, TT)], jnp.zeros((KA - DH, TT), BF16)], axis=0)
                q1 = jnp.concatenate([q_rows(g, q_off), zpad], axis=0)
                dv_scr[g] += _dot(p.astype(BF16), do1, NT_DIMS)
                dk_scr[g] += _dot(ds, q1, NT_DIMS)
                dq_acc[g, :, pl.ds(q_off, TT)] += _dot(kt1_scr[g], ds)

        scores(i, True)

        def step(jq, c):
            grads(jq)
            scores(jq + 1, False)
            return c

        lax.fori_loop(i, nk - 1, step, 0)
        grads(nk - 1)
        for g in range(HG):
            dv_ref[rows[g], :] = dv_scr[g].T[:DH, :].astype(BF16)
            dk_t = dk_scr[g].T
            dk_ref[rows[g], :] = (dk_t[:DH, :] * LN2).astype(BF16)
            dck_ref[g] = dk_t[DH:DH + 1, :]

        @pl.when(i == nk - 1)
        def _():
            for g in range(HG):
                dq_ref[rows[g], :] = (dq_acc[g, :DH, :] * (DH ** -0.5)).astype(BF16)
                dcq_ref[g] = dq_acc[g, DH:DH + 1, :]

    head = lambda h, i: (h, 0)
    row = lambda h, i: (h, 0, 0)
    return pl.pallas_call(
        body, name="attn_bwd", grid=(H // HG, nk),
        in_specs=[
            pl.BlockSpec((HG * DH, L), head),
            pl.BlockSpec((HG, TT, KA), lambda h, i: (h, i, 0)),
            pl.BlockSpec((HG, TT, DH), lambda h, i: (h, i, 0)),
            pl.BlockSpec((HG * DH, TT), lambda h, i: (H // HG + h, i)),
            pl.BlockSpec((HG * DH, L), head),
            pl.BlockSpec((HG, 1, L), row), pl.BlockSpec((HG, 1, L), row), pl.BlockSpec((HG, 1, L), row),
        ],
        out_specs=[
            pl.BlockSpec((HG * DH, L), head),
            pl.BlockSpec((HG * DH, TT), lambda h, i: (h, i)),
            pl.BlockSpec((HG * DH, TT), lambda h, i: (h, i)),
            pl.BlockSpec((HG, 1, TT), lambda h, i: (h, 0, i)),
            pl.BlockSpec((HG, 1, L), row),
        ],
        out_shape=[jax.ShapeDtypeStruct((DA, L), BF16), jax.ShapeDtypeStruct((DA, L), BF16),
                   jax.ShapeDtypeStruct((DA, L), BF16), jax.ShapeDtypeStruct((H, 1, L), F32),
                   jax.ShapeDtypeStruct((H, 1, L), F32)],
        scratch_shapes=[
            pltpu.VMEM((HG, DH + DF, L), F32),
            pltpu.VMEM((HG, DH + DF, TT), BF16),
            pltpu.VMEM((HG, TT, TT), F32), pltpu.VMEM((HG, TT, TT), F32),
            pltpu.VMEM((HG, TT, KA), F32), pltpu.VMEM((HG, TT, KA), F32)],
        compiler_params=_params(2),
    )(proj_t, kaug, vtok, proj_t, do_t, lse, dd, cq)


def _fgate_bwd(dcq, dck, sg, L):
    def body(dcq_ref, dck_ref, sg_ref, df_ref, db_ref):
        dc = jnp.concatenate([dcq_ref[h] - dck_ref[h] for h in range(H)], axis=0)
        idx = lax.broadcasted_iota(jnp.int32, (H, L), 1)
        r = dc
        s = 1
        while s < L:
            r = r + jnp.where(idx + s < L, pltpu.roll(r, L - s, 1), 0.0)
            s *= 2
        df = r * sg_ref[...]
        db_ref[...] = jnp.broadcast_to(jnp.sum(df, axis=1, keepdims=True), (H, TB))
        df_ref[...] = jnp.concatenate([df, jnp.zeros((DF - H, L), F32)], axis=0).astype(BF16)

    return pl.pallas_call(
        body, name="fgate_bwd",
        out_shape=[jax.ShapeDtypeStruct((DF, L), BF16), jax.ShapeDtypeStruct((H, TB), F32)],
        compiler_params=pltpu.CompilerParams(vmem_limit_bytes=VMEM_LIMIT),
    )(dcq, dck, sg)


def _inproj_bwd_x(w, dq_t, dk_t, dv_t, dg5_t, df_t, dout, x, meta_full, norm_g, L):
    nb = L // TB
    seq = x.shape[0]

    def body(w_ref, dq_ref, dk_ref, dv_ref, dg5_ref, df_ref, dout_ref, x_ref, meta_ref, g_ref,
             gx_ref, dmeta_ref, dg_ref):
        t = pl.program_id(0)

        @pl.when(t == 0)
        def _():
            dg_ref[...] = jnp.zeros_like(dg_ref)

        du = _dot(dq_ref[...], w_ref[0:DA, :], TN_DIMS)
        du += _dot(dk_ref[...], w_ref[DA:2 * DA, :], TN_DIMS)
        du += _dot(dv_ref[...], w_ref[2 * DA:3 * DA, :], TN_DIMS)
        du += _dot(dg5_ref[...], w_ref[3 * DA:NSEC * DA, :], TN_DIMS)
        du += _dot(df_ref[...], w_ref[NSEC * DA:DPROJ, :], TN_DIMS)
        hb = _h_block(t, x_ref, meta_ref)
        r = lax.rsqrt(jnp.mean(hb * hb, axis=-1, keepdims=True) + EPS)
        hn = hb * r
        dg_ref[...] += jnp.sum(du * hn, axis=0, keepdims=True)
        gu = du * g_ref[...]
        dh = dout_ref[...] + r * gu - hn * (r * jnp.mean(gu * hn, axis=-1, keepdims=True))
        gx_ref[...] = dh

        @pl.when(t == 0)
        def _():
            dmeta_ref[...] = dh[P0:, :]

    blk = lambda rows: pl.BlockSpec((rows, TB), lambda t: (0, t))
    return pl.pallas_call(
        body, name="inproj_bwd_x", grid=(nb,),
        in_specs=[_full_spec((DPROJ, D)), blk(DA), blk(DA), blk(DA), blk(5 * DA), blk(DF),
                  pl.BlockSpec((TB, D), lambda t: (t, 0)), _x_spec(), _full_spec((NM, D)), _full_spec((1, D))],
        out_specs=[_x_spec(), _full_spec((NM, D)), _full_spec((1, D))],
        out_shape=[jax.ShapeDtypeStruct((seq, D), F32), jax.ShapeDtypeStruct((NM, D), F32),
                   jax.ShapeDtypeStruct((1, D), F32)],
        compiler_params=_params(),
    )(w, dq_t, dk_t, dv_t, dg5_t, df_t, dout, x, meta_full, norm_g)


def _inproj_bwd_w(u, dq_t, dk_t, dv_t, dg5_t, df_t, L):
    def body(u_ref, dq_ref, dk_ref, dv_ref, dg5_ref, df_ref, dw_ref, dwf_ref):
        s = pl.program_id(0)
        u_all = u_ref[...]

        @pl.when(s < 5)
        def _():
            dw_ref[...] = _dot(dg5_ref[...], u_all)

        for step, ref in ((5, dq_ref), (6, dk_ref), (7, dv_ref)):
            @pl.when(s == step)
            def _(ref=ref):
                dw_ref[...] = _dot(ref[...], u_all)

        @pl.when(s == NSEC - 1)
        def _():
            dwf_ref[...] = _dot(df_ref[...], u_all)

    once = lambda shape: pl.BlockSpec(shape, lambda s: (0, 0), pipeline_mode=pl.Buffered(1))
    return pl.pallas_call(
        body, name="inproj_bwd_w", grid=(NSEC,),
        in_specs=[
            once((L, D)), once((DA, L)), once((DA, L)), once((DA, L)),
            pl.BlockSpec((DA, L), lambda s: (jnp.minimum(s, 4), 0)),
            once((DF, L)),
        ],
        out_specs=[pl.BlockSpec((DA, D), lambda s: (jnp.where(s < 5, s + 3, s - 5), 0)), _full_spec((DF, D))],
        out_shape=[jax.ShapeDtypeStruct((NSEC * DA, D), F32), jax.ShapeDtypeStruct((DF, D), F32)],
        compiler_params=_params(),
    )(u, dq_t, dk_t, dv_t, dg5_t, df_t)


def _adamw(w, g, m, v):
    m = ADAM_B1 * m + (1.0 - ADAM_B1) * g
    v = ADAM_B2 * v + (1.0 - ADAM_B2) * (g * g)
    m_hat = m / (1.0 - ADAM_B1 ** ADAM_STEP)
    v_hat = v / (1.0 - ADAM_B2 ** ADAM_STEP)
    delta = -ADAM_LR * (m_hat / (jnp.sqrt(v_hat) + ADAM_EPS) + ADAM_WD * w)
    return delta, m, v


def _adamw_big(own_in, land_in, own_out, land_out, w_in_t, m_in_t, v_in_t, w_out, m_out, v_out):
    cb = CB
    e_sh = D // NDEV
    in_shape = jax.ShapeDtypeStruct(w_in_t.shape, F32)
    out_shape = jax.ShapeDtypeStruct(w_out.shape, F32)

    def total(own_ref, land_ref, rows):
        g = _pick_slab(0, own_ref, land_ref, rows).astype(F32)
        for j in range(1, NDEV):
            g = g + _pick_slab(j, own_ref, land_ref, rows).astype(F32)
        return g

    def body(oi_ref, li_ref, oo_ref, lo_ref, wi_ref, mi_ref, vi_ref, wo_ref, mo_ref, vo_ref,
             gi, di, mi, vi, go, do, mo, vo):
        g = total(oi_ref, li_ref, slice(0, WSHP))[:WSH]
        d, mn, vn = _adamw(wi_ref[...], g, mi_ref[...], vi_ref[...])
        gi[...], di[...], mi[...], vi[...] = g, d, mn, vn
        g = total(oo_ref, lo_ref, slice(0, e_sh))
        d, mn, vn = _adamw(wo_ref[0], g, mo_ref[0], vo_ref[0])
        go[0], do[0], mo[0], vo[0] = g, d, mn, vn

    slab = lambda rows: pl.BlockSpec((NDEV, rows, cb), lambda i: (0, 0, i))
    ispec = pl.BlockSpec((WSH, cb), lambda i: (0, i))
    ospec = pl.BlockSpec((1, e_sh, cb), lambda i: (0, 0, i))
    return pl.pallas_call(
        body, name="adamw_big", grid=(D // cb,),
        in_specs=[slab(WSHP), slab(WSHP), slab(e_sh), slab(e_sh), ispec, ispec, ispec, ospec, ospec, ospec],
        out_specs=[ispec] * 4 + [ospec] * 4, out_shape=[in_shape] * 4 + [out_shape] * 4,
        compiler_params=_params(),
    )(own_in, land_in, own_out, land_out, w_in_t, m_in_t, v_in_t, w_out, m_out, v_out)


F0 = 3 * DA


def _unshard_w_out(own, land):
    e_sh = D // NDEV

    def body(own_ref, land_ref, wo_ref):
        for j in range(NDEV):
            wo_ref[j * e_sh:(j + 1) * e_sh, :] = _pick_slab(j, own_ref, land_ref, slice(0, e_sh), per_peer=False)

    return pl.pallas_call(
        body, name="unshard_w_out", grid=(D // CB,),
        in_specs=[pl.BlockSpec((e_sh, CB), lambda i: (0, i)), pl.BlockSpec((NDEV, e_sh, CB), lambda i: (0, 0, i))],
        out_specs=pl.BlockSpec((D, CB), lambda i: (0, i)),
        out_shape=jax.ShapeDtypeStruct((D, D), BF16),
        compiler_params=_params(),
    )(own, land)


def _unshard_w_in(w_all):
    def body(w_ref, wt_ref):
        def ref_rows(lo, hi):
            pieces, r = [], lo
            while r < hi:
                sh, off = divmod(r, WSH)
                n = min(hi - r, WSH - off)
                pieces.append(w_ref[sh, off:off + n, :])
                r += n
            return pieces

        for s in range(NSEC):
            lo = s * DA if s < 3 else s * DA + H
            wt_ref[s * DA:(s + 1) * DA, :] = jnp.concatenate(ref_rows(lo, lo + DA), axis=0)
        wt_ref[NSEC * DA:DPROJ, :] = jnp.concatenate(
            ref_rows(F0, F0 + H) + [jnp.zeros((DF - H, CB), BF16)], axis=0)

    return pl.pallas_call(
        body, name="unshard_w_in", grid=(D // CB,),
        in_specs=[pl.BlockSpec((NDEV, WSHP, CB), lambda i: (0, 0, i))],
        out_specs=pl.BlockSpec((DPROJ, CB), lambda i: (0, i)),
        out_shape=jax.ShapeDtypeStruct((DPROJ, D), BF16),
        compiler_params=_params(),
    )(w_all)


def _shard_w_in_grads(dw_main, dw_f):
    def body(dm_ref, df_ref, p_ref):
        def ref_rows(lo, hi):
            pieces, r = [], lo
            while r < hi:
                if r < F0:
                    n = min(hi, F0) - r
                    pieces.append(dm_ref[r:r + n, :])
                elif r < F0 + H:
                    n = min(hi, F0 + H) - r
                    pieces.append(df_ref[r - F0:r - F0 + n, :])
                else:
                    n = hi - r
                    pieces.append(dm_ref[r - H:r - H + n, :])
                r += n
            return pieces

        for i in range(NDEV):
            rows = jnp.concatenate(ref_rows(i * WSH, (i + 1) * WSH) + [jnp.zeros((WSHP - WSH, CB), F32)], axis=0)
            p_ref[i] = rows.astype(BF16)

    col = lambda rows: pl.BlockSpec((rows, CB), lambda i: (0, i))
    return pl.pallas_call(
        body, name="shard_w_in_grads", grid=(D // CB,),
        in_specs=[col(NSEC * DA), col(DF)],
        out_specs=pl.BlockSpec((NDEV, WSHP, CB), lambda i: (0, 0, i)),
        out_shape=jax.ShapeDtypeStruct((NDEV, WSHP, D), BF16),
        compiler_params=_params(),
    )(dw_main, dw_f)


SMALL = ("norm_g", "final_norm_g", "attn_norm_g", "conv_norm_g", "b_f", "meta", "conv_w")


def _as_rows(x):
    return jnp.concatenate([x[:, r * TB:(r + 1) * TB] for r in range(x.shape[1] // TB)], axis=0)


def _as_line(rows):
    return jnp.concatenate([rows[r:r + 1, :] for r in range(rows.shape[0])], axis=1)


def _pad_rows(x, n=8):
    return jnp.concatenate([x, jnp.zeros((n - x.shape[0], x.shape[1]), F32)], axis=0)


def _tile_rows(a, rows, lanes=TB):
    a = a.reshape(rows, lanes)
    return jnp.pad(a, ((0, -rows % 8), (0, TB - lanes)))


def _pack_small_grads(dg_norm, dg_final, dga_p, dgc_p, dcw_p, db_b, dmeta, loss):
    def body(dgn_ref, dgf_ref, dga_ref, dgc_ref, dcw_ref, db_ref, dmeta_ref, loss_ref, out_ref):
        def lane_sums(p):
            return jnp.sum(p.T, axis=0, keepdims=True)

        lane = lax.broadcasted_iota(jnp.int32, (1, TB), 1)
        b_row = jnp.where(lane == H, loss_ref[...], 0.0)
        for h in range(H):
            b_row = b_row + jnp.where(lane == h, db_ref[h:h + 1, :], 0.0)
        common = jnp.concatenate([
            _as_rows(dgn_ref[...]), _as_rows(dgf_ref[...]), _pad_rows(_as_rows(lane_sums(dga_ref[...]))),
            _pad_rows(_as_rows(lane_sums(dgc_ref[...]))), _pad_rows(b_row)], axis=0)
        dcw = [lane_sums(dcw_ref[k]) for k in range(3)]
        for j in range(NDEV):
            cw = jnp.concatenate(
                [jnp.concatenate([r[:, j * DH:(j + 1) * DH], jnp.zeros((1, TB - DH), F32)], axis=1) for r in dcw],
                axis=0)
            out_ref[j] = jnp.concatenate([common, dmeta_ref[:, j * TB:(j + 1) * TB], _pad_rows(cw)], axis=0)

    return pl.pallas_call(
        body, name="pack_small_grads", out_shape=jax.ShapeDtypeStruct((NDEV, SROWS, TB), F32),
    )(dg_norm, dg_final, dga_p, dgc_p, dcw_p, db_b, dmeta, loss)


def _adamw_small(own, land, params):
    flat = [a for n in SMALL for a in params[n]]

    def body(*refs):
        own_ref, land_ref = refs[:2]
        ins = refs[2:2 + 3 * len(SMALL)]
        outs = refs[2 + 3 * len(SMALL):]
        g = _pick_slab(0, own_ref, land_ref, slice(0, SROWS))
        for j in range(1, NDEV):
            g = g + _pick_slab(j, own_ref, land_ref, slice(0, SROWS))
        grads = dict(
            norm_g=_as_line(g[0:8]), final_norm_g=_as_line(g[8:16]), attn_norm_g=_as_line(g[16:20]),
            conv_norm_g=_as_line(g[24:28]), b_f=g[32:33, :H], meta=g[40:56], conv_w=g[56:59, :DH][None])
        for i, n in enumerate(SMALL):
            w_ref, m_ref, v_ref = ins[3 * i:3 * i + 3]
            d, mn, vn = _adamw(w_ref[...], grads[n], m_ref[...], v_ref[...])
            for o_ref, val in zip(outs[4 * i:4 * i + 4], (grads[n], d, mn, vn)):
                o_ref[...] = val
        outs[-1][...] = g[32:33, H:H + 1]

    shapes = [jax.ShapeDtypeStruct(params[n][0].shape, F32) for n in SMALL for _ in range(4)]
    res = pl.pallas_call(
        body, name="adamw_small", out_shape=shapes + [jax.ShapeDtypeStruct((1, 1), F32)],
    )(own, land, *flat)
    return {n: res[4 * i:4 * i + 4] for i, n in enumerate(SMALL)}, res[-1]


def kernel(x, meta, norm_g, w_in, b_f, conv_w, attn_norm_g, conv_norm_g, w_out, final_norm_g, loss_target, m_meta, m_norm_g, m_w_in, m_b_f, m_conv_w, m_attn_norm_g, m_conv_norm_g, m_w_out, m_final_norm_g, v_meta, v_norm_g, v_w_in, v_b_f, v_conv_w, v_attn_norm_g, v_conv_norm_g, v_w_out, v_final_norm_g):
    seq = x.shape[1]
    L = seq + TB
    assert x.shape == (1, seq, D) and L % TT == 0 and w_in.shape == (1, D, WSH)
    x2 = x[0]
    tgt = loss_target[0]

    w_in_slab = jnp.pad(w_in[0].T, ((0, WSHP - WSH), (0, 0))).astype(BF16)
    w_out_slab = w_out[0].astype(BF16)
    meta_slab = jnp.concatenate([meta, _tile_rows(conv_w[0], 3, DH)], axis=0)
    wout_flight = _split_start(w_out_slab, "gather_w_out_start", per_peer=False)
    w_all, small_all = _all_gather([w_in_slab, meta_slab], "gather_w_in")

    w_t = _unshard_w_in(w_all)
    meta_full = jnp.transpose(small_all[:, :NM, :], (1, 0, 2)).reshape(NM, D)
    conv_w_full = jnp.transpose(small_all[:, NM:NM + 3, :DH], (1, 0, 2)).reshape(3, DA)

    lane_b = lambda p: jnp.broadcast_to(p.reshape(-1, DA, 1), (p.size // DA, DA, TB))
    cw_b = lane_b(conv_w_full)
    ga_b = lane_b(attn_norm_g)[0]
    gcn_b = lane_b(conv_norm_g)[0]

    u, proj_t, f_t, ktok, vtok = _inproj_fwd(x2, meta_full, norm_g + wout_flight[4][0, 0], w_t, L)
    cq, kaug, sg = _fgate_fwd(f_t, b_f.reshape(H, 1), ktok, L)
    o_t, lse = _attn_fwd(proj_t, kaug, cq, L)
    mix_t = _gate_fwd(o_t, proj_t, cw_b, ga_b, gcn_b, L)

    w_out_own, w_out_land = _split_wait(wout_flight, mix_t, "gather_w_out_wait", per_peer=False)
    w_out_full = _unshard_w_out(w_out_own, w_out_land)
    dout, dmix_t, dw_out, loss_part, dg_final = _outproj(
        mix_t, w_out_full, x2, meta_full, final_norm_g.reshape(1, D), tgt, L)
    dwo_flight = _split_start(dw_out.reshape(NDEV, D // NDEV, D), "exchange_dw_out_start", per_peer=True)
    do_t, dd, dg5_t, dga_p, dgc_p, dcw_p = _gate_bwd(dmix_t, o_t, proj_t, cw_b, ga_b + dwo_flight[4][0, 0], gcn_b, L)
    dq_t, dk_t, dv_t, dck, dcq = _attn_bwd(proj_t, kaug, vtok, do_t, lse, dd, cq, L)
    df_t, db_f = _fgate_bwd(dcq, dck, sg, L)
    dw_main, dw_f = _inproj_bwd_w(u, dq_t, dk_t, dv_t, dg5_t, df_t, L)
    dwi_flight = _split_start(_shard_w_in_grads(dw_main, dw_f), "exchange_dw_in_start", per_peer=True)
    grad_x, dmeta, dg_norm = _inproj_bwd_x(
        w_t, dq_t, dk_t, dv_t, dg5_t, df_t, dout, x2, meta_full, norm_g + dwi_flight[4][0, 0], L)
    small_parts = _pack_small_grads(dg_norm, dg_final, dga_p, dgc_p, dcw_p, db_f, dmeta, loss_part)
    small_flight = _split_start(small_parts, "exchange_small_start", per_peer=True)
    dwo_own, dwo_land = _split_wait(dwo_flight, small_flight[4], "exchange_dw_out_wait", per_peer=True)
    dwi_own, dwi_land = _split_wait(dwi_flight, dwo_land, "exchange_dw_in_wait", per_peer=True)

    big_out = _adamw_big(dwi_own, dwi_land, dwo_own, dwo_land,
                         w_in[0].T, m_w_in[0].T, v_w_in[0].T, w_out, m_w_out, v_w_out)
    g_w_in, d_w_in, nm_w_in, nv_w_in = [a.T[None] for a in big_out[:4]]
    g_w_out, d_w_out, nm_w_out, nv_w_out = big_out[4:]
    sm_own, sm_land = _split_wait(small_flight, big_out[4], "exchange_small_wait", per_peer=True)
    line = lambda a: a.reshape(1, D)
    small, loss = _adamw_small(sm_own, sm_land, dict(
        norm_g=(norm_g, m_norm_g, v_norm_g),
        final_norm_g=(line(final_norm_g), line(m_final_norm_g), line(v_final_norm_g)),
        attn_norm_g=(attn_norm_g, m_attn_norm_g, v_attn_norm_g),
        conv_norm_g=(conv_norm_g, m_conv_norm_g, v_conv_norm_g),
        b_f=(b_f, m_b_f, v_b_f), meta=(meta, m_meta, v_meta), conv_w=(conv_w, m_conv_w, v_conv_w)))
    small["final_norm_g"] = [a.reshape(D) for a in small["final_norm_g"]]
    order = ("meta", "norm_g", "w_in", "b_f", "conv_w", "attn_norm_g", "conv_norm_g", "w_out", "final_norm_g")
    groups = []
    for k, (wi, wo) in enumerate(((g_w_in, g_w_out), (d_w_in, d_w_out), (nm_w_in, nm_w_out), (nv_w_in, nv_w_out))):
        d = dict({n: small[n][k] for n in SMALL}, w_in=wi, w_out=wo)
        groups.append([d[n] for n in order])
    return (loss[0, 0], grad_x[None], *groups[0], *groups[1], *groups[2], *groups[3])
```

```python
import jax
import jax.numpy as jnp
from jax import lax
from jax.experimental import pallas as pl
from jax.experimental.pallas import tpu as pltpu

F32 = jnp.float32
BF16 = jnp.bfloat16

D = 1024
DA = 512
H = 8
DH = 64
NM = 16
TB = 128
P0 = TB - NM
TT = 3 * TB
HG = 8
NDEV = 8
NSEC = 8
DF = 16
DPROJ = NSEC * DA + DF
WSH = 513
WSHP = 528
WROWS = WSHP + D // NDEV
SROWS = 64
EPS = 1e-6
NEG = -1e30
LOG2E = 1.4426950408889634
LN2 = 0.6931471805599453
QSCALE = DH ** -0.5 * LOG2E
KA = 128
CB = 256
VMEM_LIMIT = 56 * 1024 * 1024

ADAM_LR = 0.001
ADAM_B1 = 0.9
ADAM_B2 = 0.999
ADAM_EPS = 1e-08
ADAM_WD = 0.01
ADAM_STEP = 10

NT_DIMS = (((1,), (1,)), ((), ()))
TN_DIMS = (((0,), (0,)), ((), ()))
MESH = pl.DeviceIdType.MESH


def _params(n_axes=1, vmem=VMEM_LIMIT):
    return pltpu.CompilerParams(dimension_semantics=("arbitrary",) * n_axes, vmem_limit_bytes=vmem)


def _dot(a, b, dims=None):
    if dims is None:
        return jnp.dot(a, b, preferred_element_type=F32)
    return lax.dot_general(a, b, dims, preferred_element_type=F32)


def _my_place():
    return lax.axis_index("x"), lax.axis_index("y"), lax.axis_index("c")


def _all_gather(xs, name):
    n = len(xs)

    def body(*refs):
        x_refs, out_refs = refs[:n], refs[n:2 * n]
        send_sems, recv_sems, local_sems = refs[2 * n:]
        mx, my, mc = _my_place()

        def across(px, py, pc, axis_a):
            flip_x = pc if axis_a else 1 - pc
            return (px + flip_x) % 2, (py + 1 - flip_x) % 2, pc

        def idx(p):
            return 4 * p[0] + 2 * p[1] + p[2]

        me, sib = (mx, my, mc), (mx, my, 1 - mc)
        a_nbr, b_nbr = across(*me, True), across(*me, False)
        diag = across(*b_nbr, True)
        sib_a, sib_b = across(*sib, True), across(*sib, False)
        sib_diag = across(*sib_b, True)

        waits = []
        for t in range(n):
            out_ref = out_refs[t]

            def copy(k, block, to, src=None, out_ref=out_ref, t=t):
                return pltpu.make_async_remote_copy(
                    src_ref=out_ref.at[idx(block)] if src is None else src, dst_ref=out_ref.at[idx(block)],
                    send_sem=send_sems.at[7 * t + k], recv_sem=recv_sems.at[7 * t + k],
                    device_id=to, device_id_type=MESH)

            mine = pltpu.make_async_copy(x_refs[t], out_ref.at[idx(me)], local_sems.at[t])
            mine.start()
            started = [copy(0, me, sib, src=x_refs[t]), copy(1, me, a_nbr, src=x_refs[t]),
                       copy(2, me, b_nbr, src=x_refs[t])]
            for cp in started:
                cp.start()
            waits.append((copy, mine, started))
        relays = ((1, a_nbr, ((3, b_nbr), (4, sib))), (2, b_nbr, ((5, sib),)), (3, diag, ((6, sib),)))
        for landed, block, onward in relays:
            for copy, _, started in waits:
                copy(landed, block, me).wait_recv()
                for k, to in onward:
                    started.append(copy(k, block, to))
                    started[-1].start()
        for copy, mine, started in waits:
            for k, block in ((0, sib), (4, sib_a), (5, sib_b), (6, sib_diag)):
                copy(k, block, me).wait_recv()
            for cp in started:
                cp.wait_send()
            mine.wait()

    any_spec = pl.BlockSpec(memory_space=pl.ANY)
    return pl.pallas_call(
        body, name=name,
        out_shape=[jax.ShapeDtypeStruct((NDEV,) + x.shape, x.dtype) for x in xs],
        in_specs=[any_spec] * n, out_specs=[any_spec] * n,
        scratch_shapes=[pltpu.SemaphoreType.DMA((7 * n,)), pltpu.SemaphoreType.DMA((7 * n,)),
                        pltpu.SemaphoreType.DMA((n,))],
    )(*xs)


_HBM =pl.BlockSpec(memory_space=pltpu.HBM)
_SEM = pl.BlockSpec(memory_space=pltpu.SEMAPHORE)
_EFFECT = pltpu.SideEffectType.DATAFLOW_SIDE_EFFECTING


def _peer_of(m, place):
    mx, my, mc = place
    return ((1 - mx) if m & 4 else mx, (1 - my) if m & 2 else my, (1 - mc) if m & 1 else mc)


def _split_copies(src_ref, land_ref, send_sems, recv_sems, per_peer, incoming):
    place = _my_place()
    me = 4 * place[0] + 2 * place[1] + place[2]
    out = []
    for m in range(1, NDEV):
        px, py, pc = _peer_of(m, place)
        peer = 4 * px + 2 * py + pc
        src = (src_ref.at[me] if incoming else src_ref.at[peer]) if per_peer else src_ref
        out.append(pltpu.make_async_remote_copy(
            src_ref=src, dst_ref=land_ref.at[peer if incoming else me],
            send_sem=send_sems.at[m - 1], recv_sem=recv_sems.at[m - 1],
            device_id=(px, py, pc), device_id_type=MESH))
    return out


def _split_start(src, name, per_peer):
    slab = src.shape[1:] if per_peer else src.shape

    def body(src_ref, land_ref, send_sems, recv_sems, src_thru, land_thru, token):
        for cp in _split_copies(src_ref, land_ref, send_sems, recv_sems, per_peer, incoming=False):
            cp.start()
        token[...] = jnp.zeros_like(token)

    return pl.pallas_call(
        body, name=name,
        out_shape=(pltpu.SemaphoreType.DMA((NDEV - 1,)), pltpu.SemaphoreType.DMA((NDEV - 1,)),
                   pltpu.HBM(src.shape, src.dtype), pltpu.HBM((NDEV,) + slab, src.dtype),
                   jax.ShapeDtypeStruct((8, TB), F32)),
        in_specs=(_HBM, _HBM), out_specs=(_SEM, _SEM, _HBM, _HBM, pl.BlockSpec(memory_space=pltpu.VMEM)),
        input_output_aliases={0: 2, 1: 3},
        compiler_params=pltpu.CompilerParams(has_side_effects=_EFFECT),
    )(pltpu.with_memory_space_constraint(src, pltpu.HBM),
      pltpu.with_memory_space_constraint(lax.empty((NDEV,) + slab, src.dtype), pltpu.HBM))


def _split_wait(handles, after, name, per_peer):
    send_sems, recv_sems, src_thru, land_thru, _ = handles

    def body(src_ref, land_ref, send_sems, recv_sems, after_ref, src_out, land_out):
        for cp in _split_copies(src_ref, land_ref, send_sems, recv_sems, per_peer, incoming=False):
            cp.wait_send()
        for cp in _split_copies(src_ref, land_ref, send_sems, recv_sems, per_peer, incoming=True):
            cp.wait_recv()

    return pl.pallas_call(
        body, name=name,
        out_shape=(pltpu.HBM(src_thru.shape, src_thru.dtype), pltpu.HBM(land_thru.shape, land_thru.dtype)),
        in_specs=(_HBM, _HBM, _SEM, _SEM, pl.BlockSpec(memory_space=pl.ANY)), out_specs=(_HBM, _HBM),
        input_output_aliases={0: 0, 1: 1},
        compiler_params=pltpu.CompilerParams(has_side_effects=_EFFECT),
    )(src_thru, land_thru, send_sems, recv_sems, after)


def _pick_slab(j, own_ref, land_ref, rows, per_peer=True):
    mx, my, mc = _my_place()
    me = 4 * mx + 2 * my + mc
    own = (lambda: own_ref[j, rows, :]) if per_peer else (lambda: own_ref[rows, :])
    return lax.cond(me == j, own, lambda: land_ref[j, rows, :])


def _h_block(t, x_ref, meta_ref):
    first = jnp.concatenate([jnp.zeros((P0, D), F32), meta_ref[...]], axis=0)
    return jnp.where(t == 0, first, x_ref[...])


def _x_spec():
    return pl.BlockSpec((TB, D), lambda t: (jnp.maximum(t - 1, 0), 0))


def _x_specs3():
    return [pl.BlockSpec((TB, D), lambda j: (jnp.maximum(3 * j - 1, 0), 0)),
            pl.BlockSpec((TB, D), lambda j: (3 * j, 0)),
            pl.BlockSpec((TB, D), lambda j: (3 * j + 1, 0))]


def _h_tile(j, xa_ref, xb_ref, xc_ref, meta_ref):
    first = jnp.concatenate([jnp.zeros((P0, D), F32), meta_ref[...]], axis=0)
    return jnp.concatenate([jnp.where(j == 0, first, xa_ref[...]), xb_ref[...], xc_ref[...]], axis=0)


def _full_spec(shape):
    return pl.BlockSpec(shape, lambda *_: (0,) * len(shape))


def _sigmoid(z):
    return 1.0 / (1.0 + jnp.exp(-z))


def _lane_tiles_sum(x):
    out = x[:, :TB]
    for i in range(1, x.shape[1] // TB):
        out = out + x[:, i * TB:(i + 1) * TB]
    return out


def _inproj_fwd(x, meta_full, norm_g, w_t, L):
    nj = L // TT

    def body(xa_ref, xb_ref, xc_ref, meta_ref, g_ref, w_ref, u_ref, proj_ref, f_ref, ktok_ref, vtok_ref):
        hb = _h_tile(pl.program_id(0), xa_ref, xb_ref, xc_ref, meta_ref)
        r = lax.rsqrt(jnp.mean(hb * hb, axis=-1, keepdims=True) + EPS)
        u = (hb * r * g_ref[...]).astype(BF16)
        u_ref[...] = u
        for s in range(NSEC):
            p = _dot(u, w_ref[s * DA:(s + 1) * DA, :], NT_DIMS)
            if s == 0:
                p = p * QSCALE
            if s in (1, 2):
                tok_ref = ktok_ref if s == 1 else vtok_ref
                for h in range(H):
                    tok_ref[h] = p[:, h * DH:(h + 1) * DH].astype(BF16)
            proj_ref[s * DA:(s + 1) * DA, :] = p.T.astype(BF16)
        f_ref[...] = _dot(w_ref[NSEC * DA:DPROJ, :], u, NT_DIMS)[:H]

    return pl.pallas_call(
        body, name="inproj_fwd", grid=(nj,),
        in_specs=_x_specs3() + [_full_spec((NM, D)), _full_spec((1, D)), _full_spec((DPROJ, D))],
        out_specs=[
            pl.BlockSpec((TT, D), lambda t: (t, 0)),
            pl.BlockSpec((NSEC * DA, TT), lambda t: (0, t)),
            pl.BlockSpec((H, TT), lambda t: (0, t)),
            pl.BlockSpec((H, TT, DH), lambda t: (0, t, 0)),
            pl.BlockSpec((H, TT, DH), lambda t: (0, t, 0)),
        ],
        out_shape=[
            jax.ShapeDtypeStruct((L, D), BF16),
            jax.ShapeDtypeStruct((NSEC * DA, L), BF16),
            jax.ShapeDtypeStruct((H, L), F32),
            jax.ShapeDtypeStruct((H, L, DH), BF16),
            jax.ShapeDtypeStruct((H, L, DH), BF16),
        ],
        compiler_params=_params(),
    )(x, x, x, meta_full, norm_g, w_t)


def _split3(x):
    hi = x.astype(BF16).astype(F32)
    r = x - hi
    mid = r.astype(BF16).astype(F32)
    return hi, mid, (r - mid).astype(BF16).astype(F32)


def _bias_rows(bias):
    one = jnp.ones((1, TT), F32)
    zero = jnp.zeros((1, TT), F32)
    parts = [zero] * 3 if bias is None else list(_split3(bias))
    return jnp.concatenate([one] * 3 + parts + [zero] * (DF - 6), axis=0).astype(BF16)


def _fgate_fwd(f_t, b_col, ktok, L):
    nb = L // TB

    def body(f_ref, b_ref, ktok_ref, cq_ref, kaug_ref, sg_ref):
        z = f_ref[...] + b_ref[...]
        idx = lax.broadcasted_iota(jnp.int32, (H, L), 1)
        real = idx >= P0
        lf = jnp.where(real, jnp.minimum(z, 0.0) - jnp.log1p(jnp.exp(-jnp.abs(z))), 0.0)
        sg_ref[...] = jnp.where(real, 1.0 / (1.0 + jnp.exp(z)), 0.0)
        c = lf
        s = 1
        while s < L:
            c = c + jnp.where(idx >= s, pltpu.roll(c, s, 1), 0.0)
            s *= 2
        c = c * LOG2E
        for h in range(H):
            cq_ref[h] = c[h:h + 1, :]
        hi, mid, lo = _split3(-jnp.where(real, c, -NEG))
        lane = lax.broadcasted_iota(jnp.int32, (TB, KA), 1)
        ones = jnp.ones((3, TB), F32)
        for h in range(H):
            for b in range(nb):
                blk = slice(b * TB, (b + 1) * TB)
                cols = jnp.concatenate([
                    jnp.zeros((DH, TB), F32), hi[h:h + 1, blk], mid[h:h + 1, blk], lo[h:h + 1, blk], ones,
                    jnp.zeros((KA - DH - 6, TB), F32)], axis=0).T
                k = jnp.concatenate([ktok_ref[h, blk, :].astype(F32), jnp.zeros((TB, KA - DH), F32)], axis=1)
                kaug_ref[h, blk, :] = jnp.where(lane < DH, k, cols).astype(BF16)

    return pl.pallas_call(
        body, name="fgate_fwd",
        out_shape=[
            jax.ShapeDtypeStruct((H, 1, L), F32),
            jax.ShapeDtypeStruct((H, L, KA), BF16),
            jax.ShapeDtypeStruct((H, L), F32),
        ],
        compiler_params=pltpu.CompilerParams(vmem_limit_bytes=VMEM_LIMIT),
    )(f_t, b_col, ktok)


def _causal_mask():
    r = lax.broadcasted_iota(jnp.int32, (TT, TT), 0)
    c = lax.broadcasted_iota(jnp.int32, (TT, TT), 1)
    return r <= c


def _attn_fwd(proj_t, kaug, cq, L):
    nq = L // TT

    def body(q_ref, kaug_ref, v_ref, cq_ref, o_ref, lse_ref,
             qa_scr, s_scr, cmax_scr, m_scr, p_scr, alpha_scr, acc_scr):
        j = pl.program_id(1)
        rows = [slice(g * DH, (g + 1) * DH) for g in range(HG)]
        ones = jnp.ones((DF, TT), BF16)
        for g in range(HG):
            qa_scr[g] = jnp.concatenate(
                [q_ref[rows[g], :], _bias_rows(None), jnp.zeros((KA - DH - DF, TT), BF16)], axis=0)

        def scores(kt, masked):
            k_off = pl.multiple_of(kt * TT, TT)
            for g in range(HG):
                s = _dot(kaug_ref[g, pl.ds(k_off, TT), :], qa_scr[g])
                if masked:
                    s = jnp.where(_causal_mask(), s, NEG)
                s_scr[g] = s
                cmax_scr[g] = jnp.max(s, axis=0, keepdims=True)

        def softmax():
            for g in range(HG):
                m_old = m_scr[g]
                m_new = jnp.maximum(m_old, cmax_scr[g])
                alpha_scr[g] = jnp.exp2(m_old - m_new)
                p_scr[g] = jnp.exp2(s_scr[g] - m_new).astype(BF16)
                m_scr[g] = m_new

        def weighted_sum(kt):
            k_off = pl.multiple_of(kt * TT, TT)
            for g in range(HG):
                v1 = jnp.concatenate([v_ref[rows[g], pl.ds(k_off, TT)], ones], axis=0)
                acc_scr[g] = alpha_scr[g] * acc_scr[g] + _dot(v1, p_scr[g])

        m_scr[...] = jnp.full_like(m_scr, NEG)
        acc_scr[...] = jnp.zeros_like(acc_scr)

        scores(j, True)

        @pl.when(j >= 1)
        def _():
            softmax()
            scores(j - 1, False)

        def step(i, c):
            weighted_sum(j - i + 1)
            softmax()
            scores(j - i - 1, False)
            return c

        lax.fori_loop(1, j, step, 0)

        @pl.when(j >= 1)
        def _():
            weighted_sum(1)

        softmax()
        weighted_sum(0)
        for g in range(HG):
            l = acc_scr[g, DH:DH + 1, :]
            o_ref[rows[g], :] = acc_scr[g, :DH, :] * (1.0 / l)
            lse_ref[g] = m_scr[g] + jnp.log2(l) + cq_ref[g]

    return pl.pallas_call(
        body, name="attn_fwd", grid=(H // HG, nq),
        in_specs=[
            pl.BlockSpec((HG * DH, TT), lambda h, j: (h, j)),
            pl.BlockSpec((HG, L, KA), lambda h, j: (h, 0, 0)),
            pl.BlockSpec((HG * DH, L), lambda h, j: (2 * H // HG + h, 0)),
            pl.BlockSpec((HG, 1, TT), lambda h, j: (h, 0, j)),
        ],
        out_specs=[
            pl.BlockSpec((HG * DH, TT), lambda h, j: (h, j)),
            pl.BlockSpec((HG, 1, TT), lambda h, j: (h, 0, j)),
        ],
        out_shape=[jax.ShapeDtypeStruct((DA, L), F32), jax.ShapeDtypeStruct((H, 1, L), F32)],
        scratch_shapes=[pltpu.VMEM((HG, KA, TT), BF16), pltpu.VMEM((HG, TT, TT), F32), pltpu.VMEM((HG, 1, TT), F32),
                        pltpu.VMEM((HG, 1, TT), F32), pltpu.VMEM((HG, TT, TT), BF16), pltpu.VMEM((HG, 1, TT), F32),
                        pltpu.VMEM((HG, DH + DF, TT), F32)],
        compiler_params=_params(2),
    )(proj_t, kaug, proj_t, cq)


def _gate_group(rows, o_ref, za_ref, gb_ref, gc_ref, xc_ref, zc_ref, gcp_ref, xcp_ref, cw_ref, ga_ref, gcn_ref, first):
    n_rep = TT // TB
    f32 = lambda r: r[rows, :].astype(F32)
    o, za, gb, gc, xc, zc = o_ref[rows, :], f32(za_ref), f32(gb_ref), f32(gc_ref), f32(xc_ref), f32(zc_ref)
    a = gc * xc
    a_prev = jnp.where(first, 0.0, f32(gcp_ref) * f32(xcp_ref))
    full = jnp.concatenate([a_prev, a], axis=1)
    a1 = pltpu.roll(full, 1, 1)[:, TB:]
    a2 = pltpu.roll(full, 2, 1)[:, TB:]
    w0 = jnp.tile(cw_ref[0, rows, :], (1, n_rep))
    w1 = jnp.tile(cw_ref[1, rows, :], (1, n_rep))
    w2 = jnp.tile(cw_ref[2, rows, :], (1, n_rep))
    cv = w0 * a2 + w1 * a1 + w2 * a
    e = gb * cv
    rc = lax.rsqrt(jnp.mean(e * e, axis=0, keepdims=True) + EPS)
    ec = e * rc
    ra = lax.rsqrt(jnp.mean(o * o, axis=0, keepdims=True) + EPS)
    oa = o * ra
    g_a = jnp.tile(ga_ref[rows, :], (1, n_rep))
    g_c = jnp.tile(gcn_ref[rows, :], (1, n_rep))
    sa = _sigmoid(za)
    sc = _sigmoid(zc)
    return dict(o=o, za=za, gb=gb, gc=gc, xc=xc, zc=zc, a=a, a1=a1, a2=a2, w0=w0, w1=w1, w2=w2, cv=cv, e=e,
                rc=rc, ec=ec, ra=ra, oa=oa, g_a=g_a, g_c=g_c, sa=sa, sc=sc)


def _gate_specs(nj, rev):
    def jj(i):
        return (nj - 1 - i) if rev else i

    def sec(s):
        return pl.BlockSpec((DA, TT), lambda i: (s, jj(i)))

    def halo(s):
        return pl.BlockSpec((DA, TB), lambda i: (s, jnp.maximum(3 * jj(i) - 1, 0)))

    return [pl.BlockSpec((DA, TT), lambda i: (0, jj(i))), sec(3), sec(4), sec(5), sec(6), sec(7), halo(5), halo(6),
            _full_spec((3, DA, TB)), _full_spec((DA, TB)), _full_spec((DA, TB))]


def _gate_fwd(o_t, proj_t, cw_b, ga_b, gcn_b, L):
    nj = L // TT

    def body(o_ref, za_ref, gb_ref, gc_ref, xc_ref, zc_ref, gcp_ref, xcp_ref, cw_ref, ga_ref, gcn_ref, mix_ref):
        j = pl.program_id(0)

        def group(h, c):
            r0 = pl.multiple_of(h * DH, DH)
            g = _gate_group(pl.ds(r0, DH), o_ref, za_ref, gb_ref, gc_ref, xc_ref, zc_ref, gcp_ref, xcp_ref,
                            cw_ref, ga_ref, gcn_ref, j == 0)
            mix_ref[pl.ds(r0, DH), :] = (g["oa"] * g["g_a"] * (g["za"] * g["sa"])).astype(BF16)
            mix_ref[pl.ds(DA + r0, DH), :] = (g["ec"] * g["g_c"] * (g["zc"] * g["sc"])).astype(BF16)
            return c

        lax.fori_loop(0, H, group, 0, unroll=2)

    return pl.pallas_call(
        body, name="gate_fwd", grid=(nj,),
        in_specs=_gate_specs(nj, False),
        out_specs=pl.BlockSpec((2 * DA, TT), lambda j: (0, j)),
        out_shape=jax.ShapeDtypeStruct((2 * DA, L), BF16),
        compiler_params=_params(),
    )(o_t, proj_t, proj_t, proj_t, proj_t, proj_t, proj_t, proj_t, cw_b, ga_b, gcn_b)


def _outproj(mix_t, w_out, x, meta_full, fng, target, L):
    nj = L // TT

    def body(mix_ref, w_ref, xa_ref, xb_ref, xc_ref, meta_ref, g_ref, ta_ref, tb_ref, tc_ref,
             dout_ref, dmix_ref, dwb_ref, loss_ref, dg_ref, dw_ref):
        t = pl.program_id(0)

        @pl.when(t == 0)
        def _():
            dw_ref[...] = jnp.zeros_like(dw_ref)
            loss_ref[...] = jnp.zeros_like(loss_ref)
            dg_ref[...] = jnp.zeros_like(dg_ref)

        mix = mix_ref[...]
        o = _dot(mix, w_ref[...], TN_DIMS) + _h_tile(t, xa_ref, xb_ref, xc_ref, meta_ref)
        r = lax.rsqrt(jnp.mean(o * o, axis=-1, keepdims=True) + EPS)
        g = g_ref[...]
        orn = o * r
        tgt = jnp.concatenate([ta_ref[...], tb_ref[...], tc_ref[...]], axis=0)
        row = lax.broadcasted_iota(jnp.int32, (TT, 1), 0)
        real = jnp.where((t > 0) | (row >= TB), 1.0, 0.0)
        diff = (orn * g - tgt) * real
        loss_ref[...] += 0.5 * jnp.sum(diff * diff) * (1.0 / D)
        dy = diff * (1.0 / D)
        dg_ref[...] += jnp.sum(dy * orn, axis=0, keepdims=True)
        gy = dy * g
        dout = r * gy - orn * (r * jnp.mean(gy * orn, axis=-1, keepdims=True))
        dout_ref[...] = dout
        db = dout.astype(BF16)
        dmix_ref[...] = _dot(db, w_ref[...], NT_DIMS).T.astype(BF16)
        dw_ref[...] += _dot(mix, db)

        @pl.when(t == nj - 1)
        def _():
            dwb_ref[...] = dw_ref[...].astype(BF16)

    return pl.pallas_call(
        body, name="outproj", grid=(nj,),
        in_specs=[pl.BlockSpec((D, TT), lambda t: (0, t)), _full_spec((D, D))] + _x_specs3()
                 + [_full_spec((NM, D)), _full_spec((1, D))] + _x_specs3(),
        out_specs=[pl.BlockSpec((TT, D), lambda t: (t, 0)), pl.BlockSpec((D, TT), lambda t: (0, t)),
                   _full_spec((D, D)), _full_spec((1, 1)), _full_spec((1, D))],
        out_shape=[jax.ShapeDtypeStruct((L, D), F32), jax.ShapeDtypeStruct((D, L), BF16),
                   jax.ShapeDtypeStruct((D, D), BF16), jax.ShapeDtypeStruct((1, 1), F32),
                   jax.ShapeDtypeStruct((1, D), F32)],
        scratch_shapes=[pltpu.VMEM((D, D), F32)],
        compiler_params=_params(),
    )(mix_t, w_out, x, x, x, meta_full, fng, target, target, target)


def _gate_bwd(dmix_t, o_t, proj_t, cw_b, ga_b, gcn_b, L):
    nj = L // TT

    def body(dmix_ref, o_ref, za_ref, gb_ref, gc_ref, xc_ref, zc_ref, gcp_ref, xcp_ref, cw_ref, ga_ref, gcn_ref,
             do_ref, dd_ref, dg5_ref, dga_ref, dgc_ref, dcw_ref, carry_ref):
        i = pl.program_id(0)
        j = nj - 1 - i

        @pl.when(i == 0)
        def _():
            carry_ref[...] = jnp.zeros_like(carry_ref)
            dga_ref[...] = jnp.zeros_like(dga_ref)
            dgc_ref[...] = jnp.zeros_like(dgc_ref)
            dcw_ref[...] = jnp.zeros_like(dcw_ref)

        def group(h, c):
            r0 = pl.multiple_of(h * DH, DH)
            rows = pl.ds(r0, DH)
            sec = lambda s: pl.ds(s * DA + r0, DH)
            g = _gate_group(rows, o_ref, za_ref, gb_ref, gc_ref, xc_ref, zc_ref, gcp_ref, xcp_ref,
                            cw_ref, ga_ref, gcn_ref, j == 0)
            o, za, gb, gc, xc, zc, sa, sc = (g[n] for n in ("o", "za", "gb", "gc", "xc", "zc", "sa", "sc"))
            dya = dmix_ref[rows, :].astype(F32)
            dyc = dmix_ref[pl.ds(DA + r0, DH), :].astype(F32)

            dn = dya * (za * sa)
            dg5_ref[sec(0), :] = (dya * (g["oa"] * g["g_a"]) * (sa * (1.0 + za * (1.0 - sa)))).astype(BF16)
            dga_ref[rows, :] += _lane_tiles_sum(dn * g["oa"])
            dng = dn * g["g_a"]
            mean_a = jnp.mean(dng * g["oa"], axis=0, keepdims=True)
            do = (dng - g["oa"] * mean_a) * g["ra"]
            do_ref[rows, :] = do.astype(BF16)
            dd_ref[h] = jnp.sum(do * o, axis=0, keepdims=True)

            dnc = dyc * (zc * sc)
            dg5_ref[sec(4), :] = (dyc * (g["ec"] * g["g_c"]) * (sc * (1.0 + zc * (1.0 - sc)))).astype(BF16)
            dgc_ref[rows, :] += _lane_tiles_sum(dnc * g["ec"])
            dncg = dnc * g["g_c"]
            mean_c = jnp.mean(dncg * g["ec"], axis=0, keepdims=True)
            de = (dncg - g["ec"] * mean_c) * g["rc"]
            dg5_ref[sec(1), :] = (de * g["cv"]).astype(BF16)
            dcv = de * gb
            full = jnp.concatenate([dcv, carry_ref[rows, :]], axis=1)
            d1 = pltpu.roll(full, TT + TB - 1, 1)[:, :TT]
            d2 = pltpu.roll(full, TT + TB - 2, 1)[:, :TT]
            carry_ref[rows, :] = dcv[:, :TB]
            da = g["w2"] * dcv + g["w1"] * d1 + g["w0"] * d2
            dg5_ref[sec(2), :] = (da * xc).astype(BF16)
            dg5_ref[sec(3), :] = (da * gc).astype(BF16)
            dcw_ref[0, rows, :] += _lane_tiles_sum(dcv * g["a2"])
            dcw_ref[1, rows, :] += _lane_tiles_sum(dcv * g["a1"])
            dcw_ref[2, rows, :] += _lane_tiles_sum(dcv * g["a"])
            return c

        lax.fori_loop(0, H, group, 0, unroll=2)

    rj = lambda i: nj - 1 - i
    return pl.pallas_call(
        body, name="gate_bwd", grid=(nj,),
        in_specs=[pl.BlockSpec((2 * DA, TT), lambda i: (0, rj(i)))] + _gate_specs(nj, True),
        out_specs=[
            pl.BlockSpec((DA, TT), lambda i: (0, rj(i))),
            pl.BlockSpec((H, 1, TT), lambda i: (0, 0, rj(i))),
            pl.BlockSpec((5 * DA, TT), lambda i: (0, rj(i))),
            _full_spec((DA, TB)), _full_spec((DA, TB)), _full_spec((3, DA, TB)),
        ],
        out_shape=[
            jax.ShapeDtypeStruct((DA, L), BF16),
            jax.ShapeDtypeStruct((H, 1, L), F32),
            jax.ShapeDtypeStruct((5 * DA, L), BF16),
            jax.ShapeDtypeStruct((DA, TB), F32),
            jax.ShapeDtypeStruct((DA, TB), F32),
            jax.ShapeDtypeStruct((3, DA, TB), F32),
        ],
        scratch_shapes=[pltpu.VMEM((DA, TB), F32)],
        compiler_params=_params(),
    )(dmix_t, o_t, proj_t, proj_t, proj_t, proj_t, proj_t, proj_t, proj_t, cw_b, ga_b, gcn_b)


def _attn_bwd(proj_t, kaug, vtok, do_t, lse, dd, cq, L):
    nk = L // TT

    def body(q_ref, kaug_ref, vtok_ref, kt_ref, do_ref, lse_ref, dd_ref, cq_ref,
             dq_ref, dk_ref, dv_ref, dck_ref, dcq_ref, dq_acc, kt1_scr, s_scr, dp_scr, dv_scr, dk_scr):
        i = pl.program_id(1)

        @pl.when(i == 0)
        def _():
            dq_acc[...] = jnp.zeros_like(dq_acc)

        rows = [slice(g * DH, (g + 1) * DH) for g in range(HG)]
        ones = jnp.ones((DF, TT), BF16)
        zpad = jnp.zeros((KA - DH - DF, TT), BF16)
        for g in range(HG):
            kt1_scr[g] = jnp.concatenate([kt_ref[rows[g], :], ones], axis=0)
        dv_scr[...] = jnp.zeros_like(dv_scr)
        dk_scr[...] = jnp.zeros_like(dk_scr)

        def q_rows(g, q_off):
            bias = cq_ref[g, :, pl.ds(q_off, TT)] - lse_ref[g, :, pl.ds(q_off, TT)]
            return jnp.concatenate([q_ref[rows[g], pl.ds(q_off, TT)], _bias_rows(bias)], axis=0)

        def scores(jq, masked):
            q_off = pl.multiple_of(jq * TT, TT)
            for g in range(HG):
                s = _dot(kaug_ref[g], jnp.concatenate([q_rows(g, q_off), zpad], axis=0))
                if masked:
                    s = jnp.where(_causal_mask(), s, NEG)
                s_scr[g] = s
                dp_scr[g] = _dot(vtok_ref[g], do_ref[rows[g], pl.ds(q_off, TT)])

        def grads(jq):
            q_off = pl.multiple_of(jq * TT, TT)
            for g in range(HG):
                p = jnp.exp2(s_scr[g])
                ds = (p * (dp_scr[g] - dd_ref[g, :, pl.ds(q_off, TT)])).astype(BF16)
                do1 = jnp.concatenate([do_ref[rows[g], pl.ds(q_off, TT)], jnp.zeros((KA - DH, TT), BF16)], axis=0)
                q1 = jnp.concatenate([q_rows(g, q_off), zpad], axis=0)
                dv_scr[g] += _dot(p.astype(BF16), do1, NT_DIMS)
                dk_scr[g] += _dot(ds, q1, NT_DIMS)
                dq_acc[g, :, pl.ds(q_off, TT)] += _dot(kt1_scr[g], ds)

        scores(i, True)

        def step(jq, c):
            grads(jq)
            scores(jq + 1, False)
            return c

        lax.fori_loop(i, nk - 1, step, 0)
        grads(nk - 1)
        for g in range(HG):
            dv_ref[rows[g], :] = dv_scr[g].T[:DH, :].astype(BF16)
            dk_t = dk_scr[g].T
            dk_ref[rows[g], :] = (dk_t[:DH, :] * LN2).astype(BF16)
            dck_ref[g] = dk_t[DH:DH + 1, :]

        @pl.when(i == nk - 1)
        def _():
            for g in range(HG):
                dq_ref[rows[g], :] = (dq_acc[g, :DH, :] * (DH ** -0.5)).astype(BF16)
                dcq_ref[g] = dq_acc[g, DH:DH + 1, :]

    head = lambda h, i: (h, 0)
    row = lambda h, i: (h, 0, 0)
    return pl.pallas_call(
        body, name="attn_bwd", grid=(H // HG, nk),
        in_specs=[
            pl.BlockSpec((HG * DH, L), head),
            pl.BlockSpec((HG, TT, KA), lambda h, i: (h, i, 0)),
            pl.BlockSpec((HG, TT, DH), lambda h, i: (h, i, 0)),
            pl.BlockSpec((HG * DH, TT), lambda h, i: (H // HG + h, i)),
            pl.BlockSpec((HG * DH, L), head),
            pl.BlockSpec((HG, 1, L), row), pl.BlockSpec((HG, 1, L), row), pl.BlockSpec((HG, 1, L), row),
        ],
        out_specs=[
            pl.BlockSpec((HG * DH, L), head),
            pl.BlockSpec((HG * DH, TT), lambda h, i: (h, i)),
            pl.BlockSpec((HG * DH, TT), lambda h, i: (h, i)),
            pl.BlockSpec((HG, 1, TT), lambda h, i: (h, 0, i)),
            pl.BlockSpec((HG, 1, L), row),
        ],
        out_shape=[jax.ShapeDtypeStruct((DA, L), BF16), jax.ShapeDtypeStruct((DA, L), BF16),
                   jax.ShapeDtypeStruct((DA, L), BF16), jax.ShapeDtypeStruct((H, 1, L), F32),
                   jax.ShapeDtypeStruct((H, 1, L), F32)],
        scratch_shapes=[
            pltpu.VMEM((HG, DH + DF, L), F32),
            pltpu.VMEM((HG, DH + DF, TT), BF16),
            pltpu.VMEM((HG, TT, TT), F32), pltpu.VMEM((HG, TT, TT), F32),
            pltpu.VMEM((HG, TT, KA), F32), pltpu.VMEM((HG, TT, KA), F32)],
        compiler_params=_params(2),
    )(proj_t, kaug, vtok, proj_t, do_t, lse, dd, cq)


def _fgate_bwd(dcq, dck, sg, L):
    def body(dcq_ref, dck_ref, sg_ref, df_ref, db_ref):
        dc = jnp.concatenate([dcq_ref[h] - dck_ref[h] for h in range(H)], axis=0)
        idx = lax.broadcasted_iota(jnp.int32, (H, L), 1)
        r = dc
        s = 1
        while s < L:
            r = r + jnp.where(idx + s < L, pltpu.roll(r, L - s, 1), 0.0)
            s *= 2
        df = r * sg_ref[...]
        db_ref[...] = jnp.broadcast_to(jnp.sum(df, axis=1, keepdims=True), (H, TB))
        df_ref[...] = jnp.concatenate([df, jnp.zeros((DF - H, L), F32)], axis=0).astype(BF16)

    return pl.pallas_call(
        body, name="fgate_bwd",
        out_shape=[jax.ShapeDtypeStruct((DF, L), BF16), jax.ShapeDtypeStruct((H, TB), F32)],
        compiler_params=pltpu.CompilerParams(vmem_limit_bytes=VMEM_LIMIT),
    )(dcq, dck, sg)


def _inproj_bwd_x(w, dq_t, dk_t, dv_t, dg5_t, df_t, dout, x, meta_full, norm_g, L):
    nb = L // TB
    seq = x.shape[0]

    def body(w_ref, dq_ref, dk_ref, dv_ref, dg5_ref, df_ref, dout_ref, x_ref, meta_ref, g_ref,
             gx_ref, dmeta_ref, dg_ref):
        t = pl.program_id(0)

        @pl.when(t == 0)
        def _():
            dg_ref[...] = jnp.zeros_like(dg_ref)

        du = _dot(dq_ref[...], w_ref[0:DA, :], TN_DIMS)
        du += _dot(dk_ref[...], w_ref[DA:2 * DA, :], TN_DIMS)
        du += _dot(dv_ref[...], w_ref[2 * DA:3 * DA, :], TN_DIMS)
        du += _dot(dg5_ref[...], w_ref[3 * DA:NSEC * DA, :], TN_DIMS)
        du += _dot(df_ref[...], w_ref[NSEC * DA:DPROJ, :], TN_DIMS)
        hb = _h_block(t, x_ref, meta_ref)
        r = lax.rsqrt(jnp.mean(hb * hb, axis=-1, keepdims=True) + EPS)
        hn = hb * r
        dg_ref[...] += jnp.sum(du * hn, axis=0, keepdims=True)
        gu = du * g_ref[...]
        dh = dout_ref[...] + r * gu - hn * (r * jnp.mean(gu * hn, axis=-1, keepdims=True))
        gx_ref[...] = dh

        @pl.when(t == 0)
        def _():
            dmeta_ref[...] = dh[P0:, :]

    blk = lambda rows: pl.BlockSpec((rows, TB), lambda t: (0, t))
    return pl.pallas_call(
        body, name="inproj_bwd_x", grid=(nb,),
        in_specs=[_full_spec((DPROJ, D)), blk(DA), blk(DA), blk(DA), blk(5 * DA), blk(DF),
                  pl.BlockSpec((TB, D), lambda t: (t, 0)), _x_spec(), _full_spec((NM, D)), _full_spec((1, D))],
        out_specs=[_x_spec(), _full_spec((NM, D)), _full_spec((1, D))],
        out_shape=[jax.ShapeDtypeStruct((seq, D), F32), jax.ShapeDtypeStruct((NM, D), F32),
                   jax.ShapeDtypeStruct((1, D), F32)],
        compiler_params=_params(),
    )(w, dq_t, dk_t, dv_t, dg5_t, df_t, dout, x, meta_full, norm_g)


def _inproj_bwd_w(u, dq_t, dk_t, dv_t, dg5_t, df_t, L):
    def body(u_ref, dq_ref, dk_ref, dv_ref, dg5_ref, df_ref, dw_ref, dwf_ref):
        s = pl.program_id(0)
        u_all = u_ref[...]

        @pl.when(s < 5)
        def _():
            dw_ref[...] = _dot(dg5_ref[...], u_all)

        for step, ref in ((5, dq_ref), (6, dk_ref), (7, dv_ref)):
            @pl.when(s == step)
            def _(ref=ref):
                dw_ref[...] = _dot(ref[...], u_all)

        @pl.when(s == NSEC - 1)
        def _():
            dwf_ref[...] = _dot(df_ref[...], u_all)

    once = lambda shape: pl.BlockSpec(shape, lambda s: (0, 0), pipeline_mode=pl.Buffered(1))
    return pl.pallas_call(
        body, name="inproj_bwd_w", grid=(NSEC,),
        in_specs=[
            once((L, D)), once((DA, L)), once((DA, L)), once((DA, L)),
            pl.BlockSpec((DA, L), lambda s: (jnp.minimum(s, 4), 0)),
            once((DF, L)),
        ],
        out_specs=[pl.BlockSpec((DA, D), lambda s: (jnp.where(s < 5, s + 3, s - 5), 0)), _full_spec((DF, D))],
        out_shape=[jax.ShapeDtypeStruct((NSEC * DA, D), F32), jax.ShapeDtypeStruct((DF, D), F32)],
        compiler_params=_params(),
    )(u, dq_t, dk_t, dv_t, dg5_t, df_t)


def _adamw(w, g, m, v):
    m = ADAM_B1 * m + (1.0 - ADAM_B1) * g
    v = ADAM_B2 * v + (1.0 - ADAM_B2) * (g * g)
    m_hat = m / (1.0 - ADAM_B1 ** ADAM_STEP)
    v_hat = v / (1.0 - ADAM_B2 ** ADAM_STEP)
    delta = -ADAM_LR * (m_hat / (jnp.sqrt(v_hat) + ADAM_EPS) + ADAM_WD * w)
    return delta, m, v


def _adamw_big(own_in, land_in, own_out, land_out, w_in_t, m_in_t, v_in_t, w_out, m_out, v_out):
    cb = CB
    e_sh = D // NDEV
    in_shape = jax.ShapeDtypeStruct(w_in_t.shape, F32)
    out_shape = jax.ShapeDtypeStruct(w_out.shape, F32)

    def total(own_ref, land_ref, rows):
        g = _pick_slab(0, own_ref, land_ref, rows).astype(F32)
        for j in range(1, NDEV):
            g = g + _pick_slab(j, own_ref, land_ref, rows).astype(F32)
        return g

    def body(oi_ref, li_ref, oo_ref, lo_ref, wi_ref, mi_ref, vi_ref, wo_ref, mo_ref, vo_ref,
             gi, di, mi, vi, go, do, mo, vo):
        g = total(oi_ref, li_ref, slice(0, WSHP))[:WSH]
        d, mn, vn = _adamw(wi_ref[...], g, mi_ref[...], vi_ref[...])
        gi[...], di[...], mi[...], vi[...] = g, d, mn, vn
        g = total(oo_ref, lo_ref, slice(0, e_sh))
        d, mn, vn = _adamw(wo_ref[0], g, mo_ref[0], vo_ref[0])
        go[0], do[0], mo[0], vo[0] = g, d, mn, vn

    slab = lambda rows: pl.BlockSpec((NDEV, rows, cb), lambda i: (0, 0, i))
    ispec = pl.BlockSpec((WSH, cb), lambda i: (0, i))
    ospec = pl.BlockSpec((1, e_sh, cb), lambda i: (0, 0, i))
    return pl.pallas_call(
        body, name="adamw_big", grid=(D // cb,),
        in_specs=[slab(WSHP), slab(WSHP), slab(e_sh), slab(e_sh), ispec, ispec, ispec, ospec, ospec, ospec],
        out_specs=[ispec] * 4 + [ospec] * 4, out_shape=[in_shape] * 4 + [out_shape] * 4,
        compiler_params=_params(),
    )(own_in, land_in, own_out, land_out, w_in_t, m_in_t, v_in_t, w_out, m_out, v_out)


F0 = 3 * DA


def _unshard_w_out(own, land):
    e_sh = D // NDEV

    def body(own_ref, land_ref, wo_ref):
        for j in range(NDEV):
            wo_ref[j * e_sh:(j + 1) * e_sh, :] = _pick_slab(j, own_ref, land_ref, slice(0, e_sh), per_peer=False)

    return pl.pallas_call(
        body, name="unshard_w_out", grid=(D // CB,),
        in_specs=[pl.BlockSpec((e_sh, CB), lambda i: (0, i)), pl.BlockSpec((NDEV, e_sh, CB), lambda i: (0, 0, i))],
        out_specs=pl.BlockSpec((D, CB), lambda i: (0, i)),
        out_shape=jax.ShapeDtypeStruct((D, D), BF16),
        compiler_params=_params(),
    )(own, land)


def _unshard_w_in(w_all):
    def body(w_ref, wt_ref):
        def ref_rows(lo, hi):
            pieces, r = [], lo
            while r < hi:
                sh, off = divmod(r, WSH)
                n = min(hi - r, WSH - off)
                pieces.append(w_ref[sh, off:off + n, :])
                r += n
            return pieces

        for s in range(NSEC):
            lo = s * DA if s < 3 else s * DA + H
            wt_ref[s * DA:(s + 1) * DA, :] = jnp.concatenate(ref_rows(lo, lo + DA), axis=0)
        wt_ref[NSEC * DA:DPROJ, :] = jnp.concatenate(
            ref_rows(F0, F0 + H) + [jnp.zeros((DF - H, CB), BF16)], axis=0)

    return pl.pallas_call(
        body, name="unshard_w_in", grid=(D // CB,),
        in_specs=[pl.BlockSpec((NDEV, WSHP, CB), lambda i: (0, 0, i))],
        out_specs=pl.BlockSpec((DPROJ, CB), lambda i: (0, i)),
        out_shape=jax.ShapeDtypeStruct((DPROJ, D), BF16),
        compiler_params=_params(),
    )(w_all)


def _shard_w_in_grads(dw_main, dw_f):
    def body(dm_ref, df_ref, p_ref):
        def ref_rows(lo, hi):
            pieces, r = [], lo
            while r < hi:
                if r < F0:
                    n = min(hi, F0) - r
                    pieces.append(dm_ref[r:r + n, :])
                elif r < F0 + H:
                    n = min(hi, F0 + H) - r
                    pieces.append(df_ref[r - F0:r - F0 + n, :])
                else:
                    n = hi - r
                    pieces.append(dm_ref[r - H:r - H + n, :])
                r += n
            return pieces

        for i in range(NDEV):
            rows = jnp.concatenate(ref_rows(i * WSH, (i + 1) * WSH) + [jnp.zeros((WSHP - WSH, CB), F32)], axis=0)
            p_ref[i] = rows.astype(BF16)

    col = lambda rows: pl.BlockSpec((rows, CB), lambda i: (0, i))
    return pl.pallas_call(
        body, name="shard_w_in_grads", grid=(D // CB,),
        in_specs=[col(NSEC * DA), col(DF)],
        out_specs=pl.BlockSpec((NDEV, WSHP, CB), lambda i: (0, 0, i)),
        out_shape=jax.ShapeDtypeStruct((NDEV, WSHP, D), BF16),
        compiler_params=_params(),
    )(dw_main, dw_f)


SMALL = ("norm_g", "final_norm_g", "attn_norm_g", "conv_norm_g", "b_f", "meta", "conv_w")


def _as_rows(x):
    return jnp.concatenate([x[:, r * TB:(r + 1) * TB] for r in range(x.shape[1] // TB)], axis=0)


def _as_line(rows):
    return jnp.concatenate([rows[r:r + 1, :] for r in range(rows.shape[0])], axis=1)


def _pad_rows(x, n=8):
    return jnp.concatenate([x, jnp.zeros((n - x.shape[0], x.shape[1]), F32)], axis=0)


def _tile_rows(a, rows, lanes=TB):
    a = a.reshape(rows, lanes)
    return jnp.pad(a, ((0, -rows % 8), (0, TB - lanes)))


def _pack_small_grads(dg_norm, dg_final, dga_p, dgc_p, dcw_p, db_b, dmeta, loss):
    def body(dgn_ref, dgf_ref, dga_ref, dgc_ref, dcw_ref, db_ref, dmeta_ref, loss_ref, out_ref):
        def lane_sums(p):
            return jnp.sum(p.T, axis=0, keepdims=True)

        lane = lax.broadcasted_iota(jnp.int32, (1, TB), 1)
        b_row = jnp.where(lane == H, loss_ref[...], 0.0)
        for h in range(H):
            b_row = b_row + jnp.where(lane == h, db_ref[h:h + 1, :], 0.0)
        common = jnp.concatenate([
            _as_rows(dgn_ref[...]), _as_rows(dgf_ref[...]), _pad_rows(_as_rows(lane_sums(dga_ref[...]))),
            _pad_rows(_as_rows(lane_sums(dgc_ref[...]))), _pad_rows(b_row)], axis=0)
        dcw = [lane_sums(dcw_ref[k]) for k in range(3)]
        for j in range(NDEV):
            cw = jnp.concatenate(
                [jnp.concatenate([r[:, j * DH:(j + 1) * DH], jnp.zeros((1, TB - DH), F32)], axis=1) for r in dcw],
                axis=0)
            out_ref[j] = jnp.concatenate([common, dmeta_ref[:, j * TB:(j + 1) * TB], _pad_rows(cw)], axis=0)

    return pl.pallas_call(
        body, name="pack_small_grads", out_shape=jax.ShapeDtypeStruct((NDEV, SROWS, TB), F32),
    )(dg_norm, dg_final, dga_p, dgc_p, dcw_p, db_b, dmeta, loss)


def _adamw_small(own, land, params):
    flat = [a for n in SMALL for a in params[n]]

    def body(*refs):
        own_ref, land_ref = refs[:2]
        ins = refs[2:2 + 3 * len(SMALL)]
        outs = refs[2 + 3 * len(SMALL):]
        g = _pick_slab(0, own_ref, land_ref, slice(0, SROWS))
        for j in range(1, NDEV):
            g = g + _pick_slab(j, own_ref, land_ref, slice(0, SROWS))
        grads = dict(
            norm_g=_as_line(g[0:8]), final_norm_g=_as_line(g[8:16]), attn_norm_g=_as_line(g[16:20]),
            conv_norm_g=_as_line(g[24:28]), b_f=g[32:33, :H], meta=g[40:56], conv_w=g[56:59, :DH][None])
        for i, n in enumerate(SMALL):
            w_ref, m_ref, v_ref = ins[3 * i:3 * i + 3]
            d, mn, vn = _adamw(w_ref[...], grads[n], m_ref[...], v_ref[...])
            for o_ref, val in zip(outs[4 * i:4 * i + 4], (grads[n], d, mn, vn)):
                o_ref[...] = val
        outs[-1][...] = g[32:33, H:H + 1]

    shapes = [jax.ShapeDtypeStruct(params[n][0].shape, F32) for n in SMALL for _ in range(4)]
    res = pl.pallas_call(
        body, name="adamw_small", out_shape=shapes + [jax.ShapeDtypeStruct((1, 1), F32)],
    )(own, land, *flat)
    return {n: res[4 * i:4 * i + 4] for i, n in enumerate(SMALL)}, res[-1]


def kernel(x, meta, norm_g, w_in, b_f, conv_w, attn_norm_g, conv_norm_g, w_out, final_norm_g, loss_target, m_meta, m_norm_g, m_w_in, m_b_f, m_conv_w, m_attn_norm_g, m_conv_norm_g, m_w_out, m_final_norm_g, v_meta, v_norm_g, v_w_in, v_b_f, v_conv_w, v_attn_norm_g, v_conv_norm_g, v_w_out, v_final_norm_g):
    seq = x.shape[1]
    L = seq + TB
    assert x.shape == (1, seq, D) and L % TT == 0 and w_in.shape == (1, D, WSH)
    x2 = x[0]
    tgt = loss_target[0]

    w_in_slab = jnp.pad(w_in[0].T, ((0, WSHP - WSH), (0, 0))).astype(BF16)
    w_out_slab = w_out[0].astype(BF16)
    meta_slab = jnp.concatenate([meta, _tile_rows(conv_w[0], 3, DH)], axis=0)
    wout_flight = _split_start(w_out_slab, "gather_w_out_start", per_peer=False)
    w_all, small_all = _all_gather([w_in_slab, meta_slab], "gather_w_in")

    w_t = _unshard_w_in(w_all)
    meta_full = jnp.transpose(small_all[:, :NM, :], (1, 0, 2)).reshape(NM, D)
    conv_w_full = jnp.transpose(small_all[:, NM:NM + 3, :DH], (1, 0, 2)).reshape(3, DA)

    lane_b = lambda p: jnp.broadcast_to(p.reshape(-1, DA, 1), (p.size // DA, DA, TB))
    cw_b = lane_b(conv_w_full)
    ga_b = lane_b(attn_norm_g)[0]
    gcn_b = lane_b(conv_norm_g)[0]

    u, proj_t, f_t, ktok, vtok = _inproj_fwd(x2, meta_full, norm_g + wout_flight[4][0, 0], w_t, L)
    cq, kaug, sg = _fgate_fwd(f_t, b_f.reshape(H, 1), ktok, L)
    o_t, lse = _attn_fwd(proj_t, kaug, cq, L)
    mix_t = _gate_fwd(o_t, proj_t, cw_b, ga_b, gcn_b, L)

    w_out_own, w_out_land = _split_wait(wout_flight, mix_t, "gather_w_out_wait", per_peer=False)
    w_out_full = _unshard_w_out(w_out_own, w_out_land)
    dout, dmix_t, dw_out, loss_part, dg_final = _outproj(
        mix_t, w_out_full, x2, meta_full, final_norm_g.reshape(1, D), tgt, L)
    dwo_flight = _split_start(dw_out.reshape(NDEV, D // NDEV, D), "exchange_dw_out_start", per_peer=True)
    do_t, dd, dg5_t, dga_p, dgc_p, dcw_p = _gate_bwd(dmix_t, o_t, proj_t, cw_b, ga_b + dwo_flight[4][0, 0], gcn_b, L)
    dq_t, dk_t, dv_t, dck, dcq = _attn_bwd(proj_t, kaug, vtok, do_t, lse, dd, cq, L)
    df_t, db_f = _fgate_bwd(dcq, dck, sg, L)
    dw_main, dw_f = _inproj_bwd_w(u, dq_t, dk_t, dv_t, dg5_t, df_t, L)
    dwi_flight = _split_start(_shard_w_in_grads(dw_main, dw_f), "exchange_dw_in_start", per_peer=True)
    grad_x, dmeta, dg_norm = _inproj_bwd_x(
        w_t, dq_t, dk_t, dv_t, dg5_t, df_t, dout, x2, meta_full, norm_g + dwi_flight[4][0, 0], L)
    small_parts = _pack_small_grads(dg_norm, dg_final, dga_p, dgc_p, dcw_p, db_f, dmeta, loss_part)
    small_flight = _split_start(small_parts, "exchange_small_start", per_peer=True)
    dwo_own, dwo_land = _split_wait(dwo_flight, small_flight[4], "exchange_dw_out_wait", per_peer=True)
    dwi_own, dwi_land = _split_wait(dwi_flight, dwo_land, "exchange_dw_in_wait", per_peer=True)

    big_out = _adamw_big(dwi_own, dwi_land, dwo_own, dwo_land,
                         w_in[0].T, m_w_in[0].T, v_w_in[0].T, w_out, m_w_out, v_w_out)
    g_w_in, d_w_in, nm_w_in, nv_w_in = [a.T[None] for a in big_out[:4]]
    g_w_out, d_w_out, nm_w_out, nv_w_out = big_out[4:]
    sm_own, sm_land = _split_wait(small_flight, big_out[4], "exchange_small_wait", per_peer=True)
    line = lambda a: a.reshape(1, D)
    small, loss = _adamw_small(sm_own, sm_land, dict(
        norm_g=(norm_g, m_norm_g, v_norm_g),
        final_norm_g=(line(final_norm_g), line(m_final_norm_g), line(v_final_norm_g)),
        attn_norm_g=(attn_norm_g, m_attn_norm_g, v_attn_norm_g),
        conv_norm_g=(conv_norm_g, m_conv_norm_g, v_conv_norm_g),
        b_f=(b_f, m_b_f, v_b_f), meta=(meta, m_meta, v_meta), conv_w=(conv_w, m_conv_w, v_conv_w)))
    small["final_norm_g"] = [a.reshape(D) for a in small["final_norm_g"]]
    order = ("meta", "norm_g", "w_in", "b_f", "conv_w", "attn_norm_g", "conv_norm_g", "w_out", "final_norm_g")
    groups = []
    for k, (wi, wo) in enumerate(((g_w_in, g_w_out), (d_w_in, d_w_out), (nm_w_in, nm_w_out), (nv_w_in, nv_w_out))):
        d = dict({n: small[n][k] for n in SMALL}, w_in=wi, w_out=wo)
        groups.append([d[n] for n in order])
    return (loss[0, 0], grad_x[None], *groups[0], *groups[1], *groups[2], *groups[3])
```

```python
import jax
import jax.numpy as jnp
from jax import lax
from jax.experimental import pallas as pl
from jax.experimental.pallas import tpu as pltpu

F32 = jnp.float32
BF16 = jnp.bfloat16

D = 1024
DA = 512
H = 8
DH = 64
NM = 16
TB = 128
P0 = TB - NM
TT = 3 * TB
HG = 8
NDEV = 8
NSEC = 8
DF = 16
DPROJ = NSEC * DA + DF
WSH = 513
WSHP = 528
WROWS = WSHP + D // NDEV
SROWS = 64
EPS = 1e-6
NEG = -1e30
LOG2E = 1.4426950408889634
LN2 = 0.6931471805599453
QSCALE = DH ** -0.5 * LOG2E
KA = 128
CB = 256
VMEM_LIMIT = 56 * 1024 * 1024

ADAM_LR = 0.001
ADAM_B1 = 0.9
ADAM_B2 = 0.999
ADAM_EPS = 1e-08
ADAM_WD = 0.01
ADAM_STEP = 10

NT_DIMS = (((1,), (1,)), ((), ()))
TN_DIMS = (((0,), (0,)), ((), ()))
MESH = pl.DeviceIdType.MESH


def _params(n_axes=1, vmem=VMEM_LIMIT):
    return pltpu.CompilerParams(dimension_semantics=("arbitrary",) * n_axes, vmem_limit_bytes=vmem)


def _dot(a, b, dims=None):
    if dims is None:
        return jnp.dot(a, b, preferred_element_type=F32)
    return lax.dot_general(a, b, dims, preferred_element_type=F32)


def _my_place():
    return lax.axis_index("x"), lax.axis_index("y"), lax.axis_index("c")


def _all_gather(xs, name):
    n = len(xs)

    def body(*refs):
        x_refs, out_refs = refs[:n], refs[n:2 * n]
        send_sems, recv_sems, local_sems = refs[2 * n:]
        mx, my, mc = _my_place()

        def across(px, py, pc, axis_a):
            flip_x = pc if axis_a else 1 - pc
            return (px + flip_x) % 2, (py + 1 - flip_x) % 2, pc

        def idx(p):
            return 4 * p[0] + 2 * p[1] + p[2]

        me, sib = (mx, my, mc), (mx, my, 1 - mc)
        a_nbr, b_nbr = across(*me, True), across(*me, False)
        diag = across(*b_nbr, True)
        sib_a, sib_b = across(*sib, True), across(*sib, False)
        sib_diag = across(*sib_b, True)

        waits = []
        for t in range(n):
            out_ref = out_refs[t]

            def copy(k, block, to, src=None, out_ref=out_ref, t=t):
                return pltpu.make_async_remote_copy(
                    src_ref=out_ref.at[idx(block)] if src is None else src, dst_ref=out_ref.at[idx(block)],
                    send_sem=send_sems.at[7 * t + k], recv_sem=recv_sems.at[7 * t + k],
                    device_id=to, device_id_type=MESH)

            mine = pltpu.make_async_copy(x_refs[t], out_ref.at[idx(me)], local_sems.at[t])
            mine.start()
            started = [copy(0, me, sib, src=x_refs[t]), copy(1, me, a_nbr, src=x_refs[t]),
                       copy(2, me, b_nbr, src=x_refs[t])]
            for cp in started:
                cp.start()
            waits.append((copy, mine, started))
        relays = ((1, a_nbr, ((3, b_nbr), (4, sib))), (2, b_nbr, ((5, sib),)), (3, diag, ((6, sib),)))
        for landed, block, onward in relays:
            for copy, _, started in waits:
                copy(landed, block, me).wait_recv()
                for k, to in onward:
                    started.append(copy(k, block, to))
                    started[-1].start()
        for copy, mine, started in waits:
            for k, block in ((0, sib), (4, sib_a), (5, sib_b), (6, sib_diag)):
                copy(k, block, me).wait_recv()
            for cp in started:
                cp.wait_send()
            mine.wait()

    any_spec = pl.BlockSpec(memory_space=pl.ANY)
    return pl.pallas_call(
        body, name=name,
        out_shape=[jax.ShapeDtypeStruct((NDEV,) + x.shape, x.dtype) for x in xs],
        in_specs=[any_spec] * n, out_specs=[any_spec] * n,
        scratch_shapes=[pltpu.SemaphoreType.DMA((7 * n,)), pltpu.SemaphoreType.DMA((7 * n,)),
                        pltpu.SemaphoreType.DMA((n,))],
    )(*xs)


_HBM =pl.BlockSpec(memory_space=pltpu.HBM)
_SEM = pl.BlockSpec(memory_space=pltpu.SEMAPHORE)
_EFFECT = pltpu.SideEffectType.DATAFLOW_SIDE_EFFECTING


def _peer_of(m, place):
    mx, my, mc = place
    return ((1 - mx) if m & 4 else mx, (1 - my) if m & 2 else my, (1 - mc) if m & 1 else mc)


def _split_copies(src_ref, land_ref, send_sems, recv_sems, per_peer, incoming):
    place = _my_place()
    me = 4 * place[0] + 2 * place[1] + place[2]
    out = []
    for m in range(1, NDEV):
        px, py, pc = _peer_of(m, place)
        peer = 4 * px + 2 * py + pc
        src = (src_ref.at[me] if incoming else src_ref.at[peer]) if per_peer else src_ref
        out.append(pltpu.make_async_remote_copy(
            src_ref=src, dst_ref=land_ref.at[peer if incoming else me],
            send_sem=send_sems.at[m - 1], recv_sem=recv_sems.at[m - 1],
            device_id=(px, py, pc), device_id_type=MESH))
    return out


def _split_start(src, name, per_peer):
    slab = src.shape[1:] if per_peer else src.shape

    def body(src_ref, land_ref, send_sems, recv_sems, src_thru, land_thru, token):
        for cp in _split_copies(src_ref, land_ref, send_sems, recv_sems, per_peer, incoming=False):
            cp.start()
        token[...] = jnp.zeros_like(token)

    return pl.pallas_call(
        body, name=name,
        out_shape=(pltpu.SemaphoreType.DMA((NDEV - 1,)), pltpu.SemaphoreType.DMA((NDEV - 1,)),
                   pltpu.HBM(src.shape, src.dtype), pltpu.HBM((NDEV,) + slab, src.dtype),
                   jax.ShapeDtypeStruct((8, TB), F32)),
        in_specs=(_HBM, _HBM), out_specs=(_SEM, _SEM, _HBM, _HBM, pl.BlockSpec(memory_space=pltpu.VMEM)),
        input_output_aliases={0: 2, 1: 3},
        compiler_params=pltpu.CompilerParams(has_side_effects=_EFFECT),
    )(pltpu.with_memory_space_constraint(src, pltpu.HBM),
      pltpu.with_memory_space_constraint(lax.empty((NDEV,) + slab, src.dtype), pltpu.HBM))


def _split_wait(handles, after, name, per_peer):
    send_sems, recv_sems, src_thru, land_thru, _ = handles

    def body(src_ref, land_ref, send_sems, recv_sems, after_ref, src_out, land_out):
        for cp in _split_copies(src_ref, land_ref, send_sems, recv_sems, per_peer, incoming=False):
            cp.wait_send()
        for cp in _split_copies(src_ref, land_ref, send_sems, recv_sems, per_peer, incoming=True):
            cp.wait_recv()

    return pl.pallas_call(
        body, name=name,
        out_shape=(pltpu.HBM(src_thru.shape, src_thru.dtype), pltpu.HBM(land_thru.shape, land_thru.dtype)),
        in_specs=(_HBM, _HBM, _SEM, _SEM, pl.BlockSpec(memory_space=pl.ANY)), out_specs=(_HBM, _HBM),
        input_output_aliases={0: 0, 1: 1},
        compiler_params=pltpu.CompilerParams(has_side_effects=_EFFECT),
    )(src_thru, land_thru, send_sems, recv_sems, after)


def _pick_slab(j, own_ref, land_ref, rows, per_peer=True):
    mx, my, mc = _my_place()
    me = 4 * mx + 2 * my + mc
    own = (lambda: own_ref[j, rows, :]) if per_peer else (lambda: own_ref[rows, :])
    return lax.cond(me == j, own, lambda: land_ref[j, rows, :])


def _h_block(t, x_ref, meta_ref):
    first = jnp.concatenate([jnp.zeros((P0, D), F32), meta_ref[...]], axis=0)
    return jnp.where(t == 0, first, x_ref[...])


def _x_spec():
    return pl.BlockSpec((TB, D), lambda t: (jnp.maximum(t - 1, 0), 0))


def _x_specs3():
    return [pl.BlockSpec((TB, D), lambda j: (jnp.maximum(3 * j - 1, 0), 0)),
            pl.BlockSpec((TB, D), lambda j: (3 * j, 0)),
            pl.BlockSpec((TB, D), lambda j: (3 * j + 1, 0))]


def _h_tile(j, xa_ref, xb_ref, xc_ref, meta_ref):
    first = jnp.concatenate([jnp.zeros((P0, D), F32), meta_ref[...]], axis=0)
    return jnp.concatenate([jnp.where(j == 0, first, xa_ref[...]), xb_ref[...], xc_ref[...]], axis=0)


def _full_spec(shape):
    return pl.BlockSpec(shape, lambda *_: (0,) * len(shape))


def _sigmoid(z):
    return 1.0 / (1.0 + jnp.exp(-z))


def _lane_tiles_sum(x):
    out = x[:, :TB]
    for i in range(1, x.shape[1] // TB):
        out = out + x[:, i * TB:(i + 1) * TB]
    return out


def _inproj_fwd(x, meta_full, norm_g, w_t, L):
    nj = L // TT

    def body(xa_ref, xb_ref, xc_ref, meta_ref, g_ref, w_ref, u_ref, proj_ref, f_ref, ktok_ref, vtok_ref):
        hb = _h_tile(pl.program_id(0), xa_ref, xb_ref, xc_ref, meta_ref)
        r = lax.rsqrt(jnp.mean(hb * hb, axis=-1, keepdims=True) + EPS)
        u = (hb * r * g_ref[...]).astype(BF16)
        u_ref[...] = u
        for s in range(NSEC):
            p = _dot(u, w_ref[s * DA:(s + 1) * DA, :], NT_DIMS)
            if s == 0:
                p = p * QSCALE
            if s in (1, 2):
                tok_ref = ktok_ref if s == 1 else vtok_ref
                for h in range(H):
                    tok_ref[h] = p[:, h * DH:(h + 1) * DH].astype(BF16)
            proj_ref[s * DA:(s + 1) * DA, :] = p.T.astype(BF16)
        f_ref[...] = _dot(w_ref[NSEC * DA:DPROJ, :], u, NT_DIMS)[:H]

    return pl.pallas_call(
        body, name="inproj_fwd", grid=(nj,),
        in_specs=_x_specs3() + [_full_spec((NM, D)), _full_spec((1, D)), _full_spec((DPROJ, D))],
        out_specs=[
            pl.BlockSpec((TT, D), lambda t: (t, 0)),
            pl.BlockSpec((NSEC * DA, TT), lambda t: (0, t)),
            pl.BlockSpec((H, TT), lambda t: (0, t)),
            pl.BlockSpec((H, TT, DH), lambda t: (0, t, 0)),
            pl.BlockSpec((H, TT, DH), lambda t: (0, t, 0)),
        ],
        out_shape=[
            jax.ShapeDtypeStruct((L, D), BF16),
            jax.ShapeDtypeStruct((NSEC * DA, L), BF16),
            jax.ShapeDtypeStruct((H, L), F32),
            jax.ShapeDtypeStruct((H, L, DH), BF16),
            jax.ShapeDtypeStruct((H, L, DH), BF16),
        ],
        compiler_params=_params(),
    )(x, x, x, meta_full, norm_g, w_t)


def _split3(x):
    hi = x.astype(BF16).astype(F32)
    r = x - hi
    mid = r.astype(BF16).astype(F32)
    return hi, mid, (r - mid).astype(BF16).astype(F32)


def _bias_rows(bias):
    one = jnp.ones((1, TT), F32)
    zero = jnp.zeros((1, TT), F32)
    parts = [zero] * 3 if bias is None else list(_split3(bias))
    return jnp.concatenate([one] * 3 + parts + [zero] * (DF - 6), axis=0).astype(BF16)


def _fgate_fwd(f_t, b_col, ktok, L):
    nb = L // TB

    def body(f_ref, b_ref, ktok_ref, cq_ref, kaug_ref, sg_ref):
        z = f_ref[...] + b_ref[...]
        idx = lax.broadcasted_iota(jnp.int32, (H, L), 1)
        real = idx >= P0
        lf = jnp.where(real, jnp.minimum(z, 0.0) - jnp.log1p(jnp.exp(-jnp.abs(z))), 0.0)
        sg_ref[...] = jnp.where(real, 1.0 / (1.0 + jnp.exp(z)), 0.0)
        c = lf
        s = 1
        while s < L:
            c = c + jnp.where(idx >= s, pltpu.roll(c, s, 1), 0.0)
            s *= 2
        c = c * LOG2E
        for h in range(H):
            cq_ref[h] = c[h:h + 1, :]
        hi, mid, lo = _split3(-jnp.where(real, c, -NEG))
        lane = lax.broadcasted_iota(jnp.int32, (TB, KA), 1)
        ones = jnp.ones((3, TB), F32)
        for h in range(H):
            for b in range(nb):
                blk = slice(b * TB, (b + 1) * TB)
                cols = jnp.concatenate([
                    jnp.zeros((DH, TB), F32), hi[h:h + 1, blk], mid[h:h + 1, blk], lo[h:h + 1, blk], ones,
                    jnp.zeros((KA - DH - 6, TB), F32)], axis=0).T
                k = jnp.concatenate([ktok_ref[h, blk, :].astype(F32), jnp.zeros((TB, KA - DH), F32)], axis=1)
                kaug_ref[h, blk, :] = jnp.where(lane < DH, k, cols).astype(BF16)

    return pl.pallas_call(
        body, name="fgate_fwd",
        out_shape=[
            jax.ShapeDtypeStruct((H, 1, L), F32),
            jax.ShapeDtypeStruct((H, L, KA), BF16),
            jax.ShapeDtypeStruct((H, L), F32),
        ],
        compiler_params=pltpu.CompilerParams(vmem_limit_bytes=VMEM_LIMIT),
    )(f_t, b_col, ktok)


def _causal_mask():
    r = lax.broadcasted_iota(jnp.int32, (TT, TT), 0)
    c = lax.broadcasted_iota(jnp.int32, (TT, TT), 1)
    return r <= c


def _attn_fwd(proj_t, kaug, cq, L):
    nq = L // TT

    def body(q_ref, qn_ref, kaug_ref, v_ref, cq_ref, o_ref, lse_ref,
             qa_scr, s_scr, cmax_scr, m_scr, p_scr, alpha_scr, acc_scr):
        j = pl.program_id(0)
        rows = [slice(g * DH, (g + 1) * DH) for g in range(HG)]
        ones = jnp.ones((DF, TT), BF16)

        def load_queries(ref):
            for g in range(HG):
                qa_scr[g] = jnp.concatenate(
                    [ref[rows[g], :], _bias_rows(None), jnp.zeros((KA - DH - DF, TT), BF16)], axis=0)

        def scores(kt, masked):
            k_off = pl.multiple_of(kt * TT, TT)
            for g in range(HG):
                s = _dot(kaug_ref[g, pl.ds(k_off, TT), :], qa_scr[g])
                if masked:
                    s = jnp.where(_causal_mask(), s, NEG)
                s_scr[g] = s
                cmax_scr[g] = jnp.max(s, axis=0, keepdims=True)

        def softmax():
            for g in range(HG):
                m_old = m_scr[g]
                m_new = jnp.maximum(m_old, cmax_scr[g])
                alpha_scr[g] = jnp.exp2(m_old - m_new)
                p_scr[g] = jnp.exp2(s_scr[g] - m_new).astype(BF16)
                m_scr[g] = m_new

        def weighted_sum(kt):
            k_off = pl.multiple_of(kt * TT, TT)
            for g in range(HG):
                v1 = jnp.concatenate([v_ref[rows[g], pl.ds(k_off, TT)], ones], axis=0)
                acc_scr[g] = alpha_scr[g] * acc_scr[g] + _dot(v1, p_scr[g])

        @pl.when(j == 0)
        def _():
            load_queries(q_ref)
            scores(0, True)

        m_scr[...] = jnp.full_like(m_scr, NEG)
        acc_scr[...] = jnp.zeros_like(acc_scr)

        @pl.when(j >= 1)
        def _():
            softmax()
            scores(j - 1, False)

        def step(i, c):
            weighted_sum(j - i + 1)
            softmax()
            scores(j - i - 1, False)
            return c

        lax.fori_loop(1, j, step, 0)

        @pl.when(j >= 1)
        def _():
            weighted_sum(1)

        @pl.when(j < nq - 1)
        def _():
            softmax()
            load_queries(qn_ref)
            scores(j + 1, True)
            weighted_sum(0)

        @pl.when(j == nq - 1)
        def _():
            softmax()
            weighted_sum(0)

        for g in range(HG):
            l = acc_scr[g, DH:DH + 1, :]
            o_ref[rows[g], :] = acc_scr[g, :DH, :] * (1.0 / l)
            lse_ref[g] = m_scr[g] + jnp.log2(l) + cq_ref[g]

    assert HG == H
    return pl.pallas_call(
        body, name="attn_fwd", grid=(nq,),
        in_specs=[
            pl.BlockSpec((DA, TT), lambda j: (0, j)),
            pl.BlockSpec((DA, TT), lambda j: (0, jnp.minimum(j + 1, nq - 1))),
            pl.BlockSpec((H, L, KA), lambda j: (0, 0, 0)),
            pl.BlockSpec((DA, L), lambda j: (2, 0)),
            pl.BlockSpec((H, 1, TT), lambda j: (0, 0, j)),
        ],
        out_specs=[
            pl.BlockSpec((DA, TT), lambda j: (0, j)),
            pl.BlockSpec((H, 1, TT), lambda j: (0, 0, j)),
        ],
        out_shape=[jax.ShapeDtypeStruct((DA, L), F32), jax.ShapeDtypeStruct((H, 1, L), F32)],
        scratch_shapes=[pltpu.VMEM((HG, KA, TT), BF16), pltpu.VMEM((HG, TT, TT), F32), pltpu.VMEM((HG, 1, TT), F32),
                        pltpu.VMEM((HG, 1, TT), F32), pltpu.VMEM((HG, TT, TT), BF16), pltpu.VMEM((HG, 1, TT), F32),
                        pltpu.VMEM((HG, DH + DF, TT), F32)],
        compiler_params=_params(),
    )(proj_t, proj_t, kaug, proj_t, cq)


def _gate_group(rows, o_ref, za_ref, gb_ref, gc_ref, xc_ref, zc_ref, gcp_ref, xcp_ref, cw_ref, ga_ref, gcn_ref, first):
    n_rep = TT // TB
    f32 = lambda r: r[rows, :].astype(F32)
    o, za, gb, gc, xc, zc = o_ref[rows, :], f32(za_ref), f32(gb_ref), f32(gc_ref), f32(xc_ref), f32(zc_ref)
    a = gc * xc
    a_prev = jnp.where(first, 0.0, f32(gcp_ref) * f32(xcp_ref))
    full = jnp.concatenate([a_prev, a], axis=1)
    a1 = pltpu.roll(full, 1, 1)[:, TB:]
    a2 = pltpu.roll(full, 2, 1)[:, TB:]
    w0 = jnp.tile(cw_ref[0, rows, :], (1, n_rep))
    w1 = jnp.tile(cw_ref[1, rows, :], (1, n_rep))
    w2 = jnp.tile(cw_ref[2, rows, :], (1, n_rep))
    cv = w0 * a2 + w1 * a1 + w2 * a
    e = gb * cv
    rc = lax.rsqrt(jnp.mean(e * e, axis=0, keepdims=True) + EPS)
    ec = e * rc
    ra = lax.rsqrt(jnp.mean(o * o, axis=0, keepdims=True) + EPS)
    oa = o * ra
    g_a = jnp.tile(ga_ref[rows, :], (1, n_rep))
    g_c = jnp.tile(gcn_ref[rows, :], (1, n_rep))
    sa = _sigmoid(za)
    sc = _sigmoid(zc)
    return dict(o=o, za=za, gb=gb, gc=gc, xc=xc, zc=zc, a=a, a1=a1, a2=a2, w0=w0, w1=w1, w2=w2, cv=cv, e=e,
                rc=rc, ec=ec, ra=ra, oa=oa, g_a=g_a, g_c=g_c, sa=sa, sc=sc)


def _gate_specs(nj, rev):
    def jj(i):
        return (nj - 1 - i) if rev else i

    def sec(s):
        return pl.BlockSpec((DA, TT), lambda i: (s, jj(i)))

    def halo(s):
        return pl.BlockSpec((DA, TB), lambda i: (s, jnp.maximum(3 * jj(i) - 1, 0)))

    return [pl.BlockSpec((DA, TT), lambda i: (0, jj(i))), sec(3), sec(4), sec(5), sec(6), sec(7), halo(5), halo(6),
            _full_spec((3, DA, TB)), _full_spec((DA, TB)), _full_spec((DA, TB))]


def _gate_fwd(o_t, proj_t, cw_b, ga_b, gcn_b, L):
    nj = L // TT

    def body(o_ref, za_ref, gb_ref, gc_ref, xc_ref, zc_ref, gcp_ref, xcp_ref, cw_ref, ga_ref, gcn_ref, mix_ref):
        j = pl.program_id(0)

        def group(h, c):
            r0 = pl.multiple_of(h * DH, DH)
            g = _gate_group(pl.ds(r0, DH), o_ref, za_ref, gb_ref, gc_ref, xc_ref, zc_ref, gcp_ref, xcp_ref,
                            cw_ref, ga_ref, gcn_ref, j == 0)
            mix_ref[pl.ds(r0, DH), :] = (g["oa"] * g["g_a"] * (g["za"] * g["sa"])).astype(BF16)
            mix_ref[pl.ds(DA + r0, DH), :] = (g["ec"] * g["g_c"] * (g["zc"] * g["sc"])).astype(BF16)
            return c

        lax.fori_loop(0, H, group, 0, unroll=2)

    return pl.pallas_call(
        body, name="gate_fwd", grid=(nj,),
        in_specs=_gate_specs(nj, False),
        out_specs=pl.BlockSpec((2 * DA, TT), lambda j: (0, j)),
        out_shape=jax.ShapeDtypeStruct((2 * DA, L), BF16),
        compiler_params=_params(),
    )(o_t, proj_t, proj_t, proj_t, proj_t, proj_t, proj_t, proj_t, cw_b, ga_b, gcn_b)


def _outproj(mix_t, w_out, x, meta_full, fng, target, L):
    nj = L // TT

    def body(mix_ref, w_ref, xa_ref, xb_ref, xc_ref, meta_ref, g_ref, ta_ref, tb_ref, tc_ref,
             dout_ref, dmix_ref, dwb_ref, loss_ref, dg_ref, dw_ref):
        t = pl.program_id(0)

        @pl.when(t == 0)
        def _():
            dw_ref[...] = jnp.zeros_like(dw_ref)
            loss_ref[...] = jnp.zeros_like(loss_ref)
            dg_ref[...] = jnp.zeros_like(dg_ref)

        mix = mix_ref[...]
        o = _dot(mix, w_ref[...], TN_DIMS) + _h_tile(t, xa_ref, xb_ref, xc_ref, meta_ref)
        r = lax.rsqrt(jnp.mean(o * o, axis=-1, keepdims=True) + EPS)
        g = g_ref[...]
        orn = o * r
        tgt = jnp.concatenate([ta_ref[...], tb_ref[...], tc_ref[...]], axis=0)
        row = lax.broadcasted_iota(jnp.int32, (TT, 1), 0)
        real = jnp.where((t > 0) | (row >= TB), 1.0, 0.0)
        diff = (orn * g - tgt) * real
        loss_ref[...] += 0.5 * jnp.sum(diff * diff) * (1.0 / D)
        dy = diff * (1.0 / D)
        dg_ref[...] += jnp.sum(dy * orn, axis=0, keepdims=True)
        gy = dy * g
        dout = r * gy - orn * (r * jnp.mean(gy * orn, axis=-1, keepdims=True))
        dout_ref[...] = dout
        db = dout.astype(BF16)
        dmix_ref[...] = _dot(db, w_ref[...], NT_DIMS).T.astype(BF16)
        dw_ref[...] += _dot(mix, db)

        @pl.when(t == nj - 1)
        def _():
            dwb_ref[...] = dw_ref[...].astype(BF16)

    return pl.pallas_call(
        body, name="outproj", grid=(nj,),
        in_specs=[pl.BlockSpec((D, TT), lambda t: (0, t)), _full_spec((D, D))] + _x_specs3()
                 + [_full_spec((NM, D)), _full_spec((1, D))] + _x_specs3(),
        out_specs=[pl.BlockSpec((TT, D), lambda t: (t, 0)), pl.BlockSpec((D, TT), lambda t: (0, t)),
                   _full_spec((D, D)), _full_spec((1, 1)), _full_spec((1, D))],
        out_shape=[jax.ShapeDtypeStruct((L, D), F32), jax.ShapeDtypeStruct((D, L), BF16),
                   jax.ShapeDtypeStruct((D, D), BF16), jax.ShapeDtypeStruct((1, 1), F32),
                   jax.ShapeDtypeStruct((1, D), F32)],
        scratch_shapes=[pltpu.VMEM((D, D), F32)],
        compiler_params=_params(),
    )(mix_t, w_out, x, x, x, meta_full, fng, target, target, target)


def _gate_bwd(dmix_t, o_t, proj_t, cw_b, ga_b, gcn_b, L):
    nj = L // TT

    def body(dmix_ref, o_ref, za_ref, gb_ref, gc_ref, xc_ref, zc_ref, gcp_ref, xcp_ref, cw_ref, ga_ref, gcn_ref,
             do_ref, dd_ref, dg5_ref, dga_ref, dgc_ref, dcw_ref, carry_ref):
        i = pl.program_id(0)
        j = nj - 1 - i

        @pl.when(i == 0)
        def _():
            carry_ref[...] = jnp.zeros_like(carry_ref)
            dga_ref[...] = jnp.zeros_like(dga_ref)
            dgc_ref[...] = jnp.zeros_like(dgc_ref)
            dcw_ref[...] = jnp.zeros_like(dcw_ref)

        def group(h, c):
            r0 = pl.multiple_of(h * DH, DH)
            rows = pl.ds(r0, DH)
            sec = lambda s: pl.ds(s * DA + r0, DH)
            g = _gate_group(rows, o_ref, za_ref, gb_ref, gc_ref, xc_ref, zc_ref, gcp_ref, xcp_ref,
                            cw_ref, ga_ref, gcn_ref, j == 0)
            o, za, gb, gc, xc, zc, sa, sc = (g[n] for n in ("o", "za", "gb", "gc", "xc", "zc", "sa", "sc"))
            dya = dmix_ref[rows, :].astype(F32)
            dyc = dmix_ref[pl.ds(DA + r0, DH), :].astype(F32)

            dn = dya * (za * sa)
            dg5_ref[sec(0), :] = (dya * (g["oa"] * g["g_a"]) * (sa * (1.0 + za * (1.0 - sa)))).astype(BF16)
            dga_ref[rows, :] += _lane_tiles_sum(dn * g["oa"])
            dng = dn * g["g_a"]
            mean_a = jnp.mean(dng * g["oa"], axis=0, keepdims=True)
            do = (dng - g["oa"] * mean_a) * g["ra"]
            do_ref[rows, :] = do.astype(BF16)
            dd_ref[h] = jnp.sum(do * o, axis=0, keepdims=True)

            dnc = dyc * (zc * sc)
            dg5_ref[sec(4), :] = (dyc * (g["ec"] * g["g_c"]) * (sc * (1.0 + zc * (1.0 - sc)))).astype(BF16)
            dgc_ref[rows, :] += _lane_tiles_sum(dnc * g["ec"])
            dncg = dnc * g["g_c"]
            mean_c = jnp.mean(dncg * g["ec"], axis=0, keepdims=True)
            de = (dncg - g["ec"] * mean_c) * g["rc"]
            dg5_ref[sec(1), :] = (de * g["cv"]).astype(BF16)
            dcv = de * gb
            full = jnp.concatenate([dcv, carry_ref[rows, :]], axis=1)
            d1 = pltpu.roll(full, TT + TB - 1, 1)[:, :TT]
            d2 = pltpu.roll(full, TT + TB - 2, 1)[:, :TT]
            carry_ref[rows, :] = dcv[:, :TB]
            da = g["w2"] * dcv + g["w1"] * d1 + g["w0"] * d2
            dg5_ref[sec(2), :] = (da * xc).astype(BF16)
            dg5_ref[sec(3), :] = (da * gc).astype(BF16)
            dcw_ref[0, rows, :] += _lane_tiles_sum(dcv * g["a2"])
            dcw_ref[1, rows, :] += _lane_tiles_sum(dcv * g["a1"])
            dcw_ref[2, rows, :] += _lane_tiles_sum(dcv * g["a"])
            return c

        lax.fori_loop(0, H, group, 0, unroll=2)

    rj = lambda i: nj - 1 - i
    return pl.pallas_call(
        body, name="gate_bwd", grid=(nj,),
        in_specs=[pl.BlockSpec((2 * DA, TT), lambda i: (0, rj(i)))] + _gate_specs(nj, True),
        out_specs=[
            pl.BlockSpec((DA, TT), lambda i: (0, rj(i))),
            pl.BlockSpec((H, 1, TT), lambda i: (0, 0, rj(i))),
            pl.BlockSpec((5 * DA, TT), lambda i: (0, rj(i))),
            _full_spec((DA, TB)), _full_spec((DA, TB)), _full_spec((3, DA, TB)),
        ],
        out_shape=[
            jax.ShapeDtypeStruct((DA, L), BF16),
            jax.ShapeDtypeStruct((H, 1, L), F32),
            jax.ShapeDtypeStruct((5 * DA, L), BF16),
            jax.ShapeDtypeStruct((DA, TB), F32),
            jax.ShapeDtypeStruct((DA, TB), F32),
            jax.ShapeDtypeStruct((3, DA, TB), F32),
        ],
        scratch_shapes=[pltpu.VMEM((DA, TB), F32)],
        compiler_params=_params(),
    )(dmix_t, o_t, proj_t, proj_t, proj_t, proj_t, proj_t, proj_t, proj_t, cw_b, ga_b, gcn_b)


def _attn_bwd(proj_t, kaug, vtok, do_t, lse, dd, cq, L):
    nk = L // TT

    def body(q_ref, kaug_ref, vtok_ref, kaug_next, vtok_next, kt_ref, do_ref, lse_ref, dd_ref, cq_ref,
             dq_ref, dk_ref, dv_ref, dck_ref, dcq_ref, dq_acc, kt1_scr, s_scr, dp_scr, dv_scr, dk_scr):
        i = pl.program_id(0)
        rows = [slice(g * DH, (g + 1) * DH) for g in range(HG)]
        ones = jnp.ones((DF, TT), BF16)
        zpad = jnp.zeros((KA - DH - DF, TT), BF16)
        for g in range(HG):
            kt1_scr[g] = jnp.concatenate([kt_ref[rows[g], :], ones], axis=0)
        dv_scr[...] = jnp.zeros_like(dv_scr)
        dk_scr[...] = jnp.zeros_like(dk_scr)

        def q_rows(g, q_off):
            bias = cq_ref[g, :, pl.ds(q_off, TT)] - lse_ref[g, :, pl.ds(q_off, TT)]
            return jnp.concatenate([q_ref[rows[g], pl.ds(q_off, TT)], _bias_rows(bias)], axis=0)

        def scores(jq, masked, k_ref=kaug_ref, v_ref=vtok_ref):
            q_off = pl.multiple_of(jq * TT, TT)
            for g in range(HG):
                s = _dot(k_ref[g], jnp.concatenate([q_rows(g, q_off), zpad], axis=0))
                if masked:
                    s = jnp.where(_causal_mask(), s, NEG)
                s_scr[g] = s
                dp_scr[g] = _dot(v_ref[g], do_ref[rows[g], pl.ds(q_off, TT)])

        def grads(jq):
            q_off = pl.multiple_of(jq * TT, TT)
            for g in range(HG):
                p = jnp.exp2(s_scr[g])
                ds = (p * (dp_scr[g] - dd_ref[g, :, pl.ds(q_off, TT)])).astype(BF16)
                do1 = jnp.concatenate([do_ref[rows[g], pl.ds(q_off, TT)], jnp.zeros((KA - DH, TT), BF16)], axis=0)
                q1 = jnp.concatenate([q_rows(g, q_off), zpad], axis=0)
                dv_scr[g] += _dot(p.astype(BF16), do1, NT_DIMS)
                dk_scr[g] += _dot(ds, q1, NT_DIMS)
                dq_acc[g, :, pl.ds(q_off, TT)] += _dot(kt1_scr[g], ds)

        @pl.when(i == 0)
        def _():
            dq_acc[...] = jnp.zeros_like(dq_acc)
            scores(0, True)

        def step(jq, c):
            grads(jq)
            scores(jq + 1, False)
            return c

        lax.fori_loop(i, nk - 1, step, 0)

        @pl.when(i < nk - 1)
        def _():
            grads(nk - 1)
            scores(i + 1, True, kaug_next, vtok_next)

        @pl.when(i == nk - 1)
        def _():
            grads(nk - 1)

        for g in range(HG):
            dv_ref[rows[g], :] = dv_scr[g].T[:DH, :].astype(BF16)
            dk_t = dk_scr[g].T
            dk_ref[rows[g], :] = (dk_t[:DH, :] * LN2).astype(BF16)
            dck_ref[g] = dk_t[DH:DH + 1, :]

        @pl.when(i == nk - 1)
        def _():
            for g in range(HG):
                dq_ref[rows[g], :] = (dq_acc[g, :DH, :] * (DH ** -0.5)).astype(BF16)
                dcq_ref[g] = dq_acc[g, DH:DH + 1, :]

    assert HG == H
    head = lambda i: (0, 0)
    row = lambda i: (0, 0, 0)
    nxt = lambda i: (0, jnp.minimum(i + 1, nk - 1), 0)
    return pl.pallas_call(
        body, name="attn_bwd", grid=(nk,),
        in_specs=[
            pl.BlockSpec((DA, L), head),
            pl.BlockSpec((H, TT, KA), lambda i: (0, i, 0)),
            pl.BlockSpec((H, TT, DH), lambda i: (0, i, 0)),
            pl.BlockSpec((H, TT, KA), nxt), pl.BlockSpec((H, TT, DH), nxt),
            pl.BlockSpec((DA, TT), lambda i: (1, i)),
            pl.BlockSpec((DA, L), head),
            pl.BlockSpec((H, 1, L), row), pl.BlockSpec((H, 1, L), row), pl.BlockSpec((H, 1, L), row),
        ],
        out_specs=[
            pl.BlockSpec((DA, L), head),
            pl.BlockSpec((DA, TT), lambda i: (0, i)),
            pl.BlockSpec((DA, TT), lambda i: (0, i)),
            pl.BlockSpec((H, 1, TT), lambda i: (0, 0, i)),
            pl.BlockSpec((H, 1, L), row),
        ],
        out_shape=[jax.ShapeDtypeStruct((DA, L), BF16), jax.ShapeDtypeStruct((DA, L), BF16),
                   jax.ShapeDtypeStruct((DA, L), BF16), jax.ShapeDtypeStruct((H, 1, L), F32),
                   jax.ShapeDtypeStruct((H, 1, L), F32)],
        scratch_shapes=[
            pltpu.VMEM((HG, DH + DF, L), F32),
            pltpu.VMEM((HG, DH + DF, TT), BF16),
            pltpu.VMEM((HG, TT, TT), F32), pltpu.VMEM((HG, TT, TT), F32),
            pltpu.VMEM((HG, TT, KA), F32), pltpu.VMEM((HG, TT, KA), F32)],
        compiler_params=_params(),
    )(proj_t, kaug, vtok, kaug, vtok, proj_t, do_t, lse, dd, cq)


def _fgate_bwd(dcq, dck, sg, L):
    def body(dcq_ref, dck_ref, sg_ref, df_ref, db_ref):
        dc = jnp.concatenate([dcq_ref[h] - dck_ref[h] for h in range(H)], axis=0)
        idx = lax.broadcasted_iota(jnp.int32, (H, L), 1)
        r = dc
        s = 1
        while s < L:
            r = r + jnp.where(idx + s < L, pltpu.roll(r, L - s, 1), 0.0)
            s *= 2
        df = r * sg_ref[...]
        db_ref[...] = jnp.broadcast_to(jnp.sum(df, axis=1, keepdims=True), (H, TB))
        df_ref[...] = jnp.concatenate([df, jnp.zeros((DF - H, L), F32)], axis=0).astype(BF16)

    return pl.pallas_call(
        body, name="fgate_bwd",
        out_shape=[jax.ShapeDtypeStruct((DF, L), BF16), jax.ShapeDtypeStruct((H, TB), F32)],
        compiler_params=pltpu.CompilerParams(vmem_limit_bytes=VMEM_LIMIT),
    )(dcq, dck, sg)


def _inproj_bwd_x(w, dq_t, dk_t, dv_t, dg5_t, df_t, dout, x, meta_full, norm_g, L):
    nb = L // TB
    seq = x.shape[0]

    def body(w_ref, dq_ref, dk_ref, dv_ref, dg5_ref, df_ref, dout_ref, x_ref, meta_ref, g_ref,
             gx_ref, dmeta_ref, dg_ref):
        t = pl.program_id(0)

        @pl.when(t == 0)
        def _():
            dg_ref[...] = jnp.zeros_like(dg_ref)

        du = _dot(dq_ref[...], w_ref[0:DA, :], TN_DIMS)
        du += _dot(dk_ref[...], w_ref[DA:2 * DA, :], TN_DIMS)
        du += _dot(dv_ref[...], w_ref[2 * DA:3 * DA, :], TN_DIMS)
        du += _dot(dg5_ref[...], w_ref[3 * DA:NSEC * DA, :], TN_DIMS)
        du += _dot(df_ref[...], w_ref[NSEC * DA:DPROJ, :], TN_DIMS)
        hb = _h_block(t, x_ref, meta_ref)
        r = lax.rsqrt(jnp.mean(hb * hb, axis=-1, keepdims=True) + EPS)
        hn = hb * r
        dg_ref[...] += jnp.sum(du * hn, axis=0, keepdims=True)
        gu = du * g_ref[...]
        dh = dout_ref[...] + r * gu - hn * (r * jnp.mean(gu * hn, axis=-1, keepdims=True))
        gx_ref[...] = dh

        @pl.when(t == 0)
        def _():
            dmeta_ref[...] = dh[P0:, :]

    blk = lambda rows: pl.BlockSpec((rows, TB), lambda t: (0, t))
    return pl.pallas_call(
        body, name="inproj_bwd_x", grid=(nb,),
        in_specs=[_full_spec((DPROJ, D)), blk(DA), blk(DA), blk(DA), blk(5 * DA), blk(DF),
                  pl.BlockSpec((TB, D), lambda t: (t, 0)), _x_spec(), _full_spec((NM, D)), _full_spec((1, D))],
        out_specs=[_x_spec(), _full_spec((NM, D)), _full_spec((1, D))],
        out_shape=[jax.ShapeDtypeStruct((seq, D), F32), jax.ShapeDtypeStruct((NM, D), F32),
                   jax.ShapeDtypeStruct((1, D), F32)],
        compiler_params=_params(),
    )(w, dq_t, dk_t, dv_t, dg5_t, df_t, dout, x, meta_full, norm_g)


def _inproj_bwd_w(u, dq_t, dk_t, dv_t, dg5_t, df_t, L):
    def body(u_ref, dq_ref, dk_ref, dv_ref, dg5_ref, df_ref, dw_ref, dwf_ref):
        s = pl.program_id(0)
        u_all = u_ref[...]

        @pl.when(s < 5)
        def _():
            dw_ref[...] = _dot(dg5_ref[...], u_all)

        for step, ref in ((5, dq_ref), (6, dk_ref), (7, dv_ref)):
            @pl.when(s == step)
            def _(ref=ref):
                dw_ref[...] = _dot(ref[...], u_all)

        @pl.when(s == NSEC - 1)
        def _():
            dwf_ref[...] = _dot(df_ref[...], u_all)

    once = lambda shape: pl.BlockSpec(shape, lambda s: (0, 0), pipeline_mode=pl.Buffered(1))
    return pl.pallas_call(
        body, name="inproj_bwd_w", grid=(NSEC,),
        in_specs=[
            once((L, D)), once((DA, L)), once((DA, L)), once((DA, L)),
            pl.BlockSpec((DA, L), lambda s: (jnp.minimum(s, 4), 0)),
            once((DF, L)),
        ],
        out_specs=[pl.BlockSpec((DA, D), lambda s: (jnp.where(s < 5, s + 3, s - 5), 0)), _full_spec((DF, D))],
        out_shape=[jax.ShapeDtypeStruct((NSEC * DA, D), F32), jax.ShapeDtypeStruct((DF, D), F32)],
        compiler_params=_params(),
    )(u, dq_t, dk_t, dv_t, dg5_t, df_t)


def _adamw(w, g, m, v):
    m = ADAM_B1 * m + (1.0 - ADAM_B1) * g
    v = ADAM_B2 * v + (1.0 - ADAM_B2) * (g * g)
    m_hat = m / (1.0 - ADAM_B1 ** ADAM_STEP)
    v_hat = v / (1.0 - ADAM_B2 ** ADAM_STEP)
    delta = -ADAM_LR * (m_hat / (jnp.sqrt(v_hat) + ADAM_EPS) + ADAM_WD * w)
    return delta, m, v


def _adamw_big(own_in, land_in, own_out, land_out, w_in_t, m_in_t, v_in_t, w_out, m_out, v_out):
    cb = CB
    e_sh = D // NDEV
    in_shape = jax.ShapeDtypeStruct(w_in_t.shape, F32)
    out_shape = jax.ShapeDtypeStruct(w_out.shape, F32)

    def total(own_ref, land_ref, rows):
        g = _pick_slab(0, own_ref, land_ref, rows).astype(F32)
        for j in range(1, NDEV):
            g = g + _pick_slab(j, own_ref, land_ref, rows).astype(F32)
        return g

    def body(oi_ref, li_ref, oo_ref, lo_ref, wi_ref, mi_ref, vi_ref, wo_ref, mo_ref, vo_ref,
             gi, di, mi, vi, go, do, mo, vo):
        g = total(oi_ref, li_ref, slice(0, WSHP))[:WSH]
        d, mn, vn = _adamw(wi_ref[...], g, mi_ref[...], vi_ref[...])
        gi[...], di[...], mi[...], vi[...] = g, d, mn, vn
        g = total(oo_ref, lo_ref, slice(0, e_sh))
        d, mn, vn = _adamw(wo_ref[0], g, mo_ref[0], vo_ref[0])
        go[0], do[0], mo[0], vo[0] = g, d, mn, vn

    slab = lambda rows: pl.BlockSpec((NDEV, rows, cb), lambda i: (0, 0, i))
    ispec = pl.BlockSpec((WSH, cb), lambda i: (0, i))
    ospec = pl.BlockSpec((1, e_sh, cb), lambda i: (0, 0, i))
    return pl.pallas_call(
        body, name="adamw_big", grid=(D // cb,),
        in_specs=[slab(WSHP), slab(WSHP), slab(e_sh), slab(e_sh), ispec, ispec, ispec, ospec, ospec, ospec],
        out_specs=[ispec] * 4 + [ospec] * 4, out_shape=[in_shape] * 4 + [out_shape] * 4,
        compiler_params=_params(),
    )(own_in, land_in, own_out, land_out, w_in_t, m_in_t, v_in_t, w_out, m_out, v_out)


F0 = 3 * DA


def _unshard_w_out(own, land):
    e_sh = D // NDEV

    def body(own_ref, land_ref, wo_ref):
        for j in range(NDEV):
            wo_ref[j * e_sh:(j + 1) * e_sh, :] = _pick_slab(j, own_ref, land_ref, slice(0, e_sh), per_peer=False)

    return pl.pallas_call(
        body, name="unshard_w_out", grid=(D // CB,),
        in_specs=[pl.BlockSpec((e_sh, CB), lambda i: (0, i)), pl.BlockSpec((NDEV, e_sh, CB), lambda i: (0, 0, i))],
        out_specs=pl.BlockSpec((D, CB), lambda i: (0, i)),
        out_shape=jax.ShapeDtypeStruct((D, D), BF16),
        compiler_params=_params(),
    )(own, land)


def _unshard_w_in(w_all):
    def body(w_ref, wt_ref):
        def ref_rows(lo, hi):
            pieces, r = [], lo
            while r < hi:
                sh, off = divmod(r, WSH)
                n = min(hi - r, WSH - off)
                pieces.append(w_ref[sh, off:off + n, :])
                r += n
            return pieces

        for s in range(NSEC):
            lo = s * DA if s < 3 else s * DA + H
            wt_ref[s * DA:(s + 1) * DA, :] = jnp.concatenate(ref_rows(lo, lo + DA), axis=0)
        wt_ref[NSEC * DA:DPROJ, :] = jnp.concatenate(
            ref_rows(F0, F0 + H) + [jnp.zeros((DF - H, CB), BF16)], axis=0)

    return pl.pallas_call(
        body, name="unshard_w_in", grid=(D // CB,),
        in_specs=[pl.BlockSpec((NDEV, WSHP, CB), lambda i: (0, 0, i))],
        out_specs=pl.BlockSpec((DPROJ, CB), lambda i: (0, i)),
        out_shape=jax.ShapeDtypeStruct((DPROJ, D), BF16),
        compiler_params=_params(),
    )(w_all)


def _shard_w_in_grads(dw_main, dw_f):
    def body(dm_ref, df_ref, p_ref):
        def ref_rows(lo, hi):
            pieces, r = [], lo
            while r < hi:
                if r < F0:
                    n = min(hi, F0) - r
                    pieces.append(dm_ref[r:r + n, :])
                elif r < F0 + H:
                    n = min(hi, F0 + H) - r
                    pieces.append(df_ref[r - F0:r - F0 + n, :])
                else:
                    n = hi - r
                    pieces.append(dm_ref[r - H:r - H + n, :])
                r += n
            return pieces

        for i in range(NDEV):
            rows = jnp.concatenate(ref_rows(i * WSH, (i + 1) * WSH) + [jnp.zeros((WSHP - WSH, CB), F32)], axis=0)
            p_ref[i] = rows.astype(BF16)

    col = lambda rows: pl.BlockSpec((rows, CB), lambda i: (0, i))
    return pl.pallas_call(
        body, name="shard_w_in_grads", grid=(D // CB,),
        in_specs=[col(NSEC * DA), col(DF)],
        out_specs=pl.BlockSpec((NDEV, WSHP, CB), lambda i: (0, 0, i)),
        out_shape=jax.ShapeDtypeStruct((NDEV, WSHP, D), BF16),
        compiler_params=_params(),
    )(dw_main, dw_f)


SMALL = ("norm_g", "final_norm_g", "attn_norm_g", "conv_norm_g", "b_f", "meta", "conv_w")


def _as_rows(x):
    return jnp.concatenate([x[:, r * TB:(r + 1) * TB] for r in range(x.shape[1] // TB)], axis=0)


def _as_line(rows):
    return jnp.concatenate([rows[r:r + 1, :] for r in range(rows.shape[0])], axis=1)


def _pad_rows(x, n=8):
    return jnp.concatenate([x, jnp.zeros((n - x.shape[0], x.shape[1]), F32)], axis=0)


def _tile_rows(a, rows, lanes=TB):
    a = a.reshape(rows, lanes)
    return jnp.pad(a, ((0, -rows % 8), (0, TB - lanes)))


def _pack_small_grads(dg_norm, dg_final, dga_p, dgc_p, dcw_p, db_b, dmeta, loss):
    def body(dgn_ref, dgf_ref, dga_ref, dgc_ref, dcw_ref, db_ref, dmeta_ref, loss_ref, out_ref):
        def lane_sums(p):
            return jnp.sum(p.T, axis=0, keepdims=True)

        lane = lax.broadcasted_iota(jnp.int32, (1, TB), 1)
        b_row = jnp.where(lane == H, loss_ref[...], 0.0)
        for h in range(H):
            b_row = b_row + jnp.where(lane == h, db_ref[h:h + 1, :], 0.0)
        common = jnp.concatenate([
            _as_rows(dgn_ref[...]), _as_rows(dgf_ref[...]), _pad_rows(_as_rows(lane_sums(dga_ref[...]))),
            _pad_rows(_as_rows(lane_sums(dgc_ref[...]))), _pad_rows(b_row)], axis=0)
        dcw = [lane_sums(dcw_ref[k]) for k in range(3)]
        for j in range(NDEV):
            cw = jnp.concatenate(
                [jnp.concatenate([r[:, j * DH:(j + 1) * DH], jnp.zeros((1, TB - DH), F32)], axis=1) for r in dcw],
                axis=0)
            out_ref[j] = jnp.concatenate([common, dmeta_ref[:, j * TB:(j + 1) * TB], _pad_rows(cw)], axis=0)

    return pl.pallas_call(
        body, name="pack_small_grads", out_shape=jax.ShapeDtypeStruct((NDEV, SROWS, TB), F32),
    )(dg_norm, dg_final, dga_p, dgc_p, dcw_p, db_b, dmeta, loss)


def _adamw_small(own, land, params):
    flat = [a for n in SMALL for a in params[n]]

    def body(*refs):
        own_ref, land_ref = refs[:2]
        ins = refs[2:2 + 3 * len(SMALL)]
        outs = refs[2 + 3 * len(SMALL):]
        g = _pick_slab(0, own_ref, land_ref, slice(0, SROWS))
        for j in range(1, NDEV):
            g = g + _pick_slab(j, own_ref, land_ref, slice(0, SROWS))
        grads = dict(
            norm_g=_as_line(g[0:8]), final_norm_g=_as_line(g[8:16]), attn_norm_g=_as_line(g[16:20]),
            conv_norm_g=_as_line(g[24:28]), b_f=g[32:33, :H], meta=g[40:56], conv_w=g[56:59, :DH][None])
        for i, n in enumerate(SMALL):
            w_ref, m_ref, v_ref = ins[3 * i:3 * i + 3]
            d, mn, vn = _adamw(w_ref[...], grads[n], m_ref[...], v_ref[...])
            for o_ref, val in zip(outs[4 * i:4 * i + 4], (grads[n], d, mn, vn)):
                o_ref[...] = val
        outs[-1][...] = g[32:33, H:H + 1]

    shapes = [jax.ShapeDtypeStruct(params[n][0].shape, F32) for n in SMALL for _ in range(4)]
    res = pl.pallas_call(
        body, name="adamw_small", out_shape=shapes + [jax.ShapeDtypeStruct((1, 1), F32)],
    )(own, land, *flat)
    return {n: res[4 * i:4 * i + 4] for i, n in enumerate(SMALL)}, res[-1]


def kernel(x, meta, norm_g, w_in, b_f, conv_w, attn_norm_g, conv_norm_g, w_out, final_norm_g, loss_target, m_meta, m_norm_g, m_w_in, m_b_f, m_conv_w, m_attn_norm_g, m_conv_norm_g, m_w_out, m_final_norm_g, v_meta, v_norm_g, v_w_in, v_b_f, v_conv_w, v_attn_norm_g, v_conv_norm_g, v_w_out, v_final_norm_g):
    seq = x.shape[1]
    L = seq + TB
    assert x.shape == (1, seq, D) and L % TT == 0 and w_in.shape == (1, D, WSH)
    x2 = x[0]
    tgt = loss_target[0]

    w_in_slab = jnp.pad(w_in[0].T, ((0, WSHP - WSH), (0, 0))).astype(BF16)
    w_out_slab = w_out[0].astype(BF16)
    meta_slab = jnp.concatenate([meta, _tile_rows(conv_w[0], 3, DH)], axis=0)
    wout_flight = _split_start(w_out_slab, "gather_w_out_start", per_peer=False)
    w_all, small_all = _all_gather([w_in_slab, meta_slab], "gather_w_in")

    w_t = _unshard_w_in(w_all)
    meta_full = jnp.transpose(small_all[:, :NM, :], (1, 0, 2)).reshape(NM, D)
    conv_w_full = jnp.transpose(small_all[:, NM:NM + 3, :DH], (1, 0, 2)).reshape(3, DA)

    lane_b = lambda p: jnp.broadcast_to(p.reshape(-1, DA, 1), (p.size // DA, DA, TB))
    cw_b = lane_b(conv_w_full)
    ga_b = lane_b(attn_norm_g)[0]
    gcn_b = lane_b(conv_norm_g)[0]

    u, proj_t, f_t, ktok, vtok = _inproj_fwd(x2, meta_full, norm_g + wout_flight[4][0, 0], w_t, L)
    cq, kaug, sg = _fgate_fwd(f_t, b_f.reshape(H, 1), ktok, L)
    o_t, lse = _attn_fwd(proj_t, kaug, cq, L)
    mix_t = _gate_fwd(o_t, proj_t, cw_b, ga_b, gcn_b, L)

    w_out_own, w_out_land = _split_wait(wout_flight, mix_t, "gather_w_out_wait", per_peer=False)
    w_out_full = _unshard_w_out(w_out_own, w_out_land)
    dout, dmix_t, dw_out, loss_part, dg_final = _outproj(
        mix_t, w_out_full, x2, meta_full, final_norm_g.reshape(1, D), tgt, L)
    dwo_flight = _split_start(dw_out.reshape(NDEV, D // NDEV, D), "exchange_dw_out_start", per_peer=True)
    do_t, dd, dg5_t, dga_p, dgc_p, dcw_p = _gate_bwd(dmix_t, o_t, proj_t, cw_b, ga_b + dwo_flight[4][0, 0], gcn_b, L)
    dq_t, dk_t, dv_t, dck, dcq = _attn_bwd(proj_t, kaug, vtok, do_t, lse, dd, cq, L)
    df_t, db_f = _fgate_bwd(dcq, dck, sg, L)
    dw_main, dw_f = _inproj_bwd_w(u, dq_t, dk_t, dv_t, dg5_t, df_t, L)
    dwi_flight = _split_start(_shard_w_in_grads(dw_main, dw_f), "exchange_dw_in_start", per_peer=True)
    grad_x, dmeta, dg_norm = _inproj_bwd_x(
        w_t, dq_t, dk_t, dv_t, dg5_t, df_t, dout, x2, meta_full, norm_g + dwi_flight[4][0, 0], L)
    small_parts = _pack_small_grads(dg_norm, dg_final, dga_p, dgc_p, dcw_p, db_f, dmeta, loss_part)
    small_flight = _split_start(small_parts, "exchange_small_start", per_peer=True)
    dwo_own, dwo_land = _split_wait(dwo_flight, small_flight[4], "exchange_dw_out_wait", per_peer=True)
    dwi_own, dwi_land = _split_wait(dwi_flight, dwo_land, "exchange_dw_in_wait", per_peer=True)

    big_out = _adamw_big(dwi_own, dwi_land, dwo_own, dwo_land,
                         w_in[0].T, m_w_in[0].T, v_w_in[0].T, w_out, m_w_out, v_w_out)
    g_w_in, d_w_in, nm_w_in, nv_w_in = [a.T[None] for a in big_out[:4]]
    g_w_out, d_w_out, nm_w_out, nv_w_out = big_out[4:]
    sm_own, sm_land = _split_wait(small_flight, big_out[4], "exchange_small_wait", per_peer=True)
    line = lambda a: a.reshape(1, D)
    small, loss = _adamw_small(sm_own, sm_land, dict(
        norm_g=(norm_g, m_norm_g, v_norm_g),
        final_norm_g=(line(final_norm_g), line(m_final_norm_g), line(v_final_norm_g)),
        attn_norm_g=(attn_norm_g, m_attn_norm_g, v_attn_norm_g),
        conv_norm_g=(conv_norm_g, m_conv_norm_g, v_conv_norm_g),
        b_f=(b_f, m_b_f, v_b_f), meta=(meta, m_meta, v_meta), conv_w=(conv_w, m_conv_w, v_conv_w)))
    small["final_norm_g"] = [a.reshape(D) for a in small["final_norm_g"]]
    order = ("meta", "norm_g", "w_in", "b_f", "conv_w", "attn_norm_g", "conv_norm_g", "w_out", "final_norm_g")
    groups = []
    for k, (wi, wo) in enumerate(((g_w_in, g_w_out), (d_w_in, d_w_out), (nm_w_in, nm_w_out), (nv_w_in, nv_w_out))):
        d = dict({n: small[n][k] for n in SMALL}, w_in=wi, w_out=wo)
        groups.append([d[n] for n in order])
    return (loss[0, 0], grad_x[None], *groups[0], *groups[1], *groups[2], *groups[3])
```

```python
import jax
import jax.numpy as jnp
from jax import lax
from jax.experimental import pallas as pl
from jax.experimental.pallas import tpu as pltpu

F32 = jnp.float32
BF16 = jnp.bfloat16

D = 1024
DA = 512
H = 8
DH = 64
NM = 16
TB = 128
P0 = TB - NM
TT = 3 * TB
HG = 8
NDEV = 8
NSEC = 8
DF = 16
DPROJ = NSEC * DA + DF
WSH = 513
WSHP = 528
WROWS = WSHP + D // NDEV
SROWS = 64
EPS = 1e-6
NEG = -1e30
LOG2E = 1.4426950408889634
LN2 = 0.6931471805599453
QSCALE = DH ** -0.5 * LOG2E
KA = 128
CB = 256
VMEM_LIMIT = 56 * 1024 * 1024

ADAM_LR = 0.001
ADAM_B1 = 0.9
ADAM_B2 = 0.999
ADAM_EPS = 1e-08
ADAM_WD = 0.01
ADAM_STEP = 10

NT_DIMS = (((1,), (1,)), ((), ()))
TN_DIMS = (((0,), (0,)), ((), ()))
MESH = pl.DeviceIdType.MESH


def _params(n_axes=1, vmem=VMEM_LIMIT):
    return pltpu.CompilerParams(dimension_semantics=("arbitrary",) * n_axes, vmem_limit_bytes=vmem)


def _dot(a, b, dims=None):
    if dims is None:
        return jnp.dot(a, b, preferred_element_type=F32)
    return lax.dot_general(a, b, dims, preferred_element_type=F32)


def _my_place():
    return lax.axis_index("x"), lax.axis_index("y"), lax.axis_index("c")


def _all_gather(xs, name):
    n = len(xs)

    def body(*refs):
        x_refs, out_refs = refs[:n], refs[n:2 * n]
        send_sems, recv_sems, local_sems = refs[2 * n:]
        mx, my, mc = _my_place()

        def across(px, py, pc, axis_a):
            flip_x = pc if axis_a else 1 - pc
            return (px + flip_x) % 2, (py + 1 - flip_x) % 2, pc

        def idx(p):
            return 4 * p[0] + 2 * p[1] + p[2]

        me, sib = (mx, my, mc), (mx, my, 1 - mc)
        a_nbr, b_nbr = across(*me, True), across(*me, False)
        diag = across(*b_nbr, True)
        sib_a, sib_b = across(*sib, True), across(*sib, False)
        sib_diag = across(*sib_b, True)

        waits = []
        for t in range(n):
            out_ref = out_refs[t]

            def copy(k, block, to, src=None, out_ref=out_ref, t=t):
                return pltpu.make_async_remote_copy(
                    src_ref=out_ref.at[idx(block)] if src is None else src, dst_ref=out_ref.at[idx(block)],
                    send_sem=send_sems.at[7 * t + k], recv_sem=recv_sems.at[7 * t + k],
                    device_id=to, device_id_type=MESH)

            mine = pltpu.make_async_copy(x_refs[t], out_ref.at[idx(me)], local_sems.at[t])
            mine.start()
            started = [copy(0, me, sib, src=x_refs[t]), copy(1, me, a_nbr, src=x_refs[t]),
                       copy(2, me, b_nbr, src=x_refs[t])]
            for cp in started:
                cp.start()
            waits.append((copy, mine, started))
        relays = ((1, a_nbr, ((3, b_nbr), (4, sib))), (2, b_nbr, ((5, sib),)), (3, diag, ((6, sib),)))
        for landed, block, onward in relays:
            for copy, _, started in waits:
                copy(landed, block, me).wait_recv()
                for k, to in onward:
                    started.append(copy(k, block, to))
                    started[-1].start()
        for copy, mine, started in waits:
            for k, block in ((0, sib), (4, sib_a), (5, sib_b), (6, sib_diag)):
                copy(k, block, me).wait_recv()
            for cp in started:
                cp.wait_send()
            mine.wait()

    any_spec = pl.BlockSpec(memory_space=pl.ANY)
    return pl.pallas_call(
        body, name=name,
        out_shape=[jax.ShapeDtypeStruct((NDEV,) + x.shape, x.dtype) for x in xs],
        in_specs=[any_spec] * n, out_specs=[any_spec] * n,
        scratch_shapes=[pltpu.SemaphoreType.DMA((7 * n,)), pltpu.SemaphoreType.DMA((7 * n,)),
                        pltpu.SemaphoreType.DMA((n,))],
    )(*xs)


_HBM =pl.BlockSpec(memory_space=pltpu.HBM)
_SEM = pl.BlockSpec(memory_space=pltpu.SEMAPHORE)
_EFFECT = pltpu.SideEffectType.DATAFLOW_SIDE_EFFECTING


def _peer_of(m, place):
    mx, my, mc = place
    return ((1 - mx) if m & 4 else mx, (1 - my) if m & 2 else my, (1 - mc) if m & 1 else mc)


def _split_copies(src_ref, land_ref, send_sems, recv_sems, per_peer, incoming):
    place = _my_place()
    me = 4 * place[0] + 2 * place[1] + place[2]
    out = []
    for m in range(1, NDEV):
        px, py, pc = _peer_of(m, place)
        peer = 4 * px + 2 * py + pc
        src = (src_ref.at[me] if incoming else src_ref.at[peer]) if per_peer else src_ref
        out.append(pltpu.make_async_remote_copy(
            src_ref=src, dst_ref=land_ref.at[peer if incoming else me],
            send_sem=send_sems.at[m - 1], recv_sem=recv_sems.at[m - 1],
            device_id=(px, py, pc), device_id_type=MESH))
    return out


def _split_start(src, name, per_peer):
    slab = src.shape[1:] if per_peer else src.shape

    def body(src_ref, land_ref, send_sems, recv_sems, src_thru, land_thru, token):
        for cp in _split_copies(src_ref, land_ref, send_sems, recv_sems, per_peer, incoming=False):
            cp.start()
        token[...] = jnp.zeros_like(token)

    return pl.pallas_call(
        body, name=name,
        out_shape=(pltpu.SemaphoreType.DMA((NDEV - 1,)), pltpu.SemaphoreType.DMA((NDEV - 1,)),
                   pltpu.HBM(src.shape, src.dtype), pltpu.HBM((NDEV,) + slab, src.dtype),
                   jax.ShapeDtypeStruct((8, TB), F32)),
        in_specs=(_HBM, _HBM), out_specs=(_SEM, _SEM, _HBM, _HBM, pl.BlockSpec(memory_space=pltpu.VMEM)),
        input_output_aliases={0: 2, 1: 3},
        compiler_params=pltpu.CompilerParams(has_side_effects=_EFFECT),
    )(pltpu.with_memory_space_constraint(src, pltpu.HBM),
      pltpu.with_memory_space_constraint(lax.empty((NDEV,) + slab, src.dtype), pltpu.HBM))


def _split_wait(handles, after, name, per_peer):
    send_sems, recv_sems, src_thru, land_thru, _ = handles

    def body(src_ref, land_ref, send_sems, recv_sems, after_ref, src_out, land_out):
        for cp in _split_copies(src_ref, land_ref, send_sems, recv_sems, per_peer, incoming=False):
            cp.wait_send()
        for cp in _split_copies(src_ref, land_ref, send_sems, recv_sems, per_peer, incoming=True):
            cp.wait_recv()

    return pl.pallas_call(
        body, name=name,
        out_shape=(pltpu.HBM(src_thru.shape, src_thru.dtype), pltpu.HBM(land_thru.shape, land_thru.dtype)),
        in_specs=(_HBM, _HBM, _SEM, _SEM, pl.BlockSpec(memory_space=pl.ANY)), out_specs=(_HBM, _HBM),
        input_output_aliases={0: 0, 1: 1},
        compiler_params=pltpu.CompilerParams(has_side_effects=_EFFECT),
    )(src_thru, land_thru, send_sems, recv_sems, after)


def _pick_slab(j, own_ref, land_ref, rows, per_peer=True):
    mx, my, mc = _my_place()
    me = 4 * mx + 2 * my + mc
    own = (lambda: own_ref[j, rows, :]) if per_peer else (lambda: own_ref[rows, :])
    return lax.cond(me == j, own, lambda: land_ref[j, rows, :])


def _h_block(t, x_ref, meta_ref):
    first = jnp.concatenate([jnp.zeros((P0, D), F32), meta_ref[...]], axis=0)
    return jnp.where(t == 0, first, x_ref[...])


def _x_spec():
    return pl.BlockSpec((TB, D), lambda t: (jnp.maximum(t - 1, 0), 0))


def _x_specs3():
    return [pl.BlockSpec((TB, D), lambda j: (jnp.maximum(3 * j - 1, 0), 0)),
            pl.BlockSpec((TB, D), lambda j: (3 * j, 0)),
            pl.BlockSpec((TB, D), lambda j: (3 * j + 1, 0))]


def _h_tile(j, xa_ref, xb_ref, xc_ref, meta_ref):
    first = jnp.concatenate([jnp.zeros((P0, D), F32), meta_ref[...]], axis=0)
    return jnp.concatenate([jnp.where(j == 0, first, xa_ref[...]), xb_ref[...], xc_ref[...]], axis=0)


def _full_spec(shape):
    return pl.BlockSpec(shape, lambda *_: (0,) * len(shape))


def _sigmoid(z):
    return 1.0 / (1.0 + jnp.exp(-z))


def _lane_tiles_sum(x):
    out = x[:, :TB]
    for i in range(1, x.shape[1] // TB):
        out = out + x[:, i * TB:(i + 1) * TB]
    return out


def _inproj_fwd(x, meta_full, norm_g, w_t, L):
    nj = L // TT

    def body(xa_ref, xb_ref, xc_ref, meta_ref, g_ref, w_ref, u_ref, proj_ref, f_ref, ktok_ref, vtok_ref):
        hb = _h_tile(pl.program_id(0), xa_ref, xb_ref, xc_ref, meta_ref)
        r = lax.rsqrt(jnp.mean(hb * hb, axis=-1, keepdims=True) + EPS)
        u = (hb * r * g_ref[...]).astype(BF16)
        u_ref[...] = u
        for s in range(NSEC):
            p = _dot(u, w_ref[s * DA:(s + 1) * DA, :], NT_DIMS)
            if s == 0:
                p = p * QSCALE
            if s in (1, 2):
                tok_ref = ktok_ref if s == 1 else vtok_ref
                for h in range(H):
                    tok_ref[h] = p[:, h * DH:(h + 1) * DH].astype(BF16)
            proj_ref[s * DA:(s + 1) * DA, :] = p.T.astype(BF16)
        f_ref[...] = _dot(w_ref[NSEC * DA:DPROJ, :], u, NT_DIMS)[:H]

    return pl.pallas_call(
        body, name="inproj_fwd", grid=(nj,),
        in_specs=_x_specs3() + [_full_spec((NM, D)), _full_spec((1, D)), _full_spec((DPROJ, D))],
        out_specs=[
            pl.BlockSpec((TT, D), lambda t: (t, 0)),
            pl.BlockSpec((NSEC * DA, TT), lambda t: (0, t)),
            pl.BlockSpec((H, TT), lambda t: (0, t)),
            pl.BlockSpec((H, TT, DH), lambda t: (0, t, 0)),
            pl.BlockSpec((H, TT, DH), lambda t: (0, t, 0)),
        ],
        out_shape=[
            jax.ShapeDtypeStruct((L, D), BF16),
            jax.ShapeDtypeStruct((NSEC * DA, L), BF16),
            jax.ShapeDtypeStruct((H, L), F32),
            jax.ShapeDtypeStruct((H, L, DH), BF16),
            jax.ShapeDtypeStruct((H, L, DH), BF16),
        ],
        compiler_params=_params(),
    )(x, x, x, meta_full, norm_g, w_t)


def _split3(x):
    hi = x.astype(BF16).astype(F32)
    r = x - hi
    mid = r.astype(BF16).astype(F32)
    return hi, mid, (r - mid).astype(BF16).astype(F32)


def _bias_rows(bias):
    one = jnp.ones((1, TT), F32)
    zero = jnp.zeros((1, TT), F32)
    parts = [zero] * 3 if bias is None else list(_split3(bias))
    return jnp.concatenate([one] * 3 + parts + [zero] * (DF - 6), axis=0).astype(BF16)


def _fgate_fwd(f_t, b_col, ktok, L):
    nb = L // TB

    def body(f_ref, b_ref, ktok_ref, cq_ref, kaug_ref, sg_ref):
        z = f_ref[...] + b_ref[...]
        idx = lax.broadcasted_iota(jnp.int32, (H, L), 1)
        real = idx >= P0
        lf = jnp.where(real, jnp.minimum(z, 0.0) - jnp.log1p(jnp.exp(-jnp.abs(z))), 0.0)
        sg_ref[...] = jnp.where(real, 1.0 / (1.0 + jnp.exp(z)), 0.0)
        c = lf
        s = 1
        while s < L:
            c = c + jnp.where(idx >= s, pltpu.roll(c, s, 1), 0.0)
            s *= 2
        c = c * LOG2E
        for h in range(H):
            cq_ref[h] = c[h:h + 1, :]
        hi, mid, lo = _split3(-jnp.where(real, c, -NEG))
        lane = lax.broadcasted_iota(jnp.int32, (TB, KA), 1)
        ones = jnp.ones((3, TB), F32)
        for h in range(H):
            for b in range(nb):
                blk = slice(b * TB, (b + 1) * TB)
                cols = jnp.concatenate([
                    jnp.zeros((DH, TB), F32), hi[h:h + 1, blk], mid[h:h + 1, blk], lo[h:h + 1, blk], ones,
                    jnp.zeros((KA - DH - 6, TB), F32)], axis=0).T
                k = jnp.concatenate([ktok_ref[h, blk, :].astype(F32), jnp.zeros((TB, KA - DH), F32)], axis=1)
                kaug_ref[h, blk, :] = jnp.where(lane < DH, k, cols).astype(BF16)

    return pl.pallas_call(
        body, name="fgate_fwd",
        out_shape=[
            jax.ShapeDtypeStruct((H, 1, L), F32),
            jax.ShapeDtypeStruct((H, L, KA), BF16),
            jax.ShapeDtypeStruct((H, L), F32),
        ],
        compiler_params=pltpu.CompilerParams(vmem_limit_bytes=VMEM_LIMIT),
    )(f_t, b_col, ktok)


def _causal_mask():
    r = lax.broadcasted_iota(jnp.int32, (TT, TT), 0)
    c = lax.broadcasted_iota(jnp.int32, (TT, TT), 1)
    return r <= c


def _attn_fwd(proj_t, kaug, cq, L):
    nq = L // TT

    def body(q_ref, qn_ref, kaug_ref, v_ref, cq_ref, o_ref, lse_ref,
             qa_scr, s_scr, cmax_scr, m_scr, p_scr, alpha_scr, acc_scr):
        j = pl.program_id(0)
        rows = [slice(g * DH, (g + 1) * DH) for g in range(HG)]
        ones = jnp.ones((DF, TT), BF16)

        def load_queries(ref):
            for g in range(HG):
                qa_scr[g] = jnp.concatenate(
                    [ref[rows[g], :], _bias_rows(None), jnp.zeros((KA - DH - DF, TT), BF16)], axis=0)

        def scores(kt, masked):
            k_off = pl.multiple_of(kt * TT, TT)
            for g in range(HG):
                s = _dot(kaug_ref[g, pl.ds(k_off, TT), :], qa_scr[g])
                if masked:
                    s = jnp.where(_causal_mask(), s, NEG)
                s_scr[g] = s
                cmax_scr[g] = jnp.max(s, axis=0, keepdims=True)

        def softmax():
            for g in range(HG):
                m_old = m_scr[g]
                m_new = jnp.maximum(m_old, cmax_scr[g])
                alpha_scr[g] = jnp.exp2(m_old - m_new)
                p_scr[g] = jnp.exp2(s_scr[g] - m_new).astype(BF16)
                m_scr[g] = m_new

        def weighted_sum(kt):
            k_off = pl.multiple_of(kt * TT, TT)
            for g in range(HG):
                v1 = jnp.concatenate([v_ref[rows[g], pl.ds(k_off, TT)], ones], axis=0)
                acc_scr[g] = alpha_scr[g] * acc_scr[g] + _dot(v1, p_scr[g])

        @pl.when(j == 0)
        def _():
            load_queries(q_ref)
            scores(0, True)

        m_scr[...] = jnp.full_like(m_scr, NEG)
        acc_scr[...] = jnp.zeros_like(acc_scr)

        @pl.when(j >= 1)
        def _():
            softmax()
            scores(j - 1, False)

        def step(i, c):
            weighted_sum(j - i + 1)
            softmax()
            scores(j - i - 1, False)
            return c

        lax.fori_loop(1, j, step, 0)

        @pl.when(j >= 1)
        def _():
            weighted_sum(1)

        @pl.when(j < nq - 1)
        def _():
            softmax()
            load_queries(qn_ref)
            scores(j + 1, True)
            weighted_sum(0)

        @pl.when(j == nq - 1)
        def _():
            softmax()
            weighted_sum(0)

        for g in range(HG):
            l = acc_scr[g, DH:DH + 1, :]
            o_ref[rows[g], :] = acc_scr[g, :DH, :] * (1.0 / l)
            lse_ref[g] = m_scr[g] + jnp.log2(l) + cq_ref[g]

    assert HG == H
    return pl.pallas_call(
        body, name="attn_fwd", grid=(nq,),
        in_specs=[
            pl.BlockSpec((DA, TT), lambda j: (0, j)),
            pl.BlockSpec((DA, TT), lambda j: (0, jnp.minimum(j + 1, nq - 1))),
            pl.BlockSpec((H, L, KA), lambda j: (0, 0, 0)),
            pl.BlockSpec((DA, L), lambda j: (2, 0)),
            pl.BlockSpec((H, 1, TT), lambda j: (0, 0, j)),
        ],
        out_specs=[
            pl.BlockSpec((DA, TT), lambda j: (0, j)),
            pl.BlockSpec((H, 1, TT), lambda j: (0, 0, j)),
        ],
        out_shape=[jax.ShapeDtypeStruct((DA, L), F32), jax.ShapeDtypeStruct((H, 1, L), F32)],
        scratch_shapes=[pltpu.VMEM((HG, KA, TT), BF16), pltpu.VMEM((HG, TT, TT), F32), pltpu.VMEM((HG, 1, TT), F32),
                        pltpu.VMEM((HG, 1, TT), F32), pltpu.VMEM((HG, TT, TT), BF16), pltpu.VMEM((HG, 1, TT), F32),
                        pltpu.VMEM((HG, DH + DF, TT), F32)],
        compiler_params=_params(),
    )(proj_t, proj_t, kaug, proj_t, cq)


def _gate_group(rows, o_ref, za_ref, gb_ref, gc_ref, xc_ref, zc_ref, gcp_ref, xcp_ref, cw_ref, ga_ref, gcn_ref, first):
    n_rep = TT // TB
    f32 = lambda r: r[rows, :].astype(F32)
    o, za, gb, gc, xc, zc = o_ref[rows, :], f32(za_ref), f32(gb_ref), f32(gc_ref), f32(xc_ref), f32(zc_ref)
    a = gc * xc
    a_prev = jnp.where(first, 0.0, f32(gcp_ref) * f32(xcp_ref))
    full = jnp.concatenate([a_prev, a], axis=1)
    a1 = pltpu.roll(full, 1, 1)[:, TB:]
    a2 = pltpu.roll(full, 2, 1)[:, TB:]
    w0 = jnp.tile(cw_ref[0, rows, :], (1, n_rep))
    w1 = jnp.tile(cw_ref[1, rows, :], (1, n_rep))
    w2 = jnp.tile(cw_ref[2, rows, :], (1, n_rep))
    cv = w0 * a2 + w1 * a1 + w2 * a
    e = gb * cv
    rc = lax.rsqrt(jnp.mean(e * e, axis=0, keepdims=True) + EPS)
    ec = e * rc
    ra = lax.rsqrt(jnp.mean(o * o, axis=0, keepdims=True) + EPS)
    oa = o * ra
    g_a = jnp.tile(ga_ref[rows, :], (1, n_rep))
    g_c = jnp.tile(gcn_ref[rows, :], (1, n_rep))
    sa = _sigmoid(za)
    sc = _sigmoid(zc)
    return dict(o=o, za=za, gb=gb, gc=gc, xc=xc, zc=zc, a=a, a1=a1, a2=a2, w0=w0, w1=w1, w2=w2, cv=cv, e=e,
                rc=rc, ec=ec, ra=ra, oa=oa, g_a=g_a, g_c=g_c, sa=sa, sc=sc)


def _gate_specs(nj, rev):
    def jj(i):
        return (nj - 1 - i) if rev else i

    def sec(s):
        return pl.BlockSpec((DA, TT), lambda i: (s, jj(i)))

    def halo(s):
        return pl.BlockSpec((DA, TB), lambda i: (s, jnp.maximum(3 * jj(i) - 1, 0)))

    return [pl.BlockSpec((DA, TT), lambda i: (0, jj(i))), sec(3), sec(4), sec(5), sec(6), sec(7), halo(5), halo(6),
            _full_spec((3, DA, TB)), _full_spec((DA, TB)), _full_spec((DA, TB))]


def _gate_fwd(o_t, proj_t, cw_b, ga_b, gcn_b, L):
    nj = L // TT

    def body(o_ref, za_ref, gb_ref, gc_ref, xc_ref, zc_ref, gcp_ref, xcp_ref, cw_ref, ga_ref, gcn_ref, mix_ref):
        j = pl.program_id(0)

        def group(h, c):
            r0 = pl.multiple_of(h * DH, DH)
            g = _gate_group(pl.ds(r0, DH), o_ref, za_ref, gb_ref, gc_ref, xc_ref, zc_ref, gcp_ref, xcp_ref,
                            cw_ref, ga_ref, gcn_ref, j == 0)
            mix_ref[pl.ds(r0, DH), :] = (g["oa"] * g["g_a"] * (g["za"] * g["sa"])).astype(BF16)
            mix_ref[pl.ds(DA + r0, DH), :] = (g["ec"] * g["g_c"] * (g["zc"] * g["sc"])).astype(BF16)
            return c

        lax.fori_loop(0, H, group, 0, unroll=2)

    return pl.pallas_call(
        body, name="gate_fwd", grid=(nj,),
        in_specs=_gate_specs(nj, False),
        out_specs=pl.BlockSpec((2 * DA, TT), lambda j: (0, j)),
        out_shape=jax.ShapeDtypeStruct((2 * DA, L), BF16),
        compiler_params=_params(),
    )(o_t, proj_t, proj_t, proj_t, proj_t, proj_t, proj_t, proj_t, cw_b, ga_b, gcn_b)


def _outproj(mix_t, w_out, x, meta_full, fng, target, L):
    nj = L // TT

    def body(mix_ref, w_ref, xa_ref, xb_ref, xc_ref, meta_ref, g_ref, ta_ref, tb_ref, tc_ref,
             dout_ref, dmix_ref, dwb_ref, loss_ref, dg_ref, dw_ref):
        t = pl.program_id(0)

        @pl.when(t == 0)
        def _():
            dw_ref[...] = jnp.zeros_like(dw_ref)
            loss_ref[...] = jnp.zeros_like(loss_ref)
            dg_ref[...] = jnp.zeros_like(dg_ref)

        mix = mix_ref[...]
        o = _dot(mix, w_ref[...], TN_DIMS) + _h_tile(t, xa_ref, xb_ref, xc_ref, meta_ref)
        r = lax.rsqrt(jnp.mean(o * o, axis=-1, keepdims=True) + EPS)
        g = g_ref[...]
        orn = o * r
        tgt = jnp.concatenate([ta_ref[...], tb_ref[...], tc_ref[...]], axis=0)
        row = lax.broadcasted_iota(jnp.int32, (TT, 1), 0)
        real = jnp.where((t > 0) | (row >= TB), 1.0, 0.0)
        diff = (orn * g - tgt) * real
        loss_ref[...] += 0.5 * jnp.sum(diff * diff) * (1.0 / D)
        dy = diff * (1.0 / D)
        dg_ref[...] += jnp.sum(dy * orn, axis=0, keepdims=True)
        gy = dy * g
        dout = r * gy - orn * (r * jnp.mean(gy * orn, axis=-1, keepdims=True))
        dout_ref[...] = dout
        db = dout.astype(BF16)
        dmix_ref[...] = _dot(db, w_ref[...], NT_DIMS).T.astype(BF16)
        dw_ref[...] += _dot(mix, db)

        @pl.when(t == nj - 1)
        def _():
            dwb_ref[...] = dw_ref[...].astype(BF16)

    return pl.pallas_call(
        body, name="outproj", grid=(nj,),
        in_specs=[pl.BlockSpec((D, TT), lambda t: (0, t)), _full_spec((D, D))] + _x_specs3()
                 + [_full_spec((NM, D)), _full_spec((1, D))] + _x_specs3(),
        out_specs=[pl.BlockSpec((TT, D), lambda t: (t, 0)), pl.BlockSpec((D, TT), lambda t: (0, t)),
                   _full_spec((D, D)), _full_spec((1, 1)), _full_spec((1, D))],
        out_shape=[jax.ShapeDtypeStruct((L, D), F32), jax.ShapeDtypeStruct((D, L), BF16),
                   jax.ShapeDtypeStruct((D, D), BF16), jax.ShapeDtypeStruct((1, 1), F32),
                   jax.ShapeDtypeStruct((1, D), F32)],
        scratch_shapes=[pltpu.VMEM((D, D), F32)],
        compiler_params=_params(),
    )(mix_t, w_out, x, x, x, meta_full, fng, target, target, target)


def _gate_bwd(dmix_t, o_t, proj_t, cw_b, ga_b, gcn_b, L):
    nj = L // TT

    def body(dmix_ref, o_ref, za_ref, gb_ref, gc_ref, xc_ref, zc_ref, gcp_ref, xcp_ref, cw_ref, ga_ref, gcn_ref,
             do_ref, dd_ref, dg5_ref, dga_ref, dgc_ref, dcw_ref, carry_ref):
        i = pl.program_id(0)
        j = nj - 1 - i

        @pl.when(i == 0)
        def _():
            carry_ref[...] = jnp.zeros_like(carry_ref)
            dga_ref[...] = jnp.zeros_like(dga_ref)
            dgc_ref[...] = jnp.zeros_like(dgc_ref)
            dcw_ref[...] = jnp.zeros_like(dcw_ref)

        def group(h, c):
            r0 = pl.multiple_of(h * DH, DH)
            rows = pl.ds(r0, DH)
            sec = lambda s: pl.ds(s * DA + r0, DH)
            g = _gate_group(rows, o_ref, za_ref, gb_ref, gc_ref, xc_ref, zc_ref, gcp_ref, xcp_ref,
                            cw_ref, ga_ref, gcn_ref, j == 0)
            o, za, gb, gc, xc, zc, sa, sc = (g[n] for n in ("o", "za", "gb", "gc", "xc", "zc", "sa", "sc"))
            dya = dmix_ref[rows, :].astype(F32)
            dyc = dmix_ref[pl.ds(DA + r0, DH), :].astype(F32)

            dn = dya * (za * sa)
            dg5_ref[sec(0), :] = (dya * (g["oa"] * g["g_a"]) * (sa * (1.0 + za * (1.0 - sa)))).astype(BF16)
            dga_ref[rows, :] += _lane_tiles_sum(dn * g["oa"])
            dng = dn * g["g_a"]
            mean_a = jnp.mean(dng * g["oa"], axis=0, keepdims=True)
            do = (dng - g["oa"] * mean_a) * g["ra"]
            do_ref[rows, :] = do.astype(BF16)
            dd_ref[h] = jnp.sum(do * o, axis=0, keepdims=True)

            dnc = dyc * (zc * sc)
            dg5_ref[sec(4), :] = (dyc * (g["ec"] * g["g_c"]) * (sc * (1.0 + zc * (1.0 - sc)))).astype(BF16)
            dgc_ref[rows, :] += _lane_tiles_sum(dnc * g["ec"])
            dncg = dnc * g["g_c"]
            mean_c = jnp.mean(dncg * g["ec"], axis=0, keepdims=True)
            de = (dncg - g["ec"] * mean_c) * g["rc"]
            dg5_ref[sec(1), :] = (de * g["cv"]).astype(BF16)
            dcv = de * gb
            full = jnp.concatenate([dcv, carry_ref[rows, :]], axis=1)
            d1 = pltpu.roll(full, TT + TB - 1, 1)[:, :TT]
            d2 = pltpu.roll(full, TT + TB - 2, 1)[:, :TT]
            carry_ref[rows, :] = dcv[:, :TB]
            da = g["w2"] * dcv + g["w1"] * d1 + g["w0"] * d2
            dg5_ref[sec(2), :] = (da * xc).astype(BF16)
            dg5_ref[sec(3), :] = (da * gc).astype(BF16)
            dcw_ref[0, rows, :] += _lane_tiles_sum(dcv * g["a2"])
            dcw_ref[1, rows, :] += _lane_tiles_sum(dcv * g["a1"])
            dcw_ref[2, rows, :] += _lane_tiles_sum(dcv * g["a"])
            return c

        lax.fori_loop(0, H, group, 0, unroll=2)

    rj = lambda i: nj - 1 - i
    return pl.pallas_call(
        body, name="gate_bwd", grid=(nj,),
        in_specs=[pl.BlockSpec((2 * DA, TT), lambda i: (0, rj(i)))] + _gate_specs(nj, True),
        out_specs=[
            pl.BlockSpec((DA, TT), lambda i: (0, rj(i))),
            pl.BlockSpec((H, 1, TT), lambda i: (0, 0, rj(i))),
            pl.BlockSpec((5 * DA, TT), lambda i: (0, rj(i))),
            _full_spec((DA, TB)), _full_spec((DA, TB)), _full_spec((3, DA, TB)),
        ],
        out_shape=[
            jax.ShapeDtypeStruct((DA, L), BF16),
            jax.ShapeDtypeStruct((H, 1, L), F32),
            jax.ShapeDtypeStruct((5 * DA, L), BF16),
            jax.ShapeDtypeStruct((DA, TB), F32),
            jax.ShapeDtypeStruct((DA, TB), F32),
            jax.ShapeDtypeStruct((3, DA, TB), F32),
        ],
        scratch_shapes=[pltpu.VMEM((DA, TB), F32)],
        compiler_params=_params(),
    )(dmix_t, o_t, proj_t, proj_t, proj_t, proj_t, proj_t, proj_t, proj_t, cw_b, ga_b, gcn_b)


def _attn_bwd(proj_t, kaug, vtok, do_t, lse, dd, cq, L):
    nk = L // TT

    def body(q_ref, kaug_ref, vtok_ref, kt_ref, do_ref, lse_ref, dd_ref, cq_ref,
             dq_ref, dk_ref, dv_ref, dck_ref, dcq_ref, dq_acc, kt1_scr, s_scr, dp_scr, dv_scr, dk_scr):
        i = pl.program_id(0)
        rows = [slice(g * DH, (g + 1) * DH) for g in range(HG)]
        ones = jnp.ones((DF, TT), BF16)
        zpad = jnp.zeros((KA - DH - DF, TT), BF16)
        for g in range(HG):
            kt1_scr[g] = jnp.concatenate([kt_ref[rows[g], :], ones], axis=0)
        dv_scr[...] = jnp.zeros_like(dv_scr)
        dk_scr[...] = jnp.zeros_like(dk_scr)

        def q_rows(g, q_off):
            bias = cq_ref[g, :, pl.ds(q_off, TT)] - lse_ref[g, :, pl.ds(q_off, TT)]
            return jnp.concatenate([q_ref[rows[g], pl.ds(q_off, TT)], _bias_rows(bias)], axis=0)

        def scores(jq, masked):
            q_off = pl.multiple_of(jq * TT, TT)
            for g in range(HG):
                s = _dot(kaug_ref[g], jnp.concatenate([q_rows(g, q_off), zpad], axis=0))
                if masked:
                    s = jnp.where(_causal_mask(), s, NEG)
                s_scr[g] = s
                dp_scr[g] = _dot(vtok_ref[g], do_ref[rows[g], pl.ds(q_off, TT)])

        def grads(jq):
            q_off = pl.multiple_of(jq * TT, TT)
            for g in range(HG):
                p = jnp.exp2(s_scr[g])
                ds = (p * (dp_scr[g] - dd_ref[g, :, pl.ds(q_off, TT)])).astype(BF16)
                do1 = jnp.concatenate([do_ref[rows[g], pl.ds(q_off, TT)], jnp.zeros((KA - DH, TT), BF16)], axis=0)
                q1 = jnp.concatenate([q_rows(g, q_off), zpad], axis=0)
                dv_scr[g] += _dot(p.astype(BF16), do1, NT_DIMS)
                dk_scr[g] += _dot(ds, q1, NT_DIMS)
                dq_acc[g, :, pl.ds(q_off, TT)] += _dot(kt1_scr[g], ds)

        @pl.when(i == 0)
        def _():
            dq_acc[...] = jnp.zeros_like(dq_acc)

        scores(i, True)

        def step(jq, c):
            grads(jq)
            scores(jq + 1, False)
            return c

        lax.fori_loop(i, nk - 1, step, 0)
        grads(nk - 1)
        for g in range(HG):
            dv_ref[rows[g], :] = dv_scr[g].T[:DH, :].astype(BF16)
            dk_t = dk_scr[g].T
            dk_ref[rows[g], :] = (dk_t[:DH, :] * LN2).astype(BF16)
            dck_ref[g] = dk_t[DH:DH + 1, :]

        @pl.when(i == nk - 1)
        def _():
            for g in range(HG):
                dq_ref[rows[g], :] = (dq_acc[g, :DH, :] * (DH ** -0.5)).astype(BF16)
                dcq_ref[g] = dq_acc[g, DH:DH + 1, :]

    assert HG == H
    head = lambda i: (0, 0)
    row = lambda i: (0, 0, 0)
    return pl.pallas_call(
        body, name="attn_bwd", grid=(nk,),
        in_specs=[
            pl.BlockSpec((DA, L), head),
            pl.BlockSpec((H, TT, KA), lambda i: (0, i, 0)),
            pl.BlockSpec((H, TT, DH), lambda i: (0, i, 0)),
            pl.BlockSpec((DA, TT), lambda i: (1, i)),
            pl.BlockSpec((DA, L), head),
            pl.BlockSpec((H, 1, L), row), pl.BlockSpec((H, 1, L), row), pl.BlockSpec((H, 1, L), row),
        ],
        out_specs=[
            pl.BlockSpec((DA, L), head),
            pl.BlockSpec((DA, TT), lambda i: (0, i)),
            pl.BlockSpec((DA, TT), lambda i: (0, i)),
            pl.BlockSpec((H, 1, TT), lambda i: (0, 0, i)),
            pl.BlockSpec((H, 1, L), row),
        ],
        out_shape=[jax.ShapeDtypeStruct((DA, L), BF16), jax.ShapeDtypeStruct((DA, L), BF16),
                   jax.ShapeDtypeStruct((DA, L), BF16), jax.ShapeDtypeStruct((H, 1, L), F32),
                   jax.ShapeDtypeStruct((H, 1, L), F32)],
        scratch_shapes=[
            pltpu.VMEM((HG, DH + DF, L), F32),
            pltpu.VMEM((HG, DH + DF, TT), BF16),
            pltpu.VMEM((HG, TT, TT), F32), pltpu.VMEM((HG, TT, TT), F32),
            pltpu.VMEM((HG, TT, KA), F32), pltpu.VMEM((HG, TT, KA), F32)],
        compiler_params=_params(),
    )(proj_t, kaug, vtok, proj_t, do_t, lse, dd, cq)


def _fgate_bwd(dcq, dck, sg, L):
    def body(dcq_ref, dck_ref, sg_ref, df_ref, db_ref):
        dc = jnp.concatenate([dcq_ref[h] - dck_ref[h] for h in range(H)], axis=0)
        idx = lax.broadcasted_iota(jnp.int32, (H, L), 1)
        r = dc
        s = 1
        while s < L:
            r = r + jnp.where(idx + s < L, pltpu.roll(r, L - s, 1), 0.0)
            s *= 2
        df = r * sg_ref[...]
        db_ref[...] = jnp.broadcast_to(jnp.sum(df, axis=1, keepdims=True), (H, TB))
        df_ref[...] = jnp.concatenate([df, jnp.zeros((DF - H, L), F32)], axis=0).astype(BF16)

    return pl.pallas_call(
        body, name="fgate_bwd",
        out_shape=[jax.ShapeDtypeStruct((DF, L), BF16), jax.ShapeDtypeStruct((H, TB), F32)],
        compiler_params=pltpu.CompilerParams(vmem_limit_bytes=VMEM_LIMIT),
    )(dcq, dck, sg)


def _inproj_bwd_x(w, dq_t, dk_t, dv_t, dg5_t, df_t, dout, x, meta_full, norm_g, L):
    nj = L // TT
    seq = x.shape[0]

    def body(w_ref, dq_ref, dk_ref, dv_ref, dg5_ref, df_ref, dout_ref, xa_ref, xb_ref, xc_ref, meta_ref, g_ref,
             gx_ref, dmeta_ref, dg_ref, dh_scr, sems):
        j = pl.program_id(0)
        slot = j % 2

        def copy_out(step, slot_):
            first = pltpu.make_async_copy(dh_scr.at[slot_, pl.ds(TB, TT - TB)], gx_ref.at[pl.ds(0, TT - TB)],
                                          sems.at[slot_])
            later = pltpu.make_async_copy(dh_scr.at[slot_], gx_ref.at[pl.ds(step * TT - TB, TT)], sems.at[slot_])
            return first, later

        @pl.when(j == 0)
        def _():
            dg_ref[...] = jnp.zeros_like(dg_ref)

        du = _dot(dq_ref[...], w_ref[0:DA, :], TN_DIMS)
        du += _dot(dk_ref[...], w_ref[DA:2 * DA, :], TN_DIMS)
        du += _dot(dv_ref[...], w_ref[2 * DA:3 * DA, :], TN_DIMS)
        du += _dot(dg5_ref[...], w_ref[3 * DA:NSEC * DA, :], TN_DIMS)
        du += _dot(df_ref[...], w_ref[NSEC * DA:DPROJ, :], TN_DIMS)
        hb = _h_tile(j, xa_ref, xb_ref, xc_ref, meta_ref)
        r = lax.rsqrt(jnp.mean(hb * hb, axis=-1, keepdims=True) + EPS)
        hn = hb * r
        dg_ref[...] += jnp.sum(du * hn, axis=0, keepdims=True)
        gu = du * g_ref[...]
        dh = dout_ref[...] + r * gu - hn * (r * jnp.mean(gu * hn, axis=-1, keepdims=True))

        dh_scr[slot] = dh

        @pl.when(j == 0)
        def _():
            dmeta_ref[...] = dh[P0:TB, :]
            copy_out(0, 0)[0].start()

        @pl.when(j >= 1)
        def _():
            copy_out(j, slot)[1].start()

        @pl.when(j == 1)
        def _():
            copy_out(0, 0)[0].wait()

        @pl.when(j >= 2)
        def _():
            copy_out(j - 1, 1 - slot)[1].wait()

        @pl.when(j == nj - 1)
        def _():
            copy_out(j, slot)[0 if nj == 1 else 1].wait()

    blk = lambda rows: pl.BlockSpec((rows, TT), lambda j: (0, j))
    return pl.pallas_call(
        body, name="inproj_bwd_x", grid=(nj,),
        in_specs=[_full_spec((DPROJ, D)), blk(DA), blk(DA), blk(DA), blk(5 * DA), blk(DF),
                  pl.BlockSpec((TT, D), lambda j: (j, 0))] + _x_specs3() + [_full_spec((NM, D)), _full_spec((1, D))],
        out_specs=[pl.BlockSpec(memory_space=pl.ANY), _full_spec((NM, D)), _full_spec((1, D))],
        out_shape=[jax.ShapeDtypeStruct((seq, D), F32), jax.ShapeDtypeStruct((NM, D), F32),
                   jax.ShapeDtypeStruct((1, D), F32)],
        scratch_shapes=[pltpu.VMEM((2, TT, D), F32), pltpu.SemaphoreType.DMA((2,))],
        compiler_params=_params(),
    )(w, dq_t, dk_t, dv_t, dg5_t, df_t, dout, x, x, x, meta_full, norm_g)


def _inproj_bwd_w(u, dq_t, dk_t, dv_t, dg5_t, df_t, L):
    def body(u_ref, dq_ref, dk_ref, dv_ref, dg5_ref, df_ref, dw_ref, dwf_ref):
        s = pl.program_id(0)
        u_all = u_ref[...]

        @pl.when(s < 5)
        def _():
            dw_ref[...] = _dot(dg5_ref[...], u_all)

        for step, ref in ((5, dq_ref), (6, dk_ref), (7, dv_ref)):
            @pl.when(s == step)
            def _(ref=ref):
                dw_ref[...] = _dot(ref[...], u_all)

        @pl.when(s == NSEC - 1)
        def _():
            dwf_ref[...] = _dot(df_ref[...], u_all)

    once = lambda shape: pl.BlockSpec(shape, lambda s: (0, 0), pipeline_mode=pl.Buffered(1))
    return pl.pallas_call(
        body, name="inproj_bwd_w", grid=(NSEC,),
        in_specs=[
            once((L, D)), once((DA, L)), once((DA, L)), once((DA, L)),
            pl.BlockSpec((DA, L), lambda s: (jnp.minimum(s, 4), 0)),
            once((DF, L)),
        ],
        out_specs=[pl.BlockSpec((DA, D), lambda s: (jnp.where(s < 5, s + 3, s - 5), 0)), _full_spec((DF, D))],
        out_shape=[jax.ShapeDtypeStruct((NSEC * DA, D), F32), jax.ShapeDtypeStruct((DF, D), F32)],
        compiler_params=_params(),
    )(u, dq_t, dk_t, dv_t, dg5_t, df_t)


def _adamw(w, g, m, v):
    m = ADAM_B1 * m + (1.0 - ADAM_B1) * g
    v = ADAM_B2 * v + (1.0 - ADAM_B2) * (g * g)
    m_hat = m / (1.0 - ADAM_B1 ** ADAM_STEP)
    v_hat = v / (1.0 - ADAM_B2 ** ADAM_STEP)
    delta = -ADAM_LR * (m_hat / (jnp.sqrt(v_hat) + ADAM_EPS) + ADAM_WD * w)
    return delta, m, v


def _adamw_big(own_in, land_in, own_out, land_out, w_in_t, m_in_t, v_in_t, w_out, m_out, v_out):
    cb = CB
    e_sh = D // NDEV
    in_shape = jax.ShapeDtypeStruct(w_in_t.shape, F32)
    out_shape = jax.ShapeDtypeStruct(w_out.shape, F32)

    def total(own_ref, land_ref, rows):
        g = _pick_slab(0, own_ref, land_ref, rows).astype(F32)
        for j in range(1, NDEV):
            g = g + _pick_slab(j, own_ref, land_ref, rows).astype(F32)
        return g

    def body(oi_ref, li_ref, oo_ref, lo_ref, wi_ref, mi_ref, vi_ref, wo_ref, mo_ref, vo_ref,
             gi, di, mi, vi, go, do, mo, vo):
        g = total(oi_ref, li_ref, slice(0, WSHP))[:WSH]
        d, mn, vn = _adamw(wi_ref[...], g, mi_ref[...], vi_ref[...])
        gi[...], di[...], mi[...], vi[...] = g, d, mn, vn
        g = total(oo_ref, lo_ref, slice(0, e_sh))
        d, mn, vn = _adamw(wo_ref[0], g, mo_ref[0], vo_ref[0])
        go[0], do[0], mo[0], vo[0] = g, d, mn, vn

    slab = lambda rows: pl.BlockSpec((NDEV, rows, cb), lambda i: (0, 0, i))
    ispec = pl.BlockSpec((WSH, cb), lambda i: (0, i))
    ospec = pl.BlockSpec((1, e_sh, cb), lambda i: (0, 0, i))
    return pl.pallas_call(
        body, name="adamw_big", grid=(D // cb,),
        in_specs=[slab(WSHP), slab(WSHP), slab(e_sh), slab(e_sh), ispec, ispec, ispec, ospec, ospec, ospec],
        out_specs=[ispec] * 4 + [ospec] * 4, out_shape=[in_shape] * 4 + [out_shape] * 4,
        compiler_params=_params(),
    )(own_in, land_in, own_out, land_out, w_in_t, m_in_t, v_in_t, w_out, m_out, v_out)


F0 = 3 * DA


def _unshard_w_out(own, land):
    e_sh = D // NDEV

    def body(own_ref, land_ref, wo_ref):
        for j in range(NDEV):
            wo_ref[j * e_sh:(j + 1) * e_sh, :] = _pick_slab(j, own_ref, land_ref, slice(0, e_sh), per_peer=False)

    return pl.pallas_call(
        body, name="unshard_w_out", grid=(D // CB,),
        in_specs=[pl.BlockSpec((e_sh, CB), lambda i: (0, i)), pl.BlockSpec((NDEV, e_sh, CB), lambda i: (0, 0, i))],
        out_specs=pl.BlockSpec((D, CB), lambda i: (0, i)),
        out_shape=jax.ShapeDtypeStruct((D, D), BF16),
        compiler_params=_params(),
    )(own, land)


def _unshard_w_in(w_all):
    def body(w_ref, wt_ref):
        def ref_rows(lo, hi):
            pieces, r = [], lo
            while r < hi:
                sh, off = divmod(r, WSH)
                n = min(hi - r, WSH - off)
                pieces.append(w_ref[sh, off:off + n, :])
                r += n
            return pieces

        for s in range(NSEC):
            lo = s * DA if s < 3 else s * DA + H
            wt_ref[s * DA:(s + 1) * DA, :] = jnp.concatenate(ref_rows(lo, lo + DA), axis=0)
        wt_ref[NSEC * DA:DPROJ, :] = jnp.concatenate(
            ref_rows(F0, F0 + H) + [jnp.zeros((DF - H, CB), BF16)], axis=0)

    return pl.pallas_call(
        body, name="unshard_w_in", grid=(D // CB,),
        in_specs=[pl.BlockSpec((NDEV, WSHP, CB), lambda i: (0, 0, i))],
        out_specs=pl.BlockSpec((DPROJ, CB), lambda i: (0, i)),
        out_shape=jax.ShapeDtypeStruct((DPROJ, D), BF16),
        compiler_params=_params(),
    )(w_all)


def _shard_w_in_grads(dw_main, dw_f):
    def body(dm_ref, df_ref, p_ref):
        def ref_rows(lo, hi):
            pieces, r = [], lo
            while r < hi:
                if r < F0:
                    n = min(hi, F0) - r
                    pieces.append(dm_ref[r:r + n, :])
                elif r < F0 + H:
                    n = min(hi, F0 + H) - r
                    pieces.append(df_ref[r - F0:r - F0 + n, :])
                else:
                    n = hi - r
                    pieces.append(dm_ref[r - H:r - H + n, :])
                r += n
            return pieces

        for i in range(NDEV):
            rows = jnp.concatenate(ref_rows(i * WSH, (i + 1) * WSH) + [jnp.zeros((WSHP - WSH, CB), F32)], axis=0)
            p_ref[i] = rows.astype(BF16)

    col = lambda rows: pl.BlockSpec((rows, CB), lambda i: (0, i))
    return pl.pallas_call(
        body, name="shard_w_in_grads", grid=(D // CB,),
        in_specs=[col(NSEC * DA), col(DF)],
        out_specs=pl.BlockSpec((NDEV, WSHP, CB), lambda i: (0, 0, i)),
        out_shape=jax.ShapeDtypeStruct((NDEV, WSHP, D), BF16),
        compiler_params=_params(),
    )(dw_main, dw_f)


SMALL = ("norm_g", "final_norm_g", "attn_norm_g", "conv_norm_g", "b_f", "meta", "conv_w")


def _as_rows(x):
    return jnp.concatenate([x[:, r * TB:(r + 1) * TB] for r in range(x.shape[1] // TB)], axis=0)


def _as_line(rows):
    return jnp.concatenate([rows[r:r + 1, :] for r in range(rows.shape[0])], axis=1)


def _pad_rows(x, n=8):
    return jnp.concatenate([x, jnp.zeros((n - x.shape[0], x.shape[1]), F32)], axis=0)


def _tile_rows(a, rows, lanes=TB):
    a = a.reshape(rows, lanes)
    return jnp.pad(a, ((0, -rows % 8), (0, TB - lanes)))


def _pack_small_grads(dg_norm, dg_final, dga_p, dgc_p, dcw_p, db_b, dmeta, loss):
    def body(dgn_ref, dgf_ref, dga_ref, dgc_ref, dcw_ref, db_ref, dmeta_ref, loss_ref, out_ref):
        def lane_sums(p):
            return jnp.sum(p.T, axis=0, keepdims=True)

        lane = lax.broadcasted_iota(jnp.int32, (1, TB), 1)
        b_row = jnp.where(lane == H, loss_ref[...], 0.0)
        for h in range(H):
            b_row = b_row + jnp.where(lane == h, db_ref[h:h + 1, :], 0.0)
        common = jnp.concatenate([
            _as_rows(dgn_ref[...]), _as_rows(dgf_ref[...]), _pad_rows(_as_rows(lane_sums(dga_ref[...]))),
            _pad_rows(_as_rows(lane_sums(dgc_ref[...]))), _pad_rows(b_row)], axis=0)
        dcw = [lane_sums(dcw_ref[k]) for k in range(3)]
        for j in range(NDEV):
            cw = jnp.concatenate(
                [jnp.concatenate([r[:, j * DH:(j + 1) * DH], jnp.zeros((1, TB - DH), F32)], axis=1) for r in dcw],
                axis=0)
            out_ref[j] = jnp.concatenate([common, dmeta_ref[:, j * TB:(j + 1) * TB], _pad_rows(cw)], axis=0)

    return pl.pallas_call(
        body, name="pack_small_grads", out_shape=jax.ShapeDtypeStruct((NDEV, SROWS, TB), F32),
    )(dg_norm, dg_final, dga_p, dgc_p, dcw_p, db_b, dmeta, loss)


def _adamw_small(own, land, params):
    flat = [a for n in SMALL for a in params[n]]

    def body(*refs):
        own_ref, land_ref = refs[:2]
        ins = refs[2:2 + 3 * len(SMALL)]
        outs = refs[2 + 3 * len(SMALL):]
        g = _pick_slab(0, own_ref, land_ref, slice(0, SROWS))
        for j in range(1, NDEV):
            g = g + _pick_slab(j, own_ref, land_ref, slice(0, SROWS))
        grads = dict(
            norm_g=_as_line(g[0:8]), final_norm_g=_as_line(g[8:16]), attn_norm_g=_as_line(g[16:20]),
            conv_norm_g=_as_line(g[24:28]), b_f=g[32:33, :H], meta=g[40:56], conv_w=g[56:59, :DH][None])
        for i, n in enumerate(SMALL):
            w_ref, m_ref, v_ref = ins[3 * i:3 * i + 3]
            d, mn, vn = _adamw(w_ref[...], grads[n], m_ref[...], v_ref[...])
            for o_ref, val in zip(outs[4 * i:4 * i + 4], (grads[n], d, mn, vn)):
                o_ref[...] = val
        outs[-1][...] = g[32:33, H:H + 1]

    shapes = [jax.ShapeDtypeStruct(params[n][0].shape, F32) for n in SMALL for _ in range(4)]
    res = pl.pallas_call(
        body, name="adamw_small", out_shape=shapes + [jax.ShapeDtypeStruct((1, 1), F32)],
    )(own, land, *flat)
    return {n: res[4 * i:4 * i + 4] for i, n in enumerate(SMALL)}, res[-1]


def kernel(x, meta, norm_g, w_in, b_f, conv_w, attn_norm_g, conv_norm_g, w_out, final_norm_g, loss_target, m_meta, m_norm_g, m_w_in, m_b_f, m_conv_w, m_attn_norm_g, m_conv_norm_g, m_w_out, m_final_norm_g, v_meta, v_norm_g, v_w_in, v_b_f, v_conv_w, v_attn_norm_g, v_conv_norm_g, v_w_out, v_final_norm_g):
    seq = x.shape[1]
    L = seq + TB
    assert x.shape == (1, seq, D) and L % TT == 0 and w_in.shape == (1, D, WSH)
    x2 = x[0]
    tgt = loss_target[0]

    w_in_slab = jnp.pad(w_in[0].T, ((0, WSHP - WSH), (0, 0))).astype(BF16)
    w_out_slab = w_out[0].astype(BF16)
    meta_slab = jnp.concatenate([meta, _tile_rows(conv_w[0], 3, DH)], axis=0)
    wout_flight = _split_start(w_out_slab, "gather_w_out_start", per_peer=False)
    w_all, small_all = _all_gather([w_in_slab, meta_slab], "gather_w_in")

    w_t = _unshard_w_in(w_all)
    meta_full = jnp.transpose(small_all[:, :NM, :], (1, 0, 2)).reshape(NM, D)
    conv_w_full = jnp.transpose(small_all[:, NM:NM + 3, :DH], (1, 0, 2)).reshape(3, DA)

    lane_b = lambda p: jnp.broadcast_to(p.reshape(-1, DA, 1), (p.size // DA, DA, TB))
    cw_b = lane_b(conv_w_full)
    ga_b = lane_b(attn_norm_g)[0]
    gcn_b = lane_b(conv_norm_g)[0]

    u, proj_t, f_t, ktok, vtok = _inproj_fwd(x2, meta_full, norm_g + wout_flight[4][0, 0], w_t, L)
    cq, kaug, sg = _fgate_fwd(f_t, b_f.reshape(H, 1), ktok, L)
    o_t, lse = _attn_fwd(proj_t, kaug, cq, L)
    mix_t = _gate_fwd(o_t, proj_t, cw_b, ga_b, gcn_b, L)

    w_out_own, w_out_land = _split_wait(wout_flight, mix_t, "gather_w_out_wait", per_peer=False)
    w_out_full = _unshard_w_out(w_out_own, w_out_land)
    dout, dmix_t, dw_out, loss_part, dg_final = _outproj(
        mix_t, w_out_full, x2, meta_full, final_norm_g.reshape(1, D), tgt, L)
    dwo_flight = _split_start(dw_out.reshape(NDEV, D // NDEV, D), "exchange_dw_out_start", per_peer=True)
    do_t, dd, dg5_t, dga_p, dgc_p, dcw_p = _gate_bwd(dmix_t, o_t, proj_t, cw_b, ga_b + dwo_flight[4][0, 0], gcn_b, L)
    dq_t, dk_t, dv_t, dck, dcq = _attn_bwd(proj_t, kaug, vtok, do_t, lse, dd, cq, L)
    df_t, db_f = _fgate_bwd(dcq, dck, sg, L)
    dw_main, dw_f = _inproj_bwd_w(u, dq_t, dk_t, dv_t, dg5_t, df_t, L)
    dwi_flight = _split_start(_shard_w_in_grads(dw_main, dw_f), "exchange_dw_in_start", per_peer=True)
    grad_x, dmeta, dg_norm = _inproj_bwd_x(
        w_t, dq_t, dk_t, dv_t, dg5_t, df_t, dout, x2, meta_full, norm_g + dwi_flight[4][0, 0], L)
    small_parts = _pack_small_grads(dg_norm, dg_final, dga_p, dgc_p, dcw_p, db_f, dmeta, loss_part)
    small_flight = _split_start(small_parts, "exchange_small_start", per_peer=True)
    dwo_own, dwo_land = _split_wait(dwo_flight, small_flight[4], "exchange_dw_out_wait", per_peer=True)
    dwi_own, dwi_land = _split_wait(dwi_flight, dwo_land, "exchange_dw_in_wait", per_peer=True)

    big_out = _adamw_big(dwi_own, dwi_land, dwo_own, dwo_land,
                         w_in[0].T, m_w_in[0].T, v_w_in[0].T, w_out, m_w_out, v_w_out)
    g_w_in, d_w_in, nm_w_in, nv_w_in = [a.T[None] for a in big_out[:4]]
    g_w_out, d_w_out, nm_w_out, nv_w_out = big_out[4:]
    sm_own, sm_land = _split_wait(small_flight, big_out[4], "exchange_small_wait", per_peer=True)
    line = lambda a: a.reshape(1, D)
    small, loss = _adamw_small(sm_own, sm_land, dict(
        norm_g=(norm_g, m_norm_g, v_norm_g),
        final_norm_g=(line(final_norm_g), line(m_final_norm_g), line(v_final_norm_g)),
        attn_norm_g=(attn_norm_g, m_attn_norm_g, v_attn_norm_g),
        conv_norm_g=(conv_norm_g, m_conv_norm_g, v_conv_norm_g),
        b_f=(b_f, m_b_f, v_b_f), meta=(meta, m_meta, v_meta), conv_w=(conv_w, m_conv_w, v_conv_w)))
    small["final_norm_g"] = [a.reshape(D) for a in small["final_norm_g"]]
    order = ("meta", "norm_g", "w_in", "b_f", "conv_w", "attn_norm_g", "conv_norm_g", "w_out", "final_norm_g")
    groups = []
    for k, (wi, wo) in enumerate(((g_w_in, g_w_out), (d_w_in, d_w_out), (nm_w_in, nm_w_out), (nv_w_in, nv_w_out))):
        d = dict({n: small[n][k] for n in SMALL}, w_in=wi, w_out=wo)
        groups.append([d[n] for n in order])
    return (loss[0, 0], grad_x[None], *groups[0], *groups[1], *groups[2], *groups[3])
```

```python
import jax
import jax.numpy as jnp
from jax import lax
from jax.experimental import pallas as pl
from jax.experimental.pallas import tpu as pltpu

F32 = jnp.float32
BF16 = jnp.bfloat16

D = 1024
DA = 512
H = 8
DH = 64
NM = 16
TB = 128
P0 = TB - NM
TT = 3 * TB
HG = 8
NDEV = 8
NSEC = 8
DF = 16
DPROJ = NSEC * DA + DF
WSH = 513
WSHP = 528
WROWS = WSHP + D // NDEV
SROWS = 64
EPS = 1e-6
NEG = -1e30
LOG2E = 1.4426950408889634
LN2 = 0.6931471805599453
QSCALE = DH ** -0.5 * LOG2E
KA = 128
CB = 256
VMEM_LIMIT = 56 * 1024 * 1024

ADAM_LR = 0.001
ADAM_B1 = 0.9
ADAM_B2 = 0.999
ADAM_EPS = 1e-08
ADAM_WD = 0.01
ADAM_STEP = 10

NT_DIMS = (((1,), (1,)), ((), ()))
TN_DIMS = (((0,), (0,)), ((), ()))
MESH = pl.DeviceIdType.MESH


def _params(n_axes=1, vmem=VMEM_LIMIT):
    return pltpu.CompilerParams(dimension_semantics=("arbitrary",) * n_axes, vmem_limit_bytes=vmem)


def _dot(a, b, dims=None):
    if dims is None:
        return jnp.dot(a, b, preferred_element_type=F32)
    return lax.dot_general(a, b, dims, preferred_element_type=F32)


def _my_place():
    return lax.axis_index("x"), lax.axis_index("y"), lax.axis_index("c")


def _all_gather(xs, name):
    n = len(xs)

    def body(*refs):
        x_refs, out_refs = refs[:n], refs[n:2 * n]
        send_sems, recv_sems, local_sems = refs[2 * n:]
        mx, my, mc = _my_place()

        def across(px, py, pc, axis_a):
            flip_x = pc if axis_a else 1 - pc
            return (px + flip_x) % 2, (py + 1 - flip_x) % 2, pc

        def idx(p):
            return 4 * p[0] + 2 * p[1] + p[2]

        me, sib = (mx, my, mc), (mx, my, 1 - mc)
        a_nbr, b_nbr = across(*me, True), across(*me, False)
        diag = across(*b_nbr, True)
        sib_a, sib_b = across(*sib, True), across(*sib, False)
        sib_diag = across(*sib_b, True)

        waits = []
        for t in range(n):
            out_ref = out_refs[t]

            def copy(k, block, to, src=None, out_ref=out_ref, t=t):
                return pltpu.make_async_remote_copy(
                    src_ref=out_ref.at[idx(block)] if src is None else src, dst_ref=out_ref.at[idx(block)],
                    send_sem=send_sems.at[7 * t + k], recv_sem=recv_sems.at[7 * t + k],
                    device_id=to, device_id_type=MESH)

            mine = pltpu.make_async_copy(x_refs[t], out_ref.at[idx(me)], local_sems.at[t])
            mine.start()
            started = [copy(0, me, sib, src=x_refs[t]), copy(1, me, a_nbr, src=x_refs[t]),
                       copy(2, me, b_nbr, src=x_refs[t])]
            for cp in started:
                cp.start()
            waits.append((copy, mine, started))
        relays = ((1, a_nbr, ((3, b_nbr), (4, sib))), (2, b_nbr, ((5, sib),)), (3, diag, ((6, sib),)))
        for landed, block, onward in relays:
            for copy, _, started in waits:
                copy(landed, block, me).wait_recv()
                for k, to in onward:
                    started.append(copy(k, block, to))
                    started[-1].start()
        for copy, mine, started in waits:
            for k, block in ((0, sib), (4, sib_a), (5, sib_b), (6, sib_diag)):
                copy(k, block, me).wait_recv()
            for cp in started:
                cp.wait_send()
            mine.wait()

    any_spec = pl.BlockSpec(memory_space=pl.ANY)
    return pl.pallas_call(
        body, name=name,
        out_shape=[jax.ShapeDtypeStruct((NDEV,) + x.shape, x.dtype) for x in xs],
        in_specs=[any_spec] * n, out_specs=[any_spec] * n,
        scratch_shapes=[pltpu.SemaphoreType.DMA((7 * n,)), pltpu.SemaphoreType.DMA((7 * n,)),
                        pltpu.SemaphoreType.DMA((n,))],
    )(*xs)


_HBM = pl.BlockSpec(memory_space=pltpu.HBM)
_UNREAD = pl.BlockSpec(memory_space=pl.ANY)
_SEM = pl.BlockSpec(memory_space=pltpu.SEMAPHORE)
_EFFECT = pltpu.SideEffectType.DATAFLOW_SIDE_EFFECTING


def _peer_of(m, place):
    mx, my, mc = place
    return ((1 - mx) if m & 4 else mx, (1 - my) if m & 2 else my, (1 - mc) if m & 1 else mc)


def _split_copies(src_ref, land_ref, send_sems, recv_sems, per_peer, incoming):
    place = _my_place()
    me = 4 * place[0] + 2 * place[1] + place[2]
    out = []
    for m in range(1, NDEV):
        px, py, pc = _peer_of(m, place)
        peer = 4 * px + 2 * py + pc
        src = (src_ref.at[me] if incoming else src_ref.at[peer]) if per_peer else src_ref
        out.append(pltpu.make_async_remote_copy(
            src_ref=src, dst_ref=land_ref.at[peer if incoming else me],
            send_sem=send_sems.at[m - 1], recv_sem=recv_sems.at[m - 1],
            device_id=(px, py, pc), device_id_type=MESH))
    return out


def _split_start(src, name, per_peer):
    slab = src.shape[1:] if per_peer else src.shape

    def body(src_ref, land_ref, send_sems, recv_sems, src_thru, land_thru, token):
        for cp in _split_copies(src_ref, land_ref, send_sems, recv_sems, per_peer, incoming=False):
            cp.start()
        token[...] = jnp.zeros_like(token)

    return pl.pallas_call(
        body, name=name,
        out_shape=(pltpu.SemaphoreType.DMA((NDEV - 1,)), pltpu.SemaphoreType.DMA((NDEV - 1,)),
                   pltpu.HBM(src.shape, src.dtype), pltpu.HBM((NDEV,) + slab, src.dtype),
                   jax.ShapeDtypeStruct((8, TB), F32)),
        in_specs=(_HBM, _HBM), out_specs=(_SEM, _SEM, _HBM, _HBM, pl.BlockSpec(memory_space=pltpu.VMEM)),
        input_output_aliases={0: 2, 1: 3},
        compiler_params=pltpu.CompilerParams(has_side_effects=_EFFECT),
    )(pltpu.with_memory_space_constraint(src, pltpu.HBM),
      pltpu.with_memory_space_constraint(lax.empty((NDEV,) + slab, src.dtype), pltpu.HBM))


def _split_wait(handles, after, name, per_peer):
    send_sems, recv_sems, src_thru, land_thru, _ = handles

    def body(src_ref, land_ref, send_sems, recv_sems, after_ref, src_out, land_out):
        for cp in _split_copies(src_ref, land_ref, send_sems, recv_sems, per_peer, incoming=False):
            cp.wait_send()
        for cp in _split_copies(src_ref, land_ref, send_sems, recv_sems, per_peer, incoming=True):
            cp.wait_recv()

    return pl.pallas_call(
        body, name=name,
        out_shape=(pltpu.HBM(src_thru.shape, src_thru.dtype), pltpu.HBM(land_thru.shape, land_thru.dtype)),
        in_specs=(_HBM, _HBM, _SEM, _SEM, pl.BlockSpec(memory_space=pl.ANY)), out_specs=(_HBM, _HBM),
        input_output_aliases={0: 0, 1: 1},
        compiler_params=pltpu.CompilerParams(has_side_effects=_EFFECT),
    )(src_thru, land_thru, send_sems, recv_sems, after)


def _pick_slab(j, own_ref, land_ref, rows, per_peer=True):
    mx, my, mc = _my_place()
    me = 4 * mx + 2 * my + mc
    own = (lambda: own_ref[j, rows, :]) if per_peer else (lambda: own_ref[rows, :])
    return lax.cond(me == j, own, lambda: land_ref[j, rows, :])


def _h_block(t, x_ref, meta_ref):
    first = jnp.concatenate([jnp.zeros((P0, D), F32), meta_ref[...]], axis=0)
    return jnp.where(t == 0, first, x_ref[...])


def _x_specs3():
    return [pl.BlockSpec((TB, D), lambda j: (jnp.maximum(3 * j - 1, 0), 0)),
            pl.BlockSpec((TB, D), lambda j: (3 * j, 0)),
            pl.BlockSpec((TB, D), lambda j: (3 * j + 1, 0))]


def _h_tile(j, xa_ref, xb_ref, xc_ref, meta_ref):
    first = jnp.concatenate([jnp.zeros((P0, D), F32), meta_ref[...]], axis=0)
    return jnp.concatenate([jnp.where(j == 0, first, xa_ref[...]), xb_ref[...], xc_ref[...]], axis=0)


def _full_spec(shape):
    return pl.BlockSpec(shape, lambda *_: (0,) * len(shape))


def _sigmoid(z):
    return 1.0 / (1.0 + jnp.exp(-z))


def _lane_tiles_sum(x):
    out = x[:, :TB]
    for i in range(1, x.shape[1] // TB):
        out = out + x[:, i * TB:(i + 1) * TB]
    return out


def _inproj_fwd(x, meta_full, norm_g, w_t, L, after):
    nj = L // TT

    def body(xa_ref, xb_ref, xc_ref, meta_ref, g_ref, w_ref, _, u_ref, proj_ref, f_ref, ktok_ref, vtok_ref):
        hb = _h_tile(pl.program_id(0), xa_ref, xb_ref, xc_ref, meta_ref)
        r = lax.rsqrt(jnp.mean(hb * hb, axis=-1, keepdims=True) + EPS)
        u = (hb * r * g_ref[...]).astype(BF16)
        u_ref[...] = u
        for s in range(NSEC):
            p = _dot(u, w_ref[s * DA:(s + 1) * DA, :], NT_DIMS)
            if s == 0:
                p = p * QSCALE
            if s in (1, 2):
                tok_ref = ktok_ref if s == 1 else vtok_ref
                for h in range(H):
                    tok_ref[h] = p[:, h * DH:(h + 1) * DH].astype(BF16)
            proj_ref[s * DA:(s + 1) * DA, :] = p.T.astype(BF16)
        f_ref[...] = _dot(w_ref[NSEC * DA:DPROJ, :], u, NT_DIMS)[:H]

    return pl.pallas_call(
        body, name="inproj_fwd", grid=(nj,),
        in_specs=_x_specs3() + [_full_spec((NM, D)), _full_spec((1, D)), _full_spec((DPROJ, D)), _UNREAD],
        out_specs=[
            pl.BlockSpec((TT, D), lambda t: (t, 0)),
            pl.BlockSpec((NSEC * DA, TT), lambda t: (0, t)),
            pl.BlockSpec((H, TT), lambda t: (0, t)),
            pl.BlockSpec((H, TT, DH), lambda t: (0, t, 0)),
            pl.BlockSpec((H, TT, DH), lambda t: (0, t, 0)),
        ],
        out_shape=[
            jax.ShapeDtypeStruct((L, D), BF16),
            jax.ShapeDtypeStruct((NSEC * DA, L), BF16),
            jax.ShapeDtypeStruct((H, L), F32),
            jax.ShapeDtypeStruct((H, L, DH), BF16),
            jax.ShapeDtypeStruct((H, L, DH), BF16),
        ],
        compiler_params=_params(),
    )(x, x, x, meta_full, norm_g, w_t, after)


def _split3(x):
    hi = x.astype(BF16).astype(F32)
    r = x - hi
    mid = r.astype(BF16).astype(F32)
    return hi, mid, (r - mid).astype(BF16).astype(F32)


def _bias_rows(bias):
    one = jnp.ones((1, TT), F32)
    zero = jnp.zeros((1, TT), F32)
    parts = [zero] * 3 if bias is None else list(_split3(bias))
    return jnp.concatenate([one] * 3 + parts + [zero] * (DF - 6), axis=0).astype(BF16)


def _fgate_fwd(f_t, b_col, ktok, L):
    nb = L // TB

    def body(f_ref, b_ref, ktok_ref, cq_ref, kaug_ref, sg_ref):
        z = f_ref[...] + b_ref[...]
        idx = lax.broadcasted_iota(jnp.int32, (H, L), 1)
        real = idx >= P0
        lf = jnp.where(real, jnp.minimum(z, 0.0) - jnp.log1p(jnp.exp(-jnp.abs(z))), 0.0)
        sg_ref[...] = jnp.where(real, 1.0 / (1.0 + jnp.exp(z)), 0.0)
        c = lf
        s = 1
        while s < L:
            c = c + jnp.where(idx >= s, pltpu.roll(c, s, 1), 0.0)
            s *= 2
        c = c * LOG2E
        for h in range(H):
            cq_ref[h] = c[h:h + 1, :]
        hi, mid, lo = _split3(-jnp.where(real, c, -NEG))
        lane = lax.broadcasted_iota(jnp.int32, (TB, KA), 1)
        ones = jnp.ones((3, TB), F32)
        for h in range(H):
            for b in range(nb):
                blk = slice(b * TB, (b + 1) * TB)
                cols = jnp.concatenate([
                    jnp.zeros((DH, TB), F32), hi[h:h + 1, blk], mid[h:h + 1, blk], lo[h:h + 1, blk], ones,
                    jnp.zeros((KA - DH - 6, TB), F32)], axis=0).T
                k = jnp.concatenate([ktok_ref[h, blk, :].astype(F32), jnp.zeros((TB, KA - DH), F32)], axis=1)
                kaug_ref[h, blk, :] = jnp.where(lane < DH, k, cols).astype(BF16)

    return pl.pallas_call(
        body, name="fgate_fwd",
        out_shape=[
            jax.ShapeDtypeStruct((H, 1, L), F32),
            jax.ShapeDtypeStruct((H, L, KA), BF16),
            jax.ShapeDtypeStruct((H, L), F32),
        ],
        compiler_params=pltpu.CompilerParams(vmem_limit_bytes=VMEM_LIMIT),
    )(f_t, b_col, ktok)


def _causal_mask():
    r = lax.broadcasted_iota(jnp.int32, (TT, TT), 0)
    c = lax.broadcasted_iota(jnp.int32, (TT, TT), 1)
    return r <= c


def _attn_fwd(proj_t, kaug, cq, L):
    nq = L // TT

    def body(q_ref, qn_ref, kaug_ref, v_ref, cq_ref, o_ref, lse_ref,
             qa_scr, s_scr, cmax_scr, m_scr, p_scr, alpha_scr, acc_scr):
        j = pl.program_id(0)
        rows = [slice(g * DH, (g + 1) * DH) for g in range(HG)]
        ones = jnp.ones((DF, TT), BF16)

        def load_queries(ref):
            for g in range(HG):
                qa_scr[g] = jnp.concatenate(
                    [ref[rows[g], :], _bias_rows(None), jnp.zeros((KA - DH - DF, TT), BF16)], axis=0)

        def scores(kt, masked):
            k_off = pl.multiple_of(kt * TT, TT)
            for g in range(HG):
                s = _dot(kaug_ref[g, pl.ds(k_off, TT), :], qa_scr[g])
                if masked:
                    s = jnp.where(_causal_mask(), s, NEG)
                s_scr[g] = s
                cmax_scr[g] = jnp.max(s, axis=0, keepdims=True)

        def softmax():
            for g in range(HG):
                m_old = m_scr[g]
                m_new = jnp.maximum(m_old, cmax_scr[g])
                alpha_scr[g] = jnp.exp2(m_old - m_new)
                p_scr[g] = jnp.exp2(s_scr[g] - m_new).astype(BF16)
                m_scr[g] = m_new

        def weighted_sum(kt):
            k_off = pl.multiple_of(kt * TT, TT)
            for g in range(HG):
                v1 = jnp.concatenate([v_ref[rows[g], pl.ds(k_off, TT)], ones], axis=0)
                acc_scr[g] = alpha_scr[g] * acc_scr[g] + _dot(v1, p_scr[g])

        @pl.when(j == 0)
        def _():
            load_queries(q_ref)
            scores(0, True)

        m_scr[...] = jnp.full_like(m_scr, NEG)
        acc_scr[...] = jnp.zeros_like(acc_scr)

        @pl.when(j >= 1)
        def _():
            softmax()
            scores(j - 1, False)

        def step(i, c):
            weighted_sum(j - i + 1)
            softmax()
            scores(j - i - 1, False)
            return c

        lax.fori_loop(1, j, step, 0)

        @pl.when(j >= 1)
        def _():
            weighted_sum(1)

        @pl.when(j < nq - 1)
        def _():
            softmax()
            load_queries(qn_ref)
            scores(j + 1, True)
            weighted_sum(0)

        @pl.when(j == nq - 1)
        def _():
            softmax()
            weighted_sum(0)

        for g in range(HG):
            l = acc_scr[g, DH:DH + 1, :]
            o_ref[rows[g], :] = acc_scr[g, :DH, :] * (1.0 / l)
            lse_ref[g] = m_scr[g] + jnp.log2(l) + cq_ref[g]

    assert HG == H
    return pl.pallas_call(
        body, name="attn_fwd", grid=(nq,),
        in_specs=[
            pl.BlockSpec((DA, TT), lambda j: (0, j)),
            pl.BlockSpec((DA, TT), lambda j: (0, jnp.minimum(j + 1, nq - 1))),
            pl.BlockSpec((H, L, KA), lambda j: (0, 0, 0)),
            pl.BlockSpec((DA, L), lambda j: (2, 0)),
            pl.BlockSpec((H, 1, TT), lambda j: (0, 0, j)),
        ],
        out_specs=[
            pl.BlockSpec((DA, TT), lambda j: (0, j)),
            pl.BlockSpec((H, 1, TT), lambda j: (0, 0, j)),
        ],
        out_shape=[jax.ShapeDtypeStruct((DA, L), F32), jax.ShapeDtypeStruct((H, 1, L), F32)],
        scratch_shapes=[pltpu.VMEM((HG, KA, TT), BF16), pltpu.VMEM((HG, TT, TT), F32), pltpu.VMEM((HG, 1, TT), F32),
                        pltpu.VMEM((HG, 1, TT), F32), pltpu.VMEM((HG, TT, TT), BF16), pltpu.VMEM((HG, 1, TT), F32),
                        pltpu.VMEM((HG, DH + DF, TT), F32)],
        compiler_params=_params(),
    )(proj_t, proj_t, kaug, proj_t, cq)


def _gate_group(rows, o_ref, za_ref, gb_ref, gc_ref, xc_ref, zc_ref, gcp_ref, xcp_ref, cw_ref, ga_ref, gcn_ref, first):
    n_rep = TT // TB
    f32 = lambda r: r[rows, :].astype(F32)
    o, za, gb, gc, xc, zc = o_ref[rows, :], f32(za_ref), f32(gb_ref), f32(gc_ref), f32(xc_ref), f32(zc_ref)
    a = gc * xc
    a_prev = jnp.where(first, 0.0, f32(gcp_ref) * f32(xcp_ref))
    full = jnp.concatenate([a_prev, a], axis=1)
    a1 = pltpu.roll(full, 1, 1)[:, TB:]
    a2 = pltpu.roll(full, 2, 1)[:, TB:]
    w0 = jnp.tile(cw_ref[0, rows, :], (1, n_rep))
    w1 = jnp.tile(cw_ref[1, rows, :], (1, n_rep))
    w2 = jnp.tile(cw_ref[2, rows, :], (1, n_rep))
    cv = w0 * a2 + w1 * a1 + w2 * a
    e = gb * cv
    rc = lax.rsqrt(jnp.mean(e * e, axis=0, keepdims=True) + EPS)
    ec = e * rc
    ra = lax.rsqrt(jnp.mean(o * o, axis=0, keepdims=True) + EPS)
    oa = o * ra
    g_a = jnp.tile(ga_ref[rows, :], (1, n_rep))
    g_c = jnp.tile(gcn_ref[rows, :], (1, n_rep))
    sa = _sigmoid(za)
    sc = _sigmoid(zc)
    return dict(o=o, za=za, gb=gb, gc=gc, xc=xc, zc=zc, a=a, a1=a1, a2=a2, w0=w0, w1=w1, w2=w2, cv=cv, e=e,
                rc=rc, ec=ec, ra=ra, oa=oa, g_a=g_a, g_c=g_c, sa=sa, sc=sc)


def _gate_specs(nj, rev):
    def jj(i):
        return (nj - 1 - i) if rev else i

    def sec(s):
        return pl.BlockSpec((DA, TT), lambda i: (s, jj(i)))

    def halo(s):
        return pl.BlockSpec((DA, TB), lambda i: (s, jnp.maximum(3 * jj(i) - 1, 0)))

    return [pl.BlockSpec((DA, TT), lambda i: (0, jj(i))), sec(3), sec(4), sec(5), sec(6), sec(7), halo(5), halo(6),
            _full_spec((3, DA, TB)), _full_spec((DA, TB)), _full_spec((DA, TB))]


def _gate_fwd(o_t, proj_t, cw_b, ga_b, gcn_b, L):
    nj = L // TT

    def body(o_ref, za_ref, gb_ref, gc_ref, xc_ref, zc_ref, gcp_ref, xcp_ref, cw_ref, ga_ref, gcn_ref, mix_ref):
        j = pl.program_id(0)

        def group(h, c):
            r0 = pl.multiple_of(h * DH, DH)
            g = _gate_group(pl.ds(r0, DH), o_ref, za_ref, gb_ref, gc_ref, xc_ref, zc_ref, gcp_ref, xcp_ref,
                            cw_ref, ga_ref, gcn_ref, j == 0)
            mix_ref[pl.ds(r0, DH), :] = (g["oa"] * g["g_a"] * (g["za"] * g["sa"])).astype(BF16)
            mix_ref[pl.ds(DA + r0, DH), :] = (g["ec"] * g["g_c"] * (g["zc"] * g["sc"])).astype(BF16)
            return c

        lax.fori_loop(0, H, group, 0, unroll=2)

    return pl.pallas_call(
        body, name="gate_fwd", grid=(nj,),
        in_specs=_gate_specs(nj, False),
        out_specs=pl.BlockSpec((2 * DA, TT), lambda j: (0, j)),
        out_shape=jax.ShapeDtypeStruct((2 * DA, L), BF16),
        compiler_params=_params(),
    )(o_t, proj_t, proj_t, proj_t, proj_t, proj_t, proj_t, proj_t, cw_b, ga_b, gcn_b)


def _outproj(mix_t, w_out, x, meta_full, fng, target, L):
    nj = L // TT

    def body(mix_ref, w_ref, xa_ref, xb_ref, xc_ref, meta_ref, g_ref, ta_ref, tb_ref, tc_ref,
             dout_ref, dmix_ref, dwb_ref, loss_ref, dg_ref, dw_ref):
        t = pl.program_id(0)

        @pl.when(t == 0)
        def _():
            dw_ref[...] = jnp.zeros_like(dw_ref)
            loss_ref[...] = jnp.zeros_like(loss_ref)
            dg_ref[...] = jnp.zeros_like(dg_ref)

        mix = mix_ref[...]
        o = _dot(mix, w_ref[...], TN_DIMS) + _h_tile(t, xa_ref, xb_ref, xc_ref, meta_ref)
        r = lax.rsqrt(jnp.mean(o * o, axis=-1, keepdims=True) + EPS)
        g = g_ref[...]
        orn = o * r
        tgt = jnp.concatenate([ta_ref[...], tb_ref[...], tc_ref[...]], axis=0)
        row = lax.broadcasted_iota(jnp.int32, (TT, 1), 0)
        real = jnp.where((t > 0) | (row >= TB), 1.0, 0.0)
        diff = (orn * g - tgt) * real
        loss_ref[...] += 0.5 * jnp.sum(diff * diff) * (1.0 / D)
        dy = diff * (1.0 / D)
        dg_ref[...] += jnp.sum(dy * orn, axis=0, keepdims=True)
        gy = dy * g
        dout = r * gy - orn * (r * jnp.mean(gy * orn, axis=-1, keepdims=True))
        dout_ref[...] = dout
        db = dout.astype(BF16)
        dmix_ref[...] = _dot(db, w_ref[...], NT_DIMS).T.astype(BF16)
        dw_ref[...] += _dot(mix, db)

        @pl.when(t == nj - 1)
        def _():
            dwb_ref[...] = dw_ref[...].astype(BF16)

    return pl.pallas_call(
        body, name="outproj", grid=(nj,),
        in_specs=[pl.BlockSpec((D, TT), lambda t: (0, t)), _full_spec((D, D))] + _x_specs3()
                 + [_full_spec((NM, D)), _full_spec((1, D))] + _x_specs3(),
        out_specs=[pl.BlockSpec((TT, D), lambda t: (t, 0)), pl.BlockSpec((D, TT), lambda t: (0, t)),
                   _full_spec((D, D)), _full_spec((1, 1)), _full_spec((1, D))],
        out_shape=[jax.ShapeDtypeStruct((L, D), F32), jax.ShapeDtypeStruct((D, L), BF16),
                   jax.ShapeDtypeStruct((D, D), BF16), jax.ShapeDtypeStruct((1, 1), F32),
                   jax.ShapeDtypeStruct((1, D), F32)],
        scratch_shapes=[pltpu.VMEM((D, D), F32)],
        compiler_params=_params(),
    )(mix_t, w_out, x, x, x, meta_full, fng, target, target, target)


def _gate_bwd(dmix_t, o_t, proj_t, cw_b, ga_b, gcn_b, L, after):
    nj = L // TT

    def body(dmix_ref, o_ref, za_ref, gb_ref, gc_ref, xc_ref, zc_ref, gcp_ref, xcp_ref, cw_ref, ga_ref, gcn_ref, _,
             do_ref, dd_ref, dg5_ref, dga_ref, dgc_ref, dcw_ref, carry_ref):
        i = pl.program_id(0)
        j = nj - 1 - i

        @pl.when(i == 0)
        def _():
            carry_ref[...] = jnp.zeros_like(carry_ref)
            dga_ref[...] = jnp.zeros_like(dga_ref)
            dgc_ref[...] = jnp.zeros_like(dgc_ref)
            dcw_ref[...] = jnp.zeros_like(dcw_ref)

        def group(h, c):
            r0 = pl.multiple_of(h * DH, DH)
            rows = pl.ds(r0, DH)
            sec = lambda s: pl.ds(s * DA + r0, DH)
            g = _gate_group(rows, o_ref, za_ref, gb_ref, gc_ref, xc_ref, zc_ref, gcp_ref, xcp_ref,
                            cw_ref, ga_ref, gcn_ref, j == 0)
            o, za, gb, gc, xc, zc, sa, sc = (g[n] for n in ("o", "za", "gb", "gc", "xc", "zc", "sa", "sc"))
            dya = dmix_ref[rows, :].astype(F32)
            dyc = dmix_ref[pl.ds(DA + r0, DH), :].astype(F32)

            dn = dya * (za * sa)
            dg5_ref[sec(0), :] = (dya * (g["oa"] * g["g_a"]) * (sa * (1.0 + za * (1.0 - sa)))).astype(BF16)
            dga_ref[rows, :] += _lane_tiles_sum(dn * g["oa"])
            dng = dn * g["g_a"]
            mean_a = jnp.mean(dng * g["oa"], axis=0, keepdims=True)
            do = (dng - g["oa"] * mean_a) * g["ra"]
            do_ref[rows, :] = do.astype(BF16)
            dd_ref[h] = jnp.sum(do * o, axis=0, keepdims=True)

            dnc = dyc * (zc * sc)
            dg5_ref[sec(4), :] = (dyc * (g["ec"] * g["g_c"]) * (sc * (1.0 + zc * (1.0 - sc)))).astype(BF16)
            dgc_ref[rows, :] += _lane_tiles_sum(dnc * g["ec"])
            dncg = dnc * g["g_c"]
            mean_c = jnp.mean(dncg * g["ec"], axis=0, keepdims=True)
            de = (dncg - g["ec"] * mean_c) * g["rc"]
            dg5_ref[sec(1), :] = (de * g["cv"]).astype(BF16)
            dcv = de * gb
            full = jnp.concatenate([dcv, carry_ref[rows, :]], axis=1)
            d1 = pltpu.roll(full, TT + TB - 1, 1)[:, :TT]
            d2 = pltpu.roll(full, TT + TB - 2, 1)[:, :TT]
            carry_ref[rows, :] = dcv[:, :TB]
            da = g["w2"] * dcv + g["w1"] * d1 + g["w0"] * d2
            dg5_ref[sec(2), :] = (da * xc).astype(BF16)
            dg5_ref[sec(3), :] = (da * gc).astype(BF16)
            dcw_ref[0, rows, :] += _lane_tiles_sum(dcv * g["a2"])
            dcw_ref[1, rows, :] += _lane_tiles_sum(dcv * g["a1"])
            dcw_ref[2, rows, :] += _lane_tiles_sum(dcv * g["a"])
            return c

        lax.fori_loop(0, H, group, 0, unroll=2)

    rj = lambda i: nj - 1 - i
    return pl.pallas_call(
        body, name="gate_bwd", grid=(nj,),
        in_specs=[pl.BlockSpec((2 * DA, TT), lambda i: (0, rj(i)))] + _gate_specs(nj, True) + [_UNREAD],
        out_specs=[
            pl.BlockSpec((DA, TT), lambda i: (0, rj(i))),
            pl.BlockSpec((H, 1, TT), lambda i: (0, 0, rj(i))),
            pl.BlockSpec((5 * DA, TT), lambda i: (0, rj(i))),
            _full_spec((DA, TB)), _full_spec((DA, TB)), _full_spec((3, DA, TB)),
        ],
        out_shape=[
            jax.ShapeDtypeStruct((DA, L), BF16),
            jax.ShapeDtypeStruct((H, 1, L), F32),
            jax.ShapeDtypeStruct((5 * DA, L), BF16),
            jax.ShapeDtypeStruct((DA, TB), F32),
            jax.ShapeDtypeStruct((DA, TB), F32),
            jax.ShapeDtypeStruct((3, DA, TB), F32),
        ],
        scratch_shapes=[pltpu.VMEM((DA, TB), F32)],
        compiler_params=_params(),
    )(dmix_t, o_t, proj_t, proj_t, proj_t, proj_t, proj_t, proj_t, proj_t, cw_b, ga_b, gcn_b, after)


def _attn_bwd(proj_t, kaug, vtok, do_t, lse, dd, cq, L):
    nk = L // TT

    def body(q_ref, kaug_ref, vtok_ref, kt_ref, do_ref, lse_ref, dd_ref, cq_ref,
             dq_ref, dk_ref, dv_ref, dck_ref, dcq_ref, dq_acc, kt1_scr, s_scr, dp_scr, dv_scr, dk_scr):
        i = pl.program_id(0)
        rows = [slice(g * DH, (g + 1) * DH) for g in range(HG)]
        ones = jnp.ones((DF, TT), BF16)
        zpad = jnp.zeros((KA - DH - DF, TT), BF16)
        for g in range(HG):
            kt1_scr[g] = jnp.concatenate([kt_ref[rows[g], :], ones], axis=0)
        dv_scr[...] = jnp.zeros_like(dv_scr)
        dk_scr[...] = jnp.zeros_like(dk_scr)

        def q_rows(g, q_off):
            bias = cq_ref[g, :, pl.ds(q_off, TT)] - lse_ref[g, :, pl.ds(q_off, TT)]
            return jnp.concatenate([q_ref[rows[g], pl.ds(q_off, TT)], _bias_rows(bias)], axis=0)

        def scores(jq, masked):
            q_off = pl.multiple_of(jq * TT, TT)
            for g in range(HG):
                s = _dot(kaug_ref[g], jnp.concatenate([q_rows(g, q_off), zpad], axis=0))
                if masked:
                    s = jnp.where(_causal_mask(), s, NEG)
                s_scr[g] = s
                dp_scr[g] = _dot(vtok_ref[g], do_ref[rows[g], pl.ds(q_off, TT)])

        def grads(jq):
            q_off = pl.multiple_of(jq * TT, TT)
            for g in range(HG):
                p = jnp.exp2(s_scr[g])
                ds = (p * (dp_scr[g] - dd_ref[g, :, pl.ds(q_off, TT)])).astype(BF16)
                do1 = jnp.concatenate([do_ref[rows[g], pl.ds(q_off, TT)], jnp.zeros((KA - DH, TT), BF16)], axis=0)
                q1 = jnp.concatenate([q_rows(g, q_off), zpad], axis=0)
                dv_scr[g] += _dot(p.astype(BF16), do1, NT_DIMS)
                dk_scr[g] += _dot(ds, q1, NT_DIMS)
                dq_acc[g, :, pl.ds(q_off, TT)] += _dot(kt1_scr[g], ds)

        @pl.when(i == 0)
        def _():
            dq_acc[...] = jnp.zeros_like(dq_acc)

        scores(i, True)

        def step(jq, c):
            grads(jq)
            scores(jq + 1, False)
            return c

        lax.fori_loop(i, nk - 1, step, 0)
        grads(nk - 1)
        for g in range(HG):
            dv_ref[rows[g], :] = dv_scr[g].T[:DH, :].astype(BF16)
            dk_t = dk_scr[g].T
            dk_ref[rows[g], :] = (dk_t[:DH, :] * LN2).astype(BF16)
            dck_ref[g] = dk_t[DH:DH + 1, :]

        @pl.when(i == nk - 1)
        def _():
            for g in range(HG):
                dq_ref[rows[g], :] = (dq_acc[g, :DH, :] * (DH ** -0.5)).astype(BF16)
                dcq_ref[g] = dq_acc[g, DH:DH + 1, :]

    assert HG == H
    head = lambda i: (0, 0)
    row = lambda i: (0, 0, 0)
    return pl.pallas_call(
        body, name="attn_bwd", grid=(nk,),
        in_specs=[
            pl.BlockSpec((DA, L), head),
            pl.BlockSpec((H, TT, KA), lambda i: (0, i, 0)),
            pl.BlockSpec((H, TT, DH), lambda i: (0, i, 0)),
            pl.BlockSpec((DA, TT), lambda i: (1, i)),
            pl.BlockSpec((DA, L), head),
            pl.BlockSpec((H, 1, L), row), pl.BlockSpec((H, 1, L), row), pl.BlockSpec((H, 1, L), row),
        ],
        out_specs=[
            pl.BlockSpec((DA, L), head),
            pl.BlockSpec((DA, TT), lambda i: (0, i)),
            pl.BlockSpec((DA, TT), lambda i: (0, i)),
            pl.BlockSpec((H, 1, TT), lambda i: (0, 0, i)),
            pl.BlockSpec((H, 1, L), row),
        ],
        out_shape=[jax.ShapeDtypeStruct((DA, L), BF16), jax.ShapeDtypeStruct((DA, L), BF16),
                   jax.ShapeDtypeStruct((DA, L), BF16), jax.ShapeDtypeStruct((H, 1, L), F32),
                   jax.ShapeDtypeStruct((H, 1, L), F32)],
        scratch_shapes=[
            pltpu.VMEM((HG, DH + DF, L), F32),
            pltpu.VMEM((HG, DH + DF, TT), BF16),
            pltpu.VMEM((HG, TT, TT), F32), pltpu.VMEM((HG, TT, TT), F32),
            pltpu.VMEM((HG, TT, KA), F32), pltpu.VMEM((HG, TT, KA), F32)],
        compiler_params=_params(),
    )(proj_t, kaug, vtok, proj_t, do_t, lse, dd, cq)


def _fgate_bwd(dcq, dck, sg, L):
    def body(dcq_ref, dck_ref, sg_ref, df_ref, db_ref):
        dc = jnp.concatenate([dcq_ref[h] - dck_ref[h] for h in range(H)], axis=0)
        idx = lax.broadcasted_iota(jnp.int32, (H, L), 1)
        r = dc
        s = 1
        while s < L:
            r = r + jnp.where(idx + s < L, pltpu.roll(r, L - s, 1), 0.0)
            s *= 2
        df = r * sg_ref[...]
        db_ref[...] = jnp.broadcast_to(jnp.sum(df, axis=1, keepdims=True), (H, TB))
        df_ref[...] = jnp.concatenate([df, jnp.zeros((DF - H, L), F32)], axis=0).astype(BF16)

    return pl.pallas_call(
        body, name="fgate_bwd",
        out_shape=[jax.ShapeDtypeStruct((DF, L), BF16), jax.ShapeDtypeStruct((H, TB), F32)],
        compiler_params=pltpu.CompilerParams(vmem_limit_bytes=VMEM_LIMIT),
    )(dcq, dck, sg)


def _inproj_bwd_x(w, dq_t, dk_t, dv_t, dg5_t, df_t, dout, x, meta_full, norm_g, L, after):
    nj = L // TT
    seq = x.shape[0]

    def body(w_ref, dq_ref, dk_ref, dv_ref, dg5_ref, df_ref, dout_ref, xa_ref, xb_ref, xc_ref, meta_ref, g_ref, _,
             gx_ref, dmeta_ref, dg_ref, dh_scr, sems):
        j = pl.program_id(0)
        slot = j % 2

        def copy_out(step, slot_):
            first = pltpu.make_async_copy(dh_scr.at[slot_, pl.ds(TB, TT - TB)], gx_ref.at[pl.ds(0, TT - TB)],
                                          sems.at[slot_])
            later = pltpu.make_async_copy(dh_scr.at[slot_], gx_ref.at[pl.ds(step * TT - TB, TT)], sems.at[slot_])
            return first, later

        @pl.when(j == 0)
        def _():
            dg_ref[...] = jnp.zeros_like(dg_ref)

        du = _dot(dq_ref[...], w_ref[0:DA, :], TN_DIMS)
        du += _dot(dk_ref[...], w_ref[DA:2 * DA, :], TN_DIMS)
        du += _dot(dv_ref[...], w_ref[2 * DA:3 * DA, :], TN_DIMS)
        du += _dot(dg5_ref[...], w_ref[3 * DA:NSEC * DA, :], TN_DIMS)
        du += _dot(df_ref[...], w_ref[NSEC * DA:DPROJ, :], TN_DIMS)
        hb = _h_tile(j, xa_ref, xb_ref, xc_ref, meta_ref)
        r = lax.rsqrt(jnp.mean(hb * hb, axis=-1, keepdims=True) + EPS)
        hn = hb * r
        dg_ref[...] += jnp.sum(du * hn, axis=0, keepdims=True)
        gu = du * g_ref[...]
        dh = dout_ref[...] + r * gu - hn * (r * jnp.mean(gu * hn, axis=-1, keepdims=True))

        dh_scr[slot] = dh

        @pl.when(j == 0)
        def _():
            dmeta_ref[...] = dh[P0:TB, :]
            copy_out(0, 0)[0].start()

        @pl.when(j >= 1)
        def _():
            copy_out(j, slot)[1].start()

        @pl.when(j == 1)
        def _():
            copy_out(0, 0)[0].wait()

        @pl.when(j >= 2)
        def _():
            copy_out(j - 1, 1 - slot)[1].wait()

        @pl.when(j == nj - 1)
        def _():
            copy_out(j, slot)[0 if nj == 1 else 1].wait()

    blk = lambda rows: pl.BlockSpec((rows, TT), lambda j: (0, j))
    return pl.pallas_call(
        body, name="inproj_bwd_x", grid=(nj,),
        in_specs=[_full_spec((DPROJ, D)), blk(DA), blk(DA), blk(DA), blk(5 * DA), blk(DF),
                  pl.BlockSpec((TT, D), lambda j: (j, 0))] + _x_specs3()
                 + [_full_spec((NM, D)), _full_spec((1, D)), _UNREAD],
        out_specs=[pl.BlockSpec(memory_space=pl.ANY), _full_spec((NM, D)), _full_spec((1, D))],
        out_shape=[jax.ShapeDtypeStruct((seq, D), F32), jax.ShapeDtypeStruct((NM, D), F32),
                   jax.ShapeDtypeStruct((1, D), F32)],
        scratch_shapes=[pltpu.VMEM((2, TT, D), F32), pltpu.SemaphoreType.DMA((2,))],
        compiler_params=_params(),
    )(w, dq_t, dk_t, dv_t, dg5_t, df_t, dout, x, x, x, meta_full, norm_g, after)


def _inproj_bwd_w(u, dq_t, dk_t, dv_t, dg5_t, df_t, L):
    def body(u_ref, dq_ref, dk_ref, dv_ref, dg5_ref, df_ref, dw_ref, dwf_ref):
        s = pl.program_id(0)
        u_all = u_ref[...]

        @pl.when(s < 5)
        def _():
            dw_ref[...] = _dot(dg5_ref[...], u_all)

        for step, ref in ((5, dq_ref), (6, dk_ref), (7, dv_ref)):
            @pl.when(s == step)
            def _(ref=ref):
                dw_ref[...] = _dot(ref[...], u_all)

        @pl.when(s == NSEC - 1)
        def _():
            dwf_ref[...] = _dot(df_ref[...], u_all)

    once = lambda shape: pl.BlockSpec(shape, lambda s: (0, 0), pipeline_mode=pl.Buffered(1))
    return pl.pallas_call(
        body, name="inproj_bwd_w", grid=(NSEC,),
        in_specs=[
            once((L, D)), once((DA, L)), once((DA, L)), once((DA, L)),
            pl.BlockSpec((DA, L), lambda s: (jnp.minimum(s, 4), 0)),
            once((DF, L)),
        ],
        out_specs=[pl.BlockSpec((DA, D), lambda s: (jnp.where(s < 5, s + 3, s - 5), 0)), _full_spec((DF, D))],
        out_shape=[jax.ShapeDtypeStruct((NSEC * DA, D), F32), jax.ShapeDtypeStruct((DF, D), F32)],
        compiler_params=_params(),
    )(u, dq_t, dk_t, dv_t, dg5_t, df_t)


def _adamw(w, g, m, v):
    m = ADAM_B1 * m + (1.0 - ADAM_B1) * g
    v = ADAM_B2 * v + (1.0 - ADAM_B2) * (g * g)
    m_hat = m / (1.0 - ADAM_B1 ** ADAM_STEP)
    v_hat = v / (1.0 - ADAM_B2 ** ADAM_STEP)
    delta = -ADAM_LR * (m_hat / (jnp.sqrt(v_hat) + ADAM_EPS) + ADAM_WD * w)
    return delta, m, v


def _adamw_big(own_in, land_in, own_out, land_out, w_in_t, m_in_t, v_in_t, w_out, m_out, v_out):
    cb = CB
    e_sh = D // NDEV
    in_shape = jax.ShapeDtypeStruct(w_in_t.shape, F32)
    out_shape = jax.ShapeDtypeStruct(w_out.shape, F32)

    def total(own_ref, land_ref, rows):
        g = _pick_slab(0, own_ref, land_ref, rows).astype(F32)
        for j in range(1, NDEV):
            g = g + _pick_slab(j, own_ref, land_ref, rows).astype(F32)
        return g

    def body(oi_ref, li_ref, oo_ref, lo_ref, wi_ref, mi_ref, vi_ref, wo_ref, mo_ref, vo_ref,
             gi, di, mi, vi, go, do, mo, vo):
        g = total(oi_ref, li_ref, slice(0, WSHP))[:WSH]
        d, mn, vn = _adamw(wi_ref[...], g, mi_ref[...], vi_ref[...])
        gi[...], di[...], mi[...], vi[...] = g, d, mn, vn
        g = total(oo_ref, lo_ref, slice(0, e_sh))
        d, mn, vn = _adamw(wo_ref[0], g, mo_ref[0], vo_ref[0])
        go[0], do[0], mo[0], vo[0] = g, d, mn, vn

    slab = lambda rows: pl.BlockSpec((NDEV, rows, cb), lambda i: (0, 0, i))
    ispec = pl.BlockSpec((WSH, cb), lambda i: (0, i))
    ospec = pl.BlockSpec((1, e_sh, cb), lambda i: (0, 0, i))
    return pl.pallas_call(
        body, name="adamw_big", grid=(D // cb,),
        in_specs=[slab(WSHP), slab(WSHP), slab(e_sh), slab(e_sh), ispec, ispec, ispec, ospec, ospec, ospec],
        out_specs=[ispec] * 4 + [ospec] * 4, out_shape=[in_shape] * 4 + [out_shape] * 4,
        compiler_params=_params(),
    )(own_in, land_in, own_out, land_out, w_in_t, m_in_t, v_in_t, w_out, m_out, v_out)


F0 = 3 * DA


def _unshard_w_out(own, land):
    e_sh = D // NDEV

    def body(own_ref, land_ref, wo_ref):
        for j in range(NDEV):
            wo_ref[j * e_sh:(j + 1) * e_sh, :] = _pick_slab(j, own_ref, land_ref, slice(0, e_sh), per_peer=False)

    return pl.pallas_call(
        body, name="unshard_w_out", grid=(D // CB,),
        in_specs=[pl.BlockSpec((e_sh, CB), lambda i: (0, i)), pl.BlockSpec((NDEV, e_sh, CB), lambda i: (0, 0, i))],
        out_specs=pl.BlockSpec((D, CB), lambda i: (0, i)),
        out_shape=jax.ShapeDtypeStruct((D, D), BF16),
        compiler_params=_params(),
    )(own, land)


def _unshard_w_in(w_all, small_all, attn_gain, conv_gain):
    def body(w_ref, small_ref, ga_ref, gc_ref, wt_ref, meta_ref, cwb_ref, gab_ref, gcb_ref):
        i = pl.program_id(0)
        for k in range(CB // TB):
            meta_ref[:, k * TB:(k + 1) * TB] = small_ref[(CB // TB) * i + k, 0:NM, :]

        @pl.when(i == 0)
        def _():
            per_row = lambda line: jnp.broadcast_to(line, (TB, DA)).T
            cw = jnp.concatenate([small_ref[j, NM:NM + 3, 0:DH] for j in range(NDEV)], axis=1)
            for k in range(3):
                cwb_ref[k] = per_row(cw[k:k + 1, :])
            gab_ref[...] = per_row(ga_ref[...])
            gcb_ref[...] = per_row(gc_ref[...])

        def ref_rows(lo, hi):
            pieces, r = [], lo
            while r < hi:
                sh, off = divmod(r, WSH)
                n = min(hi - r, WSH - off)
                pieces.append(w_ref[sh, off:off + n, :])
                r += n
            return pieces

        for s in range(NSEC):
            lo = s * DA if s < 3 else s * DA + H
            wt_ref[s * DA:(s + 1) * DA, :] = jnp.concatenate(ref_rows(lo, lo + DA), axis=0)
        wt_ref[NSEC * DA:DPROJ, :] = jnp.concatenate(
            ref_rows(F0, F0 + H) + [jnp.zeros((DF - H, CB), BF16)], axis=0)

    return pl.pallas_call(
        body, name="unshard_w_in", grid=(D // CB,),
        in_specs=[pl.BlockSpec((NDEV, WSHP, CB), lambda i: (0, 0, i)), _full_spec(small_all.shape),
                  _full_spec((1, DA)), _full_spec((1, DA))],
        out_specs=[pl.BlockSpec((DPROJ, CB), lambda i: (0, i)), pl.BlockSpec((NM, CB), lambda i: (0, i)),
                   _full_spec((3, DA, TB)), _full_spec((DA, TB)), _full_spec((DA, TB))],
        out_shape=[jax.ShapeDtypeStruct((DPROJ, D), BF16), jax.ShapeDtypeStruct((NM, D), F32),
                   jax.ShapeDtypeStruct((3, DA, TB), F32), jax.ShapeDtypeStruct((DA, TB), F32),
                   jax.ShapeDtypeStruct((DA, TB), F32)],
        compiler_params=_params(),
    )(w_all, small_all, attn_gain, conv_gain)


def _shard_w_in_grads(dw_main, dw_f):
    def body(dm_ref, df_ref, p_ref):
        def ref_rows(lo, hi):
            pieces, r = [], lo
            while r < hi:
                if r < F0:
                    n = min(hi, F0) - r
                    pieces.append(dm_ref[r:r + n, :])
                elif r < F0 + H:
                    n = min(hi, F0 + H) - r
                    pieces.append(df_ref[r - F0:r - F0 + n, :])
                else:
                    n = hi - r
                    pieces.append(dm_ref[r - H:r - H + n, :])
                r += n
            return pieces

        for i in range(NDEV):
            rows = jnp.concatenate(ref_rows(i * WSH, (i + 1) * WSH) + [jnp.zeros((WSHP - WSH, CB), F32)], axis=0)
            p_ref[i] = rows.astype(BF16)

    col = lambda rows: pl.BlockSpec((rows, CB), lambda i: (0, i))
    return pl.pallas_call(
        body, name="shard_w_in_grads", grid=(D // CB,),
        in_specs=[col(NSEC * DA), col(DF)],
        out_specs=pl.BlockSpec((NDEV, WSHP, CB), lambda i: (0, 0, i)),
        out_shape=jax.ShapeDtypeStruct((NDEV, WSHP, D), BF16),
        compiler_params=_params(),
    )(dw_main, dw_f)


SMALL = ("norm_g", "final_norm_g", "attn_norm_g", "conv_norm_g", "b_f", "meta", "conv_w")


def _as_rows(x):
    return jnp.concatenate([x[:, r * TB:(r + 1) * TB] for r in range(x.shape[1] // TB)], axis=0)


def _as_line(rows):
    return jnp.concatenate([rows[r:r + 1, :] for r in range(rows.shape[0])], axis=1)


def _pad_rows(x, n=8):
    return jnp.concatenate([x, jnp.zeros((n - x.shape[0], x.shape[1]), F32)], axis=0)


def _tile_rows(a, rows, lanes=TB):
    a = a.reshape(rows, lanes)
    return jnp.pad(a, ((0, -rows % 8), (0, TB - lanes)))


def _pack_small_grads(dg_norm, dg_final, dga_p, dgc_p, dcw_p, db_b, dmeta, loss):
    def body(dgn_ref, dgf_ref, dga_ref, dgc_ref, dcw_ref, db_ref, dmeta_ref, loss_ref, out_ref):
        def lane_sums(p):
            return jnp.sum(p.T, axis=0, keepdims=True)

        lane = lax.broadcasted_iota(jnp.int32, (1, TB), 1)
        b_row = jnp.where(lane == H, loss_ref[...], 0.0)
        for h in range(H):
            b_row = b_row + jnp.where(lane == h, db_ref[h:h + 1, :], 0.0)
        common = jnp.concatenate([
            _as_rows(dgn_ref[...]), _as_rows(dgf_ref[...]), _pad_rows(_as_rows(lane_sums(dga_ref[...]))),
            _pad_rows(_as_rows(lane_sums(dgc_ref[...]))), _pad_rows(b_row)], axis=0)
        dcw = [lane_sums(dcw_ref[k]) for k in range(3)]
        for j in range(NDEV):
            cw = jnp.concatenate(
                [jnp.concatenate([r[:, j * DH:(j + 1) * DH], jnp.zeros((1, TB - DH), F32)], axis=1) for r in dcw],
                axis=0)
            out_ref[j] = jnp.concatenate([common, dmeta_ref[:, j * TB:(j + 1) * TB], _pad_rows(cw)], axis=0)

    return pl.pallas_call(
        body, name="pack_small_grads", out_shape=jax.ShapeDtypeStruct((NDEV, SROWS, TB), F32),
    )(dg_norm, dg_final, dga_p, dgc_p, dcw_p, db_b, dmeta, loss)


def _adamw_small(own, land, params):
    flat = [a for n in SMALL for a in params[n]]

    def body(*refs):
        own_ref, land_ref = refs[:2]
        ins = refs[2:2 + 3 * len(SMALL)]
        outs = refs[2 + 3 * len(SMALL):]
        g = _pick_slab(0, own_ref, land_ref, slice(0, SROWS))
        for j in range(1, NDEV):
            g = g + _pick_slab(j, own_ref, land_ref, slice(0, SROWS))
        grads = dict(
            norm_g=_as_line(g[0:8]), final_norm_g=_as_line(g[8:16]), attn_norm_g=_as_line(g[16:20]),
            conv_norm_g=_as_line(g[24:28]), b_f=g[32:33, :H], meta=g[40:56], conv_w=g[56:59, :DH][None])
        for i, n in enumerate(SMALL):
            w_ref, m_ref, v_ref = ins[3 * i:3 * i + 3]
            d, mn, vn = _adamw(w_ref[...], grads[n], m_ref[...], v_ref[...])
            for o_ref, val in zip(outs[4 * i:4 * i + 4], (grads[n], d, mn, vn)):
                o_ref[...] = val
        outs[-1][...] = g[32:33, H:H + 1]

    shapes = [jax.ShapeDtypeStruct(params[n][0].shape, F32) for n in SMALL for _ in range(4)]
    res = pl.pallas_call(
        body, name="adamw_small", out_shape=shapes + [jax.ShapeDtypeStruct((1, 1), F32)],
    )(own, land, *flat)
    return {n: res[4 * i:4 * i + 4] for i, n in enumerate(SMALL)}, res[-1]


def kernel(x, meta, norm_g, w_in, b_f, conv_w, attn_norm_g, conv_norm_g, w_out, final_norm_g, loss_target, m_meta, m_norm_g, m_w_in, m_b_f, m_conv_w, m_attn_norm_g, m_conv_norm_g, m_w_out, m_final_norm_g, v_meta, v_norm_g, v_w_in, v_b_f, v_conv_w, v_attn_norm_g, v_conv_norm_g, v_w_out, v_final_norm_g):
    seq = x.shape[1]
    L = seq + TB
    assert x.shape == (1, seq, D) and L % TT == 0 and w_in.shape == (1, D, WSH)
    x2 = x[0]
    tgt = loss_target[0]

    w_in_slab = jnp.pad(w_in[0].T, ((0, WSHP - WSH), (0, 0))).astype(BF16)
    w_out_slab = w_out[0].astype(BF16)
    meta_slab = jnp.concatenate([meta, _tile_rows(conv_w[0], 3, DH)], axis=0)
    wout_flight = _split_start(w_out_slab, "gather_w_out_start", per_peer=False)
    w_all, small_all = _all_gather([w_in_slab, meta_slab], "gather_w_in")

    w_t, meta_full, cw_b, ga_b, gcn_b = _unshard_w_in(w_all, small_all, attn_norm_g, conv_norm_g)

    u, proj_t, f_t, ktok, vtok = _inproj_fwd(x2, meta_full, norm_g, w_t, L, after=wout_flight[4])
    cq, kaug, sg = _fgate_fwd(f_t, b_f.reshape(H, 1), ktok, L)
    o_t, lse = _attn_fwd(proj_t, kaug, cq, L)
    mix_t = _gate_fwd(o_t, proj_t, cw_b, ga_b, gcn_b, L)

    w_out_own, w_out_land = _split_wait(wout_flight, mix_t, "gather_w_out_wait", per_peer=False)
    w_out_full = _unshard_w_out(w_out_own, w_out_land)
    dout, dmix_t, dw_out, loss_part, dg_final = _outproj(
        mix_t, w_out_full, x2, meta_full, final_norm_g.reshape(1, D), tgt, L)
    dwo_flight = _split_start(dw_out.reshape(NDEV, D // NDEV, D), "exchange_dw_out_start", per_peer=True)
    do_t, dd, dg5_t, dga_p, dgc_p, dcw_p = _gate_bwd(dmix_t, o_t, proj_t, cw_b, ga_b, gcn_b, L, after=dwo_flight[4])
    dq_t, dk_t, dv_t, dck, dcq = _attn_bwd(proj_t, kaug, vtok, do_t, lse, dd, cq, L)
    df_t, db_f = _fgate_bwd(dcq, dck, sg, L)
    dw_main, dw_f = _inproj_bwd_w(u, dq_t, dk_t, dv_t, dg5_t, df_t, L)
    dwi_flight = _split_start(_shard_w_in_grads(dw_main, dw_f), "exchange_dw_in_start", per_peer=True)
    grad_x, dmeta, dg_norm = _inproj_bwd_x(
        w_t, dq_t, dk_t, dv_t, dg5_t, df_t, dout, x2, meta_full, norm_g, L, after=dwi_flight[4])
    small_parts = _pack_small_grads(dg_norm, dg_final, dga_p, dgc_p, dcw_p, db_f, dmeta, loss_part)
    small_flight = _split_start(small_parts, "exchange_small_start", per_peer=True)
    dwo_own, dwo_land = _split_wait(dwo_flight, small_flight[4], "exchange_dw_out_wait", per_peer=True)
    dwi_own, dwi_land = _split_wait(dwi_flight, dwo_land, "exchange_dw_in_wait", per_peer=True)

    big_out = _adamw_big(dwi_own, dwi_land, dwo_own, dwo_land,
                         w_in[0].T, m_w_in[0].T, v_w_in[0].T, w_out, m_w_out, v_w_out)
    g_w_in, d_w_in, nm_w_in, nv_w_in = [a.T[None] for a in big_out[:4]]
    g_w_out, d_w_out, nm_w_out, nv_w_out = big_out[4:]
    sm_own, sm_land = _split_wait(small_flight, big_out[4], "exchange_small_wait", per_peer=True)
    line = lambda a: a.reshape(1, D)
    small, loss = _adamw_small(sm_own, sm_land, dict(
        norm_g=(norm_g, m_norm_g, v_norm_g),
        final_norm_g=(line(final_norm_g), line(m_final_norm_g), line(v_final_norm_g)),
        attn_norm_g=(attn_norm_g, m_attn_norm_g, v_attn_norm_g),
        conv_norm_g=(conv_norm_g, m_conv_norm_g, v_conv_norm_g),
        b_f=(b_f, m_b_f, v_b_f), meta=(meta, m_meta, v_meta), conv_w=(conv_w, m_conv_w, v_conv_w)))
    small["final_norm_g"] = [a.reshape(D) for a in small["final_norm_g"]]
    order = ("meta", "norm_g", "w_in", "b_f", "conv_w", "attn_norm_g", "conv_norm_g", "w_out", "final_norm_g")
    groups = []
    for k, (wi, wo) in enumerate(((g_w_in, g_w_out), (d_w_in, d_w_out), (nm_w_in, nm_w_out), (nv_w_in, nv_w_out))):
        d = dict({n: small[n][k] for n in SMALL}, w_in=wi, w_out=wo)
        groups.append([d[n] for n in order])
    return (loss[0, 0], grad_x[None], *groups[0], *groups[1], *groups[2], *groups[3])
```

```python
import jax
import jax.numpy as jnp
from jax import lax
from jax.experimental import pallas as pl
from jax.experimental.pallas import tpu as pltpu

F32 = jnp.float32
BF16 = jnp.bfloat16

D = 1024
DA = 512
H = 8
DH = 64
NM = 16
TB = 128
P0 = TB - NM
TT = 3 * TB
HG = 8
NDEV = 8
NSEC = 8
DF = 16
DPROJ = NSEC * DA + DF
WSH = 513
WSHP = 528
WROWS = WSHP + D // NDEV
SROWS = 64
EPS = 1e-6
NEG = -1e30
LOG2E = 1.4426950408889634
LN2 = 0.6931471805599453
QSCALE = DH ** -0.5 * LOG2E
KA = 128
CB = 256
VMEM_LIMIT = 56 * 1024 * 1024

ADAM_LR = 0.001
ADAM_B1 = 0.9
ADAM_B2 = 0.999
ADAM_EPS = 1e-08
ADAM_WD = 0.01
ADAM_STEP = 10

NT_DIMS = (((1,), (1,)), ((), ()))
TN_DIMS = (((0,), (0,)), ((), ()))
MESH = pl.DeviceIdType.MESH


def _params(n_axes=1, vmem=VMEM_LIMIT):
    return pltpu.CompilerParams(dimension_semantics=("arbitrary",) * n_axes, vmem_limit_bytes=vmem)


def _dot(a, b, dims=None):
    if dims is None:
        return jnp.dot(a, b, preferred_element_type=F32)
    return lax.dot_general(a, b, dims, preferred_element_type=F32)


def _my_place():
    return lax.axis_index("x"), lax.axis_index("y"), lax.axis_index("c")


def _all_gather(xs, name):
    n = len(xs)

    def body(*refs):
        x_refs, out_refs = refs[:n], refs[n:2 * n]
        send_sems, recv_sems, local_sems = refs[2 * n:]
        mx, my, mc = _my_place()

        def across(px, py, pc, axis_a):
            flip_x = pc if axis_a else 1 - pc
            return (px + flip_x) % 2, (py + 1 - flip_x) % 2, pc

        def idx(p):
            return 4 * p[0] + 2 * p[1] + p[2]

        me, sib = (mx, my, mc), (mx, my, 1 - mc)
        a_nbr, b_nbr = across(*me, True), across(*me, False)
        diag = across(*b_nbr, True)
        sib_a, sib_b = across(*sib, True), across(*sib, False)
        sib_diag = across(*sib_b, True)

        waits = []
        for t in range(n):
            out_ref = out_refs[t]

            def copy(k, block, to, src=None, out_ref=out_ref, t=t):
                return pltpu.make_async_remote_copy(
                    src_ref=out_ref.at[idx(block)] if src is None else src, dst_ref=out_ref.at[idx(block)],
                    send_sem=send_sems.at[7 * t + k], recv_sem=recv_sems.at[7 * t + k],
                    device_id=to, device_id_type=MESH)

            mine = pltpu.make_async_copy(x_refs[t], out_ref.at[idx(me)], local_sems.at[t])
            mine.start()
            started = [copy(0, me, sib, src=x_refs[t]), copy(1, me, a_nbr, src=x_refs[t]),
                       copy(2, me, b_nbr, src=x_refs[t])]
            for cp in started:
                cp.start()
            waits.append((copy, mine, started))
        relays = ((1, a_nbr, ((3, b_nbr), (4, sib))), (2, b_nbr, ((5, sib),)), (3, diag, ((6, sib),)))
        for landed, block, onward in relays:
            for copy, _, started in waits:
                copy(landed, block, me).wait_recv()
                for k, to in onward:
                    started.append(copy(k, block, to))
                    started[-1].start()
        for copy, mine, started in waits:
            for k, block in ((0, sib), (4, sib_a), (5, sib_b), (6, sib_diag)):
                copy(k, block, me).wait_recv()
            for cp in started:
                cp.wait_send()
            mine.wait()

    any_spec = pl.BlockSpec(memory_space=pl.ANY)
    return pl.pallas_call(
        body, name=name,
        out_shape=[jax.ShapeDtypeStruct((NDEV,) + x.shape, x.dtype) for x in xs],
        in_specs=[any_spec] * n, out_specs=[any_spec] * n,
        scratch_shapes=[pltpu.SemaphoreType.DMA((7 * n,)), pltpu.SemaphoreType.DMA((7 * n,)),
                        pltpu.SemaphoreType.DMA((n,))],
    )(*xs)


_HBM = pl.BlockSpec(memory_space=pltpu.HBM)
_UNREAD = pl.BlockSpec(memory_space=pl.ANY)
_SEM = pl.BlockSpec(memory_space=pltpu.SEMAPHORE)
_EFFECT = pltpu.SideEffectType.DATAFLOW_SIDE_EFFECTING


def _peer_of(m, place):
    mx, my, mc = place
    return ((1 - mx) if m & 4 else mx, (1 - my) if m & 2 else my, (1 - mc) if m & 1 else mc)


def _split_copies(src_ref, land_ref, send_sems, recv_sems, per_peer, incoming):
    place = _my_place()
    me = 4 * place[0] + 2 * place[1] + place[2]
    out = []
    for m in range(1, NDEV):
        px, py, pc = _peer_of(m, place)
        peer = 4 * px + 2 * py + pc
        src = (src_ref.at[me] if incoming else src_ref.at[peer]) if per_peer else src_ref
        out.append(pltpu.make_async_remote_copy(
            src_ref=src, dst_ref=land_ref.at[peer if incoming else me],
            send_sem=send_sems.at[m - 1], recv_sem=recv_sems.at[m - 1],
            device_id=(px, py, pc), device_id_type=MESH))
    return out


def _split_start(src, name, per_peer):
    slab = src.shape[1:] if per_peer else src.shape

    def body(src_ref, land_ref, send_sems, recv_sems, src_thru, land_thru, token):
        for cp in _split_copies(src_ref, land_ref, send_sems, recv_sems, per_peer, incoming=False):
            cp.start()
        token[...] = jnp.zeros_like(token)

    return pl.pallas_call(
        body, name=name,
        out_shape=(pltpu.SemaphoreType.DMA((NDEV - 1,)), pltpu.SemaphoreType.DMA((NDEV - 1,)),
                   pltpu.HBM(src.shape, src.dtype), pltpu.HBM((NDEV,) + slab, src.dtype),
                   jax.ShapeDtypeStruct((8, TB), F32)),
        in_specs=(_HBM, _HBM), out_specs=(_SEM, _SEM, _HBM, _HBM, pl.BlockSpec(memory_space=pltpu.VMEM)),
        input_output_aliases={0: 2, 1: 3},
        compiler_params=pltpu.CompilerParams(has_side_effects=_EFFECT),
    )(pltpu.with_memory_space_constraint(src, pltpu.HBM),
      pltpu.with_memory_space_constraint(lax.empty((NDEV,) + slab, src.dtype), pltpu.HBM))


def _split_wait(handles, after, name, per_peer):
    send_sems, recv_sems, src_thru, land_thru, _ = handles

    def body(src_ref, land_ref, send_sems, recv_sems, after_ref, src_out, land_out):
        for cp in _split_copies(src_ref, land_ref, send_sems, recv_sems, per_peer, incoming=False):
            cp.wait_send()
        for cp in _split_copies(src_ref, land_ref, send_sems, recv_sems, per_peer, incoming=True):
            cp.wait_recv()

    return pl.pallas_call(
        body, name=name,
        out_shape=(pltpu.HBM(src_thru.shape, src_thru.dtype), pltpu.HBM(land_thru.shape, land_thru.dtype)),
        in_specs=(_HBM, _HBM, _SEM, _SEM, pl.BlockSpec(memory_space=pl.ANY)), out_specs=(_HBM, _HBM),
        input_output_aliases={0: 0, 1: 1},
        compiler_params=pltpu.CompilerParams(has_side_effects=_EFFECT),
    )(src_thru, land_thru, send_sems, recv_sems, after)


def _pick_slab(j, own_ref, land_ref, rows, per_peer=True):
    mx, my, mc = _my_place()
    me = 4 * mx + 2 * my + mc
    own = (lambda: own_ref[j, rows, :]) if per_peer else (lambda: own_ref[rows, :])
    return lax.cond(me == j, own, lambda: land_ref[j, rows, :])


def _h_block(t, x_ref, meta_ref):
    first = jnp.concatenate([jnp.zeros((P0, D), F32), meta_ref[...]], axis=0)
    return jnp.where(t == 0, first, x_ref[...])


def _x_specs3(tile=lambda j: j):
    return [pl.BlockSpec((TB, D), lambda j: (jnp.maximum(3 * tile(j) - 1, 0), 0)),
            pl.BlockSpec((TB, D), lambda j: (3 * tile(j), 0)),
            pl.BlockSpec((TB, D), lambda j: (3 * tile(j) + 1, 0))]


def _h_tile(j, xa_ref, xb_ref, xc_ref, meta_ref):
    first = jnp.concatenate([jnp.zeros((P0, D), F32), meta_ref[...]], axis=0)
    return jnp.concatenate([jnp.where(j == 0, first, xa_ref[...]), xb_ref[...], xc_ref[...]], axis=0)


def _full_spec(shape):
    return pl.BlockSpec(shape, lambda *_: (0,) * len(shape))


def _sigmoid(z):
    return 1.0 / (1.0 + jnp.exp(-z))


def _lane_tiles_sum(x):
    out = x[:, :TB]
    for i in range(1, x.shape[1] // TB):
        out = out + x[:, i * TB:(i + 1) * TB]
    return out


def _inproj_fwd(x, meta_full, norm_g, w_t, L, after):
    nj = L // TT

    def body(xa_ref, xb_ref, xc_ref, meta_ref, g_ref, w_ref, _, u_ref, proj_ref, f_ref, ktok_ref, vtok_ref):
        hb = _h_tile(pl.program_id(0), xa_ref, xb_ref, xc_ref, meta_ref)
        r = lax.rsqrt(jnp.mean(hb * hb, axis=-1, keepdims=True) + EPS)
        u = (hb * r * g_ref[...]).astype(BF16)
        u_ref[...] = u
        for s in range(NSEC):
            p = _dot(u, w_ref[s * DA:(s + 1) * DA, :], NT_DIMS)
            if s == 0:
                p = p * QSCALE
            if s in (1, 2):
                tok_ref = ktok_ref if s == 1 else vtok_ref
                for h in range(H):
                    tok_ref[h] = p[:, h * DH:(h + 1) * DH].astype(BF16)
            proj_ref[s * DA:(s + 1) * DA, :] = p.T.astype(BF16)
        f_ref[...] = _dot(w_ref[NSEC * DA:DPROJ, :], u, NT_DIMS)[:H]

    return pl.pallas_call(
        body, name="inproj_fwd", grid=(nj,),
        in_specs=_x_specs3() + [_full_spec((NM, D)), _full_spec((1, D)), _full_spec((DPROJ, D)), _UNREAD],
        out_specs=[
            pl.BlockSpec((TT, D), lambda t: (t, 0)),
            pl.BlockSpec((NSEC * DA, TT), lambda t: (0, t)),
            pl.BlockSpec((H, TT), lambda t: (0, t)),
            pl.BlockSpec((H, TT, DH), lambda t: (0, t, 0)),
            pl.BlockSpec((H, TT, DH), lambda t: (0, t, 0)),
        ],
        out_shape=[
            jax.ShapeDtypeStruct((L, D), BF16),
            jax.ShapeDtypeStruct((NSEC * DA, L), BF16),
            jax.ShapeDtypeStruct((H, L), F32),
            jax.ShapeDtypeStruct((H, L, DH), BF16),
            jax.ShapeDtypeStruct((H, L, DH), BF16),
        ],
        compiler_params=_params(),
    )(x, x, x, meta_full, norm_g, w_t, after)


def _split3(x):
    hi = x.astype(BF16).astype(F32)
    r = x - hi
    mid = r.astype(BF16).astype(F32)
    return hi, mid, (r - mid).astype(BF16).astype(F32)


def _bias_rows(bias):
    one = jnp.ones((1, TT), F32)
    zero = jnp.zeros((1, TT), F32)
    parts = [zero] * 3 if bias is None else list(_split3(bias))
    return jnp.concatenate([one] * 3 + parts + [zero] * (DF - 6), axis=0).astype(BF16)


def _fgate_fwd(f_t, b_col, ktok, L):
    nb = L // TB

    def body(f_ref, b_ref, ktok_ref, cq_ref, kaug_ref, sg_ref):
        z = f_ref[...] + b_ref[...]
        idx = lax.broadcasted_iota(jnp.int32, (H, L), 1)
        real = idx >= P0
        lf = jnp.where(real, jnp.minimum(z, 0.0) - jnp.log1p(jnp.exp(-jnp.abs(z))), 0.0)
        sg_ref[...] = jnp.where(real, 1.0 / (1.0 + jnp.exp(z)), 0.0)
        c = lf
        s = 1
        while s < L:
            c = c + jnp.where(idx >= s, pltpu.roll(c, s, 1), 0.0)
            s *= 2
        c = c * LOG2E
        for h in range(H):
            cq_ref[h] = c[h:h + 1, :]
        hi, mid, lo = _split3(-jnp.where(real, c, -NEG))
        lane = lax.broadcasted_iota(jnp.int32, (TB, KA), 1)
        ones = jnp.ones((3, TB), F32)
        for h in range(H):
            for b in range(nb):
                blk = slice(b * TB, (b + 1) * TB)
                cols = jnp.concatenate([
                    jnp.zeros((DH, TB), F32), hi[h:h + 1, blk], mid[h:h + 1, blk], lo[h:h + 1, blk], ones,
                    jnp.zeros((KA - DH - 6, TB), F32)], axis=0).T
                k = jnp.concatenate([ktok_ref[h, blk, :].astype(F32), jnp.zeros((TB, KA - DH), F32)], axis=1)
                kaug_ref[h, blk, :] = jnp.where(lane < DH, k, cols).astype(BF16)

    return pl.pallas_call(
        body, name="fgate_fwd",
        out_shape=[
            jax.ShapeDtypeStruct((H, 1, L), F32),
            jax.ShapeDtypeStruct((H, L, KA), BF16),
            jax.ShapeDtypeStruct((H, L), F32),
        ],
        compiler_params=pltpu.CompilerParams(vmem_limit_bytes=VMEM_LIMIT),
    )(f_t, b_col, ktok)


def _causal_mask():
    r = lax.broadcasted_iota(jnp.int32, (TT, TT), 0)
    c = lax.broadcasted_iota(jnp.int32, (TT, TT), 1)
    return r <= c


def _attn_fwd(proj_t, kaug, cq, L):
    nq = L // TT

    def body(q_ref, qn_ref, kaug_ref, v_ref, cq_ref, o_ref, lse_ref,
             qa_scr, s_scr, cmax_scr, m_scr, p_scr, alpha_scr, acc_scr):
        j = pl.program_id(0)
        rows = [slice(g * DH, (g + 1) * DH) for g in range(HG)]
        ones = jnp.ones((DF, TT), BF16)

        def load_queries(ref):
            for g in range(HG):
                qa_scr[g] = jnp.concatenate(
                    [ref[rows[g], :], _bias_rows(None), jnp.zeros((KA - DH - DF, TT), BF16)], axis=0)

        def scores(kt, masked):
            k_off = pl.multiple_of(kt * TT, TT)
            for g in range(HG):
                s = _dot(kaug_ref[g, pl.ds(k_off, TT), :], qa_scr[g])
                if masked:
                    s = jnp.where(_causal_mask(), s, NEG)
                s_scr[g] = s
                cmax_scr[g] = jnp.max(s, axis=0, keepdims=True)

        def softmax():
            for g in range(HG):
                m_old = m_scr[g]
                m_new = jnp.maximum(m_old, cmax_scr[g])
                alpha_scr[g] = jnp.exp2(m_old - m_new)
                p_scr[g] = jnp.exp2(s_scr[g] - m_new).astype(BF16)
                m_scr[g] = m_new

        def weighted_sum(kt):
            k_off = pl.multiple_of(kt * TT, TT)
            for g in range(HG):
                v1 = jnp.concatenate([v_ref[rows[g], pl.ds(k_off, TT)], ones], axis=0)
                acc_scr[g] = alpha_scr[g] * acc_scr[g] + _dot(v1, p_scr[g])

        @pl.when(j == 0)
        def _():
            load_queries(q_ref)
            scores(0, True)

        m_scr[...] = jnp.full_like(m_scr, NEG)
        acc_scr[...] = jnp.zeros_like(acc_scr)

        @pl.when(j >= 1)
        def _():
            softmax()
            scores(j - 1, False)

        def step(i, c):
            weighted_sum(j - i + 1)
            softmax()
            scores(j - i - 1, False)
            return c

        lax.fori_loop(1, j, step, 0)

        @pl.when(j >= 1)
        def _():
            weighted_sum(1)

        @pl.when(j < nq - 1)
        def _():
            softmax()
            load_queries(qn_ref)
            scores(j + 1, True)
            weighted_sum(0)

        @pl.when(j == nq - 1)
        def _():
            softmax()
            weighted_sum(0)

        for g in range(HG):
            l = acc_scr[g, DH:DH + 1, :]
            o_ref[rows[g], :] = acc_scr[g, :DH, :] * (1.0 / l)
            lse_ref[g] = m_scr[g] + jnp.log2(l) + cq_ref[g]

    assert HG == H
    return pl.pallas_call(
        body, name="attn_fwd", grid=(nq,),
        in_specs=[
            pl.BlockSpec((DA, TT), lambda j: (0, j)),
            pl.BlockSpec((DA, TT), lambda j: (0, jnp.minimum(j + 1, nq - 1))),
            pl.BlockSpec((H, L, KA), lambda j: (0, 0, 0)),
            pl.BlockSpec((DA, L), lambda j: (2, 0)),
            pl.BlockSpec((H, 1, TT), lambda j: (0, 0, j)),
        ],
        out_specs=[
            pl.BlockSpec((DA, TT), lambda j: (0, j)),
            pl.BlockSpec((H, 1, TT), lambda j: (0, 0, j)),
        ],
        out_shape=[jax.ShapeDtypeStruct((DA, L), F32), jax.ShapeDtypeStruct((H, 1, L), F32)],
        scratch_shapes=[pltpu.VMEM((HG, KA, TT), BF16), pltpu.VMEM((HG, TT, TT), F32), pltpu.VMEM((HG, 1, TT), F32),
                        pltpu.VMEM((HG, 1, TT), F32), pltpu.VMEM((HG, TT, TT), BF16), pltpu.VMEM((HG, 1, TT), F32),
                        pltpu.VMEM((HG, DH + DF, TT), F32)],
        compiler_params=_params(),
    )(proj_t, proj_t, kaug, proj_t, cq)


def _gate_group(rows, o_ref, za_ref, gb_ref, gc_ref, xc_ref, zc_ref, gcp_ref, xcp_ref, cw_ref, ga_ref, gcn_ref, first):
    n_rep = TT // TB
    f32 = lambda r: r[rows, :].astype(F32)
    o, za, gb, gc, xc, zc = o_ref[rows, :], f32(za_ref), f32(gb_ref), f32(gc_ref), f32(xc_ref), f32(zc_ref)
    a = gc * xc
    a_prev = jnp.where(first, 0.0, f32(gcp_ref) * f32(xcp_ref))
    full = jnp.concatenate([a_prev, a], axis=1)
    a1 = pltpu.roll(full, 1, 1)[:, TB:]
    a2 = pltpu.roll(full, 2, 1)[:, TB:]
    w0 = jnp.tile(cw_ref[0, rows, :], (1, n_rep))
    w1 = jnp.tile(cw_ref[1, rows, :], (1, n_rep))
    w2 = jnp.tile(cw_ref[2, rows, :], (1, n_rep))
    cv = w0 * a2 + w1 * a1 + w2 * a
    e = gb * cv
    rc = lax.rsqrt(jnp.mean(e * e, axis=0, keepdims=True) + EPS)
    ec = e * rc
    ra = lax.rsqrt(jnp.mean(o * o, axis=0, keepdims=True) + EPS)
    oa = o * ra
    g_a = jnp.tile(ga_ref[rows, :], (1, n_rep))
    g_c = jnp.tile(gcn_ref[rows, :], (1, n_rep))
    sa = _sigmoid(za)
    sc = _sigmoid(zc)
    return dict(o=o, za=za, gb=gb, gc=gc, xc=xc, zc=zc, a=a, a1=a1, a2=a2, w0=w0, w1=w1, w2=w2, cv=cv, e=e,
                rc=rc, ec=ec, ra=ra, oa=oa, g_a=g_a, g_c=g_c, sa=sa, sc=sc)


def _gate_specs(nj, rev):
    def jj(i):
        return (nj - 1 - i) if rev else i

    def sec(s):
        return pl.BlockSpec((DA, TT), lambda i: (s, jj(i)))

    def halo(s):
        return pl.BlockSpec((DA, TB), lambda i: (s, jnp.maximum(3 * jj(i) - 1, 0)))

    return [pl.BlockSpec((DA, TT), lambda i: (0, jj(i))), sec(3), sec(4), sec(5), sec(6), sec(7), halo(5), halo(6),
            _full_spec((3, DA, TB)), _full_spec((DA, TB)), _full_spec((DA, TB))]


def _gate_fwd(o_t, proj_t, cw_b, ga_b, gcn_b, L):
    nj = L // TT

    def body(o_ref, za_ref, gb_ref, gc_ref, xc_ref, zc_ref, gcp_ref, xcp_ref, cw_ref, ga_ref, gcn_ref, mix_ref):
        j = pl.program_id(0)

        def group(h, c):
            r0 = pl.multiple_of(h * DH, DH)
            g = _gate_group(pl.ds(r0, DH), o_ref, za_ref, gb_ref, gc_ref, xc_ref, zc_ref, gcp_ref, xcp_ref,
                            cw_ref, ga_ref, gcn_ref, j == 0)
            mix_ref[pl.ds(r0, DH), :] = (g["oa"] * g["g_a"] * (g["za"] * g["sa"])).astype(BF16)
            mix_ref[pl.ds(DA + r0, DH), :] = (g["ec"] * g["g_c"] * (g["zc"] * g["sc"])).astype(BF16)
            return c

        lax.fori_loop(0, H, group, 0, unroll=2)

    return pl.pallas_call(
        body, name="gate_fwd", grid=(nj,),
        in_specs=_gate_specs(nj, False),
        out_specs=pl.BlockSpec((2 * DA, TT), lambda j: (0, j)),
        out_shape=jax.ShapeDtypeStruct((2 * DA, L), BF16),
        compiler_params=_params(),
    )(o_t, proj_t, proj_t, proj_t, proj_t, proj_t, proj_t, proj_t, cw_b, ga_b, gcn_b)


def _outproj(mix_t, w_out, x, meta_full, fng, target, L):
    nj = L // TT
    rp = NM
    n_bwd = 8
    cb = D // 4
    assert P0 % rp == 0 and TB % rp == 0 and (TT // rp) % n_bwd == 0

    def body(mix_ref, mixp_ref, w_ref, xa_ref, xb_ref, xc_ref, meta_ref, g_ref, ta_ref, tb_ref, tc_ref,
             dout_ref, dmix_ref, dwb_ref, loss_ref, dg_ref, dw_ref, o_scr, db_new, db_old, sq_acc, dg_acc):
        t = pl.program_id(0)

        def loss_rows(c):
            blk = c // (TB // rp)
            rows, out_rows = pl.ds((c % (TB // rp)) * rp, rp), pl.ds(c * rp, rp)
            h = (xa_ref, xb_ref, xc_ref)[blk][rows, :]
            if blk == 0:
                first = meta_ref[...] if c == P0 // rp else jnp.zeros((rp, D), F32)
                h = jnp.where(t == 0, first, h)
            o = o_scr[out_rows, :] + h
            r = lax.rsqrt(jnp.mean(o * o, axis=-1, keepdims=True) + EPS)
            orn = o * r
            g = g_ref[...]
            diff = orn * g - (ta_ref, tb_ref, tc_ref)[blk][rows, :]
            if blk == 0:
                diff = diff * jnp.where(t > 0, 1.0, 0.0)
            gy = diff * (g * (1.0 / D))
            dout = r * (gy - orn * jnp.mean(gy * orn, axis=-1, keepdims=True))
            dout_ref[out_rows, :] = dout
            db_new[out_rows, :] = dout.astype(BF16)
            sq, go = diff * diff, diff * orn
            sq_acc[...] += sq[:8] + sq[8:]
            dg_acc[...] += go[:8] + go[8:]

        def backward_cols(n):
            if n < 4:
                cols = slice(n * cb, (n + 1) * cb)
                dmix_ref[cols, :] = _dot(db_old[...], w_ref[cols, :], NT_DIMS).T.astype(BF16)
            else:
                cols = slice((n - 4) * cb, (n - 3) * cb)
                dw_ref[:, cols] += _dot(mixp_ref[...], db_old[:, cols])

        def step(forward, backward):
            if forward:
                o_scr[...] = _dot(mix_ref[...], w_ref[...], TN_DIMS)
            per = TT // rp // n_bwd
            for k in range(n_bwd):
                if forward:
                    for c in range(per * k, per * (k + 1)):
                        loss_rows(c)
                if backward:
                    backward_cols(k)
            if forward:
                db_old[...] = db_new[...]

        @pl.when(t == 0)
        def _():
            dw_ref[...] = jnp.zeros_like(dw_ref)
            sq_acc[...] = jnp.zeros_like(sq_acc)
            dg_acc[...] = jnp.zeros_like(dg_acc)
            step(True, False)

        @pl.when((t > 0) & (t < nj))
        def _():
            step(True, True)

        @pl.when(t == nj)
        def _():
            step(False, True)
            dwb_ref[...] = dw_ref[...].astype(BF16)
            loss_ref[...] = jnp.sum(sq_acc[...], keepdims=True) * (0.5 / D)
            dg_ref[...] = jnp.sum(dg_acc[...], axis=0, keepdims=True) * (1.0 / D)

    cur = lambda t: jnp.minimum(t, nj - 1)
    prev = lambda t: jnp.maximum(t - 1, 0)
    return pl.pallas_call(
        body, name="outproj", grid=(nj + 1,),
        in_specs=[pl.BlockSpec((D, TT), lambda t: (0, cur(t))), pl.BlockSpec((D, TT), lambda t: (0, prev(t))),
                  _full_spec((D, D))] + _x_specs3(cur) + [_full_spec((NM, D)), _full_spec((1, D))] + _x_specs3(cur),
        out_specs=[pl.BlockSpec((TT, D), lambda t: (cur(t), 0)), pl.BlockSpec((D, TT), lambda t: (0, prev(t))),
                   _full_spec((D, D)), _full_spec((1, 1)), _full_spec((1, D))],
        out_shape=[jax.ShapeDtypeStruct((L, D), F32), jax.ShapeDtypeStruct((D, L), BF16),
                   jax.ShapeDtypeStruct((D, D), BF16), jax.ShapeDtypeStruct((1, 1), F32),
                   jax.ShapeDtypeStruct((1, D), F32)],
        scratch_shapes=[pltpu.VMEM((D, D), F32), pltpu.VMEM((TT, D), F32), pltpu.VMEM((TT, D), BF16),
                        pltpu.VMEM((TT, D), BF16), pltpu.VMEM((8, D), F32), pltpu.VMEM((8, D), F32)],
        compiler_params=_params(),
    )(mix_t, mix_t, w_out, x, x, x, meta_full, fng, target, target, target)


def _gate_bwd(dmix_t, o_t, proj_t, cw_b, ga_b, gcn_b, L, after):
    nj = L // TT

    def body(dmix_ref, o_ref, za_ref, gb_ref, gc_ref, xc_ref, zc_ref, gcp_ref, xcp_ref, cw_ref, ga_ref, gcn_ref, _,
             do_ref, dd_ref, dg5_ref, dga_ref, dgc_ref, dcw_ref, carry_ref):
        i = pl.program_id(0)
        j = nj - 1 - i

        @pl.when(i == 0)
        def _():
            carry_ref[...] = jnp.zeros_like(carry_ref)
            dga_ref[...] = jnp.zeros_like(dga_ref)
            dgc_ref[...] = jnp.zeros_like(dgc_ref)
            dcw_ref[...] = jnp.zeros_like(dcw_ref)

        def group(h, c):
            r0 = pl.multiple_of(h * DH, DH)
            rows = pl.ds(r0, DH)
            sec = lambda s: pl.ds(s * DA + r0, DH)
            g = _gate_group(rows, o_ref, za_ref, gb_ref, gc_ref, xc_ref, zc_ref, gcp_ref, xcp_ref,
                            cw_ref, ga_ref, gcn_ref, j == 0)
            o, za, gb, gc, xc, zc, sa, sc = (g[n] for n in ("o", "za", "gb", "gc", "xc", "zc", "sa", "sc"))
            dya = dmix_ref[rows, :].astype(F32)
            dyc = dmix_ref[pl.ds(DA + r0, DH), :].astype(F32)

            dn = dya * (za * sa)
            dg5_ref[sec(0), :] = (dya * (g["oa"] * g["g_a"]) * (sa * (1.0 + za * (1.0 - sa)))).astype(BF16)
            dga_ref[rows, :] += _lane_tiles_sum(dn * g["oa"])
            dng = dn * g["g_a"]
            mean_a = jnp.mean(dng * g["oa"], axis=0, keepdims=True)
            do = (dng - g["oa"] * mean_a) * g["ra"]
            do_ref[rows, :] = do.astype(BF16)
            dd_ref[h] = jnp.sum(do * o, axis=0, keepdims=True)

            dnc = dyc * (zc * sc)
            dg5_ref[sec(4), :] = (dyc * (g["ec"] * g["g_c"]) * (sc * (1.0 + zc * (1.0 - sc)))).astype(BF16)
            dgc_ref[rows, :] += _lane_tiles_sum(dnc * g["ec"])
            dncg = dnc * g["g_c"]
            mean_c = jnp.mean(dncg * g["ec"], axis=0, keepdims=True)
            de = (dncg - g["ec"] * mean_c) * g["rc"]
            dg5_ref[sec(1), :] = (de * g["cv"]).astype(BF16)
            dcv = de * gb
            full = jnp.concatenate([dcv, carry_ref[rows, :]], axis=1)
            d1 = pltpu.roll(full, TT + TB - 1, 1)[:, :TT]
            d2 = pltpu.roll(full, TT + TB - 2, 1)[:, :TT]
            carry_ref[rows, :] = dcv[:, :TB]
            da = g["w2"] * dcv + g["w1"] * d1 + g["w0"] * d2
            dg5_ref[sec(2), :] = (da * xc).astype(BF16)
            dg5_ref[sec(3), :] = (da * gc).astype(BF16)
            dcw_ref[0, rows, :] += _lane_tiles_sum(dcv * g["a2"])
            dcw_ref[1, rows, :] += _lane_tiles_sum(dcv * g["a1"])
            dcw_ref[2, rows, :] += _lane_tiles_sum(dcv * g["a"])
            return c

        lax.fori_loop(0, H, group, 0, unroll=2)

    rj = lambda i: nj - 1 - i
    return pl.pallas_call(
        body, name="gate_bwd", grid=(nj,),
        in_specs=[pl.BlockSpec((2 * DA, TT), lambda i: (0, rj(i)))] + _gate_specs(nj, True) + [_UNREAD],
        out_specs=[
            pl.BlockSpec((DA, TT), lambda i: (0, rj(i))),
            pl.BlockSpec((H, 1, TT), lambda i: (0, 0, rj(i))),
            pl.BlockSpec((5 * DA, TT), lambda i: (0, rj(i))),
            _full_spec((DA, TB)), _full_spec((DA, TB)), _full_spec((3, DA, TB)),
        ],
        out_shape=[
            jax.ShapeDtypeStruct((DA, L), BF16),
            jax.ShapeDtypeStruct((H, 1, L), F32),
            jax.ShapeDtypeStruct((5 * DA, L), BF16),
            jax.ShapeDtypeStruct((DA, TB), F32),
            jax.ShapeDtypeStruct((DA, TB), F32),
            jax.ShapeDtypeStruct((3, DA, TB), F32),
        ],
        scratch_shapes=[pltpu.VMEM((DA, TB), F32)],
        compiler_params=_params(),
    )(dmix_t, o_t, proj_t, proj_t, proj_t, proj_t, proj_t, proj_t, proj_t, cw_b, ga_b, gcn_b, after)


def _attn_bwd(proj_t, kaug, vtok, do_t, lse, dd, cq, L):
    nk = L // TT

    def body(q_ref, kaug_ref, vtok_ref, kt_ref, do_ref, lse_ref, dd_ref, cq_ref,
             dq_ref, dk_ref, dv_ref, dck_ref, dcq_ref, dq_acc, kt1_scr, s_scr, dp_scr, dv_scr, dk_scr):
        i = pl.program_id(0)
        rows = [slice(g * DH, (g + 1) * DH) for g in range(HG)]
        ones = jnp.ones((DF, TT), BF16)
        zpad = jnp.zeros((KA - DH - DF, TT), BF16)
        for g in range(HG):
            kt1_scr[g] = jnp.concatenate([kt_ref[rows[g], :], ones], axis=0)
        dv_scr[...] = jnp.zeros_like(dv_scr)
        dk_scr[...] = jnp.zeros_like(dk_scr)

        def q_rows(g, q_off):
            bias = cq_ref[g, :, pl.ds(q_off, TT)] - lse_ref[g, :, pl.ds(q_off, TT)]
            return jnp.concatenate([q_ref[rows[g], pl.ds(q_off, TT)], _bias_rows(bias)], axis=0)

        def scores(jq, masked):
            q_off = pl.multiple_of(jq * TT, TT)
            for g in range(HG):
                s = _dot(kaug_ref[g], jnp.concatenate([q_rows(g, q_off), zpad], axis=0))
                if masked:
                    s = jnp.where(_causal_mask(), s, NEG)
                s_scr[g] = s
                dp_scr[g] = _dot(vtok_ref[g], do_ref[rows[g], pl.ds(q_off, TT)])

        def grads(jq):
            q_off = pl.multiple_of(jq * TT, TT)
            for g in range(HG):
                p = jnp.exp2(s_scr[g])
                ds = (p * (dp_scr[g] - dd_ref[g, :, pl.ds(q_off, TT)])).astype(BF16)
                do1 = jnp.concatenate([do_ref[rows[g], pl.ds(q_off, TT)], jnp.zeros((KA - DH, TT), BF16)], axis=0)
                q1 = jnp.concatenate([q_rows(g, q_off), zpad], axis=0)
                dv_scr[g] += _dot(p.astype(BF16), do1, NT_DIMS)
                dk_scr[g] += _dot(ds, q1, NT_DIMS)
                dq_acc[g, :, pl.ds(q_off, TT)] += _dot(kt1_scr[g], ds)

        @pl.when(i == 0)
        def _():
            dq_acc[...] = jnp.zeros_like(dq_acc)

        scores(i, True)

        def step(jq, c):
            grads(jq)
            scores(jq + 1, False)
            return c

        lax.fori_loop(i, nk - 1, step, 0)
        grads(nk - 1)
        for g in range(HG):
            dv_ref[rows[g], :] = dv_scr[g].T[:DH, :].astype(BF16)
            dk_t = dk_scr[g].T
            dk_ref[rows[g], :] = (dk_t[:DH, :] * LN2).astype(BF16)
            dck_ref[g] = dk_t[DH:DH + 1, :]

        @pl.when(i == nk - 1)
        def _():
            for g in range(HG):
                dq_ref[rows[g], :] = (dq_acc[g, :DH, :] * (DH ** -0.5)).astype(BF16)
                dcq_ref[g] = dq_acc[g, DH:DH + 1, :]

    assert HG == H
    head = lambda i: (0, 0)
    row = lambda i: (0, 0, 0)
    return pl.pallas_call(
        body, name="attn_bwd", grid=(nk,),
        in_specs=[
            pl.BlockSpec((DA, L), head),
            pl.BlockSpec((H, TT, KA), lambda i: (0, i, 0)),
            pl.BlockSpec((H, TT, DH), lambda i: (0, i, 0)),
            pl.BlockSpec((DA, TT), lambda i: (1, i)),
            pl.BlockSpec((DA, L), head),
            pl.BlockSpec((H, 1, L), row), pl.BlockSpec((H, 1, L), row), pl.BlockSpec((H, 1, L), row),
        ],
        out_specs=[
            pl.BlockSpec((DA, L), head),
            pl.BlockSpec((DA, TT), lambda i: (0, i)),
            pl.BlockSpec((DA, TT), lambda i: (0, i)),
            pl.BlockSpec((H, 1, TT), lambda i: (0, 0, i)),
            pl.BlockSpec((H, 1, L), row),
        ],
        out_shape=[jax.ShapeDtypeStruct((DA, L), BF16), jax.ShapeDtypeStruct((DA, L), BF16),
                   jax.ShapeDtypeStruct((DA, L), BF16), jax.ShapeDtypeStruct((H, 1, L), F32),
                   jax.ShapeDtypeStruct((H, 1, L), F32)],
        scratch_shapes=[
            pltpu.VMEM((HG, DH + DF, L), F32),
            pltpu.VMEM((HG, DH + DF, TT), BF16),
            pltpu.VMEM((HG, TT, TT), F32), pltpu.VMEM((HG, TT, TT), F32),
            pltpu.VMEM((HG, TT, KA), F32), pltpu.VMEM((HG, TT, KA), F32)],
        compiler_params=_params(),
    )(proj_t, kaug, vtok, proj_t, do_t, lse, dd, cq)


def _fgate_bwd(dcq, dck, sg, L):
    def body(dcq_ref, dck_ref, sg_ref, df_ref, db_ref):
        dc = jnp.concatenate([dcq_ref[h] - dck_ref[h] for h in range(H)], axis=0)
        idx = lax.broadcasted_iota(jnp.int32, (H, L), 1)
        r = dc
        s = 1
        while s < L:
            r = r + jnp.where(idx + s < L, pltpu.roll(r, L - s, 1), 0.0)
            s *= 2
        df = r * sg_ref[...]
        db_ref[...] = jnp.broadcast_to(jnp.sum(df, axis=1, keepdims=True), (H, TB))
        df_ref[...] = jnp.concatenate([df, jnp.zeros((DF - H, L), F32)], axis=0).astype(BF16)

    return pl.pallas_call(
        body, name="fgate_bwd",
        out_shape=[jax.ShapeDtypeStruct((DF, L), BF16), jax.ShapeDtypeStruct((H, TB), F32)],
        compiler_params=pltpu.CompilerParams(vmem_limit_bytes=VMEM_LIMIT),
    )(dcq, dck, sg)


def _inproj_bwd_x(w, dq_t, dk_t, dv_t, dg5_t, df_t, dout, x, meta_full, norm_g, L, after):
    nj = L // TT
    seq = x.shape[0]

    def body(w_ref, dq_ref, dk_ref, dv_ref, dg5_ref, df_ref, dout_ref, xa_ref, xb_ref, xc_ref, meta_ref, g_ref, _,
             gx_ref, dmeta_ref, dg_ref, dh_scr, sems):
        j = pl.program_id(0)
        slot = j % 2

        def copy_out(step, slot_):
            first = pltpu.make_async_copy(dh_scr.at[slot_, pl.ds(TB, TT - TB)], gx_ref.at[pl.ds(0, TT - TB)],
                                          sems.at[slot_])
            later = pltpu.make_async_copy(dh_scr.at[slot_], gx_ref.at[pl.ds(step * TT - TB, TT)], sems.at[slot_])
            return first, later

        @pl.when(j == 0)
        def _():
            dg_ref[...] = jnp.zeros_like(dg_ref)

        du = _dot(dq_ref[...], w_ref[0:DA, :], TN_DIMS)
        du += _dot(dk_ref[...], w_ref[DA:2 * DA, :], TN_DIMS)
        du += _dot(dv_ref[...], w_ref[2 * DA:3 * DA, :], TN_DIMS)
        du += _dot(dg5_ref[...], w_ref[3 * DA:NSEC * DA, :], TN_DIMS)
        du += _dot(df_ref[...], w_ref[NSEC * DA:DPROJ, :], TN_DIMS)
        hb = _h_tile(j, xa_ref, xb_ref, xc_ref, meta_ref)
        r = lax.rsqrt(jnp.mean(hb * hb, axis=-1, keepdims=True) + EPS)
        hn = hb * r
        dg_ref[...] += jnp.sum(du * hn, axis=0, keepdims=True)
        gu = du * g_ref[...]
        dh = dout_ref[...] + r * gu - hn * (r * jnp.mean(gu * hn, axis=-1, keepdims=True))

        dh_scr[slot] = dh

        @pl.when(j == 0)
        def _():
            dmeta_ref[...] = dh[P0:TB, :]
            copy_out(0, 0)[0].start()

        @pl.when(j >= 1)
        def _():
            copy_out(j, slot)[1].start()

        @pl.when(j == 1)
        def _():
            copy_out(0, 0)[0].wait()

        @pl.when(j >= 2)
        def _():
            copy_out(j - 1, 1 - slot)[1].wait()

        @pl.when(j == nj - 1)
        def _():
            copy_out(j, slot)[0 if nj == 1 else 1].wait()

    blk = lambda rows: pl.BlockSpec((rows, TT), lambda j: (0, j))
    return pl.pallas_call(
        body, name="inproj_bwd_x", grid=(nj,),
        in_specs=[_full_spec((DPROJ, D)), blk(DA), blk(DA), blk(DA), blk(5 * DA), blk(DF),
                  pl.BlockSpec((TT, D), lambda j: (j, 0))] + _x_specs3()
                 + [_full_spec((NM, D)), _full_spec((1, D)), _UNREAD],
        out_specs=[pl.BlockSpec(memory_space=pl.ANY), _full_spec((NM, D)), _full_spec((1, D))],
        out_shape=[jax.ShapeDtypeStruct((seq, D), F32), jax.ShapeDtypeStruct((NM, D), F32),
                   jax.ShapeDtypeStruct((1, D), F32)],
        scratch_shapes=[pltpu.VMEM((2, TT, D), F32), pltpu.SemaphoreType.DMA((2,))],
        compiler_params=_params(),
    )(w, dq_t, dk_t, dv_t, dg5_t, df_t, dout, x, x, x, meta_full, norm_g, after)


def _inproj_bwd_w(u, dq_t, dk_t, dv_t, dg5_t, df_t, L):
    def body(u_ref, dq_ref, dk_ref, dv_ref, dg5_ref, df_ref, dw_ref, dwf_ref):
        s = pl.program_id(0)
        u_all = u_ref[...]

        @pl.when(s < 5)
        def _():
            dw_ref[...] = _dot(dg5_ref[...], u_all)

        for step, ref in ((5, dq_ref), (6, dk_ref), (7, dv_ref)):
            @pl.when(s == step)
            def _(ref=ref):
                dw_ref[...] = _dot(ref[...], u_all)

        @pl.when(s == NSEC - 1)
        def _():
            dwf_ref[...] = _dot(df_ref[...], u_all)

    once = lambda shape: pl.BlockSpec(shape, lambda s: (0, 0), pipeline_mode=pl.Buffered(1))
    return pl.pallas_call(
        body, name="inproj_bwd_w", grid=(NSEC,),
        in_specs=[
            once((L, D)), once((DA, L)), once((DA, L)), once((DA, L)),
            pl.BlockSpec((DA, L), lambda s: (jnp.minimum(s, 4), 0)),
            once((DF, L)),
        ],
        out_specs=[pl.BlockSpec((DA, D), lambda s: (jnp.where(s < 5, s + 3, s - 5), 0)), _full_spec((DF, D))],
        out_shape=[jax.ShapeDtypeStruct((NSEC * DA, D), F32), jax.ShapeDtypeStruct((DF, D), F32)],
        compiler_params=_params(),
    )(u, dq_t, dk_t, dv_t, dg5_t, df_t)


def _adamw(w, g, m, v):
    m = ADAM_B1 * m + (1.0 - ADAM_B1) * g
    v = ADAM_B2 * v + (1.0 - ADAM_B2) * (g * g)
    m_hat = m / (1.0 - ADAM_B1 ** ADAM_STEP)
    v_hat = v / (1.0 - ADAM_B2 ** ADAM_STEP)
    delta = -ADAM_LR * (m_hat / (jnp.sqrt(v_hat) + ADAM_EPS) + ADAM_WD * w)
    return delta, m, v


def _adamw_big(own_in, land_in, own_out, land_out, w_in_t, m_in_t, v_in_t, w_out, m_out, v_out):
    cb = CB
    e_sh = D // NDEV
    in_shape = jax.ShapeDtypeStruct(w_in_t.shape, F32)
    out_shape = jax.ShapeDtypeStruct(w_out.shape, F32)

    def total(own_ref, land_ref, rows):
        g = _pick_slab(0, own_ref, land_ref, rows).astype(F32)
        for j in range(1, NDEV):
            g = g + _pick_slab(j, own_ref, land_ref, rows).astype(F32)
        return g

    def body(oi_ref, li_ref, oo_ref, lo_ref, wi_ref, mi_ref, vi_ref, wo_ref, mo_ref, vo_ref,
             gi, di, mi, vi, go, do, mo, vo):
        g = total(oi_ref, li_ref, slice(0, WSHP))[:WSH]
        d, mn, vn = _adamw(wi_ref[...], g, mi_ref[...], vi_ref[...])
        gi[...], di[...], mi[...], vi[...] = g, d, mn, vn
        g = total(oo_ref, lo_ref, slice(0, e_sh))
        d, mn, vn = _adamw(wo_ref[0], g, mo_ref[0], vo_ref[0])
        go[0], do[0], mo[0], vo[0] = g, d, mn, vn

    slab = lambda rows: pl.BlockSpec((NDEV, rows, cb), lambda i: (0, 0, i))
    ispec = pl.BlockSpec((WSH, cb), lambda i: (0, i))
    ospec = pl.BlockSpec((1, e_sh, cb), lambda i: (0, 0, i))
    return pl.pallas_call(
        body, name="adamw_big", grid=(D // cb,),
        in_specs=[slab(WSHP), slab(WSHP), slab(e_sh), slab(e_sh), ispec, ispec, ispec, ospec, ospec, ospec],
        out_specs=[ispec] * 4 + [ospec] * 4, out_shape=[in_shape] * 4 + [out_shape] * 4,
        compiler_params=_params(),
    )(own_in, land_in, own_out, land_out, w_in_t, m_in_t, v_in_t, w_out, m_out, v_out)


F0 = 3 * DA


def _unshard_w_out(own, land):
    e_sh = D // NDEV

    def body(own_ref, land_ref, wo_ref):
        for j in range(NDEV):
            wo_ref[j * e_sh:(j + 1) * e_sh, :] = _pick_slab(j, own_ref, land_ref, slice(0, e_sh), per_peer=False)

    return pl.pallas_call(
        body, name="unshard_w_out", grid=(D // CB,),
        in_specs=[pl.BlockSpec((e_sh, CB), lambda i: (0, i)), pl.BlockSpec((NDEV, e_sh, CB), lambda i: (0, 0, i))],
        out_specs=pl.BlockSpec((D, CB), lambda i: (0, i)),
        out_shape=jax.ShapeDtypeStruct((D, D), BF16),
        compiler_params=_params(),
    )(own, land)


def _unshard_w_in(w_all, small_all, attn_gain, conv_gain):
    def body(w_ref, small_ref, ga_ref, gc_ref, wt_ref, meta_ref, cwb_ref, gab_ref, gcb_ref):
        i = pl.program_id(0)
        for k in range(CB // TB):
            meta_ref[:, k * TB:(k + 1) * TB] = small_ref[(CB // TB) * i + k, 0:NM, :]

        @pl.when(i == 0)
        def _():
            per_row = lambda line: jnp.broadcast_to(line, (TB, DA)).T
            cw = jnp.concatenate([small_ref[j, NM:NM + 3, 0:DH] for j in range(NDEV)], axis=1)
            for k in range(3):
                cwb_ref[k] = per_row(cw[k:k + 1, :])
            gab_ref[...] = per_row(ga_ref[...])
            gcb_ref[...] = per_row(gc_ref[...])

        def ref_rows(lo, hi):
            pieces, r = [], lo
            while r < hi:
                sh, off = divmod(r, WSH)
                n = min(hi - r, WSH - off)
                pieces.append(w_ref[sh, off:off + n, :])
                r += n
            return pieces

        for s in range(NSEC):
            lo = s * DA if s < 3 else s * DA + H
            wt_ref[s * DA:(s + 1) * DA, :] = jnp.concatenate(ref_rows(lo, lo + DA), axis=0)
        wt_ref[NSEC * DA:DPROJ, :] = jnp.concatenate(
            ref_rows(F0, F0 + H) + [jnp.zeros((DF - H, CB), BF16)], axis=0)

    return pl.pallas_call(
        body, name="unshard_w_in", grid=(D // CB,),
        in_specs=[pl.BlockSpec((NDEV, WSHP, CB), lambda i: (0, 0, i)), _full_spec(small_all.shape),
                  _full_spec((1, DA)), _full_spec((1, DA))],
        out_specs=[pl.BlockSpec((DPROJ, CB), lambda i: (0, i)), pl.BlockSpec((NM, CB), lambda i: (0, i)),
                   _full_spec((3, DA, TB)), _full_spec((DA, TB)), _full_spec((DA, TB))],
        out_shape=[jax.ShapeDtypeStruct((DPROJ, D), BF16), jax.ShapeDtypeStruct((NM, D), F32),
                   jax.ShapeDtypeStruct((3, DA, TB), F32), jax.ShapeDtypeStruct((DA, TB), F32),
                   jax.ShapeDtypeStruct((DA, TB), F32)],
        compiler_params=_params(),
    )(w_all, small_all, attn_gain, conv_gain)


def _shard_w_in_grads(dw_main, dw_f):
    def body(dm_ref, df_ref, p_ref):
        def ref_rows(lo, hi):
            pieces, r = [], lo
            while r < hi:
                if r < F0:
                    n = min(hi, F0) - r
                    pieces.append(dm_ref[r:r + n, :])
                elif r < F0 + H:
                    n = min(hi, F0 + H) - r
                    pieces.append(df_ref[r - F0:r - F0 + n, :])
                else:
                    n = hi - r
                    pieces.append(dm_ref[r - H:r - H + n, :])
                r += n
            return pieces

        for i in range(NDEV):
            rows = jnp.concatenate(ref_rows(i * WSH, (i + 1) * WSH) + [jnp.zeros((WSHP - WSH, CB), F32)], axis=0)
            p_ref[i] = rows.astype(BF16)

    col = lambda rows: pl.BlockSpec((rows, CB), lambda i: (0, i))
    return pl.pallas_call(
        body, name="shard_w_in_grads", grid=(D // CB,),
        in_specs=[col(NSEC * DA), col(DF)],
        out_specs=pl.BlockSpec((NDEV, WSHP, CB), lambda i: (0, 0, i)),
        out_shape=jax.ShapeDtypeStruct((NDEV, WSHP, D), BF16),
        compiler_params=_params(),
    )(dw_main, dw_f)


SMALL = ("norm_g", "final_norm_g", "attn_norm_g", "conv_norm_g", "b_f", "meta", "conv_w")


def _as_rows(x):
    return jnp.concatenate([x[:, r * TB:(r + 1) * TB] for r in range(x.shape[1] // TB)], axis=0)


def _as_line(rows):
    return jnp.concatenate([rows[r:r + 1, :] for r in range(rows.shape[0])], axis=1)


def _pad_rows(x, n=8):
    return jnp.concatenate([x, jnp.zeros((n - x.shape[0], x.shape[1]), F32)], axis=0)


def _tile_rows(a, rows, lanes=TB):
    a = a.reshape(rows, lanes)
    return jnp.pad(a, ((0, -rows % 8), (0, TB - lanes)))


def _pack_small_grads(dg_norm, dg_final, dga_p, dgc_p, dcw_p, db_b, dmeta, loss):
    def body(dgn_ref, dgf_ref, dga_ref, dgc_ref, dcw_ref, db_ref, dmeta_ref, loss_ref, out_ref):
        def lane_sums(p):
            return jnp.sum(p.T, axis=0, keepdims=True)

        lane = lax.broadcasted_iota(jnp.int32, (1, TB), 1)
        b_row = jnp.where(lane == H, loss_ref[...], 0.0)
        for h in range(H):
            b_row = b_row + jnp.where(lane == h, db_ref[h:h + 1, :], 0.0)
        common = jnp.concatenate([
            _as_rows(dgn_ref[...]), _as_rows(dgf_ref[...]), _pad_rows(_as_rows(lane_sums(dga_ref[...]))),
            _pad_rows(_as_rows(lane_sums(dgc_ref[...]))), _pad_rows(b_row)], axis=0)
        dcw = [lane_sums(dcw_ref[k]) for k in range(3)]
        for j in range(NDEV):
            cw = jnp.concatenate(
                [jnp.concatenate([r[:, j * DH:(j + 1) * DH], jnp.zeros((1, TB - DH), F32)], axis=1) for r in dcw],
                axis=0)
            out_ref[j] = jnp.concatenate([common, dmeta_ref[:, j * TB:(j + 1) * TB], _pad_rows(cw)], axis=0)

    return pl.pallas_call(
        body, name="pack_small_grads", out_shape=jax.ShapeDtypeStruct((NDEV, SROWS, TB), F32),
    )(dg_norm, dg_final, dga_p, dgc_p, dcw_p, db_b, dmeta, loss)


def _adamw_small(own, land, params):
    flat = [a for n in SMALL for a in params[n]]

    def body(*refs):
        own_ref, land_ref = refs[:2]
        ins = refs[2:2 + 3 * len(SMALL)]
        outs = refs[2 + 3 * len(SMALL):]
        g = _pick_slab(0, own_ref, land_ref, slice(0, SROWS))
        for j in range(1, NDEV):
            g = g + _pick_slab(j, own_ref, land_ref, slice(0, SROWS))
        grads = dict(
            norm_g=_as_line(g[0:8]), final_norm_g=_as_line(g[8:16]), attn_norm_g=_as_line(g[16:20]),
            conv_norm_g=_as_line(g[24:28]), b_f=g[32:33, :H], meta=g[40:56], conv_w=g[56:59, :DH][None])
        for i, n in enumerate(SMALL):
            w_ref, m_ref, v_ref = ins[3 * i:3 * i + 3]
            d, mn, vn = _adamw(w_ref[...], grads[n], m_ref[...], v_ref[...])
            for o_ref, val in zip(outs[4 * i:4 * i + 4], (grads[n], d, mn, vn)):
                o_ref[...] = val
        outs[-1][...] = g[32:33, H:H + 1]

    shapes = [jax.ShapeDtypeStruct(params[n][0].shape, F32) for n in SMALL for _ in range(4)]
    res = pl.pallas_call(
        body, name="adamw_small", out_shape=shapes + [jax.ShapeDtypeStruct((1, 1), F32)],
    )(own, land, *flat)
    return {n: res[4 * i:4 * i + 4] for i, n in enumerate(SMALL)}, res[-1]


def kernel(x, meta, norm_g, w_in, b_f, conv_w, attn_norm_g, conv_norm_g, w_out, final_norm_g, loss_target, m_meta, m_norm_g, m_w_in, m_b_f, m_conv_w, m_attn_norm_g, m_conv_norm_g, m_w_out, m_final_norm_g, v_meta, v_norm_g, v_w_in, v_b_f, v_conv_w, v_attn_norm_g, v_conv_norm_g, v_w_out, v_final_norm_g):
    seq = x.shape[1]
    L = seq + TB
    assert x.shape == (1, seq, D) and L % TT == 0 and w_in.shape == (1, D, WSH)
    x2 = x[0]
    tgt = loss_target[0]

    w_in_slab = jnp.pad(w_in[0].T, ((0, WSHP - WSH), (0, 0))).astype(BF16)
    w_out_slab = w_out[0].astype(BF16)
    meta_slab = jnp.concatenate([meta, _tile_rows(conv_w[0], 3, DH)], axis=0)
    wout_flight = _split_start(w_out_slab, "gather_w_out_start", per_peer=False)
    w_all, small_all = _all_gather([w_in_slab, meta_slab], "gather_w_in")

    w_t, meta_full, cw_b, ga_b, gcn_b = _unshard_w_in(w_all, small_all, attn_norm_g, conv_norm_g)

    u, proj_t, f_t, ktok, vtok = _inproj_fwd(x2, meta_full, norm_g, w_t, L, after=wout_flight[4])
    cq, kaug, sg = _fgate_fwd(f_t, b_f.reshape(H, 1), ktok, L)
    o_t, lse = _attn_fwd(proj_t, kaug, cq, L)
    mix_t = _gate_fwd(o_t, proj_t, cw_b, ga_b, gcn_b, L)

    w_out_own, w_out_land = _split_wait(wout_flight, mix_t, "gather_w_out_wait", per_peer=False)
    w_out_full = _unshard_w_out(w_out_own, w_out_land)
    dout, dmix_t, dw_out, loss_part, dg_final = _outproj(
        mix_t, w_out_full, x2, meta_full, final_norm_g.reshape(1, D), tgt, L)
    dwo_flight = _split_start(dw_out.reshape(NDEV, D // NDEV, D), "exchange_dw_out_start", per_peer=True)
    do_t, dd, dg5_t, dga_p, dgc_p, dcw_p = _gate_bwd(dmix_t, o_t, proj_t, cw_b, ga_b, gcn_b, L, after=dwo_flight[4])
    dq_t, dk_t, dv_t, dck, dcq = _attn_bwd(proj_t, kaug, vtok, do_t, lse, dd, cq, L)
    df_t, db_f = _fgate_bwd(dcq, dck, sg, L)
    dw_main, dw_f = _inproj_bwd_w(u, dq_t, dk_t, dv_t, dg5_t, df_t, L)
    dwi_flight = _split_start(_shard_w_in_grads(dw_main, dw_f), "exchange_dw_in_start", per_peer=True)
    grad_x, dmeta, dg_norm = _inproj_bwd_x(
        w_t, dq_t, dk_t, dv_t, dg5_t, df_t, dout, x2, meta_full, norm_g, L, after=dwi_flight[4])
    small_parts = _pack_small_grads(dg_norm, dg_final, dga_p, dgc_p, dcw_p, db_f, dmeta, loss_part)
    small_flight = _split_start(small_parts, "exchange_small_start", per_peer=True)
    dwo_own, dwo_land = _split_wait(dwo_flight, small_flight[4], "exchange_dw_out_wait", per_peer=True)
    dwi_own, dwi_land = _split_wait(dwi_flight, dwo_land, "exchange_dw_in_wait", per_peer=True)

    big_out = _adamw_big(dwi_own, dwi_land, dwo_own, dwo_land,
                         w_in[0].T, m_w_in[0].T, v_w_in[0].T, w_out, m_w_out, v_w_out)
    g_w_in, d_w_in, nm_w_in, nv_w_in = [a.T[None] for a in big_out[:4]]
    g_w_out, d_w_out, nm_w_out, nv_w_out = big_out[4:]
    sm_own, sm_land = _split_wait(small_flight, big_out[4], "exchange_small_wait", per_peer=True)
    line = lambda a: a.reshape(1, D)
    small, loss = _adamw_small(sm_own, sm_land, dict(
        norm_g=(norm_g, m_norm_g, v_norm_g),
        final_norm_g=(line(final_norm_g), line(m_final_norm_g), line(v_final_norm_g)),
        attn_norm_g=(attn_norm_g, m_attn_norm_g, v_attn_norm_g),
        conv_norm_g=(conv_norm_g, m_conv_norm_g, v_conv_norm_g),
        b_f=(b_f, m_b_f, v_b_f), meta=(meta, m_meta, v_meta), conv_w=(conv_w, m_conv_w, v_conv_w)))
    small["final_norm_g"] = [a.reshape(D) for a in small["final_norm_g"]]
    order = ("meta", "norm_g", "w_in", "b_f", "conv_w", "attn_norm_g", "conv_norm_g", "w_out", "final_norm_g")
    groups = []
    for k, (wi, wo) in enumerate(((g_w_in, g_w_out), (d_w_in, d_w_out), (nm_w_in, nm_w_out), (nv_w_in, nv_w_out))):
        d = dict({n: small[n][k] for n in SMALL}, w_in=wi, w_out=wo)
        groups.append([d[n] for n in order])
    return (loss[0, 0], grad_x[None], *groups[0], *groups[1], *groups[2], *groups[3])
```

```python
import jax
import jax.numpy as jnp
from jax import lax
from jax.experimental import pallas as pl
from jax.experimental.pallas import tpu as pltpu

F32 = jnp.float32
BF16 = jnp.bfloat16

D = 1024
DA = 512
H = 8
DH = 64
NM = 16
TB = 128
P0 = TB - NM
TT = 3 * TB
HG = 8
NDEV = 8
NSEC = 8
DF = 16
DPROJ = NSEC * DA + DF
WSH = 513
WSHP = 528
WROWS = WSHP + D // NDEV
SROWS = 64
EPS = 1e-6
NEG = -1e30
LOG2E = 1.4426950408889634
LN2 = 0.6931471805599453
QSCALE = DH ** -0.5 * LOG2E
KA = 128
CB = 256
VMEM_LIMIT = 56 * 1024 * 1024

ADAM_LR = 0.001
ADAM_B1 = 0.9
ADAM_B2 = 0.999
ADAM_EPS = 1e-08
ADAM_WD = 0.01
ADAM_STEP = 10

NT_DIMS = (((1,), (1,)), ((), ()))
TN_DIMS = (((0,), (0,)), ((), ()))
MESH = pl.DeviceIdType.MESH


def _params(n_axes=1, vmem=VMEM_LIMIT):
    return pltpu.CompilerParams(dimension_semantics=("arbitrary",) * n_axes, vmem_limit_bytes=vmem)


def _dot(a, b, dims=None):
    if dims is None:
        return jnp.dot(a, b, preferred_element_type=F32)
    return lax.dot_general(a, b, dims, preferred_element_type=F32)


def _my_place():
    return lax.axis_index("x"), lax.axis_index("y"), lax.axis_index("c")


def _all_gather(xs, name):
    n = len(xs)

    def body(*refs):
        x_refs, out_refs = refs[:n], refs[n:2 * n]
        send_sems, recv_sems, local_sems = refs[2 * n:]
        mx, my, mc = _my_place()

        def across(px, py, pc, axis_a):
            flip_x = pc if axis_a else 1 - pc
            return (px + flip_x) % 2, (py + 1 - flip_x) % 2, pc

        def idx(p):
            return 4 * p[0] + 2 * p[1] + p[2]

        me, sib = (mx, my, mc), (mx, my, 1 - mc)
        a_nbr, b_nbr = across(*me, True), across(*me, False)
        diag = across(*b_nbr, True)
        sib_a, sib_b = across(*sib, True), across(*sib, False)
        sib_diag = across(*sib_b, True)

        waits = []
        for t in range(n):
            out_ref = out_refs[t]

            def copy(k, block, to, src=None, out_ref=out_ref, t=t):
                return pltpu.make_async_remote_copy(
                    src_ref=out_ref.at[idx(block)] if src is None else src, dst_ref=out_ref.at[idx(block)],
                    send_sem=send_sems.at[7 * t + k], recv_sem=recv_sems.at[7 * t + k],
                    device_id=to, device_id_type=MESH)

            mine = pltpu.make_async_copy(x_refs[t], out_ref.at[idx(me)], local_sems.at[t])
            mine.start()
            started = [copy(0, me, sib, src=x_refs[t]), copy(1, me, a_nbr, src=x_refs[t]),
                       copy(2, me, b_nbr, src=x_refs[t])]
            for cp in started:
                cp.start()
            waits.append((copy, mine, started))
        relays = ((1, a_nbr, ((3, b_nbr), (4, sib))), (2, b_nbr, ((5, sib),)), (3, diag, ((6, sib),)))
        for landed, block, onward in relays:
            for copy, _, started in waits:
                copy(landed, block, me).wait_recv()
                for k, to in onward:
                    started.append(copy(k, block, to))
                    started[-1].start()
        for copy, mine, started in waits:
            for k, block in ((0, sib), (4, sib_a), (5, sib_b), (6, sib_diag)):
                copy(k, block, me).wait_recv()
            for cp in started:
                cp.wait_send()
            mine.wait()

    any_spec = pl.BlockSpec(memory_space=pl.ANY)
    return pl.pallas_call(
        body, name=name,
        out_shape=[jax.ShapeDtypeStruct((NDEV,) + x.shape, x.dtype) for x in xs],
        in_specs=[any_spec] * n, out_specs=[any_spec] * n,
        scratch_shapes=[pltpu.SemaphoreType.DMA((7 * n,)), pltpu.SemaphoreType.DMA((7 * n,)),
                        pltpu.SemaphoreType.DMA((n,))],
    )(*xs)


_HBM = pl.BlockSpec(memory_space=pltpu.HBM)
_UNREAD = pl.BlockSpec(memory_space=pl.ANY)
_SEM = pl.BlockSpec(memory_space=pltpu.SEMAPHORE)
_EFFECT = pltpu.SideEffectType.DATAFLOW_SIDE_EFFECTING


def _peer_of(m, place):
    mx, my, mc = place
    return ((1 - mx) if m & 4 else mx, (1 - my) if m & 2 else my, (1 - mc) if m & 1 else mc)


def _party(chips):
    if chips:
        return (lambda p: 2 * p[0] + p[1]), (2, 4, 6)
    return (lambda p: 4 * p[0] + 2 * p[1] + p[2]), tuple(range(1, NDEV))


def _split_copies(src_ref, land_ref, send_sems, recv_sems, per_peer, incoming, chips):
    place = _my_place()
    slot, masks = _party(chips)
    me = slot(place)
    out = []
    for k, m in enumerate(masks):
        there = _peer_of(m, place)
        peer = slot(there)
        src = (src_ref.at[me] if incoming else src_ref.at[peer]) if per_peer else src_ref
        out.append(pltpu.make_async_remote_copy(
            src_ref=src, dst_ref=land_ref.at[peer if incoming else me],
            send_sem=send_sems.at[k], recv_sem=recv_sems.at[k], device_id=there, device_id_type=MESH))
    return out


def _split_start(src, name, per_peer, chips=False):
    slab = src.shape[1:] if per_peer else src.shape
    n = len(_party(chips)[1])

    def body(src_ref, land_ref, send_sems, recv_sems, src_thru, land_thru, token):
        for cp in _split_copies(src_ref, land_ref, send_sems, recv_sems, per_peer, False, chips):
            cp.start()
        token[...] = jnp.zeros_like(token)

    return pl.pallas_call(
        body, name=name,
        out_shape=(pltpu.SemaphoreType.DMA((n,)), pltpu.SemaphoreType.DMA((n,)),
                   pltpu.HBM(src.shape, src.dtype), pltpu.HBM((n + 1,) + slab, src.dtype),
                   jax.ShapeDtypeStruct((8, TB), F32)),
        in_specs=(_HBM, _HBM), out_specs=(_SEM, _SEM, _HBM, _HBM, pl.BlockSpec(memory_space=pltpu.VMEM)),
        input_output_aliases={0: 2, 1: 3},
        compiler_params=pltpu.CompilerParams(has_side_effects=_EFFECT),
    )(pltpu.with_memory_space_constraint(src, pltpu.HBM),
      pltpu.with_memory_space_constraint(lax.empty((n + 1,) + slab, src.dtype), pltpu.HBM))


def _split_wait(handles, after, name, per_peer, chips=False):
    send_sems, recv_sems, src_thru, land_thru, _ = handles

    def body(src_ref, land_ref, send_sems, recv_sems, after_ref, src_out, land_out):
        for cp in _split_copies(src_ref, land_ref, send_sems, recv_sems, per_peer, False, chips):
            cp.wait_send()
        for cp in _split_copies(src_ref, land_ref, send_sems, recv_sems, per_peer, True, chips):
            cp.wait_recv()

    return pl.pallas_call(
        body, name=name,
        out_shape=(pltpu.HBM(src_thru.shape, src_thru.dtype), pltpu.HBM(land_thru.shape, land_thru.dtype)),
        in_specs=(_HBM, _HBM, _SEM, _SEM, pl.BlockSpec(memory_space=pl.ANY)), out_specs=(_HBM, _HBM),
        input_output_aliases={0: 0, 1: 1},
        compiler_params=pltpu.CompilerParams(has_side_effects=_EFFECT),
    )(src_thru, land_thru, send_sems, recv_sems, after)


def _pick_slab(j, own_ref, land_ref, rows, per_peer=True, chips=False):
    me = _party(chips)[0](_my_place())
    own = (lambda: own_ref[j, rows, :]) if per_peer else (lambda: own_ref[rows, :])
    return lax.cond(me == j, own, lambda: land_ref[j, rows, :])


def _pair_exchange(p, name):
    def body(p_ref, got_ref, send_sems, recv_sems):
        mx, my, mc = _my_place()
        copies = [pltpu.make_async_remote_copy(
            src_ref=p_ref.at[2 * q + 1 - mc], dst_ref=got_ref.at[q], send_sem=send_sems.at[q],
            recv_sem=recv_sems.at[q], device_id=(mx, my, 1 - mc), device_id_type=MESH) for q in range(4)]
        for cp in copies:
            cp.start()
        for cp in copies:
            cp.wait_recv()
        for cp in copies:
            cp.wait_send()

    any_spec = pl.BlockSpec(memory_space=pl.ANY)
    return pl.pallas_call(
        body, name=name, out_shape=jax.ShapeDtypeStruct((4,) + p.shape[1:], p.dtype),
        in_specs=[any_spec], out_specs=any_spec,
        scratch_shapes=[pltpu.SemaphoreType.DMA((4,)), pltpu.SemaphoreType.DMA((4,))],
    )(p)


def _pair_sum(p, got):
    rows = p.shape[1]

    def body(p_ref, got_ref, out_ref):
        mc = lax.axis_index("c")
        for q in range(4):
            out_ref[q] = (p_ref[2 * q + mc].astype(F32) + got_ref[q].astype(F32)).astype(BF16)

    blk = lambda n: pl.BlockSpec((n, rows, CB), lambda i: (0, 0, i))
    return pl.pallas_call(
        body, name="pair_sum", grid=(D // CB,), in_specs=[blk(NDEV), blk(4)], out_specs=blk(4),
        out_shape=jax.ShapeDtypeStruct((4, rows, D), BF16), compiler_params=_params(),
    )(p, got)


def _h_block(t, x_ref, meta_ref):
    first = jnp.concatenate([jnp.zeros((P0, D), F32), meta_ref[...]], axis=0)
    return jnp.where(t == 0, first, x_ref[...])


def _x_specs3(tile=lambda j: j):
    return [pl.BlockSpec((TB, D), lambda j: (jnp.maximum(3 * tile(j) - 1, 0), 0)),
            pl.BlockSpec((TB, D), lambda j: (3 * tile(j), 0)),
            pl.BlockSpec((TB, D), lambda j: (3 * tile(j) + 1, 0))]


def _h_tile(j, xa_ref, xb_ref, xc_ref, meta_ref):
    first = jnp.concatenate([jnp.zeros((P0, D), F32), meta_ref[...]], axis=0)
    return jnp.concatenate([jnp.where(j == 0, first, xa_ref[...]), xb_ref[...], xc_ref[...]], axis=0)


def _full_spec(shape):
    return pl.BlockSpec(shape, lambda *_: (0,) * len(shape))


def _sigmoid(z):
    return 1.0 / (1.0 + jnp.exp(-z))


def _lane_tiles_sum(x):
    out = x[:, :TB]
    for i in range(1, x.shape[1] // TB):
        out = out + x[:, i * TB:(i + 1) * TB]
    return out


def _inproj_fwd(x, meta_full, norm_g, w_t, L, after):
    nj = L // TT

    def body(xa_ref, xb_ref, xc_ref, meta_ref, g_ref, w_ref, _, u_ref, proj_ref, f_ref, ktok_ref, vtok_ref):
        hb = _h_tile(pl.program_id(0), xa_ref, xb_ref, xc_ref, meta_ref)
        r = lax.rsqrt(jnp.mean(hb * hb, axis=-1, keepdims=True) + EPS)
        u = (hb * r * g_ref[...]).astype(BF16)
        u_ref[...] = u
        for s in range(NSEC):
            p = _dot(u, w_ref[s * DA:(s + 1) * DA, :], NT_DIMS)
            if s == 0:
                p = p * QSCALE
            if s in (1, 2):
                tok_ref = ktok_ref if s == 1 else vtok_ref
                for h in range(H):
                    tok_ref[h] = p[:, h * DH:(h + 1) * DH].astype(BF16)
            proj_ref[s * DA:(s + 1) * DA, :] = p.T.astype(BF16)
        f_ref[...] = _dot(w_ref[NSEC * DA:DPROJ, :], u, NT_DIMS)[:H]

    return pl.pallas_call(
        body, name="inproj_fwd", grid=(nj,),
        in_specs=_x_specs3() + [_full_spec((NM, D)), _full_spec((1, D)), _full_spec((DPROJ, D)), _UNREAD],
        out_specs=[
            pl.BlockSpec((TT, D), lambda t: (t, 0)),
            pl.BlockSpec((NSEC * DA, TT), lambda t: (0, t)),
            pl.BlockSpec((H, TT), lambda t: (0, t)),
            pl.BlockSpec((H, TT, DH), lambda t: (0, t, 0)),
            pl.BlockSpec((H, TT, DH), lambda t: (0, t, 0)),
        ],
        out_shape=[
            jax.ShapeDtypeStruct((L, D), BF16),
            jax.ShapeDtypeStruct((NSEC * DA, L), BF16),
            jax.ShapeDtypeStruct((H, L), F32),
            jax.ShapeDtypeStruct((H, L, DH), BF16),
            jax.ShapeDtypeStruct((H, L, DH), BF16),
        ],
        compiler_params=_params(),
    )(x, x, x, meta_full, norm_g, w_t, after)


def _split3(x):
    hi = x.astype(BF16).astype(F32)
    r = x - hi
    mid = r.astype(BF16).astype(F32)
    return hi, mid, (r - mid).astype(BF16).astype(F32)


def _bias_rows(bias):
    one = jnp.ones((1, TT), F32)
    zero = jnp.zeros((1, TT), F32)
    parts = [zero] * 3 if bias is None else list(_split3(bias))
    return jnp.concatenate([one] * 3 + parts + [zero] * (DF - 6), axis=0).astype(BF16)


def _fgate_fwd(f_t, b_col, ktok, L):
    nb = L // TB

    def body(f_ref, b_ref, ktok_ref, cq_ref, kaug_ref, sg_ref):
        z = f_ref[...] + b_ref[...]
        idx = lax.broadcasted_iota(jnp.int32, (H, L), 1)
        real = idx >= P0
        lf = jnp.where(real, jnp.minimum(z, 0.0) - jnp.log1p(jnp.exp(-jnp.abs(z))), 0.0)
        sg_ref[...] = jnp.where(real, 1.0 / (1.0 + jnp.exp(z)), 0.0)
        c = lf
        s = 1
        while s < L:
            c = c + jnp.where(idx >= s, pltpu.roll(c, s, 1), 0.0)
            s *= 2
        c = c * LOG2E
        for h in range(H):
            cq_ref[h] = c[h:h + 1, :]
        hi, mid, lo = _split3(-jnp.where(real, c, -NEG))
        lane = lax.broadcasted_iota(jnp.int32, (TB, KA), 1)
        ones = jnp.ones((3, TB), F32)
        for h in range(H):
            for b in range(nb):
                blk = slice(b * TB, (b + 1) * TB)
                cols = jnp.concatenate([
                    jnp.zeros((DH, TB), F32), hi[h:h + 1, blk], mid[h:h + 1, blk], lo[h:h + 1, blk], ones,
                    jnp.zeros((KA - DH - 6, TB), F32)], axis=0).T
                k = jnp.concatenate([ktok_ref[h, blk, :].astype(F32), jnp.zeros((TB, KA - DH), F32)], axis=1)
                kaug_ref[h, blk, :] = jnp.where(lane < DH, k, cols).astype(BF16)

    return pl.pallas_call(
        body, name="fgate_fwd",
        out_shape=[
            jax.ShapeDtypeStruct((H, 1, L), F32),
            jax.ShapeDtypeStruct((H, L, KA), BF16),
            jax.ShapeDtypeStruct((H, L), F32),
        ],
        compiler_params=pltpu.CompilerParams(vmem_limit_bytes=VMEM_LIMIT),
    )(f_t, b_col, ktok)


def _causal_mask():
    r = lax.broadcasted_iota(jnp.int32, (TT, TT), 0)
    c = lax.broadcasted_iota(jnp.int32, (TT, TT), 1)
    return r <= c


def _attn_fwd(proj_t, kaug, cq, L):
    nq = L // TT

    def body(q_ref, qn_ref, kaug_ref, v_ref, cq_ref, o_ref, lse_ref,
             qa_scr, s_scr, cmax_scr, m_scr, p_scr, alpha_scr, acc_scr):
        j = pl.program_id(0)
        rows = [slice(g * DH, (g + 1) * DH) for g in range(HG)]
        ones = jnp.ones((DF, TT), BF16)

        def load_queries(ref):
            for g in range(HG):
                qa_scr[g] = jnp.concatenate(
                    [ref[rows[g], :], _bias_rows(None), jnp.zeros((KA - DH - DF, TT), BF16)], axis=0)

        def scores(kt, masked):
            k_off = pl.multiple_of(kt * TT, TT)
            for g in range(HG):
                s = _dot(kaug_ref[g, pl.ds(k_off, TT), :], qa_scr[g])
                if masked:
                    s = jnp.where(_causal_mask(), s, NEG)
                s_scr[g] = s
                cmax_scr[g] = jnp.max(s, axis=0, keepdims=True)

        def softmax():
            for g in range(HG):
                m_old = m_scr[g]
                m_new = jnp.maximum(m_old, cmax_scr[g])
                alpha_scr[g] = jnp.exp2(m_old - m_new)
                p_scr[g] = jnp.exp2(s_scr[g] - m_new).astype(BF16)
                m_scr[g] = m_new

        def weighted_sum(kt):
            k_off = pl.multiple_of(kt * TT, TT)
            for g in range(HG):
                v1 = jnp.concatenate([v_ref[rows[g], pl.ds(k_off, TT)], ones], axis=0)
                acc_scr[g] = alpha_scr[g] * acc_scr[g] + _dot(v1, p_scr[g])

        @pl.when(j == 0)
        def _():
            load_queries(q_ref)
            scores(0, True)

        m_scr[...] = jnp.full_like(m_scr, NEG)
        acc_scr[...] = jnp.zeros_like(acc_scr)

        @pl.when(j >= 1)
        def _():
            softmax()
            scores(j - 1, False)

        def step(i, c):
            weighted_sum(j - i + 1)
            softmax()
            scores(j - i - 1, False)
            return c

        lax.fori_loop(1, j, step, 0)

        @pl.when(j >= 1)
        def _():
            weighted_sum(1)

        @pl.when(j < nq - 1)
        def _():
            softmax()
            load_queries(qn_ref)
            scores(j + 1, True)
            weighted_sum(0)

        @pl.when(j == nq - 1)
        def _():
            softmax()
            weighted_sum(0)

        for g in range(HG):
            l = acc_scr[g, DH:DH + 1, :]
            o_ref[rows[g], :] = acc_scr[g, :DH, :] * (1.0 / l)
            lse_ref[g] = m_scr[g] + jnp.log2(l) + cq_ref[g]

    assert HG == H
    return pl.pallas_call(
        body, name="attn_fwd", grid=(nq,),
        in_specs=[
            pl.BlockSpec((DA, TT), lambda j: (0, j)),
            pl.BlockSpec((DA, TT), lambda j: (0, jnp.minimum(j + 1, nq - 1))),
            pl.BlockSpec((H, L, KA), lambda j: (0, 0, 0)),
            pl.BlockSpec((DA, L), lambda j: (2, 0)),
            pl.BlockSpec((H, 1, TT), lambda j: (0, 0, j)),
        ],
        out_specs=[
            pl.BlockSpec((DA, TT), lambda j: (0, j)),
            pl.BlockSpec((H, 1, TT), lambda j: (0, 0, j)),
        ],
        out_shape=[jax.ShapeDtypeStruct((DA, L), F32), jax.ShapeDtypeStruct((H, 1, L), F32)],
        scratch_shapes=[pltpu.VMEM((HG, KA, TT), BF16), pltpu.VMEM((HG, TT, TT), F32), pltpu.VMEM((HG, 1, TT), F32),
                        pltpu.VMEM((HG, 1, TT), F32), pltpu.VMEM((HG, TT, TT), BF16), pltpu.VMEM((HG, 1, TT), F32),
                        pltpu.VMEM((HG, DH + DF, TT), F32)],
        compiler_params=_params(),
    )(proj_t, proj_t, kaug, proj_t, cq)


def _gate_group(rows, o_ref, za_ref, gb_ref, gc_ref, xc_ref, zc_ref, gcp_ref, xcp_ref, cw_ref, ga_ref, gcn_ref, first):
    n_rep = TT // TB
    f32 = lambda r: r[rows, :].astype(F32)
    o, za, gb, gc, xc, zc = o_ref[rows, :], f32(za_ref), f32(gb_ref), f32(gc_ref), f32(xc_ref), f32(zc_ref)
    a = gc * xc
    a_prev = jnp.where(first, 0.0, f32(gcp_ref) * f32(xcp_ref))
    full = jnp.concatenate([a_prev, a], axis=1)
    a1 = pltpu.roll(full, 1, 1)[:, TB:]
    a2 = pltpu.roll(full, 2, 1)[:, TB:]
    w0 = jnp.tile(cw_ref[0, rows, :], (1, n_rep))
    w1 = jnp.tile(cw_ref[1, rows, :], (1, n_rep))
    w2 = jnp.tile(cw_ref[2, rows, :], (1, n_rep))
    cv = w0 * a2 + w1 * a1 + w2 * a
    e = gb * cv
    rc = lax.rsqrt(jnp.mean(e * e, axis=0, keepdims=True) + EPS)
    ec = e * rc
    ra = lax.rsqrt(jnp.mean(o * o, axis=0, keepdims=True) + EPS)
    oa = o * ra
    g_a = jnp.tile(ga_ref[rows, :], (1, n_rep))
    g_c = jnp.tile(gcn_ref[rows, :], (1, n_rep))
    sa = _sigmoid(za)
    sc = _sigmoid(zc)
    return dict(o=o, za=za, gb=gb, gc=gc, xc=xc, zc=zc, a=a, a1=a1, a2=a2, w0=w0, w1=w1, w2=w2, cv=cv, e=e,
                rc=rc, ec=ec, ra=ra, oa=oa, g_a=g_a, g_c=g_c, sa=sa, sc=sc)


def _gate_specs(nj, rev):
    def jj(i):
        return (nj - 1 - i) if rev else i

    def sec(s):
        return pl.BlockSpec((DA, TT), lambda i: (s, jj(i)))

    def halo(s):
        return pl.BlockSpec((DA, TB), lambda i: (s, jnp.maximum(3 * jj(i) - 1, 0)))

    return [pl.BlockSpec((DA, TT), lambda i: (0, jj(i))), sec(3), sec(4), sec(5), sec(6), sec(7), halo(5), halo(6),
            _full_spec((3, DA, TB)), _full_spec((DA, TB)), _full_spec((DA, TB))]


def _gate_fwd(o_t, proj_t, cw_b, ga_b, gcn_b, L):
    nj = L // TT

    def body(o_ref, za_ref, gb_ref, gc_ref, xc_ref, zc_ref, gcp_ref, xcp_ref, cw_ref, ga_ref, gcn_ref, mix_ref):
        j = pl.program_id(0)

        def group(h, c):
            r0 = pl.multiple_of(h * DH, DH)
            g = _gate_group(pl.ds(r0, DH), o_ref, za_ref, gb_ref, gc_ref, xc_ref, zc_ref, gcp_ref, xcp_ref,
                            cw_ref, ga_ref, gcn_ref, j == 0)
            mix_ref[pl.ds(r0, DH), :] = (g["oa"] * g["g_a"] * (g["za"] * g["sa"])).astype(BF16)
            mix_ref[pl.ds(DA + r0, DH), :] = (g["ec"] * g["g_c"] * (g["zc"] * g["sc"])).astype(BF16)
            return c

        lax.fori_loop(0, H, group, 0, unroll=2)

    return pl.pallas_call(
        body, name="gate_fwd", grid=(nj,),
        in_specs=_gate_specs(nj, False),
        out_specs=pl.BlockSpec((2 * DA, TT), lambda j: (0, j)),
        out_shape=jax.ShapeDtypeStruct((2 * DA, L), BF16),
        compiler_params=_params(),
    )(o_t, proj_t, proj_t, proj_t, proj_t, proj_t, proj_t, proj_t, cw_b, ga_b, gcn_b)


def _outproj(mix_t, w_out, x, meta_full, fng, target, L):
    nj = L // TT
    rp = NM
    n_bwd = 8
    cb = D // 4
    assert P0 % rp == 0 and TB % rp == 0 and (TT // rp) % n_bwd == 0

    def body(mix_ref, mixp_ref, w_ref, xa_ref, xb_ref, xc_ref, meta_ref, g_ref, ta_ref, tb_ref, tc_ref,
             dout_ref, dmix_ref, dwb_ref, loss_ref, dg_ref, dw_ref, o_scr, db_new, db_old, sq_acc, dg_acc):
        t = pl.program_id(0)

        def loss_rows(c):
            blk = c // (TB // rp)
            rows, out_rows = pl.ds((c % (TB // rp)) * rp, rp), pl.ds(c * rp, rp)
            h = (xa_ref, xb_ref, xc_ref)[blk][rows, :]
            if blk == 0:
                first = meta_ref[...] if c == P0 // rp else jnp.zeros((rp, D), F32)
                h = jnp.where(t == 0, first, h)
            o = o_scr[out_rows, :] + h
            r = lax.rsqrt(jnp.mean(o * o, axis=-1, keepdims=True) + EPS)
            orn = o * r
            g = g_ref[...]
            diff = orn * g - (ta_ref, tb_ref, tc_ref)[blk][rows, :]
            if blk == 0:
                diff = diff * jnp.where(t > 0, 1.0, 0.0)
            gy = diff * (g * (1.0 / D))
            dout = r * (gy - orn * jnp.mean(gy * orn, axis=-1, keepdims=True))
            dout_ref[out_rows, :] = dout
            db_new[out_rows, :] = dout.astype(BF16)
            sq, go = diff * diff, diff * orn
            sq_acc[...] += sq[:8] + sq[8:]
            dg_acc[...] += go[:8] + go[8:]

        def backward_cols(n):
            if n < 4:
                cols = slice(n * cb, (n + 1) * cb)
                dmix_ref[cols, :] = _dot(db_old[...], w_ref[cols, :], NT_DIMS).T.astype(BF16)
            else:
                cols = slice((n - 4) * cb, (n - 3) * cb)
                dw_ref[:, cols] += _dot(mixp_ref[...], db_old[:, cols])

        def step(forward, backward):
            if forward:
                o_scr[...] = _dot(mix_ref[...], w_ref[...], TN_DIMS)
            per = TT // rp // n_bwd
            for k in range(n_bwd):
                if forward:
                    for c in range(per * k, per * (k + 1)):
                        loss_rows(c)
                if backward:
                    backward_cols(k)
            if forward:
                db_old[...] = db_new[...]

        @pl.when(t == 0)
        def _():
            dw_ref[...] = jnp.zeros_like(dw_ref)
            sq_acc[...] = jnp.zeros_like(sq_acc)
            dg_acc[...] = jnp.zeros_like(dg_acc)
            step(True, False)

        @pl.when((t > 0) & (t < nj))
        def _():
            step(True, True)

        @pl.when(t == nj)
        def _():
            step(False, True)
            dwb_ref[...] = dw_ref[...].astype(BF16)
            loss_ref[...] = jnp.sum(sq_acc[...], keepdims=True) * (0.5 / D)
            dg_ref[...] = jnp.sum(dg_acc[...], axis=0, keepdims=True) * (1.0 / D)

    cur = lambda t: jnp.minimum(t, nj - 1)
    prev = lambda t: jnp.maximum(t - 1, 0)
    return pl.pallas_call(
        body, name="outproj", grid=(nj + 1,),
        in_specs=[pl.BlockSpec((D, TT), lambda t: (0, cur(t))), pl.BlockSpec((D, TT), lambda t: (0, prev(t))),
                  _full_spec((D, D))] + _x_specs3(cur) + [_full_spec((NM, D)), _full_spec((1, D))] + _x_specs3(cur),
        out_specs=[pl.BlockSpec((TT, D), lambda t: (cur(t), 0)), pl.BlockSpec((D, TT), lambda t: (0, prev(t))),
                   _full_spec((D, D)), _full_spec((1, 1)), _full_spec((1, D))],
        out_shape=[jax.ShapeDtypeStruct((L, D), F32), jax.ShapeDtypeStruct((D, L), BF16),
                   jax.ShapeDtypeStruct((D, D), BF16), jax.ShapeDtypeStruct((1, 1), F32),
                   jax.ShapeDtypeStruct((1, D), F32)],
        scratch_shapes=[pltpu.VMEM((D, D), F32), pltpu.VMEM((TT, D), F32), pltpu.VMEM((TT, D), BF16),
                        pltpu.VMEM((TT, D), BF16), pltpu.VMEM((8, D), F32), pltpu.VMEM((8, D), F32)],
        compiler_params=_params(),
    )(mix_t, mix_t, w_out, x, x, x, meta_full, fng, target, target, target)


def _gate_bwd(dmix_t, o_t, proj_t, cw_b, ga_b, gcn_b, L, after):
    nj = L // TT

    def body(dmix_ref, o_ref, za_ref, gb_ref, gc_ref, xc_ref, zc_ref, gcp_ref, xcp_ref, cw_ref, ga_ref, gcn_ref, _,
             do_ref, dd_ref, dg5_ref, dga_ref, dgc_ref, dcw_ref, carry_ref):
        i = pl.program_id(0)
        j = nj - 1 - i

        @pl.when(i == 0)
        def _():
            carry_ref[...] = jnp.zeros_like(carry_ref)
            dga_ref[...] = jnp.zeros_like(dga_ref)
            dgc_ref[...] = jnp.zeros_like(dgc_ref)
            dcw_ref[...] = jnp.zeros_like(dcw_ref)

        def group(h, c):
            r0 = pl.multiple_of(h * DH, DH)
            rows = pl.ds(r0, DH)
            sec = lambda s: pl.ds(s * DA + r0, DH)
            g = _gate_group(rows, o_ref, za_ref, gb_ref, gc_ref, xc_ref, zc_ref, gcp_ref, xcp_ref,
                            cw_ref, ga_ref, gcn_ref, j == 0)
            o, za, gb, gc, xc, zc, sa, sc = (g[n] for n in ("o", "za", "gb", "gc", "xc", "zc", "sa", "sc"))
            dya = dmix_ref[rows, :].astype(F32)
            dyc = dmix_ref[pl.ds(DA + r0, DH), :].astype(F32)

            dn = dya * (za * sa)
            dg5_ref[sec(0), :] = (dya * (g["oa"] * g["g_a"]) * (sa * (1.0 + za * (1.0 - sa)))).astype(BF16)
            dga_ref[rows, :] += _lane_tiles_sum(dn * g["oa"])
            dng = dn * g["g_a"]
            mean_a = jnp.mean(dng * g["oa"], axis=0, keepdims=True)
            do = (dng - g["oa"] * mean_a) * g["ra"]
            do_ref[rows, :] = do.astype(BF16)
            dd_ref[h] = jnp.sum(do * o, axis=0, keepdims=True)

            dnc = dyc * (zc * sc)
            dg5_ref[sec(4), :] = (dyc * (g["ec"] * g["g_c"]) * (sc * (1.0 + zc * (1.0 - sc)))).astype(BF16)
            dgc_ref[rows, :] += _lane_tiles_sum(dnc * g["ec"])
            dncg = dnc * g["g_c"]
            mean_c = jnp.mean(dncg * g["ec"], axis=0, keepdims=True)
            de = (dncg - g["ec"] * mean_c) * g["rc"]
            dg5_ref[sec(1), :] = (de * g["cv"]).astype(BF16)
            dcv = de * gb
            full = jnp.concatenate([dcv, carry_ref[rows, :]], axis=1)
            d1 = pltpu.roll(full, TT + TB - 1, 1)[:, :TT]
            d2 = pltpu.roll(full, TT + TB - 2, 1)[:, :TT]
            carry_ref[rows, :] = dcv[:, :TB]
            da = g["w2"] * dcv + g["w1"] * d1 + g["w0"] * d2
            dg5_ref[sec(2), :] = (da * xc).astype(BF16)
            dg5_ref[sec(3), :] = (da * gc).astype(BF16)
            dcw_ref[0, rows, :] += _lane_tiles_sum(dcv * g["a2"])
            dcw_ref[1, rows, :] += _lane_tiles_sum(dcv * g["a1"])
            dcw_ref[2, rows, :] += _lane_tiles_sum(dcv * g["a"])
            return c

        lax.fori_loop(0, H, group, 0, unroll=2)

    rj = lambda i: nj - 1 - i
    return pl.pallas_call(
        body, name="gate_bwd", grid=(nj,),
        in_specs=[pl.BlockSpec((2 * DA, TT), lambda i: (0, rj(i)))] + _gate_specs(nj, True) + [_UNREAD],
        out_specs=[
            pl.BlockSpec((DA, TT), lambda i: (0, rj(i))),
            pl.BlockSpec((H, 1, TT), lambda i: (0, 0, rj(i))),
            pl.BlockSpec((5 * DA, TT), lambda i: (0, rj(i))),
            _full_spec((DA, TB)), _full_spec((DA, TB)), _full_spec((3, DA, TB)),
        ],
        out_shape=[
            jax.ShapeDtypeStruct((DA, L), BF16),
            jax.ShapeDtypeStruct((H, 1, L), F32),
            jax.ShapeDtypeStruct((5 * DA, L), BF16),
            jax.ShapeDtypeStruct((DA, TB), F32),
            jax.ShapeDtypeStruct((DA, TB), F32),
            jax.ShapeDtypeStruct((3, DA, TB), F32),
        ],
        scratch_shapes=[pltpu.VMEM((DA, TB), F32)],
        compiler_params=_params(),
    )(dmix_t, o_t, proj_t, proj_t, proj_t, proj_t, proj_t, proj_t, proj_t, cw_b, ga_b, gcn_b, after)


def _attn_bwd(proj_t, kaug, vtok, do_t, lse, dd, cq, L):
    nk = L // TT

    def body(q_ref, kaug_ref, vtok_ref, kt_ref, do_ref, lse_ref, dd_ref, cq_ref,
             dq_ref, dk_ref, dv_ref, dck_ref, dcq_ref, dq_acc, kt1_scr, s_scr, dp_scr, dv_scr, dk_scr):
        i = pl.program_id(0)
        rows = [slice(g * DH, (g + 1) * DH) for g in range(HG)]
        ones = jnp.ones((DF, TT), BF16)
        zpad = jnp.zeros((KA - DH - DF, TT), BF16)
        for g in range(HG):
            kt1_scr[g] = jnp.concatenate([kt_ref[rows[g], :], ones], axis=0)
        dv_scr[...] = jnp.zeros_like(dv_scr)
        dk_scr[...] = jnp.zeros_like(dk_scr)

        def q_rows(g, q_off):
            bias = cq_ref[g, :, pl.ds(q_off, TT)] - lse_ref[g, :, pl.ds(q_off, TT)]
            return jnp.concatenate([q_ref[rows[g], pl.ds(q_off, TT)], _bias_rows(bias)], axis=0)

        def scores(jq, masked):
            q_off = pl.multiple_of(jq * TT, TT)
            for g in range(HG):
                s = _dot(kaug_ref[g], jnp.concatenate([q_rows(g, q_off), zpad], axis=0))
                if masked:
                    s = jnp.where(_causal_mask(), s, NEG)
                s_scr[g] = s
                dp_scr[g] = _dot(vtok_ref[g], do_ref[rows[g], pl.ds(q_off, TT)])

        def grads(jq):
            q_off = pl.multiple_of(jq * TT, TT)
            for g in range(HG):
                p = jnp.exp2(s_scr[g])
                ds = (p * (dp_scr[g] - dd_ref[g, :, pl.ds(q_off, TT)])).astype(BF16)
                do1 = jnp.concatenate([do_ref[rows[g], pl.ds(q_off, TT)], jnp.zeros((KA - DH, TT), BF16)], axis=0)
                q1 = jnp.concatenate([q_rows(g, q_off), zpad], axis=0)
                dv_scr[g] += _dot(p.astype(BF16), do1, NT_DIMS)
                dk_scr[g] += _dot(ds, q1, NT_DIMS)
                dq_acc[g, :, pl.ds(q_off, TT)] += _dot(kt1_scr[g], ds)

        @pl.when(i == 0)
        def _():
            dq_acc[...] = jnp.zeros_like(dq_acc)

        scores(i, True)

        def step(jq, c):
            grads(jq)
            scores(jq + 1, False)
            return c

        lax.fori_loop(i, nk - 1, step, 0)
        grads(nk - 1)
        for g in range(HG):
            dv_ref[rows[g], :] = dv_scr[g].T[:DH, :].astype(BF16)
            dk_t = dk_scr[g].T
            dk_ref[rows[g], :] = (dk_t[:DH, :] * LN2).astype(BF16)
            dck_ref[g] = dk_t[DH:DH + 1, :]

        @pl.when(i == nk - 1)
        def _():
            for g in range(HG):
                dq_ref[rows[g], :] = (dq_acc[g, :DH, :] * (DH ** -0.5)).astype(BF16)
                dcq_ref[g] = dq_acc[g, DH:DH + 1, :]

    assert HG == H
    head = lambda i: (0, 0)
    row = lambda i: (0, 0, 0)
    return pl.pallas_call(
        body, name="attn_bwd", grid=(nk,),
        in_specs=[
            pl.BlockSpec((DA, L), head),
            pl.BlockSpec((H, TT, KA), lambda i: (0, i, 0)),
            pl.BlockSpec((H, TT, DH), lambda i: (0, i, 0)),
            pl.BlockSpec((DA, TT), lambda i: (1, i)),
            pl.BlockSpec((DA, L), head),
            pl.BlockSpec((H, 1, L), row), pl.BlockSpec((H, 1, L), row), pl.BlockSpec((H, 1, L), row),
        ],
        out_specs=[
            pl.BlockSpec((DA, L), head),
            pl.BlockSpec((DA, TT), lambda i: (0, i)),
            pl.BlockSpec((DA, TT), lambda i: (0, i)),
            pl.BlockSpec((H, 1, TT), lambda i: (0, 0, i)),
            pl.BlockSpec((H, 1, L), row),
        ],
        out_shape=[jax.ShapeDtypeStruct((DA, L), BF16), jax.ShapeDtypeStruct((DA, L), BF16),
                   jax.ShapeDtypeStruct((DA, L), BF16), jax.ShapeDtypeStruct((H, 1, L), F32),
                   jax.ShapeDtypeStruct((H, 1, L), F32)],
        scratch_shapes=[
            pltpu.VMEM((HG, DH + DF, L), F32),
            pltpu.VMEM((HG, DH + DF, TT), BF16),
            pltpu.VMEM((HG, TT, TT), F32), pltpu.VMEM((HG, TT, TT), F32),
            pltpu.VMEM((HG, TT, KA), F32), pltpu.VMEM((HG, TT, KA), F32)],
        compiler_params=_params(),
    )(proj_t, kaug, vtok, proj_t, do_t, lse, dd, cq)


def _fgate_bwd(dcq, dck, sg, L):
    def body(dcq_ref, dck_ref, sg_ref, df_ref, db_ref):
        dc = jnp.concatenate([dcq_ref[h] - dck_ref[h] for h in range(H)], axis=0)
        idx = lax.broadcasted_iota(jnp.int32, (H, L), 1)
        r = dc
        s = 1
        while s < L:
            r = r + jnp.where(idx + s < L, pltpu.roll(r, L - s, 1), 0.0)
            s *= 2
        df = r * sg_ref[...]
        db_ref[...] = jnp.broadcast_to(jnp.sum(df, axis=1, keepdims=True), (H, TB))
        df_ref[...] = jnp.concatenate([df, jnp.zeros((DF - H, L), F32)], axis=0).astype(BF16)

    return pl.pallas_call(
        body, name="fgate_bwd",
        out_shape=[jax.ShapeDtypeStruct((DF, L), BF16), jax.ShapeDtypeStruct((H, TB), F32)],
        compiler_params=pltpu.CompilerParams(vmem_limit_bytes=VMEM_LIMIT),
    )(dcq, dck, sg)


def _inproj_bwd_x(w, dq_t, dk_t, dv_t, dg5_t, df_t, dout, x, meta_full, norm_g, L, after):
    nj = L // TT
    seq = x.shape[0]

    def body(w_ref, dq_ref, dk_ref, dv_ref, dg5_ref, df_ref, dout_ref, xa_ref, xb_ref, xc_ref, meta_ref, g_ref, _,
             gx_ref, dmeta_ref, dg_ref, dh_scr, sems):
        j = pl.program_id(0)
        slot = j % 2

        def copy_out(step, slot_):
            first = pltpu.make_async_copy(dh_scr.at[slot_, pl.ds(TB, TT - TB)], gx_ref.at[pl.ds(0, TT - TB)],
                                          sems.at[slot_])
            later = pltpu.make_async_copy(dh_scr.at[slot_], gx_ref.at[pl.ds(step * TT - TB, TT)], sems.at[slot_])
            return first, later

        @pl.when(j == 0)
        def _():
            dg_ref[...] = jnp.zeros_like(dg_ref)

        du = _dot(dq_ref[...], w_ref[0:DA, :], TN_DIMS)
        du += _dot(dk_ref[...], w_ref[DA:2 * DA, :], TN_DIMS)
        du += _dot(dv_ref[...], w_ref[2 * DA:3 * DA, :], TN_DIMS)
        du += _dot(dg5_ref[...], w_ref[3 * DA:NSEC * DA, :], TN_DIMS)
        du += _dot(df_ref[...], w_ref[NSEC * DA:DPROJ, :], TN_DIMS)
        hb = _h_tile(j, xa_ref, xb_ref, xc_ref, meta_ref)
        r = lax.rsqrt(jnp.mean(hb * hb, axis=-1, keepdims=True) + EPS)
        hn = hb * r
        dg_ref[...] += jnp.sum(du * hn, axis=0, keepdims=True)
        gu = du * g_ref[...]
        dh = dout_ref[...] + r * gu - hn * (r * jnp.mean(gu * hn, axis=-1, keepdims=True))

        dh_scr[slot] = dh

        @pl.when(j == 0)
        def _():
            dmeta_ref[...] = dh[P0:TB, :]
            copy_out(0, 0)[0].start()

        @pl.when(j >= 1)
        def _():
            copy_out(j, slot)[1].start()

        @pl.when(j == 1)
        def _():
            copy_out(0, 0)[0].wait()

        @pl.when(j >= 2)
        def _():
            copy_out(j - 1, 1 - slot)[1].wait()

        @pl.when(j == nj - 1)
        def _():
            copy_out(j, slot)[0 if nj == 1 else 1].wait()

    blk = lambda rows: pl.BlockSpec((rows, TT), lambda j: (0, j))
    return pl.pallas_call(
        body, name="inproj_bwd_x", grid=(nj,),
        in_specs=[_full_spec((DPROJ, D)), blk(DA), blk(DA), blk(DA), blk(5 * DA), blk(DF),
                  pl.BlockSpec((TT, D), lambda j: (j, 0))] + _x_specs3()
                 + [_full_spec((NM, D)), _full_spec((1, D)), _UNREAD],
        out_specs=[pl.BlockSpec(memory_space=pl.ANY), _full_spec((NM, D)), _full_spec((1, D))],
        out_shape=[jax.ShapeDtypeStruct((seq, D), F32), jax.ShapeDtypeStruct((NM, D), F32),
                   jax.ShapeDtypeStruct((1, D), F32)],
        scratch_shapes=[pltpu.VMEM((2, TT, D), F32), pltpu.SemaphoreType.DMA((2,))],
        compiler_params=_params(),
    )(w, dq_t, dk_t, dv_t, dg5_t, df_t, dout, x, x, x, meta_full, norm_g, after)


def _inproj_bwd_w(u, dq_t, dk_t, dv_t, dg5_t, df_t, L):
    def body(u_ref, dq_ref, dk_ref, dv_ref, dg5_ref, df_ref, dw_ref, dwf_ref):
        s = pl.program_id(0)
        u_all = u_ref[...]

        @pl.when(s < 5)
        def _():
            dw_ref[...] = _dot(dg5_ref[...], u_all)

        for step, ref in ((5, dq_ref), (6, dk_ref), (7, dv_ref)):
            @pl.when(s == step)
            def _(ref=ref):
                dw_ref[...] = _dot(ref[...], u_all)

        @pl.when(s == NSEC - 1)
        def _():
            dwf_ref[...] = _dot(df_ref[...], u_all)

    once = lambda shape: pl.BlockSpec(shape, lambda s: (0, 0), pipeline_mode=pl.Buffered(1))
    return pl.pallas_call(
        body, name="inproj_bwd_w", grid=(NSEC,),
        in_specs=[
            once((L, D)), once((DA, L)), once((DA, L)), once((DA, L)),
            pl.BlockSpec((DA, L), lambda s: (jnp.minimum(s, 4), 0)),
            once((DF, L)),
        ],
        out_specs=[pl.BlockSpec((DA, D), lambda s: (jnp.where(s < 5, s + 3, s - 5), 0)), _full_spec((DF, D))],
        out_shape=[jax.ShapeDtypeStruct((NSEC * DA, D), F32), jax.ShapeDtypeStruct((DF, D), F32)],
        compiler_params=_params(),
    )(u, dq_t, dk_t, dv_t, dg5_t, df_t)


def _adamw(w, g, m, v):
    m = ADAM_B1 * m + (1.0 - ADAM_B1) * g
    v = ADAM_B2 * v + (1.0 - ADAM_B2) * (g * g)
    m_hat = m / (1.0 - ADAM_B1 ** ADAM_STEP)
    v_hat = v / (1.0 - ADAM_B2 ** ADAM_STEP)
    delta = -ADAM_LR * (m_hat / (jnp.sqrt(v_hat) + ADAM_EPS) + ADAM_WD * w)
    return delta, m, v


def _adamw_big(own_in, land_in, own_out, land_out, w_in_t, m_in_t, v_in_t, w_out, m_out, v_out):
    cb = CB
    e_sh = D // NDEV
    in_shape = jax.ShapeDtypeStruct(w_in_t.shape, F32)
    out_shape = jax.ShapeDtypeStruct(w_out.shape, F32)

    def total(own_ref, land_ref, rows, chips):
        g = _pick_slab(0, own_ref, land_ref, rows, chips=chips).astype(F32)
        for j in range(1, own_ref.shape[0]):
            g = g + _pick_slab(j, own_ref, land_ref, rows, chips=chips).astype(F32)
        return g

    def body(oi_ref, li_ref, oo_ref, lo_ref, wi_ref, mi_ref, vi_ref, wo_ref, mo_ref, vo_ref,
             gi, di, mi, vi, go, do, mo, vo):
        g = total(oi_ref, li_ref, slice(0, WSHP), True)[:WSH]
        d, mn, vn = _adamw(wi_ref[...], g, mi_ref[...], vi_ref[...])
        gi[...], di[...], mi[...], vi[...] = g, d, mn, vn
        g = total(oo_ref, lo_ref, slice(0, e_sh), False)
        d, mn, vn = _adamw(wo_ref[0], g, mo_ref[0], vo_ref[0])
        go[0], do[0], mo[0], vo[0] = g, d, mn, vn

    slab = lambda n, rows: pl.BlockSpec((n, rows, cb), lambda i: (0, 0, i))
    ispec = pl.BlockSpec((WSH, cb), lambda i: (0, i))
    ospec = pl.BlockSpec((1, e_sh, cb), lambda i: (0, 0, i))
    return pl.pallas_call(
        body, name="adamw_big", grid=(D // cb,),
        in_specs=[slab(4, WSHP), slab(4, WSHP), slab(NDEV, e_sh), slab(NDEV, e_sh),
                  ispec, ispec, ispec, ospec, ospec, ospec],
        out_specs=[ispec] * 4 + [ospec] * 4, out_shape=[in_shape] * 4 + [out_shape] * 4,
        compiler_params=_params(),
    )(own_in, land_in, own_out, land_out, w_in_t, m_in_t, v_in_t, w_out, m_out, v_out)


F0 = 3 * DA


def _unshard_w_out(own, land):
    e_sh = D // NDEV

    def body(own_ref, land_ref, wo_ref):
        for j in range(NDEV):
            wo_ref[j * e_sh:(j + 1) * e_sh, :] = _pick_slab(j, own_ref, land_ref, slice(0, e_sh), per_peer=False)

    return pl.pallas_call(
        body, name="unshard_w_out", grid=(D // CB,),
        in_specs=[pl.BlockSpec((e_sh, CB), lambda i: (0, i)), pl.BlockSpec((NDEV, e_sh, CB), lambda i: (0, 0, i))],
        out_specs=pl.BlockSpec((D, CB), lambda i: (0, i)),
        out_shape=jax.ShapeDtypeStruct((D, D), BF16),
        compiler_params=_params(),
    )(own, land)


def _unshard_w_in(w_all, small_all, attn_gain, conv_gain):
    def body(w_ref, small_ref, ga_ref, gc_ref, wt_ref, meta_ref, cwb_ref, gab_ref, gcb_ref):
        i = pl.program_id(0)
        for k in range(CB // TB):
            meta_ref[:, k * TB:(k + 1) * TB] = small_ref[(CB // TB) * i + k, 0:NM, :]

        @pl.when(i == 0)
        def _():
            per_row = lambda line: jnp.broadcast_to(line, (TB, DA)).T
            cw = jnp.concatenate([small_ref[j, NM:NM + 3, 0:DH] for j in range(NDEV)], axis=1)
            for k in range(3):
                cwb_ref[k] = per_row(cw[k:k + 1, :])
            gab_ref[...] = per_row(ga_ref[...])
            gcb_ref[...] = per_row(gc_ref[...])

        def ref_rows(lo, hi):
            pieces, r = [], lo
            while r < hi:
                sh, off = divmod(r, WSH)
                n = min(hi - r, WSH - off)
                pieces.append(w_ref[sh, off:off + n, :])
                r += n
            return pieces

        for s in range(NSEC):
            lo = s * DA if s < 3 else s * DA + H
            wt_ref[s * DA:(s + 1) * DA, :] = jnp.concatenate(ref_rows(lo, lo + DA), axis=0)
        wt_ref[NSEC * DA:DPROJ, :] = jnp.concatenate(
            ref_rows(F0, F0 + H) + [jnp.zeros((DF - H, CB), BF16)], axis=0)

    return pl.pallas_call(
        body, name="unshard_w_in", grid=(D // CB,),
        in_specs=[pl.BlockSpec((NDEV, WSHP, CB), lambda i: (0, 0, i)), _full_spec(small_all.shape),
                  _full_spec((1, DA)), _full_spec((1, DA))],
        out_specs=[pl.BlockSpec((DPROJ, CB), lambda i: (0, i)), pl.BlockSpec((NM, CB), lambda i: (0, i)),
                   _full_spec((3, DA, TB)), _full_spec((DA, TB)), _full_spec((DA, TB))],
        out_shape=[jax.ShapeDtypeStruct((DPROJ, D), BF16), jax.ShapeDtypeStruct((NM, D), F32),
                   jax.ShapeDtypeStruct((3, DA, TB), F32), jax.ShapeDtypeStruct((DA, TB), F32),
                   jax.ShapeDtypeStruct((DA, TB), F32)],
        compiler_params=_params(),
    )(w_all, small_all, attn_gain, conv_gain)


def _shard_w_in_grads(dw_main, dw_f):
    def body(dm_ref, df_ref, p_ref):
        def ref_rows(lo, hi):
            pieces, r = [], lo
            while r < hi:
                if r < F0:
                    n = min(hi, F0) - r
                    pieces.append(dm_ref[r:r + n, :])
                elif r < F0 + H:
                    n = min(hi, F0 + H) - r
                    pieces.append(df_ref[r - F0:r - F0 + n, :])
                else:
                    n = hi - r
                    pieces.append(dm_ref[r - H:r - H + n, :])
                r += n
            return pieces

        for i in range(NDEV):
            rows = jnp.concatenate(ref_rows(i * WSH, (i + 1) * WSH) + [jnp.zeros((WSHP - WSH, CB), F32)], axis=0)
            p_ref[i] = rows.astype(BF16)

    col = lambda rows: pl.BlockSpec((rows, CB), lambda i: (0, i))
    return pl.pallas_call(
        body, name="shard_w_in_grads", grid=(D // CB,),
        in_specs=[col(NSEC * DA), col(DF)],
        out_specs=pl.BlockSpec((NDEV, WSHP, CB), lambda i: (0, 0, i)),
        out_shape=jax.ShapeDtypeStruct((NDEV, WSHP, D), BF16),
        compiler_params=_params(),
    )(dw_main, dw_f)


SMALL = ("norm_g", "final_norm_g", "attn_norm_g", "conv_norm_g", "b_f", "meta", "conv_w")


def _as_rows(x):
    return jnp.concatenate([x[:, r * TB:(r + 1) * TB] for r in range(x.shape[1] // TB)], axis=0)


def _as_line(rows):
    return jnp.concatenate([rows[r:r + 1, :] for r in range(rows.shape[0])], axis=1)


def _pad_rows(x, n=8):
    return jnp.concatenate([x, jnp.zeros((n - x.shape[0], x.shape[1]), F32)], axis=0)


def _tile_rows(a, rows, lanes=TB):
    a = a.reshape(rows, lanes)
    return jnp.pad(a, ((0, -rows % 8), (0, TB - lanes)))


def _pack_small_grads(dg_norm, dg_final, dga_p, dgc_p, dcw_p, db_b, dmeta, loss):
    def body(dgn_ref, dgf_ref, dga_ref, dgc_ref, dcw_ref, db_ref, dmeta_ref, loss_ref, out_ref):
        def lane_sums(p):
            return jnp.sum(p.T, axis=0, keepdims=True)

        lane = lax.broadcasted_iota(jnp.int32, (1, TB), 1)
        b_row = jnp.where(lane == H, loss_ref[...], 0.0)
        for h in range(H):
            b_row = b_row + jnp.where(lane == h, db_ref[h:h + 1, :], 0.0)
        common = jnp.concatenate([
            _as_rows(dgn_ref[...]), _as_rows(dgf_ref[...]), _pad_rows(_as_rows(lane_sums(dga_ref[...]))),
            _pad_rows(_as_rows(lane_sums(dgc_ref[...]))), _pad_rows(b_row)], axis=0)
        dcw = [lane_sums(dcw_ref[k]) for k in range(3)]
        for j in range(NDEV):
            cw = jnp.concatenate(
                [jnp.concatenate([r[:, j * DH:(j + 1) * DH], jnp.zeros((1, TB - DH), F32)], axis=1) for r in dcw],
                axis=0)
            out_ref[j] = jnp.concatenate([common, dmeta_ref[:, j * TB:(j + 1) * TB], _pad_rows(cw)], axis=0)

    return pl.pallas_call(
        body, name="pack_small_grads", out_shape=jax.ShapeDtypeStruct((NDEV, SROWS, TB), F32),
    )(dg_norm, dg_final, dga_p, dgc_p, dcw_p, db_b, dmeta, loss)


def _adamw_small(own, land, params):
    flat = [a for n in SMALL for a in params[n]]

    def body(*refs):
        own_ref, land_ref = refs[:2]
        ins = refs[2:2 + 3 * len(SMALL)]
        outs = refs[2 + 3 * len(SMALL):]
        g = _pick_slab(0, own_ref, land_ref, slice(0, SROWS))
        for j in range(1, NDEV):
            g = g + _pick_slab(j, own_ref, land_ref, slice(0, SROWS))
        grads = dict(
            norm_g=_as_line(g[0:8]), final_norm_g=_as_line(g[8:16]), attn_norm_g=_as_line(g[16:20]),
            conv_norm_g=_as_line(g[24:28]), b_f=g[32:33, :H], meta=g[40:56], conv_w=g[56:59, :DH][None])
        for i, n in enumerate(SMALL):
            w_ref, m_ref, v_ref = ins[3 * i:3 * i + 3]
            d, mn, vn = _adamw(w_ref[...], grads[n], m_ref[...], v_ref[...])
            for o_ref, val in zip(outs[4 * i:4 * i + 4], (grads[n], d, mn, vn)):
                o_ref[...] = val
        outs[-1][...] = g[32:33, H:H + 1]

    shapes = [jax.ShapeDtypeStruct(params[n][0].shape, F32) for n in SMALL for _ in range(4)]
    res = pl.pallas_call(
        body, name="adamw_small", out_shape=shapes + [jax.ShapeDtypeStruct((1, 1), F32)],
    )(own, land, *flat)
    return {n: res[4 * i:4 * i + 4] for i, n in enumerate(SMALL)}, res[-1]


def kernel(x, meta, norm_g, w_in, b_f, conv_w, attn_norm_g, conv_norm_g, w_out, final_norm_g, loss_target, m_meta, m_norm_g, m_w_in, m_b_f, m_conv_w, m_attn_norm_g, m_conv_norm_g, m_w_out, m_final_norm_g, v_meta, v_norm_g, v_w_in, v_b_f, v_conv_w, v_attn_norm_g, v_conv_norm_g, v_w_out, v_final_norm_g):
    seq = x.shape[1]
    L = seq + TB
    assert x.shape == (1, seq, D) and L % TT == 0 and w_in.shape == (1, D, WSH)
    x2 = x[0]
    tgt = loss_target[0]

    w_in_slab = jnp.pad(w_in[0].T, ((0, WSHP - WSH), (0, 0))).astype(BF16)
    w_out_slab = w_out[0].astype(BF16)
    meta_slab = jnp.concatenate([meta, _tile_rows(conv_w[0], 3, DH)], axis=0)
    wout_flight = _split_start(w_out_slab, "gather_w_out_start", per_peer=False)
    w_all, small_all = _all_gather([w_in_slab, meta_slab], "gather_w_in")

    w_t, meta_full, cw_b, ga_b, gcn_b = _unshard_w_in(w_all, small_all, attn_norm_g, conv_norm_g)

    u, proj_t, f_t, ktok, vtok = _inproj_fwd(x2, meta_full, norm_g, w_t, L, after=wout_flight[4])
    cq, kaug, sg = _fgate_fwd(f_t, b_f.reshape(H, 1), ktok, L)
    o_t, lse = _attn_fwd(proj_t, kaug, cq, L)
    mix_t = _gate_fwd(o_t, proj_t, cw_b, ga_b, gcn_b, L)

    w_out_own, w_out_land = _split_wait(wout_flight, mix_t, "gather_w_out_wait", per_peer=False)
    w_out_full = _unshard_w_out(w_out_own, w_out_land)
    dout, dmix_t, dw_out, loss_part, dg_final = _outproj(
        mix_t, w_out_full, x2, meta_full, final_norm_g.reshape(1, D), tgt, L)
    dwo_flight = _split_start(dw_out.reshape(NDEV, D // NDEV, D), "exchange_dw_out_start", per_peer=True)
    do_t, dd, dg5_t, dga_p, dgc_p, dcw_p = _gate_bwd(dmix_t, o_t, proj_t, cw_b, ga_b, gcn_b, L, after=dwo_flight[4])
    dq_t, dk_t, dv_t, dck, dcq = _attn_bwd(proj_t, kaug, vtok, do_t, lse, dd, cq, L)
    df_t, db_f = _fgate_bwd(dcq, dck, sg, L)
    dw_main, dw_f = _inproj_bwd_w(u, dq_t, dk_t, dv_t, dg5_t, df_t, L)
    dwi_parts = _shard_w_in_grads(dw_main, dw_f)
    dwi_chip = _pair_sum(dwi_parts, _pair_exchange(dwi_parts, "exchange_dw_in_pair"))
    dwi_flight = _split_start(dwi_chip, "exchange_dw_in_start", per_peer=True, chips=True)
    grad_x, dmeta, dg_norm = _inproj_bwd_x(
        w_t, dq_t, dk_t, dv_t, dg5_t, df_t, dout, x2, meta_full, norm_g, L, after=dwi_flight[4])
    small_parts = _pack_small_grads(dg_norm, dg_final, dga_p, dgc_p, dcw_p, db_f, dmeta, loss_part)
    small_flight = _split_start(small_parts, "exchange_small_start", per_peer=True)
    dwo_own, dwo_land = _split_wait(dwo_flight, small_flight[4], "exchange_dw_out_wait", per_peer=True)
    dwi_own, dwi_land = _split_wait(dwi_flight, dwo_land, "exchange_dw_in_wait", per_peer=True, chips=True)

    big_out = _adamw_big(dwi_own, dwi_land, dwo_own, dwo_land,
                         w_in[0].T, m_w_in[0].T, v_w_in[0].T, w_out, m_w_out, v_w_out)
    g_w_in, d_w_in, nm_w_in, nv_w_in = [a.T[None] for a in big_out[:4]]
    g_w_out, d_w_out, nm_w_out, nv_w_out = big_out[4:]
    sm_own, sm_land = _split_wait(small_flight, big_out[4], "exchange_small_wait", per_peer=True)
    line = lambda a: a.reshape(1, D)
    small, loss = _adamw_small(sm_own, sm_land, dict(
        norm_g=(norm_g, m_norm_g, v_norm_g),
        final_norm_g=(line(final_norm_g), line(m_final_norm_g), line(v_final_norm_g)),
        attn_norm_g=(attn_norm_g, m_attn_norm_g, v_attn_norm_g),
        conv_norm_g=(conv_norm_g, m_conv_norm_g, v_conv_norm_g),
        b_f=(b_f, m_b_f, v_b_f), meta=(meta, m_meta, v_meta), conv_w=(conv_w, m_conv_w, v_conv_w)))
    small["final_norm_g"] = [a.reshape(D) for a in small["final_norm_g"]]
    order = ("meta", "norm_g", "w_in", "b_f", "conv_w", "attn_norm_g", "conv_norm_g", "w_out", "final_norm_g")
    groups = []
    for k, (wi, wo) in enumerate(((g_w_in, g_w_out), (d_w_in, d_w_out), (nm_w_in, nm_w_out), (nv_w_in, nv_w_out))):
        d = dict({n: small[n][k] for n in SMALL}, w_in=wi, w_out=wo)
        groups.append([d[n] for n in order])
    return (loss[0, 0], grad_x[None], *groups[0], *groups[1], *groups[2], *groups[3])
```

```python
import jax
import jax.numpy as jnp
from jax import lax
from jax.experimental import pallas as pl
from jax.experimental.pallas import tpu as pltpu

F32 = jnp.float32
BF16 = jnp.bfloat16

D = 1024
DA = 512
H = 8
DH = 64
NM = 16
TB = 128
P0 = TB - NM
TT = 3 * TB
HG = 8
NDEV = 8
NSEC = 8
DF = 16
DPROJ = NSEC * DA + DF
WSH = 513
WSHP = 528
WROWS = WSHP + D // NDEV
SROWS = 64
EPS = 1e-6
NEG = -1e30
LOG2E = 1.4426950408889634
LN2 = 0.6931471805599453
QSCALE = DH ** -0.5 * LOG2E
KA = 128
CB = 256
VMEM_LIMIT = 56 * 1024 * 1024

ADAM_LR = 0.001
ADAM_B1 = 0.9
ADAM_B2 = 0.999
ADAM_EPS = 1e-08
ADAM_WD = 0.01
ADAM_STEP = 10

NT_DIMS = (((1,), (1,)), ((), ()))
TN_DIMS = (((0,), (0,)), ((), ()))
MESH = pl.DeviceIdType.MESH


def _params(n_axes=1, vmem=VMEM_LIMIT):
    return pltpu.CompilerParams(dimension_semantics=("arbitrary",) * n_axes, vmem_limit_bytes=vmem)


def _dot(a, b, dims=None):
    if dims is None:
        return jnp.dot(a, b, preferred_element_type=F32)
    return lax.dot_general(a, b, dims, preferred_element_type=F32)


def _my_place():
    return lax.axis_index("x"), lax.axis_index("y"), lax.axis_index("c")


def _all_gather(xs, name):
    n = len(xs)

    def body(*refs):
        x_refs, out_refs = refs[:n], refs[n:2 * n]
        send_sems, recv_sems, local_sems = refs[2 * n:]
        mx, my, mc = _my_place()

        def across(px, py, pc, axis_a):
            flip_x = pc if axis_a else 1 - pc
            return (px + flip_x) % 2, (py + 1 - flip_x) % 2, pc

        def idx(p):
            return 4 * p[0] + 2 * p[1] + p[2]

        me, sib = (mx, my, mc), (mx, my, 1 - mc)
        a_nbr, b_nbr = across(*me, True), across(*me, False)
        diag = across(*b_nbr, True)
        sib_a, sib_b = across(*sib, True), across(*sib, False)
        sib_diag = across(*sib_b, True)

        waits = []
        for t in range(n):
            out_ref = out_refs[t]

            def copy(k, block, to, src=None, out_ref=out_ref, t=t):
                return pltpu.make_async_remote_copy(
                    src_ref=out_ref.at[idx(block)] if src is None else src, dst_ref=out_ref.at[idx(block)],
                    send_sem=send_sems.at[7 * t + k], recv_sem=recv_sems.at[7 * t + k],
                    device_id=to, device_id_type=MESH)

            mine = pltpu.make_async_copy(x_refs[t], out_ref.at[idx(me)], local_sems.at[t])
            mine.start()
            started = [copy(0, me, sib, src=x_refs[t]), copy(1, me, a_nbr, src=x_refs[t]),
                       copy(2, me, b_nbr, src=x_refs[t])]
            for cp in started:
                cp.start()
            waits.append((copy, mine, started))
        relays = ((1, a_nbr, ((3, b_nbr), (4, sib))), (2, b_nbr, ((5, sib),)), (3, diag, ((6, sib),)))
        for landed, block, onward in relays:
            for copy, _, started in waits:
                copy(landed, block, me).wait_recv()
                for k, to in onward:
                    started.append(copy(k, block, to))
                    started[-1].start()
        for copy, mine, started in waits:
            for k, block in ((0, sib), (4, sib_a), (5, sib_b), (6, sib_diag)):
                copy(k, block, me).wait_recv()
            for cp in started:
                cp.wait_send()
            mine.wait()

    any_spec = pl.BlockSpec(memory_space=pl.ANY)
    return pl.pallas_call(
        body, name=name,
        out_shape=[jax.ShapeDtypeStruct((NDEV,) + x.shape, x.dtype) for x in xs],
        in_specs=[any_spec] * n, out_specs=[any_spec] * n,
        scratch_shapes=[pltpu.SemaphoreType.DMA((7 * n,)), pltpu.SemaphoreType.DMA((7 * n,)),
                        pltpu.SemaphoreType.DMA((n,))],
    )(*xs)


_HBM = pl.BlockSpec(memory_space=pltpu.HBM)
_UNREAD = pl.BlockSpec(memory_space=pl.ANY)
_SEM = pl.BlockSpec(memory_space=pltpu.SEMAPHORE)
_EFFECT = pltpu.SideEffectType.DATAFLOW_SIDE_EFFECTING


def _peer_of(m, place):
    mx, my, mc = place
    return ((1 - mx) if m & 4 else mx, (1 - my) if m & 2 else my, (1 - mc) if m & 1 else mc)


def _party(chips):
    if chips:
        return (lambda p: 2 * p[0] + p[1]), (2, 4, 6)
    return (lambda p: 4 * p[0] + 2 * p[1] + p[2]), tuple(range(1, NDEV))


def _split_copies(src_ref, land_ref, send_sems, recv_sems, per_peer, incoming, chips):
    place = _my_place()
    slot, masks = _party(chips)
    me = slot(place)
    out = []
    for k, m in enumerate(masks):
        there = _peer_of(m, place)
        peer = slot(there)
        src = (src_ref.at[me] if incoming else src_ref.at[peer]) if per_peer else src_ref
        out.append(pltpu.make_async_remote_copy(
            src_ref=src, dst_ref=land_ref.at[peer if incoming else me],
            send_sem=send_sems.at[k], recv_sem=recv_sems.at[k], device_id=there, device_id_type=MESH))
    return out


def _split_start(src, name, per_peer, chips=False):
    slab = src.shape[1:] if per_peer else src.shape
    n = len(_party(chips)[1])

    def body(src_ref, land_ref, send_sems, recv_sems, src_thru, land_thru, token):
        for cp in _split_copies(src_ref, land_ref, send_sems, recv_sems, per_peer, False, chips):
            cp.start()
        token[...] = jnp.zeros_like(token)

    return pl.pallas_call(
        body, name=name,
        out_shape=(pltpu.SemaphoreType.DMA((n,)), pltpu.SemaphoreType.DMA((n,)),
                   pltpu.HBM(src.shape, src.dtype), pltpu.HBM((n + 1,) + slab, src.dtype),
                   jax.ShapeDtypeStruct((8, TB), F32)),
        in_specs=(_HBM, _HBM), out_specs=(_SEM, _SEM, _HBM, _HBM, pl.BlockSpec(memory_space=pltpu.VMEM)),
        input_output_aliases={0: 2, 1: 3},
        compiler_params=pltpu.CompilerParams(has_side_effects=_EFFECT),
    )(pltpu.with_memory_space_constraint(src, pltpu.HBM),
      pltpu.with_memory_space_constraint(lax.empty((n + 1,) + slab, src.dtype), pltpu.HBM))


def _split_wait(handles, after, name, per_peer, chips=False):
    send_sems, recv_sems, src_thru, land_thru, _ = handles

    def body(src_ref, land_ref, send_sems, recv_sems, after_ref, src_out, land_out):
        for cp in _split_copies(src_ref, land_ref, send_sems, recv_sems, per_peer, False, chips):
            cp.wait_send()
        for cp in _split_copies(src_ref, land_ref, send_sems, recv_sems, per_peer, True, chips):
            cp.wait_recv()

    return pl.pallas_call(
        body, name=name,
        out_shape=(pltpu.HBM(src_thru.shape, src_thru.dtype), pltpu.HBM(land_thru.shape, land_thru.dtype)),
        in_specs=(_HBM, _HBM, _SEM, _SEM, pl.BlockSpec(memory_space=pl.ANY)), out_specs=(_HBM, _HBM),
        input_output_aliases={0: 0, 1: 1},
        compiler_params=pltpu.CompilerParams(has_side_effects=_EFFECT),
    )(src_thru, land_thru, send_sems, recv_sems, after)


def _pick_slab(j, own_ref, land_ref, rows, per_peer=True, chips=False):
    me = _party(chips)[0](_my_place())
    own = (lambda: own_ref[j, rows, :]) if per_peer else (lambda: own_ref[rows, :])
    return lax.cond(me == j, own, lambda: land_ref[j, rows, :])


def _pair_exchange(p, name):
    def body(p_ref, got_ref, send_sems, recv_sems):
        mx, my, mc = _my_place()
        copies = [pltpu.make_async_remote_copy(
            src_ref=p_ref.at[2 * q + 1 - mc], dst_ref=got_ref.at[q], send_sem=send_sems.at[q],
            recv_sem=recv_sems.at[q], device_id=(mx, my, 1 - mc), device_id_type=MESH) for q in range(4)]
        for cp in copies:
            cp.start()
        for cp in copies:
            cp.wait_recv()
        for cp in copies:
            cp.wait_send()

    any_spec = pl.BlockSpec(memory_space=pl.ANY)
    return pl.pallas_call(
        body, name=name, out_shape=jax.ShapeDtypeStruct((4,) + p.shape[1:], p.dtype),
        in_specs=[any_spec], out_specs=any_spec,
        scratch_shapes=[pltpu.SemaphoreType.DMA((4,)), pltpu.SemaphoreType.DMA((4,))],
    )(p)


def _pair_sum(p, got):
    rows = p.shape[1]

    def body(p_ref, got_ref, out_ref):
        mc = lax.axis_index("c")
        for q in range(4):
            out_ref[q] = (p_ref[2 * q + mc].astype(F32) + got_ref[q].astype(F32)).astype(BF16)

    blk = lambda n: pl.BlockSpec((n, rows, CB), lambda i: (0, 0, i))
    return pl.pallas_call(
        body, name="pair_sum", grid=(D // CB,), in_specs=[blk(NDEV), blk(4)], out_specs=blk(4),
        out_shape=jax.ShapeDtypeStruct((4, rows, D), BF16), compiler_params=_params(),
    )(p, got)


def _h_block(t, x_ref, meta_ref):
    first = jnp.concatenate([jnp.zeros((P0, D), F32), meta_ref[...]], axis=0)
    return jnp.where(t == 0, first, x_ref[...])


def _x_specs3(tile=lambda j: j):
    return [pl.BlockSpec((TB, D), lambda j: (jnp.maximum(3 * tile(j) - 1, 0), 0)),
            pl.BlockSpec((TB, D), lambda j: (3 * tile(j), 0)),
            pl.BlockSpec((TB, D), lambda j: (3 * tile(j) + 1, 0))]


def _h_tile(j, xa_ref, xb_ref, xc_ref, meta_ref):
    first = jnp.concatenate([jnp.zeros((P0, D), F32), meta_ref[...]], axis=0)
    return jnp.concatenate([jnp.where(j == 0, first, xa_ref[...]), xb_ref[...], xc_ref[...]], axis=0)


def _full_spec(shape):
    return pl.BlockSpec(shape, lambda *_: (0,) * len(shape))


def _sigmoid(z):
    return 1.0 / (1.0 + jnp.exp(-z))


def _lane_tiles_sum(x):
    out = x[:, :TB]
    for i in range(1, x.shape[1] // TB):
        out = out + x[:, i * TB:(i + 1) * TB]
    return out


def _inproj_fwd(x, meta_full, norm_g, w_t, L, after):
    nj = L // TT

    def body(xa_ref, xb_ref, xc_ref, meta_ref, g_ref, w_ref, _, u_ref, proj_ref, f_ref, ktok_ref, vtok_ref):
        hb = _h_tile(pl.program_id(0), xa_ref, xb_ref, xc_ref, meta_ref)
        r = lax.rsqrt(jnp.mean(hb * hb, axis=-1, keepdims=True) + EPS)
        u = (hb * r * g_ref[...]).astype(BF16)
        u_ref[...] = u
        for s in range(NSEC):
            p = _dot(u, w_ref[s * DA:(s + 1) * DA, :], NT_DIMS)
            if s == 0:
                p = p * QSCALE
            if s in (1, 2):
                tok_ref = ktok_ref if s == 1 else vtok_ref
                for h in range(H):
                    tok_ref[h] = p[:, h * DH:(h + 1) * DH].astype(BF16)
            proj_ref[s * DA:(s + 1) * DA, :] = p.T.astype(BF16)
        f_ref[...] = _dot(w_ref[NSEC * DA:DPROJ, :], u, NT_DIMS)[:H]

    return pl.pallas_call(
        body, name="inproj_fwd", grid=(nj,),
        in_specs=_x_specs3() + [_full_spec((NM, D)), _full_spec((1, D)), _full_spec((DPROJ, D)), _UNREAD],
        out_specs=[
            pl.BlockSpec((TT, D), lambda t: (t, 0)),
            pl.BlockSpec((NSEC * DA, TT), lambda t: (0, t)),
            pl.BlockSpec((H, TT), lambda t: (0, t)),
            pl.BlockSpec((H, TT, DH), lambda t: (0, t, 0)),
            pl.BlockSpec((H, TT, DH), lambda t: (0, t, 0)),
        ],
        out_shape=[
            jax.ShapeDtypeStruct((L, D), BF16),
            jax.ShapeDtypeStruct((NSEC * DA, L), BF16),
            jax.ShapeDtypeStruct((H, L), F32),
            jax.ShapeDtypeStruct((H, L, DH), BF16),
            jax.ShapeDtypeStruct((H, L, DH), BF16),
        ],
        compiler_params=_params(),
    )(x, x, x, meta_full, norm_g, w_t, after)


def _split3(x):
    hi = x.astype(BF16).astype(F32)
    r = x - hi
    mid = r.astype(BF16).astype(F32)
    return hi, mid, (r - mid).astype(BF16).astype(F32)


def _bias_rows(bias):
    one = jnp.ones((1, TT), F32)
    zero = jnp.zeros((1, TT), F32)
    parts = [zero] * 3 if bias is None else list(_split3(bias))
    return jnp.concatenate([one] * 3 + parts + [zero] * (DF - 6), axis=0).astype(BF16)


def _fgate_fwd(f_t, b_col, ktok, L):
    nb = L // TB

    def body(f_ref, b_ref, ktok_ref, cq_ref, kaug_ref, sg_ref):
        z = f_ref[...] + b_ref[...]
        idx = lax.broadcasted_iota(jnp.int32, (H, L), 1)
        real = idx >= P0
        lf = jnp.where(real, jnp.minimum(z, 0.0) - jnp.log1p(jnp.exp(-jnp.abs(z))), 0.0)
        sg_ref[...] = jnp.where(real, 1.0 / (1.0 + jnp.exp(z)), 0.0)
        c = lf
        s = 1
        while s < L:
            c = c + jnp.where(idx >= s, pltpu.roll(c, s, 1), 0.0)
            s *= 2
        c = c * LOG2E
        for h in range(H):
            cq_ref[h] = c[h:h + 1, :]
        hi, mid, lo = _split3(-jnp.where(real, c, -NEG))
        lane = lax.broadcasted_iota(jnp.int32, (TB, KA), 1)
        ones = jnp.ones((3, TB), F32)
        for h in range(H):
            for b in range(nb):
                blk = slice(b * TB, (b + 1) * TB)
                cols = jnp.concatenate([
                    jnp.zeros((DH, TB), F32), hi[h:h + 1, blk], mid[h:h + 1, blk], lo[h:h + 1, blk], ones,
                    jnp.zeros((KA - DH - 6, TB), F32)], axis=0).T
                k = jnp.concatenate([ktok_ref[h, blk, :].astype(F32), jnp.zeros((TB, KA - DH), F32)], axis=1)
                kaug_ref[h, blk, :] = jnp.where(lane < DH, k, cols).astype(BF16)

    return pl.pallas_call(
        body, name="fgate_fwd",
        out_shape=[
            jax.ShapeDtypeStruct((H, 1, L), F32),
            jax.ShapeDtypeStruct((H, L, KA), BF16),
            jax.ShapeDtypeStruct((H, L), F32),
        ],
        compiler_params=pltpu.CompilerParams(vmem_limit_bytes=VMEM_LIMIT),
    )(f_t, b_col, ktok)


def _causal_mask():
    r = lax.broadcasted_iota(jnp.int32, (TT, TT), 0)
    c = lax.broadcasted_iota(jnp.int32, (TT, TT), 1)
    return r <= c


def _attn_fwd(proj_t, kaug, cq, L):
    nq = L // TT

    def body(q_ref, qn_ref, kaug_ref, v_ref, cq_ref, o_ref, lse_ref,
             qa_scr, s_scr, cmax_scr, m_scr, p_scr, alpha_scr, acc_scr):
        j = pl.program_id(0)
        rows = [slice(g * DH, (g + 1) * DH) for g in range(HG)]
        ones = jnp.ones((DF, TT), BF16)

        def load_queries(ref):
            for g in range(HG):
                qa_scr[g] = jnp.concatenate(
                    [ref[rows[g], :], _bias_rows(None), jnp.zeros((KA - DH - DF, TT), BF16)], axis=0)

        def scores(kt, masked):
            k_off = pl.multiple_of(kt * TT, TT)
            for g in range(HG):
                s = _dot(kaug_ref[g, pl.ds(k_off, TT), :], qa_scr[g])
                if masked:
                    s = jnp.where(_causal_mask(), s, NEG)
                s_scr[g] = s
                cmax_scr[g] = jnp.max(s, axis=0, keepdims=True)

        def softmax():
            for g in range(HG):
                m_old = m_scr[g]
                m_new = jnp.maximum(m_old, cmax_scr[g])
                alpha_scr[g] = jnp.exp2(m_old - m_new)
                p_scr[g] = jnp.exp2(s_scr[g] - m_new).astype(BF16)
                m_scr[g] = m_new

        def weighted_sum(kt):
            k_off = pl.multiple_of(kt * TT, TT)
            for g in range(HG):
                v1 = jnp.concatenate([v_ref[rows[g], pl.ds(k_off, TT)], ones], axis=0)
                acc_scr[g] = alpha_scr[g] * acc_scr[g] + _dot(v1, p_scr[g])

        @pl.when(j == 0)
        def _():
            load_queries(q_ref)
            scores(0, True)

        m_scr[...] = jnp.full_like(m_scr, NEG)
        acc_scr[...] = jnp.zeros_like(acc_scr)

        @pl.when(j >= 1)
        def _():
            softmax()
            scores(j - 1, False)

        def step(i, c):
            weighted_sum(j - i + 1)
            softmax()
            scores(j - i - 1, False)
            return c

        lax.fori_loop(1, j, step, 0)

        def drain(second_last, next_tile):
            if second_last:
                weighted_sum(1)
            softmax()
            if next_tile:
                load_queries(qn_ref)
                scores(j + 1, True)
            weighted_sum(0)

        @pl.when(j == 0)
        def _():
            drain(False, nq > 1)

        @pl.when((j >= 1) & (j < nq - 1))
        def _():
            drain(True, True)

        @pl.when((j >= 1) & (j == nq - 1))
        def _():
            drain(True, False)

        for g in range(HG):
            l = acc_scr[g, DH:DH + 1, :]
            o_ref[rows[g], :] = acc_scr[g, :DH, :] * (1.0 / l)
            lse_ref[g] = m_scr[g] + jnp.log2(l) + cq_ref[g]

    assert HG == H
    return pl.pallas_call(
        body, name="attn_fwd", grid=(nq,),
        in_specs=[
            pl.BlockSpec((DA, TT), lambda j: (0, j)),
            pl.BlockSpec((DA, TT), lambda j: (0, jnp.minimum(j + 1, nq - 1))),
            pl.BlockSpec((H, L, KA), lambda j: (0, 0, 0)),
            pl.BlockSpec((DA, L), lambda j: (2, 0)),
            pl.BlockSpec((H, 1, TT), lambda j: (0, 0, j)),
        ],
        out_specs=[
            pl.BlockSpec((DA, TT), lambda j: (0, j)),
            pl.BlockSpec((H, 1, TT), lambda j: (0, 0, j)),
        ],
        out_shape=[jax.ShapeDtypeStruct((DA, L), F32), jax.ShapeDtypeStruct((H, 1, L), F32)],
        scratch_shapes=[pltpu.VMEM((HG, KA, TT), BF16), pltpu.VMEM((HG, TT, TT), F32), pltpu.VMEM((HG, 1, TT), F32),
                        pltpu.VMEM((HG, 1, TT), F32), pltpu.VMEM((HG, TT, TT), BF16), pltpu.VMEM((HG, 1, TT), F32),
                        pltpu.VMEM((HG, DH + DF, TT), F32)],
        compiler_params=_params(),
    )(proj_t, proj_t, kaug, proj_t, cq)


def _gate_group(rows, o_ref, za_ref, gb_ref, gc_ref, xc_ref, zc_ref, gcp_ref, xcp_ref, cw_ref, ga_ref, gcn_ref, first):
    n_rep = TT // TB
    f32 = lambda r: r[rows, :].astype(F32)
    o, za, gb, gc, xc, zc = o_ref[rows, :], f32(za_ref), f32(gb_ref), f32(gc_ref), f32(xc_ref), f32(zc_ref)
    a = gc * xc
    a_prev = jnp.where(first, 0.0, f32(gcp_ref) * f32(xcp_ref))
    full = jnp.concatenate([a_prev, a], axis=1)
    a1 = pltpu.roll(full, 1, 1)[:, TB:]
    a2 = pltpu.roll(full, 2, 1)[:, TB:]
    w0 = jnp.tile(cw_ref[0, rows, :], (1, n_rep))
    w1 = jnp.tile(cw_ref[1, rows, :], (1, n_rep))
    w2 = jnp.tile(cw_ref[2, rows, :], (1, n_rep))
    cv = w0 * a2 + w1 * a1 + w2 * a
    e = gb * cv
    rc = lax.rsqrt(jnp.mean(e * e, axis=0, keepdims=True) + EPS)
    ec = e * rc
    ra = lax.rsqrt(jnp.mean(o * o, axis=0, keepdims=True) + EPS)
    oa = o * ra
    g_a = jnp.tile(ga_ref[rows, :], (1, n_rep))
    g_c = jnp.tile(gcn_ref[rows, :], (1, n_rep))
    sa = _sigmoid(za)
    sc = _sigmoid(zc)
    return dict(o=o, za=za, gb=gb, gc=gc, xc=xc, zc=zc, a=a, a1=a1, a2=a2, w0=w0, w1=w1, w2=w2, cv=cv, e=e,
                rc=rc, ec=ec, ra=ra, oa=oa, g_a=g_a, g_c=g_c, sa=sa, sc=sc)


def _gate_specs(nj, rev):
    def jj(i):
        return (nj - 1 - i) if rev else i

    def sec(s):
        return pl.BlockSpec((DA, TT), lambda i: (s, jj(i)))

    def halo(s):
        return pl.BlockSpec((DA, TB), lambda i: (s, jnp.maximum(3 * jj(i) - 1, 0)))

    return [pl.BlockSpec((DA, TT), lambda i: (0, jj(i))), sec(3), sec(4), sec(5), sec(6), sec(7), halo(5), halo(6),
            _full_spec((3, DA, TB)), _full_spec((DA, TB)), _full_spec((DA, TB))]


def _gate_fwd(o_t, proj_t, cw_b, ga_b, gcn_b, L):
    nj = L // TT

    def body(o_ref, za_ref, gb_ref, gc_ref, xc_ref, zc_ref, gcp_ref, xcp_ref, cw_ref, ga_ref, gcn_ref, mix_ref):
        j = pl.program_id(0)

        def group(h, c):
            r0 = pl.multiple_of(h * DH, DH)
            g = _gate_group(pl.ds(r0, DH), o_ref, za_ref, gb_ref, gc_ref, xc_ref, zc_ref, gcp_ref, xcp_ref,
                            cw_ref, ga_ref, gcn_ref, j == 0)
            mix_ref[pl.ds(r0, DH), :] = (g["oa"] * g["g_a"] * (g["za"] * g["sa"])).astype(BF16)
            mix_ref[pl.ds(DA + r0, DH), :] = (g["ec"] * g["g_c"] * (g["zc"] * g["sc"])).astype(BF16)
            return c

        lax.fori_loop(0, H, group, 0, unroll=2)

    return pl.pallas_call(
        body, name="gate_fwd", grid=(nj,),
        in_specs=_gate_specs(nj, False),
        out_specs=pl.BlockSpec((2 * DA, TT), lambda j: (0, j)),
        out_shape=jax.ShapeDtypeStruct((2 * DA, L), BF16),
        compiler_params=_params(),
    )(o_t, proj_t, proj_t, proj_t, proj_t, proj_t, proj_t, proj_t, cw_b, ga_b, gcn_b)


def _outproj(mix_t, w_out, x, meta_full, fng, target, L):
    nj = L // TT
    rp = NM
    n_bwd = 8
    cb = D // 4
    assert P0 % rp == 0 and TB % rp == 0 and (TT // rp) % n_bwd == 0

    def body(mix_ref, mixp_ref, w_ref, xa_ref, xb_ref, xc_ref, meta_ref, g_ref, ta_ref, tb_ref, tc_ref,
             dout_ref, dmix_ref, dwb_ref, loss_ref, dg_ref, dw_ref, o_scr, db_new, db_old, sq_acc, dg_acc):
        t = pl.program_id(0)

        def loss_rows(c):
            blk = c // (TB // rp)
            rows, out_rows = pl.ds((c % (TB // rp)) * rp, rp), pl.ds(c * rp, rp)
            h = (xa_ref, xb_ref, xc_ref)[blk][rows, :]
            if blk == 0:
                first = meta_ref[...] if c == P0 // rp else jnp.zeros((rp, D), F32)
                h = jnp.where(t == 0, first, h)
            o = o_scr[out_rows, :] + h
            r = lax.rsqrt(jnp.mean(o * o, axis=-1, keepdims=True) + EPS)
            orn = o * r
            g = g_ref[...]
            diff = orn * g - (ta_ref, tb_ref, tc_ref)[blk][rows, :]
            if blk == 0:
                diff = diff * jnp.where(t > 0, 1.0, 0.0)
            gy = diff * (g * (1.0 / D))
            dout = r * (gy - orn * jnp.mean(gy * orn, axis=-1, keepdims=True))
            dout_ref[out_rows, :] = dout
            db_new[out_rows, :] = dout.astype(BF16)
            sq, go = diff * diff, diff * orn
            sq_acc[...] += sq[:8] + sq[8:]
            dg_acc[...] += go[:8] + go[8:]

        def backward_cols(n):
            if n < 4:
                cols = slice(n * cb, (n + 1) * cb)
                dmix_ref[cols, :] = _dot(db_old[...], w_ref[cols, :], NT_DIMS).T.astype(BF16)
            else:
                cols = slice((n - 4) * cb, (n - 3) * cb)
                dw_ref[:, cols] += _dot(mixp_ref[...], db_old[:, cols])

        def step(forward, backward):
            if forward:
                o_scr[...] = _dot(mix_ref[...], w_ref[...], TN_DIMS)
            per = TT // rp // n_bwd
            for k in range(n_bwd):
                if forward:
                    for c in range(per * k, per * (k + 1)):
                        loss_rows(c)
                if backward:
                    backward_cols(k)
            if forward:
                db_old[...] = db_new[...]

        @pl.when(t == 0)
        def _():
            dw_ref[...] = jnp.zeros_like(dw_ref)
            sq_acc[...] = jnp.zeros_like(sq_acc)
            dg_acc[...] = jnp.zeros_like(dg_acc)
            step(True, False)

        @pl.when((t > 0) & (t < nj))
        def _():
            step(True, True)

        @pl.when(t == nj)
        def _():
            step(False, True)
            dwb_ref[...] = dw_ref[...].astype(BF16)
            loss_ref[...] = jnp.sum(sq_acc[...], keepdims=True) * (0.5 / D)
            dg_ref[...] = jnp.sum(dg_acc[...], axis=0, keepdims=True) * (1.0 / D)

    cur = lambda t: jnp.minimum(t, nj - 1)
    prev = lambda t: jnp.maximum(t - 1, 0)
    return pl.pallas_call(
        body, name="outproj", grid=(nj + 1,),
        in_specs=[pl.BlockSpec((D, TT), lambda t: (0, cur(t))), pl.BlockSpec((D, TT), lambda t: (0, prev(t))),
                  _full_spec((D, D))] + _x_specs3(cur) + [_full_spec((NM, D)), _full_spec((1, D))] + _x_specs3(cur),
        out_specs=[pl.BlockSpec((TT, D), lambda t: (cur(t), 0)), pl.BlockSpec((D, TT), lambda t: (0, prev(t))),
                   _full_spec((D, D)), _full_spec((1, 1)), _full_spec((1, D))],
        out_shape=[jax.ShapeDtypeStruct((L, D), F32), jax.ShapeDtypeStruct((D, L), BF16),
                   jax.ShapeDtypeStruct((D, D), BF16), jax.ShapeDtypeStruct((1, 1), F32),
                   jax.ShapeDtypeStruct((1, D), F32)],
        scratch_shapes=[pltpu.VMEM((D, D), F32), pltpu.VMEM((TT, D), F32), pltpu.VMEM((TT, D), BF16),
                        pltpu.VMEM((TT, D), BF16), pltpu.VMEM((8, D), F32), pltpu.VMEM((8, D), F32)],
        compiler_params=_params(),
    )(mix_t, mix_t, w_out, x, x, x, meta_full, fng, target, target, target)


def _gate_bwd(dmix_t, o_t, proj_t, cw_b, ga_b, gcn_b, L, after):
    nj = L // TT

    def body(dmix_ref, o_ref, za_ref, gb_ref, gc_ref, xc_ref, zc_ref, gcp_ref, xcp_ref, cw_ref, ga_ref, gcn_ref, _,
             do_ref, dd_ref, dg5_ref, dga_ref, dgc_ref, dcw_ref, carry_ref):
        i = pl.program_id(0)
        j = nj - 1 - i

        @pl.when(i == 0)
        def _():
            carry_ref[...] = jnp.zeros_like(carry_ref)
            dga_ref[...] = jnp.zeros_like(dga_ref)
            dgc_ref[...] = jnp.zeros_like(dgc_ref)
            dcw_ref[...] = jnp.zeros_like(dcw_ref)

        def group(h, c):
            r0 = pl.multiple_of(h * DH, DH)
            rows = pl.ds(r0, DH)
            sec = lambda s: pl.ds(s * DA + r0, DH)
            g = _gate_group(rows, o_ref, za_ref, gb_ref, gc_ref, xc_ref, zc_ref, gcp_ref, xcp_ref,
                            cw_ref, ga_ref, gcn_ref, j == 0)
            o, za, gb, gc, xc, zc, sa, sc = (g[n] for n in ("o", "za", "gb", "gc", "xc", "zc", "sa", "sc"))
            dya = dmix_ref[rows, :].astype(F32)
            dyc = dmix_ref[pl.ds(DA + r0, DH), :].astype(F32)

            dn = dya * (za * sa)
            dg5_ref[sec(0), :] = (dya * (g["oa"] * g["g_a"]) * (sa * (1.0 + za * (1.0 - sa)))).astype(BF16)
            dga_ref[rows, :] += _lane_tiles_sum(dn * g["oa"])
            dng = dn * g["g_a"]
            mean_a = jnp.mean(dng * g["oa"], axis=0, keepdims=True)
            do = (dng - g["oa"] * mean_a) * g["ra"]
            do_ref[rows, :] = do.astype(BF16)
            dd_ref[h] = jnp.sum(do * o, axis=0, keepdims=True)

            dnc = dyc * (zc * sc)
            dg5_ref[sec(4), :] = (dyc * (g["ec"] * g["g_c"]) * (sc * (1.0 + zc * (1.0 - sc)))).astype(BF16)
            dgc_ref[rows, :] += _lane_tiles_sum(dnc * g["ec"])
            dncg = dnc * g["g_c"]
            mean_c = jnp.mean(dncg * g["ec"], axis=0, keepdims=True)
            de = (dncg - g["ec"] * mean_c) * g["rc"]
            dg5_ref[sec(1), :] = (de * g["cv"]).astype(BF16)
            dcv = de * gb
            full = jnp.concatenate([dcv, carry_ref[rows, :]], axis=1)
            d1 = pltpu.roll(full, TT + TB - 1, 1)[:, :TT]
            d2 = pltpu.roll(full, TT + TB - 2, 1)[:, :TT]
            carry_ref[rows, :] = dcv[:, :TB]
            da = g["w2"] * dcv + g["w1"] * d1 + g["w0"] * d2
            dg5_ref[sec(2), :] = (da * xc).astype(BF16)
            dg5_ref[sec(3), :] = (da * gc).astype(BF16)
            dcw_ref[0, rows, :] += _lane_tiles_sum(dcv * g["a2"])
            dcw_ref[1, rows, :] += _lane_tiles_sum(dcv * g["a1"])
            dcw_ref[2, rows, :] += _lane_tiles_sum(dcv * g["a"])
            return c

        lax.fori_loop(0, H, group, 0, unroll=2)

    rj = lambda i: nj - 1 - i
    return pl.pallas_call(
        body, name="gate_bwd", grid=(nj,),
        in_specs=[pl.BlockSpec((2 * DA, TT), lambda i: (0, rj(i)))] + _gate_specs(nj, True) + [_UNREAD],
        out_specs=[
            pl.BlockSpec((DA, TT), lambda i: (0, rj(i))),
            pl.BlockSpec((H, 1, TT), lambda i: (0, 0, rj(i))),
            pl.BlockSpec((5 * DA, TT), lambda i: (0, rj(i))),
            _full_spec((DA, TB)), _full_spec((DA, TB)), _full_spec((3, DA, TB)),
        ],
        out_shape=[
            jax.ShapeDtypeStruct((DA, L), BF16),
            jax.ShapeDtypeStruct((H, 1, L), F32),
            jax.ShapeDtypeStruct((5 * DA, L), BF16),
            jax.ShapeDtypeStruct((DA, TB), F32),
            jax.ShapeDtypeStruct((DA, TB), F32),
            jax.ShapeDtypeStruct((3, DA, TB), F32),
        ],
        scratch_shapes=[pltpu.VMEM((DA, TB), F32)],
        compiler_params=_params(),
    )(dmix_t, o_t, proj_t, proj_t, proj_t, proj_t, proj_t, proj_t, proj_t, cw_b, ga_b, gcn_b, after)


def _attn_bwd(proj_t, kaug, vtok, do_t, lse, dd, cq, L):
    nk = L // TT

    def body(q_ref, kaug_ref, vtok_ref, kt_ref, do_ref, lse_ref, dd_ref, cq_ref,
             dq_ref, dk_ref, dv_ref, dck_ref, dcq_ref, dq_acc, kt1_scr, s_scr, dp_scr, dv_scr, dk_scr):
        i = pl.program_id(0)
        rows = [slice(g * DH, (g + 1) * DH) for g in range(HG)]
        ones = jnp.ones((DF, TT), BF16)
        zpad = jnp.zeros((KA - DH - DF, TT), BF16)
        for g in range(HG):
            kt1_scr[g] = jnp.concatenate([kt_ref[rows[g], :], ones], axis=0)
        dv_scr[...] = jnp.zeros_like(dv_scr)
        dk_scr[...] = jnp.zeros_like(dk_scr)

        def q_rows(g, q_off):
            bias = cq_ref[g, :, pl.ds(q_off, TT)] - lse_ref[g, :, pl.ds(q_off, TT)]
            return jnp.concatenate([q_ref[rows[g], pl.ds(q_off, TT)], _bias_rows(bias)], axis=0)

        def scores(jq, masked):
            q_off = pl.multiple_of(jq * TT, TT)
            for g in range(HG):
                s = _dot(kaug_ref[g], jnp.concatenate([q_rows(g, q_off), zpad], axis=0))
                if masked:
                    s = jnp.where(_causal_mask(), s, NEG)
                s_scr[g] = s
                dp_scr[g] = _dot(vtok_ref[g], do_ref[rows[g], pl.ds(q_off, TT)])

        def grads(jq):
            q_off = pl.multiple_of(jq * TT, TT)
            for g in range(HG):
                p = jnp.exp2(s_scr[g])
                ds = (p * (dp_scr[g] - dd_ref[g, :, pl.ds(q_off, TT)])).astype(BF16)
                do1 = jnp.concatenate([do_ref[rows[g], pl.ds(q_off, TT)], jnp.zeros((KA - DH, TT), BF16)], axis=0)
                q1 = jnp.concatenate([q_rows(g, q_off), zpad], axis=0)
                dv_scr[g] += _dot(p.astype(BF16), do1, NT_DIMS)
                dk_scr[g] += _dot(ds, q1, NT_DIMS)
                dq_acc[g, :, pl.ds(q_off, TT)] += _dot(kt1_scr[g], ds)

        @pl.when(i == 0)
        def _():
            dq_acc[...] = jnp.zeros_like(dq_acc)

        scores(i, True)

        def step(jq, c):
            grads(jq)
            scores(jq + 1, False)
            return c

        lax.fori_loop(i, nk - 1, step, 0)
        grads(nk - 1)
        for g in range(HG):
            dv_ref[rows[g], :] = dv_scr[g].T[:DH, :].astype(BF16)
            dk_t = dk_scr[g].T
            dk_ref[rows[g], :] = (dk_t[:DH, :] * LN2).astype(BF16)
            dck_ref[g] = dk_t[DH:DH + 1, :]

        @pl.when(i == nk - 1)
        def _():
            for g in range(HG):
                dq_ref[rows[g], :] = (dq_acc[g, :DH, :] * (DH ** -0.5)).astype(BF16)
                dcq_ref[g] = dq_acc[g, DH:DH + 1, :]

    assert HG == H
    head = lambda i: (0, 0)
    row = lambda i: (0, 0, 0)
    return pl.pallas_call(
        body, name="attn_bwd", grid=(nk,),
        in_specs=[
            pl.BlockSpec((DA, L), head),
            pl.BlockSpec((H, TT, KA), lambda i: (0, i, 0)),
            pl.BlockSpec((H, TT, DH), lambda i: (0, i, 0)),
            pl.BlockSpec((DA, TT), lambda i: (1, i)),
            pl.BlockSpec((DA, L), head),
            pl.BlockSpec((H, 1, L), row), pl.BlockSpec((H, 1, L), row), pl.BlockSpec((H, 1, L), row),
        ],
        out_specs=[
            pl.BlockSpec((DA, L), head),
            pl.BlockSpec((DA, TT), lambda i: (0, i)),
            pl.BlockSpec((DA, TT), lambda i: (0, i)),
            pl.BlockSpec((H, 1, TT), lambda i: (0, 0, i)),
            pl.BlockSpec((H, 1, L), row),
        ],
        out_shape=[jax.ShapeDtypeStruct((DA, L), BF16), jax.ShapeDtypeStruct((DA, L), BF16),
                   jax.ShapeDtypeStruct((DA, L), BF16), jax.ShapeDtypeStruct((H, 1, L), F32),
                   jax.ShapeDtypeStruct((H, 1, L), F32)],
        scratch_shapes=[
            pltpu.VMEM((HG, DH + DF, L), F32),
            pltpu.VMEM((HG, DH + DF, TT), BF16),
            pltpu.VMEM((HG, TT, TT), F32), pltpu.VMEM((HG, TT, TT), F32),
            pltpu.VMEM((HG, TT, KA), F32), pltpu.VMEM((HG, TT, KA), F32)],
        compiler_params=_params(),
    )(proj_t, kaug, vtok, proj_t, do_t, lse, dd, cq)


def _fgate_bwd(dcq, dck, sg, L):
    def body(dcq_ref, dck_ref, sg_ref, df_ref, db_ref):
        dc = jnp.concatenate([dcq_ref[h] - dck_ref[h] for h in range(H)], axis=0)
        idx = lax.broadcasted_iota(jnp.int32, (H, L), 1)
        r = dc
        s = 1
        while s < L:
            r = r + jnp.where(idx + s < L, pltpu.roll(r, L - s, 1), 0.0)
            s *= 2
        df = r * sg_ref[...]
        db_ref[...] = jnp.broadcast_to(jnp.sum(df, axis=1, keepdims=True), (H, TB))
        df_ref[...] = jnp.concatenate([df, jnp.zeros((DF - H, L), F32)], axis=0).astype(BF16)

    return pl.pallas_call(
        body, name="fgate_bwd",
        out_shape=[jax.ShapeDtypeStruct((DF, L), BF16), jax.ShapeDtypeStruct((H, TB), F32)],
        compiler_params=pltpu.CompilerParams(vmem_limit_bytes=VMEM_LIMIT),
    )(dcq, dck, sg)


def _inproj_bwd_x(w, dq_t, dk_t, dv_t, dg5_t, df_t, dout, x, meta_full, norm_g, L, after):
    nj = L // TT
    seq = x.shape[0]

    def body(w_ref, dq_ref, dk_ref, dv_ref, dg5_ref, df_ref, dout_ref, xa_ref, xb_ref, xc_ref, meta_ref, g_ref, _,
             gx_ref, dmeta_ref, dg_ref, dh_scr, sems):
        j = pl.program_id(0)
        slot = j % 2

        def copy_out(step, slot_):
            first = pltpu.make_async_copy(dh_scr.at[slot_, pl.ds(TB, TT - TB)], gx_ref.at[pl.ds(0, TT - TB)],
                                          sems.at[slot_])
            later = pltpu.make_async_copy(dh_scr.at[slot_], gx_ref.at[pl.ds(step * TT - TB, TT)], sems.at[slot_])
            return first, later

        @pl.when(j == 0)
        def _():
            dg_ref[...] = jnp.zeros_like(dg_ref)

        du = _dot(dq_ref[...], w_ref[0:DA, :], TN_DIMS)
        du += _dot(dk_ref[...], w_ref[DA:2 * DA, :], TN_DIMS)
        du += _dot(dv_ref[...], w_ref[2 * DA:3 * DA, :], TN_DIMS)
        du += _dot(dg5_ref[...], w_ref[3 * DA:NSEC * DA, :], TN_DIMS)
        du += _dot(df_ref[...], w_ref[NSEC * DA:DPROJ, :], TN_DIMS)
        hb = _h_tile(j, xa_ref, xb_ref, xc_ref, meta_ref)
        r = lax.rsqrt(jnp.mean(hb * hb, axis=-1, keepdims=True) + EPS)
        hn = hb * r
        dg_ref[...] += jnp.sum(du * hn, axis=0, keepdims=True)
        gu = du * g_ref[...]
        dh = dout_ref[...] + r * gu - hn * (r * jnp.mean(gu * hn, axis=-1, keepdims=True))

        dh_scr[slot] = dh

        @pl.when(j == 0)
        def _():
            dmeta_ref[...] = dh[P0:TB, :]
            copy_out(0, 0)[0].start()

        @pl.when(j >= 1)
        def _():
            copy_out(j, slot)[1].start()

        @pl.when(j == 1)
        def _():
            copy_out(0, 0)[0].wait()

        @pl.when(j >= 2)
        def _():
            copy_out(j - 1, 1 - slot)[1].wait()

        @pl.when(j == nj - 1)
        def _():
            copy_out(j, slot)[0 if nj == 1 else 1].wait()

    blk = lambda rows: pl.BlockSpec((rows, TT), lambda j: (0, j))
    return pl.pallas_call(
        body, name="inproj_bwd_x", grid=(nj,),
        in_specs=[_full_spec((DPROJ, D)), blk(DA), blk(DA), blk(DA), blk(5 * DA), blk(DF),
                  pl.BlockSpec((TT, D), lambda j: (j, 0))] + _x_specs3()
                 + [_full_spec((NM, D)), _full_spec((1, D)), _UNREAD],
        out_specs=[pl.BlockSpec(memory_space=pl.ANY), _full_spec((NM, D)), _full_spec((1, D))],
        out_shape=[jax.ShapeDtypeStruct((seq, D), F32), jax.ShapeDtypeStruct((NM, D), F32),
                   jax.ShapeDtypeStruct((1, D), F32)],
        scratch_shapes=[pltpu.VMEM((2, TT, D), F32), pltpu.SemaphoreType.DMA((2,))],
        compiler_params=_params(),
    )(w, dq_t, dk_t, dv_t, dg5_t, df_t, dout, x, x, x, meta_full, norm_g, after)


def _inproj_bwd_w(u, dq_t, dk_t, dv_t, dg5_t, df_t, L):
    def body(u_ref, dq_hbm, dk_hbm, dv_hbm, dg5_ref, df_ref, dw_ref, dwf_ref, qkv_scr, sems):
        s = pl.program_id(0)
        u_all = u_ref[...]
        fetch = [pltpu.make_async_copy(src, qkv_scr.at[k], sems.at[k])
                 for k, src in enumerate((dq_hbm, dk_hbm, dv_hbm))]

        @pl.when(s == 0)
        def _():
            for cp in fetch:
                cp.start()

        @pl.when(s < 5)
        def _():
            dw_ref[...] = _dot(dg5_ref[...], u_all)

        for k in range(3):
            @pl.when(s == 5 + k)
            def _(k=k):
                fetch[k].wait()
                dw_ref[...] = _dot(qkv_scr[k], u_all)

        @pl.when(s == NSEC - 1)
        def _():
            dwf_ref[...] = _dot(df_ref[...], u_all)

    once = lambda shape: pl.BlockSpec(shape, lambda s: (0, 0), pipeline_mode=pl.Buffered(1))
    any_spec = pl.BlockSpec(memory_space=pl.ANY)
    return pl.pallas_call(
        body, name="inproj_bwd_w", grid=(NSEC,),
        in_specs=[
            once((L, D)), any_spec, any_spec, any_spec,
            pl.BlockSpec((DA, L), lambda s: (jnp.minimum(s, 4), 0)),
            once((DF, L)),
        ],
        out_specs=[pl.BlockSpec((DA, D), lambda s: (jnp.where(s < 5, s + 3, s - 5), 0)), _full_spec((DF, D))],
        out_shape=[jax.ShapeDtypeStruct((NSEC * DA, D), F32), jax.ShapeDtypeStruct((DF, D), F32)],
        scratch_shapes=[pltpu.VMEM((3, DA, L), BF16), pltpu.SemaphoreType.DMA((3,))],
        compiler_params=_params(),
    )(u, dq_t, dk_t, dv_t, dg5_t, df_t)


def _adamw(w, g, m, v):
    m = ADAM_B1 * m + (1.0 - ADAM_B1) * g
    v = ADAM_B2 * v + (1.0 - ADAM_B2) * (g * g)
    m_hat = m / (1.0 - ADAM_B1 ** ADAM_STEP)
    v_hat = v / (1.0 - ADAM_B2 ** ADAM_STEP)
    delta = -ADAM_LR * (m_hat / (jnp.sqrt(v_hat) + ADAM_EPS) + ADAM_WD * w)
    return delta, m, v


def _adamw_big(own_in, land_in, own_out, land_out, w_in_t, m_in_t, v_in_t, w_out, m_out, v_out):
    cb = CB
    e_sh = D // NDEV
    in_shape = jax.ShapeDtypeStruct(w_in_t.shape, F32)
    out_shape = jax.ShapeDtypeStruct(w_out.shape, F32)

    def total(own_ref, land_ref, rows, chips):
        g = _pick_slab(0, own_ref, land_ref, rows, chips=chips).astype(F32)
        for j in range(1, own_ref.shape[0]):
            g = g + _pick_slab(j, own_ref, land_ref, rows, chips=chips).astype(F32)
        return g

    def body(oi_ref, li_ref, oo_ref, lo_ref, wi_ref, mi_ref, vi_ref, wo_ref, mo_ref, vo_ref,
             gi, di, mi, vi, go, do, mo, vo):
        g = total(oi_ref, li_ref, slice(0, WSHP), True)[:WSH]
        d, mn, vn = _adamw(wi_ref[...], g, mi_ref[...], vi_ref[...])
        gi[...], di[...], mi[...], vi[...] = g, d, mn, vn
        g = total(oo_ref, lo_ref, slice(0, e_sh), False)
        d, mn, vn = _adamw(wo_ref[0], g, mo_ref[0], vo_ref[0])
        go[0], do[0], mo[0], vo[0] = g, d, mn, vn

    slab = lambda n, rows: pl.BlockSpec((n, rows, cb), lambda i: (0, 0, i))
    ispec = pl.BlockSpec((WSH, cb), lambda i: (0, i))
    ospec = pl.BlockSpec((1, e_sh, cb), lambda i: (0, 0, i))
    return pl.pallas_call(
        body, name="adamw_big", grid=(D // cb,),
        in_specs=[slab(4, WSHP), slab(4, WSHP), slab(NDEV, e_sh), slab(NDEV, e_sh),
                  ispec, ispec, ispec, ospec, ospec, ospec],
        out_specs=[ispec] * 4 + [ospec] * 4, out_shape=[in_shape] * 4 + [out_shape] * 4,
        compiler_params=_params(),
    )(own_in, land_in, own_out, land_out, w_in_t, m_in_t, v_in_t, w_out, m_out, v_out)


F0 = 3 * DA


def _unshard_w_out(own, land):
    e_sh = D // NDEV

    def body(own_ref, land_ref, wo_ref):
        for j in range(NDEV):
            wo_ref[j * e_sh:(j + 1) * e_sh, :] = _pick_slab(j, own_ref, land_ref, slice(0, e_sh), per_peer=False)

    return pl.pallas_call(
        body, name="unshard_w_out", grid=(D // CB,),
        in_specs=[pl.BlockSpec((e_sh, CB), lambda i: (0, i)), pl.BlockSpec((NDEV, e_sh, CB), lambda i: (0, 0, i))],
        out_specs=pl.BlockSpec((D, CB), lambda i: (0, i)),
        out_shape=jax.ShapeDtypeStruct((D, D), BF16),
        compiler_params=_params(),
    )(own, land)


def _unshard_w_in(w_all, small_all, attn_gain, conv_gain):
    def body(w_ref, small_ref, ga_ref, gc_ref, wt_ref, meta_ref, cwb_ref, gab_ref, gcb_ref):
        i = pl.program_id(0)
        for k in range(CB // TB):
            meta_ref[:, k * TB:(k + 1) * TB] = small_ref[(CB // TB) * i + k, 0:NM, :]

        @pl.when(i == 0)
        def _():
            per_row = lambda line: jnp.broadcast_to(line, (TB, DA)).T
            cw = jnp.concatenate([small_ref[j, NM:NM + 3, 0:DH] for j in range(NDEV)], axis=1)
            for k in range(3):
                cwb_ref[k] = per_row(cw[k:k + 1, :])
            gab_ref[...] = per_row(ga_ref[...])
            gcb_ref[...] = per_row(gc_ref[...])

        def ref_rows(lo, hi):
            pieces, r = [], lo
            while r < hi:
                sh, off = divmod(r, WSH)
                n = min(hi - r, WSH - off)
                pieces.append(w_ref[sh, off:off + n, :])
                r += n
            return pieces

        for s in range(NSEC):
            lo = s * DA if s < 3 else s * DA + H
            wt_ref[s * DA:(s + 1) * DA, :] = jnp.concatenate(ref_rows(lo, lo + DA), axis=0)
        wt_ref[NSEC * DA:DPROJ, :] = jnp.concatenate(
            ref_rows(F0, F0 + H) + [jnp.zeros((DF - H, CB), BF16)], axis=0)

    return pl.pallas_call(
        body, name="unshard_w_in", grid=(D // CB,),
        in_specs=[pl.BlockSpec((NDEV, WSHP, CB), lambda i: (0, 0, i)), _full_spec(small_all.shape),
                  _full_spec((1, DA)), _full_spec((1, DA))],
        out_specs=[pl.BlockSpec((DPROJ, CB), lambda i: (0, i)), pl.BlockSpec((NM, CB), lambda i: (0, i)),
                   _full_spec((3, DA, TB)), _full_spec((DA, TB)), _full_spec((DA, TB))],
        out_shape=[jax.ShapeDtypeStruct((DPROJ, D), BF16), jax.ShapeDtypeStruct((NM, D), F32),
                   jax.ShapeDtypeStruct((3, DA, TB), F32), jax.ShapeDtypeStruct((DA, TB), F32),
                   jax.ShapeDtypeStruct((DA, TB), F32)],
        compiler_params=_params(),
    )(w_all, small_all, attn_gain, conv_gain)


def _shard_w_in_grads(dw_main, dw_f):
    def body(dm_ref, df_ref, p_ref):
        def ref_rows(lo, hi):
            pieces, r = [], lo
            while r < hi:
                if r < F0:
                    n = min(hi, F0) - r
                    pieces.append(dm_ref[r:r + n, :])
                elif r < F0 + H:
                    n = min(hi, F0 + H) - r
                    pieces.append(df_ref[r - F0:r - F0 + n, :])
                else:
                    n = hi - r
                    pieces.append(dm_ref[r - H:r - H + n, :])
                r += n
            return pieces

        for i in range(NDEV):
            rows = jnp.concatenate(ref_rows(i * WSH, (i + 1) * WSH) + [jnp.zeros((WSHP - WSH, CB), F32)], axis=0)
            p_ref[i] = rows.astype(BF16)

    col = lambda rows: pl.BlockSpec((rows, CB), lambda i: (0, i))
    return pl.pallas_call(
        body, name="shard_w_in_grads", grid=(D // CB,),
        in_specs=[col(NSEC * DA), col(DF)],
        out_specs=pl.BlockSpec((NDEV, WSHP, CB), lambda i: (0, 0, i)),
        out_shape=jax.ShapeDtypeStruct((NDEV, WSHP, D), BF16),
        compiler_params=_params(),
    )(dw_main, dw_f)


SMALL = ("norm_g", "final_norm_g", "attn_norm_g", "conv_norm_g", "b_f", "meta", "conv_w")


def _as_rows(x):
    return jnp.concatenate([x[:, r * TB:(r + 1) * TB] for r in range(x.shape[1] // TB)], axis=0)


def _as_line(rows):
    return jnp.concatenate([rows[r:r + 1, :] for r in range(rows.shape[0])], axis=1)


def _pad_rows(x, n=8):
    return jnp.concatenate([x, jnp.zeros((n - x.shape[0], x.shape[1]), F32)], axis=0)


def _tile_rows(a, rows, lanes=TB):
    a = a.reshape(rows, lanes)
    return jnp.pad(a, ((0, -rows % 8), (0, TB - lanes)))


def _pack_small_grads(dg_norm, dg_final, dga_p, dgc_p, dcw_p, db_b, dmeta, loss):
    def body(dgn_ref, dgf_ref, dga_ref, dgc_ref, dcw_ref, db_ref, dmeta_ref, loss_ref, out_ref):
        def lane_sums(p):
            return jnp.sum(p.T, axis=0, keepdims=True)

        lane = lax.broadcasted_iota(jnp.int32, (1, TB), 1)
        b_row = jnp.where(lane == H, loss_ref[...], 0.0)
        for h in range(H):
            b_row = b_row + jnp.where(lane == h, db_ref[h:h + 1, :], 0.0)
        common = jnp.concatenate([
            _as_rows(dgn_ref[...]), _as_rows(dgf_ref[...]), _pad_rows(_as_rows(lane_sums(dga_ref[...]))),
            _pad_rows(_as_rows(lane_sums(dgc_ref[...]))), _pad_rows(b_row)], axis=0)
        dcw = [lane_sums(dcw_ref[k]) for k in range(3)]
        for j in range(NDEV):
            cw = jnp.concatenate(
                [jnp.concatenate([r[:, j * DH:(j + 1) * DH], jnp.zeros((1, TB - DH), F32)], axis=1) for r in dcw],
                axis=0)
            out_ref[j] = jnp.concatenate([common, dmeta_ref[:, j * TB:(j + 1) * TB], _pad_rows(cw)], axis=0)

    return pl.pallas_call(
        body, name="pack_small_grads", out_shape=jax.ShapeDtypeStruct((NDEV, SROWS, TB), F32),
    )(dg_norm, dg_final, dga_p, dgc_p, dcw_p, db_b, dmeta, loss)


def _adamw_small(own, land, params):
    flat = [a for n in SMALL for a in params[n]]

    def body(*refs):
        own_ref, land_ref = refs[:2]
        ins = refs[2:2 + 3 * len(SMALL)]
        outs = refs[2 + 3 * len(SMALL):]
        g = _pick_slab(0, own_ref, land_ref, slice(0, SROWS))
        for j in range(1, NDEV):
            g = g + _pick_slab(j, own_ref, land_ref, slice(0, SROWS))
        grads = dict(
            norm_g=_as_line(g[0:8]), final_norm_g=_as_line(g[8:16]), attn_norm_g=_as_line(g[16:20]),
            conv_norm_g=_as_line(g[24:28]), b_f=g[32:33, :H], meta=g[40:56], conv_w=g[56:59, :DH][None])
        for i, n in enumerate(SMALL):
            w_ref, m_ref, v_ref = ins[3 * i:3 * i + 3]
            d, mn, vn = _adamw(w_ref[...], grads[n], m_ref[...], v_ref[...])
            for o_ref, val in zip(outs[4 * i:4 * i + 4], (grads[n], d, mn, vn)):
                o_ref[...] = val
        outs[-1][...] = g[32:33, H:H + 1]

    shapes = [jax.ShapeDtypeStruct(params[n][0].shape, F32) for n in SMALL for _ in range(4)]
    res = pl.pallas_call(
        body, name="adamw_small", out_shape=shapes + [jax.ShapeDtypeStruct((1, 1), F32)],
    )(own, land, *flat)
    return {n: res[4 * i:4 * i + 4] for i, n in enumerate(SMALL)}, res[-1]


def kernel(x, meta, norm_g, w_in, b_f, conv_w, attn_norm_g, conv_norm_g, w_out, final_norm_g, loss_target, m_meta, m_norm_g, m_w_in, m_b_f, m_conv_w, m_attn_norm_g, m_conv_norm_g, m_w_out, m_final_norm_g, v_meta, v_norm_g, v_w_in, v_b_f, v_conv_w, v_attn_norm_g, v_conv_norm_g, v_w_out, v_final_norm_g):
    seq = x.shape[1]
    L = seq + TB
    assert x.shape == (1, seq, D) and L % TT == 0 and w_in.shape == (1, D, WSH)
    x2 = x[0]
    tgt = loss_target[0]

    w_in_slab = jnp.pad(w_in[0].T, ((0, WSHP - WSH), (0, 0))).astype(BF16)
    w_out_slab = w_out[0].astype(BF16)
    meta_slab = jnp.concatenate([meta, _tile_rows(conv_w[0], 3, DH)], axis=0)
    wout_flight = _split_start(w_out_slab, "gather_w_out_start", per_peer=False)
    w_all, small_all = _all_gather([w_in_slab, meta_slab], "gather_w_in")

    w_t, meta_full, cw_b, ga_b, gcn_b = _unshard_w_in(w_all, small_all, attn_norm_g, conv_norm_g)

    u, proj_t, f_t, ktok, vtok = _inproj_fwd(x2, meta_full, norm_g, w_t, L, after=wout_flight[4])
    cq, kaug, sg = _fgate_fwd(f_t, b_f.reshape(H, 1), ktok, L)
    o_t, lse = _attn_fwd(proj_t, kaug, cq, L)
    mix_t = _gate_fwd(o_t, proj_t, cw_b, ga_b, gcn_b, L)

    w_out_own, w_out_land = _split_wait(wout_flight, mix_t, "gather_w_out_wait", per_peer=False)
    w_out_full = _unshard_w_out(w_out_own, w_out_land)
    dout, dmix_t, dw_out, loss_part, dg_final = _outproj(
        mix_t, w_out_full, x2, meta_full, final_norm_g.reshape(1, D), tgt, L)
    dwo_flight = _split_start(dw_out.reshape(NDEV, D // NDEV, D), "exchange_dw_out_start", per_peer=True)
    do_t, dd, dg5_t, dga_p, dgc_p, dcw_p = _gate_bwd(dmix_t, o_t, proj_t, cw_b, ga_b, gcn_b, L, after=dwo_flight[4])
    dq_t, dk_t, dv_t, dck, dcq = _attn_bwd(proj_t, kaug, vtok, do_t, lse, dd, cq, L)
    df_t, db_f = _fgate_bwd(dcq, dck, sg, L)
    dw_main, dw_f = _inproj_bwd_w(u, dq_t, dk_t, dv_t, dg5_t, df_t, L)
    dwi_parts = _shard_w_in_grads(dw_main, dw_f)
    dwi_chip = _pair_sum(dwi_parts, _pair_exchange(dwi_parts, "exchange_dw_in_pair"))
    dwi_flight = _split_start(dwi_chip, "exchange_dw_in_start", per_peer=True, chips=True)
    grad_x, dmeta, dg_norm = _inproj_bwd_x(
        w_t, dq_t, dk_t, dv_t, dg5_t, df_t, dout, x2, meta_full, norm_g, L, after=dwi_flight[4])
    small_parts = _pack_small_grads(dg_norm, dg_final, dga_p, dgc_p, dcw_p, db_f, dmeta, loss_part)
    small_flight = _split_start(small_parts, "exchange_small_start", per_peer=True)
    dwo_own, dwo_land = _split_wait(dwo_flight, small_flight[4], "exchange_dw_out_wait", per_peer=True)
    dwi_own, dwi_land = _split_wait(dwi_flight, dwo_land, "exchange_dw_in_wait", per_peer=True, chips=True)

    big_out = _adamw_big(dwi_own, dwi_land, dwo_own, dwo_land,
                         w_in[0].T, m_w_in[0].T, v_w_in[0].T, w_out, m_w_out, v_w_out)
    g_w_in, d_w_in, nm_w_in, nv_w_in = [a.T[None] for a in big_out[:4]]
    g_w_out, d_w_out, nm_w_out, nv_w_out = big_out[4:]
    sm_own, sm_land = _split_wait(small_flight, big_out[4], "exchange_small_wait", per_peer=True)
    line = lambda a: a.reshape(1, D)
    small, loss = _adamw_small(sm_own, sm_land, dict(
        norm_g=(norm_g, m_norm_g, v_norm_g),
        final_norm_g=(line(final_norm_g), line(m_final_norm_g), line(v_final_norm_g)),
        attn_norm_g=(attn_norm_g, m_attn_norm_g, v_attn_norm_g),
        conv_norm_g=(conv_norm_g, m_conv_norm_g, v_conv_norm_g),
        b_f=(b_f, m_b_f, v_b_f), meta=(meta, m_meta, v_meta), conv_w=(conv_w, m_conv_w, v_conv_w)))
    small["final_norm_g"] = [a.reshape(D) for a in small["final_norm_g"]]
    order = ("meta", "norm_g", "w_in", "b_f", "conv_w", "attn_norm_g", "conv_norm_g", "w_out", "final_norm_g")
    groups = []
    for k, (wi, wo) in enumerate(((g_w_in, g_w_out), (d_w_in, d_w_out), (nm_w_in, nm_w_out), (nv_w_in, nv_w_out))):
        d = dict({n: small[n][k] for n in SMALL}, w_in=wi, w_out=wo)
        groups.append([d[n] for n in order])
    return (loss[0, 0], grad_x[None], *groups[0], *groups[1], *groups[2], *groups[3])
```

```python
import jax
import jax.numpy as jnp
from jax import lax
from jax.experimental import pallas as pl
from jax.experimental.pallas import tpu as pltpu

F32 = jnp.float32
BF16 = jnp.bfloat16

D = 1024
DA = 512
H = 8
DH = 64
NM = 16
TB = 128
P0 = TB - NM
TT = 3 * TB
HG = 8
NDEV = 8
NSEC = 8
DF = 16
DPROJ = NSEC * DA + DF
WSH = 513
WSHP = 528
WROWS = WSHP + D // NDEV
SROWS = 64
EPS = 1e-6
NEG = -1e30
LOG2E = 1.4426950408889634
LN2 = 0.6931471805599453
QSCALE = DH ** -0.5 * LOG2E
KA = 128
CB = 256
VMEM_LIMIT = 56 * 1024 * 1024

ADAM_LR = 0.001
ADAM_B1 = 0.9
ADAM_B2 = 0.999
ADAM_EPS = 1e-08
ADAM_WD = 0.01
ADAM_STEP = 10

NT_DIMS = (((1,), (1,)), ((), ()))
TN_DIMS = (((0,), (0,)), ((), ()))
MESH = pl.DeviceIdType.MESH


def _params(n_axes=1, vmem=VMEM_LIMIT):
    return pltpu.CompilerParams(dimension_semantics=("arbitrary",) * n_axes, vmem_limit_bytes=vmem)


def _dot(a, b, dims=None):
    if dims is None:
        return jnp.dot(a, b, preferred_element_type=F32)
    return lax.dot_general(a, b, dims, preferred_element_type=F32)


def _my_place():
    return lax.axis_index("x"), lax.axis_index("y"), lax.axis_index("c")


def _all_gather(xs, name):
    n = len(xs)

    def body(*refs):
        x_refs, out_refs = refs[:n], refs[n:2 * n]
        send_sems, recv_sems, local_sems = refs[2 * n:]
        mx, my, mc = _my_place()

        def across(px, py, pc, axis_a):
            flip_x = pc if axis_a else 1 - pc
            return (px + flip_x) % 2, (py + 1 - flip_x) % 2, pc

        def idx(p):
            return 4 * p[0] + 2 * p[1] + p[2]

        me, sib = (mx, my, mc), (mx, my, 1 - mc)
        a_nbr, b_nbr = across(*me, True), across(*me, False)
        diag = across(*b_nbr, True)
        sib_a, sib_b = across(*sib, True), across(*sib, False)
        sib_diag = across(*sib_b, True)

        waits = []
        for t in range(n):
            out_ref = out_refs[t]

            def copy(k, block, to, src=None, out_ref=out_ref, t=t):
                return pltpu.make_async_remote_copy(
                    src_ref=out_ref.at[idx(block)] if src is None else src, dst_ref=out_ref.at[idx(block)],
                    send_sem=send_sems.at[7 * t + k], recv_sem=recv_sems.at[7 * t + k],
                    device_id=to, device_id_type=MESH)

            mine = pltpu.make_async_copy(x_refs[t], out_ref.at[idx(me)], local_sems.at[t])
            mine.start()
            started = [copy(0, me, sib, src=x_refs[t]), copy(1, me, a_nbr, src=x_refs[t]),
                       copy(2, me, b_nbr, src=x_refs[t])]
            for cp in started:
                cp.start()
            waits.append((copy, mine, started))
        relays = ((1, a_nbr, ((3, b_nbr), (4, sib))), (2, b_nbr, ((5, sib),)), (3, diag, ((6, sib),)))
        for landed, block, onward in relays:
            for copy, _, started in waits:
                copy(landed, block, me).wait_recv()
                for k, to in onward:
                    started.append(copy(k, block, to))
                    started[-1].start()
        for copy, mine, started in waits:
            for k, block in ((0, sib), (4, sib_a), (5, sib_b), (6, sib_diag)):
                copy(k, block, me).wait_recv()
            for cp in started:
                cp.wait_send()
            mine.wait()

    any_spec = pl.BlockSpec(memory_space=pl.ANY)
    return pl.pallas_call(
        body, name=name,
        out_shape=[jax.ShapeDtypeStruct((NDEV,) + x.shape, x.dtype) for x in xs],
        in_specs=[any_spec] * n, out_specs=[any_spec] * n,
        scratch_shapes=[pltpu.SemaphoreType.DMA((7 * n,)), pltpu.SemaphoreType.DMA((7 * n,)),
                        pltpu.SemaphoreType.DMA((n,))],
    )(*xs)


_HBM = pl.BlockSpec(memory_space=pltpu.HBM)
_UNREAD = pl.BlockSpec(memory_space=pl.ANY)
_SEM = pl.BlockSpec(memory_space=pltpu.SEMAPHORE)
_EFFECT = pltpu.SideEffectType.DATAFLOW_SIDE_EFFECTING


def _peer_of(m, place):
    mx, my, mc = place
    return ((1 - mx) if m & 4 else mx, (1 - my) if m & 2 else my, (1 - mc) if m & 1 else mc)


def _party(chips):
    if chips:
        return (lambda p: 2 * p[0] + p[1]), (2, 4, 6)
    return (lambda p: 4 * p[0] + 2 * p[1] + p[2]), tuple(range(1, NDEV))


def _split_copies(src_ref, land_ref, send_sems, recv_sems, per_peer, incoming, chips):
    place = _my_place()
    slot, masks = _party(chips)
    me = slot(place)
    out = []
    for k, m in enumerate(masks):
        there = _peer_of(m, place)
        peer = slot(there)
        src = (src_ref.at[me] if incoming else src_ref.at[peer]) if per_peer else src_ref
        out.append(pltpu.make_async_remote_copy(
            src_ref=src, dst_ref=land_ref.at[peer if incoming else me],
            send_sem=send_sems.at[k], recv_sem=recv_sems.at[k], device_id=there, device_id_type=MESH))
    return out


def _split_start(src, name, per_peer, chips=False):
    slab = src.shape[1:] if per_peer else src.shape
    n = len(_party(chips)[1])

    def body(src_ref, land_ref, send_sems, recv_sems, src_thru, land_thru, token):
        for cp in _split_copies(src_ref, land_ref, send_sems, recv_sems, per_peer, False, chips):
            cp.start()
        token[...] = jnp.zeros_like(token)

    return pl.pallas_call(
        body, name=name,
        out_shape=(pltpu.SemaphoreType.DMA((n,)), pltpu.SemaphoreType.DMA((n,)),
                   pltpu.HBM(src.shape, src.dtype), pltpu.HBM((n + 1,) + slab, src.dtype),
                   jax.ShapeDtypeStruct((8, TB), F32)),
        in_specs=(_HBM, _HBM), out_specs=(_SEM, _SEM, _HBM, _HBM, pl.BlockSpec(memory_space=pltpu.VMEM)),
        input_output_aliases={0: 2, 1: 3},
        compiler_params=pltpu.CompilerParams(has_side_effects=_EFFECT),
    )(pltpu.with_memory_space_constraint(src, pltpu.HBM),
      pltpu.with_memory_space_constraint(lax.empty((n + 1,) + slab, src.dtype), pltpu.HBM))


def _split_wait(handles, after, name, per_peer, chips=False):
    send_sems, recv_sems, src_thru, land_thru, _ = handles

    def body(src_ref, land_ref, send_sems, recv_sems, after_ref, src_out, land_out):
        for cp in _split_copies(src_ref, land_ref, send_sems, recv_sems, per_peer, False, chips):
            cp.wait_send()
        for cp in _split_copies(src_ref, land_ref, send_sems, recv_sems, per_peer, True, chips):
            cp.wait_recv()

    return pl.pallas_call(
        body, name=name,
        out_shape=(pltpu.HBM(src_thru.shape, src_thru.dtype), pltpu.HBM(land_thru.shape, land_thru.dtype)),
        in_specs=(_HBM, _HBM, _SEM, _SEM, pl.BlockSpec(memory_space=pl.ANY)), out_specs=(_HBM, _HBM),
        input_output_aliases={0: 0, 1: 1},
        compiler_params=pltpu.CompilerParams(has_side_effects=_EFFECT),
    )(src_thru, land_thru, send_sems, recv_sems, after)


def _pick_slab(j, own_ref, land_ref, rows, per_peer=True, chips=False):
    me = _party(chips)[0](_my_place())
    own = (lambda: own_ref[j, rows, :]) if per_peer else (lambda: own_ref[rows, :])
    return lax.cond(me == j, own, lambda: land_ref[j, rows, :])


def _pair_exchange(p, name):
    def body(p_ref, got_ref, send_sems, recv_sems):
        mx, my, mc = _my_place()
        copies = [pltpu.make_async_remote_copy(
            src_ref=p_ref.at[4 + q], dst_ref=got_ref.at[q], send_sem=send_sems.at[q],
            recv_sem=recv_sems.at[q], device_id=(mx, my, 1 - mc), device_id_type=MESH) for q in range(4)]
        for cp in copies:
            cp.start()
        for cp in copies:
            cp.wait_recv()
        for cp in copies:
            cp.wait_send()

    any_spec = pl.BlockSpec(memory_space=pl.ANY)
    return pl.pallas_call(
        body, name=name, out_shape=jax.ShapeDtypeStruct((4,) + p.shape[1:], p.dtype),
        in_specs=[any_spec], out_specs=any_spec,
        scratch_shapes=[pltpu.SemaphoreType.DMA((4,)), pltpu.SemaphoreType.DMA((4,))],
    )(p)


def _pair_sum(p, got):
    rows = p.shape[1]

    def body(p_ref, got_ref, out_ref):
        for q in range(4):
            out_ref[q] = (p_ref[q].astype(F32) + got_ref[q].astype(F32)).astype(BF16)

    blk = lambda n: pl.BlockSpec((n, rows, CB), lambda i: (0, 0, i))
    return pl.pallas_call(
        body, name="pair_sum", grid=(D // CB,), in_specs=[blk(4), blk(4)], out_specs=blk(4),
        out_shape=jax.ShapeDtypeStruct((4, rows, D), BF16), compiler_params=_params(),
    )(p, got)


def _h_block(t, x_ref, meta_ref):
    first = jnp.concatenate([jnp.zeros((P0, D), F32), meta_ref[...]], axis=0)
    return jnp.where(t == 0, first, x_ref[...])


def _x_specs3(tile=lambda j: j):
    return [pl.BlockSpec((TB, D), lambda j: (jnp.maximum(3 * tile(j) - 1, 0), 0)),
            pl.BlockSpec((TB, D), lambda j: (3 * tile(j), 0)),
            pl.BlockSpec((TB, D), lambda j: (3 * tile(j) + 1, 0))]


def _h_tile(j, xa_ref, xb_ref, xc_ref, meta_ref):
    first = jnp.concatenate([jnp.zeros((P0, D), F32), meta_ref[...]], axis=0)
    return jnp.concatenate([jnp.where(j == 0, first, xa_ref[...]), xb_ref[...], xc_ref[...]], axis=0)


def _full_spec(shape):
    return pl.BlockSpec(shape, lambda *_: (0,) * len(shape))


def _sigmoid(z):
    return 1.0 / (1.0 + jnp.exp(-z))


def _lane_tiles_sum(x):
    out = x[:, :TB]
    for i in range(1, x.shape[1] // TB):
        out = out + x[:, i * TB:(i + 1) * TB]
    return out


def _inproj_fwd(x, meta_full, norm_g, w_t, L, after):
    nj = L // TT

    def body(xa_ref, xb_ref, xc_ref, meta_ref, g_ref, w_ref, _, u_ref, proj_ref, f_ref, ktok_ref, vtok_ref):
        hb = _h_tile(pl.program_id(0), xa_ref, xb_ref, xc_ref, meta_ref)
        r = lax.rsqrt(jnp.mean(hb * hb, axis=-1, keepdims=True) + EPS)
        u = (hb * r * g_ref[...]).astype(BF16)
        u_ref[...] = u
        for s in range(NSEC):
            p = _dot(u, w_ref[s * DA:(s + 1) * DA, :], NT_DIMS)
            if s == 0:
                p = p * QSCALE
            if s in (1, 2):
                tok_ref = ktok_ref if s == 1 else vtok_ref
                for h in range(H):
                    tok_ref[h] = p[:, h * DH:(h + 1) * DH].astype(BF16)
            proj_ref[s * DA:(s + 1) * DA, :] = p.T.astype(BF16)
        f_ref[...] = _dot(w_ref[NSEC * DA:DPROJ, :], u, NT_DIMS)[:H]

    return pl.pallas_call(
        body, name="inproj_fwd", grid=(nj,),
        in_specs=_x_specs3() + [_full_spec((NM, D)), _full_spec((1, D)), _full_spec((DPROJ, D)), _UNREAD],
        out_specs=[
            pl.BlockSpec((TT, D), lambda t: (t, 0)),
            pl.BlockSpec((NSEC * DA, TT), lambda t: (0, t)),
            pl.BlockSpec((H, TT), lambda t: (0, t)),
            pl.BlockSpec((H, TT, DH), lambda t: (0, t, 0)),
            pl.BlockSpec((H, TT, DH), lambda t: (0, t, 0)),
        ],
        out_shape=[
            jax.ShapeDtypeStruct((L, D), BF16),
            jax.ShapeDtypeStruct((NSEC * DA, L), BF16),
            jax.ShapeDtypeStruct((H, L), F32),
            jax.ShapeDtypeStruct((H, L, DH), BF16),
            jax.ShapeDtypeStruct((H, L, DH), BF16),
        ],
        compiler_params=_params(),
    )(x, x, x, meta_full, norm_g, w_t, after)


def _split3(x):
    hi = x.astype(BF16).astype(F32)
    r = x - hi
    mid = r.astype(BF16).astype(F32)
    return hi, mid, (r - mid).astype(BF16).astype(F32)


def _bias_rows(bias):
    one = jnp.ones((1, TT), F32)
    zero = jnp.zeros((1, TT), F32)
    parts = [zero] * 3 if bias is None else list(_split3(bias))
    return jnp.concatenate([one] * 3 + parts + [zero] * (DF - 6), axis=0).astype(BF16)


def _fgate_fwd(f_t, b_col, ktok, L):
    nb = L // TB

    def body(f_ref, b_ref, ktok_ref, cq_ref, kaug_ref, sg_ref, bias_scr):
        h = pl.program_id(0)

        @pl.when(h == 0)
        def _():
            z = f_ref[...] + b_ref[...]
            idx = lax.broadcasted_iota(jnp.int32, (H, L), 1)
            real = idx >= P0
            lf = jnp.where(real, jnp.minimum(z, 0.0) - jnp.log1p(jnp.exp(-jnp.abs(z))), 0.0)
            sg_ref[...] = jnp.where(real, 1.0 / (1.0 + jnp.exp(z)), 0.0)
            c = lf
            s = 1
            while s < L:
                c = c + jnp.where(idx >= s, pltpu.roll(c, s, 1), 0.0)
                s *= 2
            c = c * LOG2E
            for hh in range(H):
                cq_ref[hh] = c[hh:hh + 1, :]
            for part, val in enumerate(_split3(-jnp.where(real, c, -NEG))):
                for hh in range(H):
                    bias_scr[part * H + hh] = val[hh:hh + 1, :]

        lane = lax.broadcasted_iota(jnp.int32, (TB, KA), 1)
        head = jnp.zeros((DH, TB), F32)
        tail = jnp.concatenate([jnp.ones((3, TB), F32), jnp.zeros((KA - DH - 6, TB), F32)], axis=0)
        for b in range(nb):
            blk = slice(b * TB, (b + 1) * TB)
            cols = jnp.concatenate(
                [head] + [bias_scr[part * H + h, :, blk] for part in range(3)] + [tail], axis=0).T
            k = jnp.concatenate([ktok_ref[0, blk, :].astype(F32), jnp.zeros((TB, KA - DH), F32)], axis=1)
            kaug_ref[0, blk, :] = jnp.where(lane < DH, k, cols).astype(BF16)

    return pl.pallas_call(
        body, name="fgate_fwd", grid=(H,),
        in_specs=[_full_spec((H, L)), _full_spec((H, 1)), pl.BlockSpec((1, L, DH), lambda h: (h, 0, 0))],
        out_specs=[_full_spec((H, 1, L)), pl.BlockSpec((1, L, KA), lambda h: (h, 0, 0)), _full_spec((H, L))],
        out_shape=[
            jax.ShapeDtypeStruct((H, 1, L), F32),
            jax.ShapeDtypeStruct((H, L, KA), BF16),
            jax.ShapeDtypeStruct((H, L), F32),
        ],
        scratch_shapes=[pltpu.VMEM((3 * H, 1, L), F32)],
        compiler_params=_params(),
    )(f_t, b_col, ktok)


def _causal_mask():
    r = lax.broadcasted_iota(jnp.int32, (TT, TT), 0)
    c = lax.broadcasted_iota(jnp.int32, (TT, TT), 1)
    return r <= c


def _attn_fwd(proj_t, kaug, cq, L):
    nq = L // TT

    def body(q_ref, qn_ref, kaug_ref, v_ref, cq_ref, o_ref, lse_ref,
             qa_scr, s_scr, cmax_scr, m_scr, p_scr, alpha_scr, acc_scr):
        j = pl.program_id(0)
        rows = [slice(g * DH, (g + 1) * DH) for g in range(HG)]
        ones = jnp.ones((DF, TT), BF16)

        def load_queries(ref):
            for g in range(HG):
                qa_scr[g] = jnp.concatenate(
                    [ref[rows[g], :], _bias_rows(None), jnp.zeros((KA - DH - DF, TT), BF16)], axis=0)

        def scores(kt, masked):
            k_off = pl.multiple_of(kt * TT, TT)
            for g in range(HG):
                s = _dot(kaug_ref[g, pl.ds(k_off, TT), :], qa_scr[g])
                if masked:
                    s = jnp.where(_causal_mask(), s, NEG)
                s_scr[g] = s
                cmax_scr[g] = jnp.max(s, axis=0, keepdims=True)

        def softmax():
            for g in range(HG):
                m_old = m_scr[g]
                m_new = jnp.maximum(m_old, cmax_scr[g])
                alpha_scr[g] = jnp.exp2(m_old - m_new)
                p_scr[g] = jnp.exp2(s_scr[g] - m_new).astype(BF16)
                m_scr[g] = m_new

        def weighted_sum(kt):
            k_off = pl.multiple_of(kt * TT, TT)
            for g in range(HG):
                v1 = jnp.concatenate([v_ref[rows[g], pl.ds(k_off, TT)], ones], axis=0)
                acc_scr[g] = alpha_scr[g] * acc_scr[g] + _dot(v1, p_scr[g])

        @pl.when(j == 0)
        def _():
            load_queries(q_ref)
            scores(0, True)

        m_scr[...] = jnp.full_like(m_scr, NEG)
        acc_scr[...] = jnp.zeros_like(acc_scr)

        @pl.when(j >= 1)
        def _():
            softmax()
            scores(j - 1, False)

        def step(i, c):
            weighted_sum(j - i + 1)
            softmax()
            scores(j - i - 1, False)
            return c

        lax.fori_loop(1, j, step, 0)

        def drain(second_last, next_tile):
            if second_last:
                weighted_sum(1)
            softmax()
            if next_tile:
                load_queries(qn_ref)
                scores(j + 1, True)
            weighted_sum(0)

        @pl.when(j == 0)
        def _():
            drain(False, nq > 1)

        @pl.when((j >= 1) & (j < nq - 1))
        def _():
            drain(True, True)

        @pl.when((j >= 1) & (j == nq - 1))
        def _():
            drain(True, False)

        for g in range(HG):
            l = acc_scr[g, DH:DH + 1, :]
            o_ref[rows[g], :] = acc_scr[g, :DH, :] * (1.0 / l)
            lse_ref[g] = m_scr[g] + jnp.log2(l) + cq_ref[g]

    assert HG == H
    return pl.pallas_call(
        body, name="attn_fwd", grid=(nq,),
        in_specs=[
            pl.BlockSpec((DA, TT), lambda j: (0, j)),
            pl.BlockSpec((DA, TT), lambda j: (0, jnp.minimum(j + 1, nq - 1))),
            pl.BlockSpec((H, L, KA), lambda j: (0, 0, 0)),
            pl.BlockSpec((DA, L), lambda j: (2, 0)),
            pl.BlockSpec((H, 1, TT), lambda j: (0, 0, j)),
        ],
        out_specs=[
            pl.BlockSpec((DA, TT), lambda j: (0, j)),
            pl.BlockSpec((H, 1, TT), lambda j: (0, 0, j)),
        ],
        out_shape=[jax.ShapeDtypeStruct((DA, L), F32), jax.ShapeDtypeStruct((H, 1, L), F32)],
        scratch_shapes=[pltpu.VMEM((HG, KA, TT), BF16), pltpu.VMEM((HG, TT, TT), F32), pltpu.VMEM((HG, 1, TT), F32),
                        pltpu.VMEM((HG, 1, TT), F32), pltpu.VMEM((HG, TT, TT), BF16), pltpu.VMEM((HG, 1, TT), F32),
                        pltpu.VMEM((HG, DH + DF, TT), F32)],
        compiler_params=_params(),
    )(proj_t, proj_t, kaug, proj_t, cq)


def _gate_group(rows, o_ref, za_ref, gb_ref, gc_ref, xc_ref, zc_ref, gcp_ref, xcp_ref, cw_ref, ga_ref, gcn_ref, first):
    n_rep = TT // TB
    f32 = lambda r: r[rows, :].astype(F32)
    o, za, gb, gc, xc, zc = o_ref[rows, :], f32(za_ref), f32(gb_ref), f32(gc_ref), f32(xc_ref), f32(zc_ref)
    a = gc * xc
    a_prev = jnp.where(first, 0.0, f32(gcp_ref) * f32(xcp_ref))
    full = jnp.concatenate([a_prev, a], axis=1)
    a1 = pltpu.roll(full, 1, 1)[:, TB:]
    a2 = pltpu.roll(full, 2, 1)[:, TB:]
    w0 = jnp.tile(cw_ref[0, rows, :], (1, n_rep))
    w1 = jnp.tile(cw_ref[1, rows, :], (1, n_rep))
    w2 = jnp.tile(cw_ref[2, rows, :], (1, n_rep))
    cv = w0 * a2 + w1 * a1 + w2 * a
    e = gb * cv
    rc = lax.rsqrt(jnp.mean(e * e, axis=0, keepdims=True) + EPS)
    ec = e * rc
    ra = lax.rsqrt(jnp.mean(o * o, axis=0, keepdims=True) + EPS)
    oa = o * ra
    g_a = jnp.tile(ga_ref[rows, :], (1, n_rep))
    g_c = jnp.tile(gcn_ref[rows, :], (1, n_rep))
    sa = _sigmoid(za)
    sc = _sigmoid(zc)
    return dict(o=o, za=za, gb=gb, gc=gc, xc=xc, zc=zc, a=a, a1=a1, a2=a2, w0=w0, w1=w1, w2=w2, cv=cv, e=e,
                rc=rc, ec=ec, ra=ra, oa=oa, g_a=g_a, g_c=g_c, sa=sa, sc=sc)


def _gate_specs(nj, rev):
    def jj(i):
        return (nj - 1 - i) if rev else i

    def sec(s):
        return pl.BlockSpec((DA, TT), lambda i: (s, jj(i)))

    def halo(s):
        return pl.BlockSpec((DA, TB), lambda i: (s, jnp.maximum(3 * jj(i) - 1, 0)))

    return [pl.BlockSpec((DA, TT), lambda i: (0, jj(i))), sec(3), sec(4), sec(5), sec(6), sec(7), halo(5), halo(6),
            _full_spec((3, DA, TB)), _full_spec((DA, TB)), _full_spec((DA, TB))]


def _gate_fwd(o_t, proj_t, cw_b, ga_b, gcn_b, L):
    nj = L // TT

    def body(o_ref, za_ref, gb_ref, gc_ref, xc_ref, zc_ref, gcp_ref, xcp_ref, cw_ref, ga_ref, gcn_ref, mix_ref):
        j = pl.program_id(0)

        def group(h, c):
            r0 = pl.multiple_of(h * DH, DH)
            g = _gate_group(pl.ds(r0, DH), o_ref, za_ref, gb_ref, gc_ref, xc_ref, zc_ref, gcp_ref, xcp_ref,
                            cw_ref, ga_ref, gcn_ref, j == 0)
            mix_ref[pl.ds(r0, DH), :] = (g["oa"] * g["g_a"] * (g["za"] * g["sa"])).astype(BF16)
            mix_ref[pl.ds(DA + r0, DH), :] = (g["ec"] * g["g_c"] * (g["zc"] * g["sc"])).astype(BF16)
            return c

        lax.fori_loop(0, H, group, 0, unroll=2)

    return pl.pallas_call(
        body, name="gate_fwd", grid=(nj,),
        in_specs=_gate_specs(nj, False),
        out_specs=pl.BlockSpec((2 * DA, TT), lambda j: (0, j)),
        out_shape=jax.ShapeDtypeStruct((2 * DA, L), BF16),
        compiler_params=_params(),
    )(o_t, proj_t, proj_t, proj_t, proj_t, proj_t, proj_t, proj_t, cw_b, ga_b, gcn_b)


def _outproj(mix_t, w_out, x, meta_full, fng, target, L):
    nj = L // TT
    rp = NM
    n_bwd = 8
    cb = D // 4
    assert P0 % rp == 0 and TB % rp == 0 and (TT // rp) % n_bwd == 0

    def body(mix_ref, mixp_ref, w_ref, xa_ref, xb_ref, xc_ref, meta_ref, g_ref, ta_ref, tb_ref, tc_ref,
             dout_ref, dmix_ref, dwb_ref, loss_ref, dg_ref, dw_ref, o_scr, db_new, db_old, sq_acc, dg_acc):
        t = pl.program_id(0)

        def loss_rows(c):
            blk = c // (TB // rp)
            rows, out_rows = pl.ds((c % (TB // rp)) * rp, rp), pl.ds(c * rp, rp)
            h = (xa_ref, xb_ref, xc_ref)[blk][rows, :]
            if blk == 0:
                first = meta_ref[...] if c == P0 // rp else jnp.zeros((rp, D), F32)
                h = jnp.where(t == 0, first, h)
            o = o_scr[out_rows, :] + h
            r = lax.rsqrt(jnp.mean(o * o, axis=-1, keepdims=True) + EPS)
            orn = o * r
            g = g_ref[...]
            diff = orn * g - (ta_ref, tb_ref, tc_ref)[blk][rows, :]
            if blk == 0:
                diff = diff * jnp.where(t > 0, 1.0, 0.0)
            gy = diff * (g * (1.0 / D))
            dout = r * (gy - orn * jnp.mean(gy * orn, axis=-1, keepdims=True))
            dout_ref[out_rows, :] = dout
            db_new[out_rows, :] = dout.astype(BF16)
            sq, go = diff * diff, diff * orn
            sq_acc[...] += sq[:8] + sq[8:]
            dg_acc[...] += go[:8] + go[8:]

        def backward_cols(n):
            if n < 4:
                cols = slice(n * cb, (n + 1) * cb)
                dmix_ref[cols, :] = _dot(db_old[...], w_ref[cols, :], NT_DIMS).T.astype(BF16)
            else:
                cols = slice((n - 4) * cb, (n - 3) * cb)
                dw_ref[:, cols] += _dot(mixp_ref[...], db_old[:, cols])

        def step(forward, backward):
            if forward:
                o_scr[...] = _dot(mix_ref[...], w_ref[...], TN_DIMS)
            per = TT // rp // n_bwd
            for k in range(n_bwd):
                if forward:
                    for c in range(per * k, per * (k + 1)):
                        loss_rows(c)
                if backward:
                    backward_cols(k)
            if forward:
                db_old[...] = db_new[...]

        @pl.when(t == 0)
        def _():
            dw_ref[...] = jnp.zeros_like(dw_ref)
            sq_acc[...] = jnp.zeros_like(sq_acc)
            dg_acc[...] = jnp.zeros_like(dg_acc)
            step(True, False)

        @pl.when((t > 0) & (t < nj))
        def _():
            step(True, True)

        @pl.when(t == nj)
        def _():
            step(False, True)
            dwb_ref[...] = dw_ref[...].astype(BF16)
            loss_ref[...] = jnp.sum(sq_acc[...], keepdims=True) * (0.5 / D)
            dg_ref[...] = jnp.sum(dg_acc[...], axis=0, keepdims=True) * (1.0 / D)

    cur = lambda t: jnp.minimum(t, nj - 1)
    prev = lambda t: jnp.maximum(t - 1, 0)
    return pl.pallas_call(
        body, name="outproj", grid=(nj + 1,),
        in_specs=[pl.BlockSpec((D, TT), lambda t: (0, cur(t))), pl.BlockSpec((D, TT), lambda t: (0, prev(t))),
                  _full_spec((D, D))] + _x_specs3(cur) + [_full_spec((NM, D)), _full_spec((1, D))] + _x_specs3(cur),
        out_specs=[pl.BlockSpec((TT, D), lambda t: (cur(t), 0)), pl.BlockSpec((D, TT), lambda t: (0, prev(t))),
                   _full_spec((D, D)), _full_spec((1, 1)), _full_spec((1, D))],
        out_shape=[jax.ShapeDtypeStruct((L, D), F32), jax.ShapeDtypeStruct((D, L), BF16),
                   jax.ShapeDtypeStruct((D, D), BF16), jax.ShapeDtypeStruct((1, 1), F32),
                   jax.ShapeDtypeStruct((1, D), F32)],
        scratch_shapes=[pltpu.VMEM((D, D), F32), pltpu.VMEM((TT, D), F32), pltpu.VMEM((TT, D), BF16),
                        pltpu.VMEM((TT, D), BF16), pltpu.VMEM((8, D), F32), pltpu.VMEM((8, D), F32)],
        compiler_params=_params(),
    )(mix_t, mix_t, w_out, x, x, x, meta_full, fng, target, target, target)


def _gate_bwd(dmix_t, o_t, proj_t, cw_b, ga_b, gcn_b, L, after):
    nj = L // TT

    def body(dmix_ref, o_ref, za_ref, gb_ref, gc_ref, xc_ref, zc_ref, gcp_ref, xcp_ref, cw_ref, ga_ref, gcn_ref, _,
             do_ref, dd_ref, dg5_ref, dga_ref, dgc_ref, dcw_ref, carry_ref):
        i = pl.program_id(0)
        j = nj - 1 - i

        @pl.when(i == 0)
        def _():
            carry_ref[...] = jnp.zeros_like(carry_ref)
            dga_ref[...] = jnp.zeros_like(dga_ref)
            dgc_ref[...] = jnp.zeros_like(dgc_ref)
            dcw_ref[...] = jnp.zeros_like(dcw_ref)

        def group(h, c):
            r0 = pl.multiple_of(h * DH, DH)
            rows = pl.ds(r0, DH)
            sec = lambda s: pl.ds(s * DA + r0, DH)
            g = _gate_group(rows, o_ref, za_ref, gb_ref, gc_ref, xc_ref, zc_ref, gcp_ref, xcp_ref,
                            cw_ref, ga_ref, gcn_ref, j == 0)
            o, za, gb, gc, xc, zc, sa, sc = (g[n] for n in ("o", "za", "gb", "gc", "xc", "zc", "sa", "sc"))
            dya = dmix_ref[rows, :].astype(F32)
            dyc = dmix_ref[pl.ds(DA + r0, DH), :].astype(F32)

            dn = dya * (za * sa)
            dg5_ref[sec(0), :] = (dya * (g["oa"] * g["g_a"]) * (sa * (1.0 + za * (1.0 - sa)))).astype(BF16)
            dga_ref[rows, :] += _lane_tiles_sum(dn * g["oa"])
            dng = dn * g["g_a"]
            mean_a = jnp.mean(dng * g["oa"], axis=0, keepdims=True)
            do = (dng - g["oa"] * mean_a) * g["ra"]
            do_ref[rows, :] = do.astype(BF16)
            dd_ref[h] = jnp.sum(do * o, axis=0, keepdims=True)

            dnc = dyc * (zc * sc)
            dg5_ref[sec(4), :] = (dyc * (g["ec"] * g["g_c"]) * (sc * (1.0 + zc * (1.0 - sc)))).astype(BF16)
            dgc_ref[rows, :] += _lane_tiles_sum(dnc * g["ec"])
            dncg = dnc * g["g_c"]
            mean_c = jnp.mean(dncg * g["ec"], axis=0, keepdims=True)
            de = (dncg - g["ec"] * mean_c) * g["rc"]
            dg5_ref[sec(1), :] = (de * g["cv"]).astype(BF16)
            dcv = de * gb
            full = jnp.concatenate([dcv, carry_ref[rows, :]], axis=1)
            d1 = pltpu.roll(full, TT + TB - 1, 1)[:, :TT]
            d2 = pltpu.roll(full, TT + TB - 2, 1)[:, :TT]
            carry_ref[rows, :] = dcv[:, :TB]
            da = g["w2"] * dcv + g["w1"] * d1 + g["w0"] * d2
            dg5_ref[sec(2), :] = (da * xc).astype(BF16)
            dg5_ref[sec(3), :] = (da * gc).astype(BF16)
            dcw_ref[0, rows, :] += _lane_tiles_sum(dcv * g["a2"])
            dcw_ref[1, rows, :] += _lane_tiles_sum(dcv * g["a1"])
            dcw_ref[2, rows, :] += _lane_tiles_sum(dcv * g["a"])
            return c

        lax.fori_loop(0, H, group, 0, unroll=2)

    rj = lambda i: nj - 1 - i
    return pl.pallas_call(
        body, name="gate_bwd", grid=(nj,),
        in_specs=[pl.BlockSpec((2 * DA, TT), lambda i: (0, rj(i)))] + _gate_specs(nj, True) + [_UNREAD],
        out_specs=[
            pl.BlockSpec((DA, TT), lambda i: (0, rj(i))),
            pl.BlockSpec((H, 1, TT), lambda i: (0, 0, rj(i))),
            pl.BlockSpec((5 * DA, TT), lambda i: (0, rj(i))),
            _full_spec((DA, TB)), _full_spec((DA, TB)), _full_spec((3, DA, TB)),
        ],
        out_shape=[
            jax.ShapeDtypeStruct((DA, L), BF16),
            jax.ShapeDtypeStruct((H, 1, L), F32),
            jax.ShapeDtypeStruct((5 * DA, L), BF16),
            jax.ShapeDtypeStruct((DA, TB), F32),
            jax.ShapeDtypeStruct((DA, TB), F32),
            jax.ShapeDtypeStruct((3, DA, TB), F32),
        ],
        scratch_shapes=[pltpu.VMEM((DA, TB), F32)],
        compiler_params=_params(),
    )(dmix_t, o_t, proj_t, proj_t, proj_t, proj_t, proj_t, proj_t, proj_t, cw_b, ga_b, gcn_b, after)


def _attn_bwd(proj_t, kaug, vtok, do_t, lse, dd, cq, L):
    nk = L // TT

    def body(q_ref, kaug_ref, vtok_ref, kt_ref, do_ref, lse_ref, dd_ref, cq_ref,
             dq_ref, dk_ref, dv_ref, dck_ref, dcq_ref, dq_acc, kt1_scr, s_scr, dp_scr, dv_scr, dk_scr):
        i = pl.program_id(0)
        rows = [slice(g * DH, (g + 1) * DH) for g in range(HG)]
        ones = jnp.ones((DF, TT), BF16)
        zpad = jnp.zeros((KA - DH - DF, TT), BF16)
        for g in range(HG):
            kt1_scr[g] = jnp.concatenate([kt_ref[rows[g], :], ones], axis=0)
        dv_scr[...] = jnp.zeros_like(dv_scr)
        dk_scr[...] = jnp.zeros_like(dk_scr)

        def q_rows(g, q_off):
            bias = cq_ref[g, :, pl.ds(q_off, TT)] - lse_ref[g, :, pl.ds(q_off, TT)]
            return jnp.concatenate([q_ref[rows[g], pl.ds(q_off, TT)], _bias_rows(bias)], axis=0)

        def scores(jq, masked):
            q_off = pl.multiple_of(jq * TT, TT)
            for g in range(HG):
                s = _dot(kaug_ref[g], jnp.concatenate([q_rows(g, q_off), zpad], axis=0))
                if masked:
                    s = jnp.where(_causal_mask(), s, NEG)
                s_scr[g] = s
                dp_scr[g] = _dot(vtok_ref[g], do_ref[rows[g], pl.ds(q_off, TT)])

        def grads(jq):
            q_off = pl.multiple_of(jq * TT, TT)
            for g in range(HG):
                p = jnp.exp2(s_scr[g])
                ds = (p * (dp_scr[g] - dd_ref[g, :, pl.ds(q_off, TT)])).astype(BF16)
                do1 = jnp.concatenate([do_ref[rows[g], pl.ds(q_off, TT)], jnp.zeros((KA - DH, TT), BF16)], axis=0)
                q1 = jnp.concatenate([q_rows(g, q_off), zpad], axis=0)
                dv_scr[g] += _dot(p.astype(BF16), do1, NT_DIMS)
                dk_scr[g] += _dot(ds, q1, NT_DIMS)
                dq_acc[g, :, pl.ds(q_off, TT)] += _dot(kt1_scr[g], ds)

        @pl.when(i == 0)
        def _():
            dq_acc[...] = jnp.zeros_like(dq_acc)

        scores(i, True)

        def step(jq, c):
            grads(jq)
            scores(jq + 1, False)
            return c

        lax.fori_loop(i, nk - 1, step, 0)
        grads(nk - 1)
        for g in range(HG):
            dv_ref[rows[g], :] = dv_scr[g].T[:DH, :].astype(BF16)
            dk_t = dk_scr[g].T
            dk_ref[rows[g], :] = (dk_t[:DH, :] * LN2).astype(BF16)
            dck_ref[g] = dk_t[DH:DH + 1, :]

        @pl.when(i == nk - 1)
        def _():
            for g in range(HG):
                dq_ref[rows[g], :] = (dq_acc[g, :DH, :] * (DH ** -0.5)).astype(BF16)
                dcq_ref[g] = dq_acc[g, DH:DH + 1, :]

    assert HG == H
    head = lambda i: (0, 0)
    row = lambda i: (0, 0, 0)
    return pl.pallas_call(
        body, name="attn_bwd", grid=(nk,),
        in_specs=[
            pl.BlockSpec((DA, L), head),
            pl.BlockSpec((H, TT, KA), lambda i: (0, i, 0)),
            pl.BlockSpec((H, TT, DH), lambda i: (0, i, 0)),
            pl.BlockSpec((DA, TT), lambda i: (1, i)),
            pl.BlockSpec((DA, L), head),
            pl.BlockSpec((H, 1, L), row), pl.BlockSpec((H, 1, L), row), pl.BlockSpec((H, 1, L), row),
        ],
        out_specs=[
            pl.BlockSpec((DA, L), head),
            pl.BlockSpec((DA, TT), lambda i: (0, i)),
            pl.BlockSpec((DA, TT), lambda i: (0, i)),
            pl.BlockSpec((H, 1, TT), lambda i: (0, 0, i)),
            pl.BlockSpec((H, 1, L), row),
        ],
        out_shape=[jax.ShapeDtypeStruct((DA, L), BF16), jax.ShapeDtypeStruct((DA, L), BF16),
                   jax.ShapeDtypeStruct((DA, L), BF16), jax.ShapeDtypeStruct((H, 1, L), F32),
                   jax.ShapeDtypeStruct((H, 1, L), F32)],
        scratch_shapes=[
            pltpu.VMEM((HG, DH + DF, L), F32),
            pltpu.VMEM((HG, DH + DF, TT), BF16),
            pltpu.VMEM((HG, TT, TT), F32), pltpu.VMEM((HG, TT, TT), F32),
            pltpu.VMEM((HG, TT, KA), F32), pltpu.VMEM((HG, TT, KA), F32)],
        compiler_params=_params(),
    )(proj_t, kaug, vtok, proj_t, do_t, lse, dd, cq)


def _fgate_bwd(dcq, dck, sg, L):
    def body(dcq_ref, dck_ref, sg_ref, df_ref, db_ref):
        dc = jnp.concatenate([dcq_ref[h] - dck_ref[h] for h in range(H)], axis=0)
        idx = lax.broadcasted_iota(jnp.int32, (H, L), 1)
        r = dc
        s = 1
        while s < L:
            r = r + jnp.where(idx + s < L, pltpu.roll(r, L - s, 1), 0.0)
            s *= 2
        df = r * sg_ref[...]
        db_ref[...] = jnp.broadcast_to(jnp.sum(df, axis=1, keepdims=True), (H, TB))
        df_ref[...] = jnp.concatenate([df, jnp.zeros((DF - H, L), F32)], axis=0).astype(BF16)

    return pl.pallas_call(
        body, name="fgate_bwd",
        out_shape=[jax.ShapeDtypeStruct((DF, L), BF16), jax.ShapeDtypeStruct((H, TB), F32)],
        compiler_params=pltpu.CompilerParams(vmem_limit_bytes=VMEM_LIMIT),
    )(dcq, dck, sg)


def _inproj_bwd_x(w, dq_t, dk_t, dv_t, dg5_t, df_t, dout, x, meta_full, norm_g, L, after):
    nj = L // TT
    seq = x.shape[0]

    def body(w_ref, dq_ref, dk_ref, dv_ref, dg5_ref, df_ref, dout_ref, xa_ref, xb_ref, xc_ref, meta_ref, g_ref, _,
             gx_ref, dmeta_ref, dg_ref, dh_scr, sems):
        j = pl.program_id(0)
        slot = j % 2

        def copy_out(step, slot_):
            first = pltpu.make_async_copy(dh_scr.at[slot_, pl.ds(TB, TT - TB)], gx_ref.at[pl.ds(0, TT - TB)],
                                          sems.at[slot_])
            later = pltpu.make_async_copy(dh_scr.at[slot_], gx_ref.at[pl.ds(step * TT - TB, TT)], sems.at[slot_])
            return first, later

        @pl.when(j == 0)
        def _():
            dg_ref[...] = jnp.zeros_like(dg_ref)

        du = _dot(dq_ref[...], w_ref[0:DA, :], TN_DIMS)
        du += _dot(dk_ref[...], w_ref[DA:2 * DA, :], TN_DIMS)
        du += _dot(dv_ref[...], w_ref[2 * DA:3 * DA, :], TN_DIMS)
        du += _dot(dg5_ref[...], w_ref[3 * DA:NSEC * DA, :], TN_DIMS)
        du += _dot(df_ref[...], w_ref[NSEC * DA:DPROJ, :], TN_DIMS)
        hb = _h_tile(j, xa_ref, xb_ref, xc_ref, meta_ref)
        r = lax.rsqrt(jnp.mean(hb * hb, axis=-1, keepdims=True) + EPS)
        hn = hb * r
        dg_ref[...] += jnp.sum(du * hn, axis=0, keepdims=True)
        gu = du * g_ref[...]
        dh = dout_ref[...] + r * gu - hn * (r * jnp.mean(gu * hn, axis=-1, keepdims=True))

        dh_scr[slot] = dh

        @pl.when(j == 0)
        def _():
            dmeta_ref[...] = dh[P0:TB, :]
            copy_out(0, 0)[0].start()

        @pl.when(j >= 1)
        def _():
            copy_out(j, slot)[1].start()

        @pl.when(j == 1)
        def _():
            copy_out(0, 0)[0].wait()

        @pl.when(j >= 2)
        def _():
            copy_out(j - 1, 1 - slot)[1].wait()

        @pl.when(j == nj - 1)
        def _():
            copy_out(j, slot)[0 if nj == 1 else 1].wait()

    blk = lambda rows: pl.BlockSpec((rows, TT), lambda j: (0, j))
    return pl.pallas_call(
        body, name="inproj_bwd_x", grid=(nj,),
        in_specs=[_full_spec((DPROJ, D)), blk(DA), blk(DA), blk(DA), blk(5 * DA), blk(DF),
                  pl.BlockSpec((TT, D), lambda j: (j, 0))] + _x_specs3()
                 + [_full_spec((NM, D)), _full_spec((1, D)), _UNREAD],
        out_specs=[pl.BlockSpec(memory_space=pl.ANY), _full_spec((NM, D)), _full_spec((1, D))],
        out_shape=[jax.ShapeDtypeStruct((seq, D), F32), jax.ShapeDtypeStruct((NM, D), F32),
                   jax.ShapeDtypeStruct((1, D), F32)],
        scratch_shapes=[pltpu.VMEM((2, TT, D), F32), pltpu.SemaphoreType.DMA((2,))],
        compiler_params=_params(),
    )(w, dq_t, dk_t, dv_t, dg5_t, df_t, dout, x, x, x, meta_full, norm_g, after)


def _inproj_bwd_w(u, dq_t, dk_t, dv_t, dg5_t, df_t, L):
    def body(u_ref, dq_hbm, dk_hbm, dv_hbm, dg5_ref, df_ref, dw_ref, dwf_ref, qkv_scr, sems):
        s = pl.program_id(0)
        u_all = u_ref[...]
        fetch = [pltpu.make_async_copy(src, qkv_scr.at[k], sems.at[k])
                 for k, src in enumerate((dq_hbm, dk_hbm, dv_hbm))]

        @pl.when(s == 0)
        def _():
            for cp in fetch:
                cp.start()

        @pl.when(s < 5)
        def _():
            dw_ref[...] = _dot(dg5_ref[...], u_all)

        for k in range(3):
            @pl.when(s == 5 + k)
            def _(k=k):
                fetch[k].wait()
                dw_ref[...] = _dot(qkv_scr[k], u_all)

        @pl.when(s == NSEC - 1)
        def _():
            dwf_ref[...] = _dot(df_ref[...], u_all)

    once = lambda shape: pl.BlockSpec(shape, lambda s: (0, 0), pipeline_mode=pl.Buffered(1))
    any_spec = pl.BlockSpec(memory_space=pl.ANY)
    return pl.pallas_call(
        body, name="inproj_bwd_w", grid=(NSEC,),
        in_specs=[
            once((L, D)), any_spec, any_spec, any_spec,
            pl.BlockSpec((DA, L), lambda s: (jnp.minimum(s, 4), 0)),
            once((DF, L)),
        ],
        out_specs=[pl.BlockSpec((DA, D), lambda s: (jnp.where(s < 5, s + 3, s - 5), 0)), _full_spec((DF, D))],
        out_shape=[jax.ShapeDtypeStruct((NSEC * DA, D), F32), jax.ShapeDtypeStruct((DF, D), F32)],
        scratch_shapes=[pltpu.VMEM((3, DA, L), BF16), pltpu.SemaphoreType.DMA((3,))],
        compiler_params=_params(),
    )(u, dq_t, dk_t, dv_t, dg5_t, df_t)


def _adamw(w, g, m, v):
    m = ADAM_B1 * m + (1.0 - ADAM_B1) * g
    v = ADAM_B2 * v + (1.0 - ADAM_B2) * (g * g)
    m_hat = m / (1.0 - ADAM_B1 ** ADAM_STEP)
    v_hat = v / (1.0 - ADAM_B2 ** ADAM_STEP)
    delta = -ADAM_LR * (m_hat / (jnp.sqrt(v_hat) + ADAM_EPS) + ADAM_WD * w)
    return delta, m, v


def _adamw_big(own_in, land_in, own_out, land_out, w_in_t, m_in_t, v_in_t, w_out, m_out, v_out):
    cb = CB
    e_sh = D // NDEV
    in_shape = jax.ShapeDtypeStruct(w_in_t.shape, F32)
    out_shape = jax.ShapeDtypeStruct(w_out.shape, F32)

    def total(own_ref, land_ref, rows, chips):
        g = _pick_slab(0, own_ref, land_ref, rows, chips=chips).astype(F32)
        for j in range(1, own_ref.shape[0]):
            g = g + _pick_slab(j, own_ref, land_ref, rows, chips=chips).astype(F32)
        return g

    def body(oi_ref, li_ref, oo_ref, lo_ref, wi_ref, mi_ref, vi_ref, wo_ref, mo_ref, vo_ref,
             gi, di, mi, vi, go, do, mo, vo):
        g = total(oi_ref, li_ref, slice(0, WSHP), True)[:WSH]
        d, mn, vn = _adamw(wi_ref[...], g, mi_ref[...], vi_ref[...])
        gi[...], di[...], mi[...], vi[...] = g, d, mn, vn
        g = total(oo_ref, lo_ref, slice(0, e_sh), False)
        d, mn, vn = _adamw(wo_ref[0], g, mo_ref[0], vo_ref[0])
        go[0], do[0], mo[0], vo[0] = g, d, mn, vn

    slab = lambda n, rows: pl.BlockSpec((n, rows, cb), lambda i: (0, 0, i))
    ispec = pl.BlockSpec((WSH, cb), lambda i: (0, i))
    ospec = pl.BlockSpec((1, e_sh, cb), lambda i: (0, 0, i))
    return pl.pallas_call(
        body, name="adamw_big", grid=(D // cb,),
        in_specs=[slab(4, WSHP), slab(4, WSHP), slab(NDEV, e_sh), slab(NDEV, e_sh),
                  ispec, ispec, ispec, ospec, ospec, ospec],
        out_specs=[ispec] * 4 + [ospec] * 4, out_shape=[in_shape] * 4 + [out_shape] * 4,
        compiler_params=_params(),
    )(own_in, land_in, own_out, land_out, w_in_t, m_in_t, v_in_t, w_out, m_out, v_out)


F0 = 3 * DA


def _unshard_w_out(own, land):
    e_sh = D // NDEV

    def body(own_ref, land_ref, wo_ref):
        for j in range(NDEV):
            wo_ref[j * e_sh:(j + 1) * e_sh, :] = _pick_slab(j, own_ref, land_ref, slice(0, e_sh), per_peer=False)

    return pl.pallas_call(
        body, name="unshard_w_out", grid=(D // CB,),
        in_specs=[pl.BlockSpec((e_sh, CB), lambda i: (0, i)), pl.BlockSpec((NDEV, e_sh, CB), lambda i: (0, 0, i))],
        out_specs=pl.BlockSpec((D, CB), lambda i: (0, i)),
        out_shape=jax.ShapeDtypeStruct((D, D), BF16),
        compiler_params=_params(),
    )(own, land)


def _unshard_w_in(w_all, small_all, attn_gain, conv_gain):
    def body(w_ref, small_ref, ga_ref, gc_ref, wt_ref, meta_ref, cwb_ref, gab_ref, gcb_ref):
        i = pl.program_id(0)
        for k in range(CB // TB):
            meta_ref[:, k * TB:(k + 1) * TB] = small_ref[(CB // TB) * i + k, 0:NM, :]

        @pl.when(i == 0)
        def _():
            per_row = lambda line: jnp.broadcast_to(line, (TB, DA)).T
            cw = jnp.concatenate([small_ref[j, NM:NM + 3, 0:DH] for j in range(NDEV)], axis=1)
            for k in range(3):
                cwb_ref[k] = per_row(cw[k:k + 1, :])
            gab_ref[...] = per_row(ga_ref[...])
            gcb_ref[...] = per_row(gc_ref[...])

        def ref_rows(lo, hi):
            pieces, r = [], lo
            while r < hi:
                sh, off = divmod(r, WSH)
                n = min(hi - r, WSH - off)
                pieces.append(w_ref[sh, off:off + n, :])
                r += n
            return pieces

        for s in range(NSEC):
            lo = s * DA if s < 3 else s * DA + H
            wt_ref[s * DA:(s + 1) * DA, :] = jnp.concatenate(ref_rows(lo, lo + DA), axis=0)
        wt_ref[NSEC * DA:DPROJ, :] = jnp.concatenate(
            ref_rows(F0, F0 + H) + [jnp.zeros((DF - H, CB), BF16)], axis=0)

    return pl.pallas_call(
        body, name="unshard_w_in", grid=(D // CB,),
        in_specs=[pl.BlockSpec((NDEV, WSHP, CB), lambda i: (0, 0, i)), _full_spec(small_all.shape),
                  _full_spec((1, DA)), _full_spec((1, DA))],
        out_specs=[pl.BlockSpec((DPROJ, CB), lambda i: (0, i)), pl.BlockSpec((NM, CB), lambda i: (0, i)),
                   _full_spec((3, DA, TB)), _full_spec((DA, TB)), _full_spec((DA, TB))],
        out_shape=[jax.ShapeDtypeStruct((DPROJ, D), BF16), jax.ShapeDtypeStruct((NM, D), F32),
                   jax.ShapeDtypeStruct((3, DA, TB), F32), jax.ShapeDtypeStruct((DA, TB), F32),
                   jax.ShapeDtypeStruct((DA, TB), F32)],
        compiler_params=_params(),
    )(w_all, small_all, attn_gain, conv_gain)


def _shard_w_in_grads(dw_main, dw_f):
    def body(dm_ref, df_ref, p_ref):
        mc = lax.axis_index("c")

        def ref_rows(lo, hi):
            pieces, r = [], lo
            while r < hi:
                if r < F0:
                    n = min(hi, F0) - r
                    pieces.append(dm_ref[r:r + n, :])
                elif r < F0 + H:
                    n = min(hi, F0 + H) - r
                    pieces.append(df_ref[r - F0:r - F0 + n, :])
                else:
                    n = hi - r
                    pieces.append(dm_ref[r - H:r - H + n, :])
                r += n
            return pieces

        for i in range(NDEV):
            rows = jnp.concatenate(ref_rows(i * WSH, (i + 1) * WSH) + [jnp.zeros((WSHP - WSH, CB), F32)], axis=0)
            p_ref[i // 2 + jnp.where(mc == i % 2, 0, 4)] = rows.astype(BF16)

    col = lambda rows: pl.BlockSpec((rows, CB), lambda i: (0, i))
    return pl.pallas_call(
        body, name="shard_w_in_grads", grid=(D // CB,),
        in_specs=[col(NSEC * DA), col(DF)],
        out_specs=pl.BlockSpec((NDEV, WSHP, CB), lambda i: (0, 0, i)),
        out_shape=jax.ShapeDtypeStruct((NDEV, WSHP, D), BF16),
        compiler_params=_params(),
    )(dw_main, dw_f)


SMALL = ("norm_g", "final_norm_g", "attn_norm_g", "conv_norm_g", "b_f", "meta", "conv_w")


def _as_rows(x):
    return jnp.concatenate([x[:, r * TB:(r + 1) * TB] for r in range(x.shape[1] // TB)], axis=0)


def _as_line(rows):
    return jnp.concatenate([rows[r:r + 1, :] for r in range(rows.shape[0])], axis=1)


def _pad_rows(x, n=8):
    return jnp.concatenate([x, jnp.zeros((n - x.shape[0], x.shape[1]), F32)], axis=0)


def _tile_rows(a, rows, lanes=TB):
    a = a.reshape(rows, lanes)
    return jnp.pad(a, ((0, -rows % 8), (0, TB - lanes)))


def _pack_small_grads(dg_norm, dg_final, dga_p, dgc_p, dcw_p, db_b, dmeta, loss):
    def body(dgn_ref, dgf_ref, dga_ref, dgc_ref, dcw_ref, db_ref, dmeta_ref, loss_ref, out_ref):
        def lane_sums(p):
            return jnp.sum(p.T, axis=0, keepdims=True)

        lane = lax.broadcasted_iota(jnp.int32, (1, TB), 1)
        b_row = jnp.where(lane == H, loss_ref[...], 0.0)
        for h in range(H):
            b_row = b_row + jnp.where(lane == h, db_ref[h:h + 1, :], 0.0)
        common = jnp.concatenate([
            _as_rows(dgn_ref[...]), _as_rows(dgf_ref[...]), _pad_rows(_as_rows(lane_sums(dga_ref[...]))),
            _pad_rows(_as_rows(lane_sums(dgc_ref[...]))), _pad_rows(b_row)], axis=0)
        dcw = [lane_sums(dcw_ref[k]) for k in range(3)]
        for j in range(NDEV):
            cw = jnp.concatenate(
                [jnp.concatenate([r[:, j * DH:(j + 1) * DH], jnp.zeros((1, TB - DH), F32)], axis=1) for r in dcw],
                axis=0)
            out_ref[j] = jnp.concatenate([common, dmeta_ref[:, j * TB:(j + 1) * TB], _pad_rows(cw)], axis=0)

    return pl.pallas_call(
        body, name="pack_small_grads", out_shape=jax.ShapeDtypeStruct((NDEV, SROWS, TB), F32),
    )(dg_norm, dg_final, dga_p, dgc_p, dcw_p, db_b, dmeta, loss)


def _adamw_small(own, land, params):
    flat = [a for n in SMALL for a in params[n]]

    def body(*refs):
        own_ref, land_ref = refs[:2]
        ins = refs[2:2 + 3 * len(SMALL)]
        outs = refs[2 + 3 * len(SMALL):]
        g = _pick_slab(0, own_ref, land_ref, slice(0, SROWS))
        for j in range(1, NDEV):
            g = g + _pick_slab(j, own_ref, land_ref, slice(0, SROWS))
        grads = dict(
            norm_g=_as_line(g[0:8]), final_norm_g=_as_line(g[8:16]), attn_norm_g=_as_line(g[16:20]),
            conv_norm_g=_as_line(g[24:28]), b_f=g[32:33, :H], meta=g[40:56], conv_w=g[56:59, :DH][None])
        for i, n in enumerate(SMALL):
            w_ref, m_ref, v_ref = ins[3 * i:3 * i + 3]
            d, mn, vn = _adamw(w_ref[...], grads[n], m_ref[...], v_ref[...])
            for o_ref, val in zip(outs[4 * i:4 * i + 4], (grads[n], d, mn, vn)):
                o_ref[...] = val
        outs[-1][...] = g[32:33, H:H + 1]

    shapes = [jax.ShapeDtypeStruct(params[n][0].shape, F32) for n in SMALL for _ in range(4)]
    res = pl.pallas_call(
        body, name="adamw_small", out_shape=shapes + [jax.ShapeDtypeStruct((1, 1), F32)],
    )(own, land, *flat)
    return {n: res[4 * i:4 * i + 4] for i, n in enumerate(SMALL)}, res[-1]


def kernel(x, meta, norm_g, w_in, b_f, conv_w, attn_norm_g, conv_norm_g, w_out, final_norm_g, loss_target, m_meta, m_norm_g, m_w_in, m_b_f, m_conv_w, m_attn_norm_g, m_conv_norm_g, m_w_out, m_final_norm_g, v_meta, v_norm_g, v_w_in, v_b_f, v_conv_w, v_attn_norm_g, v_conv_norm_g, v_w_out, v_final_norm_g):
    seq = x.shape[1]
    L = seq + TB
    assert x.shape == (1, seq, D) and L % TT == 0 and w_in.shape == (1, D, WSH)
    x2 = x[0]
    tgt = loss_target[0]

    w_in_slab = jnp.pad(w_in[0].T, ((0, WSHP - WSH), (0, 0))).astype(BF16)
    w_out_slab = w_out[0].astype(BF16)
    meta_slab = jnp.concatenate([meta, _tile_rows(conv_w[0], 3, DH)], axis=0)
    wout_flight = _split_start(w_out_slab, "gather_w_out_start", per_peer=False)
    w_all, small_all = _all_gather([w_in_slab, meta_slab], "gather_w_in")

    w_t, meta_full, cw_b, ga_b, gcn_b = _unshard_w_in(w_all, small_all, attn_norm_g, conv_norm_g)

    u, proj_t, f_t, ktok, vtok = _inproj_fwd(x2, meta_full, norm_g, w_t, L, after=wout_flight[4])
    cq, kaug, sg = _fgate_fwd(f_t, b_f.reshape(H, 1), ktok, L)
    o_t, lse = _attn_fwd(proj_t, kaug, cq, L)
    mix_t = _gate_fwd(o_t, proj_t, cw_b, ga_b, gcn_b, L)

    w_out_own, w_out_land = _split_wait(wout_flight, mix_t, "gather_w_out_wait", per_peer=False)
    w_out_full = _unshard_w_out(w_out_own, w_out_land)
    dout, dmix_t, dw_out, loss_part, dg_final = _outproj(
        mix_t, w_out_full, x2, meta_full, final_norm_g.reshape(1, D), tgt, L)
    dwo_flight = _split_start(dw_out.reshape(NDEV, D // NDEV, D), "exchange_dw_out_start", per_peer=True)
    do_t, dd, dg5_t, dga_p, dgc_p, dcw_p = _gate_bwd(dmix_t, o_t, proj_t, cw_b, ga_b, gcn_b, L, after=dwo_flight[4])
    dq_t, dk_t, dv_t, dck, dcq = _attn_bwd(proj_t, kaug, vtok, do_t, lse, dd, cq, L)
    df_t, db_f = _fgate_bwd(dcq, dck, sg, L)
    dw_main, dw_f = _inproj_bwd_w(u, dq_t, dk_t, dv_t, dg5_t, df_t, L)
    dwi_parts = _shard_w_in_grads(dw_main, dw_f)
    dwi_chip = _pair_sum(dwi_parts, _pair_exchange(dwi_parts, "exchange_dw_in_pair"))
    dwi_flight = _split_start(dwi_chip, "exchange_dw_in_start", per_peer=True, chips=True)
    grad_x, dmeta, dg_norm = _inproj_bwd_x(
        w_t, dq_t, dk_t, dv_t, dg5_t, df_t, dout, x2, meta_full, norm_g, L, after=dwi_flight[4])
    small_parts = _pack_small_grads(dg_norm, dg_final, dga_p, dgc_p, dcw_p, db_f, dmeta, loss_part)
    small_flight = _split_start(small_parts, "exchange_small_start", per_peer=True)
    dwo_own, dwo_land = _split_wait(dwo_flight, small_flight[4], "exchange_dw_out_wait", per_peer=True)
    dwi_own, dwi_land = _split_wait(dwi_flight, dwo_land, "exchange_dw_in_wait", per_peer=True, chips=True)

    big_out = _adamw_big(dwi_own, dwi_land, dwo_own, dwo_land,
                         w_in[0].T, m_w_in[0].T, v_w_in[0].T, w_out, m_w_out, v_w_out)
    g_w_in, d_w_in, nm_w_in, nv_w_in = [a.T[None] for a in big_out[:4]]
    g_w_out, d_w_out, nm_w_out, nv_w_out = big_out[4:]
    sm_own, sm_land = _split_wait(small_flight, big_out[4], "exchange_small_wait", per_peer=True)
    line = lambda a: a.reshape(1, D)
    small, loss = _adamw_small(sm_own, sm_land, dict(
        norm_g=(norm_g, m_norm_g, v_norm_g),
        final_norm_g=(line(final_norm_g), line(m_final_norm_g), line(v_final_norm_g)),
        attn_norm_g=(attn_norm_g, m_attn_norm_g, v_attn_norm_g),
        conv_norm_g=(conv_norm_g, m_conv_norm_g, v_conv_norm_g),
        b_f=(b_f, m_b_f, v_b_f), meta=(meta, m_meta, v_meta), conv_w=(conv_w, m_conv_w, v_conv_w)))
    small["final_norm_g"] = [a.reshape(D) for a in small["final_norm_g"]]
    order = ("meta", "norm_g", "w_in", "b_f", "conv_w", "attn_norm_g", "conv_norm_g", "w_out", "final_norm_g")
    groups = []
    for k, (wi, wo) in enumerate(((g_w_in, g_w_out), (d_w_in, d_w_out), (nm_w_in, nm_w_out), (nv_w_in, nv_w_out))):
        d = dict({n: small[n][k] for n in SMALL}, w_in=wi, w_out=wo)
        groups.append([d[n] for n in order])
    return (loss[0, 0], grad_x[None], *groups[0], *groups[1], *groups[2], *groups[3])
```

```python
import jax
import jax.numpy as jnp
from jax import lax
from jax.experimental import pallas as pl
from jax.experimental.pallas import tpu as pltpu

F32 = jnp.float32
BF16 = jnp.bfloat16

D = 1024
DA = 512
H = 8
DH = 64
NM = 16
TB = 128
P0 = TB - NM
TT = 3 * TB
HG = 8
NDEV = 8
NSEC = 8
DF = 16
DPROJ = NSEC * DA + DF
WSH = 513
WSHP = 528
WROWS = WSHP + D // NDEV
SROWS = 64
EPS = 1e-6
NEG = -1e30
LOG2E = 1.4426950408889634
LN2 = 0.6931471805599453
QSCALE = DH ** -0.5 * LOG2E
KA = 128
CB = 256
VMEM_LIMIT = 56 * 1024 * 1024

ADAM_LR = 0.001
ADAM_B1 = 0.9
ADAM_B2 = 0.999
ADAM_EPS = 1e-08
ADAM_WD = 0.01
ADAM_STEP = 10

NT_DIMS = (((1,), (1,)), ((), ()))
TN_DIMS = (((0,), (0,)), ((), ()))
MESH = pl.DeviceIdType.MESH


def _params(n_axes=1, vmem=VMEM_LIMIT):
    return pltpu.CompilerParams(dimension_semantics=("arbitrary",) * n_axes, vmem_limit_bytes=vmem)


def _dot(a, b, dims=None):
    if dims is None:
        return jnp.dot(a, b, preferred_element_type=F32)
    return lax.dot_general(a, b, dims, preferred_element_type=F32)


def _my_place():
    return lax.axis_index("x"), lax.axis_index("y"), lax.axis_index("c")


def _all_gather(xs, name):
    n = len(xs)

    def body(*refs):
        x_refs, out_refs = refs[:n], refs[n:2 * n]
        send_sems, recv_sems, local_sems = refs[2 * n:]
        mx, my, mc = _my_place()

        def across(px, py, pc, axis_a):
            flip_x = pc if axis_a else 1 - pc
            return (px + flip_x) % 2, (py + 1 - flip_x) % 2, pc

        def idx(p):
            return 4 * p[0] + 2 * p[1] + p[2]

        me, sib = (mx, my, mc), (mx, my, 1 - mc)
        a_nbr, b_nbr = across(*me, True), across(*me, False)
        diag = across(*b_nbr, True)
        sib_a, sib_b = across(*sib, True), across(*sib, False)
        sib_diag = across(*sib_b, True)

        waits = []
        for t in range(n):
            out_ref = out_refs[t]

            def copy(k, block, to, src=None, out_ref=out_ref, t=t):
                return pltpu.make_async_remote_copy(
                    src_ref=out_ref.at[idx(block)] if src is None else src, dst_ref=out_ref.at[idx(block)],
                    send_sem=send_sems.at[7 * t + k], recv_sem=recv_sems.at[7 * t + k],
                    device_id=to, device_id_type=MESH)

            mine = pltpu.make_async_copy(x_refs[t], out_ref.at[idx(me)], local_sems.at[t])
            mine.start()
            started = [copy(0, me, sib, src=x_refs[t]), copy(1, me, a_nbr, src=x_refs[t]),
                       copy(2, me, b_nbr, src=x_refs[t])]
            for cp in started:
                cp.start()
            waits.append((copy, mine, started))
        relays = ((1, a_nbr, ((3, b_nbr), (4, sib))), (2, b_nbr, ((5, sib),)), (3, diag, ((6, sib),)))
        for landed, block, onward in relays:
            for copy, _, started in waits:
                copy(landed, block, me).wait_recv()
                for k, to in onward:
                    started.append(copy(k, block, to))
                    started[-1].start()
        for copy, mine, started in waits:
            for k, block in ((0, sib), (4, sib_a), (5, sib_b), (6, sib_diag)):
                copy(k, block, me).wait_recv()
            for cp in started:
                cp.wait_send()
            mine.wait()

    any_spec = pl.BlockSpec(memory_space=pl.ANY)
    return pl.pallas_call(
        body, name=name,
        out_shape=[jax.ShapeDtypeStruct((NDEV,) + x.shape, x.dtype) for x in xs],
        in_specs=[any_spec] * n, out_specs=[any_spec] * n,
        scratch_shapes=[pltpu.SemaphoreType.DMA((7 * n,)), pltpu.SemaphoreType.DMA((7 * n,)),
                        pltpu.SemaphoreType.DMA((n,))],
    )(*xs)


_HBM = pl.BlockSpec(memory_space=pltpu.HBM)
_UNREAD = pl.BlockSpec(memory_space=pl.ANY)
_SEM = pl.BlockSpec(memory_space=pltpu.SEMAPHORE)
_EFFECT = pltpu.SideEffectType.DATAFLOW_SIDE_EFFECTING


def _peer_of(m, place):
    mx, my, mc = place
    return ((1 - mx) if m & 4 else mx, (1 - my) if m & 2 else my, (1 - mc) if m & 1 else mc)


def _party(chips):
    if chips:
        return (lambda p: 2 * p[0] + p[1]), (2, 4, 6)
    return (lambda p: 4 * p[0] + 2 * p[1] + p[2]), tuple(range(1, NDEV))


def _split_copies(src_ref, land_ref, send_sems, recv_sems, per_peer, incoming, chips):
    place = _my_place()
    slot, masks = _party(chips)
    me = slot(place)
    out = []
    for k, m in enumerate(masks):
        there = _peer_of(m, place)
        peer = slot(there)
        src = (src_ref.at[me] if incoming else src_ref.at[peer]) if per_peer else src_ref
        out.append(pltpu.make_async_remote_copy(
            src_ref=src, dst_ref=land_ref.at[peer if incoming else me],
            send_sem=send_sems.at[k], recv_sem=recv_sems.at[k], device_id=there, device_id_type=MESH))
    return out


def _split_start(src, name, per_peer, chips=False):
    slab = src.shape[1:] if per_peer else src.shape
    n = len(_party(chips)[1])

    def body(src_ref, land_ref, send_sems, recv_sems, src_thru, land_thru, token):
        for cp in _split_copies(src_ref, land_ref, send_sems, recv_sems, per_peer, False, chips):
            cp.start()
        token[...] = jnp.zeros_like(token)

    return pl.pallas_call(
        body, name=name,
        out_shape=(pltpu.SemaphoreType.DMA((n,)), pltpu.SemaphoreType.DMA((n,)),
                   pltpu.HBM(src.shape, src.dtype), pltpu.HBM((n + 1,) + slab, src.dtype),
                   jax.ShapeDtypeStruct((8, TB), F32)),
        in_specs=(_HBM, _HBM), out_specs=(_SEM, _SEM, _HBM, _HBM, pl.BlockSpec(memory_space=pltpu.VMEM)),
        input_output_aliases={0: 2, 1: 3},
        compiler_params=pltpu.CompilerParams(has_side_effects=_EFFECT),
    )(pltpu.with_memory_space_constraint(src, pltpu.HBM),
      pltpu.with_memory_space_constraint(lax.empty((n + 1,) + slab, src.dtype), pltpu.HBM))


def _split_wait(handles, after, name, per_peer, chips=False):
    send_sems, recv_sems, src_thru, land_thru, _ = handles

    def body(src_ref, land_ref, send_sems, recv_sems, after_ref, src_out, land_out):
        for cp in _split_copies(src_ref, land_ref, send_sems, recv_sems, per_peer, False, chips):
            cp.wait_send()
        for cp in _split_copies(src_ref, land_ref, send_sems, recv_sems, per_peer, True, chips):
            cp.wait_recv()

    return pl.pallas_call(
        body, name=name,
        out_shape=(pltpu.HBM(src_thru.shape, src_thru.dtype), pltpu.HBM(land_thru.shape, land_thru.dtype)),
        in_specs=(_HBM, _HBM, _SEM, _SEM, pl.BlockSpec(memory_space=pl.ANY)), out_specs=(_HBM, _HBM),
        input_output_aliases={0: 0, 1: 1},
        compiler_params=pltpu.CompilerParams(has_side_effects=_EFFECT),
    )(src_thru, land_thru, send_sems, recv_sems, after)


def _pick_slab(j, own_ref, land_ref, rows, per_peer=True, chips=False):
    me = _party(chips)[0](_my_place())
    own = (lambda: own_ref[j, rows, :]) if per_peer else (lambda: own_ref[rows, :])
    return lax.cond(me == j, own, lambda: land_ref[j, rows, :])


def _pair_exchange(p, name):
    def body(p_ref, got_ref, send_sems, recv_sems):
        mx, my, mc = _my_place()
        copies = [pltpu.make_async_remote_copy(
            src_ref=p_ref.at[4 + q], dst_ref=got_ref.at[q], send_sem=send_sems.at[q],
            recv_sem=recv_sems.at[q], device_id=(mx, my, 1 - mc), device_id_type=MESH) for q in range(4)]
        for cp in copies:
            cp.start()
        for cp in copies:
            cp.wait_recv()
        for cp in copies:
            cp.wait_send()

    any_spec = pl.BlockSpec(memory_space=pl.ANY)
    return pl.pallas_call(
        body, name=name, out_shape=jax.ShapeDtypeStruct((4,) + p.shape[1:], p.dtype),
        in_specs=[any_spec], out_specs=any_spec,
        scratch_shapes=[pltpu.SemaphoreType.DMA((4,)), pltpu.SemaphoreType.DMA((4,))],
    )(p)


def _pair_sum(p, got):
    rows = p.shape[1]

    def body(p_ref, got_ref, out_ref):
        for q in range(4):
            out_ref[q] = (p_ref[q].astype(F32) + got_ref[q].astype(F32)).astype(BF16)

    blk = lambda n: pl.BlockSpec((n, rows, CB), lambda i: (0, 0, i))
    return pl.pallas_call(
        body, name="pair_sum", grid=(D // CB,), in_specs=[blk(4), blk(4)], out_specs=blk(4),
        out_shape=jax.ShapeDtypeStruct((4, rows, D), BF16), compiler_params=_params(),
    )(p, got)


def _h_block(t, x_ref, meta_ref):
    first = jnp.concatenate([jnp.zeros((P0, D), F32), meta_ref[...]], axis=0)
    return jnp.where(t == 0, first, x_ref[...])


def _x_specs3(tile=lambda j: j):
    return [pl.BlockSpec((TB, D), lambda j: (jnp.maximum(3 * tile(j) - 1, 0), 0)),
            pl.BlockSpec((TB, D), lambda j: (3 * tile(j), 0)),
            pl.BlockSpec((TB, D), lambda j: (3 * tile(j) + 1, 0))]


def _h_tile(j, xa_ref, xb_ref, xc_ref, meta_ref):
    first = jnp.concatenate([jnp.zeros((P0, D), F32), meta_ref[...]], axis=0)
    return jnp.concatenate([jnp.where(j == 0, first, xa_ref[...]), xb_ref[...], xc_ref[...]], axis=0)


def _full_spec(shape):
    return pl.BlockSpec(shape, lambda *_: (0,) * len(shape))


def _sigmoid(z):
    return 1.0 / (1.0 + jnp.exp(-z))


def _lane_tiles_sum(x):
    out = x[:, :TB]
    for i in range(1, x.shape[1] // TB):
        out = out + x[:, i * TB:(i + 1) * TB]
    return out


def _inproj_fwd(x, meta_full, norm_g, w_t, L, after):
    nj = L // TT

    def body(xa_ref, xb_ref, xc_ref, meta_ref, g_ref, w_ref, _, u_ref, proj_ref, f_ref, ktok_ref, vtok_ref):
        hb = _h_tile(pl.program_id(0), xa_ref, xb_ref, xc_ref, meta_ref)
        r = lax.rsqrt(jnp.mean(hb * hb, axis=-1, keepdims=True) + EPS)
        u = (hb * r * g_ref[...]).astype(BF16)
        u_ref[...] = u
        for s in range(NSEC):
            p = _dot(u, w_ref[s * DA:(s + 1) * DA, :], NT_DIMS)
            if s == 0:
                p = p * QSCALE
            if s in (1, 2):
                tok_ref = ktok_ref if s == 1 else vtok_ref
                for h in range(H):
                    tok_ref[h] = p[:, h * DH:(h + 1) * DH].astype(BF16)
            proj_ref[s * DA:(s + 1) * DA, :] = p.T.astype(BF16)
        f_ref[...] = _dot(w_ref[NSEC * DA:DPROJ, :], u, NT_DIMS)[:H]

    return pl.pallas_call(
        body, name="inproj_fwd", grid=(nj,),
        in_specs=_x_specs3() + [_full_spec((NM, D)), _full_spec((1, D)), _full_spec((DPROJ, D)), _UNREAD],
        out_specs=[
            pl.BlockSpec((TT, D), lambda t: (t, 0)),
            pl.BlockSpec((NSEC * DA, TT), lambda t: (0, t)),
            pl.BlockSpec((H, TT), lambda t: (0, t)),
            pl.BlockSpec((H, TT, DH), lambda t: (0, t, 0)),
            pl.BlockSpec((H, TT, DH), lambda t: (0, t, 0)),
        ],
        out_shape=[
            jax.ShapeDtypeStruct((L, D), BF16),
            jax.ShapeDtypeStruct((NSEC * DA, L), BF16),
            jax.ShapeDtypeStruct((H, L), F32),
            jax.ShapeDtypeStruct((H, L, DH), BF16),
            jax.ShapeDtypeStruct((H, L, DH), BF16),
        ],
        compiler_params=_params(),
    )(x, x, x, meta_full, norm_g, w_t, after)


def _split3(x):
    hi = x.astype(BF16).astype(F32)
    r = x - hi
    mid = r.astype(BF16).astype(F32)
    return hi, mid, (r - mid).astype(BF16).astype(F32)


def _bias_rows(bias):
    one = jnp.ones((1, TT), F32)
    zero = jnp.zeros((1, TT), F32)
    parts = [zero] * 3 if bias is None else list(_split3(bias))
    return jnp.concatenate([one] * 3 + parts + [zero] * (DF - 6), axis=0).astype(BF16)


def _fgate_fwd(f_t, b_col, ktok, L):
    nb = L // TB

    def body(f_ref, b_ref, ktok_ref, cq_ref, kaug_ref, sg_ref, bias_scr):
        h = pl.program_id(0)

        @pl.when(h == 0)
        def _():
            z = f_ref[...] + b_ref[...]
            idx = lax.broadcasted_iota(jnp.int32, (H, L), 1)
            real = idx >= P0
            lf = jnp.where(real, jnp.minimum(z, 0.0) - jnp.log1p(jnp.exp(-jnp.abs(z))), 0.0)
            sg_ref[...] = jnp.where(real, 1.0 / (1.0 + jnp.exp(z)), 0.0)
            c = lf
            s = 1
            while s < L:
                c = c + jnp.where(idx >= s, pltpu.roll(c, s, 1), 0.0)
                s *= 2
            c = c * LOG2E
            for hh in range(H):
                cq_ref[hh] = c[hh:hh + 1, :]
            for part, val in enumerate(_split3(-jnp.where(real, c, -NEG))):
                for hh in range(H):
                    bias_scr[part * H + hh] = val[hh:hh + 1, :]

        lane = lax.broadcasted_iota(jnp.int32, (TB, KA), 1)
        head = jnp.zeros((DH, TB), F32)
        tail = jnp.concatenate([jnp.ones((3, TB), F32), jnp.zeros((KA - DH - 6, TB), F32)], axis=0)
        for b in range(nb):
            blk = slice(b * TB, (b + 1) * TB)
            cols = jnp.concatenate(
                [head] + [bias_scr[part * H + h, :, blk] for part in range(3)] + [tail], axis=0).T
            k = jnp.concatenate([ktok_ref[0, blk, :].astype(F32), jnp.zeros((TB, KA - DH), F32)], axis=1)
            kaug_ref[0, blk, :] = jnp.where(lane < DH, k, cols).astype(BF16)

    return pl.pallas_call(
        body, name="fgate_fwd", grid=(H,),
        in_specs=[_full_spec((H, L)), _full_spec((H, 1)), pl.BlockSpec((1, L, DH), lambda h: (h, 0, 0))],
        out_specs=[_full_spec((H, 1, L)), pl.BlockSpec((1, L, KA), lambda h: (h, 0, 0)), _full_spec((H, L))],
        out_shape=[
            jax.ShapeDtypeStruct((H, 1, L), F32),
            jax.ShapeDtypeStruct((H, L, KA), BF16),
            jax.ShapeDtypeStruct((H, L), F32),
        ],
        scratch_shapes=[pltpu.VMEM((3 * H, 1, L), F32)],
        compiler_params=_params(),
    )(f_t, b_col, ktok)


def _causal_mask():
    r = lax.broadcasted_iota(jnp.int32, (TT, TT), 0)
    c = lax.broadcasted_iota(jnp.int32, (TT, TT), 1)
    return r <= c


def _attn_fwd(proj_t, kaug, cq, L):
    nq = L // TT

    def body(q_ref, qn_ref, kaug_ref, v_ref, cq_ref, o_ref, lse_ref,
             qa_scr, s_scr, cmax_scr, m_scr, p_scr, alpha_scr, acc_scr):
        j = pl.program_id(0)
        rows = [slice(g * DH, (g + 1) * DH) for g in range(HG)]
        ones = jnp.ones((DF, TT), BF16)

        def load_queries(ref):
            for g in range(HG):
                qa_scr[g] = jnp.concatenate(
                    [ref[rows[g], :], _bias_rows(None), jnp.zeros((KA - DH - DF, TT), BF16)], axis=0)

        def scores(kt, masked):
            k_off = pl.multiple_of(kt * TT, TT)
            for g in range(HG):
                s = _dot(kaug_ref[g, pl.ds(k_off, TT), :], qa_scr[g])
                if masked:
                    s = jnp.where(_causal_mask(), s, NEG)
                s_scr[g] = s
                cmax_scr[g] = jnp.max(s, axis=0, keepdims=True)

        def softmax():
            for g in range(HG):
                m_old = m_scr[g]
                m_new = jnp.maximum(m_old, cmax_scr[g])
                alpha_scr[g] = jnp.exp2(m_old - m_new)
                p_scr[g] = jnp.exp2(s_scr[g] - m_new).astype(BF16)
                m_scr[g] = m_new

        def weighted_sum(kt):
            k_off = pl.multiple_of(kt * TT, TT)
            for g in range(HG):
                v1 = jnp.concatenate([v_ref[rows[g], pl.ds(k_off, TT)], ones], axis=0)
                acc_scr[g] = alpha_scr[g] * acc_scr[g] + _dot(v1, p_scr[g])

        @pl.when(j == 0)
        def _():
            load_queries(q_ref)
            scores(0, True)

        m_scr[...] = jnp.full_like(m_scr, NEG)
        acc_scr[...] = jnp.zeros_like(acc_scr)

        @pl.when(j >= 1)
        def _():
            softmax()
            scores(j - 1, False)

        def step(i, c):
            weighted_sum(j - i + 1)
            softmax()
            scores(j - i - 1, False)
            return c

        lax.fori_loop(1, j, step, 0)

        def drain(second_last, next_tile):
            if second_last:
                weighted_sum(1)
            softmax()
            if next_tile:
                load_queries(qn_ref)
                scores(j + 1, True)
            weighted_sum(0)

        @pl.when(j == 0)
        def _():
            drain(False, nq > 1)

        @pl.when((j >= 1) & (j < nq - 1))
        def _():
            drain(True, True)

        @pl.when((j >= 1) & (j == nq - 1))
        def _():
            drain(True, False)

        for g in range(HG):
            l = acc_scr[g, DH:DH + 1, :]
            o_ref[rows[g], :] = acc_scr[g, :DH, :] * (1.0 / l)
            lse_ref[g] = m_scr[g] + jnp.log2(l) + cq_ref[g]

    assert HG == H
    return pl.pallas_call(
        body, name="attn_fwd", grid=(nq,),
        in_specs=[
            pl.BlockSpec((DA, TT), lambda j: (0, j)),
            pl.BlockSpec((DA, TT), lambda j: (0, jnp.minimum(j + 1, nq - 1))),
            pl.BlockSpec((H, L, KA), lambda j: (0, 0, 0)),
            pl.BlockSpec((DA, L), lambda j: (2, 0)),
            pl.BlockSpec((H, 1, TT), lambda j: (0, 0, j)),
        ],
        out_specs=[
            pl.BlockSpec((DA, TT), lambda j: (0, j)),
            pl.BlockSpec((H, 1, TT), lambda j: (0, 0, j)),
        ],
        out_shape=[jax.ShapeDtypeStruct((DA, L), F32), jax.ShapeDtypeStruct((H, 1, L), F32)],
        scratch_shapes=[pltpu.VMEM((HG, KA, TT), BF16), pltpu.VMEM((HG, TT, TT), F32), pltpu.VMEM((HG, 1, TT), F32),
                        pltpu.VMEM((HG, 1, TT), F32), pltpu.VMEM((HG, TT, TT), BF16), pltpu.VMEM((HG, 1, TT), F32),
                        pltpu.VMEM((HG, DH + DF, TT), F32)],
        compiler_params=_params(),
    )(proj_t, proj_t, kaug, proj_t, cq)


def _gate_group(rows, o_ref, za_ref, gb_ref, gc_ref, xc_ref, zc_ref, gcp_ref, xcp_ref, cw_ref, ga_ref, gcn_ref, first):
    n_rep = TT // TB
    f32 = lambda r: r[rows, :].astype(F32)
    o, za, gb, gc, xc, zc = o_ref[rows, :], f32(za_ref), f32(gb_ref), f32(gc_ref), f32(xc_ref), f32(zc_ref)
    a = gc * xc
    a_prev = jnp.where(first, 0.0, f32(gcp_ref) * f32(xcp_ref))
    full = jnp.concatenate([a_prev, a], axis=1)
    a1 = pltpu.roll(full, 1, 1)[:, TB:]
    a2 = pltpu.roll(full, 2, 1)[:, TB:]
    w0 = jnp.tile(cw_ref[0, rows, :], (1, n_rep))
    w1 = jnp.tile(cw_ref[1, rows, :], (1, n_rep))
    w2 = jnp.tile(cw_ref[2, rows, :], (1, n_rep))
    cv = w0 * a2 + w1 * a1 + w2 * a
    e = gb * cv
    rc = lax.rsqrt(jnp.mean(e * e, axis=0, keepdims=True) + EPS)
    ec = e * rc
    ra = lax.rsqrt(jnp.mean(o * o, axis=0, keepdims=True) + EPS)
    oa = o * ra
    g_a = jnp.tile(ga_ref[rows, :], (1, n_rep))
    g_c = jnp.tile(gcn_ref[rows, :], (1, n_rep))
    sa = _sigmoid(za)
    sc = _sigmoid(zc)
    return dict(o=o, za=za, gb=gb, gc=gc, xc=xc, zc=zc, a=a, a1=a1, a2=a2, w0=w0, w1=w1, w2=w2, cv=cv, e=e,
                rc=rc, ec=ec, ra=ra, oa=oa, g_a=g_a, g_c=g_c, sa=sa, sc=sc)


def _gate_specs(tile):
    def sec(s):
        return pl.BlockSpec((DA, TT), lambda i: (s, tile(i)))

    def halo(s):
        return pl.BlockSpec((DA, TB), lambda i: (s, jnp.maximum(3 * tile(i) - 1, 0)))

    return [pl.BlockSpec((DA, TT), lambda i: (0, tile(i))), sec(3), sec(4), sec(5), sec(6), sec(7), halo(5), halo(6),
            _full_spec((3, DA, TB)), _full_spec((DA, TB)), _full_spec((DA, TB))]


def _outproj(o_t, proj_t, cw_b, ga_b, gcn_b, w_out, x, meta_full, fng, target, L):
    nj = L // TT
    rp = NM
    n_bwd = 8
    cb = D // 4
    assert P0 % rp == 0 and TB % rp == 0 and (TT // rp) % n_bwd == 0 and H == n_bwd

    def body(o_ref, za_ref, gb_ref, gc_ref, xcv_ref, zc_ref, gcp_ref, xcp_ref, cw_ref, ga_ref, gcn_ref,
             w_ref, xa_ref, xb_ref, xc_ref, meta_ref, g_ref, ta_ref, tb_ref, tc_ref,
             dout_ref, dmix_ref, dwb_ref, loss_ref, dg_ref,
             dw_ref, o_scr, db_new, db_old, mix_new, mix_old, sq_acc, dg_acc):
        t = pl.program_id(0)

        def gate_rows(h):
            rows = slice(h * DH, (h + 1) * DH)
            g = _gate_group(rows, o_ref, za_ref, gb_ref, gc_ref, xcv_ref, zc_ref, gcp_ref, xcp_ref,
                            cw_ref, ga_ref, gcn_ref, t == 0)
            mix_new[rows, :] = (g["oa"] * g["g_a"] * (g["za"] * g["sa"])).astype(BF16)
            mix_new[DA + h * DH:DA + (h + 1) * DH, :] = (g["ec"] * g["g_c"] * (g["zc"] * g["sc"])).astype(BF16)

        def loss_rows(c):
            blk = c // (TB // rp)
            rows, out_rows = pl.ds((c % (TB // rp)) * rp, rp), pl.ds(c * rp, rp)
            h = (xa_ref, xb_ref, xc_ref)[blk][rows, :]
            if blk == 0:
                first = meta_ref[...] if c == P0 // rp else jnp.zeros((rp, D), F32)
                h = jnp.where(t == 0, first, h)
            o = o_scr[out_rows, :] + h
            r = lax.rsqrt(jnp.mean(o * o, axis=-1, keepdims=True) + EPS)
            orn = o * r
            g = g_ref[...]
            diff = orn * g - (ta_ref, tb_ref, tc_ref)[blk][rows, :]
            if blk == 0:
                diff = diff * jnp.where(t > 0, 1.0, 0.0)
            gy = diff * (g * (1.0 / D))
            dout = r * (gy - orn * jnp.mean(gy * orn, axis=-1, keepdims=True))
            dout_ref[out_rows, :] = dout
            db_new[out_rows, :] = dout.astype(BF16)
            sq, go = diff * diff, diff * orn
            sq_acc[...] += sq[:8] + sq[8:]
            dg_acc[...] += go[:8] + go[8:]

        def backward_cols(n):
            if n < 4:
                cols = slice(n * cb, (n + 1) * cb)
                dmix_ref[cols, :] = _dot(db_old[...], w_ref[cols, :], NT_DIMS).T.astype(BF16)
            else:
                cols = slice((n - 4) * cb, (n - 3) * cb)
                dw_ref[:, cols] += _dot(mix_old[...], db_old[:, cols])

        def step(forward, backward):
            if not forward:
                for n in range(n_bwd):
                    backward_cols(n)
                return
            for h in range(H):
                gate_rows(h)
                if backward and h % 2 == 1:
                    backward_cols(h // 2)
            o_scr[...] = _dot(mix_new[...], w_ref[...], TN_DIMS)
            per = TT // rp // n_bwd
            for k in range(n_bwd):
                for c in range(per * k, per * (k + 1)):
                    loss_rows(c)
                if backward and k % 2 == 1:
                    backward_cols(n_bwd // 2 + k // 2)
            db_old[...] = db_new[...]
            mix_old[...] = mix_new[...]

        @pl.when(t == 0)
        def _():
            dw_ref[...] = jnp.zeros_like(dw_ref)
            sq_acc[...] = jnp.zeros_like(sq_acc)
            dg_acc[...] = jnp.zeros_like(dg_acc)
            step(True, False)

        @pl.when((t > 0) & (t < nj))
        def _():
            step(True, True)

        @pl.when(t == nj)
        def _():
            step(False, True)
            dwb_ref[...] = dw_ref[...].astype(BF16)
            loss_ref[...] = jnp.sum(sq_acc[...], keepdims=True) * (0.5 / D)
            dg_ref[...] = jnp.sum(dg_acc[...], axis=0, keepdims=True) * (1.0 / D)

    cur = lambda t: jnp.minimum(t, nj - 1)
    prev = lambda t: jnp.maximum(t - 1, 0)
    return pl.pallas_call(
        body, name="outproj", grid=(nj + 1,),
        in_specs=_gate_specs(cur) + [_full_spec((D, D))] + _x_specs3(cur)
                 + [_full_spec((NM, D)), _full_spec((1, D))] + _x_specs3(cur),
        out_specs=[pl.BlockSpec((TT, D), lambda t: (cur(t), 0)), pl.BlockSpec((D, TT), lambda t: (0, prev(t))),
                   _full_spec((D, D)), _full_spec((1, 1)), _full_spec((1, D))],
        out_shape=[jax.ShapeDtypeStruct((L, D), F32), jax.ShapeDtypeStruct((D, L), BF16),
                   jax.ShapeDtypeStruct((D, D), BF16), jax.ShapeDtypeStruct((1, 1), F32),
                   jax.ShapeDtypeStruct((1, D), F32)],
        scratch_shapes=[pltpu.VMEM((D, D), F32), pltpu.VMEM((TT, D), F32), pltpu.VMEM((TT, D), BF16),
                        pltpu.VMEM((TT, D), BF16), pltpu.VMEM((D, TT), BF16), pltpu.VMEM((D, TT), BF16),
                        pltpu.VMEM((8, D), F32), pltpu.VMEM((8, D), F32)],
        compiler_params=_params(),
    )(o_t, proj_t, proj_t, proj_t, proj_t, proj_t, proj_t, proj_t, cw_b, ga_b, gcn_b,
      w_out, x, x, x, meta_full, fng, target, target, target)


def _gate_bwd(dmix_t, o_t, proj_t, cw_b, ga_b, gcn_b, L, after):
    nj = L // TT

    def body(dmix_ref, o_ref, za_ref, gb_ref, gc_ref, xc_ref, zc_ref, gcp_ref, xcp_ref, cw_ref, ga_ref, gcn_ref, _,
             do_ref, dd_ref, dg5_ref, dga_ref, dgc_ref, dcw_ref, carry_ref):
        i = pl.program_id(0)
        j = nj - 1 - i

        @pl.when(i == 0)
        def _():
            carry_ref[...] = jnp.zeros_like(carry_ref)
            dga_ref[...] = jnp.zeros_like(dga_ref)
            dgc_ref[...] = jnp.zeros_like(dgc_ref)
            dcw_ref[...] = jnp.zeros_like(dcw_ref)

        def group(h, c):
            r0 = pl.multiple_of(h * DH, DH)
            rows = pl.ds(r0, DH)
            sec = lambda s: pl.ds(s * DA + r0, DH)
            g = _gate_group(rows, o_ref, za_ref, gb_ref, gc_ref, xc_ref, zc_ref, gcp_ref, xcp_ref,
                            cw_ref, ga_ref, gcn_ref, j == 0)
            o, za, gb, gc, xc, zc, sa, sc = (g[n] for n in ("o", "za", "gb", "gc", "xc", "zc", "sa", "sc"))
            dya = dmix_ref[rows, :].astype(F32)
            dyc = dmix_ref[pl.ds(DA + r0, DH), :].astype(F32)

            dn = dya * (za * sa)
            dg5_ref[sec(0), :] = (dya * (g["oa"] * g["g_a"]) * (sa * (1.0 + za * (1.0 - sa)))).astype(BF16)
            dga_ref[rows, :] += _lane_tiles_sum(dn * g["oa"])
            dng = dn * g["g_a"]
            mean_a = jnp.mean(dng * g["oa"], axis=0, keepdims=True)
            do = (dng - g["oa"] * mean_a) * g["ra"]
            do_ref[rows, :] = do.astype(BF16)
            dd_ref[h] = jnp.sum(do * o, axis=0, keepdims=True)

            dnc = dyc * (zc * sc)
            dg5_ref[sec(4), :] = (dyc * (g["ec"] * g["g_c"]) * (sc * (1.0 + zc * (1.0 - sc)))).astype(BF16)
            dgc_ref[rows, :] += _lane_tiles_sum(dnc * g["ec"])
            dncg = dnc * g["g_c"]
            mean_c = jnp.mean(dncg * g["ec"], axis=0, keepdims=True)
            de = (dncg - g["ec"] * mean_c) * g["rc"]
            dg5_ref[sec(1), :] = (de * g["cv"]).astype(BF16)
            dcv = de * gb
            full = jnp.concatenate([dcv, carry_ref[rows, :]], axis=1)
            d1 = pltpu.roll(full, TT + TB - 1, 1)[:, :TT]
            d2 = pltpu.roll(full, TT + TB - 2, 1)[:, :TT]
            carry_ref[rows, :] = dcv[:, :TB]
            da = g["w2"] * dcv + g["w1"] * d1 + g["w0"] * d2
            dg5_ref[sec(2), :] = (da * xc).astype(BF16)
            dg5_ref[sec(3), :] = (da * gc).astype(BF16)
            dcw_ref[0, rows, :] += _lane_tiles_sum(dcv * g["a2"])
            dcw_ref[1, rows, :] += _lane_tiles_sum(dcv * g["a1"])
            dcw_ref[2, rows, :] += _lane_tiles_sum(dcv * g["a"])
            return c

        lax.fori_loop(0, H, group, 0, unroll=2)

    rj = lambda i: nj - 1 - i
    return pl.pallas_call(
        body, name="gate_bwd", grid=(nj,),
        in_specs=[pl.BlockSpec((2 * DA, TT), lambda i: (0, rj(i)))] + _gate_specs(rj) + [_UNREAD],
        out_specs=[
            pl.BlockSpec((DA, TT), lambda i: (0, rj(i))),
            pl.BlockSpec((H, 1, TT), lambda i: (0, 0, rj(i))),
            pl.BlockSpec((5 * DA, TT), lambda i: (0, rj(i))),
            _full_spec((DA, TB)), _full_spec((DA, TB)), _full_spec((3, DA, TB)),
        ],
        out_shape=[
            jax.ShapeDtypeStruct((DA, L), BF16),
            jax.ShapeDtypeStruct((H, 1, L), F32),
            jax.ShapeDtypeStruct((5 * DA, L), BF16),
            jax.ShapeDtypeStruct((DA, TB), F32),
            jax.ShapeDtypeStruct((DA, TB), F32),
            jax.ShapeDtypeStruct((3, DA, TB), F32),
        ],
        scratch_shapes=[pltpu.VMEM((DA, TB), F32)],
        compiler_params=_params(),
    )(dmix_t, o_t, proj_t, proj_t, proj_t, proj_t, proj_t, proj_t, proj_t, cw_b, ga_b, gcn_b, after)


def _attn_bwd(proj_t, kaug, vtok, do_t, lse, dd, cq, L):
    nk = L // TT

    def body(q_ref, kaug_ref, vtok_ref, kt_ref, do_ref, lse_ref, dd_ref, cq_ref,
             dq_ref, dk_ref, dv_ref, dck_ref, dcq_ref, dq_acc, kt1_scr, s_scr, dp_scr, dv_scr, dk_scr):
        i = pl.program_id(0)
        rows = [slice(g * DH, (g + 1) * DH) for g in range(HG)]
        ones = jnp.ones((DF, TT), BF16)
        zpad = jnp.zeros((KA - DH - DF, TT), BF16)
        for g in range(HG):
            kt1_scr[g] = jnp.concatenate([kt_ref[rows[g], :], ones], axis=0)
        dv_scr[...] = jnp.zeros_like(dv_scr)
        dk_scr[...] = jnp.zeros_like(dk_scr)

        def q_rows(g, q_off):
            bias = cq_ref[g, :, pl.ds(q_off, TT)] - lse_ref[g, :, pl.ds(q_off, TT)]
            return jnp.concatenate([q_ref[rows[g], pl.ds(q_off, TT)], _bias_rows(bias)], axis=0)

        def scores(jq, masked):
            q_off = pl.multiple_of(jq * TT, TT)
            for g in range(HG):
                s = _dot(kaug_ref[g], jnp.concatenate([q_rows(g, q_off), zpad], axis=0))
                if masked:
                    s = jnp.where(_causal_mask(), s, NEG)
                s_scr[g] = s
                dp_scr[g] = _dot(vtok_ref[g], do_ref[rows[g], pl.ds(q_off, TT)])

        def grads(jq):
            q_off = pl.multiple_of(jq * TT, TT)
            for g in range(HG):
                p = jnp.exp2(s_scr[g])
                ds = (p * (dp_scr[g] - dd_ref[g, :, pl.ds(q_off, TT)])).astype(BF16)
                do1 = jnp.concatenate([do_ref[rows[g], pl.ds(q_off, TT)], jnp.zeros((KA - DH, TT), BF16)], axis=0)
                q1 = jnp.concatenate([q_rows(g, q_off), zpad], axis=0)
                dv_scr[g] += _dot(p.astype(BF16), do1, NT_DIMS)
                dk_scr[g] += _dot(ds, q1, NT_DIMS)
                dq_acc[g, :, pl.ds(q_off, TT)] += _dot(kt1_scr[g], ds)

        @pl.when(i == 0)
        def _():
            dq_acc[...] = jnp.zeros_like(dq_acc)

        scores(i, True)

        def step(jq, c):
            grads(jq)
            scores(jq + 1, False)
            return c

        lax.fori_loop(i, nk - 1, step, 0)
        grads(nk - 1)
        for g in range(HG):
            dv_ref[rows[g], :] = dv_scr[g].T[:DH, :].astype(BF16)
            dk_t = dk_scr[g].T
            dk_ref[rows[g], :] = (dk_t[:DH, :] * LN2).astype(BF16)
            dck_ref[g] = dk_t[DH:DH + 1, :]

        @pl.when(i == nk - 1)
        def _():
            for g in range(HG):
                dq_ref[rows[g], :] = (dq_acc[g, :DH, :] * (DH ** -0.5)).astype(BF16)
                dcq_ref[g] = dq_acc[g, DH:DH + 1, :]

    assert HG == H
    head = lambda i: (0, 0)
    row = lambda i: (0, 0, 0)
    return pl.pallas_call(
        body, name="attn_bwd", grid=(nk,),
        in_specs=[
            pl.BlockSpec((DA, L), head),
            pl.BlockSpec((H, TT, KA), lambda i: (0, i, 0)),
            pl.BlockSpec((H, TT, DH), lambda i: (0, i, 0)),
            pl.BlockSpec((DA, TT), lambda i: (1, i)),
            pl.BlockSpec((DA, L), head),
            pl.BlockSpec((H, 1, L), row), pl.BlockSpec((H, 1, L), row), pl.BlockSpec((H, 1, L), row),
        ],
        out_specs=[
            pl.BlockSpec((DA, L), head),
            pl.BlockSpec((DA, TT), lambda i: (0, i)),
            pl.BlockSpec((DA, TT), lambda i: (0, i)),
            pl.BlockSpec((H, 1, TT), lambda i: (0, 0, i)),
            pl.BlockSpec((H, 1, L), row),
        ],
        out_shape=[jax.ShapeDtypeStruct((DA, L), BF16), jax.ShapeDtypeStruct((DA, L), BF16),
                   jax.ShapeDtypeStruct((DA, L), BF16), jax.ShapeDtypeStruct((H, 1, L), F32),
                   jax.ShapeDtypeStruct((H, 1, L), F32)],
        scratch_shapes=[
            pltpu.VMEM((HG, DH + DF, L), F32),
            pltpu.VMEM((HG, DH + DF, TT), BF16),
            pltpu.VMEM((HG, TT, TT), F32), pltpu.VMEM((HG, TT, TT), F32),
            pltpu.VMEM((HG, TT, KA), F32), pltpu.VMEM((HG, TT, KA), F32)],
        compiler_params=_params(),
    )(proj_t, kaug, vtok, proj_t, do_t, lse, dd, cq)


def _fgate_bwd(dcq, dck, sg, L):
    def body(dcq_ref, dck_ref, sg_ref, df_ref, db_ref):
        dc = jnp.concatenate([dcq_ref[h] - dck_ref[h] for h in range(H)], axis=0)
        idx = lax.broadcasted_iota(jnp.int32, (H, L), 1)
        r = dc
        s = 1
        while s < L:
            r = r + jnp.where(idx + s < L, pltpu.roll(r, L - s, 1), 0.0)
            s *= 2
        df = r * sg_ref[...]
        db_ref[...] = jnp.broadcast_to(jnp.sum(df, axis=1, keepdims=True), (H, TB))
        df_ref[...] = jnp.concatenate([df, jnp.zeros((DF - H, L), F32)], axis=0).astype(BF16)

    return pl.pallas_call(
        body, name="fgate_bwd",
        out_shape=[jax.ShapeDtypeStruct((DF, L), BF16), jax.ShapeDtypeStruct((H, TB), F32)],
        compiler_params=pltpu.CompilerParams(vmem_limit_bytes=VMEM_LIMIT),
    )(dcq, dck, sg)


def _inproj_bwd_x(w, dq_t, dk_t, dv_t, dg5_t, df_t, dout, x, meta_full, norm_g, L, after):
    nj = L // TT
    seq = x.shape[0]

    def body(w_ref, dq_ref, dk_ref, dv_ref, dg5_ref, df_ref, dout_ref, xa_ref, xb_ref, xc_ref, meta_ref, g_ref, _,
             gx_ref, dmeta_ref, dg_ref, dh_scr, sems):
        j = pl.program_id(0)
        slot = j % 2

        def copy_out(step, slot_):
            first = pltpu.make_async_copy(dh_scr.at[slot_, pl.ds(TB, TT - TB)], gx_ref.at[pl.ds(0, TT - TB)],
                                          sems.at[slot_])
            later = pltpu.make_async_copy(dh_scr.at[slot_], gx_ref.at[pl.ds(step * TT - TB, TT)], sems.at[slot_])
            return first, later

        @pl.when(j == 0)
        def _():
            dg_ref[...] = jnp.zeros_like(dg_ref)

        du = _dot(dq_ref[...], w_ref[0:DA, :], TN_DIMS)
        du += _dot(dk_ref[...], w_ref[DA:2 * DA, :], TN_DIMS)
        du += _dot(dv_ref[...], w_ref[2 * DA:3 * DA, :], TN_DIMS)
        du += _dot(dg5_ref[...], w_ref[3 * DA:NSEC * DA, :], TN_DIMS)
        du += _dot(df_ref[...], w_ref[NSEC * DA:DPROJ, :], TN_DIMS)
        hb = _h_tile(j, xa_ref, xb_ref, xc_ref, meta_ref)
        r = lax.rsqrt(jnp.mean(hb * hb, axis=-1, keepdims=True) + EPS)
        hn = hb * r
        dg_ref[...] += jnp.sum(du * hn, axis=0, keepdims=True)
        gu = du * g_ref[...]
        dh = dout_ref[...] + r * gu - hn * (r * jnp.mean(gu * hn, axis=-1, keepdims=True))

        dh_scr[slot] = dh

        @pl.when(j == 0)
        def _():
            dmeta_ref[...] = dh[P0:TB, :]
            copy_out(0, 0)[0].start()

        @pl.when(j >= 1)
        def _():
            copy_out(j, slot)[1].start()

        @pl.when(j == 1)
        def _():
            copy_out(0, 0)[0].wait()

        @pl.when(j >= 2)
        def _():
            copy_out(j - 1, 1 - slot)[1].wait()

        @pl.when(j == nj - 1)
        def _():
            copy_out(j, slot)[0 if nj == 1 else 1].wait()

    blk = lambda rows: pl.BlockSpec((rows, TT), lambda j: (0, j))
    return pl.pallas_call(
        body, name="inproj_bwd_x", grid=(nj,),
        in_specs=[_full_spec((DPROJ, D)), blk(DA), blk(DA), blk(DA), blk(5 * DA), blk(DF),
                  pl.BlockSpec((TT, D), lambda j: (j, 0))] + _x_specs3()
                 + [_full_spec((NM, D)), _full_spec((1, D)), _UNREAD],
        out_specs=[pl.BlockSpec(memory_space=pl.ANY), _full_spec((NM, D)), _full_spec((1, D))],
        out_shape=[jax.ShapeDtypeStruct((seq, D), F32), jax.ShapeDtypeStruct((NM, D), F32),
                   jax.ShapeDtypeStruct((1, D), F32)],
        scratch_shapes=[pltpu.VMEM((2, TT, D), F32), pltpu.SemaphoreType.DMA((2,))],
        compiler_params=_params(),
    )(w, dq_t, dk_t, dv_t, dg5_t, df_t, dout, x, x, x, meta_full, norm_g, after)


def _inproj_bwd_w(u, dq_t, dk_t, dv_t, dg5_t, df_t, L):
    def body(u_ref, dq_hbm, dk_hbm, dv_hbm, dg5_ref, df_ref, dw_ref, dwf_ref, qkv_scr, sems):
        s = pl.program_id(0)
        u_all = u_ref[...]
        fetch = [pltpu.make_async_copy(src, qkv_scr.at[k], sems.at[k])
                 for k, src in enumerate((dq_hbm, dk_hbm, dv_hbm))]

        @pl.when(s == 0)
        def _():
            for cp in fetch:
                cp.start()

        @pl.when(s < 5)
        def _():
            dw_ref[...] = _dot(dg5_ref[...], u_all)

        for k in range(3):
            @pl.when(s == 5 + k)
            def _(k=k):
                fetch[k].wait()
                dw_ref[...] = _dot(qkv_scr[k], u_all)

        @pl.when(s == NSEC - 1)
        def _():
            dwf_ref[...] = _dot(df_ref[...], u_all)

    once = lambda shape: pl.BlockSpec(shape, lambda s: (0, 0), pipeline_mode=pl.Buffered(1))
    any_spec = pl.BlockSpec(memory_space=pl.ANY)
    return pl.pallas_call(
        body, name="inproj_bwd_w", grid=(NSEC,),
        in_specs=[
            once((L, D)), any_spec, any_spec, any_spec,
            pl.BlockSpec((DA, L), lambda s: (jnp.minimum(s, 4), 0)),
            once((DF, L)),
        ],
        out_specs=[pl.BlockSpec((DA, D), lambda s: (jnp.where(s < 5, s + 3, s - 5), 0)), _full_spec((DF, D))],
        out_shape=[jax.ShapeDtypeStruct((NSEC * DA, D), F32), jax.ShapeDtypeStruct((DF, D), F32)],
        scratch_shapes=[pltpu.VMEM((3, DA, L), BF16), pltpu.SemaphoreType.DMA((3,))],
        compiler_params=_params(),
    )(u, dq_t, dk_t, dv_t, dg5_t, df_t)


def _adamw(w, g, m, v):
    m = ADAM_B1 * m + (1.0 - ADAM_B1) * g
    v = ADAM_B2 * v + (1.0 - ADAM_B2) * (g * g)
    m_hat = m / (1.0 - ADAM_B1 ** ADAM_STEP)
    v_hat = v / (1.0 - ADAM_B2 ** ADAM_STEP)
    delta = -ADAM_LR * (m_hat / (jnp.sqrt(v_hat) + ADAM_EPS) + ADAM_WD * w)
    return delta, m, v


def _adamw_big(own_in, land_in, own_out, land_out, w_in_t, m_in_t, v_in_t, w_out, m_out, v_out):
    cb = CB
    e_sh = D // NDEV
    in_shape = jax.ShapeDtypeStruct(w_in_t.shape, F32)
    out_shape = jax.ShapeDtypeStruct(w_out.shape, F32)

    def total(own_ref, land_ref, rows, chips):
        g = _pick_slab(0, own_ref, land_ref, rows, chips=chips).astype(F32)
        for j in range(1, own_ref.shape[0]):
            g = g + _pick_slab(j, own_ref, land_ref, rows, chips=chips).astype(F32)
        return g

    def body(oi_ref, li_ref, oo_ref, lo_ref, wi_ref, mi_ref, vi_ref, wo_ref, mo_ref, vo_ref,
             gi, di, mi, vi, go, do, mo, vo):
        g = total(oi_ref, li_ref, slice(0, WSHP), True)[:WSH]
        d, mn, vn = _adamw(wi_ref[...], g, mi_ref[...], vi_ref[...])
        gi[...], di[...], mi[...], vi[...] = g, d, mn, vn
        g = total(oo_ref, lo_ref, slice(0, e_sh), False)
        d, mn, vn = _adamw(wo_ref[0], g, mo_ref[0], vo_ref[0])
        go[0], do[0], mo[0], vo[0] = g, d, mn, vn

    slab = lambda n, rows: pl.BlockSpec((n, rows, cb), lambda i: (0, 0, i))
    ispec = pl.BlockSpec((WSH, cb), lambda i: (0, i))
    ospec = pl.BlockSpec((1, e_sh, cb), lambda i: (0, 0, i))
    return pl.pallas_call(
        body, name="adamw_big", grid=(D // cb,),
        in_specs=[slab(4, WSHP), slab(4, WSHP), slab(NDEV, e_sh), slab(NDEV, e_sh),
                  ispec, ispec, ispec, ospec, ospec, ospec],
        out_specs=[ispec] * 4 + [ospec] * 4, out_shape=[in_shape] * 4 + [out_shape] * 4,
        compiler_params=_params(),
    )(own_in, land_in, own_out, land_out, w_in_t, m_in_t, v_in_t, w_out, m_out, v_out)


F0 = 3 * DA


def _unshard_w_out(own, land):
    e_sh = D // NDEV

    def body(own_ref, land_ref, wo_ref):
        for j in range(NDEV):
            wo_ref[j * e_sh:(j + 1) * e_sh, :] = _pick_slab(j, own_ref, land_ref, slice(0, e_sh), per_peer=False)

    return pl.pallas_call(
        body, name="unshard_w_out", grid=(D // CB,),
        in_specs=[pl.BlockSpec((e_sh, CB), lambda i: (0, i)), pl.BlockSpec((NDEV, e_sh, CB), lambda i: (0, 0, i))],
        out_specs=pl.BlockSpec((D, CB), lambda i: (0, i)),
        out_shape=jax.ShapeDtypeStruct((D, D), BF16),
        compiler_params=_params(),
    )(own, land)


def _unshard_w_in(w_all, small_all, attn_gain, conv_gain):
    def body(w_ref, small_ref, ga_ref, gc_ref, wt_ref, meta_ref, cwb_ref, gab_ref, gcb_ref):
        i = pl.program_id(0)
        for k in range(CB // TB):
            meta_ref[:, k * TB:(k + 1) * TB] = small_ref[(CB // TB) * i + k, 0:NM, :]

        @pl.when(i == 0)
        def _():
            per_row = lambda line: jnp.broadcast_to(line, (TB, DA)).T
            cw = jnp.concatenate([small_ref[j, NM:NM + 3, 0:DH] for j in range(NDEV)], axis=1)
            for k in range(3):
                cwb_ref[k] = per_row(cw[k:k + 1, :])
            gab_ref[...] = per_row(ga_ref[...])
            gcb_ref[...] = per_row(gc_ref[...])

        def ref_rows(lo, hi):
            pieces, r = [], lo
            while r < hi:
                sh, off = divmod(r, WSH)
                n = min(hi - r, WSH - off)
                pieces.append(w_ref[sh, off:off + n, :])
                r += n
            return pieces

        for s in range(NSEC):
            lo = s * DA if s < 3 else s * DA + H
            wt_ref[s * DA:(s + 1) * DA, :] = jnp.concatenate(ref_rows(lo, lo + DA), axis=0)
        wt_ref[NSEC * DA:DPROJ, :] = jnp.concatenate(
            ref_rows(F0, F0 + H) + [jnp.zeros((DF - H, CB), BF16)], axis=0)

    return pl.pallas_call(
        body, name="unshard_w_in", grid=(D // CB,),
        in_specs=[pl.BlockSpec((NDEV, WSHP, CB), lambda i: (0, 0, i)), _full_spec(small_all.shape),
                  _full_spec((1, DA)), _full_spec((1, DA))],
        out_specs=[pl.BlockSpec((DPROJ, CB), lambda i: (0, i)), pl.BlockSpec((NM, CB), lambda i: (0, i)),
                   _full_spec((3, DA, TB)), _full_spec((DA, TB)), _full_spec((DA, TB))],
        out_shape=[jax.ShapeDtypeStruct((DPROJ, D), BF16), jax.ShapeDtypeStruct((NM, D), F32),
                   jax.ShapeDtypeStruct((3, DA, TB), F32), jax.ShapeDtypeStruct((DA, TB), F32),
                   jax.ShapeDtypeStruct((DA, TB), F32)],
        compiler_params=_params(),
    )(w_all, small_all, attn_gain, conv_gain)


def _shard_w_in_grads(dw_main, dw_f):
    def body(dm_ref, df_ref, p_ref):
        mc = lax.axis_index("c")

        def ref_rows(lo, hi):
            pieces, r = [], lo
            while r < hi:
                if r < F0:
                    n = min(hi, F0) - r
                    pieces.append(dm_ref[r:r + n, :])
                elif r < F0 + H:
                    n = min(hi, F0 + H) - r
                    pieces.append(df_ref[r - F0:r - F0 + n, :])
                else:
                    n = hi - r
                    pieces.append(dm_ref[r - H:r - H + n, :])
                r += n
            return pieces

        for i in range(NDEV):
            rows = jnp.concatenate(ref_rows(i * WSH, (i + 1) * WSH) + [jnp.zeros((WSHP - WSH, CB), F32)], axis=0)
            p_ref[i // 2 + jnp.where(mc == i % 2, 0, 4)] = rows.astype(BF16)

    col = lambda rows: pl.BlockSpec((rows, CB), lambda i: (0, i))
    return pl.pallas_call(
        body, name="shard_w_in_grads", grid=(D // CB,),
        in_specs=[col(NSEC * DA), col(DF)],
        out_specs=pl.BlockSpec((NDEV, WSHP, CB), lambda i: (0, 0, i)),
        out_shape=jax.ShapeDtypeStruct((NDEV, WSHP, D), BF16),
        compiler_params=_params(),
    )(dw_main, dw_f)


SMALL = ("norm_g", "final_norm_g", "attn_norm_g", "conv_norm_g", "b_f", "meta", "conv_w")


def _as_rows(x):
    return jnp.concatenate([x[:, r * TB:(r + 1) * TB] for r in range(x.shape[1] // TB)], axis=0)


def _as_line(rows):
    return jnp.concatenate([rows[r:r + 1, :] for r in range(rows.shape[0])], axis=1)


def _pad_rows(x, n=8):
    return jnp.concatenate([x, jnp.zeros((n - x.shape[0], x.shape[1]), F32)], axis=0)


def _tile_rows(a, rows, lanes=TB):
    a = a.reshape(rows, lanes)
    return jnp.pad(a, ((0, -rows % 8), (0, TB - lanes)))


def _pack_small_grads(dg_norm, dg_final, dga_p, dgc_p, dcw_p, db_b, dmeta, loss):
    def body(dgn_ref, dgf_ref, dga_ref, dgc_ref, dcw_ref, db_ref, dmeta_ref, loss_ref, out_ref):
        def lane_sums(p):
            return jnp.sum(p.T, axis=0, keepdims=True)

        lane = lax.broadcasted_iota(jnp.int32, (1, TB), 1)
        b_row = jnp.where(lane == H, loss_ref[...], 0.0)
        for h in range(H):
            b_row = b_row + jnp.where(lane == h, db_ref[h:h + 1, :], 0.0)
        common = jnp.concatenate([
            _as_rows(dgn_ref[...]), _as_rows(dgf_ref[...]), _pad_rows(_as_rows(lane_sums(dga_ref[...]))),
            _pad_rows(_as_rows(lane_sums(dgc_ref[...]))), _pad_rows(b_row)], axis=0)
        dcw = [lane_sums(dcw_ref[k]) for k in range(3)]
        for j in range(NDEV):
            cw = jnp.concatenate(
                [jnp.concatenate([r[:, j * DH:(j + 1) * DH], jnp.zeros((1, TB - DH), F32)], axis=1) for r in dcw],
                axis=0)
            out_ref[j] = jnp.concatenate([common, dmeta_ref[:, j * TB:(j + 1) * TB], _pad_rows(cw)], axis=0)

    return pl.pallas_call(
        body, name="pack_small_grads", out_shape=jax.ShapeDtypeStruct((NDEV, SROWS, TB), F32),
    )(dg_norm, dg_final, dga_p, dgc_p, dcw_p, db_b, dmeta, loss)


def _adamw_small(own, land, params):
    flat = [a for n in SMALL for a in params[n]]

    def body(*refs):
        own_ref, land_ref = refs[:2]
        ins = refs[2:2 + 3 * len(SMALL)]
        outs = refs[2 + 3 * len(SMALL):]
        g = _pick_slab(0, own_ref, land_ref, slice(0, SROWS))
        for j in range(1, NDEV):
            g = g + _pick_slab(j, own_ref, land_ref, slice(0, SROWS))
        grads = dict(
            norm_g=_as_line(g[0:8]), final_norm_g=_as_line(g[8:16]), attn_norm_g=_as_line(g[16:20]),
            conv_norm_g=_as_line(g[24:28]), b_f=g[32:33, :H], meta=g[40:56], conv_w=g[56:59, :DH][None])
        for i, n in enumerate(SMALL):
            w_ref, m_ref, v_ref = ins[3 * i:3 * i + 3]
            d, mn, vn = _adamw(w_ref[...], grads[n], m_ref[...], v_ref[...])
            for o_ref, val in zip(outs[4 * i:4 * i + 4], (grads[n], d, mn, vn)):
                o_ref[...] = val
        outs[-1][...] = g[32:33, H:H + 1]

    shapes = [jax.ShapeDtypeStruct(params[n][0].shape, F32) for n in SMALL for _ in range(4)]
    res = pl.pallas_call(
        body, name="adamw_small", out_shape=shapes + [jax.ShapeDtypeStruct((1, 1), F32)],
    )(own, land, *flat)
    return {n: res[4 * i:4 * i + 4] for i, n in enumerate(SMALL)}, res[-1]


def kernel(x, meta, norm_g, w_in, b_f, conv_w, attn_norm_g, conv_norm_g, w_out, final_norm_g, loss_target, m_meta, m_norm_g, m_w_in, m_b_f, m_conv_w, m_attn_norm_g, m_conv_norm_g, m_w_out, m_final_norm_g, v_meta, v_norm_g, v_w_in, v_b_f, v_conv_w, v_attn_norm_g, v_conv_norm_g, v_w_out, v_final_norm_g):
    seq = x.shape[1]
    L = seq + TB
    assert x.shape == (1, seq, D) and L % TT == 0 and w_in.shape == (1, D, WSH)
    x2 = x[0]
    tgt = loss_target[0]

    w_in_slab = jnp.pad(w_in[0].T, ((0, WSHP - WSH), (0, 0))).astype(BF16)
    w_out_slab = w_out[0].astype(BF16)
    meta_slab = jnp.concatenate([meta, _tile_rows(conv_w[0], 3, DH)], axis=0)
    wout_flight = _split_start(w_out_slab, "gather_w_out_start", per_peer=False)
    w_all, small_all = _all_gather([w_in_slab, meta_slab], "gather_w_in")

    w_t, meta_full, cw_b, ga_b, gcn_b = _unshard_w_in(w_all, small_all, attn_norm_g, conv_norm_g)

    u, proj_t, f_t, ktok, vtok = _inproj_fwd(x2, meta_full, norm_g, w_t, L, after=wout_flight[4])
    cq, kaug, sg = _fgate_fwd(f_t, b_f.reshape(H, 1), ktok, L)
    o_t, lse = _attn_fwd(proj_t, kaug, cq, L)

    w_out_own, w_out_land = _split_wait(wout_flight, o_t, "gather_w_out_wait", per_peer=False)
    w_out_full = _unshard_w_out(w_out_own, w_out_land)
    dout, dmix_t, dw_out, loss_part, dg_final = _outproj(
        o_t, proj_t, cw_b, ga_b, gcn_b, w_out_full, x2, meta_full, final_norm_g.reshape(1, D), tgt, L)
    dwo_flight = _split_start(dw_out.reshape(NDEV, D // NDEV, D), "exchange_dw_out_start", per_peer=True)
    do_t, dd, dg5_t, dga_p, dgc_p, dcw_p = _gate_bwd(dmix_t, o_t, proj_t, cw_b, ga_b, gcn_b, L, after=dwo_flight[4])
    dq_t, dk_t, dv_t, dck, dcq = _attn_bwd(proj_t, kaug, vtok, do_t, lse, dd, cq, L)
    df_t, db_f = _fgate_bwd(dcq, dck, sg, L)
    dw_main, dw_f = _inproj_bwd_w(u, dq_t, dk_t, dv_t, dg5_t, df_t, L)
    dwi_parts = _shard_w_in_grads(dw_main, dw_f)
    dwi_chip = _pair_sum(dwi_parts, _pair_exchange(dwi_parts, "exchange_dw_in_pair"))
    dwi_flight = _split_start(dwi_chip, "exchange_dw_in_start", per_peer=True, chips=True)
    grad_x, dmeta, dg_norm = _inproj_bwd_x(
        w_t, dq_t, dk_t, dv_t, dg5_t, df_t, dout, x2, meta_full, norm_g, L, after=dwi_flight[4])
    small_parts = _pack_small_grads(dg_norm, dg_final, dga_p, dgc_p, dcw_p, db_f, dmeta, loss_part)
    small_flight = _split_start(small_parts, "exchange_small_start", per_peer=True)
    dwo_own, dwo_land = _split_wait(dwo_flight, small_flight[4], "exchange_dw_out_wait", per_peer=True)
    dwi_own, dwi_land = _split_wait(dwi_flight, dwo_land, "exchange_dw_in_wait", per_peer=True, chips=True)

    big_out = _adamw_big(dwi_own, dwi_land, dwo_own, dwo_land,
                         w_in[0].T, m_w_in[0].T, v_w_in[0].T, w_out, m_w_out, v_w_out)
    g_w_in, d_w_in, nm_w_in, nv_w_in = [a.T[None] for a in big_out[:4]]
    g_w_out, d_w_out, nm_w_out, nv_w_out = big_out[4:]
    sm_own, sm_land = _split_wait(small_flight, big_out[4], "exchange_small_wait", per_peer=True)
    line = lambda a: a.reshape(1, D)
    small, loss = _adamw_small(sm_own, sm_land, dict(
        norm_g=(norm_g, m_norm_g, v_norm_g),
        final_norm_g=(line(final_norm_g), line(m_final_norm_g), line(v_final_norm_g)),
        attn_norm_g=(attn_norm_g, m_attn_norm_g, v_attn_norm_g),
        conv_norm_g=(conv_norm_g, m_conv_norm_g, v_conv_norm_g),
        b_f=(b_f, m_b_f, v_b_f), meta=(meta, m_meta, v_meta), conv_w=(conv_w, m_conv_w, v_conv_w)))
    small["final_norm_g"] = [a.reshape(D) for a in small["final_norm_g"]]
    order = ("meta", "norm_g", "w_in", "b_f", "conv_w", "attn_norm_g", "conv_norm_g", "w_out", "final_norm_g")
    groups = []
    for k, (wi, wo) in enumerate(((g_w_in, g_w_out), (d_w_in, d_w_out), (nm_w_in, nm_w_out), (nv_w_in, nv_w_out))):
        d = dict({n: small[n][k] for n in SMALL}, w_in=wi, w_out=wo)
        groups.append([d[n] for n in order])
    return (loss[0, 0], grad_x[None], *groups[0], *groups[1], *groups[2], *groups[3])
```

```python
import jax
import jax.numpy as jnp
from jax import lax
from jax.experimental import pallas as pl
from jax.experimental.pallas import tpu as pltpu

F32 = jnp.float32
BF16 = jnp.bfloat16

D = 1024
DA = 512
H = 8
DH = 64
NM = 16
TB = 128
P0 = TB - NM
TT = 3 * TB
HG = 8
NDEV = 8
NSEC = 8
DF = 16
DPROJ = NSEC * DA + DF
WSH = 513
WSHP = 528
WROWS = WSHP + D // NDEV
SROWS = 64
EPS = 1e-6
NEG = -1e30
LOG2E = 1.4426950408889634
LN2 = 0.6931471805599453
QSCALE = DH ** -0.5 * LOG2E
KA = 128
CB = 256
VMEM_LIMIT = 56 * 1024 * 1024

ADAM_LR = 0.001
ADAM_B1 = 0.9
ADAM_B2 = 0.999
ADAM_EPS = 1e-08
ADAM_WD = 0.01
ADAM_STEP = 10

NT_DIMS = (((1,), (1,)), ((), ()))
TN_DIMS = (((0,), (0,)), ((), ()))
MESH = pl.DeviceIdType.MESH


def _params(n_axes=1, vmem=VMEM_LIMIT):
    return pltpu.CompilerParams(dimension_semantics=("arbitrary",) * n_axes, vmem_limit_bytes=vmem)


def _dot(a, b, dims=None):
    if dims is None:
        return jnp.dot(a, b, preferred_element_type=F32)
    return lax.dot_general(a, b, dims, preferred_element_type=F32)


def _my_place():
    return lax.axis_index("x"), lax.axis_index("y"), lax.axis_index("c")


def _all_gather(xs, name):
    n = len(xs)

    def body(*refs):
        x_refs, out_refs = refs[:n], refs[n:2 * n]
        send_sems, recv_sems, local_sems = refs[2 * n:]
        mx, my, mc = _my_place()

        def across(px, py, pc, axis_a):
            flip_x = pc if axis_a else 1 - pc
            return (px + flip_x) % 2, (py + 1 - flip_x) % 2, pc

        def idx(p):
            return 4 * p[0] + 2 * p[1] + p[2]

        me, sib = (mx, my, mc), (mx, my, 1 - mc)
        a_nbr, b_nbr = across(*me, True), across(*me, False)
        diag = across(*b_nbr, True)
        sib_a, sib_b = across(*sib, True), across(*sib, False)
        sib_diag = across(*sib_b, True)

        waits = []
        for t in range(n):
            out_ref = out_refs[t]

            def copy(k, block, to, src=None, out_ref=out_ref, t=t):
                return pltpu.make_async_remote_copy(
                    src_ref=out_ref.at[idx(block)] if src is None else src, dst_ref=out_ref.at[idx(block)],
                    send_sem=send_sems.at[7 * t + k], recv_sem=recv_sems.at[7 * t + k],
                    device_id=to, device_id_type=MESH)

            mine = pltpu.make_async_copy(x_refs[t], out_ref.at[idx(me)], local_sems.at[t])
            mine.start()
            started = [copy(0, me, sib, src=x_refs[t]), copy(1, me, a_nbr, src=x_refs[t]),
                       copy(2, me, b_nbr, src=x_refs[t])]
            for cp in started:
                cp.start()
            waits.append((copy, mine, started))
        relays = ((1, a_nbr, ((3, b_nbr), (4, sib))), (2, b_nbr, ((5, sib),)), (3, diag, ((6, sib),)))
        for landed, block, onward in relays:
            for copy, _, started in waits:
                copy(landed, block, me).wait_recv()
                for k, to in onward:
                    started.append(copy(k, block, to))
                    started[-1].start()
        for copy, mine, started in waits:
            for k, block in ((0, sib), (4, sib_a), (5, sib_b), (6, sib_diag)):
                copy(k, block, me).wait_recv()
            for cp in started:
                cp.wait_send()
            mine.wait()

    any_spec = pl.BlockSpec(memory_space=pl.ANY)
    return pl.pallas_call(
        body, name=name,
        out_shape=[jax.ShapeDtypeStruct((NDEV,) + x.shape, x.dtype) for x in xs],
        in_specs=[any_spec] * n, out_specs=[any_spec] * n,
        scratch_shapes=[pltpu.SemaphoreType.DMA((7 * n,)), pltpu.SemaphoreType.DMA((7 * n,)),
                        pltpu.SemaphoreType.DMA((n,))],
    )(*xs)


_HBM = pl.BlockSpec(memory_space=pltpu.HBM)
_UNREAD = pl.BlockSpec(memory_space=pl.ANY)
_SEM = pl.BlockSpec(memory_space=pltpu.SEMAPHORE)
_EFFECT = pltpu.SideEffectType.DATAFLOW_SIDE_EFFECTING


def _peer_of(m, place):
    mx, my, mc = place
    return ((1 - mx) if m & 4 else mx, (1 - my) if m & 2 else my, (1 - mc) if m & 1 else mc)


def _party(chips):
    if chips:
        return (lambda p: 2 * p[0] + p[1]), (2, 4, 6)
    return (lambda p: 4 * p[0] + 2 * p[1] + p[2]), tuple(range(1, NDEV))


def _split_copies(src_ref, land_ref, send_sems, recv_sems, per_peer, incoming, chips):
    place = _my_place()
    slot, masks = _party(chips)
    me = slot(place)
    out = []
    for k, m in enumerate(masks):
        there = _peer_of(m, place)
        peer = slot(there)
        src = (src_ref.at[me] if incoming else src_ref.at[peer]) if per_peer else src_ref
        out.append(pltpu.make_async_remote_copy(
            src_ref=src, dst_ref=land_ref.at[peer if incoming else me],
            send_sem=send_sems.at[k], recv_sem=recv_sems.at[k], device_id=there, device_id_type=MESH))
    return out


def _split_start(src, name, per_peer, chips=False):
    slab = src.shape[1:] if per_peer else src.shape
    n = len(_party(chips)[1])

    def body(src_ref, land_ref, send_sems, recv_sems, src_thru, land_thru, token):
        for cp in _split_copies(src_ref, land_ref, send_sems, recv_sems, per_peer, False, chips):
            cp.start()
        token[...] = jnp.zeros_like(token)

    return pl.pallas_call(
        body, name=name,
        out_shape=(pltpu.SemaphoreType.DMA((n,)), pltpu.SemaphoreType.DMA((n,)),
                   pltpu.HBM(src.shape, src.dtype), pltpu.HBM((n + 1,) + slab, src.dtype),
                   jax.ShapeDtypeStruct((8, TB), F32)),
        in_specs=(_HBM, _HBM), out_specs=(_SEM, _SEM, _HBM, _HBM, pl.BlockSpec(memory_space=pltpu.VMEM)),
        input_output_aliases={0: 2, 1: 3},
        compiler_params=pltpu.CompilerParams(has_side_effects=_EFFECT),
    )(pltpu.with_memory_space_constraint(src, pltpu.HBM),
      pltpu.with_memory_space_constraint(lax.empty((n + 1,) + slab, src.dtype), pltpu.HBM))


def _split_wait(handles, after, name, per_peer, chips=False):
    send_sems, recv_sems, src_thru, land_thru, _ = handles

    def body(src_ref, land_ref, send_sems, recv_sems, after_ref, src_out, land_out):
        for cp in _split_copies(src_ref, land_ref, send_sems, recv_sems, per_peer, False, chips):
            cp.wait_send()
        for cp in _split_copies(src_ref, land_ref, send_sems, recv_sems, per_peer, True, chips):
            cp.wait_recv()

    return pl.pallas_call(
        body, name=name,
        out_shape=(pltpu.HBM(src_thru.shape, src_thru.dtype), pltpu.HBM(land_thru.shape, land_thru.dtype)),
        in_specs=(_HBM, _HBM, _SEM, _SEM, pl.BlockSpec(memory_space=pl.ANY)), out_specs=(_HBM, _HBM),
        input_output_aliases={0: 0, 1: 1},
        compiler_params=pltpu.CompilerParams(has_side_effects=_EFFECT),
    )(src_thru, land_thru, send_sems, recv_sems, after)


def _pick_slab(j, own_ref, land_ref, rows, per_peer=True, chips=False):
    me = _party(chips)[0](_my_place())
    own = (lambda: own_ref[j, rows, :]) if per_peer else (lambda: own_ref[rows, :])
    return lax.cond(me == j, own, lambda: land_ref[j, rows, :])


def _pair_exchange(p, name):
    def body(p_ref, got_ref, send_sems, recv_sems):
        mx, my, mc = _my_place()
        copies = [pltpu.make_async_remote_copy(
            src_ref=p_ref.at[4 + q], dst_ref=got_ref.at[q], send_sem=send_sems.at[q],
            recv_sem=recv_sems.at[q], device_id=(mx, my, 1 - mc), device_id_type=MESH) for q in range(4)]
        for cp in copies:
            cp.start()
        for cp in copies:
            cp.wait_recv()
        for cp in copies:
            cp.wait_send()

    any_spec = pl.BlockSpec(memory_space=pl.ANY)
    return pl.pallas_call(
        body, name=name, out_shape=jax.ShapeDtypeStruct((4,) + p.shape[1:], p.dtype),
        in_specs=[any_spec], out_specs=any_spec,
        scratch_shapes=[pltpu.SemaphoreType.DMA((4,)), pltpu.SemaphoreType.DMA((4,))],
    )(p)


def _pair_sum(p, got):
    rows = p.shape[1]

    def body(p_ref, got_ref, out_ref):
        for q in range(4):
            out_ref[q] = (p_ref[q].astype(F32) + got_ref[q].astype(F32)).astype(BF16)

    blk = lambda n: pl.BlockSpec((n, rows, CB), lambda i: (0, 0, i))
    return pl.pallas_call(
        body, name="pair_sum", grid=(D // CB,), in_specs=[blk(4), blk(4)], out_specs=blk(4),
        out_shape=jax.ShapeDtypeStruct((4, rows, D), BF16), compiler_params=_params(),
    )(p, got)


def _h_block(t, x_ref, meta_ref):
    first = jnp.concatenate([jnp.zeros((P0, D), F32), meta_ref[...]], axis=0)
    return jnp.where(t == 0, first, x_ref[...])


def _x_specs3(tile=lambda j: j):
    return [pl.BlockSpec((TB, D), lambda j: (jnp.maximum(3 * tile(j) - 1, 0), 0)),
            pl.BlockSpec((TB, D), lambda j: (3 * tile(j), 0)),
            pl.BlockSpec((TB, D), lambda j: (3 * tile(j) + 1, 0))]


def _h_tile(j, xa_ref, xb_ref, xc_ref, meta_ref):
    first = jnp.concatenate([jnp.zeros((P0, D), F32), meta_ref[...]], axis=0)
    return jnp.concatenate([jnp.where(j == 0, first, xa_ref[...]), xb_ref[...], xc_ref[...]], axis=0)


def _full_spec(shape):
    return pl.BlockSpec(shape, lambda *_: (0,) * len(shape))


def _sigmoid(z):
    return 1.0 / (1.0 + jnp.exp(-z))


def _lane_tiles_sum(x):
    out = x[:, :TB]
    for i in range(1, x.shape[1] // TB):
        out = out + x[:, i * TB:(i + 1) * TB]
    return out


def _inproj_fwd(x, meta_full, norm_g, w_t, L, after):
    nj = L // TT

    def body(xa_ref, xb_ref, xc_ref, meta_ref, g_ref, w_ref, _, u_ref, proj_ref, f_ref, ktok_ref, vtok_ref):
        hb = _h_tile(pl.program_id(0), xa_ref, xb_ref, xc_ref, meta_ref)
        r = lax.rsqrt(jnp.mean(hb * hb, axis=-1, keepdims=True) + EPS)
        u = (hb * r * g_ref[...]).astype(BF16)
        u_ref[...] = u
        for s in range(NSEC):
            p = _dot(u, w_ref[s * DA:(s + 1) * DA, :], NT_DIMS)
            if s == 0:
                p = p * QSCALE
            if s in (1, 2):
                tok_ref = ktok_ref if s == 1 else vtok_ref
                for h in range(H):
                    tok_ref[h] = p[:, h * DH:(h + 1) * DH].astype(BF16)
            proj_ref[s * DA:(s + 1) * DA, :] = p.T.astype(BF16)
        f_ref[...] = _dot(w_ref[NSEC * DA:DPROJ, :], u, NT_DIMS)[:H]

    return pl.pallas_call(
        body, name="inproj_fwd", grid=(nj,),
        in_specs=_x_specs3() + [_full_spec((NM, D)), _full_spec((1, D)), _full_spec((DPROJ, D)), _UNREAD],
        out_specs=[
            pl.BlockSpec((TT, D), lambda t: (t, 0)),
            pl.BlockSpec((NSEC * DA, TT), lambda t: (0, t)),
            pl.BlockSpec((H, TT), lambda t: (0, t)),
            pl.BlockSpec((H, TT, DH), lambda t: (0, t, 0)),
            pl.BlockSpec((H, TT, DH), lambda t: (0, t, 0)),
        ],
        out_shape=[
            jax.ShapeDtypeStruct((L, D), BF16),
            jax.ShapeDtypeStruct((NSEC * DA, L), BF16),
            jax.ShapeDtypeStruct((H, L), F32),
            jax.ShapeDtypeStruct((H, L, DH), BF16),
            jax.ShapeDtypeStruct((H, L, DH), BF16),
        ],
        compiler_params=_params(),
    )(x, x, x, meta_full, norm_g, w_t, after)


def _split3(x):
    hi = x.astype(BF16).astype(F32)
    r = x - hi
    mid = r.astype(BF16).astype(F32)
    return hi, mid, (r - mid).astype(BF16).astype(F32)


def _bias_rows(bias):
    one = jnp.ones((1, TT), F32)
    zero = jnp.zeros((1, TT), F32)
    parts = [zero] * 3 if bias is None else list(_split3(bias))
    return jnp.concatenate([one] * 3 + parts + [zero] * (DF - 6), axis=0).astype(BF16)


def _fgate_fwd(f_t, b_col, ktok, L):
    nb = L // TB

    def body(f_ref, b_ref, ktok_ref, cq_ref, kaug_ref, sg_ref, bias_scr):
        h = pl.program_id(0)

        @pl.when(h == 0)
        def _():
            z = f_ref[...] + b_ref[...]
            idx = lax.broadcasted_iota(jnp.int32, (H, L), 1)
            real = idx >= P0
            lf = jnp.where(real, jnp.minimum(z, 0.0) - jnp.log1p(jnp.exp(-jnp.abs(z))), 0.0)
            sg_ref[...] = jnp.where(real, 1.0 / (1.0 + jnp.exp(z)), 0.0)
            c = lf
            s = 1
            while s < L:
                c = c + jnp.where(idx >= s, pltpu.roll(c, s, 1), 0.0)
                s *= 2
            c = c * LOG2E
            for hh in range(H):
                cq_ref[hh] = c[hh:hh + 1, :]
            for part, val in enumerate(_split3(-jnp.where(real, c, -NEG))):
                for hh in range(H):
                    bias_scr[part * H + hh] = val[hh:hh + 1, :]

        lane = lax.broadcasted_iota(jnp.int32, (TB, KA), 1)
        head = jnp.zeros((DH, TB), F32)
        tail = jnp.concatenate([jnp.ones((3, TB), F32), jnp.zeros((KA - DH - 6, TB), F32)], axis=0)
        for b in range(nb):
            blk = slice(b * TB, (b + 1) * TB)
            cols = jnp.concatenate(
                [head] + [bias_scr[part * H + h, :, blk] for part in range(3)] + [tail], axis=0).T
            k = jnp.concatenate([ktok_ref[0, blk, :].astype(F32), jnp.zeros((TB, KA - DH), F32)], axis=1)
            kaug_ref[0, blk, :] = jnp.where(lane < DH, k, cols).astype(BF16)

    return pl.pallas_call(
        body, name="fgate_fwd", grid=(H,),
        in_specs=[_full_spec((H, L)), _full_spec((H, 1)), pl.BlockSpec((1, L, DH), lambda h: (h, 0, 0))],
        out_specs=[_full_spec((H, 1, L)), pl.BlockSpec((1, L, KA), lambda h: (h, 0, 0)), _full_spec((H, L))],
        out_shape=[
            jax.ShapeDtypeStruct((H, 1, L), F32),
            jax.ShapeDtypeStruct((H, L, KA), BF16),
            jax.ShapeDtypeStruct((H, L), F32),
        ],
        scratch_shapes=[pltpu.VMEM((3 * H, 1, L), F32)],
        compiler_params=_params(),
    )(f_t, b_col, ktok)


def _causal_mask():
    r = lax.broadcasted_iota(jnp.int32, (TT, TT), 0)
    c = lax.broadcasted_iota(jnp.int32, (TT, TT), 1)
    return r <= c


def _attn_fwd(proj_t, kaug, cq, L):
    nq = L // TT

    def body(q_ref, qn_ref, kaug_ref, v_ref, cq_ref, o_ref, lse_ref,
             qa_scr, s_scr, cmax_scr, m_scr, p_scr, alpha_scr, acc_scr):
        j = pl.program_id(0)
        rows = [slice(g * DH, (g + 1) * DH) for g in range(HG)]
        ones = jnp.ones((DF, TT), BF16)

        def load_queries(ref):
            for g in range(HG):
                qa_scr[g] = jnp.concatenate(
                    [ref[rows[g], :], _bias_rows(None), jnp.zeros((KA - DH - DF, TT), BF16)], axis=0)

        def scores(kt, masked):
            k_off = pl.multiple_of(kt * TT, TT)
            for g in range(HG):
                s = _dot(kaug_ref[g, pl.ds(k_off, TT), :], qa_scr[g])
                if masked:
                    s = jnp.where(_causal_mask(), s, NEG)
                s_scr[g] = s
                cmax_scr[g] = jnp.max(s, axis=0, keepdims=True)

        def softmax():
            for g in range(HG):
                m_old = m_scr[g]
                m_new = jnp.maximum(m_old, cmax_scr[g])
                alpha_scr[g] = jnp.exp2(m_old - m_new)
                p_scr[g] = jnp.exp2(s_scr[g] - m_new).astype(BF16)
                m_scr[g] = m_new

        def weighted_sum(kt):
            k_off = pl.multiple_of(kt * TT, TT)
            for g in range(HG):
                v1 = jnp.concatenate([v_ref[rows[g], pl.ds(k_off, TT)], ones], axis=0)
                acc_scr[g] = alpha_scr[g] * acc_scr[g] + _dot(v1, p_scr[g])

        @pl.when(j == 0)
        def _():
            load_queries(q_ref)
            scores(0, True)

        m_scr[...] = jnp.full_like(m_scr, NEG)
        acc_scr[...] = jnp.zeros_like(acc_scr)

        @pl.when(j >= 1)
        def _():
            softmax()
            scores(j - 1, False)

        def step(i, c):
            weighted_sum(j - i + 1)
            softmax()
            scores(j - i - 1, False)
            return c

        lax.fori_loop(1, j, step, 0)

        def drain(second_last, next_tile):
            if second_last:
                weighted_sum(1)
            softmax()
            if next_tile:
                load_queries(qn_ref)
                scores(j + 1, True)
            weighted_sum(0)

        @pl.when(j == 0)
        def _():
            drain(False, nq > 1)

        @pl.when((j >= 1) & (j < nq - 1))
        def _():
            drain(True, True)

        @pl.when((j >= 1) & (j == nq - 1))
        def _():
            drain(True, False)

        for g in range(HG):
            l = acc_scr[g, DH:DH + 1, :]
            o_ref[rows[g], :] = acc_scr[g, :DH, :] * (1.0 / l)
            lse_ref[g] = m_scr[g] + jnp.log2(l) + cq_ref[g]

    assert HG == H
    return pl.pallas_call(
        body, name="attn_fwd", grid=(nq,),
        in_specs=[
            pl.BlockSpec((DA, TT), lambda j: (0, j)),
            pl.BlockSpec((DA, TT), lambda j: (0, jnp.minimum(j + 1, nq - 1))),
            pl.BlockSpec((H, L, KA), lambda j: (0, 0, 0)),
            pl.BlockSpec((DA, L), lambda j: (2, 0)),
            pl.BlockSpec((H, 1, TT), lambda j: (0, 0, j)),
        ],
        out_specs=[
            pl.BlockSpec((DA, TT), lambda j: (0, j)),
            pl.BlockSpec((H, 1, TT), lambda j: (0, 0, j)),
        ],
        out_shape=[jax.ShapeDtypeStruct((DA, L), F32), jax.ShapeDtypeStruct((H, 1, L), F32)],
        scratch_shapes=[pltpu.VMEM((HG, KA, TT), BF16), pltpu.VMEM((HG, TT, TT), F32), pltpu.VMEM((HG, 1, TT), F32),
                        pltpu.VMEM((HG, 1, TT), F32), pltpu.VMEM((HG, TT, TT), BF16), pltpu.VMEM((HG, 1, TT), F32),
                        pltpu.VMEM((HG, DH + DF, TT), F32)],
        compiler_params=_params(),
    )(proj_t, proj_t, kaug, proj_t, cq)


def _gate_group(rows, o_ref, za_ref, gb_ref, gc_ref, xc_ref, zc_ref, gcp_ref, xcp_ref, cw_ref, ga_ref, gcn_ref, first):
    n_rep = TT // TB
    f32 = lambda r: r[rows, :].astype(F32)
    o, za, gb, gc, xc, zc = o_ref[rows, :], f32(za_ref), f32(gb_ref), f32(gc_ref), f32(xc_ref), f32(zc_ref)
    a = gc * xc
    a_prev = jnp.where(first, 0.0, f32(gcp_ref) * f32(xcp_ref))
    full = jnp.concatenate([a_prev, a], axis=1)
    a1 = pltpu.roll(full, 1, 1)[:, TB:]
    a2 = pltpu.roll(full, 2, 1)[:, TB:]
    w0 = jnp.tile(cw_ref[0, rows, :], (1, n_rep))
    w1 = jnp.tile(cw_ref[1, rows, :], (1, n_rep))
    w2 = jnp.tile(cw_ref[2, rows, :], (1, n_rep))
    cv = w0 * a2 + w1 * a1 + w2 * a
    e = gb * cv
    rc = lax.rsqrt(jnp.mean(e * e, axis=0, keepdims=True) + EPS)
    ec = e * rc
    ra = lax.rsqrt(jnp.mean(o * o, axis=0, keepdims=True) + EPS)
    oa = o * ra
    g_a = jnp.tile(ga_ref[rows, :], (1, n_rep))
    g_c = jnp.tile(gcn_ref[rows, :], (1, n_rep))
    sa = _sigmoid(za)
    sc = _sigmoid(zc)
    return dict(o=o, za=za, gb=gb, gc=gc, xc=xc, zc=zc, a=a, a1=a1, a2=a2, w0=w0, w1=w1, w2=w2, cv=cv, e=e,
                rc=rc, ec=ec, ra=ra, oa=oa, g_a=g_a, g_c=g_c, sa=sa, sc=sc)


def _gate_specs(tile):
    def sec(s):
        return pl.BlockSpec((DA, TT), lambda i: (s, tile(i)))

    def halo(s):
        return pl.BlockSpec((DA, TB), lambda i: (s, jnp.maximum(3 * tile(i) - 1, 0)))

    return [pl.BlockSpec((DA, TT), lambda i: (0, tile(i))), sec(3), sec(4), sec(5), sec(6), sec(7), halo(5), halo(6),
            _full_spec((3, DA, TB)), _full_spec((DA, TB)), _full_spec((DA, TB))]


def _gate_outproj(o_t, proj_t, cw_b, ga_b, gcn_b, w_out, x, meta_full, fng, target, L):
    nj = L // TT
    rp = NM
    n_bwd = 8
    cb = D // 4
    assert P0 % rp == 0 and TB % rp == 0 and (TT // rp) % n_bwd == 0 and H == n_bwd

    def body(o_ref, za_ref, gb_ref, gc_ref, xcv_ref, zc_ref, gcp_ref, xcp_ref, cw_ref, ga_ref, gcn_ref,
             o2_ref, za2_ref, gb2_ref, gc2_ref, xcv2_ref, zc2_ref, gcp2_ref, xcp2_ref,
             w_ref, xa_ref, xb_ref, xc_ref, meta_ref, g_ref, ta_ref, tb_ref, tc_ref,
             dout_ref, dwb_ref, loss_ref, dg_ref, do_ref, dd_ref, dg5_ref, dga_ref, dgc_ref, dcw_ref,
             dw_ref, o_scr, db_new, db_old, mix_new, mix_old, dmix_new, dmix_old, sq_acc, dg_acc, carry_ref):
        t = pl.program_id(0)
        first_a = t == nj - 1
        first_c = t == nj + 1

        def gate_rows(h):
            rows = slice(h * DH, (h + 1) * DH)
            g = _gate_group(rows, o_ref, za_ref, gb_ref, gc_ref, xcv_ref, zc_ref, gcp_ref, xcp_ref,
                            cw_ref, ga_ref, gcn_ref, first_a)
            mix_new[rows, :] = (g["oa"] * g["g_a"] * (g["za"] * g["sa"])).astype(BF16)
            mix_new[DA + h * DH:DA + (h + 1) * DH, :] = (g["ec"] * g["g_c"] * (g["zc"] * g["sc"])).astype(BF16)

        def loss_rows(c):
            blk = c // (TB // rp)
            rows, out_rows = pl.ds((c % (TB // rp)) * rp, rp), pl.ds(c * rp, rp)
            h = (xa_ref, xb_ref, xc_ref)[blk][rows, :]
            if blk == 0:
                first = meta_ref[...] if c == P0 // rp else jnp.zeros((rp, D), F32)
                h = jnp.where(first_a, first, h)
            o = o_scr[out_rows, :] + h
            r = lax.rsqrt(jnp.mean(o * o, axis=-1, keepdims=True) + EPS)
            orn = o * r
            g = g_ref[...]
            diff = orn * g - (ta_ref, tb_ref, tc_ref)[blk][rows, :]
            if blk == 0:
                diff = diff * jnp.where(first_a, 0.0, 1.0)
            gy = diff * (g * (1.0 / D))
            dout = r * (gy - orn * jnp.mean(gy * orn, axis=-1, keepdims=True))
            dout_ref[out_rows, :] = dout
            db_new[out_rows, :] = dout.astype(BF16)
            sq, go = diff * diff, diff * orn
            sq_acc[...] += sq[:8] + sq[8:]
            dg_acc[...] += go[:8] + go[8:]

        def backward_cols(n):
            if n < 4:
                cols = slice(n * cb, (n + 1) * cb)
                dmix_new[cols, :] = _dot(db_old[...], w_ref[cols, :], NT_DIMS).T.astype(BF16)
            else:
                cols = slice((n - 4) * cb, (n - 3) * cb)
                dw_ref[:, cols] += _dot(mix_old[...], db_old[:, cols])

        def gate_bwd_rows(h):
            rows = slice(h * DH, (h + 1) * DH)
            sec = lambda s: slice(s * DA + h * DH, s * DA + (h + 1) * DH)
            g = _gate_group(rows, o2_ref, za2_ref, gb2_ref, gc2_ref, xcv2_ref, zc2_ref, gcp2_ref, xcp2_ref,
                            cw_ref, ga_ref, gcn_ref, first_c)
            o, za, gb, gc, xc, zc, sa, sc = (g[n] for n in ("o", "za", "gb", "gc", "xc", "zc", "sa", "sc"))
            dya = dmix_old[rows, :].astype(F32)
            dyc = dmix_old[DA + h * DH:DA + (h + 1) * DH, :].astype(F32)

            dn = dya * (za * sa)
            dg5_ref[sec(0), :] = (dya * (g["oa"] * g["g_a"]) * (sa * (1.0 + za * (1.0 - sa)))).astype(BF16)
            dga_ref[rows, :] += _lane_tiles_sum(dn * g["oa"])
            dng = dn * g["g_a"]
            mean_a = jnp.mean(dng * g["oa"], axis=0, keepdims=True)
            do = (dng - g["oa"] * mean_a) * g["ra"]
            do_ref[rows, :] = do.astype(BF16)
            dd_ref[h] = jnp.sum(do * o, axis=0, keepdims=True)

            dnc = dyc * (zc * sc)
            dg5_ref[sec(4), :] = (dyc * (g["ec"] * g["g_c"]) * (sc * (1.0 + zc * (1.0 - sc)))).astype(BF16)
            dgc_ref[rows, :] += _lane_tiles_sum(dnc * g["ec"])
            dncg = dnc * g["g_c"]
            mean_c = jnp.mean(dncg * g["ec"], axis=0, keepdims=True)
            de = (dncg - g["ec"] * mean_c) * g["rc"]
            dg5_ref[sec(1), :] = (de * g["cv"]).astype(BF16)
            dcv = de * gb
            full = jnp.concatenate([dcv, carry_ref[rows, :]], axis=1)
            d1 = pltpu.roll(full, TT + TB - 1, 1)[:, :TT]
            d2 = pltpu.roll(full, TT + TB - 2, 1)[:, :TT]
            carry_ref[rows, :] = dcv[:, :TB]
            da = g["w2"] * dcv + g["w1"] * d1 + g["w0"] * d2
            dg5_ref[sec(2), :] = (da * xc).astype(BF16)
            dg5_ref[sec(3), :] = (da * gc).astype(BF16)
            dcw_ref[0, rows, :] += _lane_tiles_sum(dcv * g["a2"])
            dcw_ref[1, rows, :] += _lane_tiles_sum(dcv * g["a1"])
            dcw_ref[2, rows, :] += _lane_tiles_sum(dcv * g["a"])

        def step(a, b, c):
            half = H // 2
            for h in range(H):
                if a:
                    gate_rows(h)
                if c and h < half:
                    gate_bwd_rows(h)
                if b and h % 2 == 1:
                    backward_cols(h // 2)
            if a:
                o_scr[...] = _dot(mix_new[...], w_ref[...], TN_DIMS)
            per = TT // rp // n_bwd
            for k in range(n_bwd):
                if a:
                    for piece in range(per * k, per * (k + 1)):
                        loss_rows(piece)
                if c and k % 2 == 0:
                    gate_bwd_rows(half + k // 2)
                if b and k % 2 == 1:
                    backward_cols(n_bwd // 2 + k // 2)
            if a:
                db_old[...] = db_new[...]
                mix_old[...] = mix_new[...]
            if b:
                dmix_old[...] = dmix_new[...]

        @pl.when(t == 0)
        def _():
            dw_ref[...] = jnp.zeros_like(dw_ref)
            sq_acc[...] = jnp.zeros_like(sq_acc)
            dg_acc[...] = jnp.zeros_like(dg_acc)
            carry_ref[...] = jnp.zeros_like(carry_ref)
            dga_ref[...] = jnp.zeros_like(dga_ref)
            dgc_ref[...] = jnp.zeros_like(dgc_ref)
            dcw_ref[...] = jnp.zeros_like(dcw_ref)
            step(True, False, False)

        @pl.when(t == 1)
        def _():
            step(True, True, False)

        @pl.when((t >= 2) & (t < nj))
        def _():
            step(True, True, True)

        @pl.when(t == nj)
        def _():
            step(False, True, True)
            dwb_ref[...] = dw_ref[...].astype(BF16)
            loss_ref[...] = jnp.sum(sq_acc[...], keepdims=True) * (0.5 / D)
            dg_ref[...] = jnp.sum(dg_acc[...], axis=0, keepdims=True) * (1.0 / D)

        @pl.when(t == nj + 1)
        def _():
            step(False, False, True)

    assert nj >= 2
    tile_a = lambda t: jnp.clip(nj - 1 - t, 0, nj - 1)
    tile_c = lambda t: jnp.clip(nj + 1 - t, 0, nj - 1)
    at_c = lambda shape: pl.BlockSpec(shape, lambda t: (0,) * (len(shape) - 1) + (tile_c(t),))
    return pl.pallas_call(
        body, name="gate_outproj", grid=(nj + 2,),
        in_specs=_gate_specs(tile_a) + _gate_specs(tile_c)[:8] + [_full_spec((D, D))] + _x_specs3(tile_a)
                 + [_full_spec((NM, D)), _full_spec((1, D))] + _x_specs3(tile_a),
        out_specs=[pl.BlockSpec((TT, D), lambda t: (tile_a(t), 0)), _full_spec((D, D)), _full_spec((1, 1)),
                   _full_spec((1, D)), at_c((DA, TT)), at_c((H, 1, TT)), at_c((5 * DA, TT)),
                   _full_spec((DA, TB)), _full_spec((DA, TB)), _full_spec((3, DA, TB))],
        out_shape=[jax.ShapeDtypeStruct((L, D), F32), jax.ShapeDtypeStruct((D, D), BF16),
                   jax.ShapeDtypeStruct((1, 1), F32), jax.ShapeDtypeStruct((1, D), F32),
                   jax.ShapeDtypeStruct((DA, L), BF16),
                   jax.ShapeDtypeStruct((H, 1, L), F32),
                   jax.ShapeDtypeStruct((5 * DA, L), BF16),
                   jax.ShapeDtypeStruct((DA, TB), F32),
                   jax.ShapeDtypeStruct((DA, TB), F32),
                   jax.ShapeDtypeStruct((3, DA, TB), F32)],
        scratch_shapes=[pltpu.VMEM((D, D), F32), pltpu.VMEM((TT, D), F32), pltpu.VMEM((TT, D), BF16),
                        pltpu.VMEM((TT, D), BF16), pltpu.VMEM((D, TT), BF16), pltpu.VMEM((D, TT), BF16),
                        pltpu.VMEM((D, TT), BF16), pltpu.VMEM((D, TT), BF16),
                        pltpu.VMEM((8, D), F32), pltpu.VMEM((8, D), F32), pltpu.VMEM((DA, TB), F32)],
        compiler_params=_params(),
    )(o_t, proj_t, proj_t, proj_t, proj_t, proj_t, proj_t, proj_t, cw_b, ga_b, gcn_b,
      o_t, proj_t, proj_t, proj_t, proj_t, proj_t, proj_t, proj_t,
      w_out, x, x, x, meta_full, fng, target, target, target)


def _outproj(o_t, proj_t, cw_b, ga_b, gcn_b, w_out, x, meta_full, fng, target, L):
    nj = L // TT
    rp = NM
    n_bwd = 8
    cb = D // 4
    assert P0 % rp == 0 and TB % rp == 0 and (TT // rp) % n_bwd == 0 and H == n_bwd

    def body(o_ref, za_ref, gb_ref, gc_ref, xcv_ref, zc_ref, gcp_ref, xcp_ref, cw_ref, ga_ref, gcn_ref,
             w_ref, xa_ref, xb_ref, xc_ref, meta_ref, g_ref, ta_ref, tb_ref, tc_ref,
             dout_ref, dmix_ref, dwb_ref, loss_ref, dg_ref,
             dw_ref, o_scr, db_new, db_old, mix_new, mix_old, sq_acc, dg_acc):
        t = pl.program_id(0)

        def gate_rows(h):
            rows = slice(h * DH, (h + 1) * DH)
            g = _gate_group(rows, o_ref, za_ref, gb_ref, gc_ref, xcv_ref, zc_ref, gcp_ref, xcp_ref,
                            cw_ref, ga_ref, gcn_ref, t == 0)
            mix_new[rows, :] = (g["oa"] * g["g_a"] * (g["za"] * g["sa"])).astype(BF16)
            mix_new[DA + h * DH:DA + (h + 1) * DH, :] = (g["ec"] * g["g_c"] * (g["zc"] * g["sc"])).astype(BF16)

        def loss_rows(c):
            blk = c // (TB // rp)
            rows, out_rows = pl.ds((c % (TB // rp)) * rp, rp), pl.ds(c * rp, rp)
            h = (xa_ref, xb_ref, xc_ref)[blk][rows, :]
            if blk == 0:
                first = meta_ref[...] if c == P0 // rp else jnp.zeros((rp, D), F32)
                h = jnp.where(t == 0, first, h)
            o = o_scr[out_rows, :] + h
            r = lax.rsqrt(jnp.mean(o * o, axis=-1, keepdims=True) + EPS)
            orn = o * r
            g = g_ref[...]
            diff = orn * g - (ta_ref, tb_ref, tc_ref)[blk][rows, :]
            if blk == 0:
                diff = diff * jnp.where(t > 0, 1.0, 0.0)
            gy = diff * (g * (1.0 / D))
            dout = r * (gy - orn * jnp.mean(gy * orn, axis=-1, keepdims=True))
            dout_ref[out_rows, :] = dout
            db_new[out_rows, :] = dout.astype(BF16)
            sq, go = diff * diff, diff * orn
            sq_acc[...] += sq[:8] + sq[8:]
            dg_acc[...] += go[:8] + go[8:]

        def backward_cols(n):
            if n < 4:
                cols = slice(n * cb, (n + 1) * cb)
                dmix_ref[cols, :] = _dot(db_old[...], w_ref[cols, :], NT_DIMS).T.astype(BF16)
            else:
                cols = slice((n - 4) * cb, (n - 3) * cb)
                dw_ref[:, cols] += _dot(mix_old[...], db_old[:, cols])

        def step(forward, backward):
            if not forward:
                for n in range(n_bwd):
                    backward_cols(n)
                return
            for h in range(H):
                gate_rows(h)
                if backward and h % 2 == 1:
                    backward_cols(h // 2)
            o_scr[...] = _dot(mix_new[...], w_ref[...], TN_DIMS)
            per = TT // rp // n_bwd
            for k in range(n_bwd):
                for c in range(per * k, per * (k + 1)):
                    loss_rows(c)
                if backward and k % 2 == 1:
                    backward_cols(n_bwd // 2 + k // 2)
            db_old[...] = db_new[...]
            mix_old[...] = mix_new[...]

        @pl.when(t == 0)
        def _():
            dw_ref[...] = jnp.zeros_like(dw_ref)
            sq_acc[...] = jnp.zeros_like(sq_acc)
            dg_acc[...] = jnp.zeros_like(dg_acc)
            step(True, False)

        @pl.when((t > 0) & (t < nj))
        def _():
            step(True, True)

        @pl.when(t == nj)
        def _():
            step(False, True)
            dwb_ref[...] = dw_ref[...].astype(BF16)
            loss_ref[...] = jnp.sum(sq_acc[...], keepdims=True) * (0.5 / D)
            dg_ref[...] = jnp.sum(dg_acc[...], axis=0, keepdims=True) * (1.0 / D)

    cur = lambda t: jnp.minimum(t, nj - 1)
    prev = lambda t: jnp.maximum(t - 1, 0)
    return pl.pallas_call(
        body, name="outproj", grid=(nj + 1,),
        in_specs=_gate_specs(cur) + [_full_spec((D, D))] + _x_specs3(cur)
                 + [_full_spec((NM, D)), _full_spec((1, D))] + _x_specs3(cur),
        out_specs=[pl.BlockSpec((TT, D), lambda t: (cur(t), 0)), pl.BlockSpec((D, TT), lambda t: (0, prev(t))),
                   _full_spec((D, D)), _full_spec((1, 1)), _full_spec((1, D))],
        out_shape=[jax.ShapeDtypeStruct((L, D), F32), jax.ShapeDtypeStruct((D, L), BF16),
                   jax.ShapeDtypeStruct((D, D), BF16), jax.ShapeDtypeStruct((1, 1), F32),
                   jax.ShapeDtypeStruct((1, D), F32)],
        scratch_shapes=[pltpu.VMEM((D, D), F32), pltpu.VMEM((TT, D), F32), pltpu.VMEM((TT, D), BF16),
                        pltpu.VMEM((TT, D), BF16), pltpu.VMEM((D, TT), BF16), pltpu.VMEM((D, TT), BF16),
                        pltpu.VMEM((8, D), F32), pltpu.VMEM((8, D), F32)],
        compiler_params=_params(),
    )(o_t, proj_t, proj_t, proj_t, proj_t, proj_t, proj_t, proj_t, cw_b, ga_b, gcn_b,
      w_out, x, x, x, meta_full, fng, target, target, target)


def _gate_bwd(dmix_t, o_t, proj_t, cw_b, ga_b, gcn_b, L, after):
    nj = L // TT

    def body(dmix_ref, o_ref, za_ref, gb_ref, gc_ref, xc_ref, zc_ref, gcp_ref, xcp_ref, cw_ref, ga_ref, gcn_ref, _,
             do_ref, dd_ref, dg5_ref, dga_ref, dgc_ref, dcw_ref, carry_ref):
        i = pl.program_id(0)
        j = nj - 1 - i

        @pl.when(i == 0)
        def _():
            carry_ref[...] = jnp.zeros_like(carry_ref)
            dga_ref[...] = jnp.zeros_like(dga_ref)
            dgc_ref[...] = jnp.zeros_like(dgc_ref)
            dcw_ref[...] = jnp.zeros_like(dcw_ref)

        def group(h, c):
            r0 = pl.multiple_of(h * DH, DH)
            rows = pl.ds(r0, DH)
            sec = lambda s: pl.ds(s * DA + r0, DH)
            g = _gate_group(rows, o_ref, za_ref, gb_ref, gc_ref, xc_ref, zc_ref, gcp_ref, xcp_ref,
                            cw_ref, ga_ref, gcn_ref, j == 0)
            o, za, gb, gc, xc, zc, sa, sc = (g[n] for n in ("o", "za", "gb", "gc", "xc", "zc", "sa", "sc"))
            dya = dmix_ref[rows, :].astype(F32)
            dyc = dmix_ref[pl.ds(DA + r0, DH), :].astype(F32)

            dn = dya * (za * sa)
            dg5_ref[sec(0), :] = (dya * (g["oa"] * g["g_a"]) * (sa * (1.0 + za * (1.0 - sa)))).astype(BF16)
            dga_ref[rows, :] += _lane_tiles_sum(dn * g["oa"])
            dng = dn * g["g_a"]
            mean_a = jnp.mean(dng * g["oa"], axis=0, keepdims=True)
            do = (dng - g["oa"] * mean_a) * g["ra"]
            do_ref[rows, :] = do.astype(BF16)
            dd_ref[h] = jnp.sum(do * o, axis=0, keepdims=True)

            dnc = dyc * (zc * sc)
            dg5_ref[sec(4), :] = (dyc * (g["ec"] * g["g_c"]) * (sc * (1.0 + zc * (1.0 - sc)))).astype(BF16)
            dgc_ref[rows, :] += _lane_tiles_sum(dnc * g["ec"])
            dncg = dnc * g["g_c"]
            mean_c = jnp.mean(dncg * g["ec"], axis=0, keepdims=True)
            de = (dncg - g["ec"] * mean_c) * g["rc"]
            dg5_ref[sec(1), :] = (de * g["cv"]).astype(BF16)
            dcv = de * gb
            full = jnp.concatenate([dcv, carry_ref[rows, :]], axis=1)
            d1 = pltpu.roll(full, TT + TB - 1, 1)[:, :TT]
            d2 = pltpu.roll(full, TT + TB - 2, 1)[:, :TT]
            carry_ref[rows, :] = dcv[:, :TB]
            da = g["w2"] * dcv + g["w1"] * d1 + g["w0"] * d2
            dg5_ref[sec(2), :] = (da * xc).astype(BF16)
            dg5_ref[sec(3), :] = (da * gc).astype(BF16)
            dcw_ref[0, rows, :] += _lane_tiles_sum(dcv * g["a2"])
            dcw_ref[1, rows, :] += _lane_tiles_sum(dcv * g["a1"])
            dcw_ref[2, rows, :] += _lane_tiles_sum(dcv * g["a"])
            return c

        lax.fori_loop(0, H, group, 0, unroll=4)

    rj = lambda i: nj - 1 - i
    return pl.pallas_call(
        body, name="gate_bwd", grid=(nj,),
        in_specs=[pl.BlockSpec((2 * DA, TT), lambda i: (0, rj(i)))] + _gate_specs(rj) + [_UNREAD],
        out_specs=[
            pl.BlockSpec((DA, TT), lambda i: (0, rj(i))),
            pl.BlockSpec((H, 1, TT), lambda i: (0, 0, rj(i))),
            pl.BlockSpec((5 * DA, TT), lambda i: (0, rj(i))),
            _full_spec((DA, TB)), _full_spec((DA, TB)), _full_spec((3, DA, TB)),
        ],
        out_shape=[
            jax.ShapeDtypeStruct((DA, L), BF16),
            jax.ShapeDtypeStruct((H, 1, L), F32),
            jax.ShapeDtypeStruct((5 * DA, L), BF16),
            jax.ShapeDtypeStruct((DA, TB), F32),
            jax.ShapeDtypeStruct((DA, TB), F32),
            jax.ShapeDtypeStruct((3, DA, TB), F32),
        ],
        scratch_shapes=[pltpu.VMEM((DA, TB), F32)],
        compiler_params=_params(),
    )(dmix_t, o_t, proj_t, proj_t, proj_t, proj_t, proj_t, proj_t, proj_t, cw_b, ga_b, gcn_b, after)


def _attn_bwd(proj_t, kaug, vtok, do_t, lse, dd, cq, L, after):
    nk = L // TT

    def body(q_ref, kaug_ref, vtok_ref, kt_ref, do_ref, lse_ref, dd_ref, cq_ref, _,
             dq_ref, dk_ref, dv_ref, dck_ref, dcq_ref, dq_acc, kt1_scr, s_scr, dp_scr, dv_scr, dk_scr):
        i = pl.program_id(0)
        rows = [slice(g * DH, (g + 1) * DH) for g in range(HG)]
        ones = jnp.ones((DF, TT), BF16)
        zpad = jnp.zeros((KA - DH - DF, TT), BF16)
        for g in range(HG):
            kt1_scr[g] = jnp.concatenate([kt_ref[rows[g], :], ones], axis=0)
        dv_scr[...] = jnp.zeros_like(dv_scr)
        dk_scr[...] = jnp.zeros_like(dk_scr)

        def q_rows(g, q_off):
            bias = cq_ref[g, :, pl.ds(q_off, TT)] - lse_ref[g, :, pl.ds(q_off, TT)]
            return jnp.concatenate([q_ref[rows[g], pl.ds(q_off, TT)], _bias_rows(bias)], axis=0)

        def scores(jq, masked):
            q_off = pl.multiple_of(jq * TT, TT)
            for g in range(HG):
                s = _dot(kaug_ref[g], jnp.concatenate([q_rows(g, q_off), zpad], axis=0))
                if masked:
                    s = jnp.where(_causal_mask(), s, NEG)
                s_scr[g] = s
                dp_scr[g] = _dot(vtok_ref[g], do_ref[rows[g], pl.ds(q_off, TT)])

        def grads(jq):
            q_off = pl.multiple_of(jq * TT, TT)
            for g in range(HG):
                p = jnp.exp2(s_scr[g])
                ds = (p * (dp_scr[g] - dd_ref[g, :, pl.ds(q_off, TT)])).astype(BF16)
                do1 = jnp.concatenate([do_ref[rows[g], pl.ds(q_off, TT)], jnp.zeros((KA - DH, TT), BF16)], axis=0)
                q1 = jnp.concatenate([q_rows(g, q_off), zpad], axis=0)
                dv_scr[g] += _dot(p.astype(BF16), do1, NT_DIMS)
                dk_scr[g] += _dot(ds, q1, NT_DIMS)
                dq_acc[g, :, pl.ds(q_off, TT)] += _dot(kt1_scr[g], ds)

        @pl.when(i == 0)
        def _():
            dq_acc[...] = jnp.zeros_like(dq_acc)

        scores(i, True)

        def step(jq, c):
            grads(jq)
            scores(jq + 1, False)
            return c

        lax.fori_loop(i, nk - 1, step, 0)
        grads(nk - 1)
        for g in range(HG):
            dv_ref[rows[g], :] = dv_scr[g].T[:DH, :].astype(BF16)
            dk_t = dk_scr[g].T
            dk_ref[rows[g], :] = (dk_t[:DH, :] * LN2).astype(BF16)
            dck_ref[g] = dk_t[DH:DH + 1, :]

        @pl.when(i == nk - 1)
        def _():
            for g in range(HG):
                dq_ref[rows[g], :] = (dq_acc[g, :DH, :] * (DH ** -0.5)).astype(BF16)
                dcq_ref[g] = dq_acc[g, DH:DH + 1, :]

    assert HG == H
    head = lambda i: (0, 0)
    row = lambda i: (0, 0, 0)
    return pl.pallas_call(
        body, name="attn_bwd", grid=(nk,),
        in_specs=[
            pl.BlockSpec((DA, L), head),
            pl.BlockSpec((H, TT, KA), lambda i: (0, i, 0)),
            pl.BlockSpec((H, TT, DH), lambda i: (0, i, 0)),
            pl.BlockSpec((DA, TT), lambda i: (1, i)),
            pl.BlockSpec((DA, L), head),
            pl.BlockSpec((H, 1, L), row), pl.BlockSpec((H, 1, L), row), pl.BlockSpec((H, 1, L), row), _UNREAD,
        ],
        out_specs=[
            pl.BlockSpec((DA, L), head),
            pl.BlockSpec((DA, TT), lambda i: (0, i)),
            pl.BlockSpec((DA, TT), lambda i: (0, i)),
            pl.BlockSpec((H, 1, TT), lambda i: (0, 0, i)),
            pl.BlockSpec((H, 1, L), row),
        ],
        out_shape=[jax.ShapeDtypeStruct((DA, L), BF16), jax.ShapeDtypeStruct((DA, L), BF16),
                   jax.ShapeDtypeStruct((DA, L), BF16), jax.ShapeDtypeStruct((H, 1, L), F32),
                   jax.ShapeDtypeStruct((H, 1, L), F32)],
        scratch_shapes=[
            pltpu.VMEM((HG, DH + DF, L), F32),
            pltpu.VMEM((HG, DH + DF, TT), BF16),
            pltpu.VMEM((HG, TT, TT), F32), pltpu.VMEM((HG, TT, TT), F32),
            pltpu.VMEM((HG, TT, KA), F32), pltpu.VMEM((HG, TT, KA), F32)],
        compiler_params=_params(),
    )(proj_t, kaug, vtok, proj_t, do_t, lse, dd, cq, after)


def _fgate_bwd(dcq, dck, sg, L):
    def body(dcq_ref, dck_ref, sg_ref, df_ref, db_ref):
        dc = jnp.concatenate([dcq_ref[h] - dck_ref[h] for h in range(H)], axis=0)
        idx = lax.broadcasted_iota(jnp.int32, (H, L), 1)
        r = dc
        s = 1
        while s < L:
            r = r + jnp.where(idx + s < L, pltpu.roll(r, L - s, 1), 0.0)
            s *= 2
        df = r * sg_ref[...]
        db_ref[...] = jnp.broadcast_to(jnp.sum(df, axis=1, keepdims=True), (H, TB))
        df_ref[...] = jnp.concatenate([df, jnp.zeros((DF - H, L), F32)], axis=0).astype(BF16)

    return pl.pallas_call(
        body, name="fgate_bwd",
        out_shape=[jax.ShapeDtypeStruct((DF, L), BF16), jax.ShapeDtypeStruct((H, TB), F32)],
        compiler_params=pltpu.CompilerParams(vmem_limit_bytes=VMEM_LIMIT),
    )(dcq, dck, sg)


def _inproj_bwd_x(w, dq_t, dk_t, dv_t, dg5_t, df_t, dout, x, meta_full, norm_g, L, after):
    nj = L // TT
    seq = x.shape[0]

    def body(w_ref, dq_ref, dk_ref, dv_ref, dg5_ref, df_ref, dout_ref, xa_ref, xb_ref, xc_ref, meta_ref, g_ref, _,
             gx_ref, dmeta_ref, dg_ref, dh_scr, sems):
        j = pl.program_id(0)
        slot = j % 2

        def copy_out(step, slot_):
            first = pltpu.make_async_copy(dh_scr.at[slot_, pl.ds(TB, TT - TB)], gx_ref.at[pl.ds(0, TT - TB)],
                                          sems.at[slot_])
            later = pltpu.make_async_copy(dh_scr.at[slot_], gx_ref.at[pl.ds(step * TT - TB, TT)], sems.at[slot_])
            return first, later

        @pl.when(j == 0)
        def _():
            dg_ref[...] = jnp.zeros_like(dg_ref)

        du = _dot(dq_ref[...], w_ref[0:DA, :], TN_DIMS)
        du += _dot(dk_ref[...], w_ref[DA:2 * DA, :], TN_DIMS)
        du += _dot(dv_ref[...], w_ref[2 * DA:3 * DA, :], TN_DIMS)
        du += _dot(dg5_ref[...], w_ref[3 * DA:NSEC * DA, :], TN_DIMS)
        du += _dot(df_ref[...], w_ref[NSEC * DA:DPROJ, :], TN_DIMS)
        hb = _h_tile(j, xa_ref, xb_ref, xc_ref, meta_ref)
        r = lax.rsqrt(jnp.mean(hb * hb, axis=-1, keepdims=True) + EPS)
        hn = hb * r
        dg_ref[...] += jnp.sum(du * hn, axis=0, keepdims=True)
        gu = du * g_ref[...]
        dh = dout_ref[...] + r * gu - hn * (r * jnp.mean(gu * hn, axis=-1, keepdims=True))

        dh_scr[slot] = dh

        @pl.when(j == 0)
        def _():
            dmeta_ref[...] = dh[P0:TB, :]
            copy_out(0, 0)[0].start()

        @pl.when(j >= 1)
        def _():
            copy_out(j, slot)[1].start()

        @pl.when(j == 1)
        def _():
            copy_out(0, 0)[0].wait()

        @pl.when(j >= 2)
        def _():
            copy_out(j - 1, 1 - slot)[1].wait()

        @pl.when(j == nj - 1)
        def _():
            copy_out(j, slot)[0 if nj == 1 else 1].wait()

    blk = lambda rows: pl.BlockSpec((rows, TT), lambda j: (0, j))
    return pl.pallas_call(
        body, name="inproj_bwd_x", grid=(nj,),
        in_specs=[_full_spec((DPROJ, D)), blk(DA), blk(DA), blk(DA), blk(5 * DA), blk(DF),
                  pl.BlockSpec((TT, D), lambda j: (j, 0))] + _x_specs3()
                 + [_full_spec((NM, D)), _full_spec((1, D)), _UNREAD],
        out_specs=[pl.BlockSpec(memory_space=pl.ANY), _full_spec((NM, D)), _full_spec((1, D))],
        out_shape=[jax.ShapeDtypeStruct((seq, D), F32), jax.ShapeDtypeStruct((NM, D), F32),
                   jax.ShapeDtypeStruct((1, D), F32)],
        scratch_shapes=[pltpu.VMEM((2, TT, D), F32), pltpu.SemaphoreType.DMA((2,))],
        compiler_params=_params(),
    )(w, dq_t, dk_t, dv_t, dg5_t, df_t, dout, x, x, x, meta_full, norm_g, after)


def _inproj_bwd_w(u, dq_t, dk_t, dv_t, dg5_t, df_t, L):
    def body(u_ref, dq_hbm, dk_hbm, dv_hbm, dg5_ref, df_ref, dw_ref, dwf_ref, qkv_scr, sems):
        s = pl.program_id(0)
        u_all = u_ref[...]
        fetch = [pltpu.make_async_copy(src, qkv_scr.at[k], sems.at[k])
                 for k, src in enumerate((dq_hbm, dk_hbm, dv_hbm))]

        @pl.when(s == 0)
        def _():
            for cp in fetch:
                cp.start()

        @pl.when(s < 5)
        def _():
            dw_ref[...] = _dot(dg5_ref[...], u_all)

        for k in range(3):
            @pl.when(s == 5 + k)
            def _(k=k):
                fetch[k].wait()
                dw_ref[...] = _dot(qkv_scr[k], u_all)

        @pl.when(s == NSEC - 1)
        def _():
            dwf_ref[...] = _dot(df_ref[...], u_all)

    once = lambda shape: pl.BlockSpec(shape, lambda s: (0, 0), pipeline_mode=pl.Buffered(1))
    any_spec = pl.BlockSpec(memory_space=pl.ANY)
    return pl.pallas_call(
        body, name="inproj_bwd_w", grid=(NSEC,),
        in_specs=[
            once((L, D)), any_spec, any_spec, any_spec,
            pl.BlockSpec((DA, L), lambda s: (jnp.minimum(s, 4), 0)),
            once((DF, L)),
        ],
        out_specs=[pl.BlockSpec((DA, D), lambda s: (jnp.where(s < 5, s + 3, s - 5), 0)), _full_spec((DF, D))],
        out_shape=[jax.ShapeDtypeStruct((NSEC * DA, D), F32), jax.ShapeDtypeStruct((DF, D), F32)],
        scratch_shapes=[pltpu.VMEM((3, DA, L), BF16), pltpu.SemaphoreType.DMA((3,))],
        compiler_params=_params(),
    )(u, dq_t, dk_t, dv_t, dg5_t, df_t)


def _adamw(w, g, m, v):
    m = ADAM_B1 * m + (1.0 - ADAM_B1) * g
    v = ADAM_B2 * v + (1.0 - ADAM_B2) * (g * g)
    m_hat = m / (1.0 - ADAM_B1 ** ADAM_STEP)
    v_hat = v / (1.0 - ADAM_B2 ** ADAM_STEP)
    delta = -ADAM_LR * (m_hat / (jnp.sqrt(v_hat) + ADAM_EPS) + ADAM_WD * w)
    return delta, m, v


def _adamw_big(own_in, land_in, own_out, land_out, w_in_t, m_in_t, v_in_t, w_out, m_out, v_out):
    cb = CB
    e_sh = D // NDEV
    in_shape = jax.ShapeDtypeStruct(w_in_t.shape, F32)
    out_shape = jax.ShapeDtypeStruct(w_out.shape, F32)

    def total(own_ref, land_ref, rows, chips):
        g = _pick_slab(0, own_ref, land_ref, rows, chips=chips).astype(F32)
        for j in range(1, own_ref.shape[0]):
            g = g + _pick_slab(j, own_ref, land_ref, rows, chips=chips).astype(F32)
        return g

    def body(oi_ref, li_ref, oo_ref, lo_ref, wi_ref, mi_ref, vi_ref, wo_ref, mo_ref, vo_ref,
             gi, di, mi, vi, go, do, mo, vo):
        g = total(oi_ref, li_ref, slice(0, WSHP), True)[:WSH]
        d, mn, vn = _adamw(wi_ref[...], g, mi_ref[...], vi_ref[...])
        gi[...], di[...], mi[...], vi[...] = g, d, mn, vn
        g = total(oo_ref, lo_ref, slice(0, e_sh), False)
        d, mn, vn = _adamw(wo_ref[0], g, mo_ref[0], vo_ref[0])
        go[0], do[0], mo[0], vo[0] = g, d, mn, vn

    slab = lambda n, rows: pl.BlockSpec((n, rows, cb), lambda i: (0, 0, i))
    ispec = pl.BlockSpec((WSH, cb), lambda i: (0, i))
    ospec = pl.BlockSpec((1, e_sh, cb), lambda i: (0, 0, i))
    return pl.pallas_call(
        body, name="adamw_big", grid=(D // cb,),
        in_specs=[slab(4, WSHP), slab(4, WSHP), slab(NDEV, e_sh), slab(NDEV, e_sh),
                  ispec, ispec, ispec, ospec, ospec, ospec],
        out_specs=[ispec] * 4 + [ospec] * 4, out_shape=[in_shape] * 4 + [out_shape] * 4,
        compiler_params=_params(),
    )(own_in, land_in, own_out, land_out, w_in_t, m_in_t, v_in_t, w_out, m_out, v_out)


F0 = 3 * DA


def _unshard_w_out(own, land):
    e_sh = D // NDEV

    def body(own_ref, land_ref, wo_ref):
        for j in range(NDEV):
            wo_ref[j * e_sh:(j + 1) * e_sh, :] = _pick_slab(j, own_ref, land_ref, slice(0, e_sh), per_peer=False)

    return pl.pallas_call(
        body, name="unshard_w_out", grid=(D // CB,),
        in_specs=[pl.BlockSpec((e_sh, CB), lambda i: (0, i)), pl.BlockSpec((NDEV, e_sh, CB), lambda i: (0, 0, i))],
        out_specs=pl.BlockSpec((D, CB), lambda i: (0, i)),
        out_shape=jax.ShapeDtypeStruct((D, D), BF16),
        compiler_params=_params(),
    )(own, land)


def _unshard_w_in(w_all, small_all, attn_gain, conv_gain):
    def body(w_ref, small_ref, ga_ref, gc_ref, wt_ref, meta_ref, cwb_ref, gab_ref, gcb_ref):
        i = pl.program_id(0)
        for k in range(CB // TB):
            meta_ref[:, k * TB:(k + 1) * TB] = small_ref[(CB // TB) * i + k, 0:NM, :]

        @pl.when(i == 0)
        def _():
            per_row = lambda line: jnp.broadcast_to(line, (TB, DA)).T
            cw = jnp.concatenate([small_ref[j, NM:NM + 3, 0:DH] for j in range(NDEV)], axis=1)
            for k in range(3):
                cwb_ref[k] = per_row(cw[k:k + 1, :])
            gab_ref[...] = per_row(ga_ref[...])
            gcb_ref[...] = per_row(gc_ref[...])

        def ref_rows(lo, hi):
            pieces, r = [], lo
            while r < hi:
                sh, off = divmod(r, WSH)
                n = min(hi - r, WSH - off)
                pieces.append(w_ref[sh, off:off + n, :])
                r += n
            return pieces

        for s in range(NSEC):
            lo = s * DA if s < 3 else s * DA + H
            wt_ref[s * DA:(s + 1) * DA, :] = jnp.concatenate(ref_rows(lo, lo + DA), axis=0)
        wt_ref[NSEC * DA:DPROJ, :] = jnp.concatenate(
            ref_rows(F0, F0 + H) + [jnp.zeros((DF - H, CB), BF16)], axis=0)

    return pl.pallas_call(
        body, name="unshard_w_in", grid=(D // CB,),
        in_specs=[pl.BlockSpec((NDEV, WSHP, CB), lambda i: (0, 0, i)), _full_spec(small_all.shape),
                  _full_spec((1, DA)), _full_spec((1, DA))],
        out_specs=[pl.BlockSpec((DPROJ, CB), lambda i: (0, i)), pl.BlockSpec((NM, CB), lambda i: (0, i)),
                   _full_spec((3, DA, TB)), _full_spec((DA, TB)), _full_spec((DA, TB))],
        out_shape=[jax.ShapeDtypeStruct((DPROJ, D), BF16), jax.ShapeDtypeStruct((NM, D), F32),
                   jax.ShapeDtypeStruct((3, DA, TB), F32), jax.ShapeDtypeStruct((DA, TB), F32),
                   jax.ShapeDtypeStruct((DA, TB), F32)],
        compiler_params=_params(),
    )(w_all, small_all, attn_gain, conv_gain)


def _shard_w_in_grads(dw_main, dw_f):
    def body(dm_ref, df_ref, p_ref):
        mc = lax.axis_index("c")

        def ref_rows(lo, hi):
            pieces, r = [], lo
            while r < hi:
                if r < F0:
                    n = min(hi, F0) - r
                    pieces.append(dm_ref[r:r + n, :])
                elif r < F0 + H:
                    n = min(hi, F0 + H) - r
                    pieces.append(df_ref[r - F0:r - F0 + n, :])
                else:
                    n = hi - r
                    pieces.append(dm_ref[r - H:r - H + n, :])
                r += n
            return pieces

        for i in range(NDEV):
            rows = jnp.concatenate(ref_rows(i * WSH, (i + 1) * WSH) + [jnp.zeros((WSHP - WSH, CB), F32)], axis=0)
            p_ref[i // 2 + jnp.where(mc == i % 2, 0, 4)] = rows.astype(BF16)

    col = lambda rows: pl.BlockSpec((rows, CB), lambda i: (0, i))
    return pl.pallas_call(
        body, name="shard_w_in_grads", grid=(D // CB,),
        in_specs=[col(NSEC * DA), col(DF)],
        out_specs=pl.BlockSpec((NDEV, WSHP, CB), lambda i: (0, 0, i)),
        out_shape=jax.ShapeDtypeStruct((NDEV, WSHP, D), BF16),
        compiler_params=_params(),
    )(dw_main, dw_f)


SMALL = ("norm_g", "final_norm_g", "attn_norm_g", "conv_norm_g", "b_f", "meta", "conv_w")


def _as_rows(x):
    return jnp.concatenate([x[:, r * TB:(r + 1) * TB] for r in range(x.shape[1] // TB)], axis=0)


def _as_line(rows):
    return jnp.concatenate([rows[r:r + 1, :] for r in range(rows.shape[0])], axis=1)


def _pad_rows(x, n=8):
    return jnp.concatenate([x, jnp.zeros((n - x.shape[0], x.shape[1]), F32)], axis=0)


def _tile_rows(a, rows, lanes=TB):
    a = a.reshape(rows, lanes)
    return jnp.pad(a, ((0, -rows % 8), (0, TB - lanes)))


def _pack_small_grads(dg_norm, dg_final, dga_p, dgc_p, dcw_p, db_b, dmeta, loss):
    def body(dgn_ref, dgf_ref, dga_ref, dgc_ref, dcw_ref, db_ref, dmeta_ref, loss_ref, out_ref):
        def lane_sums(p):
            return jnp.sum(p.T, axis=0, keepdims=True)

        lane = lax.broadcasted_iota(jnp.int32, (1, TB), 1)
        b_row = jnp.where(lane == H, loss_ref[...], 0.0)
        for h in range(H):
            b_row = b_row + jnp.where(lane == h, db_ref[h:h + 1, :], 0.0)
        common = jnp.concatenate([
            _as_rows(dgn_ref[...]), _as_rows(dgf_ref[...]), _pad_rows(_as_rows(lane_sums(dga_ref[...]))),
            _pad_rows(_as_rows(lane_sums(dgc_ref[...]))), _pad_rows(b_row)], axis=0)
        dcw = [lane_sums(dcw_ref[k]) for k in range(3)]
        for j in range(NDEV):
            cw = jnp.concatenate(
                [jnp.concatenate([r[:, j * DH:(j + 1) * DH], jnp.zeros((1, TB - DH), F32)], axis=1) for r in dcw],
                axis=0)
            out_ref[j] = jnp.concatenate([common, dmeta_ref[:, j * TB:(j + 1) * TB], _pad_rows(cw)], axis=0)

    return pl.pallas_call(
        body, name="pack_small_grads", out_shape=jax.ShapeDtypeStruct((NDEV, SROWS, TB), F32),
    )(dg_norm, dg_final, dga_p, dgc_p, dcw_p, db_b, dmeta, loss)


def _adamw_small(own, land, params):
    flat = [a for n in SMALL for a in params[n]]

    def body(*refs):
        own_ref, land_ref = refs[:2]
        ins = refs[2:2 + 3 * len(SMALL)]
        outs = refs[2 + 3 * len(SMALL):]
        g = _pick_slab(0, own_ref, land_ref, slice(0, SROWS))
        for j in range(1, NDEV):
            g = g + _pick_slab(j, own_ref, land_ref, slice(0, SROWS))
        grads = dict(
            norm_g=_as_line(g[0:8]), final_norm_g=_as_line(g[8:16]), attn_norm_g=_as_line(g[16:20]),
            conv_norm_g=_as_line(g[24:28]), b_f=g[32:33, :H], meta=g[40:56], conv_w=g[56:59, :DH][None])
        for i, n in enumerate(SMALL):
            w_ref, m_ref, v_ref = ins[3 * i:3 * i + 3]
            d, mn, vn = _adamw(w_ref[...], grads[n], m_ref[...], v_ref[...])
            for o_ref, val in zip(outs[4 * i:4 * i + 4], (grads[n], d, mn, vn)):
                o_ref[...] = val
        outs[-1][...] = g[32:33, H:H + 1]

    shapes = [jax.ShapeDtypeStruct(params[n][0].shape, F32) for n in SMALL for _ in range(4)]
    res = pl.pallas_call(
        body, name="adamw_small", out_shape=shapes + [jax.ShapeDtypeStruct((1, 1), F32)],
    )(own, land, *flat)
    return {n: res[4 * i:4 * i + 4] for i, n in enumerate(SMALL)}, res[-1]


def kernel(x, meta, norm_g, w_in, b_f, conv_w, attn_norm_g, conv_norm_g, w_out, final_norm_g, loss_target, m_meta, m_norm_g, m_w_in, m_b_f, m_conv_w, m_attn_norm_g, m_conv_norm_g, m_w_out, m_final_norm_g, v_meta, v_norm_g, v_w_in, v_b_f, v_conv_w, v_attn_norm_g, v_conv_norm_g, v_w_out, v_final_norm_g):
    seq = x.shape[1]
    L = seq + TB
    assert x.shape == (1, seq, D) and L % TT == 0 and w_in.shape == (1, D, WSH)
    x2 = x[0]
    tgt = loss_target[0]

    w_in_slab = jnp.pad(w_in[0].T, ((0, WSHP - WSH), (0, 0))).astype(BF16)
    w_out_slab = w_out[0].astype(BF16)
    meta_slab = jnp.concatenate([meta, _tile_rows(conv_w[0], 3, DH)], axis=0)
    wout_flight = _split_start(w_out_slab, "gather_w_out_start", per_peer=False)
    w_all, small_all = _all_gather([w_in_slab, meta_slab], "gather_w_in")

    w_t, meta_full, cw_b, ga_b, gcn_b = _unshard_w_in(w_all, small_all, attn_norm_g, conv_norm_g)

    u, proj_t, f_t, ktok, vtok = _inproj_fwd(x2, meta_full, norm_g, w_t, L, after=wout_flight[4])
    cq, kaug, sg = _fgate_fwd(f_t, b_f.reshape(H, 1), ktok, L)
    o_t, lse = _attn_fwd(proj_t, kaug, cq, L)

    w_out_own, w_out_land = _split_wait(wout_flight, o_t, "gather_w_out_wait", per_peer=False)
    w_out_full = _unshard_w_out(w_out_own, w_out_land)
    dout, dw_out, loss_part, dg_final, do_t, dd, dg5_t, dga_p, dgc_p, dcw_p = _gate_outproj(
        o_t, proj_t, cw_b, ga_b, gcn_b, w_out_full, x2, meta_full, final_norm_g.reshape(1, D), tgt, L)
    dwo_flight = _split_start(dw_out.reshape(NDEV, D // NDEV, D), "exchange_dw_out_start", per_peer=True)
    dq_t, dk_t, dv_t, dck, dcq = _attn_bwd(proj_t, kaug, vtok, do_t, lse, dd, cq, L, after=dwo_flight[4])
    df_t, db_f = _fgate_bwd(dcq, dck, sg, L)
    dw_main, dw_f = _inproj_bwd_w(u, dq_t, dk_t, dv_t, dg5_t, df_t, L)
    dwi_parts = _shard_w_in_grads(dw_main, dw_f)
    dwi_chip = _pair_sum(dwi_parts, _pair_exchange(dwi_parts, "exchange_dw_in_pair"))
    dwi_flight = _split_start(dwi_chip, "exchange_dw_in_start", per_peer=True, chips=True)
    grad_x, dmeta, dg_norm = _inproj_bwd_x(
        w_t, dq_t, dk_t, dv_t, dg5_t, df_t, dout, x2, meta_full, norm_g, L, after=dwi_flight[4])
    small_parts = _pack_small_grads(dg_norm, dg_final, dga_p, dgc_p, dcw_p, db_f, dmeta, loss_part)
    small_flight = _split_start(small_parts, "exchange_small_start", per_peer=True)
    dwo_own, dwo_land = _split_wait(dwo_flight, small_flight[4], "exchange_dw_out_wait", per_peer=True)
    dwi_own, dwi_land = _split_wait(dwi_flight, dwo_land, "exchange_dw_in_wait", per_peer=True, chips=True)

    big_out = _adamw_big(dwi_own, dwi_land, dwo_own, dwo_land,
                         w_in[0].T, m_w_in[0].T, v_w_in[0].T, w_out, m_w_out, v_w_out)
    g_w_in, d_w_in, nm_w_in, nv_w_in = [a.T[None] for a in big_out[:4]]
    g_w_out, d_w_out, nm_w_out, nv_w_out = big_out[4:]
    sm_own, sm_land = _split_wait(small_flight, big_out[4], "exchange_small_wait", per_peer=True)
    line = lambda a: a.reshape(1, D)
    small, loss = _adamw_small(sm_own, sm_land, dict(
        norm_g=(norm_g, m_norm_g, v_norm_g),
        final_norm_g=(line(final_norm_g), line(m_final_norm_g), line(v_final_norm_g)),
        attn_norm_g=(attn_norm_g, m_attn_norm_g, v_attn_norm_g),
        conv_norm_g=(conv_norm_g, m_conv_norm_g, v_conv_norm_g),
        b_f=(b_f, m_b_f, v_b_f), meta=(meta, m_meta, v_meta), conv_w=(conv_w, m_conv_w, v_conv_w)))
    small["final_norm_g"] = [a.reshape(D) for a in small["final_norm_g"]]
    order = ("meta", "norm_g", "w_in", "b_f", "conv_w", "attn_norm_g", "conv_norm_g", "w_out", "final_norm_g")
    groups = []
    for k, (wi, wo) in enumerate(((g_w_in, g_w_out), (d_w_in, d_w_out), (nm_w_in, nm_w_out), (nv_w_in, nv_w_out))):
        d = dict({n: small[n][k] for n in SMALL}, w_in=wi, w_out=wo)
        groups.append([d[n] for n in order])
    return (loss[0, 0], grad_x[None], *groups[0], *groups[1], *groups[2], *groups[3])
```

```python
import jax
import jax.numpy as jnp
from jax import lax
from jax.experimental import pallas as pl
from jax.experimental.pallas import tpu as pltpu

F32 = jnp.float32
BF16 = jnp.bfloat16

D = 1024
DA = 512
H = 8
DH = 64
NM = 16
TB = 128
P0 = TB - NM
TT = 3 * TB
HG = 8
NDEV = 8
NSEC = 8
DF = 16
DPROJ = NSEC * DA + DF
WSH = 513
WSHP = 528
WROWS = WSHP + D // NDEV
SROWS = 64
EPS = 1e-6
NEG = -1e30
LOG2E = 1.4426950408889634
LN2 = 0.6931471805599453
QSCALE = DH ** -0.5 * LOG2E
KA = 128
CB = 256
VMEM_LIMIT = 56 * 1024 * 1024

ADAM_LR = 0.001
ADAM_B1 = 0.9
ADAM_B2 = 0.999
ADAM_EPS = 1e-08
ADAM_WD = 0.01
ADAM_STEP = 10

NT_DIMS = (((1,), (1,)), ((), ()))
TN_DIMS = (((0,), (0,)), ((), ()))
MESH = pl.DeviceIdType.MESH


def _params(n_axes=1, vmem=VMEM_LIMIT):
    return pltpu.CompilerParams(dimension_semantics=("arbitrary",) * n_axes, vmem_limit_bytes=vmem)


def _dot(a, b, dims=None):
    if dims is None:
        return jnp.dot(a, b, preferred_element_type=F32)
    return lax.dot_general(a, b, dims, preferred_element_type=F32)


def _my_place():
    return lax.axis_index("x"), lax.axis_index("y"), lax.axis_index("c")


def _all_gather(xs, name):
    n = len(xs)

    def body(*refs):
        x_refs, out_refs = refs[:n], refs[n:2 * n]
        send_sems, recv_sems, local_sems = refs[2 * n:]
        mx, my, mc = _my_place()

        def across(px, py, pc, axis_a):
            flip_x = pc if axis_a else 1 - pc
            return (px + flip_x) % 2, (py + 1 - flip_x) % 2, pc

        def idx(p):
            return 4 * p[0] + 2 * p[1] + p[2]

        me, sib = (mx, my, mc), (mx, my, 1 - mc)
        a_nbr, b_nbr = across(*me, True), across(*me, False)
        diag = across(*b_nbr, True)
        sib_a, sib_b = across(*sib, True), across(*sib, False)
        sib_diag = across(*sib_b, True)

        waits = []
        for t in range(n):
            out_ref = out_refs[t]

            def copy(k, block, to, src=None, out_ref=out_ref, t=t):
                return pltpu.make_async_remote_copy(
                    src_ref=out_ref.at[idx(block)] if src is None else src, dst_ref=out_ref.at[idx(block)],
                    send_sem=send_sems.at[7 * t + k], recv_sem=recv_sems.at[7 * t + k],
                    device_id=to, device_id_type=MESH)

            mine = pltpu.make_async_copy(x_refs[t], out_ref.at[idx(me)], local_sems.at[t])
            mine.start()
            started = [copy(0, me, sib, src=x_refs[t]), copy(1, me, a_nbr, src=x_refs[t]),
                       copy(2, me, b_nbr, src=x_refs[t])]
            for cp in started:
                cp.start()
            waits.append((copy, mine, started))
        relays = ((1, a_nbr, ((3, b_nbr), (4, sib))), (2, b_nbr, ((5, sib),)), (3, diag, ((6, sib),)))
        for landed, block, onward in relays:
            for copy, _, started in waits:
                copy(landed, block, me).wait_recv()
                for k, to in onward:
                    started.append(copy(k, block, to))
                    started[-1].start()
        for copy, mine, started in waits:
            for k, block in ((0, sib), (4, sib_a), (5, sib_b), (6, sib_diag)):
                copy(k, block, me).wait_recv()
            for cp in started:
                cp.wait_send()
            mine.wait()

    any_spec = pl.BlockSpec(memory_space=pl.ANY)
    return pl.pallas_call(
        body, name=name,
        out_shape=[jax.ShapeDtypeStruct((NDEV,) + x.shape, x.dtype) for x in xs],
        in_specs=[any_spec] * n, out_specs=[any_spec] * n,
        scratch_shapes=[pltpu.SemaphoreType.DMA((7 * n,)), pltpu.SemaphoreType.DMA((7 * n,)),
                        pltpu.SemaphoreType.DMA((n,))],
    )(*xs)


_HBM = pl.BlockSpec(memory_space=pltpu.HBM)
_UNREAD = pl.BlockSpec(memory_space=pl.ANY)
_SEM = pl.BlockSpec(memory_space=pltpu.SEMAPHORE)
_EFFECT = pltpu.SideEffectType.DATAFLOW_SIDE_EFFECTING


def _peer_of(m, place):
    mx, my, mc = place
    return ((1 - mx) if m & 4 else mx, (1 - my) if m & 2 else my, (1 - mc) if m & 1 else mc)


def _party(chips):
    if chips:
        return (lambda p: 2 * p[0] + p[1]), (2, 4, 6)
    return (lambda p: 4 * p[0] + 2 * p[1] + p[2]), tuple(range(1, NDEV))


def _split_copies(src_ref, land_ref, send_sems, recv_sems, per_peer, incoming, chips):
    place = _my_place()
    slot, masks = _party(chips)
    me = slot(place)
    out = []
    for k, m in enumerate(masks):
        there = _peer_of(m, place)
        peer = slot(there)
        src = (src_ref.at[me] if incoming else src_ref.at[peer]) if per_peer else src_ref
        out.append(pltpu.make_async_remote_copy(
            src_ref=src, dst_ref=land_ref.at[peer if incoming else me],
            send_sem=send_sems.at[k], recv_sem=recv_sems.at[k], device_id=there, device_id_type=MESH))
    return out


def _split_start(src, name, per_peer, chips=False):
    slab = src.shape[1:] if per_peer else src.shape
    n = len(_party(chips)[1])

    def body(src_ref, land_ref, send_sems, recv_sems, src_thru, land_thru, token):
        for cp in _split_copies(src_ref, land_ref, send_sems, recv_sems, per_peer, False, chips):
            cp.start()
        token[...] = jnp.zeros_like(token)

    return pl.pallas_call(
        body, name=name,
        out_shape=(pltpu.SemaphoreType.DMA((n,)), pltpu.SemaphoreType.DMA((n,)),
                   pltpu.HBM(src.shape, src.dtype), pltpu.HBM((n + 1,) + slab, src.dtype),
                   jax.ShapeDtypeStruct((8, TB), F32)),
        in_specs=(_HBM, _HBM), out_specs=(_SEM, _SEM, _HBM, _HBM, pl.BlockSpec(memory_space=pltpu.VMEM)),
        input_output_aliases={0: 2, 1: 3},
        compiler_params=pltpu.CompilerParams(has_side_effects=_EFFECT),
    )(pltpu.with_memory_space_constraint(src, pltpu.HBM),
      pltpu.with_memory_space_constraint(lax.empty((n + 1,) + slab, src.dtype), pltpu.HBM))


def _split_wait(handles, after, name, per_peer, chips=False):
    send_sems, recv_sems, src_thru, land_thru, _ = handles

    def body(src_ref, land_ref, send_sems, recv_sems, after_ref, src_out, land_out):
        for cp in _split_copies(src_ref, land_ref, send_sems, recv_sems, per_peer, False, chips):
            cp.wait_send()
        for cp in _split_copies(src_ref, land_ref, send_sems, recv_sems, per_peer, True, chips):
            cp.wait_recv()

    return pl.pallas_call(
        body, name=name,
        out_shape=(pltpu.HBM(src_thru.shape, src_thru.dtype), pltpu.HBM(land_thru.shape, land_thru.dtype)),
        in_specs=(_HBM, _HBM, _SEM, _SEM, pl.BlockSpec(memory_space=pl.ANY)), out_specs=(_HBM, _HBM),
        input_output_aliases={0: 0, 1: 1},
        compiler_params=pltpu.CompilerParams(has_side_effects=_EFFECT),
    )(src_thru, land_thru, send_sems, recv_sems, after)


def _pick_slab(j, own_ref, land_ref, rows, per_peer=True, chips=False):
    me = _party(chips)[0](_my_place())
    own = (lambda: own_ref[j, rows, :]) if per_peer else (lambda: own_ref[rows, :])
    return lax.cond(me == j, own, lambda: land_ref[j, rows, :])


def _pair_exchange(p, name):
    def body(p_ref, got_ref, send_sems, recv_sems):
        mx, my, mc = _my_place()
        copies = [pltpu.make_async_remote_copy(
            src_ref=p_ref.at[4 + q], dst_ref=got_ref.at[q], send_sem=send_sems.at[q],
            recv_sem=recv_sems.at[q], device_id=(mx, my, 1 - mc), device_id_type=MESH) for q in range(4)]
        for cp in copies:
            cp.start()
        for cp in copies:
            cp.wait_recv()
        for cp in copies:
            cp.wait_send()

    any_spec = pl.BlockSpec(memory_space=pl.ANY)
    return pl.pallas_call(
        body, name=name, out_shape=jax.ShapeDtypeStruct((4,) + p.shape[1:], p.dtype),
        in_specs=[any_spec], out_specs=any_spec,
        scratch_shapes=[pltpu.SemaphoreType.DMA((4,)), pltpu.SemaphoreType.DMA((4,))],
    )(p)


def _pair_sum(p, got):
    rows = p.shape[1]

    def body(p_ref, got_ref, out_ref):
        for q in range(4):
            out_ref[q] = (p_ref[q].astype(F32) + got_ref[q].astype(F32)).astype(BF16)

    blk = lambda n: pl.BlockSpec((n, rows, CB), lambda i: (0, 0, i))
    return pl.pallas_call(
        body, name="pair_sum", grid=(D // CB,), in_specs=[blk(4), blk(4)], out_specs=blk(4),
        out_shape=jax.ShapeDtypeStruct((4, rows, D), BF16), compiler_params=_params(),
    )(p, got)


def _h_block(t, x_ref, meta_ref):
    first = jnp.concatenate([jnp.zeros((P0, D), F32), meta_ref[...]], axis=0)
    return jnp.where(t == 0, first, x_ref[...])


def _x_specs3(tile=lambda j: j):
    return [pl.BlockSpec((TB, D), lambda j: (jnp.maximum(3 * tile(j) - 1, 0), 0)),
            pl.BlockSpec((TB, D), lambda j: (3 * tile(j), 0)),
            pl.BlockSpec((TB, D), lambda j: (3 * tile(j) + 1, 0))]


def _h_tile(j, xa_ref, xb_ref, xc_ref, meta_ref):
    first = jnp.concatenate([jnp.zeros((P0, D), F32), meta_ref[...]], axis=0)
    return jnp.concatenate([jnp.where(j == 0, first, xa_ref[...]), xb_ref[...], xc_ref[...]], axis=0)


def _full_spec(shape):
    return pl.BlockSpec(shape, lambda *_: (0,) * len(shape))


def _sigmoid(z):
    return 1.0 / (1.0 + jnp.exp(-z))


def _lane_tiles_sum(x):
    out = x[:, :TB]
    for i in range(1, x.shape[1] // TB):
        out = out + x[:, i * TB:(i + 1) * TB]
    return out


def _inproj_fwd(x, meta_full, norm_g, w_t, L, after):
    nj = L // TT

    def body(xa_ref, xb_ref, xc_ref, meta_ref, g_ref, w_ref, _, u_ref, proj_ref, gate_ref, f_ref, ktok_ref, vtok_ref):
        hb = _h_tile(pl.program_id(0), xa_ref, xb_ref, xc_ref, meta_ref)
        r = lax.rsqrt(jnp.mean(hb * hb, axis=-1, keepdims=True) + EPS)
        u = (hb * r * g_ref[...]).astype(BF16)
        u_ref[...] = u
        for s in range(NSEC):
            p = _dot(u, w_ref[s * DA:(s + 1) * DA, :], NT_DIMS)
            if s == 0:
                p = p * QSCALE
            if s in (1, 2):
                tok_ref = ktok_ref if s == 1 else vtok_ref
                for h in range(H):
                    tok_ref[h] = p[:, h * DH:(h + 1) * DH].astype(BF16)
            out_ref, s_out = (proj_ref, s) if s < 3 else (gate_ref, s - 3)
            out_ref[s_out * DA:(s_out + 1) * DA, :] = p.T.astype(BF16)
        f_ref[...] = _dot(w_ref[NSEC * DA:DPROJ, :], u, NT_DIMS)[:H]

    return pl.pallas_call(
        body, name="inproj_fwd", grid=(nj,),
        in_specs=_x_specs3() + [_full_spec((NM, D)), _full_spec((1, D)), _full_spec((DPROJ, D)), _UNREAD],
        out_specs=[
            pl.BlockSpec((TT, D), lambda t: (t, 0)),
            pl.BlockSpec((3 * DA, TT), lambda t: (0, t)),
            pl.BlockSpec((None, (NSEC - 3) * DA, TT), lambda t: (t, 0, 0)),
            pl.BlockSpec((H, TT), lambda t: (0, t)),
            pl.BlockSpec((H, TT, DH), lambda t: (0, t, 0)),
            pl.BlockSpec((H, TT, DH), lambda t: (0, t, 0)),
        ],
        out_shape=[
            jax.ShapeDtypeStruct((L, D), BF16),
            jax.ShapeDtypeStruct((3 * DA, L), BF16),
            jax.ShapeDtypeStruct((nj, (NSEC - 3) * DA, TT), BF16),
            jax.ShapeDtypeStruct((H, L), F32),
            jax.ShapeDtypeStruct((H, L, DH), BF16),
            jax.ShapeDtypeStruct((H, L, DH), BF16),
        ],
        compiler_params=_params(),
    )(x, x, x, meta_full, norm_g, w_t, after)


def _split3(x):
    hi = x.astype(BF16).astype(F32)
    r = x - hi
    mid = r.astype(BF16).astype(F32)
    return hi, mid, (r - mid).astype(BF16).astype(F32)


def _bias_rows(bias):
    one = jnp.ones((1, TT), F32)
    zero = jnp.zeros((1, TT), F32)
    parts = [zero] * 3 if bias is None else list(_split3(bias))
    return jnp.concatenate([one] * 3 + parts + [zero] * (DF - 6), axis=0).astype(BF16)


def _fgate_fwd(f_t, b_col, ktok, L):
    nb = L // TB

    def body(f_ref, b_ref, ktok_ref, cq_ref, kaug_ref, sg_ref, bias_scr):
        h = pl.program_id(0)

        @pl.when(h == 0)
        def _():
            z = f_ref[...] + b_ref[...]
            idx = lax.broadcasted_iota(jnp.int32, (H, L), 1)
            real = idx >= P0
            lf = jnp.where(real, jnp.minimum(z, 0.0) - jnp.log1p(jnp.exp(-jnp.abs(z))), 0.0)
            sg_ref[...] = jnp.where(real, 1.0 / (1.0 + jnp.exp(z)), 0.0)
            c = lf
            s = 1
            while s < L:
                c = c + jnp.where(idx >= s, pltpu.roll(c, s, 1), 0.0)
                s *= 2
            c = c * LOG2E
            for hh in range(H):
                cq_ref[hh] = c[hh:hh + 1, :]
            for part, val in enumerate(_split3(-jnp.where(real, c, -NEG))):
                for hh in range(H):
                    bias_scr[part * H + hh] = val[hh:hh + 1, :]

        lane = lax.broadcasted_iota(jnp.int32, (TB, KA), 1)
        head = jnp.zeros((DH, TB), F32)
        tail = jnp.concatenate([jnp.ones((3, TB), F32), jnp.zeros((KA - DH - 6, TB), F32)], axis=0)
        for b in range(nb):
            blk = slice(b * TB, (b + 1) * TB)
            cols = jnp.concatenate(
                [head] + [bias_scr[part * H + h, :, blk] for part in range(3)] + [tail], axis=0).T
            k = jnp.concatenate([ktok_ref[0, blk, :].astype(F32), jnp.zeros((TB, KA - DH), F32)], axis=1)
            kaug_ref[0, blk, :] = jnp.where(lane < DH, k, cols).astype(BF16)

    return pl.pallas_call(
        body, name="fgate_fwd", grid=(H,),
        in_specs=[_full_spec((H, L)), _full_spec((H, 1)), pl.BlockSpec((1, L, DH), lambda h: (h, 0, 0))],
        out_specs=[_full_spec((H, 1, L)), pl.BlockSpec((1, L, KA), lambda h: (h, 0, 0)), _full_spec((H, L))],
        out_shape=[
            jax.ShapeDtypeStruct((H, 1, L), F32),
            jax.ShapeDtypeStruct((H, L, KA), BF16),
            jax.ShapeDtypeStruct((H, L), F32),
        ],
        scratch_shapes=[pltpu.VMEM((3 * H, 1, L), F32)],
        compiler_params=_params(),
    )(f_t, b_col, ktok)


def _causal_mask():
    r = lax.broadcasted_iota(jnp.int32, (TT, TT), 0)
    c = lax.broadcasted_iota(jnp.int32, (TT, TT), 1)
    return r <= c


def _attn_fwd(proj_t, kaug, cq, L):
    nq = L // TT

    def body(q_ref, qn_ref, kaug_ref, v_ref, cq_ref, o_ref, lse_ref,
             qa_scr, s_scr, cmax_scr, m_scr, p_scr, alpha_scr, acc_scr):
        j = pl.program_id(0)
        rows = [slice(g * DH, (g + 1) * DH) for g in range(HG)]
        ones = jnp.ones((DF, TT), BF16)

        def load_queries(ref):
            for g in range(HG):
                qa_scr[g] = jnp.concatenate(
                    [ref[rows[g], :], _bias_rows(None), jnp.zeros((KA - DH - DF, TT), BF16)], axis=0)

        def scores(kt, masked):
            k_off = pl.multiple_of(kt * TT, TT)
            for g in range(HG):
                s = _dot(kaug_ref[g, pl.ds(k_off, TT), :], qa_scr[g])
                if masked:
                    s = jnp.where(_causal_mask(), s, NEG)
                s_scr[g] = s
                cmax_scr[g] = jnp.max(s, axis=0, keepdims=True)

        def softmax():
            for g in range(HG):
                m_old = m_scr[g]
                m_new = jnp.maximum(m_old, cmax_scr[g])
                alpha_scr[g] = jnp.exp2(m_old - m_new)
                p_scr[g] = jnp.exp2(s_scr[g] - m_new).astype(BF16)
                m_scr[g] = m_new

        def weighted_sum(kt):
            k_off = pl.multiple_of(kt * TT, TT)
            for g in range(HG):
                v1 = jnp.concatenate([v_ref[rows[g], pl.ds(k_off, TT)], ones], axis=0)
                acc_scr[g] = alpha_scr[g] * acc_scr[g] + _dot(v1, p_scr[g])

        @pl.when(j == 0)
        def _():
            load_queries(q_ref)
            scores(0, True)

        m_scr[...] = jnp.full_like(m_scr, NEG)
        acc_scr[...] = jnp.zeros_like(acc_scr)

        @pl.when(j >= 1)
        def _():
            softmax()
            scores(j - 1, False)

        def step(i, c):
            weighted_sum(j - i + 1)
            softmax()
            scores(j - i - 1, False)
            return c

        lax.fori_loop(1, j, step, 0)

        def drain(second_last, next_tile):
            if second_last:
                weighted_sum(1)
            softmax()
            if next_tile:
                load_queries(qn_ref)
                scores(j + 1, True)
            weighted_sum(0)

        @pl.when(j == 0)
        def _():
            drain(False, nq > 1)

        @pl.when((j >= 1) & (j < nq - 1))
        def _():
            drain(True, True)

        @pl.when((j >= 1) & (j == nq - 1))
        def _():
            drain(True, False)

        for g in range(HG):
            l = acc_scr[g, DH:DH + 1, :]
            o_ref[rows[g], :] = acc_scr[g, :DH, :] * (1.0 / l)
            lse_ref[g] = m_scr[g] + jnp.log2(l) + cq_ref[g]

    assert HG == H
    return pl.pallas_call(
        body, name="attn_fwd", grid=(nq,),
        in_specs=[
            pl.BlockSpec((DA, TT), lambda j: (0, j)),
            pl.BlockSpec((DA, TT), lambda j: (0, jnp.minimum(j + 1, nq - 1))),
            pl.BlockSpec((H, L, KA), lambda j: (0, 0, 0)),
            pl.BlockSpec((DA, L), lambda j: (2, 0)),
            pl.BlockSpec((H, 1, TT), lambda j: (0, 0, j)),
        ],
        out_specs=[
            pl.BlockSpec((None, DA, TT), lambda j: (j, 0, 0)),
            pl.BlockSpec((H, 1, TT), lambda j: (0, 0, j)),
        ],
        out_shape=[jax.ShapeDtypeStruct((nq, DA, TT), F32), jax.ShapeDtypeStruct((H, 1, L), F32)],
        scratch_shapes=[pltpu.VMEM((HG, KA, TT), BF16), pltpu.VMEM((HG, TT, TT), F32), pltpu.VMEM((HG, 1, TT), F32),
                        pltpu.VMEM((HG, 1, TT), F32), pltpu.VMEM((HG, TT, TT), BF16), pltpu.VMEM((HG, 1, TT), F32),
                        pltpu.VMEM((HG, DH + DF, TT), F32)],
        compiler_params=_params(),
    )(proj_t, proj_t, kaug, proj_t, cq)


def _gate_group(rows, o_ref, za_ref, gb_ref, gc_ref, xc_ref, zc_ref, gcp_ref, xcp_ref, cw_ref, ga_ref, gcn_ref, first):
    n_rep = TT // TB
    f32 = lambda r: r[rows, :].astype(F32)
    o, za, gb, gc, xc, zc = o_ref[rows, :], f32(za_ref), f32(gb_ref), f32(gc_ref), f32(xc_ref), f32(zc_ref)
    a = gc * xc
    a_prev = jnp.where(first, 0.0, f32(gcp_ref) * f32(xcp_ref))
    full = jnp.concatenate([a_prev, a], axis=1)
    a1 = pltpu.roll(full, 1, 1)[:, TB:]
    a2 = pltpu.roll(full, 2, 1)[:, TB:]
    w0 = jnp.tile(cw_ref[0, rows, :], (1, n_rep))
    w1 = jnp.tile(cw_ref[1, rows, :], (1, n_rep))
    w2 = jnp.tile(cw_ref[2, rows, :], (1, n_rep))
    cv = w0 * a2 + w1 * a1 + w2 * a
    e = gb * cv
    rc = lax.rsqrt(jnp.mean(e * e, axis=0, keepdims=True) + EPS)
    ec = e * rc
    ra = lax.rsqrt(jnp.mean(o * o, axis=0, keepdims=True) + EPS)
    oa = o * ra
    g_a = jnp.tile(ga_ref[rows, :], (1, n_rep))
    g_c = jnp.tile(gcn_ref[rows, :], (1, n_rep))
    sa = _sigmoid(za)
    sc = _sigmoid(zc)
    return dict(o=o, za=za, gb=gb, gc=gc, xc=xc, zc=zc, a=a, a1=a1, a2=a2, w0=w0, w1=w1, w2=w2, cv=cv, e=e,
                rc=rc, ec=ec, ra=ra, oa=oa, g_a=g_a, g_c=g_c, sa=sa, sc=sc)


def _gate_specs(tile):
    def sec(s):
        return pl.BlockSpec((None, DA, TT), lambda i: (tile(i), s, 0))

    def halo(s):
        return pl.BlockSpec((None, DA, TB), lambda i: (jnp.maximum(tile(i) - 1, 0), s, TT // TB - 1))

    return [sec(0), sec(0), sec(1), sec(2), sec(3), sec(4), halo(2), halo(3),
            _full_spec((3, DA, TB)), _full_spec((DA, TB)), _full_spec((DA, TB))]


def _gate_outproj(o_t, gate_t, cw_b, ga_b, gcn_b, w_out, x, meta_full, fng, target, L):
    nj = L // TT
    rp = NM
    n_bwd = 8
    cb = D // 4
    assert P0 % rp == 0 and TB % rp == 0 and (TT // rp) % n_bwd == 0 and H == n_bwd

    def body(o_ref, za_ref, gb_ref, gc_ref, xcv_ref, zc_ref, gcp_ref, xcp_ref, cw_ref, ga_ref, gcn_ref,
             o2_ref, za2_ref, gb2_ref, gc2_ref, xcv2_ref, zc2_ref, gcp2_ref, xcp2_ref,
             w_ref, xa_ref, xb_ref, xc_ref, meta_ref, g_ref, ta_ref, tb_ref, tc_ref,
             dout_ref, dwb_ref, loss_ref, dg_ref, do_ref, dd_ref, dg5_ref, dga_ref, dgc_ref, dcw_ref,
             dw_ref, o_scr, db_new, db_old, mix_new, mix_old, dmix_new, dmix_old, sq_acc, dg_acc, carry_ref):
        t = pl.program_id(0)
        first_a = t == nj - 1
        first_c = t == nj + 1

        def gate_rows(h):
            rows = slice(h * DH, (h + 1) * DH)
            g = _gate_group(rows, o_ref, za_ref, gb_ref, gc_ref, xcv_ref, zc_ref, gcp_ref, xcp_ref,
                            cw_ref, ga_ref, gcn_ref, first_a)
            mix_new[rows, :] = (g["oa"] * g["g_a"] * (g["za"] * g["sa"])).astype(BF16)
            mix_new[DA + h * DH:DA + (h + 1) * DH, :] = (g["ec"] * g["g_c"] * (g["zc"] * g["sc"])).astype(BF16)

        def loss_rows(c):
            blk = c // (TB // rp)
            rows, out_rows = pl.ds((c % (TB // rp)) * rp, rp), pl.ds(c * rp, rp)
            h = (xa_ref, xb_ref, xc_ref)[blk][rows, :]
            if blk == 0:
                first = meta_ref[...] if c == P0 // rp else jnp.zeros((rp, D), F32)
                h = jnp.where(first_a, first, h)
            o = o_scr[out_rows, :] + h
            r = lax.rsqrt(jnp.mean(o * o, axis=-1, keepdims=True) + EPS)
            orn = o * r
            g = g_ref[...]
            diff = orn * g - (ta_ref, tb_ref, tc_ref)[blk][rows, :]
            if blk == 0:
                diff = diff * jnp.where(first_a, 0.0, 1.0)
            gy = diff * (g * (1.0 / D))
            dout = r * (gy - orn * jnp.mean(gy * orn, axis=-1, keepdims=True))
            dout_ref[out_rows, :] = dout
            db_new[out_rows, :] = dout.astype(BF16)
            sq, go = diff * diff, diff * orn
            sq_acc[...] += sq[:8] + sq[8:]
            dg_acc[...] += go[:8] + go[8:]

        def backward_cols(n):
            if n < 4:
                cols = slice(n * cb, (n + 1) * cb)
                dmix_new[cols, :] = _dot(db_old[...], w_ref[cols, :], NT_DIMS).T.astype(BF16)
            else:
                cols = slice((n - 4) * cb, (n - 3) * cb)
                dw_ref[:, cols] += _dot(mix_old[...], db_old[:, cols])

        def gate_bwd_rows(h):
            rows = slice(h * DH, (h + 1) * DH)
            sec = lambda s: slice(s * DA + h * DH, s * DA + (h + 1) * DH)
            g = _gate_group(rows, o2_ref, za2_ref, gb2_ref, gc2_ref, xcv2_ref, zc2_ref, gcp2_ref, xcp2_ref,
                            cw_ref, ga_ref, gcn_ref, first_c)
            o, za, gb, gc, xc, zc, sa, sc = (g[n] for n in ("o", "za", "gb", "gc", "xc", "zc", "sa", "sc"))
            dya = dmix_old[rows, :].astype(F32)
            dyc = dmix_old[DA + h * DH:DA + (h + 1) * DH, :].astype(F32)

            dn = dya * (za * sa)
            dg5_ref[sec(0), :] = (dya * (g["oa"] * g["g_a"]) * (sa * (1.0 + za * (1.0 - sa)))).astype(BF16)
            dga_ref[rows, :] += _lane_tiles_sum(dn * g["oa"])
            dng = dn * g["g_a"]
            mean_a = jnp.mean(dng * g["oa"], axis=0, keepdims=True)
            do = (dng - g["oa"] * mean_a) * g["ra"]
            do_ref[rows, :] = do.astype(BF16)
            dd_ref[h] = jnp.sum(do * o, axis=0, keepdims=True)

            dnc = dyc * (zc * sc)
            dg5_ref[sec(4), :] = (dyc * (g["ec"] * g["g_c"]) * (sc * (1.0 + zc * (1.0 - sc)))).astype(BF16)
            dgc_ref[rows, :] += _lane_tiles_sum(dnc * g["ec"])
            dncg = dnc * g["g_c"]
            mean_c = jnp.mean(dncg * g["ec"], axis=0, keepdims=True)
            de = (dncg - g["ec"] * mean_c) * g["rc"]
            dg5_ref[sec(1), :] = (de * g["cv"]).astype(BF16)
            dcv = de * gb
            full = jnp.concatenate([dcv, carry_ref[rows, :]], axis=1)
            d1 = pltpu.roll(full, TT + TB - 1, 1)[:, :TT]
            d2 = pltpu.roll(full, TT + TB - 2, 1)[:, :TT]
            carry_ref[rows, :] = dcv[:, :TB]
            da = g["w2"] * dcv + g["w1"] * d1 + g["w0"] * d2
            dg5_ref[sec(2), :] = (da * xc).astype(BF16)
            dg5_ref[sec(3), :] = (da * gc).astype(BF16)
            dcw_ref[0, rows, :] += _lane_tiles_sum(dcv * g["a2"])
            dcw_ref[1, rows, :] += _lane_tiles_sum(dcv * g["a1"])
            dcw_ref[2, rows, :] += _lane_tiles_sum(dcv * g["a"])

        def step(a, b, c):
            half = H // 2
            for h in range(H):
                if a:
                    gate_rows(h)
                if c and h < half:
                    gate_bwd_rows(h)
                if b and h % 2 == 1:
                    backward_cols(h // 2)
            if a:
                o_scr[...] = _dot(mix_new[...], w_ref[...], TN_DIMS)
            per = TT // rp // n_bwd
            for k in range(n_bwd):
                if a:
                    for piece in range(per * k, per * (k + 1)):
                        loss_rows(piece)
                if c and k % 2 == 0:
                    gate_bwd_rows(half + k // 2)
                if b and k % 2 == 1:
                    backward_cols(n_bwd // 2 + k // 2)
            if a:
                db_old[...] = db_new[...]
                mix_old[...] = mix_new[...]
            if b:
                dmix_old[...] = dmix_new[...]

        @pl.when(t == 0)
        def _():
            dw_ref[...] = jnp.zeros_like(dw_ref)
            sq_acc[...] = jnp.zeros_like(sq_acc)
            dg_acc[...] = jnp.zeros_like(dg_acc)
            carry_ref[...] = jnp.zeros_like(carry_ref)
            dga_ref[...] = jnp.zeros_like(dga_ref)
            dgc_ref[...] = jnp.zeros_like(dgc_ref)
            dcw_ref[...] = jnp.zeros_like(dcw_ref)
            step(True, False, False)

        @pl.when(t == 1)
        def _():
            step(True, True, False)

        @pl.when((t >= 2) & (t < nj))
        def _():
            step(True, True, True)

        @pl.when(t == nj)
        def _():
            step(False, True, True)
            dwb_ref[...] = dw_ref[...].astype(BF16)
            loss_ref[...] = jnp.sum(sq_acc[...], keepdims=True) * (0.5 / D)
            dg_ref[...] = jnp.sum(dg_acc[...], axis=0, keepdims=True) * (1.0 / D)

        @pl.when(t == nj + 1)
        def _():
            step(False, False, True)

    assert nj >= 2
    tile_a = lambda t: jnp.clip(nj - 1 - t, 0, nj - 1)
    tile_c = lambda t: jnp.clip(nj + 1 - t, 0, nj - 1)
    at_c = lambda shape: pl.BlockSpec(shape, lambda t: (0,) * (len(shape) - 1) + (tile_c(t),))
    return pl.pallas_call(
        body, name="gate_outproj", grid=(nj + 2,),
        in_specs=_gate_specs(tile_a) + _gate_specs(tile_c)[:8] + [_full_spec((D, D))] + _x_specs3(tile_a)
                 + [_full_spec((NM, D)), _full_spec((1, D))] + _x_specs3(tile_a),
        out_specs=[pl.BlockSpec((TT, D), lambda t: (tile_a(t), 0)), _full_spec((D, D)), _full_spec((1, 1)),
                   _full_spec((1, D)), at_c((DA, TT)), at_c((H, 1, TT)), at_c((5 * DA, TT)),
                   _full_spec((DA, TB)), _full_spec((DA, TB)), _full_spec((3, DA, TB))],
        out_shape=[jax.ShapeDtypeStruct((L, D), F32), jax.ShapeDtypeStruct((D, D), BF16),
                   jax.ShapeDtypeStruct((1, 1), F32), jax.ShapeDtypeStruct((1, D), F32),
                   jax.ShapeDtypeStruct((DA, L), BF16),
                   jax.ShapeDtypeStruct((H, 1, L), F32),
                   jax.ShapeDtypeStruct((5 * DA, L), BF16),
                   jax.ShapeDtypeStruct((DA, TB), F32),
                   jax.ShapeDtypeStruct((DA, TB), F32),
                   jax.ShapeDtypeStruct((3, DA, TB), F32)],
        scratch_shapes=[pltpu.VMEM((D, D), F32), pltpu.VMEM((TT, D), F32), pltpu.VMEM((TT, D), BF16),
                        pltpu.VMEM((TT, D), BF16), pltpu.VMEM((D, TT), BF16), pltpu.VMEM((D, TT), BF16),
                        pltpu.VMEM((D, TT), BF16), pltpu.VMEM((D, TT), BF16),
                        pltpu.VMEM((8, D), F32), pltpu.VMEM((8, D), F32), pltpu.VMEM((DA, TB), F32)],
        compiler_params=_params(),
    )(o_t, gate_t, gate_t, gate_t, gate_t, gate_t, gate_t, gate_t, cw_b, ga_b, gcn_b,
      o_t, gate_t, gate_t, gate_t, gate_t, gate_t, gate_t, gate_t,
      w_out, x, x, x, meta_full, fng, target, target, target)


def _attn_bwd(proj_t, kaug, vtok, do_t, lse, dd, cq, L, after):
    nk = L // TT

    def body(q_ref, kaug_ref, vtok_ref, kt_ref, do_ref, lse_ref, dd_ref, cq_ref, _,
             dq_ref, dk_ref, dv_ref, dck_ref, dcq_ref, dq_acc, kt1_scr, s_scr, dp_scr, dv_scr, dk_scr):
        i = pl.program_id(0)
        rows = [slice(g * DH, (g + 1) * DH) for g in range(HG)]
        ones = jnp.ones((DF, TT), BF16)
        zpad = jnp.zeros((KA - DH - DF, TT), BF16)
        for g in range(HG):
            kt1_scr[g] = jnp.concatenate([kt_ref[rows[g], :], ones], axis=0)
        dv_scr[...] = jnp.zeros_like(dv_scr)
        dk_scr[...] = jnp.zeros_like(dk_scr)

        def q_rows(g, q_off):
            bias = cq_ref[g, :, pl.ds(q_off, TT)] - lse_ref[g, :, pl.ds(q_off, TT)]
            return jnp.concatenate([q_ref[rows[g], pl.ds(q_off, TT)], _bias_rows(bias)], axis=0)

        def scores(jq, masked):
            q_off = pl.multiple_of(jq * TT, TT)
            for g in range(HG):
                s = _dot(kaug_ref[g], jnp.concatenate([q_rows(g, q_off), zpad], axis=0))
                if masked:
                    s = jnp.where(_causal_mask(), s, NEG)
                s_scr[g] = s
                dp_scr[g] = _dot(vtok_ref[g], do_ref[rows[g], pl.ds(q_off, TT)])

        def grads(jq):
            q_off = pl.multiple_of(jq * TT, TT)
            for g in range(HG):
                p = jnp.exp2(s_scr[g])
                ds = (p * (dp_scr[g] - dd_ref[g, :, pl.ds(q_off, TT)])).astype(BF16)
                do1 = jnp.concatenate([do_ref[rows[g], pl.ds(q_off, TT)], jnp.zeros((KA - DH, TT), BF16)], axis=0)
                q1 = jnp.concatenate([q_rows(g, q_off), zpad], axis=0)
                dv_scr[g] += _dot(p.astype(BF16), do1, NT_DIMS)
                dk_scr[g] += _dot(ds, q1, NT_DIMS)
                dq_acc[g, :, pl.ds(q_off, TT)] += _dot(kt1_scr[g], ds)

        @pl.when(i == 0)
        def _():
            dq_acc[...] = jnp.zeros_like(dq_acc)

        scores(i, True)

        def step(jq, c):
            grads(jq)
            scores(jq + 1, False)
            return c

        lax.fori_loop(i, nk - 1, step, 0)
        grads(nk - 1)
        for g in range(HG):
            dv_ref[rows[g], :] = dv_scr[g].T[:DH, :].astype(BF16)
            dk_t = dk_scr[g].T
            dk_ref[rows[g], :] = (dk_t[:DH, :] * LN2).astype(BF16)
            dck_ref[g] = dk_t[DH:DH + 1, :]

        @pl.when(i == nk - 1)
        def _():
            for g in range(HG):
                dq_ref[rows[g], :] = (dq_acc[g, :DH, :] * (DH ** -0.5)).astype(BF16)
                dcq_ref[g] = dq_acc[g, DH:DH + 1, :]

    assert HG == H
    head = lambda i: (0, 0)
    row = lambda i: (0, 0, 0)
    return pl.pallas_call(
        body, name="attn_bwd", grid=(nk,),
        in_specs=[
            pl.BlockSpec((DA, L), head),
            pl.BlockSpec((H, TT, KA), lambda i: (0, i, 0)),
            pl.BlockSpec((H, TT, DH), lambda i: (0, i, 0)),
            pl.BlockSpec((DA, TT), lambda i: (1, i)),
            pl.BlockSpec((DA, L), head),
            pl.BlockSpec((H, 1, L), row), pl.BlockSpec((H, 1, L), row), pl.BlockSpec((H, 1, L), row), _UNREAD,
        ],
        out_specs=[
            pl.BlockSpec((DA, L), head),
            pl.BlockSpec((DA, TT), lambda i: (0, i)),
            pl.BlockSpec((DA, TT), lambda i: (0, i)),
            pl.BlockSpec((H, 1, TT), lambda i: (0, 0, i)),
            pl.BlockSpec((H, 1, L), row),
        ],
        out_shape=[jax.ShapeDtypeStruct((DA, L), BF16), jax.ShapeDtypeStruct((DA, L), BF16),
                   jax.ShapeDtypeStruct((DA, L), BF16), jax.ShapeDtypeStruct((H, 1, L), F32),
                   jax.ShapeDtypeStruct((H, 1, L), F32)],
        scratch_shapes=[
            pltpu.VMEM((HG, DH + DF, L), F32),
            pltpu.VMEM((HG, DH + DF, TT), BF16),
            pltpu.VMEM((HG, TT, TT), F32), pltpu.VMEM((HG, TT, TT), F32),
            pltpu.VMEM((HG, TT, KA), F32), pltpu.VMEM((HG, TT, KA), F32)],
        compiler_params=_params(),
    )(proj_t, kaug, vtok, proj_t, do_t, lse, dd, cq, after)


def _fgate_bwd(dcq, dck, sg, L):
    def body(dcq_ref, dck_ref, sg_ref, df_ref, db_ref):
        dc = jnp.concatenate([dcq_ref[h] - dck_ref[h] for h in range(H)], axis=0)
        idx = lax.broadcasted_iota(jnp.int32, (H, L), 1)
        r = dc
        s = 1
        while s < L:
            r = r + jnp.where(idx + s < L, pltpu.roll(r, L - s, 1), 0.0)
            s *= 2
        df = r * sg_ref[...]
        db_ref[...] = jnp.broadcast_to(jnp.sum(df, axis=1, keepdims=True), (H, TB))
        df_ref[...] = jnp.concatenate([df, jnp.zeros((DF - H, L), F32)], axis=0).astype(BF16)

    return pl.pallas_call(
        body, name="fgate_bwd",
        out_shape=[jax.ShapeDtypeStruct((DF, L), BF16), jax.ShapeDtypeStruct((H, TB), F32)],
        compiler_params=pltpu.CompilerParams(vmem_limit_bytes=VMEM_LIMIT),
    )(dcq, dck, sg)


def _inproj_bwd_x(w, dq_t, dk_t, dv_t, dg5_t, df_t, dout, x, meta_full, norm_g, L, after):
    nj = L // TT
    seq = x.shape[0]

    def body(w_ref, dq_ref, dk_ref, dv_ref, dg5_ref, df_ref, dout_ref, xa_ref, xb_ref, xc_ref, meta_ref, g_ref, _,
             gx_ref, dmeta_ref, dg_ref, dh_scr, sems):
        j = pl.program_id(0)
        slot = j % 2

        def copy_out(step, slot_):
            first = pltpu.make_async_copy(dh_scr.at[slot_, pl.ds(TB, TT - TB)], gx_ref.at[pl.ds(0, TT - TB)],
                                          sems.at[slot_])
            later = pltpu.make_async_copy(dh_scr.at[slot_], gx_ref.at[pl.ds(step * TT - TB, TT)], sems.at[slot_])
            return first, later

        @pl.when(j == 0)
        def _():
            dg_ref[...] = jnp.zeros_like(dg_ref)

        du = _dot(dq_ref[...], w_ref[0:DA, :], TN_DIMS)
        du += _dot(dk_ref[...], w_ref[DA:2 * DA, :], TN_DIMS)
        du += _dot(dv_ref[...], w_ref[2 * DA:3 * DA, :], TN_DIMS)
        du += _dot(dg5_ref[...], w_ref[3 * DA:NSEC * DA, :], TN_DIMS)
        du += _dot(df_ref[...], w_ref[NSEC * DA:DPROJ, :], TN_DIMS)
        hb = _h_tile(j, xa_ref, xb_ref, xc_ref, meta_ref)
        r = lax.rsqrt(jnp.mean(hb * hb, axis=-1, keepdims=True) + EPS)
        hn = hb * r
        dg_ref[...] += jnp.sum(du * hn, axis=0, keepdims=True)
        gu = du * g_ref[...]
        dh = dout_ref[...] + r * gu - hn * (r * jnp.mean(gu * hn, axis=-1, keepdims=True))

        dh_scr[slot] = dh

        @pl.when(j == 0)
        def _():
            dmeta_ref[...] = dh[P0:TB, :]
            copy_out(0, 0)[0].start()

        @pl.when(j >= 1)
        def _():
            copy_out(j, slot)[1].start()

        @pl.when(j == 1)
        def _():
            copy_out(0, 0)[0].wait()

        @pl.when(j >= 2)
        def _():
            copy_out(j - 1, 1 - slot)[1].wait()

        @pl.when(j == nj - 1)
        def _():
            copy_out(j, slot)[0 if nj == 1 else 1].wait()

    blk = lambda rows: pl.BlockSpec((rows, TT), lambda j: (0, j))
    return pl.pallas_call(
        body, name="inproj_bwd_x", grid=(nj,),
        in_specs=[_full_spec((DPROJ, D)), blk(DA), blk(DA), blk(DA), blk(5 * DA), blk(DF),
                  pl.BlockSpec((TT, D), lambda j: (j, 0))] + _x_specs3()
                 + [_full_spec((NM, D)), _full_spec((1, D)), _UNREAD],
        out_specs=[pl.BlockSpec(memory_space=pl.ANY), _full_spec((NM, D)), _full_spec((1, D))],
        out_shape=[jax.ShapeDtypeStruct((seq, D), F32), jax.ShapeDtypeStruct((NM, D), F32),
                   jax.ShapeDtypeStruct((1, D), F32)],
        scratch_shapes=[pltpu.VMEM((2, TT, D), F32), pltpu.SemaphoreType.DMA((2,))],
        compiler_params=_params(),
    )(w, dq_t, dk_t, dv_t, dg5_t, df_t, dout, x, x, x, meta_full, norm_g, after)


def _inproj_bwd_w(u, dq_t, dk_t, dv_t, dg5_t, df_t, L):
    def body(u_ref, dq_hbm, dk_hbm, dv_hbm, dg5_ref, df_ref, dw_ref, dwf_ref, qkv_scr, sems):
        s = pl.program_id(0)
        u_all = u_ref[...]
        fetch = [pltpu.make_async_copy(src, qkv_scr.at[k], sems.at[k])
                 for k, src in enumerate((dq_hbm, dk_hbm, dv_hbm))]

        @pl.when(s == 0)
        def _():
            for cp in fetch:
                cp.start()

        @pl.when(s < 5)
        def _():
            dw_ref[...] = _dot(dg5_ref[...], u_all)

        for k in range(3):
            @pl.when(s == 5 + k)
            def _(k=k):
                fetch[k].wait()
                dw_ref[...] = _dot(qkv_scr[k], u_all)

        @pl.when(s == NSEC - 1)
        def _():
            dwf_ref[...] = _dot(df_ref[...], u_all)

    once = lambda shape: pl.BlockSpec(shape, lambda s: (0, 0), pipeline_mode=pl.Buffered(1))
    any_spec = pl.BlockSpec(memory_space=pl.ANY)
    return pl.pallas_call(
        body, name="inproj_bwd_w", grid=(NSEC,),
        in_specs=[
            once((L, D)), any_spec, any_spec, any_spec,
            pl.BlockSpec((DA, L), lambda s: (jnp.minimum(s, 4), 0)),
            once((DF, L)),
        ],
        out_specs=[pl.BlockSpec((DA, D), lambda s: (jnp.where(s < 5, s + 3, s - 5), 0)), _full_spec((DF, D))],
        out_shape=[jax.ShapeDtypeStruct((NSEC * DA, D), F32), jax.ShapeDtypeStruct((DF, D), F32)],
        scratch_shapes=[pltpu.VMEM((3, DA, L), BF16), pltpu.SemaphoreType.DMA((3,))],
        compiler_params=_params(),
    )(u, dq_t, dk_t, dv_t, dg5_t, df_t)


def _adamw(w, g, m, v):
    m = ADAM_B1 * m + (1.0 - ADAM_B1) * g
    v = ADAM_B2 * v + (1.0 - ADAM_B2) * (g * g)
    m_hat = m / (1.0 - ADAM_B1 ** ADAM_STEP)
    v_hat = v / (1.0 - ADAM_B2 ** ADAM_STEP)
    delta = -ADAM_LR * (m_hat / (jnp.sqrt(v_hat) + ADAM_EPS) + ADAM_WD * w)
    return delta, m, v


def _adamw_big(own_in, land_in, own_out, land_out, w_in_t, m_in_t, v_in_t, w_out, m_out, v_out):
    cb = CB
    e_sh = D // NDEV
    in_shape = jax.ShapeDtypeStruct(w_in_t.shape, F32)
    out_shape = jax.ShapeDtypeStruct(w_out.shape, F32)

    def total(own_ref, land_ref, rows, chips):
        g = _pick_slab(0, own_ref, land_ref, rows, chips=chips).astype(F32)
        for j in range(1, own_ref.shape[0]):
            g = g + _pick_slab(j, own_ref, land_ref, rows, chips=chips).astype(F32)
        return g

    def body(oi_ref, li_ref, oo_ref, lo_ref, wi_ref, mi_ref, vi_ref, wo_ref, mo_ref, vo_ref,
             gi, di, mi, vi, go, do, mo, vo):
        g = total(oi_ref, li_ref, slice(0, WSHP), True)[:WSH]
        d, mn, vn = _adamw(wi_ref[...], g, mi_ref[...], vi_ref[...])
        gi[...], di[...], mi[...], vi[...] = g, d, mn, vn
        g = total(oo_ref, lo_ref, slice(0, e_sh), False)
        d, mn, vn = _adamw(wo_ref[0], g, mo_ref[0], vo_ref[0])
        go[0], do[0], mo[0], vo[0] = g, d, mn, vn

    slab = lambda n, rows: pl.BlockSpec((n, rows, cb), lambda i: (0, 0, i))
    ispec = pl.BlockSpec((WSH, cb), lambda i: (0, i))
    ospec = pl.BlockSpec((1, e_sh, cb), lambda i: (0, 0, i))
    return pl.pallas_call(
        body, name="adamw_big", grid=(D // cb,),
        in_specs=[slab(4, WSHP), slab(4, WSHP), slab(NDEV, e_sh), slab(NDEV, e_sh),
                  ispec, ispec, ispec, ospec, ospec, ospec],
        out_specs=[ispec] * 4 + [ospec] * 4, out_shape=[in_shape] * 4 + [out_shape] * 4,
        compiler_params=_params(),
    )(own_in, land_in, own_out, land_out, w_in_t, m_in_t, v_in_t, w_out, m_out, v_out)


F0 = 3 * DA


def _unshard_w_out(own, land):
    e_sh = D // NDEV

    def body(own_ref, land_ref, wo_ref):
        for j in range(NDEV):
            wo_ref[j * e_sh:(j + 1) * e_sh, :] = _pick_slab(j, own_ref, land_ref, slice(0, e_sh), per_peer=False)

    return pl.pallas_call(
        body, name="unshard_w_out", grid=(D // CB,),
        in_specs=[pl.BlockSpec((e_sh, CB), lambda i: (0, i)), pl.BlockSpec((NDEV, e_sh, CB), lambda i: (0, 0, i))],
        out_specs=pl.BlockSpec((D, CB), lambda i: (0, i)),
        out_shape=jax.ShapeDtypeStruct((D, D), BF16),
        compiler_params=_params(),
    )(own, land)


def _unshard_w_in(w_all, small_all, attn_gain, conv_gain):
    def body(w_ref, small_ref, ga_ref, gc_ref, wt_ref, meta_ref, cwb_ref, gab_ref, gcb_ref):
        i = pl.program_id(0)
        for k in range(CB // TB):
            meta_ref[:, k * TB:(k + 1) * TB] = small_ref[(CB // TB) * i + k, 0:NM, :]

        @pl.when(i == 0)
        def _():
            per_row = lambda line: jnp.broadcast_to(line, (TB, DA)).T
            cw = jnp.concatenate([small_ref[j, NM:NM + 3, 0:DH] for j in range(NDEV)], axis=1)
            for k in range(3):
                cwb_ref[k] = per_row(cw[k:k + 1, :])
            gab_ref[...] = per_row(ga_ref[...])
            gcb_ref[...] = per_row(gc_ref[...])

        def ref_rows(lo, hi):
            pieces, r = [], lo
            while r < hi:
                sh, off = divmod(r, WSH)
                n = min(hi - r, WSH - off)
                pieces.append(w_ref[sh, off:off + n, :])
                r += n
            return pieces

        for s in range(NSEC):
            lo = s * DA if s < 3 else s * DA + H
            wt_ref[s * DA:(s + 1) * DA, :] = jnp.concatenate(ref_rows(lo, lo + DA), axis=0)
        wt_ref[NSEC * DA:DPROJ, :] = jnp.concatenate(
            ref_rows(F0, F0 + H) + [jnp.zeros((DF - H, CB), BF16)], axis=0)

    return pl.pallas_call(
        body, name="unshard_w_in", grid=(D // CB,),
        in_specs=[pl.BlockSpec((NDEV, WSHP, CB), lambda i: (0, 0, i)), _full_spec(small_all.shape),
                  _full_spec((1, DA)), _full_spec((1, DA))],
        out_specs=[pl.BlockSpec((DPROJ, CB), lambda i: (0, i)), pl.BlockSpec((NM, CB), lambda i: (0, i)),
                   _full_spec((3, DA, TB)), _full_spec((DA, TB)), _full_spec((DA, TB))],
        out_shape=[jax.ShapeDtypeStruct((DPROJ, D), BF16), jax.ShapeDtypeStruct((NM, D), F32),
                   jax.ShapeDtypeStruct((3, DA, TB), F32), jax.ShapeDtypeStruct((DA, TB), F32),
                   jax.ShapeDtypeStruct((DA, TB), F32)],
        compiler_params=_params(),
    )(w_all, small_all, attn_gain, conv_gain)


def _shard_w_in_grads(dw_main, dw_f):
    def body(dm_ref, df_ref, p_ref):
        mc = lax.axis_index("c")

        def ref_rows(lo, hi):
            pieces, r = [], lo
            while r < hi:
                if r < F0:
                    n = min(hi, F0) - r
                    pieces.append(dm_ref[r:r + n, :])
                elif r < F0 + H:
                    n = min(hi, F0 + H) - r
                    pieces.append(df_ref[r - F0:r - F0 + n, :])
                else:
                    n = hi - r
                    pieces.append(dm_ref[r - H:r - H + n, :])
                r += n
            return pieces

        for i in range(NDEV):
            rows = jnp.concatenate(ref_rows(i * WSH, (i + 1) * WSH) + [jnp.zeros((WSHP - WSH, CB), F32)], axis=0)
            p_ref[i // 2 + jnp.where(mc == i % 2, 0, 4)] = rows.astype(BF16)

    col = lambda rows: pl.BlockSpec((rows, CB), lambda i: (0, i))
    return pl.pallas_call(
        body, name="shard_w_in_grads", grid=(D // CB,),
        in_specs=[col(NSEC * DA), col(DF)],
        out_specs=pl.BlockSpec((NDEV, WSHP, CB), lambda i: (0, 0, i)),
        out_shape=jax.ShapeDtypeStruct((NDEV, WSHP, D), BF16),
        compiler_params=_params(),
    )(dw_main, dw_f)


SMALL = ("norm_g", "final_norm_g", "attn_norm_g", "conv_norm_g", "b_f", "meta", "conv_w")


def _as_rows(x):
    return jnp.concatenate([x[:, r * TB:(r + 1) * TB] for r in range(x.shape[1] // TB)], axis=0)


def _as_line(rows):
    return jnp.concatenate([rows[r:r + 1, :] for r in range(rows.shape[0])], axis=1)


def _pad_rows(x, n=8):
    return jnp.concatenate([x, jnp.zeros((n - x.shape[0], x.shape[1]), F32)], axis=0)


def _tile_rows(a, rows, lanes=TB):
    a = a.reshape(rows, lanes)
    return jnp.pad(a, ((0, -rows % 8), (0, TB - lanes)))


def _pack_small_grads(dg_norm, dg_final, dga_p, dgc_p, dcw_p, db_b, dmeta, loss):
    def body(dgn_ref, dgf_ref, dga_ref, dgc_ref, dcw_ref, db_ref, dmeta_ref, loss_ref, out_ref):
        def lane_sums(p):
            return jnp.sum(p.T, axis=0, keepdims=True)

        lane = lax.broadcasted_iota(jnp.int32, (1, TB), 1)
        b_row = jnp.where(lane == H, loss_ref[...], 0.0)
        for h in range(H):
            b_row = b_row + jnp.where(lane == h, db_ref[h:h + 1, :], 0.0)
        common = jnp.concatenate([
            _as_rows(dgn_ref[...]), _as_rows(dgf_ref[...]), _pad_rows(_as_rows(lane_sums(dga_ref[...]))),
            _pad_rows(_as_rows(lane_sums(dgc_ref[...]))), _pad_rows(b_row)], axis=0)
        dcw = [lane_sums(dcw_ref[k]) for k in range(3)]
        for j in range(NDEV):
            cw = jnp.concatenate(
                [jnp.concatenate([r[:, j * DH:(j + 1) * DH], jnp.zeros((1, TB - DH), F32)], axis=1) for r in dcw],
                axis=0)
            out_ref[j] = jnp.concatenate([common, dmeta_ref[:, j * TB:(j + 1) * TB], _pad_rows(cw)], axis=0)

    return pl.pallas_call(
        body, name="pack_small_grads", out_shape=jax.ShapeDtypeStruct((NDEV, SROWS, TB), F32),
    )(dg_norm, dg_final, dga_p, dgc_p, dcw_p, db_b, dmeta, loss)


def _adamw_small(own, land, params):
    flat = [a for n in SMALL for a in params[n]]

    def body(*refs):
        own_ref, land_ref = refs[:2]
        ins = refs[2:2 + 3 * len(SMALL)]
        outs = refs[2 + 3 * len(SMALL):]
        g = _pick_slab(0, own_ref, land_ref, slice(0, SROWS))
        for j in range(1, NDEV):
            g = g + _pick_slab(j, own_ref, land_ref, slice(0, SROWS))
        grads = dict(
            norm_g=_as_line(g[0:8]), final_norm_g=_as_line(g[8:16]), attn_norm_g=_as_line(g[16:20]),
            conv_norm_g=_as_line(g[24:28]), b_f=g[32:33, :H], meta=g[40:56], conv_w=g[56:59, :DH][None])
        for i, n in enumerate(SMALL):
            w_ref, m_ref, v_ref = ins[3 * i:3 * i + 3]
            d, mn, vn = _adamw(w_ref[...], grads[n], m_ref[...], v_ref[...])
            for o_ref, val in zip(outs[4 * i:4 * i + 4], (grads[n], d, mn, vn)):
                o_ref[...] = val
        outs[-1][...] = g[32:33, H:H + 1]

    shapes = [jax.ShapeDtypeStruct(params[n][0].shape, F32) for n in SMALL for _ in range(4)]
    res = pl.pallas_call(
        body, name="adamw_small", out_shape=shapes + [jax.ShapeDtypeStruct((1, 1), F32)],
    )(own, land, *flat)
    return {n: res[4 * i:4 * i + 4] for i, n in enumerate(SMALL)}, res[-1]


def kernel(x, meta, norm_g, w_in, b_f, conv_w, attn_norm_g, conv_norm_g, w_out, final_norm_g, loss_target, m_meta, m_norm_g, m_w_in, m_b_f, m_conv_w, m_attn_norm_g, m_conv_norm_g, m_w_out, m_final_norm_g, v_meta, v_norm_g, v_w_in, v_b_f, v_conv_w, v_attn_norm_g, v_conv_norm_g, v_w_out, v_final_norm_g):
    seq = x.shape[1]
    L = seq + TB
    assert x.shape == (1, seq, D) and L % TT == 0 and w_in.shape == (1, D, WSH)
    x2 = x[0]
    tgt = loss_target[0]

    w_in_slab = jnp.pad(w_in[0].T, ((0, WSHP - WSH), (0, 0))).astype(BF16)
    w_out_slab = w_out[0].astype(BF16)
    meta_slab = jnp.concatenate([meta, _tile_rows(conv_w[0], 3, DH)], axis=0)
    wout_flight = _split_start(w_out_slab, "gather_w_out_start", per_peer=False)
    w_all, small_all = _all_gather([w_in_slab, meta_slab], "gather_w_in")

    w_t, meta_full, cw_b, ga_b, gcn_b = _unshard_w_in(w_all, small_all, attn_norm_g, conv_norm_g)

    u, proj_t, gate_t, f_t, ktok, vtok = _inproj_fwd(x2, meta_full, norm_g, w_t, L, after=wout_flight[4])
    cq, kaug, sg = _fgate_fwd(f_t, b_f.reshape(H, 1), ktok, L)
    o_t, lse = _attn_fwd(proj_t, kaug, cq, L)

    w_out_own, w_out_land = _split_wait(wout_flight, o_t, "gather_w_out_wait", per_peer=False)
    w_out_full = _unshard_w_out(w_out_own, w_out_land)
    dout, dw_out, loss_part, dg_final, do_t, dd, dg5_t, dga_p, dgc_p, dcw_p = _gate_outproj(
        o_t, gate_t, cw_b, ga_b, gcn_b, w_out_full, x2, meta_full, final_norm_g.reshape(1, D), tgt, L)
    dwo_flight = _split_start(dw_out.reshape(NDEV, D // NDEV, D), "exchange_dw_out_start", per_peer=True)
    dq_t, dk_t, dv_t, dck, dcq = _attn_bwd(proj_t, kaug, vtok, do_t, lse, dd, cq, L, after=dwo_flight[4])
    df_t, db_f = _fgate_bwd(dcq, dck, sg, L)
    dw_main, dw_f = _inproj_bwd_w(u, dq_t, dk_t, dv_t, dg5_t, df_t, L)
    dwi_parts = _shard_w_in_grads(dw_main, dw_f)
    dwi_chip = _pair_sum(dwi_parts, _pair_exchange(dwi_parts, "exchange_dw_in_pair"))
    dwi_flight = _split_start(dwi_chip, "exchange_dw_in_start", per_peer=True, chips=True)
    grad_x, dmeta, dg_norm = _inproj_bwd_x(
        w_t, dq_t, dk_t, dv_t, dg5_t, df_t, dout, x2, meta_full, norm_g, L, after=dwi_flight[4])
    small_parts = _pack_small_grads(dg_norm, dg_final, dga_p, dgc_p, dcw_p, db_f, dmeta, loss_part)
    small_flight = _split_start(small_parts, "exchange_small_start", per_peer=True)
    dwo_own, dwo_land = _split_wait(dwo_flight, small_flight[4], "exchange_dw_out_wait", per_peer=True)
    dwi_own, dwi_land = _split_wait(dwi_flight, dwo_land, "exchange_dw_in_wait", per_peer=True, chips=True)

    big_out = _adamw_big(dwi_own, dwi_land, dwo_own, dwo_land,
                         w_in[0].T, m_w_in[0].T, v_w_in[0].T, w_out, m_w_out, v_w_out)
    g_w_in, d_w_in, nm_w_in, nv_w_in = [a.T[None] for a in big_out[:4]]
    g_w_out, d_w_out, nm_w_out, nv_w_out = big_out[4:]
    sm_own, sm_land = _split_wait(small_flight, big_out[4], "exchange_small_wait", per_peer=True)
    line = lambda a: a.reshape(1, D)
    small, loss = _adamw_small(sm_own, sm_land, dict(
        norm_g=(norm_g, m_norm_g, v_norm_g),
        final_norm_g=(line(final_norm_g), line(m_final_norm_g), line(v_final_norm_g)),
        attn_norm_g=(attn_norm_g, m_attn_norm_g, v_attn_norm_g),
        conv_norm_g=(conv_norm_g, m_conv_norm_g, v_conv_norm_g),
        b_f=(b_f, m_b_f, v_b_f), meta=(meta, m_meta, v_meta), conv_w=(conv_w, m_conv_w, v_conv_w)))
    small["final_norm_g"] = [a.reshape(D) for a in small["final_norm_g"]]
    order = ("meta", "norm_g", "w_in", "b_f", "conv_w", "attn_norm_g", "conv_norm_g", "w_out", "final_norm_g")
    groups = []
    for k, (wi, wo) in enumerate(((g_w_in, g_w_out), (d_w_in, d_w_out), (nm_w_in, nm_w_out), (nv_w_in, nv_w_out))):
        d = dict({n: small[n][k] for n in SMALL}, w_in=wi, w_out=wo)
        groups.append([d[n] for n in order])
    return (loss[0, 0], grad_x[None], *groups[0], *groups[1], *groups[2], *groups[3])
```

```python
import jax
import jax.numpy as jnp
from jax import lax
from jax.experimental import pallas as pl
from jax.experimental.pallas import tpu as pltpu

F32 = jnp.float32
BF16 = jnp.bfloat16

D = 1024
DA = 512
H = 8
DH = 64
NM = 16
TB = 128
P0 = TB - NM
TT = 3 * TB
HG = 8
NDEV = 8
NSEC = 8
DF = 16
DPROJ = NSEC * DA + DF
WSH = 513
WSHP = 528
WROWS = WSHP + D // NDEV
SROWS = 64
EPS = 1e-6
NEG = -1e30
LOG2E = 1.4426950408889634
LN2 = 0.6931471805599453
QSCALE = DH ** -0.5 * LOG2E
KA = 128
CB = 256
VMEM_LIMIT = 56 * 1024 * 1024

ADAM_LR = 0.001
ADAM_B1 = 0.9
ADAM_B2 = 0.999
ADAM_EPS = 1e-08
ADAM_WD = 0.01
ADAM_STEP = 10

NT_DIMS = (((1,), (1,)), ((), ()))
TN_DIMS = (((0,), (0,)), ((), ()))
MESH = pl.DeviceIdType.MESH


def _params(n_axes=1, vmem=VMEM_LIMIT):
    return pltpu.CompilerParams(dimension_semantics=("arbitrary",) * n_axes, vmem_limit_bytes=vmem)


def _dot(a, b, dims=None):
    if dims is None:
        return jnp.dot(a, b, preferred_element_type=F32)
    return lax.dot_general(a, b, dims, preferred_element_type=F32)


def _my_place():
    return lax.axis_index("x"), lax.axis_index("y"), lax.axis_index("c")


def _all_gather(xs, name):
    n = len(xs)

    def body(*refs):
        x_refs, out_refs = refs[:n], refs[n:2 * n]
        send_sems, recv_sems, local_sems = refs[2 * n:]
        mx, my, mc = _my_place()

        def across(px, py, pc, axis_a):
            flip_x = pc if axis_a else 1 - pc
            return (px + flip_x) % 2, (py + 1 - flip_x) % 2, pc

        def idx(p):
            return 4 * p[0] + 2 * p[1] + p[2]

        me, sib = (mx, my, mc), (mx, my, 1 - mc)
        a_nbr, b_nbr = across(*me, True), across(*me, False)
        diag = across(*b_nbr, True)
        sib_a, sib_b = across(*sib, True), across(*sib, False)
        sib_diag = across(*sib_b, True)

        waits = []
        for t in range(n):
            out_ref = out_refs[t]

            def copy(k, block, to, src=None, out_ref=out_ref, t=t):
                return pltpu.make_async_remote_copy(
                    src_ref=out_ref.at[idx(block)] if src is None else src, dst_ref=out_ref.at[idx(block)],
                    send_sem=send_sems.at[7 * t + k], recv_sem=recv_sems.at[7 * t + k],
                    device_id=to, device_id_type=MESH)

            mine = pltpu.make_async_copy(x_refs[t], out_ref.at[idx(me)], local_sems.at[t])
            mine.start()
            started = [copy(0, me, sib, src=x_refs[t]), copy(1, me, a_nbr, src=x_refs[t]),
                       copy(2, me, b_nbr, src=x_refs[t])]
            for cp in started:
                cp.start()
            waits.append((copy, mine, started))
        relays = ((1, a_nbr, ((3, b_nbr), (4, sib))), (2, b_nbr, ((5, sib),)), (3, diag, ((6, sib),)))
        for landed, block, onward in relays:
            for copy, _, started in waits:
                copy(landed, block, me).wait_recv()
                for k, to in onward:
                    started.append(copy(k, block, to))
                    started[-1].start()
        for copy, mine, started in waits:
            for k, block in ((0, sib), (4, sib_a), (5, sib_b), (6, sib_diag)):
                copy(k, block, me).wait_recv()
            for cp in started:
                cp.wait_send()
            mine.wait()

    any_spec = pl.BlockSpec(memory_space=pl.ANY)
    return pl.pallas_call(
        body, name=name,
        out_shape=[jax.ShapeDtypeStruct((NDEV,) + x.shape, x.dtype) for x in xs],
        in_specs=[any_spec] * n, out_specs=[any_spec] * n,
        scratch_shapes=[pltpu.SemaphoreType.DMA((7 * n,)), pltpu.SemaphoreType.DMA((7 * n,)),
                        pltpu.SemaphoreType.DMA((n,))],
    )(*xs)


_HBM = pl.BlockSpec(memory_space=pltpu.HBM)
_UNREAD = pl.BlockSpec(memory_space=pl.ANY)
_SEM = pl.BlockSpec(memory_space=pltpu.SEMAPHORE)
_EFFECT = pltpu.SideEffectType.DATAFLOW_SIDE_EFFECTING


def _peer_of(m, place):
    mx, my, mc = place
    return ((1 - mx) if m & 4 else mx, (1 - my) if m & 2 else my, (1 - mc) if m & 1 else mc)


def _party(chips):
    if chips:
        return (lambda p: 2 * p[0] + p[1]), (2, 4, 6)
    return (lambda p: 4 * p[0] + 2 * p[1] + p[2]), tuple(range(1, NDEV))


def _split_copies(src_ref, land_ref, send_sems, recv_sems, per_peer, incoming, chips):
    place = _my_place()
    slot, masks = _party(chips)
    me = slot(place)
    out = []
    for k, m in enumerate(masks):
        there = _peer_of(m, place)
        peer = slot(there)
        src = (src_ref.at[me] if incoming else src_ref.at[peer]) if per_peer else src_ref
        out.append(pltpu.make_async_remote_copy(
            src_ref=src, dst_ref=land_ref.at[peer if incoming else me],
            send_sem=send_sems.at[k], recv_sem=recv_sems.at[k], device_id=there, device_id_type=MESH))
    return out


def _split_start(src, name, per_peer, chips=False):
    slab = src.shape[1:] if per_peer else src.shape
    n = len(_party(chips)[1])

    def body(src_ref, land_ref, send_sems, recv_sems, src_thru, land_thru, token):
        for cp in _split_copies(src_ref, land_ref, send_sems, recv_sems, per_peer, False, chips):
            cp.start()
        token[...] = jnp.zeros_like(token)

    return pl.pallas_call(
        body, name=name,
        out_shape=(pltpu.SemaphoreType.DMA((n,)), pltpu.SemaphoreType.DMA((n,)),
                   pltpu.HBM(src.shape, src.dtype), pltpu.HBM((n + 1,) + slab, src.dtype),
                   jax.ShapeDtypeStruct((8, TB), F32)),
        in_specs=(_HBM, _HBM), out_specs=(_SEM, _SEM, _HBM, _HBM, pl.BlockSpec(memory_space=pltpu.VMEM)),
        input_output_aliases={0: 2, 1: 3},
        compiler_params=pltpu.CompilerParams(has_side_effects=_EFFECT),
    )(pltpu.with_memory_space_constraint(src, pltpu.HBM),
      pltpu.with_memory_space_constraint(lax.empty((n + 1,) + slab, src.dtype), pltpu.HBM))


def _split_wait(handles, after, name, per_peer, chips=False):
    send_sems, recv_sems, src_thru, land_thru, _ = handles

    def body(src_ref, land_ref, send_sems, recv_sems, after_ref, src_out, land_out):
        for cp in _split_copies(src_ref, land_ref, send_sems, recv_sems, per_peer, False, chips):
            cp.wait_send()
        for cp in _split_copies(src_ref, land_ref, send_sems, recv_sems, per_peer, True, chips):
            cp.wait_recv()

    return pl.pallas_call(
        body, name=name,
        out_shape=(pltpu.HBM(src_thru.shape, src_thru.dtype), pltpu.HBM(land_thru.shape, land_thru.dtype)),
        in_specs=(_HBM, _HBM, _SEM, _SEM, pl.BlockSpec(memory_space=pl.ANY)), out_specs=(_HBM, _HBM),
        input_output_aliases={0: 0, 1: 1},
        compiler_params=pltpu.CompilerParams(has_side_effects=_EFFECT),
    )(src_thru, land_thru, send_sems, recv_sems, after)


def _pick_slab(j, own_ref, land_ref, rows, per_peer=True, chips=False):
    me = _party(chips)[0](_my_place())
    own = (lambda: own_ref[j, rows, :]) if per_peer else (lambda: own_ref[rows, :])
    return lax.cond(me == j, own, lambda: land_ref[j, rows, :])


def _pair_exchange(p, name):
    def body(p_ref, got_ref, send_sems, recv_sems):
        mx, my, mc = _my_place()
        copies = [pltpu.make_async_remote_copy(
            src_ref=p_ref.at[4 + q], dst_ref=got_ref.at[q], send_sem=send_sems.at[q],
            recv_sem=recv_sems.at[q], device_id=(mx, my, 1 - mc), device_id_type=MESH) for q in range(4)]
        for cp in copies:
            cp.start()
        for cp in copies:
            cp.wait_recv()
        for cp in copies:
            cp.wait_send()

    any_spec = pl.BlockSpec(memory_space=pl.ANY)
    return pl.pallas_call(
        body, name=name, out_shape=jax.ShapeDtypeStruct((4,) + p.shape[1:], p.dtype),
        in_specs=[any_spec], out_specs=any_spec,
        scratch_shapes=[pltpu.SemaphoreType.DMA((4,)), pltpu.SemaphoreType.DMA((4,))],
    )(p)


def _pair_sum(p, got):
    rows = p.shape[1]

    def body(p_ref, got_ref, out_ref):
        for q in range(4):
            out_ref[q] = (p_ref[q].astype(F32) + got_ref[q].astype(F32)).astype(BF16)

    blk = lambda n: pl.BlockSpec((n, rows, CB), lambda i: (0, 0, i))
    return pl.pallas_call(
        body, name="pair_sum", grid=(D // CB,), in_specs=[blk(4), blk(4)], out_specs=blk(4),
        out_shape=jax.ShapeDtypeStruct((4, rows, D), BF16), compiler_params=_params(),
    )(p, got)


def _h_block(t, x_ref, meta_ref):
    first = jnp.concatenate([jnp.zeros((P0, D), F32), meta_ref[...]], axis=0)
    return jnp.where(t == 0, first, x_ref[...])


def _x_specs3(tile=lambda j: j):
    return [pl.BlockSpec((TB, D), lambda j: (jnp.maximum(3 * tile(j) - 1, 0), 0)),
            pl.BlockSpec((TB, D), lambda j: (3 * tile(j), 0)),
            pl.BlockSpec((TB, D), lambda j: (3 * tile(j) + 1, 0))]


def _h_tile(j, xa_ref, xb_ref, xc_ref, meta_ref):
    first = jnp.concatenate([jnp.zeros((P0, D), F32), meta_ref[...]], axis=0)
    return jnp.concatenate([jnp.where(j == 0, first, xa_ref[...]), xb_ref[...], xc_ref[...]], axis=0)


def _full_spec(shape):
    return pl.BlockSpec(shape, lambda *_: (0,) * len(shape))


def _sigmoid(z):
    return 1.0 / (1.0 + jnp.exp(-z))


def _lane_tiles_sum(x):
    out = x[:, :TB]
    for i in range(1, x.shape[1] // TB):
        out = out + x[:, i * TB:(i + 1) * TB]
    return out


def _inproj_fwd(x, meta_full, norm_g, w_t, L, after):
    nj = L // TT

    def body(xa_ref, xb_ref, xc_ref, meta_ref, g_ref, w_ref, _, u_ref, proj_ref, gate_ref, f_ref, ktok_ref, vtok_ref):
        hb = _h_tile(pl.program_id(0), xa_ref, xb_ref, xc_ref, meta_ref)
        r = lax.rsqrt(jnp.mean(hb * hb, axis=-1, keepdims=True) + EPS)
        u = (hb * r * g_ref[...]).astype(BF16)
        u_ref[...] = u
        for s in range(NSEC):
            p = _dot(u, w_ref[s * DA:(s + 1) * DA, :], NT_DIMS)
            if s == 0:
                p = p * QSCALE
            if s in (1, 2):
                tok_ref = ktok_ref if s == 1 else vtok_ref
                for h in range(H):
                    tok_ref[h] = p[:, h * DH:(h + 1) * DH].astype(BF16)
            out_ref, s_out = (proj_ref, s) if s < 3 else (gate_ref, s - 3)
            out_ref[s_out * DA:(s_out + 1) * DA, :] = p.T.astype(BF16)
        f_ref[...] = _dot(w_ref[NSEC * DA:DPROJ, :], u, NT_DIMS)[:H]

    return pl.pallas_call(
        body, name="inproj_fwd", grid=(nj,),
        in_specs=_x_specs3() + [_full_spec((NM, D)), _full_spec((1, D)), _full_spec((DPROJ, D)), _UNREAD],
        out_specs=[
            pl.BlockSpec((TT, D), lambda t: (t, 0)),
            pl.BlockSpec((3 * DA, TT), lambda t: (0, t)),
            pl.BlockSpec((None, (NSEC - 3) * DA, TT), lambda t: (t, 0, 0)),
            pl.BlockSpec((H, TT), lambda t: (0, t)),
            pl.BlockSpec((H, TT, DH), lambda t: (0, t, 0)),
            pl.BlockSpec((H, TT, DH), lambda t: (0, t, 0)),
        ],
        out_shape=[
            jax.ShapeDtypeStruct((L, D), BF16),
            jax.ShapeDtypeStruct((3 * DA, L), BF16),
            jax.ShapeDtypeStruct((nj, (NSEC - 3) * DA, TT), BF16),
            jax.ShapeDtypeStruct((H, L), F32),
            jax.ShapeDtypeStruct((H, L, DH), BF16),
            jax.ShapeDtypeStruct((H, L, DH), BF16),
        ],
        compiler_params=_params(),
    )(x, x, x, meta_full, norm_g, w_t, after)


def _split3(x):
    hi = x.astype(BF16).astype(F32)
    r = x - hi
    mid = r.astype(BF16).astype(F32)
    return hi, mid, (r - mid).astype(BF16).astype(F32)


def _bias_rows(bias):
    one = jnp.ones((1, TT), F32)
    zero = jnp.zeros((1, TT), F32)
    parts = [zero] * 3 if bias is None else list(_split3(bias))
    return jnp.concatenate([one] * 3 + parts + [zero] * (DF - 6), axis=0).astype(BF16)


def _fgate_fwd(f_t, b_col, ktok, L):
    nb = L // TB

    def body(f_ref, b_ref, ktok_ref, cq_ref, kaug_ref, sg_ref, bias_scr):
        h = pl.program_id(0)

        @pl.when(h == 0)
        def _():
            z = f_ref[...] + b_ref[...]
            idx = lax.broadcasted_iota(jnp.int32, (H, L), 1)
            real = idx >= P0
            lf = jnp.where(real, jnp.minimum(z, 0.0) - jnp.log1p(jnp.exp(-jnp.abs(z))), 0.0)
            sg_ref[...] = jnp.where(real, 1.0 / (1.0 + jnp.exp(z)), 0.0)
            c = lf
            s = 1
            while s < L:
                c = c + jnp.where(idx >= s, pltpu.roll(c, s, 1), 0.0)
                s *= 2
            c = c * LOG2E
            for hh in range(H):
                cq_ref[hh] = c[hh:hh + 1, :]
            for part, val in enumerate(_split3(-jnp.where(real, c, -NEG))):
                for hh in range(H):
                    bias_scr[part * H + hh] = val[hh:hh + 1, :]

        lane = lax.broadcasted_iota(jnp.int32, (TB, KA), 1)
        head = jnp.zeros((DH, TB), F32)
        tail = jnp.concatenate([jnp.ones((3, TB), F32), jnp.zeros((KA - DH - 6, TB), F32)], axis=0)
        for b in range(nb):
            blk = slice(b * TB, (b + 1) * TB)
            cols = jnp.concatenate(
                [head] + [bias_scr[part * H + h, :, blk] for part in range(3)] + [tail], axis=0).T
            k = jnp.concatenate([ktok_ref[0, blk, :].astype(F32), jnp.zeros((TB, KA - DH), F32)], axis=1)
            kaug_ref[0, blk, :] = jnp.where(lane < DH, k, cols).astype(BF16)

    return pl.pallas_call(
        body, name="fgate_fwd", grid=(H,),
        in_specs=[_full_spec((H, L)), _full_spec((H, 1)), pl.BlockSpec((1, L, DH), lambda h: (h, 0, 0))],
        out_specs=[_full_spec((H, 1, L)), pl.BlockSpec((1, L, KA), lambda h: (h, 0, 0)), _full_spec((H, L))],
        out_shape=[
            jax.ShapeDtypeStruct((H, 1, L), F32),
            jax.ShapeDtypeStruct((H, L, KA), BF16),
            jax.ShapeDtypeStruct((H, L), F32),
        ],
        scratch_shapes=[pltpu.VMEM((3 * H, 1, L), F32)],
        compiler_params=_params(),
    )(f_t, b_col, ktok)


def _causal_mask():
    r = lax.broadcasted_iota(jnp.int32, (TT, TT), 0)
    c = lax.broadcasted_iota(jnp.int32, (TT, TT), 1)
    return r <= c


def _attn_fwd(proj_t, kaug, cq, L):
    nq = L // TT

    def body(q_ref, qn_ref, kaug_ref, v_ref, cq_ref, o_ref, lse_ref,
             qa_scr, s_scr, cmax_scr, m_scr, p_scr, alpha_scr, acc_scr):
        j = pl.program_id(0)
        rows = [slice(g * DH, (g + 1) * DH) for g in range(HG)]
        ones = jnp.ones((DF, TT), BF16)

        def load_queries(ref):
            for g in range(HG):
                qa_scr[g] = jnp.concatenate(
                    [ref[rows[g], :], _bias_rows(None), jnp.zeros((KA - DH - DF, TT), BF16)], axis=0)

        def scores(kt, masked):
            k_off = pl.multiple_of(kt * TT, TT)
            for g in range(HG):
                s = _dot(kaug_ref[g, pl.ds(k_off, TT), :], qa_scr[g])
                if masked:
                    s = jnp.where(_causal_mask(), s, NEG)
                s_scr[g] = s
                cmax_scr[g] = jnp.max(s, axis=0, keepdims=True)

        def softmax():
            for g in range(HG):
                m_old = m_scr[g]
                m_new = jnp.maximum(m_old, cmax_scr[g])
                alpha_scr[g] = jnp.exp2(m_old - m_new)
                p_scr[g] = jnp.exp2(s_scr[g] - m_new).astype(BF16)
                m_scr[g] = m_new

        def weighted_sum(kt):
            k_off = pl.multiple_of(kt * TT, TT)
            for g in range(HG):
                v1 = jnp.concatenate([v_ref[rows[g], pl.ds(k_off, TT)], ones], axis=0)
                acc_scr[g] = alpha_scr[g] * acc_scr[g] + _dot(v1, p_scr[g])

        @pl.when(j == 0)
        def _():
            load_queries(q_ref)
            scores(0, True)

        m_scr[...] = jnp.full_like(m_scr, NEG)
        acc_scr[...] = jnp.zeros_like(acc_scr)

        @pl.when(j >= 1)
        def _():
            softmax()
            scores(j - 1, False)

        def step(i, c):
            weighted_sum(j - i + 1)
            softmax()
            scores(j - i - 1, False)
            return c

        lax.fori_loop(1, j, step, 0)

        def drain(second_last, next_tile):
            if second_last:
                weighted_sum(1)
            softmax()
            if next_tile:
                load_queries(qn_ref)
                scores(j + 1, True)
            weighted_sum(0)

        @pl.when(j == 0)
        def _():
            drain(False, nq > 1)

        @pl.when((j >= 1) & (j < nq - 1))
        def _():
            drain(True, True)

        @pl.when((j >= 1) & (j == nq - 1))
        def _():
            drain(True, False)

        for g in range(HG):
            l = acc_scr[g, DH:DH + 1, :]
            o_ref[rows[g], :] = acc_scr[g, :DH, :] * (1.0 / l)
            lse_ref[g] = m_scr[g] + jnp.log2(l) + cq_ref[g]

    assert HG == H
    return pl.pallas_call(
        body, name="attn_fwd", grid=(nq,),
        in_specs=[
            pl.BlockSpec((DA, TT), lambda j: (0, j)),
            pl.BlockSpec((DA, TT), lambda j: (0, jnp.minimum(j + 1, nq - 1))),
            pl.BlockSpec((H, L, KA), lambda j: (0, 0, 0)),
            pl.BlockSpec((DA, L), lambda j: (2, 0)),
            pl.BlockSpec((H, 1, TT), lambda j: (0, 0, j)),
        ],
        out_specs=[
            pl.BlockSpec((None, DA, TT), lambda j: (j, 0, 0)),
            pl.BlockSpec((H, 1, TT), lambda j: (0, 0, j)),
        ],
        out_shape=[jax.ShapeDtypeStruct((nq, DA, TT), F32), jax.ShapeDtypeStruct((H, 1, L), F32)],
        scratch_shapes=[pltpu.VMEM((HG, KA, TT), BF16), pltpu.VMEM((HG, TT, TT), F32), pltpu.VMEM((HG, 1, TT), F32),
                        pltpu.VMEM((HG, 1, TT), F32), pltpu.VMEM((HG, TT, TT), BF16), pltpu.VMEM((HG, 1, TT), F32),
                        pltpu.VMEM((HG, DH + DF, TT), F32)],
        compiler_params=_params(),
    )(proj_t, proj_t, kaug, proj_t, cq)


def _gate_group(rows, o_ref, za_ref, gb_ref, gc_ref, xc_ref, zc_ref, gcp_ref, xcp_ref, cw_ref, ga_ref, gcn_ref, first):
    n_rep = TT // TB
    f32 = lambda r: r[rows, :].astype(F32)
    o, za, gb, gc, xc, zc = o_ref[rows, :], f32(za_ref), f32(gb_ref), f32(gc_ref), f32(xc_ref), f32(zc_ref)
    a = gc * xc
    a_prev = jnp.where(first, 0.0, f32(gcp_ref) * f32(xcp_ref))
    full = jnp.concatenate([a_prev, a], axis=1)
    a1 = pltpu.roll(full, 1, 1)[:, TB:]
    a2 = pltpu.roll(full, 2, 1)[:, TB:]
    w0 = jnp.tile(cw_ref[0, rows, :], (1, n_rep))
    w1 = jnp.tile(cw_ref[1, rows, :], (1, n_rep))
    w2 = jnp.tile(cw_ref[2, rows, :], (1, n_rep))
    cv = w0 * a2 + w1 * a1 + w2 * a
    e = gb * cv
    rc = lax.rsqrt(jnp.mean(e * e, axis=0, keepdims=True) + EPS)
    ec = e * rc
    ra = lax.rsqrt(jnp.mean(o * o, axis=0, keepdims=True) + EPS)
    oa = o * ra
    g_a = jnp.tile(ga_ref[rows, :], (1, n_rep))
    g_c = jnp.tile(gcn_ref[rows, :], (1, n_rep))
    sa = _sigmoid(za)
    sc = _sigmoid(zc)
    return dict(o=o, za=za, gb=gb, gc=gc, xc=xc, zc=zc, a=a, a1=a1, a2=a2, w0=w0, w1=w1, w2=w2, cv=cv, e=e,
                rc=rc, ec=ec, ra=ra, oa=oa, g_a=g_a, g_c=g_c, sa=sa, sc=sc)


def _gate_specs(tile):
    halo = pl.BlockSpec((None, 2 * DA, TB),
                        lambda i: (jnp.maximum(tile(i) - 1, 0), 1, TT // TB - 1))
    return [pl.BlockSpec((None, DA, TT), lambda i: (tile(i), 0, 0)),
            pl.BlockSpec((None, 5 * DA, TT), lambda i: (tile(i), 0, 0)), halo,
            _full_spec((3, DA, TB)), _full_spec((DA, TB)), _full_spec((DA, TB))]


def _gate_views(g5_ref, halo_ref):
    return [g5_ref.at[pl.ds(s * DA, DA)] for s in range(5)] + [halo_ref.at[pl.ds(s * DA, DA)] for s in range(2)]


def _gate_outproj(o_t, gate_t, cw_b, ga_b, gcn_b, w_out, x, meta_full, fng, target, L):
    nj = L // TT
    rp = NM
    n_bwd = 8
    cb = D // 4
    assert P0 % rp == 0 and TB % rp == 0 and (TT // rp) % n_bwd == 0 and H == n_bwd

    def body(o_ref, g5_ref, halo_ref, cw_ref, ga_ref, gcn_ref, o2_ref, g52_ref, halo2_ref,
             w_ref, xa_ref, xb_ref, xc_ref, meta_ref, g_ref, ta_ref, tb_ref, tc_ref,
             dout_ref, dwb_ref, loss_ref, dg_ref, do_ref, dd_ref, dg5_ref, dga_ref, dgc_ref, dcw_ref,
             dw_ref, o_scr, db_new, db_old, mix_new, mix_old, dmix_new, dmix_old, sq_acc, dg_acc, carry_ref):
        za_ref, gb_ref, gc_ref, xcv_ref, zc_ref, gcp_ref, xcp_ref = _gate_views(g5_ref, halo_ref)
        za2_ref, gb2_ref, gc2_ref, xcv2_ref, zc2_ref, gcp2_ref, xcp2_ref = _gate_views(g52_ref, halo2_ref)
        t = pl.program_id(0)
        first_a = t == nj - 1
        first_c = t == nj + 1

        def gate_rows(h):
            rows = slice(h * DH, (h + 1) * DH)
            g = _gate_group(rows, o_ref, za_ref, gb_ref, gc_ref, xcv_ref, zc_ref, gcp_ref, xcp_ref,
                            cw_ref, ga_ref, gcn_ref, first_a)
            mix_new[rows, :] = (g["oa"] * g["g_a"] * (g["za"] * g["sa"])).astype(BF16)
            mix_new[DA + h * DH:DA + (h + 1) * DH, :] = (g["ec"] * g["g_c"] * (g["zc"] * g["sc"])).astype(BF16)

        def loss_rows(c):
            blk = c // (TB // rp)
            rows, out_rows = pl.ds((c % (TB // rp)) * rp, rp), pl.ds(c * rp, rp)
            h = (xa_ref, xb_ref, xc_ref)[blk][rows, :]
            if blk == 0:
                first = meta_ref[...] if c == P0 // rp else jnp.zeros((rp, D), F32)
                h = jnp.where(first_a, first, h)
            o = o_scr[out_rows, :] + h
            r = lax.rsqrt(jnp.mean(o * o, axis=-1, keepdims=True) + EPS)
            orn = o * r
            g = g_ref[...]
            diff = orn * g - (ta_ref, tb_ref, tc_ref)[blk][rows, :]
            if blk == 0:
                diff = diff * jnp.where(first_a, 0.0, 1.0)
            gy = diff * (g * (1.0 / D))
            dout = r * (gy - orn * jnp.mean(gy * orn, axis=-1, keepdims=True))
            dout_ref[out_rows, :] = dout
            db_new[out_rows, :] = dout.astype(BF16)
            sq, go = diff * diff, diff * orn
            sq_acc[...] += sq[:8] + sq[8:]
            dg_acc[...] += go[:8] + go[8:]

        def backward_cols(n):
            if n < 4:
                cols = slice(n * cb, (n + 1) * cb)
                dmix_new[cols, :] = _dot(db_old[...], w_ref[cols, :], NT_DIMS).T.astype(BF16)
            else:
                cols = slice((n - 4) * cb, (n - 3) * cb)
                dw_ref[:, cols] += _dot(mix_old[...], db_old[:, cols])

        def gate_bwd_rows(h):
            rows = slice(h * DH, (h + 1) * DH)
            sec = lambda s: slice(s * DA + h * DH, s * DA + (h + 1) * DH)
            g = _gate_group(rows, o2_ref, za2_ref, gb2_ref, gc2_ref, xcv2_ref, zc2_ref, gcp2_ref, xcp2_ref,
                            cw_ref, ga_ref, gcn_ref, first_c)
            o, za, gb, gc, xc, zc, sa, sc = (g[n] for n in ("o", "za", "gb", "gc", "xc", "zc", "sa", "sc"))
            dya = dmix_old[rows, :].astype(F32)
            dyc = dmix_old[DA + h * DH:DA + (h + 1) * DH, :].astype(F32)

            dn = dya * (za * sa)
            dg5_ref[sec(0), :] = (dya * (g["oa"] * g["g_a"]) * (sa * (1.0 + za * (1.0 - sa)))).astype(BF16)
            dga_ref[rows, :] += _lane_tiles_sum(dn * g["oa"])
            dng = dn * g["g_a"]
            mean_a = jnp.mean(dng * g["oa"], axis=0, keepdims=True)
            do = (dng - g["oa"] * mean_a) * g["ra"]
            do_ref[rows, :] = do.astype(BF16)
            dd_ref[h] = jnp.sum(do * o, axis=0, keepdims=True)

            dnc = dyc * (zc * sc)
            dg5_ref[sec(4), :] = (dyc * (g["ec"] * g["g_c"]) * (sc * (1.0 + zc * (1.0 - sc)))).astype(BF16)
            dgc_ref[rows, :] += _lane_tiles_sum(dnc * g["ec"])
            dncg = dnc * g["g_c"]
            mean_c = jnp.mean(dncg * g["ec"], axis=0, keepdims=True)
            de = (dncg - g["ec"] * mean_c) * g["rc"]
            dg5_ref[sec(1), :] = (de * g["cv"]).astype(BF16)
            dcv = de * gb
            full = jnp.concatenate([dcv, carry_ref[rows, :]], axis=1)
            d1 = pltpu.roll(full, TT + TB - 1, 1)[:, :TT]
            d2 = pltpu.roll(full, TT + TB - 2, 1)[:, :TT]
            carry_ref[rows, :] = dcv[:, :TB]
            da = g["w2"] * dcv + g["w1"] * d1 + g["w0"] * d2
            dg5_ref[sec(2), :] = (da * xc).astype(BF16)
            dg5_ref[sec(3), :] = (da * gc).astype(BF16)
            dcw_ref[0, rows, :] += _lane_tiles_sum(dcv * g["a2"])
            dcw_ref[1, rows, :] += _lane_tiles_sum(dcv * g["a1"])
            dcw_ref[2, rows, :] += _lane_tiles_sum(dcv * g["a"])

        def step(a, b, c):
            half = H // 2
            for h in range(H):
                if a:
                    gate_rows(h)
                if c and h < half:
                    gate_bwd_rows(h)
                if b and h % 2 == 1:
                    backward_cols(h // 2)
            if a:
                o_scr[...] = _dot(mix_new[...], w_ref[...], TN_DIMS)
            per = TT // rp // n_bwd
            for k in range(n_bwd):
                if a:
                    for piece in range(per * k, per * (k + 1)):
                        loss_rows(piece)
                if c and k % 2 == 0:
                    gate_bwd_rows(half + k // 2)
                if b and k % 2 == 1:
                    backward_cols(n_bwd // 2 + k // 2)
            if a:
                db_old[...] = db_new[...]
                mix_old[...] = mix_new[...]
            if b:
                dmix_old[...] = dmix_new[...]

        @pl.when(t == 0)
        def _():
            dw_ref[...] = jnp.zeros_like(dw_ref)
            sq_acc[...] = jnp.zeros_like(sq_acc)
            dg_acc[...] = jnp.zeros_like(dg_acc)
            carry_ref[...] = jnp.zeros_like(carry_ref)
            dga_ref[...] = jnp.zeros_like(dga_ref)
            dgc_ref[...] = jnp.zeros_like(dgc_ref)
            dcw_ref[...] = jnp.zeros_like(dcw_ref)
            step(True, False, False)

        @pl.when(t == 1)
        def _():
            step(True, True, False)

        @pl.when((t >= 2) & (t < nj))
        def _():
            step(True, True, True)

        @pl.when(t == nj)
        def _():
            step(False, True, True)
            dwb_ref[...] = dw_ref[...].astype(BF16)
            loss_ref[...] = jnp.sum(sq_acc[...], keepdims=True) * (0.5 / D)
            dg_ref[...] = jnp.sum(dg_acc[...], axis=0, keepdims=True) * (1.0 / D)

        @pl.when(t == nj + 1)
        def _():
            step(False, False, True)

    assert nj >= 2
    tile_a = lambda t: jnp.clip(nj - 1 - t, 0, nj - 1)
    tile_c = lambda t: jnp.clip(nj + 1 - t, 0, nj - 1)
    at_c = lambda shape: pl.BlockSpec(shape, lambda t: (0,) * (len(shape) - 1) + (tile_c(t),))
    return pl.pallas_call(
        body, name="gate_outproj", grid=(nj + 2,),
        in_specs=_gate_specs(tile_a) + _gate_specs(tile_c)[:3] + [_full_spec((D, D))] + _x_specs3(tile_a)
                 + [_full_spec((NM, D)), _full_spec((1, D))] + _x_specs3(tile_a),
        out_specs=[pl.BlockSpec((TT, D), lambda t: (tile_a(t), 0)), _full_spec((D, D)), _full_spec((1, 1)),
                   _full_spec((1, D)), at_c((DA, TT)), at_c((H, 1, TT)), at_c((5 * DA, TT)),
                   _full_spec((DA, TB)), _full_spec((DA, TB)), _full_spec((3, DA, TB))],
        out_shape=[jax.ShapeDtypeStruct((L, D), F32), jax.ShapeDtypeStruct((D, D), BF16),
                   jax.ShapeDtypeStruct((1, 1), F32), jax.ShapeDtypeStruct((1, D), F32),
                   jax.ShapeDtypeStruct((DA, L), BF16),
                   jax.ShapeDtypeStruct((H, 1, L), F32),
                   jax.ShapeDtypeStruct((5 * DA, L), BF16),
                   jax.ShapeDtypeStruct((DA, TB), F32),
                   jax.ShapeDtypeStruct((DA, TB), F32),
                   jax.ShapeDtypeStruct((3, DA, TB), F32)],
        scratch_shapes=[pltpu.VMEM((D, D), F32), pltpu.VMEM((TT, D), F32), pltpu.VMEM((TT, D), BF16),
                        pltpu.VMEM((TT, D), BF16), pltpu.VMEM((D, TT), BF16), pltpu.VMEM((D, TT), BF16),
                        pltpu.VMEM((D, TT), BF16), pltpu.VMEM((D, TT), BF16),
                        pltpu.VMEM((8, D), F32), pltpu.VMEM((8, D), F32), pltpu.VMEM((DA, TB), F32)],
        compiler_params=_params(),
    )(o_t, gate_t, gate_t, cw_b, ga_b, gcn_b, o_t, gate_t, gate_t,
      w_out, x, x, x, meta_full, fng, target, target, target)


def _attn_bwd(proj_t, kaug, vtok, do_t, lse, dd, cq, L, after):
    nk = L // TT

    def body(q_ref, kaug_ref, vtok_ref, kt_ref, do_ref, lse_ref, dd_ref, cq_ref, _,
             dq_ref, dk_ref, dv_ref, dck_ref, dcq_ref, dq_acc, kt1_scr, s_scr, dp_scr, dv_scr, dk_scr):
        i = pl.program_id(0)
        rows = [slice(g * DH, (g + 1) * DH) for g in range(HG)]
        ones = jnp.ones((DF, TT), BF16)
        zpad = jnp.zeros((KA - DH - DF, TT), BF16)
        for g in range(HG):
            kt1_scr[g] = jnp.concatenate([kt_ref[rows[g], :], ones], axis=0)
        dv_scr[...] = jnp.zeros_like(dv_scr)
        dk_scr[...] = jnp.zeros_like(dk_scr)

        def q_rows(g, q_off):
            bias = cq_ref[g, :, pl.ds(q_off, TT)] - lse_ref[g, :, pl.ds(q_off, TT)]
            return jnp.concatenate([q_ref[rows[g], pl.ds(q_off, TT)], _bias_rows(bias)], axis=0)

        def scores(jq, masked):
            q_off = pl.multiple_of(jq * TT, TT)
            for g in range(HG):
                s = _dot(kaug_ref[g], jnp.concatenate([q_rows(g, q_off), zpad], axis=0))
                if masked:
                    s = jnp.where(_causal_mask(), s, NEG)
                s_scr[g] = s
                dp_scr[g] = _dot(vtok_ref[g], do_ref[rows[g], pl.ds(q_off, TT)])

        def grads(jq):
            q_off = pl.multiple_of(jq * TT, TT)
            for g in range(HG):
                p = jnp.exp2(s_scr[g])
                ds = (p * (dp_scr[g] - dd_ref[g, :, pl.ds(q_off, TT)])).astype(BF16)
                do1 = jnp.concatenate([do_ref[rows[g], pl.ds(q_off, TT)], jnp.zeros((KA - DH, TT), BF16)], axis=0)
                q1 = jnp.concatenate([q_rows(g, q_off), zpad], axis=0)
                dv_scr[g] += _dot(p.astype(BF16), do1, NT_DIMS)
                dk_scr[g] += _dot(ds, q1, NT_DIMS)
                dq_acc[g, :, pl.ds(q_off, TT)] += _dot(kt1_scr[g], ds)

        @pl.when(i == 0)
        def _():
            dq_acc[...] = jnp.zeros_like(dq_acc)

        scores(i, True)

        def step(jq, c):
            grads(jq)
            scores(jq + 1, False)
            return c

        lax.fori_loop(i, nk - 1, step, 0)
        grads(nk - 1)
        for g in range(HG):
            dv_ref[rows[g], :] = dv_scr[g].T[:DH, :].astype(BF16)
            dk_t = dk_scr[g].T
            dk_ref[rows[g], :] = (dk_t[:DH, :] * LN2).astype(BF16)
            dck_ref[g] = dk_t[DH:DH + 1, :]

        @pl.when(i == nk - 1)
        def _():
            for g in range(HG):
                dq_ref[rows[g], :] = (dq_acc[g, :DH, :] * (DH ** -0.5)).astype(BF16)
                dcq_ref[g] = dq_acc[g, DH:DH + 1, :]

    assert HG == H
    head = lambda i: (0, 0)
    row = lambda i: (0, 0, 0)
    return pl.pallas_call(
        body, name="attn_bwd", grid=(nk,),
        in_specs=[
            pl.BlockSpec((DA, L), head),
            pl.BlockSpec((H, TT, KA), lambda i: (0, i, 0)),
            pl.BlockSpec((H, TT, DH), lambda i: (0, i, 0)),
            pl.BlockSpec((DA, TT), lambda i: (1, i)),
            pl.BlockSpec((DA, L), head),
            pl.BlockSpec((H, 1, L), row), pl.BlockSpec((H, 1, L), row), pl.BlockSpec((H, 1, L), row), _UNREAD,
        ],
        out_specs=[
            pl.BlockSpec((DA, L), head),
            pl.BlockSpec((DA, TT), lambda i: (0, i)),
            pl.BlockSpec((DA, TT), lambda i: (0, i)),
            pl.BlockSpec((H, 1, TT), lambda i: (0, 0, i)),
            pl.BlockSpec((H, 1, L), row),
        ],
        out_shape=[jax.ShapeDtypeStruct((DA, L), BF16), jax.ShapeDtypeStruct((DA, L), BF16),
                   jax.ShapeDtypeStruct((DA, L), BF16), jax.ShapeDtypeStruct((H, 1, L), F32),
                   jax.ShapeDtypeStruct((H, 1, L), F32)],
        scratch_shapes=[
            pltpu.VMEM((HG, DH + DF, L), F32),
            pltpu.VMEM((HG, DH + DF, TT), BF16),
            pltpu.VMEM((HG, TT, TT), F32), pltpu.VMEM((HG, TT, TT), F32),
            pltpu.VMEM((HG, TT, KA), F32), pltpu.VMEM((HG, TT, KA), F32)],
        compiler_params=_params(),
    )(proj_t, kaug, vtok, proj_t, do_t, lse, dd, cq, after)


def _fgate_bwd(dcq, dck, sg, L):
    def body(dcq_ref, dck_ref, sg_ref, df_ref, db_ref):
        dc = jnp.concatenate([dcq_ref[h] - dck_ref[h] for h in range(H)], axis=0)
        idx = lax.broadcasted_iota(jnp.int32, (H, L), 1)
        r = dc
        s = 1
        while s < L:
            r = r + jnp.where(idx + s < L, pltpu.roll(r, L - s, 1), 0.0)
            s *= 2
        df = r * sg_ref[...]
        db_ref[...] = jnp.broadcast_to(jnp.sum(df, axis=1, keepdims=True), (H, TB))
        df_ref[...] = jnp.concatenate([df, jnp.zeros((DF - H, L), F32)], axis=0).astype(BF16)

    return pl.pallas_call(
        body, name="fgate_bwd",
        out_shape=[jax.ShapeDtypeStruct((DF, L), BF16), jax.ShapeDtypeStruct((H, TB), F32)],
        compiler_params=pltpu.CompilerParams(vmem_limit_bytes=VMEM_LIMIT),
    )(dcq, dck, sg)


def _inproj_bwd_x(w, dq_t, dk_t, dv_t, dg5_t, df_t, dout, x, meta_full, norm_g, L, after):
    nj = L // TT
    seq = x.shape[0]

    def body(w_ref, dq_ref, dk_ref, dv_ref, dg5_ref, df_ref, dout_ref, xa_ref, xb_ref, xc_ref, meta_ref, g_ref, _,
             gx_ref, dmeta_ref, dg_ref, dh_scr, sems):
        j = pl.program_id(0)
        slot = j % 2

        def copy_out(step, slot_):
            first = pltpu.make_async_copy(dh_scr.at[slot_, pl.ds(TB, TT - TB)], gx_ref.at[pl.ds(0, TT - TB)],
                                          sems.at[slot_])
            later = pltpu.make_async_copy(dh_scr.at[slot_], gx_ref.at[pl.ds(step * TT - TB, TT)], sems.at[slot_])
            return first, later

        @pl.when(j == 0)
        def _():
            dg_ref[...] = jnp.zeros_like(dg_ref)

        du = _dot(dq_ref[...], w_ref[0:DA, :], TN_DIMS)
        du += _dot(dk_ref[...], w_ref[DA:2 * DA, :], TN_DIMS)
        du += _dot(dv_ref[...], w_ref[2 * DA:3 * DA, :], TN_DIMS)
        du += _dot(dg5_ref[...], w_ref[3 * DA:NSEC * DA, :], TN_DIMS)
        du += _dot(df_ref[...], w_ref[NSEC * DA:DPROJ, :], TN_DIMS)
        hb = _h_tile(j, xa_ref, xb_ref, xc_ref, meta_ref)
        r = lax.rsqrt(jnp.mean(hb * hb, axis=-1, keepdims=True) + EPS)
        hn = hb * r
        dg_ref[...] += jnp.sum(du * hn, axis=0, keepdims=True)
        gu = du * g_ref[...]
        dh = dout_ref[...] + r * gu - hn * (r * jnp.mean(gu * hn, axis=-1, keepdims=True))

        dh_scr[slot] = dh

        @pl.when(j == 0)
        def _():
            dmeta_ref[...] = dh[P0:TB, :]
            copy_out(0, 0)[0].start()

        @pl.when(j >= 1)
        def _():
            copy_out(j, slot)[1].start()

        @pl.when(j == 1)
        def _():
            copy_out(0, 0)[0].wait()

        @pl.when(j >= 2)
        def _():
            copy_out(j - 1, 1 - slot)[1].wait()

        @pl.when(j == nj - 1)
        def _():
            copy_out(j, slot)[0 if nj == 1 else 1].wait()

    blk = lambda rows: pl.BlockSpec((rows, TT), lambda j: (0, j))
    return pl.pallas_call(
        body, name="inproj_bwd_x", grid=(nj,),
        in_specs=[_full_spec((DPROJ, D)), blk(DA), blk(DA), blk(DA), blk(5 * DA), blk(DF),
                  pl.BlockSpec((TT, D), lambda j: (j, 0))] + _x_specs3()
                 + [_full_spec((NM, D)), _full_spec((1, D)), _UNREAD],
        out_specs=[pl.BlockSpec(memory_space=pl.ANY), _full_spec((NM, D)), _full_spec((1, D))],
        out_shape=[jax.ShapeDtypeStruct((seq, D), F32), jax.ShapeDtypeStruct((NM, D), F32),
                   jax.ShapeDtypeStruct((1, D), F32)],
        scratch_shapes=[pltpu.VMEM((2, TT, D), F32), pltpu.SemaphoreType.DMA((2,))],
        compiler_params=_params(),
    )(w, dq_t, dk_t, dv_t, dg5_t, df_t, dout, x, x, x, meta_full, norm_g, after)


def _inproj_bwd_w(u, dq_t, dk_t, dv_t, dg5_t, df_t, L):
    def body(u_ref, dq_hbm, dk_hbm, dv_hbm, dg5_ref, df_ref, dw_ref, dwf_ref, qkv_scr, sems):
        s = pl.program_id(0)
        u_all = u_ref[...]
        fetch = [pltpu.make_async_copy(src, qkv_scr.at[k], sems.at[k])
                 for k, src in enumerate((dq_hbm, dk_hbm, dv_hbm))]

        @pl.when(s == 0)
        def _():
            for cp in fetch:
                cp.start()

        @pl.when(s < 5)
        def _():
            dw_ref[...] = _dot(dg5_ref[...], u_all)

        for k in range(3):
            @pl.when(s == 5 + k)
            def _(k=k):
                fetch[k].wait()
                dw_ref[...] = _dot(qkv_scr[k], u_all)

        @pl.when(s == NSEC - 1)
        def _():
            dwf_ref[...] = _dot(df_ref[...], u_all)

    once = lambda shape: pl.BlockSpec(shape, lambda s: (0, 0), pipeline_mode=pl.Buffered(1))
    any_spec = pl.BlockSpec(memory_space=pl.ANY)
    return pl.pallas_call(
        body, name="inproj_bwd_w", grid=(NSEC,),
        in_specs=[
            once((L, D)), any_spec, any_spec, any_spec,
            pl.BlockSpec((DA, L), lambda s: (jnp.minimum(s, 4), 0)),
            once((DF, L)),
        ],
        out_specs=[pl.BlockSpec((DA, D), lambda s: (jnp.where(s < 5, s + 3, s - 5), 0)), _full_spec((DF, D))],
        out_shape=[jax.ShapeDtypeStruct((NSEC * DA, D), F32), jax.ShapeDtypeStruct((DF, D), F32)],
        scratch_shapes=[pltpu.VMEM((3, DA, L), BF16), pltpu.SemaphoreType.DMA((3,))],
        compiler_params=_params(),
    )(u, dq_t, dk_t, dv_t, dg5_t, df_t)


def _adamw(w, g, m, v):
    m = ADAM_B1 * m + (1.0 - ADAM_B1) * g
    v = ADAM_B2 * v + (1.0 - ADAM_B2) * (g * g)
    m_hat = m / (1.0 - ADAM_B1 ** ADAM_STEP)
    v_hat = v / (1.0 - ADAM_B2 ** ADAM_STEP)
    delta = -ADAM_LR * (m_hat / (jnp.sqrt(v_hat) + ADAM_EPS) + ADAM_WD * w)
    return delta, m, v


def _adamw_big(own_in, land_in, own_out, land_out, w_in_t, m_in_t, v_in_t, w_out, m_out, v_out):
    cb = CB
    e_sh = D // NDEV
    in_shape = jax.ShapeDtypeStruct(w_in_t.shape, F32)
    out_shape = jax.ShapeDtypeStruct(w_out.shape, F32)

    def total(own_ref, land_ref, rows, chips):
        g = _pick_slab(0, own_ref, land_ref, rows, chips=chips).astype(F32)
        for j in range(1, own_ref.shape[0]):
            g = g + _pick_slab(j, own_ref, land_ref, rows, chips=chips).astype(F32)
        return g

    def body(oi_ref, li_ref, oo_ref, lo_ref, wi_ref, mi_ref, vi_ref, wo_ref, mo_ref, vo_ref,
             gi, di, mi, vi, go, do, mo, vo):
        g = total(oi_ref, li_ref, slice(0, WSHP), True)[:WSH]
        d, mn, vn = _adamw(wi_ref[...], g, mi_ref[...], vi_ref[...])
        gi[...], di[...], mi[...], vi[...] = g, d, mn, vn
        g = total(oo_ref, lo_ref, slice(0, e_sh), False)
        d, mn, vn = _adamw(wo_ref[0], g, mo_ref[0], vo_ref[0])
        go[0], do[0], mo[0], vo[0] = g, d, mn, vn

    slab = lambda n, rows: pl.BlockSpec((n, rows, cb), lambda i: (0, 0, i))
    ispec = pl.BlockSpec((WSH, cb), lambda i: (0, i))
    ospec = pl.BlockSpec((1, e_sh, cb), lambda i: (0, 0, i))
    return pl.pallas_call(
        body, name="adamw_big", grid=(D // cb,),
        in_specs=[slab(4, WSHP), slab(4, WSHP), slab(NDEV, e_sh), slab(NDEV, e_sh),
                  ispec, ispec, ispec, ospec, ospec, ospec],
        out_specs=[ispec] * 4 + [ospec] * 4, out_shape=[in_shape] * 4 + [out_shape] * 4,
        compiler_params=_params(),
    )(own_in, land_in, own_out, land_out, w_in_t, m_in_t, v_in_t, w_out, m_out, v_out)


F0 = 3 * DA


def _unshard_w_out(own, land):
    e_sh = D // NDEV

    def body(own_ref, land_ref, wo_ref):
        for j in range(NDEV):
            wo_ref[j * e_sh:(j + 1) * e_sh, :] = _pick_slab(j, own_ref, land_ref, slice(0, e_sh), per_peer=False)

    return pl.pallas_call(
        body, name="unshard_w_out", grid=(D // CB,),
        in_specs=[pl.BlockSpec((e_sh, CB), lambda i: (0, i)), pl.BlockSpec((NDEV, e_sh, CB), lambda i: (0, 0, i))],
        out_specs=pl.BlockSpec((D, CB), lambda i: (0, i)),
        out_shape=jax.ShapeDtypeStruct((D, D), BF16),
        compiler_params=_params(),
    )(own, land)


def _unshard_w_in(w_all, small_all, attn_gain, conv_gain):
    def body(w_ref, small_ref, ga_ref, gc_ref, wt_ref, meta_ref, cwb_ref, gab_ref, gcb_ref):
        i = pl.program_id(0)
        for k in range(CB // TB):
            meta_ref[:, k * TB:(k + 1) * TB] = small_ref[(CB // TB) * i + k, 0:NM, :]

        @pl.when(i == 0)
        def _():
            per_row = lambda line: jnp.broadcast_to(line, (TB, DA)).T
            cw = jnp.concatenate([small_ref[j, NM:NM + 3, 0:DH] for j in range(NDEV)], axis=1)
            for k in range(3):
                cwb_ref[k] = per_row(cw[k:k + 1, :])
            gab_ref[...] = per_row(ga_ref[...])
            gcb_ref[...] = per_row(gc_ref[...])

        def ref_rows(lo, hi):
            pieces, r = [], lo
            while r < hi:
                sh, off = divmod(r, WSH)
                n = min(hi - r, WSH - off)
                pieces.append(w_ref[sh, off:off + n, :])
                r += n
            return pieces

        for s in range(NSEC):
            lo = s * DA if s < 3 else s * DA + H
            wt_ref[s * DA:(s + 1) * DA, :] = jnp.concatenate(ref_rows(lo, lo + DA), axis=0)
        wt_ref[NSEC * DA:DPROJ, :] = jnp.concatenate(
            ref_rows(F0, F0 + H) + [jnp.zeros((DF - H, CB), BF16)], axis=0)

    return pl.pallas_call(
        body, name="unshard_w_in", grid=(D // CB,),
        in_specs=[pl.BlockSpec((NDEV, WSHP, CB), lambda i: (0, 0, i)), _full_spec(small_all.shape),
                  _full_spec((1, DA)), _full_spec((1, DA))],
        out_specs=[pl.BlockSpec((DPROJ, CB), lambda i: (0, i)), pl.BlockSpec((NM, CB), lambda i: (0, i)),
                   _full_spec((3, DA, TB)), _full_spec((DA, TB)), _full_spec((DA, TB))],
        out_shape=[jax.ShapeDtypeStruct((DPROJ, D), BF16), jax.ShapeDtypeStruct((NM, D), F32),
                   jax.ShapeDtypeStruct((3, DA, TB), F32), jax.ShapeDtypeStruct((DA, TB), F32),
                   jax.ShapeDtypeStruct((DA, TB), F32)],
        compiler_params=_params(),
    )(w_all, small_all, attn_gain, conv_gain)


def _shard_w_in_grads(dw_main, dw_f):
    def body(dm_ref, df_ref, p_ref):
        mc = lax.axis_index("c")

        def ref_rows(lo, hi):
            pieces, r = [], lo
            while r < hi:
                if r < F0:
                    n = min(hi, F0) - r
                    pieces.append(dm_ref[r:r + n, :])
                elif r < F0 + H:
                    n = min(hi, F0 + H) - r
                    pieces.append(df_ref[r - F0:r - F0 + n, :])
                else:
                    n = hi - r
                    pieces.append(dm_ref[r - H:r - H + n, :])
                r += n
            return pieces

        for i in range(NDEV):
            rows = jnp.concatenate(ref_rows(i * WSH, (i + 1) * WSH) + [jnp.zeros((WSHP - WSH, CB), F32)], axis=0)
            p_ref[i // 2 + jnp.where(mc == i % 2, 0, 4)] = rows.astype(BF16)

    col = lambda rows: pl.BlockSpec((rows, CB), lambda i: (0, i))
    return pl.pallas_call(
        body, name="shard_w_in_grads", grid=(D // CB,),
        in_specs=[col(NSEC * DA), col(DF)],
        out_specs=pl.BlockSpec((NDEV, WSHP, CB), lambda i: (0, 0, i)),
        out_shape=jax.ShapeDtypeStruct((NDEV, WSHP, D), BF16),
        compiler_params=_params(),
    )(dw_main, dw_f)


SMALL = ("norm_g", "final_norm_g", "attn_norm_g", "conv_norm_g", "b_f", "meta", "conv_w")


def _as_rows(x):
    return jnp.concatenate([x[:, r * TB:(r + 1) * TB] for r in range(x.shape[1] // TB)], axis=0)


def _as_line(rows):
    return jnp.concatenate([rows[r:r + 1, :] for r in range(rows.shape[0])], axis=1)


def _pad_rows(x, n=8):
    return jnp.concatenate([x, jnp.zeros((n - x.shape[0], x.shape[1]), F32)], axis=0)


def _tile_rows(a, rows, lanes=TB):
    a = a.reshape(rows, lanes)
    return jnp.pad(a, ((0, -rows % 8), (0, TB - lanes)))


def _pack_small_grads(dg_norm, dg_final, dga_p, dgc_p, dcw_p, db_b, dmeta, loss):
    def body(dgn_ref, dgf_ref, dga_ref, dgc_ref, dcw_ref, db_ref, dmeta_ref, loss_ref, out_ref):
        def lane_sums(p):
            return jnp.sum(p.T, axis=0, keepdims=True)

        lane = lax.broadcasted_iota(jnp.int32, (1, TB), 1)
        b_row = jnp.where(lane == H, loss_ref[...], 0.0)
        for h in range(H):
            b_row = b_row + jnp.where(lane == h, db_ref[h:h + 1, :], 0.0)
        common = jnp.concatenate([
            _as_rows(dgn_ref[...]), _as_rows(dgf_ref[...]), _pad_rows(_as_rows(lane_sums(dga_ref[...]))),
            _pad_rows(_as_rows(lane_sums(dgc_ref[...]))), _pad_rows(b_row)], axis=0)
        dcw = [lane_sums(dcw_ref[k]) for k in range(3)]
        for j in range(NDEV):
            cw = jnp.concatenate(
                [jnp.concatenate([r[:, j * DH:(j + 1) * DH], jnp.zeros((1, TB - DH), F32)], axis=1) for r in dcw],
                axis=0)
            out_ref[j] = jnp.concatenate([common, dmeta_ref[:, j * TB:(j + 1) * TB], _pad_rows(cw)], axis=0)

    return pl.pallas_call(
        body, name="pack_small_grads", out_shape=jax.ShapeDtypeStruct((NDEV, SROWS, TB), F32),
    )(dg_norm, dg_final, dga_p, dgc_p, dcw_p, db_b, dmeta, loss)


def _adamw_small(own, land, params):
    flat = [a for n in SMALL for a in params[n]]

    def body(*refs):
        own_ref, land_ref = refs[:2]
        ins = refs[2:2 + 3 * len(SMALL)]
        outs = refs[2 + 3 * len(SMALL):]
        g = _pick_slab(0, own_ref, land_ref, slice(0, SROWS))
        for j in range(1, NDEV):
            g = g + _pick_slab(j, own_ref, land_ref, slice(0, SROWS))
        grads = dict(
            norm_g=_as_line(g[0:8]), final_norm_g=_as_line(g[8:16]), attn_norm_g=_as_line(g[16:20]),
            conv_norm_g=_as_line(g[24:28]), b_f=g[32:33, :H], meta=g[40:56], conv_w=g[56:59, :DH][None])
        for i, n in enumerate(SMALL):
            w_ref, m_ref, v_ref = ins[3 * i:3 * i + 3]
            d, mn, vn = _adamw(w_ref[...], grads[n], m_ref[...], v_ref[...])
            for o_ref, val in zip(outs[4 * i:4 * i + 4], (grads[n], d, mn, vn)):
                o_ref[...] = val
        outs[-1][...] = g[32:33, H:H + 1]

    shapes = [jax.ShapeDtypeStruct(params[n][0].shape, F32) for n in SMALL for _ in range(4)]
    res = pl.pallas_call(
        body, name="adamw_small", out_shape=shapes + [jax.ShapeDtypeStruct((1, 1), F32)],
    )(own, land, *flat)
    return {n: res[4 * i:4 * i + 4] for i, n in enumerate(SMALL)}, res[-1]


def kernel(x, meta, norm_g, w_in, b_f, conv_w, attn_norm_g, conv_norm_g, w_out, final_norm_g, loss_target, m_meta, m_norm_g, m_w_in, m_b_f, m_conv_w, m_attn_norm_g, m_conv_norm_g, m_w_out, m_final_norm_g, v_meta, v_norm_g, v_w_in, v_b_f, v_conv_w, v_attn_norm_g, v_conv_norm_g, v_w_out, v_final_norm_g):
    seq = x.shape[1]
    L = seq + TB
    assert x.shape == (1, seq, D) and L % TT == 0 and w_in.shape == (1, D, WSH)
    x2 = x[0]
    tgt = loss_target[0]

    w_in_slab = jnp.pad(w_in[0].T, ((0, WSHP - WSH), (0, 0))).astype(BF16)
    w_out_slab = w_out[0].astype(BF16)
    meta_slab = jnp.concatenate([meta, _tile_rows(conv_w[0], 3, DH)], axis=0)
    wout_flight = _split_start(w_out_slab, "gather_w_out_start", per_peer=False)
    w_all, small_all = _all_gather([w_in_slab, meta_slab], "gather_w_in")

    w_t, meta_full, cw_b, ga_b, gcn_b = _unshard_w_in(w_all, small_all, attn_norm_g, conv_norm_g)

    u, proj_t, gate_t, f_t, ktok, vtok = _inproj_fwd(x2, meta_full, norm_g, w_t, L, after=wout_flight[4])
    cq, kaug, sg = _fgate_fwd(f_t, b_f.reshape(H, 1), ktok, L)
    o_t, lse = _attn_fwd(proj_t, kaug, cq, L)

    w_out_own, w_out_land = _split_wait(wout_flight, o_t, "gather_w_out_wait", per_peer=False)
    w_out_full = _unshard_w_out(w_out_own, w_out_land)
    dout, dw_out, loss_part, dg_final, do_t, dd, dg5_t, dga_p, dgc_p, dcw_p = _gate_outproj(
        o_t, gate_t, cw_b, ga_b, gcn_b, w_out_full, x2, meta_full, final_norm_g.reshape(1, D), tgt, L)
    dwo_flight = _split_start(dw_out.reshape(NDEV, D // NDEV, D), "exchange_dw_out_start", per_peer=True)
    dq_t, dk_t, dv_t, dck, dcq = _attn_bwd(proj_t, kaug, vtok, do_t, lse, dd, cq, L, after=dwo_flight[4])
    df_t, db_f = _fgate_bwd(dcq, dck, sg, L)
    dw_main, dw_f = _inproj_bwd_w(u, dq_t, dk_t, dv_t, dg5_t, df_t, L)
    dwi_parts = _shard_w_in_grads(dw_main, dw_f)
    dwi_chip = _pair_sum(dwi_parts, _pair_exchange(dwi_parts, "exchange_dw_in_pair"))
    dwi_flight = _split_start(dwi_chip, "exchange_dw_in_start", per_peer=True, chips=True)
    grad_x, dmeta, dg_norm = _inproj_bwd_x(
        w_t, dq_t, dk_t, dv_t, dg5_t, df_t, dout, x2, meta_full, norm_g, L, after=dwi_flight[4])
    small_parts = _pack_small_grads(dg_norm, dg_final, dga_p, dgc_p, dcw_p, db_f, dmeta, loss_part)
    small_flight = _split_start(small_parts, "exchange_small_start", per_peer=True)
    dwo_own, dwo_land = _split_wait(dwo_flight, small_flight[4], "exchange_dw_out_wait", per_peer=True)
    dwi_own, dwi_land = _split_wait(dwi_flight, dwo_land, "exchange_dw_in_wait", per_peer=True, chips=True)

    big_out = _adamw_big(dwi_own, dwi_land, dwo_own, dwo_land,
                         w_in[0].T, m_w_in[0].T, v_w_in[0].T, w_out, m_w_out, v_w_out)
    g_w_in, d_w_in, nm_w_in, nv_w_in = [a.T[None] for a in big_out[:4]]
    g_w_out, d_w_out, nm_w_out, nv_w_out = big_out[4:]
    sm_own, sm_land = _split_wait(small_flight, big_out[4], "exchange_small_wait", per_peer=True)
    line = lambda a: a.reshape(1, D)
    small, loss = _adamw_small(sm_own, sm_land, dict(
        norm_g=(norm_g, m_norm_g, v_norm_g),
        final_norm_g=(line(final_norm_g), line(m_final_norm_g), line(v_final_norm_g)),
        attn_norm_g=(attn_norm_g, m_attn_norm_g, v_attn_norm_g),
        conv_norm_g=(conv_norm_g, m_conv_norm_g, v_conv_norm_g),
        b_f=(b_f, m_b_f, v_b_f), meta=(meta, m_meta, v_meta), conv_w=(conv_w, m_conv_w, v_conv_w)))
    small["final_norm_g"] = [a.reshape(D) for a in small["final_norm_g"]]
    order = ("meta", "norm_g", "w_in", "b_f", "conv_w", "attn_norm_g", "conv_norm_g", "w_out", "final_norm_g")
    groups = []
    for k, (wi, wo) in enumerate(((g_w_in, g_w_out), (d_w_in, d_w_out), (nm_w_in, nm_w_out), (nv_w_in, nv_w_out))):
        d = dict({n: small[n][k] for n in SMALL}, w_in=wi, w_out=wo)
        groups.append([d[n] for n in order])
    return (loss[0, 0], grad_x[None], *groups[0], *groups[1], *groups[2], *groups[3])
```

```python
import jax
import jax.numpy as jnp
from jax import lax
from jax.experimental import pallas as pl
from jax.experimental.pallas import tpu as pltpu

F32 = jnp.float32
BF16 = jnp.bfloat16

D = 1024
DA = 512
H = 8
DH = 64
NM = 16
TB = 128
P0 = TB - NM
TT = 3 * TB
HG = 8
NDEV = 8
NSEC = 8
DF = 16
DPROJ = NSEC * DA + DF
WSH = 513
WSHP = 528
SROWS = 64
EPS = 1e-6
NEG = -1e30
LOG2E = 1.4426950408889634
LN2 = 0.6931471805599453
QSCALE = DH ** -0.5 * LOG2E
KA = 128
CB = 256
VMEM_LIMIT = 56 * 1024 * 1024

ADAM_LR = 0.001
ADAM_B1 = 0.9
ADAM_B2 = 0.999
ADAM_EPS = 1e-08
ADAM_WD = 0.01
ADAM_STEP = 10

NT_DIMS = (((1,), (1,)), ((), ()))
TN_DIMS = (((0,), (0,)), ((), ()))
MESH = pl.DeviceIdType.MESH


def _params(n_axes=1, vmem=VMEM_LIMIT):
    return pltpu.CompilerParams(dimension_semantics=("arbitrary",) * n_axes, vmem_limit_bytes=vmem)


def _dot(a, b, dims=None):
    if dims is None:
        return jnp.dot(a, b, preferred_element_type=F32)
    return lax.dot_general(a, b, dims, preferred_element_type=F32)


def _my_place():
    return lax.axis_index("x"), lax.axis_index("y"), lax.axis_index("c")


def _all_gather(xs, name):
    n = len(xs)

    def body(*refs):
        x_refs, out_refs = refs[:n], refs[n:2 * n]
        send_sems, recv_sems, local_sems = refs[2 * n:]
        mx, my, mc = _my_place()

        def across(px, py, pc, axis_a):
            flip_x = pc if axis_a else 1 - pc
            return (px + flip_x) % 2, (py + 1 - flip_x) % 2, pc

        def idx(p):
            return 4 * p[0] + 2 * p[1] + p[2]

        me, sib = (mx, my, mc), (mx, my, 1 - mc)
        a_nbr, b_nbr = across(*me, True), across(*me, False)
        diag = across(*b_nbr, True)
        sib_a, sib_b = across(*sib, True), across(*sib, False)
        sib_diag = across(*sib_b, True)

        waits = []
        for t in range(n):
            out_ref = out_refs[t]

            def copy(k, block, to, src=None, out_ref=out_ref, t=t):
                return pltpu.make_async_remote_copy(
                    src_ref=out_ref.at[idx(block)] if src is None else src, dst_ref=out_ref.at[idx(block)],
                    send_sem=send_sems.at[7 * t + k], recv_sem=recv_sems.at[7 * t + k],
                    device_id=to, device_id_type=MESH)

            mine = pltpu.make_async_copy(x_refs[t], out_ref.at[idx(me)], local_sems.at[t])
            mine.start()
            started = [copy(0, me, sib, src=x_refs[t]), copy(1, me, a_nbr, src=x_refs[t]),
                       copy(2, me, b_nbr, src=x_refs[t])]
            for cp in started:
                cp.start()
            waits.append((copy, mine, started))
        relays = ((1, a_nbr, ((3, b_nbr), (4, sib))), (2, b_nbr, ((5, sib),)), (3, diag, ((6, sib),)))
        for landed, block, onward in relays:
            for copy, _, started in waits:
                copy(landed, block, me).wait_recv()
                for k, to in onward:
                    started.append(copy(k, block, to))
                    started[-1].start()
        for copy, mine, started in waits:
            for k, block in ((0, sib), (4, sib_a), (5, sib_b), (6, sib_diag)):
                copy(k, block, me).wait_recv()
            for cp in started:
                cp.wait_send()
            mine.wait()

    any_spec = pl.BlockSpec(memory_space=pl.ANY)
    return pl.pallas_call(
        body, name=name,
        out_shape=[jax.ShapeDtypeStruct((NDEV,) + x.shape, x.dtype) for x in xs],
        in_specs=[any_spec] * n, out_specs=[any_spec] * n,
        scratch_shapes=[pltpu.SemaphoreType.DMA((7 * n,)), pltpu.SemaphoreType.DMA((7 * n,)),
                        pltpu.SemaphoreType.DMA((n,))],
    )(*xs)


_HBM = pl.BlockSpec(memory_space=pltpu.HBM)
_UNREAD = pl.BlockSpec(memory_space=pl.ANY)
_SEM = pl.BlockSpec(memory_space=pltpu.SEMAPHORE)
_EFFECT = pltpu.SideEffectType.DATAFLOW_SIDE_EFFECTING


def _peer_of(m, place):
    mx, my, mc = place
    return ((1 - mx) if m & 4 else mx, (1 - my) if m & 2 else my, (1 - mc) if m & 1 else mc)


def _party(chips):
    if chips:
        return (lambda p: 2 * p[0] + p[1]), (2, 4, 6)
    return (lambda p: 4 * p[0] + 2 * p[1] + p[2]), tuple(range(1, NDEV))


def _split_copies(src_ref, land_ref, send_sems, recv_sems, per_peer, incoming, chips):
    place = _my_place()
    slot, masks = _party(chips)
    me = slot(place)
    out = []
    for k, m in enumerate(masks):
        there = _peer_of(m, place)
        peer = slot(there)
        src = (src_ref.at[me] if incoming else src_ref.at[peer]) if per_peer else src_ref
        out.append(pltpu.make_async_remote_copy(
            src_ref=src, dst_ref=land_ref.at[peer if incoming else me],
            send_sem=send_sems.at[k], recv_sem=recv_sems.at[k], device_id=there, device_id_type=MESH))
    return out


def _split_start(src, name, per_peer, chips=False):
    slab = src.shape[1:] if per_peer else src.shape
    n = len(_party(chips)[1])

    def body(src_ref, land_ref, send_sems, recv_sems, src_thru, land_thru, token):
        for cp in _split_copies(src_ref, land_ref, send_sems, recv_sems, per_peer, False, chips):
            cp.start()
        token[...] = jnp.zeros_like(token)

    return pl.pallas_call(
        body, name=name,
        out_shape=(pltpu.SemaphoreType.DMA((n,)), pltpu.SemaphoreType.DMA((n,)),
                   pltpu.HBM(src.shape, src.dtype), pltpu.HBM((n + 1,) + slab, src.dtype),
                   jax.ShapeDtypeStruct((8, TB), F32)),
        in_specs=(_HBM, _HBM), out_specs=(_SEM, _SEM, _HBM, _HBM, pl.BlockSpec(memory_space=pltpu.VMEM)),
        input_output_aliases={0: 2, 1: 3},
        compiler_params=pltpu.CompilerParams(has_side_effects=_EFFECT),
    )(pltpu.with_memory_space_constraint(src, pltpu.HBM),
      pltpu.with_memory_space_constraint(lax.empty((n + 1,) + slab, src.dtype), pltpu.HBM))


def _split_wait(handles, after, name, per_peer, chips=False):
    send_sems, recv_sems, src_thru, land_thru, _ = handles

    def body(src_ref, land_ref, send_sems, recv_sems, after_ref, src_out, land_out):
        for cp in _split_copies(src_ref, land_ref, send_sems, recv_sems, per_peer, False, chips):
            cp.wait_send()
        for cp in _split_copies(src_ref, land_ref, send_sems, recv_sems, per_peer, True, chips):
            cp.wait_recv()

    return pl.pallas_call(
        body, name=name,
        out_shape=(pltpu.HBM(src_thru.shape, src_thru.dtype), pltpu.HBM(land_thru.shape, land_thru.dtype)),
        in_specs=(_HBM, _HBM, _SEM, _SEM, pl.BlockSpec(memory_space=pl.ANY)), out_specs=(_HBM, _HBM),
        input_output_aliases={0: 0, 1: 1},
        compiler_params=pltpu.CompilerParams(has_side_effects=_EFFECT),
    )(src_thru, land_thru, send_sems, recv_sems, after)


def _pick_slab(j, own_ref, land_ref, rows, per_peer=True, chips=False):
    me = _party(chips)[0](_my_place())
    own = (lambda: own_ref[j, rows, :]) if per_peer else (lambda: own_ref[rows, :])
    return lax.cond(me == j, own, lambda: land_ref[j, rows, :])


def _pair_exchange(p, name):
    def body(p_ref, got_ref, send_sems, recv_sems):
        mx, my, mc = _my_place()
        copies = [pltpu.make_async_remote_copy(
            src_ref=p_ref.at[4 + q], dst_ref=got_ref.at[q], send_sem=send_sems.at[q],
            recv_sem=recv_sems.at[q], device_id=(mx, my, 1 - mc), device_id_type=MESH) for q in range(4)]
        for cp in copies:
            cp.start()
        for cp in copies:
            cp.wait_recv()
        for cp in copies:
            cp.wait_send()

    any_spec = pl.BlockSpec(memory_space=pl.ANY)
    return pl.pallas_call(
        body, name=name, out_shape=jax.ShapeDtypeStruct((4,) + p.shape[1:], p.dtype),
        in_specs=[any_spec], out_specs=any_spec,
        scratch_shapes=[pltpu.SemaphoreType.DMA((4,)), pltpu.SemaphoreType.DMA((4,))],
    )(p)


def _pair_sum(p, got):
    rows = p.shape[1]

    def body(p_ref, got_ref, out_ref):
        for q in range(4):
            out_ref[q] = (p_ref[q].astype(F32) + got_ref[q].astype(F32)).astype(BF16)

    blk = lambda n: pl.BlockSpec((n, rows, CB), lambda i: (0, 0, i))
    return pl.pallas_call(
        body, name="pair_sum", grid=(D // CB,), in_specs=[blk(4), blk(4)], out_specs=blk(4),
        out_shape=jax.ShapeDtypeStruct((4, rows, D), BF16), compiler_params=_params(),
    )(p, got)


def _x_specs3(tile=lambda j: j):
    return [pl.BlockSpec((TB, D), lambda j: (jnp.maximum(3 * tile(j) - 1, 0), 0)),
            pl.BlockSpec((TB, D), lambda j: (3 * tile(j), 0)),
            pl.BlockSpec((TB, D), lambda j: (3 * tile(j) + 1, 0))]


def _h_tile(j, xa_ref, xb_ref, xc_ref, meta_ref):
    first = jnp.concatenate([jnp.zeros((P0, D), F32), meta_ref[...]], axis=0)
    return jnp.concatenate([jnp.where(j == 0, first, xa_ref[...]), xb_ref[...], xc_ref[...]], axis=0)


def _full_spec(shape):
    return pl.BlockSpec(shape, lambda *_: (0,) * len(shape))


def _sigmoid(z):
    return 1.0 / (1.0 + jnp.exp(-z))


def _lane_tiles_sum(x):
    out = x[:, :TB]
    for i in range(1, x.shape[1] // TB):
        out = out + x[:, i * TB:(i + 1) * TB]
    return out


def _inproj_fwd(x, meta_full, norm_g, w_t, L, after):
    nj = L // TT

    def body(xa_ref, xb_ref, xc_ref, meta_ref, g_ref, w_ref, _, u_ref, proj_ref, gate_ref, f_ref, ktok_ref, vtok_ref):
        hb = _h_tile(pl.program_id(0), xa_ref, xb_ref, xc_ref, meta_ref)
        r = lax.rsqrt(jnp.mean(hb * hb, axis=-1, keepdims=True) + EPS)
        u = (hb * r * g_ref[...]).astype(BF16)
        u_ref[...] = u
        for s in range(NSEC):
            p = _dot(u, w_ref[s * DA:(s + 1) * DA, :], NT_DIMS)
            if s == 0:
                p = p * QSCALE
            if s in (1, 2):
                tok_ref = ktok_ref if s == 1 else vtok_ref
                for h in range(H):
                    tok_ref[h] = p[:, h * DH:(h + 1) * DH].astype(BF16)
            out_ref, s_out = (proj_ref, s) if s < 3 else (gate_ref, s - 3)
            out_ref[s_out * DA:(s_out + 1) * DA, :] = p.T.astype(BF16)
        f_ref[...] = _dot(w_ref[NSEC * DA:DPROJ, :], u, NT_DIMS)[:H]

    return pl.pallas_call(
        body, name="inproj_fwd", grid=(nj,),
        in_specs=_x_specs3() + [_full_spec((NM, D)), _full_spec((1, D)), _full_spec((DPROJ, D)), _UNREAD],
        out_specs=[
            pl.BlockSpec((TT, D), lambda t: (t, 0)),
            pl.BlockSpec((3 * DA, TT), lambda t: (0, t)),
            pl.BlockSpec((None, (NSEC - 3) * DA, TT), lambda t: (t, 0, 0)),
            pl.BlockSpec((H, TT), lambda t: (0, t)),
            pl.BlockSpec((H, TT, DH), lambda t: (0, t, 0)),
            pl.BlockSpec((H, TT, DH), lambda t: (0, t, 0)),
        ],
        out_shape=[
            jax.ShapeDtypeStruct((L, D), BF16),
            jax.ShapeDtypeStruct((3 * DA, L), BF16),
            jax.ShapeDtypeStruct((nj, (NSEC - 3) * DA, TT), BF16),
            jax.ShapeDtypeStruct((H, L), F32),
            jax.ShapeDtypeStruct((H, L, DH), BF16),
            jax.ShapeDtypeStruct((H, L, DH), BF16),
        ],
        compiler_params=_params(),
    )(x, x, x, meta_full, norm_g, w_t, after)


def _split3(x):
    hi = x.astype(BF16).astype(F32)
    r = x - hi
    mid = r.astype(BF16).astype(F32)
    return hi, mid, (r - mid).astype(BF16).astype(F32)


def _bias_rows(bias):
    one = jnp.ones((1, TT), F32)
    zero = jnp.zeros((1, TT), F32)
    parts = [zero] * 3 if bias is None else list(_split3(bias))
    return jnp.concatenate([one] * 3 + parts + [zero] * (DF - 6), axis=0).astype(BF16)


def _fgate_fwd(f_t, b_col, ktok, L):
    nb = L // TB

    def body(f_ref, b_ref, ktok_ref, cq_ref, kaug_ref, sg_ref, bias_scr):
        h = pl.program_id(0)

        @pl.when(h == 0)
        def _():
            z = f_ref[...] + b_ref[...]
            idx = lax.broadcasted_iota(jnp.int32, (H, L), 1)
            real = idx >= P0
            lf = jnp.where(real, jnp.minimum(z, 0.0) - jnp.log1p(jnp.exp(-jnp.abs(z))), 0.0)
            sg_ref[...] = jnp.where(real, 1.0 / (1.0 + jnp.exp(z)), 0.0)
            c = lf
            s = 1
            while s < L:
                c = c + jnp.where(idx >= s, pltpu.roll(c, s, 1), 0.0)
                s *= 2
            c = c * LOG2E
            for hh in range(H):
                cq_ref[hh] = c[hh:hh + 1, :]
            for part, val in enumerate(_split3(-jnp.where(real, c, -NEG))):
                for hh in range(H):
                    bias_scr[part * H + hh] = val[hh:hh + 1, :]

        lane = lax.broadcasted_iota(jnp.int32, (TB, KA), 1)
        head = jnp.zeros((DH, TB), F32)
        tail = jnp.concatenate([jnp.ones((3, TB), F32), jnp.zeros((KA - DH - 6, TB), F32)], axis=0)
        for b in range(nb):
            blk = slice(b * TB, (b + 1) * TB)
            cols = jnp.concatenate(
                [head] + [bias_scr[part * H + h, :, blk] for part in range(3)] + [tail], axis=0).T
            k = jnp.concatenate([ktok_ref[0, blk, :].astype(F32), jnp.zeros((TB, KA - DH), F32)], axis=1)
            kaug_ref[0, blk, :] = jnp.where(lane < DH, k, cols).astype(BF16)

    return pl.pallas_call(
        body, name="fgate_fwd", grid=(H,),
        in_specs=[_full_spec((H, L)), _full_spec((H, 1)), pl.BlockSpec((1, L, DH), lambda h: (h, 0, 0))],
        out_specs=[_full_spec((H, 1, L)), pl.BlockSpec((1, L, KA), lambda h: (h, 0, 0)), _full_spec((H, L))],
        out_shape=[
            jax.ShapeDtypeStruct((H, 1, L), F32),
            jax.ShapeDtypeStruct((H, L, KA), BF16),
            jax.ShapeDtypeStruct((H, L), F32),
        ],
        scratch_shapes=[pltpu.VMEM((3 * H, 1, L), F32)],
        compiler_params=_params(),
    )(f_t, b_col, ktok)


def _causal_mask():
    r = lax.broadcasted_iota(jnp.int32, (TT, TT), 0)
    c = lax.broadcasted_iota(jnp.int32, (TT, TT), 1)
    return r <= c


def _attn_fwd(proj_t, kaug, cq, L):
    nq = L // TT

    def body(q_ref, qn_ref, kaug_hbm, proj_hbm, cq_ref, o_ref, lse_ref,
             qa_scr, s_scr, cmax_scr, m_scr, p_scr, alpha_scr, acc_scr, kaug_ref, v_ref, sems):
        j = pl.program_id(0)
        rows = [slice(g * DH, (g + 1) * DH) for g in range(HG)]
        ones = jnp.ones((DF, TT), BF16)

        def fetch(kt):
            off = pl.multiple_of(kt * TT, TT)
            return (pltpu.make_async_copy(kaug_hbm.at[:, pl.ds(off, TT), :], kaug_ref.at[:, pl.ds(off, TT), :],
                                          sems.at[0]),
                    pltpu.make_async_copy(proj_hbm.at[pl.ds(2 * DA, DA), pl.ds(off, TT)],
                                          v_ref.at[:, pl.ds(off, TT)], sems.at[1]))

        def load_queries(ref):
            for g in range(HG):
                qa_scr[g] = jnp.concatenate(
                    [ref[rows[g], :], _bias_rows(None), jnp.zeros((KA - DH - DF, TT), BF16)], axis=0)

        def scores(kt, masked):
            k_off = pl.multiple_of(kt * TT, TT)
            for g in range(HG):
                s = _dot(kaug_ref[g, pl.ds(k_off, TT), :], qa_scr[g])
                if masked:
                    s = jnp.where(_causal_mask(), s, NEG)
                s_scr[g] = s
                cmax_scr[g] = jnp.max(s, axis=0, keepdims=True)

        def softmax():
            for g in range(HG):
                m_old = m_scr[g]
                m_new = jnp.maximum(m_old, cmax_scr[g])
                alpha_scr[g] = jnp.exp2(m_old - m_new)
                p_scr[g] = jnp.exp2(s_scr[g] - m_new).astype(BF16)
                m_scr[g] = m_new

        def weighted_sum(kt):
            k_off = pl.multiple_of(kt * TT, TT)
            for g in range(HG):
                v1 = jnp.concatenate([v_ref[rows[g], pl.ds(k_off, TT)], ones], axis=0)
                acc_scr[g] = alpha_scr[g] * acc_scr[g] + _dot(v1, p_scr[g])

        @pl.when(j == 0)
        def _():
            for cp in fetch(0):
                cp.start()
            for cp in fetch(0):
                cp.wait()
            load_queries(q_ref)
            scores(0, True)

        @pl.when(j < nq - 1)
        def _():
            for cp in fetch(j + 1):
                cp.start()

        m_scr[...] = jnp.full_like(m_scr, NEG)
        acc_scr[...] = jnp.zeros_like(acc_scr)

        @pl.when(j >= 1)
        def _():
            softmax()
            scores(j - 1, False)

        def step(i, c):
            weighted_sum(j - i + 1)
            softmax()
            scores(j - i - 1, False)
            return c

        lax.fori_loop(1, j, step, 0)

        def drain(second_last, next_tile):
            if second_last:
                weighted_sum(1)
            softmax()
            if next_tile:
                for cp in fetch(j + 1):
                    cp.wait()
                load_queries(qn_ref)
                scores(j + 1, True)
            weighted_sum(0)

        @pl.when(j == 0)
        def _():
            drain(False, nq > 1)

        @pl.when((j >= 1) & (j < nq - 1))
        def _():
            drain(True, True)

        @pl.when((j >= 1) & (j == nq - 1))
        def _():
            drain(True, False)

        for g in range(HG):
            l = acc_scr[g, DH:DH + 1, :]
            o_ref[rows[g], :] = acc_scr[g, :DH, :] * (1.0 / l)
            lse_ref[g] = m_scr[g] + jnp.log2(l) + cq_ref[g]

    assert HG == H
    return pl.pallas_call(
        body, name="attn_fwd", grid=(nq,),
        in_specs=[
            pl.BlockSpec((DA, TT), lambda j: (0, j)),
            pl.BlockSpec((DA, TT), lambda j: (0, jnp.minimum(j + 1, nq - 1))),
            pl.BlockSpec(memory_space=pl.ANY),
            pl.BlockSpec(memory_space=pl.ANY),
            pl.BlockSpec((H, 1, TT), lambda j: (0, 0, j)),
        ],
        out_specs=[
            pl.BlockSpec((None, DA, TT), lambda j: (j, 0, 0)),
            pl.BlockSpec((H, 1, TT), lambda j: (0, 0, j)),
        ],
        out_shape=[jax.ShapeDtypeStruct((nq, DA, TT), F32), jax.ShapeDtypeStruct((H, 1, L), F32)],
        scratch_shapes=[pltpu.VMEM((HG, KA, TT), BF16), pltpu.VMEM((HG, TT, TT), F32), pltpu.VMEM((HG, 1, TT), F32),
                        pltpu.VMEM((HG, 1, TT), F32), pltpu.VMEM((HG, TT, TT), BF16), pltpu.VMEM((HG, 1, TT), F32),
                        pltpu.VMEM((HG, DH + DF, TT), F32),
                        pltpu.VMEM((H, L, KA), BF16), pltpu.VMEM((DA, L), BF16), pltpu.SemaphoreType.DMA((2,))],
        compiler_params=_params(),
    )(proj_t, proj_t, kaug, proj_t, cq)


def _gate_group(rows, o_ref, za_ref, gb_ref, gc_ref, xc_ref, zc_ref, gcp_ref, xcp_ref, cw_ref, ga_ref, gcn_ref, first):
    n_rep = TT // TB
    f32 = lambda r: r[rows, :].astype(F32)
    o, za, gb, gc, xc, zc = o_ref[rows, :], f32(za_ref), f32(gb_ref), f32(gc_ref), f32(xc_ref), f32(zc_ref)
    a = gc * xc
    a_prev = jnp.where(first, 0.0, f32(gcp_ref) * f32(xcp_ref))
    full = jnp.concatenate([a_prev, a], axis=1)
    a1 = pltpu.roll(full, 1, 1)[:, TB:]
    a2 = pltpu.roll(full, 2, 1)[:, TB:]
    w0 = jnp.tile(cw_ref[0, rows, :], (1, n_rep))
    w1 = jnp.tile(cw_ref[1, rows, :], (1, n_rep))
    w2 = jnp.tile(cw_ref[2, rows, :], (1, n_rep))
    cv = w0 * a2 + w1 * a1 + w2 * a
    e = gb * cv
    rc = lax.rsqrt(jnp.mean(e * e, axis=0, keepdims=True) + EPS)
    ec = e * rc
    ra = lax.rsqrt(jnp.mean(o * o, axis=0, keepdims=True) + EPS)
    oa = o * ra
    g_a = jnp.tile(ga_ref[rows, :], (1, n_rep))
    g_c = jnp.tile(gcn_ref[rows, :], (1, n_rep))
    sa = _sigmoid(za)
    sc = _sigmoid(zc)
    return dict(o=o, za=za, gb=gb, gc=gc, xc=xc, zc=zc, a=a, a1=a1, a2=a2, w0=w0, w1=w1, w2=w2, cv=cv, e=e,
                rc=rc, ec=ec, ra=ra, oa=oa, g_a=g_a, g_c=g_c, sa=sa, sc=sc)


def _gate_specs(tile):
    halo = pl.BlockSpec((None, 2 * DA, TB),
                        lambda i: (jnp.maximum(tile(i) - 1, 0), 1, TT // TB - 1))
    return [pl.BlockSpec((None, DA, TT), lambda i: (tile(i), 0, 0)),
            pl.BlockSpec((None, 5 * DA, TT), lambda i: (tile(i), 0, 0)), halo,
            _full_spec((3, DA, TB)), _full_spec((DA, TB)), _full_spec((DA, TB))]


def _gate_views(g5_ref, halo_ref):
    return [g5_ref.at[pl.ds(s * DA, DA)] for s in range(5)] + [halo_ref.at[pl.ds(s * DA, DA)] for s in range(2)]


def _gate_outproj(o_t, gate_t, cw_b, ga_b, gcn_b, w_out, x, meta_full, fng, target, L):
    nj = L // TT
    rp = NM
    n_bwd = 8
    cb = D // 4
    assert P0 % rp == 0 and TB % rp == 0 and (TT // rp) % n_bwd == 0 and H == n_bwd

    def body(o_ref, g5_ref, halo_ref, cw_ref, ga_ref, gcn_ref, o2_ref, g52_ref, halo2_ref,
             w_ref, xa_ref, xb_ref, xc_ref, meta_ref, g_ref, ta_ref, tb_ref, tc_ref,
             dout_ref, dwb_ref, loss_ref, dg_ref, do_ref, dd_ref, dg5_ref, dga_ref, dgc_ref, dcw_ref,
             dw_ref, o_scr, db_new, db_old, mix_new, mix_old, dmix_new, dmix_old, sq_acc, dg_acc, carry_ref):
        za_ref, gb_ref, gc_ref, xcv_ref, zc_ref, gcp_ref, xcp_ref = _gate_views(g5_ref, halo_ref)
        za2_ref, gb2_ref, gc2_ref, xcv2_ref, zc2_ref, gcp2_ref, xcp2_ref = _gate_views(g52_ref, halo2_ref)
        t = pl.program_id(0)
        first_a = t == nj - 1
        first_c = t == nj + 1

        def gate_rows(h):
            rows = slice(h * DH, (h + 1) * DH)
            g = _gate_group(rows, o_ref, za_ref, gb_ref, gc_ref, xcv_ref, zc_ref, gcp_ref, xcp_ref,
                            cw_ref, ga_ref, gcn_ref, first_a)
            mix_new[rows, :] = (g["oa"] * g["g_a"] * (g["za"] * g["sa"])).astype(BF16)
            mix_new[DA + h * DH:DA + (h + 1) * DH, :] = (g["ec"] * g["g_c"] * (g["zc"] * g["sc"])).astype(BF16)

        def loss_rows(c):
            blk = c // (TB // rp)
            rows, out_rows = pl.ds((c % (TB // rp)) * rp, rp), pl.ds(c * rp, rp)
            h = (xa_ref, xb_ref, xc_ref)[blk][rows, :]
            if blk == 0:
                first = meta_ref[...] if c == P0 // rp else jnp.zeros((rp, D), F32)
                h = jnp.where(first_a, first, h)
            o = o_scr[out_rows, :] + h
            r = lax.rsqrt(jnp.mean(o * o, axis=-1, keepdims=True) + EPS)
            orn = o * r
            g = g_ref[...]
            diff = orn * g - (ta_ref, tb_ref, tc_ref)[blk][rows, :]
            if blk == 0:
                diff = diff * jnp.where(first_a, 0.0, 1.0)
            gy = diff * (g * (1.0 / D))
            dout = r * (gy - orn * jnp.mean(gy * orn, axis=-1, keepdims=True))
            dout_ref[out_rows, :] = dout
            db_new[out_rows, :] = dout.astype(BF16)
            sq, go = diff * diff, diff * orn
            sq_acc[...] += sq[:8] + sq[8:]
            dg_acc[...] += go[:8] + go[8:]

        def backward_cols(n):
            if n < 4:
                cols = slice(n * cb, (n + 1) * cb)
                dmix_new[cols, :] = _dot(db_old[...], w_ref[cols, :], NT_DIMS).T.astype(BF16)
            else:
                cols = slice((n - 4) * cb, (n - 3) * cb)
                dw_ref[:, cols] += _dot(mix_old[...], db_old[:, cols])

        def gate_bwd_rows(h):
            rows = slice(h * DH, (h + 1) * DH)
            sec = lambda s: slice(s * DA + h * DH, s * DA + (h + 1) * DH)
            g = _gate_group(rows, o2_ref, za2_ref, gb2_ref, gc2_ref, xcv2_ref, zc2_ref, gcp2_ref, xcp2_ref,
                            cw_ref, ga_ref, gcn_ref, first_c)
            o, za, gb, gc, xc, zc, sa, sc = (g[n] for n in ("o", "za", "gb", "gc", "xc", "zc", "sa", "sc"))
            dya = dmix_old[rows, :].astype(F32)
            dyc = dmix_old[DA + h * DH:DA + (h + 1) * DH, :].astype(F32)

            dn = dya * (za * sa)
            dg5_ref[sec(0), :] = (dya * (g["oa"] * g["g_a"]) * (sa * (1.0 + za * (1.0 - sa)))).astype(BF16)
            dga_ref[rows, :] += _lane_tiles_sum(dn * g["oa"])
            dng = dn * g["g_a"]
            mean_a = jnp.mean(dng * g["oa"], axis=0, keepdims=True)
            do = (dng - g["oa"] * mean_a) * g["ra"]
            do_ref[rows, :] = do.astype(BF16)
            dd_ref[h] = jnp.sum(do * o, axis=0, keepdims=True)

            dnc = dyc * (zc * sc)
            dg5_ref[sec(4), :] = (dyc * (g["ec"] * g["g_c"]) * (sc * (1.0 + zc * (1.0 - sc)))).astype(BF16)
            dgc_ref[rows, :] += _lane_tiles_sum(dnc * g["ec"])
            dncg = dnc * g["g_c"]
            mean_c = jnp.mean(dncg * g["ec"], axis=0, keepdims=True)
            de = (dncg - g["ec"] * mean_c) * g["rc"]
            dg5_ref[sec(1), :] = (de * g["cv"]).astype(BF16)
            dcv = de * gb
            full = jnp.concatenate([dcv, carry_ref[rows, :]], axis=1)
            d1 = pltpu.roll(full, TT + TB - 1, 1)[:, :TT]
            d2 = pltpu.roll(full, TT + TB - 2, 1)[:, :TT]
            carry_ref[rows, :] = dcv[:, :TB]
            da = g["w2"] * dcv + g["w1"] * d1 + g["w0"] * d2
            dg5_ref[sec(2), :] = (da * xc).astype(BF16)
            dg5_ref[sec(3), :] = (da * gc).astype(BF16)
            dcw_ref[0, rows, :] += _lane_tiles_sum(dcv * g["a2"])
            dcw_ref[1, rows, :] += _lane_tiles_sum(dcv * g["a1"])
            dcw_ref[2, rows, :] += _lane_tiles_sum(dcv * g["a"])

        def step(a, b, c):
            half = H // 2
            for h in range(H):
                if a:
                    gate_rows(h)
                if c and h < half:
                    gate_bwd_rows(h)
                if b and h % 2 == 1:
                    backward_cols(h // 2)
            if a:
                o_scr[...] = _dot(mix_new[...], w_ref[...], TN_DIMS)
            per = TT // rp // n_bwd
            for k in range(n_bwd):
                if a:
                    for piece in range(per * k, per * (k + 1)):
                        loss_rows(piece)
                if c and k % 2 == 0:
                    gate_bwd_rows(half + k // 2)
                if b and k % 2 == 1:
                    backward_cols(n_bwd // 2 + k // 2)
            if a:
                db_old[...] = db_new[...]
                mix_old[...] = mix_new[...]
            if b:
                dmix_old[...] = dmix_new[...]

        @pl.when(t == 0)
        def _():
            dw_ref[...] = jnp.zeros_like(dw_ref)
            sq_acc[...] = jnp.zeros_like(sq_acc)
            dg_acc[...] = jnp.zeros_like(dg_acc)
            carry_ref[...] = jnp.zeros_like(carry_ref)
            dga_ref[...] = jnp.zeros_like(dga_ref)
            dgc_ref[...] = jnp.zeros_like(dgc_ref)
            dcw_ref[...] = jnp.zeros_like(dcw_ref)
            step(True, False, False)

        @pl.when(t == 1)
        def _():
            step(True, True, False)

        @pl.when((t >= 2) & (t < nj))
        def _():
            step(True, True, True)

        @pl.when(t == nj)
        def _():
            step(False, True, True)
            dwb_ref[...] = dw_ref[...].astype(BF16)
            loss_ref[...] = jnp.sum(sq_acc[...], keepdims=True) * (0.5 / D)
            dg_ref[...] = jnp.sum(dg_acc[...], axis=0, keepdims=True) * (1.0 / D)

        @pl.when(t == nj + 1)
        def _():
            step(False, False, True)

    assert nj >= 2
    tile_a = lambda t: jnp.clip(nj - 1 - t, 0, nj - 1)
    tile_c = lambda t: jnp.clip(nj + 1 - t, 0, nj - 1)
    at_c = lambda shape: pl.BlockSpec(shape, lambda t: (0,) * (len(shape) - 1) + (tile_c(t),))
    return pl.pallas_call(
        body, name="gate_outproj", grid=(nj + 2,),
        in_specs=_gate_specs(tile_a) + _gate_specs(tile_c)[:3] + [_full_spec((D, D))] + _x_specs3(tile_a)
                 + [_full_spec((NM, D)), _full_spec((1, D))] + _x_specs3(tile_a),
        out_specs=[pl.BlockSpec((TT, D), lambda t: (tile_a(t), 0)), _full_spec((D, D)), _full_spec((1, 1)),
                   _full_spec((1, D)), at_c((DA, TT)), at_c((H, 1, TT)), at_c((5 * DA, TT)),
                   _full_spec((DA, TB)), _full_spec((DA, TB)), _full_spec((3, DA, TB))],
        out_shape=[jax.ShapeDtypeStruct((L, D), F32), jax.ShapeDtypeStruct((D, D), BF16),
                   jax.ShapeDtypeStruct((1, 1), F32), jax.ShapeDtypeStruct((1, D), F32),
                   jax.ShapeDtypeStruct((DA, L), BF16),
                   jax.ShapeDtypeStruct((H, 1, L), F32),
                   jax.ShapeDtypeStruct((5 * DA, L), BF16),
                   jax.ShapeDtypeStruct((DA, TB), F32),
                   jax.ShapeDtypeStruct((DA, TB), F32),
                   jax.ShapeDtypeStruct((3, DA, TB), F32)],
        scratch_shapes=[pltpu.VMEM((D, D), F32), pltpu.VMEM((TT, D), F32), pltpu.VMEM((TT, D), BF16),
                        pltpu.VMEM((TT, D), BF16), pltpu.VMEM((D, TT), BF16), pltpu.VMEM((D, TT), BF16),
                        pltpu.VMEM((D, TT), BF16), pltpu.VMEM((D, TT), BF16),
                        pltpu.VMEM((8, D), F32), pltpu.VMEM((8, D), F32), pltpu.VMEM((DA, TB), F32)],
        compiler_params=_params(),
    )(o_t, gate_t, gate_t, cw_b, ga_b, gcn_b, o_t, gate_t, gate_t,
      w_out, x, x, x, meta_full, fng, target, target, target)


def _attn_bwd(proj_t, kaug, vtok, do_t, lse, dd, cq, L, after):
    nk = L // TT

    def body(q_ref, kaug_ref, vtok_ref, kt_ref, do_ref, lse_ref, dd_ref, cq_ref, _,
             dq_ref, dk_ref, dv_ref, dck_ref, dcq_ref, dq_acc, kt1_scr, s_scr, dp_scr, dv_scr, dk_scr):
        i = pl.program_id(0)
        rows = [slice(g * DH, (g + 1) * DH) for g in range(HG)]
        ones = jnp.ones((DF, TT), BF16)
        zpad = jnp.zeros((KA - DH - DF, TT), BF16)
        for g in range(HG):
            kt1_scr[g] = jnp.concatenate([kt_ref[rows[g], :], ones], axis=0)
        dv_scr[...] = jnp.zeros_like(dv_scr)
        dk_scr[...] = jnp.zeros_like(dk_scr)

        def q_rows(g, q_off):
            bias = cq_ref[g, :, pl.ds(q_off, TT)] - lse_ref[g, :, pl.ds(q_off, TT)]
            return jnp.concatenate([q_ref[rows[g], pl.ds(q_off, TT)], _bias_rows(bias)], axis=0)

        def scores(jq, masked):
            q_off = pl.multiple_of(jq * TT, TT)
            for g in range(HG):
                s = _dot(kaug_ref[g], jnp.concatenate([q_rows(g, q_off), zpad], axis=0))
                if masked:
                    s = jnp.where(_causal_mask(), s, NEG)
                s_scr[g] = s
                dp_scr[g] = _dot(vtok_ref[g], do_ref[rows[g], pl.ds(q_off, TT)])

        def grads(jq):
            q_off = pl.multiple_of(jq * TT, TT)
            for g in range(HG):
                p = jnp.exp2(s_scr[g])
                ds = (p * (dp_scr[g] - dd_ref[g, :, pl.ds(q_off, TT)])).astype(BF16)
                do1 = jnp.concatenate([do_ref[rows[g], pl.ds(q_off, TT)], jnp.zeros((KA - DH, TT), BF16)], axis=0)
                q1 = jnp.concatenate([q_rows(g, q_off), zpad], axis=0)
                dv_scr[g] += _dot(p.astype(BF16), do1, NT_DIMS)
                dk_scr[g] += _dot(ds, q1, NT_DIMS)
                dq_acc[g, :, pl.ds(q_off, TT)] += _dot(kt1_scr[g], ds)

        @pl.when(i == 0)
        def _():
            dq_acc[...] = jnp.zeros_like(dq_acc)

        scores(i, True)

        def step(jq, c):
            grads(jq)
            scores(jq + 1, False)
            return c

        lax.fori_loop(i, nk - 1, step, 0)
        grads(nk - 1)
        for g in range(HG):
            dv_ref[rows[g], :] = dv_scr[g].T[:DH, :].astype(BF16)
            dk_t = dk_scr[g].T
            dk_ref[rows[g], :] = (dk_t[:DH, :] * LN2).astype(BF16)
            dck_ref[g] = dk_t[DH:DH + 1, :]

        @pl.when(i == nk - 1)
        def _():
            for g in range(HG):
                dq_ref[rows[g], :] = (dq_acc[g, :DH, :] * (DH ** -0.5)).astype(BF16)
                dcq_ref[g] = dq_acc[g, DH:DH + 1, :]

    assert HG == H
    head = lambda i: (0, 0)
    row = lambda i: (0, 0, 0)
    return pl.pallas_call(
        body, name="attn_bwd", grid=(nk,),
        in_specs=[
            pl.BlockSpec((DA, L), head),
            pl.BlockSpec((H, TT, KA), lambda i: (0, i, 0)),
            pl.BlockSpec((H, TT, DH), lambda i: (0, i, 0)),
            pl.BlockSpec((DA, TT), lambda i: (1, i)),
            pl.BlockSpec((DA, L), head),
            pl.BlockSpec((H, 1, L), row), pl.BlockSpec((H, 1, L), row), pl.BlockSpec((H, 1, L), row), _UNREAD,
        ],
        out_specs=[
            pl.BlockSpec((DA, L), head),
            pl.BlockSpec((DA, TT), lambda i: (0, i)),
            pl.BlockSpec((DA, TT), lambda i: (0, i)),
            pl.BlockSpec((H, 1, TT), lambda i: (0, 0, i)),
            pl.BlockSpec((H, 1, L), row),
        ],
        out_shape=[jax.ShapeDtypeStruct((DA, L), BF16), jax.ShapeDtypeStruct((DA, L), BF16),
                   jax.ShapeDtypeStruct((DA, L), BF16), jax.ShapeDtypeStruct((H, 1, L), F32),
                   jax.ShapeDtypeStruct((H, 1, L), F32)],
        scratch_shapes=[
            pltpu.VMEM((HG, DH + DF, L), F32),
            pltpu.VMEM((HG, DH + DF, TT), BF16),
            pltpu.VMEM((HG, TT, TT), F32), pltpu.VMEM((HG, TT, TT), F32),
            pltpu.VMEM((HG, TT, KA), F32), pltpu.VMEM((HG, TT, KA), F32)],
        compiler_params=_params(),
    )(proj_t, kaug, vtok, proj_t, do_t, lse, dd, cq, after)


def _fgate_bwd(dcq, dck, sg, L):
    def body(dcq_ref, dck_ref, sg_ref, df_ref, db_ref):
        dc = jnp.concatenate([dcq_ref[h] - dck_ref[h] for h in range(H)], axis=0)
        idx = lax.broadcasted_iota(jnp.int32, (H, L), 1)
        r = dc
        s = 1
        while s < L:
            r = r + jnp.where(idx + s < L, pltpu.roll(r, L - s, 1), 0.0)
            s *= 2
        df = r * sg_ref[...]
        db_ref[...] = jnp.broadcast_to(jnp.sum(df, axis=1, keepdims=True), (H, TB))
        df_ref[...] = jnp.concatenate([df, jnp.zeros((DF - H, L), F32)], axis=0).astype(BF16)

    return pl.pallas_call(
        body, name="fgate_bwd",
        out_shape=[jax.ShapeDtypeStruct((DF, L), BF16), jax.ShapeDtypeStruct((H, TB), F32)],
        compiler_params=pltpu.CompilerParams(vmem_limit_bytes=VMEM_LIMIT),
    )(dcq, dck, sg)


def _inproj_bwd_x(w, dq_t, dk_t, dv_t, dg5_t, df_t, dout, x, meta_full, norm_g, L, after):
    nj = L // TT
    seq = x.shape[0]

    def body(w_ref, dq_ref, dk_ref, dv_ref, dg5_ref, df_ref, dout_ref, xa_ref, xb_ref, xc_ref, meta_ref, g_ref, _,
             gx_ref, dmeta_ref, dg_ref, dh_scr, sems):
        j = pl.program_id(0)
        slot = j % 2

        def copy_out(step, slot_):
            first = pltpu.make_async_copy(dh_scr.at[slot_, pl.ds(TB, TT - TB)], gx_ref.at[pl.ds(0, TT - TB)],
                                          sems.at[slot_])
            later = pltpu.make_async_copy(dh_scr.at[slot_], gx_ref.at[pl.ds(step * TT - TB, TT)], sems.at[slot_])
            return first, later

        @pl.when(j == 0)
        def _():
            dg_ref[...] = jnp.zeros_like(dg_ref)

        du = _dot(dq_ref[...], w_ref[0:DA, :], TN_DIMS)
        du += _dot(dk_ref[...], w_ref[DA:2 * DA, :], TN_DIMS)
        du += _dot(dv_ref[...], w_ref[2 * DA:3 * DA, :], TN_DIMS)
        du += _dot(dg5_ref[...], w_ref[3 * DA:NSEC * DA, :], TN_DIMS)
        du += _dot(df_ref[...], w_ref[NSEC * DA:DPROJ, :], TN_DIMS)
        hb = _h_tile(j, xa_ref, xb_ref, xc_ref, meta_ref)
        r = lax.rsqrt(jnp.mean(hb * hb, axis=-1, keepdims=True) + EPS)
        hn = hb * r
        dg_ref[...] += jnp.sum(du * hn, axis=0, keepdims=True)
        gu = du * g_ref[...]
        dh = dout_ref[...] + r * gu - hn * (r * jnp.mean(gu * hn, axis=-1, keepdims=True))

        dh_scr[slot] = dh

        @pl.when(j == 0)
        def _():
            dmeta_ref[...] = dh[P0:TB, :]
            copy_out(0, 0)[0].start()

        @pl.when(j >= 1)
        def _():
            copy_out(j, slot)[1].start()

        @pl.when(j == 1)
        def _():
            copy_out(0, 0)[0].wait()

        @pl.when(j >= 2)
        def _():
            copy_out(j - 1, 1 - slot)[1].wait()

        @pl.when(j == nj - 1)
        def _():
            copy_out(j, slot)[0 if nj == 1 else 1].wait()

    blk = lambda rows: pl.BlockSpec((rows, TT), lambda j: (0, j))
    return pl.pallas_call(
        body, name="inproj_bwd_x", grid=(nj,),
        in_specs=[_full_spec((DPROJ, D)), blk(DA), blk(DA), blk(DA), blk(5 * DA), blk(DF),
                  pl.BlockSpec((TT, D), lambda j: (j, 0))] + _x_specs3()
                 + [_full_spec((NM, D)), _full_spec((1, D)), _UNREAD],
        out_specs=[pl.BlockSpec(memory_space=pl.ANY), _full_spec((NM, D)), _full_spec((1, D))],
        out_shape=[jax.ShapeDtypeStruct((seq, D), F32), jax.ShapeDtypeStruct((NM, D), F32),
                   jax.ShapeDtypeStruct((1, D), F32)],
        scratch_shapes=[pltpu.VMEM((2, TT, D), F32), pltpu.SemaphoreType.DMA((2,))],
        compiler_params=_params(),
    )(w, dq_t, dk_t, dv_t, dg5_t, df_t, dout, x, x, x, meta_full, norm_g, after)


def _inproj_bwd_w(u, dq_t, dk_t, dv_t, dg5_t, df_t, L):
    def body(u_ref, dq_hbm, dk_hbm, dv_hbm, dg5_ref, df_ref, dw_ref, dwf_ref, qkv_scr, sems):
        s = pl.program_id(0)
        u_all = u_ref[...]
        fetch = [pltpu.make_async_copy(src, qkv_scr.at[k], sems.at[k])
                 for k, src in enumerate((dq_hbm, dk_hbm, dv_hbm))]

        @pl.when(s == 0)
        def _():
            for cp in fetch:
                cp.start()

        @pl.when(s < 5)
        def _():
            dw_ref[...] = _dot(dg5_ref[...], u_all)

        for k in range(3):
            @pl.when(s == 5 + k)
            def _(k=k):
                fetch[k].wait()
                dw_ref[...] = _dot(qkv_scr[k], u_all)

        @pl.when(s == NSEC - 1)
        def _():
            dwf_ref[...] = _dot(df_ref[...], u_all)

    once = lambda shape: pl.BlockSpec(shape, lambda s: (0, 0), pipeline_mode=pl.Buffered(1))
    any_spec = pl.BlockSpec(memory_space=pl.ANY)
    return pl.pallas_call(
        body, name="inproj_bwd_w", grid=(NSEC,),
        in_specs=[
            once((L, D)), any_spec, any_spec, any_spec,
            pl.BlockSpec((DA, L), lambda s: (jnp.minimum(s, 4), 0)),
            once((DF, L)),
        ],
        out_specs=[pl.BlockSpec((DA, D), lambda s: (jnp.where(s < 5, s + 3, s - 5), 0)), _full_spec((DF, D))],
        out_shape=[jax.ShapeDtypeStruct((NSEC * DA, D), F32), jax.ShapeDtypeStruct((DF, D), F32)],
        scratch_shapes=[pltpu.VMEM((3, DA, L), BF16), pltpu.SemaphoreType.DMA((3,))],
        compiler_params=_params(),
    )(u, dq_t, dk_t, dv_t, dg5_t, df_t)


def _adamw(w, g, m, v):
    m = ADAM_B1 * m + (1.0 - ADAM_B1) * g
    v = ADAM_B2 * v + (1.0 - ADAM_B2) * (g * g)
    m_hat = m / (1.0 - ADAM_B1 ** ADAM_STEP)
    v_hat = v / (1.0 - ADAM_B2 ** ADAM_STEP)
    delta = -ADAM_LR * (m_hat / (jnp.sqrt(v_hat) + ADAM_EPS) + ADAM_WD * w)
    return delta, m, v


def _adamw_big(own_in, land_in, own_out, land_out, w_in_t, m_in_t, v_in_t, w_out, m_out, v_out):
    cb = CB
    e_sh = D // NDEV
    in_shape = jax.ShapeDtypeStruct(w_in_t.shape, F32)
    out_shape = jax.ShapeDtypeStruct(w_out.shape, F32)

    def total(own_ref, land_ref, rows, chips):
        g = _pick_slab(0, own_ref, land_ref, rows, chips=chips).astype(F32)
        for j in range(1, own_ref.shape[0]):
            g = g + _pick_slab(j, own_ref, land_ref, rows, chips=chips).astype(F32)
        return g

    def body(oi_ref, li_ref, oo_ref, lo_ref, wi_ref, mi_ref, vi_ref, wo_ref, mo_ref, vo_ref,
             gi, di, mi, vi, go, do, mo, vo):
        g = total(oi_ref, li_ref, slice(0, WSHP), True)[:WSH]
        d, mn, vn = _adamw(wi_ref[...], g, mi_ref[...], vi_ref[...])
        gi[...], di[...], mi[...], vi[...] = g, d, mn, vn
        g = total(oo_ref, lo_ref, slice(0, e_sh), False)
        d, mn, vn = _adamw(wo_ref[0], g, mo_ref[0], vo_ref[0])
        go[0], do[0], mo[0], vo[0] = g, d, mn, vn

    slab = lambda n, rows: pl.BlockSpec((n, rows, cb), lambda i: (0, 0, i))
    ispec = pl.BlockSpec((WSH, cb), lambda i: (0, i))
    ospec = pl.BlockSpec((1, e_sh, cb), lambda i: (0, 0, i))
    return pl.pallas_call(
        body, name="adamw_big", grid=(D // cb,),
        in_specs=[slab(4, WSHP), slab(4, WSHP), slab(NDEV, e_sh), slab(NDEV, e_sh),
                  ispec, ispec, ispec, ospec, ospec, ospec],
        out_specs=[ispec] * 4 + [ospec] * 4, out_shape=[in_shape] * 4 + [out_shape] * 4,
        compiler_params=_params(),
    )(own_in, land_in, own_out, land_out, w_in_t, m_in_t, v_in_t, w_out, m_out, v_out)


F0 = 3 * DA


def _unshard_w_out(own, land):
    e_sh = D // NDEV

    def body(own_ref, land_ref, wo_ref):
        for j in range(NDEV):
            wo_ref[j * e_sh:(j + 1) * e_sh, :] = _pick_slab(j, own_ref, land_ref, slice(0, e_sh), per_peer=False)

    return pl.pallas_call(
        body, name="unshard_w_out", grid=(D // CB,),
        in_specs=[pl.BlockSpec((e_sh, CB), lambda i: (0, i)), pl.BlockSpec((NDEV, e_sh, CB), lambda i: (0, 0, i))],
        out_specs=pl.BlockSpec((D, CB), lambda i: (0, i)),
        out_shape=jax.ShapeDtypeStruct((D, D), BF16),
        compiler_params=_params(),
    )(own, land)


def _unshard_w_in(w_all, small_all, attn_gain, conv_gain):
    def body(w_ref, small_ref, ga_ref, gc_ref, wt_ref, meta_ref, cwb_ref, gab_ref, gcb_ref):
        i = pl.program_id(0)
        for k in range(CB // TB):
            meta_ref[:, k * TB:(k + 1) * TB] = small_ref[(CB // TB) * i + k, 0:NM, :]

        @pl.when(i == 0)
        def _():
            per_row = lambda line: jnp.broadcast_to(line, (TB, DA)).T
            cw = jnp.concatenate([small_ref[j, NM:NM + 3, 0:DH] for j in range(NDEV)], axis=1)
            for k in range(3):
                cwb_ref[k] = per_row(cw[k:k + 1, :])
            gab_ref[...] = per_row(ga_ref[...])
            gcb_ref[...] = per_row(gc_ref[...])

        def ref_rows(lo, hi):
            pieces, r = [], lo
            while r < hi:
                sh, off = divmod(r, WSH)
                n = min(hi - r, WSH - off)
                pieces.append(w_ref[sh, off:off + n, :])
                r += n
            return pieces

        for s in range(NSEC):
            lo = s * DA if s < 3 else s * DA + H
            wt_ref[s * DA:(s + 1) * DA, :] = jnp.concatenate(ref_rows(lo, lo + DA), axis=0)
        wt_ref[NSEC * DA:DPROJ, :] = jnp.concatenate(
            ref_rows(F0, F0 + H) + [jnp.zeros((DF - H, CB), BF16)], axis=0)

    return pl.pallas_call(
        body, name="unshard_w_in", grid=(D // CB,),
        in_specs=[pl.BlockSpec((NDEV, WSHP, CB), lambda i: (0, 0, i)), _full_spec(small_all.shape),
                  _full_spec((1, DA)), _full_spec((1, DA))],
        out_specs=[pl.BlockSpec((DPROJ, CB), lambda i: (0, i)), pl.BlockSpec((NM, CB), lambda i: (0, i)),
                   _full_spec((3, DA, TB)), _full_spec((DA, TB)), _full_spec((DA, TB))],
        out_shape=[jax.ShapeDtypeStruct((DPROJ, D), BF16), jax.ShapeDtypeStruct((NM, D), F32),
                   jax.ShapeDtypeStruct((3, DA, TB), F32), jax.ShapeDtypeStruct((DA, TB), F32),
                   jax.ShapeDtypeStruct((DA, TB), F32)],
        compiler_params=_params(),
    )(w_all, small_all, attn_gain, conv_gain)


def _shard_w_in_grads(dw_main, dw_f):
    def body(dm_ref, df_ref, p_ref):
        mc = lax.axis_index("c")

        def ref_rows(lo, hi):
            pieces, r = [], lo
            while r < hi:
                if r < F0:
                    n = min(hi, F0) - r
                    pieces.append(dm_ref[r:r + n, :])
                elif r < F0 + H:
                    n = min(hi, F0 + H) - r
                    pieces.append(df_ref[r - F0:r - F0 + n, :])
                else:
                    n = hi - r
                    pieces.append(dm_ref[r - H:r - H + n, :])
                r += n
            return pieces

        for i in range(NDEV):
            rows = jnp.concatenate(ref_rows(i * WSH, (i + 1) * WSH) + [jnp.zeros((WSHP - WSH, CB), F32)], axis=0)
            p_ref[i // 2 + jnp.where(mc == i % 2, 0, 4)] = rows.astype(BF16)

    col = lambda rows: pl.BlockSpec((rows, CB), lambda i: (0, i))
    return pl.pallas_call(
        body, name="shard_w_in_grads", grid=(D // CB,),
        in_specs=[col(NSEC * DA), col(DF)],
        out_specs=pl.BlockSpec((NDEV, WSHP, CB), lambda i: (0, 0, i)),
        out_shape=jax.ShapeDtypeStruct((NDEV, WSHP, D), BF16),
        compiler_params=_params(),
    )(dw_main, dw_f)


SMALL = ("norm_g", "final_norm_g", "attn_norm_g", "conv_norm_g", "b_f", "meta", "conv_w")


def _as_rows(x):
    return jnp.concatenate([x[:, r * TB:(r + 1) * TB] for r in range(x.shape[1] // TB)], axis=0)


def _as_line(rows):
    return jnp.concatenate([rows[r:r + 1, :] for r in range(rows.shape[0])], axis=1)


def _pad_rows(x, n=8):
    return jnp.concatenate([x, jnp.zeros((n - x.shape[0], x.shape[1]), F32)], axis=0)


def _tile_rows(a, rows, lanes=TB):
    a = a.reshape(rows, lanes)
    return jnp.pad(a, ((0, -rows % 8), (0, TB - lanes)))


def _pack_small_grads(dg_norm, dg_final, dga_p, dgc_p, dcw_p, db_b, dmeta, loss):
    def body(dgn_ref, dgf_ref, dga_ref, dgc_ref, dcw_ref, db_ref, dmeta_ref, loss_ref, out_ref):
        def lane_sums(p):
            return jnp.sum(p.T, axis=0, keepdims=True)

        lane = lax.broadcasted_iota(jnp.int32, (1, TB), 1)
        b_row = jnp.where(lane == H, loss_ref[...], 0.0)
        for h in range(H):
            b_row = b_row + jnp.where(lane == h, db_ref[h:h + 1, :], 0.0)
        common = jnp.concatenate([
            _as_rows(dgn_ref[...]), _as_rows(dgf_ref[...]), _pad_rows(_as_rows(lane_sums(dga_ref[...]))),
            _pad_rows(_as_rows(lane_sums(dgc_ref[...]))), _pad_rows(b_row)], axis=0)
        dcw = [lane_sums(dcw_ref[k]) for k in range(3)]
        for j in range(NDEV):
            cw = jnp.concatenate(
                [jnp.concatenate([r[:, j * DH:(j + 1) * DH], jnp.zeros((1, TB - DH), F32)], axis=1) for r in dcw],
                axis=0)
            out_ref[j] = jnp.concatenate([common, dmeta_ref[:, j * TB:(j + 1) * TB], _pad_rows(cw)], axis=0)

    return pl.pallas_call(
        body, name="pack_small_grads", out_shape=jax.ShapeDtypeStruct((NDEV, SROWS, TB), F32),
    )(dg_norm, dg_final, dga_p, dgc_p, dcw_p, db_b, dmeta, loss)


def _adamw_small(own, land, params):
    flat = [a for n in SMALL for a in params[n]]

    def body(*refs):
        own_ref, land_ref = refs[:2]
        ins = refs[2:2 + 3 * len(SMALL)]
        outs = refs[2 + 3 * len(SMALL):]
        g = _pick_slab(0, own_ref, land_ref, slice(0, SROWS))
        for j in range(1, NDEV):
            g = g + _pick_slab(j, own_ref, land_ref, slice(0, SROWS))
        grads = dict(
            norm_g=_as_line(g[0:8]), final_norm_g=_as_line(g[8:16]), attn_norm_g=_as_line(g[16:20]),
            conv_norm_g=_as_line(g[24:28]), b_f=g[32:33, :H], meta=g[40:56], conv_w=g[56:59, :DH][None])
        for i, n in enumerate(SMALL):
            w_ref, m_ref, v_ref = ins[3 * i:3 * i + 3]
            d, mn, vn = _adamw(w_ref[...], grads[n], m_ref[...], v_ref[...])
            for o_ref, val in zip(outs[4 * i:4 * i + 4], (grads[n], d, mn, vn)):
                o_ref[...] = val
        outs[-1][...] = g[32:33, H:H + 1]

    shapes = [jax.ShapeDtypeStruct(params[n][0].shape, F32) for n in SMALL for _ in range(4)]
    res = pl.pallas_call(
        body, name="adamw_small", out_shape=shapes + [jax.ShapeDtypeStruct((1, 1), F32)],
    )(own, land, *flat)
    return {n: res[4 * i:4 * i + 4] for i, n in enumerate(SMALL)}, res[-1]


def kernel(x, meta, norm_g, w_in, b_f, conv_w, attn_norm_g, conv_norm_g, w_out, final_norm_g, loss_target, m_meta, m_norm_g, m_w_in, m_b_f, m_conv_w, m_attn_norm_g, m_conv_norm_g, m_w_out, m_final_norm_g, v_meta, v_norm_g, v_w_in, v_b_f, v_conv_w, v_attn_norm_g, v_conv_norm_g, v_w_out, v_final_norm_g):
    seq = x.shape[1]
    L = seq + TB
    assert x.shape == (1, seq, D) and L % TT == 0 and w_in.shape == (1, D, WSH)
    x2 = x[0]
    tgt = loss_target[0]

    w_in_slab = jnp.pad(w_in[0].T, ((0, WSHP - WSH), (0, 0))).astype(BF16)
    w_out_slab = w_out[0].astype(BF16)
    meta_slab = jnp.concatenate([meta, _tile_rows(conv_w[0], 3, DH)], axis=0)
    wout_flight = _split_start(w_out_slab, "gather_w_out_start", per_peer=False)
    w_all, small_all = _all_gather([w_in_slab, meta_slab], "gather_w_in")

    w_t, meta_full, cw_b, ga_b, gcn_b = _unshard_w_in(w_all, small_all, attn_norm_g, conv_norm_g)

    u, proj_t, gate_t, f_t, ktok, vtok = _inproj_fwd(x2, meta_full, norm_g, w_t, L, after=wout_flight[4])
    cq, kaug, sg = _fgate_fwd(f_t, b_f.reshape(H, 1), ktok, L)
    o_t, lse = _attn_fwd(proj_t, kaug, cq, L)

    w_out_own, w_out_land = _split_wait(wout_flight, o_t, "gather_w_out_wait", per_peer=False)
    w_out_full = _unshard_w_out(w_out_own, w_out_land)
    dout, dw_out, loss_part, dg_final, do_t, dd, dg5_t, dga_p, dgc_p, dcw_p = _gate_outproj(
        o_t, gate_t, cw_b, ga_b, gcn_b, w_out_full, x2, meta_full, final_norm_g.reshape(1, D), tgt, L)
    dwo_flight = _split_start(dw_out.reshape(NDEV, D // NDEV, D), "exchange_dw_out_start", per_peer=True)
    dq_t, dk_t, dv_t, dck, dcq = _attn_bwd(proj_t, kaug, vtok, do_t, lse, dd, cq, L, after=dwo_flight[4])
    df_t, db_f = _fgate_bwd(dcq, dck, sg, L)
    dw_main, dw_f = _inproj_bwd_w(u, dq_t, dk_t, dv_t, dg5_t, df_t, L)
    dwi_parts = _shard_w_in_grads(dw_main, dw_f)
    dwi_chip = _pair_sum(dwi_parts, _pair_exchange(dwi_parts, "exchange_dw_in_pair"))
    dwi_flight = _split_start(dwi_chip, "exchange_dw_in_start", per_peer=True, chips=True)
    grad_x, dmeta, dg_norm = _inproj_bwd_x(
        w_t, dq_t, dk_t, dv_t, dg5_t, df_t, dout, x2, meta_full, norm_g, L, after=dwi_flight[4])
    small_parts = _pack_small_grads(dg_norm, dg_final, dga_p, dgc_p, dcw_p, db_f, dmeta, loss_part)
    small_flight = _split_start(small_parts, "exchange_small_start", per_peer=True)
    dwo_own, dwo_land = _split_wait(dwo_flight, small_flight[4], "exchange_dw_out_wait", per_peer=True)
    dwi_own, dwi_land = _split_wait(dwi_flight, dwo_land, "exchange_dw_in_wait", per_peer=True, chips=True)

    big_out = _adamw_big(dwi_own, dwi_land, dwo_own, dwo_land,
                         w_in[0].T, m_w_in[0].T, v_w_in[0].T, w_out, m_w_out, v_w_out)
    g_w_in, d_w_in, nm_w_in, nv_w_in = [a.T[None] for a in big_out[:4]]
    g_w_out, d_w_out, nm_w_out, nv_w_out = big_out[4:]
    sm_own, sm_land = _split_wait(small_flight, big_out[4], "exchange_small_wait", per_peer=True)
    line = lambda a: a.reshape(1, D)
    small, loss = _adamw_small(sm_own, sm_land, dict(
        norm_g=(norm_g, m_norm_g, v_norm_g),
        final_norm_g=(line(final_norm_g), line(m_final_norm_g), line(v_final_norm_g)),
        attn_norm_g=(attn_norm_g, m_attn_norm_g, v_attn_norm_g),
        conv_norm_g=(conv_norm_g, m_conv_norm_g, v_conv_norm_g),
        b_f=(b_f, m_b_f, v_b_f), meta=(meta, m_meta, v_meta), conv_w=(conv_w, m_conv_w, v_conv_w)))
    small["final_norm_g"] = [a.reshape(D) for a in small["final_norm_g"]]
    order = ("meta", "norm_g", "w_in", "b_f", "conv_w", "attn_norm_g", "conv_norm_g", "w_out", "final_norm_g")
    groups = []
    for k, (wi, wo) in enumerate(((g_w_in, g_w_out), (d_w_in, d_w_out), (nm_w_in, nm_w_out), (nv_w_in, nv_w_out))):
        d = dict({n: small[n][k] for n in SMALL}, w_in=wi, w_out=wo)
        groups.append([d[n] for n in order])
    return (loss[0, 0], grad_x[None], *groups[0], *groups[1], *groups[2], *groups[3])
```

```python
import jax
import jax.numpy as jnp
from jax import lax
from jax.experimental import pallas as pl
from jax.experimental.pallas import tpu as pltpu

F32 = jnp.float32
BF16 = jnp.bfloat16

D = 1024
DA = 512
H = 8
DH = 64
NM = 16
TB = 128
P0 = TB - NM
TT = 3 * TB
HG = 8
NDEV = 8
NSEC = 8
DF = 16
DPROJ = NSEC * DA + DF
WSH = 513
WSHP = 528
SROWS = 64
EPS = 1e-6
NEG = -1e30
LOG2E = 1.4426950408889634
LN2 = 0.6931471805599453
QSCALE = DH ** -0.5 * LOG2E
KA = 128
CB = 256
VMEM_LIMIT = 56 * 1024 * 1024

ADAM_LR = 0.001
ADAM_B1 = 0.9
ADAM_B2 = 0.999
ADAM_EPS = 1e-08
ADAM_WD = 0.01
ADAM_STEP = 10

NT_DIMS = (((1,), (1,)), ((), ()))
TN_DIMS = (((0,), (0,)), ((), ()))
MESH = pl.DeviceIdType.MESH


def _params(n_axes=1, vmem=VMEM_LIMIT):
    return pltpu.CompilerParams(dimension_semantics=("arbitrary",) * n_axes, vmem_limit_bytes=vmem)


def _dot(a, b, dims=None):
    if dims is None:
        return jnp.dot(a, b, preferred_element_type=F32)
    return lax.dot_general(a, b, dims, preferred_element_type=F32)


def _my_place():
    return lax.axis_index("x"), lax.axis_index("y"), lax.axis_index("c")


def _all_gather(xs, name):
    n = len(xs)

    def body(*refs):
        x_refs, out_refs = refs[:n], refs[n:2 * n]
        send_sems, recv_sems, local_sems = refs[2 * n:]
        mx, my, mc = _my_place()

        def across(px, py, pc, axis_a):
            flip_x = pc if axis_a else 1 - pc
            return (px + flip_x) % 2, (py + 1 - flip_x) % 2, pc

        def idx(p):
            return 4 * p[0] + 2 * p[1] + p[2]

        me, sib = (mx, my, mc), (mx, my, 1 - mc)
        a_nbr, b_nbr = across(*me, True), across(*me, False)
        diag = across(*b_nbr, True)
        sib_a, sib_b = across(*sib, True), across(*sib, False)
        sib_diag = across(*sib_b, True)

        waits = []
        for t in range(n):
            out_ref = out_refs[t]

            def copy(k, block, to, src=None, out_ref=out_ref, t=t):
                return pltpu.make_async_remote_copy(
                    src_ref=out_ref.at[idx(block)] if src is None else src, dst_ref=out_ref.at[idx(block)],
                    send_sem=send_sems.at[7 * t + k], recv_sem=recv_sems.at[7 * t + k],
                    device_id=to, device_id_type=MESH)

            mine = pltpu.make_async_copy(x_refs[t], out_ref.at[idx(me)], local_sems.at[t])
            mine.start()
            started = [copy(0, me, sib, src=x_refs[t]), copy(1, me, a_nbr, src=x_refs[t]),
                       copy(2, me, b_nbr, src=x_refs[t])]
            for cp in started:
                cp.start()
            waits.append((copy, mine, started))
        relays = ((1, a_nbr, ((3, b_nbr), (4, sib))), (2, b_nbr, ((5, sib),)), (3, diag, ((6, sib),)))
        for landed, block, onward in relays:
            for copy, _, started in waits:
                copy(landed, block, me).wait_recv()
                for k, to in onward:
                    started.append(copy(k, block, to))
                    started[-1].start()
        for copy, mine, started in waits:
            for k, block in ((0, sib), (4, sib_a), (5, sib_b), (6, sib_diag)):
                copy(k, block, me).wait_recv()
            for cp in started:
                cp.wait_send()
            mine.wait()

    any_spec = pl.BlockSpec(memory_space=pl.ANY)
    return pl.pallas_call(
        body, name=name,
        out_shape=[jax.ShapeDtypeStruct((NDEV,) + x.shape, x.dtype) for x in xs],
        in_specs=[any_spec] * n, out_specs=[any_spec] * n,
        scratch_shapes=[pltpu.SemaphoreType.DMA((7 * n,)), pltpu.SemaphoreType.DMA((7 * n,)),
                        pltpu.SemaphoreType.DMA((n,))],
    )(*xs)


_HBM = pl.BlockSpec(memory_space=pltpu.HBM)
_UNREAD = pl.BlockSpec(memory_space=pl.ANY)
_SEM = pl.BlockSpec(memory_space=pltpu.SEMAPHORE)
_EFFECT = pltpu.SideEffectType.DATAFLOW_SIDE_EFFECTING


def _peer_of(m, place):
    mx, my, mc = place
    return ((1 - mx) if m & 4 else mx, (1 - my) if m & 2 else my, (1 - mc) if m & 1 else mc)


def _party(chips):
    if chips == "pair":
        return (lambda p: p[2]), (1,)
    if chips:
        return (lambda p: 2 * p[0] + p[1]), (2, 4, 6)
    return (lambda p: 4 * p[0] + 2 * p[1] + p[2]), tuple(range(1, NDEV))


def _split_copies(src_ref, land_ref, send_sems, recv_sems, per_peer, incoming, chips):
    place = _my_place()
    slot, masks = _party(chips)
    me = slot(place)
    out = []
    for k, m in enumerate(masks):
        there = _peer_of(m, place)
        peer = slot(there)
        src = (src_ref.at[me] if incoming else src_ref.at[peer]) if per_peer else src_ref
        out.append(pltpu.make_async_remote_copy(
            src_ref=src, dst_ref=land_ref.at[peer if incoming else me],
            send_sem=send_sems.at[k], recv_sem=recv_sems.at[k], device_id=there, device_id_type=MESH))
    return out


def _split_start(src, name, per_peer, chips=False):
    slab = src.shape[1:] if per_peer else src.shape
    n = len(_party(chips)[1])

    def body(src_ref, land_ref, send_sems, recv_sems, src_thru, land_thru, token):
        for cp in _split_copies(src_ref, land_ref, send_sems, recv_sems, per_peer, False, chips):
            cp.start()
        token[...] = jnp.zeros_like(token)

    return pl.pallas_call(
        body, name=name,
        out_shape=(pltpu.SemaphoreType.DMA((n,)), pltpu.SemaphoreType.DMA((n,)),
                   pltpu.HBM(src.shape, src.dtype), pltpu.HBM((n + 1,) + slab, src.dtype),
                   jax.ShapeDtypeStruct((8, TB), F32)),
        in_specs=(_HBM, _HBM), out_specs=(_SEM, _SEM, _HBM, _HBM, pl.BlockSpec(memory_space=pltpu.VMEM)),
        input_output_aliases={0: 2, 1: 3},
        compiler_params=pltpu.CompilerParams(has_side_effects=_EFFECT),
    )(pltpu.with_memory_space_constraint(src, pltpu.HBM),
      pltpu.with_memory_space_constraint(lax.empty((n + 1,) + slab, src.dtype), pltpu.HBM))


def _split_wait(handles, after, name, per_peer, chips=False):
    send_sems, recv_sems, src_thru, land_thru, _ = handles

    def body(src_ref, land_ref, send_sems, recv_sems, after_ref, src_out, land_out):
        for cp in _split_copies(src_ref, land_ref, send_sems, recv_sems, per_peer, False, chips):
            cp.wait_send()
        for cp in _split_copies(src_ref, land_ref, send_sems, recv_sems, per_peer, True, chips):
            cp.wait_recv()

    return pl.pallas_call(
        body, name=name,
        out_shape=(pltpu.HBM(src_thru.shape, src_thru.dtype), pltpu.HBM(land_thru.shape, land_thru.dtype)),
        in_specs=(_HBM, _HBM, _SEM, _SEM, pl.BlockSpec(memory_space=pl.ANY)), out_specs=(_HBM, _HBM),
        input_output_aliases={0: 0, 1: 1},
        compiler_params=pltpu.CompilerParams(has_side_effects=_EFFECT),
    )(src_thru, land_thru, send_sems, recv_sems, after)


def _pick_slab(j, own_ref, land_ref, rows, per_peer=True, chips=False):
    me = _party(chips)[0](_my_place())
    own = (lambda: own_ref[j, rows, :]) if per_peer else (lambda: own_ref[rows, :])
    return lax.cond(me == j, own, lambda: land_ref[j, rows, :])


def _pair_sum(p, land):
    rows = p.shape[1]

    def body(p_ref, land_ref, out_ref):
        sib = 1 - lax.axis_index("c")
        for q in range(4):
            out_ref[q] = (p_ref[q].astype(F32) + land_ref[sib, q].astype(F32)).astype(BF16)

    blk = pl.BlockSpec((4, rows, CB), lambda i: (0, 0, i))
    return pl.pallas_call(
        body, name="pair_sum", grid=(D // CB,),
        in_specs=[blk, pl.BlockSpec((2, 4, rows, CB), lambda i: (0, 0, 0, i))], out_specs=blk,
        out_shape=jax.ShapeDtypeStruct((4, rows, D), BF16), compiler_params=_params(),
    )(p, land)


def _x_specs3(tile=lambda j: j):
    return [pl.BlockSpec((TB, D), lambda j: (jnp.maximum(3 * tile(j) - 1, 0), 0)),
            pl.BlockSpec((TB, D), lambda j: (3 * tile(j), 0)),
            pl.BlockSpec((TB, D), lambda j: (3 * tile(j) + 1, 0))]


def _h_tile(j, xa_ref, xb_ref, xc_ref, meta_ref):
    first = jnp.concatenate([jnp.zeros((P0, D), F32), meta_ref[...]], axis=0)
    return jnp.concatenate([jnp.where(j == 0, first, xa_ref[...]), xb_ref[...], xc_ref[...]], axis=0)


def _full_spec(shape):
    return pl.BlockSpec(shape, lambda *_: (0,) * len(shape))


def _sigmoid(z):
    return 1.0 / (1.0 + jnp.exp(-z))


def _lane_tiles_sum(x):
    out = x[:, :TB]
    for i in range(1, x.shape[1] // TB):
        out = out + x[:, i * TB:(i + 1) * TB]
    return out


def _inproj_fwd(x, meta_full, norm_g, w_t, L, after):
    nj = L // TT

    def body(xa_ref, xb_ref, xc_ref, meta_ref, g_ref, w_ref, _, u_ref, proj_ref, gate_ref, f_ref, ktok_ref, vtok_ref):
        hb = _h_tile(pl.program_id(0), xa_ref, xb_ref, xc_ref, meta_ref)
        r = lax.rsqrt(jnp.mean(hb * hb, axis=-1, keepdims=True) + EPS)
        u = (hb * r * g_ref[...]).astype(BF16)
        u_ref[...] = u
        for s in range(NSEC):
            p = _dot(u, w_ref[s * DA:(s + 1) * DA, :], NT_DIMS)
            if s == 0:
                p = p * QSCALE
            if s in (1, 2):
                tok_ref = ktok_ref if s == 1 else vtok_ref
                for h in range(H):
                    tok_ref[h] = p[:, h * DH:(h + 1) * DH].astype(BF16)
            out_ref, s_out = (proj_ref, s) if s < 3 else (gate_ref, s - 3)
            out_ref[s_out * DA:(s_out + 1) * DA, :] = p.T.astype(BF16)
        f_ref[...] = _dot(w_ref[NSEC * DA:DPROJ, :], u, NT_DIMS)[:H]

    return pl.pallas_call(
        body, name="inproj_fwd", grid=(nj,),
        in_specs=_x_specs3() + [_full_spec((NM, D)), _full_spec((1, D)), _full_spec((DPROJ, D)), _UNREAD],
        out_specs=[
            pl.BlockSpec((TT, D), lambda t: (t, 0)),
            pl.BlockSpec((3 * DA, TT), lambda t: (0, t)),
            pl.BlockSpec((None, (NSEC - 3) * DA, TT), lambda t: (t, 0, 0)),
            pl.BlockSpec((H, TT), lambda t: (0, t)),
            pl.BlockSpec((H, TT, DH), lambda t: (0, t, 0)),
            pl.BlockSpec((H, TT, DH), lambda t: (0, t, 0)),
        ],
        out_shape=[
            jax.ShapeDtypeStruct((L, D), BF16),
            jax.ShapeDtypeStruct((3 * DA, L), BF16),
            jax.ShapeDtypeStruct((nj, (NSEC - 3) * DA, TT), BF16),
            jax.ShapeDtypeStruct((H, L), F32),
            jax.ShapeDtypeStruct((H, L, DH), BF16),
            jax.ShapeDtypeStruct((H, L, DH), BF16),
        ],
        compiler_params=_params(),
    )(x, x, x, meta_full, norm_g, w_t, after)


def _split3(x):
    hi = x.astype(BF16).astype(F32)
    r = x - hi
    mid = r.astype(BF16).astype(F32)
    return hi, mid, (r - mid).astype(BF16).astype(F32)


def _bias_rows(bias):
    one = jnp.ones((1, TT), F32)
    zero = jnp.zeros((1, TT), F32)
    parts = [zero] * 3 if bias is None else list(_split3(bias))
    return jnp.concatenate([one] * 3 + parts + [zero] * (DF - 6), axis=0).astype(BF16)


def _fgate_fwd(f_t, b_col, ktok, L):
    nb = L // TB

    def body(f_ref, b_ref, ktok_ref, cq_ref, kaug_ref, sg_ref, bias_scr):
        h = pl.program_id(0)

        @pl.when(h == 0)
        def _():
            z = f_ref[...] + b_ref[...]
            idx = lax.broadcasted_iota(jnp.int32, (H, L), 1)
            real = idx >= P0
            lf = jnp.where(real, jnp.minimum(z, 0.0) - jnp.log1p(jnp.exp(-jnp.abs(z))), 0.0)
            sg_ref[...] = jnp.where(real, 1.0 / (1.0 + jnp.exp(z)), 0.0)
            c = lf
            s = 1
            while s < L:
                c = c + jnp.where(idx >= s, pltpu.roll(c, s, 1), 0.0)
                s *= 2
            c = c * LOG2E
            for hh in range(H):
                cq_ref[hh] = c[hh:hh + 1, :]
            for part, val in enumerate(_split3(-jnp.where(real, c, -NEG))):
                for hh in range(H):
                    bias_scr[part * H + hh] = val[hh:hh + 1, :]

        lane = lax.broadcasted_iota(jnp.int32, (TB, KA), 1)
        head = jnp.zeros((DH, TB), F32)
        tail = jnp.concatenate([jnp.ones((3, TB), F32), jnp.zeros((KA - DH - 6, TB), F32)], axis=0)
        for b in range(nb):
            blk = slice(b * TB, (b + 1) * TB)
            cols = jnp.concatenate(
                [head] + [bias_scr[part * H + h, :, blk] for part in range(3)] + [tail], axis=0).T
            k = jnp.concatenate([ktok_ref[0, blk, :].astype(F32), jnp.zeros((TB, KA - DH), F32)], axis=1)
            kaug_ref[0, blk, :] = jnp.where(lane < DH, k, cols).astype(BF16)

    return pl.pallas_call(
        body, name="fgate_fwd", grid=(H,),
        in_specs=[_full_spec((H, L)), _full_spec((H, 1)), pl.BlockSpec((1, L, DH), lambda h: (h, 0, 0))],
        out_specs=[_full_spec((H, 1, L)), pl.BlockSpec((1, L, KA), lambda h: (h, 0, 0)), _full_spec((H, L))],
        out_shape=[
            jax.ShapeDtypeStruct((H, 1, L), F32),
            jax.ShapeDtypeStruct((H, L, KA), BF16),
            jax.ShapeDtypeStruct((H, L), F32),
        ],
        scratch_shapes=[pltpu.VMEM((3 * H, 1, L), F32)],
        compiler_params=_params(),
    )(f_t, b_col, ktok)


def _causal_mask():
    r = lax.broadcasted_iota(jnp.int32, (TT, TT), 0)
    c = lax.broadcasted_iota(jnp.int32, (TT, TT), 1)
    return r <= c


def _attn_fwd(proj_t, kaug, cq, L):
    nq = L // TT

    def body(q_ref, qn_ref, kaug_ref, v_ref, cq_ref, o_ref, lse_ref,
             qa_scr, s_scr, cmax_scr, m_scr, p_scr, alpha_scr, acc_scr):
        j = pl.program_id(0)
        rows = [slice(g * DH, (g + 1) * DH) for g in range(HG)]
        ones = jnp.ones((DF, TT), BF16)

        def load_queries(ref):
            for g in range(HG):
                qa_scr[g] = jnp.concatenate(
                    [ref[rows[g], :], _bias_rows(None), jnp.zeros((KA - DH - DF, TT), BF16)], axis=0)

        def scores(kt, masked):
            k_off = pl.multiple_of(kt * TT, TT)
            for g in range(HG):
                s = _dot(kaug_ref[g, pl.ds(k_off, TT), :], qa_scr[g])
                if masked:
                    s = jnp.where(_causal_mask(), s, NEG)
                s_scr[g] = s
                cmax_scr[g] = jnp.max(s, axis=0, keepdims=True)

        def softmax():
            for g in range(HG):
                m_old = m_scr[g]
                m_new = jnp.maximum(m_old, cmax_scr[g])
                alpha_scr[g] = jnp.exp2(m_old - m_new)
                p_scr[g] = jnp.exp2(s_scr[g] - m_new).astype(BF16)
                m_scr[g] = m_new

        def weighted_sum(kt):
            k_off = pl.multiple_of(kt * TT, TT)
            for g in range(HG):
                v1 = jnp.concatenate([v_ref[rows[g], pl.ds(k_off, TT)], ones], axis=0)
                acc_scr[g] = alpha_scr[g] * acc_scr[g] + _dot(v1, p_scr[g])

        @pl.when(j == 0)
        def _():
            load_queries(q_ref)
            scores(0, True)

        m_scr[...] = jnp.full_like(m_scr, NEG)
        acc_scr[...] = jnp.zeros_like(acc_scr)

        @pl.when(j >= 1)
        def _():
            softmax()
            scores(j - 1, False)

        def step(i, c):
            weighted_sum(j - i + 1)
            softmax()
            scores(j - i - 1, False)
            return c

        lax.fori_loop(1, j, step, 0)

        def drain(second_last, next_tile):
            if second_last:
                weighted_sum(1)
            softmax()
            if next_tile:
                load_queries(qn_ref)
                scores(j + 1, True)
            weighted_sum(0)

        @pl.when(j == 0)
        def _():
            drain(False, nq > 1)

        @pl.when((j >= 1) & (j < nq - 1))
        def _():
            drain(True, True)

        @pl.when((j >= 1) & (j == nq - 1))
        def _():
            drain(True, False)

        for g in range(HG):
            l = acc_scr[g, DH:DH + 1, :]
            o_ref[rows[g], :] = acc_scr[g, :DH, :] * (1.0 / l)
            lse_ref[g] = m_scr[g] + jnp.log2(l) + cq_ref[g]

    assert HG == H
    return pl.pallas_call(
        body, name="attn_fwd", grid=(nq,),
        in_specs=[
            pl.BlockSpec((DA, TT), lambda j: (0, j)),
            pl.BlockSpec((DA, TT), lambda j: (0, jnp.minimum(j + 1, nq - 1))),
            pl.BlockSpec((H, L, KA), lambda j: (0, 0, 0)),
            pl.BlockSpec((DA, L), lambda j: (2, 0)),
            pl.BlockSpec((H, 1, TT), lambda j: (0, 0, j)),
        ],
        out_specs=[
            pl.BlockSpec((None, DA, TT), lambda j: (j, 0, 0)),
            pl.BlockSpec((H, 1, TT), lambda j: (0, 0, j)),
        ],
        out_shape=[jax.ShapeDtypeStruct((nq, DA, TT), F32), jax.ShapeDtypeStruct((H, 1, L), F32)],
        scratch_shapes=[pltpu.VMEM((HG, KA, TT), BF16), pltpu.VMEM((HG, TT, TT), F32), pltpu.VMEM((HG, 1, TT), F32),
                        pltpu.VMEM((HG, 1, TT), F32), pltpu.VMEM((HG, TT, TT), BF16), pltpu.VMEM((HG, 1, TT), F32),
                        pltpu.VMEM((HG, DH + DF, TT), F32)],
        compiler_params=_params(),
    )(proj_t, proj_t, kaug, proj_t, cq)


def _gate_group(rows, o_ref, za_ref, gb_ref, gc_ref, xc_ref, zc_ref, gcp_ref, xcp_ref, cw_ref, ga_ref, gcn_ref, first):
    n_rep = TT // TB
    f32 = lambda r: r[rows, :].astype(F32)
    o, za, gb, gc, xc, zc = o_ref[rows, :], f32(za_ref), f32(gb_ref), f32(gc_ref), f32(xc_ref), f32(zc_ref)
    a = gc * xc
    a_prev = jnp.where(first, 0.0, f32(gcp_ref) * f32(xcp_ref))
    full = jnp.concatenate([a_prev, a], axis=1)
    a1 = pltpu.roll(full, 1, 1)[:, TB:]
    a2 = pltpu.roll(full, 2, 1)[:, TB:]
    w0 = jnp.tile(cw_ref[0, rows, :], (1, n_rep))
    w1 = jnp.tile(cw_ref[1, rows, :], (1, n_rep))
    w2 = jnp.tile(cw_ref[2, rows, :], (1, n_rep))
    cv = w0 * a2 + w1 * a1 + w2 * a
    e = gb * cv
    rc = lax.rsqrt(jnp.mean(e * e, axis=0, keepdims=True) + EPS)
    ec = e * rc
    ra = lax.rsqrt(jnp.mean(o * o, axis=0, keepdims=True) + EPS)
    oa = o * ra
    g_a = jnp.tile(ga_ref[rows, :], (1, n_rep))
    g_c = jnp.tile(gcn_ref[rows, :], (1, n_rep))
    sa = _sigmoid(za)
    sc = _sigmoid(zc)
    return dict(o=o, za=za, gb=gb, gc=gc, xc=xc, zc=zc, a=a, a1=a1, a2=a2, w0=w0, w1=w1, w2=w2, cv=cv, e=e,
                rc=rc, ec=ec, ra=ra, oa=oa, g_a=g_a, g_c=g_c, sa=sa, sc=sc)


def _gate_specs(tile):
    halo = pl.BlockSpec((None, 2 * DA, TB),
                        lambda i: (jnp.maximum(tile(i) - 1, 0), 1, TT // TB - 1))
    return [pl.BlockSpec((None, DA, TT), lambda i: (tile(i), 0, 0)),
            pl.BlockSpec((None, 5 * DA, TT), lambda i: (tile(i), 0, 0)), halo,
            _full_spec((3, DA, TB)), _full_spec((DA, TB)), _full_spec((DA, TB))]


def _gate_views(g5_ref, halo_ref):
    return [g5_ref.at[pl.ds(s * DA, DA)] for s in range(5)] + [halo_ref.at[pl.ds(s * DA, DA)] for s in range(2)]


def _gate_outproj(o_t, gate_t, cw_b, ga_b, gcn_b, w_out, x, meta_full, fng, target, L):
    nj = L // TT
    rp = NM
    n_bwd = 8
    cb = D // 4
    assert P0 % rp == 0 and TB % rp == 0 and (TT // rp) % n_bwd == 0 and H == n_bwd

    def body(o_ref, g5_ref, halo_ref, cw_ref, ga_ref, gcn_ref, o2_ref, g52_ref, halo2_ref,
             w_ref, xa_ref, xb_ref, xc_ref, meta_ref, g_ref, ta_ref, tb_ref, tc_ref,
             dout_ref, dwb_ref, loss_ref, dg_ref, do_ref, dd_ref, dg5_ref, dga_ref, dgc_ref, dcw_ref,
             dw_ref, o_scr, db_new, db_old, mix_new, mix_old, dmix_new, dmix_old, sq_acc, dg_acc, carry_ref):
        za_ref, gb_ref, gc_ref, xcv_ref, zc_ref, gcp_ref, xcp_ref = _gate_views(g5_ref, halo_ref)
        za2_ref, gb2_ref, gc2_ref, xcv2_ref, zc2_ref, gcp2_ref, xcp2_ref = _gate_views(g52_ref, halo2_ref)
        t = pl.program_id(0)
        first_a = t == nj - 1
        first_c = t == nj + 1

        def gate_rows(h):
            rows = slice(h * DH, (h + 1) * DH)
            g = _gate_group(rows, o_ref, za_ref, gb_ref, gc_ref, xcv_ref, zc_ref, gcp_ref, xcp_ref,
                            cw_ref, ga_ref, gcn_ref, first_a)
            mix_new[rows, :] = (g["oa"] * g["g_a"] * (g["za"] * g["sa"])).astype(BF16)
            mix_new[DA + h * DH:DA + (h + 1) * DH, :] = (g["ec"] * g["g_c"] * (g["zc"] * g["sc"])).astype(BF16)

        def loss_rows(c):
            blk = c // (TB // rp)
            rows, out_rows = pl.ds((c % (TB // rp)) * rp, rp), pl.ds(c * rp, rp)
            h = (xa_ref, xb_ref, xc_ref)[blk][rows, :]
            if blk == 0:
                first = meta_ref[...] if c == P0 // rp else jnp.zeros((rp, D), F32)
                h = jnp.where(first_a, first, h)
            o = o_scr[out_rows, :] + h
            r = lax.rsqrt(jnp.mean(o * o, axis=-1, keepdims=True) + EPS)
            orn = o * r
            g = g_ref[...]
            diff = orn * g - (ta_ref, tb_ref, tc_ref)[blk][rows, :]
            if blk == 0:
                diff = diff * jnp.where(first_a, 0.0, 1.0)
            gy = diff * (g * (1.0 / D))
            dout = r * (gy - orn * jnp.mean(gy * orn, axis=-1, keepdims=True))
            dout_ref[out_rows, :] = dout
            db_new[out_rows, :] = dout.astype(BF16)
            sq, go = diff * diff, diff * orn
            sq_acc[...] += sq[:8] + sq[8:]
            dg_acc[...] += go[:8] + go[8:]

        def backward_cols(n):
            if n < 4:
                cols = slice(n * cb, (n + 1) * cb)
                dmix_new[cols, :] = _dot(db_old[...], w_ref[cols, :], NT_DIMS).T.astype(BF16)
            else:
                cols = slice((n - 4) * cb, (n - 3) * cb)
                dw_ref[:, cols] += _dot(mix_old[...], db_old[:, cols])

        def gate_bwd_rows(h):
            rows = slice(h * DH, (h + 1) * DH)
            sec = lambda s: slice(s * DA + h * DH, s * DA + (h + 1) * DH)
            g = _gate_group(rows, o2_ref, za2_ref, gb2_ref, gc2_ref, xcv2_ref, zc2_ref, gcp2_ref, xcp2_ref,
                            cw_ref, ga_ref, gcn_ref, first_c)
            o, za, gb, gc, xc, zc, sa, sc = (g[n] for n in ("o", "za", "gb", "gc", "xc", "zc", "sa", "sc"))
            dya = dmix_old[rows, :].astype(F32)
            dyc = dmix_old[DA + h * DH:DA + (h + 1) * DH, :].astype(F32)

            dn = dya * (za * sa)
            dg5_ref[sec(0), :] = (dya * (g["oa"] * g["g_a"]) * (sa * (1.0 + za * (1.0 - sa)))).astype(BF16)
            dga_ref[rows, :] += _lane_tiles_sum(dn * g["oa"])
            dng = dn * g["g_a"]
            mean_a = jnp.mean(dng * g["oa"], axis=0, keepdims=True)
            do = (dng - g["oa"] * mean_a) * g["ra"]
            do_ref[rows, :] = do.astype(BF16)
            dd_ref[h] = jnp.sum(do * o, axis=0, keepdims=True)

            dnc = dyc * (zc * sc)
            dg5_ref[sec(4), :] = (dyc * (g["ec"] * g["g_c"]) * (sc * (1.0 + zc * (1.0 - sc)))).astype(BF16)
            dgc_ref[rows, :] += _lane_tiles_sum(dnc * g["ec"])
            dncg = dnc * g["g_c"]
            mean_c = jnp.mean(dncg * g["ec"], axis=0, keepdims=True)
            de = (dncg - g["ec"] * mean_c) * g["rc"]
            dg5_ref[sec(1), :] = (de * g["cv"]).astype(BF16)
            dcv = de * gb
            full = jnp.concatenate([dcv, carry_ref[rows, :]], axis=1)
            d1 = pltpu.roll(full, TT + TB - 1, 1)[:, :TT]
            d2 = pltpu.roll(full, TT + TB - 2, 1)[:, :TT]
            carry_ref[rows, :] = dcv[:, :TB]
            da = g["w2"] * dcv + g["w1"] * d1 + g["w0"] * d2
            dg5_ref[sec(2), :] = (da * xc).astype(BF16)
            dg5_ref[sec(3), :] = (da * gc).astype(BF16)
            dcw_ref[0, rows, :] += _lane_tiles_sum(dcv * g["a2"])
            dcw_ref[1, rows, :] += _lane_tiles_sum(dcv * g["a1"])
            dcw_ref[2, rows, :] += _lane_tiles_sum(dcv * g["a"])

        def step(a, b, c):
            half = H // 2
            for h in range(H):
                if a:
                    gate_rows(h)
                if c and h < half:
                    gate_bwd_rows(h)
                if b and h % 2 == 1:
                    backward_cols(h // 2)
            if a:
                o_scr[...] = _dot(mix_new[...], w_ref[...], TN_DIMS)
            per = TT // rp // n_bwd
            for k in range(n_bwd):
                if a:
                    for piece in range(per * k, per * (k + 1)):
                        loss_rows(piece)
                if c and k % 2 == 0:
                    gate_bwd_rows(half + k // 2)
                if b and k % 2 == 1:
                    backward_cols(n_bwd // 2 + k // 2)
            if a:
                db_old[...] = db_new[...]
                mix_old[...] = mix_new[...]
            if b:
                dmix_old[...] = dmix_new[...]

        @pl.when(t == 0)
        def _():
            dw_ref[...] = jnp.zeros_like(dw_ref)
            sq_acc[...] = jnp.zeros_like(sq_acc)
            dg_acc[...] = jnp.zeros_like(dg_acc)
            carry_ref[...] = jnp.zeros_like(carry_ref)
            dga_ref[...] = jnp.zeros_like(dga_ref)
            dgc_ref[...] = jnp.zeros_like(dgc_ref)
            dcw_ref[...] = jnp.zeros_like(dcw_ref)
            step(True, False, False)

        @pl.when(t == 1)
        def _():
            step(True, True, False)

        @pl.when((t >= 2) & (t < nj))
        def _():
            step(True, True, True)

        @pl.when(t == nj)
        def _():
            step(False, True, True)
            dwb_ref[...] = dw_ref[...].astype(BF16)
            loss_ref[...] = jnp.sum(sq_acc[...], keepdims=True) * (0.5 / D)
            dg_ref[...] = jnp.sum(dg_acc[...], axis=0, keepdims=True) * (1.0 / D)

        @pl.when(t == nj + 1)
        def _():
            step(False, False, True)

    assert nj >= 2
    tile_a = lambda t: jnp.clip(nj - 1 - t, 0, nj - 1)
    tile_c = lambda t: jnp.clip(nj + 1 - t, 0, nj - 1)
    at_c = lambda shape: pl.BlockSpec(shape, lambda t: (0,) * (len(shape) - 1) + (tile_c(t),))
    return pl.pallas_call(
        body, name="gate_outproj", grid=(nj + 2,),
        in_specs=_gate_specs(tile_a) + _gate_specs(tile_c)[:3] + [_full_spec((D, D))] + _x_specs3(tile_a)
                 + [_full_spec((NM, D)), _full_spec((1, D))] + _x_specs3(tile_a),
        out_specs=[pl.BlockSpec((TT, D), lambda t: (tile_a(t), 0)), _full_spec((D, D)), _full_spec((1, 1)),
                   _full_spec((1, D)), at_c((DA, TT)), at_c((H, 1, TT)), at_c((5 * DA, TT)),
                   _full_spec((DA, TB)), _full_spec((DA, TB)), _full_spec((3, DA, TB))],
        out_shape=[jax.ShapeDtypeStruct((L, D), F32), jax.ShapeDtypeStruct((D, D), BF16),
                   jax.ShapeDtypeStruct((1, 1), F32), jax.ShapeDtypeStruct((1, D), F32),
                   jax.ShapeDtypeStruct((DA, L), BF16),
                   jax.ShapeDtypeStruct((H, 1, L), F32),
                   jax.ShapeDtypeStruct((5 * DA, L), BF16),
                   jax.ShapeDtypeStruct((DA, TB), F32),
                   jax.ShapeDtypeStruct((DA, TB), F32),
                   jax.ShapeDtypeStruct((3, DA, TB), F32)],
        scratch_shapes=[pltpu.VMEM((D, D), F32), pltpu.VMEM((TT, D), F32), pltpu.VMEM((TT, D), BF16),
                        pltpu.VMEM((TT, D), BF16), pltpu.VMEM((D, TT), BF16), pltpu.VMEM((D, TT), BF16),
                        pltpu.VMEM((D, TT), BF16), pltpu.VMEM((D, TT), BF16),
                        pltpu.VMEM((8, D), F32), pltpu.VMEM((8, D), F32), pltpu.VMEM((DA, TB), F32)],
        compiler_params=_params(),
    )(o_t, gate_t, gate_t, cw_b, ga_b, gcn_b, o_t, gate_t, gate_t,
      w_out, x, x, x, meta_full, fng, target, target, target)


def _attn_bwd(proj_t, kaug, vtok, do_t, lse, dd, cq, L, after):
    nk = L // TT

    def body(q_ref, kaug_ref, vtok_ref, kt_ref, do_ref, lse_ref, dd_ref, cq_ref, _,
             dq_ref, dk_ref, dv_ref, dck_ref, dcq_ref, dq_acc, kt1_scr, s_scr, dp_scr, dv_scr, dk_scr):
        i = pl.program_id(0)
        rows = [slice(g * DH, (g + 1) * DH) for g in range(HG)]
        ones = jnp.ones((DF, TT), BF16)
        zpad = jnp.zeros((KA - DH - DF, TT), BF16)
        for g in range(HG):
            kt1_scr[g] = jnp.concatenate([kt_ref[rows[g], :], ones], axis=0)
        dv_scr[...] = jnp.zeros_like(dv_scr)
        dk_scr[...] = jnp.zeros_like(dk_scr)

        def q_rows(g, q_off):
            bias = cq_ref[g, :, pl.ds(q_off, TT)] - lse_ref[g, :, pl.ds(q_off, TT)]
            return jnp.concatenate([q_ref[rows[g], pl.ds(q_off, TT)], _bias_rows(bias)], axis=0)

        def scores(jq, masked):
            q_off = pl.multiple_of(jq * TT, TT)
            for g in range(HG):
                s = _dot(kaug_ref[g], jnp.concatenate([q_rows(g, q_off), zpad], axis=0))
                if masked:
                    s = jnp.where(_causal_mask(), s, NEG)
                s_scr[g] = s
                dp_scr[g] = _dot(vtok_ref[g], do_ref[rows[g], pl.ds(q_off, TT)])

        def grads(jq):
            q_off = pl.multiple_of(jq * TT, TT)
            for g in range(HG):
                p = jnp.exp2(s_scr[g])
                ds = (p * (dp_scr[g] - dd_ref[g, :, pl.ds(q_off, TT)])).astype(BF16)
                do1 = jnp.concatenate([do_ref[rows[g], pl.ds(q_off, TT)], jnp.zeros((KA - DH, TT), BF16)], axis=0)
                q1 = jnp.concatenate([q_rows(g, q_off), zpad], axis=0)
                dv_scr[g] += _dot(p.astype(BF16), do1, NT_DIMS)
                dk_scr[g] += _dot(ds, q1, NT_DIMS)
                dq_acc[g, :, pl.ds(q_off, TT)] += _dot(kt1_scr[g], ds)

        @pl.when(i == 0)
        def _():
            dq_acc[...] = jnp.zeros_like(dq_acc)

        scores(i, True)

        def step(jq, c):
            grads(jq)
            scores(jq + 1, False)
            return c

        lax.fori_loop(i, nk - 1, step, 0)
        grads(nk - 1)
        for g in range(HG):
            dv_ref[rows[g], :] = dv_scr[g].T[:DH, :].astype(BF16)
            dk_t = dk_scr[g].T
            dk_ref[rows[g], :] = (dk_t[:DH, :] * LN2).astype(BF16)
            dck_ref[g] = dk_t[DH:DH + 1, :]

        @pl.when(i == nk - 1)
        def _():
            for g in range(HG):
                dq_ref[rows[g], :] = (dq_acc[g, :DH, :] * (DH ** -0.5)).astype(BF16)
                dcq_ref[g] = dq_acc[g, DH:DH + 1, :]

    assert HG == H
    head = lambda i: (0, 0)
    row = lambda i: (0, 0, 0)
    return pl.pallas_call(
        body, name="attn_bwd", grid=(nk,),
        in_specs=[
            pl.BlockSpec((DA, L), head),
            pl.BlockSpec((H, TT, KA), lambda i: (0, i, 0)),
            pl.BlockSpec((H, TT, DH), lambda i: (0, i, 0)),
            pl.BlockSpec((DA, TT), lambda i: (1, i)),
            pl.BlockSpec((DA, L), head),
            pl.BlockSpec((H, 1, L), row), pl.BlockSpec((H, 1, L), row), pl.BlockSpec((H, 1, L), row), _UNREAD,
        ],
        out_specs=[
            pl.BlockSpec((DA, L), head),
            pl.BlockSpec((DA, TT), lambda i: (0, i)),
            pl.BlockSpec((DA, TT), lambda i: (0, i)),
            pl.BlockSpec((H, 1, TT), lambda i: (0, 0, i)),
            pl.BlockSpec((H, 1, L), row),
        ],
        out_shape=[jax.ShapeDtypeStruct((DA, L), BF16), jax.ShapeDtypeStruct((DA, L), BF16),
                   jax.ShapeDtypeStruct((DA, L), BF16), jax.ShapeDtypeStruct((H, 1, L), F32),
                   jax.ShapeDtypeStruct((H, 1, L), F32)],
        scratch_shapes=[
            pltpu.VMEM((HG, DH + DF, L), F32),
            pltpu.VMEM((HG, DH + DF, TT), BF16),
            pltpu.VMEM((HG, TT, TT), F32), pltpu.VMEM((HG, TT, TT), F32),
            pltpu.VMEM((HG, TT, KA), F32), pltpu.VMEM((HG, TT, KA), F32)],
        compiler_params=_params(),
    )(proj_t, kaug, vtok, proj_t, do_t, lse, dd, cq, after)


def _fgate_bwd(dcq, dck, sg, L):
    def body(dcq_ref, dck_ref, sg_ref, df_ref, db_ref):
        dc = jnp.concatenate([dcq_ref[h] - dck_ref[h] for h in range(H)], axis=0)
        idx = lax.broadcasted_iota(jnp.int32, (H, L), 1)
        r = dc
        s = 1
        while s < L:
            r = r + jnp.where(idx + s < L, pltpu.roll(r, L - s, 1), 0.0)
            s *= 2
        df = r * sg_ref[...]
        db_ref[...] = jnp.broadcast_to(jnp.sum(df, axis=1, keepdims=True), (H, TB))
        df_ref[...] = jnp.concatenate([df, jnp.zeros((DF - H, L), F32)], axis=0).astype(BF16)

    return pl.pallas_call(
        body, name="fgate_bwd",
        out_shape=[jax.ShapeDtypeStruct((DF, L), BF16), jax.ShapeDtypeStruct((H, TB), F32)],
        compiler_params=pltpu.CompilerParams(vmem_limit_bytes=VMEM_LIMIT),
    )(dcq, dck, sg)


def _inproj_bwd_x(w, dq_t, dk_t, dv_t, dg5_t, df_t, dout, x, meta_full, norm_g, L, after):
    nj = L // TT
    seq = x.shape[0]

    def body(w_ref, dq_ref, dk_ref, dv_ref, dg5_ref, df_ref, dout_ref, xa_ref, xb_ref, xc_ref, meta_ref, g_ref, _,
             gx_ref, dmeta_ref, dg_ref, dh_scr, sems):
        j = pl.program_id(0)
        slot = j % 2

        def copy_out(step, slot_):
            first = pltpu.make_async_copy(dh_scr.at[slot_, pl.ds(TB, TT - TB)], gx_ref.at[pl.ds(0, TT - TB)],
                                          sems.at[slot_])
            later = pltpu.make_async_copy(dh_scr.at[slot_], gx_ref.at[pl.ds(step * TT - TB, TT)], sems.at[slot_])
            return first, later

        @pl.when(j == 0)
        def _():
            dg_ref[...] = jnp.zeros_like(dg_ref)

        du = _dot(dq_ref[...], w_ref[0:DA, :], TN_DIMS)
        du += _dot(dk_ref[...], w_ref[DA:2 * DA, :], TN_DIMS)
        du += _dot(dv_ref[...], w_ref[2 * DA:3 * DA, :], TN_DIMS)
        du += _dot(dg5_ref[...], w_ref[3 * DA:NSEC * DA, :], TN_DIMS)
        du += _dot(df_ref[...], w_ref[NSEC * DA:DPROJ, :], TN_DIMS)
        hb = _h_tile(j, xa_ref, xb_ref, xc_ref, meta_ref)
        r = lax.rsqrt(jnp.mean(hb * hb, axis=-1, keepdims=True) + EPS)
        hn = hb * r
        dg_ref[...] += jnp.sum(du * hn, axis=0, keepdims=True)
        gu = du * g_ref[...]
        dh = dout_ref[...] + r * gu - hn * (r * jnp.mean(gu * hn, axis=-1, keepdims=True))

        dh_scr[slot] = dh

        @pl.when(j == 0)
        def _():
            dmeta_ref[...] = dh[P0:TB, :]
            copy_out(0, 0)[0].start()

        @pl.when(j >= 1)
        def _():
            copy_out(j, slot)[1].start()

        @pl.when(j == 1)
        def _():
            copy_out(0, 0)[0].wait()

        @pl.when(j >= 2)
        def _():
            copy_out(j - 1, 1 - slot)[1].wait()

        @pl.when(j == nj - 1)
        def _():
            copy_out(j, slot)[0 if nj == 1 else 1].wait()

    blk = lambda rows: pl.BlockSpec((rows, TT), lambda j: (0, j))
    return pl.pallas_call(
        body, name="inproj_bwd_x", grid=(nj,),
        in_specs=[_full_spec((DPROJ, D)), blk(DA), blk(DA), blk(DA), blk(5 * DA), blk(DF),
                  pl.BlockSpec((TT, D), lambda j: (j, 0))] + _x_specs3()
                 + [_full_spec((NM, D)), _full_spec((1, D)), _UNREAD],
        out_specs=[pl.BlockSpec(memory_space=pl.ANY), _full_spec((NM, D)), _full_spec((1, D))],
        out_shape=[jax.ShapeDtypeStruct((seq, D), F32), jax.ShapeDtypeStruct((NM, D), F32),
                   jax.ShapeDtypeStruct((1, D), F32)],
        scratch_shapes=[pltpu.VMEM((2, TT, D), F32), pltpu.SemaphoreType.DMA((2,))],
        compiler_params=_params(),
    )(w, dq_t, dk_t, dv_t, dg5_t, df_t, dout, x, x, x, meta_full, norm_g, after)


def _inproj_bwd_w(u, dq_t, dk_t, dv_t, dg5_t, df_t, L):
    def body(u_ref, dq_hbm, dk_hbm, dv_hbm, dg5_ref, df_ref, dw_ref, dwf_ref, qkv_scr, sems):
        s = pl.program_id(0)
        u_all = u_ref[...]
        fetch = [pltpu.make_async_copy(src, qkv_scr.at[k], sems.at[k])
                 for k, src in enumerate((dq_hbm, dk_hbm, dv_hbm))]

        @pl.when(s == 0)
        def _():
            for cp in fetch:
                cp.start()

        @pl.when(s < 5)
        def _():
            dw_ref[...] = _dot(dg5_ref[...], u_all)

        for k in range(3):
            @pl.when(s == 5 + k)
            def _(k=k):
                fetch[k].wait()
                dw_ref[...] = _dot(qkv_scr[k], u_all)

        @pl.when(s == NSEC - 1)
        def _():
            dwf_ref[...] = _dot(df_ref[...], u_all)

    once = lambda shape: pl.BlockSpec(shape, lambda s: (0, 0), pipeline_mode=pl.Buffered(1))
    any_spec = pl.BlockSpec(memory_space=pl.ANY)
    return pl.pallas_call(
        body, name="inproj_bwd_w", grid=(NSEC,),
        in_specs=[
            once((L, D)), any_spec, any_spec, any_spec,
            pl.BlockSpec((DA, L), lambda s: (jnp.minimum(s, 4), 0)),
            once((DF, L)),
        ],
        out_specs=[pl.BlockSpec((DA, D), lambda s: (jnp.where(s < 5, s + 3, s - 5), 0)), _full_spec((DF, D))],
        out_shape=[jax.ShapeDtypeStruct((NSEC * DA, D), F32), jax.ShapeDtypeStruct((DF, D), F32)],
        scratch_shapes=[pltpu.VMEM((3, DA, L), BF16), pltpu.SemaphoreType.DMA((3,))],
        compiler_params=_params(),
    )(u, dq_t, dk_t, dv_t, dg5_t, df_t)


def _adamw(w, g, m, v):
    m = ADAM_B1 * m + (1.0 - ADAM_B1) * g
    v = ADAM_B2 * v + (1.0 - ADAM_B2) * (g * g)
    m_hat = m / (1.0 - ADAM_B1 ** ADAM_STEP)
    v_hat = v / (1.0 - ADAM_B2 ** ADAM_STEP)
    delta = -ADAM_LR * (m_hat / (jnp.sqrt(v_hat) + ADAM_EPS) + ADAM_WD * w)
    return delta, m, v


def _adamw_big(own_in, land_in, own_out, land_out, w_in_t, m_in_t, v_in_t, w_out, m_out, v_out):
    cb = CB
    e_sh = D // NDEV
    in_shape = jax.ShapeDtypeStruct(w_in_t.shape, F32)
    out_shape = jax.ShapeDtypeStruct(w_out.shape, F32)

    def total(own_ref, land_ref, rows, chips):
        g = _pick_slab(0, own_ref, land_ref, rows, chips=chips).astype(F32)
        for j in range(1, own_ref.shape[0]):
            g = g + _pick_slab(j, own_ref, land_ref, rows, chips=chips).astype(F32)
        return g

    def body(oi_ref, li_ref, oo_ref, lo_ref, wi_ref, mi_ref, vi_ref, wo_ref, mo_ref, vo_ref,
             gi, di, mi, vi, go, do, mo, vo):
        g = total(oi_ref, li_ref, slice(0, WSHP), True)[:WSH]
        d, mn, vn = _adamw(wi_ref[...], g, mi_ref[...], vi_ref[...])
        gi[...], di[...], mi[...], vi[...] = g, d, mn, vn
        g = total(oo_ref, lo_ref, slice(0, e_sh), False)
        d, mn, vn = _adamw(wo_ref[0], g, mo_ref[0], vo_ref[0])
        go[0], do[0], mo[0], vo[0] = g, d, mn, vn

    slab = lambda n, rows: pl.BlockSpec((n, rows, cb), lambda i: (0, 0, i))
    ispec = pl.BlockSpec((WSH, cb), lambda i: (0, i))
    ospec = pl.BlockSpec((1, e_sh, cb), lambda i: (0, 0, i))
    return pl.pallas_call(
        body, name="adamw_big", grid=(D // cb,),
        in_specs=[slab(4, WSHP), slab(4, WSHP), slab(NDEV, e_sh), slab(NDEV, e_sh),
                  ispec, ispec, ispec, ospec, ospec, ospec],
        out_specs=[ispec] * 4 + [ospec] * 4, out_shape=[in_shape] * 4 + [out_shape] * 4,
        compiler_params=_params(),
    )(own_in, land_in, own_out, land_out, w_in_t, m_in_t, v_in_t, w_out, m_out, v_out)


F0 = 3 * DA


def _unshard_w_out(own, land):
    e_sh = D // NDEV

    def body(own_ref, land_ref, wo_ref):
        for j in range(NDEV):
            wo_ref[j * e_sh:(j + 1) * e_sh, :] = _pick_slab(j, own_ref, land_ref, slice(0, e_sh), per_peer=False)

    return pl.pallas_call(
        body, name="unshard_w_out", grid=(D // CB,),
        in_specs=[pl.BlockSpec((e_sh, CB), lambda i: (0, i)), pl.BlockSpec((NDEV, e_sh, CB), lambda i: (0, 0, i))],
        out_specs=pl.BlockSpec((D, CB), lambda i: (0, i)),
        out_shape=jax.ShapeDtypeStruct((D, D), BF16),
        compiler_params=_params(),
    )(own, land)


def _unshard_w_in(w_all, small_all, attn_gain, conv_gain):
    def body(w_ref, small_ref, ga_ref, gc_ref, wt_ref, meta_ref, cwb_ref, gab_ref, gcb_ref):
        i = pl.program_id(0)
        for k in range(CB // TB):
            meta_ref[:, k * TB:(k + 1) * TB] = small_ref[(CB // TB) * i + k, 0:NM, :]

        @pl.when(i == 0)
        def _():
            per_row = lambda line: jnp.broadcast_to(line, (TB, DA)).T
            cw = jnp.concatenate([small_ref[j, NM:NM + 3, 0:DH] for j in range(NDEV)], axis=1)
            for k in range(3):
                cwb_ref[k] = per_row(cw[k:k + 1, :])
            gab_ref[...] = per_row(ga_ref[...])
            gcb_ref[...] = per_row(gc_ref[...])

        def ref_rows(lo, hi):
            pieces, r = [], lo
            while r < hi:
                sh, off = divmod(r, WSH)
                n = min(hi - r, WSH - off)
                pieces.append(w_ref[sh, off:off + n, :])
                r += n
            return pieces

        for s in range(NSEC):
            lo = s * DA if s < 3 else s * DA + H
            wt_ref[s * DA:(s + 1) * DA, :] = jnp.concatenate(ref_rows(lo, lo + DA), axis=0)
        wt_ref[NSEC * DA:DPROJ, :] = jnp.concatenate(
            ref_rows(F0, F0 + H) + [jnp.zeros((DF - H, CB), BF16)], axis=0)

    return pl.pallas_call(
        body, name="unshard_w_in", grid=(D // CB,),
        in_specs=[pl.BlockSpec((NDEV, WSHP, CB), lambda i: (0, 0, i)), _full_spec(small_all.shape),
                  _full_spec((1, DA)), _full_spec((1, DA))],
        out_specs=[pl.BlockSpec((DPROJ, CB), lambda i: (0, i)), pl.BlockSpec((NM, CB), lambda i: (0, i)),
                   _full_spec((3, DA, TB)), _full_spec((DA, TB)), _full_spec((DA, TB))],
        out_shape=[jax.ShapeDtypeStruct((DPROJ, D), BF16), jax.ShapeDtypeStruct((NM, D), F32),
                   jax.ShapeDtypeStruct((3, DA, TB), F32), jax.ShapeDtypeStruct((DA, TB), F32),
                   jax.ShapeDtypeStruct((DA, TB), F32)],
        compiler_params=_params(),
    )(w_all, small_all, attn_gain, conv_gain)


def _shard_w_in_grads(dw_main, dw_f, own_core, name, after=None):
    def body(dm_ref, df_ref, _, p_ref):
        mc = lax.axis_index("c")

        def ref_rows(lo, hi):
            pieces, r = [], lo
            while r < hi:
                if r < F0:
                    n = min(hi, F0) - r
                    pieces.append(dm_ref[r:r + n, :])
                elif r < F0 + H:
                    n = min(hi, F0 + H) - r
                    pieces.append(df_ref[r - F0:r - F0 + n, :])
                else:
                    n = hi - r
                    pieces.append(dm_ref[r - H:r - H + n, :])
                r += n
            return pieces

        for i in range(NDEV):
            @pl.when((mc == i % 2) if own_core else (mc != i % 2))
            def _(i=i):
                rows = jnp.concatenate(
                    ref_rows(i * WSH, (i + 1) * WSH) + [jnp.zeros((WSHP - WSH, CB), F32)], axis=0)
                p_ref[i // 2] = rows.astype(BF16)

    col = lambda rows: pl.BlockSpec((rows, CB), lambda i: (0, i))
    return pl.pallas_call(
        body, name=name, grid=(D // CB,),
        in_specs=[col(NSEC * DA), col(DF), _UNREAD],
        out_specs=pl.BlockSpec((4, WSHP, CB), lambda i: (0, 0, i)),
        out_shape=jax.ShapeDtypeStruct((4, WSHP, D), BF16),
        compiler_params=_params(),
    )(dw_main, dw_f, dw_f if after is None else after)


SMALL = ("norm_g", "final_norm_g", "attn_norm_g", "conv_norm_g", "b_f", "meta", "conv_w")


def _as_rows(x):
    return jnp.concatenate([x[:, r * TB:(r + 1) * TB] for r in range(x.shape[1] // TB)], axis=0)


def _as_line(rows):
    return jnp.concatenate([rows[r:r + 1, :] for r in range(rows.shape[0])], axis=1)


def _pad_rows(x, n=8):
    return jnp.concatenate([x, jnp.zeros((n - x.shape[0], x.shape[1]), F32)], axis=0)


def _tile_rows(a, rows, lanes=TB):
    a = a.reshape(rows, lanes)
    return jnp.pad(a, ((0, -rows % 8), (0, TB - lanes)))


def _pack_small_grads(dg_norm, dg_final, dga_p, dgc_p, dcw_p, db_b, dmeta, loss):
    def body(dgn_ref, dgf_ref, dga_ref, dgc_ref, dcw_ref, db_ref, dmeta_ref, loss_ref, out_ref):
        def lane_sums(p):
            return jnp.sum(p.T, axis=0, keepdims=True)

        lane = lax.broadcasted_iota(jnp.int32, (1, TB), 1)
        b_row = jnp.where(lane == H, loss_ref[...], 0.0)
        for h in range(H):
            b_row = b_row + jnp.where(lane == h, db_ref[h:h + 1, :], 0.0)
        common = jnp.concatenate([
            _as_rows(dgn_ref[...]), _as_rows(dgf_ref[...]), _pad_rows(_as_rows(lane_sums(dga_ref[...]))),
            _pad_rows(_as_rows(lane_sums(dgc_ref[...]))), _pad_rows(b_row)], axis=0)
        dcw = [lane_sums(dcw_ref[k]) for k in range(3)]
        for j in range(NDEV):
            cw = jnp.concatenate(
                [jnp.concatenate([r[:, j * DH:(j + 1) * DH], jnp.zeros((1, TB - DH), F32)], axis=1) for r in dcw],
                axis=0)
            out_ref[j] = jnp.concatenate([common, dmeta_ref[:, j * TB:(j + 1) * TB], _pad_rows(cw)], axis=0)

    return pl.pallas_call(
        body, name="pack_small_grads", out_shape=jax.ShapeDtypeStruct((NDEV, SROWS, TB), F32),
    )(dg_norm, dg_final, dga_p, dgc_p, dcw_p, db_b, dmeta, loss)


def _adamw_small(own, land, params):
    flat = [a for n in SMALL for a in params[n]]

    def body(*refs):
        own_ref, land_ref = refs[:2]
        ins = refs[2:2 + 3 * len(SMALL)]
        outs = refs[2 + 3 * len(SMALL):]
        g = _pick_slab(0, own_ref, land_ref, slice(0, SROWS))
        for j in range(1, NDEV):
            g = g + _pick_slab(j, own_ref, land_ref, slice(0, SROWS))
        grads = dict(
            norm_g=_as_line(g[0:8]), final_norm_g=_as_line(g[8:16]), attn_norm_g=_as_line(g[16:20]),
            conv_norm_g=_as_line(g[24:28]), b_f=g[32:33, :H], meta=g[40:56], conv_w=g[56:59, :DH][None])
        for i, n in enumerate(SMALL):
            w_ref, m_ref, v_ref = ins[3 * i:3 * i + 3]
            d, mn, vn = _adamw(w_ref[...], grads[n], m_ref[...], v_ref[...])
            for o_ref, val in zip(outs[4 * i:4 * i + 4], (grads[n], d, mn, vn)):
                o_ref[...] = val
        outs[-1][...] = g[32:33, H:H + 1]

    shapes = [jax.ShapeDtypeStruct(params[n][0].shape, F32) for n in SMALL for _ in range(4)]
    res = pl.pallas_call(
        body, name="adamw_small", out_shape=shapes + [jax.ShapeDtypeStruct((1, 1), F32)],
    )(own, land, *flat)
    return {n: res[4 * i:4 * i + 4] for i, n in enumerate(SMALL)}, res[-1]


def kernel(x, meta, norm_g, w_in, b_f, conv_w, attn_norm_g, conv_norm_g, w_out, final_norm_g, loss_target, m_meta, m_norm_g, m_w_in, m_b_f, m_conv_w, m_attn_norm_g, m_conv_norm_g, m_w_out, m_final_norm_g, v_meta, v_norm_g, v_w_in, v_b_f, v_conv_w, v_attn_norm_g, v_conv_norm_g, v_w_out, v_final_norm_g):
    seq = x.shape[1]
    L = seq + TB
    assert x.shape == (1, seq, D) and L % TT == 0 and w_in.shape == (1, D, WSH)
    x2 = x[0]
    tgt = loss_target[0]

    w_in_slab = jnp.pad(w_in[0].T, ((0, WSHP - WSH), (0, 0))).astype(BF16)
    w_out_slab = w_out[0].astype(BF16)
    meta_slab = jnp.concatenate([meta, _tile_rows(conv_w[0], 3, DH)], axis=0)
    wout_flight = _split_start(w_out_slab, "gather_w_out_start", per_peer=False)
    w_all, small_all = _all_gather([w_in_slab, meta_slab], "gather_w_in")

    w_t, meta_full, cw_b, ga_b, gcn_b = _unshard_w_in(w_all, small_all, attn_norm_g, conv_norm_g)

    u, proj_t, gate_t, f_t, ktok, vtok = _inproj_fwd(x2, meta_full, norm_g, w_t, L, after=wout_flight[4])
    cq, kaug, sg = _fgate_fwd(f_t, b_f.reshape(H, 1), ktok, L)
    o_t, lse = _attn_fwd(proj_t, kaug, cq, L)

    w_out_own, w_out_land = _split_wait(wout_flight, o_t, "gather_w_out_wait", per_peer=False)
    w_out_full = _unshard_w_out(w_out_own, w_out_land)
    dout, dw_out, loss_part, dg_final, do_t, dd, dg5_t, dga_p, dgc_p, dcw_p = _gate_outproj(
        o_t, gate_t, cw_b, ga_b, gcn_b, w_out_full, x2, meta_full, final_norm_g.reshape(1, D), tgt, L)
    dwo_flight = _split_start(dw_out.reshape(NDEV, D // NDEV, D), "exchange_dw_out_start", per_peer=True)
    dq_t, dk_t, dv_t, dck, dcq = _attn_bwd(proj_t, kaug, vtok, do_t, lse, dd, cq, L, after=dwo_flight[4])
    df_t, db_f = _fgate_bwd(dcq, dck, sg, L)
    dw_main, dw_f = _inproj_bwd_w(u, dq_t, dk_t, dv_t, dg5_t, df_t, L)
    dwi_send = _shard_w_in_grads(dw_main, dw_f, False, "shard_w_in_grads_sibling")
    pair_flight = _split_start(dwi_send, "exchange_dw_in_pair_start", per_peer=False, chips="pair")
    dwi_keep = _shard_w_in_grads(dw_main, dw_f, True, "shard_w_in_grads_own", after=pair_flight[4])
    _, pair_land = _split_wait(pair_flight, dwi_keep, "exchange_dw_in_pair_wait", per_peer=False, chips="pair")
    dwi_chip = _pair_sum(dwi_keep, pair_land)
    dwi_flight = _split_start(dwi_chip, "exchange_dw_in_start", per_peer=True, chips=True)
    grad_x, dmeta, dg_norm = _inproj_bwd_x(
        w_t, dq_t, dk_t, dv_t, dg5_t, df_t, dout, x2, meta_full, norm_g, L, after=dwi_flight[4])
    small_parts = _pack_small_grads(dg_norm, dg_final, dga_p, dgc_p, dcw_p, db_f, dmeta, loss_part)
    small_flight = _split_start(small_parts, "exchange_small_start", per_peer=True)
    dwo_own, dwo_land = _split_wait(dwo_flight, small_flight[4], "exchange_dw_out_wait", per_peer=True)
    dwi_own, dwi_land = _split_wait(dwi_flight, dwo_land, "exchange_dw_in_wait", per_peer=True, chips=True)

    big_out = _adamw_big(dwi_own, dwi_land, dwo_own, dwo_land,
                         w_in[0].T, m_w_in[0].T, v_w_in[0].T, w_out, m_w_out, v_w_out)
    g_w_in, d_w_in, nm_w_in, nv_w_in = [a.T[None] for a in big_out[:4]]
    g_w_out, d_w_out, nm_w_out, nv_w_out = big_out[4:]
    sm_own, sm_land = _split_wait(small_flight, big_out[4], "exchange_small_wait", per_peer=True)
    line = lambda a: a.reshape(1, D)
    small, loss = _adamw_small(sm_own, sm_land, dict(
        norm_g=(norm_g, m_norm_g, v_norm_g),
        final_norm_g=(line(final_norm_g), line(m_final_norm_g), line(v_final_norm_g)),
        attn_norm_g=(attn_norm_g, m_attn_norm_g, v_attn_norm_g),
        conv_norm_g=(conv_norm_g, m_conv_norm_g, v_conv_norm_g),
        b_f=(b_f, m_b_f, v_b_f), meta=(meta, m_meta, v_meta), conv_w=(conv_w, m_conv_w, v_conv_w)))
    small["final_norm_g"] = [a.reshape(D) for a in small["final_norm_g"]]
    order = ("meta", "norm_g", "w_in", "b_f", "conv_w", "attn_norm_g", "conv_norm_g", "w_out", "final_norm_g")
    groups = []
    for k, (wi, wo) in enumerate(((g_w_in, g_w_out), (d_w_in, d_w_out), (nm_w_in, nm_w_out), (nv_w_in, nv_w_out))):
        d = dict({n: small[n][k] for n in SMALL}, w_in=wi, w_out=wo)
        groups.append([d[n] for n in order])
    return (loss[0, 0], grad_x[None], *groups[0], *groups[1], *groups[2], *groups[3])
```

```python
import jax
import jax.numpy as jnp
from jax import lax
from jax.experimental import pallas as pl
from jax.experimental.pallas import tpu as pltpu

F32 = jnp.float32
BF16 = jnp.bfloat16

D = 1024
DA = 512
H = 8
DH = 64
NM = 16
TB = 128
P0 = TB - NM
TT = 3 * TB
HG = 8
NDEV = 8
NSEC = 8
DF = 16
DPROJ = NSEC * DA + DF
WSH = 513
WSHP = 528
WROWS = WSHP + D // NDEV
SROWS = 64
EPS = 1e-6
NEG = -1e30
LOG2E = 1.4426950408889634
LN2 = 0.6931471805599453
QSCALE = DH ** -0.5 * LOG2E
KA = 128
CB = 256
VMEM_LIMIT = 56 * 1024 * 1024

ADAM_LR = 0.001
ADAM_B1 = 0.9
ADAM_B2 = 0.999
ADAM_EPS = 1e-08
ADAM_WD = 0.01
ADAM_STEP = 10

NT_DIMS = (((1,), (1,)), ((), ()))
TN_DIMS = (((0,), (0,)), ((), ()))
MESH = pl.DeviceIdType.MESH


def _params(n_axes=1, vmem=VMEM_LIMIT):
    return pltpu.CompilerParams(dimension_semantics=("arbitrary",) * n_axes, vmem_limit_bytes=vmem)


def _dot(a, b, dims=None):
    if dims is None:
        return jnp.dot(a, b, preferred_element_type=F32)
    return lax.dot_general(a, b, dims, preferred_element_type=F32)


def _my_place():
    return lax.axis_index("x"), lax.axis_index("y"), lax.axis_index("c")


def _all_gather(xs, name):
    n = len(xs)

    def body(*refs):
        x_refs, out_refs = refs[:n], refs[n:2 * n]
        send_sems, recv_sems, local_sems = refs[2 * n:]
        mx, my, mc = _my_place()

        def across(px, py, pc, axis_a):
            flip_x = pc if axis_a else 1 - pc
            return (px + flip_x) % 2, (py + 1 - flip_x) % 2, pc

        def idx(p):
            return 4 * p[0] + 2 * p[1] + p[2]

        me, sib = (mx, my, mc), (mx, my, 1 - mc)
        a_nbr, b_nbr = across(*me, True), across(*me, False)
        diag = across(*b_nbr, True)
        sib_a, sib_b = across(*sib, True), across(*sib, False)
        sib_diag = across(*sib_b, True)

        waits = []
        for t in range(n):
            out_ref = out_refs[t]

            def copy(k, block, to, src=None, out_ref=out_ref, t=t):
                return pltpu.make_async_remote_copy(
                    src_ref=out_ref.at[idx(block)] if src is None else src, dst_ref=out_ref.at[idx(block)],
                    send_sem=send_sems.at[7 * t + k], recv_sem=recv_sems.at[7 * t + k],
                    device_id=to, device_id_type=MESH)

            mine = pltpu.make_async_copy(x_refs[t], out_ref.at[idx(me)], local_sems.at[t])
            mine.start()
            started = [copy(0, me, sib, src=x_refs[t]), copy(1, me, a_nbr, src=x_refs[t]),
                       copy(2, me, b_nbr, src=x_refs[t])]
            for cp in started:
                cp.start()
            waits.append((copy, mine, started))
        relays = ((1, a_nbr, ((3, b_nbr), (4, sib))), (2, b_nbr, ((5, sib),)), (3, diag, ((6, sib),)))
        for landed, block, onward in relays:
            for copy, _, started in waits:
                copy(landed, block, me).wait_recv()
                for k, to in onward:
                    started.append(copy(k, block, to))
                    started[-1].start()
        for copy, mine, started in waits:
            for k, block in ((0, sib), (4, sib_a), (5, sib_b), (6, sib_diag)):
                copy(k, block, me).wait_recv()
            for cp in started:
                cp.wait_send()
            mine.wait()

    any_spec = pl.BlockSpec(memory_space=pl.ANY)
    return pl.pallas_call(
        body, name=name,
        out_shape=[jax.ShapeDtypeStruct((NDEV,) + x.shape, x.dtype) for x in xs],
        in_specs=[any_spec] * n, out_specs=[any_spec] * n,
        scratch_shapes=[pltpu.SemaphoreType.DMA((7 * n,)), pltpu.SemaphoreType.DMA((7 * n,)),
                        pltpu.SemaphoreType.DMA((n,))],
    )(*xs)


_HBM = pl.BlockSpec(memory_space=pltpu.HBM)
_UNREAD = pl.BlockSpec(memory_space=pl.ANY)
_SEM = pl.BlockSpec(memory_space=pltpu.SEMAPHORE)
_EFFECT = pltpu.SideEffectType.DATAFLOW_SIDE_EFFECTING


def _peer_of(m, place):
    mx, my, mc = place
    return ((1 - mx) if m & 4 else mx, (1 - my) if m & 2 else my, (1 - mc) if m & 1 else mc)


def _party(chips):
    if chips:
        return (lambda p: 2 * p[0] + p[1]), (2, 4, 6)
    return (lambda p: 4 * p[0] + 2 * p[1] + p[2]), tuple(range(1, NDEV))


def _split_copies(src_ref, land_ref, send_sems, recv_sems, per_peer, incoming, chips):
    place = _my_place()
    slot, masks = _party(chips)
    me = slot(place)
    out = []
    for k, m in enumerate(masks):
        there = _peer_of(m, place)
        peer = slot(there)
        src = (src_ref.at[me] if incoming else src_ref.at[peer]) if per_peer else src_ref
        out.append(pltpu.make_async_remote_copy(
            src_ref=src, dst_ref=land_ref.at[peer if incoming else me],
            send_sem=send_sems.at[k], recv_sem=recv_sems.at[k], device_id=there, device_id_type=MESH))
    return out


def _split_start(src, name, per_peer, chips=False):
    slab = src.shape[1:] if per_peer else src.shape
    n = len(_party(chips)[1])

    def body(src_ref, land_ref, send_sems, recv_sems, src_thru, land_thru, token):
        for cp in _split_copies(src_ref, land_ref, send_sems, recv_sems, per_peer, False, chips):
            cp.start()
        token[...] = jnp.zeros_like(token)

    return pl.pallas_call(
        body, name=name,
        out_shape=(pltpu.SemaphoreType.DMA((n,)), pltpu.SemaphoreType.DMA((n,)),
                   pltpu.HBM(src.shape, src.dtype), pltpu.HBM((n + 1,) + slab, src.dtype),
                   jax.ShapeDtypeStruct((8, TB), F32)),
        in_specs=(_HBM, _HBM), out_specs=(_SEM, _SEM, _HBM, _HBM, pl.BlockSpec(memory_space=pltpu.VMEM)),
        input_output_aliases={0: 2, 1: 3},
        compiler_params=pltpu.CompilerParams(has_side_effects=_EFFECT),
    )(pltpu.with_memory_space_constraint(src, pltpu.HBM),
      pltpu.with_memory_space_constraint(lax.empty((n + 1,) + slab, src.dtype), pltpu.HBM))


def _split_wait(handles, after, name, per_peer, chips=False):
    send_sems, recv_sems, src_thru, land_thru, _ = handles

    def body(src_ref, land_ref, send_sems, recv_sems, after_ref, src_out, land_out):
        for cp in _split_copies(src_ref, land_ref, send_sems, recv_sems, per_peer, False, chips):
            cp.wait_send()
        for cp in _split_copies(src_ref, land_ref, send_sems, recv_sems, per_peer, True, chips):
            cp.wait_recv()

    return pl.pallas_call(
        body, name=name,
        out_shape=(pltpu.HBM(src_thru.shape, src_thru.dtype), pltpu.HBM(land_thru.shape, land_thru.dtype)),
        in_specs=(_HBM, _HBM, _SEM, _SEM, pl.BlockSpec(memory_space=pl.ANY)), out_specs=(_HBM, _HBM),
        input_output_aliases={0: 0, 1: 1},
        compiler_params=pltpu.CompilerParams(has_side_effects=_EFFECT),
    )(src_thru, land_thru, send_sems, recv_sems, after)


def _pick_slab(j, own_ref, land_ref, rows, per_peer=True, chips=False):
    me = _party(chips)[0](_my_place())
    own = (lambda: own_ref[j, rows, :]) if per_peer else (lambda: own_ref[rows, :])
    return lax.cond(me == j, own, lambda: land_ref[j, rows, :])


def _pair_exchange(p, name):
    def body(p_ref, got_ref, send_sems, recv_sems):
        mx, my, mc = _my_place()
        copies = [pltpu.make_async_remote_copy(
            src_ref=p_ref.at[4 + q], dst_ref=got_ref.at[q], send_sem=send_sems.at[q],
            recv_sem=recv_sems.at[q], device_id=(mx, my, 1 - mc), device_id_type=MESH) for q in range(4)]
        for cp in copies:
            cp.start()
        for cp in copies:
            cp.wait_recv()
        for cp in copies:
            cp.wait_send()

    any_spec = pl.BlockSpec(memory_space=pl.ANY)
    return pl.pallas_call(
        body, name=name, out_shape=jax.ShapeDtypeStruct((4,) + p.shape[1:], p.dtype),
        in_specs=[any_spec], out_specs=any_spec,
        scratch_shapes=[pltpu.SemaphoreType.DMA((4,)), pltpu.SemaphoreType.DMA((4,))],
    )(p)


def _pair_sum(p, got):
    rows = p.shape[1]

    def body(p_ref, got_ref, out_ref):
        for q in range(4):
            out_ref[q] = (p_ref[q].astype(F32) + got_ref[q].astype(F32)).astype(BF16)

    blk = lambda n: pl.BlockSpec((n, rows, CB), lambda i: (0, 0, i))
    return pl.pallas_call(
        body, name="pair_sum", grid=(D // CB,), in_specs=[blk(4), blk(4)], out_specs=blk(4),
        out_shape=jax.ShapeDtypeStruct((4, rows, D), BF16), compiler_params=_params(),
    )(p, got)


def _h_block(t, x_ref, meta_ref):
    first = jnp.concatenate([jnp.zeros((P0, D), F32), meta_ref[...]], axis=0)
    return jnp.where(t == 0, first, x_ref[...])


def _x_specs3(tile=lambda j: j):
    return [pl.BlockSpec((TB, D), lambda j: (jnp.maximum(3 * tile(j) - 1, 0), 0)),
            pl.BlockSpec((TB, D), lambda j: (3 * tile(j), 0)),
            pl.BlockSpec((TB, D), lambda j: (3 * tile(j) + 1, 0))]


def _h_tile(j, xa_ref, xb_ref, xc_ref, meta_ref):
    first = jnp.concatenate([jnp.zeros((P0, D), F32), meta_ref[...]], axis=0)
    return jnp.concatenate([jnp.where(j == 0, first, xa_ref[...]), xb_ref[...], xc_ref[...]], axis=0)


def _full_spec(shape):
    return pl.BlockSpec(shape, lambda *_: (0,) * len(shape))


def _sigmoid(z):
    return 1.0 / (1.0 + jnp.exp(-z))


def _lane_tiles_sum(x):
    out = x[:, :TB]
    for i in range(1, x.shape[1] // TB):
        out = out + x[:, i * TB:(i + 1) * TB]
    return out


def _inproj_fwd(x, meta_full, norm_g, w_t, L, after):
    nj = L // TT

    def body(xa_ref, xb_ref, xc_ref, meta_ref, g_ref, w_ref, _, u_ref, proj_ref, gate_ref, f_ref, ktok_ref, vtok_ref):
        hb = _h_tile(pl.program_id(0), xa_ref, xb_ref, xc_ref, meta_ref)
        r = lax.rsqrt(jnp.mean(hb * hb, axis=-1, keepdims=True) + EPS)
        u = (hb * r * g_ref[...]).astype(BF16)
        u_ref[...] = u
        for s in range(NSEC):
            p = _dot(u, w_ref[s * DA:(s + 1) * DA, :], NT_DIMS)
            if s == 0:
                p = p * QSCALE
            if s in (1, 2):
                tok_ref = ktok_ref if s == 1 else vtok_ref
                for h in range(H):
                    tok_ref[h] = p[:, h * DH:(h + 1) * DH].astype(BF16)
            out_ref, s_out = (proj_ref, s) if s < 3 else (gate_ref, s - 3)
            out_ref[s_out * DA:(s_out + 1) * DA, :] = p.T.astype(BF16)
        f_ref[...] = _dot(w_ref[NSEC * DA:DPROJ, :], u, NT_DIMS)[:H]

    return pl.pallas_call(
        body, name="inproj_fwd", grid=(nj,),
        in_specs=_x_specs3() + [_full_spec((NM, D)), _full_spec((1, D)), _full_spec((DPROJ, D)), _UNREAD],
        out_specs=[
            pl.BlockSpec((TT, D), lambda t: (t, 0)),
            pl.BlockSpec((3 * DA, TT), lambda t: (0, t)),
            pl.BlockSpec((None, (NSEC - 3) * DA, TT), lambda t: (t, 0, 0)),
            pl.BlockSpec((H, TT), lambda t: (0, t)),
            pl.BlockSpec((H, TT, DH), lambda t: (0, t, 0)),
            pl.BlockSpec((H, TT, DH), lambda t: (0, t, 0)),
        ],
        out_shape=[
            jax.ShapeDtypeStruct((L, D), BF16),
            jax.ShapeDtypeStruct((3 * DA, L), BF16),
            jax.ShapeDtypeStruct((nj, (NSEC - 3) * DA, TT), BF16),
            jax.ShapeDtypeStruct((H, L), F32),
            jax.ShapeDtypeStruct((H, L, DH), BF16),
            jax.ShapeDtypeStruct((H, L, DH), BF16),
        ],
        compiler_params=_params(),
    )(x, x, x, meta_full, norm_g, w_t, after)


def _split3(x):
    hi = x.astype(BF16).astype(F32)
    r = x - hi
    mid = r.astype(BF16).astype(F32)
    return hi, mid, (r - mid).astype(BF16).astype(F32)


def _bias_rows(bias):
    one = jnp.ones((1, TT), F32)
    zero = jnp.zeros((1, TT), F32)
    parts = [zero] * 3 if bias is None else list(_split3(bias))
    return jnp.concatenate([one] * 3 + parts + [zero] * (DF - 6), axis=0).astype(BF16)


def _fgate_fwd(f_t, b_col, ktok, L):
    nb = L // TB

    def body(f_ref, b_ref, ktok_ref, cq_ref, kaug_ref, sg_ref, bias_scr):
        h = pl.program_id(0)

        @pl.when(h == 0)
        def _():
            z = f_ref[...] + b_ref[...]
            idx = lax.broadcasted_iota(jnp.int32, (H, L), 1)
            real = idx >= P0
            lf = jnp.where(real, jnp.minimum(z, 0.0) - jnp.log1p(jnp.exp(-jnp.abs(z))), 0.0)
            sg_ref[...] = jnp.where(real, 1.0 / (1.0 + jnp.exp(z)), 0.0)
            c = lf
            s = 1
            while s < L:
                c = c + jnp.where(idx >= s, pltpu.roll(c, s, 1), 0.0)
                s *= 2
            c = c * LOG2E
            for hh in range(H):
                cq_ref[hh] = c[hh:hh + 1, :]
            for part, val in enumerate(_split3(-jnp.where(real, c, -NEG))):
                for hh in range(H):
                    bias_scr[part * H + hh] = val[hh:hh + 1, :]

        lane = lax.broadcasted_iota(jnp.int32, (TB, KA), 1)
        head = jnp.zeros((DH, TB), F32)
        tail = jnp.concatenate([jnp.ones((3, TB), F32), jnp.zeros((KA - DH - 6, TB), F32)], axis=0)
        for b in range(nb):
            blk = slice(b * TB, (b + 1) * TB)
            cols = jnp.concatenate(
                [head] + [bias_scr[part * H + h, :, blk] for part in range(3)] + [tail], axis=0).T
            k = jnp.concatenate([ktok_ref[0, blk, :].astype(F32), jnp.zeros((TB, KA - DH), F32)], axis=1)
            kaug_ref[0, blk, :] = jnp.where(lane < DH, k, cols).astype(BF16)

    return pl.pallas_call(
        body, name="fgate_fwd", grid=(H,),
        in_specs=[_full_spec((H, L)), _full_spec((H, 1)), pl.BlockSpec((1, L, DH), lambda h: (h, 0, 0))],
        out_specs=[_full_spec((H, 1, L)), pl.BlockSpec((1, L, KA), lambda h: (h, 0, 0)), _full_spec((H, L))],
        out_shape=[
            jax.ShapeDtypeStruct((H, 1, L), F32),
            jax.ShapeDtypeStruct((H, L, KA), BF16),
            jax.ShapeDtypeStruct((H, L), F32),
        ],
        scratch_shapes=[pltpu.VMEM((3 * H, 1, L), F32)],
        compiler_params=_params(),
    )(f_t, b_col, ktok)


def _causal_mask():
    r = lax.broadcasted_iota(jnp.int32, (TT, TT), 0)
    c = lax.broadcasted_iota(jnp.int32, (TT, TT), 1)
    return r <= c


def _attn_fwd(proj_t, kaug, cq, L):
    nq = L // TT

    def body(q_ref, qn_ref, kaug_ref, v_ref, cq_ref, o_ref, lse_ref,
             qa_scr, s_scr, cmax_scr, m_scr, p_scr, alpha_scr, acc_scr):
        j = pl.program_id(0)
        rows = [slice(g * DH, (g + 1) * DH) for g in range(HG)]
        ones = jnp.ones((DF, TT), BF16)

        def load_queries(ref):
            for g in range(HG):
                qa_scr[g] = jnp.concatenate(
                    [ref[rows[g], :], _bias_rows(None), jnp.zeros((KA - DH - DF, TT), BF16)], axis=0)

        def scores(kt, masked):
            k_off = pl.multiple_of(kt * TT, TT)
            for g in range(HG):
                s = _dot(kaug_ref[g, pl.ds(k_off, TT), :], qa_scr[g])
                if masked:
                    s = jnp.where(_causal_mask(), s, NEG)
                s_scr[g] = s
                cmax_scr[g] = jnp.max(s, axis=0, keepdims=True)

        def softmax():
            for g in range(HG):
                m_old = m_scr[g]
                m_new = jnp.maximum(m_old, cmax_scr[g])
                alpha_scr[g] = jnp.exp2(m_old - m_new)
                p_scr[g] = jnp.exp2(s_scr[g] - m_new).astype(BF16)
                m_scr[g] = m_new

        def weighted_sum(kt):
            k_off = pl.multiple_of(kt * TT, TT)
            for g in range(HG):
                v1 = jnp.concatenate([v_ref[rows[g], pl.ds(k_off, TT)], ones], axis=0)
                acc_scr[g] = alpha_scr[g] * acc_scr[g] + _dot(v1, p_scr[g])

        @pl.when(j == 0)
        def _():
            load_queries(q_ref)
            scores(0, True)

        m_scr[...] = jnp.full_like(m_scr, NEG)
        acc_scr[...] = jnp.zeros_like(acc_scr)

        @pl.when(j >= 1)
        def _():
            softmax()
            scores(j - 1, False)

        def step(i, c):
            weighted_sum(j - i + 1)
            softmax()
            scores(j - i - 1, False)
            return c

        lax.fori_loop(1, j, step, 0)

        def drain(second_last, next_tile):
            if second_last:
                weighted_sum(1)
            softmax()
            if next_tile:
                load_queries(qn_ref)
                scores(j + 1, True)
            weighted_sum(0)

        @pl.when(j == 0)
        def _():
            drain(False, nq > 1)

        @pl.when((j >= 1) & (j < nq - 1))
        def _():
            drain(True, True)

        @pl.when((j >= 1) & (j == nq - 1))
        def _():
            drain(True, False)

        for g in range(HG):
            l = acc_scr[g, DH:DH + 1, :]
            o_ref[rows[g], :] = acc_scr[g, :DH, :] * (1.0 / l)
            lse_ref[g] = m_scr[g] + jnp.log2(l) + cq_ref[g]

    assert HG == H
    return pl.pallas_call(
        body, name="attn_fwd", grid=(nq,),
        in_specs=[
            pl.BlockSpec((DA, TT), lambda j: (0, j)),
            pl.BlockSpec((DA, TT), lambda j: (0, jnp.minimum(j + 1, nq - 1))),
            pl.BlockSpec((H, L, KA), lambda j: (0, 0, 0)),
            pl.BlockSpec((DA, L), lambda j: (2, 0)),
            pl.BlockSpec((H, 1, TT), lambda j: (0, 0, j)),
        ],
        out_specs=[
            pl.BlockSpec((None, DA, TT), lambda j: (j, 0, 0)),
            pl.BlockSpec((H, 1, TT), lambda j: (0, 0, j)),
        ],
        out_shape=[jax.ShapeDtypeStruct((nq, DA, TT), F32), jax.ShapeDtypeStruct((H, 1, L), F32)],
        scratch_shapes=[pltpu.VMEM((HG, KA, TT), BF16), pltpu.VMEM((HG, TT, TT), F32), pltpu.VMEM((HG, 1, TT), F32),
                        pltpu.VMEM((HG, 1, TT), F32), pltpu.VMEM((HG, TT, TT), BF16), pltpu.VMEM((HG, 1, TT), F32),
                        pltpu.VMEM((HG, DH + DF, TT), F32)],
        compiler_params=_params(),
    )(proj_t, proj_t, kaug, proj_t, cq)


def _gate_group(rows, o_ref, za_ref, gb_ref, gc_ref, xc_ref, zc_ref, gcp_ref, xcp_ref, cw_ref, ga_ref, gcn_ref, first):
    n_rep = TT // TB
    f32 = lambda r: r[rows, :].astype(F32)
    o, za, gb, gc, xc, zc = o_ref[rows, :], f32(za_ref), f32(gb_ref), f32(gc_ref), f32(xc_ref), f32(zc_ref)
    a = gc * xc
    a_prev = jnp.where(first, 0.0, f32(gcp_ref) * f32(xcp_ref))
    full = jnp.concatenate([a_prev, a], axis=1)
    a1 = pltpu.roll(full, 1, 1)[:, TB:]
    a2 = pltpu.roll(full, 2, 1)[:, TB:]
    w0 = jnp.tile(cw_ref[0, rows, :], (1, n_rep))
    w1 = jnp.tile(cw_ref[1, rows, :], (1, n_rep))
    w2 = jnp.tile(cw_ref[2, rows, :], (1, n_rep))
    cv = w0 * a2 + w1 * a1 + w2 * a
    e = gb * cv
    rc = lax.rsqrt(jnp.mean(e * e, axis=0, keepdims=True) + EPS)
    ec = e * rc
    ra = lax.rsqrt(jnp.mean(o * o, axis=0, keepdims=True) + EPS)
    oa = o * ra
    g_a = jnp.tile(ga_ref[rows, :], (1, n_rep))
    g_c = jnp.tile(gcn_ref[rows, :], (1, n_rep))
    sa = _sigmoid(za)
    sc = _sigmoid(zc)
    return dict(o=o, za=za, gb=gb, gc=gc, xc=xc, zc=zc, a=a, a1=a1, a2=a2, w0=w0, w1=w1, w2=w2, cv=cv, e=e,
                rc=rc, ec=ec, ra=ra, oa=oa, g_a=g_a, g_c=g_c, sa=sa, sc=sc)


def _gate_specs(tile):
    halo = pl.BlockSpec((None, 2 * DA, TB),
                        lambda i: (jnp.maximum(tile(i) - 1, 0), 1, TT // TB - 1))
    return [pl.BlockSpec((None, DA, TT), lambda i: (tile(i), 0, 0)),
            pl.BlockSpec((None, 5 * DA, TT), lambda i: (tile(i), 0, 0)), halo,
            _full_spec((3, DA, TB)), _full_spec((DA, TB)), _full_spec((DA, TB))]


def _gate_views(g5_ref, halo_ref):
    return [g5_ref.at[pl.ds(s * DA, DA)] for s in range(5)] + [halo_ref.at[pl.ds(s * DA, DA)] for s in range(2)]


def _gate_outproj(o_t, gate_t, cw_b, ga_b, gcn_b, w_out, x, meta_full, fng, target, L):
    nj = L // TT
    rp = NM
    n_bwd = 8
    cb = D // 4
    n_ring = 5
    assert P0 % rp == 0 and TB % rp == 0 and (TT // rp) % n_bwd == 0 and H == n_bwd

    def body(o_hbm, g5_hbm, halo_ref, cw_ref, ga_ref, gcn_ref, halo2_ref,
             w_ref, xa_ref, xb_ref, xc_ref, meta_ref, g_ref, ta_ref, tb_ref, tc_ref,
             dout_ref, dwb_ref, loss_ref, dg_ref, do_ref, dd_ref, dg5_ref, dga_ref, dgc_ref, dcw_ref,
             dw_ref, o_scr, db_new, db_old, mix_new, mix_old, dmix_new, dmix_old, sq_acc, dg_acc, carry_ref,
             o_ring, g5_ring, ring_sems):
        t = pl.program_id(0)

        def fetch(step):
            tile, slot = nj - 1 - step, step % n_ring
            return (pltpu.make_async_copy(o_hbm.at[tile], o_ring.at[slot], ring_sems.at[0, slot]),
                    pltpu.make_async_copy(g5_hbm.at[tile], g5_ring.at[slot], ring_sems.at[1, slot]))

        @pl.when(t == 0)
        def _():
            for step in range(2):
                for cp in fetch(step):
                    cp.start()

        @pl.when(t + 2 < nj)
        def _():
            for cp in fetch(t + 2):
                cp.start()

        @pl.when(t < nj)
        def _():
            for cp in fetch(t):
                cp.wait()

        slot_a, slot_c = t % n_ring, (t + n_ring - 2) % n_ring
        o_ref, o2_ref = o_ring.at[slot_a], o_ring.at[slot_c]
        za_ref, gb_ref, gc_ref, xcv_ref, zc_ref, gcp_ref, xcp_ref = _gate_views(g5_ring.at[slot_a], halo_ref)
        za2_ref, gb2_ref, gc2_ref, xcv2_ref, zc2_ref, gcp2_ref, xcp2_ref = _gate_views(g5_ring.at[slot_c], halo2_ref)
        first_a = t == nj - 1
        first_c = t == nj + 1

        def gate_rows(h):
            rows = slice(h * DH, (h + 1) * DH)
            g = _gate_group(rows, o_ref, za_ref, gb_ref, gc_ref, xcv_ref, zc_ref, gcp_ref, xcp_ref,
                            cw_ref, ga_ref, gcn_ref, first_a)
            mix_new[rows, :] = (g["oa"] * g["g_a"] * (g["za"] * g["sa"])).astype(BF16)
            mix_new[DA + h * DH:DA + (h + 1) * DH, :] = (g["ec"] * g["g_c"] * (g["zc"] * g["sc"])).astype(BF16)

        def loss_rows(c):
            blk = c // (TB // rp)
            rows, out_rows = pl.ds((c % (TB // rp)) * rp, rp), pl.ds(c * rp, rp)
            h = (xa_ref, xb_ref, xc_ref)[blk][rows, :]
            if blk == 0:
                first = meta_ref[...] if c == P0 // rp else jnp.zeros((rp, D), F32)
                h = jnp.where(first_a, first, h)
            o = o_scr[out_rows, :] + h
            r = lax.rsqrt(jnp.mean(o * o, axis=-1, keepdims=True) + EPS)
            orn = o * r
            g = g_ref[...]
            diff = orn * g - (ta_ref, tb_ref, tc_ref)[blk][rows, :]
            if blk == 0:
                diff = diff * jnp.where(first_a, 0.0, 1.0)
            gy = diff * (g * (1.0 / D))
            dout = r * (gy - orn * jnp.mean(gy * orn, axis=-1, keepdims=True))
            dout_ref[out_rows, :] = dout
            db_new[out_rows, :] = dout.astype(BF16)
            sq, go = diff * diff, diff * orn
            sq_acc[...] += sq[:8] + sq[8:]
            dg_acc[...] += go[:8] + go[8:]

        def backward_cols(n):
            if n < 4:
                cols = slice(n * cb, (n + 1) * cb)
                dmix_new[cols, :] = _dot(db_old[...], w_ref[cols, :], NT_DIMS).T.astype(BF16)
            else:
                cols = slice((n - 4) * cb, (n - 3) * cb)
                dw_ref[:, cols] += _dot(mix_old[...], db_old[:, cols])

        def gate_bwd_rows(h):
            rows = slice(h * DH, (h + 1) * DH)
            sec = lambda s: slice(s * DA + h * DH, s * DA + (h + 1) * DH)
            g = _gate_group(rows, o2_ref, za2_ref, gb2_ref, gc2_ref, xcv2_ref, zc2_ref, gcp2_ref, xcp2_ref,
                            cw_ref, ga_ref, gcn_ref, first_c)
            o, za, gb, gc, xc, zc, sa, sc = (g[n] for n in ("o", "za", "gb", "gc", "xc", "zc", "sa", "sc"))
            dya = dmix_old[rows, :].astype(F32)
            dyc = dmix_old[DA + h * DH:DA + (h + 1) * DH, :].astype(F32)

            dn = dya * (za * sa)
            dg5_ref[sec(0), :] = (dya * (g["oa"] * g["g_a"]) * (sa * (1.0 + za * (1.0 - sa)))).astype(BF16)
            dga_ref[rows, :] += _lane_tiles_sum(dn * g["oa"])
            dng = dn * g["g_a"]
            mean_a = jnp.mean(dng * g["oa"], axis=0, keepdims=True)
            do = (dng - g["oa"] * mean_a) * g["ra"]
            do_ref[rows, :] = do.astype(BF16)
            dd_ref[h] = jnp.sum(do * o, axis=0, keepdims=True)

            dnc = dyc * (zc * sc)
            dg5_ref[sec(4), :] = (dyc * (g["ec"] * g["g_c"]) * (sc * (1.0 + zc * (1.0 - sc)))).astype(BF16)
            dgc_ref[rows, :] += _lane_tiles_sum(dnc * g["ec"])
            dncg = dnc * g["g_c"]
            mean_c = jnp.mean(dncg * g["ec"], axis=0, keepdims=True)
            de = (dncg - g["ec"] * mean_c) * g["rc"]
            dg5_ref[sec(1), :] = (de * g["cv"]).astype(BF16)
            dcv = de * gb
            full = jnp.concatenate([dcv, carry_ref[rows, :]], axis=1)
            d1 = pltpu.roll(full, TT + TB - 1, 1)[:, :TT]
            d2 = pltpu.roll(full, TT + TB - 2, 1)[:, :TT]
            carry_ref[rows, :] = dcv[:, :TB]
            da = g["w2"] * dcv + g["w1"] * d1 + g["w0"] * d2
            dg5_ref[sec(2), :] = (da * xc).astype(BF16)
            dg5_ref[sec(3), :] = (da * gc).astype(BF16)
            dcw_ref[0, rows, :] += _lane_tiles_sum(dcv * g["a2"])
            dcw_ref[1, rows, :] += _lane_tiles_sum(dcv * g["a1"])
            dcw_ref[2, rows, :] += _lane_tiles_sum(dcv * g["a"])

        def step(a, b, c):
            half = H // 2
            for h in range(H):
                if a:
                    gate_rows(h)
                if c and h < half:
                    gate_bwd_rows(h)
                if b and h % 2 == 1:
                    backward_cols(h // 2)
            if a:
                o_scr[...] = _dot(mix_new[...], w_ref[...], TN_DIMS)
            per = TT // rp // n_bwd
            for k in range(n_bwd):
                if a:
                    for piece in range(per * k, per * (k + 1)):
                        loss_rows(piece)
                if c and k % 2 == 0:
                    gate_bwd_rows(half + k // 2)
                if b and k % 2 == 1:
                    backward_cols(n_bwd // 2 + k // 2)
            if a:
                db_old[...] = db_new[...]
                mix_old[...] = mix_new[...]
            if b:
                dmix_old[...] = dmix_new[...]

        @pl.when(t == 0)
        def _():
            dw_ref[...] = jnp.zeros_like(dw_ref)
            sq_acc[...] = jnp.zeros_like(sq_acc)
            dg_acc[...] = jnp.zeros_like(dg_acc)
            carry_ref[...] = jnp.zeros_like(carry_ref)
            dga_ref[...] = jnp.zeros_like(dga_ref)
            dgc_ref[...] = jnp.zeros_like(dgc_ref)
            dcw_ref[...] = jnp.zeros_like(dcw_ref)
            step(True, False, False)

        @pl.when(t == 1)
        def _():
            step(True, True, False)

        @pl.when((t >= 2) & (t < nj))
        def _():
            step(True, True, True)

        @pl.when(t == nj)
        def _():
            step(False, True, True)
            dwb_ref[...] = dw_ref[...].astype(BF16)
            loss_ref[...] = jnp.sum(sq_acc[...], keepdims=True) * (0.5 / D)
            dg_ref[...] = jnp.sum(dg_acc[...], axis=0, keepdims=True) * (1.0 / D)

        @pl.when(t == nj + 1)
        def _():
            step(False, False, True)

    assert nj >= 2
    tile_a = lambda t: jnp.clip(nj - 1 - t, 0, nj - 1)
    tile_c = lambda t: jnp.clip(nj + 1 - t, 0, nj - 1)
    at_c = lambda shape: pl.BlockSpec(shape, lambda t: (0,) * (len(shape) - 1) + (tile_c(t),))
    return pl.pallas_call(
        body, name="gate_outproj", grid=(nj + 2,),
        in_specs=[pl.BlockSpec(memory_space=pl.ANY)] * 2 + _gate_specs(tile_a)[2:] + _gate_specs(tile_c)[2:3]
                 + [_full_spec((D, D))] + _x_specs3(tile_a)
                 + [_full_spec((NM, D)), _full_spec((1, D))] + _x_specs3(tile_a),
        out_specs=[pl.BlockSpec((TT, D), lambda t: (tile_a(t), 0)), _full_spec((D, D)), _full_spec((1, 1)),
                   _full_spec((1, D)), at_c((DA, TT)), at_c((H, 1, TT)), at_c((5 * DA, TT)),
                   _full_spec((DA, TB)), _full_spec((DA, TB)), _full_spec((3, DA, TB))],
        out_shape=[jax.ShapeDtypeStruct((L, D), F32), jax.ShapeDtypeStruct((D, D), BF16),
                   jax.ShapeDtypeStruct((1, 1), F32), jax.ShapeDtypeStruct((1, D), F32),
                   jax.ShapeDtypeStruct((DA, L), BF16),
                   jax.ShapeDtypeStruct((H, 1, L), F32),
                   jax.ShapeDtypeStruct((5 * DA, L), BF16),
                   jax.ShapeDtypeStruct((DA, TB), F32),
                   jax.ShapeDtypeStruct((DA, TB), F32),
                   jax.ShapeDtypeStruct((3, DA, TB), F32)],
        scratch_shapes=[pltpu.VMEM((D, D), F32), pltpu.VMEM((TT, D), F32), pltpu.VMEM((TT, D), BF16),
                        pltpu.VMEM((TT, D), BF16), pltpu.VMEM((D, TT), BF16), pltpu.VMEM((D, TT), BF16),
                        pltpu.VMEM((D, TT), BF16), pltpu.VMEM((D, TT), BF16),
                        pltpu.VMEM((8, D), F32), pltpu.VMEM((8, D), F32), pltpu.VMEM((DA, TB), F32),
                        pltpu.VMEM((n_ring, DA, TT), F32), pltpu.VMEM((n_ring, 5 * DA, TT), BF16),
                        pltpu.SemaphoreType.DMA((2, n_ring))],
        compiler_params=_params(),
    )(o_t, gate_t, gate_t, cw_b, ga_b, gcn_b, gate_t,
      w_out, x, x, x, meta_full, fng, target, target, target)


def _attn_bwd(proj_t, kaug, vtok, do_t, lse, dd, cq, L, after):
    nk = L // TT

    def body(q_ref, kaug_ref, vtok_ref, kt_ref, do_ref, lse_ref, dd_ref, cq_ref, _,
             dq_ref, dk_ref, dv_ref, dck_ref, dcq_ref, dq_acc, kt1_scr, s_scr, dp_scr, dv_scr, dk_scr):
        i = pl.program_id(0)
        rows = [slice(g * DH, (g + 1) * DH) for g in range(HG)]
        ones = jnp.ones((DF, TT), BF16)
        zpad = jnp.zeros((KA - DH - DF, TT), BF16)
        for g in range(HG):
            kt1_scr[g] = jnp.concatenate([kt_ref[rows[g], :], ones], axis=0)
        dv_scr[...] = jnp.zeros_like(dv_scr)
        dk_scr[...] = jnp.zeros_like(dk_scr)

        def q_rows(g, q_off):
            bias = cq_ref[g, :, pl.ds(q_off, TT)] - lse_ref[g, :, pl.ds(q_off, TT)]
            return jnp.concatenate([q_ref[rows[g], pl.ds(q_off, TT)], _bias_rows(bias)], axis=0)

        def scores(jq, masked):
            q_off = pl.multiple_of(jq * TT, TT)
            for g in range(HG):
                s = _dot(kaug_ref[g], jnp.concatenate([q_rows(g, q_off), zpad], axis=0))
                if masked:
                    s = jnp.where(_causal_mask(), s, NEG)
                s_scr[g] = s
                dp_scr[g] = _dot(vtok_ref[g], do_ref[rows[g], pl.ds(q_off, TT)])

        def grads(jq):
            q_off = pl.multiple_of(jq * TT, TT)
            for g in range(HG):
                p = jnp.exp2(s_scr[g])
                ds = (p * (dp_scr[g] - dd_ref[g, :, pl.ds(q_off, TT)])).astype(BF16)
                do1 = jnp.concatenate([do_ref[rows[g], pl.ds(q_off, TT)], jnp.zeros((KA - DH, TT), BF16)], axis=0)
                q1 = jnp.concatenate([q_rows(g, q_off), zpad], axis=0)
                dv_scr[g] += _dot(p.astype(BF16), do1, NT_DIMS)
                dk_scr[g] += _dot(ds, q1, NT_DIMS)
                dq_acc[g, :, pl.ds(q_off, TT)] += _dot(kt1_scr[g], ds)

        @pl.when(i == 0)
        def _():
            dq_acc[...] = jnp.zeros_like(dq_acc)

        scores(i, True)

        def step(jq, c):
            grads(jq)
            scores(jq + 1, False)
            return c

        lax.fori_loop(i, nk - 1, step, 0)
        grads(nk - 1)
        for g in range(HG):
            dv_ref[rows[g], :] = dv_scr[g].T[:DH, :].astype(BF16)
            dk_t = dk_scr[g].T
            dk_ref[rows[g], :] = (dk_t[:DH, :] * LN2).astype(BF16)
            dck_ref[g] = dk_t[DH:DH + 1, :]

        @pl.when(i == nk - 1)
        def _():
            for g in range(HG):
                dq_ref[rows[g], :] = (dq_acc[g, :DH, :] * (DH ** -0.5)).astype(BF16)
                dcq_ref[g] = dq_acc[g, DH:DH + 1, :]

    assert HG == H
    head = lambda i: (0, 0)
    row = lambda i: (0, 0, 0)
    return pl.pallas_call(
        body, name="attn_bwd", grid=(nk,),
        in_specs=[
            pl.BlockSpec((DA, L), head),
            pl.BlockSpec((H, TT, KA), lambda i: (0, i, 0)),
            pl.BlockSpec((H, TT, DH), lambda i: (0, i, 0)),
            pl.BlockSpec((DA, TT), lambda i: (1, i)),
            pl.BlockSpec((DA, L), head),
            pl.BlockSpec((H, 1, L), row), pl.BlockSpec((H, 1, L), row), pl.BlockSpec((H, 1, L), row), _UNREAD,
        ],
        out_specs=[
            pl.BlockSpec((DA, L), head),
            pl.BlockSpec((DA, TT), lambda i: (0, i)),
            pl.BlockSpec((DA, TT), lambda i: (0, i)),
            pl.BlockSpec((H, 1, TT), lambda i: (0, 0, i)),
            pl.BlockSpec((H, 1, L), row),
        ],
        out_shape=[jax.ShapeDtypeStruct((DA, L), BF16), jax.ShapeDtypeStruct((DA, L), BF16),
                   jax.ShapeDtypeStruct((DA, L), BF16), jax.ShapeDtypeStruct((H, 1, L), F32),
                   jax.ShapeDtypeStruct((H, 1, L), F32)],
        scratch_shapes=[
            pltpu.VMEM((HG, DH + DF, L), F32),
            pltpu.VMEM((HG, DH + DF, TT), BF16),
            pltpu.VMEM((HG, TT, TT), F32), pltpu.VMEM((HG, TT, TT), F32),
            pltpu.VMEM((HG, TT, KA), F32), pltpu.VMEM((HG, TT, KA), F32)],
        compiler_params=_params(),
    )(proj_t, kaug, vtok, proj_t, do_t, lse, dd, cq, after)


def _fgate_bwd(dcq, dck, sg, L):
    def body(dcq_ref, dck_ref, sg_ref, df_ref, db_ref):
        dc = jnp.concatenate([dcq_ref[h] - dck_ref[h] for h in range(H)], axis=0)
        idx = lax.broadcasted_iota(jnp.int32, (H, L), 1)
        r = dc
        s = 1
        while s < L:
            r = r + jnp.where(idx + s < L, pltpu.roll(r, L - s, 1), 0.0)
            s *= 2
        df = r * sg_ref[...]
        db_ref[...] = jnp.broadcast_to(jnp.sum(df, axis=1, keepdims=True), (H, TB))
        df_ref[...] = jnp.concatenate([df, jnp.zeros((DF - H, L), F32)], axis=0).astype(BF16)

    return pl.pallas_call(
        body, name="fgate_bwd",
        out_shape=[jax.ShapeDtypeStruct((DF, L), BF16), jax.ShapeDtypeStruct((H, TB), F32)],
        compiler_params=pltpu.CompilerParams(vmem_limit_bytes=VMEM_LIMIT),
    )(dcq, dck, sg)


def _inproj_bwd_x(w, dq_t, dk_t, dv_t, dg5_t, df_t, dout, x, meta_full, norm_g, L, after):
    nj = L // TT
    seq = x.shape[0]

    def body(w_ref, dq_ref, dk_ref, dv_ref, dg5_ref, df_ref, dout_ref, xa_ref, xb_ref, xc_ref, meta_ref, g_ref, _,
             gx_ref, dmeta_ref, dg_ref, dh_scr, sems):
        j = pl.program_id(0)
        slot = j % 2

        def copy_out(step, slot_):
            first = pltpu.make_async_copy(dh_scr.at[slot_, pl.ds(TB, TT - TB)], gx_ref.at[pl.ds(0, TT - TB)],
                                          sems.at[slot_])
            later = pltpu.make_async_copy(dh_scr.at[slot_], gx_ref.at[pl.ds(step * TT - TB, TT)], sems.at[slot_])
            return first, later

        @pl.when(j == 0)
        def _():
            dg_ref[...] = jnp.zeros_like(dg_ref)

        du = _dot(dq_ref[...], w_ref[0:DA, :], TN_DIMS)
        du += _dot(dk_ref[...], w_ref[DA:2 * DA, :], TN_DIMS)
        du += _dot(dv_ref[...], w_ref[2 * DA:3 * DA, :], TN_DIMS)
        du += _dot(dg5_ref[...], w_ref[3 * DA:NSEC * DA, :], TN_DIMS)
        du += _dot(df_ref[...], w_ref[NSEC * DA:DPROJ, :], TN_DIMS)
        hb = _h_tile(j, xa_ref, xb_ref, xc_ref, meta_ref)
        r = lax.rsqrt(jnp.mean(hb * hb, axis=-1, keepdims=True) + EPS)
        hn = hb * r
        dg_ref[...] += jnp.sum(du * hn, axis=0, keepdims=True)
        gu = du * g_ref[...]
        dh = dout_ref[...] + r * gu - hn * (r * jnp.mean(gu * hn, axis=-1, keepdims=True))

        dh_scr[slot] = dh

        @pl.when(j == 0)
        def _():
            dmeta_ref[...] = dh[P0:TB, :]
            copy_out(0, 0)[0].start()

        @pl.when(j >= 1)
        def _():
            copy_out(j, slot)[1].start()

        @pl.when(j == 1)
        def _():
            copy_out(0, 0)[0].wait()

        @pl.when(j >= 2)
        def _():
            copy_out(j - 1, 1 - slot)[1].wait()

        @pl.when(j == nj - 1)
        def _():
            copy_out(j, slot)[0 if nj == 1 else 1].wait()

    blk = lambda rows: pl.BlockSpec((rows, TT), lambda j: (0, j))
    return pl.pallas_call(
        body, name="inproj_bwd_x", grid=(nj,),
        in_specs=[_full_spec((DPROJ, D)), blk(DA), blk(DA), blk(DA), blk(5 * DA), blk(DF),
                  pl.BlockSpec((TT, D), lambda j: (j, 0))] + _x_specs3()
                 + [_full_spec((NM, D)), _full_spec((1, D)), _UNREAD],
        out_specs=[pl.BlockSpec(memory_space=pl.ANY), _full_spec((NM, D)), _full_spec((1, D))],
        out_shape=[jax.ShapeDtypeStruct((seq, D), F32), jax.ShapeDtypeStruct((NM, D), F32),
                   jax.ShapeDtypeStruct((1, D), F32)],
        scratch_shapes=[pltpu.VMEM((2, TT, D), F32), pltpu.SemaphoreType.DMA((2,))],
        compiler_params=_params(),
    )(w, dq_t, dk_t, dv_t, dg5_t, df_t, dout, x, x, x, meta_full, norm_g, after)


def _inproj_bwd_w(u, dq_t, dk_t, dv_t, dg5_t, df_t, L):
    def body(u_ref, dq_hbm, dk_hbm, dv_hbm, dg5_ref, df_ref, dw_ref, dwf_ref, qkv_scr, sems):
        s = pl.program_id(0)
        u_all = u_ref[...]
        fetch = [pltpu.make_async_copy(src, qkv_scr.at[k], sems.at[k])
                 for k, src in enumerate((dq_hbm, dk_hbm, dv_hbm))]

        @pl.when(s == 0)
        def _():
            for cp in fetch:
                cp.start()

        @pl.when(s < 5)
        def _():
            dw_ref[...] = _dot(dg5_ref[...], u_all)

        for k in range(3):
            @pl.when(s == 5 + k)
            def _(k=k):
                fetch[k].wait()
                dw_ref[...] = _dot(qkv_scr[k], u_all)

        @pl.when(s == NSEC - 1)
        def _():
            dwf_ref[...] = _dot(df_ref[...], u_all)

    once = lambda shape: pl.BlockSpec(shape, lambda s: (0, 0), pipeline_mode=pl.Buffered(1))
    any_spec = pl.BlockSpec(memory_space=pl.ANY)
    return pl.pallas_call(
        body, name="inproj_bwd_w", grid=(NSEC,),
        in_specs=[
            once((L, D)), any_spec, any_spec, any_spec,
            pl.BlockSpec((DA, L), lambda s: (jnp.minimum(s, 4), 0)),
            once((DF, L)),
        ],
        out_specs=[pl.BlockSpec((DA, D), lambda s: (jnp.where(s < 5, s + 3, s - 5), 0)), _full_spec((DF, D))],
        out_shape=[jax.ShapeDtypeStruct((NSEC * DA, D), F32), jax.ShapeDtypeStruct((DF, D), F32)],
        scratch_shapes=[pltpu.VMEM((3, DA, L), BF16), pltpu.SemaphoreType.DMA((3,))],
        compiler_params=_params(),
    )(u, dq_t, dk_t, dv_t, dg5_t, df_t)


def _adamw(w, g, m, v):
    m = ADAM_B1 * m + (1.0 - ADAM_B1) * g
    v = ADAM_B2 * v + (1.0 - ADAM_B2) * (g * g)
    m_hat = m / (1.0 - ADAM_B1 ** ADAM_STEP)
    v_hat = v / (1.0 - ADAM_B2 ** ADAM_STEP)
    delta = -ADAM_LR * (m_hat / (jnp.sqrt(v_hat) + ADAM_EPS) + ADAM_WD * w)
    return delta, m, v


def _adamw_big(own_in, land_in, own_out, land_out, w_in_t, m_in_t, v_in_t, w_out, m_out, v_out):
    cb = CB
    e_sh = D // NDEV
    in_shape = jax.ShapeDtypeStruct(w_in_t.shape, F32)
    out_shape = jax.ShapeDtypeStruct(w_out.shape, F32)

    def total(own_ref, land_ref, rows, chips):
        g = _pick_slab(0, own_ref, land_ref, rows, chips=chips).astype(F32)
        for j in range(1, own_ref.shape[0]):
            g = g + _pick_slab(j, own_ref, land_ref, rows, chips=chips).astype(F32)
        return g

    def body(oi_ref, li_ref, oo_ref, lo_ref, wi_ref, mi_ref, vi_ref, wo_ref, mo_ref, vo_ref,
             gi, di, mi, vi, go, do, mo, vo):
        g = total(oi_ref, li_ref, slice(0, WSHP), True)[:WSH]
        d, mn, vn = _adamw(wi_ref[...], g, mi_ref[...], vi_ref[...])
        gi[...], di[...], mi[...], vi[...] = g, d, mn, vn
        g = total(oo_ref, lo_ref, slice(0, e_sh), False)
        d, mn, vn = _adamw(wo_ref[0], g, mo_ref[0], vo_ref[0])
        go[0], do[0], mo[0], vo[0] = g, d, mn, vn

    slab = lambda n, rows: pl.BlockSpec((n, rows, cb), lambda i: (0, 0, i))
    ispec = pl.BlockSpec((WSH, cb), lambda i: (0, i))
    ospec = pl.BlockSpec((1, e_sh, cb), lambda i: (0, 0, i))
    return pl.pallas_call(
        body, name="adamw_big", grid=(D // cb,),
        in_specs=[slab(4, WSHP), slab(4, WSHP), slab(NDEV, e_sh), slab(NDEV, e_sh),
                  ispec, ispec, ispec, ospec, ospec, ospec],
        out_specs=[ispec] * 4 + [ospec] * 4, out_shape=[in_shape] * 4 + [out_shape] * 4,
        compiler_params=_params(),
    )(own_in, land_in, own_out, land_out, w_in_t, m_in_t, v_in_t, w_out, m_out, v_out)


F0 = 3 * DA


def _unshard_w_out(own, land):
    e_sh = D // NDEV

    def body(own_ref, land_ref, wo_ref):
        for j in range(NDEV):
            wo_ref[j * e_sh:(j + 1) * e_sh, :] = _pick_slab(j, own_ref, land_ref, slice(0, e_sh), per_peer=False)

    return pl.pallas_call(
        body, name="unshard_w_out", grid=(D // CB,),
        in_specs=[pl.BlockSpec((e_sh, CB), lambda i: (0, i)), pl.BlockSpec((NDEV, e_sh, CB), lambda i: (0, 0, i))],
        out_specs=pl.BlockSpec((D, CB), lambda i: (0, i)),
        out_shape=jax.ShapeDtypeStruct((D, D), BF16),
        compiler_params=_params(),
    )(own, land)


def _unshard_w_in(w_all, small_all, attn_gain, conv_gain):
    def body(w_ref, small_ref, ga_ref, gc_ref, wt_ref, meta_ref, cwb_ref, gab_ref, gcb_ref):
        i = pl.program_id(0)
        for k in range(CB // TB):
            meta_ref[:, k * TB:(k + 1) * TB] = small_ref[(CB // TB) * i + k, 0:NM, :]

        @pl.when(i == 0)
        def _():
            per_row = lambda line: jnp.broadcast_to(line, (TB, DA)).T
            cw = jnp.concatenate([small_ref[j, NM:NM + 3, 0:DH] for j in range(NDEV)], axis=1)
            for k in range(3):
                cwb_ref[k] = per_row(cw[k:k + 1, :])
            gab_ref[...] = per_row(ga_ref[...])
            gcb_ref[...] = per_row(gc_ref[...])

        def ref_rows(lo, hi):
            pieces, r = [], lo
            while r < hi:
                sh, off = divmod(r, WSH)
                n = min(hi - r, WSH - off)
                pieces.append(w_ref[sh, off:off + n, :])
                r += n
            return pieces

        for s in range(NSEC):
            lo = s * DA if s < 3 else s * DA + H
            wt_ref[s * DA:(s + 1) * DA, :] = jnp.concatenate(ref_rows(lo, lo + DA), axis=0)
        wt_ref[NSEC * DA:DPROJ, :] = jnp.concatenate(
            ref_rows(F0, F0 + H) + [jnp.zeros((DF - H, CB), BF16)], axis=0)

    return pl.pallas_call(
        body, name="unshard_w_in", grid=(D // CB,),
        in_specs=[pl.BlockSpec((NDEV, WSHP, CB), lambda i: (0, 0, i)), _full_spec(small_all.shape),
                  _full_spec((1, DA)), _full_spec((1, DA))],
        out_specs=[pl.BlockSpec((DPROJ, CB), lambda i: (0, i)), pl.BlockSpec((NM, CB), lambda i: (0, i)),
                   _full_spec((3, DA, TB)), _full_spec((DA, TB)), _full_spec((DA, TB))],
        out_shape=[jax.ShapeDtypeStruct((DPROJ, D), BF16), jax.ShapeDtypeStruct((NM, D), F32),
                   jax.ShapeDtypeStruct((3, DA, TB), F32), jax.ShapeDtypeStruct((DA, TB), F32),
                   jax.ShapeDtypeStruct((DA, TB), F32)],
        compiler_params=_params(),
    )(w_all, small_all, attn_gain, conv_gain)


def _shard_w_in_grads(dw_main, dw_f):
    def body(dm_ref, df_ref, p_ref):
        mc = lax.axis_index("c")

        def ref_rows(lo, hi):
            pieces, r = [], lo
            while r < hi:
                if r < F0:
                    n = min(hi, F0) - r
                    pieces.append(dm_ref[r:r + n, :])
                elif r < F0 + H:
                    n = min(hi, F0 + H) - r
                    pieces.append(df_ref[r - F0:r - F0 + n, :])
                else:
                    n = hi - r
                    pieces.append(dm_ref[r - H:r - H + n, :])
                r += n
            return pieces

        for i in range(NDEV):
            rows = jnp.concatenate(ref_rows(i * WSH, (i + 1) * WSH) + [jnp.zeros((WSHP - WSH, CB), F32)], axis=0)
            p_ref[i // 2 + jnp.where(mc == i % 2, 0, 4)] = rows.astype(BF16)

    col = lambda rows: pl.BlockSpec((rows, CB), lambda i: (0, i))
    return pl.pallas_call(
        body, name="shard_w_in_grads", grid=(D // CB,),
        in_specs=[col(NSEC * DA), col(DF)],
        out_specs=pl.BlockSpec((NDEV, WSHP, CB), lambda i: (0, 0, i)),
        out_shape=jax.ShapeDtypeStruct((NDEV, WSHP, D), BF16),
        compiler_params=_params(),
    )(dw_main, dw_f)


SMALL = ("norm_g", "final_norm_g", "attn_norm_g", "conv_norm_g", "b_f", "meta", "conv_w")


def _as_rows(x):
    return jnp.concatenate([x[:, r * TB:(r + 1) * TB] for r in range(x.shape[1] // TB)], axis=0)


def _as_line(rows):
    return jnp.concatenate([rows[r:r + 1, :] for r in range(rows.shape[0])], axis=1)


def _pad_rows(x, n=8):
    return jnp.concatenate([x, jnp.zeros((n - x.shape[0], x.shape[1]), F32)], axis=0)


def _tile_rows(a, rows, lanes=TB):
    a = a.reshape(rows, lanes)
    return jnp.pad(a, ((0, -rows % 8), (0, TB - lanes)))


def _pack_small_grads(dg_norm, dg_final, dga_p, dgc_p, dcw_p, db_b, dmeta, loss):
    def body(dgn_ref, dgf_ref, dga_ref, dgc_ref, dcw_ref, db_ref, dmeta_ref, loss_ref, out_ref):
        def lane_sums(p):
            return jnp.sum(p.T, axis=0, keepdims=True)

        lane = lax.broadcasted_iota(jnp.int32, (1, TB), 1)
        b_row = jnp.where(lane == H, loss_ref[...], 0.0)
        for h in range(H):
            b_row = b_row + jnp.where(lane == h, db_ref[h:h + 1, :], 0.0)
        common = jnp.concatenate([
            _as_rows(dgn_ref[...]), _as_rows(dgf_ref[...]), _pad_rows(_as_rows(lane_sums(dga_ref[...]))),
            _pad_rows(_as_rows(lane_sums(dgc_ref[...]))), _pad_rows(b_row)], axis=0)
        dcw = [lane_sums(dcw_ref[k]) for k in range(3)]
        for j in range(NDEV):
            cw = jnp.concatenate(
                [jnp.concatenate([r[:, j * DH:(j + 1) * DH], jnp.zeros((1, TB - DH), F32)], axis=1) for r in dcw],
                axis=0)
            out_ref[j] = jnp.concatenate([common, dmeta_ref[:, j * TB:(j + 1) * TB], _pad_rows(cw)], axis=0)

    return pl.pallas_call(
        body, name="pack_small_grads", out_shape=jax.ShapeDtypeStruct((NDEV, SROWS, TB), F32),
    )(dg_norm, dg_final, dga_p, dgc_p, dcw_p, db_b, dmeta, loss)


def _adamw_small(own, land, params):
    flat = [a for n in SMALL for a in params[n]]

    def body(*refs):
        own_ref, land_ref = refs[:2]
        ins = refs[2:2 + 3 * len(SMALL)]
        outs = refs[2 + 3 * len(SMALL):]
        g = _pick_slab(0, own_ref, land_ref, slice(0, SROWS))
        for j in range(1, NDEV):
            g = g + _pick_slab(j, own_ref, land_ref, slice(0, SROWS))
        grads = dict(
            norm_g=_as_line(g[0:8]), final_norm_g=_as_line(g[8:16]), attn_norm_g=_as_line(g[16:20]),
            conv_norm_g=_as_line(g[24:28]), b_f=g[32:33, :H], meta=g[40:56], conv_w=g[56:59, :DH][None])
        for i, n in enumerate(SMALL):
            w_ref, m_ref, v_ref = ins[3 * i:3 * i + 3]
            d, mn, vn = _adamw(w_ref[...], grads[n], m_ref[...], v_ref[...])
            for o_ref, val in zip(outs[4 * i:4 * i + 4], (grads[n], d, mn, vn)):
                o_ref[...] = val
        outs[-1][...] = g[32:33, H:H + 1]

    shapes = [jax.ShapeDtypeStruct(params[n][0].shape, F32) for n in SMALL for _ in range(4)]
    res = pl.pallas_call(
        body, name="adamw_small", out_shape=shapes + [jax.ShapeDtypeStruct((1, 1), F32)],
    )(own, land, *flat)
    return {n: res[4 * i:4 * i + 4] for i, n in enumerate(SMALL)}, res[-1]


def kernel(x, meta, norm_g, w_in, b_f, conv_w, attn_norm_g, conv_norm_g, w_out, final_norm_g, loss_target, m_meta, m_norm_g, m_w_in, m_b_f, m_conv_w, m_attn_norm_g, m_conv_norm_g, m_w_out, m_final_norm_g, v_meta, v_norm_g, v_w_in, v_b_f, v_conv_w, v_attn_norm_g, v_conv_norm_g, v_w_out, v_final_norm_g):
    seq = x.shape[1]
    L = seq + TB
    assert x.shape == (1, seq, D) and L % TT == 0 and w_in.shape == (1, D, WSH)
    x2 = x[0]
    tgt = loss_target[0]

    w_in_slab = jnp.pad(w_in[0].T, ((0, WSHP - WSH), (0, 0))).astype(BF16)
    w_out_slab = w_out[0].astype(BF16)
    meta_slab = jnp.concatenate([meta, _tile_rows(conv_w[0], 3, DH)], axis=0)
    wout_flight = _split_start(w_out_slab, "gather_w_out_start", per_peer=False)
    w_all, small_all = _all_gather([w_in_slab, meta_slab], "gather_w_in")

    w_t, meta_full, cw_b, ga_b, gcn_b = _unshard_w_in(w_all, small_all, attn_norm_g, conv_norm_g)

    u, proj_t, gate_t, f_t, ktok, vtok = _inproj_fwd(x2, meta_full, norm_g, w_t, L, after=wout_flight[4])
    cq, kaug, sg = _fgate_fwd(f_t, b_f.reshape(H, 1), ktok, L)
    o_t, lse = _attn_fwd(proj_t, kaug, cq, L)

    w_out_own, w_out_land = _split_wait(wout_flight, o_t, "gather_w_out_wait", per_peer=False)
    w_out_full = _unshard_w_out(w_out_own, w_out_land)
    dout, dw_out, loss_part, dg_final, do_t, dd, dg5_t, dga_p, dgc_p, dcw_p = _gate_outproj(
        o_t, gate_t, cw_b, ga_b, gcn_b, w_out_full, x2, meta_full, final_norm_g.reshape(1, D), tgt, L)
    dwo_flight = _split_start(dw_out.reshape(NDEV, D // NDEV, D), "exchange_dw_out_start", per_peer=True)
    dq_t, dk_t, dv_t, dck, dcq = _attn_bwd(proj_t, kaug, vtok, do_t, lse, dd, cq, L, after=dwo_flight[4])
    df_t, db_f = _fgate_bwd(dcq, dck, sg, L)
    dw_main, dw_f = _inproj_bwd_w(u, dq_t, dk_t, dv_t, dg5_t, df_t, L)
    dwi_parts = _shard_w_in_grads(dw_main, dw_f)
    dwi_chip = _pair_sum(dwi_parts, _pair_exchange(dwi_parts, "exchange_dw_in_pair"))
    dwi_flight = _split_start(dwi_chip, "exchange_dw_in_start", per_peer=True, chips=True)
    grad_x, dmeta, dg_norm = _inproj_bwd_x(
        w_t, dq_t, dk_t, dv_t, dg5_t, df_t, dout, x2, meta_full, norm_g, L, after=dwi_flight[4])
    small_parts = _pack_small_grads(dg_norm, dg_final, dga_p, dgc_p, dcw_p, db_f, dmeta, loss_part)
    small_flight = _split_start(small_parts, "exchange_small_start", per_peer=True)
    dwo_own, dwo_land = _split_wait(dwo_flight, small_flight[4], "exchange_dw_out_wait", per_peer=True)
    dwi_own, dwi_land = _split_wait(dwi_flight, dwo_land, "exchange_dw_in_wait", per_peer=True, chips=True)

    big_out = _adamw_big(dwi_own, dwi_land, dwo_own, dwo_land,
                         w_in[0].T, m_w_in[0].T, v_w_in[0].T, w_out, m_w_out, v_w_out)
    g_w_in, d_w_in, nm_w_in, nv_w_in = [a.T[None] for a in big_out[:4]]
    g_w_out, d_w_out, nm_w_out, nv_w_out = big_out[4:]
    sm_own, sm_land = _split_wait(small_flight, big_out[4], "exchange_small_wait", per_peer=True)
    line = lambda a: a.reshape(1, D)
    small, loss = _adamw_small(sm_own, sm_land, dict(
        norm_g=(norm_g, m_norm_g, v_norm_g),
        final_norm_g=(line(final_norm_g), line(m_final_norm_g), line(v_final_norm_g)),
        attn_norm_g=(attn_norm_g, m_attn_norm_g, v_attn_norm_g),
        conv_norm_g=(conv_norm_g, m_conv_norm_g, v_conv_norm_g),
        b_f=(b_f, m_b_f, v_b_f), meta=(meta, m_meta, v_meta), conv_w=(conv_w, m_conv_w, v_conv_w)))
    small["final_norm_g"] = [a.reshape(D) for a in small["final_norm_g"]]
    order = ("meta", "norm_g", "w_in", "b_f", "conv_w", "attn_norm_g", "conv_norm_g", "w_out", "final_norm_g")
    groups = []
    for k, (wi, wo) in enumerate(((g_w_in, g_w_out), (d_w_in, d_w_out), (nm_w_in, nm_w_out), (nv_w_in, nv_w_out))):
        d = dict({n: small[n][k] for n in SMALL}, w_in=wi, w_out=wo)
        groups.append([d[n] for n in order])
    return (loss[0, 0], grad_x[None], *groups[0], *groups[1], *groups[2], *groups[3])
```

```python
import jax
import jax.numpy as jnp
from jax import lax
from jax.experimental import pallas as pl
from jax.experimental.pallas import tpu as pltpu

F32 = jnp.float32
BF16 = jnp.bfloat16

D = 1024
DA = 512
H = 8
DH = 64
NM = 16
TB = 128
P0 = TB - NM
TT = 3 * TB
HG = 8
NDEV = 8
NSEC = 8
DF = 16
DPROJ = NSEC * DA + DF
WSH = 513
WSHP = 528
WROWS = WSHP + D // NDEV
SROWS = 64
EPS = 1e-6
NEG = -1e30
LOG2E = 1.4426950408889634
LN2 = 0.6931471805599453
QSCALE = DH ** -0.5 * LOG2E
KA = 128
CB = 256
VMEM_LIMIT = 56 * 1024 * 1024

ADAM_LR = 0.001
ADAM_B1 = 0.9
ADAM_B2 = 0.999
ADAM_EPS = 1e-08
ADAM_WD = 0.01
ADAM_STEP = 10

NT_DIMS = (((1,), (1,)), ((), ()))
TN_DIMS = (((0,), (0,)), ((), ()))
MESH = pl.DeviceIdType.MESH


def _params(n_axes=1, vmem=VMEM_LIMIT):
    return pltpu.CompilerParams(dimension_semantics=("arbitrary",) * n_axes, vmem_limit_bytes=vmem)


def _dot(a, b, dims=None):
    if dims is None:
        return jnp.dot(a, b, preferred_element_type=F32)
    return lax.dot_general(a, b, dims, preferred_element_type=F32)


def _my_place():
    return lax.axis_index("x"), lax.axis_index("y"), lax.axis_index("c")


def _all_gather(xs, name):
    n = len(xs)

    def body(*refs):
        x_refs, out_refs = refs[:n], refs[n:2 * n]
        send_sems, recv_sems, local_sems = refs[2 * n:]
        mx, my, mc = _my_place()

        def across(px, py, pc, axis_a):
            flip_x = pc if axis_a else 1 - pc
            return (px + flip_x) % 2, (py + 1 - flip_x) % 2, pc

        def idx(p):
            return 4 * p[0] + 2 * p[1] + p[2]

        me, sib = (mx, my, mc), (mx, my, 1 - mc)
        a_nbr, b_nbr = across(*me, True), across(*me, False)
        diag = across(*b_nbr, True)
        sib_a, sib_b = across(*sib, True), across(*sib, False)
        sib_diag = across(*sib_b, True)

        waits = []
        for t in range(n):
            out_ref = out_refs[t]

            def copy(k, block, to, src=None, out_ref=out_ref, t=t):
                return pltpu.make_async_remote_copy(
                    src_ref=out_ref.at[idx(block)] if src is None else src, dst_ref=out_ref.at[idx(block)],
                    send_sem=send_sems.at[7 * t + k], recv_sem=recv_sems.at[7 * t + k],
                    device_id=to, device_id_type=MESH)

            mine = pltpu.make_async_copy(x_refs[t], out_ref.at[idx(me)], local_sems.at[t])
            mine.start()
            started = [copy(0, me, sib, src=x_refs[t]), copy(1, me, a_nbr, src=x_refs[t]),
                       copy(2, me, b_nbr, src=x_refs[t])]
            for cp in started:
                cp.start()
            waits.append((copy, mine, started))
        relays = ((1, a_nbr, ((3, b_nbr), (4, sib))), (2, b_nbr, ((5, sib),)), (3, diag, ((6, sib),)))
        for landed, block, onward in relays:
            for copy, _, started in waits:
                copy(landed, block, me).wait_recv()
                for k, to in onward:
                    started.append(copy(k, block, to))
                    started[-1].start()
        for copy, mine, started in waits:
            for k, block in ((0, sib), (4, sib_a), (5, sib_b), (6, sib_diag)):
                copy(k, block, me).wait_recv()
            for cp in started:
                cp.wait_send()
            mine.wait()

    any_spec = pl.BlockSpec(memory_space=pl.ANY)
    return pl.pallas_call(
        body, name=name,
        out_shape=[jax.ShapeDtypeStruct((NDEV,) + x.shape, x.dtype) for x in xs],
        in_specs=[any_spec] * n, out_specs=[any_spec] * n,
        scratch_shapes=[pltpu.SemaphoreType.DMA((7 * n,)), pltpu.SemaphoreType.DMA((7 * n,)),
                        pltpu.SemaphoreType.DMA((n,))],
    )(*xs)


_HBM = pl.BlockSpec(memory_space=pltpu.HBM)
_UNREAD = pl.BlockSpec(memory_space=pl.ANY)
_SEM = pl.BlockSpec(memory_space=pltpu.SEMAPHORE)
_EFFECT = pltpu.SideEffectType.DATAFLOW_SIDE_EFFECTING


def _peer_of(m, place):
    mx, my, mc = place
    return ((1 - mx) if m & 4 else mx, (1 - my) if m & 2 else my, (1 - mc) if m & 1 else mc)


def _party(chips):
    if chips:
        return (lambda p: 2 * p[0] + p[1]), (2, 4, 6)
    return (lambda p: 4 * p[0] + 2 * p[1] + p[2]), tuple(range(1, NDEV))


def _split_copies(src_ref, land_ref, send_sems, recv_sems, per_peer, incoming, chips):
    place = _my_place()
    slot, masks = _party(chips)
    me = slot(place)
    out = []
    for k, m in enumerate(masks):
        there = _peer_of(m, place)
        peer = slot(there)
        src = (src_ref.at[me] if incoming else src_ref.at[peer]) if per_peer else src_ref
        out.append(pltpu.make_async_remote_copy(
            src_ref=src, dst_ref=land_ref.at[peer if incoming else me],
            send_sem=send_sems.at[k], recv_sem=recv_sems.at[k], device_id=there, device_id_type=MESH))
    return out


def _split_start(src, name, per_peer, chips=False):
    slab = src.shape[1:] if per_peer else src.shape
    n = len(_party(chips)[1])

    def body(src_ref, land_ref, send_sems, recv_sems, src_thru, land_thru, token):
        for cp in _split_copies(src_ref, land_ref, send_sems, recv_sems, per_peer, False, chips):
            cp.start()
        token[...] = jnp.zeros_like(token)

    return pl.pallas_call(
        body, name=name,
        out_shape=(pltpu.SemaphoreType.DMA((n,)), pltpu.SemaphoreType.DMA((n,)),
                   pltpu.HBM(src.shape, src.dtype), pltpu.HBM((n + 1,) + slab, src.dtype),
                   jax.ShapeDtypeStruct((8, TB), F32)),
        in_specs=(_HBM, _HBM), out_specs=(_SEM, _SEM, _HBM, _HBM, pl.BlockSpec(memory_space=pltpu.VMEM)),
        input_output_aliases={0: 2, 1: 3},
        compiler_params=pltpu.CompilerParams(has_side_effects=_EFFECT),
    )(pltpu.with_memory_space_constraint(src, pltpu.HBM),
      pltpu.with_memory_space_constraint(lax.empty((n + 1,) + slab, src.dtype), pltpu.HBM))


def _split_wait(handles, after, name, per_peer, chips=False):
    send_sems, recv_sems, src_thru, land_thru, _ = handles

    def body(src_ref, land_ref, send_sems, recv_sems, after_ref, src_out, land_out):
        for cp in _split_copies(src_ref, land_ref, send_sems, recv_sems, per_peer, False, chips):
            cp.wait_send()
        for cp in _split_copies(src_ref, land_ref, send_sems, recv_sems, per_peer, True, chips):
            cp.wait_recv()

    return pl.pallas_call(
        body, name=name,
        out_shape=(pltpu.HBM(src_thru.shape, src_thru.dtype), pltpu.HBM(land_thru.shape, land_thru.dtype)),
        in_specs=(_HBM, _HBM, _SEM, _SEM, pl.BlockSpec(memory_space=pl.ANY)), out_specs=(_HBM, _HBM),
        input_output_aliases={0: 0, 1: 1},
        compiler_params=pltpu.CompilerParams(has_side_effects=_EFFECT),
    )(src_thru, land_thru, send_sems, recv_sems, after)


def _pick_slab(j, own_ref, land_ref, rows, per_peer=True, chips=False):
    me = _party(chips)[0](_my_place())
    own = (lambda: own_ref[j, rows, :]) if per_peer else (lambda: own_ref[rows, :])
    return lax.cond(me == j, own, lambda: land_ref[j, rows, :])


def _pair_exchange(p, name):
    def body(p_ref, got_ref, send_sems, recv_sems):
        mx, my, mc = _my_place()
        copies = [pltpu.make_async_remote_copy(
            src_ref=p_ref.at[4 + q], dst_ref=got_ref.at[q], send_sem=send_sems.at[q],
            recv_sem=recv_sems.at[q], device_id=(mx, my, 1 - mc), device_id_type=MESH) for q in range(4)]
        for cp in copies:
            cp.start()
        for cp in copies:
            cp.wait_recv()
        for cp in copies:
            cp.wait_send()

    any_spec = pl.BlockSpec(memory_space=pl.ANY)
    return pl.pallas_call(
        body, name=name, out_shape=jax.ShapeDtypeStruct((4,) + p.shape[1:], p.dtype),
        in_specs=[any_spec], out_specs=any_spec,
        scratch_shapes=[pltpu.SemaphoreType.DMA((4,)), pltpu.SemaphoreType.DMA((4,))],
    )(p)


def _pair_sum(p, got):
    rows = p.shape[1]

    def body(p_ref, got_ref, out_ref):
        for q in range(4):
            out_ref[q] = (p_ref[q].astype(F32) + got_ref[q].astype(F32)).astype(BF16)

    blk = lambda n: pl.BlockSpec((n, rows, CB), lambda i: (0, 0, i))
    return pl.pallas_call(
        body, name="pair_sum", grid=(D // CB,), in_specs=[blk(4), blk(4)], out_specs=blk(4),
        out_shape=jax.ShapeDtypeStruct((4, rows, D), BF16), compiler_params=_params(),
    )(p, got)


def _h_block(t, x_ref, meta_ref):
    first = jnp.concatenate([jnp.zeros((P0, D), F32), meta_ref[...]], axis=0)
    return jnp.where(t == 0, first, x_ref[...])


def _x_specs3(tile=lambda j: j):
    return [pl.BlockSpec((TB, D), lambda j: (jnp.maximum(3 * tile(j) - 1, 0), 0)),
            pl.BlockSpec((TB, D), lambda j: (3 * tile(j), 0)),
            pl.BlockSpec((TB, D), lambda j: (3 * tile(j) + 1, 0))]


def _h_tile(j, xa_ref, xb_ref, xc_ref, meta_ref):
    first = jnp.concatenate([jnp.zeros((P0, D), F32), meta_ref[...]], axis=0)
    return jnp.concatenate([jnp.where(j == 0, first, xa_ref[...]), xb_ref[...], xc_ref[...]], axis=0)


def _full_spec(shape):
    return pl.BlockSpec(shape, lambda *_: (0,) * len(shape))


def _sigmoid(z):
    return 1.0 / (1.0 + jnp.exp(-z))


def _lane_tiles_sum(x):
    out = x[:, :TB]
    for i in range(1, x.shape[1] // TB):
        out = out + x[:, i * TB:(i + 1) * TB]
    return out


def _inproj_fwd(x, meta_full, norm_g, w_t, L, after):
    nj = L // TT

    def body(xa_ref, xb_ref, xc_ref, meta_ref, g_ref, w_ref, _, u_ref, proj_ref, gate_ref, f_ref, ktok_ref, vtok_ref):
        hb = _h_tile(pl.program_id(0), xa_ref, xb_ref, xc_ref, meta_ref)
        r = lax.rsqrt(jnp.mean(hb * hb, axis=-1, keepdims=True) + EPS)
        u = (hb * r * g_ref[...]).astype(BF16)
        u_ref[...] = u
        for s in range(NSEC):
            p = _dot(u, w_ref[s * DA:(s + 1) * DA, :], NT_DIMS)
            if s == 0:
                p = p * QSCALE
            if s in (1, 2):
                tok_ref = ktok_ref if s == 1 else vtok_ref
                for h in range(H):
                    tok_ref[h] = p[:, h * DH:(h + 1) * DH].astype(BF16)
            out_ref, s_out = (proj_ref, s) if s < 3 else (gate_ref, s - 3)
            out_ref[s_out * DA:(s_out + 1) * DA, :] = p.T.astype(BF16)
        f_ref[...] = _dot(w_ref[NSEC * DA:DPROJ, :], u, NT_DIMS)[:H]

    return pl.pallas_call(
        body, name="inproj_fwd", grid=(nj,),
        in_specs=_x_specs3() + [_full_spec((NM, D)), _full_spec((1, D)), _full_spec((DPROJ, D)), _UNREAD],
        out_specs=[
            pl.BlockSpec((TT, D), lambda t: (t, 0)),
            pl.BlockSpec((3 * DA, TT), lambda t: (0, t)),
            pl.BlockSpec((None, (NSEC - 3) * DA, TT), lambda t: (t, 0, 0)),
            pl.BlockSpec((H, TT), lambda t: (0, t)),
            pl.BlockSpec((H, TT, DH), lambda t: (0, t, 0)),
            pl.BlockSpec((H, TT, DH), lambda t: (0, t, 0)),
        ],
        out_shape=[
            jax.ShapeDtypeStruct((L, D), BF16),
            jax.ShapeDtypeStruct((3 * DA, L), BF16),
            jax.ShapeDtypeStruct((nj, (NSEC - 3) * DA, TT), BF16),
            jax.ShapeDtypeStruct((H, L), F32),
            jax.ShapeDtypeStruct((H, L, DH), BF16),
            jax.ShapeDtypeStruct((H, L, DH), BF16),
        ],
        compiler_params=_params(),
    )(x, x, x, meta_full, norm_g, w_t, after)


def _split3(x):
    hi = x.astype(BF16).astype(F32)
    r = x - hi
    mid = r.astype(BF16).astype(F32)
    return hi, mid, (r - mid).astype(BF16).astype(F32)


def _bias_rows(bias):
    one = jnp.ones((1, TT), F32)
    zero = jnp.zeros((1, TT), F32)
    parts = [zero] * 3 if bias is None else list(_split3(bias))
    return jnp.concatenate([one] * 3 + parts + [zero] * (DF - 6), axis=0).astype(BF16)


def _fgate_fwd(f_t, b_col, ktok, L):
    nb = L // TB

    def body(f_ref, b_ref, ktok_ref, cq_ref, kaug_ref, sg_ref, bias_scr):
        h = pl.program_id(0)

        @pl.when(h == 0)
        def _():
            z = f_ref[...] + b_ref[...]
            idx = lax.broadcasted_iota(jnp.int32, (H, L), 1)
            real = idx >= P0
            lf = jnp.where(real, jnp.minimum(z, 0.0) - jnp.log1p(jnp.exp(-jnp.abs(z))), 0.0)
            sg_ref[...] = jnp.where(real, 1.0 / (1.0 + jnp.exp(z)), 0.0)
            c = lf
            s = 1
            while s < L:
                c = c + jnp.where(idx >= s, pltpu.roll(c, s, 1), 0.0)
                s *= 2
            c = c * LOG2E
            for hh in range(H):
                cq_ref[hh] = c[hh:hh + 1, :]
            for part, val in enumerate(_split3(-jnp.where(real, c, -NEG))):
                for hh in range(H):
                    bias_scr[part * H + hh] = val[hh:hh + 1, :]

        lane = lax.broadcasted_iota(jnp.int32, (TB, KA), 1)
        head = jnp.zeros((DH, TB), F32)
        tail = jnp.concatenate([jnp.ones((3, TB), F32), jnp.zeros((KA - DH - 6, TB), F32)], axis=0)
        for b in range(nb):
            blk = slice(b * TB, (b + 1) * TB)
            cols = jnp.concatenate(
                [head] + [bias_scr[part * H + h, :, blk] for part in range(3)] + [tail], axis=0).T
            k = jnp.concatenate([ktok_ref[0, blk, :].astype(F32), jnp.zeros((TB, KA - DH), F32)], axis=1)
            kaug_ref[0, blk, :] = jnp.where(lane < DH, k, cols).astype(BF16)

    return pl.pallas_call(
        body, name="fgate_fwd", grid=(H,),
        in_specs=[_full_spec((H, L)), _full_spec((H, 1)), pl.BlockSpec((1, L, DH), lambda h: (h, 0, 0))],
        out_specs=[_full_spec((H, 1, L)), pl.BlockSpec((1, L, KA), lambda h: (h, 0, 0)), _full_spec((H, L))],
        out_shape=[
            jax.ShapeDtypeStruct((H, 1, L), F32),
            jax.ShapeDtypeStruct((H, L, KA), BF16),
            jax.ShapeDtypeStruct((H, L), F32),
        ],
        scratch_shapes=[pltpu.VMEM((3 * H, 1, L), F32)],
        compiler_params=_params(),
    )(f_t, b_col, ktok)


def _causal_mask():
    r = lax.broadcasted_iota(jnp.int32, (TT, TT), 0)
    c = lax.broadcasted_iota(jnp.int32, (TT, TT), 1)
    return r <= c


def _attn_fwd(proj_t, kaug, cq, L):
    nq = L // TT

    def body(q_ref, qn_ref, kaug_ref, v_ref, cq_ref, o_ref, lse_ref,
             qa_scr, s_scr, cmax_scr, m_scr, p_scr, alpha_scr, acc_scr):
        j = pl.program_id(0)
        rows = [slice(g * DH, (g + 1) * DH) for g in range(HG)]
        ones = jnp.ones((DF, TT), BF16)

        def load_queries(ref):
            for g in range(HG):
                qa_scr[g] = jnp.concatenate(
                    [ref[rows[g], :], _bias_rows(None), jnp.zeros((KA - DH - DF, TT), BF16)], axis=0)

        def scores(kt, masked):
            k_off = pl.multiple_of(kt * TT, TT)
            for g in range(HG):
                s = _dot(kaug_ref[g, pl.ds(k_off, TT), :], qa_scr[g])
                if masked:
                    s = jnp.where(_causal_mask(), s, NEG)
                s_scr[g] = s
                cmax_scr[g] = jnp.max(s, axis=0, keepdims=True)

        def softmax():
            for g in range(HG):
                m_old = m_scr[g]
                m_new = jnp.maximum(m_old, cmax_scr[g])
                alpha_scr[g] = jnp.exp2(m_old - m_new)
                p_scr[g] = jnp.exp2(s_scr[g] - m_new).astype(BF16)
                m_scr[g] = m_new

        def weighted_sum(kt):
            k_off = pl.multiple_of(kt * TT, TT)
            for g in range(HG):
                v1 = jnp.concatenate([v_ref[rows[g], pl.ds(k_off, TT)], ones], axis=0)
                acc_scr[g] = alpha_scr[g] * acc_scr[g] + _dot(v1, p_scr[g])

        @pl.when(j == 0)
        def _():
            load_queries(q_ref)
            scores(0, True)

        m_scr[...] = jnp.full_like(m_scr, NEG)
        acc_scr[...] = jnp.zeros_like(acc_scr)

        @pl.when(j >= 1)
        def _():
            softmax()
            scores(j - 1, False)

        def step(i, c):
            weighted_sum(j - i + 1)
            softmax()
            scores(j - i - 1, False)
            return c

        lax.fori_loop(1, j, step, 0)

        def drain(second_last, next_tile):
            if second_last:
                weighted_sum(1)
            softmax()
            if next_tile:
                load_queries(qn_ref)
                scores(j + 1, True)
            weighted_sum(0)

        @pl.when(j == 0)
        def _():
            drain(False, nq > 1)

        @pl.when((j >= 1) & (j < nq - 1))
        def _():
            drain(True, True)

        @pl.when((j >= 1) & (j == nq - 1))
        def _():
            drain(True, False)

        for g in range(HG):
            l = acc_scr[g, DH:DH + 1, :]
            o_ref[rows[g], :] = acc_scr[g, :DH, :] * (1.0 / l)
            lse_ref[g] = m_scr[g] + jnp.log2(l) + cq_ref[g]

    assert HG == H
    return pl.pallas_call(
        body, name="attn_fwd", grid=(nq,),
        in_specs=[
            pl.BlockSpec((DA, TT), lambda j: (0, j)),
            pl.BlockSpec((DA, TT), lambda j: (0, jnp.minimum(j + 1, nq - 1))),
            pl.BlockSpec((H, L, KA), lambda j: (0, 0, 0)),
            pl.BlockSpec((DA, L), lambda j: (2, 0)),
            pl.BlockSpec((H, 1, TT), lambda j: (0, 0, j)),
        ],
        out_specs=[
            pl.BlockSpec((None, DA, TT), lambda j: (j, 0, 0)),
            pl.BlockSpec((H, 1, TT), lambda j: (0, 0, j)),
        ],
        out_shape=[jax.ShapeDtypeStruct((nq, DA, TT), F32), jax.ShapeDtypeStruct((H, 1, L), F32)],
        scratch_shapes=[pltpu.VMEM((HG, KA, TT), BF16), pltpu.VMEM((HG, TT, TT), F32), pltpu.VMEM((HG, 1, TT), F32),
                        pltpu.VMEM((HG, 1, TT), F32), pltpu.VMEM((HG, TT, TT), BF16), pltpu.VMEM((HG, 1, TT), F32),
                        pltpu.VMEM((HG, DH + DF, TT), F32)],
        compiler_params=_params(),
    )(proj_t, proj_t, kaug, proj_t, cq)


def _gate_group(rows, o_ref, za_ref, gb_ref, gc_ref, xc_ref, zc_ref, gcp_ref, xcp_ref, cw_ref, ga_ref, gcn_ref, first):
    n_rep = TT // TB
    f32 = lambda r: r[rows, :].astype(F32)
    o, za, gb, gc, xc, zc = o_ref[rows, :], f32(za_ref), f32(gb_ref), f32(gc_ref), f32(xc_ref), f32(zc_ref)
    a = gc * xc
    a_prev = jnp.where(first, 0.0, f32(gcp_ref) * f32(xcp_ref))
    full = jnp.concatenate([a_prev, a], axis=1)
    a1 = pltpu.roll(full, 1, 1)[:, TB:]
    a2 = pltpu.roll(full, 2, 1)[:, TB:]
    w0 = jnp.tile(cw_ref[0, rows, :], (1, n_rep))
    w1 = jnp.tile(cw_ref[1, rows, :], (1, n_rep))
    w2 = jnp.tile(cw_ref[2, rows, :], (1, n_rep))
    cv = w0 * a2 + w1 * a1 + w2 * a
    e = gb * cv
    rc = lax.rsqrt(jnp.mean(e * e, axis=0, keepdims=True) + EPS)
    ec = e * rc
    ra = lax.rsqrt(jnp.mean(o * o, axis=0, keepdims=True) + EPS)
    oa = o * ra
    g_a = jnp.tile(ga_ref[rows, :], (1, n_rep))
    g_c = jnp.tile(gcn_ref[rows, :], (1, n_rep))
    sa = _sigmoid(za)
    sc = _sigmoid(zc)
    return dict(o=o, za=za, gb=gb, gc=gc, xc=xc, zc=zc, a=a, a1=a1, a2=a2, w0=w0, w1=w1, w2=w2, cv=cv, e=e,
                rc=rc, ec=ec, ra=ra, oa=oa, g_a=g_a, g_c=g_c, sa=sa, sc=sc)


def _gate_specs(tile):
    halo = pl.BlockSpec((None, 2 * DA, TB),
                        lambda i: (jnp.maximum(tile(i) - 1, 0), 1, TT // TB - 1))
    return [pl.BlockSpec((None, DA, TT), lambda i: (tile(i), 0, 0)),
            pl.BlockSpec((None, 5 * DA, TT), lambda i: (tile(i), 0, 0)), halo,
            _full_spec((3, DA, TB)), _full_spec((DA, TB)), _full_spec((DA, TB))]


def _gate_views(g5_ref, halo_ref):
    return [g5_ref.at[pl.ds(s * DA, DA)] for s in range(5)] + [halo_ref.at[pl.ds(s * DA, DA)] for s in range(2)]


def _gate_outproj(o_t, gate_t, cw_b, ga_b, gcn_b, w_out, x, meta_full, fng, target, L):
    nj = L // TT
    rp = NM
    n_bwd = 8
    cb = D // 4
    n_ring = 5
    assert P0 % rp == 0 and TB % rp == 0 and (TT // rp) % n_bwd == 0 and H == n_bwd

    def body(o_hbm, g5_hbm, halo_ref, cw_ref, ga_ref, gcn_ref, halo2_ref,
             w_ref, xa_ref, xb_ref, xc_ref, meta_ref, g_ref, ta_ref, tb_ref, tc_ref,
             dout_ref, dwb_ref, loss_ref, dg_ref, do_ref, dd_ref, dg5_ref, dga_ref, dgc_ref, dcw_ref,
             dw_ref, o_scr, db_new, db_old, mix_new, mix_old, dmix_new, dmix_old, sq_acc, dg_acc, carry_ref,
             o_ring, g5_ring, ring_sems):
        t = pl.program_id(0)

        def fetch(step):
            tile, slot = nj - 1 - step, step % n_ring
            return (pltpu.make_async_copy(o_hbm.at[tile], o_ring.at[slot], ring_sems.at[0, slot]),
                    pltpu.make_async_copy(g5_hbm.at[tile], g5_ring.at[slot], ring_sems.at[1, slot]))

        @pl.when(t == 0)
        def _():
            for step in range(2):
                for cp in fetch(step):
                    cp.start()

        @pl.when(t + 2 < nj)
        def _():
            for cp in fetch(t + 2):
                cp.start()

        @pl.when(t < nj)
        def _():
            for cp in fetch(t):
                cp.wait()

        slot_a, slot_c = t % n_ring, (t + n_ring - 2) % n_ring
        o_ref, o2_ref = o_ring.at[slot_a], o_ring.at[slot_c]
        za_ref, gb_ref, gc_ref, xcv_ref, zc_ref, gcp_ref, xcp_ref = _gate_views(g5_ring.at[slot_a], halo_ref)
        za2_ref, gb2_ref, gc2_ref, xcv2_ref, zc2_ref, gcp2_ref, xcp2_ref = _gate_views(g5_ring.at[slot_c], halo2_ref)
        first_a = t == nj - 1
        first_c = t == nj + 1

        def gate_rows(h):
            rows = slice(h * DH, (h + 1) * DH)
            g = _gate_group(rows, o_ref, za_ref, gb_ref, gc_ref, xcv_ref, zc_ref, gcp_ref, xcp_ref,
                            cw_ref, ga_ref, gcn_ref, first_a)
            mix_new[rows, :] = (g["oa"] * g["g_a"] * (g["za"] * g["sa"])).astype(BF16)
            mix_new[DA + h * DH:DA + (h + 1) * DH, :] = (g["ec"] * g["g_c"] * (g["zc"] * g["sc"])).astype(BF16)

        def loss_rows(c):
            blk = c // (TB // rp)
            rows, out_rows = pl.ds((c % (TB // rp)) * rp, rp), pl.ds(c * rp, rp)
            h = (xa_ref, xb_ref, xc_ref)[blk][rows, :]
            if blk == 0:
                first = meta_ref[...] if c == P0 // rp else jnp.zeros((rp, D), F32)
                h = jnp.where(first_a, first, h)
            o = o_scr[out_rows, :] + h
            r = lax.rsqrt(jnp.mean(o * o, axis=-1, keepdims=True) + EPS)
            orn = o * r
            g = g_ref[...]
            diff = orn * g - (ta_ref, tb_ref, tc_ref)[blk][rows, :]
            if blk == 0:
                diff = diff * jnp.where(first_a, 0.0, 1.0)
            gy = diff * (g * (1.0 / D))
            dout = r * (gy - orn * jnp.mean(gy * orn, axis=-1, keepdims=True))
            dout_ref[out_rows, :] = dout
            db_new[out_rows, :] = dout.astype(BF16)
            sq, go = diff * diff, diff * orn
            sq_acc[...] += sq[:8] + sq[8:]
            dg_acc[...] += go[:8] + go[8:]

        def backward_cols(n):
            if n < 4:
                cols = slice(n * cb, (n + 1) * cb)
                dmix_new[cols, :] = _dot(db_old[...], w_ref[cols, :], NT_DIMS).T.astype(BF16)
            else:
                cols = slice((n - 4) * cb, (n - 3) * cb)
                dw_ref[:, cols] += _dot(mix_old[...], db_old[:, cols])

        def gate_bwd_rows(h):
            rows = slice(h * DH, (h + 1) * DH)
            sec = lambda s: slice(s * DA + h * DH, s * DA + (h + 1) * DH)
            g = _gate_group(rows, o2_ref, za2_ref, gb2_ref, gc2_ref, xcv2_ref, zc2_ref, gcp2_ref, xcp2_ref,
                            cw_ref, ga_ref, gcn_ref, first_c)
            o, za, gb, gc, xc, zc, sa, sc = (g[n] for n in ("o", "za", "gb", "gc", "xc", "zc", "sa", "sc"))
            dya = dmix_old[rows, :].astype(F32)
            dyc = dmix_old[DA + h * DH:DA + (h + 1) * DH, :].astype(F32)

            dn = dya * (za * sa)
            dg5_ref[sec(0), :] = (dya * (g["oa"] * g["g_a"]) * (sa * (1.0 + za * (1.0 - sa)))).astype(BF16)
            dga_ref[rows, :] += _lane_tiles_sum(dn * g["oa"])
            dng = dn * g["g_a"]
            mean_a = jnp.mean(dng * g["oa"], axis=0, keepdims=True)
            do = (dng - g["oa"] * mean_a) * g["ra"]
            do_ref[rows, :] = do.astype(BF16)
            dd_ref[h] = jnp.sum(do * o, axis=0, keepdims=True)

            dnc = dyc * (zc * sc)
            dg5_ref[sec(4), :] = (dyc * (g["ec"] * g["g_c"]) * (sc * (1.0 + zc * (1.0 - sc)))).astype(BF16)
            dgc_ref[rows, :] += _lane_tiles_sum(dnc * g["ec"])
            dncg = dnc * g["g_c"]
            mean_c = jnp.mean(dncg * g["ec"], axis=0, keepdims=True)
            de = (dncg - g["ec"] * mean_c) * g["rc"]
            dg5_ref[sec(1), :] = (de * g["cv"]).astype(BF16)
            dcv = de * gb
            full = jnp.concatenate([dcv, carry_ref[rows, :]], axis=1)
            d1 = pltpu.roll(full, TT + TB - 1, 1)[:, :TT]
            d2 = pltpu.roll(full, TT + TB - 2, 1)[:, :TT]
            carry_ref[rows, :] = dcv[:, :TB]
            da = g["w2"] * dcv + g["w1"] * d1 + g["w0"] * d2
            dg5_ref[sec(2), :] = (da * xc).astype(BF16)
            dg5_ref[sec(3), :] = (da * gc).astype(BF16)
            dcw_ref[0, rows, :] += _lane_tiles_sum(dcv * g["a2"])
            dcw_ref[1, rows, :] += _lane_tiles_sum(dcv * g["a1"])
            dcw_ref[2, rows, :] += _lane_tiles_sum(dcv * g["a"])

        def step(a, b, c):
            half = H // 2
            for h in range(H):
                if a:
                    gate_rows(h)
                if c and h < half:
                    gate_bwd_rows(h)
                if b and h % 2 == 1:
                    backward_cols(h // 2)
            if a:
                o_scr[...] = _dot(mix_new[...], w_ref[...], TN_DIMS)
            per = TT // rp // n_bwd
            for k in range(n_bwd):
                if a:
                    for piece in range(per * k, per * (k + 1)):
                        loss_rows(piece)
                if c and k % 2 == 0:
                    gate_bwd_rows(half + k // 2)
                if b and k % 2 == 1:
                    backward_cols(n_bwd // 2 + k // 2)
            if a:
                db_old[...] = db_new[...]
                mix_old[...] = mix_new[...]
            if b:
                dmix_old[...] = dmix_new[...]

        @pl.when(t == 0)
        def _():
            dw_ref[...] = jnp.zeros_like(dw_ref)
            sq_acc[...] = jnp.zeros_like(sq_acc)
            dg_acc[...] = jnp.zeros_like(dg_acc)
            carry_ref[...] = jnp.zeros_like(carry_ref)
            dga_ref[...] = jnp.zeros_like(dga_ref)
            dgc_ref[...] = jnp.zeros_like(dgc_ref)
            dcw_ref[...] = jnp.zeros_like(dcw_ref)
            step(True, False, False)

        @pl.when(t == 1)
        def _():
            step(True, True, False)

        @pl.when((t >= 2) & (t < nj))
        def _():
            step(True, True, True)

        @pl.when(t == nj)
        def _():
            step(False, True, True)
            dwb_ref[...] = dw_ref[...].astype(BF16)
            loss_ref[...] = jnp.sum(sq_acc[...], keepdims=True) * (0.5 / D)
            dg_ref[...] = jnp.sum(dg_acc[...], axis=0, keepdims=True) * (1.0 / D)

        @pl.when(t == nj + 1)
        def _():
            step(False, False, True)

    assert nj >= 2
    tile_a = lambda t: jnp.clip(nj - 1 - t, 0, nj - 1)
    tile_c = lambda t: jnp.clip(nj + 1 - t, 0, nj - 1)
    at_c = lambda shape: pl.BlockSpec(shape, lambda t: (0,) * (len(shape) - 1) + (tile_c(t),))
    return pl.pallas_call(
        body, name="gate_outproj", grid=(nj + 2,),
        in_specs=[pl.BlockSpec(memory_space=pl.ANY)] * 2 + _gate_specs(tile_a)[2:] + _gate_specs(tile_c)[2:3]
                 + [_full_spec((D, D))] + _x_specs3(tile_a)
                 + [_full_spec((NM, D)), _full_spec((1, D))] + _x_specs3(tile_a),
        out_specs=[pl.BlockSpec((TT, D), lambda t: (tile_a(t), 0)), _full_spec((D, D)), _full_spec((1, 1)),
                   _full_spec((1, D)), at_c((DA, TT)), at_c((H, 1, TT)), at_c((5 * DA, TT)),
                   _full_spec((DA, TB)), _full_spec((DA, TB)), _full_spec((3, DA, TB))],
        out_shape=[jax.ShapeDtypeStruct((L, D), F32), jax.ShapeDtypeStruct((D, D), BF16),
                   jax.ShapeDtypeStruct((1, 1), F32), jax.ShapeDtypeStruct((1, D), F32),
                   jax.ShapeDtypeStruct((DA, L), BF16),
                   jax.ShapeDtypeStruct((H, 1, L), F32),
                   jax.ShapeDtypeStruct((5 * DA, L), BF16),
                   jax.ShapeDtypeStruct((DA, TB), F32),
                   jax.ShapeDtypeStruct((DA, TB), F32),
                   jax.ShapeDtypeStruct((3, DA, TB), F32)],
        scratch_shapes=[pltpu.VMEM((D, D), F32), pltpu.VMEM((TT, D), F32), pltpu.VMEM((TT, D), BF16),
                        pltpu.VMEM((TT, D), BF16), pltpu.VMEM((D, TT), BF16), pltpu.VMEM((D, TT), BF16),
                        pltpu.VMEM((D, TT), BF16), pltpu.VMEM((D, TT), BF16),
                        pltpu.VMEM((8, D), F32), pltpu.VMEM((8, D), F32), pltpu.VMEM((DA, TB), F32),
                        pltpu.VMEM((n_ring, DA, TT), F32), pltpu.VMEM((n_ring, 5 * DA, TT), BF16),
                        pltpu.SemaphoreType.DMA((2, n_ring))],
        compiler_params=_params(),
    )(o_t, gate_t, gate_t, cw_b, ga_b, gcn_b, gate_t,
      w_out, x, x, x, meta_full, fng, target, target, target)


def _attn_bwd(proj_t, kaug, vtok, do_t, lse, dd, cq, L, after):
    nk = L // TT

    def body(q_ref, kaug_ref, vtok_ref, kt_ref, do_ref, lse_ref, dd_ref, cq_ref, _,
             dq_ref, dk_ref, dv_ref, dck_ref, dcq_ref, dq_acc, kt1_scr, s_scr, dp_scr, dv_scr, dk_scr):
        i = pl.program_id(0)
        rows = [slice(g * DH, (g + 1) * DH) for g in range(HG)]
        ones = jnp.ones((DF, TT), BF16)
        zpad = jnp.zeros((KA - DH - DF, TT), BF16)
        for g in range(HG):
            kt1_scr[g] = jnp.concatenate([kt_ref[rows[g], :], ones], axis=0)
        dv_scr[...] = jnp.zeros_like(dv_scr)
        dk_scr[...] = jnp.zeros_like(dk_scr)

        def q_rows(g, q_off):
            bias = cq_ref[g, :, pl.ds(q_off, TT)] - lse_ref[g, :, pl.ds(q_off, TT)]
            return jnp.concatenate([q_ref[rows[g], pl.ds(q_off, TT)], _bias_rows(bias)], axis=0)

        def scores(jq, masked):
            q_off = pl.multiple_of(jq * TT, TT)
            for g in range(HG):
                s = _dot(kaug_ref[g], jnp.concatenate([q_rows(g, q_off), zpad], axis=0))
                if masked:
                    s = jnp.where(_causal_mask(), s, NEG)
                s_scr[g] = s
                dp_scr[g] = _dot(vtok_ref[g], do_ref[rows[g], pl.ds(q_off, TT)])

        def grads(jq):
            q_off = pl.multiple_of(jq * TT, TT)
            for g in range(HG):
                p = jnp.exp2(s_scr[g])
                ds = (p * (dp_scr[g] - dd_ref[g, :, pl.ds(q_off, TT)])).astype(BF16)
                do1 = jnp.concatenate([do_ref[rows[g], pl.ds(q_off, TT)], jnp.zeros((KA - DH, TT), BF16)], axis=0)
                q1 = jnp.concatenate([q_rows(g, q_off), zpad], axis=0)
                dv_scr[g] += _dot(p.astype(BF16), do1, NT_DIMS)
                dk_scr[g] += _dot(ds, q1, NT_DIMS)
                dq_acc[g, :, pl.ds(q_off, TT)] += _dot(kt1_scr[g], ds)

        @pl.when(i == 0)
        def _():
            dq_acc[...] = jnp.zeros_like(dq_acc)

        scores(i, True)

        def step(jq, c):
            grads(jq)
            scores(jq + 1, False)
            return c

        lax.fori_loop(i, nk - 1, step, 0)
        grads(nk - 1)
        for g in range(HG):
            dv_ref[rows[g], :] = dv_scr[g].T[:DH, :].astype(BF16)
            dk_t = dk_scr[g].T
            dk_ref[rows[g], :] = (dk_t[:DH, :] * LN2).astype(BF16)
            dck_ref[g] = dk_t[DH:DH + 1, :]

        @pl.when(i == nk - 1)
        def _():
            for g in range(HG):
                dq_ref[rows[g], :] = (dq_acc[g, :DH, :] * (DH ** -0.5)).astype(BF16)
                dcq_ref[g] = dq_acc[g, DH:DH + 1, :]

    assert HG == H
    head = lambda i: (0, 0)
    row = lambda i: (0, 0, 0)
    return pl.pallas_call(
        body, name="attn_bwd", grid=(nk,),
        in_specs=[
            pl.BlockSpec((DA, L), head),
            pl.BlockSpec((H, TT, KA), lambda i: (0, i, 0)),
            pl.BlockSpec((H, TT, DH), lambda i: (0, i, 0)),
            pl.BlockSpec((DA, TT), lambda i: (1, i)),
            pl.BlockSpec((DA, L), head),
            pl.BlockSpec((H, 1, L), row), pl.BlockSpec((H, 1, L), row), pl.BlockSpec((H, 1, L), row), _UNREAD,
        ],
        out_specs=[
            pl.BlockSpec((DA, L), head),
            pl.BlockSpec((DA, TT), lambda i: (0, i)),
            pl.BlockSpec((DA, TT), lambda i: (0, i)),
            pl.BlockSpec((H, 1, TT), lambda i: (0, 0, i)),
            pl.BlockSpec((H, 1, L), row),
        ],
        out_shape=[jax.ShapeDtypeStruct((DA, L), BF16), jax.ShapeDtypeStruct((DA, L), BF16),
                   jax.ShapeDtypeStruct((DA, L), BF16), jax.ShapeDtypeStruct((H, 1, L), F32),
                   jax.ShapeDtypeStruct((H, 1, L), F32)],
        scratch_shapes=[
            pltpu.VMEM((HG, DH + DF, L), F32),
            pltpu.VMEM((HG, DH + DF, TT), BF16),
            pltpu.VMEM((HG, TT, TT), F32), pltpu.VMEM((HG, TT, TT), F32),
            pltpu.VMEM((HG, TT, KA), F32), pltpu.VMEM((HG, TT, KA), F32)],
        compiler_params=_params(),
    )(proj_t, kaug, vtok, proj_t, do_t, lse, dd, cq, after)


def _fgate_bwd(dcq, dck, sg, L):
    def body(dcq_ref, dck_ref, sg_ref, df_ref, db_ref):
        dc = jnp.concatenate([dcq_ref[h] - dck_ref[h] for h in range(H)], axis=0)
        idx = lax.broadcasted_iota(jnp.int32, (H, L), 1)
        r = dc
        s = 1
        while s < L:
            r = r + jnp.where(idx + s < L, pltpu.roll(r, L - s, 1), 0.0)
            s *= 2
        df = r * sg_ref[...]
        db_ref[...] = jnp.broadcast_to(jnp.sum(df, axis=1, keepdims=True), (H, TB))
        df_ref[...] = jnp.concatenate([df, jnp.zeros((DF - H, L), F32)], axis=0).astype(BF16)

    return pl.pallas_call(
        body, name="fgate_bwd",
        out_shape=[jax.ShapeDtypeStruct((DF, L), BF16), jax.ShapeDtypeStruct((H, TB), F32)],
        compiler_params=pltpu.CompilerParams(vmem_limit_bytes=VMEM_LIMIT),
    )(dcq, dck, sg)


def _inproj_bwd_x(w, dq_t, dk_t, dv_t, dg5_t, df_t, dout, x, meta_full, norm_g, L, after):
    nj = L // TT
    seq = x.shape[0]

    def body(w_ref, dq_ref, dk_ref, dv_ref, dg5_ref, df_ref, dout_ref, xa_ref, xb_ref, xc_ref, meta_ref, g_ref, _,
             gx_ref, dmeta_ref, dg_ref, dh_scr, sems):
        j = pl.program_id(0)
        slot = j % 2

        def copy_out(step, slot_):
            first = pltpu.make_async_copy(dh_scr.at[slot_, pl.ds(TB, TT - TB)], gx_ref.at[pl.ds(0, TT - TB)],
                                          sems.at[slot_])
            later = pltpu.make_async_copy(dh_scr.at[slot_], gx_ref.at[pl.ds(step * TT - TB, TT)], sems.at[slot_])
            return first, later

        @pl.when(j == 0)
        def _():
            dg_ref[...] = jnp.zeros_like(dg_ref)

        du = _dot(dq_ref[...], w_ref[0:DA, :], TN_DIMS)
        du += _dot(dk_ref[...], w_ref[DA:2 * DA, :], TN_DIMS)
        du += _dot(dv_ref[...], w_ref[2 * DA:3 * DA, :], TN_DIMS)
        du += _dot(dg5_ref[...], w_ref[3 * DA:NSEC * DA, :], TN_DIMS)
        du += _dot(df_ref[...], w_ref[NSEC * DA:DPROJ, :], TN_DIMS)
        hb = _h_tile(j, xa_ref, xb_ref, xc_ref, meta_ref)
        r = lax.rsqrt(jnp.mean(hb * hb, axis=-1, keepdims=True) + EPS)
        hn = hb * r
        dg_ref[...] += jnp.sum(du * hn, axis=0, keepdims=True)
        gu = du * g_ref[...]
        dh = dout_ref[...] + r * gu - hn * (r * jnp.mean(gu * hn, axis=-1, keepdims=True))

        dh_scr[slot] = dh

        @pl.when(j == 0)
        def _():
            dmeta_ref[...] = dh[P0:TB, :]
            copy_out(0, 0)[0].start()

        @pl.when(j >= 1)
        def _():
            copy_out(j, slot)[1].start()

        @pl.when(j == 1)
        def _():
            copy_out(0, 0)[0].wait()

        @pl.when(j >= 2)
        def _():
            copy_out(j - 1, 1 - slot)[1].wait()

        @pl.when(j == nj - 1)
        def _():
            copy_out(j, slot)[0 if nj == 1 else 1].wait()

    blk = lambda rows: pl.BlockSpec((rows, TT), lambda j: (0, j))
    return pl.pallas_call(
        body, name="inproj_bwd_x", grid=(nj,),
        in_specs=[_full_spec((DPROJ, D)), blk(DA), blk(DA), blk(DA), blk(5 * DA), blk(DF),
                  pl.BlockSpec((TT, D), lambda j: (j, 0))] + _x_specs3()
                 + [_full_spec((NM, D)), _full_spec((1, D)), _UNREAD],
        out_specs=[pl.BlockSpec(memory_space=pl.ANY), _full_spec((NM, D)), _full_spec((1, D))],
        out_shape=[jax.ShapeDtypeStruct((seq, D), F32), jax.ShapeDtypeStruct((NM, D), F32),
                   jax.ShapeDtypeStruct((1, D), F32)],
        scratch_shapes=[pltpu.VMEM((2, TT, D), F32), pltpu.SemaphoreType.DMA((2,))],
        compiler_params=_params(),
    )(w, dq_t, dk_t, dv_t, dg5_t, df_t, dout, x, x, x, meta_full, norm_g, after)


def _inproj_bwd_w(u, dq_t, dk_t, dv_t, dg5_t, df_t, L):
    def body(u_ref, dq_hbm, dk_hbm, dv_hbm, dg5_ref, df_ref, dw_ref, dwf_ref, qkv_scr, sems):
        s = pl.program_id(0)
        u_all = u_ref[...]
        fetch = [pltpu.make_async_copy(src, qkv_scr.at[k], sems.at[k])
                 for k, src in enumerate((dq_hbm, dk_hbm, dv_hbm))]

        @pl.when(s == 0)
        def _():
            for cp in fetch:
                cp.start(priority=1)

        @pl.when(s < 5)
        def _():
            dw_ref[...] = _dot(dg5_ref[...], u_all)

        for k in range(3):
            @pl.when(s == 5 + k)
            def _(k=k):
                fetch[k].wait()
                dw_ref[...] = _dot(qkv_scr[k], u_all)

        @pl.when(s == NSEC - 1)
        def _():
            dwf_ref[...] = _dot(df_ref[...], u_all)

    once = lambda shape: pl.BlockSpec(shape, lambda s: (0, 0), pipeline_mode=pl.Buffered(1))
    any_spec = pl.BlockSpec(memory_space=pl.ANY)
    return pl.pallas_call(
        body, name="inproj_bwd_w", grid=(NSEC,),
        in_specs=[
            once((L, D)), any_spec, any_spec, any_spec,
            pl.BlockSpec((DA, L), lambda s: (jnp.minimum(s, 4), 0)),
            once((DF, L)),
        ],
        out_specs=[pl.BlockSpec((DA, D), lambda s: (jnp.where(s < 5, s + 3, s - 5), 0)), _full_spec((DF, D))],
        out_shape=[jax.ShapeDtypeStruct((NSEC * DA, D), F32), jax.ShapeDtypeStruct((DF, D), F32)],
        scratch_shapes=[pltpu.VMEM((3, DA, L), BF16), pltpu.SemaphoreType.DMA((3,))],
        compiler_params=_params(),
    )(u, dq_t, dk_t, dv_t, dg5_t, df_t)


def _adamw(w, g, m, v):
    m = ADAM_B1 * m + (1.0 - ADAM_B1) * g
    v = ADAM_B2 * v + (1.0 - ADAM_B2) * (g * g)
    m_hat = m / (1.0 - ADAM_B1 ** ADAM_STEP)
    v_hat = v / (1.0 - ADAM_B2 ** ADAM_STEP)
    delta = -ADAM_LR * (m_hat / (jnp.sqrt(v_hat) + ADAM_EPS) + ADAM_WD * w)
    return delta, m, v


def _adamw_big(own_in, land_in, own_out, land_out, w_in_t, m_in_t, v_in_t, w_out, m_out, v_out):
    cb = CB
    e_sh = D // NDEV
    in_shape = jax.ShapeDtypeStruct(w_in_t.shape, F32)
    out_shape = jax.ShapeDtypeStruct(w_out.shape, F32)

    def total(own_ref, land_ref, rows, chips):
        g = _pick_slab(0, own_ref, land_ref, rows, chips=chips).astype(F32)
        for j in range(1, own_ref.shape[0]):
            g = g + _pick_slab(j, own_ref, land_ref, rows, chips=chips).astype(F32)
        return g

    def body(oi_ref, li_ref, oo_ref, lo_ref, wi_ref, mi_ref, vi_ref, wo_ref, mo_ref, vo_ref,
             gi, di, mi, vi, go, do, mo, vo):
        g = total(oi_ref, li_ref, slice(0, WSHP), True)[:WSH]
        d, mn, vn = _adamw(wi_ref[...], g, mi_ref[...], vi_ref[...])
        gi[...], di[...], mi[...], vi[...] = g, d, mn, vn
        g = total(oo_ref, lo_ref, slice(0, e_sh), False)
        d, mn, vn = _adamw(wo_ref[0], g, mo_ref[0], vo_ref[0])
        go[0], do[0], mo[0], vo[0] = g, d, mn, vn

    slab = lambda n, rows: pl.BlockSpec((n, rows, cb), lambda i: (0, 0, i))
    ispec = pl.BlockSpec((WSH, cb), lambda i: (0, i))
    ospec = pl.BlockSpec((1, e_sh, cb), lambda i: (0, 0, i))
    return pl.pallas_call(
        body, name="adamw_big", grid=(D // cb,),
        in_specs=[slab(4, WSHP), slab(4, WSHP), slab(NDEV, e_sh), slab(NDEV, e_sh),
                  ispec, ispec, ispec, ospec, ospec, ospec],
        out_specs=[ispec] * 4 + [ospec] * 4, out_shape=[in_shape] * 4 + [out_shape] * 4,
        compiler_params=_params(),
    )(own_in, land_in, own_out, land_out, w_in_t, m_in_t, v_in_t, w_out, m_out, v_out)


F0 = 3 * DA


def _unshard_w_out(own, land):
    e_sh = D // NDEV

    def body(own_ref, land_ref, wo_ref):
        for j in range(NDEV):
            wo_ref[j * e_sh:(j + 1) * e_sh, :] = _pick_slab(j, own_ref, land_ref, slice(0, e_sh), per_peer=False)

    return pl.pallas_call(
        body, name="unshard_w_out", grid=(D // CB,),
        in_specs=[pl.BlockSpec((e_sh, CB), lambda i: (0, i)), pl.BlockSpec((NDEV, e_sh, CB), lambda i: (0, 0, i))],
        out_specs=pl.BlockSpec((D, CB), lambda i: (0, i)),
        out_shape=jax.ShapeDtypeStruct((D, D), BF16),
        compiler_params=_params(),
    )(own, land)


def _unshard_w_in(w_all, small_all, attn_gain, conv_gain):
    def body(w_ref, small_ref, ga_ref, gc_ref, wt_ref, meta_ref, cwb_ref, gab_ref, gcb_ref):
        i = pl.program_id(0)
        for k in range(CB // TB):
            meta_ref[:, k * TB:(k + 1) * TB] = small_ref[(CB // TB) * i + k, 0:NM, :]

        @pl.when(i == 0)
        def _():
            per_row = lambda line: jnp.broadcast_to(line, (TB, DA)).T
            cw = jnp.concatenate([small_ref[j, NM:NM + 3, 0:DH] for j in range(NDEV)], axis=1)
            for k in range(3):
                cwb_ref[k] = per_row(cw[k:k + 1, :])
            gab_ref[...] = per_row(ga_ref[...])
            gcb_ref[...] = per_row(gc_ref[...])

        def ref_rows(lo, hi):
            pieces, r = [], lo
            while r < hi:
                sh, off = divmod(r, WSH)
                n = min(hi - r, WSH - off)
                pieces.append(w_ref[sh, off:off + n, :])
                r += n
            return pieces

        for s in range(NSEC):
            lo = s * DA if s < 3 else s * DA + H
            wt_ref[s * DA:(s + 1) * DA, :] = jnp.concatenate(ref_rows(lo, lo + DA), axis=0)
        wt_ref[NSEC * DA:DPROJ, :] = jnp.concatenate(
            ref_rows(F0, F0 + H) + [jnp.zeros((DF - H, CB), BF16)], axis=0)

    return pl.pallas_call(
        body, name="unshard_w_in", grid=(D // CB,),
        in_specs=[pl.BlockSpec((NDEV, WSHP, CB), lambda i: (0, 0, i)), _full_spec(small_all.shape),
                  _full_spec((1, DA)), _full_spec((1, DA))],
        out_specs=[pl.BlockSpec((DPROJ, CB), lambda i: (0, i)), pl.BlockSpec((NM, CB), lambda i: (0, i)),
                   _full_spec((3, DA, TB)), _full_spec((DA, TB)), _full_spec((DA, TB))],
        out_shape=[jax.ShapeDtypeStruct((DPROJ, D), BF16), jax.ShapeDtypeStruct((NM, D), F32),
                   jax.ShapeDtypeStruct((3, DA, TB), F32), jax.ShapeDtypeStruct((DA, TB), F32),
                   jax.ShapeDtypeStruct((DA, TB), F32)],
        compiler_params=_params(),
    )(w_all, small_all, attn_gain, conv_gain)


def _shard_w_in_grads(dw_main, dw_f):
    def body(dm_ref, df_ref, p_ref):
        mc = lax.axis_index("c")

        def ref_rows(lo, hi):
            pieces, r = [], lo
            while r < hi:
                if r < F0:
                    n = min(hi, F0) - r
                    pieces.append(dm_ref[r:r + n, :])
                elif r < F0 + H:
                    n = min(hi, F0 + H) - r
                    pieces.append(df_ref[r - F0:r - F0 + n, :])
                else:
                    n = hi - r
                    pieces.append(dm_ref[r - H:r - H + n, :])
                r += n
            return pieces

        for i in range(NDEV):
            rows = jnp.concatenate(ref_rows(i * WSH, (i + 1) * WSH) + [jnp.zeros((WSHP - WSH, CB), F32)], axis=0)
            p_ref[i // 2 + jnp.where(mc == i % 2, 0, 4)] = rows.astype(BF16)

    col = lambda rows: pl.BlockSpec((rows, CB), lambda i: (0, i))
    return pl.pallas_call(
        body, name="shard_w_in_grads", grid=(D // CB,),
        in_specs=[col(NSEC * DA), col(DF)],
        out_specs=pl.BlockSpec((NDEV, WSHP, CB), lambda i: (0, 0, i)),
        out_shape=jax.ShapeDtypeStruct((NDEV, WSHP, D), BF16),
        compiler_params=_params(),
    )(dw_main, dw_f)


SMALL = ("norm_g", "final_norm_g", "attn_norm_g", "conv_norm_g", "b_f", "meta", "conv_w")


def _as_rows(x):
    return jnp.concatenate([x[:, r * TB:(r + 1) * TB] for r in range(x.shape[1] // TB)], axis=0)


def _as_line(rows):
    return jnp.concatenate([rows[r:r + 1, :] for r in range(rows.shape[0])], axis=1)


def _pad_rows(x, n=8):
    return jnp.concatenate([x, jnp.zeros((n - x.shape[0], x.shape[1]), F32)], axis=0)


def _tile_rows(a, rows, lanes=TB):
    a = a.reshape(rows, lanes)
    return jnp.pad(a, ((0, -rows % 8), (0, TB - lanes)))


def _pack_small_grads(dg_norm, dg_final, dga_p, dgc_p, dcw_p, db_b, dmeta, loss):
    def body(dgn_ref, dgf_ref, dga_ref, dgc_ref, dcw_ref, db_ref, dmeta_ref, loss_ref, out_ref):
        def lane_sums(p):
            return jnp.sum(p.T, axis=0, keepdims=True)

        lane = lax.broadcasted_iota(jnp.int32, (1, TB), 1)
        b_row = jnp.where(lane == H, loss_ref[...], 0.0)
        for h in range(H):
            b_row = b_row + jnp.where(lane == h, db_ref[h:h + 1, :], 0.0)
        common = jnp.concatenate([
            _as_rows(dgn_ref[...]), _as_rows(dgf_ref[...]), _pad_rows(_as_rows(lane_sums(dga_ref[...]))),
            _pad_rows(_as_rows(lane_sums(dgc_ref[...]))), _pad_rows(b_row)], axis=0)
        dcw = [lane_sums(dcw_ref[k]) for k in range(3)]
        for j in range(NDEV):
            cw = jnp.concatenate(
                [jnp.concatenate([r[:, j * DH:(j + 1) * DH], jnp.zeros((1, TB - DH), F32)], axis=1) for r in dcw],
                axis=0)
            out_ref[j] = jnp.concatenate([common, dmeta_ref[:, j * TB:(j + 1) * TB], _pad_rows(cw)], axis=0)

    return pl.pallas_call(
        body, name="pack_small_grads", out_shape=jax.ShapeDtypeStruct((NDEV, SROWS, TB), F32),
    )(dg_norm, dg_final, dga_p, dgc_p, dcw_p, db_b, dmeta, loss)


def _adamw_small(own, land, params):
    flat = [a for n in SMALL for a in params[n]]

    def body(*refs):
        own_ref, land_ref = refs[:2]
        ins = refs[2:2 + 3 * len(SMALL)]
        outs = refs[2 + 3 * len(SMALL):]
        g = _pick_slab(0, own_ref, land_ref, slice(0, SROWS))
        for j in range(1, NDEV):
            g = g + _pick_slab(j, own_ref, land_ref, slice(0, SROWS))
        grads = dict(
            norm_g=_as_line(g[0:8]), final_norm_g=_as_line(g[8:16]), attn_norm_g=_as_line(g[16:20]),
            conv_norm_g=_as_line(g[24:28]), b_f=g[32:33, :H], meta=g[40:56], conv_w=g[56:59, :DH][None])
        for i, n in enumerate(SMALL):
            w_ref, m_ref, v_ref = ins[3 * i:3 * i + 3]
            d, mn, vn = _adamw(w_ref[...], grads[n], m_ref[...], v_ref[...])
            for o_ref, val in zip(outs[4 * i:4 * i + 4], (grads[n], d, mn, vn)):
                o_ref[...] = val
        outs[-1][...] = g[32:33, H:H + 1]

    shapes = [jax.ShapeDtypeStruct(params[n][0].shape, F32) for n in SMALL for _ in range(4)]
    res = pl.pallas_call(
        body, name="adamw_small", out_shape=shapes + [jax.ShapeDtypeStruct((1, 1), F32)],
    )(own, land, *flat)
    return {n: res[4 * i:4 * i + 4] for i, n in enumerate(SMALL)}, res[-1]


def kernel(x, meta, norm_g, w_in, b_f, conv_w, attn_norm_g, conv_norm_g, w_out, final_norm_g, loss_target, m_meta, m_norm_g, m_w_in, m_b_f, m_conv_w, m_attn_norm_g, m_conv_norm_g, m_w_out, m_final_norm_g, v_meta, v_norm_g, v_w_in, v_b_f, v_conv_w, v_attn_norm_g, v_conv_norm_g, v_w_out, v_final_norm_g):
    seq = x.shape[1]
    L = seq + TB
    assert x.shape == (1, seq, D) and L % TT == 0 and w_in.shape == (1, D, WSH)
    x2 = x[0]
    tgt = loss_target[0]

    w_in_slab = jnp.pad(w_in[0].T, ((0, WSHP - WSH), (0, 0))).astype(BF16)
    w_out_slab = w_out[0].astype(BF16)
    meta_slab = jnp.concatenate([meta, _tile_rows(conv_w[0], 3, DH)], axis=0)
    wout_flight = _split_start(w_out_slab, "gather_w_out_start", per_peer=False)
    w_all, small_all = _all_gather([w_in_slab, meta_slab], "gather_w_in")

    w_t, meta_full, cw_b, ga_b, gcn_b = _unshard_w_in(w_all, small_all, attn_norm_g, conv_norm_g)

    u, proj_t, gate_t, f_t, ktok, vtok = _inproj_fwd(x2, meta_full, norm_g, w_t, L, after=wout_flight[4])
    cq, kaug, sg = _fgate_fwd(f_t, b_f.reshape(H, 1), ktok, L)
    o_t, lse = _attn_fwd(proj_t, kaug, cq, L)

    w_out_own, w_out_land = _split_wait(wout_flight, o_t, "gather_w_out_wait", per_peer=False)
    w_out_full = _unshard_w_out(w_out_own, w_out_land)
    dout, dw_out, loss_part, dg_final, do_t, dd, dg5_t, dga_p, dgc_p, dcw_p = _gate_outproj(
        o_t, gate_t, cw_b, ga_b, gcn_b, w_out_full, x2, meta_full, final_norm_g.reshape(1, D), tgt, L)
    dwo_flight = _split_start(dw_out.reshape(NDEV, D // NDEV, D), "exchange_dw_out_start", per_peer=True)
    dq_t, dk_t, dv_t, dck, dcq = _attn_bwd(proj_t, kaug, vtok, do_t, lse, dd, cq, L, after=dwo_flight[4])
    df_t, db_f = _fgate_bwd(dcq, dck, sg, L)
    dw_main, dw_f = _inproj_bwd_w(u, dq_t, dk_t, dv_t, dg5_t, df_t, L)
    dwi_parts = _shard_w_in_grads(dw_main, dw_f)
    dwi_chip = _pair_sum(dwi_parts, _pair_exchange(dwi_parts, "exchange_dw_in_pair"))
    dwi_flight = _split_start(dwi_chip, "exchange_dw_in_start", per_peer=True, chips=True)
    grad_x, dmeta, dg_norm = _inproj_bwd_x(
        w_t, dq_t, dk_t, dv_t, dg5_t, df_t, dout, x2, meta_full, norm_g, L, after=dwi_flight[4])
    small_parts = _pack_small_grads(dg_norm, dg_final, dga_p, dgc_p, dcw_p, db_f, dmeta, loss_part)
    small_flight = _split_start(small_parts, "exchange_small_start", per_peer=True)
    dwo_own, dwo_land = _split_wait(dwo_flight, small_flight[4], "exchange_dw_out_wait", per_peer=True)
    dwi_own, dwi_land = _split_wait(dwi_flight, dwo_land, "exchange_dw_in_wait", per_peer=True, chips=True)

    big_out = _adamw_big(dwi_own, dwi_land, dwo_own, dwo_land,
                         w_in[0].T, m_w_in[0].T, v_w_in[0].T, w_out, m_w_out, v_w_out)
    g_w_in, d_w_in, nm_w_in, nv_w_in = [a.T[None] for a in big_out[:4]]
    g_w_out, d_w_out, nm_w_out, nv_w_out = big_out[4:]
    sm_own, sm_land = _split_wait(small_flight, big_out[4], "exchange_small_wait", per_peer=True)
    line = lambda a: a.reshape(1, D)
    small, loss = _adamw_small(sm_own, sm_land, dict(
        norm_g=(norm_g, m_norm_g, v_norm_g),
        final_norm_g=(line(final_norm_g), line(m_final_norm_g), line(v_final_norm_g)),
        attn_norm_g=(attn_norm_g, m_attn_norm_g, v_attn_norm_g),
        conv_norm_g=(conv_norm_g, m_conv_norm_g, v_conv_norm_g),
        b_f=(b_f, m_b_f, v_b_f), meta=(meta, m_meta, v_meta), conv_w=(conv_w, m_conv_w, v_conv_w)))
    small["final_norm_g"] = [a.reshape(D) for a in small["final_norm_g"]]
    order = ("meta", "norm_g", "w_in", "b_f", "conv_w", "attn_norm_g", "conv_norm_g", "w_out", "final_norm_g")
    groups = []
    for k, (wi, wo) in enumerate(((g_w_in, g_w_out), (d_w_in, d_w_out), (nm_w_in, nm_w_out), (nv_w_in, nv_w_out))):
        d = dict({n: small[n][k] for n in SMALL}, w_in=wi, w_out=wo)
        groups.append([d[n] for n in order])
    return (loss[0, 0], grad_x[None], *groups[0], *groups[1], *groups[2], *groups[3])
```

```python
import jax
import jax.numpy as jnp
from jax import lax
from jax.experimental import pallas as pl
from jax.experimental.pallas import tpu as pltpu

F32 = jnp.float32
BF16 = jnp.bfloat16

D = 1024
DA = 512
H = 8
DH = 64
NM = 16
TB = 128
P0 = TB - NM
TT = 3 * TB
HG = 8
NDEV = 8
NSEC = 8
DF = 16
DPROJ = NSEC * DA + DF
WSH = 513
WSHP = 528
WROWS = WSHP + D // NDEV
SROWS = 64
EPS = 1e-6
NEG = -1e30
LOG2E = 1.4426950408889634
LN2 = 0.6931471805599453
QSCALE = DH ** -0.5 * LOG2E
KA = 128
CB = 256
VMEM_LIMIT = 56 * 1024 * 1024

ADAM_LR = 0.001
ADAM_B1 = 0.9
ADAM_B2 = 0.999
ADAM_EPS = 1e-08
ADAM_WD = 0.01
ADAM_STEP = 10

NT_DIMS = (((1,), (1,)), ((), ()))
TN_DIMS = (((0,), (0,)), ((), ()))
MESH = pl.DeviceIdType.MESH


def _params(n_axes=1, vmem=VMEM_LIMIT):
    return pltpu.CompilerParams(dimension_semantics=("arbitrary",) * n_axes, vmem_limit_bytes=vmem)


def _dot(a, b, dims=None):
    if dims is None:
        return jnp.dot(a, b, preferred_element_type=F32)
    return lax.dot_general(a, b, dims, preferred_element_type=F32)


def _my_place():
    return lax.axis_index("x"), lax.axis_index("y"), lax.axis_index("c")


def _all_gather(xs, name):
    n = len(xs)

    def body(*refs):
        x_refs, out_refs = refs[:n], refs[n:2 * n]
        send_sems, recv_sems, local_sems = refs[2 * n:]
        mx, my, mc = _my_place()

        def across(px, py, pc, axis_a):
            flip_x = pc if axis_a else 1 - pc
            return (px + flip_x) % 2, (py + 1 - flip_x) % 2, pc

        def idx(p):
            return 4 * p[0] + 2 * p[1] + p[2]

        me, sib = (mx, my, mc), (mx, my, 1 - mc)
        a_nbr, b_nbr = across(*me, True), across(*me, False)
        diag = across(*b_nbr, True)
        sib_a, sib_b = across(*sib, True), across(*sib, False)
        sib_diag = across(*sib_b, True)

        waits = []
        for t in range(n):
            out_ref = out_refs[t]

            def copy(k, block, to, src=None, out_ref=out_ref, t=t):
                return pltpu.make_async_remote_copy(
                    src_ref=out_ref.at[idx(block)] if src is None else src, dst_ref=out_ref.at[idx(block)],
                    send_sem=send_sems.at[7 * t + k], recv_sem=recv_sems.at[7 * t + k],
                    device_id=to, device_id_type=MESH)

            mine = pltpu.make_async_copy(x_refs[t], out_ref.at[idx(me)], local_sems.at[t])
            mine.start()
            started = [copy(0, me, sib, src=x_refs[t]), copy(1, me, a_nbr, src=x_refs[t]),
                       copy(2, me, b_nbr, src=x_refs[t])]
            for cp in started:
                cp.start()
            waits.append((copy, mine, started))
        relays = ((1, a_nbr, ((3, b_nbr), (4, sib))), (2, b_nbr, ((5, sib),)), (3, diag, ((6, sib),)))
        for landed, block, onward in relays:
            for copy, _, started in waits:
                copy(landed, block, me).wait_recv()
                for k, to in onward:
                    started.append(copy(k, block, to))
                    started[-1].start()
        for copy, mine, started in waits:
            for k, block in ((0, sib), (4, sib_a), (5, sib_b), (6, sib_diag)):
                copy(k, block, me).wait_recv()
            for cp in started:
                cp.wait_send()
            mine.wait()

    any_spec = pl.BlockSpec(memory_space=pl.ANY)
    return pl.pallas_call(
        body, name=name,
        out_shape=[jax.ShapeDtypeStruct((NDEV,) + x.shape, x.dtype) for x in xs],
        in_specs=[any_spec] * n, out_specs=[any_spec] * n,
        scratch_shapes=[pltpu.SemaphoreType.DMA((7 * n,)), pltpu.SemaphoreType.DMA((7 * n,)),
                        pltpu.SemaphoreType.DMA((n,))],
    )(*xs)


_HBM = pl.BlockSpec(memory_space=pltpu.HBM)
_UNREAD = pl.BlockSpec(memory_space=pl.ANY)
_SEM = pl.BlockSpec(memory_space=pltpu.SEMAPHORE)
_EFFECT = pltpu.SideEffectType.DATAFLOW_SIDE_EFFECTING


def _peer_of(m, place):
    mx, my, mc = place
    return ((1 - mx) if m & 4 else mx, (1 - my) if m & 2 else my, (1 - mc) if m & 1 else mc)


def _party(chips):
    if chips:
        return (lambda p: 2 * p[0] + p[1]), (2, 4, 6)
    return (lambda p: 4 * p[0] + 2 * p[1] + p[2]), tuple(range(1, NDEV))


def _split_copies(src_ref, land_ref, send_sems, recv_sems, per_peer, incoming, chips):
    place = _my_place()
    slot, masks = _party(chips)
    me = slot(place)
    out = []
    for k, m in enumerate(masks):
        there = _peer_of(m, place)
        peer = slot(there)
        src = (src_ref.at[me] if incoming else src_ref.at[peer]) if per_peer else src_ref
        out.append(pltpu.make_async_remote_copy(
            src_ref=src, dst_ref=land_ref.at[peer if incoming else me],
            send_sem=send_sems.at[k], recv_sem=recv_sems.at[k], device_id=there, device_id_type=MESH))
    return out


def _split_start(src, name, per_peer, chips=False):
    slab = src.shape[1:] if per_peer else src.shape
    n = len(_party(chips)[1])

    def body(src_ref, land_ref, send_sems, recv_sems, src_thru, land_thru, token):
        for cp in _split_copies(src_ref, land_ref, send_sems, recv_sems, per_peer, False, chips):
            cp.start()
        token[...] = jnp.zeros_like(token)

    return pl.pallas_call(
        body, name=name,
        out_shape=(pltpu.SemaphoreType.DMA((n,)), pltpu.SemaphoreType.DMA((n,)),
                   pltpu.HBM(src.shape, src.dtype), pltpu.HBM((n + 1,) + slab, src.dtype),
                   jax.ShapeDtypeStruct((8, TB), F32)),
        in_specs=(_HBM, _HBM), out_specs=(_SEM, _SEM, _HBM, _HBM, pl.BlockSpec(memory_space=pltpu.VMEM)),
        input_output_aliases={0: 2, 1: 3},
        compiler_params=pltpu.CompilerParams(has_side_effects=_EFFECT),
    )(pltpu.with_memory_space_constraint(src, pltpu.HBM),
      pltpu.with_memory_space_constraint(lax.empty((n + 1,) + slab, src.dtype), pltpu.HBM))


def _split_wait(handles, after, name, per_peer, chips=False):
    send_sems, recv_sems, src_thru, land_thru, _ = handles

    def body(src_ref, land_ref, send_sems, recv_sems, after_ref, src_out, land_out):
        for cp in _split_copies(src_ref, land_ref, send_sems, recv_sems, per_peer, False, chips):
            cp.wait_send()
        for cp in _split_copies(src_ref, land_ref, send_sems, recv_sems, per_peer, True, chips):
            cp.wait_recv()

    return pl.pallas_call(
        body, name=name,
        out_shape=(pltpu.HBM(src_thru.shape, src_thru.dtype), pltpu.HBM(land_thru.shape, land_thru.dtype)),
        in_specs=(_HBM, _HBM, _SEM, _SEM, pl.BlockSpec(memory_space=pl.ANY)), out_specs=(_HBM, _HBM),
        input_output_aliases={0: 0, 1: 1},
        compiler_params=pltpu.CompilerParams(has_side_effects=_EFFECT),
    )(src_thru, land_thru, send_sems, recv_sems, after)


def _pick_slab(j, own_ref, land_ref, rows, per_peer=True, chips=False):
    me = _party(chips)[0](_my_place())
    own = (lambda: own_ref[j, rows, :]) if per_peer else (lambda: own_ref[rows, :])
    return lax.cond(me == j, own, lambda: land_ref[j, rows, :])


def _pair_exchange(p, name):
    def body(p_ref, got_ref, send_sems, recv_sems):
        mx, my, mc = _my_place()
        copies = [pltpu.make_async_remote_copy(
            src_ref=p_ref.at[4 + q], dst_ref=got_ref.at[q], send_sem=send_sems.at[q],
            recv_sem=recv_sems.at[q], device_id=(mx, my, 1 - mc), device_id_type=MESH) for q in range(4)]
        for cp in copies:
            cp.start()
        for cp in copies:
            cp.wait_recv()
        for cp in copies:
            cp.wait_send()

    any_spec = pl.BlockSpec(memory_space=pl.ANY)
    return pl.pallas_call(
        body, name=name, out_shape=jax.ShapeDtypeStruct((4,) + p.shape[1:], p.dtype),
        in_specs=[any_spec], out_specs=any_spec,
        scratch_shapes=[pltpu.SemaphoreType.DMA((4,)), pltpu.SemaphoreType.DMA((4,))],
    )(p)


def _pair_sum(p, got):
    rows = p.shape[1]

    def body(p_ref, got_ref, out_ref):
        for q in range(4):
            out_ref[q] = (p_ref[q].astype(F32) + got_ref[q].astype(F32)).astype(BF16)

    blk = lambda n: pl.BlockSpec((n, rows, CB), lambda i: (0, 0, i))
    return pl.pallas_call(
        body, name="pair_sum", grid=(D // CB,), in_specs=[blk(4), blk(4)], out_specs=blk(4),
        out_shape=jax.ShapeDtypeStruct((4, rows, D), BF16), compiler_params=_params(),
    )(p, got)


def _h_block(t, x_ref, meta_ref):
    first = jnp.concatenate([jnp.zeros((P0, D), F32), meta_ref[...]], axis=0)
    return jnp.where(t == 0, first, x_ref[...])


def _x_specs3(tile=lambda j: j):
    return [pl.BlockSpec((TB, D), lambda j: (jnp.maximum(3 * tile(j) - 1, 0), 0)),
            pl.BlockSpec((TB, D), lambda j: (3 * tile(j), 0)),
            pl.BlockSpec((TB, D), lambda j: (3 * tile(j) + 1, 0))]


def _h_tile(j, xa_ref, xb_ref, xc_ref, meta_ref):
    first = jnp.concatenate([jnp.zeros((P0, D), F32), meta_ref[...]], axis=0)
    return jnp.concatenate([jnp.where(j == 0, first, xa_ref[...]), xb_ref[...], xc_ref[...]], axis=0)


def _full_spec(shape):
    return pl.BlockSpec(shape, lambda *_: (0,) * len(shape))


def _sigmoid(z):
    return 1.0 / (1.0 + jnp.exp(-z))


def _lane_tiles_sum(x):
    out = x[:, :TB]
    for i in range(1, x.shape[1] // TB):
        out = out + x[:, i * TB:(i + 1) * TB]
    return out


def _inproj_fwd(x, meta_full, norm_g, w_t, L, after):
    nj = L // TT

    def body(xa_ref, xb_ref, xc_ref, meta_ref, g_ref, w_ref, _, u_ref, proj_ref, gate_ref, f_ref, ktok_ref, vtok_ref):
        hb = _h_tile(pl.program_id(0), xa_ref, xb_ref, xc_ref, meta_ref)
        r = lax.rsqrt(jnp.mean(hb * hb, axis=-1, keepdims=True) + EPS)
        u = (hb * r * g_ref[...]).astype(BF16)
        u_ref[...] = u
        for s in range(NSEC):
            p = _dot(u, w_ref[s * DA:(s + 1) * DA, :], NT_DIMS)
            if s == 0:
                p = p * QSCALE
            if s in (1, 2):
                tok_ref = ktok_ref if s == 1 else vtok_ref
                for h in range(H):
                    tok_ref[h] = p[:, h * DH:(h + 1) * DH].astype(BF16)
            out_ref, s_out = (proj_ref, s) if s < 3 else (gate_ref, s - 3)
            out_ref[s_out * DA:(s_out + 1) * DA, :] = p.T.astype(BF16)
        f_ref[...] = _dot(w_ref[NSEC * DA:DPROJ, :], u, NT_DIMS)[:H]

    return pl.pallas_call(
        body, name="inproj_fwd", grid=(nj,),
        in_specs=_x_specs3() + [_full_spec((NM, D)), _full_spec((1, D)), _full_spec((DPROJ, D)), _UNREAD],
        out_specs=[
            pl.BlockSpec((TT, D), lambda t: (t, 0)),
            pl.BlockSpec((3 * DA, TT), lambda t: (0, t)),
            pl.BlockSpec((None, (NSEC - 3) * DA, TT), lambda t: (t, 0, 0)),
            pl.BlockSpec((H, TT), lambda t: (0, t)),
            pl.BlockSpec((H, TT, DH), lambda t: (0, t, 0)),
            pl.BlockSpec((H, TT, DH), lambda t: (0, t, 0)),
        ],
        out_shape=[
            jax.ShapeDtypeStruct((L, D), BF16),
            jax.ShapeDtypeStruct((3 * DA, L), BF16),
            jax.ShapeDtypeStruct((nj, (NSEC - 3) * DA, TT), BF16),
            jax.ShapeDtypeStruct((H, L), F32),
            jax.ShapeDtypeStruct((H, L, DH), BF16),
            jax.ShapeDtypeStruct((H, L, DH), BF16),
        ],
        compiler_params=_params(),
    )(x, x, x, meta_full, norm_g, w_t, after)


def _split3(x):
    hi = x.astype(BF16).astype(F32)
    r = x - hi
    mid = r.astype(BF16).astype(F32)
    return hi, mid, (r - mid).astype(BF16).astype(F32)


def _bias_rows(bias):
    one = jnp.ones((1, TT), F32)
    zero = jnp.zeros((1, TT), F32)
    parts = [zero] * 3 if bias is None else list(_split3(bias))
    return jnp.concatenate([one] * 3 + parts + [zero] * (DF - 6), axis=0).astype(BF16)


def _fgate_fwd(f_t, b_col, ktok, L):
    nb = L // TB

    def body(f_ref, b_ref, ktok_ref, cq_ref, kaug_ref, sg_ref, bias_scr):
        h = pl.program_id(0)

        @pl.when(h == 0)
        def _():
            z = f_ref[...] + b_ref[...]
            idx = lax.broadcasted_iota(jnp.int32, (H, L), 1)
            real = idx >= P0
            lf = jnp.where(real, jnp.minimum(z, 0.0) - jnp.log1p(jnp.exp(-jnp.abs(z))), 0.0)
            sg_ref[...] = jnp.where(real, 1.0 / (1.0 + jnp.exp(z)), 0.0)
            c = lf
            s = 1
            while s < L:
                c = c + jnp.where(idx >= s, pltpu.roll(c, s, 1), 0.0)
                s *= 2
            c = c * LOG2E
            for hh in range(H):
                cq_ref[hh] = c[hh:hh + 1, :]
            for part, val in enumerate(_split3(-jnp.where(real, c, -NEG))):
                for hh in range(H):
                    bias_scr[part * H + hh] = val[hh:hh + 1, :]

        lane = lax.broadcasted_iota(jnp.int32, (TB, KA), 1)
        head = jnp.zeros((DH, TB), F32)
        tail = jnp.concatenate([jnp.ones((3, TB), F32), jnp.zeros((KA - DH - 6, TB), F32)], axis=0)
        for b in range(nb):
            blk = slice(b * TB, (b + 1) * TB)
            cols = jnp.concatenate(
                [head] + [bias_scr[part * H + h, :, blk] for part in range(3)] + [tail], axis=0).T
            k = jnp.concatenate([ktok_ref[0, blk, :].astype(F32), jnp.zeros((TB, KA - DH), F32)], axis=1)
            kaug_ref[0, blk, :] = jnp.where(lane < DH, k, cols).astype(BF16)

    return pl.pallas_call(
        body, name="fgate_fwd", grid=(H,),
        in_specs=[_full_spec((H, L)), _full_spec((H, 1)), pl.BlockSpec((1, L, DH), lambda h: (h, 0, 0))],
        out_specs=[_full_spec((H, 1, L)), pl.BlockSpec((1, L, KA), lambda h: (h, 0, 0)), _full_spec((H, L))],
        out_shape=[
            jax.ShapeDtypeStruct((H, 1, L), F32),
            jax.ShapeDtypeStruct((H, L, KA), BF16),
            jax.ShapeDtypeStruct((H, L), F32),
        ],
        scratch_shapes=[pltpu.VMEM((3 * H, 1, L), F32)],
        compiler_params=_params(),
    )(f_t, b_col, ktok)


def _causal_mask():
    r = lax.broadcasted_iota(jnp.int32, (TT, TT), 0)
    c = lax.broadcasted_iota(jnp.int32, (TT, TT), 1)
    return r <= c


def _attn_fwd(proj_t, kaug, cq, L):
    nq = L // TT

    def body(q_ref, qn_ref, kaug_ref, v_ref, cq_ref, o_ref, lse_ref,
             qa_scr, s_scr, cmax_scr, m_scr, p_scr, alpha_scr, acc_scr):
        j = pl.program_id(0)
        rows = [slice(g * DH, (g + 1) * DH) for g in range(HG)]
        ones = jnp.ones((DF, TT), BF16)

        def load_queries(ref):
            for g in range(HG):
                qa_scr[g] = jnp.concatenate(
                    [ref[rows[g], :], _bias_rows(None), jnp.zeros((KA - DH - DF, TT), BF16)], axis=0)

        def scores(kt, masked):
            k_off = pl.multiple_of(kt * TT, TT)
            for g in range(HG):
                s = _dot(kaug_ref[g, pl.ds(k_off, TT), :], qa_scr[g])
                if masked:
                    s = jnp.where(_causal_mask(), s, NEG)
                s_scr[g] = s
                cmax_scr[g] = jnp.max(s, axis=0, keepdims=True)

        def softmax():
            for g in range(HG):
                m_old = m_scr[g]
                m_new = jnp.maximum(m_old, cmax_scr[g])
                alpha_scr[g] = jnp.exp2(m_old - m_new)
                p_scr[g] = jnp.exp2(s_scr[g] - m_new).astype(BF16)
                m_scr[g] = m_new

        def weighted_sum(kt):
            k_off = pl.multiple_of(kt * TT, TT)
            for g in range(HG):
                v1 = jnp.concatenate([v_ref[rows[g], pl.ds(k_off, TT)], ones], axis=0)
                acc_scr[g] = alpha_scr[g] * acc_scr[g] + _dot(v1, p_scr[g])

        @pl.when(j == 0)
        def _():
            load_queries(q_ref)
            scores(0, True)

        m_scr[...] = jnp.full_like(m_scr, NEG)
        acc_scr[...] = jnp.zeros_like(acc_scr)

        @pl.when(j >= 1)
        def _():
            softmax()
            scores(j - 1, False)

        def step(i, c):
            weighted_sum(j - i + 1)
            softmax()
            scores(j - i - 1, False)
            return c

        lax.fori_loop(1, j, step, 0)

        def drain(second_last, next_tile):
            if second_last:
                weighted_sum(1)
            softmax()
            if next_tile:
                load_queries(qn_ref)
                scores(j + 1, True)
            weighted_sum(0)

        @pl.when(j == 0)
        def _():
            drain(False, nq > 1)

        @pl.when((j >= 1) & (j < nq - 1))
        def _():
            drain(True, True)

        @pl.when((j >= 1) & (j == nq - 1))
        def _():
            drain(True, False)

        for g in range(HG):
            l = acc_scr[g, DH:DH + 1, :]
            o_ref[rows[g], :] = acc_scr[g, :DH, :] * (1.0 / l)
            lse_ref[g] = m_scr[g] + jnp.log2(l) + cq_ref[g]

    assert HG == H
    return pl.pallas_call(
        body, name="attn_fwd", grid=(nq,),
        in_specs=[
            pl.BlockSpec((DA, TT), lambda j: (0, j)),
            pl.BlockSpec((DA, TT), lambda j: (0, jnp.minimum(j + 1, nq - 1))),
            pl.BlockSpec((H, L, KA), lambda j: (0, 0, 0)),
            pl.BlockSpec((DA, L), lambda j: (2, 0)),
            pl.BlockSpec((H, 1, TT), lambda j: (0, 0, j)),
        ],
        out_specs=[
            pl.BlockSpec((None, DA, TT), lambda j: (j, 0, 0)),
            pl.BlockSpec((H, 1, TT), lambda j: (0, 0, j)),
        ],
        out_shape=[jax.ShapeDtypeStruct((nq, DA, TT), F32), jax.ShapeDtypeStruct((H, 1, L), F32)],
        scratch_shapes=[pltpu.VMEM((HG, KA, TT), BF16), pltpu.VMEM((HG, TT, TT), F32), pltpu.VMEM((HG, 1, TT), F32),
                        pltpu.VMEM((HG, 1, TT), F32), pltpu.VMEM((HG, TT, TT), BF16), pltpu.VMEM((HG, 1, TT), F32),
                        pltpu.VMEM((HG, DH + DF, TT), F32)],
        compiler_params=_params(),
    )(proj_t, proj_t, kaug, proj_t, cq)


def _gate_group(rows, o_ref, za_ref, gb_ref, gc_ref, xc_ref, zc_ref, gcp_ref, xcp_ref, cw_ref, ga_ref, gcn_ref, first):
    n_rep = TT // TB
    f32 = lambda r: r[rows, :].astype(F32)
    o, za, gb, gc, xc, zc = o_ref[rows, :], f32(za_ref), f32(gb_ref), f32(gc_ref), f32(xc_ref), f32(zc_ref)
    a = gc * xc
    a_prev = jnp.where(first, 0.0, f32(gcp_ref) * f32(xcp_ref))
    full = jnp.concatenate([a_prev, a], axis=1)
    a1 = pltpu.roll(full, 1, 1)[:, TB:]
    a2 = pltpu.roll(full, 2, 1)[:, TB:]
    w0 = jnp.tile(cw_ref[0, rows, :], (1, n_rep))
    w1 = jnp.tile(cw_ref[1, rows, :], (1, n_rep))
    w2 = jnp.tile(cw_ref[2, rows, :], (1, n_rep))
    cv = w0 * a2 + w1 * a1 + w2 * a
    e = gb * cv
    rc = lax.rsqrt(jnp.mean(e * e, axis=0, keepdims=True) + EPS)
    ec = e * rc
    ra = lax.rsqrt(jnp.mean(o * o, axis=0, keepdims=True) + EPS)
    oa = o * ra
    g_a = jnp.tile(ga_ref[rows, :], (1, n_rep))
    g_c = jnp.tile(gcn_ref[rows, :], (1, n_rep))
    sa = _sigmoid(za)
    sc = _sigmoid(zc)
    return dict(o=o, za=za, gb=gb, gc=gc, xc=xc, zc=zc, a=a, a1=a1, a2=a2, w0=w0, w1=w1, w2=w2, cv=cv, e=e,
                rc=rc, ec=ec, ra=ra, oa=oa, g_a=g_a, g_c=g_c, sa=sa, sc=sc)


def _gate_specs(tile):
    halo = pl.BlockSpec((None, 2 * DA, TB),
                        lambda i: (jnp.maximum(tile(i) - 1, 0), 1, TT // TB - 1))
    return [pl.BlockSpec((None, DA, TT), lambda i: (tile(i), 0, 0)),
            pl.BlockSpec((None, 5 * DA, TT), lambda i: (tile(i), 0, 0)), halo,
            _full_spec((3, DA, TB)), _full_spec((DA, TB)), _full_spec((DA, TB))]


def _gate_views(g5_ref, halo_ref):
    return [g5_ref.at[pl.ds(s * DA, DA)] for s in range(5)] + [halo_ref.at[pl.ds(s * DA, DA)] for s in range(2)]


def _gate_outproj(o_t, gate_t, cw_b, ga_b, gcn_b, w_own, w_land, x, meta_full, fng, target, L):
    nj = L // TT
    rp = NM
    n_bwd = 8
    cb = D // 4
    n_ring = 5
    e_sh = D // NDEV
    assert P0 % rp == 0 and TB % rp == 0 and (TT // rp) % n_bwd == 0 and H == n_bwd

    def body(o_hbm, g5_hbm, halo_ref, cw_ref, ga_ref, gcn_ref, halo2_ref,
             wown_ref, wland_ref, xa_ref, xb_ref, xc_ref, meta_ref, g_ref, ta_ref, tb_ref, tc_ref,
             dout_ref, dwb_ref, loss_ref, dg_ref, do_ref, dd_ref, dg5_ref, dga_ref, dgc_ref, dcw_ref,
             dw_ref, o_scr, db_new, db_old, mix_new, mix_old, dmix_new, dmix_old, sq_acc, dg_acc, carry_ref,
             o_ring, g5_ring, ring_sems, w_ref):
        t = pl.program_id(0)

        def fetch(step):
            tile, slot = nj - 1 - step, step % n_ring
            return (pltpu.make_async_copy(o_hbm.at[tile], o_ring.at[slot], ring_sems.at[0, slot]),
                    pltpu.make_async_copy(g5_hbm.at[tile], g5_ring.at[slot], ring_sems.at[1, slot]))

        @pl.when(t == 0)
        def _():
            for step in range(2):
                for cp in fetch(step):
                    cp.start()

        @pl.when(t + 2 < nj)
        def _():
            for cp in fetch(t + 2):
                cp.start()

        @pl.when(t < nj)
        def _():
            for cp in fetch(t):
                cp.wait()

        slot_a, slot_c = t % n_ring, (t + n_ring - 2) % n_ring
        o_ref, o2_ref = o_ring.at[slot_a], o_ring.at[slot_c]
        za_ref, gb_ref, gc_ref, xcv_ref, zc_ref, gcp_ref, xcp_ref = _gate_views(g5_ring.at[slot_a], halo_ref)
        za2_ref, gb2_ref, gc2_ref, xcv2_ref, zc2_ref, gcp2_ref, xcp2_ref = _gate_views(g5_ring.at[slot_c], halo2_ref)
        first_a = t == nj - 1
        first_c = t == nj + 1

        def gate_rows(h):
            rows = slice(h * DH, (h + 1) * DH)
            g = _gate_group(rows, o_ref, za_ref, gb_ref, gc_ref, xcv_ref, zc_ref, gcp_ref, xcp_ref,
                            cw_ref, ga_ref, gcn_ref, first_a)
            mix_new[rows, :] = (g["oa"] * g["g_a"] * (g["za"] * g["sa"])).astype(BF16)
            mix_new[DA + h * DH:DA + (h + 1) * DH, :] = (g["ec"] * g["g_c"] * (g["zc"] * g["sc"])).astype(BF16)

        def loss_rows(c):
            blk = c // (TB // rp)
            rows, out_rows = pl.ds((c % (TB // rp)) * rp, rp), pl.ds(c * rp, rp)
            h = (xa_ref, xb_ref, xc_ref)[blk][rows, :]
            if blk == 0:
                first = meta_ref[...] if c == P0 // rp else jnp.zeros((rp, D), F32)
                h = jnp.where(first_a, first, h)
            o = o_scr[out_rows, :] + h
            r = lax.rsqrt(jnp.mean(o * o, axis=-1, keepdims=True) + EPS)
            orn = o * r
            g = g_ref[...]
            diff = orn * g - (ta_ref, tb_ref, tc_ref)[blk][rows, :]
            if blk == 0:
                diff = diff * jnp.where(first_a, 0.0, 1.0)
            gy = diff * (g * (1.0 / D))
            dout = r * (gy - orn * jnp.mean(gy * orn, axis=-1, keepdims=True))
            dout_ref[out_rows, :] = dout
            db_new[out_rows, :] = dout.astype(BF16)
            sq, go = diff * diff, diff * orn
            sq_acc[...] += sq[:8] + sq[8:]
            dg_acc[...] += go[:8] + go[8:]

        def backward_cols(n):
            if n < 4:
                cols = slice(n * cb, (n + 1) * cb)
                dmix_new[cols, :] = _dot(db_old[...], w_ref[cols, :], NT_DIMS).T.astype(BF16)
            else:
                cols = slice((n - 4) * cb, (n - 3) * cb)
                dw_ref[:, cols] += _dot(mix_old[...], db_old[:, cols])

        def gate_bwd_rows(h):
            rows = slice(h * DH, (h + 1) * DH)
            sec = lambda s: slice(s * DA + h * DH, s * DA + (h + 1) * DH)
            g = _gate_group(rows, o2_ref, za2_ref, gb2_ref, gc2_ref, xcv2_ref, zc2_ref, gcp2_ref, xcp2_ref,
                            cw_ref, ga_ref, gcn_ref, first_c)
            o, za, gb, gc, xc, zc, sa, sc = (g[n] for n in ("o", "za", "gb", "gc", "xc", "zc", "sa", "sc"))
            dya = dmix_old[rows, :].astype(F32)
            dyc = dmix_old[DA + h * DH:DA + (h + 1) * DH, :].astype(F32)

            dn = dya * (za * sa)
            dg5_ref[sec(0), :] = (dya * (g["oa"] * g["g_a"]) * (sa * (1.0 + za * (1.0 - sa)))).astype(BF16)
            dga_ref[rows, :] += _lane_tiles_sum(dn * g["oa"])
            dng = dn * g["g_a"]
            mean_a = jnp.mean(dng * g["oa"], axis=0, keepdims=True)
            do = (dng - g["oa"] * mean_a) * g["ra"]
            do_ref[rows, :] = do.astype(BF16)
            dd_ref[h] = jnp.sum(do * o, axis=0, keepdims=True)

            dnc = dyc * (zc * sc)
            dg5_ref[sec(4), :] = (dyc * (g["ec"] * g["g_c"]) * (sc * (1.0 + zc * (1.0 - sc)))).astype(BF16)
            dgc_ref[rows, :] += _lane_tiles_sum(dnc * g["ec"])
            dncg = dnc * g["g_c"]
            mean_c = jnp.mean(dncg * g["ec"], axis=0, keepdims=True)
            de = (dncg - g["ec"] * mean_c) * g["rc"]
            dg5_ref[sec(1), :] = (de * g["cv"]).astype(BF16)
            dcv = de * gb
            full = jnp.concatenate([dcv, carry_ref[rows, :]], axis=1)
            d1 = pltpu.roll(full, TT + TB - 1, 1)[:, :TT]
            d2 = pltpu.roll(full, TT + TB - 2, 1)[:, :TT]
            carry_ref[rows, :] = dcv[:, :TB]
            da = g["w2"] * dcv + g["w1"] * d1 + g["w0"] * d2
            dg5_ref[sec(2), :] = (da * xc).astype(BF16)
            dg5_ref[sec(3), :] = (da * gc).astype(BF16)
            dcw_ref[0, rows, :] += _lane_tiles_sum(dcv * g["a2"])
            dcw_ref[1, rows, :] += _lane_tiles_sum(dcv * g["a1"])
            dcw_ref[2, rows, :] += _lane_tiles_sum(dcv * g["a"])

        def step(a, b, c):
            half = H // 2
            for h in range(H):
                if a:
                    gate_rows(h)
                if c and h < half:
                    gate_bwd_rows(h)
                if b and h % 2 == 1:
                    backward_cols(h // 2)
            if a:
                o_scr[...] = _dot(mix_new[...], w_ref[...], TN_DIMS)
            per = TT // rp // n_bwd
            for k in range(n_bwd):
                if a:
                    for piece in range(per * k, per * (k + 1)):
                        loss_rows(piece)
                if c and k % 2 == 0:
                    gate_bwd_rows(half + k // 2)
                if b and k % 2 == 1:
                    backward_cols(n_bwd // 2 + k // 2)
            if a:
                db_old[...] = db_new[...]
                mix_old[...] = mix_new[...]
            if b:
                dmix_old[...] = dmix_new[...]

        @pl.when(t == 0)
        def _():
            dw_ref[...] = jnp.zeros_like(dw_ref)
            sq_acc[...] = jnp.zeros_like(sq_acc)
            dg_acc[...] = jnp.zeros_like(dg_acc)
            carry_ref[...] = jnp.zeros_like(carry_ref)
            dga_ref[...] = jnp.zeros_like(dga_ref)
            dgc_ref[...] = jnp.zeros_like(dgc_ref)
            dcw_ref[...] = jnp.zeros_like(dcw_ref)
            for j in range(NDEV):
                w_ref[j * e_sh:(j + 1) * e_sh, :] = _pick_slab(j, wown_ref, wland_ref, slice(0, e_sh), per_peer=False)
            step(True, False, False)

        @pl.when(t == 1)
        def _():
            step(True, True, False)

        @pl.when((t >= 2) & (t < nj))
        def _():
            step(True, True, True)

        @pl.when(t == nj)
        def _():
            step(False, True, True)
            dwb_ref[...] = dw_ref[...].astype(BF16)
            loss_ref[...] = jnp.sum(sq_acc[...], keepdims=True) * (0.5 / D)
            dg_ref[...] = jnp.sum(dg_acc[...], axis=0, keepdims=True) * (1.0 / D)

        @pl.when(t == nj + 1)
        def _():
            step(False, False, True)

    assert nj >= 2
    tile_a = lambda t: jnp.clip(nj - 1 - t, 0, nj - 1)
    tile_c = lambda t: jnp.clip(nj + 1 - t, 0, nj - 1)
    at_c = lambda shape: pl.BlockSpec(shape, lambda t: (0,) * (len(shape) - 1) + (tile_c(t),))
    return pl.pallas_call(
        body, name="gate_outproj", grid=(nj + 2,),
        in_specs=[pl.BlockSpec(memory_space=pl.ANY)] * 2 + _gate_specs(tile_a)[2:] + _gate_specs(tile_c)[2:3]
                 + [_full_spec((e_sh, D)), _full_spec((NDEV, e_sh, D))] + _x_specs3(tile_a)
                 + [_full_spec((NM, D)), _full_spec((1, D))] + _x_specs3(tile_a),
        out_specs=[pl.BlockSpec((TT, D), lambda t: (tile_a(t), 0)), _full_spec((D, D)), _full_spec((1, 1)),
                   _full_spec((1, D)), at_c((DA, TT)), at_c((H, 1, TT)), at_c((5 * DA, TT)),
                   _full_spec((DA, TB)), _full_spec((DA, TB)), _full_spec((3, DA, TB))],
        out_shape=[jax.ShapeDtypeStruct((L, D), F32), jax.ShapeDtypeStruct((D, D), BF16),
                   jax.ShapeDtypeStruct((1, 1), F32), jax.ShapeDtypeStruct((1, D), F32),
                   jax.ShapeDtypeStruct((DA, L), BF16),
                   jax.ShapeDtypeStruct((H, 1, L), F32),
                   jax.ShapeDtypeStruct((5 * DA, L), BF16),
                   jax.ShapeDtypeStruct((DA, TB), F32),
                   jax.ShapeDtypeStruct((DA, TB), F32),
                   jax.ShapeDtypeStruct((3, DA, TB), F32)],
        scratch_shapes=[pltpu.VMEM((D, D), F32), pltpu.VMEM((TT, D), F32), pltpu.VMEM((TT, D), BF16),
                        pltpu.VMEM((TT, D), BF16), pltpu.VMEM((D, TT), BF16), pltpu.VMEM((D, TT), BF16),
                        pltpu.VMEM((D, TT), BF16), pltpu.VMEM((D, TT), BF16),
                        pltpu.VMEM((8, D), F32), pltpu.VMEM((8, D), F32), pltpu.VMEM((DA, TB), F32),
                        pltpu.VMEM((n_ring, DA, TT), F32), pltpu.VMEM((n_ring, 5 * DA, TT), BF16),
                        pltpu.SemaphoreType.DMA((2, n_ring)), pltpu.VMEM((D, D), BF16)],
        compiler_params=_params(),
    )(o_t, gate_t, gate_t, cw_b, ga_b, gcn_b, gate_t,
      w_own, w_land, x, x, x, meta_full, fng, target, target, target)


def _attn_bwd(proj_t, kaug, vtok, do_t, lse, dd, cq, L, after):
    nk = L // TT

    def body(q_ref, kaug_ref, vtok_ref, kt_ref, do_ref, lse_ref, dd_ref, cq_ref, _,
             dq_ref, dk_ref, dv_ref, dck_ref, dcq_ref, dq_acc, kt1_scr, s_scr, dp_scr, dv_scr, dk_scr):
        i = pl.program_id(0)
        rows = [slice(g * DH, (g + 1) * DH) for g in range(HG)]
        ones = jnp.ones((DF, TT), BF16)
        zpad = jnp.zeros((KA - DH - DF, TT), BF16)
        for g in range(HG):
            kt1_scr[g] = jnp.concatenate([kt_ref[rows[g], :], ones], axis=0)
        dv_scr[...] = jnp.zeros_like(dv_scr)
        dk_scr[...] = jnp.zeros_like(dk_scr)

        def q_rows(g, q_off):
            bias = cq_ref[g, :, pl.ds(q_off, TT)] - lse_ref[g, :, pl.ds(q_off, TT)]
            return jnp.concatenate([q_ref[rows[g], pl.ds(q_off, TT)], _bias_rows(bias)], axis=0)

        def scores(jq, masked):
            q_off = pl.multiple_of(jq * TT, TT)
            for g in range(HG):
                s = _dot(kaug_ref[g], jnp.concatenate([q_rows(g, q_off), zpad], axis=0))
                if masked:
                    s = jnp.where(_causal_mask(), s, NEG)
                s_scr[g] = s
                dp_scr[g] = _dot(vtok_ref[g], do_ref[rows[g], pl.ds(q_off, TT)])

        def grads(jq):
            q_off = pl.multiple_of(jq * TT, TT)
            for g in range(HG):
                p = jnp.exp2(s_scr[g])
                ds = (p * (dp_scr[g] - dd_ref[g, :, pl.ds(q_off, TT)])).astype(BF16)
                do1 = jnp.concatenate([do_ref[rows[g], pl.ds(q_off, TT)], jnp.zeros((KA - DH, TT), BF16)], axis=0)
                q1 = jnp.concatenate([q_rows(g, q_off), zpad], axis=0)
                dv_scr[g] += _dot(p.astype(BF16), do1, NT_DIMS)
                dk_scr[g] += _dot(ds, q1, NT_DIMS)
                dq_acc[g, :, pl.ds(q_off, TT)] += _dot(kt1_scr[g], ds)

        @pl.when(i == 0)
        def _():
            dq_acc[...] = jnp.zeros_like(dq_acc)

        scores(i, True)

        def step(jq, c):
            grads(jq)
            scores(jq + 1, False)
            return c

        lax.fori_loop(i, nk - 1, step, 0)
        grads(nk - 1)
        for g in range(HG):
            dv_ref[rows[g], :] = dv_scr[g].T[:DH, :].astype(BF16)
            dk_t = dk_scr[g].T
            dk_ref[rows[g], :] = (dk_t[:DH, :] * LN2).astype(BF16)
            dck_ref[g] = dk_t[DH:DH + 1, :]

        @pl.when(i == nk - 1)
        def _():
            for g in range(HG):
                dq_ref[rows[g], :] = (dq_acc[g, :DH, :] * (DH ** -0.5)).astype(BF16)
                dcq_ref[g] = dq_acc[g, DH:DH + 1, :]

    assert HG == H
    head = lambda i: (0, 0)
    row = lambda i: (0, 0, 0)
    return pl.pallas_call(
        body, name="attn_bwd", grid=(nk,),
        in_specs=[
            pl.BlockSpec((DA, L), head),
            pl.BlockSpec((H, TT, KA), lambda i: (0, i, 0)),
            pl.BlockSpec((H, TT, DH), lambda i: (0, i, 0)),
            pl.BlockSpec((DA, TT), lambda i: (1, i)),
            pl.BlockSpec((DA, L), head),
            pl.BlockSpec((H, 1, L), row), pl.BlockSpec((H, 1, L), row), pl.BlockSpec((H, 1, L), row), _UNREAD,
        ],
        out_specs=[
            pl.BlockSpec((DA, L), head),
            pl.BlockSpec((DA, TT), lambda i: (0, i)),
            pl.BlockSpec((DA, TT), lambda i: (0, i)),
            pl.BlockSpec((H, 1, TT), lambda i: (0, 0, i)),
            pl.BlockSpec((H, 1, L), row),
        ],
        out_shape=[jax.ShapeDtypeStruct((DA, L), BF16), jax.ShapeDtypeStruct((DA, L), BF16),
                   jax.ShapeDtypeStruct((DA, L), BF16), jax.ShapeDtypeStruct((H, 1, L), F32),
                   jax.ShapeDtypeStruct((H, 1, L), F32)],
        scratch_shapes=[
            pltpu.VMEM((HG, DH + DF, L), F32),
            pltpu.VMEM((HG, DH + DF, TT), BF16),
            pltpu.VMEM((HG, TT, TT), F32), pltpu.VMEM((HG, TT, TT), F32),
            pltpu.VMEM((HG, TT, KA), F32), pltpu.VMEM((HG, TT, KA), F32)],
        compiler_params=_params(),
    )(proj_t, kaug, vtok, proj_t, do_t, lse, dd, cq, after)


def _fgate_bwd(dcq, dck, sg, L):
    def body(dcq_ref, dck_ref, sg_ref, df_ref, db_ref):
        dc = jnp.concatenate([dcq_ref[h] - dck_ref[h] for h in range(H)], axis=0)
        idx = lax.broadcasted_iota(jnp.int32, (H, L), 1)
        r = dc
        s = 1
        while s < L:
            r = r + jnp.where(idx + s < L, pltpu.roll(r, L - s, 1), 0.0)
            s *= 2
        df = r * sg_ref[...]
        db_ref[...] = jnp.broadcast_to(jnp.sum(df, axis=1, keepdims=True), (H, TB))
        df_ref[...] = jnp.concatenate([df, jnp.zeros((DF - H, L), F32)], axis=0).astype(BF16)

    return pl.pallas_call(
        body, name="fgate_bwd",
        out_shape=[jax.ShapeDtypeStruct((DF, L), BF16), jax.ShapeDtypeStruct((H, TB), F32)],
        compiler_params=pltpu.CompilerParams(vmem_limit_bytes=VMEM_LIMIT),
    )(dcq, dck, sg)


def _inproj_bwd_x(w, dq_t, dk_t, dv_t, dg5_t, df_t, dout, x, meta_full, norm_g, L, after):
    nj = L // TT
    seq = x.shape[0]

    def body(w_ref, dq_ref, dk_ref, dv_ref, dg5_ref, df_ref, dout_ref, xa_ref, xb_ref, xc_ref, meta_ref, g_ref, _,
             gx_ref, dmeta_ref, dg_ref, dh_scr, sems):
        j = pl.program_id(0)
        slot = j % 2

        def copy_out(step, slot_):
            first = pltpu.make_async_copy(dh_scr.at[slot_, pl.ds(TB, TT - TB)], gx_ref.at[pl.ds(0, TT - TB)],
                                          sems.at[slot_])
            later = pltpu.make_async_copy(dh_scr.at[slot_], gx_ref.at[pl.ds(step * TT - TB, TT)], sems.at[slot_])
            return first, later

        @pl.when(j == 0)
        def _():
            dg_ref[...] = jnp.zeros_like(dg_ref)

        du = _dot(dq_ref[...], w_ref[0:DA, :], TN_DIMS)
        du += _dot(dk_ref[...], w_ref[DA:2 * DA, :], TN_DIMS)
        du += _dot(dv_ref[...], w_ref[2 * DA:3 * DA, :], TN_DIMS)
        du += _dot(dg5_ref[...], w_ref[3 * DA:NSEC * DA, :], TN_DIMS)
        du += _dot(df_ref[...], w_ref[NSEC * DA:DPROJ, :], TN_DIMS)
        hb = _h_tile(j, xa_ref, xb_ref, xc_ref, meta_ref)
        r = lax.rsqrt(jnp.mean(hb * hb, axis=-1, keepdims=True) + EPS)
        hn = hb * r
        dg_ref[...] += jnp.sum(du * hn, axis=0, keepdims=True)
        gu = du * g_ref[...]
        dh = dout_ref[...] + r * gu - hn * (r * jnp.mean(gu * hn, axis=-1, keepdims=True))

        dh_scr[slot] = dh

        @pl.when(j == 0)
        def _():
            dmeta_ref[...] = dh[P0:TB, :]
            copy_out(0, 0)[0].start()

        @pl.when(j >= 1)
        def _():
            copy_out(j, slot)[1].start()

        @pl.when(j == 1)
        def _():
            copy_out(0, 0)[0].wait()

        @pl.when(j >= 2)
        def _():
            copy_out(j - 1, 1 - slot)[1].wait()

        @pl.when(j == nj - 1)
        def _():
            copy_out(j, slot)[0 if nj == 1 else 1].wait()

    blk = lambda rows: pl.BlockSpec((rows, TT), lambda j: (0, j))
    return pl.pallas_call(
        body, name="inproj_bwd_x", grid=(nj,),
        in_specs=[_full_spec((DPROJ, D)), blk(DA), blk(DA), blk(DA), blk(5 * DA), blk(DF),
                  pl.BlockSpec((TT, D), lambda j: (j, 0))] + _x_specs3()
                 + [_full_spec((NM, D)), _full_spec((1, D)), _UNREAD],
        out_specs=[pl.BlockSpec(memory_space=pl.ANY), _full_spec((NM, D)), _full_spec((1, D))],
        out_shape=[jax.ShapeDtypeStruct((seq, D), F32), jax.ShapeDtypeStruct((NM, D), F32),
                   jax.ShapeDtypeStruct((1, D), F32)],
        scratch_shapes=[pltpu.VMEM((2, TT, D), F32), pltpu.SemaphoreType.DMA((2,))],
        compiler_params=_params(),
    )(w, dq_t, dk_t, dv_t, dg5_t, df_t, dout, x, x, x, meta_full, norm_g, after)


def _inproj_bwd_w(u, dq_t, dk_t, dv_t, dg5_t, df_t, L):
    def body(u_ref, dq_hbm, dk_hbm, dv_hbm, dg5_ref, df_ref, dw_ref, dwf_ref, qkv_scr, sems):
        s = pl.program_id(0)
        u_all = u_ref[...]
        fetch = [pltpu.make_async_copy(src, qkv_scr.at[k], sems.at[k])
                 for k, src in enumerate((dq_hbm, dk_hbm, dv_hbm))]

        @pl.when(s == 0)
        def _():
            for cp in fetch:
                cp.start()

        @pl.when(s < 5)
        def _():
            dw_ref[...] = _dot(dg5_ref[...], u_all)

        for k in range(3):
            @pl.when(s == 5 + k)
            def _(k=k):
                fetch[k].wait()
                dw_ref[...] = _dot(qkv_scr[k], u_all)

        @pl.when(s == NSEC - 1)
        def _():
            dwf_ref[...] = _dot(df_ref[...], u_all)

    once = lambda shape: pl.BlockSpec(shape, lambda s: (0, 0), pipeline_mode=pl.Buffered(1))
    any_spec = pl.BlockSpec(memory_space=pl.ANY)
    return pl.pallas_call(
        body, name="inproj_bwd_w", grid=(NSEC,),
        in_specs=[
            once((L, D)), any_spec, any_spec, any_spec,
            pl.BlockSpec((DA, L), lambda s: (jnp.minimum(s, 4), 0)),
            once((DF, L)),
        ],
        out_specs=[pl.BlockSpec((DA, D), lambda s: (jnp.where(s < 5, s + 3, s - 5), 0)), _full_spec((DF, D))],
        out_shape=[jax.ShapeDtypeStruct((NSEC * DA, D), F32), jax.ShapeDtypeStruct((DF, D), F32)],
        scratch_shapes=[pltpu.VMEM((3, DA, L), BF16), pltpu.SemaphoreType.DMA((3,))],
        compiler_params=_params(),
    )(u, dq_t, dk_t, dv_t, dg5_t, df_t)


def _adamw(w, g, m, v):
    m = ADAM_B1 * m + (1.0 - ADAM_B1) * g
    v = ADAM_B2 * v + (1.0 - ADAM_B2) * (g * g)
    m_hat = m / (1.0 - ADAM_B1 ** ADAM_STEP)
    v_hat = v / (1.0 - ADAM_B2 ** ADAM_STEP)
    delta = -ADAM_LR * (m_hat / (jnp.sqrt(v_hat) + ADAM_EPS) + ADAM_WD * w)
    return delta, m, v


def _adamw_big(own_in, land_in, own_out, land_out, w_in_t, m_in_t, v_in_t, w_out, m_out, v_out):
    cb = CB
    e_sh = D // NDEV
    in_shape = jax.ShapeDtypeStruct(w_in_t.shape, F32)
    out_shape = jax.ShapeDtypeStruct(w_out.shape, F32)

    def total(own_ref, land_ref, rows, chips):
        g = _pick_slab(0, own_ref, land_ref, rows, chips=chips).astype(F32)
        for j in range(1, own_ref.shape[0]):
            g = g + _pick_slab(j, own_ref, land_ref, rows, chips=chips).astype(F32)
        return g

    def body(oi_ref, li_ref, oo_ref, lo_ref, wi_ref, mi_ref, vi_ref, wo_ref, mo_ref, vo_ref,
             gi, di, mi, vi, go, do, mo, vo):
        g = total(oi_ref, li_ref, slice(0, WSHP), True)[:WSH]
        d, mn, vn = _adamw(wi_ref[...], g, mi_ref[...], vi_ref[...])
        gi[...], di[...], mi[...], vi[...] = g, d, mn, vn
        g = total(oo_ref, lo_ref, slice(0, e_sh), False)
        d, mn, vn = _adamw(wo_ref[0], g, mo_ref[0], vo_ref[0])
        go[0], do[0], mo[0], vo[0] = g, d, mn, vn

    slab = lambda n, rows: pl.BlockSpec((n, rows, cb), lambda i: (0, 0, i))
    ispec = pl.BlockSpec((WSH, cb), lambda i: (0, i))
    ospec = pl.BlockSpec((1, e_sh, cb), lambda i: (0, 0, i))
    return pl.pallas_call(
        body, name="adamw_big", grid=(D // cb,),
        in_specs=[slab(4, WSHP), slab(4, WSHP), slab(NDEV, e_sh), slab(NDEV, e_sh),
                  ispec, ispec, ispec, ospec, ospec, ospec],
        out_specs=[ispec] * 4 + [ospec] * 4, out_shape=[in_shape] * 4 + [out_shape] * 4,
        compiler_params=_params(),
    )(own_in, land_in, own_out, land_out, w_in_t, m_in_t, v_in_t, w_out, m_out, v_out)


F0 = 3 * DA


def _unshard_w_in(w_all, small_all, attn_gain, conv_gain):
    def body(w_ref, small_ref, ga_ref, gc_ref, wt_ref, meta_ref, cwb_ref, gab_ref, gcb_ref):
        i = pl.program_id(0)
        for k in range(CB // TB):
            meta_ref[:, k * TB:(k + 1) * TB] = small_ref[(CB // TB) * i + k, 0:NM, :]

        @pl.when(i == 0)
        def _():
            per_row = lambda line: jnp.broadcast_to(line, (TB, DA)).T
            cw = jnp.concatenate([small_ref[j, NM:NM + 3, 0:DH] for j in range(NDEV)], axis=1)
            for k in range(3):
                cwb_ref[k] = per_row(cw[k:k + 1, :])
            gab_ref[...] = per_row(ga_ref[...])
            gcb_ref[...] = per_row(gc_ref[...])

        def ref_rows(lo, hi):
            pieces, r = [], lo
            while r < hi:
                sh, off = divmod(r, WSH)
                n = min(hi - r, WSH - off)
                pieces.append(w_ref[sh, off:off + n, :])
                r += n
            return pieces

        for s in range(NSEC):
            lo = s * DA if s < 3 else s * DA + H
            wt_ref[s * DA:(s + 1) * DA, :] = jnp.concatenate(ref_rows(lo, lo + DA), axis=0)
        wt_ref[NSEC * DA:DPROJ, :] = jnp.concatenate(
            ref_rows(F0, F0 + H) + [jnp.zeros((DF - H, CB), BF16)], axis=0)

    return pl.pallas_call(
        body, name="unshard_w_in", grid=(D // CB,),
        in_specs=[pl.BlockSpec((NDEV, WSHP, CB), lambda i: (0, 0, i)), _full_spec(small_all.shape),
                  _full_spec((1, DA)), _full_spec((1, DA))],
        out_specs=[pl.BlockSpec((DPROJ, CB), lambda i: (0, i)), pl.BlockSpec((NM, CB), lambda i: (0, i)),
                   _full_spec((3, DA, TB)), _full_spec((DA, TB)), _full_spec((DA, TB))],
        out_shape=[jax.ShapeDtypeStruct((DPROJ, D), BF16), jax.ShapeDtypeStruct((NM, D), F32),
                   jax.ShapeDtypeStruct((3, DA, TB), F32), jax.ShapeDtypeStruct((DA, TB), F32),
                   jax.ShapeDtypeStruct((DA, TB), F32)],
        compiler_params=_params(),
    )(w_all, small_all, attn_gain, conv_gain)


def _shard_w_in_grads(dw_main, dw_f):
    def body(dm_ref, df_ref, p_ref):
        mc = lax.axis_index("c")

        def ref_rows(lo, hi):
            pieces, r = [], lo
            while r < hi:
                if r < F0:
                    n = min(hi, F0) - r
                    pieces.append(dm_ref[r:r + n, :])
                elif r < F0 + H:
                    n = min(hi, F0 + H) - r
                    pieces.append(df_ref[r - F0:r - F0 + n, :])
                else:
                    n = hi - r
                    pieces.append(dm_ref[r - H:r - H + n, :])
                r += n
            return pieces

        for i in range(NDEV):
            rows = jnp.concatenate(ref_rows(i * WSH, (i + 1) * WSH) + [jnp.zeros((WSHP - WSH, CB), F32)], axis=0)
            p_ref[i // 2 + jnp.where(mc == i % 2, 0, 4)] = rows.astype(BF16)

    col = lambda rows: pl.BlockSpec((rows, CB), lambda i: (0, i))
    return pl.pallas_call(
        body, name="shard_w_in_grads", grid=(D // CB,),
        in_specs=[col(NSEC * DA), col(DF)],
        out_specs=pl.BlockSpec((NDEV, WSHP, CB), lambda i: (0, 0, i)),
        out_shape=jax.ShapeDtypeStruct((NDEV, WSHP, D), BF16),
        compiler_params=_params(),
    )(dw_main, dw_f)


SMALL = ("norm_g", "final_norm_g", "attn_norm_g", "conv_norm_g", "b_f", "meta", "conv_w")


def _as_rows(x):
    return jnp.concatenate([x[:, r * TB:(r + 1) * TB] for r in range(x.shape[1] // TB)], axis=0)


def _as_line(rows):
    return jnp.concatenate([rows[r:r + 1, :] for r in range(rows.shape[0])], axis=1)


def _pad_rows(x, n=8):
    return jnp.concatenate([x, jnp.zeros((n - x.shape[0], x.shape[1]), F32)], axis=0)


def _tile_rows(a, rows, lanes=TB):
    a = a.reshape(rows, lanes)
    return jnp.pad(a, ((0, -rows % 8), (0, TB - lanes)))


def _pack_small_grads(dg_norm, dg_final, dga_p, dgc_p, dcw_p, db_b, dmeta, loss):
    def body(dgn_ref, dgf_ref, dga_ref, dgc_ref, dcw_ref, db_ref, dmeta_ref, loss_ref, out_ref):
        def lane_sums(p):
            return jnp.sum(p.T, axis=0, keepdims=True)

        lane = lax.broadcasted_iota(jnp.int32, (1, TB), 1)
        b_row = jnp.where(lane == H, loss_ref[...], 0.0)
        for h in range(H):
            b_row = b_row + jnp.where(lane == h, db_ref[h:h + 1, :], 0.0)
        common = jnp.concatenate([
            _as_rows(dgn_ref[...]), _as_rows(dgf_ref[...]), _pad_rows(_as_rows(lane_sums(dga_ref[...]))),
            _pad_rows(_as_rows(lane_sums(dgc_ref[...]))), _pad_rows(b_row)], axis=0)
        dcw = [lane_sums(dcw_ref[k]) for k in range(3)]
        for j in range(NDEV):
            cw = jnp.concatenate(
                [jnp.concatenate([r[:, j * DH:(j + 1) * DH], jnp.zeros((1, TB - DH), F32)], axis=1) for r in dcw],
                axis=0)
            out_ref[j] = jnp.concatenate([common, dmeta_ref[:, j * TB:(j + 1) * TB], _pad_rows(cw)], axis=0)

    return pl.pallas_call(
        body, name="pack_small_grads", out_shape=jax.ShapeDtypeStruct((NDEV, SROWS, TB), F32),
    )(dg_norm, dg_final, dga_p, dgc_p, dcw_p, db_b, dmeta, loss)


def _adamw_small(own, land, params):
    flat = [a for n in SMALL for a in params[n]]

    def body(*refs):
        own_ref, land_ref = refs[:2]
        ins = refs[2:2 + 3 * len(SMALL)]
        outs = refs[2 + 3 * len(SMALL):]
        g = _pick_slab(0, own_ref, land_ref, slice(0, SROWS))
        for j in range(1, NDEV):
            g = g + _pick_slab(j, own_ref, land_ref, slice(0, SROWS))
        grads = dict(
            norm_g=_as_line(g[0:8]), final_norm_g=_as_line(g[8:16]), attn_norm_g=_as_line(g[16:20]),
            conv_norm_g=_as_line(g[24:28]), b_f=g[32:33, :H], meta=g[40:56], conv_w=g[56:59, :DH][None])
        for i, n in enumerate(SMALL):
            w_ref, m_ref, v_ref = ins[3 * i:3 * i + 3]
            d, mn, vn = _adamw(w_ref[...], grads[n], m_ref[...], v_ref[...])
            for o_ref, val in zip(outs[4 * i:4 * i + 4], (grads[n], d, mn, vn)):
                o_ref[...] = val
        outs[-1][...] = g[32:33, H:H + 1]

    shapes = [jax.ShapeDtypeStruct(params[n][0].shape, F32) for n in SMALL for _ in range(4)]
    res = pl.pallas_call(
        body, name="adamw_small", out_shape=shapes + [jax.ShapeDtypeStruct((1, 1), F32)],
    )(own, land, *flat)
    return {n: res[4 * i:4 * i + 4] for i, n in enumerate(SMALL)}, res[-1]


def kernel(x, meta, norm_g, w_in, b_f, conv_w, attn_norm_g, conv_norm_g, w_out, final_norm_g, loss_target, m_meta, m_norm_g, m_w_in, m_b_f, m_conv_w, m_attn_norm_g, m_conv_norm_g, m_w_out, m_final_norm_g, v_meta, v_norm_g, v_w_in, v_b_f, v_conv_w, v_attn_norm_g, v_conv_norm_g, v_w_out, v_final_norm_g):
    seq = x.shape[1]
    L = seq + TB
    assert x.shape == (1, seq, D) and L % TT == 0 and w_in.shape == (1, D, WSH)
    x2 = x[0]
    tgt = loss_target[0]

    w_in_slab = jnp.pad(w_in[0].T, ((0, WSHP - WSH), (0, 0))).astype(BF16)
    w_out_slab = w_out[0].astype(BF16)
    meta_slab = jnp.concatenate([meta, _tile_rows(conv_w[0], 3, DH)], axis=0)
    wout_flight = _split_start(w_out_slab, "gather_w_out_start", per_peer=False)
    w_all, small_all = _all_gather([w_in_slab, meta_slab], "gather_w_in")

    w_t, meta_full, cw_b, ga_b, gcn_b = _unshard_w_in(w_all, small_all, attn_norm_g, conv_norm_g)

    u, proj_t, gate_t, f_t, ktok, vtok = _inproj_fwd(x2, meta_full, norm_g, w_t, L, after=wout_flight[4])
    cq, kaug, sg = _fgate_fwd(f_t, b_f.reshape(H, 1), ktok, L)
    o_t, lse = _attn_fwd(proj_t, kaug, cq, L)

    w_out_own, w_out_land = _split_wait(wout_flight, o_t, "gather_w_out_wait", per_peer=False)
    dout, dw_out, loss_part, dg_final, do_t, dd, dg5_t, dga_p, dgc_p, dcw_p = _gate_outproj(
        o_t, gate_t, cw_b, ga_b, gcn_b, w_out_own, w_out_land, x2, meta_full, final_norm_g.reshape(1, D), tgt, L)
    dwo_flight = _split_start(dw_out.reshape(NDEV, D // NDEV, D), "exchange_dw_out_start", per_peer=True)
    dq_t, dk_t, dv_t, dck, dcq = _attn_bwd(proj_t, kaug, vtok, do_t, lse, dd, cq, L, after=dwo_flight[4])
    df_t, db_f = _fgate_bwd(dcq, dck, sg, L)
    dw_main, dw_f = _inproj_bwd_w(u, dq_t, dk_t, dv_t, dg5_t, df_t, L)
    dwi_parts = _shard_w_in_grads(dw_main, dw_f)
    dwi_chip = _pair_sum(dwi_parts, _pair_exchange(dwi_parts, "exchange_dw_in_pair"))
    dwi_flight = _split_start(dwi_chip, "exchange_dw_in_start", per_peer=True, chips=True)
    grad_x, dmeta, dg_norm = _inproj_bwd_x(
        w_t, dq_t, dk_t, dv_t, dg5_t, df_t, dout, x2, meta_full, norm_g, L, after=dwi_flight[4])
    small_parts = _pack_small_grads(dg_norm, dg_final, dga_p, dgc_p, dcw_p, db_f, dmeta, loss_part)
    small_flight = _split_start(small_parts, "exchange_small_start", per_peer=True)
    dwo_own, dwo_land = _split_wait(dwo_flight, small_flight[4], "exchange_dw_out_wait", per_peer=True)
    dwi_own, dwi_land = _split_wait(dwi_flight, dwo_land, "exchange_dw_in_wait", per_peer=True, chips=True)

    big_out = _adamw_big(dwi_own, dwi_land, dwo_own, dwo_land,
                         w_in[0].T, m_w_in[0].T, v_w_in[0].T, w_out, m_w_out, v_w_out)
    g_w_in, d_w_in, nm_w_in, nv_w_in = [a.T[None] for a in big_out[:4]]
    g_w_out, d_w_out, nm_w_out, nv_w_out = big_out[4:]
    sm_own, sm_land = _split_wait(small_flight, big_out[4], "exchange_small_wait", per_peer=True)
    line = lambda a: a.reshape(1, D)
    small, loss = _adamw_small(sm_own, sm_land, dict(
        norm_g=(norm_g, m_norm_g, v_norm_g),
        final_norm_g=(line(final_norm_g), line(m_final_norm_g), line(v_final_norm_g)),
        attn_norm_g=(attn_norm_g, m_attn_norm_g, v_attn_norm_g),
        conv_norm_g=(conv_norm_g, m_conv_norm_g, v_conv_norm_g),
        b_f=(b_f, m_b_f, v_b_f), meta=(meta, m_meta, v_meta), conv_w=(conv_w, m_conv_w, v_conv_w)))
    small["final_norm_g"] = [a.reshape(D) for a in small["final_norm_g"]]
    order = ("meta", "norm_g", "w_in", "b_f", "conv_w", "attn_norm_g", "conv_norm_g", "w_out", "final_norm_g")
    groups = []
    for k, (wi, wo) in enumerate(((g_w_in, g_w_out), (d_w_in, d_w_out), (nm_w_in, nm_w_out), (nv_w_in, nv_w_out))):
        d = dict({n: small[n][k] for n in SMALL}, w_in=wi, w_out=wo)
        groups.append([d[n] for n in order])
    return (loss[0, 0], grad_x[None], *groups[0], *groups[1], *groups[2], *groups[3])
```

```python
import jax
import jax.numpy as jnp
from jax import lax
from jax.experimental import pallas as pl
from jax.experimental.pallas import tpu as pltpu

F32 = jnp.float32
BF16 = jnp.bfloat16

D = 1024
DA = 512
H = 8
DH = 64
NM = 16
TB = 128
P0 = TB - NM
TT = 3 * TB
HG = 8
NDEV = 8
NSEC = 8
DF = 16
DPROJ = NSEC * DA + DF
WSH = 513
WSHP = 528
WROWS = WSHP + D // NDEV
SROWS = 64
EPS = 1e-6
NEG = -1e30
LOG2E = 1.4426950408889634
LN2 = 0.6931471805599453
QSCALE = DH ** -0.5 * LOG2E
KA = 128
CB = 256
VMEM_LIMIT = 56 * 1024 * 1024

ADAM_LR = 0.001
ADAM_B1 = 0.9
ADAM_B2 = 0.999
ADAM_EPS = 1e-08
ADAM_WD = 0.01
ADAM_STEP = 10

NT_DIMS = (((1,), (1,)), ((), ()))
TN_DIMS = (((0,), (0,)), ((), ()))
MESH = pl.DeviceIdType.MESH


def _params(n_axes=1, vmem=VMEM_LIMIT):
    return pltpu.CompilerParams(dimension_semantics=("arbitrary",) * n_axes, vmem_limit_bytes=vmem)


def _dot(a, b, dims=None):
    if dims is None:
        return jnp.dot(a, b, preferred_element_type=F32)
    return lax.dot_general(a, b, dims, preferred_element_type=F32)


def _my_place():
    return lax.axis_index("x"), lax.axis_index("y"), lax.axis_index("c")


def _all_gather(xs, name):
    n = len(xs)

    def body(*refs):
        x_refs, out_refs = refs[:n], refs[n:2 * n]
        send_sems, recv_sems, local_sems = refs[2 * n:]
        mx, my, mc = _my_place()

        def across(px, py, pc, axis_a):
            flip_x = pc if axis_a else 1 - pc
            return (px + flip_x) % 2, (py + 1 - flip_x) % 2, pc

        def idx(p):
            return 4 * p[0] + 2 * p[1] + p[2]

        me, sib = (mx, my, mc), (mx, my, 1 - mc)
        a_nbr, b_nbr = across(*me, True), across(*me, False)
        diag = across(*b_nbr, True)
        sib_a, sib_b = across(*sib, True), across(*sib, False)
        sib_diag = across(*sib_b, True)

        waits = []
        for t in range(n):
            out_ref = out_refs[t]

            def copy(k, block, to, src=None, out_ref=out_ref, t=t):
                return pltpu.make_async_remote_copy(
                    src_ref=out_ref.at[idx(block)] if src is None else src, dst_ref=out_ref.at[idx(block)],
                    send_sem=send_sems.at[7 * t + k], recv_sem=recv_sems.at[7 * t + k],
                    device_id=to, device_id_type=MESH)

            mine = pltpu.make_async_copy(x_refs[t], out_ref.at[idx(me)], local_sems.at[t])
            mine.start()
            started = [copy(0, me, sib, src=x_refs[t]), copy(1, me, a_nbr, src=x_refs[t]),
                       copy(2, me, b_nbr, src=x_refs[t])]
            for cp in started:
                cp.start()
            waits.append((copy, mine, started))
        relays = ((1, a_nbr, ((3, b_nbr), (4, sib))), (2, b_nbr, ((5, sib),)), (3, diag, ((6, sib),)))
        for landed, block, onward in relays:
            for copy, _, started in waits:
                copy(landed, block, me).wait_recv()
                for k, to in onward:
                    started.append(copy(k, block, to))
                    started[-1].start()
        for copy, mine, started in waits:
            for k, block in ((0, sib), (4, sib_a), (5, sib_b), (6, sib_diag)):
                copy(k, block, me).wait_recv()
            for cp in started:
                cp.wait_send()
            mine.wait()

    any_spec = pl.BlockSpec(memory_space=pl.ANY)
    return pl.pallas_call(
        body, name=name,
        out_shape=[jax.ShapeDtypeStruct((NDEV,) + x.shape, x.dtype) for x in xs],
        in_specs=[any_spec] * n, out_specs=[any_spec] * n,
        scratch_shapes=[pltpu.SemaphoreType.DMA((7 * n,)), pltpu.SemaphoreType.DMA((7 * n,)),
                        pltpu.SemaphoreType.DMA((n,))],
    )(*xs)


_HBM = pl.BlockSpec(memory_space=pltpu.HBM)
_UNREAD = pl.BlockSpec(memory_space=pl.ANY)
_SEM = pl.BlockSpec(memory_space=pltpu.SEMAPHORE)
_EFFECT = pltpu.SideEffectType.DATAFLOW_SIDE_EFFECTING


def _peer_of(m, place):
    mx, my, mc = place
    return ((1 - mx) if m & 4 else mx, (1 - my) if m & 2 else my, (1 - mc) if m & 1 else mc)


def _party(chips):
    if chips:
        return (lambda p: 2 * p[0] + p[1]), (2, 4, 6)
    return (lambda p: 4 * p[0] + 2 * p[1] + p[2]), tuple(range(1, NDEV))


def _split_copies(src_ref, land_ref, send_sems, recv_sems, per_peer, incoming, chips):
    place = _my_place()
    slot, masks = _party(chips)
    me = slot(place)
    out = []
    for k, m in enumerate(masks):
        there = _peer_of(m, place)
        peer = slot(there)
        src = (src_ref.at[me] if incoming else src_ref.at[peer]) if per_peer else src_ref
        out.append(pltpu.make_async_remote_copy(
            src_ref=src, dst_ref=land_ref.at[peer if incoming else me],
            send_sem=send_sems.at[k], recv_sem=recv_sems.at[k], device_id=there, device_id_type=MESH))
    return out


def _split_start(src, name, per_peer, chips=False):
    slab = src.shape[1:] if per_peer else src.shape
    n = len(_party(chips)[1])

    def body(src_ref, land_ref, send_sems, recv_sems, src_thru, land_thru, token):
        for cp in _split_copies(src_ref, land_ref, send_sems, recv_sems, per_peer, False, chips):
            cp.start()
        token[...] = jnp.zeros_like(token)

    return pl.pallas_call(
        body, name=name,
        out_shape=(pltpu.SemaphoreType.DMA((n,)), pltpu.SemaphoreType.DMA((n,)),
                   pltpu.HBM(src.shape, src.dtype), pltpu.HBM((n + 1,) + slab, src.dtype),
                   jax.ShapeDtypeStruct((8, TB), F32)),
        in_specs=(_HBM, _HBM), out_specs=(_SEM, _SEM, _HBM, _HBM, pl.BlockSpec(memory_space=pltpu.VMEM)),
        input_output_aliases={0: 2, 1: 3},
        compiler_params=pltpu.CompilerParams(has_side_effects=_EFFECT),
    )(pltpu.with_memory_space_constraint(src, pltpu.HBM),
      pltpu.with_memory_space_constraint(lax.empty((n + 1,) + slab, src.dtype), pltpu.HBM))


def _split_wait(handles, after, name, per_peer, chips=False):
    send_sems, recv_sems, src_thru, land_thru, _ = handles

    def body(src_ref, land_ref, send_sems, recv_sems, after_ref, src_out, land_out):
        for cp in _split_copies(src_ref, land_ref, send_sems, recv_sems, per_peer, False, chips):
            cp.wait_send()
        for cp in _split_copies(src_ref, land_ref, send_sems, recv_sems, per_peer, True, chips):
            cp.wait_recv()

    return pl.pallas_call(
        body, name=name,
        out_shape=(pltpu.HBM(src_thru.shape, src_thru.dtype), pltpu.HBM(land_thru.shape, land_thru.dtype)),
        in_specs=(_HBM, _HBM, _SEM, _SEM, pl.BlockSpec(memory_space=pl.ANY)), out_specs=(_HBM, _HBM),
        input_output_aliases={0: 0, 1: 1},
        compiler_params=pltpu.CompilerParams(has_side_effects=_EFFECT),
    )(src_thru, land_thru, send_sems, recv_sems, after)


def _pick_slab(j, own_ref, land_ref, rows, per_peer=True, chips=False):
    me = _party(chips)[0](_my_place())
    own = (lambda: own_ref[j, rows, :]) if per_peer else (lambda: own_ref[rows, :])
    return lax.cond(me == j, own, lambda: land_ref[j, rows, :])


def _pair_exchange(p, name):
    def body(p_ref, got_ref, send_sems, recv_sems):
        mx, my, mc = _my_place()
        copies = [pltpu.make_async_remote_copy(
            src_ref=p_ref.at[4 + q], dst_ref=got_ref.at[q], send_sem=send_sems.at[q],
            recv_sem=recv_sems.at[q], device_id=(mx, my, 1 - mc), device_id_type=MESH) for q in range(4)]
        for cp in copies:
            cp.start()
        for cp in copies:
            cp.wait_recv()
        for cp in copies:
            cp.wait_send()

    any_spec = pl.BlockSpec(memory_space=pl.ANY)
    return pl.pallas_call(
        body, name=name, out_shape=jax.ShapeDtypeStruct((4,) + p.shape[1:], p.dtype),
        in_specs=[any_spec], out_specs=any_spec,
        scratch_shapes=[pltpu.SemaphoreType.DMA((4,)), pltpu.SemaphoreType.DMA((4,))],
    )(p)


def _pair_sum(p, got):
    rows = p.shape[1]

    def body(p_ref, got_ref, out_ref):
        for q in range(4):
            out_ref[q] = (p_ref[q].astype(F32) + got_ref[q].astype(F32)).astype(BF16)

    blk = lambda n: pl.BlockSpec((n, rows, CB), lambda i: (0, 0, i))
    return pl.pallas_call(
        body, name="pair_sum", grid=(D // CB,), in_specs=[blk(4), blk(4)], out_specs=blk(4),
        out_shape=jax.ShapeDtypeStruct((4, rows, D), BF16), compiler_params=_params(),
    )(p, got)


def _h_block(t, x_ref, meta_ref):
    first = jnp.concatenate([jnp.zeros((P0, D), F32), meta_ref[...]], axis=0)
    return jnp.where(t == 0, first, x_ref[...])


def _x_specs3(tile=lambda j: j):
    return [pl.BlockSpec((TB, D), lambda j: (jnp.maximum(3 * tile(j) - 1, 0), 0)),
            pl.BlockSpec((TB, D), lambda j: (3 * tile(j), 0)),
            pl.BlockSpec((TB, D), lambda j: (3 * tile(j) + 1, 0))]


def _h_tile(j, xa_ref, xb_ref, xc_ref, meta_ref):
    first = jnp.concatenate([jnp.zeros((P0, D), F32), meta_ref[...]], axis=0)
    return jnp.concatenate([jnp.where(j == 0, first, xa_ref[...]), xb_ref[...], xc_ref[...]], axis=0)


def _full_spec(shape):
    return pl.BlockSpec(shape, lambda *_: (0,) * len(shape))


def _sigmoid(z):
    return 1.0 / (1.0 + jnp.exp(-z))


def _lane_tiles_sum(x):
    out = x[:, :TB]
    for i in range(1, x.shape[1] // TB):
        out = out + x[:, i * TB:(i + 1) * TB]
    return out


def _inproj_fwd(x, meta_full, norm_g, w_t, L, after):
    nj = L // TT

    def body(xa_ref, xb_ref, xc_ref, meta_ref, g_ref, w_ref, _, u_ref, proj_ref, gate_ref, f_ref, ktok_ref, vtok_ref):
        hb = _h_tile(pl.program_id(0), xa_ref, xb_ref, xc_ref, meta_ref)
        r = lax.rsqrt(jnp.mean(hb * hb, axis=-1, keepdims=True) + EPS)
        u = (hb * r * g_ref[...]).astype(BF16)
        u_ref[...] = u
        for s in range(NSEC):
            p = _dot(u, w_ref[s * DA:(s + 1) * DA, :], NT_DIMS)
            if s == 0:
                p = p * QSCALE
            if s in (1, 2):
                tok_ref = ktok_ref if s == 1 else vtok_ref
                for h in range(H):
                    tok_ref[h] = p[:, h * DH:(h + 1) * DH].astype(BF16)
            out_ref, s_out = (proj_ref, s) if s < 3 else (gate_ref, s - 3)
            out_ref[s_out * DA:(s_out + 1) * DA, :] = p.T.astype(BF16)
        f_ref[...] = _dot(w_ref[NSEC * DA:DPROJ, :], u, NT_DIMS)[:H]

    return pl.pallas_call(
        body, name="inproj_fwd", grid=(nj,),
        in_specs=_x_specs3() + [_full_spec((NM, D)), _full_spec((1, D)), _full_spec((DPROJ, D)), _UNREAD],
        out_specs=[
            pl.BlockSpec((TT, D), lambda t: (t, 0)),
            pl.BlockSpec((3 * DA, TT), lambda t: (0, t)),
            pl.BlockSpec((None, (NSEC - 3) * DA, TT), lambda t: (t, 0, 0)),
            pl.BlockSpec((H, TT), lambda t: (0, t)),
            pl.BlockSpec((H, TT, DH), lambda t: (0, t, 0)),
            pl.BlockSpec((H, TT, DH), lambda t: (0, t, 0)),
        ],
        out_shape=[
            jax.ShapeDtypeStruct((L, D), BF16),
            jax.ShapeDtypeStruct((3 * DA, L), BF16),
            jax.ShapeDtypeStruct((nj, (NSEC - 3) * DA, TT), BF16),
            jax.ShapeDtypeStruct((H, L), F32),
            jax.ShapeDtypeStruct((H, L, DH), BF16),
            jax.ShapeDtypeStruct((H, L, DH), BF16),
        ],
        compiler_params=_params(),
    )(x, x, x, meta_full, norm_g, w_t, after)


def _split3(x):
    hi = x.astype(BF16).astype(F32)
    r = x - hi
    mid = r.astype(BF16).astype(F32)
    return hi, mid, (r - mid).astype(BF16).astype(F32)


def _bias_rows(bias):
    one = jnp.ones((1, TT), F32)
    zero = jnp.zeros((1, TT), F32)
    parts = [zero] * 3 if bias is None else list(_split3(bias))
    return jnp.concatenate([one] * 3 + parts + [zero] * (DF - 6), axis=0).astype(BF16)


def _fgate_fwd(f_t, b_col, ktok, L):
    nb = L // TB

    def body(f_ref, b_ref, ktok_ref, cq_ref, kaug_ref, sg_ref, bias_scr):
        h = pl.program_id(0)

        @pl.when(h == 0)
        def _():
            z = f_ref[...] + b_ref[...]
            idx = lax.broadcasted_iota(jnp.int32, (H, L), 1)
            real = idx >= P0
            lf = jnp.where(real, jnp.minimum(z, 0.0) - jnp.log1p(jnp.exp(-jnp.abs(z))), 0.0)
            sg_ref[...] = jnp.where(real, 1.0 / (1.0 + jnp.exp(z)), 0.0)
            c = lf
            s = 1
            while s < L:
                c = c + jnp.where(idx >= s, pltpu.roll(c, s, 1), 0.0)
                s *= 2
            c = c * LOG2E
            for hh in range(H):
                cq_ref[hh] = c[hh:hh + 1, :]
            for part, val in enumerate(_split3(-jnp.where(real, c, -NEG))):
                for hh in range(H):
                    bias_scr[part * H + hh] = val[hh:hh + 1, :]

        lane = lax.broadcasted_iota(jnp.int32, (TB, KA), 1)
        head = jnp.zeros((DH, TB), F32)
        tail = jnp.concatenate([jnp.ones((3, TB), F32), jnp.zeros((KA - DH - 6, TB), F32)], axis=0)
        for b in range(nb):
            blk = slice(b * TB, (b + 1) * TB)
            cols = jnp.concatenate(
                [head] + [bias_scr[part * H + h, :, blk] for part in range(3)] + [tail], axis=0).T
            k = jnp.concatenate([ktok_ref[0, blk, :].astype(F32), jnp.zeros((TB, KA - DH), F32)], axis=1)
            kaug_ref[0, blk, :] = jnp.where(lane < DH, k, cols).astype(BF16)

    return pl.pallas_call(
        body, name="fgate_fwd", grid=(H,),
        in_specs=[_full_spec((H, L)), _full_spec((H, 1)), pl.BlockSpec((1, L, DH), lambda h: (h, 0, 0))],
        out_specs=[_full_spec((H, 1, L)), pl.BlockSpec((1, L, KA), lambda h: (h, 0, 0)), _full_spec((H, L))],
        out_shape=[
            jax.ShapeDtypeStruct((H, 1, L), F32),
            jax.ShapeDtypeStruct((H, L, KA), BF16),
            jax.ShapeDtypeStruct((H, L), F32),
        ],
        scratch_shapes=[pltpu.VMEM((3 * H, 1, L), F32)],
        compiler_params=_params(),
    )(f_t, b_col, ktok)


def _causal_mask():
    r = lax.broadcasted_iota(jnp.int32, (TT, TT), 0)
    c = lax.broadcasted_iota(jnp.int32, (TT, TT), 1)
    return r <= c


def _attn_fwd(proj_t, kaug, cq, L):
    nq = L // TT

    def body(q_ref, qn_ref, kaug_ref, v_ref, cq_ref, o_ref, lse_ref,
             qa_scr, s_scr, cmax_scr, m_scr, p_scr, alpha_scr, acc_scr):
        j = pl.program_id(0)
        rows = [slice(g * DH, (g + 1) * DH) for g in range(HG)]
        ones = jnp.ones((DF, TT), BF16)

        def load_queries(ref):
            for g in range(HG):
                qa_scr[g] = jnp.concatenate(
                    [ref[rows[g], :], _bias_rows(None), jnp.zeros((KA - DH - DF, TT), BF16)], axis=0)

        def scores(kt, masked):
            k_off = pl.multiple_of(kt * TT, TT)
            for g in range(HG):
                s = _dot(kaug_ref[g, pl.ds(k_off, TT), :], qa_scr[g])
                if masked:
                    s = jnp.where(_causal_mask(), s, NEG)
                s_scr[g] = s
                cmax_scr[g] = jnp.max(s, axis=0, keepdims=True)

        def softmax():
            for g in range(HG):
                m_old = m_scr[g]
                m_new = jnp.maximum(m_old, cmax_scr[g])
                alpha_scr[g] = jnp.exp2(m_old - m_new)
                p_scr[g] = jnp.exp2(s_scr[g] - m_new).astype(BF16)
                m_scr[g] = m_new

        def weighted_sum(kt):
            k_off = pl.multiple_of(kt * TT, TT)
            for g in range(HG):
                v1 = jnp.concatenate([v_ref[rows[g], pl.ds(k_off, TT)], ones], axis=0)
                acc_scr[g] = alpha_scr[g] * acc_scr[g] + _dot(v1, p_scr[g])

        @pl.when(j == 0)
        def _():
            load_queries(q_ref)
            scores(0, True)

        m_scr[...] = jnp.full_like(m_scr, NEG)
        acc_scr[...] = jnp.zeros_like(acc_scr)

        @pl.when(j >= 1)
        def _():
            softmax()
            scores(j - 1, False)

        def step(i, c):
            weighted_sum(j - i + 1)
            softmax()
            scores(j - i - 1, False)
            return c

        lax.fori_loop(1, j, step, 0)

        def drain(second_last, next_tile):
            if second_last:
                weighted_sum(1)
            softmax()
            if next_tile:
                load_queries(qn_ref)
                scores(j + 1, True)
            weighted_sum(0)

        @pl.when(j == 0)
        def _():
            drain(False, nq > 1)

        @pl.when((j >= 1) & (j < nq - 1))
        def _():
            drain(True, True)

        @pl.when((j >= 1) & (j == nq - 1))
        def _():
            drain(True, False)

        for g in range(HG):
            l = acc_scr[g, DH:DH + 1, :]
            o_ref[rows[g], :] = acc_scr[g, :DH, :] * (1.0 / l)
            lse_ref[g] = m_scr[g] + jnp.log2(l) + cq_ref[g]

    assert HG == H
    return pl.pallas_call(
        body, name="attn_fwd", grid=(nq,),
        in_specs=[
            pl.BlockSpec((DA, TT), lambda j: (0, j)),
            pl.BlockSpec((DA, TT), lambda j: (0, jnp.minimum(j + 1, nq - 1))),
            pl.BlockSpec((H, L, KA), lambda j: (0, 0, 0)),
            pl.BlockSpec((DA, L), lambda j: (2, 0)),
            pl.BlockSpec((H, 1, TT), lambda j: (0, 0, j)),
        ],
        out_specs=[
            pl.BlockSpec((None, DA, TT), lambda j: (j, 0, 0)),
            pl.BlockSpec((H, 1, TT), lambda j: (0, 0, j)),
        ],
        out_shape=[jax.ShapeDtypeStruct((nq, DA, TT), F32), jax.ShapeDtypeStruct((H, 1, L), F32)],
        scratch_shapes=[pltpu.VMEM((HG, KA, TT), BF16), pltpu.VMEM((HG, TT, TT), F32), pltpu.VMEM((HG, 1, TT), F32),
                        pltpu.VMEM((HG, 1, TT), F32), pltpu.VMEM((HG, TT, TT), BF16), pltpu.VMEM((HG, 1, TT), F32),
                        pltpu.VMEM((HG, DH + DF, TT), F32)],
        compiler_params=_params(),
    )(proj_t, proj_t, kaug, proj_t, cq)


def _gate_group(rows, o_ref, za_ref, gb_ref, gc_ref, xc_ref, zc_ref, gcp_ref, xcp_ref, cw_ref, ga_ref, gcn_ref, first):
    n_rep = TT // TB
    f32 = lambda r: r[rows, :].astype(F32)
    o, za, gb, gc, xc, zc = o_ref[rows, :], f32(za_ref), f32(gb_ref), f32(gc_ref), f32(xc_ref), f32(zc_ref)
    a = gc * xc
    a_prev = jnp.where(first, 0.0, f32(gcp_ref) * f32(xcp_ref))
    full = jnp.concatenate([a_prev, a], axis=1)
    a1 = pltpu.roll(full, 1, 1)[:, TB:]
    a2 = pltpu.roll(full, 2, 1)[:, TB:]
    w0 = jnp.tile(cw_ref[0, rows, :], (1, n_rep))
    w1 = jnp.tile(cw_ref[1, rows, :], (1, n_rep))
    w2 = jnp.tile(cw_ref[2, rows, :], (1, n_rep))
    cv = w0 * a2 + w1 * a1 + w2 * a
    e = gb * cv
    rc = lax.rsqrt(jnp.mean(e * e, axis=0, keepdims=True) + EPS)
    ec = e * rc
    ra = lax.rsqrt(jnp.mean(o * o, axis=0, keepdims=True) + EPS)
    oa = o * ra
    g_a = jnp.tile(ga_ref[rows, :], (1, n_rep))
    g_c = jnp.tile(gcn_ref[rows, :], (1, n_rep))
    sa = _sigmoid(za)
    sc = _sigmoid(zc)
    return dict(o=o, za=za, gb=gb, gc=gc, xc=xc, zc=zc, a=a, a1=a1, a2=a2, w0=w0, w1=w1, w2=w2, cv=cv, e=e,
                rc=rc, ec=ec, ra=ra, oa=oa, g_a=g_a, g_c=g_c, sa=sa, sc=sc)


def _gate_specs(tile):
    halo = pl.BlockSpec((None, 2 * DA, TB),
                        lambda i: (jnp.maximum(tile(i) - 1, 0), 1, TT // TB - 1))
    return [pl.BlockSpec((None, DA, TT), lambda i: (tile(i), 0, 0)),
            pl.BlockSpec((None, 5 * DA, TT), lambda i: (tile(i), 0, 0)), halo,
            _full_spec((3, DA, TB)), _full_spec((DA, TB)), _full_spec((DA, TB))]


def _gate_views(g5_ref, halo_ref):
    return [g5_ref.at[pl.ds(s * DA, DA)] for s in range(5)] + [halo_ref.at[pl.ds(s * DA, DA)] for s in range(2)]


def _gate_outproj(o_t, gate_t, cw_b, ga_b, gcn_b, w_own, w_land, x, meta_full, fng, target, L):
    nj = L // TT
    rp = NM
    n_bwd = 8
    cb = D // 4
    n_ring = 5
    e_sh = D // NDEV
    assert P0 % rp == 0 and TB % rp == 0 and (TT // rp) % n_bwd == 0 and H == n_bwd

    def body(o_hbm, g5_hbm, halo_ref, cw_ref, ga_ref, gcn_ref, halo2_ref,
             wown_ref, wland_ref, xa_ref, xb_ref, xc_ref, meta_ref, g_ref, ta_ref, tb_ref, tc_ref,
             dout_ref, dwb_ref, loss_ref, dg_ref, do_ref, dd_ref, dg5_ref, dga_ref, dgc_ref, dcw_ref,
             dw_ref, o_scr, db_new, db_old, mix_new, mix_old, dmix_new, dmix_old, sq_acc, dg_acc, carry_ref,
             o_ring, g5_ring, ring_sems, w_ref):
        t = pl.program_id(0)

        def fetch(step):
            tile, slot = nj - 1 - step, step % n_ring
            return (pltpu.make_async_copy(o_hbm.at[tile], o_ring.at[slot], ring_sems.at[0, slot]),
                    pltpu.make_async_copy(g5_hbm.at[tile], g5_ring.at[slot], ring_sems.at[1, slot]))

        @pl.when(t == 0)
        def _():
            for step in range(2):
                for cp in fetch(step):
                    cp.start()

        @pl.when(t + 2 < nj)
        def _():
            for cp in fetch(t + 2):
                cp.start()

        @pl.when(t < nj)
        def _():
            for cp in fetch(t):
                cp.wait()

        slot_a, slot_c = t % n_ring, (t + n_ring - 2) % n_ring
        o_ref, o2_ref = o_ring.at[slot_a], o_ring.at[slot_c]
        za_ref, gb_ref, gc_ref, xcv_ref, zc_ref, gcp_ref, xcp_ref = _gate_views(g5_ring.at[slot_a], halo_ref)
        za2_ref, gb2_ref, gc2_ref, xcv2_ref, zc2_ref, gcp2_ref, xcp2_ref = _gate_views(g5_ring.at[slot_c], halo2_ref)
        first_a = t == nj - 1
        first_c = t == nj + 1

        def gate_rows(h):
            rows = slice(h * DH, (h + 1) * DH)
            g = _gate_group(rows, o_ref, za_ref, gb_ref, gc_ref, xcv_ref, zc_ref, gcp_ref, xcp_ref,
                            cw_ref, ga_ref, gcn_ref, first_a)
            mix_new[rows, :] = (g["oa"] * g["g_a"] * (g["za"] * g["sa"])).astype(BF16)
            mix_new[DA + h * DH:DA + (h + 1) * DH, :] = (g["ec"] * g["g_c"] * (g["zc"] * g["sc"])).astype(BF16)

        def loss_rows(c):
            blk = c // (TB // rp)
            rows, out_rows = pl.ds((c % (TB // rp)) * rp, rp), pl.ds(c * rp, rp)
            h = (xa_ref, xb_ref, xc_ref)[blk][rows, :]
            if blk == 0:
                first = meta_ref[...] if c == P0 // rp else jnp.zeros((rp, D), F32)
                h = jnp.where(first_a, first, h)
            o = o_scr[out_rows, :] + h
            r = lax.rsqrt(jnp.mean(o * o, axis=-1, keepdims=True) + EPS)
            orn = o * r
            g = g_ref[...]
            diff = orn * g - (ta_ref, tb_ref, tc_ref)[blk][rows, :]
            if blk == 0:
                diff = diff * jnp.where(first_a, 0.0, 1.0)
            gy = diff * (g * (1.0 / D))
            dout = r * (gy - orn * jnp.mean(gy * orn, axis=-1, keepdims=True))
            dout_ref[out_rows, :] = dout
            db_new[out_rows, :] = dout.astype(BF16)
            sq, go = diff * diff, diff * orn
            sq_acc[...] += sq[:8] + sq[8:]
            dg_acc[...] += go[:8] + go[8:]

        def backward_cols(n):
            if n < 4:
                cols = slice(n * cb, (n + 1) * cb)
                dmix_new[cols, :] = _dot(db_old[...], w_ref[cols, :], NT_DIMS).T.astype(BF16)
            else:
                cols = slice((n - 4) * cb, (n - 3) * cb)
                dw_ref[:, cols] += _dot(mix_old[...], db_old[:, cols])

        def gate_bwd_rows(h):
            rows = slice(h * DH, (h + 1) * DH)
            sec = lambda s: slice(s * DA + h * DH, s * DA + (h + 1) * DH)
            g = _gate_group(rows, o2_ref, za2_ref, gb2_ref, gc2_ref, xcv2_ref, zc2_ref, gcp2_ref, xcp2_ref,
                            cw_ref, ga_ref, gcn_ref, first_c)
            o, za, gb, gc, xc, zc, sa, sc = (g[n] for n in ("o", "za", "gb", "gc", "xc", "zc", "sa", "sc"))
            dya = dmix_old[rows, :].astype(F32)
            dyc = dmix_old[DA + h * DH:DA + (h + 1) * DH, :].astype(F32)

            dn = dya * (za * sa)
            dg5_ref[sec(0), :] = (dya * (g["oa"] * g["g_a"]) * (sa * (1.0 + za * (1.0 - sa)))).astype(BF16)
            dga_ref[rows, :] += _lane_tiles_sum(dn * g["oa"])
            dng = dn * g["g_a"]
            mean_a = jnp.mean(dng * g["oa"], axis=0, keepdims=True)
            do = (dng - g["oa"] * mean_a) * g["ra"]
            do_ref[rows, :] = do.astype(BF16)
            dd_ref[h] = jnp.sum(do * o, axis=0, keepdims=True)

            dnc = dyc * (zc * sc)
            dg5_ref[sec(4), :] = (dyc * (g["ec"] * g["g_c"]) * (sc * (1.0 + zc * (1.0 - sc)))).astype(BF16)
            dgc_ref[rows, :] += _lane_tiles_sum(dnc * g["ec"])
            dncg = dnc * g["g_c"]
            mean_c = jnp.mean(dncg * g["ec"], axis=0, keepdims=True)
            de = (dncg - g["ec"] * mean_c) * g["rc"]
            dg5_ref[sec(1), :] = (de * g["cv"]).astype(BF16)
            dcv = de * gb
            full = jnp.concatenate([dcv, carry_ref[rows, :]], axis=1)
            d1 = pltpu.roll(full, TT + TB - 1, 1)[:, :TT]
            d2 = pltpu.roll(full, TT + TB - 2, 1)[:, :TT]
            carry_ref[rows, :] = dcv[:, :TB]
            da = g["w2"] * dcv + g["w1"] * d1 + g["w0"] * d2
            dg5_ref[sec(2), :] = (da * xc).astype(BF16)
            dg5_ref[sec(3), :] = (da * gc).astype(BF16)
            dcw_ref[0, rows, :] += _lane_tiles_sum(dcv * g["a2"])
            dcw_ref[1, rows, :] += _lane_tiles_sum(dcv * g["a1"])
            dcw_ref[2, rows, :] += _lane_tiles_sum(dcv * g["a"])

        def step(a, b, c):
            half = H // 2
            for h in range(H):
                if a:
                    gate_rows(h)
                if c and h < half:
                    gate_bwd_rows(h)
                if b and h % 2 == 1:
                    backward_cols(h // 2)
            if a:
                o_scr[...] = _dot(mix_new[...], w_ref[...], TN_DIMS)
            per = TT // rp // n_bwd
            for k in range(n_bwd):
                if a:
                    for piece in range(per * k, per * (k + 1)):
                        loss_rows(piece)
                if c and k % 2 == 0:
                    gate_bwd_rows(half + k // 2)
                if b and k % 2 == 1:
                    backward_cols(n_bwd // 2 + k // 2)
            if a:
                db_old[...] = db_new[...]
                mix_old[...] = mix_new[...]
            if b:
                dmix_old[...] = dmix_new[...]

        @pl.when(t == 0)
        def _():
            dw_ref[...] = jnp.zeros_like(dw_ref)
            sq_acc[...] = jnp.zeros_like(sq_acc)
            dg_acc[...] = jnp.zeros_like(dg_acc)
            carry_ref[...] = jnp.zeros_like(carry_ref)
            dga_ref[...] = jnp.zeros_like(dga_ref)
            dgc_ref[...] = jnp.zeros_like(dgc_ref)
            dcw_ref[...] = jnp.zeros_like(dcw_ref)
            for j in range(NDEV):
                w_ref[j * e_sh:(j + 1) * e_sh, :] = _pick_slab(j, wown_ref, wland_ref, slice(0, e_sh), per_peer=False)
            step(True, False, False)

        @pl.when(t == 1)
        def _():
            step(True, True, False)

        @pl.when((t >= 2) & (t < nj))
        def _():
            step(True, True, True)

        @pl.when(t == nj)
        def _():
            step(False, True, True)
            dwb_ref[...] = dw_ref[...].astype(BF16)
            loss_ref[...] = jnp.sum(sq_acc[...], keepdims=True) * (0.5 / D)
            dg_ref[...] = jnp.sum(dg_acc[...], axis=0, keepdims=True) * (1.0 / D)

        @pl.when(t == nj + 1)
        def _():
            step(False, False, True)

    assert nj >= 2
    tile_a = lambda t: jnp.clip(nj - 1 - t, 0, nj - 1)
    tile_c = lambda t: jnp.clip(nj + 1 - t, 0, nj - 1)
    at_c = lambda shape: pl.BlockSpec(shape, lambda t: (0,) * (len(shape) - 1) + (tile_c(t),))
    return pl.pallas_call(
        body, name="gate_outproj", grid=(nj + 2,),
        in_specs=[pl.BlockSpec(memory_space=pl.ANY)] * 2 + _gate_specs(tile_a)[2:] + _gate_specs(tile_c)[2:3]
                 + [_full_spec((e_sh, D)), _full_spec((NDEV, e_sh, D))] + _x_specs3(tile_a)
                 + [_full_spec((NM, D)), _full_spec((1, D))] + _x_specs3(tile_a),
        out_specs=[pl.BlockSpec((TT, D), lambda t: (tile_a(t), 0)), _full_spec((D, D)), _full_spec((1, 1)),
                   _full_spec((1, D)), at_c((DA, TT)), at_c((H, 1, TT)), at_c((5 * DA, TT)),
                   _full_spec((DA, TB)), _full_spec((DA, TB)), _full_spec((3, DA, TB))],
        out_shape=[jax.ShapeDtypeStruct((L, D), F32), jax.ShapeDtypeStruct((D, D), BF16),
                   jax.ShapeDtypeStruct((1, 1), F32), jax.ShapeDtypeStruct((1, D), F32),
                   jax.ShapeDtypeStruct((DA, L), BF16),
                   jax.ShapeDtypeStruct((H, 1, L), F32),
                   jax.ShapeDtypeStruct((5 * DA, L), BF16),
                   jax.ShapeDtypeStruct((DA, TB), F32),
                   jax.ShapeDtypeStruct((DA, TB), F32),
                   jax.ShapeDtypeStruct((3, DA, TB), F32)],
        scratch_shapes=[pltpu.VMEM((D, D), F32), pltpu.VMEM((TT, D), F32), pltpu.VMEM((TT, D), BF16),
                        pltpu.VMEM((TT, D), BF16), pltpu.VMEM((D, TT), BF16), pltpu.VMEM((D, TT), BF16),
                        pltpu.VMEM((D, TT), BF16), pltpu.VMEM((D, TT), BF16),
                        pltpu.VMEM((8, D), F32), pltpu.VMEM((8, D), F32), pltpu.VMEM((DA, TB), F32),
                        pltpu.VMEM((n_ring, DA, TT), F32), pltpu.VMEM((n_ring, 5 * DA, TT), BF16),
                        pltpu.SemaphoreType.DMA((2, n_ring)), pltpu.VMEM((D, D), BF16)],
        compiler_params=_params(),
    )(o_t, gate_t, gate_t, cw_b, ga_b, gcn_b, gate_t,
      w_own, w_land, x, x, x, meta_full, fng, target, target, target)


def _attn_bwd(proj_t, kaug, vtok, do_t, lse, dd, cq, L, after):
    nk = L // TT

    def body(q_ref, kaug_ref, vtok_ref, kt_ref, do_ref, lse_ref, dd_ref, cq_ref, _,
             dq_ref, dk_ref, dv_ref, dck_ref, dcq_ref, dq_acc, kt1_scr, s_scr, dp_scr, dv_scr, dk_scr):
        i = pl.program_id(0)
        rows = [slice(g * DH, (g + 1) * DH) for g in range(HG)]
        ones = jnp.ones((DF, TT), BF16)
        zpad = jnp.zeros((KA - DH - DF, TT), BF16)
        for g in range(HG):
            kt1_scr[g] = jnp.concatenate([kt_ref[rows[g], :], ones], axis=0)
        dv_scr[...] = jnp.zeros_like(dv_scr)
        dk_scr[...] = jnp.zeros_like(dk_scr)

        def q_rows(g, q_off):
            bias = cq_ref[g, :, pl.ds(q_off, TT)] - lse_ref[g, :, pl.ds(q_off, TT)]
            return jnp.concatenate([q_ref[rows[g], pl.ds(q_off, TT)], _bias_rows(bias)], axis=0)

        def scores(jq, masked):
            q_off = pl.multiple_of(jq * TT, TT)
            for g in range(HG):
                s = _dot(kaug_ref[g], jnp.concatenate([q_rows(g, q_off), zpad], axis=0))
                if masked:
                    s = jnp.where(_causal_mask(), s, NEG)
                s_scr[g] = s
                dp_scr[g] = _dot(vtok_ref[g], do_ref[rows[g], pl.ds(q_off, TT)])

        def grads(jq):
            q_off = pl.multiple_of(jq * TT, TT)
            for g in range(HG):
                p = jnp.exp2(s_scr[g])
                ds = (p * (dp_scr[g] - dd_ref[g, :, pl.ds(q_off, TT)])).astype(BF16)
                do1 = jnp.concatenate([do_ref[rows[g], pl.ds(q_off, TT)], jnp.zeros((KA - DH, TT), BF16)], axis=0)
                q1 = jnp.concatenate([q_rows(g, q_off), zpad], axis=0)
                dv_scr[g] += _dot(p.astype(BF16), do1, NT_DIMS)
                dk_scr[g] += _dot(ds, q1, NT_DIMS)
                dq_acc[g, :, pl.ds(q_off, TT)] += _dot(kt1_scr[g], ds)

        @pl.when(i == 0)
        def _():
            dq_acc[...] = jnp.zeros_like(dq_acc)

        scores(i, True)

        def step(jq, c):
            grads(jq)
            scores(jq + 1, False)
            return c

        lax.fori_loop(i, nk - 1, step, 0)
        grads(nk - 1)
        for g in range(HG):
            dv_ref[rows[g], :] = dv_scr[g].T[:DH, :].astype(BF16)
            dk_t = dk_scr[g].T
            dk_ref[rows[g], :] = (dk_t[:DH, :] * LN2).astype(BF16)
            dck_ref[g] = dk_t[DH:DH + 1, :]

        @pl.when(i == nk - 1)
        def _():
            for g in range(HG):
                dq_ref[rows[g], :] = (dq_acc[g, :DH, :] * (DH ** -0.5)).astype(BF16)
                dcq_ref[g] = dq_acc[g, DH:DH + 1, :]

    assert HG == H
    head = lambda i: (0, 0)
    row = lambda i: (0, 0, 0)
    return pl.pallas_call(
        body, name="attn_bwd", grid=(nk,),
        in_specs=[
            pl.BlockSpec((DA, L), head),
            pl.BlockSpec((H, TT, KA), lambda i: (0, i, 0)),
            pl.BlockSpec((H, TT, DH), lambda i: (0, i, 0)),
            pl.BlockSpec((DA, TT), lambda i: (1, i)),
            pl.BlockSpec((DA, L), head),
            pl.BlockSpec((H, 1, L), row), pl.BlockSpec((H, 1, L), row), pl.BlockSpec((H, 1, L), row), _UNREAD,
        ],
        out_specs=[
            pl.BlockSpec((DA, L), head),
            pl.BlockSpec((DA, TT), lambda i: (0, i)),
            pl.BlockSpec((DA, TT), lambda i: (0, i)),
            pl.BlockSpec((H, 1, TT), lambda i: (0, 0, i)),
            pl.BlockSpec((H, 1, L), row),
        ],
        out_shape=[jax.ShapeDtypeStruct((DA, L), BF16), jax.ShapeDtypeStruct((DA, L), BF16),
                   jax.ShapeDtypeStruct((DA, L), BF16), jax.ShapeDtypeStruct((H, 1, L), F32),
                   jax.ShapeDtypeStruct((H, 1, L), F32)],
        scratch_shapes=[
            pltpu.VMEM((HG, DH + DF, L), F32),
            pltpu.VMEM((HG, DH + DF, TT), BF16),
            pltpu.VMEM((HG, TT, TT), F32), pltpu.VMEM((HG, TT, TT), F32),
            pltpu.VMEM((HG, TT, KA), F32), pltpu.VMEM((HG, TT, KA), F32)],
        compiler_params=_params(),
    )(proj_t, kaug, vtok, proj_t, do_t, lse, dd, cq, after)


def _fgate_bwd(dcq, dck, sg, L):
    def body(dcq_ref, dck_ref, sg_ref, df_ref, db_ref):
        dc = jnp.concatenate([dcq_ref[h] - dck_ref[h] for h in range(H)], axis=0)
        idx = lax.broadcasted_iota(jnp.int32, (H, L), 1)
        r = dc
        s = 1
        while s < L:
            r = r + jnp.where(idx + s < L, pltpu.roll(r, L - s, 1), 0.0)
            s *= 2
        df = r * sg_ref[...]
        db_ref[...] = jnp.broadcast_to(jnp.sum(df, axis=1, keepdims=True), (H, TB))
        df_ref[...] = jnp.concatenate([df, jnp.zeros((DF - H, L), F32)], axis=0).astype(BF16)

    return pl.pallas_call(
        body, name="fgate_bwd",
        out_shape=[jax.ShapeDtypeStruct((DF, L), BF16), jax.ShapeDtypeStruct((H, TB), F32)],
        compiler_params=pltpu.CompilerParams(vmem_limit_bytes=VMEM_LIMIT),
    )(dcq, dck, sg)


def _inproj_bwd_x(w, dq_t, dk_t, dv_t, dg5_t, df_t, dout, x, meta_full, norm_g, L, after):
    nj = L // TT
    seq = x.shape[0]

    def body(w_ref, dq_ref, dk_ref, dv_ref, dg5_ref, df_ref, dout_ref, xa_ref, xb_ref, xc_ref, meta_ref, g_ref, _,
             gx_ref, dmeta_ref, dg_ref, dh_scr, sems):
        j = pl.program_id(0)
        slot = j % 2

        def copy_out(step, slot_):
            first = pltpu.make_async_copy(dh_scr.at[slot_, pl.ds(TB, TT - TB)], gx_ref.at[pl.ds(0, TT - TB)],
                                          sems.at[slot_])
            later = pltpu.make_async_copy(dh_scr.at[slot_], gx_ref.at[pl.ds(step * TT - TB, TT)], sems.at[slot_])
            return first, later

        @pl.when(j == 0)
        def _():
            dg_ref[...] = jnp.zeros_like(dg_ref)

        du = _dot(dq_ref[...], w_ref[0:DA, :], TN_DIMS)
        du += _dot(dk_ref[...], w_ref[DA:2 * DA, :], TN_DIMS)
        du += _dot(dv_ref[...], w_ref[2 * DA:3 * DA, :], TN_DIMS)
        du += _dot(dg5_ref[...], w_ref[3 * DA:NSEC * DA, :], TN_DIMS)
        du += _dot(df_ref[...], w_ref[NSEC * DA:DPROJ, :], TN_DIMS)
        hb = _h_tile(j, xa_ref, xb_ref, xc_ref, meta_ref)
        r = lax.rsqrt(jnp.mean(hb * hb, axis=-1, keepdims=True) + EPS)
        hn = hb * r
        dg_ref[...] += jnp.sum(du * hn, axis=0, keepdims=True)
        gu = du * g_ref[...]
        dh = dout_ref[...] + r * gu - hn * (r * jnp.mean(gu * hn, axis=-1, keepdims=True))

        dh_scr[slot] = dh

        @pl.when(j == 0)
        def _():
            dmeta_ref[...] = dh[P0:TB, :]
            copy_out(0, 0)[0].start()

        @pl.when(j >= 1)
        def _():
            copy_out(j, slot)[1].start()

        @pl.when(j == 1)
        def _():
            copy_out(0, 0)[0].wait()

        @pl.when(j >= 2)
        def _():
            copy_out(j - 1, 1 - slot)[1].wait()

        @pl.when(j == nj - 1)
        def _():
            copy_out(j, slot)[0 if nj == 1 else 1].wait()

    blk = lambda rows: pl.BlockSpec((rows, TT), lambda j: (0, j))
    return pl.pallas_call(
        body, name="inproj_bwd_x", grid=(nj,),
        in_specs=[_full_spec((DPROJ, D)), blk(DA), blk(DA), blk(DA), blk(5 * DA), blk(DF),
                  pl.BlockSpec((TT, D), lambda j: (j, 0))] + _x_specs3()
                 + [_full_spec((NM, D)), _full_spec((1, D)), _UNREAD],
        out_specs=[pl.BlockSpec(memory_space=pl.ANY), _full_spec((NM, D)), _full_spec((1, D))],
        out_shape=[jax.ShapeDtypeStruct((seq, D), F32), jax.ShapeDtypeStruct((NM, D), F32),
                   jax.ShapeDtypeStruct((1, D), F32)],
        scratch_shapes=[pltpu.VMEM((2, TT, D), F32), pltpu.SemaphoreType.DMA((2,))],
        compiler_params=_params(),
    )(w, dq_t, dk_t, dv_t, dg5_t, df_t, dout, x, x, x, meta_full, norm_g, after)


def _inproj_bwd_w(u, dq_t, dk_t, dv_t, dg5_t, df_t, L):
    def body(u_ref, dq_hbm, dk_hbm, dv_hbm, dg5_ref, df_ref, dw_ref, dwf_ref, qkv_scr, sems):
        s = pl.program_id(0)
        fetch = [pltpu.make_async_copy(src, qkv_scr.at[k], sems.at[k])
                 for k, src in enumerate((dq_hbm, dk_hbm, dv_hbm))]

        @pl.when(s == 0)
        def _():
            for cp in fetch:
                cp.start()

        @pl.when(s < 5)
        def _():
            dw_ref[...] = _dot(dg5_ref[...], u_ref[...])

        for k in range(3):
            @pl.when(s == 5 + k)
            def _(k=k):
                fetch[k].wait()
                dw_ref[...] = _dot(qkv_scr[k], u_ref[...])

        @pl.when(s == NSEC - 1)
        def _():
            dwf_ref[...] = _dot(df_ref[...], u_ref[...])

    once = lambda shape: pl.BlockSpec(shape, lambda s: (0, 0), pipeline_mode=pl.Buffered(1))
    any_spec = pl.BlockSpec(memory_space=pl.ANY)
    return pl.pallas_call(
        body, name="inproj_bwd_w", grid=(NSEC,),
        in_specs=[
            once((L, D)), any_spec, any_spec, any_spec,
            pl.BlockSpec((DA, L), lambda s: (jnp.minimum(s, 4), 0)),
            once((DF, L)),
        ],
        out_specs=[pl.BlockSpec((DA, D), lambda s: (jnp.where(s < 5, s + 3, s - 5), 0)), _full_spec((DF, D))],
        out_shape=[jax.ShapeDtypeStruct((NSEC * DA, D), F32), jax.ShapeDtypeStruct((DF, D), F32)],
        scratch_shapes=[pltpu.VMEM((3, DA, L), BF16), pltpu.SemaphoreType.DMA((3,))],
        compiler_params=_params(),
    )(u, dq_t, dk_t, dv_t, dg5_t, df_t)


def _adamw(w, g, m, v):
    m = ADAM_B1 * m + (1.0 - ADAM_B1) * g
    v = ADAM_B2 * v + (1.0 - ADAM_B2) * (g * g)
    m_hat = m / (1.0 - ADAM_B1 ** ADAM_STEP)
    v_hat = v / (1.0 - ADAM_B2 ** ADAM_STEP)
    delta = -ADAM_LR * (m_hat / (jnp.sqrt(v_hat) + ADAM_EPS) + ADAM_WD * w)
    return delta, m, v


def _adamw_big(own_in, land_in, own_out, land_out, w_in_t, m_in_t, v_in_t, w_out, m_out, v_out):
    cb = CB
    e_sh = D // NDEV
    in_shape = jax.ShapeDtypeStruct(w_in_t.shape, F32)
    out_shape = jax.ShapeDtypeStruct(w_out.shape, F32)

    def total(own_ref, land_ref, rows, chips):
        g = _pick_slab(0, own_ref, land_ref, rows, chips=chips).astype(F32)
        for j in range(1, own_ref.shape[0]):
            g = g + _pick_slab(j, own_ref, land_ref, rows, chips=chips).astype(F32)
        return g

    def body(oi_ref, li_ref, oo_ref, lo_ref, wi_ref, mi_ref, vi_ref, wo_ref, mo_ref, vo_ref,
             gi, di, mi, vi, go, do, mo, vo):
        g = total(oi_ref, li_ref, slice(0, WSHP), True)[:WSH]
        d, mn, vn = _adamw(wi_ref[...], g, mi_ref[...], vi_ref[...])
        gi[...], di[...], mi[...], vi[...] = g, d, mn, vn
        g = total(oo_ref, lo_ref, slice(0, e_sh), False)
        d, mn, vn = _adamw(wo_ref[0], g, mo_ref[0], vo_ref[0])
        go[0], do[0], mo[0], vo[0] = g, d, mn, vn

    slab = lambda n, rows: pl.BlockSpec((n, rows, cb), lambda i: (0, 0, i))
    ispec = pl.BlockSpec((WSH, cb), lambda i: (0, i))
    ospec = pl.BlockSpec((1, e_sh, cb), lambda i: (0, 0, i))
    return pl.pallas_call(
        body, name="adamw_big", grid=(D // cb,),
        in_specs=[slab(4, WSHP), slab(4, WSHP), slab(NDEV, e_sh), slab(NDEV, e_sh),
                  ispec, ispec, ispec, ospec, ospec, ospec],
        out_specs=[ispec] * 4 + [ospec] * 4, out_shape=[in_shape] * 4 + [out_shape] * 4,
        compiler_params=_params(),
    )(own_in, land_in, own_out, land_out, w_in_t, m_in_t, v_in_t, w_out, m_out, v_out)


F0 = 3 * DA


def _unshard_w_in(w_all, small_all, attn_gain, conv_gain):
    def body(w_ref, small_ref, ga_ref, gc_ref, wt_ref, meta_ref, cwb_ref, gab_ref, gcb_ref):
        i = pl.program_id(0)
        for k in range(CB // TB):
            meta_ref[:, k * TB:(k + 1) * TB] = small_ref[(CB // TB) * i + k, 0:NM, :]

        @pl.when(i == 0)
        def _():
            per_row = lambda line: jnp.broadcast_to(line, (TB, DA)).T
            cw = jnp.concatenate([small_ref[j, NM:NM + 3, 0:DH] for j in range(NDEV)], axis=1)
            for k in range(3):
                cwb_ref[k] = per_row(cw[k:k + 1, :])
            gab_ref[...] = per_row(ga_ref[...])
            gcb_ref[...] = per_row(gc_ref[...])

        def ref_rows(lo, hi):
            pieces, r = [], lo
            while r < hi:
                sh, off = divmod(r, WSH)
                n = min(hi - r, WSH - off)
                pieces.append(w_ref[sh, off:off + n, :])
                r += n
            return pieces

        for s in range(NSEC):
            lo = s * DA if s < 3 else s * DA + H
            wt_ref[s * DA:(s + 1) * DA, :] = jnp.concatenate(ref_rows(lo, lo + DA), axis=0)
        wt_ref[NSEC * DA:DPROJ, :] = jnp.concatenate(
            ref_rows(F0, F0 + H) + [jnp.zeros((DF - H, CB), BF16)], axis=0)

    return pl.pallas_call(
        body, name="unshard_w_in", grid=(D // CB,),
        in_specs=[pl.BlockSpec((NDEV, WSHP, CB), lambda i: (0, 0, i)), _full_spec(small_all.shape),
                  _full_spec((1, DA)), _full_spec((1, DA))],
        out_specs=[pl.BlockSpec((DPROJ, CB), lambda i: (0, i)), pl.BlockSpec((NM, CB), lambda i: (0, i)),
                   _full_spec((3, DA, TB)), _full_spec((DA, TB)), _full_spec((DA, TB))],
        out_shape=[jax.ShapeDtypeStruct((DPROJ, D), BF16), jax.ShapeDtypeStruct((NM, D), F32),
                   jax.ShapeDtypeStruct((3, DA, TB), F32), jax.ShapeDtypeStruct((DA, TB), F32),
                   jax.ShapeDtypeStruct((DA, TB), F32)],
        compiler_params=_params(),
    )(w_all, small_all, attn_gain, conv_gain)


def _shard_w_in_grads(dw_main, dw_f):
    def body(dm_ref, df_ref, p_ref):
        mc = lax.axis_index("c")

        def ref_rows(lo, hi):
            pieces, r = [], lo
            while r < hi:
                if r < F0:
                    n = min(hi, F0) - r
                    pieces.append(dm_ref[r:r + n, :])
                elif r < F0 + H:
                    n = min(hi, F0 + H) - r
                    pieces.append(df_ref[r - F0:r - F0 + n, :])
                else:
                    n = hi - r
                    pieces.append(dm_ref[r - H:r - H + n, :])
                r += n
            return pieces

        for i in range(NDEV):
            rows = jnp.concatenate(ref_rows(i * WSH, (i + 1) * WSH) + [jnp.zeros((WSHP - WSH, CB), F32)], axis=0)
            p_ref[i // 2 + jnp.where(mc == i % 2, 0, 4)] = rows.astype(BF16)

    col = lambda rows: pl.BlockSpec((rows, CB), lambda i: (0, i))
    return pl.pallas_call(
        body, name="shard_w_in_grads", grid=(D // CB,),
        in_specs=[col(NSEC * DA), col(DF)],
        out_specs=pl.BlockSpec((NDEV, WSHP, CB), lambda i: (0, 0, i)),
        out_shape=jax.ShapeDtypeStruct((NDEV, WSHP, D), BF16),
        compiler_params=_params(),
    )(dw_main, dw_f)


SMALL = ("norm_g", "final_norm_g", "attn_norm_g", "conv_norm_g", "b_f", "meta", "conv_w")


def _as_rows(x):
    return jnp.concatenate([x[:, r * TB:(r + 1) * TB] for r in range(x.shape[1] // TB)], axis=0)


def _as_line(rows):
    return jnp.concatenate([rows[r:r + 1, :] for r in range(rows.shape[0])], axis=1)


def _pad_rows(x, n=8):
    return jnp.concatenate([x, jnp.zeros((n - x.shape[0], x.shape[1]), F32)], axis=0)


def _tile_rows(a, rows, lanes=TB):
    a = a.reshape(rows, lanes)
    return jnp.pad(a, ((0, -rows % 8), (0, TB - lanes)))


def _pack_small_grads(dg_norm, dg_final, dga_p, dgc_p, dcw_p, db_b, dmeta, loss):
    def body(dgn_ref, dgf_ref, dga_ref, dgc_ref, dcw_ref, db_ref, dmeta_ref, loss_ref, out_ref):
        def lane_sums(p):
            return jnp.sum(p.T, axis=0, keepdims=True)

        lane = lax.broadcasted_iota(jnp.int32, (1, TB), 1)
        b_row = jnp.where(lane == H, loss_ref[...], 0.0)
        for h in range(H):
            b_row = b_row + jnp.where(lane == h, db_ref[h:h + 1, :], 0.0)
        common = jnp.concatenate([
            _as_rows(dgn_ref[...]), _as_rows(dgf_ref[...]), _pad_rows(_as_rows(lane_sums(dga_ref[...]))),
            _pad_rows(_as_rows(lane_sums(dgc_ref[...]))), _pad_rows(b_row)], axis=0)
        dcw = [lane_sums(dcw_ref[k]) for k in range(3)]
        for j in range(NDEV):
            cw = jnp.concatenate(
                [jnp.concatenate([r[:, j * DH:(j + 1) * DH], jnp.zeros((1, TB - DH), F32)], axis=1) for r in dcw],
                axis=0)
            out_ref[j] = jnp.concatenate([common, dmeta_ref[:, j * TB:(j + 1) * TB], _pad_rows(cw)], axis=0)

    return pl.pallas_call(
        body, name="pack_small_grads", out_shape=jax.ShapeDtypeStruct((NDEV, SROWS, TB), F32),
    )(dg_norm, dg_final, dga_p, dgc_p, dcw_p, db_b, dmeta, loss)


def _adamw_small(own, land, params):
    flat = [a for n in SMALL for a in params[n]]

    def body(*refs):
        own_ref, land_ref = refs[:2]
        ins = refs[2:2 + 3 * len(SMALL)]
        outs = refs[2 + 3 * len(SMALL):]
        g = _pick_slab(0, own_ref, land_ref, slice(0, SROWS))
        for j in range(1, NDEV):
            g = g + _pick_slab(j, own_ref, land_ref, slice(0, SROWS))
        grads = dict(
            norm_g=_as_line(g[0:8]), final_norm_g=_as_line(g[8:16]), attn_norm_g=_as_line(g[16:20]),
            conv_norm_g=_as_line(g[24:28]), b_f=g[32:33, :H], meta=g[40:56], conv_w=g[56:59, :DH][None])
        for i, n in enumerate(SMALL):
            w_ref, m_ref, v_ref = ins[3 * i:3 * i + 3]
            d, mn, vn = _adamw(w_ref[...], grads[n], m_ref[...], v_ref[...])
            for o_ref, val in zip(outs[4 * i:4 * i + 4], (grads[n], d, mn, vn)):
                o_ref[...] = val
        outs[-1][...] = g[32:33, H:H + 1]

    shapes = [jax.ShapeDtypeStruct(params[n][0].shape, F32) for n in SMALL for _ in range(4)]
    res = pl.pallas_call(
        body, name="adamw_small", out_shape=shapes + [jax.ShapeDtypeStruct((1, 1), F32)],
    )(own, land, *flat)
    return {n: res[4 * i:4 * i + 4] for i, n in enumerate(SMALL)}, res[-1]


def kernel(x, meta, norm_g, w_in, b_f, conv_w, attn_norm_g, conv_norm_g, w_out, final_norm_g, loss_target, m_meta, m_norm_g, m_w_in, m_b_f, m_conv_w, m_attn_norm_g, m_conv_norm_g, m_w_out, m_final_norm_g, v_meta, v_norm_g, v_w_in, v_b_f, v_conv_w, v_attn_norm_g, v_conv_norm_g, v_w_out, v_final_norm_g):
    seq = x.shape[1]
    L = seq + TB
    assert x.shape == (1, seq, D) and L % TT == 0 and w_in.shape == (1, D, WSH)
    x2 = x[0]
    tgt = loss_target[0]

    w_in_slab = jnp.pad(w_in[0].T, ((0, WSHP - WSH), (0, 0))).astype(BF16)
    w_out_slab = w_out[0].astype(BF16)
    meta_slab = jnp.concatenate([meta, _tile_rows(conv_w[0], 3, DH)], axis=0)
    wout_flight = _split_start(w_out_slab, "gather_w_out_start", per_peer=False)
    w_all, small_all = _all_gather([w_in_slab, meta_slab], "gather_w_in")

    w_t, meta_full, cw_b, ga_b, gcn_b = _unshard_w_in(w_all, small_all, attn_norm_g, conv_norm_g)

    u, proj_t, gate_t, f_t, ktok, vtok = _inproj_fwd(x2, meta_full, norm_g, w_t, L, after=wout_flight[4])
    cq, kaug, sg = _fgate_fwd(f_t, b_f.reshape(H, 1), ktok, L)
    o_t, lse = _attn_fwd(proj_t, kaug, cq, L)

    w_out_own, w_out_land = _split_wait(wout_flight, o_t, "gather_w_out_wait", per_peer=False)
    dout, dw_out, loss_part, dg_final, do_t, dd, dg5_t, dga_p, dgc_p, dcw_p = _gate_outproj(
        o_t, gate_t, cw_b, ga_b, gcn_b, w_out_own, w_out_land, x2, meta_full, final_norm_g.reshape(1, D), tgt, L)
    dwo_flight = _split_start(dw_out.reshape(NDEV, D // NDEV, D), "exchange_dw_out_start", per_peer=True)
    dq_t, dk_t, dv_t, dck, dcq = _attn_bwd(proj_t, kaug, vtok, do_t, lse, dd, cq, L, after=dwo_flight[4])
    df_t, db_f = _fgate_bwd(dcq, dck, sg, L)
    dw_main, dw_f = _inproj_bwd_w(u, dq_t, dk_t, dv_t, dg5_t, df_t, L)
    dwi_parts = _shard_w_in_grads(dw_main, dw_f)
    dwi_chip = _pair_sum(dwi_parts, _pair_exchange(dwi_parts, "exchange_dw_in_pair"))
    dwi_flight = _split_start(dwi_chip, "exchange_dw_in_start", per_peer=True, chips=True)
    grad_x, dmeta, dg_norm = _inproj_bwd_x(
        w_t, dq_t, dk_t, dv_t, dg5_t, df_t, dout, x2, meta_full, norm_g, L, after=dwi_flight[4])
    small_parts = _pack_small_grads(dg_norm, dg_final, dga_p, dgc_p, dcw_p, db_f, dmeta, loss_part)
    small_flight = _split_start(small_parts, "exchange_small_start", per_peer=True)
    dwo_own, dwo_land = _split_wait(dwo_flight, small_flight[4], "exchange_dw_out_wait", per_peer=True)
    dwi_own, dwi_land = _split_wait(dwi_flight, dwo_land, "exchange_dw_in_wait", per_peer=True, chips=True)

    big_out = _adamw_big(dwi_own, dwi_land, dwo_own, dwo_land,
                         w_in[0].T, m_w_in[0].T, v_w_in[0].T, w_out, m_w_out, v_w_out)
    g_w_in, d_w_in, nm_w_in, nv_w_in = [a.T[None] for a in big_out[:4]]
    g_w_out, d_w_out, nm_w_out, nv_w_out = big_out[4:]
    sm_own, sm_land = _split_wait(small_flight, big_out[4], "exchange_small_wait", per_peer=True)
    line = lambda a: a.reshape(1, D)
    small, loss = _adamw_small(sm_own, sm_land, dict(
        norm_g=(norm_g, m_norm_g, v_norm_g),
        final_norm_g=(line(final_norm_g), line(m_final_norm_g), line(v_final_norm_g)),
        attn_norm_g=(attn_norm_g, m_attn_norm_g, v_attn_norm_g),
        conv_norm_g=(conv_norm_g, m_conv_norm_g, v_conv_norm_g),
        b_f=(b_f, m_b_f, v_b_f), meta=(meta, m_meta, v_meta), conv_w=(conv_w, m_conv_w, v_conv_w)))
    small["final_norm_g"] = [a.reshape(D) for a in small["final_norm_g"]]
    order = ("meta", "norm_g", "w_in", "b_f", "conv_w", "attn_norm_g", "conv_norm_g", "w_out", "final_norm_g")
    groups = []
    for k, (wi, wo) in enumerate(((g_w_in, g_w_out), (d_w_in, d_w_out), (nm_w_in, nm_w_out), (nv_w_in, nv_w_out))):
        d = dict({n: small[n][k] for n in SMALL}, w_in=wi, w_out=wo)
        groups.append([d[n] for n in order])
    return (loss[0, 0], grad_x[None], *groups[0], *groups[1], *groups[2], *groups[3])
```

```python
import jax
import jax.numpy as jnp
from jax import lax
from jax.experimental import pallas as pl
from jax.experimental.pallas import tpu as pltpu

F32 = jnp.float32
BF16 = jnp.bfloat16

D = 1024
DA = 512
H = 8
DH = 64
NM = 16
TB = 128
P0 = TB - NM
TT = 3 * TB
HG = 8
NDEV = 8
NSEC = 8
DF = 16
DPROJ = NSEC * DA + DF
WSH = 513
WSHP = 528
WROWS = WSHP + D // NDEV
SROWS = 64
EPS = 1e-6
NEG = -1e30
LOG2E = 1.4426950408889634
LN2 = 0.6931471805599453
QSCALE = DH ** -0.5 * LOG2E
KA = 128
CB = 256
VMEM_LIMIT = 56 * 1024 * 1024

ADAM_LR = 0.001
ADAM_B1 = 0.9
ADAM_B2 = 0.999
ADAM_EPS = 1e-08
ADAM_WD = 0.01
ADAM_STEP = 10

NT_DIMS = (((1,), (1,)), ((), ()))
TN_DIMS = (((0,), (0,)), ((), ()))
MESH = pl.DeviceIdType.MESH


def _params(n_axes=1, vmem=VMEM_LIMIT):
    return pltpu.CompilerParams(dimension_semantics=("arbitrary",) * n_axes, vmem_limit_bytes=vmem)


def _dot(a, b, dims=None):
    if dims is None:
        return jnp.dot(a, b, preferred_element_type=F32)
    return lax.dot_general(a, b, dims, preferred_element_type=F32)


def _my_place():
    return lax.axis_index("x"), lax.axis_index("y"), lax.axis_index("c")


def _all_gather(xs, name):
    n = len(xs)

    def body(*refs):
        x_refs, out_refs = refs[:n], refs[n:2 * n]
        send_sems, recv_sems, local_sems = refs[2 * n:]
        mx, my, mc = _my_place()

        def across(px, py, pc, axis_a):
            flip_x = pc if axis_a else 1 - pc
            return (px + flip_x) % 2, (py + 1 - flip_x) % 2, pc

        def idx(p):
            return 4 * p[0] + 2 * p[1] + p[2]

        me, sib = (mx, my, mc), (mx, my, 1 - mc)
        a_nbr, b_nbr = across(*me, True), across(*me, False)
        diag = across(*b_nbr, True)
        sib_a, sib_b = across(*sib, True), across(*sib, False)
        sib_diag = across(*sib_b, True)

        waits = []
        for t in range(n):
            out_ref = out_refs[t]

            def copy(k, block, to, src=None, out_ref=out_ref, t=t):
                return pltpu.make_async_remote_copy(
                    src_ref=out_ref.at[idx(block)] if src is None else src, dst_ref=out_ref.at[idx(block)],
                    send_sem=send_sems.at[7 * t + k], recv_sem=recv_sems.at[7 * t + k],
                    device_id=to, device_id_type=MESH)

            mine = pltpu.make_async_copy(x_refs[t], out_ref.at[idx(me)], local_sems.at[t])
            mine.start()
            started = [copy(0, me, sib, src=x_refs[t]), copy(1, me, a_nbr, src=x_refs[t]),
                       copy(2, me, b_nbr, src=x_refs[t])]
            for cp in started:
                cp.start()
            waits.append((copy, mine, started))
        relays = ((1, a_nbr, ((3, b_nbr), (4, sib))), (2, b_nbr, ((5, sib),)), (3, diag, ((6, sib),)))
        for landed, block, onward in relays:
            for copy, _, started in waits:
                copy(landed, block, me).wait_recv()
                for k, to in onward:
                    started.append(copy(k, block, to))
                    started[-1].start()
        for copy, mine, started in waits:
            for k, block in ((0, sib), (4, sib_a), (5, sib_b), (6, sib_diag)):
                copy(k, block, me).wait_recv()
            for cp in started:
                cp.wait_send()
            mine.wait()

    any_spec = pl.BlockSpec(memory_space=pl.ANY)
    return pl.pallas_call(
        body, name=name,
        out_shape=[jax.ShapeDtypeStruct((NDEV,) + x.shape, x.dtype) for x in xs],
        in_specs=[any_spec] * n, out_specs=[any_spec] * n,
        scratch_shapes=[pltpu.SemaphoreType.DMA((7 * n,)), pltpu.SemaphoreType.DMA((7 * n,)),
                        pltpu.SemaphoreType.DMA((n,))],
    )(*xs)


_HBM = pl.BlockSpec(memory_space=pltpu.HBM)
_UNREAD = pl.BlockSpec(memory_space=pl.ANY)
_SEM = pl.BlockSpec(memory_space=pltpu.SEMAPHORE)
_EFFECT = pltpu.SideEffectType.DATAFLOW_SIDE_EFFECTING


def _peer_of(m, place):
    mx, my, mc = place
    return ((1 - mx) if m & 4 else mx, (1 - my) if m & 2 else my, (1 - mc) if m & 1 else mc)


def _party(chips):
    if chips:
        return (lambda p: 2 * p[0] + p[1]), (2, 4, 6)
    return (lambda p: 4 * p[0] + 2 * p[1] + p[2]), tuple(range(1, NDEV))


def _split_copies(src_ref, land_ref, send_sems, recv_sems, per_peer, incoming, chips):
    place = _my_place()
    slot, masks = _party(chips)
    me = slot(place)
    out = []
    for k, m in enumerate(masks):
        there = _peer_of(m, place)
        peer = slot(there)
        src = (src_ref.at[me] if incoming else src_ref.at[peer]) if per_peer else src_ref
        out.append(pltpu.make_async_remote_copy(
            src_ref=src, dst_ref=land_ref.at[peer if incoming else me],
            send_sem=send_sems.at[k], recv_sem=recv_sems.at[k], device_id=there, device_id_type=MESH))
    return out


def _split_start(src, name, per_peer, chips=False):
    slab = src.shape[1:] if per_peer else src.shape
    n = len(_party(chips)[1])

    def body(src_ref, land_ref, send_sems, recv_sems, src_thru, land_thru, token):
        for cp in _split_copies(src_ref, land_ref, send_sems, recv_sems, per_peer, False, chips):
            cp.start()
        token[...] = jnp.zeros_like(token)

    return pl.pallas_call(
        body, name=name,
        out_shape=(pltpu.SemaphoreType.DMA((n,)), pltpu.SemaphoreType.DMA((n,)),
                   pltpu.HBM(src.shape, src.dtype), pltpu.HBM((n + 1,) + slab, src.dtype),
                   jax.ShapeDtypeStruct((8, TB), F32)),
        in_specs=(_HBM, _HBM), out_specs=(_SEM, _SEM, _HBM, _HBM, pl.BlockSpec(memory_space=pltpu.VMEM)),
        input_output_aliases={0: 2, 1: 3},
        compiler_params=pltpu.CompilerParams(has_side_effects=_EFFECT),
    )(pltpu.with_memory_space_constraint(src, pltpu.HBM),
      pltpu.with_memory_space_constraint(lax.empty((n + 1,) + slab, src.dtype), pltpu.HBM))


def _split_wait(handles, after, name, per_peer, chips=False):
    send_sems, recv_sems, src_thru, land_thru, _ = handles

    def body(src_ref, land_ref, send_sems, recv_sems, after_ref, src_out, land_out):
        for cp in _split_copies(src_ref, land_ref, send_sems, recv_sems, per_peer, False, chips):
            cp.wait_send()
        for cp in _split_copies(src_ref, land_ref, send_sems, recv_sems, per_peer, True, chips):
            cp.wait_recv()

    return pl.pallas_call(
        body, name=name,
        out_shape=(pltpu.HBM(src_thru.shape, src_thru.dtype), pltpu.HBM(land_thru.shape, land_thru.dtype)),
        in_specs=(_HBM, _HBM, _SEM, _SEM, pl.BlockSpec(memory_space=pl.ANY)), out_specs=(_HBM, _HBM),
        input_output_aliases={0: 0, 1: 1},
        compiler_params=pltpu.CompilerParams(has_side_effects=_EFFECT),
    )(src_thru, land_thru, send_sems, recv_sems, after)


def _pick_slab(j, own_ref, land_ref, rows, per_peer=True, chips=False):
    me = _party(chips)[0](_my_place())
    own = (lambda: own_ref[j, rows, :]) if per_peer else (lambda: own_ref[rows, :])
    return lax.cond(me == j, own, lambda: land_ref[j, rows, :])


def _pair_exchange(p, name):
    def body(p_ref, got_ref, send_sems, recv_sems):
        mx, my, mc = _my_place()
        copies = [pltpu.make_async_remote_copy(
            src_ref=p_ref.at[4 + q], dst_ref=got_ref.at[q], send_sem=send_sems.at[q],
            recv_sem=recv_sems.at[q], device_id=(mx, my, 1 - mc), device_id_type=MESH) for q in range(4)]
        for cp in copies:
            cp.start()
        for cp in copies:
            cp.wait_recv()
        for cp in copies:
            cp.wait_send()

    any_spec = pl.BlockSpec(memory_space=pl.ANY)
    return pl.pallas_call(
        body, name=name, out_shape=jax.ShapeDtypeStruct((4,) + p.shape[1:], p.dtype),
        in_specs=[any_spec], out_specs=any_spec,
        scratch_shapes=[pltpu.SemaphoreType.DMA((4,)), pltpu.SemaphoreType.DMA((4,))],
    )(p)


def _pair_sum(p, got):
    rows = p.shape[1]

    def body(p_ref, got_ref, out_ref):
        for q in range(4):
            out_ref[q] = (p_ref[q].astype(F32) + got_ref[q].astype(F32)).astype(BF16)

    blk = lambda n: pl.BlockSpec((n, rows, CB), lambda i: (0, 0, i))
    return pl.pallas_call(
        body, name="pair_sum", grid=(D // CB,), in_specs=[blk(4), blk(4)], out_specs=blk(4),
        out_shape=jax.ShapeDtypeStruct((4, rows, D), BF16), compiler_params=_params(),
    )(p, got)


def _h_block(t, x_ref, meta_ref):
    first = jnp.concatenate([jnp.zeros((P0, D), F32), meta_ref[...]], axis=0)
    return jnp.where(t == 0, first, x_ref[...])


def _x_specs3(tile=lambda j: j):
    return [pl.BlockSpec((TB, D), lambda j: (jnp.maximum(3 * tile(j) - 1, 0), 0)),
            pl.BlockSpec((TB, D), lambda j: (3 * tile(j), 0)),
            pl.BlockSpec((TB, D), lambda j: (3 * tile(j) + 1, 0))]


def _h_tile(j, xa_ref, xb_ref, xc_ref, meta_ref):
    first = jnp.concatenate([jnp.zeros((P0, D), F32), meta_ref[...]], axis=0)
    return jnp.concatenate([jnp.where(j == 0, first, xa_ref[...]), xb_ref[...], xc_ref[...]], axis=0)


def _full_spec(shape):
    return pl.BlockSpec(shape, lambda *_: (0,) * len(shape))


def _sigmoid(z):
    return 1.0 / (1.0 + jnp.exp(-z))


def _lane_tiles_sum(x):
    out = x[:, :TB]
    for i in range(1, x.shape[1] // TB):
        out = out + x[:, i * TB:(i + 1) * TB]
    return out


def _inproj_fwd(x, meta_full, norm_g, w_t, L, after):
    nj = L // TT

    def body(xa_ref, xb_ref, xc_ref, meta_ref, g_ref, w_ref, _, u_ref, proj_ref, gate_ref, f_ref, ktok_ref, vtok_ref):
        hb = _h_tile(pl.program_id(0), xa_ref, xb_ref, xc_ref, meta_ref)
        r = lax.rsqrt(jnp.mean(hb * hb, axis=-1, keepdims=True) + EPS)
        u = (hb * r * g_ref[...]).astype(BF16)
        u_ref[...] = u
        for s in range(NSEC):
            p = _dot(u_ref[...], w_ref[s * DA:(s + 1) * DA, :], NT_DIMS)
            if s == 0:
                p = p * QSCALE
            if s in (1, 2):
                tok_ref = ktok_ref if s == 1 else vtok_ref
                for h in range(H):
                    tok_ref[h] = p[:, h * DH:(h + 1) * DH].astype(BF16)
            out_ref, s_out = (proj_ref, s) if s < 3 else (gate_ref, s - 3)
            out_ref[s_out * DA:(s_out + 1) * DA, :] = p.T.astype(BF16)
        f_ref[...] = _dot(w_ref[NSEC * DA:DPROJ, :], u_ref[...], NT_DIMS)[:H]

    return pl.pallas_call(
        body, name="inproj_fwd", grid=(nj,),
        in_specs=_x_specs3() + [_full_spec((NM, D)), _full_spec((1, D)), _full_spec((DPROJ, D)), _UNREAD],
        out_specs=[
            pl.BlockSpec((TT, D), lambda t: (t, 0)),
            pl.BlockSpec((3 * DA, TT), lambda t: (0, t)),
            pl.BlockSpec((None, (NSEC - 3) * DA, TT), lambda t: (t, 0, 0)),
            pl.BlockSpec((H, TT), lambda t: (0, t)),
            pl.BlockSpec((H, TT, DH), lambda t: (0, t, 0)),
            pl.BlockSpec((H, TT, DH), lambda t: (0, t, 0)),
        ],
        out_shape=[
            jax.ShapeDtypeStruct((L, D), BF16),
            jax.ShapeDtypeStruct((3 * DA, L), BF16),
            jax.ShapeDtypeStruct((nj, (NSEC - 3) * DA, TT), BF16),
            jax.ShapeDtypeStruct((H, L), F32),
            jax.ShapeDtypeStruct((H, L, DH), BF16),
            jax.ShapeDtypeStruct((H, L, DH), BF16),
        ],
        compiler_params=_params(),
    )(x, x, x, meta_full, norm_g, w_t, after)


def _split3(x):
    hi = x.astype(BF16).astype(F32)
    r = x - hi
    mid = r.astype(BF16).astype(F32)
    return hi, mid, (r - mid).astype(BF16).astype(F32)


def _bias_rows(bias):
    one = jnp.ones((1, TT), F32)
    zero = jnp.zeros((1, TT), F32)
    parts = [zero] * 3 if bias is None else list(_split3(bias))
    return jnp.concatenate([one] * 3 + parts + [zero] * (DF - 6), axis=0).astype(BF16)


def _fgate_fwd(f_t, b_col, ktok, L):
    nb = L // TB

    def body(f_ref, b_ref, ktok_ref, cq_ref, kaug_ref, sg_ref, bias_scr):
        h = pl.program_id(0)

        @pl.when(h == 0)
        def _():
            z = f_ref[...] + b_ref[...]
            idx = lax.broadcasted_iota(jnp.int32, (H, L), 1)
            real = idx >= P0
            lf = jnp.where(real, jnp.minimum(z, 0.0) - jnp.log1p(jnp.exp(-jnp.abs(z))), 0.0)
            sg_ref[...] = jnp.where(real, 1.0 / (1.0 + jnp.exp(z)), 0.0)
            c = lf
            s = 1
            while s < L:
                c = c + jnp.where(idx >= s, pltpu.roll(c, s, 1), 0.0)
                s *= 2
            c = c * LOG2E
            for hh in range(H):
                cq_ref[hh] = c[hh:hh + 1, :]
            for part, val in enumerate(_split3(-jnp.where(real, c, -NEG))):
                for hh in range(H):
                    bias_scr[part * H + hh] = val[hh:hh + 1, :]

        lane = lax.broadcasted_iota(jnp.int32, (TB, KA), 1)
        head = jnp.zeros((DH, TB), F32)
        tail = jnp.concatenate([jnp.ones((3, TB), F32), jnp.zeros((KA - DH - 6, TB), F32)], axis=0)
        for b in range(nb):
            blk = slice(b * TB, (b + 1) * TB)
            cols = jnp.concatenate(
                [head] + [bias_scr[part * H + h, :, blk] for part in range(3)] + [tail], axis=0).T
            k = jnp.concatenate([ktok_ref[0, blk, :].astype(F32), jnp.zeros((TB, KA - DH), F32)], axis=1)
            kaug_ref[0, blk, :] = jnp.where(lane < DH, k, cols).astype(BF16)

    return pl.pallas_call(
        body, name="fgate_fwd", grid=(H,),
        in_specs=[_full_spec((H, L)), _full_spec((H, 1)), pl.BlockSpec((1, L, DH), lambda h: (h, 0, 0))],
        out_specs=[_full_spec((H, 1, L)), pl.BlockSpec((1, L, KA), lambda h: (h, 0, 0)), _full_spec((H, L))],
        out_shape=[
            jax.ShapeDtypeStruct((H, 1, L), F32),
            jax.ShapeDtypeStruct((H, L, KA), BF16),
            jax.ShapeDtypeStruct((H, L), F32),
        ],
        scratch_shapes=[pltpu.VMEM((3 * H, 1, L), F32)],
        compiler_params=_params(),
    )(f_t, b_col, ktok)


def _causal_mask():
    r = lax.broadcasted_iota(jnp.int32, (TT, TT), 0)
    c = lax.broadcasted_iota(jnp.int32, (TT, TT), 1)
    return r <= c


def _attn_fwd(proj_t, kaug, cq, L):
    nq = L // TT

    def body(q_ref, qn_ref, kaug_ref, v_ref, cq_ref, o_ref, lse_ref,
             qa_scr, s_scr, cmax_scr, m_scr, p_scr, alpha_scr, acc_scr):
        j = pl.program_id(0)
        rows = [slice(g * DH, (g + 1) * DH) for g in range(HG)]
        ones = jnp.ones((DF, TT), BF16)

        def load_queries(ref):
            for g in range(HG):
                qa_scr[g] = jnp.concatenate(
                    [ref[rows[g], :], _bias_rows(None), jnp.zeros((KA - DH - DF, TT), BF16)], axis=0)

        def scores(kt, masked):
            k_off = pl.multiple_of(kt * TT, TT)
            for g in range(HG):
                s = _dot(kaug_ref[g, pl.ds(k_off, TT), :], qa_scr[g])
                if masked:
                    s = jnp.where(_causal_mask(), s, NEG)
                s_scr[g] = s
                cmax_scr[g] = jnp.max(s, axis=0, keepdims=True)

        def softmax():
            for g in range(HG):
                m_old = m_scr[g]
                m_new = jnp.maximum(m_old, cmax_scr[g])
                alpha_scr[g] = jnp.exp2(m_old - m_new)
                p_scr[g] = jnp.exp2(s_scr[g] - m_new).astype(BF16)
                m_scr[g] = m_new

        def weighted_sum(kt):
            k_off = pl.multiple_of(kt * TT, TT)
            for g in range(HG):
                v1 = jnp.concatenate([v_ref[rows[g], pl.ds(k_off, TT)], ones], axis=0)
                acc_scr[g] = alpha_scr[g] * acc_scr[g] + _dot(v1, p_scr[g])

        @pl.when(j == 0)
        def _():
            load_queries(q_ref)
            scores(0, True)

        m_scr[...] = jnp.full_like(m_scr, NEG)
        acc_scr[...] = jnp.zeros_like(acc_scr)

        @pl.when(j >= 1)
        def _():
            softmax()
            scores(j - 1, False)

        def step(i, c):
            weighted_sum(j - i + 1)
            softmax()
            scores(j - i - 1, False)
            return c

        lax.fori_loop(1, j, step, 0)

        def drain(second_last, next_tile):
            if second_last:
                weighted_sum(1)
            softmax()
            if next_tile:
                load_queries(qn_ref)
                scores(j + 1, True)
            weighted_sum(0)

        @pl.when(j == 0)
        def _():
            drain(False, nq > 1)

        @pl.when((j >= 1) & (j < nq - 1))
        def _():
            drain(True, True)

        @pl.when((j >= 1) & (j == nq - 1))
        def _():
            drain(True, False)

        for g in range(HG):
            l = acc_scr[g, DH:DH + 1, :]
            o_ref[rows[g], :] = acc_scr[g, :DH, :] * (1.0 / l)
            lse_ref[g] = m_scr[g] + jnp.log2(l) + cq_ref[g]

    assert HG == H
    return pl.pallas_call(
        body, name="attn_fwd", grid=(nq,),
        in_specs=[
            pl.BlockSpec((DA, TT), lambda j: (0, j)),
            pl.BlockSpec((DA, TT), lambda j: (0, jnp.minimum(j + 1, nq - 1))),
            pl.BlockSpec((H, L, KA), lambda j: (0, 0, 0)),
            pl.BlockSpec((DA, L), lambda j: (2, 0)),
            pl.BlockSpec((H, 1, TT), lambda j: (0, 0, j)),
        ],
        out_specs=[
            pl.BlockSpec((None, DA, TT), lambda j: (j, 0, 0)),
            pl.BlockSpec((H, 1, TT), lambda j: (0, 0, j)),
        ],
        out_shape=[jax.ShapeDtypeStruct((nq, DA, TT), F32), jax.ShapeDtypeStruct((H, 1, L), F32)],
        scratch_shapes=[pltpu.VMEM((HG, KA, TT), BF16), pltpu.VMEM((HG, TT, TT), F32), pltpu.VMEM((HG, 1, TT), F32),
                        pltpu.VMEM((HG, 1, TT), F32), pltpu.VMEM((HG, TT, TT), BF16), pltpu.VMEM((HG, 1, TT), F32),
                        pltpu.VMEM((HG, DH + DF, TT), F32)],
        compiler_params=_params(),
    )(proj_t, proj_t, kaug, proj_t, cq)


def _gate_group(rows, o_ref, za_ref, gb_ref, gc_ref, xc_ref, zc_ref, gcp_ref, xcp_ref, cw_ref, ga_ref, gcn_ref, first):
    n_rep = TT // TB
    f32 = lambda r: r[rows, :].astype(F32)
    o, za, gb, gc, xc, zc = o_ref[rows, :], f32(za_ref), f32(gb_ref), f32(gc_ref), f32(xc_ref), f32(zc_ref)
    a = gc * xc
    a_prev = jnp.where(first, 0.0, f32(gcp_ref) * f32(xcp_ref))
    full = jnp.concatenate([a_prev, a], axis=1)
    a1 = pltpu.roll(full, 1, 1)[:, TB:]
    a2 = pltpu.roll(full, 2, 1)[:, TB:]
    w0 = jnp.tile(cw_ref[0, rows, :], (1, n_rep))
    w1 = jnp.tile(cw_ref[1, rows, :], (1, n_rep))
    w2 = jnp.tile(cw_ref[2, rows, :], (1, n_rep))
    cv = w0 * a2 + w1 * a1 + w2 * a
    e = gb * cv
    rc = lax.rsqrt(jnp.mean(e * e, axis=0, keepdims=True) + EPS)
    ec = e * rc
    ra = lax.rsqrt(jnp.mean(o * o, axis=0, keepdims=True) + EPS)
    oa = o * ra
    g_a = jnp.tile(ga_ref[rows, :], (1, n_rep))
    g_c = jnp.tile(gcn_ref[rows, :], (1, n_rep))
    sa = _sigmoid(za)
    sc = _sigmoid(zc)
    return dict(o=o, za=za, gb=gb, gc=gc, xc=xc, zc=zc, a=a, a1=a1, a2=a2, w0=w0, w1=w1, w2=w2, cv=cv, e=e,
                rc=rc, ec=ec, ra=ra, oa=oa, g_a=g_a, g_c=g_c, sa=sa, sc=sc)


def _gate_specs(tile):
    halo = pl.BlockSpec((None, 2 * DA, TB),
                        lambda i: (jnp.maximum(tile(i) - 1, 0), 1, TT // TB - 1))
    return [pl.BlockSpec((None, DA, TT), lambda i: (tile(i), 0, 0)),
            pl.BlockSpec((None, 5 * DA, TT), lambda i: (tile(i), 0, 0)), halo,
            _full_spec((3, DA, TB)), _full_spec((DA, TB)), _full_spec((DA, TB))]


def _gate_views(g5_ref, halo_ref):
    return [g5_ref.at[pl.ds(s * DA, DA)] for s in range(5)] + [halo_ref.at[pl.ds(s * DA, DA)] for s in range(2)]


def _gate_outproj(o_t, gate_t, cw_b, ga_b, gcn_b, w_own, w_land, x, meta_full, fng, target, L):
    nj = L // TT
    rp = NM
    n_bwd = 8
    cb = D // 4
    n_ring = 5
    e_sh = D // NDEV
    assert P0 % rp == 0 and TB % rp == 0 and (TT // rp) % n_bwd == 0 and H == n_bwd

    def body(o_hbm, g5_hbm, halo_ref, cw_ref, ga_ref, gcn_ref, halo2_ref,
             wown_ref, wland_ref, xa_ref, xb_ref, xc_ref, meta_ref, g_ref, ta_ref, tb_ref, tc_ref,
             dout_ref, dwb_ref, loss_ref, dg_ref, do_ref, dd_ref, dg5_ref, dga_ref, dgc_ref, dcw_ref,
             dw_ref, o_scr, db_new, db_old, mix_new, mix_old, dmix_new, dmix_old, sq_acc, dg_acc, carry_ref,
             o_ring, g5_ring, ring_sems, w_ref):
        t = pl.program_id(0)

        def fetch(step):
            tile, slot = nj - 1 - step, step % n_ring
            return (pltpu.make_async_copy(o_hbm.at[tile], o_ring.at[slot], ring_sems.at[0, slot]),
                    pltpu.make_async_copy(g5_hbm.at[tile], g5_ring.at[slot], ring_sems.at[1, slot]))

        @pl.when(t == 0)
        def _():
            for step in range(2):
                for cp in fetch(step):
                    cp.start()

        @pl.when(t + 2 < nj)
        def _():
            for cp in fetch(t + 2):
                cp.start()

        @pl.when(t < nj)
        def _():
            for cp in fetch(t):
                cp.wait()

        slot_a, slot_c = t % n_ring, (t + n_ring - 2) % n_ring
        o_ref, o2_ref = o_ring.at[slot_a], o_ring.at[slot_c]
        za_ref, gb_ref, gc_ref, xcv_ref, zc_ref, gcp_ref, xcp_ref = _gate_views(g5_ring.at[slot_a], halo_ref)
        za2_ref, gb2_ref, gc2_ref, xcv2_ref, zc2_ref, gcp2_ref, xcp2_ref = _gate_views(g5_ring.at[slot_c], halo2_ref)
        first_a = t == nj - 1
        first_c = t == nj + 1

        def gate_rows(h):
            rows = slice(h * DH, (h + 1) * DH)
            g = _gate_group(rows, o_ref, za_ref, gb_ref, gc_ref, xcv_ref, zc_ref, gcp_ref, xcp_ref,
                            cw_ref, ga_ref, gcn_ref, first_a)
            mix_new[rows, :] = (g["oa"] * g["g_a"] * (g["za"] * g["sa"])).astype(BF16)
            mix_new[DA + h * DH:DA + (h + 1) * DH, :] = (g["ec"] * g["g_c"] * (g["zc"] * g["sc"])).astype(BF16)

        def loss_rows(c):
            blk = c // (TB // rp)
            rows, out_rows = pl.ds((c % (TB // rp)) * rp, rp), pl.ds(c * rp, rp)
            h = (xa_ref, xb_ref, xc_ref)[blk][rows, :]
            if blk == 0:
                first = meta_ref[...] if c == P0 // rp else jnp.zeros((rp, D), F32)
                h = jnp.where(first_a, first, h)
            o = o_scr[out_rows, :] + h
            r = lax.rsqrt(jnp.mean(o * o, axis=-1, keepdims=True) + EPS)
            orn = o * r
            g = g_ref[...]
            diff = orn * g - (ta_ref, tb_ref, tc_ref)[blk][rows, :]
            if blk == 0:
                diff = diff * jnp.where(first_a, 0.0, 1.0)
            gy = diff * (g * (1.0 / D))
            dout = r * (gy - orn * jnp.mean(gy * orn, axis=-1, keepdims=True))
            dout_ref[out_rows, :] = dout
            db_new[out_rows, :] = dout.astype(BF16)
            sq, go = diff * diff, diff * orn
            sq_acc[...] += sq[:8] + sq[8:]
            dg_acc[...] += go[:8] + go[8:]

        def backward_cols(n):
            if n < 4:
                cols = slice(n * cb, (n + 1) * cb)
                dmix_new[cols, :] = _dot(db_old[...], w_ref[cols, :], NT_DIMS).T.astype(BF16)
            else:
                cols = slice((n - 4) * cb, (n - 3) * cb)
                dw_ref[:, cols] += _dot(mix_old[...], db_old[:, cols])

        def gate_bwd_rows(h):
            rows = slice(h * DH, (h + 1) * DH)
            sec = lambda s: slice(s * DA + h * DH, s * DA + (h + 1) * DH)
            g = _gate_group(rows, o2_ref, za2_ref, gb2_ref, gc2_ref, xcv2_ref, zc2_ref, gcp2_ref, xcp2_ref,
                            cw_ref, ga_ref, gcn_ref, first_c)
            o, za, gb, gc, xc, zc, sa, sc = (g[n] for n in ("o", "za", "gb", "gc", "xc", "zc", "sa", "sc"))
            dya = dmix_old[rows, :].astype(F32)
            dyc = dmix_old[DA + h * DH:DA + (h + 1) * DH, :].astype(F32)

            dn = dya * (za * sa)
            dg5_ref[sec(0), :] = (dya * (g["oa"] * g["g_a"]) * (sa * (1.0 + za * (1.0 - sa)))).astype(BF16)
            dga_ref[rows, :] += _lane_tiles_sum(dn * g["oa"])
            dng = dn * g["g_a"]
            mean_a = jnp.mean(dng * g["oa"], axis=0, keepdims=True)
            do = (dng - g["oa"] * mean_a) * g["ra"]
            do_ref[rows, :] = do.astype(BF16)
            dd_ref[h] = jnp.sum(do * o, axis=0, keepdims=True)

            dnc = dyc * (zc * sc)
            dg5_ref[sec(4), :] = (dyc * (g["ec"] * g["g_c"]) * (sc * (1.0 + zc * (1.0 - sc)))).astype(BF16)
            dgc_ref[rows, :] += _lane_tiles_sum(dnc * g["ec"])
            dncg = dnc * g["g_c"]
            mean_c = jnp.mean(dncg * g["ec"], axis=0, keepdims=True)
            de = (dncg - g["ec"] * mean_c) * g["rc"]
            dg5_ref[sec(1), :] = (de * g["cv"]).astype(BF16)
            dcv = de * gb
            full = jnp.concatenate([dcv, carry_ref[rows, :]], axis=1)
            d1 = pltpu.roll(full, TT + TB - 1, 1)[:, :TT]
            d2 = pltpu.roll(full, TT + TB - 2, 1)[:, :TT]
            carry_ref[rows, :] = dcv[:, :TB]
            da = g["w2"] * dcv + g["w1"] * d1 + g["w0"] * d2
            dg5_ref[sec(2), :] = (da * xc).astype(BF16)
            dg5_ref[sec(3), :] = (da * gc).astype(BF16)
            dcw_ref[0, rows, :] += _lane_tiles_sum(dcv * g["a2"])
            dcw_ref[1, rows, :] += _lane_tiles_sum(dcv * g["a1"])
            dcw_ref[2, rows, :] += _lane_tiles_sum(dcv * g["a"])

        def step(a, b, c):
            half = H // 2
            for h in range(H):
                if a:
                    gate_rows(h)
                if c and h < half:
                    gate_bwd_rows(h)
                if b and h % 2 == 1:
                    backward_cols(h // 2)
            if a:
                o_scr[...] = _dot(mix_new[...], w_ref[...], TN_DIMS)
            per = TT // rp // n_bwd
            for k in range(n_bwd):
                if a:
                    for piece in range(per * k, per * (k + 1)):
                        loss_rows(piece)
                if c and k % 2 == 0:
                    gate_bwd_rows(half + k // 2)
                if b and k % 2 == 1:
                    backward_cols(n_bwd // 2 + k // 2)
            if a:
                db_old[...] = db_new[...]
                mix_old[...] = mix_new[...]
            if b:
                dmix_old[...] = dmix_new[...]

        @pl.when(t == 0)
        def _():
            dw_ref[...] = jnp.zeros_like(dw_ref)
            sq_acc[...] = jnp.zeros_like(sq_acc)
            dg_acc[...] = jnp.zeros_like(dg_acc)
            carry_ref[...] = jnp.zeros_like(carry_ref)
            dga_ref[...] = jnp.zeros_like(dga_ref)
            dgc_ref[...] = jnp.zeros_like(dgc_ref)
            dcw_ref[...] = jnp.zeros_like(dcw_ref)
            for j in range(NDEV):
                w_ref[j * e_sh:(j + 1) * e_sh, :] = _pick_slab(j, wown_ref, wland_ref, slice(0, e_sh), per_peer=False)
            step(True, False, False)

        @pl.when(t == 1)
        def _():
            step(True, True, False)

        @pl.when((t >= 2) & (t < nj))
        def _():
            step(True, True, True)

        @pl.when(t == nj)
        def _():
            step(False, True, True)
            dwb_ref[...] = dw_ref[...].astype(BF16)
            loss_ref[...] = jnp.sum(sq_acc[...], keepdims=True) * (0.5 / D)
            dg_ref[...] = jnp.sum(dg_acc[...], axis=0, keepdims=True) * (1.0 / D)

        @pl.when(t == nj + 1)
        def _():
            step(False, False, True)

    assert nj >= 2
    tile_a = lambda t: jnp.clip(nj - 1 - t, 0, nj - 1)
    tile_c = lambda t: jnp.clip(nj + 1 - t, 0, nj - 1)
    at_c = lambda shape: pl.BlockSpec(shape, lambda t: (0,) * (len(shape) - 1) + (tile_c(t),))
    return pl.pallas_call(
        body, name="gate_outproj", grid=(nj + 2,),
        in_specs=[pl.BlockSpec(memory_space=pl.ANY)] * 2 + _gate_specs(tile_a)[2:] + _gate_specs(tile_c)[2:3]
                 + [_full_spec((e_sh, D)), _full_spec((NDEV, e_sh, D))] + _x_specs3(tile_a)
                 + [_full_spec((NM, D)), _full_spec((1, D))] + _x_specs3(tile_a),
        out_specs=[pl.BlockSpec((TT, D), lambda t: (tile_a(t), 0)), _full_spec((D, D)), _full_spec((1, 1)),
                   _full_spec((1, D)), at_c((DA, TT)), at_c((H, 1, TT)), at_c((5 * DA, TT)),
                   _full_spec((DA, TB)), _full_spec((DA, TB)), _full_spec((3, DA, TB))],
        out_shape=[jax.ShapeDtypeStruct((L, D), F32), jax.ShapeDtypeStruct((D, D), BF16),
                   jax.ShapeDtypeStruct((1, 1), F32), jax.ShapeDtypeStruct((1, D), F32),
                   jax.ShapeDtypeStruct((DA, L), BF16),
                   jax.ShapeDtypeStruct((H, 1, L), F32),
                   jax.ShapeDtypeStruct((5 * DA, L), BF16),
                   jax.ShapeDtypeStruct((DA, TB), F32),
                   jax.ShapeDtypeStruct((DA, TB), F32),
                   jax.ShapeDtypeStruct((3, DA, TB), F32)],
        scratch_shapes=[pltpu.VMEM((D, D), F32), pltpu.VMEM((TT, D), F32), pltpu.VMEM((TT, D), BF16),
                        pltpu.VMEM((TT, D), BF16), pltpu.VMEM((D, TT), BF16), pltpu.VMEM((D, TT), BF16),
                        pltpu.VMEM((D, TT), BF16), pltpu.VMEM((D, TT), BF16),
                        pltpu.VMEM((8, D), F32), pltpu.VMEM((8, D), F32), pltpu.VMEM((DA, TB), F32),
                        pltpu.VMEM((n_ring, DA, TT), F32), pltpu.VMEM((n_ring, 5 * DA, TT), BF16),
                        pltpu.SemaphoreType.DMA((2, n_ring)), pltpu.VMEM((D, D), BF16)],
        compiler_params=_params(),
    )(o_t, gate_t, gate_t, cw_b, ga_b, gcn_b, gate_t,
      w_own, w_land, x, x, x, meta_full, fng, target, target, target)


def _attn_bwd(proj_t, kaug, vtok, do_t, lse, dd, cq, L, after):
    nk = L // TT

    def body(q_ref, kaug_ref, vtok_ref, kt_ref, do_ref, lse_ref, dd_ref, cq_ref, _,
             dq_ref, dk_ref, dv_ref, dck_ref, dcq_ref, dq_acc, kt1_scr, s_scr, dp_scr, dv_scr, dk_scr):
        i = pl.program_id(0)
        rows = [slice(g * DH, (g + 1) * DH) for g in range(HG)]
        ones = jnp.ones((DF, TT), BF16)
        zpad = jnp.zeros((KA - DH - DF, TT), BF16)
        for g in range(HG):
            kt1_scr[g] = jnp.concatenate([kt_ref[rows[g], :], ones], axis=0)
        dv_scr[...] = jnp.zeros_like(dv_scr)
        dk_scr[...] = jnp.zeros_like(dk_scr)

        def q_rows(g, q_off):
            bias = cq_ref[g, :, pl.ds(q_off, TT)] - lse_ref[g, :, pl.ds(q_off, TT)]
            return jnp.concatenate([q_ref[rows[g], pl.ds(q_off, TT)], _bias_rows(bias)], axis=0)

        def scores(jq, masked):
            q_off = pl.multiple_of(jq * TT, TT)
            for g in range(HG):
                s = _dot(kaug_ref[g], jnp.concatenate([q_rows(g, q_off), zpad], axis=0))
                if masked:
                    s = jnp.where(_causal_mask(), s, NEG)
                s_scr[g] = s
                dp_scr[g] = _dot(vtok_ref[g], do_ref[rows[g], pl.ds(q_off, TT)])

        def grads(jq):
            q_off = pl.multiple_of(jq * TT, TT)
            for g in range(HG):
                p = jnp.exp2(s_scr[g])
                ds = (p * (dp_scr[g] - dd_ref[g, :, pl.ds(q_off, TT)])).astype(BF16)
                do1 = jnp.concatenate([do_ref[rows[g], pl.ds(q_off, TT)], jnp.zeros((KA - DH, TT), BF16)], axis=0)
                q1 = jnp.concatenate([q_rows(g, q_off), zpad], axis=0)
                dv_scr[g] += _dot(p.astype(BF16), do1, NT_DIMS)
                dk_scr[g] += _dot(ds, q1, NT_DIMS)
                dq_acc[g, :, pl.ds(q_off, TT)] += _dot(kt1_scr[g], ds)

        @pl.when(i == 0)
        def _():
            dq_acc[...] = jnp.zeros_like(dq_acc)

        scores(i, True)

        def step(jq, c):
            grads(jq)
            scores(jq + 1, False)
            return c

        lax.fori_loop(i, nk - 1, step, 0)
        grads(nk - 1)
        for g in range(HG):
            dv_ref[rows[g], :] = dv_scr[g].T[:DH, :].astype(BF16)
            dk_t = dk_scr[g].T
            dk_ref[rows[g], :] = (dk_t[:DH, :] * LN2).astype(BF16)
            dck_ref[g] = dk_t[DH:DH + 1, :]

        @pl.when(i == nk - 1)
        def _():
            for g in range(HG):
                dq_ref[rows[g], :] = (dq_acc[g, :DH, :] * (DH ** -0.5)).astype(BF16)
                dcq_ref[g] = dq_acc[g, DH:DH + 1, :]

    assert HG == H
    head = lambda i: (0, 0)
    row = lambda i: (0, 0, 0)
    return pl.pallas_call(
        body, name="attn_bwd", grid=(nk,),
        in_specs=[
            pl.BlockSpec((DA, L), head),
            pl.BlockSpec((H, TT, KA), lambda i: (0, i, 0)),
            pl.BlockSpec((H, TT, DH), lambda i: (0, i, 0)),
            pl.BlockSpec((DA, TT), lambda i: (1, i)),
            pl.BlockSpec((DA, L), head),
            pl.BlockSpec((H, 1, L), row), pl.BlockSpec((H, 1, L), row), pl.BlockSpec((H, 1, L), row), _UNREAD,
        ],
        out_specs=[
            pl.BlockSpec((DA, L), head),
            pl.BlockSpec((DA, TT), lambda i: (0, i)),
            pl.BlockSpec((DA, TT), lambda i: (0, i)),
            pl.BlockSpec((H, 1, TT), lambda i: (0, 0, i)),
            pl.BlockSpec((H, 1, L), row),
        ],
        out_shape=[jax.ShapeDtypeStruct((DA, L), BF16), jax.ShapeDtypeStruct((DA, L), BF16),
                   jax.ShapeDtypeStruct((DA, L), BF16), jax.ShapeDtypeStruct((H, 1, L), F32),
                   jax.ShapeDtypeStruct((H, 1, L), F32)],
        scratch_shapes=[
            pltpu.VMEM((HG, DH + DF, L), F32),
            pltpu.VMEM((HG, DH + DF, TT), BF16),
            pltpu.VMEM((HG, TT, TT), F32), pltpu.VMEM((HG, TT, TT), F32),
            pltpu.VMEM((HG, TT, KA), F32), pltpu.VMEM((HG, TT, KA), F32)],
        compiler_params=_params(),
    )(proj_t, kaug, vtok, proj_t, do_t, lse, dd, cq, after)


def _fgate_bwd(dcq, dck, sg, L):
    def body(dcq_ref, dck_ref, sg_ref, df_ref, db_ref):
        dc = jnp.concatenate([dcq_ref[h] - dck_ref[h] for h in range(H)], axis=0)
        idx = lax.broadcasted_iota(jnp.int32, (H, L), 1)
        r = dc
        s = 1
        while s < L:
            r = r + jnp.where(idx + s < L, pltpu.roll(r, L - s, 1), 0.0)
            s *= 2
        df = r * sg_ref[...]
        db_ref[...] = jnp.broadcast_to(jnp.sum(df, axis=1, keepdims=True), (H, TB))
        df_ref[...] = jnp.concatenate([df, jnp.zeros((DF - H, L), F32)], axis=0).astype(BF16)

    return pl.pallas_call(
        body, name="fgate_bwd",
        out_shape=[jax.ShapeDtypeStruct((DF, L), BF16), jax.ShapeDtypeStruct((H, TB), F32)],
        compiler_params=pltpu.CompilerParams(vmem_limit_bytes=VMEM_LIMIT),
    )(dcq, dck, sg)


def _inproj_bwd_x(w, dq_t, dk_t, dv_t, dg5_t, df_t, dout, x, meta_full, norm_g, L, after):
    nj = L // TT
    seq = x.shape[0]

    def body(w_ref, dq_ref, dk_ref, dv_ref, dg5_ref, df_ref, dout_ref, xa_ref, xb_ref, xc_ref, meta_ref, g_ref, _,
             gx_ref, dmeta_ref, dg_ref, dh_scr, sems):
        j = pl.program_id(0)
        slot = j % 2

        def copy_out(step, slot_):
            first = pltpu.make_async_copy(dh_scr.at[slot_, pl.ds(TB, TT - TB)], gx_ref.at[pl.ds(0, TT - TB)],
                                          sems.at[slot_])
            later = pltpu.make_async_copy(dh_scr.at[slot_], gx_ref.at[pl.ds(step * TT - TB, TT)], sems.at[slot_])
            return first, later

        @pl.when(j == 0)
        def _():
            dg_ref[...] = jnp.zeros_like(dg_ref)

        du = _dot(dq_ref[...], w_ref[0:DA, :], TN_DIMS)
        du += _dot(dk_ref[...], w_ref[DA:2 * DA, :], TN_DIMS)
        du += _dot(dv_ref[...], w_ref[2 * DA:3 * DA, :], TN_DIMS)
        du += _dot(dg5_ref[...], w_ref[3 * DA:NSEC * DA, :], TN_DIMS)
        du += _dot(df_ref[...], w_ref[NSEC * DA:DPROJ, :], TN_DIMS)
        hb = _h_tile(j, xa_ref, xb_ref, xc_ref, meta_ref)
        r = lax.rsqrt(jnp.mean(hb * hb, axis=-1, keepdims=True) + EPS)
        hn = hb * r
        dg_ref[...] += jnp.sum(du * hn, axis=0, keepdims=True)
        gu = du * g_ref[...]
        dh = dout_ref[...] + r * gu - hn * (r * jnp.mean(gu * hn, axis=-1, keepdims=True))

        dh_scr[slot] = dh

        @pl.when(j == 0)
        def _():
            dmeta_ref[...] = dh[P0:TB, :]
            copy_out(0, 0)[0].start()

        @pl.when(j >= 1)
        def _():
            copy_out(j, slot)[1].start()

        @pl.when(j == 1)
        def _():
            copy_out(0, 0)[0].wait()

        @pl.when(j >= 2)
        def _():
            copy_out(j - 1, 1 - slot)[1].wait()

        @pl.when(j == nj - 1)
        def _():
            copy_out(j, slot)[0 if nj == 1 else 1].wait()

    blk = lambda rows: pl.BlockSpec((rows, TT), lambda j: (0, j))
    return pl.pallas_call(
        body, name="inproj_bwd_x", grid=(nj,),
        in_specs=[_full_spec((DPROJ, D)), blk(DA), blk(DA), blk(DA), blk(5 * DA), blk(DF),
                  pl.BlockSpec((TT, D), lambda j: (j, 0))] + _x_specs3()
                 + [_full_spec((NM, D)), _full_spec((1, D)), _UNREAD],
        out_specs=[pl.BlockSpec(memory_space=pl.ANY), _full_spec((NM, D)), _full_spec((1, D))],
        out_shape=[jax.ShapeDtypeStruct((seq, D), F32), jax.ShapeDtypeStruct((NM, D), F32),
                   jax.ShapeDtypeStruct((1, D), F32)],
        scratch_shapes=[pltpu.VMEM((2, TT, D), F32), pltpu.SemaphoreType.DMA((2,))],
        compiler_params=_params(),
    )(w, dq_t, dk_t, dv_t, dg5_t, df_t, dout, x, x, x, meta_full, norm_g, after)


def _inproj_bwd_w(u, dq_t, dk_t, dv_t, dg5_t, df_t, L):
    def body(u_ref, dq_hbm, dk_hbm, dv_hbm, dg5_ref, df_ref, dw_ref, dwf_ref, qkv_scr, sems):
        s = pl.program_id(0)
        fetch = [pltpu.make_async_copy(src, qkv_scr.at[k], sems.at[k])
                 for k, src in enumerate((dq_hbm, dk_hbm, dv_hbm))]

        @pl.when(s == 0)
        def _():
            for cp in fetch:
                cp.start()

        @pl.when(s < 5)
        def _():
            dw_ref[...] = _dot(dg5_ref[...], u_ref[...])

        for k in range(3):
            @pl.when(s == 5 + k)
            def _(k=k):
                fetch[k].wait()
                dw_ref[...] = _dot(qkv_scr[k], u_ref[...])

        @pl.when(s == NSEC - 1)
        def _():
            dwf_ref[...] = _dot(df_ref[...], u_ref[...])

    once = lambda shape: pl.BlockSpec(shape, lambda s: (0, 0), pipeline_mode=pl.Buffered(1))
    any_spec = pl.BlockSpec(memory_space=pl.ANY)
    return pl.pallas_call(
        body, name="inproj_bwd_w", grid=(NSEC,),
        in_specs=[
            once((L, D)), any_spec, any_spec, any_spec,
            pl.BlockSpec((DA, L), lambda s: (jnp.minimum(s, 4), 0)),
            once((DF, L)),
        ],
        out_specs=[pl.BlockSpec((DA, D), lambda s: (jnp.where(s < 5, s + 3, s - 5), 0)), _full_spec((DF, D))],
        out_shape=[jax.ShapeDtypeStruct((NSEC * DA, D), F32), jax.ShapeDtypeStruct((DF, D), F32)],
        scratch_shapes=[pltpu.VMEM((3, DA, L), BF16), pltpu.SemaphoreType.DMA((3,))],
        compiler_params=_params(),
    )(u, dq_t, dk_t, dv_t, dg5_t, df_t)


def _adamw(w, g, m, v):
    m = ADAM_B1 * m + (1.0 - ADAM_B1) * g
    v = ADAM_B2 * v + (1.0 - ADAM_B2) * (g * g)
    m_hat = m / (1.0 - ADAM_B1 ** ADAM_STEP)
    v_hat = v / (1.0 - ADAM_B2 ** ADAM_STEP)
    delta = -ADAM_LR * (m_hat / (jnp.sqrt(v_hat) + ADAM_EPS) + ADAM_WD * w)
    return delta, m, v


def _adamw_big(own_in, land_in, own_out, land_out, w_in_t, m_in_t, v_in_t, w_out, m_out, v_out):
    cb = CB
    e_sh = D // NDEV
    in_shape = jax.ShapeDtypeStruct(w_in_t.shape, F32)
    out_shape = jax.ShapeDtypeStruct(w_out.shape, F32)

    def total(own_ref, land_ref, rows, chips):
        g = _pick_slab(0, own_ref, land_ref, rows, chips=chips).astype(F32)
        for j in range(1, own_ref.shape[0]):
            g = g + _pick_slab(j, own_ref, land_ref, rows, chips=chips).astype(F32)
        return g

    def body(oi_ref, li_ref, oo_ref, lo_ref, wi_ref, mi_ref, vi_ref, wo_ref, mo_ref, vo_ref,
             gi, di, mi, vi, go, do, mo, vo):
        g = total(oi_ref, li_ref, slice(0, WSHP), True)[:WSH]
        d, mn, vn = _adamw(wi_ref[...], g, mi_ref[...], vi_ref[...])
        gi[...], di[...], mi[...], vi[...] = g, d, mn, vn
        g = total(oo_ref, lo_ref, slice(0, e_sh), False)
        d, mn, vn = _adamw(wo_ref[0], g, mo_ref[0], vo_ref[0])
        go[0], do[0], mo[0], vo[0] = g, d, mn, vn

    slab = lambda n, rows: pl.BlockSpec((n, rows, cb), lambda i: (0, 0, i))
    ispec = pl.BlockSpec((WSH, cb), lambda i: (0, i))
    ospec = pl.BlockSpec((1, e_sh, cb), lambda i: (0, 0, i))
    return pl.pallas_call(
        body, name="adamw_big", grid=(D // cb,),
        in_specs=[slab(4, WSHP), slab(4, WSHP), slab(NDEV, e_sh), slab(NDEV, e_sh),
                  ispec, ispec, ispec, ospec, ospec, ospec],
        out_specs=[ispec] * 4 + [ospec] * 4, out_shape=[in_shape] * 4 + [out_shape] * 4,
        compiler_params=_params(),
    )(own_in, land_in, own_out, land_out, w_in_t, m_in_t, v_in_t, w_out, m_out, v_out)


F0 = 3 * DA


def _unshard_w_in(w_all, small_all, attn_gain, conv_gain):
    def body(w_ref, small_ref, ga_ref, gc_ref, wt_ref, meta_ref, cwb_ref, gab_ref, gcb_ref):
        i = pl.program_id(0)
        for k in range(CB // TB):
            meta_ref[:, k * TB:(k + 1) * TB] = small_ref[(CB // TB) * i + k, 0:NM, :]

        @pl.when(i == 0)
        def _():
            per_row = lambda line: jnp.broadcast_to(line, (TB, DA)).T
            cw = jnp.concatenate([small_ref[j, NM:NM + 3, 0:DH] for j in range(NDEV)], axis=1)
            for k in range(3):
                cwb_ref[k] = per_row(cw[k:k + 1, :])
            gab_ref[...] = per_row(ga_ref[...])
            gcb_ref[...] = per_row(gc_ref[...])

        def ref_rows(lo, hi):
            pieces, r = [], lo
            while r < hi:
                sh, off = divmod(r, WSH)
                n = min(hi - r, WSH - off)
                pieces.append(w_ref[sh, off:off + n, :])
                r += n
            return pieces

        for s in range(NSEC):
            lo = s * DA if s < 3 else s * DA + H
            wt_ref[s * DA:(s + 1) * DA, :] = jnp.concatenate(ref_rows(lo, lo + DA), axis=0)
        wt_ref[NSEC * DA:DPROJ, :] = jnp.concatenate(
            ref_rows(F0, F0 + H) + [jnp.zeros((DF - H, CB), BF16)], axis=0)

    return pl.pallas_call(
        body, name="unshard_w_in", grid=(D // CB,),
        in_specs=[pl.BlockSpec((NDEV, WSHP, CB), lambda i: (0, 0, i)), _full_spec(small_all.shape),
                  _full_spec((1, DA)), _full_spec((1, DA))],
        out_specs=[pl.BlockSpec((DPROJ, CB), lambda i: (0, i)), pl.BlockSpec((NM, CB), lambda i: (0, i)),
                   _full_spec((3, DA, TB)), _full_spec((DA, TB)), _full_spec((DA, TB))],
        out_shape=[jax.ShapeDtypeStruct((DPROJ, D), BF16), jax.ShapeDtypeStruct((NM, D), F32),
                   jax.ShapeDtypeStruct((3, DA, TB), F32), jax.ShapeDtypeStruct((DA, TB), F32),
                   jax.ShapeDtypeStruct((DA, TB), F32)],
        compiler_params=_params(),
    )(w_all, small_all, attn_gain, conv_gain)


def _shard_w_in_grads(dw_main, dw_f):
    def body(dm_ref, df_ref, p_ref):
        mc = lax.axis_index("c")

        def ref_rows(lo, hi):
            pieces, r = [], lo
            while r < hi:
                if r < F0:
                    n = min(hi, F0) - r
                    pieces.append(dm_ref[r:r + n, :])
                elif r < F0 + H:
                    n = min(hi, F0 + H) - r
                    pieces.append(df_ref[r - F0:r - F0 + n, :])
                else:
                    n = hi - r
                    pieces.append(dm_ref[r - H:r - H + n, :])
                r += n
            return pieces

        for i in range(NDEV):
            rows = jnp.concatenate(ref_rows(i * WSH, (i + 1) * WSH) + [jnp.zeros((WSHP - WSH, CB), F32)], axis=0)
            p_ref[i // 2 + jnp.where(mc == i % 2, 0, 4)] = rows.astype(BF16)

    col = lambda rows: pl.BlockSpec((rows, CB), lambda i: (0, i))
    return pl.pallas_call(
        body, name="shard_w_in_grads", grid=(D // CB,),
        in_specs=[col(NSEC * DA), col(DF)],
        out_specs=pl.BlockSpec((NDEV, WSHP, CB), lambda i: (0, 0, i)),
        out_shape=jax.ShapeDtypeStruct((NDEV, WSHP, D), BF16),
        compiler_params=_params(),
    )(dw_main, dw_f)


SMALL = ("norm_g", "final_norm_g", "attn_norm_g", "conv_norm_g", "b_f", "meta", "conv_w")


def _as_rows(x):
    return jnp.concatenate([x[:, r * TB:(r + 1) * TB] for r in range(x.shape[1] // TB)], axis=0)


def _as_line(rows):
    return jnp.concatenate([rows[r:r + 1, :] for r in range(rows.shape[0])], axis=1)


def _pad_rows(x, n=8):
    return jnp.concatenate([x, jnp.zeros((n - x.shape[0], x.shape[1]), F32)], axis=0)


def _tile_rows(a, rows, lanes=TB):
    a = a.reshape(rows, lanes)
    return jnp.pad(a, ((0, -rows % 8), (0, TB - lanes)))


def _pack_small_grads(dg_norm, dg_final, dga_p, dgc_p, dcw_p, db_b, dmeta, loss):
    def body(dgn_ref, dgf_ref, dga_ref, dgc_ref, dcw_ref, db_ref, dmeta_ref, loss_ref, out_ref):
        def lane_sums(p):
            return jnp.sum(p.T, axis=0, keepdims=True)

        lane = lax.broadcasted_iota(jnp.int32, (1, TB), 1)
        b_row = jnp.where(lane == H, loss_ref[...], 0.0)
        for h in range(H):
            b_row = b_row + jnp.where(lane == h, db_ref[h:h + 1, :], 0.0)
        common = jnp.concatenate([
            _as_rows(dgn_ref[...]), _as_rows(dgf_ref[...]), _pad_rows(_as_rows(lane_sums(dga_ref[...]))),
            _pad_rows(_as_rows(lane_sums(dgc_ref[...]))), _pad_rows(b_row)], axis=0)
        dcw = [lane_sums(dcw_ref[k]) for k in range(3)]
        for j in range(NDEV):
            cw = jnp.concatenate(
                [jnp.concatenate([r[:, j * DH:(j + 1) * DH], jnp.zeros((1, TB - DH), F32)], axis=1) for r in dcw],
                axis=0)
            out_ref[j] = jnp.concatenate([common, dmeta_ref[:, j * TB:(j + 1) * TB], _pad_rows(cw)], axis=0)

    return pl.pallas_call(
        body, name="pack_small_grads", out_shape=jax.ShapeDtypeStruct((NDEV, SROWS, TB), F32),
    )(dg_norm, dg_final, dga_p, dgc_p, dcw_p, db_b, dmeta, loss)


def _adamw_small(own, land, params):
    flat = [a for n in SMALL for a in params[n]]

    def body(*refs):
        own_ref, land_ref = refs[:2]
        ins = refs[2:2 + 3 * len(SMALL)]
        outs = refs[2 + 3 * len(SMALL):]
        g = _pick_slab(0, own_ref, land_ref, slice(0, SROWS))
        for j in range(1, NDEV):
            g = g + _pick_slab(j, own_ref, land_ref, slice(0, SROWS))
        grads = dict(
            norm_g=_as_line(g[0:8]), final_norm_g=_as_line(g[8:16]), attn_norm_g=_as_line(g[16:20]),
            conv_norm_g=_as_line(g[24:28]), b_f=g[32:33, :H], meta=g[40:56], conv_w=g[56:59, :DH][None])
        for i, n in enumerate(SMALL):
            w_ref, m_ref, v_ref = ins[3 * i:3 * i + 3]
            d, mn, vn = _adamw(w_ref[...], grads[n], m_ref[...], v_ref[...])
            for o_ref, val in zip(outs[4 * i:4 * i + 4], (grads[n], d, mn, vn)):
                o_ref[...] = val
        outs[-1][...] = g[32:33, H:H + 1]

    shapes = [jax.ShapeDtypeStruct(params[n][0].shape, F32) for n in SMALL for _ in range(4)]
    res = pl.pallas_call(
        body, name="adamw_small", out_shape=shapes + [jax.ShapeDtypeStruct((1, 1), F32)],
    )(own, land, *flat)
    return {n: res[4 * i:4 * i + 4] for i, n in enumerate(SMALL)}, res[-1]


def kernel(x, meta, norm_g, w_in, b_f, conv_w, attn_norm_g, conv_norm_g, w_out, final_norm_g, loss_target, m_meta, m_norm_g, m_w_in, m_b_f, m_conv_w, m_attn_norm_g, m_conv_norm_g, m_w_out, m_final_norm_g, v_meta, v_norm_g, v_w_in, v_b_f, v_conv_w, v_attn_norm_g, v_conv_norm_g, v_w_out, v_final_norm_g):
    seq = x.shape[1]
    L = seq + TB
    assert x.shape == (1, seq, D) and L % TT == 0 and w_in.shape == (1, D, WSH)
    x2 = x[0]
    tgt = loss_target[0]

    w_in_slab = jnp.pad(w_in[0].T, ((0, WSHP - WSH), (0, 0))).astype(BF16)
    w_out_slab = w_out[0].astype(BF16)
    meta_slab = jnp.concatenate([meta, _tile_rows(conv_w[0], 3, DH)], axis=0)
    wout_flight = _split_start(w_out_slab, "gather_w_out_start", per_peer=False)
    w_all, small_all = _all_gather([w_in_slab, meta_slab], "gather_w_in")

    w_t, meta_full, cw_b, ga_b, gcn_b = _unshard_w_in(w_all, small_all, attn_norm_g, conv_norm_g)

    u, proj_t, gate_t, f_t, ktok, vtok = _inproj_fwd(x2, meta_full, norm_g, w_t, L, after=wout_flight[4])
    cq, kaug, sg = _fgate_fwd(f_t, b_f.reshape(H, 1), ktok, L)
    o_t, lse = _attn_fwd(proj_t, kaug, cq, L)

    w_out_own, w_out_land = _split_wait(wout_flight, o_t, "gather_w_out_wait", per_peer=False)
    dout, dw_out, loss_part, dg_final, do_t, dd, dg5_t, dga_p, dgc_p, dcw_p = _gate_outproj(
        o_t, gate_t, cw_b, ga_b, gcn_b, w_out_own, w_out_land, x2, meta_full, final_norm_g.reshape(1, D), tgt, L)
    dwo_flight = _split_start(dw_out.reshape(NDEV, D // NDEV, D), "exchange_dw_out_start", per_peer=True)
    dq_t, dk_t, dv_t, dck, dcq = _attn_bwd(proj_t, kaug, vtok, do_t, lse, dd, cq, L, after=dwo_flight[4])
    df_t, db_f = _fgate_bwd(dcq, dck, sg, L)
    dw_main, dw_f = _inproj_bwd_w(u, dq_t, dk_t, dv_t, dg5_t, df_t, L)
    dwi_parts = _shard_w_in_grads(dw_main, dw_f)
    dwi_chip = _pair_sum(dwi_parts, _pair_exchange(dwi_parts, "exchange_dw_in_pair"))
    dwi_flight = _split_start(dwi_chip, "exchange_dw_in_start", per_peer=True, chips=True)
    grad_x, dmeta, dg_norm = _inproj_bwd_x(
        w_t, dq_t, dk_t, dv_t, dg5_t, df_t, dout, x2, meta_full, norm_g, L, after=dwi_flight[4])
    small_parts = _pack_small_grads(dg_norm, dg_final, dga_p, dgc_p, dcw_p, db_f, dmeta, loss_part)
    small_flight = _split_start(small_parts, "exchange_small_start", per_peer=True)
    dwo_own, dwo_land = _split_wait(dwo_flight, small_flight[4], "exchange_dw_out_wait", per_peer=True)
    dwi_own, dwi_land = _split_wait(dwi_flight, dwo_land, "exchange_dw_in_wait", per_peer=True, chips=True)

    big_out = _adamw_big(dwi_own, dwi_land, dwo_own, dwo_land,
                         w_in[0].T, m_w_in[0].T, v_w_in[0].T, w_out, m_w_out, v_w_out)
    g_w_in, d_w_in, nm_w_in, nv_w_in = [a.T[None] for a in big_out[:4]]
    g_w_out, d_w_out, nm_w_out, nv_w_out = big_out[4:]
    sm_own, sm_land = _split_wait(small_flight, big_out[4], "exchange_small_wait", per_peer=True)
    line = lambda a: a.reshape(1, D)
    small, loss = _adamw_small(sm_own, sm_land, dict(
        norm_g=(norm_g, m_norm_g, v_norm_g),
        final_norm_g=(line(final_norm_g), line(m_final_norm_g), line(v_final_norm_g)),
        attn_norm_g=(attn_norm_g, m_attn_norm_g, v_attn_norm_g),
        conv_norm_g=(conv_norm_g, m_conv_norm_g, v_conv_norm_g),
        b_f=(b_f, m_b_f, v_b_f), meta=(meta, m_meta, v_meta), conv_w=(conv_w, m_conv_w, v_conv_w)))
    small["final_norm_g"] = [a.reshape(D) for a in small["final_norm_g"]]
    order = ("meta", "norm_g", "w_in", "b_f", "conv_w", "attn_norm_g", "conv_norm_g", "w_out", "final_norm_g")
    groups = []
    for k, (wi, wo) in enumerate(((g_w_in, g_w_out), (d_w_in, d_w_out), (nm_w_in, nm_w_out), (nv_w_in, nv_w_out))):
        d = dict({n: small[n][k] for n in SMALL}, w_in=wi, w_out=wo)
        groups.append([d[n] for n in order])
    return (loss[0, 0], grad_x[None], *groups[0], *groups[1], *groups[2], *groups[3])
```
